```python
import math
import jax, jax.numpy as jnp
from jax import lax
import numpy as np


D_MODEL = 1024
BATCH = 16
SEQ = 2048
DEPTH = 2

N_MEM = 256
XA_HEADS = 4
XA_HEAD_DIM = D_MODEL // XA_HEADS

POOL_WINDOWS = (2, 4, 8, 16)
POOL_GROUP_DIM = D_MODEL // 8
POOL_WIDTH = len(POOL_WINDOWS) * POOL_GROUP_DIM

SSM_HEAD_DIM = 64
SSM_INNER = D_MODEL
SSM_HEADS = SSM_INNER // SSM_HEAD_DIM
SSM_GROUPS = 2
SSM_STATE = 128
SSM_CONV = 4
SSM_CHUNK = 128
SSM_GN = SSM_GROUPS * SSM_STATE
SSM_CONV_DIM = SSM_INNER + 2 * SSM_GN

AB_IN = POOL_WIDTH + SSM_INNER + SSM_CONV_DIM + SSM_HEADS
AB_OUT = POOL_WIDTH + SSM_INNER

CONF_DIM = D_MODEL
CONF_KERNEL = 31
SC_DIM = D_MODEL
SC_KERNEL = 3
CD_IN = 2 * CONF_DIM + 3 * SC_DIM
CD_OUT = CONF_DIM + SC_DIM

MLP_HIDDEN = 4 * D_MODEL
N_EVEN = (DEPTH + 1) // 2
N_ODD = DEPTH // 2
RMS_EPS = 1e-6
LN_EPS = 1e-5

kernel_name = 'hybrid_pool_ssd_conformer_shortconv'


def rmsnorm(x, g):
    xf = x.astype(jnp.float32)
    y = xf * lax.rsqrt(jnp.mean(xf * xf, axis=-1, keepdims=True) + RMS_EPS)
    return (y * g.astype(jnp.float32)).astype(x.dtype)


def causal_dwconv(u, w):
    width, ch = w.shape
    return lax.conv_general_dilated(
        u, w.astype(u.dtype)[:, None, :], window_strides=(1,),
        padding=[(width - 1, 0)], dimension_numbers=('NWC', 'WIO', 'NWC'),
        feature_group_count=ch)


def pool_mixer(u, pool_w, pool_scale):
    S = u.shape[1]
    count = jnp.arange(1, S + 1, dtype=jnp.float32)[None, :, None]
    outs = []
    for g, w in enumerate(POOL_WINDOWS):
        ug = u[..., g * POOL_GROUP_DIM:(g + 1) * POOL_GROUP_DIM]
        cs = jnp.cumsum(ug.astype(jnp.float32), axis=1)
        cs_prev = jnp.pad(cs, ((0, 0), (w, 0), (0, 0)))[:, :S]
        mean = (cs - cs_prev) / jnp.minimum(count, float(w))
        outs.append((mean.astype(u.dtype) - ug) @ pool_w[g])
    return jnp.concatenate(outs, axis=-1) * pool_scale


def ssd_chunked(xh, dt, a, bm, cm):
    Bsz, S, H, P = xh.shape
    G, N = bm.shape[2], bm.shape[3]
    R = H // G
    nc = S // SSM_CHUNK
    L = SSM_CHUNK
    x = xh.reshape(Bsz, nc, L, G, R, P).astype(jnp.float32)
    dtc = dt.reshape(Bsz, nc, L, G, R)
    b = bm.reshape(Bsz, nc, L, G, N).astype(jnp.float32)
    c = cm.reshape(Bsz, nc, L, G, N).astype(jnp.float32)
    dA = dtc * a.reshape(G, R)
    cs = jnp.moveaxis(jnp.cumsum(dA, axis=2), 2, -1)
    xdt = x * dtc[..., None]
    causal = jnp.tril(jnp.ones((L, L), dtype=bool))
    diff = cs[..., :, None] - cs[..., None, :]
    decay = jnp.exp(jnp.where(causal, diff, -jnp.inf))
    cb = jnp.einsum('bclgn,bcsgn->bcgls', c, b)
    y_diag = jnp.einsum('bcgls,bcgrls,bcsgrp->bclgrp', cb, decay, xdt)
    decay_to_end = jnp.exp(cs[..., -1:] - cs)
    states = jnp.einsum('bclgn,bcgrl,bclgrp->bcgrpn', b, decay_to_end, xdt)
    chunk_decay = jnp.exp(cs[..., -1])

    def step(h, inp):
        st, dec = inp
        return dec[..., None, None] * h + st, h

    h0 = jnp.zeros((Bsz, G, R, P, N), jnp.float32)
    _, prev = lax.scan(step, h0, (jnp.moveaxis(states, 1, 0), jnp.moveaxis(chunk_decay, 1, 0)))
    prev = jnp.moveaxis(prev, 0, 1)
    y_off = jnp.einsum('bclgn,bcgrpn,bcgrl->bclgrp', c, prev, jnp.exp(cs))
    return (y_diag + y_off).reshape(Bsz, S, H, P)


def mixer_ab(h, w_in, pool_w, pool_scale, conv_w, conv_b, dt_bias, a_log, d_skip, norm_w, w_out):
    Bsz, S, _ = h.shape
    u = h @ w_in
    o1 = POOL_WIDTH
    o2 = o1 + SSM_INNER
    o3 = o2 + SSM_CONV_DIM
    pool_u, z, xbc, dt_raw = u[..., :o1], u[..., o1:o2], u[..., o2:o3], u[..., o3:]
    pool_out = pool_mixer(pool_u, pool_w, pool_scale)
    xbc = jax.nn.silu(causal_dwconv(xbc, conv_w) + conv_b)
    xs = xbc[..., :SSM_INNER]
    bm = xbc[..., SSM_INNER:SSM_INNER + SSM_GN].reshape(Bsz, S, SSM_GROUPS, SSM_STATE)
    cm = xbc[..., SSM_INNER + SSM_GN:].reshape(Bsz, S, SSM_GROUPS, SSM_STATE)
    dt = jax.nn.softplus((dt_raw + dt_bias).astype(jnp.float32))
    a = -jnp.exp(a_log.astype(jnp.float32))
    xh = xs.reshape(Bsz, S, SSM_HEADS, SSM_HEAD_DIM)
    y = ssd_chunked(xh, dt, a, bm, cm)
    y = y + d_skip.astype(jnp.float32)[:, None] * xh.astype(jnp.float32)
    gsz = SSM_INNER // SSM_GROUPS
    y = y.reshape(Bsz, S, SSM_GROUPS, gsz) * jax.nn.silu(z.astype(jnp.float32)).reshape(Bsz, S, SSM_GROUPS, gsz)
    y = y * lax.rsqrt(jnp.mean(y * y, axis=-1, keepdims=True) + RMS_EPS)
    y = y.reshape(Bsz, S, SSM_INNER) * norm_w.astype(jnp.float32)
    mix = jnp.concatenate([pool_out, y.astype(h.dtype)], axis=-1)
    return mix @ w_out


def mixer_cd(h, w_in, dw_w, dw_b, ln_g, ln_b, sc_w, w_out):
    u = h @ w_in
    v = u[..., :CONF_DIM] * jax.nn.sigmoid(u[..., CONF_DIM:2 * CONF_DIM])
    v = causal_dwconv(v, dw_w) + dw_b
    vf = v.astype(jnp.float32)
    mu = jnp.mean(vf, axis=-1, keepdims=True)
    var = jnp.mean(jnp.square(vf - mu), axis=-1, keepdims=True)
    vn = (vf - mu) * lax.rsqrt(var + LN_EPS) * ln_g.astype(jnp.float32) + ln_b.astype(jnp.float32)
    conf_out = jax.nn.silu(vn).astype(h.dtype)
    off = 2 * CONF_DIM
    bg = u[..., off:off + SC_DIM]
    cg = u[..., off + SC_DIM:off + 2 * SC_DIM]
    hh = u[..., off + 2 * SC_DIM:]
    sc_out = bg * causal_dwconv(cg * hh, sc_w)
    return jnp.concatenate([conf_out, sc_out], axis=-1) @ w_out


def cross_attn(h, mem, wq, wkv, wo):
    Bsz, S, D = h.shape
    M = mem.shape[1]
    q = (h @ wq).reshape(Bsz, S, XA_HEADS, XA_HEAD_DIM)
    kv = mem @ wkv
    k = kv[..., :D].reshape(Bsz, M, XA_HEADS, XA_HEAD_DIM)
    v = kv[..., D:].reshape(Bsz, M, XA_HEADS, XA_HEAD_DIM)
    s = jnp.einsum('bshd,bmhd->bhsm', q, k).astype(jnp.float32) / math.sqrt(XA_HEAD_DIM)
    p = jax.nn.softmax(s, axis=-1).astype(h.dtype)
    o = jnp.einsum('bhsm,bmhd->bshd', p, v).reshape(Bsz, S, D)
    return o @ wo


def sq_relu_mlp(h, w1, w2):
    r = jax.nn.relu(h @ w1)
    return (r * r) @ w2


def _fwd_setup_inputs(seed: int = 0) -> dict:
    key = jax.random.key(seed)
    ks = jax.random.split(key, 26)
    f32 = jnp.float32

    def nrm(k, shape, scale):
        return scale * jax.random.normal(k, shape, f32)

    dt0 = jnp.exp(jax.random.uniform(ks[14], (N_EVEN, SSM_HEADS), f32, math.log(1e-3), math.log(1e-1)))
    return {
        'x': nrm(ks[0], (BATCH, SEQ, D_MODEL), 1.0),
        'mem': nrm(ks[1], (BATCH, N_MEM, D_MODEL), 1.0),
        'norm_gains': 1.0 + nrm(ks[2], (DEPTH, 6, D_MODEL), 0.05),
        'xa_wq': nrm(ks[3], (DEPTH, D_MODEL, D_MODEL), D_MODEL ** -0.5),
        'xa_wkv': nrm(ks[4], (DEPTH, D_MODEL, 2 * D_MODEL), D_MODEL ** -0.5),
        'xa_wo': nrm(ks[5], (DEPTH, D_MODEL, D_MODEL), D_MODEL ** -0.5),
        'mlp_w1': nrm(ks[6], (DEPTH, D_MODEL, MLP_HIDDEN), D_MODEL ** -0.5),
        'mlp_w2': nrm(ks[7], (DEPTH, MLP_HIDDEN, D_MODEL), MLP_HIDDEN ** -0.5),
        'ab_w_in': nrm(ks[8], (N_EVEN, D_MODEL, AB_IN), D_MODEL ** -0.5),
        'pool_w': nrm(ks[9], (N_EVEN, len(POOL_WINDOWS), POOL_GROUP_DIM, POOL_GROUP_DIM), POOL_GROUP_DIM ** -0.5),
        'pool_scale': 1.0 + nrm(ks[10], (N_EVEN, POOL_WIDTH), 0.1),
        'ssm_conv_w': nrm(ks[11], (N_EVEN, SSM_CONV, SSM_CONV_DIM), SSM_CONV ** -0.5),
        'ssm_conv_b': nrm(ks[12], (N_EVEN, SSM_CONV_DIM), 0.02),
        'ssm_dt_bias': dt0 + jnp.log(-jnp.expm1(-dt0)),
        'ssm_a_log': jnp.log(jax.random.uniform(ks[15], (N_EVEN, SSM_HEADS), f32, 1.0, 16.0)),
        'ssm_d': 1.0 + nrm(ks[16], (N_EVEN, SSM_HEADS), 0.1),
        'ssm_norm': 1.0 + nrm(ks[17], (N_EVEN, SSM_INNER), 0.05),
        'ab_w_out': nrm(ks[18], (N_EVEN, AB_OUT, D_MODEL), AB_OUT ** -0.5),
        'cd_w_in': nrm(ks[19], (N_ODD, D_MODEL, CD_IN), D_MODEL ** -0.5),
        'conf_dw_w': nrm(ks[20], (N_ODD, CONF_KERNEL, CONF_DIM), CONF_KERNEL ** -0.5),
        'conf_dw_b': nrm(ks[21], (N_ODD, CONF_DIM), 0.02),
        'conf_ln_g': 1.0 + nrm(ks[22], (N_ODD, CONF_DIM), 0.05),
        'conf_ln_b': nrm(ks[23], (N_ODD, CONF_DIM), 0.02),
        'sc_conv_w': nrm(ks[24], (N_ODD, SC_KERNEL, SC_DIM), SC_KERNEL ** -0.5),
        'cd_w_out': nrm(ks[25], (N_ODD, CD_OUT, D_MODEL), CD_OUT ** -0.5),
    }


def _fwd_reference(x, mem, norm_gains, xa_wq, xa_wkv, xa_wo, mlp_w1, mlp_w2,
              ab_w_in, pool_w, pool_scale, ssm_conv_w, ssm_conv_b, ssm_dt_bias,
              ssm_a_log, ssm_d, ssm_norm, ab_w_out,
              cd_w_in, conf_dw_w, conf_dw_b, conf_ln_g, conf_ln_b, sc_conv_w, cd_w_out):
    for layer in range(DEPTH):
        g = norm_gains[layer]
        i = layer // 2
        h = rmsnorm(x, g[0])
        if layer % 2 == 0:
            m = mixer_ab(h, ab_w_in[i], pool_w[i], pool_scale[i], ssm_conv_w[i], ssm_conv_b[i],
                         ssm_dt_bias[i], ssm_a_log[i], ssm_d[i], ssm_norm[i], ab_w_out[i])
        else:
            m = mixer_cd(h, cd_w_in[i], conf_dw_w[i], conf_dw_b[i], conf_ln_g[i], conf_ln_b[i],
                         sc_conv_w[i], cd_w_out[i])
        x = x + rmsnorm(m, g[1])
        h = rmsnorm(x, g[2])
        x = x + rmsnorm(cross_attn(h, mem, xa_wq[layer], xa_wkv[layer], xa_wo[layer]), g[3])
        h = rmsnorm(x, g[4])
        x = x + rmsnorm(sq_relu_mlp(h, mlp_w1[layer], mlp_w2[layer]), g[5])
    return x


import jax as _jax
import jax.numpy as _jnp

TWIN_FORMAT = 'train_step'
FWD_PARAMS = ['x', 'mem', 'norm_gains', 'xa_wq', 'xa_wkv', 'xa_wo', 'mlp_w1', 'mlp_w2', 'ab_w_in', 'pool_w', 'pool_scale', 'ssm_conv_w', 'ssm_conv_b', 'ssm_dt_bias', 'ssm_a_log', 'ssm_d', 'ssm_norm', 'ab_w_out', 'cd_w_in', 'conf_dw_w', 'conf_dw_b', 'conf_ln_g', 'conf_ln_b', 'sc_conv_w', 'cd_w_out']
TWIN_WEIGHTS = ['norm_gains', 'xa_wq', 'xa_wkv', 'xa_wo', 'mlp_w1', 'mlp_w2', 'ab_w_in', 'pool_w', 'pool_scale', 'ssm_conv_w', 'ssm_conv_b', 'ssm_dt_bias', 'ssm_a_log', 'ssm_d', 'ssm_norm', 'ab_w_out', 'cd_w_in', 'conf_dw_w', 'conf_dw_b', 'conf_ln_g', 'conf_ln_b', 'sc_conv_w', 'cd_w_out']
TWIN_DIFF_INPUT = 'x'
TWIN_INPUTS = ['x', 'mem', 'norm_gains', 'xa_wq', 'xa_wkv', 'xa_wo', 'mlp_w1', 'mlp_w2', 'ab_w_in', 'pool_w', 'pool_scale', 'ssm_conv_w', 'ssm_conv_b', 'ssm_dt_bias', 'ssm_a_log', 'ssm_d', 'ssm_norm', 'ab_w_out', 'cd_w_in', 'conf_dw_w', 'conf_dw_b', 'conf_ln_g', 'conf_ln_b', 'sc_conv_w', 'cd_w_out', 'loss_target', 'm_norm_gains', 'm_xa_wq', 'm_xa_wkv', 'm_xa_wo', 'm_mlp_w1', 'm_mlp_w2', 'm_ab_w_in', 'm_pool_w', 'm_pool_scale', 'm_ssm_conv_w', 'm_ssm_conv_b', 'm_ssm_dt_bias', 'm_ssm_a_log', 'm_ssm_d', 'm_ssm_norm', 'm_ab_w_out', 'm_cd_w_in', 'm_conf_dw_w', 'm_conf_dw_b', 'm_conf_ln_g', 'm_conf_ln_b', 'm_sc_conv_w', 'm_cd_w_out', 'v_norm_gains', 'v_xa_wq', 'v_xa_wkv', 'v_xa_wo', 'v_mlp_w1', 'v_mlp_w2', 'v_ab_w_in', 'v_pool_w', 'v_pool_scale', 'v_ssm_conv_w', 'v_ssm_conv_b', 'v_ssm_dt_bias', 'v_ssm_a_log', 'v_ssm_d', 'v_ssm_norm', 'v_ab_w_out', 'v_cd_w_in', 'v_conf_dw_w', 'v_conf_dw_b', 'v_conf_ln_g', 'v_conf_ln_b', 'v_sc_conv_w', 'v_cd_w_out']
TWIN_OUTPUTS = ['loss', 'grad_x', 'grad_norm_gains', 'grad_xa_wq', 'grad_xa_wkv', 'grad_xa_wo', 'grad_mlp_w1', 'grad_mlp_w2', 'grad_ab_w_in', 'grad_pool_w', 'grad_pool_scale', 'grad_ssm_conv_w', 'grad_ssm_conv_b', 'grad_ssm_dt_bias', 'grad_ssm_a_log', 'grad_ssm_d', 'grad_ssm_norm', 'grad_ab_w_out', 'grad_cd_w_in', 'grad_conf_dw_w', 'grad_conf_dw_b', 'grad_conf_ln_g', 'grad_conf_ln_b', 'grad_sc_conv_w', 'grad_cd_w_out', 'delta_norm_gains', 'delta_xa_wq', 'delta_xa_wkv', 'delta_xa_wo', 'delta_mlp_w1', 'delta_mlp_w2', 'delta_ab_w_in', 'delta_pool_w', 'delta_pool_scale', 'delta_ssm_conv_w', 'delta_ssm_conv_b', 'delta_ssm_dt_bias', 'delta_ssm_a_log', 'delta_ssm_d', 'delta_ssm_norm', 'delta_ab_w_out', 'delta_cd_w_in', 'delta_conf_dw_w', 'delta_conf_dw_b', 'delta_conf_ln_g', 'delta_conf_ln_b', 'delta_sc_conv_w', 'delta_cd_w_out', 'new_m_norm_gains', 'new_m_xa_wq', 'new_m_xa_wkv', 'new_m_xa_wo', 'new_m_mlp_w1', 'new_m_mlp_w2', 'new_m_ab_w_in', 'new_m_pool_w', 'new_m_pool_scale', 'new_m_ssm_conv_w', 'new_m_ssm_conv_b', 'new_m_ssm_dt_bias', 'new_m_ssm_a_log', 'new_m_ssm_d', 'new_m_ssm_norm', 'new_m_ab_w_out', 'new_m_cd_w_in', 'new_m_conf_dw_w', 'new_m_conf_dw_b', 'new_m_conf_ln_g', 'new_m_conf_ln_b', 'new_m_sc_conv_w', 'new_m_cd_w_out', 'new_v_norm_gains', 'new_v_xa_wq', 'new_v_xa_wkv', 'new_v_xa_wo', 'new_v_mlp_w1', 'new_v_mlp_w2', 'new_v_ab_w_in', 'new_v_pool_w', 'new_v_pool_scale', 'new_v_ssm_conv_w', 'new_v_ssm_conv_b', 'new_v_ssm_dt_bias', 'new_v_ssm_a_log', 'new_v_ssm_d', 'new_v_ssm_norm', 'new_v_ab_w_out', 'new_v_cd_w_in', 'new_v_conf_dw_w', 'new_v_conf_dw_b', 'new_v_conf_ln_g', 'new_v_conf_ln_b', 'new_v_sc_conv_w', 'new_v_cd_w_out']
TWIN_LEAF_KINDS = {'loss': 'loss', 'grad_x': 'grad_x', 'grad_norm_gains': 'grad_w', 'grad_xa_wq': 'grad_w', 'grad_xa_wkv': 'grad_w', 'grad_xa_wo': 'grad_w', 'grad_mlp_w1': 'grad_w', 'grad_mlp_w2': 'grad_w', 'grad_ab_w_in': 'grad_w', 'grad_pool_w': 'grad_w', 'grad_pool_scale': 'grad_w', 'grad_ssm_conv_w': 'grad_w', 'grad_ssm_conv_b': 'grad_w', 'grad_ssm_dt_bias': 'grad_w', 'grad_ssm_a_log': 'grad_w', 'grad_ssm_d': 'grad_w', 'grad_ssm_norm': 'grad_w', 'grad_ab_w_out': 'grad_w', 'grad_cd_w_in': 'grad_w', 'grad_conf_dw_w': 'grad_w', 'grad_conf_dw_b': 'grad_w', 'grad_conf_ln_g': 'grad_w', 'grad_conf_ln_b': 'grad_w', 'grad_sc_conv_w': 'grad_w', 'grad_cd_w_out': 'grad_w', 'delta_norm_gains': 'delta_w', 'delta_xa_wq': 'delta_w', 'delta_xa_wkv': 'delta_w', 'delta_xa_wo': 'delta_w', 'delta_mlp_w1': 'delta_w', 'delta_mlp_w2': 'delta_w', 'delta_ab_w_in': 'delta_w', 'delta_pool_w': 'delta_w', 'delta_pool_scale': 'delta_w', 'delta_ssm_conv_w': 'delta_w', 'delta_ssm_conv_b': 'delta_w', 'delta_ssm_dt_bias': 'delta_w', 'delta_ssm_a_log': 'delta_w', 'delta_ssm_d': 'delta_w', 'delta_ssm_norm': 'delta_w', 'delta_ab_w_out': 'delta_w', 'delta_cd_w_in': 'delta_w', 'delta_conf_dw_w': 'delta_w', 'delta_conf_dw_b': 'delta_w', 'delta_conf_ln_g': 'delta_w', 'delta_conf_ln_b': 'delta_w', 'delta_sc_conv_w': 'delta_w', 'delta_cd_w_out': 'delta_w', 'new_m_norm_gains': 'new_m', 'new_m_xa_wq': 'new_m', 'new_m_xa_wkv': 'new_m', 'new_m_xa_wo': 'new_m', 'new_m_mlp_w1': 'new_m', 'new_m_mlp_w2': 'new_m', 'new_m_ab_w_in': 'new_m', 'new_m_pool_w': 'new_m', 'new_m_pool_scale': 'new_m', 'new_m_ssm_conv_w': 'new_m', 'new_m_ssm_conv_b': 'new_m', 'new_m_ssm_dt_bias': 'new_m', 'new_m_ssm_a_log': 'new_m', 'new_m_ssm_d': 'new_m', 'new_m_ssm_norm': 'new_m', 'new_m_ab_w_out': 'new_m', 'new_m_cd_w_in': 'new_m', 'new_m_conf_dw_w': 'new_m', 'new_m_conf_dw_b': 'new_m', 'new_m_conf_ln_g': 'new_m', 'new_m_conf_ln_b': 'new_m', 'new_m_sc_conv_w': 'new_m', 'new_m_cd_w_out': 'new_m', 'new_v_norm_gains': 'new_v', 'new_v_xa_wq': 'new_v', 'new_v_xa_wkv': 'new_v', 'new_v_xa_wo': 'new_v', 'new_v_mlp_w1': 'new_v', 'new_v_mlp_w2': 'new_v', 'new_v_ab_w_in': 'new_v', 'new_v_pool_w': 'new_v', 'new_v_pool_scale': 'new_v', 'new_v_ssm_conv_w': 'new_v', 'new_v_ssm_conv_b': 'new_v', 'new_v_ssm_dt_bias': 'new_v', 'new_v_ssm_a_log': 'new_v', 'new_v_ssm_d': 'new_v', 'new_v_ssm_norm': 'new_v', 'new_v_ab_w_out': 'new_v', 'new_v_cd_w_in': 'new_v', 'new_v_conf_dw_w': 'new_v', 'new_v_conf_dw_b': 'new_v', 'new_v_conf_ln_g': 'new_v', 'new_v_conf_ln_b': 'new_v', 'new_v_sc_conv_w': 'new_v', 'new_v_cd_w_out': 'new_v'}


def _forward(args):
    return _fwd_reference(*[args[k] for k in FWD_PARAMS])


def _output_shape():
    out = _jax.eval_shape(lambda: _forward(_fwd_setup_inputs(0)))
    return out.shape, out.dtype

N_MICROBATCH = 1
ADAM_LR = 0.001
ADAM_B1 = 0.9
ADAM_B2 = 0.999
ADAM_EPS = 1e-08
ADAM_WD = 0.01
ADAM_STEP = 10
PER_EXAMPLE_BATCH_AXIS = {'x': 0, 'mem': 0, 'loss_target': 0}
SHARED_INPUTS = []
_WEIGHT_DTYPES = {'norm_gains': _jnp.float32, 'xa_wq': _jnp.float32, 'xa_wkv': _jnp.float32, 'xa_wo': _jnp.float32, 'mlp_w1': _jnp.float32, 'mlp_w2': _jnp.float32, 'ab_w_in': _jnp.float32, 'pool_w': _jnp.float32, 'pool_scale': _jnp.float32, 'ssm_conv_w': _jnp.float32, 'ssm_conv_b': _jnp.float32, 'ssm_dt_bias': _jnp.float32, 'ssm_a_log': _jnp.float32, 'ssm_d': _jnp.float32, 'ssm_norm': _jnp.float32, 'ab_w_out': _jnp.float32, 'cd_w_in': _jnp.float32, 'conf_dw_w': _jnp.float32, 'conf_dw_b': _jnp.float32, 'conf_ln_g': _jnp.float32, 'conf_ln_b': _jnp.float32, 'sc_conv_w': _jnp.float32, 'cd_w_out': _jnp.float32}
MOMENT_SCALE = {'norm_gains': 2.405272e+01, 'xa_wq': 2.563391e+00, 'xa_wkv': 7.569954e+00, 'xa_wo': 1.048096e+01, 'mlp_w1': 2.326710e+00, 'mlp_w2': 1.009577e+01, 'ab_w_in': 8.992577e-01, 'pool_w': 1.027968e+00, 'pool_scale': 1.248973e+00, 'ssm_conv_w': 1.847979e+00, 'ssm_conv_b': 7.186911e+00, 'ssm_dt_bias': 2.168569e+00, 'ssm_a_log': 1.534931e+01, 'ssm_d': 1.525905e+01, 'ssm_norm': 3.285845e+00, 'ab_w_out': 3.408668e+00, 'cd_w_in': 1.417484e+00, 'conf_dw_w': 3.543711e+00, 'conf_dw_b': 1.880776e+01, 'conf_ln_g': 8.142003e+00, 'conf_ln_b': 1.112071e+01, 'sc_conv_w': 8.260303e-01, 'cd_w_out': 5.638984e+00}


def _to_microbatches(a, axis):
    t = _jnp.moveaxis(a, axis, 0)
    t = t.reshape((N_MICROBATCH, t.shape[0] // N_MICROBATCH) + t.shape[1:])
    return _jnp.moveaxis(t, 1, axis + 1)


def setup_inputs(seed: int = 0) -> dict:
    inp = _fwd_setup_inputs(seed)
    key = _jax.random.fold_in(_jax.random.key(seed), 7919)
    shape, _ = _output_shape()
    out = dict(inp)
    out["loss_target"] = _jax.random.normal(_jax.random.fold_in(key, 0), shape, _jnp.float32)
    for i, name in enumerate(TWIN_WEIGHTS):
        w = inp[name].astype(_jnp.float32)
        if MOMENT_SCALE is None:
            s = _jnp.sqrt(_jnp.mean(_jnp.square(w)) + 1e-30)
        else:
            s = MOMENT_SCALE[name]
        km, kv = _jax.random.split(_jax.random.fold_in(key, i + 1))
        out[name] = w
        out["m_" + name] = s * _jax.random.normal(km, w.shape, _jnp.float32)
        out["v_" + name] = (s * s) * _jax.random.uniform(kv, w.shape, _jnp.float32, 0.5, 1.5)
    if N_MICROBATCH > 1:
        for name, axis in PER_EXAMPLE_BATCH_AXIS.items():
            out[name] = _to_microbatches(out[name], axis)
    return {'x': out['x'], 'mem': out['mem'], 'norm_gains': out['norm_gains'], 'xa_wq': out['xa_wq'], 'xa_wkv': out['xa_wkv'], 'xa_wo': out['xa_wo'], 'mlp_w1': out['mlp_w1'], 'mlp_w2': out['mlp_w2'], 'ab_w_in': out['ab_w_in'], 'pool_w': out['pool_w'], 'pool_scale': out['pool_scale'], 'ssm_conv_w': out['ssm_conv_w'], 'ssm_conv_b': out['ssm_conv_b'], 'ssm_dt_bias': out['ssm_dt_bias'], 'ssm_a_log': out['ssm_a_log'], 'ssm_d': out['ssm_d'], 'ssm_norm': out['ssm_norm'], 'ab_w_out': out['ab_w_out'], 'cd_w_in': out['cd_w_in'], 'conf_dw_w': out['conf_dw_w'], 'conf_dw_b': out['conf_dw_b'], 'conf_ln_g': out['conf_ln_g'], 'conf_ln_b': out['conf_ln_b'], 'sc_conv_w': out['sc_conv_w'], 'cd_w_out': out['cd_w_out'], 'loss_target': out['loss_target'], 'm_norm_gains': out['m_norm_gains'], 'm_xa_wq': out['m_xa_wq'], 'm_xa_wkv': out['m_xa_wkv'], 'm_xa_wo': out['m_xa_wo'], 'm_mlp_w1': out['m_mlp_w1'], 'm_mlp_w2': out['m_mlp_w2'], 'm_ab_w_in': out['m_ab_w_in'], 'm_pool_w': out['m_pool_w'], 'm_pool_scale': out['m_pool_scale'], 'm_ssm_conv_w': out['m_ssm_conv_w'], 'm_ssm_conv_b': out['m_ssm_conv_b'], 'm_ssm_dt_bias': out['m_ssm_dt_bias'], 'm_ssm_a_log': out['m_ssm_a_log'], 'm_ssm_d': out['m_ssm_d'], 'm_ssm_norm': out['m_ssm_norm'], 'm_ab_w_out': out['m_ab_w_out'], 'm_cd_w_in': out['m_cd_w_in'], 'm_conf_dw_w': out['m_conf_dw_w'], 'm_conf_dw_b': out['m_conf_dw_b'], 'm_conf_ln_g': out['m_conf_ln_g'], 'm_conf_ln_b': out['m_conf_ln_b'], 'm_sc_conv_w': out['m_sc_conv_w'], 'm_cd_w_out': out['m_cd_w_out'], 'v_norm_gains': out['v_norm_gains'], 'v_xa_wq': out['v_xa_wq'], 'v_xa_wkv': out['v_xa_wkv'], 'v_xa_wo': out['v_xa_wo'], 'v_mlp_w1': out['v_mlp_w1'], 'v_mlp_w2': out['v_mlp_w2'], 'v_ab_w_in': out['v_ab_w_in'], 'v_pool_w': out['v_pool_w'], 'v_pool_scale': out['v_pool_scale'], 'v_ssm_conv_w': out['v_ssm_conv_w'], 'v_ssm_conv_b': out['v_ssm_conv_b'], 'v_ssm_dt_bias': out['v_ssm_dt_bias'], 'v_ssm_a_log': out['v_ssm_a_log'], 'v_ssm_d': out['v_ssm_d'], 'v_ssm_norm': out['v_ssm_norm'], 'v_ab_w_out': out['v_ab_w_out'], 'v_cd_w_in': out['v_cd_w_in'], 'v_conf_dw_w': out['v_conf_dw_w'], 'v_conf_dw_b': out['v_conf_dw_b'], 'v_conf_ln_g': out['v_conf_ln_g'], 'v_conf_ln_b': out['v_conf_ln_b'], 'v_sc_conv_w': out['v_sc_conv_w'], 'v_cd_w_out': out['v_cd_w_out']}


def _loss(weights, diff, rest, loss_target):
    with _jax.named_scope("forward"):
        args = {**rest, TWIN_DIFF_INPUT: diff, **{k: w.astype(_WEIGHT_DTYPES[k]) for k, w in weights.items()}}
        y = _forward(args)
    with _jax.named_scope("loss_head"):
        err = _jnp.square(y.astype(_jnp.float32) - loss_target)
        return 0.5 * _jnp.sum(_jnp.mean(err, axis=-1)) if err.ndim else 0.5 * err


def _adamw(w, g, m, v):
    m = ADAM_B1 * m + (1.0 - ADAM_B1) * g
    v = ADAM_B2 * v + (1.0 - ADAM_B2) * _jnp.square(g)
    m_hat = m / (1.0 - ADAM_B1 ** ADAM_STEP)
    v_hat = v / (1.0 - ADAM_B2 ** ADAM_STEP)
    delta = -ADAM_LR * (m_hat / (_jnp.sqrt(v_hat) + ADAM_EPS) + ADAM_WD * w)
    return delta, m, v


def reference(x, mem, norm_gains, xa_wq, xa_wkv, xa_wo, mlp_w1, mlp_w2, ab_w_in, pool_w, pool_scale, ssm_conv_w, ssm_conv_b, ssm_dt_bias, ssm_a_log, ssm_d, ssm_norm, ab_w_out, cd_w_in, conf_dw_w, conf_dw_b, conf_ln_g, conf_ln_b, sc_conv_w, cd_w_out, loss_target, m_norm_gains, m_xa_wq, m_xa_wkv, m_xa_wo, m_mlp_w1, m_mlp_w2, m_ab_w_in, m_pool_w, m_pool_scale, m_ssm_conv_w, m_ssm_conv_b, m_ssm_dt_bias, m_ssm_a_log, m_ssm_d, m_ssm_norm, m_ab_w_out, m_cd_w_in, m_conf_dw_w, m_conf_dw_b, m_conf_ln_g, m_conf_ln_b, m_sc_conv_w, m_cd_w_out, v_norm_gains, v_xa_wq, v_xa_wkv, v_xa_wo, v_mlp_w1, v_mlp_w2, v_ab_w_in, v_pool_w, v_pool_scale, v_ssm_conv_w, v_ssm_conv_b, v_ssm_dt_bias, v_ssm_a_log, v_ssm_d, v_ssm_norm, v_ab_w_out, v_cd_w_in, v_conf_dw_w, v_conf_dw_b, v_conf_ln_g, v_conf_ln_b, v_sc_conv_w, v_cd_w_out):
    given = dict(x=x, mem=mem, norm_gains=norm_gains, xa_wq=xa_wq, xa_wkv=xa_wkv, xa_wo=xa_wo, mlp_w1=mlp_w1, mlp_w2=mlp_w2, ab_w_in=ab_w_in, pool_w=pool_w, pool_scale=pool_scale, ssm_conv_w=ssm_conv_w, ssm_conv_b=ssm_conv_b, ssm_dt_bias=ssm_dt_bias, ssm_a_log=ssm_a_log, ssm_d=ssm_d, ssm_norm=ssm_norm, ab_w_out=ab_w_out, cd_w_in=cd_w_in, conf_dw_w=conf_dw_w, conf_dw_b=conf_dw_b, conf_ln_g=conf_ln_g, conf_ln_b=conf_ln_b, sc_conv_w=sc_conv_w, cd_w_out=cd_w_out, loss_target=loss_target, m_norm_gains=m_norm_gains, m_xa_wq=m_xa_wq, m_xa_wkv=m_xa_wkv, m_xa_wo=m_xa_wo, m_mlp_w1=m_mlp_w1, m_mlp_w2=m_mlp_w2, m_ab_w_in=m_ab_w_in, m_pool_w=m_pool_w, m_pool_scale=m_pool_scale, m_ssm_conv_w=m_ssm_conv_w, m_ssm_conv_b=m_ssm_conv_b, m_ssm_dt_bias=m_ssm_dt_bias, m_ssm_a_log=m_ssm_a_log, m_ssm_d=m_ssm_d, m_ssm_norm=m_ssm_norm, m_ab_w_out=m_ab_w_out, m_cd_w_in=m_cd_w_in, m_conf_dw_w=m_conf_dw_w, m_conf_dw_b=m_conf_dw_b, m_conf_ln_g=m_conf_ln_g, m_conf_ln_b=m_conf_ln_b, m_sc_conv_w=m_sc_conv_w, m_cd_w_out=m_cd_w_out, v_norm_gains=v_norm_gains, v_xa_wq=v_xa_wq, v_xa_wkv=v_xa_wkv, v_xa_wo=v_xa_wo, v_mlp_w1=v_mlp_w1, v_mlp_w2=v_mlp_w2, v_ab_w_in=v_ab_w_in, v_pool_w=v_pool_w, v_pool_scale=v_pool_scale, v_ssm_conv_w=v_ssm_conv_w, v_ssm_conv_b=v_ssm_conv_b, v_ssm_dt_bias=v_ssm_dt_bias, v_ssm_a_log=v_ssm_a_log, v_ssm_d=v_ssm_d, v_ssm_norm=v_ssm_norm, v_ab_w_out=v_ab_w_out, v_cd_w_in=v_cd_w_in, v_conf_dw_w=v_conf_dw_w, v_conf_dw_b=v_conf_dw_b, v_conf_ln_g=v_conf_ln_g, v_conf_ln_b=v_conf_ln_b, v_sc_conv_w=v_sc_conv_w, v_cd_w_out=v_cd_w_out)
    weights = {n: given[n] for n in TWIN_WEIGHTS}
    shared = {n: given[n] for n in SHARED_INPUTS}
    per_example = {n: given[n] for n in ['x', 'mem']}
    grad_fn = _jax.value_and_grad(_loss, argnums=(0, 1))

    def one_microbatch(ex, loss_target):
        ex = dict(ex)
        diff = ex.pop(TWIN_DIFF_INPUT)
        return grad_fn(weights, diff, {**shared, **ex}, loss_target)

    if N_MICROBATCH == 1:
        loss, (grad_w, grad_x) = one_microbatch(per_example, given["loss_target"])
    else:
        def body(carry, xs):
            loss_sum, grad_sum = carry
            l_k, (gw_k, gx_k) = one_microbatch(xs[0], xs[1])
            with _jax.named_scope("update"):
                return (loss_sum + l_k, _jax.tree.map(_jnp.add, grad_sum, gw_k)), gx_k

        init = (_jnp.zeros((), _jnp.float32), _jax.tree.map(_jnp.zeros_like, weights))
        (loss, grad_w), grad_x = _jax.lax.scan(body, init, (per_example, given["loss_target"]))
    with _jax.named_scope("update"):
        delta_w, new_m, new_v = {}, {}, {}
        for n in TWIN_WEIGHTS:
            delta_w[n], new_m[n], new_v[n] = _adamw(weights[n], grad_w[n], given["m_" + n], given["v_" + n])
    return (loss, grad_x, *[grad_w[n] for n in TWIN_WEIGHTS], *[delta_w[n] for n in TWIN_WEIGHTS],
            *[new_m[n] for n in TWIN_WEIGHTS], *[new_v[n] for n in TWIN_WEIGHTS])
```

```python
import functools
import math

import numpy as np
import jax
import jax.numpy as jnp
from jax import lax
from jax.experimental import pallas as pl
from jax.experimental.pallas import tpu as pltpu

BF = jnp.bfloat16
F32 = jnp.float32
HI = lax.Precision.HIGHEST

N_DEV = 8
D = 1024
N_MEM = 256
XA_HEADS = 4
XA_DH = D // XA_HEADS
POOL_GROUPS = 4
PG = 128
POOL_W = POOL_GROUPS * PG
SSM_INNER = 1024
SSM_GROUPS = 2
SSM_GSZ = SSM_INNER // SSM_GROUPS
SSM_HEADS = 16
SSM_P = 64
SSM_N = 128
SSM_CONV = 4
SSM_CONV_DIM = SSM_INNER + 2 * SSM_GROUPS * SSM_N
CHUNK = 128
AB_IN = POOL_W + SSM_INNER + SSM_CONV_DIM + SSM_HEADS
AB_IN_PAD = POOL_W + SSM_INNER + SSM_CONV_DIM + 128
AB_OUT = POOL_W + SSM_INNER
CONF_K = 31
SC_K = 3
CD_IN = 5 * D
CD_OUT = 2 * D
MLP_H = 4 * D
RMS_EPS = 1e-6
LN_EPS = 1e-5
ADAM_LR = 0.001
ADAM_B1 = 0.9
ADAM_B2 = 0.999
ADAM_EPS = 1e-08
ADAM_WD = 0.01
ADAM_STEP = 10
VMEM_LIMIT = 56 * 1024 * 1024
LANE = 128

NAMES = ['x', 'mem', 'norm_gains', 'xa_wq', 'xa_wkv', 'xa_wo', 'mlp_w1', 'mlp_w2', 'ab_w_in', 'pool_w', 'pool_scale',
         'ssm_conv_w', 'ssm_conv_b', 'ssm_dt_bias', 'ssm_a_log', 'ssm_d', 'ssm_norm', 'ab_w_out', 'cd_w_in', 'conf_dw_w',
         'conf_dw_b', 'conf_ln_g', 'conf_ln_b', 'sc_conv_w', 'cd_w_out', 'loss_target']
WEIGHTS = NAMES[2:25]
BIG = [('xa_wq', 1), ('xa_wkv', 2), ('xa_wo', 1), ('mlp_w1', 2), ('mlp_w2', 1), ('cd_w_in', 2), ('cd_w_out', 1),
       ('ab_w_out', 1), ('ab_w_in', 2)]
SMALL_SHARDED = ['norm_gains', 'ssm_conv_w', 'conf_dw_w', 'conf_dw_b', 'conf_ln_g', 'conf_ln_b', 'sc_conv_w']
REPLICATED = ['pool_w', 'pool_scale', 'ssm_conv_b', 'ssm_dt_bias', 'ssm_a_log', 'ssm_d', 'ssm_norm']


def _dg(a, b, ca, cb, prec=None):
    return lax.dot_general(a, b, (((ca,), (cb,)), ((), ())), precision=prec, preferred_element_type=F32)


@functools.partial(jax.custom_vjp, nondiff_argnums=(2, 3))
def bdot(a, b, ca, cb):
    return _dg(a.astype(BF), b.astype(BF), ca, cb)


def _bdot_fwd(a, b, ca, cb):
    return bdot(a, b, ca, cb), (a, b)


def _bdot_bwd(ca, cb, res, g):
    a, b = res
    g16, a16, b16 = g.astype(BF), a.astype(BF), b.astype(BF)
    da = _dg(g16, b16, 1, 1 - cb) if ca == 1 else _dg(b16, g16, 1 - cb, 1)
    db = _dg(g16, a16, 0, 1 - ca) if cb == 1 else _dg(a16, g16, 1 - ca, 0)
    return da.astype(a.dtype), db.astype(b.dtype)


bdot.defvjp(_bdot_fwd, _bdot_bwd)


@jax.custom_vjp
def cmat(a, c, ct):
    return _dg(a, c, 1, 0, HI)


def _cmat_fwd(a, c, ct):
    return cmat(a, c, ct), (c, ct)


def _cmat_bwd(res, g):
    c, ct = res
    return _dg(g, ct, 1, 0, HI), jnp.zeros_like(c), jnp.zeros_like(ct)


cmat.defvjp(_cmat_fwd, _cmat_bwd)


@jax.custom_vjp
def cmatl(c, ct, a):
    return _dg(c, a, 1, 0, HI)


def _cmatl_fwd(c, ct, a):
    return cmatl(c, ct, a), (c, ct)


def _cmatl_bwd(res, g):
    c, ct = res
    return jnp.zeros_like(c), jnp.zeros_like(ct), _dg(ct, g, 1, 0, HI)


cmatl.defvjp(_cmatl_fwd, _cmatl_bwd)


def _shift_down(x, k):
    t = lax.broadcasted_iota(jnp.int32, x.shape, 0)
    return jnp.where(t >= k, pltpu.roll(x, k, 0), 0.0)


def _shift_up(x, k):
    n = x.shape[0]
    t = lax.broadcasted_iota(jnp.int32, x.shape, 0)
    return jnp.where(t < n - k, pltpu.roll(x, n - k, 0), 0.0)


@functools.partial(jax.custom_vjp, nondiff_argnums=(1,))
def shift(x, k):
    return _shift_down(x, k)


def _shift_fwd(x, k):
    return _shift_down(x, k), None


def _shift_bwd(k, _, g):
    return (_shift_up(g, k),)


shift.defvjp(_shift_fwd, _shift_bwd)


@functools.partial(jax.custom_vjp, nondiff_argnums=(2,))
def cconv(u, w, width):
    acc = u * w[width - 1:width, :]
    for k in range(width - 1):
        acc = acc + _shift_down(u, width - 1 - k) * w[k:k + 1, :]
    return acc


def _cconv_fwd(u, w, width):
    return cconv(u, w, width), (u, w)


def _cconv_bwd(width, res, g):
    u, w = res
    rows = lax.broadcasted_iota(jnp.int32, w.shape, 0)
    du = g * w[width - 1:width, :]
    dw = jnp.where(rows == width - 1, jnp.sum(g * u, axis=0, keepdims=True), 0.0)
    for k in range(width - 1):
        s = width - 1 - k
        du = du + _shift_up(g, s) * w[k:k + 1, :]
        dw = dw + jnp.where(rows == k, jnp.sum(g * _shift_down(u, s), axis=0, keepdims=True), 0.0)
    return du, dw


cconv.defvjp(_cconv_fwd, _cconv_bwd)


def _rms(x, g):
    return x * lax.rsqrt(jnp.mean(x * x, axis=-1, keepdims=True) + RMS_EPS) * g


def _params(sem=None):
    return pltpu.CompilerParams(dimension_semantics=sem, vmem_limit_bytes=VMEM_LIMIT)


def _f32(v):
    return v if v.dtype == F32 else v.astype(F32)


def _first(axes):
    ok = None
    for ax in axes:
        c = pl.program_id(ax) == 0
        ok = c if ok is None else jnp.logical_and(ok, c)
    return ok


def fwd_call(fn, name, grid, ins, in_specs, out_shapes, out_specs):
    n_in = len(ins)

    def body(*refs):
        outs = fn(*[_f32(r[...]) for r in refs[:n_in]])
        for r, o in zip(refs[n_in:], outs):
            r[...] = o.astype(r.dtype)

    return pl.pallas_call(body, name=name, grid=grid, in_specs=in_specs, out_specs=out_specs, out_shape=out_shapes,
                          compiler_params=_params())(*ins)


def bwd_call(fn, name, grid, ins, in_specs, cots, cot_specs, gidx, g_shapes, g_specs, g_acc):
    n_in, n_cot = len(ins), len(cots)

    def body(*refs):
        vals = [_f32(r[...]) for r in refs[:n_in]]

        def f_sel(*dv):
            full = list(vals)
            for i, v in zip(gidx, dv):
                full[i] = v
            return tuple(fn(*full))

        outs, vjp = jax.vjp(f_sel, *[vals[i] for i in gidx])
        cts = tuple(_f32(r[...]) for r in refs[n_in:n_in + n_cot])
        grads = vjp(cts)
        for r, g, acc in zip(refs[n_in + n_cot:], grads, g_acc):
            if acc is None:
                r[...] = g.astype(r.dtype)
            else:
                @pl.when(_first(acc))
                def _():
                    r[...] = jnp.zeros_like(r)

                r[...] += g.astype(r.dtype)

    return pl.pallas_call(body, name=name, grid=grid, in_specs=list(in_specs) + list(cot_specs), out_specs=g_specs,
                          out_shape=g_shapes, compiler_params=_params())(*ins, *cots)


def _tile(dim, pref):
    if dim <= pref:
        return dim
    best = None
    for t in range(LANE, pref + 1, LANE):
        if dim % t == 0:
            best = t
    assert best is not None, dim
    return best


def matmul(a, b, mode, name, out_dtype=F32, tm=1024, tn=1024, tk=1024):
    if mode == 'nn':
        (m, k), (k2, n) = a.shape, b.shape
    elif mode == 'nt':
        (m, k), (n, k2) = a.shape, b.shape
    else:
        (k, m), (k2, n) = a.shape, b.shape
    assert k == k2, (name, a.shape, b.shape)
    tm, tn, tk = _tile(m, tm), _tile(n, tn), _tile(k, tk)
    nk = k // tk
    ca = 0 if mode == 'tn' else 1
    cb = 1 if mode == 'nt' else 0
    a_spec = pl.BlockSpec((tk, tm), lambda i, j, kk: (kk, i)) if mode == 'tn' else pl.BlockSpec((tm, tk), lambda i, j, kk: (i, kk))
    b_spec = pl.BlockSpec((tn, tk), lambda i, j, kk: (j, kk)) if mode == 'nt' else pl.BlockSpec((tk, tn), lambda i, j, kk: (kk, j))

    def body(a_ref, b_ref, o_ref, acc):
        kk = pl.program_id(2)

        @pl.when(kk == 0)
        def _():
            acc[...] = jnp.zeros_like(acc)

        acc[...] += _dg(a_ref[...].astype(BF), b_ref[...].astype(BF), ca, cb)

        @pl.when(kk == nk - 1)
        def _():
            o_ref[...] = acc[...].astype(o_ref.dtype)

    return pl.pallas_call(
        body, name=name, grid=(m // tm, n // tn, nk), in_specs=[a_spec, b_spec],
        out_specs=pl.BlockSpec((tm, tn), lambda i, j, kk: (i, j)), out_shape=jax.ShapeDtypeStruct((m, n), out_dtype),
        scratch_shapes=[pltpu.VMEM((tm, tn), F32)],
        compiler_params=_params(("parallel", "parallel", "arbitrary")))(a, b)


_FLIPS = [(0, 0, 1), (1, 0, 0), (0, 1, 0), (1, 1, 0), (1, 0, 1), (0, 1, 1), (1, 1, 1)]


def _me():
    return lax.axis_index("x"), lax.axis_index("y"), lax.axis_index("c")


def _flip(pos, f):
    return tuple(jnp.where(fi == 1, 1 - p, p) if fi else p for p, fi in zip(pos, f))


def _slot(pos):
    return 4 * pos[0] + 2 * pos[1] + pos[2]


def all_gather(v, name):
    def body(v_ref, out_ref, send_sems, recv_sems, local_sem):
        me = _me()
        sibling = _flip(me, (0, 0, 1))
        chips = [_flip(me, f) for f in ((1, 0, 0), (0, 1, 0), (1, 1, 0))]

        def copy(k, block, to, src=None):
            return pltpu.make_async_remote_copy(
                src_ref=out_ref.at[_slot(block)] if src is None else src, dst_ref=out_ref.at[_slot(block)],
                send_sem=send_sems.at[k], recv_sem=recv_sems.at[k], device_id=to, device_id_type=pl.DeviceIdType.MESH)

        mine = pltpu.make_async_copy(v_ref, out_ref.at[_slot(me)], local_sem)
        mine.start()
        first = [copy(0, me, sibling, src=v_ref)] + [copy(1 + j, me, chip, src=v_ref) for j, chip in enumerate(chips)]
        for cp in first:
            cp.start()
        passed = [copy(4 + j, chip, sibling) for j, chip in enumerate(chips)]
        for j, chip in enumerate(chips):
            copy(1 + j, chip, me).wait_recv()
            passed[j].start()
        copy(0, sibling, me).wait_recv()
        for j, chip in enumerate(chips):
            copy(4 + j, _flip(chip, (0, 0, 1)), me).wait_recv()
        for cp in first + passed:
            cp.wait_send()
        mine.wait()

    return pl.pallas_call(
        body, name=name, out_shape=jax.ShapeDtypeStruct((N_DEV,) + v.shape, v.dtype),
        in_specs=[pl.BlockSpec(memory_space=pl.ANY)], out_specs=pl.BlockSpec(memory_space=pl.ANY),
        scratch_shapes=[pltpu.SemaphoreType.DMA((7,)), pltpu.SemaphoreType.DMA((7,)), pltpu.SemaphoreType.DMA(())],
    )(v)


def all_to_all(v, name):
    def body(v_ref, out_ref, send_sems, recv_sems, local_sem):
        me = _me()
        mine = pltpu.make_async_copy(v_ref.at[_slot(me)], out_ref.at[_slot(me)], local_sem)
        mine.start()
        copies = []
        for k, f in enumerate(_FLIPS):
            peer = _flip(me, f)
            cp = pltpu.make_async_remote_copy(
                src_ref=v_ref.at[_slot(peer)], dst_ref=out_ref.at[_slot(me)], send_sem=send_sems.at[k],
                recv_sem=recv_sems.at[k], device_id=peer, device_id_type=pl.DeviceIdType.MESH)
            cp.start()
            copies.append(cp)
        for k, f in enumerate(_FLIPS):
            peer = _flip(me, f)
            pltpu.make_async_remote_copy(
                src_ref=v_ref.at[_slot(peer)], dst_ref=out_ref.at[_slot(peer)], send_sem=send_sems.at[k],
                recv_sem=recv_sems.at[k], device_id=peer, device_id_type=pl.DeviceIdType.MESH).wait_recv()
        for cp in copies:
            cp.wait_send()
        mine.wait()

    return pl.pallas_call(
        body, name=name, out_shape=jax.ShapeDtypeStruct(v.shape, v.dtype),
        in_specs=[pl.BlockSpec(memory_space=pl.ANY)], out_specs=pl.BlockSpec(memory_space=pl.ANY),
        scratch_shapes=[pltpu.SemaphoreType.DMA((7,)), pltpu.SemaphoreType.DMA((7,)), pltpu.SemaphoreType.DMA(())],
    )(v)


def sum_slots(v, name, tr=256):
    _, r, c = v.shape
    tr = _tile_rows(r, tr)

    def body(v_ref, o_ref):
        acc = v_ref[0].astype(F32)
        for s in range(1, N_DEV):
            acc = acc + v_ref[s].astype(F32)
        o_ref[...] = acc

    return pl.pallas_call(body, name=name, grid=(r // tr,), in_specs=[pl.BlockSpec((N_DEV, tr, c), lambda i: (0, i, 0))],
                          out_specs=pl.BlockSpec((tr, c), lambda i: (i, 0)), out_shape=jax.ShapeDtypeStruct((r, c), F32),
                          compiler_params=_params())(v)


def _tile_rows(r, pref):
    if r <= pref:
        return r
    best = None
    for t in range(8, pref + 1, 8):
        if r % t == 0:
            best = t
    return r if best is None else best


def adamw(w, m, v, g, name):
    r, c = w.shape
    tr = _tile_rows(r, 512 if c <= 1024 else 128)

    def body(w_ref, m_ref, v_ref, g_ref, d_ref, nm_ref, nv_ref):
        gg = g_ref[...]
        nm = ADAM_B1 * m_ref[...] + (1.0 - ADAM_B1) * gg
        nv = ADAM_B2 * v_ref[...] + (1.0 - ADAM_B2) * jnp.square(gg)
        m_hat = nm / (1.0 - ADAM_B1 ** ADAM_STEP)
        v_hat = nv / (1.0 - ADAM_B2 ** ADAM_STEP)
        d_ref[...] = -ADAM_LR * (m_hat / (jnp.sqrt(v_hat) + ADAM_EPS) + ADAM_WD * w_ref[...])
        nm_ref[...] = nm
        nv_ref[...] = nv

    spec = pl.BlockSpec((tr, c), lambda i: (i, 0))
    sh = jax.ShapeDtypeStruct((r, c), F32)
    return pl.pallas_call(body, name=name, grid=(r // tr,), in_specs=[spec] * 4, out_specs=[spec] * 3,
                          out_shape=[sh] * 3, compiler_params=_params())(w, m, v, g)


def seg_in(x, g):
    return (_rms(x, g),)


def seg_in_res(x, g):
    return x, _rms(x, g)


def seg_res(x, m, ga, gb):
    x1 = x + _rms(m, ga)
    return x1, _rms(x1, gb)


def seg_out(x, m, ga):
    return (x + _rms(m, ga),)


def seg_act(r):
    t = jnp.maximum(r, 0.0)
    return (t * t,)


def seg_ln(v, g, b):
    mu = jnp.mean(v, axis=-1, keepdims=True)
    var = jnp.mean(jnp.square(v - mu), axis=-1, keepdims=True)
    vn = (v - mu) * lax.rsqrt(var + LN_EPS) * g + b
    return (jax.nn.silu(vn),)


def make_pool_fn(group):
    window = 2 ** (group + 1)

    def pool_fn(ug, pw, scale):
        s = ug
        for lvl in range(group + 1):
            s = s + shift(s, 2 ** lvl)
        cnt = jnp.minimum(lax.broadcasted_iota(jnp.int32, ug.shape, 0) + 1, window).astype(F32)
        return (bdot(s / cnt - ug, pw, 1, 0) * scale,)

    return pool_fn


def conv4_fn(xr, w, b):
    return (jax.nn.silu(cconv(xr, w, SSM_CONV) + b),)


def cd1_fn(val, gate, bg, cg, hh, dww, dwb, scw):
    v = val * jax.nn.sigmoid(gate)
    vc = cconv(v, dww, CONF_K) + dwb
    sc = bg * cconv(cg * hh, scw, SC_K)
    return vc, sc


def attn_fn(q, k, v):
    s = bdot(q, k, 1, 1) / math.sqrt(XA_DH)
    p = jax.nn.softmax(s, axis=-1)
    return (bdot(p, v, 1, 0),)


def ssd_chunk(xs, bm, cm, z, dtraw, dtb, alog, dsk, nw, h0, h1, h2, h3, e64, e64t, e128, e128t, tril, trilt):
    hin = (h0, h1, h2, h3)
    dt = jax.nn.softplus(dtraw + dtb)
    a = -jnp.exp(alog)
    d_a = dt * a
    cs = cmatl(tril, trilt, d_a)
    cs64 = cmat(cs, e64, e64t)
    dt64 = cmat(dt, e64, e64t)
    tot64 = jnp.sum(cmat(d_a, e64, e64t), axis=0, keepdims=True)
    cs128 = cmat(cs, e128, e128t)
    d64 = cmat(jnp.broadcast_to(dsk, (8, LANE)), e64, e64t)[0:1, :]
    xdt = xs * dt64
    cb = bdot(cm, bm, 1, 1)
    li = lax.broadcasted_iota(jnp.int32, (CHUNK, CHUNK), 0)
    si = lax.broadcasted_iota(jnp.int32, (CHUNK, CHUNK), 1)
    causal = li >= si
    lane = lax.broadcasted_iota(jnp.int32, (CHUNK, LANE), 1)
    xw = xdt * jnp.exp(tot64 - cs64)
    ecs = jnp.exp(cs64)
    etot = jnp.exp(tot64)
    ycols, hout = [], []
    for j in range(4):
        sl = slice(j * LANE, (j + 1) * LANE)
        xj = xdt[:, sl]
        ys = []
        for hh in range(2):
            r = 2 * j + hh
            col = cs128[:, r * LANE:(r + 1) * LANE]
            decay = jnp.exp(jnp.where(causal, col - col.T, -1e30))
            ys.append(bdot(cb * decay, xj, 1, 0))
        y_diag = jnp.where(lane < SSM_P, ys[0], ys[1])
        y_off = bdot(cm, hin[j], 1, 0) * ecs[:, sl]
        ycols.append(y_diag + y_off)
        hout.append(etot[:, sl] * hin[j] + bdot(bm, xw[:, sl], 0, 0))
    y = jnp.concatenate(ycols, axis=1) + d64 * xs
    y = y * jax.nn.silu(z)
    yn = y * lax.rsqrt(jnp.mean(y * y, axis=-1, keepdims=True) + RMS_EPS) * nw
    return (yn,) + tuple(hout)


def _ssd_consts():
    h = np.arange(LANE)[:, None]
    e64 = np.stack([(h == g * 8 + np.arange(SSM_GSZ)[None, :] // SSM_P) for g in range(SSM_GROUPS)]).astype(np.float32)
    e128 = np.stack([(h == g * 8 + np.arange(8 * LANE)[None, :] // LANE) for g in range(SSM_GROUPS)]).astype(np.float32)
    tril = np.tril(np.ones((CHUNK, CHUNK), np.float32))
    return (jnp.asarray(e64), jnp.asarray(e64.transpose(0, 2, 1)), jnp.asarray(e128), jnp.asarray(e128.transpose(0, 2, 1)),
            jnp.asarray(tril), jnp.asarray(tril.T))


def _ssd_specs(nc, rev):
    def ci(c):
        return nc - 1 - c if rev else c

    def row(width, col):
        return pl.BlockSpec((CHUNK, width), lambda g, b, c: (b * nc + ci(c), col(g)))

    data = [row(SSM_GSZ, lambda g: g), row(SSM_N, lambda g: 8 + g), row(SSM_N, lambda g: 10 + g),
            row(SSM_GSZ, lambda g: 1 + g), row(LANE, lambda g: 24)]
    par = [pl.BlockSpec((1, LANE), lambda g, b, c: (0, 0))] * 3 + [pl.BlockSpec((1, SSM_GSZ), lambda g, b, c: (0, g))]
    cst = [pl.BlockSpec((None, LANE, SSM_GSZ), lambda g, b, c: (g, 0, 0)), pl.BlockSpec((None, SSM_GSZ, LANE), lambda g, b, c: (g, 0, 0)),
           pl.BlockSpec((None, LANE, 8 * LANE), lambda g, b, c: (g, 0, 0)), pl.BlockSpec((None, 8 * LANE, LANE), lambda g, b, c: (g, 0, 0)),
           pl.BlockSpec((CHUNK, CHUNK), lambda g, b, c: (0, 0)), pl.BlockSpec((CHUNK, CHUNK), lambda g, b, c: (0, 0))]
    hsave = pl.BlockSpec((None, None, None, 4, SSM_N, LANE), lambda g, b, c: (g, b, ci(c), 0, 0, 0))
    yn = row(SSM_GSZ, lambda g: g)
    return data, par, cst, hsave, yn, row


def ssd_fwd(xbc_act, u, dtb, alog, dsk, nw, consts, bsz, seq):
    nc = seq // CHUNK
    data, par, cst, hsave, yn_spec, _ = _ssd_specs(nc, False)

    def body(xs, bm, cm, z, dtr, dtb_r, alog_r, dsk_r, nw_r, e64, e64t, e128, e128t, tril, trilt, yn_ref, hs_ref, h):
        @pl.when(pl.program_id(2) == 0)
        def _():
            h[...] = jnp.zeros_like(h)

        hs_ref[...] = h[...]
        outs = ssd_chunk(xs[...], bm[...], cm[...], z[...], dtr[...], dtb_r[...], alog_r[...], dsk_r[...], nw_r[...],
                         h[0], h[1], h[2], h[3], e64[...], e64t[...], e128[...], e128t[...], tril[...], trilt[...])
        yn_ref[...] = outs[0].astype(yn_ref.dtype)
        for j in range(4):
            h[j] = outs[1 + j]

    t = bsz * seq
    return pl.pallas_call(
        body, name="ssd_fwd", grid=(SSM_GROUPS, bsz, nc), in_specs=data + par + cst, out_specs=[yn_spec, hsave],
        out_shape=[jax.ShapeDtypeStruct((t, SSM_INNER), BF), jax.ShapeDtypeStruct((SSM_GROUPS, bsz, nc, 4, SSM_N, LANE), F32)],
        scratch_shapes=[pltpu.VMEM((4, SSM_N, LANE), F32)], compiler_params=_params(),
    )(xbc_act, xbc_act, xbc_act, u, u, dtb, alog, dsk, nw, *consts)


def ssd_bwd(xbc_act, u, dtb, alog, dsk, nw, consts, hs, dmix, bsz, seq):
    nc = seq // CHUNK
    data, par, cst, hsave, _, row = _ssd_specs(nc, True)
    t = bsz * seq
    dyn_spec = row(SSM_GSZ, lambda g: POOL_W // SSM_GSZ + g)

    def body(xs, bm, cm, z, dtr, dtb_r, alog_r, dsk_r, nw_r, e64, e64t, e128, e128t, tril, trilt, hs_ref, dyn_ref,
             dxs, dbm, dcm, dz, ddt, ddtb, dalog, ddsk, dnw, dh):
        @pl.when(pl.program_id(2) == 0)
        def _():
            dh[...] = jnp.zeros_like(dh)

        cst_vals = (e64[...], e64t[...], e128[...], e128t[...], tril[...], trilt[...])

        def f(*args):
            return ssd_chunk(*args, *cst_vals)

        prim = (xs[...], bm[...], cm[...], z[...], dtr[...], dtb_r[...], alog_r[...], dsk_r[...], nw_r[...],
                hs_ref[0], hs_ref[1], hs_ref[2], hs_ref[3])
        _, vjp = jax.vjp(f, *prim)
        g = vjp((dyn_ref[...].astype(F32), dh[0], dh[1], dh[2], dh[3]))
        dxs[...] = g[0]
        dbm[...] = g[1]
        dcm[...] = g[2]
        dz[...] = g[3]
        ddt[...] = g[4]

        @pl.when(_first((1, 2)))
        def _():
            for r in (ddtb, dalog, ddsk, dnw):
                r[...] = jnp.zeros_like(r)

        ddtb[...] += g[5]
        dalog[...] += g[6]
        ddsk[...] += g[7]
        dnw[...] += g[8]
        for j in range(4):
            dh[j] = g[9 + j]

    gpar = pl.BlockSpec((None, 1, LANE), lambda g, b, c: (g, 0, 0))
    out_specs = [row(SSM_GSZ, lambda g: g), row(SSM_N, lambda g: g), row(SSM_N, lambda g: g), row(SSM_GSZ, lambda g: g),
                 pl.BlockSpec((None, CHUNK, LANE), lambda g, b, c: (g, b * nc + nc - 1 - c, 0)),
                 gpar, gpar, gpar, pl.BlockSpec((1, SSM_GSZ), lambda g, b, c: (0, g))]
    gp = jax.ShapeDtypeStruct((SSM_GROUPS, 1, LANE), F32)
    out_shape = [jax.ShapeDtypeStruct((t, SSM_INNER), F32), jax.ShapeDtypeStruct((t, SSM_GROUPS * SSM_N), F32),
                 jax.ShapeDtypeStruct((t, SSM_GROUPS * SSM_N), F32), jax.ShapeDtypeStruct((t, SSM_INNER), F32),
                 jax.ShapeDtypeStruct((SSM_GROUPS, t, LANE), F32), gp, gp, gp, jax.ShapeDtypeStruct((1, SSM_INNER), F32)]
    return pl.pallas_call(
        body, name="ssd_bwd", grid=(SSM_GROUPS, bsz, nc), in_specs=data + par + cst + [hsave, dyn_spec], out_specs=out_specs,
        out_shape=out_shape, scratch_shapes=[pltpu.VMEM((4, SSM_N, LANE), F32)], compiler_params=_params(),
    )(xbc_act, xbc_act, xbc_act, u, u, dtb, alog, dsk, nw, *consts, hs, dmix)


TB = 512


def _rows(d, col=0):
    return pl.BlockSpec((TB, d), lambda i: (i, col))


def _par(d):
    return pl.BlockSpec((1, d), lambda i: (0, 0))


def _sd(shape, dtype=F32):
    return jax.ShapeDtypeStruct(shape, dtype)


FLAT_ROW_TILE = 256


def _round_up(n, m):
    return -(-n // m) * m


def _pad_rows(a, rows):
    return jnp.pad(a, ((0, rows - a.shape[0]), (0, 0)))


def _pack128(arrs):
    flat = jnp.concatenate([a.reshape(-1) for a in arrs])
    n = flat.shape[0]
    rows = -(-n // (8 * LANE)) * 8
    return jnp.pad(flat, (0, rows * LANE - n)).reshape(rows, LANE)


def _unpack128(packed, shapes):
    flat = packed.reshape(-1)
    out, off = [], 0
    for s in shapes:
        n = int(np.prod(s))
        out.append(flat[off:off + n].reshape(s))
        off += n
    return out


def kernel(x, mem, norm_gains, xa_wq, xa_wkv, xa_wo, mlp_w1, mlp_w2, ab_w_in, pool_w, pool_scale, ssm_conv_w, ssm_conv_b, ssm_dt_bias, ssm_a_log, ssm_d, ssm_norm, ab_w_out, cd_w_in, conf_dw_w, conf_dw_b, conf_ln_g, conf_ln_b, sc_conv_w, cd_w_out, loss_target, m_norm_gains, m_xa_wq, m_xa_wkv, m_xa_wo, m_mlp_w1, m_mlp_w2, m_ab_w_in, m_pool_w, m_pool_scale, m_ssm_conv_w, m_ssm_conv_b, m_ssm_dt_bias, m_ssm_a_log, m_ssm_d, m_ssm_norm, m_ab_w_out, m_cd_w_in, m_conf_dw_w, m_conf_dw_b, m_conf_ln_g, m_conf_ln_b, m_sc_conv_w, m_cd_w_out, v_norm_gains, v_xa_wq, v_xa_wkv, v_xa_wo, v_mlp_w1, v_mlp_w2, v_ab_w_in, v_pool_w, v_pool_scale, v_ssm_conv_w, v_ssm_conv_b, v_ssm_dt_bias, v_ssm_a_log, v_ssm_d, v_ssm_norm, v_ab_w_out, v_cd_w_in, v_conf_dw_w, v_conf_dw_b, v_conf_ln_g, v_conf_ln_b, v_sc_conv_w, v_cd_w_out):
    args = locals()
    w = {n: args[n] for n in WEIGHTS}
    mom_m = {n: args["m_" + n] for n in WEIGHTS}
    mom_v = {n: args["v_" + n] for n in WEIGHTS}
    bsz, seq, _ = x.shape
    full = gather_weights(w)
    loss_local, grad_x, grads = local_step(x, mem, loss_target, full)
    loss = lax.psum(loss_local, ("x", "y", "c"))
    g_own = reduce_grads(grads, w)
    outs = {}
    for n in WEIGHTS:
        shp = w[n].shape
        view = (-1, shp[-1]) if shp[-1] >= LANE else (1, -1)
        if n in SMALL_SHARDED or n in REPLICATED:
            continue
        d, nm, nv = adamw(w[n].reshape(view), mom_m[n].reshape(view), mom_v[n].reshape(view), g_own[n].reshape(view), "adamw_" + n)
        outs[n] = (g_own[n], d.reshape(shp), nm.reshape(shp), nv.reshape(shp))
    small = SMALL_SHARDED + REPLICATED
    shapes = [w[n].shape for n in small]
    d, nm, nv = adamw(_pack128([w[n] for n in small]), _pack128([mom_m[n] for n in small]), _pack128([mom_v[n] for n in small]),
                      _pack128([g_own[n] for n in small]), "adamw_small")
    for n, dd, mm, vv in zip(small, _unpack128(d, shapes), _unpack128(nm, shapes), _unpack128(nv, shapes)):
        outs[n] = (g_own[n], dd, mm, vv)
    return (loss, grad_x.reshape(x.shape), *[outs[n][0] for n in WEIGHTS], *[outs[n][1] for n in WEIGHTS],
            *[outs[n][2] for n in WEIGHTS], *[outs[n][3] for n in WEIGHTS])


def _big_rows(shape):
    return int(np.prod(shape)) // D


def gather_weights(w):
    flat = jnp.concatenate([w[n].astype(BF).reshape(-1, D) for n, _ in BIG], axis=0)
    flat = _pad_rows(flat, _round_up(flat.shape[0], FLAT_ROW_TILE))
    gathered = all_gather(flat, "gather_big")
    full, off = {}, 0
    for n, axis in BIG:
        shp = w[n].shape
        rows = _big_rows(shp)
        blk = gathered[:, off:off + rows].reshape((N_DEV,) + shp)
        off += rows
        if axis == 1:
            full[n] = blk.transpose(1, 0, 2, 3).reshape(shp[0], N_DEV * shp[1], shp[2])
        else:
            full[n] = blk.transpose(1, 2, 0, 3).reshape(shp[0], shp[1], N_DEV * shp[2])
    shapes = [w[n].shape for n in SMALL_SHARDED]
    gs = all_gather(_pack128([w[n] for n in SMALL_SHARDED]), "gather_small")
    per_dev = [_unpack128(gs[d], shapes) for d in range(N_DEV)]
    for i, n in enumerate(SMALL_SHARDED):
        full[n] = jnp.concatenate([per_dev[d][i] for d in range(N_DEV)], axis=-1)
    for n in REPLICATED:
        full[n] = w[n]
    return full


def reduce_grads(grads, w):
    me = _slot(_me())
    blocks = []
    for n, axis in BIG:
        g = grads[n]
        shp = w[n].shape
        if axis == 1:
            blk = g.reshape(shp[0], N_DEV, shp[1], shp[2]).transpose(1, 0, 2, 3)
        else:
            blk = g.reshape(shp[0], shp[1], N_DEV, shp[2]).transpose(2, 0, 1, 3)
        blocks.append(blk.reshape(N_DEV, -1, D))
    send = jnp.concatenate(blocks, axis=1)
    send = jnp.pad(send, ((0, 0), (0, _round_up(send.shape[1], FLAT_ROW_TILE) - send.shape[1]), (0, 0)))
    got = all_to_all(send, "scatter_big")
    summed = sum_slots(got, "sum_big", FLAT_ROW_TILE)
    out, off = {}, 0
    for n, _ in BIG:
        rows = _big_rows(w[n].shape)
        out[n] = summed[off:off + rows].reshape(w[n].shape)
        off += rows
    small = SMALL_SHARDED + REPLICATED
    gs = all_gather(_pack128([grads[n] for n in small]), "gather_small_grads")
    tot = _unpack128(sum_slots(gs, "sum_small"), [grads[n].shape for n in small])
    for n, g in zip(small, tot):
        if n in SMALL_SHARDED:
            width = w[n].shape[-1]
            g = lax.dynamic_slice_in_dim(g, me * width, width, axis=g.ndim - 1)
        out[n] = g
    return out


def local_step(x, mem, target, p):
    bsz, seq, _ = x.shape
    t = bsz * seq
    nb = t // TB
    nc = seq // CHUNK
    x0 = x.reshape(t, D)
    mem2 = mem.reshape(bsz * N_MEM, D)
    tgt = target.reshape(t, D)
    gains = p['norm_gains']

    def gain(layer, i):
        return gains[layer, i].reshape(1, D)

    consts = _ssd_consts()
    grads = {}
    saved = [dict(), dict()]

    def run_seg_res(xin, m, ga, gb, name):
        return fwd_call(seg_res, name, (nb,), [xin, m, ga, gb], [_rows(D), _rows(D), _par(D), _par(D)],
                        [_sd((t, D)), _sd((t, D), BF)], [_rows(D), _rows(D)])

    def attn_specs():
        nq = seq // TB
        q = pl.BlockSpec((TB, XA_DH), lambda b, h, i: (b * nq + i, h))
        k = pl.BlockSpec((N_MEM, XA_DH), lambda b, h, i: (b, h))
        v = pl.BlockSpec((N_MEM, XA_DH), lambda b, h, i: (b, XA_HEADS + h))
        return (bsz, XA_HEADS, nq), q, k, v

    def attention_fwd(layer, xin, hin, sv):
        q = matmul(hin, p['xa_wq'][layer], 'nn', f"q_{layer}", BF)
        kv = matmul(mem2, p['xa_wkv'][layer], 'nn', f"kv_{layer}", BF)
        grid, qs, ks, vs = attn_specs()
        o, = fwd_call(attn_fn, f"attn_{layer}", grid, [q, kv, kv], [qs, ks, vs], [_sd((t, D), BF)], [qs])
        ao = matmul(o, p['xa_wo'][layer], 'nn', f"ao_{layer}")
        sv.update(q=q, kv=kv, o=o, ao=ao)
        return ao

    def mlp_fwd(layer, hin, sv):
        r = matmul(hin, p['mlp_w1'][layer], 'nn', f"mlp1_{layer}")
        rr, = fwd_call(seg_act, f"act_{layer}", (nb,), [r], [_rows(MLP_H)], [_sd((t, MLP_H), BF)], [_rows(MLP_H)])
        mo = matmul(rr, p['mlp_w2'][layer], 'nn', f"mlp2_{layer}")
        sv.update(r=r, rr=rr, mo=mo)
        return mo

    sv = saved[0]
    h0, = fwd_call(seg_in, "norm_in", (nb,), [x0, gain(0, 0)], [_rows(D), _par(D)], [_sd((t, D), BF)], [_rows(D)])
    w_ab_in = jnp.pad(p['ab_w_in'][0], ((0, 0), (0, AB_IN_PAD - AB_IN)))
    u0 = matmul(h0, w_ab_in, 'nn', "ab_in")
    pool_outs = []
    for g in range(POOL_GROUPS):
        seqspec = pl.BlockSpec((seq, PG), lambda b, g=g: (b, g))
        po, = fwd_call(make_pool_fn(g), f"pool_{g}", (bsz,), [u0, p['pool_w'][0, g], p['pool_scale']],
                       [seqspec, pl.BlockSpec((PG, PG), lambda b: (0, 0)), pl.BlockSpec((1, PG), lambda b, g=g: (0, g))],
                       [_sd((t, PG), BF)], [pl.BlockSpec((seq, PG), lambda b: (b, 0))])
        pool_outs.append(po)
    cw = 256
    ncb = SSM_CONV_DIM // cw
    cbase = (POOL_W + SSM_INNER) // cw
    conv_in_specs = [pl.BlockSpec((seq, cw), lambda j, b: (b, cbase + j)), pl.BlockSpec((SSM_CONV, cw), lambda j, b: (0, j)),
                     pl.BlockSpec((1, cw), lambda j, b: (0, j))]
    conv_out_spec = pl.BlockSpec((seq, cw), lambda j, b: (b, j))
    xbc_act, = fwd_call(conv4_fn, "ssm_conv", (ncb, bsz), [u0, p['ssm_conv_w'][0], p['ssm_conv_b']], conv_in_specs,
                        [_sd((t, SSM_CONV_DIM))], [conv_out_spec])
    dtb = jnp.pad(p['ssm_dt_bias'], ((0, 0), (0, LANE - SSM_HEADS)))
    alog = jnp.pad(p['ssm_a_log'], ((0, 0), (0, LANE - SSM_HEADS)))
    dsk = jnp.pad(p['ssm_d'], ((0, 0), (0, LANE - SSM_HEADS)))
    yn, hs = ssd_fwd(xbc_act, u0, dtb, alog, dsk, p['ssm_norm'], consts, bsz, seq)
    mix0 = jnp.concatenate(pool_outs + [yn], axis=1)
    m0 = matmul(mix0, p['ab_w_out'][0], 'nn', "ab_out")
    x1, h2 = run_seg_res(x0, m0, gain(0, 1), gain(0, 2), "res_0a")
    ao0 = attention_fwd(0, x1, h2, sv)
    x2, h3 = run_seg_res(x1, ao0, gain(0, 3), gain(0, 4), "res_0b")
    mo0 = mlp_fwd(0, h3, sv)
    x3, h4 = run_seg_res(x2, mo0, gain(0, 5), gain(1, 0), "res_0c")

    sv1 = saved[1]
    u1 = matmul(h4, p['cd_w_in'][0], 'nn', "cd_in")
    nd = D // LANE

    def cd_col(k):
        return pl.BlockSpec((seq, LANE), lambda j, b, k=k: (b, k * nd + j))

    cd_par = [pl.BlockSpec((CONF_K, LANE), lambda j, b: (0, j)), pl.BlockSpec((1, LANE), lambda j, b: (0, j)),
              pl.BlockSpec((SC_K, LANE), lambda j, b: (0, j))]
    cd_ins = [u1] * 5 + [p['conf_dw_w'][0], p['conf_dw_b'], p['sc_conv_w'][0]]
    cd_in_specs = [cd_col(k) for k in range(5)] + cd_par
    cd_out_spec = pl.BlockSpec((seq, LANE), lambda j, b: (b, j))
    vconv, sc_out = fwd_call(cd1_fn, "cd_conv", (nd, bsz), cd_ins, cd_in_specs, [_sd((t, D)), _sd((t, D), BF)],
                             [cd_out_spec, cd_out_spec])
    conf, = fwd_call(seg_ln, "conf_ln", (nb,), [vconv, p['conf_ln_g'], p['conf_ln_b']], [_rows(D), _par(D), _par(D)],
                     [_sd((t, D), BF)], [_rows(D)])
    mix1 = jnp.concatenate([conf, sc_out], axis=1)
    m1 = matmul(mix1, p['cd_w_out'][0], 'nn', "cd_out")
    x4, h5 = run_seg_res(x3, m1, gain(1, 1), gain(1, 2), "res_1a")
    ao1 = attention_fwd(1, x4, h5, sv1)
    x5, h6 = run_seg_res(x4, ao1, gain(1, 3), gain(1, 4), "res_1b")
    mo1 = mlp_fwd(1, h6, sv1)

    def loss_body(x_ref, m_ref, g_ref, t_ref, dy_ref, acc_ref):
        y = x_ref[...] + _rms(m_ref[...], g_ref[...])
        d = y - t_ref[...]
        dy_ref[...] = d / float(D)

        @pl.when(pl.program_id(0) == 0)
        def _():
            acc_ref[...] = jnp.zeros_like(acc_ref)

        acc_ref[...] += jnp.sum(d * d, axis=0, keepdims=True)

    dy, lanes = pl.pallas_call(
        loss_body, name="loss_head", grid=(nb,), in_specs=[_rows(D), _rows(D), _par(D), _rows(D)],
        out_specs=[_rows(D), _par(D)], out_shape=[_sd((t, D)), _sd((1, D))], compiler_params=_params())(x5, mo1, gain(1, 5), tgt)
    loss = 0.5 * jnp.sum(lanes) / float(D)

    gain_grads = {}

    def bwd_seg_out(xin, m, ga, dyv, name):
        dx, dm, dga = bwd_call(seg_out, name, (nb,), [xin, m, ga], [_rows(D), _rows(D), _par(D)], [dyv], [_rows(D)],
                               [0, 1, 2], [_sd((t, D)), _sd((t, D), BF), _sd((1, D))], [_rows(D), _rows(D), _par(D)],
                               [None, None, (0,)])
        return dx, dm, dga

    def bwd_seg_res(xin, m, ga, gb, dx1, dh, name):
        return bwd_call(seg_res, name, (nb,), [xin, m, ga, gb], [_rows(D), _rows(D), _par(D), _par(D)], [dx1, dh],
                        [_rows(D), _rows(D)], [0, 1, 2, 3], [_sd((t, D)), _sd((t, D), BF), _sd((1, D)), _sd((1, D))],
                        [_rows(D), _rows(D), _par(D), _par(D)], [None, None, (0,), (0,)])

    def mlp_bwd(layer, hin, dmo, sv):
        grads_w2 = matmul(sv['rr'], dmo, 'tn', f"d_mlp_w2_{layer}", BF)
        drr = matmul(dmo, p['mlp_w2'][layer], 'nt', f"d_rr_{layer}")
        dr, = bwd_call(seg_act, f"d_act_{layer}", (nb,), [sv['r']], [_rows(MLP_H)], [drr], [_rows(MLP_H)], [0],
                       [_sd((t, MLP_H), BF)], [_rows(MLP_H)], [None])
        grads_w1 = matmul(hin, dr, 'tn', f"d_mlp_w1_{layer}", BF)
        dh = matmul(dr, p['mlp_w1'][layer], 'nt', f"d_h_mlp_{layer}")
        return dh, grads_w1, grads_w2

    def attention_bwd(layer, hin, dao, sv):
        g_wo = matmul(sv['o'], dao, 'tn', f"d_xa_wo_{layer}", BF)
        do = matmul(dao, p['xa_wo'][layer], 'nt', f"d_o_{layer}", BF)
        grid, qs, ks, vs = attn_specs()
        kvo = pl.BlockSpec((N_MEM, XA_DH), lambda b, h, i: (b, h))
        dq, dk, dv = bwd_call(attn_fn, f"d_attn_{layer}", grid, [sv['q'], sv['kv'], sv['kv']], [qs, ks, vs], [do], [qs],
                              [0, 1, 2], [_sd((t, D), BF), _sd((bsz * N_MEM, D)), _sd((bsz * N_MEM, D))], [qs, kvo, kvo],
                              [None, (2,), (2,)])
        dkv = jnp.concatenate([dk, dv], axis=1)
        g_wkv = matmul(mem2, dkv, 'tn', f"d_xa_wkv_{layer}", BF)
        g_wq = matmul(hin, dq, 'tn', f"d_xa_wq_{layer}", BF)
        dh = matmul(dq, p['xa_wq'][layer], 'nt', f"d_h_attn_{layer}")
        return dh, g_wq, g_wkv, g_wo

    per_layer = {k: [None, None] for k in ('xa_wq', 'xa_wkv', 'xa_wo', 'mlp_w1', 'mlp_w2')}

    dx5, dmo1, gain_grads[(1, 5)] = bwd_seg_out(x5, mo1, gain(1, 5), dy, "d_out")
    dh6, per_layer['mlp_w1'][1], per_layer['mlp_w2'][1] = mlp_bwd(1, h6, dmo1, sv1)
    dx4, dao1, gain_grads[(1, 3)], gain_grads[(1, 4)] = bwd_seg_res(x4, ao1, gain(1, 3), gain(1, 4), dx5, dh6, "d_res_1b")
    dh5, per_layer['xa_wq'][1], per_layer['xa_wkv'][1], per_layer['xa_wo'][1] = attention_bwd(1, h5, dao1, sv1)
    dx3, dm1, gain_grads[(1, 1)], gain_grads[(1, 2)] = bwd_seg_res(x3, m1, gain(1, 1), gain(1, 2), dx4, dh5, "d_res_1a")
    grads['cd_w_out'] = matmul(mix1, dm1, 'tn', "d_cd_w_out", BF)[None]
    dmix1 = matmul(dm1, p['cd_w_out'][0], 'nt', "d_mix1")
    dvconv, dlg, dlb = bwd_call(seg_ln, "d_conf_ln", (nb,), [vconv, p['conf_ln_g'], p['conf_ln_b']],
                                [_rows(D), _par(D), _par(D)], [dmix1], [_rows(D, 0)], [0, 1, 2],
                                [_sd((t, D)), _sd((1, D)), _sd((1, D))], [_rows(D), _par(D), _par(D)], [None, (0,), (0,)])
    grads['conf_ln_g'], grads['conf_ln_b'] = dlg, dlb
    cd_g = bwd_call(cd1_fn, "d_cd_conv", (nd, bsz), cd_ins, cd_in_specs, [dvconv, dmix1],
                    [cd_out_spec, pl.BlockSpec((seq, LANE), lambda j, b: (b, nd + j))], list(range(8)),
                    [_sd((t, D))] * 5 + [_sd((CONF_K, D)), _sd((1, D)), _sd((SC_K, D))], [cd_out_spec] * 5 + cd_par,
                    [None] * 5 + [(1,), (1,), (1,)])
    du1 = jnp.concatenate(cd_g[:5], axis=1)
    grads['conf_dw_w'], grads['conf_dw_b'], grads['sc_conv_w'] = cd_g[5][None], cd_g[6], cd_g[7][None]
    grads['cd_w_in'] = matmul(h4, du1, 'tn', "d_cd_w_in", BF)[None]
    dh4 = matmul(du1, p['cd_w_in'][0], 'nt', "d_h_cd")

    dx2, dmo0, gain_grads[(0, 5)], gain_grads[(1, 0)] = bwd_seg_res(x2, mo0, gain(0, 5), gain(1, 0), dx3, dh4, "d_res_0c")
    dh3, per_layer['mlp_w1'][0], per_layer['mlp_w2'][0] = mlp_bwd(0, h3, dmo0, sv)
    dx1, dao0, gain_grads[(0, 3)], gain_grads[(0, 4)] = bwd_seg_res(x1, ao0, gain(0, 3), gain(0, 4), dx2, dh3, "d_res_0b")
    dh2, per_layer['xa_wq'][0], per_layer['xa_wkv'][0], per_layer['xa_wo'][0] = attention_bwd(0, h2, dao0, sv)
    dx0r, dm0, gain_grads[(0, 1)], gain_grads[(0, 2)] = bwd_seg_res(x0, m0, gain(0, 1), gain(0, 2), dx1, dh2, "d_res_0a")
    grads['ab_w_out'] = matmul(mix0, dm0, 'tn', "d_ab_w_out", BF)[None]
    dmix0 = matmul(dm0, p['ab_w_out'][0], 'nt', "d_mix0")
    dxs, dbm, dcm, dz, ddt, ddtb, dalog, ddsk, dnw = ssd_bwd(xbc_act, u0, dtb, alog, dsk, p['ssm_norm'], consts, hs, dmix0,
                                                             bsz, seq)
    grads['ssm_dt_bias'] = (ddtb[0] + ddtb[1])[:, :SSM_HEADS]
    grads['ssm_a_log'] = (dalog[0] + dalog[1])[:, :SSM_HEADS]
    grads['ssm_d'] = (ddsk[0] + ddsk[1])[:, :SSM_HEADS]
    grads['ssm_norm'] = dnw
    dxbc_act = jnp.concatenate([dxs, dbm, dcm], axis=1)
    dxr, dcw, dcb = bwd_call(conv4_fn, "d_ssm_conv", (ncb, bsz), [u0, p['ssm_conv_w'][0], p['ssm_conv_b']], conv_in_specs,
                             [dxbc_act], [conv_out_spec], [0, 1, 2],
                             [_sd((t, SSM_CONV_DIM)), _sd((SSM_CONV, SSM_CONV_DIM)), _sd((1, SSM_CONV_DIM))],
                             [conv_out_spec, conv_in_specs[1], conv_in_specs[2]], [None, (1,), (1,)])
    grads['ssm_conv_w'], grads['ssm_conv_b'] = dcw[None], dcb
    dpool, dpw, dps = [], [], []
    for g in range(POOL_GROUPS):
        seqspec = pl.BlockSpec((seq, PG), lambda b, g=g: (b, g))
        one = pl.BlockSpec((seq, PG), lambda b: (b, 0))
        wspec = pl.BlockSpec((PG, PG), lambda b: (0, 0))
        sspec = pl.BlockSpec((1, PG), lambda b, g=g: (0, g))
        a, bb, c = bwd_call(make_pool_fn(g), f"d_pool_{g}", (bsz,), [u0, p['pool_w'][0, g], p['pool_scale']],
                            [seqspec, wspec, sspec], [dmix0], [seqspec], [0, 1, 2],
                            [_sd((t, PG)), _sd((PG, PG)), _sd((1, PG))], [one, wspec, pl.BlockSpec((1, PG), lambda b: (0, 0))],
                            [None, (0,), (0,)])
        dpool.append(a)
        dpw.append(bb)
        dps.append(c)
    grads['pool_w'] = jnp.stack(dpw)[None]
    grads['pool_scale'] = jnp.concatenate(dps, axis=1)
    du0 = jnp.concatenate(dpool + [dz, dxr, ddt[0] + ddt[1]], axis=1)
    grads['ab_w_in'] = matmul(h0, du0, 'tn', "d_ab_w_in", BF)[None, :, :AB_IN]
    dh0 = matmul(du0, w_ab_in, 'nt', "d_h_ab")
    dx, dg00 = bwd_call(seg_in_res, "d_norm_in", (nb,), [x0, gain(0, 0)], [_rows(D), _par(D)], [dx0r, dh0],
                        [_rows(D), _rows(D)], [0, 1], [_sd((t, D)), _sd((1, D))], [_rows(D), _par(D)], [None, (0,)])
    gain_grads[(0, 0)] = dg00
    grads['norm_gains'] = jnp.stack([jnp.concatenate([gain_grads[(l, i)] for i in range(6)], axis=0) for l in range(2)])
    for k, v in per_layer.items():
        grads[k] = jnp.stack(v)
    return loss, dx, grads
```

```python
import functools
import math

import numpy as np
import jax
import jax.numpy as jnp
from jax import lax
from jax.experimental import pallas as pl
from jax.experimental.pallas import tpu as pltpu

BF = jnp.bfloat16
F32 = jnp.float32
HI = lax.Precision.HIGHEST

N_DEV = 8
D = 1024
N_MEM = 256
XA_HEADS = 4
XA_DH = D // XA_HEADS
POOL_GROUPS = 4
PG = 128
POOL_W = POOL_GROUPS * PG
SSM_INNER = 1024
SSM_GROUPS = 2
SSM_GSZ = SSM_INNER // SSM_GROUPS
SSM_HEADS = 16
SSM_P = 64
SSM_N = 128
SSM_CONV = 4
SSM_CONV_DIM = SSM_INNER + 2 * SSM_GROUPS * SSM_N
CHUNK = 128
AB_IN = POOL_W + SSM_INNER + SSM_CONV_DIM + SSM_HEADS
AB_IN_PAD = POOL_W + SSM_INNER + SSM_CONV_DIM + 128
AB_OUT = POOL_W + SSM_INNER
CONF_K = 31
SC_K = 3
CD_IN = 5 * D
CD_OUT = 2 * D
MLP_H = 4 * D
RMS_EPS = 1e-6
LN_EPS = 1e-5
ADAM_LR = 0.001
ADAM_B1 = 0.9
ADAM_B2 = 0.999
ADAM_EPS = 1e-08
ADAM_WD = 0.01
ADAM_STEP = 10
VMEM_LIMIT = 56 * 1024 * 1024
LANE = 128

NAMES = ['x', 'mem', 'norm_gains', 'xa_wq', 'xa_wkv', 'xa_wo', 'mlp_w1', 'mlp_w2', 'ab_w_in', 'pool_w', 'pool_scale',
         'ssm_conv_w', 'ssm_conv_b', 'ssm_dt_bias', 'ssm_a_log', 'ssm_d', 'ssm_norm', 'ab_w_out', 'cd_w_in', 'conf_dw_w',
         'conf_dw_b', 'conf_ln_g', 'conf_ln_b', 'sc_conv_w', 'cd_w_out', 'loss_target']
WEIGHTS = NAMES[2:25]
BIG = [('xa_wq', 1), ('xa_wkv', 2), ('xa_wo', 1), ('mlp_w1', 2), ('mlp_w2', 1), ('cd_w_in', 2), ('cd_w_out', 1),
       ('ab_w_out', 1), ('ab_w_in', 2)]
SMALL_SHARDED = ['norm_gains', 'ssm_conv_w', 'conf_dw_w', 'conf_dw_b', 'conf_ln_g', 'conf_ln_b', 'sc_conv_w']
REPLICATED = ['pool_w', 'pool_scale', 'ssm_conv_b', 'ssm_dt_bias', 'ssm_a_log', 'ssm_d', 'ssm_norm']


def _dg(a, b, ca, cb, prec=None):
    return lax.dot_general(a, b, (((ca,), (cb,)), ((), ())), precision=prec, preferred_element_type=F32)


@functools.partial(jax.custom_vjp, nondiff_argnums=(2, 3))
def bdot(a, b, ca, cb):
    return _dg(a.astype(BF), b.astype(BF), ca, cb)


def _bdot_fwd(a, b, ca, cb):
    return bdot(a, b, ca, cb), (a, b)


def _bdot_bwd(ca, cb, res, g):
    a, b = res
    g16, a16, b16 = g.astype(BF), a.astype(BF), b.astype(BF)
    da = _dg(g16, b16, 1, 1 - cb) if ca == 1 else _dg(b16, g16, 1 - cb, 1)
    db = _dg(g16, a16, 0, 1 - ca) if cb == 1 else _dg(a16, g16, 1 - ca, 0)
    return da.astype(a.dtype), db.astype(b.dtype)


bdot.defvjp(_bdot_fwd, _bdot_bwd)


@jax.custom_vjp
def cmat(a, c, ct):
    return _dg(a, c, 1, 0, HI)


def _cmat_fwd(a, c, ct):
    return cmat(a, c, ct), (c, ct)


def _cmat_bwd(res, g):
    c, ct = res
    return _dg(g, ct, 1, 0, HI), jnp.zeros_like(c), jnp.zeros_like(ct)


cmat.defvjp(_cmat_fwd, _cmat_bwd)


@jax.custom_vjp
def cmatl(c, ct, a):
    return _dg(c, a, 1, 0, HI)


def _cmatl_fwd(c, ct, a):
    return cmatl(c, ct, a), (c, ct)


def _cmatl_bwd(res, g):
    c, ct = res
    return jnp.zeros_like(c), jnp.zeros_like(ct), _dg(ct, g, 1, 0, HI)


cmatl.defvjp(_cmatl_fwd, _cmatl_bwd)


def _shift_down(x, k):
    t = lax.broadcasted_iota(jnp.int32, x.shape, 0)
    return jnp.where(t >= k, pltpu.roll(x, k, 0), 0.0)


def _shift_up(x, k):
    n = x.shape[0]
    t = lax.broadcasted_iota(jnp.int32, x.shape, 0)
    return jnp.where(t < n - k, pltpu.roll(x, n - k, 0), 0.0)


@functools.partial(jax.custom_vjp, nondiff_argnums=(1,))
def shift(x, k):
    return _shift_down(x, k)


def _shift_fwd(x, k):
    return _shift_down(x, k), None


def _shift_bwd(k, _, g):
    return (_shift_up(g, k),)


shift.defvjp(_shift_fwd, _shift_bwd)


@functools.partial(jax.custom_vjp, nondiff_argnums=(2,))
def cconv(u, w, width):
    acc = u * w[width - 1:width, :]
    for k in range(width - 1):
        acc = acc + _shift_down(u, width - 1 - k) * w[k:k + 1, :]
    return acc


def _cconv_fwd(u, w, width):
    return cconv(u, w, width), (u, w)


def _cconv_bwd(width, res, g):
    u, w = res
    rows = lax.broadcasted_iota(jnp.int32, w.shape, 0)
    du = g * w[width - 1:width, :]
    dw = jnp.where(rows == width - 1, jnp.sum(g * u, axis=0, keepdims=True), 0.0)
    for k in range(width - 1):
        s = width - 1 - k
        du = du + _shift_up(g, s) * w[k:k + 1, :]
        dw = dw + jnp.where(rows == k, jnp.sum(g * _shift_down(u, s), axis=0, keepdims=True), 0.0)
    return du, dw


cconv.defvjp(_cconv_fwd, _cconv_bwd)


def _rms(x, g):
    return x * lax.rsqrt(jnp.mean(x * x, axis=-1, keepdims=True) + RMS_EPS) * g


def _params(sem=None):
    return pltpu.CompilerParams(dimension_semantics=sem, vmem_limit_bytes=VMEM_LIMIT)


def _f32(v):
    return v if v.dtype == F32 else v.astype(F32)


def _first(axes):
    ok = None
    for ax in axes:
        c = pl.program_id(ax) == 0
        ok = c if ok is None else jnp.logical_and(ok, c)
    return ok


def fwd_call(fn, name, grid, ins, in_specs, out_shapes, out_specs):
    n_in = len(ins)

    def body(*refs):
        outs = fn(*[_f32(r[...]) for r in refs[:n_in]])
        for r, o in zip(refs[n_in:], outs):
            r[...] = o.astype(r.dtype)

    return pl.pallas_call(body, name=name, grid=grid, in_specs=in_specs, out_specs=out_specs, out_shape=out_shapes,
                          compiler_params=_params())(*ins)


def bwd_call(fn, name, grid, ins, in_specs, cots, cot_specs, gidx, g_shapes, g_specs, g_acc):
    n_in, n_cot = len(ins), len(cots)

    def body(*refs):
        vals = [_f32(r[...]) for r in refs[:n_in]]

        def f_sel(*dv):
            full = list(vals)
            for i, v in zip(gidx, dv):
                full[i] = v
            return tuple(fn(*full))

        outs, vjp = jax.vjp(f_sel, *[vals[i] for i in gidx])
        cts = tuple(_f32(r[...]) for r in refs[n_in:n_in + n_cot])
        grads = vjp(cts)
        for r, g, acc in zip(refs[n_in + n_cot:], grads, g_acc):
            if acc is None:
                r[...] = g.astype(r.dtype)
            else:
                @pl.when(_first(acc))
                def _():
                    r[...] = jnp.zeros_like(r)

                r[...] += g.astype(r.dtype)

    return pl.pallas_call(body, name=name, grid=grid, in_specs=list(in_specs) + list(cot_specs), out_specs=g_specs,
                          out_shape=g_shapes, compiler_params=_params())(*ins, *cots)


def _tile(dim, pref):
    if dim <= pref:
        return dim
    best = None
    for t in range(LANE, pref + 1, LANE):
        if dim % t == 0:
            best = t
    assert best is not None, dim
    return best


def matmul(a, b, mode, name, out_dtype=F32, tm=1024, tn=1024, tk=1024):
    if mode == 'nn':
        (m, k), (k2, n) = a.shape, b.shape
    elif mode == 'nt':
        (m, k), (n, k2) = a.shape, b.shape
    else:
        (k, m), (k2, n) = a.shape, b.shape
    assert k == k2, (name, a.shape, b.shape)
    tm, tn, tk = _tile(m, tm), _tile(n, tn), _tile(k, tk)
    nk = k // tk
    ca = 0 if mode == 'tn' else 1
    cb = 1 if mode == 'nt' else 0
    a_spec = pl.BlockSpec((tk, tm), lambda i, j, kk: (kk, i)) if mode == 'tn' else pl.BlockSpec((tm, tk), lambda i, j, kk: (i, kk))
    b_spec = pl.BlockSpec((tn, tk), lambda i, j, kk: (j, kk)) if mode == 'nt' else pl.BlockSpec((tk, tn), lambda i, j, kk: (kk, j))

    def body(a_ref, b_ref, o_ref, acc):
        kk = pl.program_id(2)

        @pl.when(kk == 0)
        def _():
            acc[...] = jnp.zeros_like(acc)

        acc[...] += _dg(a_ref[...].astype(BF), b_ref[...].astype(BF), ca, cb)

        @pl.when(kk == nk - 1)
        def _():
            o_ref[...] = acc[...].astype(o_ref.dtype)

    return pl.pallas_call(
        body, name=name, grid=(m // tm, n // tn, nk), in_specs=[a_spec, b_spec],
        out_specs=pl.BlockSpec((tm, tn), lambda i, j, kk: (i, j)), out_shape=jax.ShapeDtypeStruct((m, n), out_dtype),
        scratch_shapes=[pltpu.VMEM((tm, tn), F32)],
        compiler_params=_params(("parallel", "parallel", "arbitrary")))(a, b)


_FLIPS = [(0, 0, 1), (1, 0, 0), (0, 1, 0), (1, 1, 0), (1, 0, 1), (0, 1, 1), (1, 1, 1)]


def _me():
    return lax.axis_index("x"), lax.axis_index("y"), lax.axis_index("c")


def _flip(pos, f):
    return tuple(jnp.where(fi == 1, 1 - p, p) if fi else p for p, fi in zip(pos, f))


def _slot(pos):
    return 4 * pos[0] + 2 * pos[1] + pos[2]


def all_gather(v, name):
    def body(v_ref, out_ref, send_sems, recv_sems, local_sem):
        me = _me()
        sibling = _flip(me, (0, 0, 1))
        chips = [_flip(me, f) for f in ((1, 0, 0), (0, 1, 0), (1, 1, 0))]

        def copy(k, block, to, src=None):
            return pltpu.make_async_remote_copy(
                src_ref=out_ref.at[_slot(block)] if src is None else src, dst_ref=out_ref.at[_slot(block)],
                send_sem=send_sems.at[k], recv_sem=recv_sems.at[k], device_id=to, device_id_type=pl.DeviceIdType.MESH)

        mine = pltpu.make_async_copy(v_ref, out_ref.at[_slot(me)], local_sem)
        mine.start()
        first = [copy(0, me, sibling, src=v_ref)] + [copy(1 + j, me, chip, src=v_ref) for j, chip in enumerate(chips)]
        for cp in first:
            cp.start()
        passed = [copy(4 + j, chip, sibling) for j, chip in enumerate(chips)]
        for j, chip in enumerate(chips):
            copy(1 + j, chip, me).wait_recv()
            passed[j].start()
        copy(0, sibling, me).wait_recv()
        for j, chip in enumerate(chips):
            copy(4 + j, _flip(chip, (0, 0, 1)), me).wait_recv()
        for cp in first + passed:
            cp.wait_send()
        mine.wait()

    return pl.pallas_call(
        body, name=name, out_shape=jax.ShapeDtypeStruct((N_DEV,) + v.shape, v.dtype),
        in_specs=[pl.BlockSpec(memory_space=pl.ANY)], out_specs=pl.BlockSpec(memory_space=pl.ANY),
        scratch_shapes=[pltpu.SemaphoreType.DMA((7,)), pltpu.SemaphoreType.DMA((7,)), pltpu.SemaphoreType.DMA(())],
    )(v)


def all_to_all(v, name):
    def body(v_ref, out_ref, send_sems, recv_sems, local_sem):
        me = _me()
        mine = pltpu.make_async_copy(v_ref.at[_slot(me)], out_ref.at[_slot(me)], local_sem)
        mine.start()
        copies = []
        for k, f in enumerate(_FLIPS):
            peer = _flip(me, f)
            cp = pltpu.make_async_remote_copy(
                src_ref=v_ref.at[_slot(peer)], dst_ref=out_ref.at[_slot(me)], send_sem=send_sems.at[k],
                recv_sem=recv_sems.at[k], device_id=peer, device_id_type=pl.DeviceIdType.MESH)
            cp.start()
            copies.append(cp)
        for k, f in enumerate(_FLIPS):
            peer = _flip(me, f)
            pltpu.make_async_remote_copy(
                src_ref=v_ref.at[_slot(peer)], dst_ref=out_ref.at[_slot(peer)], send_sem=send_sems.at[k],
                recv_sem=recv_sems.at[k], device_id=peer, device_id_type=pl.DeviceIdType.MESH).wait_recv()
        for cp in copies:
            cp.wait_send()
        mine.wait()

    return pl.pallas_call(
        body, name=name, out_shape=jax.ShapeDtypeStruct(v.shape, v.dtype),
        in_specs=[pl.BlockSpec(memory_space=pl.ANY)], out_specs=pl.BlockSpec(memory_space=pl.ANY),
        scratch_shapes=[pltpu.SemaphoreType.DMA((7,)), pltpu.SemaphoreType.DMA((7,)), pltpu.SemaphoreType.DMA(())],
    )(v)


def sum_slots(v, name, tr=256):
    _, r, c = v.shape
    tr = _tile_rows(r, tr)

    def body(v_ref, o_ref):
        acc = v_ref[0].astype(F32)
        for s in range(1, N_DEV):
            acc = acc + v_ref[s].astype(F32)
        o_ref[...] = acc

    return pl.pallas_call(body, name=name, grid=(r // tr,), in_specs=[pl.BlockSpec((N_DEV, tr, c), lambda i: (0, i, 0))],
                          out_specs=pl.BlockSpec((tr, c), lambda i: (i, 0)), out_shape=jax.ShapeDtypeStruct((r, c), F32),
                          compiler_params=_params())(v)


def _tile_rows(r, pref):
    if r <= pref:
        return r
    best = None
    for t in range(8, pref + 1, 8):
        if r % t == 0:
            best = t
    return r if best is None else best


def adamw(w, m, v, g, name):
    r, c = w.shape
    tr = _tile_rows(r, 512 if c <= 1024 else 128)

    def body(w_ref, m_ref, v_ref, g_ref, d_ref, nm_ref, nv_ref):
        gg = g_ref[...]
        nm = ADAM_B1 * m_ref[...] + (1.0 - ADAM_B1) * gg
        nv = ADAM_B2 * v_ref[...] + (1.0 - ADAM_B2) * jnp.square(gg)
        m_hat = nm / (1.0 - ADAM_B1 ** ADAM_STEP)
        v_hat = nv / (1.0 - ADAM_B2 ** ADAM_STEP)
        d_ref[...] = -ADAM_LR * (m_hat / (jnp.sqrt(v_hat) + ADAM_EPS) + ADAM_WD * w_ref[...])
        nm_ref[...] = nm
        nv_ref[...] = nv

    spec = pl.BlockSpec((tr, c), lambda i: (i, 0))
    sh = jax.ShapeDtypeStruct((r, c), F32)
    return pl.pallas_call(body, name=name, grid=(r // tr,), in_specs=[spec] * 4, out_specs=[spec] * 3,
                          out_shape=[sh] * 3, compiler_params=_params())(w, m, v, g)


def seg_in(x, g):
    return (_rms(x, g),)


def seg_in_res(x, g):
    return x, _rms(x, g)


def seg_res(x, m, ga, gb):
    x1 = x + _rms(m, ga)
    return x1, _rms(x1, gb)


def seg_out(x, m, ga):
    return (x + _rms(m, ga),)


def seg_act(r):
    t = jnp.maximum(r, 0.0)
    return (t * t,)


def seg_ln(v, g, b):
    mu = jnp.mean(v, axis=-1, keepdims=True)
    var = jnp.mean(jnp.square(v - mu), axis=-1, keepdims=True)
    vn = (v - mu) * lax.rsqrt(var + LN_EPS) * g + b
    return (jax.nn.silu(vn),)


def make_pool_fn(group):
    window = 2 ** (group + 1)

    def pool_fn(ug, pw, scale):
        s = ug
        for lvl in range(group + 1):
            s = s + shift(s, 2 ** lvl)
        cnt = jnp.minimum(lax.broadcasted_iota(jnp.int32, ug.shape, 0) + 1, window).astype(F32)
        return (bdot(s / cnt - ug, pw, 1, 0) * scale,)

    return pool_fn


def conv4_fn(xr, w, b):
    return (jax.nn.silu(cconv(xr, w, SSM_CONV) + b),)


def cd1_fn(val, gate, bg, cg, hh, dww, dwb, scw):
    v = val * jax.nn.sigmoid(gate)
    vc = cconv(v, dww, CONF_K) + dwb
    sc = bg * cconv(cg * hh, scw, SC_K)
    return vc, sc


def attn_fn(q, k, v):
    s = bdot(q, k, 1, 1) / math.sqrt(XA_DH)
    p = jax.nn.softmax(s, axis=-1)
    return (bdot(p, v, 1, 0),)


def ssd_chunk(xs, bm, cm, z, dtraw, dtb, alog, dsk, nw, h0, h1, h2, h3, e64, e64t, e128, e128t, tril, trilt):
    hin = (h0, h1, h2, h3)
    dt = jax.nn.softplus(dtraw + dtb)
    a = -jnp.exp(alog)
    d_a = dt * a
    cs = cmatl(tril, trilt, d_a)
    cs64 = cmat(cs, e64, e64t)
    dt64 = cmat(dt, e64, e64t)
    tot64 = jnp.sum(cmat(d_a, e64, e64t), axis=0, keepdims=True)
    cs128 = cmat(cs, e128, e128t)
    d64 = cmat(jnp.broadcast_to(dsk, (8, LANE)), e64, e64t)[0:1, :]
    xdt = xs * dt64
    cb = bdot(cm, bm, 1, 1)
    li = lax.broadcasted_iota(jnp.int32, (CHUNK, CHUNK), 0)
    si = lax.broadcasted_iota(jnp.int32, (CHUNK, CHUNK), 1)
    causal = li >= si
    lane = lax.broadcasted_iota(jnp.int32, (CHUNK, LANE), 1)
    xw = xdt * jnp.exp(tot64 - cs64)
    ecs = jnp.exp(cs64)
    etot = jnp.exp(tot64)
    ycols, hout = [], []
    for j in range(4):
        sl = slice(j * LANE, (j + 1) * LANE)
        xj = xdt[:, sl]
        ys = []
        for hh in range(2):
            r = 2 * j + hh
            col = cs128[:, r * LANE:(r + 1) * LANE]
            decay = jnp.exp(jnp.where(causal, col - col.T, -1e30))
            ys.append(bdot(cb * decay, xj, 1, 0))
        y_diag = jnp.where(lane < SSM_P, ys[0], ys[1])
        y_off = bdot(cm, hin[j], 1, 0) * ecs[:, sl]
        ycols.append(y_diag + y_off)
        hout.append(etot[:, sl] * hin[j] + bdot(bm, xw[:, sl], 0, 0))
    y = jnp.concatenate(ycols, axis=1) + d64 * xs
    y = y * jax.nn.silu(z)
    yn = y * lax.rsqrt(jnp.mean(y * y, axis=-1, keepdims=True) + RMS_EPS) * nw
    return (yn,) + tuple(hout)


def _ssd_consts():
    h = np.arange(LANE)[:, None]
    e64 = np.stack([(h == g * 8 + np.arange(SSM_GSZ)[None, :] // SSM_P) for g in range(SSM_GROUPS)]).astype(np.float32)
    e128 = np.stack([(h == g * 8 + np.arange(8 * LANE)[None, :] // LANE) for g in range(SSM_GROUPS)]).astype(np.float32)
    tril = np.tril(np.ones((CHUNK, CHUNK), np.float32))
    return (jnp.asarray(e64), jnp.asarray(e64.transpose(0, 2, 1)), jnp.asarray(e128), jnp.asarray(e128.transpose(0, 2, 1)),
            jnp.asarray(tril), jnp.asarray(tril.T))


def _ssd_specs(nc, rev):
    def ci(c):
        return nc - 1 - c if rev else c

    def row(width, col):
        return pl.BlockSpec((CHUNK, width), lambda g, b, c: (b * nc + ci(c), col(g)))

    data = [row(SSM_GSZ, lambda g: g), row(SSM_N, lambda g: 8 + g), row(SSM_N, lambda g: 10 + g),
            row(SSM_GSZ, lambda g: 1 + g), row(LANE, lambda g: 24)]
    par = [pl.BlockSpec((1, LANE), lambda g, b, c: (0, 0))] * 3 + [pl.BlockSpec((1, SSM_GSZ), lambda g, b, c: (0, g))]
    cst = [pl.BlockSpec((None, LANE, SSM_GSZ), lambda g, b, c: (g, 0, 0)), pl.BlockSpec((None, SSM_GSZ, LANE), lambda g, b, c: (g, 0, 0)),
           pl.BlockSpec((None, LANE, 8 * LANE), lambda g, b, c: (g, 0, 0)), pl.BlockSpec((None, 8 * LANE, LANE), lambda g, b, c: (g, 0, 0)),
           pl.BlockSpec((CHUNK, CHUNK), lambda g, b, c: (0, 0)), pl.BlockSpec((CHUNK, CHUNK), lambda g, b, c: (0, 0))]
    hsave = pl.BlockSpec((None, None, None, 4, SSM_N, LANE), lambda g, b, c: (g, b, ci(c), 0, 0, 0))
    yn = row(SSM_GSZ, lambda g: g)
    return data, par, cst, hsave, yn, row


def ssd_fwd(xbc_act, u, dtb, alog, dsk, nw, consts, bsz, seq):
    nc = seq // CHUNK
    data, par, cst, hsave, yn_spec, _ = _ssd_specs(nc, False)

    def body(xs, bm, cm, z, dtr, dtb_r, alog_r, dsk_r, nw_r, e64, e64t, e128, e128t, tril, trilt, yn_ref, hs_ref, h):
        @pl.when(pl.program_id(2) == 0)
        def _():
            h[...] = jnp.zeros_like(h)

        hs_ref[...] = h[...]
        outs = ssd_chunk(xs[...], bm[...], cm[...], z[...], dtr[...], dtb_r[...], alog_r[...], dsk_r[...], nw_r[...],
                         h[0], h[1], h[2], h[3], e64[...], e64t[...], e128[...], e128t[...], tril[...], trilt[...])
        yn_ref[...] = outs[0].astype(yn_ref.dtype)
        for j in range(4):
            h[j] = outs[1 + j]

    t = bsz * seq
    return pl.pallas_call(
        body, name="ssd_fwd", grid=(SSM_GROUPS, bsz, nc), in_specs=data + par + cst, out_specs=[yn_spec, hsave],
        out_shape=[jax.ShapeDtypeStruct((t, SSM_INNER), BF), jax.ShapeDtypeStruct((SSM_GROUPS, bsz, nc, 4, SSM_N, LANE), F32)],
        scratch_shapes=[pltpu.VMEM((4, SSM_N, LANE), F32)], compiler_params=_params(),
    )(xbc_act, xbc_act, xbc_act, u, u, dtb, alog, dsk, nw, *consts)


def ssd_bwd(xbc_act, u, dtb, alog, dsk, nw, consts, hs, dmix, bsz, seq):
    nc = seq // CHUNK
    data, par, cst, hsave, _, row = _ssd_specs(nc, True)
    t = bsz * seq
    dyn_spec = row(SSM_GSZ, lambda g: POOL_W // SSM_GSZ + g)

    def body(xs, bm, cm, z, dtr, dtb_r, alog_r, dsk_r, nw_r, e64, e64t, e128, e128t, tril, trilt, hs_ref, dyn_ref,
             dxs, dbm, dcm, dz, ddt, ddtb, dalog, ddsk, dnw, dh):
        @pl.when(pl.program_id(2) == 0)
        def _():
            dh[...] = jnp.zeros_like(dh)

        cst_vals = (e64[...], e64t[...], e128[...], e128t[...], tril[...], trilt[...])

        def f(*args):
            return ssd_chunk(*args, *cst_vals)

        prim = (xs[...], bm[...], cm[...], z[...], dtr[...], dtb_r[...], alog_r[...], dsk_r[...], nw_r[...],
                hs_ref[0], hs_ref[1], hs_ref[2], hs_ref[3])
        _, vjp = jax.vjp(f, *prim)
        g = vjp((dyn_ref[...].astype(F32), dh[0], dh[1], dh[2], dh[3]))
        dxs[...] = g[0]
        dbm[...] = g[1]
        dcm[...] = g[2]
        dz[...] = g[3]
        ddt[...] = g[4]

        @pl.when(_first((1, 2)))
        def _():
            for r in (ddtb, dalog, ddsk, dnw):
                r[...] = jnp.zeros_like(r)

        ddtb[...] += g[5]
        dalog[...] += g[6]
        ddsk[...] += g[7]
        dnw[...] += g[8]
        for j in range(4):
            dh[j] = g[9 + j]

    gpar = pl.BlockSpec((None, 1, LANE), lambda g, b, c: (g, 0, 0))
    out_specs = [row(SSM_GSZ, lambda g: g), row(SSM_N, lambda g: g), row(SSM_N, lambda g: g), row(SSM_GSZ, lambda g: g),
                 pl.BlockSpec((None, CHUNK, LANE), lambda g, b, c: (g, b * nc + nc - 1 - c, 0)),
                 gpar, gpar, gpar, pl.BlockSpec((1, SSM_GSZ), lambda g, b, c: (0, g))]
    gp = jax.ShapeDtypeStruct((SSM_GROUPS, 1, LANE), F32)
    out_shape = [jax.ShapeDtypeStruct((t, SSM_INNER), F32), jax.ShapeDtypeStruct((t, SSM_GROUPS * SSM_N), F32),
                 jax.ShapeDtypeStruct((t, SSM_GROUPS * SSM_N), F32), jax.ShapeDtypeStruct((t, SSM_INNER), F32),
                 jax.ShapeDtypeStruct((SSM_GROUPS, t, LANE), F32), gp, gp, gp, jax.ShapeDtypeStruct((1, SSM_INNER), F32)]
    return pl.pallas_call(
        body, name="ssd_bwd", grid=(SSM_GROUPS, bsz, nc), in_specs=data + par + cst + [hsave, dyn_spec], out_specs=out_specs,
        out_shape=out_shape, scratch_shapes=[pltpu.VMEM((4, SSM_N, LANE), F32)], compiler_params=_params(),
    )(xbc_act, xbc_act, xbc_act, u, u, dtb, alog, dsk, nw, *consts, hs, dmix)


TB = 512


def _rows(d, col=0):
    return pl.BlockSpec((TB, d), lambda i: (i, col))


def _par(d):
    return pl.BlockSpec((1, d), lambda i: (0, 0))


def _sd(shape, dtype=F32):
    return jax.ShapeDtypeStruct(shape, dtype)


def _round_up(n, m):
    return -(-n // m) * m


def _pad_rows(a, rows):
    return jnp.pad(a, ((0, rows - a.shape[0]), (0, 0)))


def _pack128(arrs):
    flat = jnp.concatenate([a.reshape(-1) for a in arrs])
    n = flat.shape[0]
    rows = -(-n // (8 * LANE)) * 8
    return jnp.pad(flat, (0, rows * LANE - n)).reshape(rows, LANE)


def _unpack128(packed, shapes):
    flat = packed.reshape(-1)
    out, off = [], 0
    for s in shapes:
        n = int(np.prod(s))
        out.append(flat[off:off + n].reshape(s))
        off += n
    return out


def kernel(x, mem, norm_gains, xa_wq, xa_wkv, xa_wo, mlp_w1, mlp_w2, ab_w_in, pool_w, pool_scale, ssm_conv_w, ssm_conv_b, ssm_dt_bias, ssm_a_log, ssm_d, ssm_norm, ab_w_out, cd_w_in, conf_dw_w, conf_dw_b, conf_ln_g, conf_ln_b, sc_conv_w, cd_w_out, loss_target, m_norm_gains, m_xa_wq, m_xa_wkv, m_xa_wo, m_mlp_w1, m_mlp_w2, m_ab_w_in, m_pool_w, m_pool_scale, m_ssm_conv_w, m_ssm_conv_b, m_ssm_dt_bias, m_ssm_a_log, m_ssm_d, m_ssm_norm, m_ab_w_out, m_cd_w_in, m_conf_dw_w, m_conf_dw_b, m_conf_ln_g, m_conf_ln_b, m_sc_conv_w, m_cd_w_out, v_norm_gains, v_xa_wq, v_xa_wkv, v_xa_wo, v_mlp_w1, v_mlp_w2, v_ab_w_in, v_pool_w, v_pool_scale, v_ssm_conv_w, v_ssm_conv_b, v_ssm_dt_bias, v_ssm_a_log, v_ssm_d, v_ssm_norm, v_ab_w_out, v_cd_w_in, v_conf_dw_w, v_conf_dw_b, v_conf_ln_g, v_conf_ln_b, v_sc_conv_w, v_cd_w_out):
    args = locals()
    w = {n: args[n] for n in WEIGHTS}
    mom_m = {n: args["m_" + n] for n in WEIGHTS}
    mom_v = {n: args["v_" + n] for n in WEIGHTS}
    ex = Exchange(w)
    loss_local, grad_x, small_grads = local_step(x, mem, loss_target, ex)
    loss = lax.psum(loss_local, ("x", "y", "c"))
    g_own = ex.finish(small_grads, grad_x)
    outs = {}
    for n in WEIGHTS:
        shp = w[n].shape
        view = (-1, shp[-1]) if shp[-1] >= LANE else (1, -1)
        if n in SMALL_SHARDED or n in REPLICATED:
            continue
        d, nm, nv = adamw(w[n].reshape(view), mom_m[n].reshape(view), mom_v[n].reshape(view), g_own[n].reshape(view), "adamw_" + n)
        outs[n] = (g_own[n], d.reshape(shp), nm.reshape(shp), nv.reshape(shp))
    small = SMALL_SHARDED + REPLICATED
    shapes = [w[n].shape for n in small]
    d, nm, nv = adamw(_pack128([w[n] for n in small]), _pack128([mom_m[n] for n in small]), _pack128([mom_v[n] for n in small]),
                      _pack128([g_own[n] for n in small]), "adamw_small")
    for n, dd, mm, vv in zip(small, _unpack128(d, shapes), _unpack128(nm, shapes), _unpack128(nv, shapes)):
        outs[n] = (g_own[n], dd, mm, vv)
    return (loss, grad_x.reshape(x.shape), *[outs[n][0] for n in WEIGHTS], *[outs[n][1] for n in WEIGHTS],
            *[outs[n][2] for n in WEIGHTS], *[outs[n][3] for n in WEIGHTS])


G_AB = (('ab_w_in', 0), ('ab_w_out', 0))
G_L0 = (('xa_wq', 0), ('xa_wkv', 0), ('xa_wo', 0), ('mlp_w1', 0), ('mlp_w2', 0))
G_L1 = (('xa_wq', 1), ('xa_wkv', 1), ('xa_wo', 1), ('mlp_w1', 1), ('mlp_w2', 1))
G_CD = (('cd_w_in', 0), ('cd_w_out', 0))
SHARD_AXIS = dict(BIG)
MEMBER_ROW_TILE = 64
FLAT_ROW_TILE = 128


def _members(group, w):
    out = []
    for n, layer in group:
        shp = w[n].shape[1:]
        rows = shp[0] * shp[1] // D
        out.append((n, layer, shp, rows, _round_up(rows, MEMBER_ROW_TILE)))
    return out


def _group_rows(group, w):
    return _round_up(sum(m[4] for m in _members(group, w)), FLAT_ROW_TILE)


def _flat_shards(group, w):
    parts = [_pad_rows(w[n][layer].astype(BF).reshape(-1, D), padded) for n, layer, _, _, padded in _members(group, w)]
    flat = jnp.concatenate(parts, axis=0)
    return _pad_rows(flat, _group_rows(group, w))


def _full_from_slots(land, group, w):
    out, off = {}, 0
    for n, layer, shp, rows, padded in _members(group, w):
        blk = land[:, off:off + rows]
        off += padded
        if SHARD_AXIS[n] == 1:
            out[(n, layer)] = blk.reshape(N_DEV * shp[0], shp[1])
        else:
            out[(n, layer)] = blk.reshape(N_DEV, shp[0], shp[1]).transpose(1, 0, 2).reshape(shp[0], N_DEV * shp[1])
    return out


def _slots_from_full(grads, group, w):
    parts = []
    for n, layer, shp, rows, padded in _members(group, w):
        g = grads[(n, layer)].astype(BF)
        if SHARD_AXIS[n] == 1:
            blk = g.reshape(N_DEV, rows, D)
        else:
            blk = g.reshape(shp[0], N_DEV, shp[1]).transpose(1, 0, 2).reshape(N_DEV, rows, D)
        parts.append(jnp.pad(blk, ((0, 0), (0, padded - rows), (0, 0))))
    send = jnp.concatenate(parts, axis=1)
    return jnp.pad(send, ((0, 0), (0, _group_rows(group, w) - send.shape[1]), (0, 0)))


def _own_from_sum(summed, group, w):
    out, off = {}, 0
    for n, layer, shp, rows, padded in _members(group, w):
        out[(n, layer)] = summed[off:off + rows].reshape(shp)
        off += padded
    return out


_HBM = pl.BlockSpec(memory_space=pltpu.HBM)
_SEM = pl.BlockSpec(memory_space=pltpu.SEMAPHORE)
_ANY = pl.BlockSpec(memory_space=pl.ANY)


def _peer_copy(k, src, dst, send_sems, recv_sems, peer):
    return pltpu.make_async_remote_copy(src_ref=src, dst_ref=dst, send_sem=send_sems.at[k], recv_sem=recv_sems.at[k],
                                        device_id=peer, device_id_type=pl.DeviceIdType.MESH)


def exchange_start(src, name, scatter):
    shape = src.shape[-2:]

    def body(src_ref, land_ref, send_sems, recv_sems, src_thru, land_thru, token):
        me = _me()
        for k, f in enumerate(_FLIPS):
            peer = _flip(me, f)
            piece = src_ref.at[_slot(peer)] if scatter else src_ref
            _peer_copy(k, piece, land_ref.at[_slot(me)], send_sems, recv_sems, peer).start()
        token[...] = jnp.zeros_like(token)

    land = pltpu.with_memory_space_constraint(lax.empty((N_DEV,) + shape, src.dtype), pltpu.HBM)
    return pl.pallas_call(
        body, name=name,
        out_shape=(pltpu.SemaphoreType.DMA((7,)), pltpu.SemaphoreType.DMA((7,)), pltpu.HBM(src.shape, src.dtype),
                   pltpu.HBM((N_DEV,) + shape, src.dtype), jax.ShapeDtypeStruct((8, LANE), F32)),
        in_specs=(_HBM, _HBM), out_specs=(_SEM, _SEM, _HBM, _HBM, pl.BlockSpec(memory_space=pltpu.VMEM)),
        input_output_aliases={0: 2, 1: 3},
        compiler_params=pltpu.CompilerParams(has_side_effects=pltpu.SideEffectType.DATAFLOW_SIDE_EFFECTING),
    )(pltpu.with_memory_space_constraint(src, pltpu.HBM), land)


def exchange_wait(handles, after, name, scatter):
    send_sems, recv_sems, src_thru, land_thru, _ = handles

    def body(src_ref, land_ref, send_sems, recv_sems, after_ref, src_dead, got_ref):
        me = _me()
        for k, f in enumerate(_FLIPS):
            peer = _flip(me, f)
            piece = src_ref.at[_slot(peer)] if scatter else src_ref
            cp = _peer_copy(k, piece, land_ref.at[_slot(peer)], send_sems, recv_sems, peer)
            cp.wait_send()
            cp.wait_recv()

    return pl.pallas_call(
        body, name=name, out_shape=(pltpu.HBM(src_thru.shape, src_thru.dtype), pltpu.HBM(land_thru.shape, land_thru.dtype)),
        in_specs=(_HBM, _HBM, _SEM, _SEM, _ANY), out_specs=(_HBM, _HBM), input_output_aliases={0: 0, 1: 1},
        compiler_params=pltpu.CompilerParams(has_side_effects=pltpu.SideEffectType.DATAFLOW_SIDE_EFFECTING),
    )(src_thru, land_thru, send_sems, recv_sems, after)


class Exchange:
    def __init__(self, w):
        self.w = w
        self.me = _slot(_me())
        shapes = [w[n].shape for n in SMALL_SHARDED]
        gs = all_gather(_pack128([w[n] for n in SMALL_SHARDED]), "gather_small")
        per_dev = [_unpack128(gs[d], shapes) for d in range(N_DEV)]
        self.small = {n: jnp.concatenate([per_dev[d][i] for d in range(N_DEV)], axis=-1) for i, n in enumerate(SMALL_SHARDED)}
        self.small.update({n: w[n] for n in REPLICATED})
        self.now = _full_from_slots(all_gather(_flat_shards(G_AB, w), "gather_ab"), G_AB, w)
        self.gathers = {'l0': (G_L0, exchange_start(_flat_shards(G_L0, w), "gather_l0_start", False)),
                        'l1cd': (G_L1 + G_CD, exchange_start(_flat_shards(G_L1 + G_CD, w), "gather_l1cd_start", False))}
        self.tokens = [h[4] for _, h in self.gathers.values()]
        self.reductions = []

    def take_tokens(self):
        toks, self.tokens = self.tokens, []
        return toks

    def weights(self, key, after):
        if key == 'ab':
            return self.now
        group, handles = self.gathers[key]
        _, land = exchange_wait(handles, after, f"gather_{key}_wait", False)
        own = handles[2]
        land = lax.dynamic_update_slice(land, own[None], (self.me, 0, 0))
        return _full_from_slots(land, group, self.w)

    def put_grads(self, key, group, grads):
        send = _slots_from_full(grads, group, self.w)
        handles = exchange_start(send, f"reduce_{key}_start", True)
        self.reductions.append((key, group, handles))
        self.tokens.append(handles[4])

    def finish(self, small_grads, after):
        w = self.w
        own = {}
        ab = small_grads.pop('ab')
        summed = sum_slots(all_to_all(_slots_from_full(ab, G_AB, w), "reduce_ab"), "sum_ab", FLAT_ROW_TILE)
        own.update(_own_from_sum(summed, G_AB, w))
        for key, group, handles in self.reductions:
            send, land = exchange_wait(handles, after, f"reduce_{key}_wait", True)
            mine = lax.dynamic_slice_in_dim(send, self.me, 1, axis=0)
            land = lax.dynamic_update_slice(land, mine, (self.me, 0, 0))
            own.update(_own_from_sum(sum_slots(land, f"sum_{key}", FLAT_ROW_TILE), group, w))
        out = {}
        for n, _ in BIG:
            layers = w[n].shape[0]
            out[n] = jnp.stack([own[(n, layer)] for layer in range(layers)])
        small = SMALL_SHARDED + REPLICATED
        gs = all_gather(_pack128([small_grads[n] for n in small]), "gather_small_grads")
        tot = _unpack128(sum_slots(gs, "sum_small", 1024), [small_grads[n].shape for n in small])
        for n, g in zip(small, tot):
            if n in SMALL_SHARDED:
                width = w[n].shape[-1]
                g = lax.dynamic_slice_in_dim(g, self.me * width, width, axis=g.ndim - 1)
            out[n] = g
        return out


def local_step(x, mem, target, ex):
    bsz, seq, _ = x.shape
    t = bsz * seq
    nb = t // TB
    nc = seq // CHUNK
    x0 = x.reshape(t, D)
    mem2 = mem.reshape(bsz * N_MEM, D)
    tgt = target.reshape(t, D)
    p = ex.small
    gains = p['norm_gains']
    big = dict(ex.weights('ab', None))

    def gain(layer, i):
        g = gains[layer, i].reshape(1, D)
        for tok in ex.take_tokens():
            g = g + tok[0, 0]
        return g

    consts = _ssd_consts()
    grads = {}
    saved = [dict(), dict()]

    def run_seg_res(xin, m, ga, gb, name):
        return fwd_call(seg_res, name, (nb,), [xin, m, ga, gb], [_rows(D), _rows(D), _par(D), _par(D)],
                        [_sd((t, D)), _sd((t, D), BF)], [_rows(D), _rows(D)])

    def attn_specs():
        nq = seq // TB
        q = pl.BlockSpec((TB, XA_DH), lambda b, h, i: (b * nq + i, h))
        k = pl.BlockSpec((N_MEM, XA_DH), lambda b, h, i: (b, h))
        v = pl.BlockSpec((N_MEM, XA_DH), lambda b, h, i: (b, XA_HEADS + h))
        return (bsz, XA_HEADS, nq), q, k, v

    def attention_fwd(layer, xin, hin, sv):
        q = matmul(hin, big[('xa_wq', layer)], 'nn', f"q_{layer}", BF)
        kv = matmul(mem2, big[('xa_wkv', layer)], 'nn', f"kv_{layer}", BF)
        grid, qs, ks, vs = attn_specs()
        o, = fwd_call(attn_fn, f"attn_{layer}", grid, [q, kv, kv], [qs, ks, vs], [_sd((t, D), BF)], [qs])
        ao = matmul(o, big[('xa_wo', layer)], 'nn', f"ao_{layer}")
        sv.update(q=q, kv=kv, o=o, ao=ao)
        return ao

    def mlp_fwd(layer, hin, sv):
        r = matmul(hin, big[('mlp_w1', layer)], 'nn', f"mlp1_{layer}")
        rr, = fwd_call(seg_act, f"act_{layer}", (nb,), [r], [_rows(MLP_H)], [_sd((t, MLP_H), BF)], [_rows(MLP_H)])
        mo = matmul(rr, big[('mlp_w2', layer)], 'nn', f"mlp2_{layer}")
        sv.update(r=r, rr=rr, mo=mo)
        return mo

    sv = saved[0]
    h0, = fwd_call(seg_in, "norm_in", (nb,), [x0, gain(0, 0)], [_rows(D), _par(D)], [_sd((t, D), BF)], [_rows(D)])
    w_ab_in = jnp.pad(big[('ab_w_in', 0)], ((0, 0), (0, AB_IN_PAD - AB_IN)))
    u0 = matmul(h0, w_ab_in, 'nn', "ab_in")
    pool_outs = []
    for g in range(POOL_GROUPS):
        seqspec = pl.BlockSpec((seq, PG), lambda b, g=g: (b, g))
        po, = fwd_call(make_pool_fn(g), f"pool_{g}", (bsz,), [u0, p['pool_w'][0, g], p['pool_scale']],
                       [seqspec, pl.BlockSpec((PG, PG), lambda b: (0, 0)), pl.BlockSpec((1, PG), lambda b, g=g: (0, g))],
                       [_sd((t, PG), BF)], [pl.BlockSpec((seq, PG), lambda b: (b, 0))])
        pool_outs.append(po)
    cw = 256
    ncb = SSM_CONV_DIM // cw
    cbase = (POOL_W + SSM_INNER) // cw
    conv_in_specs = [pl.BlockSpec((seq, cw), lambda j, b: (b, cbase + j)), pl.BlockSpec((SSM_CONV, cw), lambda j, b: (0, j)),
                     pl.BlockSpec((1, cw), lambda j, b: (0, j))]
    conv_out_spec = pl.BlockSpec((seq, cw), lambda j, b: (b, j))
    xbc_act, = fwd_call(conv4_fn, "ssm_conv", (ncb, bsz), [u0, p['ssm_conv_w'][0], p['ssm_conv_b']], conv_in_specs,
                        [_sd((t, SSM_CONV_DIM))], [conv_out_spec])
    dtb = jnp.pad(p['ssm_dt_bias'], ((0, 0), (0, LANE - SSM_HEADS)))
    alog = jnp.pad(p['ssm_a_log'], ((0, 0), (0, LANE - SSM_HEADS)))
    dsk = jnp.pad(p['ssm_d'], ((0, 0), (0, LANE - SSM_HEADS)))
    yn, hs = ssd_fwd(xbc_act, u0, dtb, alog, dsk, p['ssm_norm'], consts, bsz, seq)
    mix0 = jnp.concatenate(pool_outs + [yn], axis=1)
    m0 = matmul(mix0, big[('ab_w_out', 0)], 'nn', "ab_out")
    x1, h2 = run_seg_res(x0, m0, gain(0, 1), gain(0, 2), "res_0a")
    big.update(ex.weights('l0', h2))
    ao0 = attention_fwd(0, x1, h2, sv)
    x2, h3 = run_seg_res(x1, ao0, gain(0, 3), gain(0, 4), "res_0b")
    mo0 = mlp_fwd(0, h3, sv)
    x3, h4 = run_seg_res(x2, mo0, gain(0, 5), gain(1, 0), "res_0c")

    sv1 = saved[1]
    big.update(ex.weights('l1cd', h4))
    u1 = matmul(h4, big[('cd_w_in', 0)], 'nn', "cd_in")
    nd = D // LANE

    def cd_col(k):
        return pl.BlockSpec((seq, LANE), lambda j, b, k=k: (b, k * nd + j))

    cd_par = [pl.BlockSpec((CONF_K, LANE), lambda j, b: (0, j)), pl.BlockSpec((1, LANE), lambda j, b: (0, j)),
              pl.BlockSpec((SC_K, LANE), lambda j, b: (0, j))]
    cd_ins = [u1] * 5 + [p['conf_dw_w'][0], p['conf_dw_b'], p['sc_conv_w'][0]]
    cd_in_specs = [cd_col(k) for k in range(5)] + cd_par
    cd_out_spec = pl.BlockSpec((seq, LANE), lambda j, b: (b, j))
    vconv, sc_out = fwd_call(cd1_fn, "cd_conv", (nd, bsz), cd_ins, cd_in_specs, [_sd((t, D)), _sd((t, D), BF)],
                             [cd_out_spec, cd_out_spec])
    conf, = fwd_call(seg_ln, "conf_ln", (nb,), [vconv, p['conf_ln_g'], p['conf_ln_b']], [_rows(D), _par(D), _par(D)],
                     [_sd((t, D), BF)], [_rows(D)])
    mix1 = jnp.concatenate([conf, sc_out], axis=1)
    m1 = matmul(mix1, big[('cd_w_out', 0)], 'nn', "cd_out")
    x4, h5 = run_seg_res(x3, m1, gain(1, 1), gain(1, 2), "res_1a")
    ao1 = attention_fwd(1, x4, h5, sv1)
    x5, h6 = run_seg_res(x4, ao1, gain(1, 3), gain(1, 4), "res_1b")
    mo1 = mlp_fwd(1, h6, sv1)

    def loss_body(x_ref, m_ref, g_ref, t_ref, dy_ref, acc_ref):
        y = x_ref[...] + _rms(m_ref[...], g_ref[...])
        d = y - t_ref[...]
        dy_ref[...] = d / float(D)

        @pl.when(pl.program_id(0) == 0)
        def _():
            acc_ref[...] = jnp.zeros_like(acc_ref)

        acc_ref[...] += jnp.sum(d * d, axis=0, keepdims=True)

    dy, lanes = pl.pallas_call(
        loss_body, name="loss_head", grid=(nb,), in_specs=[_rows(D), _rows(D), _par(D), _rows(D)],
        out_specs=[_rows(D), _par(D)], out_shape=[_sd((t, D)), _sd((1, D))], compiler_params=_params())(x5, mo1, gain(1, 5), tgt)
    loss = 0.5 * jnp.sum(lanes) / float(D)

    gain_grads = {}

    def bwd_seg_out(xin, m, ga, dyv, name):
        dx, dm, dga = bwd_call(seg_out, name, (nb,), [xin, m, ga], [_rows(D), _rows(D), _par(D)], [dyv], [_rows(D)],
                               [0, 1, 2], [_sd((t, D)), _sd((t, D), BF), _sd((1, D))], [_rows(D), _rows(D), _par(D)],
                               [None, None, (0,)])
        return dx, dm, dga

    def bwd_seg_res(xin, m, ga, gb, dx1, dh, name):
        return bwd_call(seg_res, name, (nb,), [xin, m, ga, gb], [_rows(D), _rows(D), _par(D), _par(D)], [dx1, dh],
                        [_rows(D), _rows(D)], [0, 1, 2, 3], [_sd((t, D)), _sd((t, D), BF), _sd((1, D)), _sd((1, D))],
                        [_rows(D), _rows(D), _par(D), _par(D)], [None, None, (0,), (0,)])

    def mlp_bwd(layer, hin, dmo, sv):
        grads_w2 = matmul(sv['rr'], dmo, 'tn', f"d_mlp_w2_{layer}", BF)
        drr = matmul(dmo, big[('mlp_w2', layer)], 'nt', f"d_rr_{layer}")
        dr, = bwd_call(seg_act, f"d_act_{layer}", (nb,), [sv['r']], [_rows(MLP_H)], [drr], [_rows(MLP_H)], [0],
                       [_sd((t, MLP_H), BF)], [_rows(MLP_H)], [None])
        grads_w1 = matmul(hin, dr, 'tn', f"d_mlp_w1_{layer}", BF)
        dh = matmul(dr, big[('mlp_w1', layer)], 'nt', f"d_h_mlp_{layer}")
        return dh, grads_w1, grads_w2

    def attention_bwd(layer, hin, dao, sv):
        g_wo = matmul(sv['o'], dao, 'tn', f"d_xa_wo_{layer}", BF)
        do = matmul(dao, big[('xa_wo', layer)], 'nt', f"d_o_{layer}", BF)
        grid, qs, ks, vs = attn_specs()
        kvo = pl.BlockSpec((N_MEM, XA_DH), lambda b, h, i: (b, h))
        dq, dk, dv = bwd_call(attn_fn, f"d_attn_{layer}", grid, [sv['q'], sv['kv'], sv['kv']], [qs, ks, vs], [do], [qs],
                              [0, 1, 2], [_sd((t, D), BF), _sd((bsz * N_MEM, D)), _sd((bsz * N_MEM, D))], [qs, kvo, kvo],
                              [None, (2,), (2,)])
        dkv = jnp.concatenate([dk, dv], axis=1)
        g_wkv = matmul(mem2, dkv, 'tn', f"d_xa_wkv_{layer}", BF)
        g_wq = matmul(hin, dq, 'tn', f"d_xa_wq_{layer}", BF)
        dh = matmul(dq, big[('xa_wq', layer)], 'nt', f"d_h_attn_{layer}")
        return dh, g_wq, g_wkv, g_wo

    per_layer = {k: [None, None] for k in ('xa_wq', 'xa_wkv', 'xa_wo', 'mlp_w1', 'mlp_w2')}

    dx5, dmo1, gain_grads[(1, 5)] = bwd_seg_out(x5, mo1, gain(1, 5), dy, "d_out")
    dh6, per_layer['mlp_w1'][1], per_layer['mlp_w2'][1] = mlp_bwd(1, h6, dmo1, sv1)
    dx4, dao1, gain_grads[(1, 3)], gain_grads[(1, 4)] = bwd_seg_res(x4, ao1, gain(1, 3), gain(1, 4), dx5, dh6, "d_res_1b")
    dh5, per_layer['xa_wq'][1], per_layer['xa_wkv'][1], per_layer['xa_wo'][1] = attention_bwd(1, h5, dao1, sv1)
    ex.put_grads('l1', G_L1, {(k, 1): v[1] for k, v in per_layer.items()})
    dx3, dm1, gain_grads[(1, 1)], gain_grads[(1, 2)] = bwd_seg_res(x3, m1, gain(1, 1), gain(1, 2), dx4, dh5, "d_res_1a")
    g_cd_out = matmul(mix1, dm1, 'tn', "d_cd_w_out", BF)
    dmix1 = matmul(dm1, big[('cd_w_out', 0)], 'nt', "d_mix1")
    dvconv, dlg, dlb = bwd_call(seg_ln, "d_conf_ln", (nb,), [vconv, p['conf_ln_g'], p['conf_ln_b']],
                                [_rows(D), _par(D), _par(D)], [dmix1], [_rows(D, 0)], [0, 1, 2],
                                [_sd((t, D)), _sd((1, D)), _sd((1, D))], [_rows(D), _par(D), _par(D)], [None, (0,), (0,)])
    grads['conf_ln_g'], grads['conf_ln_b'] = dlg, dlb
    cd_g = bwd_call(cd1_fn, "d_cd_conv", (nd, bsz), cd_ins, cd_in_specs, [dvconv, dmix1],
                    [cd_out_spec, pl.BlockSpec((seq, LANE), lambda j, b: (b, nd + j))], list(range(8)),
                    [_sd((t, D))] * 5 + [_sd((CONF_K, D)), _sd((1, D)), _sd((SC_K, D))], [cd_out_spec] * 5 + cd_par,
                    [None] * 5 + [(1,), (1,), (1,)])
    du1 = jnp.concatenate(cd_g[:5], axis=1)
    grads['conf_dw_w'], grads['conf_dw_b'], grads['sc_conv_w'] = cd_g[5][None], cd_g[6], cd_g[7][None]
    g_cd_in = matmul(h4, du1, 'tn', "d_cd_w_in", BF)
    ex.put_grads('cd', G_CD, {('cd_w_in', 0): g_cd_in, ('cd_w_out', 0): g_cd_out})
    dh4 = matmul(du1, big[('cd_w_in', 0)], 'nt', "d_h_cd")

    dx2, dmo0, gain_grads[(0, 5)], gain_grads[(1, 0)] = bwd_seg_res(x2, mo0, gain(0, 5), gain(1, 0), dx3, dh4, "d_res_0c")
    dh3, per_layer['mlp_w1'][0], per_layer['mlp_w2'][0] = mlp_bwd(0, h3, dmo0, sv)
    dx1, dao0, gain_grads[(0, 3)], gain_grads[(0, 4)] = bwd_seg_res(x1, ao0, gain(0, 3), gain(0, 4), dx2, dh3, "d_res_0b")
    dh2, per_layer['xa_wq'][0], per_layer['xa_wkv'][0], per_layer['xa_wo'][0] = attention_bwd(0, h2, dao0, sv)
    ex.put_grads('l0', G_L0, {(k, 0): v[0] for k, v in per_layer.items()})
    dx0r, dm0, gain_grads[(0, 1)], gain_grads[(0, 2)] = bwd_seg_res(x0, m0, gain(0, 1), gain(0, 2), dx1, dh2, "d_res_0a")
    g_ab_out = matmul(mix0, dm0, 'tn', "d_ab_w_out", BF)
    dmix0 = matmul(dm0, big[('ab_w_out', 0)], 'nt', "d_mix0")
    dxs, dbm, dcm, dz, ddt, ddtb, dalog, ddsk, dnw = ssd_bwd(xbc_act, u0, dtb, alog, dsk, p['ssm_norm'], consts, hs, dmix0,
                                                             bsz, seq)
    grads['ssm_dt_bias'] = (ddtb[0] + ddtb[1])[:, :SSM_HEADS]
    grads['ssm_a_log'] = (dalog[0] + dalog[1])[:, :SSM_HEADS]
    grads['ssm_d'] = (ddsk[0] + ddsk[1])[:, :SSM_HEADS]
    grads['ssm_norm'] = dnw
    dxbc_act = jnp.concatenate([dxs, dbm, dcm], axis=1)
    dxr, dcw, dcb = bwd_call(conv4_fn, "d_ssm_conv", (ncb, bsz), [u0, p['ssm_conv_w'][0], p['ssm_conv_b']], conv_in_specs,
                             [dxbc_act], [conv_out_spec], [0, 1, 2],
                             [_sd((t, SSM_CONV_DIM)), _sd((SSM_CONV, SSM_CONV_DIM)), _sd((1, SSM_CONV_DIM))],
                             [conv_out_spec, conv_in_specs[1], conv_in_specs[2]], [None, (1,), (1,)])
    grads['ssm_conv_w'], grads['ssm_conv_b'] = dcw[None], dcb
    dpool, dpw, dps = [], [], []
    for g in range(POOL_GROUPS):
        seqspec = pl.BlockSpec((seq, PG), lambda b, g=g: (b, g))
        one = pl.BlockSpec((seq, PG), lambda b: (b, 0))
        wspec = pl.BlockSpec((PG, PG), lambda b: (0, 0))
        sspec = pl.BlockSpec((1, PG), lambda b, g=g: (0, g))
        a, bb, c = bwd_call(make_pool_fn(g), f"d_pool_{g}", (bsz,), [u0, p['pool_w'][0, g], p['pool_scale']],
                            [seqspec, wspec, sspec], [dmix0], [seqspec], [0, 1, 2],
                            [_sd((t, PG)), _sd((PG, PG)), _sd((1, PG))], [one, wspec, pl.BlockSpec((1, PG), lambda b: (0, 0))],
                            [None, (0,), (0,)])
        dpool.append(a)
        dpw.append(bb)
        dps.append(c)
    grads['pool_w'] = jnp.stack(dpw)[None]
    grads['pool_scale'] = jnp.concatenate(dps, axis=1)
    du0 = jnp.concatenate(dpool + [dz, dxr, ddt[0] + ddt[1]], axis=1)
    grads['ab'] = {('ab_w_in', 0): matmul(h0, du0, 'tn', "d_ab_w_in", BF)[:, :AB_IN], ('ab_w_out', 0): g_ab_out}
    dh0 = matmul(du0, w_ab_in, 'nt', "d_h_ab")
    dx, dg00 = bwd_call(seg_in_res, "d_norm_in", (nb,), [x0, gain(0, 0)], [_rows(D), _par(D)], [dx0r, dh0],
                        [_rows(D), _rows(D)], [0, 1], [_sd((t, D)), _sd((1, D))], [_rows(D), _par(D)], [None, (0,)])
    gain_grads[(0, 0)] = dg00
    grads['norm_gains'] = jnp.stack([jnp.concatenate([gain_grads[(l, i)] for i in range(6)], axis=0) for l in range(2)])
    return loss, dx, grads
```

```python
import functools
import math

import numpy as np
import jax
import jax.numpy as jnp
from jax import lax
from jax.experimental import pallas as pl
from jax.experimental.pallas import tpu as pltpu

BF = jnp.bfloat16
F32 = jnp.float32
HI = lax.Precision.HIGHEST

N_DEV = 8
D = 1024
N_MEM = 256
XA_HEADS = 4
XA_DH = D // XA_HEADS
POOL_GROUPS = 4
PG = 128
POOL_W = POOL_GROUPS * PG
SSM_INNER = 1024
SSM_GROUPS = 2
SSM_GSZ = SSM_INNER // SSM_GROUPS
SSM_HEADS = 16
SSM_P = 64
SSM_N = 128
SSM_CONV = 4
SSM_CONV_DIM = SSM_INNER + 2 * SSM_GROUPS * SSM_N
CHUNK = 128
AB_IN = POOL_W + SSM_INNER + SSM_CONV_DIM + SSM_HEADS
AB_IN_PAD = POOL_W + SSM_INNER + SSM_CONV_DIM + 128
AB_OUT = POOL_W + SSM_INNER
CONF_K = 31
SC_K = 3
CD_IN = 5 * D
CD_OUT = 2 * D
MLP_H = 4 * D
RMS_EPS = 1e-6
LN_EPS = 1e-5
ADAM_LR = 0.001
ADAM_B1 = 0.9
ADAM_B2 = 0.999
ADAM_EPS = 1e-08
ADAM_WD = 0.01
ADAM_STEP = 10
VMEM_LIMIT = 56 * 1024 * 1024
LANE = 128

NAMES = ['x', 'mem', 'norm_gains', 'xa_wq', 'xa_wkv', 'xa_wo', 'mlp_w1', 'mlp_w2', 'ab_w_in', 'pool_w', 'pool_scale',
         'ssm_conv_w', 'ssm_conv_b', 'ssm_dt_bias', 'ssm_a_log', 'ssm_d', 'ssm_norm', 'ab_w_out', 'cd_w_in', 'conf_dw_w',
         'conf_dw_b', 'conf_ln_g', 'conf_ln_b', 'sc_conv_w', 'cd_w_out', 'loss_target']
WEIGHTS = NAMES[2:25]
BIG = [('xa_wq', 1), ('xa_wkv', 2), ('xa_wo', 1), ('mlp_w1', 2), ('mlp_w2', 1), ('cd_w_in', 2), ('cd_w_out', 1),
       ('ab_w_out', 1), ('ab_w_in', 2)]
SMALL_SHARDED = ['norm_gains', 'ssm_conv_w', 'conf_dw_w', 'conf_dw_b', 'conf_ln_g', 'conf_ln_b', 'sc_conv_w']
REPLICATED = ['pool_w', 'pool_scale', 'ssm_conv_b', 'ssm_dt_bias', 'ssm_a_log', 'ssm_d', 'ssm_norm']


def _dg(a, b, ca, cb, prec=None):
    return lax.dot_general(a, b, (((ca,), (cb,)), ((), ())), precision=prec, preferred_element_type=F32)


@functools.partial(jax.custom_vjp, nondiff_argnums=(2, 3))
def bdot(a, b, ca, cb):
    return _dg(a.astype(BF), b.astype(BF), ca, cb)


def _bdot_fwd(a, b, ca, cb):
    return bdot(a, b, ca, cb), (a, b)


def _bdot_bwd(ca, cb, res, g):
    a, b = res
    g16, a16, b16 = g.astype(BF), a.astype(BF), b.astype(BF)
    da = _dg(g16, b16, 1, 1 - cb) if ca == 1 else _dg(b16, g16, 1 - cb, 1)
    db = _dg(g16, a16, 0, 1 - ca) if cb == 1 else _dg(a16, g16, 1 - ca, 0)
    return da.astype(a.dtype), db.astype(b.dtype)


bdot.defvjp(_bdot_fwd, _bdot_bwd)


def _split3(a):
    a1 = a.astype(BF)
    r1 = a - a1.astype(F32)
    a2 = r1.astype(BF)
    a3 = (r1 - a2.astype(F32)).astype(BF)
    return a1, a2, a3


def _exact_right(a, c):
    m = a.shape[0]
    if m % 16:
        return sum(_dg(p, c, 1, 0) for p in _split3(a))
    o = _dg(jnp.concatenate(_split3(a), axis=0), c, 1, 0)
    return o[:m] + o[m:2 * m] + o[2 * m:]


def _exact_left(c, a):
    n = a.shape[1]
    o = _dg(c, jnp.concatenate(_split3(a), axis=1), 1, 0)
    return o[:, :n] + o[:, n:2 * n] + o[:, 2 * n:]


@jax.custom_vjp
def cmat(a, c, ct):
    return _exact_right(a, c)


def _cmat_fwd(a, c, ct):
    return cmat(a, c, ct), (c, ct)


def _cmat_bwd(res, g):
    c, ct = res
    return _exact_right(g, ct), jnp.zeros_like(c), jnp.zeros_like(ct)


cmat.defvjp(_cmat_fwd, _cmat_bwd)


@jax.custom_vjp
def cmatl(c, ct, a):
    return _exact_left(c, a)


def _cmatl_fwd(c, ct, a):
    return cmatl(c, ct, a), (c, ct)


def _cmatl_bwd(res, g):
    c, ct = res
    return jnp.zeros_like(c), jnp.zeros_like(ct), _exact_left(ct, g)


cmatl.defvjp(_cmatl_fwd, _cmatl_bwd)


def _shift_down(x, k):
    t = lax.broadcasted_iota(jnp.int32, x.shape, 0)
    return jnp.where(t >= k, pltpu.roll(x, k, 0), 0.0)


def _shift_up(x, k):
    n = x.shape[0]
    t = lax.broadcasted_iota(jnp.int32, x.shape, 0)
    return jnp.where(t < n - k, pltpu.roll(x, n - k, 0), 0.0)


@functools.partial(jax.custom_vjp, nondiff_argnums=(1,))
def shift(x, k):
    return _shift_down(x, k)


def _shift_fwd(x, k):
    return _shift_down(x, k), None


def _shift_bwd(k, _, g):
    return (_shift_up(g, k),)


shift.defvjp(_shift_fwd, _shift_bwd)


@functools.partial(jax.custom_vjp, nondiff_argnums=(2,))
def cconv(u, w, width):
    acc = u * w[width - 1:width, :]
    for k in range(width - 1):
        acc = acc + _shift_down(u, width - 1 - k) * w[k:k + 1, :]
    return acc


def _cconv_fwd(u, w, width):
    return cconv(u, w, width), (u, w)


def _cconv_bwd(width, res, g):
    u, w = res
    rows = lax.broadcasted_iota(jnp.int32, w.shape, 0)
    du = g * w[width - 1:width, :]
    dw = jnp.where(rows == width - 1, jnp.sum(g * u, axis=0, keepdims=True), 0.0)
    for k in range(width - 1):
        s = width - 1 - k
        du = du + _shift_up(g, s) * w[k:k + 1, :]
        dw = dw + jnp.where(rows == k, jnp.sum(g * _shift_down(u, s), axis=0, keepdims=True), 0.0)
    return du, dw


cconv.defvjp(_cconv_fwd, _cconv_bwd)


def _rms(x, g):
    return x * lax.rsqrt(jnp.mean(x * x, axis=-1, keepdims=True) + RMS_EPS) * g


def _params(sem=None):
    return pltpu.CompilerParams(dimension_semantics=sem, vmem_limit_bytes=VMEM_LIMIT)


def _f32(v):
    return v if v.dtype == F32 else v.astype(F32)


def _first(axes):
    ok = None
    for ax in axes:
        c = pl.program_id(ax) == 0
        ok = c if ok is None else jnp.logical_and(ok, c)
    return ok


def fwd_call(fn, name, grid, ins, in_specs, out_shapes, out_specs):
    n_in = len(ins)

    def body(*refs):
        outs = fn(*[_f32(r[...]) for r in refs[:n_in]])
        for r, o in zip(refs[n_in:], outs):
            r[...] = o.astype(r.dtype)

    return pl.pallas_call(body, name=name, grid=grid, in_specs=in_specs, out_specs=out_specs, out_shape=out_shapes,
                          compiler_params=_params())(*ins)


def bwd_call(fn, name, grid, ins, in_specs, cots, cot_specs, gidx, g_shapes, g_specs, g_acc):
    n_in, n_cot = len(ins), len(cots)

    def body(*refs):
        vals = [_f32(r[...]) for r in refs[:n_in]]

        def f_sel(*dv):
            full = list(vals)
            for i, v in zip(gidx, dv):
                full[i] = v
            return tuple(fn(*full))

        outs, vjp = jax.vjp(f_sel, *[vals[i] for i in gidx])
        cts = tuple(_f32(r[...]) for r in refs[n_in:n_in + n_cot])
        grads = vjp(cts)
        for r, g, acc in zip(refs[n_in + n_cot:], grads, g_acc):
            if acc is None:
                r[...] = g.astype(r.dtype)
            else:
                @pl.when(_first(acc))
                def _():
                    r[...] = jnp.zeros_like(r)

                r[...] += g.astype(r.dtype)

    return pl.pallas_call(body, name=name, grid=grid, in_specs=list(in_specs) + list(cot_specs), out_specs=g_specs,
                          out_shape=g_shapes, compiler_params=_params())(*ins, *cots)


def _tile(dim, pref):
    if dim <= pref:
        return dim
    best = None
    for t in range(LANE, pref + 1, LANE):
        if dim % t == 0:
            best = t
    assert best is not None, dim
    return best


def matmul(a, b, mode, name, out_dtype=F32, tm=1024, tn=1024, tk=1024, epilogue=None, extras=()):
    if mode == 'nn':
        (m, k), (k2, n) = a.shape, b.shape
    elif mode == 'nt':
        (m, k), (n, k2) = a.shape, b.shape
    else:
        (k, m), (k2, n) = a.shape, b.shape
    assert k == k2, (name, a.shape, b.shape)
    tm, tn, tk = _tile(m, tm), _tile(n, tn), _tile(k, tk)
    nk = k // tk
    ca = 0 if mode == 'tn' else 1
    cb = 1 if mode == 'nt' else 0
    a_spec = pl.BlockSpec((tk, tm), lambda i, j, kk: (kk, i)) if mode == 'tn' else pl.BlockSpec((tm, tk), lambda i, j, kk: (i, kk))
    b_spec = pl.BlockSpec((tn, tk), lambda i, j, kk: (j, kk)) if mode == 'nt' else pl.BlockSpec((tk, tn), lambda i, j, kk: (kk, j))

    n_extra = len(extras)
    out_dtypes = out_dtype if isinstance(out_dtype, tuple) else (out_dtype,)

    def body(a_ref, b_ref, *refs):
        extra_refs, o_refs, acc = refs[:n_extra], refs[n_extra:-1], refs[-1]
        kk = pl.program_id(2)

        @pl.when(kk == 0)
        def _():
            acc[...] = jnp.zeros_like(acc)

        acc[...] += _dg(a_ref[...].astype(BF), b_ref[...].astype(BF), ca, cb)

        @pl.when(kk == nk - 1)
        def _():
            outs = (acc[...],) if epilogue is None else epilogue(acc[...], *[_f32(e[...]) for e in extra_refs])
            for o_ref, o in zip(o_refs, outs):
                o_ref[...] = o.astype(o_ref.dtype)

    tile = pl.BlockSpec((tm, tn), lambda i, j, kk: (i, j))
    outs = pl.pallas_call(
        body, name=name, grid=(m // tm, n // tn, nk), in_specs=[a_spec, b_spec] + [tile] * n_extra,
        out_specs=[tile] * len(out_dtypes), out_shape=[jax.ShapeDtypeStruct((m, n), dt) for dt in out_dtypes],
        scratch_shapes=[pltpu.VMEM((tm, tn), F32)],
        compiler_params=_params(("parallel", "parallel", "arbitrary")))(a, b, *extras)
    return outs if isinstance(out_dtype, tuple) else outs[0]


_FLIPS = [(0, 0, 1), (1, 0, 0), (0, 1, 0), (1, 1, 0), (1, 0, 1), (0, 1, 1), (1, 1, 1)]


def _me():
    return lax.axis_index("x"), lax.axis_index("y"), lax.axis_index("c")


def _flip(pos, f):
    return tuple(jnp.where(fi == 1, 1 - p, p) if fi else p for p, fi in zip(pos, f))


def _slot(pos):
    return 4 * pos[0] + 2 * pos[1] + pos[2]


def all_gather(v, name):
    def body(v_ref, out_ref, send_sems, recv_sems, local_sem):
        me = _me()
        sibling = _flip(me, (0, 0, 1))
        chips = [_flip(me, f) for f in ((1, 0, 0), (0, 1, 0), (1, 1, 0))]

        def copy(k, block, to, src=None):
            return pltpu.make_async_remote_copy(
                src_ref=out_ref.at[_slot(block)] if src is None else src, dst_ref=out_ref.at[_slot(block)],
                send_sem=send_sems.at[k], recv_sem=recv_sems.at[k], device_id=to, device_id_type=pl.DeviceIdType.MESH)

        mine = pltpu.make_async_copy(v_ref, out_ref.at[_slot(me)], local_sem)
        mine.start()
        first = [copy(0, me, sibling, src=v_ref)] + [copy(1 + j, me, chip, src=v_ref) for j, chip in enumerate(chips)]
        for cp in first:
            cp.start()
        passed = [copy(4 + j, chip, sibling) for j, chip in enumerate(chips)]
        for j, chip in enumerate(chips):
            copy(1 + j, chip, me).wait_recv()
            passed[j].start()
        copy(0, sibling, me).wait_recv()
        for j, chip in enumerate(chips):
            copy(4 + j, _flip(chip, (0, 0, 1)), me).wait_recv()
        for cp in first + passed:
            cp.wait_send()
        mine.wait()

    return pl.pallas_call(
        body, name=name, out_shape=jax.ShapeDtypeStruct((N_DEV,) + v.shape, v.dtype),
        in_specs=[pl.BlockSpec(memory_space=pl.ANY)], out_specs=pl.BlockSpec(memory_space=pl.ANY),
        scratch_shapes=[pltpu.SemaphoreType.DMA((7,)), pltpu.SemaphoreType.DMA((7,)), pltpu.SemaphoreType.DMA(())],
    )(v)


def all_to_all(v, name):
    def body(v_ref, out_ref, send_sems, recv_sems, local_sem):
        me = _me()
        mine = pltpu.make_async_copy(v_ref.at[_slot(me)], out_ref.at[_slot(me)], local_sem)
        mine.start()
        copies = []
        for k, f in enumerate(_FLIPS):
            peer = _flip(me, f)
            cp = pltpu.make_async_remote_copy(
                src_ref=v_ref.at[_slot(peer)], dst_ref=out_ref.at[_slot(me)], send_sem=send_sems.at[k],
                recv_sem=recv_sems.at[k], device_id=peer, device_id_type=pl.DeviceIdType.MESH)
            cp.start()
            copies.append(cp)
        for k, f in enumerate(_FLIPS):
            peer = _flip(me, f)
            pltpu.make_async_remote_copy(
                src_ref=v_ref.at[_slot(peer)], dst_ref=out_ref.at[_slot(peer)], send_sem=send_sems.at[k],
                recv_sem=recv_sems.at[k], device_id=peer, device_id_type=pl.DeviceIdType.MESH).wait_recv()
        for cp in copies:
            cp.wait_send()
        mine.wait()

    return pl.pallas_call(
        body, name=name, out_shape=jax.ShapeDtypeStruct(v.shape, v.dtype),
        in_specs=[pl.BlockSpec(memory_space=pl.ANY)], out_specs=pl.BlockSpec(memory_space=pl.ANY),
        scratch_shapes=[pltpu.SemaphoreType.DMA((7,)), pltpu.SemaphoreType.DMA((7,)), pltpu.SemaphoreType.DMA(())],
    )(v)


def sum_slots(v, name, tr=256):
    _, r, c = v.shape
    tr = _tile_rows(r, tr)

    def body(v_ref, o_ref):
        acc = v_ref[0].astype(F32)
        for s in range(1, N_DEV):
            acc = acc + v_ref[s].astype(F32)
        o_ref[...] = acc

    return pl.pallas_call(body, name=name, grid=(r // tr,), in_specs=[pl.BlockSpec((N_DEV, tr, c), lambda i: (0, i, 0))],
                          out_specs=pl.BlockSpec((tr, c), lambda i: (i, 0)), out_shape=jax.ShapeDtypeStruct((r, c), F32),
                          compiler_params=_params())(v)


def _tile_rows(r, pref):
    if r <= pref:
        return r
    best = None
    for t in range(8, pref + 1, 8):
        if r % t == 0:
            best = t
    return r if best is None else best


def adamw(w, m, v, g, name):
    r, c = w.shape
    tr = _tile_rows(r, 512 if c <= 1024 else 128)

    def body(w_ref, m_ref, v_ref, g_ref, d_ref, nm_ref, nv_ref):
        gg = g_ref[...]
        nm = ADAM_B1 * m_ref[...] + (1.0 - ADAM_B1) * gg
        nv = ADAM_B2 * v_ref[...] + (1.0 - ADAM_B2) * jnp.square(gg)
        m_hat = nm / (1.0 - ADAM_B1 ** ADAM_STEP)
        v_hat = nv / (1.0 - ADAM_B2 ** ADAM_STEP)
        d_ref[...] = -ADAM_LR * (m_hat / (jnp.sqrt(v_hat) + ADAM_EPS) + ADAM_WD * w_ref[...])
        nm_ref[...] = nm
        nv_ref[...] = nv

    spec = pl.BlockSpec((tr, c), lambda i: (i, 0))
    sh = jax.ShapeDtypeStruct((r, c), F32)
    return pl.pallas_call(body, name=name, grid=(r // tr,), in_specs=[spec] * 4, out_specs=[spec] * 3,
                          out_shape=[sh] * 3, compiler_params=_params())(w, m, v, g)


def seg_in(x, g):
    return (_rms(x, g),)


def seg_in_res(x, g):
    return x, _rms(x, g)


def seg_res(x, m, ga, gb):
    x1 = x + _rms(m, ga)
    return x1, _rms(x1, gb)


def seg_out(x, m, ga):
    return (x + _rms(m, ga),)


def act_epilogue(r):
    t = jnp.maximum(r, 0.0)
    return r, t * t


def act_bwd_epilogue(drr, r):
    return (drr * (2.0 * jnp.maximum(r, 0.0)),)


def seg_ln(v, g, b):
    mu = jnp.mean(v, axis=-1, keepdims=True)
    var = jnp.mean(jnp.square(v - mu), axis=-1, keepdims=True)
    vn = (v - mu) * lax.rsqrt(var + LN_EPS) * g + b
    return (jax.nn.silu(vn),)


def make_pool_fn(group):
    window = 2 ** (group + 1)

    def pool_fn(ug, pw, scale):
        s = ug
        for lvl in range(group + 1):
            s = s + shift(s, 2 ** lvl)
        cnt = jnp.minimum(lax.broadcasted_iota(jnp.int32, ug.shape, 0) + 1, window).astype(F32)
        return (bdot(s / cnt - ug, pw, 1, 0) * scale,)

    return pool_fn


def conv4_fn(xr, w, b):
    return (jax.nn.silu(cconv(xr, w, SSM_CONV) + b),)


def cd1_fn(val, gate, bg, cg, hh, dww, dwb, scw):
    v = val * jax.nn.sigmoid(gate)
    vc = cconv(v, dww, CONF_K) + dwb
    sc = bg * cconv(cg * hh, scw, SC_K)
    return vc, sc


def attn_fn(q, k, v):
    s = bdot(q, k, 1, 1) / math.sqrt(XA_DH)
    p = jax.nn.softmax(s, axis=-1)
    return (bdot(p, v, 1, 0),)


def ssd_chunk(xs, bm, cm, z, dtraw, dtb, alog, dsk, nw, h0, h1, h2, h3, e64, e64t, ecat, ecatt, tril, trilt):
    hin = (h0, h1, h2, h3)
    dt = jax.nn.softplus(dtraw + dtb)
    a = -jnp.exp(alog)
    d_a = dt * a
    cs = cmatl(tril, trilt, d_a)
    cs_cat = cmat(cs, ecat, ecatt)
    cs64, cs128 = cs_cat[:, :SSM_GSZ], cs_cat[:, SSM_GSZ:]
    dt64 = cmat(dt, e64, e64t)
    row = lax.broadcasted_iota(jnp.int32, (8, LANE), 0)
    heads = jnp.where(row == 0, dsk, jnp.where(row == 1, jnp.sum(d_a, axis=0, keepdims=True), 0.0))
    heads64 = cmat(heads, e64, e64t)
    d64, tot64 = heads64[0:1, :], heads64[1:2, :]
    xdt = xs * dt64
    cb = bdot(cm, bm, 1, 1)
    li = lax.broadcasted_iota(jnp.int32, (CHUNK, CHUNK), 0)
    si = lax.broadcasted_iota(jnp.int32, (CHUNK, CHUNK), 1)
    causal = li >= si
    lane = lax.broadcasted_iota(jnp.int32, (CHUNK, LANE), 1)
    xw = xdt * jnp.exp(tot64 - cs64)
    ecs = jnp.exp(cs64)
    etot = jnp.exp(tot64)
    ycols, hout = [], []
    for j in range(4):
        sl = slice(j * LANE, (j + 1) * LANE)
        xj = xdt[:, sl]
        ys = []
        for hh in range(2):
            r = 2 * j + hh
            col = cs128[:, r * LANE:(r + 1) * LANE]
            decay = jnp.exp(jnp.where(causal, col - col.T, -1e30))
            ys.append(bdot(cb * decay, xj, 1, 0))
        y_diag = jnp.where(lane < SSM_P, ys[0], ys[1])
        y_off = bdot(cm, hin[j], 1, 0) * ecs[:, sl]
        ycols.append(y_diag + y_off)
        hout.append(etot[:, sl] * hin[j] + bdot(bm, xw[:, sl], 0, 0))
    y = jnp.concatenate(ycols, axis=1) + d64 * xs
    y = y * jax.nn.silu(z)
    yn = y * lax.rsqrt(jnp.mean(y * y, axis=-1, keepdims=True) + RMS_EPS) * nw
    return (yn,) + tuple(hout)


def _ssd_consts():
    h = np.arange(LANE)[:, None]
    e64 = np.stack([(h == g * 8 + np.arange(SSM_GSZ)[None, :] // SSM_P) for g in range(SSM_GROUPS)]).astype(np.float32)
    e128 = np.stack([(h == g * 8 + np.arange(8 * LANE)[None, :] // LANE) for g in range(SSM_GROUPS)]).astype(np.float32)
    ecat = np.concatenate([e64, e128], axis=2)
    tril = np.tril(np.ones((CHUNK, CHUNK), np.float32))
    return tuple(jnp.asarray(c, dtype=BF) for c in (e64, e64.transpose(0, 2, 1), ecat, ecat.transpose(0, 2, 1), tril, tril.T))


def _ssd_specs(nc, rev):
    def ci(c):
        return nc - 1 - c if rev else c

    def row(width, col):
        return pl.BlockSpec((CHUNK, width), lambda g, b, c: (b * nc + ci(c), col(g)))

    data = [row(SSM_GSZ, lambda g: g), row(SSM_N, lambda g: 8 + g), row(SSM_N, lambda g: 10 + g),
            row(SSM_GSZ, lambda g: 1 + g), row(LANE, lambda g: 24)]
    par = [pl.BlockSpec((1, LANE), lambda g, b, c: (0, 0))] * 3 + [pl.BlockSpec((1, SSM_GSZ), lambda g, b, c: (0, g))]
    cst = [pl.BlockSpec((None, LANE, SSM_GSZ), lambda g, b, c: (g, 0, 0)), pl.BlockSpec((None, SSM_GSZ, LANE), lambda g, b, c: (g, 0, 0)),
           pl.BlockSpec((None, LANE, 12 * LANE), lambda g, b, c: (g, 0, 0)), pl.BlockSpec((None, 12 * LANE, LANE), lambda g, b, c: (g, 0, 0)),
           pl.BlockSpec((CHUNK, CHUNK), lambda g, b, c: (0, 0)), pl.BlockSpec((CHUNK, CHUNK), lambda g, b, c: (0, 0))]
    hsave = pl.BlockSpec((None, None, None, 4, SSM_N, LANE), lambda g, b, c: (g, b, ci(c), 0, 0, 0))
    yn = row(SSM_GSZ, lambda g: g)
    return data, par, cst, hsave, yn, row


def ssd_fwd(xbc_act, u, dtb, alog, dsk, nw, consts, bsz, seq):
    nc = seq // CHUNK
    data, par, cst, hsave, yn_spec, _ = _ssd_specs(nc, False)

    def body(xs, bm, cm, z, dtr, dtb_r, alog_r, dsk_r, nw_r, e64, e64t, ecat, ecatt, tril, trilt, yn_ref, hs_ref, h):
        @pl.when(pl.program_id(2) == 0)
        def _():
            h[...] = jnp.zeros_like(h)

        hs_ref[...] = h[...]
        outs = ssd_chunk(xs[...], bm[...], cm[...], z[...], dtr[...], dtb_r[...], alog_r[...], dsk_r[...], nw_r[...],
                         h[0], h[1], h[2], h[3], e64[...], e64t[...], ecat[...], ecatt[...], tril[...], trilt[...])
        yn_ref[...] = outs[0].astype(yn_ref.dtype)
        for j in range(4):
            h[j] = outs[1 + j]

    t = bsz * seq
    return pl.pallas_call(
        body, name="ssd_fwd", grid=(SSM_GROUPS, bsz, nc), in_specs=data + par + cst, out_specs=[yn_spec, hsave],
        out_shape=[jax.ShapeDtypeStruct((t, SSM_INNER), BF), jax.ShapeDtypeStruct((SSM_GROUPS, bsz, nc, 4, SSM_N, LANE), F32)],
        scratch_shapes=[pltpu.VMEM((4, SSM_N, LANE), F32)], compiler_params=_params(),
    )(xbc_act, xbc_act, xbc_act, u, u, dtb, alog, dsk, nw, *consts)


def ssd_bwd(xbc_act, u, dtb, alog, dsk, nw, consts, hs, dmix, bsz, seq):
    nc = seq // CHUNK
    data, par, cst, hsave, _, row = _ssd_specs(nc, True)
    t = bsz * seq
    dyn_spec = row(SSM_GSZ, lambda g: POOL_W // SSM_GSZ + g)

    def body(xs, bm, cm, z, dtr, dtb_r, alog_r, dsk_r, nw_r, e64, e64t, ecat, ecatt, tril, trilt, hs_ref, dyn_ref,
             dxs, dbm, dcm, dz, ddt, ddtb, dalog, ddsk, dnw, dh):
        @pl.when(pl.program_id(2) == 0)
        def _():
            dh[...] = jnp.zeros_like(dh)

        cst_vals = (e64[...], e64t[...], ecat[...], ecatt[...], tril[...], trilt[...])

        def f(*args):
            return ssd_chunk(*args, *cst_vals)

        prim = (xs[...], bm[...], cm[...], z[...], dtr[...], dtb_r[...], alog_r[...], dsk_r[...], nw_r[...],
                hs_ref[0], hs_ref[1], hs_ref[2], hs_ref[3])
        _, vjp = jax.vjp(f, *prim)
        g = vjp((dyn_ref[...].astype(F32), dh[0], dh[1], dh[2], dh[3]))
        dxs[...] = g[0]
        dbm[...] = g[1]
        dcm[...] = g[2]
        dz[...] = g[3]
        ddt[...] = g[4]

        @pl.when(_first((1, 2)))
        def _():
            for r in (ddtb, dalog, ddsk, dnw):
                r[...] = jnp.zeros_like(r)

        ddtb[...] += g[5]
        dalog[...] += g[6]
        ddsk[...] += g[7]
        dnw[...] += g[8]
        for j in range(4):
            dh[j] = g[9 + j]

    gpar = pl.BlockSpec((None, 1, LANE), lambda g, b, c: (g, 0, 0))
    out_specs = [row(SSM_GSZ, lambda g: g), row(SSM_N, lambda g: g), row(SSM_N, lambda g: g), row(SSM_GSZ, lambda g: g),
                 pl.BlockSpec((None, CHUNK, LANE), lambda g, b, c: (g, b * nc + nc - 1 - c, 0)),
                 gpar, gpar, gpar, pl.BlockSpec((1, SSM_GSZ), lambda g, b, c: (0, g))]
    gp = jax.ShapeDtypeStruct((SSM_GROUPS, 1, LANE), F32)
    out_shape = [jax.ShapeDtypeStruct((t, SSM_INNER), F32), jax.ShapeDtypeStruct((t, SSM_GROUPS * SSM_N), F32),
                 jax.ShapeDtypeStruct((t, SSM_GROUPS * SSM_N), F32), jax.ShapeDtypeStruct((t, SSM_INNER), F32),
                 jax.ShapeDtypeStruct((SSM_GROUPS, t, LANE), F32), gp, gp, gp, jax.ShapeDtypeStruct((1, SSM_INNER), F32)]
    return pl.pallas_call(
        body, name="ssd_bwd", grid=(SSM_GROUPS, bsz, nc), in_specs=data + par + cst + [hsave, dyn_spec], out_specs=out_specs,
        out_shape=out_shape, scratch_shapes=[pltpu.VMEM((4, SSM_N, LANE), F32)], compiler_params=_params(),
    )(xbc_act, xbc_act, xbc_act, u, u, dtb, alog, dsk, nw, *consts, hs, dmix)


TB = 512


def _rows(d, col=0):
    return pl.BlockSpec((TB, d), lambda i: (i, col))


def _par(d):
    return pl.BlockSpec((1, d), lambda i: (0, 0))


def _sd(shape, dtype=F32):
    return jax.ShapeDtypeStruct(shape, dtype)


def _round_up(n, m):
    return -(-n // m) * m


def _pad_rows(a, rows):
    return jnp.pad(a, ((0, rows - a.shape[0]), (0, 0)))


def _pack128(arrs):
    flat = jnp.concatenate([a.reshape(-1) for a in arrs])
    n = flat.shape[0]
    rows = -(-n // (8 * LANE)) * 8
    return jnp.pad(flat, (0, rows * LANE - n)).reshape(rows, LANE)


def _unpack128(packed, shapes):
    flat = packed.reshape(-1)
    out, off = [], 0
    for s in shapes:
        n = int(np.prod(s))
        out.append(flat[off:off + n].reshape(s))
        off += n
    return out


def kernel(x, mem, norm_gains, xa_wq, xa_wkv, xa_wo, mlp_w1, mlp_w2, ab_w_in, pool_w, pool_scale, ssm_conv_w, ssm_conv_b, ssm_dt_bias, ssm_a_log, ssm_d, ssm_norm, ab_w_out, cd_w_in, conf_dw_w, conf_dw_b, conf_ln_g, conf_ln_b, sc_conv_w, cd_w_out, loss_target, m_norm_gains, m_xa_wq, m_xa_wkv, m_xa_wo, m_mlp_w1, m_mlp_w2, m_ab_w_in, m_pool_w, m_pool_scale, m_ssm_conv_w, m_ssm_conv_b, m_ssm_dt_bias, m_ssm_a_log, m_ssm_d, m_ssm_norm, m_ab_w_out, m_cd_w_in, m_conf_dw_w, m_conf_dw_b, m_conf_ln_g, m_conf_ln_b, m_sc_conv_w, m_cd_w_out, v_norm_gains, v_xa_wq, v_xa_wkv, v_xa_wo, v_mlp_w1, v_mlp_w2, v_ab_w_in, v_pool_w, v_pool_scale, v_ssm_conv_w, v_ssm_conv_b, v_ssm_dt_bias, v_ssm_a_log, v_ssm_d, v_ssm_norm, v_ab_w_out, v_cd_w_in, v_conf_dw_w, v_conf_dw_b, v_conf_ln_g, v_conf_ln_b, v_sc_conv_w, v_cd_w_out):
    args = locals()
    w = {n: args[n] for n in WEIGHTS}
    mom_m = {n: args["m_" + n] for n in WEIGHTS}
    mom_v = {n: args["v_" + n] for n in WEIGHTS}
    ex = Exchange(w)
    loss_local, grad_x, small_grads = local_step(x, mem, loss_target, ex)
    loss = lax.psum(loss_local, ("x", "y", "c"))
    outs = {}

    def update_big(names, own):
        last = None
        for n in names:
            shp = w[n].shape
            view = (-1, shp[-1])
            g = jnp.stack([own[(n, layer)] for layer in range(shp[0])])
            d, nm, nv = adamw(w[n].reshape(view), mom_m[n].reshape(view), mom_v[n].reshape(view), g.reshape(view), "adamw_" + n)
            outs[n] = (g, d.reshape(shp), nm.reshape(shp), nv.reshape(shp))
            last = d
        return last

    own = {}
    for key in ('l1', 'cd', 'l0'):
        own.update(ex.reduced(key, grad_x))
    late = update_big(['xa_wq', 'xa_wkv', 'xa_wo', 'mlp_w1', 'mlp_w2', 'cd_w_in', 'cd_w_out'], own)
    g_own = ex.reduced_small(small_grads)
    update_big(['ab_w_in', 'ab_w_out'], ex.reduced('ab', late))
    small = SMALL_SHARDED + REPLICATED
    shapes = [w[n].shape for n in small]
    d, nm, nv = adamw(_pack128([w[n] for n in small]), _pack128([mom_m[n] for n in small]), _pack128([mom_v[n] for n in small]),
                      _pack128([g_own[n] for n in small]), "adamw_small")
    for n, dd, mm, vv in zip(small, _unpack128(d, shapes), _unpack128(nm, shapes), _unpack128(nv, shapes)):
        outs[n] = (g_own[n], dd, mm, vv)
    return (loss, grad_x.reshape(x.shape), *[outs[n][0] for n in WEIGHTS], *[outs[n][1] for n in WEIGHTS],
            *[outs[n][2] for n in WEIGHTS], *[outs[n][3] for n in WEIGHTS])


G_AB = (('ab_w_in', 0), ('ab_w_out', 0))
G_L0 = (('xa_wq', 0), ('xa_wkv', 0), ('xa_wo', 0), ('mlp_w1', 0), ('mlp_w2', 0))
G_L1 = (('xa_wq', 1), ('xa_wkv', 1), ('xa_wo', 1), ('mlp_w1', 1), ('mlp_w2', 1))
G_CD = (('cd_w_in', 0), ('cd_w_out', 0))
SHARD_AXIS = dict(BIG)
MEMBER_ROW_TILE = 64
FLAT_ROW_TILE = 128


def _members(group, w):
    out = []
    for n, layer in group:
        shp = w[n].shape[1:]
        rows = shp[0] * shp[1] // D
        out.append((n, layer, shp, rows, _round_up(rows, MEMBER_ROW_TILE)))
    return out


def _group_rows(group, w):
    return _round_up(sum(m[4] for m in _members(group, w)), FLAT_ROW_TILE)


def _flat_shards(group, w):
    parts = [_pad_rows(w[n][layer].astype(BF).reshape(-1, D), padded) for n, layer, _, _, padded in _members(group, w)]
    flat = jnp.concatenate(parts, axis=0)
    return _pad_rows(flat, _group_rows(group, w))


def _full_from_slots(land, group, w):
    out, off = {}, 0
    for n, layer, shp, rows, padded in _members(group, w):
        blk = land[:, off:off + rows]
        off += padded
        if SHARD_AXIS[n] == 1:
            out[(n, layer)] = blk.reshape(N_DEV * shp[0], shp[1])
        else:
            out[(n, layer)] = blk.reshape(N_DEV, shp[0], shp[1]).transpose(1, 0, 2).reshape(shp[0], N_DEV * shp[1])
    return out


def _slots_from_full(grads, group, w):
    parts = []
    for n, layer, shp, rows, padded in _members(group, w):
        g = grads[(n, layer)].astype(BF)
        if SHARD_AXIS[n] == 1:
            blk = g.reshape(N_DEV, rows, D)
        else:
            blk = g.reshape(shp[0], N_DEV, shp[1]).transpose(1, 0, 2).reshape(N_DEV, rows, D)
        parts.append(jnp.pad(blk, ((0, 0), (0, padded - rows), (0, 0))))
    send = jnp.concatenate(parts, axis=1)
    return jnp.pad(send, ((0, 0), (0, _group_rows(group, w) - send.shape[1]), (0, 0)))


def _own_from_sum(summed, group, w):
    out, off = {}, 0
    for n, layer, shp, rows, padded in _members(group, w):
        out[(n, layer)] = summed[off:off + rows].reshape(shp)
        off += padded
    return out


_HBM = pl.BlockSpec(memory_space=pltpu.HBM)
_SEM = pl.BlockSpec(memory_space=pltpu.SEMAPHORE)
_ANY = pl.BlockSpec(memory_space=pl.ANY)


def _peer_copy(k, src, dst, send_sems, recv_sems, peer):
    return pltpu.make_async_remote_copy(src_ref=src, dst_ref=dst, send_sem=send_sems.at[k], recv_sem=recv_sems.at[k],
                                        device_id=peer, device_id_type=pl.DeviceIdType.MESH)


def exchange_start(src, name, scatter):
    shape = src.shape[-2:]

    def body(src_ref, land_ref, send_sems, recv_sems, src_thru, land_thru, token):
        me = _me()
        for k, f in enumerate(_FLIPS):
            peer = _flip(me, f)
            piece = src_ref.at[_slot(peer)] if scatter else src_ref
            _peer_copy(k, piece, land_ref.at[_slot(me)], send_sems, recv_sems, peer).start()
        token[...] = jnp.zeros_like(token)

    land = pltpu.with_memory_space_constraint(lax.empty((N_DEV,) + shape, src.dtype), pltpu.HBM)
    return pl.pallas_call(
        body, name=name,
        out_shape=(pltpu.SemaphoreType.DMA((7,)), pltpu.SemaphoreType.DMA((7,)), pltpu.HBM(src.shape, src.dtype),
                   pltpu.HBM((N_DEV,) + shape, src.dtype), jax.ShapeDtypeStruct((8, LANE), F32)),
        in_specs=(_HBM, _HBM), out_specs=(_SEM, _SEM, _HBM, _HBM, pl.BlockSpec(memory_space=pltpu.VMEM)),
        input_output_aliases={0: 2, 1: 3},
        compiler_params=pltpu.CompilerParams(has_side_effects=pltpu.SideEffectType.DATAFLOW_SIDE_EFFECTING),
    )(pltpu.with_memory_space_constraint(src, pltpu.HBM), land)


def exchange_wait(handles, after, name, scatter):
    send_sems, recv_sems, src_thru, land_thru, _ = handles

    def body(src_ref, land_ref, send_sems, recv_sems, after_ref, src_dead, got_ref, token):
        me = _me()
        for k, f in enumerate(_FLIPS):
            peer = _flip(me, f)
            piece = src_ref.at[_slot(peer)] if scatter else src_ref
            cp = _peer_copy(k, piece, land_ref.at[_slot(peer)], send_sems, recv_sems, peer)
            cp.wait_send()
            cp.wait_recv()
        token[...] = jnp.zeros_like(token)

    return pl.pallas_call(
        body, name=name, out_shape=(pltpu.HBM(src_thru.shape, src_thru.dtype), pltpu.HBM(land_thru.shape, land_thru.dtype),
                                    jax.ShapeDtypeStruct((8, LANE), F32)),
        in_specs=(_HBM, _HBM, _SEM, _SEM, _ANY), out_specs=(_HBM, _HBM, pl.BlockSpec(memory_space=pltpu.VMEM)),
        input_output_aliases={0: 0, 1: 1},
        compiler_params=pltpu.CompilerParams(has_side_effects=pltpu.SideEffectType.DATAFLOW_SIDE_EFFECTING),
    )(src_thru, land_thru, send_sems, recv_sems, after)


class Exchange:
    def __init__(self, w):
        self.w = w
        self.me = _slot(_me())
        shapes = [w[n].shape for n in SMALL_SHARDED]
        gs = all_gather(_pack128([w[n] for n in SMALL_SHARDED]), "gather_small")
        per_dev = [_unpack128(gs[d], shapes) for d in range(N_DEV)]
        self.small = {n: jnp.concatenate([per_dev[d][i] for d in range(N_DEV)], axis=-1) for i, n in enumerate(SMALL_SHARDED)}
        self.small.update({n: w[n] for n in REPLICATED})
        self.now = _full_from_slots(all_gather(_flat_shards(G_AB, w), "gather_ab"), G_AB, w)
        self.gathers = {'l0': (G_L0, exchange_start(_flat_shards(G_L0, w), "gather_l0_start", False))}
        self.tokens = [self.gathers['l0'][1][4]]
        self.reductions = {}

    def take_tokens(self):
        toks, self.tokens = self.tokens, []
        return toks

    def weights(self, key, after):
        if key == 'ab':
            return self.now
        group, handles = self.gathers[key]
        _, land, done = exchange_wait(handles, after, f"gather_{key}_wait", False)
        if key == 'l0':
            for nxt, grp in (('cd', G_CD), ('l1', G_L1)):
                src = _flat_shards(grp, self.w) + done[0, 0].astype(BF)
                self.gathers[nxt] = (grp, exchange_start(src, f"gather_{nxt}_start", False))
                self.tokens.append(self.gathers[nxt][1][4])
        land = lax.dynamic_update_slice(land, handles[2][None], (self.me, 0, 0))
        return _full_from_slots(land, group, self.w)

    def put_grads(self, key, group, grads):
        send = _slots_from_full(grads, group, self.w)
        handles = exchange_start(send, f"reduce_{key}_start", True)
        self.reductions[key] = (group, handles)
        self.tokens.append(handles[4])

    def reduced(self, key, after):
        group, handles = self.reductions[key]
        send, land, _ = exchange_wait(handles, after, f"reduce_{key}_wait", True)
        mine = lax.dynamic_slice_in_dim(send, self.me, 1, axis=0)
        land = lax.dynamic_update_slice(land, mine, (self.me, 0, 0))
        return _own_from_sum(sum_slots(land, f"sum_{key}", FLAT_ROW_TILE), group, self.w)

    def reduced_small(self, small_grads):
        small = SMALL_SHARDED + REPLICATED
        gs = all_gather(_pack128([small_grads[n] for n in small]), "gather_small_grads")
        tot = _unpack128(sum_slots(gs, "sum_small", 1024), [small_grads[n].shape for n in small])
        out = {}
        for n, g in zip(small, tot):
            if n in SMALL_SHARDED:
                width = self.w[n].shape[-1]
                g = lax.dynamic_slice_in_dim(g, self.me * width, width, axis=g.ndim - 1)
            out[n] = g
        return out


def local_step(x, mem, target, ex):
    bsz, seq, _ = x.shape
    t = bsz * seq
    nb = t // TB
    nc = seq // CHUNK
    x0 = x.reshape(t, D)
    mem2 = mem.reshape(bsz * N_MEM, D)
    tgt = target.reshape(t, D)
    p = ex.small
    gains = p['norm_gains']
    big = dict(ex.weights('ab', None))

    def gain(layer, i):
        g = gains[layer, i].reshape(1, D)
        for tok in ex.take_tokens():
            g = g + tok[0, 0]
        return g

    consts = _ssd_consts()
    grads = {}
    saved = [dict(), dict()]

    def run_seg_res(xin, m, ga, gb, name):
        return fwd_call(seg_res, name, (nb,), [xin, m, ga, gb], [_rows(D), _rows(D), _par(D), _par(D)],
                        [_sd((t, D)), _sd((t, D), BF)], [_rows(D), _rows(D)])

    def attn_specs():
        nq = seq // TB
        q = pl.BlockSpec((TB, XA_DH), lambda b, h, i: (b * nq + i, h))
        k = pl.BlockSpec((N_MEM, XA_DH), lambda b, h, i: (b, h))
        v = pl.BlockSpec((N_MEM, XA_DH), lambda b, h, i: (b, XA_HEADS + h))
        return (bsz, XA_HEADS, nq), q, k, v

    def attention_fwd(layer, xin, hin, sv):
        q = matmul(hin, big[('xa_wq', layer)], 'nn', f"q_{layer}", BF)
        kv = matmul(mem2, big[('xa_wkv', layer)], 'nn', f"kv_{layer}", BF)
        grid, qs, ks, vs = attn_specs()
        o, = fwd_call(attn_fn, f"attn_{layer}", grid, [q, kv, kv], [qs, ks, vs], [_sd((t, D), BF)], [qs])
        ao = matmul(o, big[('xa_wo', layer)], 'nn', f"ao_{layer}")
        sv.update(q=q, kv=kv, o=o, ao=ao)
        return ao

    def mlp_fwd(layer, hin, sv):
        r, rr = matmul(hin, big[('mlp_w1', layer)], 'nn', f"mlp1_{layer}", (F32, BF), epilogue=act_epilogue)
        mo = matmul(rr, big[('mlp_w2', layer)], 'nn', f"mlp2_{layer}")
        sv.update(r=r, rr=rr, mo=mo)
        return mo

    sv = saved[0]
    h0, = fwd_call(seg_in, "norm_in", (nb,), [x0, gain(0, 0)], [_rows(D), _par(D)], [_sd((t, D), BF)], [_rows(D)])
    w_ab_in = jnp.pad(big[('ab_w_in', 0)], ((0, 0), (0, AB_IN_PAD - AB_IN)))
    u0 = matmul(h0, w_ab_in, 'nn', "ab_in")
    pool_outs = []
    for g in range(POOL_GROUPS):
        seqspec = pl.BlockSpec((seq, PG), lambda b, g=g: (b, g))
        po, = fwd_call(make_pool_fn(g), f"pool_{g}", (bsz,), [u0, p['pool_w'][0, g], p['pool_scale']],
                       [seqspec, pl.BlockSpec((PG, PG), lambda b: (0, 0)), pl.BlockSpec((1, PG), lambda b, g=g: (0, g))],
                       [_sd((t, PG), BF)], [pl.BlockSpec((seq, PG), lambda b: (b, 0))])
        pool_outs.append(po)
    cw = 256
    ncb = SSM_CONV_DIM // cw
    cbase = (POOL_W + SSM_INNER) // cw
    conv_in_specs = [pl.BlockSpec((seq, cw), lambda j, b: (b, cbase + j)), pl.BlockSpec((SSM_CONV, cw), lambda j, b: (0, j)),
                     pl.BlockSpec((1, cw), lambda j, b: (0, j))]
    conv_out_spec = pl.BlockSpec((seq, cw), lambda j, b: (b, j))
    xbc_act, = fwd_call(conv4_fn, "ssm_conv", (ncb, bsz), [u0, p['ssm_conv_w'][0], p['ssm_conv_b']], conv_in_specs,
                        [_sd((t, SSM_CONV_DIM))], [conv_out_spec])
    dtb = jnp.pad(p['ssm_dt_bias'], ((0, 0), (0, LANE - SSM_HEADS)))
    alog = jnp.pad(p['ssm_a_log'], ((0, 0), (0, LANE - SSM_HEADS)))
    dsk = jnp.pad(p['ssm_d'], ((0, 0), (0, LANE - SSM_HEADS)))
    yn, hs = ssd_fwd(xbc_act, u0, dtb, alog, dsk, p['ssm_norm'], consts, bsz, seq)
    mix0 = jnp.concatenate(pool_outs + [yn], axis=1)
    m0 = matmul(mix0, big[('ab_w_out', 0)], 'nn', "ab_out")
    x1, h2 = run_seg_res(x0, m0, gain(0, 1), gain(0, 2), "res_0a")
    big.update(ex.weights('l0', h2))
    ao0 = attention_fwd(0, x1, h2, sv)
    x2, h3 = run_seg_res(x1, ao0, gain(0, 3), gain(0, 4), "res_0b")
    mo0 = mlp_fwd(0, h3, sv)
    x3, h4 = run_seg_res(x2, mo0, gain(0, 5), gain(1, 0), "res_0c")

    sv1 = saved[1]
    big.update(ex.weights('cd', h4))
    u1 = matmul(h4, big[('cd_w_in', 0)], 'nn', "cd_in")
    nd = D // LANE

    def cd_col(k):
        return pl.BlockSpec((seq, LANE), lambda j, b, k=k: (b, k * nd + j))

    cd_par = [pl.BlockSpec((CONF_K, LANE), lambda j, b: (0, j)), pl.BlockSpec((1, LANE), lambda j, b: (0, j)),
              pl.BlockSpec((SC_K, LANE), lambda j, b: (0, j))]
    cd_ins = [u1] * 5 + [p['conf_dw_w'][0], p['conf_dw_b'], p['sc_conv_w'][0]]
    cd_in_specs = [cd_col(k) for k in range(5)] + cd_par
    cd_out_spec = pl.BlockSpec((seq, LANE), lambda j, b: (b, j))
    vconv, sc_out = fwd_call(cd1_fn, "cd_conv", (nd, bsz), cd_ins, cd_in_specs, [_sd((t, D)), _sd((t, D), BF)],
                             [cd_out_spec, cd_out_spec])
    conf, = fwd_call(seg_ln, "conf_ln", (nb,), [vconv, p['conf_ln_g'], p['conf_ln_b']], [_rows(D), _par(D), _par(D)],
                     [_sd((t, D), BF)], [_rows(D)])
    mix1 = jnp.concatenate([conf, sc_out], axis=1)
    m1 = matmul(mix1, big[('cd_w_out', 0)], 'nn', "cd_out")
    x4, h5 = run_seg_res(x3, m1, gain(1, 1), gain(1, 2), "res_1a")
    big.update(ex.weights('l1', h5))
    ao1 = attention_fwd(1, x4, h5, sv1)
    x5, h6 = run_seg_res(x4, ao1, gain(1, 3), gain(1, 4), "res_1b")
    mo1 = mlp_fwd(1, h6, sv1)

    def loss_body(x_ref, m_ref, g_ref, t_ref, dy_ref, acc_ref):
        y = x_ref[...] + _rms(m_ref[...], g_ref[...])
        d = y - t_ref[...]
        dy_ref[...] = d / float(D)

        @pl.when(pl.program_id(0) == 0)
        def _():
            acc_ref[...] = jnp.zeros_like(acc_ref)

        acc_ref[...] += jnp.sum(d * d, axis=0, keepdims=True)

    dy, lanes = pl.pallas_call(
        loss_body, name="loss_head", grid=(nb,), in_specs=[_rows(D), _rows(D), _par(D), _rows(D)],
        out_specs=[_rows(D), _par(D)], out_shape=[_sd((t, D)), _sd((1, D))], compiler_params=_params())(x5, mo1, gain(1, 5), tgt)
    loss = 0.5 * jnp.sum(lanes) / float(D)

    gain_grads = {}

    def bwd_seg_out(xin, m, ga, dyv, name):
        dx, dm, dga = bwd_call(seg_out, name, (nb,), [xin, m, ga], [_rows(D), _rows(D), _par(D)], [dyv], [_rows(D)],
                               [0, 1, 2], [_sd((t, D)), _sd((t, D), BF), _sd((1, D))], [_rows(D), _rows(D), _par(D)],
                               [None, None, (0,)])
        return dx, dm, dga

    def bwd_seg_res(xin, m, ga, gb, dx1, dh, name):
        return bwd_call(seg_res, name, (nb,), [xin, m, ga, gb], [_rows(D), _rows(D), _par(D), _par(D)], [dx1, dh],
                        [_rows(D), _rows(D)], [0, 1, 2, 3], [_sd((t, D)), _sd((t, D), BF), _sd((1, D)), _sd((1, D))],
                        [_rows(D), _rows(D), _par(D), _par(D)], [None, None, (0,), (0,)])

    def mlp_bwd(layer, hin, dmo, sv):
        grads_w2 = matmul(sv['rr'], dmo, 'tn', f"d_mlp_w2_{layer}", BF)
        dr, = matmul(dmo, big[('mlp_w2', layer)], 'nt', f"d_r_{layer}", (BF,), epilogue=act_bwd_epilogue, extras=[sv['r']])
        grads_w1 = matmul(hin, dr, 'tn', f"d_mlp_w1_{layer}", BF)
        dh = matmul(dr, big[('mlp_w1', layer)], 'nt', f"d_h_mlp_{layer}")
        return dh, grads_w1, grads_w2

    def attention_bwd(layer, hin, dao, sv):
        g_wo = matmul(sv['o'], dao, 'tn', f"d_xa_wo_{layer}", BF)
        do = matmul(dao, big[('xa_wo', layer)], 'nt', f"d_o_{layer}", BF)
        grid, qs, ks, vs = attn_specs()
        kvo = pl.BlockSpec((N_MEM, XA_DH), lambda b, h, i: (b, h))
        dq, dk, dv = bwd_call(attn_fn, f"d_attn_{layer}", grid, [sv['q'], sv['kv'], sv['kv']], [qs, ks, vs], [do], [qs],
                              [0, 1, 2], [_sd((t, D), BF), _sd((bsz * N_MEM, D)), _sd((bsz * N_MEM, D))], [qs, kvo, kvo],
                              [None, (2,), (2,)])
        dkv = jnp.concatenate([dk, dv], axis=1)
        g_wkv = matmul(mem2, dkv, 'tn', f"d_xa_wkv_{layer}", BF)
        g_wq = matmul(hin, dq, 'tn', f"d_xa_wq_{layer}", BF)
        dh = matmul(dq, big[('xa_wq', layer)], 'nt', f"d_h_attn_{layer}")
        return dh, g_wq, g_wkv, g_wo

    per_layer = {k: [None, None] for k in ('xa_wq', 'xa_wkv', 'xa_wo', 'mlp_w1', 'mlp_w2')}

    dx5, dmo1, gain_grads[(1, 5)] = bwd_seg_out(x5, mo1, gain(1, 5), dy, "d_out")
    dh6, per_layer['mlp_w1'][1], per_layer['mlp_w2'][1] = mlp_bwd(1, h6, dmo1, sv1)
    dx4, dao1, gain_grads[(1, 3)], gain_grads[(1, 4)] = bwd_seg_res(x4, ao1, gain(1, 3), gain(1, 4), dx5, dh6, "d_res_1b")
    dh5, per_layer['xa_wq'][1], per_layer['xa_wkv'][1], per_layer['xa_wo'][1] = attention_bwd(1, h5, dao1, sv1)
    ex.put_grads('l1', G_L1, {(k, 1): v[1] for k, v in per_layer.items()})
    dx3, dm1, gain_grads[(1, 1)], gain_grads[(1, 2)] = bwd_seg_res(x3, m1, gain(1, 1), gain(1, 2), dx4, dh5, "d_res_1a")
    g_cd_out = matmul(mix1, dm1, 'tn', "d_cd_w_out", BF)
    dmix1 = matmul(dm1, big[('cd_w_out', 0)], 'nt', "d_mix1")
    dvconv, dlg, dlb = bwd_call(seg_ln, "d_conf_ln", (nb,), [vconv, p['conf_ln_g'], p['conf_ln_b']],
                                [_rows(D), _par(D), _par(D)], [dmix1], [_rows(D, 0)], [0, 1, 2],
                                [_sd((t, D)), _sd((1, D)), _sd((1, D))], [_rows(D), _par(D), _par(D)], [None, (0,), (0,)])
    grads['conf_ln_g'], grads['conf_ln_b'] = dlg, dlb
    cd_g = bwd_call(cd1_fn, "d_cd_conv", (nd, bsz), cd_ins, cd_in_specs, [dvconv, dmix1],
                    [cd_out_spec, pl.BlockSpec((seq, LANE), lambda j, b: (b, nd + j))], list(range(8)),
                    [_sd((t, D))] * 5 + [_sd((CONF_K, D)), _sd((1, D)), _sd((SC_K, D))], [cd_out_spec] * 5 + cd_par,
                    [None] * 5 + [(1,), (1,), (1,)])
    du1 = jnp.concatenate(cd_g[:5], axis=1)
    grads['conf_dw_w'], grads['conf_dw_b'], grads['sc_conv_w'] = cd_g[5][None], cd_g[6], cd_g[7][None]
    g_cd_in = matmul(h4, du1, 'tn', "d_cd_w_in", BF)
    ex.put_grads('cd', G_CD, {('cd_w_in', 0): g_cd_in, ('cd_w_out', 0): g_cd_out})
    dh4 = matmul(du1, big[('cd_w_in', 0)], 'nt', "d_h_cd")

    dx2, dmo0, gain_grads[(0, 5)], gain_grads[(1, 0)] = bwd_seg_res(x2, mo0, gain(0, 5), gain(1, 0), dx3, dh4, "d_res_0c")
    dh3, per_layer['mlp_w1'][0], per_layer['mlp_w2'][0] = mlp_bwd(0, h3, dmo0, sv)
    dx1, dao0, gain_grads[(0, 3)], gain_grads[(0, 4)] = bwd_seg_res(x1, ao0, gain(0, 3), gain(0, 4), dx2, dh3, "d_res_0b")
    dh2, per_layer['xa_wq'][0], per_layer['xa_wkv'][0], per_layer['xa_wo'][0] = attention_bwd(0, h2, dao0, sv)
    ex.put_grads('l0', G_L0, {(k, 0): v[0] for k, v in per_layer.items()})
    dx0r, dm0, gain_grads[(0, 1)], gain_grads[(0, 2)] = bwd_seg_res(x0, m0, gain(0, 1), gain(0, 2), dx1, dh2, "d_res_0a")
    g_ab_out = matmul(mix0, dm0, 'tn', "d_ab_w_out", BF)
    dmix0 = matmul(dm0, big[('ab_w_out', 0)], 'nt', "d_mix0")
    dxs, dbm, dcm, dz, ddt, ddtb, dalog, ddsk, dnw = ssd_bwd(xbc_act, u0, dtb, alog, dsk, p['ssm_norm'], consts, hs, dmix0,
                                                             bsz, seq)
    grads['ssm_dt_bias'] = (ddtb[0] + ddtb[1])[:, :SSM_HEADS]
    grads['ssm_a_log'] = (dalog[0] + dalog[1])[:, :SSM_HEADS]
    grads['ssm_d'] = (ddsk[0] + ddsk[1])[:, :SSM_HEADS]
    grads['ssm_norm'] = dnw
    dxbc_act = jnp.concatenate([dxs, dbm, dcm], axis=1)
    dxr, dcw, dcb = bwd_call(conv4_fn, "d_ssm_conv", (ncb, bsz), [u0, p['ssm_conv_w'][0], p['ssm_conv_b']], conv_in_specs,
                             [dxbc_act], [conv_out_spec], [0, 1, 2],
                             [_sd((t, SSM_CONV_DIM)), _sd((SSM_CONV, SSM_CONV_DIM)), _sd((1, SSM_CONV_DIM))],
                             [conv_out_spec, conv_in_specs[1], conv_in_specs[2]], [None, (1,), (1,)])
    grads['ssm_conv_w'], grads['ssm_conv_b'] = dcw[None], dcb
    dpool, dpw, dps = [], [], []
    for g in range(POOL_GROUPS):
        seqspec = pl.BlockSpec((seq, PG), lambda b, g=g: (b, g))
        one = pl.BlockSpec((seq, PG), lambda b: (b, 0))
        wspec = pl.BlockSpec((PG, PG), lambda b: (0, 0))
        sspec = pl.BlockSpec((1, PG), lambda b, g=g: (0, g))
        a, bb, c = bwd_call(make_pool_fn(g), f"d_pool_{g}", (bsz,), [u0, p['pool_w'][0, g], p['pool_scale']],
                            [seqspec, wspec, sspec], [dmix0], [seqspec], [0, 1, 2],
                            [_sd((t, PG)), _sd((PG, PG)), _sd((1, PG))], [one, wspec, pl.BlockSpec((1, PG), lambda b: (0, 0))],
                            [None, (0,), (0,)])
        dpool.append(a)
        dpw.append(bb)
        dps.append(c)
    grads['pool_w'] = jnp.stack(dpw)[None]
    grads['pool_scale'] = jnp.concatenate(dps, axis=1)
    du0 = jnp.concatenate(dpool + [dz, dxr, ddt[0] + ddt[1]], axis=1)
    ex.put_grads('ab', G_AB, {('ab_w_in', 0): matmul(h0, du0, 'tn', "d_ab_w_in", BF)[:, :AB_IN], ('ab_w_out', 0): g_ab_out})
    dh0 = matmul(du0, w_ab_in, 'nt', "d_h_ab")
    dx, dg00 = bwd_call(seg_in_res, "d_norm_in", (nb,), [x0, gain(0, 0)], [_rows(D), _par(D)], [dx0r, dh0],
                        [_rows(D), _rows(D)], [0, 1], [_sd((t, D)), _sd((1, D))], [_rows(D), _par(D)], [None, (0,)])
    gain_grads[(0, 0)] = dg00
    grads['norm_gains'] = jnp.stack([jnp.concatenate([gain_grads[(l, i)] for i in range(6)], axis=0) for l in range(2)])
    return loss, dx, grads
```

```python
import functools
import math

import numpy as np
import jax
import jax.numpy as jnp
from jax import lax
from jax.experimental import pallas as pl
from jax.experimental.pallas import tpu as pltpu

BF = jnp.bfloat16
F32 = jnp.float32
HI = lax.Precision.HIGHEST

N_DEV = 8
D = 1024
N_MEM = 256
XA_HEADS = 4
XA_DH = D // XA_HEADS
POOL_GROUPS = 4
PG = 128
POOL_W = POOL_GROUPS * PG
SSM_INNER = 1024
SSM_GROUPS = 2
SSM_GSZ = SSM_INNER // SSM_GROUPS
SSM_HEADS = 16
SSM_P = 64
SSM_N = 128
SSM_CONV = 4
SSM_CONV_DIM = SSM_INNER + 2 * SSM_GROUPS * SSM_N
CHUNK = 128
AB_IN = POOL_W + SSM_INNER + SSM_CONV_DIM + SSM_HEADS
AB_IN_PAD = POOL_W + SSM_INNER + SSM_CONV_DIM + 128
AB_OUT = POOL_W + SSM_INNER
CONF_K = 31
SC_K = 3
CD_IN = 5 * D
CD_OUT = 2 * D
MLP_H = 4 * D
RMS_EPS = 1e-6
LN_EPS = 1e-5
ADAM_LR = 0.001
ADAM_B1 = 0.9
ADAM_B2 = 0.999
ADAM_EPS = 1e-08
ADAM_WD = 0.01
ADAM_STEP = 10
VMEM_LIMIT = 56 * 1024 * 1024
LANE = 128

NAMES = ['x', 'mem', 'norm_gains', 'xa_wq', 'xa_wkv', 'xa_wo', 'mlp_w1', 'mlp_w2', 'ab_w_in', 'pool_w', 'pool_scale',
         'ssm_conv_w', 'ssm_conv_b', 'ssm_dt_bias', 'ssm_a_log', 'ssm_d', 'ssm_norm', 'ab_w_out', 'cd_w_in', 'conf_dw_w',
         'conf_dw_b', 'conf_ln_g', 'conf_ln_b', 'sc_conv_w', 'cd_w_out', 'loss_target']
WEIGHTS = NAMES[2:25]
BIG = [('xa_wq', 1), ('xa_wkv', 2), ('xa_wo', 1), ('mlp_w1', 2), ('mlp_w2', 1), ('cd_w_in', 2), ('cd_w_out', 1),
       ('ab_w_out', 1), ('ab_w_in', 2)]
SMALL_SHARDED = ['norm_gains', 'ssm_conv_w', 'conf_dw_w', 'conf_dw_b', 'conf_ln_g', 'conf_ln_b', 'sc_conv_w']
REPLICATED = ['pool_w', 'pool_scale', 'ssm_conv_b', 'ssm_dt_bias', 'ssm_a_log', 'ssm_d', 'ssm_norm']


def _dg(a, b, ca, cb, prec=None):
    return lax.dot_general(a, b, (((ca,), (cb,)), ((), ())), precision=prec, preferred_element_type=F32)


@functools.partial(jax.custom_vjp, nondiff_argnums=(2, 3))
def bdot(a, b, ca, cb):
    return _dg(a.astype(BF), b.astype(BF), ca, cb)


def _bdot_fwd(a, b, ca, cb):
    return bdot(a, b, ca, cb), (a, b)


def _bdot_bwd(ca, cb, res, g):
    a, b = res
    g16, a16, b16 = g.astype(BF), a.astype(BF), b.astype(BF)
    da = _dg(g16, b16, 1, 1 - cb) if ca == 1 else _dg(b16, g16, 1 - cb, 1)
    db = _dg(g16, a16, 0, 1 - ca) if cb == 1 else _dg(a16, g16, 1 - ca, 0)
    return da.astype(a.dtype), db.astype(b.dtype)


bdot.defvjp(_bdot_fwd, _bdot_bwd)


def _split3(a):
    a1 = a.astype(BF)
    r1 = a - a1.astype(F32)
    a2 = r1.astype(BF)
    a3 = (r1 - a2.astype(F32)).astype(BF)
    return a1, a2, a3


def _exact_right(a, c):
    m = a.shape[0]
    if m % 16:
        return sum(_dg(p, c, 1, 0) for p in _split3(a))
    o = _dg(jnp.concatenate(_split3(a), axis=0), c, 1, 0)
    return o[:m] + o[m:2 * m] + o[2 * m:]


def _exact_left(c, a):
    n = a.shape[1]
    o = _dg(c, jnp.concatenate(_split3(a), axis=1), 1, 0)
    return o[:, :n] + o[:, n:2 * n] + o[:, 2 * n:]


@jax.custom_vjp
def cmat(a, c, ct):
    return _exact_right(a, c)


def _cmat_fwd(a, c, ct):
    return cmat(a, c, ct), (c, ct)


def _cmat_bwd(res, g):
    c, ct = res
    return _exact_right(g, ct), jnp.zeros_like(c), jnp.zeros_like(ct)


cmat.defvjp(_cmat_fwd, _cmat_bwd)


@jax.custom_vjp
def cmatl(c, ct, a):
    return _exact_left(c, a)


def _cmatl_fwd(c, ct, a):
    return cmatl(c, ct, a), (c, ct)


def _cmatl_bwd(res, g):
    c, ct = res
    return jnp.zeros_like(c), jnp.zeros_like(ct), _exact_left(ct, g)


cmatl.defvjp(_cmatl_fwd, _cmatl_bwd)


SUBLANES = 8


def _taps(x, shifts, down):
    n, c = x.shape
    pad = _round_up(max(shifts), SUBLANES)
    if pad == 0:
        return {0: x}
    zeros = jnp.zeros((pad, c), x.dtype)
    xp = jnp.concatenate([zeros, x] if down else [x, zeros], axis=0)
    rolled, out = {0: xp}, {}
    for s in shifts:
        a, b = divmod(s, SUBLANES)
        if b not in rolled:
            rolled[b] = pltpu.roll(xp, b if down else n + pad - b, 0)
        off = pad - SUBLANES * a if down else SUBLANES * a
        out[s] = rolled[b][off:off + n]
    return out


def _shift_down(x, k):
    return _taps(x, [k], True)[k]


def _shift_up(x, k):
    return _taps(x, [k], False)[k]


@functools.partial(jax.custom_vjp, nondiff_argnums=(1,))
def shift(x, k):
    return _shift_down(x, k)


def _shift_fwd(x, k):
    return _shift_down(x, k), None


def _shift_bwd(k, _, g):
    return (_shift_up(g, k),)


shift.defvjp(_shift_fwd, _shift_bwd)


@functools.partial(jax.custom_vjp, nondiff_argnums=(2,))
def cconv(u, w, width):
    taps = _taps(u, list(range(width)), True)
    acc = u * w[width - 1:width, :]
    for k in range(width - 1):
        acc = acc + taps[width - 1 - k] * w[k:k + 1, :]
    return acc


def _cconv_fwd(u, w, width):
    return cconv(u, w, width), (u, w)


def _cconv_bwd(width, res, g):
    u, w = res
    rows = lax.broadcasted_iota(jnp.int32, w.shape, 0)
    du = g * w[width - 1:width, :]
    dw = jnp.where(rows == width - 1, jnp.sum(g * u, axis=0, keepdims=True), 0.0)
    g_taps = _taps(g, list(range(width)), False)
    u_taps = _taps(u, list(range(width)), True)
    for k in range(width - 1):
        s = width - 1 - k
        du = du + g_taps[s] * w[k:k + 1, :]
        dw = dw + jnp.where(rows == k, jnp.sum(g * u_taps[s], axis=0, keepdims=True), 0.0)
    return du, dw


cconv.defvjp(_cconv_fwd, _cconv_bwd)


def _rms(x, g):
    return x * lax.rsqrt(jnp.mean(x * x, axis=-1, keepdims=True) + RMS_EPS) * g


def _params(sem=None):
    return pltpu.CompilerParams(dimension_semantics=sem, vmem_limit_bytes=VMEM_LIMIT)


def _f32(v):
    return v if v.dtype == F32 else v.astype(F32)


def _first(axes):
    ok = None
    for ax in axes:
        c = pl.program_id(ax) == 0
        ok = c if ok is None else jnp.logical_and(ok, c)
    return ok


def fwd_call(fn, name, grid, ins, in_specs, out_shapes, out_specs):
    n_in = len(ins)

    def body(*refs):
        outs = fn(*[_f32(r[...]) for r in refs[:n_in]])
        for r, o in zip(refs[n_in:], outs):
            r[...] = o.astype(r.dtype)

    return pl.pallas_call(body, name=name, grid=grid, in_specs=in_specs, out_specs=out_specs, out_shape=out_shapes,
                          compiler_params=_params())(*ins)


def bwd_call(fn, name, grid, ins, in_specs, cots, cot_specs, gidx, g_shapes, g_specs, g_acc):
    n_in, n_cot = len(ins), len(cots)

    def body(*refs):
        vals = [_f32(r[...]) for r in refs[:n_in]]

        def f_sel(*dv):
            full = list(vals)
            for i, v in zip(gidx, dv):
                full[i] = v
            return tuple(fn(*full))

        outs, vjp = jax.vjp(f_sel, *[vals[i] for i in gidx])
        cts = tuple(_f32(r[...]) for r in refs[n_in:n_in + n_cot])
        grads = vjp(cts)
        for r, g, acc in zip(refs[n_in + n_cot:], grads, g_acc):
            if acc is None:
                r[...] = g.astype(r.dtype)
            else:
                @pl.when(_first(acc))
                def _():
                    r[...] = jnp.zeros_like(r)

                r[...] += g.astype(r.dtype)

    return pl.pallas_call(body, name=name, grid=grid, in_specs=list(in_specs) + list(cot_specs), out_specs=g_specs,
                          out_shape=g_shapes, compiler_params=_params())(*ins, *cots)


def _tile(dim, pref):
    if dim <= pref:
        return dim
    best = None
    for t in range(LANE, pref + 1, LANE):
        if dim % t == 0:
            best = t
    assert best is not None, dim
    return best


MATMUL_VMEM_BUDGET = 40 * 1024 * 1024


def _matmul_tiles(m, n, k, a_bytes, b_bytes, out_bytes):
    tn = _tile(n, 1024)
    for tk_pref in (k, 2048, 1024, 512):
        tk = _tile(k, tk_pref)
        for tm_pref in (1024, 512, 256):
            tm = _tile(m, tm_pref)
            need = 2 * (tm * tk * a_bytes + tk * tn * b_bytes + tm * tn * out_bytes) + (0 if tk == k else tm * tn * 4)
            need += (tm * tk * 2 if a_bytes == 4 else 0) + (tk * tn * 2 if b_bytes == 4 else 0)
            if need <= MATMUL_VMEM_BUDGET:
                return tm, tn, tk
    raise ValueError((m, n, k))


def matmul(a, b, mode, name, out_dtype=F32, epilogue=None, extras=()):
    if mode == 'nn':
        (m, k), (k2, n) = a.shape, b.shape
    elif mode == 'nt':
        (m, k), (n, k2) = a.shape, b.shape
    else:
        (k, m), (k2, n) = a.shape, b.shape
    assert k == k2, (name, a.shape, b.shape)
    n_extra = len(extras)
    out_dtypes = out_dtype if isinstance(out_dtype, tuple) else (out_dtype,)
    per_out = sum(jnp.dtype(dt).itemsize for dt in out_dtypes) + sum(e.dtype.itemsize for e in extras)
    tm, tn, tk = _matmul_tiles(m, n, k, a.dtype.itemsize, b.dtype.itemsize, per_out)
    nk = k // tk
    ca = 0 if mode == 'tn' else 1
    cb = 1 if mode == 'nt' else 0
    a_spec = pl.BlockSpec((tk, tm), lambda i, j, kk: (kk, i)) if mode == 'tn' else pl.BlockSpec((tm, tk), lambda i, j, kk: (i, kk))
    b_spec = pl.BlockSpec((tn, tk), lambda i, j, kk: (j, kk)) if mode == 'nt' else pl.BlockSpec((tk, tn), lambda i, j, kk: (kk, j))

    def finish(o_refs, extra_refs, acc):
        outs = (acc,) if epilogue is None else epilogue(acc, *[_f32(e[...]) for e in extra_refs])
        for o_ref, o in zip(o_refs, outs):
            o_ref[...] = o.astype(o_ref.dtype)

    def body_whole_k(a_ref, b_ref, *refs):
        finish(refs[n_extra:], refs[:n_extra], _dg(a_ref[...].astype(BF), b_ref[...].astype(BF), ca, cb))

    def body_split_k(a_ref, b_ref, *refs):
        extra_refs, o_refs, acc = refs[:n_extra], refs[n_extra:-1], refs[-1]
        kk = pl.program_id(2)

        @pl.when(kk == 0)
        def _():
            acc[...] = jnp.zeros_like(acc)

        acc[...] += _dg(a_ref[...].astype(BF), b_ref[...].astype(BF), ca, cb)

        @pl.when(kk == nk - 1)
        def _():
            finish(o_refs, extra_refs, acc[...])

    tile = pl.BlockSpec((tm, tn), lambda i, j, kk: (i, j))
    outs = pl.pallas_call(
        body_whole_k if nk == 1 else body_split_k, name=name, grid=(m // tm, n // tn, nk),
        in_specs=[a_spec, b_spec] + [tile] * n_extra, out_specs=[tile] * len(out_dtypes),
        out_shape=[jax.ShapeDtypeStruct((m, n), dt) for dt in out_dtypes],
        scratch_shapes=[] if nk == 1 else [pltpu.VMEM((tm, tn), F32)],
        compiler_params=_params(("parallel", "parallel", "arbitrary")))(a, b, *extras)
    return outs if isinstance(out_dtype, tuple) else outs[0]


_FLIPS = [(0, 0, 1), (1, 0, 0), (0, 1, 0), (1, 1, 0), (1, 0, 1), (0, 1, 1), (1, 1, 1)]


def _me():
    return lax.axis_index("x"), lax.axis_index("y"), lax.axis_index("c")


def _flip(pos, f):
    return tuple(jnp.where(fi == 1, 1 - p, p) if fi else p for p, fi in zip(pos, f))


def _slot(pos):
    return 4 * pos[0] + 2 * pos[1] + pos[2]


def all_gather(v, name):
    def body(v_ref, out_ref, send_sems, recv_sems, local_sem):
        me = _me()
        sibling = _flip(me, (0, 0, 1))
        chips = [_flip(me, f) for f in ((1, 0, 0), (0, 1, 0), (1, 1, 0))]

        def copy(k, block, to, src=None):
            return pltpu.make_async_remote_copy(
                src_ref=out_ref.at[_slot(block)] if src is None else src, dst_ref=out_ref.at[_slot(block)],
                send_sem=send_sems.at[k], recv_sem=recv_sems.at[k], device_id=to, device_id_type=pl.DeviceIdType.MESH)

        mine = pltpu.make_async_copy(v_ref, out_ref.at[_slot(me)], local_sem)
        mine.start()
        first = [copy(0, me, sibling, src=v_ref)] + [copy(1 + j, me, chip, src=v_ref) for j, chip in enumerate(chips)]
        for cp in first:
            cp.start()
        passed = [copy(4 + j, chip, sibling) for j, chip in enumerate(chips)]
        for j, chip in enumerate(chips):
            copy(1 + j, chip, me).wait_recv()
            passed[j].start()
        copy(0, sibling, me).wait_recv()
        for j, chip in enumerate(chips):
            copy(4 + j, _flip(chip, (0, 0, 1)), me).wait_recv()
        for cp in first + passed:
            cp.wait_send()
        mine.wait()

    return pl.pallas_call(
        body, name=name, out_shape=jax.ShapeDtypeStruct((N_DEV,) + v.shape, v.dtype),
        in_specs=[pl.BlockSpec(memory_space=pl.ANY)], out_specs=pl.BlockSpec(memory_space=pl.ANY),
        scratch_shapes=[pltpu.SemaphoreType.DMA((7,)), pltpu.SemaphoreType.DMA((7,)), pltpu.SemaphoreType.DMA(())],
    )(v)


def all_to_all(v, name):
    def body(v_ref, out_ref, send_sems, recv_sems, local_sem):
        me = _me()
        mine = pltpu.make_async_copy(v_ref.at[_slot(me)], out_ref.at[_slot(me)], local_sem)
        mine.start()
        copies = []
        for k, f in enumerate(_FLIPS):
            peer = _flip(me, f)
            cp = pltpu.make_async_remote_copy(
                src_ref=v_ref.at[_slot(peer)], dst_ref=out_ref.at[_slot(me)], send_sem=send_sems.at[k],
                recv_sem=recv_sems.at[k], device_id=peer, device_id_type=pl.DeviceIdType.MESH)
            cp.start()
            copies.append(cp)
        for k, f in enumerate(_FLIPS):
            peer = _flip(me, f)
            pltpu.make_async_remote_copy(
                src_ref=v_ref.at[_slot(peer)], dst_ref=out_ref.at[_slot(peer)], send_sem=send_sems.at[k],
                recv_sem=recv_sems.at[k], device_id=peer, device_id_type=pl.DeviceIdType.MESH).wait_recv()
        for cp in copies:
            cp.wait_send()
        mine.wait()

    return pl.pallas_call(
        body, name=name, out_shape=jax.ShapeDtypeStruct(v.shape, v.dtype),
        in_specs=[pl.BlockSpec(memory_space=pl.ANY)], out_specs=pl.BlockSpec(memory_space=pl.ANY),
        scratch_shapes=[pltpu.SemaphoreType.DMA((7,)), pltpu.SemaphoreType.DMA((7,)), pltpu.SemaphoreType.DMA(())],
    )(v)


def sum_slots(v, name, tr=256):
    _, r, c = v.shape
    tr = _tile_rows(r, tr)

    def body(v_ref, o_ref):
        acc = v_ref[0].astype(F32)
        for s in range(1, N_DEV):
            acc = acc + v_ref[s].astype(F32)
        o_ref[...] = acc

    return pl.pallas_call(body, name=name, grid=(r // tr,), in_specs=[pl.BlockSpec((N_DEV, tr, c), lambda i: (0, i, 0))],
                          out_specs=pl.BlockSpec((tr, c), lambda i: (i, 0)), out_shape=jax.ShapeDtypeStruct((r, c), F32),
                          compiler_params=_params())(v)


def _tile_rows(r, pref):
    if r <= pref:
        return r
    best = None
    for t in range(8, pref + 1, 8):
        if r % t == 0:
            best = t
    return r if best is None else best


def adamw(w, m, v, g, name):
    r, c = w.shape
    tr = _tile_rows(r, 512 if c <= 1024 else 128)

    def body(w_ref, m_ref, v_ref, g_ref, d_ref, nm_ref, nv_ref):
        gg = g_ref[...]
        nm = ADAM_B1 * m_ref[...] + (1.0 - ADAM_B1) * gg
        nv = ADAM_B2 * v_ref[...] + (1.0 - ADAM_B2) * jnp.square(gg)
        m_hat = nm / (1.0 - ADAM_B1 ** ADAM_STEP)
        v_hat = nv / (1.0 - ADAM_B2 ** ADAM_STEP)
        d_ref[...] = -ADAM_LR * (m_hat / (jnp.sqrt(v_hat) + ADAM_EPS) + ADAM_WD * w_ref[...])
        nm_ref[...] = nm
        nv_ref[...] = nv

    spec = pl.BlockSpec((tr, c), lambda i: (i, 0))
    sh = jax.ShapeDtypeStruct((r, c), F32)
    return pl.pallas_call(body, name=name, grid=(r // tr,), in_specs=[spec] * 4, out_specs=[spec] * 3,
                          out_shape=[sh] * 3, compiler_params=_params())(w, m, v, g)


def seg_in(x, g):
    return (_rms(x, g),)


def seg_in_res(x, g):
    return x, _rms(x, g)


def seg_res(x, m, ga, gb):
    x1 = x + _rms(m, ga)
    return x1, _rms(x1, gb)


def seg_out(x, m, ga):
    return (x + _rms(m, ga),)


def act_epilogue(r):
    t = jnp.maximum(r, 0.0)
    return r, t * t


def act_bwd_epilogue(drr, r):
    return (drr * (2.0 * jnp.maximum(r, 0.0)),)


def seg_ln(v, g, b):
    mu = jnp.mean(v, axis=-1, keepdims=True)
    var = jnp.mean(jnp.square(v - mu), axis=-1, keepdims=True)
    vn = (v - mu) * lax.rsqrt(var + LN_EPS) * g + b
    return (jax.nn.silu(vn),)


def make_pool_fn(group):
    window = 2 ** (group + 1)

    def pool_fn(ug, pw, scale):
        s = ug
        for lvl in range(group + 1):
            s = s + shift(s, 2 ** lvl)
        cnt = jnp.minimum(lax.broadcasted_iota(jnp.int32, ug.shape, 0) + 1, window).astype(F32)
        return (bdot(s / cnt - ug, pw, 1, 0) * scale,)

    return pool_fn


def conv4_fn(xr, w, b):
    return (jax.nn.silu(cconv(xr, w, SSM_CONV) + b),)


def cd1_fn(val, gate, bg, cg, hh, dww, dwb, scw):
    v = val * jax.nn.sigmoid(gate)
    vc = cconv(v, dww, CONF_K) + dwb
    sc = bg * cconv(cg * hh, scw, SC_K)
    return vc, sc


def attn_fn(q, k, v):
    s = bdot(q, k, 1, 1) / math.sqrt(XA_DH)
    p = jax.nn.softmax(s, axis=-1)
    return (bdot(p, v, 1, 0),)


def ssd_chunk(xs, bm, cm, z, dtraw, dtb, alog, dsk, nw, h0, h1, h2, h3, e64, e64t, ecat, ecatt, tril, trilt):
    hin = (h0, h1, h2, h3)
    dt = jax.nn.softplus(dtraw + dtb)
    a = -jnp.exp(alog)
    d_a = dt * a
    cs = cmatl(tril, trilt, d_a)
    cs_cat = cmat(cs, ecat, ecatt)
    cs64, cs128 = cs_cat[:, :SSM_GSZ], cs_cat[:, SSM_GSZ:]
    dt64 = cmat(dt, e64, e64t)
    row = lax.broadcasted_iota(jnp.int32, (8, LANE), 0)
    heads = jnp.where(row == 0, dsk, jnp.where(row == 1, jnp.sum(d_a, axis=0, keepdims=True), 0.0))
    heads64 = cmat(heads, e64, e64t)
    d64, tot64 = heads64[0:1, :], heads64[1:2, :]
    xdt = xs * dt64
    cb = bdot(cm, bm, 1, 1)
    li = lax.broadcasted_iota(jnp.int32, (CHUNK, CHUNK), 0)
    si = lax.broadcasted_iota(jnp.int32, (CHUNK, CHUNK), 1)
    causal = li >= si
    lane = lax.broadcasted_iota(jnp.int32, (CHUNK, LANE), 1)
    xw = xdt * jnp.exp(tot64 - cs64)
    ecs = jnp.exp(cs64)
    etot = jnp.exp(tot64)
    ycols, hout = [], []
    for j in range(4):
        sl = slice(j * LANE, (j + 1) * LANE)
        xj = xdt[:, sl]
        ys = []
        for hh in range(2):
            r = 2 * j + hh
            col = cs128[:, r * LANE:(r + 1) * LANE]
            decay = jnp.exp(jnp.where(causal, col - col.T, -1e30))
            ys.append(bdot(cb * decay, xj, 1, 0))
        y_diag = jnp.where(lane < SSM_P, ys[0], ys[1])
        y_off = bdot(cm, hin[j], 1, 0) * ecs[:, sl]
        ycols.append(y_diag + y_off)
        hout.append(etot[:, sl] * hin[j] + bdot(bm, xw[:, sl], 0, 0))
    y = jnp.concatenate(ycols, axis=1) + d64 * xs
    y = y * jax.nn.silu(z)
    yn = y * lax.rsqrt(jnp.mean(y * y, axis=-1, keepdims=True) + RMS_EPS) * nw
    return (yn,) + tuple(hout)


def _ssd_consts():
    h = np.arange(LANE)[:, None]
    e64 = np.stack([(h == g * 8 + np.arange(SSM_GSZ)[None, :] // SSM_P) for g in range(SSM_GROUPS)]).astype(np.float32)
    e128 = np.stack([(h == g * 8 + np.arange(8 * LANE)[None, :] // LANE) for g in range(SSM_GROUPS)]).astype(np.float32)
    ecat = np.concatenate([e64, e128], axis=2)
    tril = np.tril(np.ones((CHUNK, CHUNK), np.float32))
    return tuple(jnp.asarray(c, dtype=BF) for c in (e64, e64.transpose(0, 2, 1), ecat, ecat.transpose(0, 2, 1), tril, tril.T))


def _ssd_specs(nc, rev):
    def ci(c):
        return nc - 1 - c if rev else c

    def row(width, col):
        return pl.BlockSpec((CHUNK, width), lambda g, b, c: (b * nc + ci(c), col(g)))

    data = [row(SSM_GSZ, lambda g: g), row(SSM_N, lambda g: 8 + g), row(SSM_N, lambda g: 10 + g),
            row(SSM_GSZ, lambda g: 1 + g), row(LANE, lambda g: 24)]
    par = [pl.BlockSpec((1, LANE), lambda g, b, c: (0, 0))] * 3 + [pl.BlockSpec((1, SSM_GSZ), lambda g, b, c: (0, g))]
    cst = [pl.BlockSpec((None, LANE, SSM_GSZ), lambda g, b, c: (g, 0, 0)), pl.BlockSpec((None, SSM_GSZ, LANE), lambda g, b, c: (g, 0, 0)),
           pl.BlockSpec((None, LANE, 12 * LANE), lambda g, b, c: (g, 0, 0)), pl.BlockSpec((None, 12 * LANE, LANE), lambda g, b, c: (g, 0, 0)),
           pl.BlockSpec((CHUNK, CHUNK), lambda g, b, c: (0, 0)), pl.BlockSpec((CHUNK, CHUNK), lambda g, b, c: (0, 0))]
    hsave = pl.BlockSpec((None, None, None, 4, SSM_N, LANE), lambda g, b, c: (g, b, ci(c), 0, 0, 0))
    yn = row(SSM_GSZ, lambda g: g)
    return data, par, cst, hsave, yn, row


def ssd_fwd(xbc_act, u, dtb, alog, dsk, nw, consts, bsz, seq):
    nc = seq // CHUNK
    data, par, cst, hsave, yn_spec, _ = _ssd_specs(nc, False)

    def body(xs, bm, cm, z, dtr, dtb_r, alog_r, dsk_r, nw_r, e64, e64t, ecat, ecatt, tril, trilt, yn_ref, hs_ref, h):
        @pl.when(pl.program_id(2) == 0)
        def _():
            h[...] = jnp.zeros_like(h)

        hs_ref[...] = h[...]
        outs = ssd_chunk(xs[...], bm[...], cm[...], z[...], dtr[...], dtb_r[...], alog_r[...], dsk_r[...], nw_r[...],
                         h[0], h[1], h[2], h[3], e64[...], e64t[...], ecat[...], ecatt[...], tril[...], trilt[...])
        yn_ref[...] = outs[0].astype(yn_ref.dtype)
        for j in range(4):
            h[j] = outs[1 + j]

    t = bsz * seq
    return pl.pallas_call(
        body, name="ssd_fwd", grid=(SSM_GROUPS, bsz, nc), in_specs=data + par + cst, out_specs=[yn_spec, hsave],
        out_shape=[jax.ShapeDtypeStruct((t, SSM_INNER), BF), jax.ShapeDtypeStruct((SSM_GROUPS, bsz, nc, 4, SSM_N, LANE), F32)],
        scratch_shapes=[pltpu.VMEM((4, SSM_N, LANE), F32)], compiler_params=_params(),
    )(xbc_act, xbc_act, xbc_act, u, u, dtb, alog, dsk, nw, *consts)


def ssd_bwd(xbc_act, u, dtb, alog, dsk, nw, consts, hs, dmix, bsz, seq):
    nc = seq // CHUNK
    data, par, cst, hsave, _, row = _ssd_specs(nc, True)
    t = bsz * seq
    dyn_spec = row(SSM_GSZ, lambda g: POOL_W // SSM_GSZ + g)

    def body(xs, bm, cm, z, dtr, dtb_r, alog_r, dsk_r, nw_r, e64, e64t, ecat, ecatt, tril, trilt, hs_ref, dyn_ref,
             dxs, dbm, dcm, dz, ddt, ddtb, dalog, ddsk, dnw, dh):
        @pl.when(pl.program_id(2) == 0)
        def _():
            dh[...] = jnp.zeros_like(dh)

        cst_vals = (e64[...], e64t[...], ecat[...], ecatt[...], tril[...], trilt[...])

        def f(*args):
            return ssd_chunk(*args, *cst_vals)

        prim = (xs[...], bm[...], cm[...], z[...], dtr[...], dtb_r[...], alog_r[...], dsk_r[...], nw_r[...],
                hs_ref[0], hs_ref[1], hs_ref[2], hs_ref[3])
        _, vjp = jax.vjp(f, *prim)
        g = vjp((dyn_ref[...].astype(F32), dh[0], dh[1], dh[2], dh[3]))
        dxs[...] = g[0]
        dbm[...] = g[1]
        dcm[...] = g[2]
        dz[...] = g[3].astype(dz.dtype)
        ddt[...] = g[4]

        @pl.when(_first((1, 2)))
        def _():
            for r in (ddtb, dalog, ddsk, dnw):
                r[...] = jnp.zeros_like(r)

        ddtb[...] += g[5]
        dalog[...] += g[6]
        ddsk[...] += g[7]
        dnw[...] += g[8]
        for j in range(4):
            dh[j] = g[9 + j]

    gpar = pl.BlockSpec((None, 1, LANE), lambda g, b, c: (g, 0, 0))
    out_specs = [row(SSM_GSZ, lambda g: g), row(SSM_N, lambda g: g), row(SSM_N, lambda g: g), row(SSM_GSZ, lambda g: g),
                 pl.BlockSpec((None, CHUNK, LANE), lambda g, b, c: (g, b * nc + nc - 1 - c, 0)),
                 gpar, gpar, gpar, pl.BlockSpec((1, SSM_GSZ), lambda g, b, c: (0, g))]
    gp = jax.ShapeDtypeStruct((SSM_GROUPS, 1, LANE), F32)
    out_shape = [jax.ShapeDtypeStruct((t, SSM_INNER), F32), jax.ShapeDtypeStruct((t, SSM_GROUPS * SSM_N), F32),
                 jax.ShapeDtypeStruct((t, SSM_GROUPS * SSM_N), F32), jax.ShapeDtypeStruct((t, SSM_INNER), BF),
                 jax.ShapeDtypeStruct((SSM_GROUPS, t, LANE), F32), gp, gp, gp, jax.ShapeDtypeStruct((1, SSM_INNER), F32)]
    return pl.pallas_call(
        body, name="ssd_bwd", grid=(SSM_GROUPS, bsz, nc), in_specs=data + par + cst + [hsave, dyn_spec], out_specs=out_specs,
        out_shape=out_shape, scratch_shapes=[pltpu.VMEM((4, SSM_N, LANE), F32)], compiler_params=_params(),
    )(xbc_act, xbc_act, xbc_act, u, u, dtb, alog, dsk, nw, *consts, hs, dmix)


TB = 512


def _rows(d, col=0):
    return pl.BlockSpec((TB, d), lambda i: (i, col))


def _par(d):
    return pl.BlockSpec((1, d), lambda i: (0, 0))


def _sd(shape, dtype=F32):
    return jax.ShapeDtypeStruct(shape, dtype)


def _round_up(n, m):
    return -(-n // m) * m


def _pad_rows(a, rows):
    return jnp.pad(a, ((0, rows - a.shape[0]), (0, 0)))


def _pack128(arrs):
    flat = jnp.concatenate([a.reshape(-1) for a in arrs])
    n = flat.shape[0]
    rows = -(-n // (8 * LANE)) * 8
    return jnp.pad(flat, (0, rows * LANE - n)).reshape(rows, LANE)


def _unpack128(packed, shapes):
    flat = packed.reshape(-1)
    out, off = [], 0
    for s in shapes:
        n = int(np.prod(s))
        out.append(flat[off:off + n].reshape(s))
        off += n
    return out


def kernel(x, mem, norm_gains, xa_wq, xa_wkv, xa_wo, mlp_w1, mlp_w2, ab_w_in, pool_w, pool_scale, ssm_conv_w, ssm_conv_b, ssm_dt_bias, ssm_a_log, ssm_d, ssm_norm, ab_w_out, cd_w_in, conf_dw_w, conf_dw_b, conf_ln_g, conf_ln_b, sc_conv_w, cd_w_out, loss_target, m_norm_gains, m_xa_wq, m_xa_wkv, m_xa_wo, m_mlp_w1, m_mlp_w2, m_ab_w_in, m_pool_w, m_pool_scale, m_ssm_conv_w, m_ssm_conv_b, m_ssm_dt_bias, m_ssm_a_log, m_ssm_d, m_ssm_norm, m_ab_w_out, m_cd_w_in, m_conf_dw_w, m_conf_dw_b, m_conf_ln_g, m_conf_ln_b, m_sc_conv_w, m_cd_w_out, v_norm_gains, v_xa_wq, v_xa_wkv, v_xa_wo, v_mlp_w1, v_mlp_w2, v_ab_w_in, v_pool_w, v_pool_scale, v_ssm_conv_w, v_ssm_conv_b, v_ssm_dt_bias, v_ssm_a_log, v_ssm_d, v_ssm_norm, v_ab_w_out, v_cd_w_in, v_conf_dw_w, v_conf_dw_b, v_conf_ln_g, v_conf_ln_b, v_sc_conv_w, v_cd_w_out):
    args = locals()
    w = {n: args[n] for n in WEIGHTS}
    mom_m = {n: args["m_" + n] for n in WEIGHTS}
    mom_v = {n: args["v_" + n] for n in WEIGHTS}
    ex = Exchange(w)
    loss_local, grad_x, small_grads = local_step(x, mem, loss_target, ex)
    loss = lax.psum(loss_local, ("x", "y", "c"))
    outs = {}

    def update_big(names, own):
        last = None
        for n in names:
            shp = w[n].shape
            view = (-1, shp[-1])
            g = jnp.stack([own[(n, layer)] for layer in range(shp[0])])
            d, nm, nv = adamw(w[n].reshape(view), mom_m[n].reshape(view), mom_v[n].reshape(view), g.reshape(view), "adamw_" + n)
            outs[n] = (g, d.reshape(shp), nm.reshape(shp), nv.reshape(shp))
            last = d
        return last

    own = {}
    for key in ('l1', 'cd', 'l0'):
        own.update(ex.reduced(key, grad_x))
    late = update_big(['xa_wq', 'xa_wkv', 'xa_wo', 'mlp_w1', 'mlp_w2', 'cd_w_in', 'cd_w_out'], own)
    g_own = ex.reduced_small(small_grads)
    update_big(['ab_w_in', 'ab_w_out'], ex.reduced('ab', late))
    small = SMALL_SHARDED + REPLICATED
    shapes = [w[n].shape for n in small]
    d, nm, nv = adamw(_pack128([w[n] for n in small]), _pack128([mom_m[n] for n in small]), _pack128([mom_v[n] for n in small]),
                      _pack128([g_own[n] for n in small]), "adamw_small")
    for n, dd, mm, vv in zip(small, _unpack128(d, shapes), _unpack128(nm, shapes), _unpack128(nv, shapes)):
        outs[n] = (g_own[n], dd, mm, vv)
    return (loss, grad_x.reshape(x.shape), *[outs[n][0] for n in WEIGHTS], *[outs[n][1] for n in WEIGHTS],
            *[outs[n][2] for n in WEIGHTS], *[outs[n][3] for n in WEIGHTS])


G_AB = (('ab_w_in', 0), ('ab_w_out', 0))
G_L0 = (('xa_wq', 0), ('xa_wkv', 0), ('xa_wo', 0), ('mlp_w1', 0), ('mlp_w2', 0))
G_L1 = (('xa_wq', 1), ('xa_wkv', 1), ('xa_wo', 1), ('mlp_w1', 1), ('mlp_w2', 1))
G_CD = (('cd_w_in', 0), ('cd_w_out', 0))
GATHER_CHAIN = {'l0': ('cd', G_CD), 'cd': ('l1', G_L1)}
SHARD_AXIS = dict(BIG)
MEMBER_ROW_TILE = 64
FLAT_ROW_TILE = 128


def _members(group, w):
    out = []
    for n, layer in group:
        shp = w[n].shape[1:]
        rows = shp[0] * shp[1] // D
        out.append((n, layer, shp, rows, _round_up(rows, MEMBER_ROW_TILE)))
    return out


def _group_rows(group, w):
    return _round_up(sum(m[4] for m in _members(group, w)), FLAT_ROW_TILE)


def _flat_shards(group, w):
    parts = [_pad_rows(w[n][layer].astype(BF).reshape(-1, D), padded) for n, layer, _, _, padded in _members(group, w)]
    flat = jnp.concatenate(parts, axis=0)
    return _pad_rows(flat, _group_rows(group, w))


def _full_from_slots(land, group, w):
    out, off = {}, 0
    for n, layer, shp, rows, padded in _members(group, w):
        blk = land[:, off:off + rows]
        off += padded
        if SHARD_AXIS[n] == 1:
            out[(n, layer)] = blk.reshape(N_DEV * shp[0], shp[1])
        else:
            out[(n, layer)] = blk.reshape(N_DEV, shp[0], shp[1]).transpose(1, 0, 2).reshape(shp[0], N_DEV * shp[1])
    return out


def _slots_from_full(grads, group, w):
    parts = []
    for n, layer, shp, rows, padded in _members(group, w):
        g = grads[(n, layer)].astype(BF)
        if SHARD_AXIS[n] == 1:
            blk = g.reshape(N_DEV, rows, D)
        else:
            blk = g.reshape(shp[0], N_DEV, shp[1]).transpose(1, 0, 2).reshape(N_DEV, rows, D)
        parts.append(jnp.pad(blk, ((0, 0), (0, padded - rows), (0, 0))))
    send = jnp.concatenate(parts, axis=1)
    return jnp.pad(send, ((0, 0), (0, _group_rows(group, w) - send.shape[1]), (0, 0)))


def _own_from_sum(summed, group, w):
    out, off = {}, 0
    for n, layer, shp, rows, padded in _members(group, w):
        out[(n, layer)] = summed[off:off + rows].reshape(shp)
        off += padded
    return out


_HBM = pl.BlockSpec(memory_space=pltpu.HBM)
_SEM = pl.BlockSpec(memory_space=pltpu.SEMAPHORE)
_ANY = pl.BlockSpec(memory_space=pl.ANY)


def _peer_copy(k, src, dst, send_sems, recv_sems, peer):
    return pltpu.make_async_remote_copy(src_ref=src, dst_ref=dst, send_sem=send_sems.at[k], recv_sem=recv_sems.at[k],
                                        device_id=peer, device_id_type=pl.DeviceIdType.MESH)


def exchange_start(src, name, scatter):
    shape = src.shape[-2:]

    def body(src_ref, land_ref, send_sems, recv_sems, src_thru, land_thru, token):
        me = _me()
        for k, f in enumerate(_FLIPS):
            peer = _flip(me, f)
            piece = src_ref.at[_slot(peer)] if scatter else src_ref
            _peer_copy(k, piece, land_ref.at[_slot(me)], send_sems, recv_sems, peer).start()
        token[...] = jnp.zeros_like(token)

    land = pltpu.with_memory_space_constraint(lax.empty((N_DEV,) + shape, src.dtype), pltpu.HBM)
    return pl.pallas_call(
        body, name=name,
        out_shape=(pltpu.SemaphoreType.DMA((7,)), pltpu.SemaphoreType.DMA((7,)), pltpu.HBM(src.shape, src.dtype),
                   pltpu.HBM((N_DEV,) + shape, src.dtype), jax.ShapeDtypeStruct((8, LANE), F32)),
        in_specs=(_HBM, _HBM), out_specs=(_SEM, _SEM, _HBM, _HBM, pl.BlockSpec(memory_space=pltpu.VMEM)),
        input_output_aliases={0: 2, 1: 3},
        compiler_params=pltpu.CompilerParams(has_side_effects=pltpu.SideEffectType.DATAFLOW_SIDE_EFFECTING),
    )(pltpu.with_memory_space_constraint(src, pltpu.HBM), land)


def exchange_wait(handles, after, name, scatter):
    send_sems, recv_sems, src_thru, land_thru, _ = handles

    def body(src_ref, land_ref, send_sems, recv_sems, after_ref, src_dead, got_ref, token):
        me = _me()
        for k, f in enumerate(_FLIPS):
            peer = _flip(me, f)
            piece = src_ref.at[_slot(peer)] if scatter else src_ref
            cp = _peer_copy(k, piece, land_ref.at[_slot(peer)], send_sems, recv_sems, peer)
            cp.wait_send()
            cp.wait_recv()
        token[...] = jnp.zeros_like(token)

    return pl.pallas_call(
        body, name=name, out_shape=(pltpu.HBM(src_thru.shape, src_thru.dtype), pltpu.HBM(land_thru.shape, land_thru.dtype),
                                    jax.ShapeDtypeStruct((8, LANE), F32)),
        in_specs=(_HBM, _HBM, _SEM, _SEM, _ANY), out_specs=(_HBM, _HBM, pl.BlockSpec(memory_space=pltpu.VMEM)),
        input_output_aliases={0: 0, 1: 1},
        compiler_params=pltpu.CompilerParams(has_side_effects=pltpu.SideEffectType.DATAFLOW_SIDE_EFFECTING),
    )(src_thru, land_thru, send_sems, recv_sems, after)


class Exchange:
    def __init__(self, w):
        self.w = w
        self.me = _slot(_me())
        shapes = [w[n].shape for n in SMALL_SHARDED]
        gs = all_gather(_pack128([w[n] for n in SMALL_SHARDED]), "gather_small")
        per_dev = [_unpack128(gs[d], shapes) for d in range(N_DEV)]
        self.small = {n: jnp.concatenate([per_dev[d][i] for d in range(N_DEV)], axis=-1) for i, n in enumerate(SMALL_SHARDED)}
        self.small.update({n: w[n] for n in REPLICATED})
        self.now = _full_from_slots(all_gather(_flat_shards(G_AB, w), "gather_ab"), G_AB, w)
        self.gathers = {'l0': (G_L0, exchange_start(_flat_shards(G_L0, w), "gather_l0_start", False))}
        self.tokens = [self.gathers['l0'][1][4]]
        self.reductions = {}

    def take_tokens(self):
        toks, self.tokens = self.tokens, []
        return toks

    def weights(self, key, after):
        if key == 'ab':
            return self.now
        group, handles = self.gathers[key]
        _, land, done = exchange_wait(handles, after, f"gather_{key}_wait", False)
        nxt = GATHER_CHAIN.get(key)
        if nxt is not None:
            src = _flat_shards(nxt[1], self.w) + done[0, 0].astype(BF)
            self.gathers[nxt[0]] = (nxt[1], exchange_start(src, f"gather_{nxt[0]}_start", False))
            self.tokens.append(self.gathers[nxt[0]][1][4])
        land = lax.dynamic_update_slice(land, handles[2][None], (self.me, 0, 0))
        return _full_from_slots(land, group, self.w)

    def put_grads(self, key, group, grads):
        send = _slots_from_full(grads, group, self.w)
        handles = exchange_start(send, f"reduce_{key}_start", True)
        self.reductions[key] = (group, handles)
        self.tokens.append(handles[4])

    def reduced(self, key, after):
        group, handles = self.reductions[key]
        send, land, _ = exchange_wait(handles, after, f"reduce_{key}_wait", True)
        mine = lax.dynamic_slice_in_dim(send, self.me, 1, axis=0)
        land = lax.dynamic_update_slice(land, mine, (self.me, 0, 0))
        return _own_from_sum(sum_slots(land, f"sum_{key}", FLAT_ROW_TILE), group, self.w)

    def reduced_small(self, small_grads):
        small = SMALL_SHARDED + REPLICATED
        gs = all_gather(_pack128([small_grads[n] for n in small]), "gather_small_grads")
        tot = _unpack128(sum_slots(gs, "sum_small", 1024), [small_grads[n].shape for n in small])
        out = {}
        for n, g in zip(small, tot):
            if n in SMALL_SHARDED:
                width = self.w[n].shape[-1]
                g = lax.dynamic_slice_in_dim(g, self.me * width, width, axis=g.ndim - 1)
            out[n] = g
        return out


def local_step(x, mem, target, ex):
    bsz, seq, _ = x.shape
    t = bsz * seq
    nb = t // TB
    nc = seq // CHUNK
    x0 = x.reshape(t, D)
    mem2 = mem.reshape(bsz * N_MEM, D)
    tgt = target.reshape(t, D)
    p = ex.small
    gains = p['norm_gains']
    big = dict(ex.weights('ab', None))

    def gain(layer, i):
        g = gains[layer, i].reshape(1, D)
        for tok in ex.take_tokens():
            g = g + tok[0, 0]
        return g

    consts = _ssd_consts()
    grads = {}
    saved = [dict(), dict()]

    def run_seg_res(xin, m, ga, gb, name):
        return fwd_call(seg_res, name, (nb,), [xin, m, ga, gb], [_rows(D), _rows(D), _par(D), _par(D)],
                        [_sd((t, D)), _sd((t, D), BF)], [_rows(D), _rows(D)])

    def attn_specs():
        nq = seq // TB
        q = pl.BlockSpec((TB, XA_DH), lambda b, h, i: (b * nq + i, h))
        k = pl.BlockSpec((N_MEM, XA_DH), lambda b, h, i: (b, h))
        v = pl.BlockSpec((N_MEM, XA_DH), lambda b, h, i: (b, XA_HEADS + h))
        return (bsz, XA_HEADS, nq), q, k, v

    def attention_fwd(layer, xin, hin, sv):
        q = matmul(hin, big[('xa_wq', layer)], 'nn', f"q_{layer}", BF)
        kv = matmul(mem2, big[('xa_wkv', layer)], 'nn', f"kv_{layer}", BF)
        grid, qs, ks, vs = attn_specs()
        o, = fwd_call(attn_fn, f"attn_{layer}", grid, [q, kv, kv], [qs, ks, vs], [_sd((t, D), BF)], [qs])
        ao = matmul(o, big[('xa_wo', layer)], 'nn', f"ao_{layer}")
        sv.update(q=q, kv=kv, o=o, ao=ao)
        return ao

    def mlp_fwd(layer, hin, sv):
        r, rr = matmul(hin, big[('mlp_w1', layer)], 'nn', f"mlp1_{layer}", (F32, BF), epilogue=act_epilogue)
        mo = matmul(rr, big[('mlp_w2', layer)], 'nn', f"mlp2_{layer}")
        sv.update(r=r, rr=rr, mo=mo)
        return mo

    sv = saved[0]
    h0, = fwd_call(seg_in, "norm_in", (nb,), [x0, gain(0, 0)], [_rows(D), _par(D)], [_sd((t, D), BF)], [_rows(D)])
    w_ab_in = jnp.pad(big[('ab_w_in', 0)], ((0, 0), (0, AB_IN_PAD - AB_IN)))
    u0 = matmul(h0, w_ab_in, 'nn', "ab_in")
    pool_outs = []
    for g in range(POOL_GROUPS):
        seqspec = pl.BlockSpec((seq, PG), lambda b, g=g: (b, g))
        po, = fwd_call(make_pool_fn(g), f"pool_{g}", (bsz,), [u0, p['pool_w'][0, g], p['pool_scale']],
                       [seqspec, pl.BlockSpec((PG, PG), lambda b: (0, 0)), pl.BlockSpec((1, PG), lambda b, g=g: (0, g))],
                       [_sd((t, PG), BF)], [pl.BlockSpec((seq, PG), lambda b: (b, 0))])
        pool_outs.append(po)
    cw = 256
    ncb = SSM_CONV_DIM // cw
    cbase = (POOL_W + SSM_INNER) // cw
    conv_in_specs = [pl.BlockSpec((seq, cw), lambda j, b: (b, cbase + j)), pl.BlockSpec((SSM_CONV, cw), lambda j, b: (0, j)),
                     pl.BlockSpec((1, cw), lambda j, b: (0, j))]
    conv_out_spec = pl.BlockSpec((seq, cw), lambda j, b: (b, j))
    xbc_act, = fwd_call(conv4_fn, "ssm_conv", (ncb, bsz), [u0, p['ssm_conv_w'][0], p['ssm_conv_b']], conv_in_specs,
                        [_sd((t, SSM_CONV_DIM))], [conv_out_spec])
    dtb = jnp.pad(p['ssm_dt_bias'], ((0, 0), (0, LANE - SSM_HEADS)))
    alog = jnp.pad(p['ssm_a_log'], ((0, 0), (0, LANE - SSM_HEADS)))
    dsk = jnp.pad(p['ssm_d'], ((0, 0), (0, LANE - SSM_HEADS)))
    yn, hs = ssd_fwd(xbc_act, u0, dtb, alog, dsk, p['ssm_norm'], consts, bsz, seq)
    mix0 = jnp.concatenate(pool_outs + [yn], axis=1)
    m0 = matmul(mix0, big[('ab_w_out', 0)], 'nn', "ab_out")
    x1, h2 = run_seg_res(x0, m0, gain(0, 1), gain(0, 2), "res_0a")
    big.update(ex.weights('l0', h2))
    ao0 = attention_fwd(0, x1, h2, sv)
    x2, h3 = run_seg_res(x1, ao0, gain(0, 3), gain(0, 4), "res_0b")
    mo0 = mlp_fwd(0, h3, sv)
    big.update(ex.weights('cd', sv['r']))
    x3, h4 = run_seg_res(x2, mo0, gain(0, 5), gain(1, 0), "res_0c")

    sv1 = saved[1]
    u1 = matmul(h4, big[('cd_w_in', 0)], 'nn', "cd_in")
    nd = D // LANE

    def cd_col(k):
        return pl.BlockSpec((seq, LANE), lambda j, b, k=k: (b, k * nd + j))

    cd_par = [pl.BlockSpec((CONF_K, LANE), lambda j, b: (0, j)), pl.BlockSpec((1, LANE), lambda j, b: (0, j)),
              pl.BlockSpec((SC_K, LANE), lambda j, b: (0, j))]
    cd_ins = [u1] * 5 + [p['conf_dw_w'][0], p['conf_dw_b'], p['sc_conv_w'][0]]
    cd_in_specs = [cd_col(k) for k in range(5)] + cd_par
    cd_out_spec = pl.BlockSpec((seq, LANE), lambda j, b: (b, j))
    vconv, sc_out = fwd_call(cd1_fn, "cd_conv", (nd, bsz), cd_ins, cd_in_specs, [_sd((t, D)), _sd((t, D), BF)],
                             [cd_out_spec, cd_out_spec])
    conf, = fwd_call(seg_ln, "conf_ln", (nb,), [vconv, p['conf_ln_g'], p['conf_ln_b']], [_rows(D), _par(D), _par(D)],
                     [_sd((t, D), BF)], [_rows(D)])
    mix1 = jnp.concatenate([conf, sc_out], axis=1)
    m1 = matmul(mix1, big[('cd_w_out', 0)], 'nn', "cd_out")
    x4, h5 = run_seg_res(x3, m1, gain(1, 1), gain(1, 2), "res_1a")
    big.update(ex.weights('l1', h5))
    ao1 = attention_fwd(1, x4, h5, sv1)
    x5, h6 = run_seg_res(x4, ao1, gain(1, 3), gain(1, 4), "res_1b")
    mo1 = mlp_fwd(1, h6, sv1)

    def loss_body(x_ref, m_ref, g_ref, t_ref, dy_ref, acc_ref):
        y = x_ref[...] + _rms(m_ref[...], g_ref[...])
        d = y - t_ref[...]
        dy_ref[...] = d / float(D)

        @pl.when(pl.program_id(0) == 0)
        def _():
            acc_ref[...] = jnp.zeros_like(acc_ref)

        acc_ref[...] += jnp.sum(d * d, axis=0, keepdims=True)

    dy, lanes = pl.pallas_call(
        loss_body, name="loss_head", grid=(nb,), in_specs=[_rows(D), _rows(D), _par(D), _rows(D)],
        out_specs=[_rows(D), _par(D)], out_shape=[_sd((t, D)), _sd((1, D))], compiler_params=_params())(x5, mo1, gain(1, 5), tgt)
    loss = 0.5 * jnp.sum(lanes) / float(D)

    gain_grads = {}

    def bwd_seg_out(xin, m, ga, dyv, name):
        dx, dm, dga = bwd_call(seg_out, name, (nb,), [xin, m, ga], [_rows(D), _rows(D), _par(D)], [dyv], [_rows(D)],
                               [0, 1, 2], [_sd((t, D)), _sd((t, D), BF), _sd((1, D))], [_rows(D), _rows(D), _par(D)],
                               [None, None, (0,)])
        return dx, dm, dga

    def bwd_seg_res(xin, m, ga, gb, dx1, dh, name):
        return bwd_call(seg_res, name, (nb,), [xin, m, ga, gb], [_rows(D), _rows(D), _par(D), _par(D)], [dx1, dh],
                        [_rows(D), _rows(D)], [0, 1, 2, 3], [_sd((t, D)), _sd((t, D), BF), _sd((1, D)), _sd((1, D))],
                        [_rows(D), _rows(D), _par(D), _par(D)], [None, None, (0,), (0,)])

    def mlp_bwd(layer, hin, dmo, sv):
        grads_w2 = matmul(sv['rr'], dmo, 'tn', f"d_mlp_w2_{layer}", BF)
        dr, = matmul(dmo, big[('mlp_w2', layer)], 'nt', f"d_r_{layer}", (BF,), epilogue=act_bwd_epilogue, extras=[sv['r']])
        grads_w1 = matmul(hin, dr, 'tn', f"d_mlp_w1_{layer}", BF)
        dh = matmul(dr, big[('mlp_w1', layer)], 'nt', f"d_h_mlp_{layer}")
        return dh, grads_w1, grads_w2

    def attention_bwd(layer, hin, dao, sv):
        g_wo = matmul(sv['o'], dao, 'tn', f"d_xa_wo_{layer}", BF)
        do = matmul(dao, big[('xa_wo', layer)], 'nt', f"d_o_{layer}", BF)
        grid, qs, ks, vs = attn_specs()
        kvo = pl.BlockSpec((N_MEM, XA_DH), lambda b, h, i: (b, h))
        dq, dk, dv = bwd_call(attn_fn, f"d_attn_{layer}", grid, [sv['q'], sv['kv'], sv['kv']], [qs, ks, vs], [do], [qs],
                              [0, 1, 2], [_sd((t, D), BF), _sd((bsz * N_MEM, D)), _sd((bsz * N_MEM, D))], [qs, kvo, kvo],
                              [None, (2,), (2,)])
        dkv = jnp.concatenate([dk, dv], axis=1)
        g_wkv = matmul(mem2, dkv, 'tn', f"d_xa_wkv_{layer}", BF)
        g_wq = matmul(hin, dq, 'tn', f"d_xa_wq_{layer}", BF)
        dh = matmul(dq, big[('xa_wq', layer)], 'nt', f"d_h_attn_{layer}")
        return dh, g_wq, g_wkv, g_wo

    per_layer = {k: [None, None] for k in ('xa_wq', 'xa_wkv', 'xa_wo', 'mlp_w1', 'mlp_w2')}

    dx5, dmo1, gain_grads[(1, 5)] = bwd_seg_out(x5, mo1, gain(1, 5), dy, "d_out")
    dh6, per_layer['mlp_w1'][1], per_layer['mlp_w2'][1] = mlp_bwd(1, h6, dmo1, sv1)
    dx4, dao1, gain_grads[(1, 3)], gain_grads[(1, 4)] = bwd_seg_res(x4, ao1, gain(1, 3), gain(1, 4), dx5, dh6, "d_res_1b")
    dh5, per_layer['xa_wq'][1], per_layer['xa_wkv'][1], per_layer['xa_wo'][1] = attention_bwd(1, h5, dao1, sv1)
    ex.put_grads('l1', G_L1, {(k, 1): v[1] for k, v in per_layer.items()})
    dx3, dm1, gain_grads[(1, 1)], gain_grads[(1, 2)] = bwd_seg_res(x3, m1, gain(1, 1), gain(1, 2), dx4, dh5, "d_res_1a")
    g_cd_out = matmul(mix1, dm1, 'tn', "d_cd_w_out", BF)
    dmix1 = matmul(dm1, big[('cd_w_out', 0)], 'nt', "d_mix1")
    dvconv, dlg, dlb = bwd_call(seg_ln, "d_conf_ln", (nb,), [vconv, p['conf_ln_g'], p['conf_ln_b']],
                                [_rows(D), _par(D), _par(D)], [dmix1], [_rows(D, 0)], [0, 1, 2],
                                [_sd((t, D)), _sd((1, D)), _sd((1, D))], [_rows(D), _par(D), _par(D)], [None, (0,), (0,)])
    grads['conf_ln_g'], grads['conf_ln_b'] = dlg, dlb
    cd_g = bwd_call(cd1_fn, "d_cd_conv", (nd, bsz), cd_ins, cd_in_specs, [dvconv, dmix1],
                    [cd_out_spec, pl.BlockSpec((seq, LANE), lambda j, b: (b, nd + j))], list(range(8)),
                    [_sd((t, D), BF)] * 5 + [_sd((CONF_K, D)), _sd((1, D)), _sd((SC_K, D))], [cd_out_spec] * 5 + cd_par,
                    [None] * 5 + [(1,), (1,), (1,)])
    du1 = jnp.concatenate(cd_g[:5], axis=1)
    grads['conf_dw_w'], grads['conf_dw_b'], grads['sc_conv_w'] = cd_g[5][None], cd_g[6], cd_g[7][None]
    g_cd_in = matmul(h4, du1, 'tn', "d_cd_w_in", BF)
    ex.put_grads('cd', G_CD, {('cd_w_in', 0): g_cd_in, ('cd_w_out', 0): g_cd_out})
    dh4 = matmul(du1, big[('cd_w_in', 0)], 'nt', "d_h_cd")

    dx2, dmo0, gain_grads[(0, 5)], gain_grads[(1, 0)] = bwd_seg_res(x2, mo0, gain(0, 5), gain(1, 0), dx3, dh4, "d_res_0c")
    dh3, per_layer['mlp_w1'][0], per_layer['mlp_w2'][0] = mlp_bwd(0, h3, dmo0, sv)
    dx1, dao0, gain_grads[(0, 3)], gain_grads[(0, 4)] = bwd_seg_res(x1, ao0, gain(0, 3), gain(0, 4), dx2, dh3, "d_res_0b")
    dh2, per_layer['xa_wq'][0], per_layer['xa_wkv'][0], per_layer['xa_wo'][0] = attention_bwd(0, h2, dao0, sv)
    ex.put_grads('l0', G_L0, {(k, 0): v[0] for k, v in per_layer.items()})
    dx0r, dm0, gain_grads[(0, 1)], gain_grads[(0, 2)] = bwd_seg_res(x0, m0, gain(0, 1), gain(0, 2), dx1, dh2, "d_res_0a")
    g_ab_out = matmul(mix0, dm0, 'tn', "d_ab_w_out", BF)
    dmix0 = matmul(dm0, big[('ab_w_out', 0)], 'nt', "d_mix0")
    dxs, dbm, dcm, dz, ddt, ddtb, dalog, ddsk, dnw = ssd_bwd(xbc_act, u0, dtb, alog, dsk, p['ssm_norm'], consts, hs, dmix0,
                                                             bsz, seq)
    grads['ssm_dt_bias'] = (ddtb[0] + ddtb[1])[:, :SSM_HEADS]
    grads['ssm_a_log'] = (dalog[0] + dalog[1])[:, :SSM_HEADS]
    grads['ssm_d'] = (ddsk[0] + ddsk[1])[:, :SSM_HEADS]
    grads['ssm_norm'] = dnw
    dxbc_act = jnp.concatenate([dxs, dbm, dcm], axis=1)
    dxr, dcw, dcb = bwd_call(conv4_fn, "d_ssm_conv", (ncb, bsz), [u0, p['ssm_conv_w'][0], p['ssm_conv_b']], conv_in_specs,
                             [dxbc_act], [conv_out_spec], [0, 1, 2],
                             [_sd((t, SSM_CONV_DIM), BF), _sd((SSM_CONV, SSM_CONV_DIM)), _sd((1, SSM_CONV_DIM))],
                             [conv_out_spec, conv_in_specs[1], conv_in_specs[2]], [None, (1,), (1,)])
    grads['ssm_conv_w'], grads['ssm_conv_b'] = dcw[None], dcb
    dpool, dpw, dps = [], [], []
    for g in range(POOL_GROUPS):
        seqspec = pl.BlockSpec((seq, PG), lambda b, g=g: (b, g))
        one = pl.BlockSpec((seq, PG), lambda b: (b, 0))
        wspec = pl.BlockSpec((PG, PG), lambda b: (0, 0))
        sspec = pl.BlockSpec((1, PG), lambda b, g=g: (0, g))
        a, bb, c = bwd_call(make_pool_fn(g), f"d_pool_{g}", (bsz,), [u0, p['pool_w'][0, g], p['pool_scale']],
                            [seqspec, wspec, sspec], [dmix0], [seqspec], [0, 1, 2],
                            [_sd((t, PG), BF), _sd((PG, PG)), _sd((1, PG))], [one, wspec, pl.BlockSpec((1, PG), lambda b: (0, 0))],
                            [None, (0,), (0,)])
        dpool.append(a)
        dpw.append(bb)
        dps.append(c)
    grads['pool_w'] = jnp.stack(dpw)[None]
    grads['pool_scale'] = jnp.concatenate(dps, axis=1)
    du0 = jnp.concatenate(dpool + [dz, dxr, (ddt[0] + ddt[1]).astype(BF)], axis=1)
    ex.put_grads('ab', G_AB, {('ab_w_in', 0): matmul(h0, du0, 'tn', "d_ab_w_in", BF)[:, :AB_IN], ('ab_w_out', 0): g_ab_out})
    dh0 = matmul(du0, w_ab_in, 'nt', "d_h_ab")
    dx, dg00 = bwd_call(seg_in_res, "d_norm_in", (nb,), [x0, gain(0, 0)], [_rows(D), _par(D)], [dx0r, dh0],
                        [_rows(D), _rows(D)], [0, 1], [_sd((t, D)), _sd((1, D))], [_rows(D), _par(D)], [None, (0,)])
    gain_grads[(0, 0)] = dg00
    grads['norm_gains'] = jnp.stack([jnp.concatenate([gain_grads[(l, i)] for i in range(6)], axis=0) for l in range(2)])
    return loss, dx, grads
```

```python
import functools
import math

import numpy as np
import jax
import jax.numpy as jnp
from jax import lax
from jax.experimental import pallas as pl
from jax.experimental.pallas import tpu as pltpu

BF = jnp.bfloat16
F32 = jnp.float32
HI = lax.Precision.HIGHEST

N_DEV = 8
D = 1024
N_MEM = 256
XA_HEADS = 4
XA_DH = D // XA_HEADS
POOL_GROUPS = 4
PG = 128
POOL_W = POOL_GROUPS * PG
SSM_INNER = 1024
SSM_GROUPS = 2
SSM_GSZ = SSM_INNER // SSM_GROUPS
SSM_HEADS = 16
SSM_P = 64
SSM_N = 128
SSM_CONV = 4
SSM_CONV_DIM = SSM_INNER + 2 * SSM_GROUPS * SSM_N
SSM_XBC_G = SSM_GSZ + 2 * SSM_N
CHUNK = 128
AB_IN = POOL_W + SSM_INNER + SSM_CONV_DIM + SSM_HEADS
AB_IN_PAD = POOL_W + SSM_INNER + SSM_CONV_DIM + 128
AB_OUT = POOL_W + SSM_INNER
CONF_K = 31
SC_K = 3
CD_IN = 5 * D
CD_OUT = 2 * D
MLP_H = 4 * D
RMS_EPS = 1e-6
LN_EPS = 1e-5
ADAM_LR = 0.001
ADAM_B1 = 0.9
ADAM_B2 = 0.999
ADAM_EPS = 1e-08
ADAM_WD = 0.01
ADAM_STEP = 10
VMEM_LIMIT = 56 * 1024 * 1024
LANE = 128

NAMES = ['x', 'mem', 'norm_gains', 'xa_wq', 'xa_wkv', 'xa_wo', 'mlp_w1', 'mlp_w2', 'ab_w_in', 'pool_w', 'pool_scale',
         'ssm_conv_w', 'ssm_conv_b', 'ssm_dt_bias', 'ssm_a_log', 'ssm_d', 'ssm_norm', 'ab_w_out', 'cd_w_in', 'conf_dw_w',
         'conf_dw_b', 'conf_ln_g', 'conf_ln_b', 'sc_conv_w', 'cd_w_out', 'loss_target']
WEIGHTS = NAMES[2:25]
BIG = [('xa_wq', 1), ('xa_wkv', 2), ('xa_wo', 1), ('mlp_w1', 2), ('mlp_w2', 1), ('cd_w_in', 2), ('cd_w_out', 1),
       ('ab_w_out', 1), ('ab_w_in', 2)]
SMALL_SHARDED = ['norm_gains', 'ssm_conv_w', 'conf_dw_w', 'conf_dw_b', 'conf_ln_g', 'conf_ln_b', 'sc_conv_w']
REPLICATED = ['pool_w', 'pool_scale', 'ssm_conv_b', 'ssm_dt_bias', 'ssm_a_log', 'ssm_d', 'ssm_norm']


def _dg(a, b, ca, cb, prec=None):
    return lax.dot_general(a, b, (((ca,), (cb,)), ((), ())), precision=prec, preferred_element_type=F32)


@functools.partial(jax.custom_vjp, nondiff_argnums=(2, 3))
def bdot(a, b, ca, cb):
    return _dg(a.astype(BF), b.astype(BF), ca, cb)


def _bdot_fwd(a, b, ca, cb):
    return bdot(a, b, ca, cb), (a, b)


def _bdot_bwd(ca, cb, res, g):
    a, b = res
    g16, a16, b16 = g.astype(BF), a.astype(BF), b.astype(BF)
    da = _dg(g16, b16, 1, 1 - cb) if ca == 1 else _dg(b16, g16, 1 - cb, 1)
    db = _dg(g16, a16, 0, 1 - ca) if cb == 1 else _dg(a16, g16, 1 - ca, 0)
    return da.astype(a.dtype), db.astype(b.dtype)


bdot.defvjp(_bdot_fwd, _bdot_bwd)


def _split3(a):
    a1 = a.astype(BF)
    r1 = a - a1.astype(F32)
    a2 = r1.astype(BF)
    a3 = (r1 - a2.astype(F32)).astype(BF)
    return a1, a2, a3


def _exact_right(a, c):
    m = a.shape[0]
    if m % 16:
        return sum(_dg(p, c, 1, 0) for p in _split3(a))
    o = _dg(jnp.concatenate(_split3(a), axis=0), c, 1, 0)
    return o[:m] + o[m:2 * m] + o[2 * m:]


def _exact_left(c, a):
    n = a.shape[1]
    o = _dg(c, jnp.concatenate(_split3(a), axis=1), 1, 0)
    return o[:, :n] + o[:, n:2 * n] + o[:, 2 * n:]


@jax.custom_vjp
def cmat(a, c, ct):
    return _exact_right(a, c)


def _cmat_fwd(a, c, ct):
    return cmat(a, c, ct), (c, ct)


def _cmat_bwd(res, g):
    c, ct = res
    return _exact_right(g, ct), jnp.zeros_like(c), jnp.zeros_like(ct)


cmat.defvjp(_cmat_fwd, _cmat_bwd)


@jax.custom_vjp
def cmatl(c, ct, a):
    return _exact_left(c, a)


def _cmatl_fwd(c, ct, a):
    return cmatl(c, ct, a), (c, ct)


def _cmatl_bwd(res, g):
    c, ct = res
    return jnp.zeros_like(c), jnp.zeros_like(ct), _exact_left(ct, g)


cmatl.defvjp(_cmatl_fwd, _cmatl_bwd)


SUBLANES = 8


def _taps(x, shifts, down):
    n, c = x.shape
    pad = _round_up(max(shifts), SUBLANES)
    if pad == 0:
        return {0: x}
    zeros = jnp.zeros((pad, c), x.dtype)
    xp = jnp.concatenate([zeros, x] if down else [x, zeros], axis=0)
    rolled, out = {0: xp}, {}
    for s in shifts:
        a, b = divmod(s, SUBLANES)
        if b not in rolled:
            rolled[b] = pltpu.roll(xp, b if down else n + pad - b, 0)
        off = pad - SUBLANES * a if down else SUBLANES * a
        out[s] = rolled[b][off:off + n]
    return out


def _shift_down(x, k):
    return _taps(x, [k], True)[k]


def _shift_up(x, k):
    return _taps(x, [k], False)[k]


@functools.partial(jax.custom_vjp, nondiff_argnums=(1,))
def shift(x, k):
    return _shift_down(x, k)


def _shift_fwd(x, k):
    return _shift_down(x, k), None


def _shift_bwd(k, _, g):
    return (_shift_up(g, k),)


shift.defvjp(_shift_fwd, _shift_bwd)


@functools.partial(jax.custom_vjp, nondiff_argnums=(2,))
def cconv(u, w, width):
    taps = _taps(u, list(range(width)), True)
    acc = u * w[width - 1:width, :]
    for k in range(width - 1):
        acc = acc + taps[width - 1 - k] * w[k:k + 1, :]
    return acc


def _cconv_fwd(u, w, width):
    return cconv(u, w, width), (u, w)


def _cconv_bwd(width, res, g):
    u, w = res
    rows = lax.broadcasted_iota(jnp.int32, w.shape, 0)
    du = g * w[width - 1:width, :]
    dw = jnp.where(rows == width - 1, jnp.sum(g * u, axis=0, keepdims=True), 0.0)
    g_taps = _taps(g, list(range(width)), False)
    u_taps = _taps(u, list(range(width)), True)
    for k in range(width - 1):
        s = width - 1 - k
        du = du + g_taps[s] * w[k:k + 1, :]
        dw = dw + jnp.where(rows == k, jnp.sum(g * u_taps[s], axis=0, keepdims=True), 0.0)
    return du, dw


cconv.defvjp(_cconv_fwd, _cconv_bwd)


def _rms(x, g):
    return x * lax.rsqrt(jnp.mean(x * x, axis=-1, keepdims=True) + RMS_EPS) * g


def _params(sem=None):
    return pltpu.CompilerParams(dimension_semantics=sem, vmem_limit_bytes=VMEM_LIMIT)


def _f32(v):
    return v if v.dtype == F32 else v.astype(F32)


def _first(axes):
    ok = None
    for ax in axes:
        c = pl.program_id(ax) == 0
        ok = c if ok is None else jnp.logical_and(ok, c)
    return ok


def fwd_call(fn, name, grid, ins, in_specs, out_shapes, out_specs):
    n_in = len(ins)

    def body(*refs):
        outs = fn(*[_f32(r[...]) for r in refs[:n_in]])
        for r, o in zip(refs[n_in:], outs):
            r[...] = o.astype(r.dtype)

    return pl.pallas_call(body, name=name, grid=grid, in_specs=in_specs, out_specs=out_specs, out_shape=out_shapes,
                          compiler_params=_params())(*ins)


def bwd_call(fn, name, grid, ins, in_specs, cots, cot_specs, gidx, g_shapes, g_specs, g_acc):
    n_in, n_cot = len(ins), len(cots)

    def body(*refs):
        vals = [_f32(r[...]) for r in refs[:n_in]]

        def f_sel(*dv):
            full = list(vals)
            for i, v in zip(gidx, dv):
                full[i] = v
            return tuple(fn(*full))

        outs, vjp = jax.vjp(f_sel, *[vals[i] for i in gidx])
        cts = tuple(_f32(r[...]) for r in refs[n_in:n_in + n_cot])
        grads = vjp(cts)
        for r, g, acc in zip(refs[n_in + n_cot:], grads, g_acc):
            if acc is None:
                r[...] = g.astype(r.dtype)
            else:
                @pl.when(_first(acc))
                def _():
                    r[...] = jnp.zeros_like(r)

                r[...] += g.astype(r.dtype)

    return pl.pallas_call(body, name=name, grid=grid, in_specs=list(in_specs) + list(cot_specs), out_specs=g_specs,
                          out_shape=g_shapes, compiler_params=_params())(*ins, *cots)


def _tile(dim, pref):
    if dim <= pref:
        return dim
    best = None
    for t in range(LANE, pref + 1, LANE):
        if dim % t == 0:
            best = t
    assert best is not None, dim
    return best


MATMUL_VMEM_BUDGET = 40 * 1024 * 1024


def _matmul_tiles(m, n, k, a_bytes, b_bytes, out_bytes):
    tn = _tile(n, 1024)
    for tk_pref in (k, 2048, 1024, 512):
        tk = _tile(k, tk_pref)
        for tm_pref in (1024, 512, 256):
            tm = _tile(m, tm_pref)
            need = 2 * (tm * tk * a_bytes + tk * tn * b_bytes + tm * tn * out_bytes) + (0 if tk == k else tm * tn * 4)
            need += (tm * tk * 2 if a_bytes == 4 else 0) + (tk * tn * 2 if b_bytes == 4 else 0)
            if need <= MATMUL_VMEM_BUDGET:
                return tm, tn, tk
    raise ValueError((m, n, k))


def matmul(a, b, mode, name, out_dtype=F32, epilogue=None, extras=()):
    if mode == 'nn':
        (m, k), (k2, n) = a.shape, b.shape
    elif mode == 'nt':
        (m, k), (n, k2) = a.shape, b.shape
    else:
        (k, m), (k2, n) = a.shape, b.shape
    assert k == k2, (name, a.shape, b.shape)
    n_extra = len(extras)
    out_dtypes = out_dtype if isinstance(out_dtype, tuple) else (out_dtype,)
    per_out = sum(jnp.dtype(dt).itemsize for dt in out_dtypes) + sum(e.dtype.itemsize for e in extras)
    tm, tn, tk = _matmul_tiles(m, n, k, a.dtype.itemsize, b.dtype.itemsize, per_out)
    nk = k // tk
    ca = 0 if mode == 'tn' else 1
    cb = 1 if mode == 'nt' else 0
    a_spec = pl.BlockSpec((tk, tm), lambda i, j, kk: (kk, i)) if mode == 'tn' else pl.BlockSpec((tm, tk), lambda i, j, kk: (i, kk))
    b_spec = pl.BlockSpec((tn, tk), lambda i, j, kk: (j, kk)) if mode == 'nt' else pl.BlockSpec((tk, tn), lambda i, j, kk: (kk, j))

    def finish(o_refs, extra_refs, acc):
        outs = (acc,) if epilogue is None else epilogue(acc, *[_f32(e[...]) for e in extra_refs])
        for o_ref, o in zip(o_refs, outs):
            o_ref[...] = o.astype(o_ref.dtype)

    def body_whole_k(a_ref, b_ref, *refs):
        finish(refs[n_extra:], refs[:n_extra], _dg(a_ref[...].astype(BF), b_ref[...].astype(BF), ca, cb))

    def body_split_k(a_ref, b_ref, *refs):
        extra_refs, o_refs, acc = refs[:n_extra], refs[n_extra:-1], refs[-1]
        kk = pl.program_id(2)

        @pl.when(kk == 0)
        def _():
            acc[...] = jnp.zeros_like(acc)

        acc[...] += _dg(a_ref[...].astype(BF), b_ref[...].astype(BF), ca, cb)

        @pl.when(kk == nk - 1)
        def _():
            finish(o_refs, extra_refs, acc[...])

    tile = pl.BlockSpec((tm, tn), lambda i, j, kk: (i, j))
    outs = pl.pallas_call(
        body_whole_k if nk == 1 else body_split_k, name=name, grid=(m // tm, n // tn, nk),
        in_specs=[a_spec, b_spec] + [tile] * n_extra, out_specs=[tile] * len(out_dtypes),
        out_shape=[jax.ShapeDtypeStruct((m, n), dt) for dt in out_dtypes],
        scratch_shapes=[] if nk == 1 else [pltpu.VMEM((tm, tn), F32)],
        compiler_params=_params(("parallel", "parallel", "arbitrary")))(a, b, *extras)
    return outs if isinstance(out_dtype, tuple) else outs[0]


_FLIPS = [(0, 0, 1), (1, 0, 0), (0, 1, 0), (1, 1, 0), (1, 0, 1), (0, 1, 1), (1, 1, 1)]


def _me():
    return lax.axis_index("x"), lax.axis_index("y"), lax.axis_index("c")


def _flip(pos, f):
    return tuple(jnp.where(fi == 1, 1 - p, p) if fi else p for p, fi in zip(pos, f))


def _slot(pos):
    return 4 * pos[0] + 2 * pos[1] + pos[2]


def all_gather(v, name):
    def body(v_ref, out_ref, send_sems, recv_sems, local_sem):
        me = _me()
        sibling = _flip(me, (0, 0, 1))
        chips = [_flip(me, f) for f in ((1, 0, 0), (0, 1, 0), (1, 1, 0))]

        def copy(k, block, to, src=None):
            return pltpu.make_async_remote_copy(
                src_ref=out_ref.at[_slot(block)] if src is None else src, dst_ref=out_ref.at[_slot(block)],
                send_sem=send_sems.at[k], recv_sem=recv_sems.at[k], device_id=to, device_id_type=pl.DeviceIdType.MESH)

        mine = pltpu.make_async_copy(v_ref, out_ref.at[_slot(me)], local_sem)
        mine.start()
        first = [copy(0, me, sibling, src=v_ref)] + [copy(1 + j, me, chip, src=v_ref) for j, chip in enumerate(chips)]
        for cp in first:
            cp.start()
        passed = [copy(4 + j, chip, sibling) for j, chip in enumerate(chips)]
        for j, chip in enumerate(chips):
            copy(1 + j, chip, me).wait_recv()
            passed[j].start()
        copy(0, sibling, me).wait_recv()
        for j, chip in enumerate(chips):
            copy(4 + j, _flip(chip, (0, 0, 1)), me).wait_recv()
        for cp in first + passed:
            cp.wait_send()
        mine.wait()

    return pl.pallas_call(
        body, name=name, out_shape=jax.ShapeDtypeStruct((N_DEV,) + v.shape, v.dtype),
        in_specs=[pl.BlockSpec(memory_space=pl.ANY)], out_specs=pl.BlockSpec(memory_space=pl.ANY),
        scratch_shapes=[pltpu.SemaphoreType.DMA((7,)), pltpu.SemaphoreType.DMA((7,)), pltpu.SemaphoreType.DMA(())],
    )(v)


def all_to_all(v, name):
    def body(v_ref, out_ref, send_sems, recv_sems, local_sem):
        me = _me()
        mine = pltpu.make_async_copy(v_ref.at[_slot(me)], out_ref.at[_slot(me)], local_sem)
        mine.start()
        copies = []
        for k, f in enumerate(_FLIPS):
            peer = _flip(me, f)
            cp = pltpu.make_async_remote_copy(
                src_ref=v_ref.at[_slot(peer)], dst_ref=out_ref.at[_slot(me)], send_sem=send_sems.at[k],
                recv_sem=recv_sems.at[k], device_id=peer, device_id_type=pl.DeviceIdType.MESH)
            cp.start()
            copies.append(cp)
        for k, f in enumerate(_FLIPS):
            peer = _flip(me, f)
            pltpu.make_async_remote_copy(
                src_ref=v_ref.at[_slot(peer)], dst_ref=out_ref.at[_slot(peer)], send_sem=send_sems.at[k],
                recv_sem=recv_sems.at[k], device_id=peer, device_id_type=pl.DeviceIdType.MESH).wait_recv()
        for cp in copies:
            cp.wait_send()
        mine.wait()

    return pl.pallas_call(
        body, name=name, out_shape=jax.ShapeDtypeStruct(v.shape, v.dtype),
        in_specs=[pl.BlockSpec(memory_space=pl.ANY)], out_specs=pl.BlockSpec(memory_space=pl.ANY),
        scratch_shapes=[pltpu.SemaphoreType.DMA((7,)), pltpu.SemaphoreType.DMA((7,)), pltpu.SemaphoreType.DMA(())],
    )(v)


def sum_slots(v, name, tr=256):
    _, r, c = v.shape
    tr = _tile_rows(r, tr)

    def body(v_ref, o_ref):
        acc = v_ref[0].astype(F32)
        for s in range(1, N_DEV):
            acc = acc + v_ref[s].astype(F32)
        o_ref[...] = acc

    return pl.pallas_call(body, name=name, grid=(r // tr,), in_specs=[pl.BlockSpec((N_DEV, tr, c), lambda i: (0, i, 0))],
                          out_specs=pl.BlockSpec((tr, c), lambda i: (i, 0)), out_shape=jax.ShapeDtypeStruct((r, c), F32),
                          compiler_params=_params())(v)


def _tile_rows(r, pref):
    if r <= pref:
        return r
    best = None
    for t in range(8, pref + 1, 8):
        if r % t == 0:
            best = t
    return r if best is None else best


def adamw(w, m, v, g, name):
    r, c = w.shape
    tr = _tile_rows(r, 512 if c <= 1024 else 128)

    def body(w_ref, m_ref, v_ref, g_ref, d_ref, nm_ref, nv_ref):
        gg = g_ref[...]
        nm = ADAM_B1 * m_ref[...] + (1.0 - ADAM_B1) * gg
        nv = ADAM_B2 * v_ref[...] + (1.0 - ADAM_B2) * jnp.square(gg)
        m_hat = nm / (1.0 - ADAM_B1 ** ADAM_STEP)
        v_hat = nv / (1.0 - ADAM_B2 ** ADAM_STEP)
        d_ref[...] = -ADAM_LR * (m_hat / (jnp.sqrt(v_hat) + ADAM_EPS) + ADAM_WD * w_ref[...])
        nm_ref[...] = nm
        nv_ref[...] = nv

    spec = pl.BlockSpec((tr, c), lambda i: (i, 0))
    sh = jax.ShapeDtypeStruct((r, c), F32)
    return pl.pallas_call(body, name=name, grid=(r // tr,), in_specs=[spec] * 4, out_specs=[spec] * 3,
                          out_shape=[sh] * 3, compiler_params=_params())(w, m, v, g)


def seg_in(x, g):
    return (_rms(x, g),)


def seg_in_res(x, g):
    return x, _rms(x, g)


def seg_res(x, m, ga, gb):
    x1 = x + _rms(m, ga)
    return x1, _rms(x1, gb)


def seg_out(x, m, ga):
    return (x + _rms(m, ga),)


def act_epilogue(r):
    t = jnp.maximum(r, 0.0)
    return r, t * t


def act_bwd_epilogue(drr, r):
    return (drr * (2.0 * jnp.maximum(r, 0.0)),)


def seg_ln(v, g, b):
    mu = jnp.mean(v, axis=-1, keepdims=True)
    var = jnp.mean(jnp.square(v - mu), axis=-1, keepdims=True)
    vn = (v - mu) * lax.rsqrt(var + LN_EPS) * g + b
    return (jax.nn.silu(vn),)


def make_pool_fn(group):
    window = 2 ** (group + 1)

    def pool_fn(ug, pw, scale):
        s = ug
        for lvl in range(group + 1):
            s = s + shift(s, 2 ** lvl)
        cnt = jnp.minimum(lax.broadcasted_iota(jnp.int32, ug.shape, 0) + 1, window).astype(F32)
        return (bdot(s / cnt - ug, pw, 1, 0) * scale,)

    return pool_fn


def conv4_fn(xr, w, b):
    return (jax.nn.silu(cconv(xr, w, SSM_CONV) + b),)


def cd1_fn(u, dww, dwb, scw):
    val, gate, bg, cg, hh = (u[:, k * LANE:(k + 1) * LANE] for k in range(5))
    v = val * jax.nn.sigmoid(gate)
    vc = cconv(v, dww, CONF_K) + dwb
    sc = bg * cconv(cg * hh, scw, SC_K)
    return vc, sc


def attn_fn(q, k, v):
    s = bdot(q, k, 1, 1) / math.sqrt(XA_DH)
    p = jax.nn.softmax(s, axis=-1)
    return (bdot(p, v, 1, 0),)


def ssd_chunk(xbc, z, dtraw, dtb, alog, dsk, nw, h0, h1, h2, h3, e64, e64t, ecat, ecatt, tril, trilt):
    xs, bm, cm = xbc[:, :SSM_GSZ], xbc[:, SSM_GSZ:SSM_GSZ + SSM_N], xbc[:, SSM_GSZ + SSM_N:]
    hin = (h0, h1, h2, h3)
    dt = jax.nn.softplus(dtraw + dtb)
    a = -jnp.exp(alog)
    d_a = dt * a
    cs = cmatl(tril, trilt, d_a)
    cs_cat = cmat(cs, ecat, ecatt)
    cs64, cs128 = cs_cat[:, :SSM_GSZ], cs_cat[:, SSM_GSZ:]
    dt64 = cmat(dt, e64, e64t)
    row = lax.broadcasted_iota(jnp.int32, (8, LANE), 0)
    heads = jnp.where(row == 0, dsk, jnp.where(row == 1, jnp.sum(d_a, axis=0, keepdims=True), 0.0))
    heads64 = cmat(heads, e64, e64t)
    d64, tot64 = heads64[0:1, :], heads64[1:2, :]
    xdt = xs * dt64
    cb = bdot(cm, bm, 1, 1)
    li = lax.broadcasted_iota(jnp.int32, (CHUNK, CHUNK), 0)
    si = lax.broadcasted_iota(jnp.int32, (CHUNK, CHUNK), 1)
    causal = li >= si
    lane = lax.broadcasted_iota(jnp.int32, (CHUNK, LANE), 1)
    xw = xdt * jnp.exp(tot64 - cs64)
    ecs = jnp.exp(cs64)
    etot = jnp.exp(tot64)
    ycols, hout = [], []
    for j in range(4):
        sl = slice(j * LANE, (j + 1) * LANE)
        xj = xdt[:, sl]
        ys = []
        for hh in range(2):
            r = 2 * j + hh
            col = cs128[:, r * LANE:(r + 1) * LANE]
            decay = jnp.exp(jnp.where(causal, col - col.T, -1e30))
            ys.append(bdot(cb * decay, xj, 1, 0))
        y_diag = jnp.where(lane < SSM_P, ys[0], ys[1])
        y_off = bdot(cm, hin[j], 1, 0) * ecs[:, sl]
        ycols.append(y_diag + y_off)
        hout.append(etot[:, sl] * hin[j] + bdot(bm, xw[:, sl], 0, 0))
    y = jnp.concatenate(ycols, axis=1) + d64 * xs
    y = y * jax.nn.silu(z)
    yn = y * lax.rsqrt(jnp.mean(y * y, axis=-1, keepdims=True) + RMS_EPS) * nw
    return (yn,) + tuple(hout)


def _xbc_group(a, axis):
    parts = []
    for g in range(SSM_GROUPS):
        for start, width in ((g * SSM_GSZ, SSM_GSZ), (SSM_INNER + g * SSM_N, SSM_N), (SSM_INNER + (SSM_GROUPS + g) * SSM_N, SSM_N)):
            parts.append(lax.slice_in_dim(a, start, start + width, axis=axis))
    return jnp.concatenate(parts, axis=axis)


def _xbc_ungroup(a, axis):
    xs, bs, cs = [], [], []
    for g in range(SSM_GROUPS):
        base = g * SSM_XBC_G
        xs.append(lax.slice_in_dim(a, base, base + SSM_GSZ, axis=axis))
        bs.append(lax.slice_in_dim(a, base + SSM_GSZ, base + SSM_GSZ + SSM_N, axis=axis))
        cs.append(lax.slice_in_dim(a, base + SSM_GSZ + SSM_N, base + SSM_XBC_G, axis=axis))
    return jnp.concatenate(xs + bs + cs, axis=axis)


def _ssd_consts():
    h = np.arange(LANE)[:, None]
    e64 = np.stack([(h == g * 8 + np.arange(SSM_GSZ)[None, :] // SSM_P) for g in range(SSM_GROUPS)]).astype(np.float32)
    e128 = np.stack([(h == g * 8 + np.arange(8 * LANE)[None, :] // LANE) for g in range(SSM_GROUPS)]).astype(np.float32)
    ecat = np.concatenate([e64, e128], axis=2)
    tril = np.tril(np.ones((CHUNK, CHUNK), np.float32))
    return tuple(jnp.asarray(c, dtype=BF) for c in (e64, e64.transpose(0, 2, 1), ecat, ecat.transpose(0, 2, 1), tril, tril.T))


def _ssd_specs(nc, rev):
    def ci(c):
        return nc - 1 - c if rev else c

    def row(width, col):
        return pl.BlockSpec((CHUNK, width), lambda g, b, c: (b * nc + ci(c), col(g)))

    data = [row(SSM_XBC_G, lambda g: g),
            row(SSM_GSZ, lambda g: 1 + g), row(LANE, lambda g: 24)]
    par = [pl.BlockSpec((1, LANE), lambda g, b, c: (0, 0))] * 3 + [pl.BlockSpec((1, SSM_GSZ), lambda g, b, c: (0, g))]
    cst = [pl.BlockSpec((None, LANE, SSM_GSZ), lambda g, b, c: (g, 0, 0)), pl.BlockSpec((None, SSM_GSZ, LANE), lambda g, b, c: (g, 0, 0)),
           pl.BlockSpec((None, LANE, 12 * LANE), lambda g, b, c: (g, 0, 0)), pl.BlockSpec((None, 12 * LANE, LANE), lambda g, b, c: (g, 0, 0)),
           pl.BlockSpec((CHUNK, CHUNK), lambda g, b, c: (0, 0)), pl.BlockSpec((CHUNK, CHUNK), lambda g, b, c: (0, 0))]
    hsave = pl.BlockSpec((None, None, None, 4, SSM_N, LANE), lambda g, b, c: (g, b, ci(c), 0, 0, 0))
    yn = row(SSM_GSZ, lambda g: g)
    return data, par, cst, hsave, yn, row


def ssd_fwd(xbc_act, u, dtb, alog, dsk, nw, consts, bsz, seq):
    nc = seq // CHUNK
    data, par, cst, hsave, yn_spec, _ = _ssd_specs(nc, False)

    def body(xbc, z, dtr, dtb_r, alog_r, dsk_r, nw_r, e64, e64t, ecat, ecatt, tril, trilt, yn_ref, hs_ref, h):
        @pl.when(pl.program_id(2) == 0)
        def _():
            h[...] = jnp.zeros_like(h)

        hs_ref[...] = h[...]
        outs = ssd_chunk(xbc[...], z[...], dtr[...], dtb_r[...], alog_r[...], dsk_r[...], nw_r[...],
                         h[0], h[1], h[2], h[3], e64[...], e64t[...], ecat[...], ecatt[...], tril[...], trilt[...])
        yn_ref[...] = outs[0].astype(yn_ref.dtype)
        for j in range(4):
            h[j] = outs[1 + j]

    t = bsz * seq
    return pl.pallas_call(
        body, name="ssd_fwd", grid=(SSM_GROUPS, bsz, nc), in_specs=data + par + cst, out_specs=[yn_spec, hsave],
        out_shape=[jax.ShapeDtypeStruct((t, SSM_INNER), BF), jax.ShapeDtypeStruct((SSM_GROUPS, bsz, nc, 4, SSM_N, LANE), F32)],
        scratch_shapes=[pltpu.VMEM((4, SSM_N, LANE), F32)], compiler_params=_params(),
    )(xbc_act, u, u, dtb, alog, dsk, nw, *consts)


def ssd_bwd(xbc_act, u, dtb, alog, dsk, nw, consts, hs, dmix, bsz, seq):
    nc = seq // CHUNK
    data, par, cst, hsave, _, row = _ssd_specs(nc, True)
    t = bsz * seq
    dyn_spec = row(SSM_GSZ, lambda g: POOL_W // SSM_GSZ + g)

    def body(xbc, z, dtr, dtb_r, alog_r, dsk_r, nw_r, e64, e64t, ecat, ecatt, tril, trilt, hs_ref, dyn_ref,
             dxbc, dz, ddt, ddtb, dalog, ddsk, dnw, dh):
        @pl.when(pl.program_id(2) == 0)
        def _():
            dh[...] = jnp.zeros_like(dh)

        cst_vals = (e64[...], e64t[...], ecat[...], ecatt[...], tril[...], trilt[...])

        def f(*args):
            return ssd_chunk(*args, *cst_vals)

        prim = (xbc[...], z[...], dtr[...], dtb_r[...], alog_r[...], dsk_r[...], nw_r[...],
                hs_ref[0], hs_ref[1], hs_ref[2], hs_ref[3])
        _, vjp = jax.vjp(f, *prim)
        g = vjp((dyn_ref[...].astype(F32), dh[0], dh[1], dh[2], dh[3]))
        dxbc[...] = g[0]
        dz[...] = g[1].astype(dz.dtype)
        ddt[...] = g[2]

        @pl.when(_first((1, 2)))
        def _():
            for r in (ddtb, dalog, ddsk, dnw):
                r[...] = jnp.zeros_like(r)

        ddtb[...] += g[3]
        dalog[...] += g[4]
        ddsk[...] += g[5]
        dnw[...] += g[6]
        for j in range(4):
            dh[j] = g[7 + j]

    gpar = pl.BlockSpec((None, 1, LANE), lambda g, b, c: (g, 0, 0))
    out_specs = [row(SSM_XBC_G, lambda g: g), row(SSM_GSZ, lambda g: g),
                 pl.BlockSpec((None, CHUNK, LANE), lambda g, b, c: (g, b * nc + nc - 1 - c, 0)),
                 gpar, gpar, gpar, pl.BlockSpec((1, SSM_GSZ), lambda g, b, c: (0, g))]
    gp = jax.ShapeDtypeStruct((SSM_GROUPS, 1, LANE), F32)
    out_shape = [jax.ShapeDtypeStruct((t, SSM_CONV_DIM), F32), jax.ShapeDtypeStruct((t, SSM_INNER), BF),
                 jax.ShapeDtypeStruct((SSM_GROUPS, t, LANE), F32), gp, gp, gp, jax.ShapeDtypeStruct((1, SSM_INNER), F32)]
    return pl.pallas_call(
        body, name="ssd_bwd", grid=(SSM_GROUPS, bsz, nc), in_specs=data + par + cst + [hsave, dyn_spec], out_specs=out_specs,
        out_shape=out_shape, scratch_shapes=[pltpu.VMEM((4, SSM_N, LANE), F32)], compiler_params=_params(),
    )(xbc_act, u, u, dtb, alog, dsk, nw, *consts, hs, dmix)


TB = 512


def _rows(d, col=0):
    return pl.BlockSpec((TB, d), lambda i: (i, col))


def _par(d):
    return pl.BlockSpec((1, d), lambda i: (0, 0))


def _sd(shape, dtype=F32):
    return jax.ShapeDtypeStruct(shape, dtype)


def _round_up(n, m):
    return -(-n // m) * m


def _pad_rows(a, rows):
    return jnp.pad(a, ((0, rows - a.shape[0]), (0, 0)))


def _pack128(arrs):
    flat = jnp.concatenate([a.reshape(-1) for a in arrs])
    n = flat.shape[0]
    rows = -(-n // (8 * LANE)) * 8
    return jnp.pad(flat, (0, rows * LANE - n)).reshape(rows, LANE)


def _unpack128(packed, shapes):
    flat = packed.reshape(-1)
    out, off = [], 0
    for s in shapes:
        n = int(np.prod(s))
        out.append(flat[off:off + n].reshape(s))
        off += n
    return out


def kernel(x, mem, norm_gains, xa_wq, xa_wkv, xa_wo, mlp_w1, mlp_w2, ab_w_in, pool_w, pool_scale, ssm_conv_w, ssm_conv_b, ssm_dt_bias, ssm_a_log, ssm_d, ssm_norm, ab_w_out, cd_w_in, conf_dw_w, conf_dw_b, conf_ln_g, conf_ln_b, sc_conv_w, cd_w_out, loss_target, m_norm_gains, m_xa_wq, m_xa_wkv, m_xa_wo, m_mlp_w1, m_mlp_w2, m_ab_w_in, m_pool_w, m_pool_scale, m_ssm_conv_w, m_ssm_conv_b, m_ssm_dt_bias, m_ssm_a_log, m_ssm_d, m_ssm_norm, m_ab_w_out, m_cd_w_in, m_conf_dw_w, m_conf_dw_b, m_conf_ln_g, m_conf_ln_b, m_sc_conv_w, m_cd_w_out, v_norm_gains, v_xa_wq, v_xa_wkv, v_xa_wo, v_mlp_w1, v_mlp_w2, v_ab_w_in, v_pool_w, v_pool_scale, v_ssm_conv_w, v_ssm_conv_b, v_ssm_dt_bias, v_ssm_a_log, v_ssm_d, v_ssm_norm, v_ab_w_out, v_cd_w_in, v_conf_dw_w, v_conf_dw_b, v_conf_ln_g, v_conf_ln_b, v_sc_conv_w, v_cd_w_out):
    args = locals()
    w = {n: args[n] for n in WEIGHTS}
    mom_m = {n: args["m_" + n] for n in WEIGHTS}
    mom_v = {n: args["v_" + n] for n in WEIGHTS}
    ex = Exchange(w)
    loss_local, grad_x, small_grads = local_step(x, mem, loss_target, ex)
    loss = lax.psum(loss_local, ("x", "y", "c"))
    outs = {}

    def update_big(names, own):
        last = None
        for n in names:
            shp = w[n].shape
            view = (-1, shp[-1])
            g = jnp.stack([own[(n, layer)] for layer in range(shp[0])])
            d, nm, nv = adamw(w[n].reshape(view), mom_m[n].reshape(view), mom_v[n].reshape(view), g.reshape(view), "adamw_" + n)
            outs[n] = (g, d.reshape(shp), nm.reshape(shp), nv.reshape(shp))
            last = d
        return last

    own = {}
    for key in ('l1', 'cd', 'l0'):
        own.update(ex.reduced(key, grad_x))
    late = update_big(['xa_wq', 'xa_wkv', 'xa_wo', 'mlp_w1', 'mlp_w2', 'cd_w_in', 'cd_w_out'], own)
    g_own = ex.reduced_small(small_grads)
    update_big(['ab_w_in', 'ab_w_out'], ex.reduced('ab', late))
    small = SMALL_SHARDED + REPLICATED
    shapes = [w[n].shape for n in small]
    d, nm, nv = adamw(_pack128([w[n] for n in small]), _pack128([mom_m[n] for n in small]), _pack128([mom_v[n] for n in small]),
                      _pack128([g_own[n] for n in small]), "adamw_small")
    for n, dd, mm, vv in zip(small, _unpack128(d, shapes), _unpack128(nm, shapes), _unpack128(nv, shapes)):
        outs[n] = (g_own[n], dd, mm, vv)
    return (loss, grad_x.reshape(x.shape), *[outs[n][0] for n in WEIGHTS], *[outs[n][1] for n in WEIGHTS],
            *[outs[n][2] for n in WEIGHTS], *[outs[n][3] for n in WEIGHTS])


G_AB = (('ab_w_in', 0), ('ab_w_out', 0))
G_L0 = (('xa_wq', 0), ('xa_wkv', 0), ('xa_wo', 0), ('mlp_w1', 0), ('mlp_w2', 0))
G_L1 = (('xa_wq', 1), ('xa_wkv', 1), ('xa_wo', 1), ('mlp_w1', 1), ('mlp_w2', 1))
G_CD = (('cd_w_in', 0), ('cd_w_out', 0))
GATHER_CHAIN = {'l0': ('cd', G_CD), 'cd': ('l1', G_L1)}
SHARD_AXIS = dict(BIG)
MEMBER_ROW_TILE = 64
FLAT_ROW_TILE = 128


def _members(group, w):
    out = []
    for n, layer in group:
        shp = w[n].shape[1:]
        if SHARD_AXIS[n] == 2:
            shp = (shp[1], shp[0])
        assert shp[1] == D, (n, shp)
        out.append((n, layer, shp, shp[0], _round_up(shp[0], MEMBER_ROW_TILE)))
    return out


def _group_rows(group, w):
    return _round_up(sum(m[4] for m in _members(group, w)), FLAT_ROW_TILE)


def _flat_shards(group, w):
    parts = []
    for n, layer, _, _, padded in _members(group, w):
        shard = w[n][layer].astype(BF)
        parts.append(_pad_rows(shard.T if SHARD_AXIS[n] == 2 else shard, padded))
    return _pad_rows(jnp.concatenate(parts, axis=0), _group_rows(group, w))


def _full_from_slots(land, group, w):
    out, off = {}, 0
    for n, layer, shp, rows, padded in _members(group, w):
        out[(n, layer)] = land[:, off:off + rows].reshape(N_DEV * rows, D)
        off += padded
    return out


def _slots_from_full(grads, group, w):
    parts = []
    for n, layer, shp, rows, padded in _members(group, w):
        blk = grads[(n, layer)].astype(BF).reshape(N_DEV, rows, D)
        parts.append(jnp.pad(blk, ((0, 0), (0, padded - rows), (0, 0))))
    send = jnp.concatenate(parts, axis=1)
    return jnp.pad(send, ((0, 0), (0, _group_rows(group, w) - send.shape[1]), (0, 0)))


def _own_from_sum(summed, group, w):
    out, off = {}, 0
    for n, layer, shp, rows, padded in _members(group, w):
        g = summed[off:off + rows]
        out[(n, layer)] = g.T if SHARD_AXIS[n] == 2 else g
        off += padded
    return out


_HBM = pl.BlockSpec(memory_space=pltpu.HBM)
_SEM = pl.BlockSpec(memory_space=pltpu.SEMAPHORE)
_ANY = pl.BlockSpec(memory_space=pl.ANY)


def _peer_copy(k, src, dst, send_sems, recv_sems, peer):
    return pltpu.make_async_remote_copy(src_ref=src, dst_ref=dst, send_sem=send_sems.at[k], recv_sem=recv_sems.at[k],
                                        device_id=peer, device_id_type=pl.DeviceIdType.MESH)


def exchange_start(src, name, scatter):
    shape = src.shape[-2:]

    def body(src_ref, land_ref, send_sems, recv_sems, src_thru, land_thru, token):
        me = _me()
        for k, f in enumerate(_FLIPS):
            peer = _flip(me, f)
            piece = src_ref.at[_slot(peer)] if scatter else src_ref
            _peer_copy(k, piece, land_ref.at[_slot(me)], send_sems, recv_sems, peer).start()
        token[...] = jnp.zeros_like(token)

    land = pltpu.with_memory_space_constraint(lax.empty((N_DEV,) + shape, src.dtype), pltpu.HBM)
    return pl.pallas_call(
        body, name=name,
        out_shape=(pltpu.SemaphoreType.DMA((7,)), pltpu.SemaphoreType.DMA((7,)), pltpu.HBM(src.shape, src.dtype),
                   pltpu.HBM((N_DEV,) + shape, src.dtype), jax.ShapeDtypeStruct((8, LANE), F32)),
        in_specs=(_HBM, _HBM), out_specs=(_SEM, _SEM, _HBM, _HBM, pl.BlockSpec(memory_space=pltpu.VMEM)),
        input_output_aliases={0: 2, 1: 3},
        compiler_params=pltpu.CompilerParams(has_side_effects=pltpu.SideEffectType.DATAFLOW_SIDE_EFFECTING),
    )(pltpu.with_memory_space_constraint(src, pltpu.HBM), land)


def exchange_wait(handles, after, name, scatter):
    send_sems, recv_sems, src_thru, land_thru, _ = handles

    def body(src_ref, land_ref, send_sems, recv_sems, after_ref, src_dead, got_ref, token):
        me = _me()
        for k, f in enumerate(_FLIPS):
            peer = _flip(me, f)
            piece = src_ref.at[_slot(peer)] if scatter else src_ref
            cp = _peer_copy(k, piece, land_ref.at[_slot(peer)], send_sems, recv_sems, peer)
            cp.wait_send()
            cp.wait_recv()
        token[...] = jnp.zeros_like(token)

    return pl.pallas_call(
        body, name=name, out_shape=(pltpu.HBM(src_thru.shape, src_thru.dtype), pltpu.HBM(land_thru.shape, land_thru.dtype),
                                    jax.ShapeDtypeStruct((8, LANE), F32)),
        in_specs=(_HBM, _HBM, _SEM, _SEM, _ANY), out_specs=(_HBM, _HBM, pl.BlockSpec(memory_space=pltpu.VMEM)),
        input_output_aliases={0: 0, 1: 1},
        compiler_params=pltpu.CompilerParams(has_side_effects=pltpu.SideEffectType.DATAFLOW_SIDE_EFFECTING),
    )(src_thru, land_thru, send_sems, recv_sems, after)


class Exchange:
    def __init__(self, w):
        self.w = w
        self.me = _slot(_me())
        shapes = [w[n].shape for n in SMALL_SHARDED]
        gs = all_gather(_pack128([w[n] for n in SMALL_SHARDED]), "gather_small")
        per_dev = [_unpack128(gs[d], shapes) for d in range(N_DEV)]
        self.small = {n: jnp.concatenate([per_dev[d][i] for d in range(N_DEV)], axis=-1) for i, n in enumerate(SMALL_SHARDED)}
        self.small.update({n: w[n] for n in REPLICATED})
        self.now = _full_from_slots(all_gather(_flat_shards(G_AB, w), "gather_ab"), G_AB, w)
        self.gathers = {'l0': (G_L0, exchange_start(_flat_shards(G_L0, w), "gather_l0_start", False))}
        self.tokens = [self.gathers['l0'][1][4]]
        self.reductions = {}

    def take_tokens(self):
        toks, self.tokens = self.tokens, []
        return toks

    def weights(self, key, after):
        if key == 'ab':
            return self.now
        group, handles = self.gathers[key]
        _, land, done = exchange_wait(handles, after, f"gather_{key}_wait", False)
        nxt = GATHER_CHAIN.get(key)
        if nxt is not None:
            src = _flat_shards(nxt[1], self.w) + done[0, 0].astype(BF)
            self.gathers[nxt[0]] = (nxt[1], exchange_start(src, f"gather_{nxt[0]}_start", False))
            self.tokens.append(self.gathers[nxt[0]][1][4])
        land = lax.dynamic_update_slice(land, handles[2][None], (self.me, 0, 0))
        return _full_from_slots(land, group, self.w)

    def put_grads(self, key, group, grads):
        send = _slots_from_full(grads, group, self.w)
        handles = exchange_start(send, f"reduce_{key}_start", True)
        self.reductions[key] = (group, handles)
        self.tokens.append(handles[4])

    def reduced(self, key, after):
        group, handles = self.reductions[key]
        send, land, _ = exchange_wait(handles, after, f"reduce_{key}_wait", True)
        mine = lax.dynamic_slice_in_dim(send, self.me, 1, axis=0)
        land = lax.dynamic_update_slice(land, mine, (self.me, 0, 0))
        return _own_from_sum(sum_slots(land, f"sum_{key}", FLAT_ROW_TILE), group, self.w)

    def reduced_small(self, small_grads):
        small = SMALL_SHARDED + REPLICATED
        gs = all_gather(_pack128([small_grads[n] for n in small]), "gather_small_grads")
        tot = _unpack128(sum_slots(gs, "sum_small", 1024), [small_grads[n].shape for n in small])
        out = {}
        for n, g in zip(small, tot):
            if n in SMALL_SHARDED:
                width = self.w[n].shape[-1]
                g = lax.dynamic_slice_in_dim(g, self.me * width, width, axis=g.ndim - 1)
            out[n] = g
        return out


def local_step(x, mem, target, ex):
    bsz, seq, _ = x.shape
    t = bsz * seq
    nb = t // TB
    nc = seq // CHUNK
    x0 = x.reshape(t, D)
    mem2 = mem.reshape(bsz * N_MEM, D)
    tgt = target.reshape(t, D)
    p = ex.small
    gains = p['norm_gains']
    big = dict(ex.weights('ab', None))

    def gain(layer, i):
        g = gains[layer, i].reshape(1, D)
        for tok in ex.take_tokens():
            g = g + tok[0, 0]
        return g

    consts = _ssd_consts()
    grads = {}
    saved = [dict(), dict()]

    def run_seg_res(xin, m, ga, gb, name):
        return fwd_call(seg_res, name, (nb,), [xin, m, ga, gb], [_rows(D), _rows(D), _par(D), _par(D)],
                        [_sd((t, D)), _sd((t, D), BF)], [_rows(D), _rows(D)])

    def attn_specs():
        nq = seq // TB
        q = pl.BlockSpec((TB, XA_DH), lambda b, h, i: (b * nq + i, h))
        k = pl.BlockSpec((N_MEM, XA_DH), lambda b, h, i: (b, h))
        v = pl.BlockSpec((N_MEM, XA_DH), lambda b, h, i: (b, XA_HEADS + h))
        return (bsz, XA_HEADS, nq), q, k, v

    def attention_fwd(layer, xin, hin, sv):
        q = matmul(hin, big[('xa_wq', layer)], 'nn', f"q_{layer}", BF)
        kv = matmul(mem2, big[('xa_wkv', layer)], 'nt', f"kv_{layer}", BF)
        grid, qs, ks, vs = attn_specs()
        o, = fwd_call(attn_fn, f"attn_{layer}", grid, [q, kv, kv], [qs, ks, vs], [_sd((t, D), BF)], [qs])
        ao = matmul(o, big[('xa_wo', layer)], 'nn', f"ao_{layer}")
        sv.update(q=q, kv=kv, o=o, ao=ao)
        return ao

    def mlp_fwd(layer, hin, sv):
        r, rr = matmul(hin, big[('mlp_w1', layer)], 'nt', f"mlp1_{layer}", (F32, BF), epilogue=act_epilogue)
        mo = matmul(rr, big[('mlp_w2', layer)], 'nn', f"mlp2_{layer}")
        sv.update(r=r, rr=rr, mo=mo)
        return mo

    sv = saved[0]
    h0, = fwd_call(seg_in, "norm_in", (nb,), [x0, gain(0, 0)], [_rows(D), _par(D)], [_sd((t, D), BF)], [_rows(D)])
    xbc0 = POOL_W + SSM_INNER
    w_ab_in = big[('ab_w_in', 0)]
    w_ab_in = _pad_rows(jnp.concatenate([w_ab_in[:xbc0], _xbc_group(w_ab_in[xbc0:xbc0 + SSM_CONV_DIM], 0),
                                         w_ab_in[xbc0 + SSM_CONV_DIM:]], axis=0), AB_IN_PAD)
    conv_w, conv_b = _xbc_group(p['ssm_conv_w'][0], 1), _xbc_group(p['ssm_conv_b'], 1)
    u0 = matmul(h0, w_ab_in, 'nt', "ab_in")
    pool_outs = []
    for g in range(POOL_GROUPS):
        seqspec = pl.BlockSpec((seq, PG), lambda b, g=g: (b, g))
        po, = fwd_call(make_pool_fn(g), f"pool_{g}", (bsz,), [u0, p['pool_w'][0, g], p['pool_scale']],
                       [seqspec, pl.BlockSpec((PG, PG), lambda b: (0, 0)), pl.BlockSpec((1, PG), lambda b, g=g: (0, g))],
                       [_sd((t, PG), BF)], [pl.BlockSpec((seq, PG), lambda b: (b, 0))])
        pool_outs.append(po)
    cw = 256
    ncb = SSM_CONV_DIM // cw
    cbase = (POOL_W + SSM_INNER) // cw
    conv_in_specs = [pl.BlockSpec((seq, cw), lambda j, b: (b, cbase + j)), pl.BlockSpec((SSM_CONV, cw), lambda j, b: (0, j)),
                     pl.BlockSpec((1, cw), lambda j, b: (0, j))]
    conv_out_spec = pl.BlockSpec((seq, cw), lambda j, b: (b, j))
    xbc_act, = fwd_call(conv4_fn, "ssm_conv", (ncb, bsz), [u0, conv_w, conv_b], conv_in_specs,
                        [_sd((t, SSM_CONV_DIM))], [conv_out_spec])
    dtb = jnp.pad(p['ssm_dt_bias'], ((0, 0), (0, LANE - SSM_HEADS)))
    alog = jnp.pad(p['ssm_a_log'], ((0, 0), (0, LANE - SSM_HEADS)))
    dsk = jnp.pad(p['ssm_d'], ((0, 0), (0, LANE - SSM_HEADS)))
    yn, hs = ssd_fwd(xbc_act, u0, dtb, alog, dsk, p['ssm_norm'], consts, bsz, seq)
    mix0 = jnp.concatenate(pool_outs + [yn], axis=1)
    m0 = matmul(mix0, big[('ab_w_out', 0)], 'nn', "ab_out")
    x1, h2 = run_seg_res(x0, m0, gain(0, 1), gain(0, 2), "res_0a")
    big.update(ex.weights('l0', h2))
    ao0 = attention_fwd(0, x1, h2, sv)
    x2, h3 = run_seg_res(x1, ao0, gain(0, 3), gain(0, 4), "res_0b")
    mo0 = mlp_fwd(0, h3, sv)
    big.update(ex.weights('cd', sv['r']))
    x3, h4 = run_seg_res(x2, mo0, gain(0, 5), gain(1, 0), "res_0c")

    sv1 = saved[1]
    nd = D // LANE
    w_cd_in = big[('cd_w_in', 0)].reshape(5, nd, LANE, D).transpose(1, 0, 2, 3).reshape(CD_IN, D)
    u1 = matmul(h4, w_cd_in, 'nt', "cd_in")
    cd_par = [pl.BlockSpec((CONF_K, LANE), lambda j, b: (0, j)), pl.BlockSpec((1, LANE), lambda j, b: (0, j)),
              pl.BlockSpec((SC_K, LANE), lambda j, b: (0, j))]
    cd_ins = [u1, p['conf_dw_w'][0], p['conf_dw_b'], p['sc_conv_w'][0]]
    cd_u_spec = pl.BlockSpec((seq, 5 * LANE), lambda j, b: (b, j))
    cd_in_specs = [cd_u_spec] + cd_par
    cd_out_spec = pl.BlockSpec((seq, LANE), lambda j, b: (b, j))
    vconv, sc_out = fwd_call(cd1_fn, "cd_conv", (nd, bsz), cd_ins, cd_in_specs, [_sd((t, D)), _sd((t, D), BF)],
                             [cd_out_spec, cd_out_spec])
    conf, = fwd_call(seg_ln, "conf_ln", (nb,), [vconv, p['conf_ln_g'], p['conf_ln_b']], [_rows(D), _par(D), _par(D)],
                     [_sd((t, D), BF)], [_rows(D)])
    mix1 = jnp.concatenate([conf, sc_out], axis=1)
    m1 = matmul(mix1, big[('cd_w_out', 0)], 'nn', "cd_out")
    x4, h5 = run_seg_res(x3, m1, gain(1, 1), gain(1, 2), "res_1a")
    big.update(ex.weights('l1', h5))
    ao1 = attention_fwd(1, x4, h5, sv1)
    x5, h6 = run_seg_res(x4, ao1, gain(1, 3), gain(1, 4), "res_1b")
    mo1 = mlp_fwd(1, h6, sv1)

    def loss_body(x_ref, m_ref, g_ref, t_ref, dy_ref, acc_ref):
        y = x_ref[...] + _rms(m_ref[...], g_ref[...])
        d = y - t_ref[...]
        dy_ref[...] = d / float(D)

        @pl.when(pl.program_id(0) == 0)
        def _():
            acc_ref[...] = jnp.zeros_like(acc_ref)

        acc_ref[...] += jnp.sum(d * d, axis=0, keepdims=True)

    dy, lanes = pl.pallas_call(
        loss_body, name="loss_head", grid=(nb,), in_specs=[_rows(D), _rows(D), _par(D), _rows(D)],
        out_specs=[_rows(D), _par(D)], out_shape=[_sd((t, D)), _sd((1, D))], compiler_params=_params())(x5, mo1, gain(1, 5), tgt)
    loss = 0.5 * jnp.sum(lanes) / float(D)

    gain_grads = {}

    def bwd_seg_out(xin, m, ga, dyv, name):
        dx, dm, dga = bwd_call(seg_out, name, (nb,), [xin, m, ga], [_rows(D), _rows(D), _par(D)], [dyv], [_rows(D)],
                               [0, 1, 2], [_sd((t, D)), _sd((t, D), BF), _sd((1, D))], [_rows(D), _rows(D), _par(D)],
                               [None, None, (0,)])
        return dx, dm, dga

    def bwd_seg_res(xin, m, ga, gb, dx1, dh, name):
        return bwd_call(seg_res, name, (nb,), [xin, m, ga, gb], [_rows(D), _rows(D), _par(D), _par(D)], [dx1, dh],
                        [_rows(D), _rows(D)], [0, 1, 2, 3], [_sd((t, D)), _sd((t, D), BF), _sd((1, D)), _sd((1, D))],
                        [_rows(D), _rows(D), _par(D), _par(D)], [None, None, (0,), (0,)])

    def mlp_bwd(layer, hin, dmo, sv):
        grads_w2 = matmul(sv['rr'], dmo, 'tn', f"d_mlp_w2_{layer}", BF)
        dr, = matmul(dmo, big[('mlp_w2', layer)], 'nt', f"d_r_{layer}", (BF,), epilogue=act_bwd_epilogue, extras=[sv['r']])
        grads_w1 = matmul(dr, hin, 'tn', f"d_mlp_w1_{layer}", BF)
        dh = matmul(dr, big[('mlp_w1', layer)], 'nn', f"d_h_mlp_{layer}")
        return dh, grads_w1, grads_w2

    def attention_bwd(layer, hin, dao, sv):
        g_wo = matmul(sv['o'], dao, 'tn', f"d_xa_wo_{layer}", BF)
        do = matmul(dao, big[('xa_wo', layer)], 'nt', f"d_o_{layer}", BF)
        grid, qs, ks, vs = attn_specs()
        kvo = pl.BlockSpec((N_MEM, XA_DH), lambda b, h, i: (b, h))
        dq, dk, dv = bwd_call(attn_fn, f"d_attn_{layer}", grid, [sv['q'], sv['kv'], sv['kv']], [qs, ks, vs], [do], [qs],
                              [0, 1, 2], [_sd((t, D), BF), _sd((bsz * N_MEM, D)), _sd((bsz * N_MEM, D))], [qs, kvo, kvo],
                              [None, (2,), (2,)])
        dkv = jnp.concatenate([dk, dv], axis=1)
        g_wkv = matmul(dkv, mem2, 'tn', f"d_xa_wkv_{layer}", BF)
        g_wq = matmul(hin, dq, 'tn', f"d_xa_wq_{layer}", BF)
        dh = matmul(dq, big[('xa_wq', layer)], 'nt', f"d_h_attn_{layer}")
        return dh, g_wq, g_wkv, g_wo

    per_layer = {k: [None, None] for k in ('xa_wq', 'xa_wkv', 'xa_wo', 'mlp_w1', 'mlp_w2')}

    dx5, dmo1, gain_grads[(1, 5)] = bwd_seg_out(x5, mo1, gain(1, 5), dy, "d_out")
    dh6, per_layer['mlp_w1'][1], per_layer['mlp_w2'][1] = mlp_bwd(1, h6, dmo1, sv1)
    dx4, dao1, gain_grads[(1, 3)], gain_grads[(1, 4)] = bwd_seg_res(x4, ao1, gain(1, 3), gain(1, 4), dx5, dh6, "d_res_1b")
    dh5, per_layer['xa_wq'][1], per_layer['xa_wkv'][1], per_layer['xa_wo'][1] = attention_bwd(1, h5, dao1, sv1)
    ex.put_grads('l1', G_L1, {(k, 1): v[1] for k, v in per_layer.items()})
    dx3, dm1, gain_grads[(1, 1)], gain_grads[(1, 2)] = bwd_seg_res(x3, m1, gain(1, 1), gain(1, 2), dx4, dh5, "d_res_1a")
    g_cd_out = matmul(mix1, dm1, 'tn', "d_cd_w_out", BF)
    dmix1 = matmul(dm1, big[('cd_w_out', 0)], 'nt', "d_mix1")
    dvconv, dlg, dlb = bwd_call(seg_ln, "d_conf_ln", (nb,), [vconv, p['conf_ln_g'], p['conf_ln_b']],
                                [_rows(D), _par(D), _par(D)], [dmix1], [_rows(D, 0)], [0, 1, 2],
                                [_sd((t, D)), _sd((1, D)), _sd((1, D))], [_rows(D), _par(D), _par(D)], [None, (0,), (0,)])
    grads['conf_ln_g'], grads['conf_ln_b'] = dlg, dlb
    cd_g = bwd_call(cd1_fn, "d_cd_conv", (nd, bsz), cd_ins, cd_in_specs, [dvconv, dmix1],
                    [cd_out_spec, pl.BlockSpec((seq, LANE), lambda j, b: (b, nd + j))], list(range(4)),
                    [_sd((t, CD_IN), BF), _sd((CONF_K, D)), _sd((1, D)), _sd((SC_K, D))], [cd_u_spec] + cd_par,
                    [None, (1,), (1,), (1,)])
    du1 = cd_g[0]
    grads['conf_dw_w'], grads['conf_dw_b'], grads['sc_conv_w'] = cd_g[1][None], cd_g[2], cd_g[3][None]
    g_cd_in = matmul(du1, h4, 'tn', "d_cd_w_in", BF).reshape(nd, 5, LANE, D).transpose(1, 0, 2, 3).reshape(CD_IN, D)
    ex.put_grads('cd', G_CD, {('cd_w_in', 0): g_cd_in, ('cd_w_out', 0): g_cd_out})
    dh4 = matmul(du1, w_cd_in, 'nn', "d_h_cd")

    dx2, dmo0, gain_grads[(0, 5)], gain_grads[(1, 0)] = bwd_seg_res(x2, mo0, gain(0, 5), gain(1, 0), dx3, dh4, "d_res_0c")
    dh3, per_layer['mlp_w1'][0], per_layer['mlp_w2'][0] = mlp_bwd(0, h3, dmo0, sv)
    dx1, dao0, gain_grads[(0, 3)], gain_grads[(0, 4)] = bwd_seg_res(x1, ao0, gain(0, 3), gain(0, 4), dx2, dh3, "d_res_0b")
    dh2, per_layer['xa_wq'][0], per_layer['xa_wkv'][0], per_layer['xa_wo'][0] = attention_bwd(0, h2, dao0, sv)
    ex.put_grads('l0', G_L0, {(k, 0): v[0] for k, v in per_layer.items()})
    dx0r, dm0, gain_grads[(0, 1)], gain_grads[(0, 2)] = bwd_seg_res(x0, m0, gain(0, 1), gain(0, 2), dx1, dh2, "d_res_0a")
    g_ab_out = matmul(mix0, dm0, 'tn', "d_ab_w_out", BF)
    dmix0 = matmul(dm0, big[('ab_w_out', 0)], 'nt', "d_mix0")
    dxbc_act, dz, ddt, ddtb, dalog, ddsk, dnw = ssd_bwd(xbc_act, u0, dtb, alog, dsk, p['ssm_norm'], consts, hs, dmix0, bsz, seq)
    grads['ssm_dt_bias'] = (ddtb[0] + ddtb[1])[:, :SSM_HEADS]
    grads['ssm_a_log'] = (dalog[0] + dalog[1])[:, :SSM_HEADS]
    grads['ssm_d'] = (ddsk[0] + ddsk[1])[:, :SSM_HEADS]
    grads['ssm_norm'] = dnw
    dxr, dcw, dcb = bwd_call(conv4_fn, "d_ssm_conv", (ncb, bsz), [u0, conv_w, conv_b], conv_in_specs,
                             [dxbc_act], [conv_out_spec], [0, 1, 2],
                             [_sd((t, SSM_CONV_DIM), BF), _sd((SSM_CONV, SSM_CONV_DIM)), _sd((1, SSM_CONV_DIM))],
                             [conv_out_spec, conv_in_specs[1], conv_in_specs[2]], [None, (1,), (1,)])
    grads['ssm_conv_w'], grads['ssm_conv_b'] = _xbc_ungroup(dcw, 1)[None], _xbc_ungroup(dcb, 1)
    dpool, dpw, dps = [], [], []
    for g in range(POOL_GROUPS):
        seqspec = pl.BlockSpec((seq, PG), lambda b, g=g: (b, g))
        one = pl.BlockSpec((seq, PG), lambda b: (b, 0))
        wspec = pl.BlockSpec((PG, PG), lambda b: (0, 0))
        sspec = pl.BlockSpec((1, PG), lambda b, g=g: (0, g))
        a, bb, c = bwd_call(make_pool_fn(g), f"d_pool_{g}", (bsz,), [u0, p['pool_w'][0, g], p['pool_scale']],
                            [seqspec, wspec, sspec], [dmix0], [seqspec], [0, 1, 2],
                            [_sd((t, PG), BF), _sd((PG, PG)), _sd((1, PG))], [one, wspec, pl.BlockSpec((1, PG), lambda b: (0, 0))],
                            [None, (0,), (0,)])
        dpool.append(a)
        dpw.append(bb)
        dps.append(c)
    grads['pool_w'] = jnp.stack(dpw)[None]
    grads['pool_scale'] = jnp.concatenate(dps, axis=1)
    du0 = jnp.concatenate(dpool + [dz, dxr, (ddt[0] + ddt[1]).astype(BF)], axis=1)
    g_ab_in = matmul(du0, h0, 'tn', "d_ab_w_in", BF)
    g_ab_in = jnp.concatenate([g_ab_in[:xbc0], _xbc_ungroup(g_ab_in[xbc0:xbc0 + SSM_CONV_DIM], 0),
                               g_ab_in[xbc0 + SSM_CONV_DIM:AB_IN]], axis=0)
    ex.put_grads('ab', G_AB, {('ab_w_in', 0): g_ab_in, ('ab_w_out', 0): g_ab_out})
    dh0 = matmul(du0, w_ab_in, 'nn', "d_h_ab")
    dx, dg00 = bwd_call(seg_in_res, "d_norm_in", (nb,), [x0, gain(0, 0)], [_rows(D), _par(D)], [dx0r, dh0],
                        [_rows(D), _rows(D)], [0, 1], [_sd((t, D)), _sd((1, D))], [_rows(D), _par(D)], [None, (0,)])
    gain_grads[(0, 0)] = dg00
    grads['norm_gains'] = jnp.stack([jnp.concatenate([gain_grads[(l, i)] for i in range(6)], axis=0) for l in range(2)])
    return loss, dx, grads
```

```python
import functools
import math

import numpy as np
import jax
import jax.numpy as jnp
from jax import lax
from jax.experimental import pallas as pl
from jax.experimental.pallas import tpu as pltpu

BF = jnp.bfloat16
F32 = jnp.float32
HI = lax.Precision.HIGHEST

N_DEV = 8
D = 1024
N_MEM = 256
XA_HEADS = 4
XA_DH = D // XA_HEADS
POOL_GROUPS = 4
PG = 128
POOL_W = POOL_GROUPS * PG
SSM_INNER = 1024
SSM_GROUPS = 2
SSM_GSZ = SSM_INNER // SSM_GROUPS
SSM_HEADS = 16
SSM_P = 64
SSM_N = 128
SSM_CONV = 4
SSM_CONV_DIM = SSM_INNER + 2 * SSM_GROUPS * SSM_N
SSM_XBC_G = SSM_GSZ + 2 * SSM_N
CHUNK = 128
AB_IN = POOL_W + SSM_INNER + SSM_CONV_DIM + SSM_HEADS
AB_IN_PAD = POOL_W + SSM_INNER + SSM_CONV_DIM + 128
AB_OUT = POOL_W + SSM_INNER
CONF_K = 31
SC_K = 3
CD_IN = 5 * D
CD_OUT = 2 * D
MLP_H = 4 * D
RMS_EPS = 1e-6
LN_EPS = 1e-5
ADAM_LR = 0.001
ADAM_B1 = 0.9
ADAM_B2 = 0.999
ADAM_EPS = 1e-08
ADAM_WD = 0.01
ADAM_STEP = 10
VMEM_LIMIT = 56 * 1024 * 1024
LANE = 128

NAMES = ['x', 'mem', 'norm_gains', 'xa_wq', 'xa_wkv', 'xa_wo', 'mlp_w1', 'mlp_w2', 'ab_w_in', 'pool_w', 'pool_scale',
         'ssm_conv_w', 'ssm_conv_b', 'ssm_dt_bias', 'ssm_a_log', 'ssm_d', 'ssm_norm', 'ab_w_out', 'cd_w_in', 'conf_dw_w',
         'conf_dw_b', 'conf_ln_g', 'conf_ln_b', 'sc_conv_w', 'cd_w_out', 'loss_target']
WEIGHTS = NAMES[2:25]
BIG = [('xa_wq', 1), ('xa_wkv', 2), ('xa_wo', 1), ('mlp_w1', 2), ('mlp_w2', 1), ('cd_w_in', 2), ('cd_w_out', 1),
       ('ab_w_out', 1), ('ab_w_in', 2)]
SMALL_SHARDED = ['norm_gains', 'ssm_conv_w', 'conf_dw_w', 'conf_dw_b', 'conf_ln_g', 'conf_ln_b', 'sc_conv_w']
REPLICATED = ['pool_w', 'pool_scale', 'ssm_conv_b', 'ssm_dt_bias', 'ssm_a_log', 'ssm_d', 'ssm_norm']


def _dg(a, b, ca, cb, prec=None):
    return lax.dot_general(a, b, (((ca,), (cb,)), ((), ())), precision=prec, preferred_element_type=F32)


@functools.partial(jax.custom_vjp, nondiff_argnums=(2, 3))
def bdot(a, b, ca, cb):
    return _dg(a.astype(BF), b.astype(BF), ca, cb)


def _bdot_fwd(a, b, ca, cb):
    return bdot(a, b, ca, cb), (a, b)


def _bdot_bwd(ca, cb, res, g):
    a, b = res
    g16, a16, b16 = g.astype(BF), a.astype(BF), b.astype(BF)
    da = _dg(g16, b16, 1, 1 - cb) if ca == 1 else _dg(b16, g16, 1 - cb, 1)
    db = _dg(g16, a16, 0, 1 - ca) if cb == 1 else _dg(a16, g16, 1 - ca, 0)
    return da.astype(a.dtype), db.astype(b.dtype)


bdot.defvjp(_bdot_fwd, _bdot_bwd)


def _split3(a):
    a1 = a.astype(BF)
    r1 = a - a1.astype(F32)
    a2 = r1.astype(BF)
    a3 = (r1 - a2.astype(F32)).astype(BF)
    return a1, a2, a3


def _exact_right(a, c):
    m = a.shape[0]
    if m % 16:
        return sum(_dg(p, c, 1, 0) for p in _split3(a))
    o = _dg(jnp.concatenate(_split3(a), axis=0), c, 1, 0)
    return o[:m] + o[m:2 * m] + o[2 * m:]


def _exact_left(c, a):
    n = a.shape[1]
    o = _dg(c, jnp.concatenate(_split3(a), axis=1), 1, 0)
    return o[:, :n] + o[:, n:2 * n] + o[:, 2 * n:]


@jax.custom_vjp
def cmat(a, c, ct):
    return _exact_right(a, c)


def _cmat_fwd(a, c, ct):
    return cmat(a, c, ct), (c, ct)


def _cmat_bwd(res, g):
    c, ct = res
    return _exact_right(g, ct), jnp.zeros_like(c), jnp.zeros_like(ct)


cmat.defvjp(_cmat_fwd, _cmat_bwd)


@jax.custom_vjp
def cmatl(c, ct, a):
    return _exact_left(c, a)


def _cmatl_fwd(c, ct, a):
    return cmatl(c, ct, a), (c, ct)


def _cmatl_bwd(res, g):
    c, ct = res
    return jnp.zeros_like(c), jnp.zeros_like(ct), _exact_left(ct, g)


cmatl.defvjp(_cmatl_fwd, _cmatl_bwd)


SUBLANES = 8


def _taps(x, shifts, down):
    n, c = x.shape
    pad = _round_up(max(shifts), SUBLANES)
    if pad == 0:
        return {0: x}
    zeros = jnp.zeros((pad, c), x.dtype)
    xp = jnp.concatenate([zeros, x] if down else [x, zeros], axis=0)
    rolled, out = {0: xp}, {}
    for s in shifts:
        a, b = divmod(s, SUBLANES)
        if b not in rolled:
            rolled[b] = pltpu.roll(xp, b if down else n + pad - b, 0)
        off = pad - SUBLANES * a if down else SUBLANES * a
        out[s] = rolled[b][off:off + n]
    return out


def _shift_down(x, k):
    return _taps(x, [k], True)[k]


def _shift_up(x, k):
    return _taps(x, [k], False)[k]


@functools.partial(jax.custom_vjp, nondiff_argnums=(1,))
def shift(x, k):
    return _shift_down(x, k)


def _shift_fwd(x, k):
    return _shift_down(x, k), None


def _shift_bwd(k, _, g):
    return (_shift_up(g, k),)


shift.defvjp(_shift_fwd, _shift_bwd)


@functools.partial(jax.custom_vjp, nondiff_argnums=(2,))
def cconv(u, w, width):
    taps = _taps(u, list(range(width)), True)
    acc = u * w[width - 1:width, :]
    for k in range(width - 1):
        acc = acc + taps[width - 1 - k] * w[k:k + 1, :]
    return acc


def _cconv_fwd(u, w, width):
    return cconv(u, w, width), (u, w)


def _cconv_bwd(width, res, g):
    u, w = res
    rows = lax.broadcasted_iota(jnp.int32, w.shape, 0)
    du = g * w[width - 1:width, :]
    dw = jnp.where(rows == width - 1, jnp.sum(g * u, axis=0, keepdims=True), 0.0)
    g_taps = _taps(g, list(range(width)), False)
    u_taps = _taps(u, list(range(width)), True)
    for k in range(width - 1):
        s = width - 1 - k
        du = du + g_taps[s] * w[k:k + 1, :]
        dw = dw + jnp.where(rows == k, jnp.sum(g * u_taps[s], axis=0, keepdims=True), 0.0)
    return du, dw


cconv.defvjp(_cconv_fwd, _cconv_bwd)


def _rms(x, g):
    return x * lax.rsqrt(jnp.mean(x * x, axis=-1, keepdims=True) + RMS_EPS) * g


def _params(sem=None):
    return pltpu.CompilerParams(dimension_semantics=sem, vmem_limit_bytes=VMEM_LIMIT)


def _f32(v):
    return v if v.dtype == F32 else v.astype(F32)


def _first(axes):
    ok = None
    for ax in axes:
        c = pl.program_id(ax) == 0
        ok = c if ok is None else jnp.logical_and(ok, c)
    return ok


def fwd_call(fn, name, grid, ins, in_specs, out_shapes, out_specs):
    n_in = len(ins)

    def body(*refs):
        outs = fn(*[_f32(r[...]) for r in refs[:n_in]])
        for r, o in zip(refs[n_in:], outs):
            r[...] = o.astype(r.dtype)

    return pl.pallas_call(body, name=name, grid=grid, in_specs=in_specs, out_specs=out_specs, out_shape=out_shapes,
                          compiler_params=_params())(*ins)


def bwd_call(fn, name, grid, ins, in_specs, cots, cot_specs, gidx, g_shapes, g_specs, g_acc):
    n_in, n_cot = len(ins), len(cots)

    def body(*refs):
        vals = [_f32(r[...]) for r in refs[:n_in]]

        def f_sel(*dv):
            full = list(vals)
            for i, v in zip(gidx, dv):
                full[i] = v
            return tuple(fn(*full))

        outs, vjp = jax.vjp(f_sel, *[vals[i] for i in gidx])
        cts = tuple(_f32(r[...]) for r in refs[n_in:n_in + n_cot])
        grads = vjp(cts)
        for r, g, acc in zip(refs[n_in + n_cot:], grads, g_acc):
            if acc is None:
                r[...] = g.astype(r.dtype)
            else:
                @pl.when(_first(acc))
                def _():
                    r[...] = jnp.zeros_like(r)

                r[...] += g.astype(r.dtype)

    return pl.pallas_call(body, name=name, grid=grid, in_specs=list(in_specs) + list(cot_specs), out_specs=g_specs,
                          out_shape=g_shapes, compiler_params=_params())(*ins, *cots)


def _tile(dim, pref):
    if dim <= pref:
        return dim
    best = None
    for t in range(LANE, pref + 1, LANE):
        if dim % t == 0:
            best = t
    assert best is not None, dim
    return best


MATMUL_VMEM_BUDGET = 40 * 1024 * 1024


def _matmul_tiles(m, n, k, a_bytes, b_bytes, out_bytes):
    tn = _tile(n, 1024)
    for tk_pref in (k, 2048, 1024, 512):
        tk = _tile(k, tk_pref)
        for tm_pref in (1024, 512, 256):
            tm = _tile(m, tm_pref)
            need = 2 * (tm * tk * a_bytes + tk * tn * b_bytes + tm * tn * out_bytes) + (0 if tk == k else tm * tn * 4)
            need += (tm * tk * 2 if a_bytes == 4 else 0) + (tk * tn * 2 if b_bytes == 4 else 0)
            if need <= MATMUL_VMEM_BUDGET:
                return tm, tn, tk
    raise ValueError((m, n, k))


def matmul(a, b, mode, name, out_dtype=F32, epilogue=None, extras=()):
    if mode == 'nn':
        (m, k), (k2, n) = a.shape, b.shape
    elif mode == 'nt':
        (m, k), (n, k2) = a.shape, b.shape
    else:
        (k, m), (k2, n) = a.shape, b.shape
    assert k == k2, (name, a.shape, b.shape)
    n_extra = len(extras)
    out_dtypes = out_dtype if isinstance(out_dtype, tuple) else (out_dtype,)
    per_out = sum(jnp.dtype(dt).itemsize for dt in out_dtypes) + sum(e.dtype.itemsize for e in extras)
    tm, tn, tk = _matmul_tiles(m, n, k, a.dtype.itemsize, b.dtype.itemsize, per_out)
    nk = k // tk
    ca = 0 if mode == 'tn' else 1
    cb = 1 if mode == 'nt' else 0
    a_spec = pl.BlockSpec((tk, tm), lambda i, j, kk: (kk, i)) if mode == 'tn' else pl.BlockSpec((tm, tk), lambda i, j, kk: (i, kk))
    b_spec = pl.BlockSpec((tn, tk), lambda i, j, kk: (j, kk)) if mode == 'nt' else pl.BlockSpec((tk, tn), lambda i, j, kk: (kk, j))

    def finish(o_refs, extra_refs, acc):
        outs = (acc,) if epilogue is None else epilogue(acc, *[_f32(e[...]) for e in extra_refs])
        for o_ref, o in zip(o_refs, outs):
            o_ref[...] = o.astype(o_ref.dtype)

    def body_whole_k(a_ref, b_ref, *refs):
        finish(refs[n_extra:], refs[:n_extra], _dg(a_ref[...].astype(BF), b_ref[...].astype(BF), ca, cb))

    def body_split_k(a_ref, b_ref, *refs):
        extra_refs, o_refs, acc = refs[:n_extra], refs[n_extra:-1], refs[-1]
        kk = pl.program_id(2)

        @pl.when(kk == 0)
        def _():
            acc[...] = jnp.zeros_like(acc)

        acc[...] += _dg(a_ref[...].astype(BF), b_ref[...].astype(BF), ca, cb)

        @pl.when(kk == nk - 1)
        def _():
            finish(o_refs, extra_refs, acc[...])

    tile = pl.BlockSpec((tm, tn), lambda i, j, kk: (i, j))
    outs = pl.pallas_call(
        body_whole_k if nk == 1 else body_split_k, name=name, grid=(m // tm, n // tn, nk),
        in_specs=[a_spec, b_spec] + [tile] * n_extra, out_specs=[tile] * len(out_dtypes),
        out_shape=[jax.ShapeDtypeStruct((m, n), dt) for dt in out_dtypes],
        scratch_shapes=[] if nk == 1 else [pltpu.VMEM((tm, tn), F32)],
        compiler_params=_params(("parallel", "parallel", "arbitrary")))(a, b, *extras)
    return outs if isinstance(out_dtype, tuple) else outs[0]


_FLIPS = [(0, 0, 1), (1, 0, 0), (0, 1, 0), (1, 1, 0), (1, 0, 1), (0, 1, 1), (1, 1, 1)]


def _me():
    return lax.axis_index("x"), lax.axis_index("y"), lax.axis_index("c")


def _flip(pos, f):
    return tuple(jnp.where(fi == 1, 1 - p, p) if fi else p for p, fi in zip(pos, f))


def _slot(pos):
    return 4 * pos[0] + 2 * pos[1] + pos[2]


def all_gather(v, name):
    def body(v_ref, out_ref, send_sems, recv_sems, local_sem):
        me = _me()
        sibling = _flip(me, (0, 0, 1))
        chips = [_flip(me, f) for f in ((1, 0, 0), (0, 1, 0), (1, 1, 0))]

        def copy(k, block, to, src=None):
            return pltpu.make_async_remote_copy(
                src_ref=out_ref.at[_slot(block)] if src is None else src, dst_ref=out_ref.at[_slot(block)],
                send_sem=send_sems.at[k], recv_sem=recv_sems.at[k], device_id=to, device_id_type=pl.DeviceIdType.MESH)

        mine = pltpu.make_async_copy(v_ref, out_ref.at[_slot(me)], local_sem)
        mine.start()
        first = [copy(0, me, sibling, src=v_ref)] + [copy(1 + j, me, chip, src=v_ref) for j, chip in enumerate(chips)]
        for cp in first:
            cp.start()
        passed = [copy(4 + j, chip, sibling) for j, chip in enumerate(chips)]
        for j, chip in enumerate(chips):
            copy(1 + j, chip, me).wait_recv()
            passed[j].start()
        copy(0, sibling, me).wait_recv()
        for j, chip in enumerate(chips):
            copy(4 + j, _flip(chip, (0, 0, 1)), me).wait_recv()
        for cp in first + passed:
            cp.wait_send()
        mine.wait()

    return pl.pallas_call(
        body, name=name, out_shape=jax.ShapeDtypeStruct((N_DEV,) + v.shape, v.dtype),
        in_specs=[pl.BlockSpec(memory_space=pl.ANY)], out_specs=pl.BlockSpec(memory_space=pl.ANY),
        scratch_shapes=[pltpu.SemaphoreType.DMA((7,)), pltpu.SemaphoreType.DMA((7,)), pltpu.SemaphoreType.DMA(())],
    )(v)


def all_to_all(v, name):
    def body(v_ref, out_ref, send_sems, recv_sems, local_sem):
        me = _me()
        mine = pltpu.make_async_copy(v_ref.at[_slot(me)], out_ref.at[_slot(me)], local_sem)
        mine.start()
        copies = []
        for k, f in enumerate(_FLIPS):
            peer = _flip(me, f)
            cp = pltpu.make_async_remote_copy(
                src_ref=v_ref.at[_slot(peer)], dst_ref=out_ref.at[_slot(me)], send_sem=send_sems.at[k],
                recv_sem=recv_sems.at[k], device_id=peer, device_id_type=pl.DeviceIdType.MESH)
            cp.start()
            copies.append(cp)
        for k, f in enumerate(_FLIPS):
            peer = _flip(me, f)
            pltpu.make_async_remote_copy(
                src_ref=v_ref.at[_slot(peer)], dst_ref=out_ref.at[_slot(peer)], send_sem=send_sems.at[k],
                recv_sem=recv_sems.at[k], device_id=peer, device_id_type=pl.DeviceIdType.MESH).wait_recv()
        for cp in copies:
            cp.wait_send()
        mine.wait()

    return pl.pallas_call(
        body, name=name, out_shape=jax.ShapeDtypeStruct(v.shape, v.dtype),
        in_specs=[pl.BlockSpec(memory_space=pl.ANY)], out_specs=pl.BlockSpec(memory_space=pl.ANY),
        scratch_shapes=[pltpu.SemaphoreType.DMA((7,)), pltpu.SemaphoreType.DMA((7,)), pltpu.SemaphoreType.DMA(())],
    )(v)


def sum_slots(v, name, tr=256):
    _, r, c = v.shape
    tr = _tile_rows(r, tr)

    def body(v_ref, o_ref):
        acc = v_ref[0].astype(F32)
        for s in range(1, N_DEV):
            acc = acc + v_ref[s].astype(F32)
        o_ref[...] = acc

    return pl.pallas_call(body, name=name, grid=(r // tr,), in_specs=[pl.BlockSpec((N_DEV, tr, c), lambda i: (0, i, 0))],
                          out_specs=pl.BlockSpec((tr, c), lambda i: (i, 0)), out_shape=jax.ShapeDtypeStruct((r, c), F32),
                          compiler_params=_params())(v)


def _tile_rows(r, pref):
    if r <= pref:
        return r
    best = None
    for t in range(8, pref + 1, 8):
        if r % t == 0:
            best = t
    return r if best is None else best


def adamw(w, m, v, g, name):
    r, c = w.shape
    tr = _tile_rows(r, 512 if c <= 1024 else 128)

    def body(w_ref, m_ref, v_ref, g_ref, d_ref, nm_ref, nv_ref):
        gg = g_ref[...]
        nm = ADAM_B1 * m_ref[...] + (1.0 - ADAM_B1) * gg
        nv = ADAM_B2 * v_ref[...] + (1.0 - ADAM_B2) * jnp.square(gg)
        m_hat = nm / (1.0 - ADAM_B1 ** ADAM_STEP)
        v_hat = nv / (1.0 - ADAM_B2 ** ADAM_STEP)
        d_ref[...] = -ADAM_LR * (m_hat / (jnp.sqrt(v_hat) + ADAM_EPS) + ADAM_WD * w_ref[...])
        nm_ref[...] = nm
        nv_ref[...] = nv

    spec = pl.BlockSpec((tr, c), lambda i: (i, 0))
    sh = jax.ShapeDtypeStruct((r, c), F32)
    return pl.pallas_call(body, name=name, grid=(r // tr,), in_specs=[spec] * 4, out_specs=[spec] * 3,
                          out_shape=[sh] * 3, compiler_params=_params())(w, m, v, g)


def seg_in(x, g):
    return (_rms(x, g),)


def seg_in_res(x, g):
    return x, _rms(x, g)


def seg_res(x, m, ga, gb):
    x1 = x + _rms(m, ga)
    return x1, _rms(x1, gb)


def seg_out(x, m, ga):
    return (x + _rms(m, ga),)


def act_epilogue(r):
    t = jnp.maximum(r, 0.0)
    return r, t * t


def act_bwd_epilogue(drr, r):
    return (drr * (2.0 * jnp.maximum(r, 0.0)),)


def seg_ln(v, g, b):
    mu = jnp.mean(v, axis=-1, keepdims=True)
    var = jnp.mean(jnp.square(v - mu), axis=-1, keepdims=True)
    vn = (v - mu) * lax.rsqrt(var + LN_EPS) * g + b
    return (jax.nn.silu(vn),)


def make_pool_fn(group):
    window = 2 ** (group + 1)

    def pool_fn(ug, pw, scale):
        s = ug
        for lvl in range(group + 1):
            s = s + shift(s, 2 ** lvl)
        cnt = jnp.minimum(lax.broadcasted_iota(jnp.int32, ug.shape, 0) + 1, window).astype(F32)
        return (bdot(s / cnt - ug, pw, 1, 0) * scale,)

    return pool_fn


def conv4_fn(xr, w, b):
    return (jax.nn.silu(cconv(xr, w, SSM_CONV) + b),)


def cd1_fn(u, dww, dwb, scw):
    val, gate, bg, cg, hh = (u[:, k * LANE:(k + 1) * LANE] for k in range(5))
    v = val * jax.nn.sigmoid(gate)
    vc = cconv(v, dww, CONF_K) + dwb
    sc = bg * cconv(cg * hh, scw, SC_K)
    return vc, sc


def attn_fn(q, kv):
    outs = []
    for h in range(XA_HEADS):
        cols = slice(h * XA_DH, (h + 1) * XA_DH)
        s = bdot(q[:, cols], kv[:, cols], 1, 1) / math.sqrt(XA_DH)
        p = jax.nn.softmax(s, axis=-1)
        outs.append(bdot(p, kv[:, D + h * XA_DH:D + (h + 1) * XA_DH], 1, 0))
    return (jnp.concatenate(outs, axis=1),)


def ssd_chunk(xbc, z, dtraw, dtb, alog, dsk, nw, h0, h1, h2, h3, e64, e64t, ecat, ecatt, tril, trilt):
    xs, bm, cm = xbc[:, :SSM_GSZ], xbc[:, SSM_GSZ:SSM_GSZ + SSM_N], xbc[:, SSM_GSZ + SSM_N:]
    hin = (h0, h1, h2, h3)
    dt = jax.nn.softplus(dtraw + dtb)
    a = -jnp.exp(alog)
    d_a = dt * a
    cs = cmatl(tril, trilt, d_a)
    cs_cat = cmat(cs, ecat, ecatt)
    cs64, cs128 = cs_cat[:, :SSM_GSZ], cs_cat[:, SSM_GSZ:]
    dt64 = cmat(dt, e64, e64t)
    row = lax.broadcasted_iota(jnp.int32, (8, LANE), 0)
    heads = jnp.where(row == 0, dsk, jnp.where(row == 1, jnp.sum(d_a, axis=0, keepdims=True), 0.0))
    heads64 = cmat(heads, e64, e64t)
    d64, tot64 = heads64[0:1, :], heads64[1:2, :]
    xdt = xs * dt64
    cb = bdot(cm, bm, 1, 1)
    li = lax.broadcasted_iota(jnp.int32, (CHUNK, CHUNK), 0)
    si = lax.broadcasted_iota(jnp.int32, (CHUNK, CHUNK), 1)
    causal = li >= si
    lane = lax.broadcasted_iota(jnp.int32, (CHUNK, LANE), 1)
    xw = xdt * jnp.exp(tot64 - cs64)
    ecs = jnp.exp(cs64)
    etot = jnp.exp(tot64)
    ycols, hout = [], []
    for j in range(4):
        sl = slice(j * LANE, (j + 1) * LANE)
        xj = xdt[:, sl]
        ys = []
        for hh in range(2):
            r = 2 * j + hh
            col = cs128[:, r * LANE:(r + 1) * LANE]
            decay = jnp.exp(jnp.where(causal, col - col.T, -1e30))
            ys.append(bdot(cb * decay, xj, 1, 0))
        y_diag = jnp.where(lane < SSM_P, ys[0], ys[1])
        y_off = bdot(cm, hin[j], 1, 0) * ecs[:, sl]
        ycols.append(y_diag + y_off)
        hout.append(etot[:, sl] * hin[j] + bdot(bm, xw[:, sl], 0, 0))
    y = jnp.concatenate(ycols, axis=1) + d64 * xs
    y = y * jax.nn.silu(z)
    yn = y * lax.rsqrt(jnp.mean(y * y, axis=-1, keepdims=True) + RMS_EPS) * nw
    return (yn,) + tuple(hout)


def _xbc_group(a, axis):
    parts = []
    for g in range(SSM_GROUPS):
        for start, width in ((g * SSM_GSZ, SSM_GSZ), (SSM_INNER + g * SSM_N, SSM_N), (SSM_INNER + (SSM_GROUPS + g) * SSM_N, SSM_N)):
            parts.append(lax.slice_in_dim(a, start, start + width, axis=axis))
    return jnp.concatenate(parts, axis=axis)


def _xbc_ungroup(a, axis):
    xs, bs, cs = [], [], []
    for g in range(SSM_GROUPS):
        base = g * SSM_XBC_G
        xs.append(lax.slice_in_dim(a, base, base + SSM_GSZ, axis=axis))
        bs.append(lax.slice_in_dim(a, base + SSM_GSZ, base + SSM_GSZ + SSM_N, axis=axis))
        cs.append(lax.slice_in_dim(a, base + SSM_GSZ + SSM_N, base + SSM_XBC_G, axis=axis))
    return jnp.concatenate(xs + bs + cs, axis=axis)


def _ssd_consts():
    h = np.arange(LANE)[:, None]
    e64 = np.stack([(h == g * 8 + np.arange(SSM_GSZ)[None, :] // SSM_P) for g in range(SSM_GROUPS)]).astype(np.float32)
    e128 = np.stack([(h == g * 8 + np.arange(8 * LANE)[None, :] // LANE) for g in range(SSM_GROUPS)]).astype(np.float32)
    ecat = np.concatenate([e64, e128], axis=2)
    tril = np.tril(np.ones((CHUNK, CHUNK), np.float32))
    return tuple(jnp.asarray(c, dtype=BF) for c in (e64, e64.transpose(0, 2, 1), ecat, ecat.transpose(0, 2, 1), tril, tril.T))


def _ssd_specs(nc, rev):
    def ci(c):
        return nc - 1 - c if rev else c

    def row(width, col):
        return pl.BlockSpec((CHUNK, width), lambda b, c: (b * nc + ci(c), col))

    def whole(shape):
        return pl.BlockSpec(shape, lambda b, c: (0,) * len(shape))

    data = [row(SSM_CONV_DIM, 0),
            row(SSM_GSZ, 1), row(SSM_GSZ, 2), row(LANE, 24)]
    par = [whole((1, LANE))] * 3 + [whole((1, SSM_INNER))]
    cst = [whole((SSM_GROUPS, LANE, SSM_GSZ)), whole((SSM_GROUPS, SSM_GSZ, LANE)), whole((SSM_GROUPS, LANE, 12 * LANE)),
           whole((SSM_GROUPS, 12 * LANE, LANE)), whole((CHUNK, CHUNK)), whole((CHUNK, CHUNK))]
    hsave = pl.BlockSpec((None, None, SSM_GROUPS, 4, SSM_N, LANE), lambda b, c: (b, ci(c), 0, 0, 0, 0))
    return data, par, cst, hsave, row, whole


def _ssd_group_args(g, xbc, z, dtr, dtb, alog, dsk, nw):
    return (xbc[:, g * SSM_XBC_G:(g + 1) * SSM_XBC_G], z[g], dtr, dtb, alog, dsk, nw[:, g * SSM_GSZ:(g + 1) * SSM_GSZ])


def ssd_fwd(xbc_act, u, dtb, alog, dsk, nw, consts, bsz, seq):
    nc = seq // CHUNK
    data, par, cst, hsave, row, _ = _ssd_specs(nc, False)

    def body(xbc, z0, z1, dtr, dtb_r, alog_r, dsk_r, nw_r, e64, e64t, ecat, ecatt, tril, trilt, yn_ref, hs_ref, h):
        @pl.when(pl.program_id(1) == 0)
        def _():
            h[...] = jnp.zeros_like(h)

        hs_ref[...] = h[...]
        ys = []
        for g in range(SSM_GROUPS):
            args = _ssd_group_args(g, xbc[...], (z0[...], z1[...]), dtr[...], dtb_r[...], alog_r[...], dsk_r[...], nw_r[...])
            outs = ssd_chunk(*args, h[g, 0], h[g, 1], h[g, 2], h[g, 3], e64[g], e64t[g], ecat[g], ecatt[g], tril[...], trilt[...])
            ys.append(outs[0])
            for j in range(4):
                h[g, j] = outs[1 + j]
        yn_ref[...] = jnp.concatenate(ys, axis=1).astype(yn_ref.dtype)

    t = bsz * seq
    return pl.pallas_call(
        body, name="ssd_fwd", grid=(bsz, nc), in_specs=data + par + cst, out_specs=[row(SSM_INNER, 0), hsave],
        out_shape=[jax.ShapeDtypeStruct((t, SSM_INNER), BF), jax.ShapeDtypeStruct((bsz, nc, SSM_GROUPS, 4, SSM_N, LANE), F32)],
        scratch_shapes=[pltpu.VMEM((SSM_GROUPS, 4, SSM_N, LANE), F32)], compiler_params=_params(),
    )(xbc_act, u, u, u, dtb, alog, dsk, nw, *consts)


def ssd_bwd(xbc_act, u, dtb, alog, dsk, nw, consts, hs, dmix, bsz, seq):
    nc = seq // CHUNK
    data, par, cst, hsave, row, whole = _ssd_specs(nc, True)
    t = bsz * seq
    pcol = POOL_W // SSM_GSZ

    def body(xbc, z0, z1, dtr, dtb_r, alog_r, dsk_r, nw_r, e64, e64t, ecat, ecatt, tril, trilt, hs_ref, dy0, dy1,
             dxbc, dz, ddt, ddtb, dalog, ddsk, dnw, dh):
        @pl.when(pl.program_id(1) == 0)
        def _():
            dh[...] = jnp.zeros_like(dh)

        per_group = []
        for g, dyn in enumerate((dy0, dy1)):
            cst_vals = (e64[g], e64t[g], ecat[g], ecatt[g], tril[...], trilt[...])
            prim = _ssd_group_args(g, xbc[...], (z0[...], z1[...]), dtr[...], dtb_r[...], alog_r[...], dsk_r[...], nw_r[...])
            prim = prim + (hs_ref[g, 0], hs_ref[g, 1], hs_ref[g, 2], hs_ref[g, 3])
            _, vjp = jax.vjp(lambda *args, c=cst_vals: ssd_chunk(*args, *c), *prim)
            gr = vjp((dyn[...].astype(F32), dh[g, 0], dh[g, 1], dh[g, 2], dh[g, 3]))
            for j in range(4):
                dh[g, j] = gr[7 + j]
            per_group.append(gr)
        g0, g1 = per_group
        dxbc[...] = jnp.concatenate([g0[0], g1[0]], axis=1)
        dz[...] = jnp.concatenate([g0[1], g1[1]], axis=1).astype(dz.dtype)
        ddt[...] = g0[2] + g1[2]

        @pl.when(_first((0, 1)))
        def _():
            for r in (ddtb, dalog, ddsk, dnw):
                r[...] = jnp.zeros_like(r)

        ddtb[...] += g0[3] + g1[3]
        dalog[...] += g0[4] + g1[4]
        ddsk[...] += g0[5] + g1[5]
        dnw[...] += jnp.concatenate([g0[6], g1[6]], axis=1)

    out_specs = [row(SSM_CONV_DIM, 0), row(SSM_INNER, 0), row(LANE, 0), whole((1, LANE)), whole((1, LANE)), whole((1, LANE)),
                 whole((1, SSM_INNER))]
    lane = jax.ShapeDtypeStruct((1, LANE), F32)
    out_shape = [jax.ShapeDtypeStruct((t, SSM_CONV_DIM), F32), jax.ShapeDtypeStruct((t, SSM_INNER), BF),
                 jax.ShapeDtypeStruct((t, LANE), F32), lane, lane, lane, jax.ShapeDtypeStruct((1, SSM_INNER), F32)]
    return pl.pallas_call(
        body, name="ssd_bwd", grid=(bsz, nc), in_specs=data + par + cst + [hsave, row(SSM_GSZ, pcol), row(SSM_GSZ, pcol + 1)],
        out_specs=out_specs, out_shape=out_shape, scratch_shapes=[pltpu.VMEM((SSM_GROUPS, 4, SSM_N, LANE), F32)],
        compiler_params=_params(),
    )(xbc_act, u, u, u, dtb, alog, dsk, nw, *consts, hs, dmix, dmix)


TB = 512


def _rows(d, col=0):
    return pl.BlockSpec((TB, d), lambda i: (i, col))


def _par(d):
    return pl.BlockSpec((1, d), lambda i: (0, 0))


def _sd(shape, dtype=F32):
    return jax.ShapeDtypeStruct(shape, dtype)


def _round_up(n, m):
    return -(-n // m) * m


def _pad_rows(a, rows):
    return jnp.pad(a, ((0, rows - a.shape[0]), (0, 0)))


def _pack128(arrs):
    flat = jnp.concatenate([a.reshape(-1) for a in arrs])
    n = flat.shape[0]
    rows = -(-n // (8 * LANE)) * 8
    return jnp.pad(flat, (0, rows * LANE - n)).reshape(rows, LANE)


def _unpack128(packed, shapes):
    flat = packed.reshape(-1)
    out, off = [], 0
    for s in shapes:
        n = int(np.prod(s))
        out.append(flat[off:off + n].reshape(s))
        off += n
    return out


def kernel(x, mem, norm_gains, xa_wq, xa_wkv, xa_wo, mlp_w1, mlp_w2, ab_w_in, pool_w, pool_scale, ssm_conv_w, ssm_conv_b, ssm_dt_bias, ssm_a_log, ssm_d, ssm_norm, ab_w_out, cd_w_in, conf_dw_w, conf_dw_b, conf_ln_g, conf_ln_b, sc_conv_w, cd_w_out, loss_target, m_norm_gains, m_xa_wq, m_xa_wkv, m_xa_wo, m_mlp_w1, m_mlp_w2, m_ab_w_in, m_pool_w, m_pool_scale, m_ssm_conv_w, m_ssm_conv_b, m_ssm_dt_bias, m_ssm_a_log, m_ssm_d, m_ssm_norm, m_ab_w_out, m_cd_w_in, m_conf_dw_w, m_conf_dw_b, m_conf_ln_g, m_conf_ln_b, m_sc_conv_w, m_cd_w_out, v_norm_gains, v_xa_wq, v_xa_wkv, v_xa_wo, v_mlp_w1, v_mlp_w2, v_ab_w_in, v_pool_w, v_pool_scale, v_ssm_conv_w, v_ssm_conv_b, v_ssm_dt_bias, v_ssm_a_log, v_ssm_d, v_ssm_norm, v_ab_w_out, v_cd_w_in, v_conf_dw_w, v_conf_dw_b, v_conf_ln_g, v_conf_ln_b, v_sc_conv_w, v_cd_w_out):
    args = locals()
    w = {n: args[n] for n in WEIGHTS}
    mom_m = {n: args["m_" + n] for n in WEIGHTS}
    mom_v = {n: args["v_" + n] for n in WEIGHTS}
    ex = Exchange(w)
    loss_local, grad_x, small_grads = local_step(x, mem, loss_target, ex)
    loss = lax.psum(loss_local, ("x", "y", "c"))
    outs = {}

    def update_big(names, own):
        last = None
        for n in names:
            shp = w[n].shape
            view = (-1, shp[-1])
            g = jnp.stack([own[(n, layer)] for layer in range(shp[0])])
            d, nm, nv = adamw(w[n].reshape(view), mom_m[n].reshape(view), mom_v[n].reshape(view), g.reshape(view), "adamw_" + n)
            outs[n] = (g, d.reshape(shp), nm.reshape(shp), nv.reshape(shp))
            last = d
        return last

    started = ex.put_small(small_grads)
    own = {}
    for key in ('l1', 'cd', 'l0'):
        own.update(ex.reduced(key, started))
    late = update_big(['xa_wq', 'xa_wkv', 'xa_wo', 'mlp_w1', 'mlp_w2', 'cd_w_in', 'cd_w_out'], own)
    g_own = ex.reduced_small(late)
    update_big(['ab_w_in', 'ab_w_out'], ex.reduced('ab', late))
    small = SMALL_SHARDED + REPLICATED
    shapes = [w[n].shape for n in small]
    d, nm, nv = adamw(_pack128([w[n] for n in small]), _pack128([mom_m[n] for n in small]), _pack128([mom_v[n] for n in small]),
                      _pack128([g_own[n] for n in small]), "adamw_small")
    for n, dd, mm, vv in zip(small, _unpack128(d, shapes), _unpack128(nm, shapes), _unpack128(nv, shapes)):
        outs[n] = (g_own[n], dd, mm, vv)
    return (loss, grad_x.reshape(x.shape), *[outs[n][0] for n in WEIGHTS], *[outs[n][1] for n in WEIGHTS],
            *[outs[n][2] for n in WEIGHTS], *[outs[n][3] for n in WEIGHTS])


G_AB = (('ab_w_in', 0), ('ab_w_out', 0))
G_L0 = (('xa_wq', 0), ('xa_wkv', 0), ('xa_wo', 0), ('mlp_w1', 0), ('mlp_w2', 0))
G_L1 = (('xa_wq', 1), ('xa_wkv', 1), ('xa_wo', 1), ('mlp_w1', 1), ('mlp_w2', 1))
G_CD = (('cd_w_in', 0), ('cd_w_out', 0))
GATHER_CHAIN = {'l0': ('cd', G_CD), 'cd': ('l1', G_L1)}
SHARD_AXIS = dict(BIG)
MEMBER_ROW_TILE = 64
FLAT_ROW_TILE = 128


def _members(group, w):
    out = []
    for n, layer in group:
        shp = w[n].shape[1:]
        if SHARD_AXIS[n] == 2:
            shp = (shp[1], shp[0])
        assert shp[1] == D, (n, shp)
        out.append((n, layer, shp, shp[0], _round_up(shp[0], MEMBER_ROW_TILE)))
    return out


def _group_rows(group, w):
    return _round_up(sum(m[4] for m in _members(group, w)), FLAT_ROW_TILE)


def _flat_shards(group, w):
    parts = []
    for n, layer, _, _, padded in _members(group, w):
        shard = w[n][layer].astype(BF)
        parts.append(_pad_rows(shard.T if SHARD_AXIS[n] == 2 else shard, padded))
    return _pad_rows(jnp.concatenate(parts, axis=0), _group_rows(group, w))


def _full_from_slots(land, group, w):
    out, off = {}, 0
    for n, layer, shp, rows, padded in _members(group, w):
        out[(n, layer)] = land[:, off:off + rows].reshape(N_DEV * rows, D)
        off += padded
    return out


def _slots_from_full(grads, group, w):
    parts = []
    for n, layer, shp, rows, padded in _members(group, w):
        blk = grads[(n, layer)].astype(BF).reshape(N_DEV, rows, D)
        parts.append(jnp.pad(blk, ((0, 0), (0, padded - rows), (0, 0))))
    send = jnp.concatenate(parts, axis=1)
    return jnp.pad(send, ((0, 0), (0, _group_rows(group, w) - send.shape[1]), (0, 0)))


def _own_from_sum(summed, group, w):
    out, off = {}, 0
    for n, layer, shp, rows, padded in _members(group, w):
        g = summed[off:off + rows]
        out[(n, layer)] = g.T if SHARD_AXIS[n] == 2 else g
        off += padded
    return out


_HBM = pl.BlockSpec(memory_space=pltpu.HBM)
_SEM = pl.BlockSpec(memory_space=pltpu.SEMAPHORE)
_ANY = pl.BlockSpec(memory_space=pl.ANY)


def _peer_copy(k, src, dst, send_sems, recv_sems, peer):
    return pltpu.make_async_remote_copy(src_ref=src, dst_ref=dst, send_sem=send_sems.at[k], recv_sem=recv_sems.at[k],
                                        device_id=peer, device_id_type=pl.DeviceIdType.MESH)


def exchange_start(src, name, scatter):
    shape = src.shape[-2:]

    def body(src_ref, land_ref, send_sems, recv_sems, src_thru, land_thru, token):
        me = _me()
        for k, f in enumerate(_FLIPS):
            peer = _flip(me, f)
            piece = src_ref.at[_slot(peer)] if scatter else src_ref
            _peer_copy(k, piece, land_ref.at[_slot(me)], send_sems, recv_sems, peer).start()
        token[...] = jnp.zeros_like(token)

    land = pltpu.with_memory_space_constraint(lax.empty((N_DEV,) + shape, src.dtype), pltpu.HBM)
    return pl.pallas_call(
        body, name=name,
        out_shape=(pltpu.SemaphoreType.DMA((7,)), pltpu.SemaphoreType.DMA((7,)), pltpu.HBM(src.shape, src.dtype),
                   pltpu.HBM((N_DEV,) + shape, src.dtype), jax.ShapeDtypeStruct((8, LANE), F32)),
        in_specs=(_HBM, _HBM), out_specs=(_SEM, _SEM, _HBM, _HBM, pl.BlockSpec(memory_space=pltpu.VMEM)),
        input_output_aliases={0: 2, 1: 3},
        compiler_params=pltpu.CompilerParams(has_side_effects=pltpu.SideEffectType.DATAFLOW_SIDE_EFFECTING),
    )(pltpu.with_memory_space_constraint(src, pltpu.HBM), land)


def exchange_wait(handles, after, name, scatter):
    send_sems, recv_sems, src_thru, land_thru, _ = handles

    def body(src_ref, land_ref, send_sems, recv_sems, after_ref, src_dead, got_ref, token):
        me = _me()
        for k, f in enumerate(_FLIPS):
            peer = _flip(me, f)
            piece = src_ref.at[_slot(peer)] if scatter else src_ref
            cp = _peer_copy(k, piece, land_ref.at[_slot(peer)], send_sems, recv_sems, peer)
            cp.wait_send()
            cp.wait_recv()
        token[...] = jnp.zeros_like(token)

    return pl.pallas_call(
        body, name=name, out_shape=(pltpu.HBM(src_thru.shape, src_thru.dtype), pltpu.HBM(land_thru.shape, land_thru.dtype),
                                    jax.ShapeDtypeStruct((8, LANE), F32)),
        in_specs=(_HBM, _HBM, _SEM, _SEM, _ANY), out_specs=(_HBM, _HBM, pl.BlockSpec(memory_space=pltpu.VMEM)),
        input_output_aliases={0: 0, 1: 1},
        compiler_params=pltpu.CompilerParams(has_side_effects=pltpu.SideEffectType.DATAFLOW_SIDE_EFFECTING),
    )(src_thru, land_thru, send_sems, recv_sems, after)


class Exchange:
    def __init__(self, w):
        self.w = w
        self.me = _slot(_me())
        shapes = [w[n].shape for n in SMALL_SHARDED]
        gs = all_gather(_pack128([w[n] for n in SMALL_SHARDED]), "gather_small")
        per_dev = [_unpack128(gs[d], shapes) for d in range(N_DEV)]
        self.small = {n: jnp.concatenate([per_dev[d][i] for d in range(N_DEV)], axis=-1) for i, n in enumerate(SMALL_SHARDED)}
        self.small.update({n: w[n] for n in REPLICATED})
        self.now = _full_from_slots(all_gather(_flat_shards(G_AB, w), "gather_ab"), G_AB, w)
        self.gathers = {'l0': (G_L0, exchange_start(_flat_shards(G_L0, w), "gather_l0_start", False))}
        self.tokens = [self.gathers['l0'][1][4]]
        self.reductions = {}

    def take_tokens(self):
        toks, self.tokens = self.tokens, []
        return toks

    def weights(self, key, after):
        if key == 'ab':
            return self.now
        group, handles = self.gathers[key]
        _, land, done = exchange_wait(handles, after, f"gather_{key}_wait", False)
        nxt = GATHER_CHAIN.get(key)
        if nxt is not None:
            src = _flat_shards(nxt[1], self.w) + done[0, 0].astype(BF)
            self.gathers[nxt[0]] = (nxt[1], exchange_start(src, f"gather_{nxt[0]}_start", False))
            self.tokens.append(self.gathers[nxt[0]][1][4])
        land = lax.dynamic_update_slice(land, handles[2][None], (self.me, 0, 0))
        return _full_from_slots(land, group, self.w)

    def put_grads(self, key, group, grads):
        send = _slots_from_full(grads, group, self.w)
        handles = exchange_start(send, f"reduce_{key}_start", True)
        self.reductions[key] = (group, handles)
        self.tokens.append(handles[4])

    def reduced(self, key, after):
        group, handles = self.reductions[key]
        send, land, _ = exchange_wait(handles, after, f"reduce_{key}_wait", True)
        mine = lax.dynamic_slice_in_dim(send, self.me, 1, axis=0)
        land = lax.dynamic_update_slice(land, mine, (self.me, 0, 0))
        return _own_from_sum(sum_slots(land, f"sum_{key}", FLAT_ROW_TILE), group, self.w)

    def put_small(self, small_grads):
        small = SMALL_SHARDED + REPLICATED
        self.small_shapes = [small_grads[n].shape for n in small]
        self.small_handles = exchange_start(_pack128([small_grads[n] for n in small]), "gather_small_grads_start", False)
        return self.small_handles[4]

    def reduced_small(self, after):
        small = SMALL_SHARDED + REPLICATED
        src, land, _ = exchange_wait(self.small_handles, after, "gather_small_grads_wait", False)
        gs = lax.dynamic_update_slice(land, src[None], (self.me, 0, 0))
        tot = _unpack128(sum_slots(gs, "sum_small", 1024), self.small_shapes)
        out = {}
        for n, g in zip(small, tot):
            if n in SMALL_SHARDED:
                width = self.w[n].shape[-1]
                g = lax.dynamic_slice_in_dim(g, self.me * width, width, axis=g.ndim - 1)
            out[n] = g
        return out


def local_step(x, mem, target, ex):
    bsz, seq, _ = x.shape
    t = bsz * seq
    nb = t // TB
    nc = seq // CHUNK
    x0 = x.reshape(t, D)
    mem2 = mem.reshape(bsz * N_MEM, D)
    tgt = target.reshape(t, D)
    p = ex.small
    gains = p['norm_gains']
    big = dict(ex.weights('ab', None))

    def gain(layer, i):
        g = gains[layer, i].reshape(1, D)
        for tok in ex.take_tokens():
            g = g + tok[0, 0]
        return g

    consts = _ssd_consts()
    grads = {}
    saved = [dict(), dict()]

    def run_seg_res(xin, m, ga, gb, name):
        return fwd_call(seg_res, name, (nb,), [xin, m, ga, gb], [_rows(D), _rows(D), _par(D), _par(D)],
                        [_sd((t, D)), _sd((t, D), BF)], [_rows(D), _rows(D)])

    def attn_specs():
        nq = seq // TB
        q = pl.BlockSpec((TB, D), lambda b, i: (b * nq + i, 0))
        kv = pl.BlockSpec((N_MEM, 2 * D), lambda b, i: (b, 0))
        return (bsz, nq), q, kv

    def attention_fwd(layer, xin, hin, sv):
        q = matmul(hin, big[('xa_wq', layer)], 'nn', f"q_{layer}", BF)
        kv = matmul(mem2, big[('xa_wkv', layer)], 'nt', f"kv_{layer}", BF)
        grid, qs, kvs = attn_specs()
        o, = fwd_call(attn_fn, f"attn_{layer}", grid, [q, kv], [qs, kvs], [_sd((t, D), BF)], [qs])
        ao = matmul(o, big[('xa_wo', layer)], 'nn', f"ao_{layer}")
        sv.update(q=q, kv=kv, o=o, ao=ao)
        return ao

    def mlp_fwd(layer, hin, sv):
        r, rr = matmul(hin, big[('mlp_w1', layer)], 'nt', f"mlp1_{layer}", (BF, BF), epilogue=act_epilogue)
        mo = matmul(rr, big[('mlp_w2', layer)], 'nn', f"mlp2_{layer}")
        sv.update(r=r, rr=rr, mo=mo)
        return mo

    sv = saved[0]
    h0, = fwd_call(seg_in, "norm_in", (nb,), [x0, gain(0, 0)], [_rows(D), _par(D)], [_sd((t, D), BF)], [_rows(D)])
    xbc0 = POOL_W + SSM_INNER
    w_ab_in = big[('ab_w_in', 0)]
    w_ab_in = _pad_rows(jnp.concatenate([w_ab_in[:xbc0], _xbc_group(w_ab_in[xbc0:xbc0 + SSM_CONV_DIM], 0),
                                         w_ab_in[xbc0 + SSM_CONV_DIM:]], axis=0), AB_IN_PAD)
    conv_w, conv_b = _xbc_group(p['ssm_conv_w'][0], 1), _xbc_group(p['ssm_conv_b'], 1)
    u0 = matmul(h0, w_ab_in, 'nt', "ab_in")
    pool_outs = []
    for g in range(POOL_GROUPS):
        seqspec = pl.BlockSpec((seq, PG), lambda b, g=g: (b, g))
        po, = fwd_call(make_pool_fn(g), f"pool_{g}", (bsz,), [u0, p['pool_w'][0, g], p['pool_scale']],
                       [seqspec, pl.BlockSpec((PG, PG), lambda b: (0, 0)), pl.BlockSpec((1, PG), lambda b, g=g: (0, g))],
                       [_sd((t, PG), BF)], [pl.BlockSpec((seq, PG), lambda b: (b, 0))])
        pool_outs.append(po)
    cw = 256
    ncb = SSM_CONV_DIM // cw
    cbase = (POOL_W + SSM_INNER) // cw
    conv_in_specs = [pl.BlockSpec((seq, cw), lambda j, b: (b, cbase + j)), pl.BlockSpec((SSM_CONV, cw), lambda j, b: (0, j)),
                     pl.BlockSpec((1, cw), lambda j, b: (0, j))]
    conv_out_spec = pl.BlockSpec((seq, cw), lambda j, b: (b, j))
    xbc_act, = fwd_call(conv4_fn, "ssm_conv", (ncb, bsz), [u0, conv_w, conv_b], conv_in_specs,
                        [_sd((t, SSM_CONV_DIM))], [conv_out_spec])
    dtb = jnp.pad(p['ssm_dt_bias'], ((0, 0), (0, LANE - SSM_HEADS)))
    alog = jnp.pad(p['ssm_a_log'], ((0, 0), (0, LANE - SSM_HEADS)))
    dsk = jnp.pad(p['ssm_d'], ((0, 0), (0, LANE - SSM_HEADS)))
    yn, hs = ssd_fwd(xbc_act, u0, dtb, alog, dsk, p['ssm_norm'], consts, bsz, seq)
    mix0 = jnp.concatenate(pool_outs + [yn], axis=1)
    m0 = matmul(mix0, big[('ab_w_out', 0)], 'nn', "ab_out")
    x1, h2 = run_seg_res(x0, m0, gain(0, 1), gain(0, 2), "res_0a")
    big.update(ex.weights('l0', h2))
    ao0 = attention_fwd(0, x1, h2, sv)
    x2, h3 = run_seg_res(x1, ao0, gain(0, 3), gain(0, 4), "res_0b")
    mo0 = mlp_fwd(0, h3, sv)
    big.update(ex.weights('cd', sv['r']))
    x3, h4 = run_seg_res(x2, mo0, gain(0, 5), gain(1, 0), "res_0c")

    sv1 = saved[1]
    nd = D // LANE
    w_cd_in = big[('cd_w_in', 0)].reshape(5, nd, LANE, D).transpose(1, 0, 2, 3).reshape(CD_IN, D)
    u1 = matmul(h4, w_cd_in, 'nt', "cd_in")
    cd_par = [pl.BlockSpec((CONF_K, LANE), lambda j, b: (0, j)), pl.BlockSpec((1, LANE), lambda j, b: (0, j)),
              pl.BlockSpec((SC_K, LANE), lambda j, b: (0, j))]
    cd_ins = [u1, p['conf_dw_w'][0], p['conf_dw_b'], p['sc_conv_w'][0]]
    cd_u_spec = pl.BlockSpec((seq, 5 * LANE), lambda j, b: (b, j))
    cd_in_specs = [cd_u_spec] + cd_par
    cd_out_spec = pl.BlockSpec((seq, LANE), lambda j, b: (b, j))
    vconv, sc_out = fwd_call(cd1_fn, "cd_conv", (nd, bsz), cd_ins, cd_in_specs, [_sd((t, D)), _sd((t, D), BF)],
                             [cd_out_spec, cd_out_spec])
    conf, = fwd_call(seg_ln, "conf_ln", (nb,), [vconv, p['conf_ln_g'], p['conf_ln_b']], [_rows(D), _par(D), _par(D)],
                     [_sd((t, D), BF)], [_rows(D)])
    mix1 = jnp.concatenate([conf, sc_out], axis=1)
    m1 = matmul(mix1, big[('cd_w_out', 0)], 'nn', "cd_out")
    x4, h5 = run_seg_res(x3, m1, gain(1, 1), gain(1, 2), "res_1a")
    big.update(ex.weights('l1', h5))
    ao1 = attention_fwd(1, x4, h5, sv1)
    x5, h6 = run_seg_res(x4, ao1, gain(1, 3), gain(1, 4), "res_1b")
    mo1 = mlp_fwd(1, h6, sv1)

    def loss_body(x_ref, m_ref, g_ref, t_ref, dy_ref, acc_ref):
        y = x_ref[...] + _rms(m_ref[...], g_ref[...])
        d = y - t_ref[...]
        dy_ref[...] = d / float(D)

        @pl.when(pl.program_id(0) == 0)
        def _():
            acc_ref[...] = jnp.zeros_like(acc_ref)

        acc_ref[...] += jnp.sum(d * d, axis=0, keepdims=True)

    dy, lanes = pl.pallas_call(
        loss_body, name="loss_head", grid=(nb,), in_specs=[_rows(D), _rows(D), _par(D), _rows(D)],
        out_specs=[_rows(D), _par(D)], out_shape=[_sd((t, D)), _sd((1, D))], compiler_params=_params())(x5, mo1, gain(1, 5), tgt)
    loss = 0.5 * jnp.sum(lanes) / float(D)

    gain_grads = {}

    def bwd_seg_out(xin, m, ga, dyv, name):
        dx, dm, dga = bwd_call(seg_out, name, (nb,), [xin, m, ga], [_rows(D), _rows(D), _par(D)], [dyv], [_rows(D)],
                               [0, 1, 2], [_sd((t, D)), _sd((t, D), BF), _sd((1, D))], [_rows(D), _rows(D), _par(D)],
                               [None, None, (0,)])
        return dx, dm, dga

    def bwd_seg_res(xin, m, ga, gb, dx1, dh, name):
        return bwd_call(seg_res, name, (nb,), [xin, m, ga, gb], [_rows(D), _rows(D), _par(D), _par(D)], [dx1, dh],
                        [_rows(D), _rows(D)], [0, 1, 2, 3], [_sd((t, D)), _sd((t, D), BF), _sd((1, D)), _sd((1, D))],
                        [_rows(D), _rows(D), _par(D), _par(D)], [None, None, (0,), (0,)])

    def mlp_bwd(layer, hin, dmo, sv):
        grads_w2 = matmul(sv['rr'], dmo, 'tn', f"d_mlp_w2_{layer}", BF)
        dr, = matmul(dmo, big[('mlp_w2', layer)], 'nt', f"d_r_{layer}", (BF,), epilogue=act_bwd_epilogue, extras=[sv['r']])
        grads_w1 = matmul(dr, hin, 'tn', f"d_mlp_w1_{layer}", BF)
        dh = matmul(dr, big[('mlp_w1', layer)], 'nn', f"d_h_mlp_{layer}")
        return dh, grads_w1, grads_w2

    def attention_bwd(layer, hin, dao, sv):
        g_wo = matmul(sv['o'], dao, 'tn', f"d_xa_wo_{layer}", BF)
        do = matmul(dao, big[('xa_wo', layer)], 'nt', f"d_o_{layer}", BF)
        grid, qs, kvs = attn_specs()
        dq, dkv = bwd_call(attn_fn, f"d_attn_{layer}", grid, [sv['q'], sv['kv']], [qs, kvs], [do], [qs], [0, 1],
                           [_sd((t, D), BF), _sd((bsz * N_MEM, 2 * D))], [qs, kvs], [None, (1,)])
        g_wkv = matmul(dkv, mem2, 'tn', f"d_xa_wkv_{layer}", BF)
        g_wq = matmul(hin, dq, 'tn', f"d_xa_wq_{layer}", BF)
        dh = matmul(dq, big[('xa_wq', layer)], 'nt', f"d_h_attn_{layer}")
        return dh, g_wq, g_wkv, g_wo

    per_layer = {k: [None, None] for k in ('xa_wq', 'xa_wkv', 'xa_wo', 'mlp_w1', 'mlp_w2')}

    dx5, dmo1, gain_grads[(1, 5)] = bwd_seg_out(x5, mo1, gain(1, 5), dy, "d_out")
    dh6, per_layer['mlp_w1'][1], per_layer['mlp_w2'][1] = mlp_bwd(1, h6, dmo1, sv1)
    dx4, dao1, gain_grads[(1, 3)], gain_grads[(1, 4)] = bwd_seg_res(x4, ao1, gain(1, 3), gain(1, 4), dx5, dh6, "d_res_1b")
    dh5, per_layer['xa_wq'][1], per_layer['xa_wkv'][1], per_layer['xa_wo'][1] = attention_bwd(1, h5, dao1, sv1)
    ex.put_grads('l1', G_L1, {(k, 1): v[1] for k, v in per_layer.items()})
    dx3, dm1, gain_grads[(1, 1)], gain_grads[(1, 2)] = bwd_seg_res(x3, m1, gain(1, 1), gain(1, 2), dx4, dh5, "d_res_1a")
    g_cd_out = matmul(mix1, dm1, 'tn', "d_cd_w_out", BF)
    dmix1 = matmul(dm1, big[('cd_w_out', 0)], 'nt', "d_mix1")
    dvconv, dlg, dlb = bwd_call(seg_ln, "d_conf_ln", (nb,), [vconv, p['conf_ln_g'], p['conf_ln_b']],
                                [_rows(D), _par(D), _par(D)], [dmix1], [_rows(D, 0)], [0, 1, 2],
                                [_sd((t, D)), _sd((1, D)), _sd((1, D))], [_rows(D), _par(D), _par(D)], [None, (0,), (0,)])
    grads['conf_ln_g'], grads['conf_ln_b'] = dlg, dlb
    cd_g = bwd_call(cd1_fn, "d_cd_conv", (nd, bsz), cd_ins, cd_in_specs, [dvconv, dmix1],
                    [cd_out_spec, pl.BlockSpec((seq, LANE), lambda j, b: (b, nd + j))], list(range(4)),
                    [_sd((t, CD_IN), BF), _sd((CONF_K, D)), _sd((1, D)), _sd((SC_K, D))], [cd_u_spec] + cd_par,
                    [None, (1,), (1,), (1,)])
    du1 = cd_g[0]
    grads['conf_dw_w'], grads['conf_dw_b'], grads['sc_conv_w'] = cd_g[1][None], cd_g[2], cd_g[3][None]
    g_cd_in = matmul(du1, h4, 'tn', "d_cd_w_in", BF).reshape(nd, 5, LANE, D).transpose(1, 0, 2, 3).reshape(CD_IN, D)
    ex.put_grads('cd', G_CD, {('cd_w_in', 0): g_cd_in, ('cd_w_out', 0): g_cd_out})
    dh4 = matmul(du1, w_cd_in, 'nn', "d_h_cd")

    dx2, dmo0, gain_grads[(0, 5)], gain_grads[(1, 0)] = bwd_seg_res(x2, mo0, gain(0, 5), gain(1, 0), dx3, dh4, "d_res_0c")
    dh3, per_layer['mlp_w1'][0], per_layer['mlp_w2'][0] = mlp_bwd(0, h3, dmo0, sv)
    dx1, dao0, gain_grads[(0, 3)], gain_grads[(0, 4)] = bwd_seg_res(x1, ao0, gain(0, 3), gain(0, 4), dx2, dh3, "d_res_0b")
    dh2, per_layer['xa_wq'][0], per_layer['xa_wkv'][0], per_layer['xa_wo'][0] = attention_bwd(0, h2, dao0, sv)
    ex.put_grads('l0', G_L0, {(k, 0): v[0] for k, v in per_layer.items()})
    dx0r, dm0, gain_grads[(0, 1)], gain_grads[(0, 2)] = bwd_seg_res(x0, m0, gain(0, 1), gain(0, 2), dx1, dh2, "d_res_0a")
    g_ab_out = matmul(mix0, dm0, 'tn', "d_ab_w_out", BF)
    dmix0 = matmul(dm0, big[('ab_w_out', 0)], 'nt', "d_mix0")
    dxbc_act, dz, ddt, ddtb, dalog, ddsk, dnw = ssd_bwd(xbc_act, u0, dtb, alog, dsk, p['ssm_norm'], consts, hs, dmix0, bsz, seq)
    grads['ssm_dt_bias'] = ddtb[:, :SSM_HEADS]
    grads['ssm_a_log'] = dalog[:, :SSM_HEADS]
    grads['ssm_d'] = ddsk[:, :SSM_HEADS]
    grads['ssm_norm'] = dnw
    dxr, dcw, dcb = bwd_call(conv4_fn, "d_ssm_conv", (ncb, bsz), [u0, conv_w, conv_b], conv_in_specs,
                             [dxbc_act], [conv_out_spec], [0, 1, 2],
                             [_sd((t, SSM_CONV_DIM), BF), _sd((SSM_CONV, SSM_CONV_DIM)), _sd((1, SSM_CONV_DIM))],
                             [conv_out_spec, conv_in_specs[1], conv_in_specs[2]], [None, (1,), (1,)])
    grads['ssm_conv_w'], grads['ssm_conv_b'] = _xbc_ungroup(dcw, 1)[None], _xbc_ungroup(dcb, 1)
    dpool, dpw, dps = [], [], []
    for g in range(POOL_GROUPS):
        seqspec = pl.BlockSpec((seq, PG), lambda b, g=g: (b, g))
        one = pl.BlockSpec((seq, PG), lambda b: (b, 0))
        wspec = pl.BlockSpec((PG, PG), lambda b: (0, 0))
        sspec = pl.BlockSpec((1, PG), lambda b, g=g: (0, g))
        a, bb, c = bwd_call(make_pool_fn(g), f"d_pool_{g}", (bsz,), [u0, p['pool_w'][0, g], p['pool_scale']],
                            [seqspec, wspec, sspec], [dmix0], [seqspec], [0, 1, 2],
                            [_sd((t, PG), BF), _sd((PG, PG)), _sd((1, PG))], [one, wspec, pl.BlockSpec((1, PG), lambda b: (0, 0))],
                            [None, (0,), (0,)])
        dpool.append(a)
        dpw.append(bb)
        dps.append(c)
    grads['pool_w'] = jnp.stack(dpw)[None]
    grads['pool_scale'] = jnp.concatenate(dps, axis=1)
    du0 = jnp.concatenate(dpool + [dz, dxr, ddt.astype(BF)], axis=1)
    g_ab_in = matmul(du0, h0, 'tn', "d_ab_w_in", BF)
    g_ab_in = jnp.concatenate([g_ab_in[:xbc0], _xbc_ungroup(g_ab_in[xbc0:xbc0 + SSM_CONV_DIM], 0),
                               g_ab_in[xbc0 + SSM_CONV_DIM:AB_IN]], axis=0)
    ex.put_grads('ab', G_AB, {('ab_w_in', 0): g_ab_in, ('ab_w_out', 0): g_ab_out})
    dh0 = matmul(du0, w_ab_in, 'nn', "d_h_ab")
    dx, dg00 = bwd_call(seg_in_res, "d_norm_in", (nb,), [x0, gain(0, 0)], [_rows(D), _par(D)], [dx0r, dh0],
                        [_rows(D), _rows(D)], [0, 1], [_sd((t, D)), _sd((1, D))], [_rows(D), _par(D)], [None, (0,)])
    gain_grads[(0, 0)] = dg00
    grads['norm_gains'] = jnp.stack([jnp.concatenate([gain_grads[(l, i)] for i in range(6)], axis=0) for l in range(2)])
    return loss, dx, grads
```

```python
import functools
import math

import numpy as np
import jax
import jax.numpy as jnp
from jax import lax
from jax.experimental import pallas as pl
from jax.experimental.pallas import tpu as pltpu

BF = jnp.bfloat16
F32 = jnp.float32
HI = lax.Precision.HIGHEST

N_DEV = 8
D = 1024
N_MEM = 256
XA_HEADS = 4
XA_DH = D // XA_HEADS
POOL_GROUPS = 4
PG = 128
POOL_W = POOL_GROUPS * PG
SSM_INNER = 1024
SSM_GROUPS = 2
SSM_GSZ = SSM_INNER // SSM_GROUPS
SSM_HEADS = 16
SSM_P = 64
SSM_N = 128
SSM_CONV = 4
SSM_CONV_DIM = SSM_INNER + 2 * SSM_GROUPS * SSM_N
SSM_XBC_G = SSM_GSZ + 2 * SSM_N
CHUNK = 128
AB_IN = POOL_W + SSM_INNER + SSM_CONV_DIM + SSM_HEADS
AB_IN_PAD = POOL_W + SSM_INNER + SSM_CONV_DIM + 128
AB_OUT = POOL_W + SSM_INNER
CONF_K = 31
SC_K = 3
CD_IN = 5 * D
CD_OUT = 2 * D
MLP_H = 4 * D
RMS_EPS = 1e-6
LN_EPS = 1e-5
ADAM_LR = 0.001
ADAM_B1 = 0.9
ADAM_B2 = 0.999
ADAM_EPS = 1e-08
ADAM_WD = 0.01
ADAM_STEP = 10
VMEM_LIMIT = 56 * 1024 * 1024
LANE = 128

NAMES = ['x', 'mem', 'norm_gains', 'xa_wq', 'xa_wkv', 'xa_wo', 'mlp_w1', 'mlp_w2', 'ab_w_in', 'pool_w', 'pool_scale',
         'ssm_conv_w', 'ssm_conv_b', 'ssm_dt_bias', 'ssm_a_log', 'ssm_d', 'ssm_norm', 'ab_w_out', 'cd_w_in', 'conf_dw_w',
         'conf_dw_b', 'conf_ln_g', 'conf_ln_b', 'sc_conv_w', 'cd_w_out', 'loss_target']
WEIGHTS = NAMES[2:25]
BIG = [('xa_wq', 1), ('xa_wkv', 2), ('xa_wo', 1), ('mlp_w1', 2), ('mlp_w2', 1), ('cd_w_in', 2), ('cd_w_out', 1),
       ('ab_w_out', 1), ('ab_w_in', 2)]
SMALL_SHARDED = ['norm_gains', 'ssm_conv_w', 'conf_dw_w', 'conf_dw_b', 'conf_ln_g', 'conf_ln_b', 'sc_conv_w']
REPLICATED = ['pool_w', 'pool_scale', 'ssm_conv_b', 'ssm_dt_bias', 'ssm_a_log', 'ssm_d', 'ssm_norm']


def _dg(a, b, ca, cb, prec=None):
    return lax.dot_general(a, b, (((ca,), (cb,)), ((), ())), precision=prec, preferred_element_type=F32)


@functools.partial(jax.custom_vjp, nondiff_argnums=(2, 3))
def bdot(a, b, ca, cb):
    return _dg(a.astype(BF), b.astype(BF), ca, cb)


def _bdot_fwd(a, b, ca, cb):
    return bdot(a, b, ca, cb), (a, b)


def _bdot_bwd(ca, cb, res, g):
    a, b = res
    g16, a16, b16 = g.astype(BF), a.astype(BF), b.astype(BF)
    da = _dg(g16, b16, 1, 1 - cb) if ca == 1 else _dg(b16, g16, 1 - cb, 1)
    db = _dg(g16, a16, 0, 1 - ca) if cb == 1 else _dg(a16, g16, 1 - ca, 0)
    return da.astype(a.dtype), db.astype(b.dtype)


bdot.defvjp(_bdot_fwd, _bdot_bwd)


def _split3(a):
    a1 = a.astype(BF)
    r1 = a - a1.astype(F32)
    a2 = r1.astype(BF)
    a3 = (r1 - a2.astype(F32)).astype(BF)
    return a1, a2, a3


def _exact_right(a, c):
    m = a.shape[0]
    if m % 16:
        return sum(_dg(p, c, 1, 0) for p in _split3(a))
    o = _dg(jnp.concatenate(_split3(a), axis=0), c, 1, 0)
    return o[:m] + o[m:2 * m] + o[2 * m:]


def _exact_left(c, a):
    n = a.shape[1]
    o = _dg(c, jnp.concatenate(_split3(a), axis=1), 1, 0)
    return o[:, :n] + o[:, n:2 * n] + o[:, 2 * n:]


@jax.custom_vjp
def cmat(a, c, ct):
    return _exact_right(a, c)


def _cmat_fwd(a, c, ct):
    return cmat(a, c, ct), (c, ct)


def _cmat_bwd(res, g):
    c, ct = res
    return _exact_right(g, ct), jnp.zeros_like(c), jnp.zeros_like(ct)


cmat.defvjp(_cmat_fwd, _cmat_bwd)


@jax.custom_vjp
def cmatl(c, ct, a):
    return _exact_left(c, a)


def _cmatl_fwd(c, ct, a):
    return cmatl(c, ct, a), (c, ct)


def _cmatl_bwd(res, g):
    c, ct = res
    return jnp.zeros_like(c), jnp.zeros_like(ct), _exact_left(ct, g)


cmatl.defvjp(_cmatl_fwd, _cmatl_bwd)


SUBLANES = 8


def _taps(x, shifts, down):
    n, c = x.shape
    pad = _round_up(max(shifts), SUBLANES)
    if pad == 0:
        return {0: x}
    zeros = jnp.zeros((pad, c), x.dtype)
    xp = jnp.concatenate([zeros, x] if down else [x, zeros], axis=0)
    rolled, out = {0: xp}, {}
    for s in shifts:
        a, b = divmod(s, SUBLANES)
        if b not in rolled:
            rolled[b] = pltpu.roll(xp, b if down else n + pad - b, 0)
        off = pad - SUBLANES * a if down else SUBLANES * a
        out[s] = rolled[b][off:off + n]
    return out


def _shift_down(x, k):
    return _taps(x, [k], True)[k]


def _shift_up(x, k):
    return _taps(x, [k], False)[k]


@functools.partial(jax.custom_vjp, nondiff_argnums=(1,))
def shift(x, k):
    return _shift_down(x, k)


def _shift_fwd(x, k):
    return _shift_down(x, k), None


def _shift_bwd(k, _, g):
    return (_shift_up(g, k),)


shift.defvjp(_shift_fwd, _shift_bwd)


@functools.partial(jax.custom_vjp, nondiff_argnums=(2,))
def cconv(u, w, width):
    taps = _taps(u, list(range(width)), True)
    acc = u * w[width - 1:width, :]
    for k in range(width - 1):
        acc = acc + taps[width - 1 - k] * w[k:k + 1, :]
    return acc


def _cconv_fwd(u, w, width):
    return cconv(u, w, width), (u, w)


def _cconv_bwd(width, res, g):
    u, w = res
    rows = lax.broadcasted_iota(jnp.int32, w.shape, 0)
    du = g * w[width - 1:width, :]
    dw = jnp.where(rows == width - 1, jnp.sum(g * u, axis=0, keepdims=True), 0.0)
    g_taps = _taps(g, list(range(width)), False)
    u_taps = _taps(u, list(range(width)), True)
    for k in range(width - 1):
        s = width - 1 - k
        du = du + g_taps[s] * w[k:k + 1, :]
        dw = dw + jnp.where(rows == k, jnp.sum(g * u_taps[s], axis=0, keepdims=True), 0.0)
    return du, dw


cconv.defvjp(_cconv_fwd, _cconv_bwd)


def _rms(x, g):
    return x * lax.rsqrt(jnp.mean(x * x, axis=-1, keepdims=True) + RMS_EPS) * g


def _params(sem=None):
    return pltpu.CompilerParams(dimension_semantics=sem, vmem_limit_bytes=VMEM_LIMIT)


def _f32(v):
    return v if v.dtype == F32 else v.astype(F32)


def _first(axes):
    ok = None
    for ax in axes:
        c = pl.program_id(ax) == 0
        ok = c if ok is None else jnp.logical_and(ok, c)
    return ok


def fwd_call(fn, name, grid, ins, in_specs, out_shapes, out_specs):
    n_in = len(ins)

    def body(*refs):
        outs = fn(*[_f32(r[...]) for r in refs[:n_in]])
        for r, o in zip(refs[n_in:], outs):
            r[...] = o.astype(r.dtype)

    return pl.pallas_call(body, name=name, grid=grid, in_specs=in_specs, out_specs=out_specs, out_shape=out_shapes,
                          compiler_params=_params())(*ins)


def bwd_call(fn, name, grid, ins, in_specs, cots, cot_specs, gidx, g_shapes, g_specs, g_acc):
    n_in, n_cot = len(ins), len(cots)

    def body(*refs):
        vals = [_f32(r[...]) for r in refs[:n_in]]

        def f_sel(*dv):
            full = list(vals)
            for i, v in zip(gidx, dv):
                full[i] = v
            return tuple(fn(*full))

        outs, vjp = jax.vjp(f_sel, *[vals[i] for i in gidx])
        cts = tuple(_f32(r[...]) for r in refs[n_in:n_in + n_cot])
        grads = vjp(cts)
        for r, g, acc in zip(refs[n_in + n_cot:], grads, g_acc):
            if acc is None:
                r[...] = g.astype(r.dtype)
            else:
                @pl.when(_first(acc))
                def _():
                    r[...] = jnp.zeros_like(r)

                r[...] += g.astype(r.dtype)

    return pl.pallas_call(body, name=name, grid=grid, in_specs=list(in_specs) + list(cot_specs), out_specs=g_specs,
                          out_shape=g_shapes, compiler_params=_params())(*ins, *cots)


def _tile(dim, pref):
    if dim <= pref:
        return dim
    best = None
    for t in range(LANE, pref + 1, LANE):
        if dim % t == 0:
            best = t
    assert best is not None, dim
    return best


MATMUL_VMEM_BUDGET = 40 * 1024 * 1024


def _matmul_tiles(m, n, k, a_bytes, b_bytes, out_bytes):
    tn = _tile(n, 1024)
    for tk_pref in (k, 2048, 1024, 512):
        tk = _tile(k, tk_pref)
        for tm_pref in (1024, 512, 256):
            tm = _tile(m, tm_pref)
            need = 2 * (tm * tk * a_bytes + tk * tn * b_bytes + tm * tn * out_bytes) + (0 if tk == k else tm * tn * 4)
            need += (tm * tk * 2 if a_bytes == 4 else 0) + (tk * tn * 2 if b_bytes == 4 else 0)
            if need <= MATMUL_VMEM_BUDGET:
                return tm, tn, tk
    raise ValueError((m, n, k))


def matmul(a, b, mode, name, out_dtype=F32, epilogue=None, extras=()):
    if mode == 'nn':
        (m, k), (k2, n) = a.shape, b.shape
    elif mode == 'nt':
        (m, k), (n, k2) = a.shape, b.shape
    else:
        (k, m), (k2, n) = a.shape, b.shape
    assert k == k2, (name, a.shape, b.shape)
    n_extra = len(extras)
    out_dtypes = out_dtype if isinstance(out_dtype, tuple) else (out_dtype,)
    per_out = sum(jnp.dtype(dt).itemsize for dt in out_dtypes) + sum(e.dtype.itemsize for e in extras)
    tm, tn, tk = _matmul_tiles(m, n, k, a.dtype.itemsize, b.dtype.itemsize, per_out)
    nk = k // tk
    ca = 0 if mode == 'tn' else 1
    cb = 1 if mode == 'nt' else 0
    a_spec = pl.BlockSpec((tk, tm), lambda i, j, kk: (kk, i)) if mode == 'tn' else pl.BlockSpec((tm, tk), lambda i, j, kk: (i, kk))
    b_spec = pl.BlockSpec((tn, tk), lambda i, j, kk: (j, kk)) if mode == 'nt' else pl.BlockSpec((tk, tn), lambda i, j, kk: (kk, j))

    def finish(o_refs, extra_refs, acc):
        outs = (acc,) if epilogue is None else epilogue(acc, *[_f32(e[...]) for e in extra_refs])
        for o_ref, o in zip(o_refs, outs):
            o_ref[...] = o.astype(o_ref.dtype)

    def body_whole_k(a_ref, b_ref, *refs):
        finish(refs[n_extra:], refs[:n_extra], _dg(a_ref[...].astype(BF), b_ref[...].astype(BF), ca, cb))

    def body_split_k(a_ref, b_ref, *refs):
        extra_refs, o_refs, acc = refs[:n_extra], refs[n_extra:-1], refs[-1]
        kk = pl.program_id(2)

        @pl.when(kk == 0)
        def _():
            acc[...] = jnp.zeros_like(acc)

        acc[...] += _dg(a_ref[...].astype(BF), b_ref[...].astype(BF), ca, cb)

        @pl.when(kk == nk - 1)
        def _():
            finish(o_refs, extra_refs, acc[...])

    tile = pl.BlockSpec((tm, tn), lambda i, j, kk: (i, j))
    outs = pl.pallas_call(
        body_whole_k if nk == 1 else body_split_k, name=name, grid=(m // tm, n // tn, nk),
        in_specs=[a_spec, b_spec] + [tile] * n_extra, out_specs=[tile] * len(out_dtypes),
        out_shape=[jax.ShapeDtypeStruct((m, n), dt) for dt in out_dtypes],
        scratch_shapes=[] if nk == 1 else [pltpu.VMEM((tm, tn), F32)],
        compiler_params=_params(("parallel", "parallel", "arbitrary")))(a, b, *extras)
    return outs if isinstance(out_dtype, tuple) else outs[0]


_FLIPS = [(0, 0, 1), (1, 0, 0), (0, 1, 0), (1, 1, 0), (1, 0, 1), (0, 1, 1), (1, 1, 1)]


def _me():
    return lax.axis_index("x"), lax.axis_index("y"), lax.axis_index("c")


def _flip(pos, f):
    return tuple(jnp.where(fi == 1, 1 - p, p) if fi else p for p, fi in zip(pos, f))


def _slot(pos):
    return 4 * pos[0] + 2 * pos[1] + pos[2]


def all_gather(v, name):
    def body(v_ref, out_ref, send_sems, recv_sems, local_sem):
        me = _me()
        sibling = _flip(me, (0, 0, 1))
        chips = [_flip(me, f) for f in ((1, 0, 0), (0, 1, 0), (1, 1, 0))]

        def copy(k, block, to, src=None):
            return pltpu.make_async_remote_copy(
                src_ref=out_ref.at[_slot(block)] if src is None else src, dst_ref=out_ref.at[_slot(block)],
                send_sem=send_sems.at[k], recv_sem=recv_sems.at[k], device_id=to, device_id_type=pl.DeviceIdType.MESH)

        mine = pltpu.make_async_copy(v_ref, out_ref.at[_slot(me)], local_sem)
        mine.start()
        first = [copy(0, me, sibling, src=v_ref)] + [copy(1 + j, me, chip, src=v_ref) for j, chip in enumerate(chips)]
        for cp in first:
            cp.start()
        passed = [copy(4 + j, chip, sibling) for j, chip in enumerate(chips)]
        for j, chip in enumerate(chips):
            copy(1 + j, chip, me).wait_recv()
            passed[j].start()
        copy(0, sibling, me).wait_recv()
        for j, chip in enumerate(chips):
            copy(4 + j, _flip(chip, (0, 0, 1)), me).wait_recv()
        for cp in first + passed:
            cp.wait_send()
        mine.wait()

    return pl.pallas_call(
        body, name=name, out_shape=jax.ShapeDtypeStruct((N_DEV,) + v.shape, v.dtype),
        in_specs=[pl.BlockSpec(memory_space=pl.ANY)], out_specs=pl.BlockSpec(memory_space=pl.ANY),
        scratch_shapes=[pltpu.SemaphoreType.DMA((7,)), pltpu.SemaphoreType.DMA((7,)), pltpu.SemaphoreType.DMA(())],
    )(v)


def all_to_all(v, name):
    def body(v_ref, out_ref, send_sems, recv_sems, local_sem):
        me = _me()
        mine = pltpu.make_async_copy(v_ref.at[_slot(me)], out_ref.at[_slot(me)], local_sem)
        mine.start()
        copies = []
        for k, f in enumerate(_FLIPS):
            peer = _flip(me, f)
            cp = pltpu.make_async_remote_copy(
                src_ref=v_ref.at[_slot(peer)], dst_ref=out_ref.at[_slot(me)], send_sem=send_sems.at[k],
                recv_sem=recv_sems.at[k], device_id=peer, device_id_type=pl.DeviceIdType.MESH)
            cp.start()
            copies.append(cp)
        for k, f in enumerate(_FLIPS):
            peer = _flip(me, f)
            pltpu.make_async_remote_copy(
                src_ref=v_ref.at[_slot(peer)], dst_ref=out_ref.at[_slot(peer)], send_sem=send_sems.at[k],
                recv_sem=recv_sems.at[k], device_id=peer, device_id_type=pl.DeviceIdType.MESH).wait_recv()
        for cp in copies:
            cp.wait_send()
        mine.wait()

    return pl.pallas_call(
        body, name=name, out_shape=jax.ShapeDtypeStruct(v.shape, v.dtype),
        in_specs=[pl.BlockSpec(memory_space=pl.ANY)], out_specs=pl.BlockSpec(memory_space=pl.ANY),
        scratch_shapes=[pltpu.SemaphoreType.DMA((7,)), pltpu.SemaphoreType.DMA((7,)), pltpu.SemaphoreType.DMA(())],
    )(v)


def sum_slots(v, name, tr=256):
    _, r, c = v.shape
    tr = _tile_rows(r, tr)

    def body(v_ref, o_ref):
        acc = v_ref[0].astype(F32)
        for s in range(1, N_DEV):
            acc = acc + v_ref[s].astype(F32)
        o_ref[...] = acc

    return pl.pallas_call(body, name=name, grid=(r // tr,), in_specs=[pl.BlockSpec((N_DEV, tr, c), lambda i: (0, i, 0))],
                          out_specs=pl.BlockSpec((tr, c), lambda i: (i, 0)), out_shape=jax.ShapeDtypeStruct((r, c), F32),
                          compiler_params=_params())(v)


def _tile_rows(r, pref):
    if r <= pref:
        return r
    best = None
    for t in range(8, pref + 1, 8):
        if r % t == 0:
            best = t
    return r if best is None else best


def _adamw_math(w, m, v, g):
    nm = ADAM_B1 * m + (1.0 - ADAM_B1) * g
    nv = ADAM_B2 * v + (1.0 - ADAM_B2) * jnp.square(g)
    m_hat = nm / (1.0 - ADAM_B1 ** ADAM_STEP)
    v_hat = nv / (1.0 - ADAM_B2 ** ADAM_STEP)
    return -ADAM_LR * (m_hat / (jnp.sqrt(v_hat) + ADAM_EPS) + ADAM_WD * w), nm, nv


def update_from_slots(lands, offs, w, m, v, transposed, name):
    layers, a, b = w.shape
    n_land = len(lands)
    if transposed:
        rb, tk = LANE, 512
        assert a % tk == 0 and b % rb == 0 and all(o % rb == 0 for o in offs), (name, w.shape, offs)
        grid = (layers, a // tk, b // rb)
        land_block = (N_DEV, rb, tk)
        tile = pl.BlockSpec((None, tk, rb), lambda l, i, j: (l, i, j))

        def land_spec(layer):
            base = offs[layer] // rb
            return pl.BlockSpec(land_block, lambda l, i, j: (0, base + jnp.where(l == layer, j, 0), jnp.where(l == layer, i, 0)))
    else:
        tr = max(t for t in (256, 128, 64) if a % t == 0 and all(o % t == 0 for o in offs))
        grid = (layers, a // tr)
        land_block = (N_DEV, tr, b)
        tile = pl.BlockSpec((None, tr, b), lambda l, i: (l, i, 0))

        def land_spec(layer):
            base = offs[layer] // tr
            return pl.BlockSpec(land_block, lambda l, i: (0, base + jnp.where(l == layer, i, 0), 0))

    def body(*refs):
        land_refs, (w_ref, m_ref, v_ref, g_ref, d_ref, nm_ref, nv_ref, acc) = refs[:n_land], refs[n_land:]
        for layer, land in enumerate(land_refs):
            @pl.when(pl.program_id(0) == layer)
            def _(land=land):
                s = land[0].astype(F32)
                for k in range(1, N_DEV):
                    s = s + land[k].astype(F32)
                acc[...] = s

        g = acc[...].T if transposed else acc[...]
        d, nm, nv = _adamw_math(w_ref[...], m_ref[...], v_ref[...], g)
        g_ref[...] = g
        d_ref[...] = d
        nm_ref[...] = nm
        nv_ref[...] = nv

    sh = jax.ShapeDtypeStruct(w.shape, F32)
    return pl.pallas_call(
        body, name=name, grid=grid, in_specs=[land_spec(layer) for layer in range(n_land)] + [tile] * 3, out_specs=[tile] * 4,
        out_shape=[sh] * 4, scratch_shapes=[pltpu.VMEM(land_block[1:], F32)], compiler_params=_params())(*lands, w, m, v)


def adamw(w, m, v, g, name):
    r, c = w.shape
    tr = _tile_rows(r, 512 if c <= 1024 else 128)

    def body(w_ref, m_ref, v_ref, g_ref, d_ref, nm_ref, nv_ref):
        d_ref[...], nm_ref[...], nv_ref[...] = _adamw_math(w_ref[...], m_ref[...], v_ref[...], g_ref[...])

    spec = pl.BlockSpec((tr, c), lambda i: (i, 0))
    sh = jax.ShapeDtypeStruct((r, c), F32)
    return pl.pallas_call(body, name=name, grid=(r // tr,), in_specs=[spec] * 4, out_specs=[spec] * 3,
                          out_shape=[sh] * 3, compiler_params=_params())(w, m, v, g)


def seg_in(x, g):
    return (_rms(x, g),)


def seg_in_res(x, g):
    return x, _rms(x, g)


def seg_res(x, m, ga, gb):
    x1 = x + _rms(m, ga)
    return x1, _rms(x1, gb)


def seg_out(x, m, ga):
    return (x + _rms(m, ga),)


def act_epilogue(r):
    t = jnp.maximum(r, 0.0)
    return r, t * t


def act_bwd_epilogue(drr, r):
    return (drr * (2.0 * jnp.maximum(r, 0.0)),)


def seg_ln(v, g, b):
    mu = jnp.mean(v, axis=-1, keepdims=True)
    var = jnp.mean(jnp.square(v - mu), axis=-1, keepdims=True)
    vn = (v - mu) * lax.rsqrt(var + LN_EPS) * g + b
    return (jax.nn.silu(vn),)


def make_pool_fn(group):
    window = 2 ** (group + 1)

    def pool_fn(ug, pw, scale):
        s = ug
        for lvl in range(group + 1):
            s = s + shift(s, 2 ** lvl)
        cnt = jnp.minimum(lax.broadcasted_iota(jnp.int32, ug.shape, 0) + 1, window).astype(F32)
        return (bdot(s / cnt - ug, pw, 1, 0) * scale,)

    return pool_fn


def conv4_fn(xr, w, b):
    return (jax.nn.silu(cconv(xr, w, SSM_CONV) + b),)


def cd1_fn(u, dww, dwb, scw):
    val, gate, bg, cg, hh = (u[:, k * LANE:(k + 1) * LANE] for k in range(5))
    v = val * jax.nn.sigmoid(gate)
    vc = cconv(v, dww, CONF_K) + dwb
    sc = bg * cconv(cg * hh, scw, SC_K)
    return vc, sc


def attn_fn(q, kv):
    outs = []
    for h in range(XA_HEADS):
        cols = slice(h * XA_DH, (h + 1) * XA_DH)
        s = bdot(q[:, cols], kv[:, cols], 1, 1) / math.sqrt(XA_DH)
        p = jax.nn.softmax(s, axis=-1)
        outs.append(bdot(p, kv[:, D + h * XA_DH:D + (h + 1) * XA_DH], 1, 0))
    return (jnp.concatenate(outs, axis=1),)


def ssd_chunk(xbc, z, dtraw, dtb, alog, dsk, nw, h0, h1, h2, h3, e64, e64t, ecat, ecatt, tril, trilt):
    xs, bm, cm = xbc[:, :SSM_GSZ], xbc[:, SSM_GSZ:SSM_GSZ + SSM_N], xbc[:, SSM_GSZ + SSM_N:]
    hin = (h0, h1, h2, h3)
    dt = jax.nn.softplus(dtraw + dtb)
    a = -jnp.exp(alog)
    d_a = dt * a
    cs = cmatl(tril, trilt, d_a)
    cs_cat = cmat(cs, ecat, ecatt)
    cs64, cs128 = cs_cat[:, :SSM_GSZ], cs_cat[:, SSM_GSZ:]
    dt64 = cmat(dt, e64, e64t)
    row = lax.broadcasted_iota(jnp.int32, (8, LANE), 0)
    heads = jnp.where(row == 0, dsk, jnp.where(row == 1, jnp.sum(d_a, axis=0, keepdims=True), 0.0))
    heads64 = cmat(heads, e64, e64t)
    d64, tot64 = heads64[0:1, :], heads64[1:2, :]
    xdt = xs * dt64
    cb = bdot(cm, bm, 1, 1)
    li = lax.broadcasted_iota(jnp.int32, (CHUNK, CHUNK), 0)
    si = lax.broadcasted_iota(jnp.int32, (CHUNK, CHUNK), 1)
    causal = li >= si
    lane = lax.broadcasted_iota(jnp.int32, (CHUNK, LANE), 1)
    xw = xdt * jnp.exp(tot64 - cs64)
    ecs = jnp.exp(cs64)
    etot = jnp.exp(tot64)
    ycols, hout = [], []
    for j in range(4):
        sl = slice(j * LANE, (j + 1) * LANE)
        xj = xdt[:, sl]
        ys = []
        for hh in range(2):
            r = 2 * j + hh
            col = cs128[:, r * LANE:(r + 1) * LANE]
            decay = jnp.exp(jnp.where(causal, col - col.T, -1e30))
            ys.append(bdot(cb * decay, xj, 1, 0))
        y_diag = jnp.where(lane < SSM_P, ys[0], ys[1])
        y_off = bdot(cm, hin[j], 1, 0) * ecs[:, sl]
        ycols.append(y_diag + y_off)
        hout.append(etot[:, sl] * hin[j] + bdot(bm, xw[:, sl], 0, 0))
    y = jnp.concatenate(ycols, axis=1) + d64 * xs
    y = y * jax.nn.silu(z)
    yn = y * lax.rsqrt(jnp.mean(y * y, axis=-1, keepdims=True) + RMS_EPS) * nw
    return (yn,) + tuple(hout)


def _xbc_group(a, axis):
    parts = []
    for g in range(SSM_GROUPS):
        for start, width in ((g * SSM_GSZ, SSM_GSZ), (SSM_INNER + g * SSM_N, SSM_N), (SSM_INNER + (SSM_GROUPS + g) * SSM_N, SSM_N)):
            parts.append(lax.slice_in_dim(a, start, start + width, axis=axis))
    return jnp.concatenate(parts, axis=axis)


def _xbc_ungroup(a, axis):
    xs, bs, cs = [], [], []
    for g in range(SSM_GROUPS):
        base = g * SSM_XBC_G
        xs.append(lax.slice_in_dim(a, base, base + SSM_GSZ, axis=axis))
        bs.append(lax.slice_in_dim(a, base + SSM_GSZ, base + SSM_GSZ + SSM_N, axis=axis))
        cs.append(lax.slice_in_dim(a, base + SSM_GSZ + SSM_N, base + SSM_XBC_G, axis=axis))
    return jnp.concatenate(xs + bs + cs, axis=axis)


def _ssd_consts():
    h = np.arange(LANE)[:, None]
    e64 = np.stack([(h == g * 8 + np.arange(SSM_GSZ)[None, :] // SSM_P) for g in range(SSM_GROUPS)]).astype(np.float32)
    e128 = np.stack([(h == g * 8 + np.arange(8 * LANE)[None, :] // LANE) for g in range(SSM_GROUPS)]).astype(np.float32)
    ecat = np.concatenate([e64, e128], axis=2)
    tril = np.tril(np.ones((CHUNK, CHUNK), np.float32))
    return tuple(jnp.asarray(c, dtype=BF) for c in (e64, e64.transpose(0, 2, 1), ecat, ecat.transpose(0, 2, 1), tril, tril.T))


def _ssd_specs(nc, rev):
    def ci(c):
        return nc - 1 - c if rev else c

    def row(width, col):
        return pl.BlockSpec((CHUNK, width), lambda b, c: (b * nc + ci(c), col))

    def whole(shape):
        return pl.BlockSpec(shape, lambda b, c: (0,) * len(shape))

    data = [row(SSM_CONV_DIM, 0),
            row(SSM_GSZ, 1), row(SSM_GSZ, 2), row(LANE, 24)]
    par = [whole((1, LANE))] * 3 + [whole((1, SSM_INNER))]
    cst = [whole((SSM_GROUPS, LANE, SSM_GSZ)), whole((SSM_GROUPS, SSM_GSZ, LANE)), whole((SSM_GROUPS, LANE, 12 * LANE)),
           whole((SSM_GROUPS, 12 * LANE, LANE)), whole((CHUNK, CHUNK)), whole((CHUNK, CHUNK))]
    hsave = pl.BlockSpec((None, None, SSM_GROUPS, 4, SSM_N, LANE), lambda b, c: (b, ci(c), 0, 0, 0, 0))
    return data, par, cst, hsave, row, whole


def _ssd_group_args(g, xbc, z, dtr, dtb, alog, dsk, nw):
    return (xbc[:, g * SSM_XBC_G:(g + 1) * SSM_XBC_G], z[g], dtr, dtb, alog, dsk, nw[:, g * SSM_GSZ:(g + 1) * SSM_GSZ])


def ssd_fwd(xbc_act, u, dtb, alog, dsk, nw, consts, bsz, seq):
    nc = seq // CHUNK
    data, par, cst, hsave, row, _ = _ssd_specs(nc, False)

    def body(xbc, z0, z1, dtr, dtb_r, alog_r, dsk_r, nw_r, e64, e64t, ecat, ecatt, tril, trilt, yn_ref, hs_ref, h):
        @pl.when(pl.program_id(1) == 0)
        def _():
            h[...] = jnp.zeros_like(h)

        hs_ref[...] = h[...]
        ys = []
        for g in range(SSM_GROUPS):
            args = _ssd_group_args(g, xbc[...], (z0[...], z1[...]), dtr[...], dtb_r[...], alog_r[...], dsk_r[...], nw_r[...])
            outs = ssd_chunk(*args, h[g, 0], h[g, 1], h[g, 2], h[g, 3], e64[g], e64t[g], ecat[g], ecatt[g], tril[...], trilt[...])
            ys.append(outs[0])
            for j in range(4):
                h[g, j] = outs[1 + j]
        yn_ref[...] = jnp.concatenate(ys, axis=1).astype(yn_ref.dtype)

    t = bsz * seq
    return pl.pallas_call(
        body, name="ssd_fwd", grid=(bsz, nc), in_specs=data + par + cst, out_specs=[row(SSM_INNER, 0), hsave],
        out_shape=[jax.ShapeDtypeStruct((t, SSM_INNER), BF), jax.ShapeDtypeStruct((bsz, nc, SSM_GROUPS, 4, SSM_N, LANE), F32)],
        scratch_shapes=[pltpu.VMEM((SSM_GROUPS, 4, SSM_N, LANE), F32)], compiler_params=_params(),
    )(xbc_act, u, u, u, dtb, alog, dsk, nw, *consts)


def ssd_bwd(xbc_act, u, dtb, alog, dsk, nw, consts, hs, dmix, bsz, seq):
    nc = seq // CHUNK
    data, par, cst, hsave, row, whole = _ssd_specs(nc, True)
    t = bsz * seq
    pcol = POOL_W // SSM_GSZ

    def body(xbc, z0, z1, dtr, dtb_r, alog_r, dsk_r, nw_r, e64, e64t, ecat, ecatt, tril, trilt, hs_ref, dy0, dy1,
             dxbc, dz, ddt, ddtb, dalog, ddsk, dnw, dh):
        @pl.when(pl.program_id(1) == 0)
        def _():
            dh[...] = jnp.zeros_like(dh)

        per_group = []
        for g, dyn in enumerate((dy0, dy1)):
            cst_vals = (e64[g], e64t[g], ecat[g], ecatt[g], tril[...], trilt[...])
            prim = _ssd_group_args(g, xbc[...], (z0[...], z1[...]), dtr[...], dtb_r[...], alog_r[...], dsk_r[...], nw_r[...])
            prim = prim + (hs_ref[g, 0], hs_ref[g, 1], hs_ref[g, 2], hs_ref[g, 3])
            _, vjp = jax.vjp(lambda *args, c=cst_vals: ssd_chunk(*args, *c), *prim)
            gr = vjp((dyn[...].astype(F32), dh[g, 0], dh[g, 1], dh[g, 2], dh[g, 3]))
            for j in range(4):
                dh[g, j] = gr[7 + j]
            per_group.append(gr)
        g0, g1 = per_group
        dxbc[...] = jnp.concatenate([g0[0], g1[0]], axis=1)
        dz[...] = jnp.concatenate([g0[1], g1[1]], axis=1).astype(dz.dtype)
        ddt[...] = g0[2] + g1[2]

        @pl.when(_first((0, 1)))
        def _():
            for r in (ddtb, dalog, ddsk, dnw):
                r[...] = jnp.zeros_like(r)

        ddtb[...] += g0[3] + g1[3]
        dalog[...] += g0[4] + g1[4]
        ddsk[...] += g0[5] + g1[5]
        dnw[...] += jnp.concatenate([g0[6], g1[6]], axis=1)

    out_specs = [row(SSM_CONV_DIM, 0), row(SSM_INNER, 0), row(LANE, 0), whole((1, LANE)), whole((1, LANE)), whole((1, LANE)),
                 whole((1, SSM_INNER))]
    lane = jax.ShapeDtypeStruct((1, LANE), F32)
    out_shape = [jax.ShapeDtypeStruct((t, SSM_CONV_DIM), F32), jax.ShapeDtypeStruct((t, SSM_INNER), BF),
                 jax.ShapeDtypeStruct((t, LANE), F32), lane, lane, lane, jax.ShapeDtypeStruct((1, SSM_INNER), F32)]
    return pl.pallas_call(
        body, name="ssd_bwd", grid=(bsz, nc), in_specs=data + par + cst + [hsave, row(SSM_GSZ, pcol), row(SSM_GSZ, pcol + 1)],
        out_specs=out_specs, out_shape=out_shape, scratch_shapes=[pltpu.VMEM((SSM_GROUPS, 4, SSM_N, LANE), F32)],
        compiler_params=_params(),
    )(xbc_act, u, u, u, dtb, alog, dsk, nw, *consts, hs, dmix, dmix)


TB = 512


def _rows(d, col=0):
    return pl.BlockSpec((TB, d), lambda i: (i, col))


def _par(d):
    return pl.BlockSpec((1, d), lambda i: (0, 0))


def _sd(shape, dtype=F32):
    return jax.ShapeDtypeStruct(shape, dtype)


def _round_up(n, m):
    return -(-n // m) * m


def _pad_rows(a, rows):
    return jnp.pad(a, ((0, rows - a.shape[0]), (0, 0)))


def _pack128(arrs):
    flat = jnp.concatenate([a.reshape(-1) for a in arrs])
    n = flat.shape[0]
    rows = -(-n // (8 * LANE)) * 8
    return jnp.pad(flat, (0, rows * LANE - n)).reshape(rows, LANE)


def _unpack128(packed, shapes):
    flat = packed.reshape(-1)
    out, off = [], 0
    for s in shapes:
        n = int(np.prod(s))
        out.append(flat[off:off + n].reshape(s))
        off += n
    return out


def kernel(x, mem, norm_gains, xa_wq, xa_wkv, xa_wo, mlp_w1, mlp_w2, ab_w_in, pool_w, pool_scale, ssm_conv_w, ssm_conv_b, ssm_dt_bias, ssm_a_log, ssm_d, ssm_norm, ab_w_out, cd_w_in, conf_dw_w, conf_dw_b, conf_ln_g, conf_ln_b, sc_conv_w, cd_w_out, loss_target, m_norm_gains, m_xa_wq, m_xa_wkv, m_xa_wo, m_mlp_w1, m_mlp_w2, m_ab_w_in, m_pool_w, m_pool_scale, m_ssm_conv_w, m_ssm_conv_b, m_ssm_dt_bias, m_ssm_a_log, m_ssm_d, m_ssm_norm, m_ab_w_out, m_cd_w_in, m_conf_dw_w, m_conf_dw_b, m_conf_ln_g, m_conf_ln_b, m_sc_conv_w, m_cd_w_out, v_norm_gains, v_xa_wq, v_xa_wkv, v_xa_wo, v_mlp_w1, v_mlp_w2, v_ab_w_in, v_pool_w, v_pool_scale, v_ssm_conv_w, v_ssm_conv_b, v_ssm_dt_bias, v_ssm_a_log, v_ssm_d, v_ssm_norm, v_ab_w_out, v_cd_w_in, v_conf_dw_w, v_conf_dw_b, v_conf_ln_g, v_conf_ln_b, v_sc_conv_w, v_cd_w_out):
    args = locals()
    w = {n: args[n] for n in WEIGHTS}
    mom_m = {n: args["m_" + n] for n in WEIGHTS}
    mom_v = {n: args["v_" + n] for n in WEIGHTS}
    ex = Exchange(w)
    loss_local, grad_x, small_grads = local_step(x, mem, loss_target, ex)
    loss = lax.psum(loss_local, ("x", "y", "c"))
    outs = {}

    def update_big(names, own):
        last = None
        for n in names:
            shp = w[n].shape
            view = (-1, shp[-1])
            g = jnp.stack([own[(n, layer)] for layer in range(shp[0])])
            d, nm, nv = adamw(w[n].reshape(view), mom_m[n].reshape(view), mom_v[n].reshape(view), g.reshape(view), "adamw_" + n)
            outs[n] = (g, d.reshape(shp), nm.reshape(shp), nv.reshape(shp))
            last = d
        return last

    started = ex.put_small(small_grads)
    landed = {key: ex.landed(key, started) for key in ('l1', 'cd', 'l0')}
    late = []
    for n, keys in (('mlp_w1', ('l0', 'l1')), ('mlp_w2', ('l0', 'l1')), ('xa_wkv', ('l0', 'l1')), ('xa_wq', ('l0', 'l1')),
                    ('xa_wo', ('l0', 'l1')), ('cd_w_in', ('cd',)), ('cd_w_out', ('cd',))):
        lands = [landed[key][0] for key in keys]
        offs = [landed[key][1][(n, layer)] for layer, key in enumerate(keys)]
        outs[n] = update_from_slots(lands, offs, w[n], mom_m[n], mom_v[n], SHARD_AXIS[n] == 2, "update_" + n)
        late.append(outs[n][1])
    g_own = ex.reduced_small(late)
    update_big(['ab_w_in', 'ab_w_out'], ex.reduced('ab', late))
    small = SMALL_SHARDED + REPLICATED
    shapes = [w[n].shape for n in small]
    d, nm, nv = adamw(_pack128([w[n] for n in small]), _pack128([mom_m[n] for n in small]), _pack128([mom_v[n] for n in small]),
                      _pack128([g_own[n] for n in small]), "adamw_small")
    for n, dd, mm, vv in zip(small, _unpack128(d, shapes), _unpack128(nm, shapes), _unpack128(nv, shapes)):
        outs[n] = (g_own[n], dd, mm, vv)
    return (loss, grad_x.reshape(x.shape), *[outs[n][0] for n in WEIGHTS], *[outs[n][1] for n in WEIGHTS],
            *[outs[n][2] for n in WEIGHTS], *[outs[n][3] for n in WEIGHTS])


G_AB = (('ab_w_in', 0), ('ab_w_out', 0))
G_L0 = (('xa_wq', 0), ('xa_wkv', 0), ('xa_wo', 0), ('mlp_w1', 0), ('mlp_w2', 0))
G_L1 = (('xa_wq', 1), ('xa_wkv', 1), ('xa_wo', 1), ('mlp_w1', 1), ('mlp_w2', 1))
G_CD = (('cd_w_in', 0), ('cd_w_out', 0))
GATHER_CHAIN = {'l0': ('cd', G_CD), 'cd': ('l1', G_L1)}
SHARD_AXIS = dict(BIG)
MEMBER_ROW_TILE = 64
FLAT_ROW_TILE = 128


def _members(group, w):
    out = []
    for n, layer in group:
        shp = w[n].shape[1:]
        if SHARD_AXIS[n] == 2:
            shp = (shp[1], shp[0])
        assert shp[1] == D, (n, shp)
        out.append((n, layer, shp, shp[0], _round_up(shp[0], MEMBER_ROW_TILE)))
    return out


def _group_rows(group, w):
    return _round_up(sum(m[4] for m in _members(group, w)), FLAT_ROW_TILE)


def _flat_shards(group, w):
    parts = []
    for n, layer, _, _, padded in _members(group, w):
        shard = w[n][layer].astype(BF)
        parts.append(_pad_rows(shard.T if SHARD_AXIS[n] == 2 else shard, padded))
    return _pad_rows(jnp.concatenate(parts, axis=0), _group_rows(group, w))


def _full_from_slots(land, group, w):
    out, off = {}, 0
    for n, layer, shp, rows, padded in _members(group, w):
        out[(n, layer)] = land[:, off:off + rows].reshape(N_DEV * rows, D)
        off += padded
    return out


def _slots_from_full(grads, group, w):
    parts = []
    for n, layer, shp, rows, padded in _members(group, w):
        blk = grads[(n, layer)].astype(BF).reshape(N_DEV, rows, D)
        parts.append(jnp.pad(blk, ((0, 0), (0, padded - rows), (0, 0))))
    send = jnp.concatenate(parts, axis=1)
    return jnp.pad(send, ((0, 0), (0, _group_rows(group, w) - send.shape[1]), (0, 0)))


def _own_from_sum(summed, group, w):
    out, off = {}, 0
    for n, layer, shp, rows, padded in _members(group, w):
        g = summed[off:off + rows]
        out[(n, layer)] = g.T if SHARD_AXIS[n] == 2 else g
        off += padded
    return out


_HBM = pl.BlockSpec(memory_space=pltpu.HBM)
_SEM = pl.BlockSpec(memory_space=pltpu.SEMAPHORE)
_ANY = pl.BlockSpec(memory_space=pl.ANY)


def _peer_copy(k, src, dst, send_sems, recv_sems, peer):
    return pltpu.make_async_remote_copy(src_ref=src, dst_ref=dst, send_sem=send_sems.at[k], recv_sem=recv_sems.at[k],
                                        device_id=peer, device_id_type=pl.DeviceIdType.MESH)


def exchange_start(src, name, scatter):
    shape = src.shape[-2:]

    def body(src_ref, land_ref, send_sems, recv_sems, src_thru, land_thru, token):
        me = _me()
        for k, f in enumerate(_FLIPS):
            peer = _flip(me, f)
            piece = src_ref.at[_slot(peer)] if scatter else src_ref
            _peer_copy(k, piece, land_ref.at[_slot(me)], send_sems, recv_sems, peer).start()
        token[...] = jnp.zeros_like(token)

    land = pltpu.with_memory_space_constraint(lax.empty((N_DEV,) + shape, src.dtype), pltpu.HBM)
    return pl.pallas_call(
        body, name=name,
        out_shape=(pltpu.SemaphoreType.DMA((7,)), pltpu.SemaphoreType.DMA((7,)), pltpu.HBM(src.shape, src.dtype),
                   pltpu.HBM((N_DEV,) + shape, src.dtype), jax.ShapeDtypeStruct((8, LANE), F32)),
        in_specs=(_HBM, _HBM), out_specs=(_SEM, _SEM, _HBM, _HBM, pl.BlockSpec(memory_space=pltpu.VMEM)),
        input_output_aliases={0: 2, 1: 3},
        compiler_params=pltpu.CompilerParams(has_side_effects=pltpu.SideEffectType.DATAFLOW_SIDE_EFFECTING),
    )(pltpu.with_memory_space_constraint(src, pltpu.HBM), land)


def exchange_wait(handles, after, name, scatter):
    send_sems, recv_sems, src_thru, land_thru, _ = handles
    after = list(after) if isinstance(after, (list, tuple)) else [after]

    def body(src_ref, land_ref, send_sems, recv_sems, *rest):
        token = rest[-1]
        me = _me()
        for k, f in enumerate(_FLIPS):
            peer = _flip(me, f)
            piece = src_ref.at[_slot(peer)] if scatter else src_ref
            cp = _peer_copy(k, piece, land_ref.at[_slot(peer)], send_sems, recv_sems, peer)
            cp.wait_send()
            cp.wait_recv()
        token[...] = jnp.zeros_like(token)

    return pl.pallas_call(
        body, name=name, out_shape=(pltpu.HBM(src_thru.shape, src_thru.dtype), pltpu.HBM(land_thru.shape, land_thru.dtype),
                                    jax.ShapeDtypeStruct((8, LANE), F32)),
        in_specs=(_HBM, _HBM, _SEM, _SEM) + (_ANY,) * len(after), out_specs=(_HBM, _HBM, pl.BlockSpec(memory_space=pltpu.VMEM)),
        input_output_aliases={0: 0, 1: 1},
        compiler_params=pltpu.CompilerParams(has_side_effects=pltpu.SideEffectType.DATAFLOW_SIDE_EFFECTING),
    )(src_thru, land_thru, send_sems, recv_sems, *after)


class Exchange:
    def __init__(self, w):
        self.w = w
        self.me = _slot(_me())
        shapes = [w[n].shape for n in SMALL_SHARDED]
        gs = all_gather(_pack128([w[n] for n in SMALL_SHARDED]), "gather_small")
        per_dev = [_unpack128(gs[d], shapes) for d in range(N_DEV)]
        self.small = {n: jnp.concatenate([per_dev[d][i] for d in range(N_DEV)], axis=-1) for i, n in enumerate(SMALL_SHARDED)}
        self.small.update({n: w[n] for n in REPLICATED})
        self.now = _full_from_slots(all_gather(_flat_shards(G_AB, w), "gather_ab"), G_AB, w)
        self.gathers = {'l0': (G_L0, exchange_start(_flat_shards(G_L0, w), "gather_l0_start", False))}
        self.tokens = [self.gathers['l0'][1][4]]
        self.reductions = {}

    def take_tokens(self):
        toks, self.tokens = self.tokens, []
        return toks

    def weights(self, key, after):
        if key == 'ab':
            return self.now
        group, handles = self.gathers[key]
        _, land, done = exchange_wait(handles, after, f"gather_{key}_wait", False)
        nxt = GATHER_CHAIN.get(key)
        if nxt is not None:
            src = _flat_shards(nxt[1], self.w) + done[0, 0].astype(BF)
            self.gathers[nxt[0]] = (nxt[1], exchange_start(src, f"gather_{nxt[0]}_start", False))
            self.tokens.append(self.gathers[nxt[0]][1][4])
        land = lax.dynamic_update_slice(land, handles[2][None], (self.me, 0, 0))
        return _full_from_slots(land, group, self.w)

    def put_grads(self, key, group, grads):
        send = _slots_from_full(grads, group, self.w)
        handles = exchange_start(send, f"reduce_{key}_start", True)
        self.reductions[key] = (group, handles)
        self.tokens.append(handles[4])

    def landed(self, key, after):
        group, handles = self.reductions[key]
        send, land, _ = exchange_wait(handles, after, f"reduce_{key}_wait", True)
        mine = lax.dynamic_slice_in_dim(send, self.me, 1, axis=0)
        land = lax.dynamic_update_slice(land, mine, (self.me, 0, 0))
        offs, off = {}, 0
        for n, layer, _, _, padded in _members(group, self.w):
            offs[(n, layer)] = off
            off += padded
        return land, offs

    def reduced(self, key, after):
        land, _ = self.landed(key, after)
        return _own_from_sum(sum_slots(land, f"sum_{key}", FLAT_ROW_TILE), self.reductions[key][0], self.w)

    def put_small(self, small_grads):
        small = SMALL_SHARDED + REPLICATED
        self.small_shapes = [small_grads[n].shape for n in small]
        self.small_handles = exchange_start(_pack128([small_grads[n] for n in small]), "gather_small_grads_start", False)
        return self.small_handles[4]

    def reduced_small(self, after):
        small = SMALL_SHARDED + REPLICATED
        src, land, _ = exchange_wait(self.small_handles, after, "gather_small_grads_wait", False)
        gs = lax.dynamic_update_slice(land, src[None], (self.me, 0, 0))
        tot = _unpack128(sum_slots(gs, "sum_small", 1024), self.small_shapes)
        out = {}
        for n, g in zip(small, tot):
            if n in SMALL_SHARDED:
                width = self.w[n].shape[-1]
                g = lax.dynamic_slice_in_dim(g, self.me * width, width, axis=g.ndim - 1)
            out[n] = g
        return out


def local_step(x, mem, target, ex):
    bsz, seq, _ = x.shape
    t = bsz * seq
    nb = t // TB
    nc = seq // CHUNK
    x0 = x.reshape(t, D)
    mem2 = mem.reshape(bsz * N_MEM, D)
    tgt = target.reshape(t, D)
    p = ex.small
    gains = p['norm_gains']
    big = dict(ex.weights('ab', None))

    def gain(layer, i):
        g = gains[layer, i].reshape(1, D)
        for tok in ex.take_tokens():
            g = g + tok[0, 0]
        return g

    consts = _ssd_consts()
    grads = {}
    saved = [dict(), dict()]

    def run_seg_res(xin, m, ga, gb, name):
        return fwd_call(seg_res, name, (nb,), [xin, m, ga, gb], [_rows(D), _rows(D), _par(D), _par(D)],
                        [_sd((t, D)), _sd((t, D), BF)], [_rows(D), _rows(D)])

    def attn_specs():
        nq = seq // TB
        q = pl.BlockSpec((TB, D), lambda b, i: (b * nq + i, 0))
        kv = pl.BlockSpec((N_MEM, 2 * D), lambda b, i: (b, 0))
        return (bsz, nq), q, kv

    def attention_fwd(layer, xin, hin, sv):
        q = matmul(hin, big[('xa_wq', layer)], 'nn', f"q_{layer}", BF)
        kv = matmul(mem2, big[('xa_wkv', layer)], 'nt', f"kv_{layer}", BF)
        grid, qs, kvs = attn_specs()
        o, = fwd_call(attn_fn, f"attn_{layer}", grid, [q, kv], [qs, kvs], [_sd((t, D), BF)], [qs])
        ao = matmul(o, big[('xa_wo', layer)], 'nn', f"ao_{layer}")
        sv.update(q=q, kv=kv, o=o, ao=ao)
        return ao

    def mlp_fwd(layer, hin, sv):
        r, rr = matmul(hin, big[('mlp_w1', layer)], 'nt', f"mlp1_{layer}", (BF, BF), epilogue=act_epilogue)
        mo = matmul(rr, big[('mlp_w2', layer)], 'nn', f"mlp2_{layer}")
        sv.update(r=r, rr=rr, mo=mo)
        return mo

    sv = saved[0]
    h0, = fwd_call(seg_in, "norm_in", (nb,), [x0, gain(0, 0)], [_rows(D), _par(D)], [_sd((t, D), BF)], [_rows(D)])
    xbc0 = POOL_W + SSM_INNER
    w_ab_in = big[('ab_w_in', 0)]
    w_ab_in = _pad_rows(jnp.concatenate([w_ab_in[:xbc0], _xbc_group(w_ab_in[xbc0:xbc0 + SSM_CONV_DIM], 0),
                                         w_ab_in[xbc0 + SSM_CONV_DIM:]], axis=0), AB_IN_PAD)
    conv_w, conv_b = _xbc_group(p['ssm_conv_w'][0], 1), _xbc_group(p['ssm_conv_b'], 1)
    u0 = matmul(h0, w_ab_in, 'nt', "ab_in")
    pool_outs = []
    for g in range(POOL_GROUPS):
        seqspec = pl.BlockSpec((seq, PG), lambda b, g=g: (b, g))
        po, = fwd_call(make_pool_fn(g), f"pool_{g}", (bsz,), [u0, p['pool_w'][0, g], p['pool_scale']],
                       [seqspec, pl.BlockSpec((PG, PG), lambda b: (0, 0)), pl.BlockSpec((1, PG), lambda b, g=g: (0, g))],
                       [_sd((t, PG), BF)], [pl.BlockSpec((seq, PG), lambda b: (b, 0))])
        pool_outs.append(po)
    cw = 256
    ncb = SSM_CONV_DIM // cw
    cbase = (POOL_W + SSM_INNER) // cw
    conv_in_specs = [pl.BlockSpec((seq, cw), lambda j, b: (b, cbase + j)), pl.BlockSpec((SSM_CONV, cw), lambda j, b: (0, j)),
                     pl.BlockSpec((1, cw), lambda j, b: (0, j))]
    conv_out_spec = pl.BlockSpec((seq, cw), lambda j, b: (b, j))
    xbc_act, = fwd_call(conv4_fn, "ssm_conv", (ncb, bsz), [u0, conv_w, conv_b], conv_in_specs,
                        [_sd((t, SSM_CONV_DIM))], [conv_out_spec])
    dtb = jnp.pad(p['ssm_dt_bias'], ((0, 0), (0, LANE - SSM_HEADS)))
    alog = jnp.pad(p['ssm_a_log'], ((0, 0), (0, LANE - SSM_HEADS)))
    dsk = jnp.pad(p['ssm_d'], ((0, 0), (0, LANE - SSM_HEADS)))
    yn, hs = ssd_fwd(xbc_act, u0, dtb, alog, dsk, p['ssm_norm'], consts, bsz, seq)
    mix0 = jnp.concatenate(pool_outs + [yn], axis=1)
    m0 = matmul(mix0, big[('ab_w_out', 0)], 'nn', "ab_out")
    x1, h2 = run_seg_res(x0, m0, gain(0, 1), gain(0, 2), "res_0a")
    big.update(ex.weights('l0', h2))
    ao0 = attention_fwd(0, x1, h2, sv)
    x2, h3 = run_seg_res(x1, ao0, gain(0, 3), gain(0, 4), "res_0b")
    mo0 = mlp_fwd(0, h3, sv)
    big.update(ex.weights('cd', sv['r']))
    x3, h4 = run_seg_res(x2, mo0, gain(0, 5), gain(1, 0), "res_0c")

    sv1 = saved[1]
    nd = D // LANE
    w_cd_in = big[('cd_w_in', 0)].reshape(5, nd, LANE, D).transpose(1, 0, 2, 3).reshape(CD_IN, D)
    u1 = matmul(h4, w_cd_in, 'nt', "cd_in")
    cd_par = [pl.BlockSpec((CONF_K, LANE), lambda j, b: (0, j)), pl.BlockSpec((1, LANE), lambda j, b: (0, j)),
              pl.BlockSpec((SC_K, LANE), lambda j, b: (0, j))]
    cd_ins = [u1, p['conf_dw_w'][0], p['conf_dw_b'], p['sc_conv_w'][0]]
    cd_u_spec = pl.BlockSpec((seq, 5 * LANE), lambda j, b: (b, j))
    cd_in_specs = [cd_u_spec] + cd_par
    cd_out_spec = pl.BlockSpec((seq, LANE), lambda j, b: (b, j))
    vconv, sc_out = fwd_call(cd1_fn, "cd_conv", (nd, bsz), cd_ins, cd_in_specs, [_sd((t, D)), _sd((t, D), BF)],
                             [cd_out_spec, cd_out_spec])
    conf, = fwd_call(seg_ln, "conf_ln", (nb,), [vconv, p['conf_ln_g'], p['conf_ln_b']], [_rows(D), _par(D), _par(D)],
                     [_sd((t, D), BF)], [_rows(D)])
    mix1 = jnp.concatenate([conf, sc_out], axis=1)
    m1 = matmul(mix1, big[('cd_w_out', 0)], 'nn', "cd_out")
    x4, h5 = run_seg_res(x3, m1, gain(1, 1), gain(1, 2), "res_1a")
    big.update(ex.weights('l1', h5))
    ao1 = attention_fwd(1, x4, h5, sv1)
    x5, h6 = run_seg_res(x4, ao1, gain(1, 3), gain(1, 4), "res_1b")
    mo1 = mlp_fwd(1, h6, sv1)

    def loss_body(x_ref, m_ref, g_ref, t_ref, dy_ref, acc_ref):
        y = x_ref[...] + _rms(m_ref[...], g_ref[...])
        d = y - t_ref[...]
        dy_ref[...] = d / float(D)

        @pl.when(pl.program_id(0) == 0)
        def _():
            acc_ref[...] = jnp.zeros_like(acc_ref)

        acc_ref[...] += jnp.sum(d * d, axis=0, keepdims=True)

    dy, lanes = pl.pallas_call(
        loss_body, name="loss_head", grid=(nb,), in_specs=[_rows(D), _rows(D), _par(D), _rows(D)],
        out_specs=[_rows(D), _par(D)], out_shape=[_sd((t, D)), _sd((1, D))], compiler_params=_params())(x5, mo1, gain(1, 5), tgt)
    loss = 0.5 * jnp.sum(lanes) / float(D)

    gain_grads = {}

    def bwd_seg_out(xin, m, ga, dyv, name):
        dx, dm, dga = bwd_call(seg_out, name, (nb,), [xin, m, ga], [_rows(D), _rows(D), _par(D)], [dyv], [_rows(D)],
                               [0, 1, 2], [_sd((t, D)), _sd((t, D), BF), _sd((1, D))], [_rows(D), _rows(D), _par(D)],
                               [None, None, (0,)])
        return dx, dm, dga

    def bwd_seg_res(xin, m, ga, gb, dx1, dh, name):
        return bwd_call(seg_res, name, (nb,), [xin, m, ga, gb], [_rows(D), _rows(D), _par(D), _par(D)], [dx1, dh],
                        [_rows(D), _rows(D)], [0, 1, 2, 3], [_sd((t, D)), _sd((t, D), BF), _sd((1, D)), _sd((1, D))],
                        [_rows(D), _rows(D), _par(D), _par(D)], [None, None, (0,), (0,)])

    def mlp_bwd(layer, hin, dmo, sv):
        grads_w2 = matmul(sv['rr'], dmo, 'tn', f"d_mlp_w2_{layer}", BF)
        dr, = matmul(dmo, big[('mlp_w2', layer)], 'nt', f"d_r_{layer}", (BF,), epilogue=act_bwd_epilogue, extras=[sv['r']])
        grads_w1 = matmul(dr, hin, 'tn', f"d_mlp_w1_{layer}", BF)
        dh = matmul(dr, big[('mlp_w1', layer)], 'nn', f"d_h_mlp_{layer}")
        return dh, grads_w1, grads_w2

    def attention_bwd(layer, hin, dao, sv):
        g_wo = matmul(sv['o'], dao, 'tn', f"d_xa_wo_{layer}", BF)
        do = matmul(dao, big[('xa_wo', layer)], 'nt', f"d_o_{layer}", BF)
        grid, qs, kvs = attn_specs()
        dq, dkv = bwd_call(attn_fn, f"d_attn_{layer}", grid, [sv['q'], sv['kv']], [qs, kvs], [do], [qs], [0, 1],
                           [_sd((t, D), BF), _sd((bsz * N_MEM, 2 * D))], [qs, kvs], [None, (1,)])
        g_wkv = matmul(dkv, mem2, 'tn', f"d_xa_wkv_{layer}", BF)
        g_wq = matmul(hin, dq, 'tn', f"d_xa_wq_{layer}", BF)
        dh = matmul(dq, big[('xa_wq', layer)], 'nt', f"d_h_attn_{layer}")
        return dh, g_wq, g_wkv, g_wo

    per_layer = {k: [None, None] for k in ('xa_wq', 'xa_wkv', 'xa_wo', 'mlp_w1', 'mlp_w2')}

    dx5, dmo1, gain_grads[(1, 5)] = bwd_seg_out(x5, mo1, gain(1, 5), dy, "d_out")
    dh6, per_layer['mlp_w1'][1], per_layer['mlp_w2'][1] = mlp_bwd(1, h6, dmo1, sv1)
    dx4, dao1, gain_grads[(1, 3)], gain_grads[(1, 4)] = bwd_seg_res(x4, ao1, gain(1, 3), gain(1, 4), dx5, dh6, "d_res_1b")
    dh5, per_layer['xa_wq'][1], per_layer['xa_wkv'][1], per_layer['xa_wo'][1] = attention_bwd(1, h5, dao1, sv1)
    ex.put_grads('l1', G_L1, {(k, 1): v[1] for k, v in per_layer.items()})
    dx3, dm1, gain_grads[(1, 1)], gain_grads[(1, 2)] = bwd_seg_res(x3, m1, gain(1, 1), gain(1, 2), dx4, dh5, "d_res_1a")
    g_cd_out = matmul(mix1, dm1, 'tn', "d_cd_w_out", BF)
    dmix1 = matmul(dm1, big[('cd_w_out', 0)], 'nt', "d_mix1")
    dvconv, dlg, dlb = bwd_call(seg_ln, "d_conf_ln", (nb,), [vconv, p['conf_ln_g'], p['conf_ln_b']],
                                [_rows(D), _par(D), _par(D)], [dmix1], [_rows(D, 0)], [0, 1, 2],
                                [_sd((t, D)), _sd((1, D)), _sd((1, D))], [_rows(D), _par(D), _par(D)], [None, (0,), (0,)])
    grads['conf_ln_g'], grads['conf_ln_b'] = dlg, dlb
    cd_g = bwd_call(cd1_fn, "d_cd_conv", (nd, bsz), cd_ins, cd_in_specs, [dvconv, dmix1],
                    [cd_out_spec, pl.BlockSpec((seq, LANE), lambda j, b: (b, nd + j))], list(range(4)),
                    [_sd((t, CD_IN), BF), _sd((CONF_K, D)), _sd((1, D)), _sd((SC_K, D))], [cd_u_spec] + cd_par,
                    [None, (1,), (1,), (1,)])
    du1 = cd_g[0]
    grads['conf_dw_w'], grads['conf_dw_b'], grads['sc_conv_w'] = cd_g[1][None], cd_g[2], cd_g[3][None]
    g_cd_in = matmul(du1, h4, 'tn', "d_cd_w_in", BF).reshape(nd, 5, LANE, D).transpose(1, 0, 2, 3).reshape(CD_IN, D)
    ex.put_grads('cd', G_CD, {('cd_w_in', 0): g_cd_in, ('cd_w_out', 0): g_cd_out})
    dh4 = matmul(du1, w_cd_in, 'nn', "d_h_cd")

    dx2, dmo0, gain_grads[(0, 5)], gain_grads[(1, 0)] = bwd_seg_res(x2, mo0, gain(0, 5), gain(1, 0), dx3, dh4, "d_res_0c")
    dh3, per_layer['mlp_w1'][0], per_layer['mlp_w2'][0] = mlp_bwd(0, h3, dmo0, sv)
    dx1, dao0, gain_grads[(0, 3)], gain_grads[(0, 4)] = bwd_seg_res(x1, ao0, gain(0, 3), gain(0, 4), dx2, dh3, "d_res_0b")
    dh2, per_layer['xa_wq'][0], per_layer['xa_wkv'][0], per_layer['xa_wo'][0] = attention_bwd(0, h2, dao0, sv)
    ex.put_grads('l0', G_L0, {(k, 0): v[0] for k, v in per_layer.items()})
    dx0r, dm0, gain_grads[(0, 1)], gain_grads[(0, 2)] = bwd_seg_res(x0, m0, gain(0, 1), gain(0, 2), dx1, dh2, "d_res_0a")
    g_ab_out = matmul(mix0, dm0, 'tn', "d_ab_w_out", BF)
    dmix0 = matmul(dm0, big[('ab_w_out', 0)], 'nt', "d_mix0")
    dxbc_act, dz, ddt, ddtb, dalog, ddsk, dnw = ssd_bwd(xbc_act, u0, dtb, alog, dsk, p['ssm_norm'], consts, hs, dmix0, bsz, seq)
    grads['ssm_dt_bias'] = ddtb[:, :SSM_HEADS]
    grads['ssm_a_log'] = dalog[:, :SSM_HEADS]
    grads['ssm_d'] = ddsk[:, :SSM_HEADS]
    grads['ssm_norm'] = dnw
    dxr, dcw, dcb = bwd_call(conv4_fn, "d_ssm_conv", (ncb, bsz), [u0, conv_w, conv_b], conv_in_specs,
                             [dxbc_act], [conv_out_spec], [0, 1, 2],
                             [_sd((t, SSM_CONV_DIM), BF), _sd((SSM_CONV, SSM_CONV_DIM)), _sd((1, SSM_CONV_DIM))],
                             [conv_out_spec, conv_in_specs[1], conv_in_specs[2]], [None, (1,), (1,)])
    grads['ssm_conv_w'], grads['ssm_conv_b'] = _xbc_ungroup(dcw, 1)[None], _xbc_ungroup(dcb, 1)
    dpool, dpw, dps = [], [], []
    for g in range(POOL_GROUPS):
        seqspec = pl.BlockSpec((seq, PG), lambda b, g=g: (b, g))
        one = pl.BlockSpec((seq, PG), lambda b: (b, 0))
        wspec = pl.BlockSpec((PG, PG), lambda b: (0, 0))
        sspec = pl.BlockSpec((1, PG), lambda b, g=g: (0, g))
        a, bb, c = bwd_call(make_pool_fn(g), f"d_pool_{g}", (bsz,), [u0, p['pool_w'][0, g], p['pool_scale']],
                            [seqspec, wspec, sspec], [dmix0], [seqspec], [0, 1, 2],
                            [_sd((t, PG), BF), _sd((PG, PG)), _sd((1, PG))], [one, wspec, pl.BlockSpec((1, PG), lambda b: (0, 0))],
                            [None, (0,), (0,)])
        dpool.append(a)
        dpw.append(bb)
        dps.append(c)
    grads['pool_w'] = jnp.stack(dpw)[None]
    grads['pool_scale'] = jnp.concatenate(dps, axis=1)
    du0 = jnp.concatenate(dpool + [dz, dxr, ddt.astype(BF)], axis=1)
    g_ab_in = matmul(du0, h0, 'tn', "d_ab_w_in", BF)
    g_ab_in = jnp.concatenate([g_ab_in[:xbc0], _xbc_ungroup(g_ab_in[xbc0:xbc0 + SSM_CONV_DIM], 0),
                               g_ab_in[xbc0 + SSM_CONV_DIM:AB_IN]], axis=0)
    ex.put_grads('ab', G_AB, {('ab_w_in', 0): g_ab_in, ('ab_w_out', 0): g_ab_out})
    dh0 = matmul(du0, w_ab_in, 'nn', "d_h_ab")
    dx, dg00 = bwd_call(seg_in_res, "d_norm_in", (nb,), [x0, gain(0, 0)], [_rows(D), _par(D)], [dx0r, dh0],
                        [_rows(D), _rows(D)], [0, 1], [_sd((t, D)), _sd((1, D))], [_rows(D), _par(D)], [None, (0,)])
    gain_grads[(0, 0)] = dg00
    grads['norm_gains'] = jnp.stack([jnp.concatenate([gain_grads[(l, i)] for i in range(6)], axis=0) for l in range(2)])
    return loss, dx, grads
```

```python
import functools
import math

import numpy as np
import jax
import jax.numpy as jnp
from jax import lax
from jax.experimental import pallas as pl
from jax.experimental.pallas import tpu as pltpu

BF = jnp.bfloat16
F32 = jnp.float32
HI = lax.Precision.HIGHEST

N_DEV = 8
D = 1024
N_MEM = 256
XA_HEADS = 4
XA_DH = D // XA_HEADS
POOL_GROUPS = 4
PG = 128
POOL_W = POOL_GROUPS * PG
SSM_INNER = 1024
SSM_GROUPS = 2
SSM_GSZ = SSM_INNER // SSM_GROUPS
SSM_HEADS = 16
SSM_P = 64
SSM_N = 128
SSM_CONV = 4
SSM_CONV_DIM = SSM_INNER + 2 * SSM_GROUPS * SSM_N
SSM_XBC_G = SSM_GSZ + 2 * SSM_N
CHUNK = 128
AB_IN = POOL_W + SSM_INNER + SSM_CONV_DIM + SSM_HEADS
AB_IN_PAD = POOL_W + SSM_INNER + SSM_CONV_DIM + 128
AB_OUT = POOL_W + SSM_INNER
CONF_K = 31
SC_K = 3
CD_IN = 5 * D
CD_OUT = 2 * D
MLP_H = 4 * D
RMS_EPS = 1e-6
LN_EPS = 1e-5
ADAM_LR = 0.001
ADAM_B1 = 0.9
ADAM_B2 = 0.999
ADAM_EPS = 1e-08
ADAM_WD = 0.01
ADAM_STEP = 10
VMEM_LIMIT = 56 * 1024 * 1024
LANE = 128

NAMES = ['x', 'mem', 'norm_gains', 'xa_wq', 'xa_wkv', 'xa_wo', 'mlp_w1', 'mlp_w2', 'ab_w_in', 'pool_w', 'pool_scale',
         'ssm_conv_w', 'ssm_conv_b', 'ssm_dt_bias', 'ssm_a_log', 'ssm_d', 'ssm_norm', 'ab_w_out', 'cd_w_in', 'conf_dw_w',
         'conf_dw_b', 'conf_ln_g', 'conf_ln_b', 'sc_conv_w', 'cd_w_out', 'loss_target']
WEIGHTS = NAMES[2:25]
BIG = [('xa_wq', 1), ('xa_wkv', 2), ('xa_wo', 1), ('mlp_w1', 2), ('mlp_w2', 1), ('cd_w_in', 2), ('cd_w_out', 1),
       ('ab_w_out', 1), ('ab_w_in', 2)]
SMALL_SHARDED = ['norm_gains', 'ssm_conv_w', 'conf_dw_w', 'conf_dw_b', 'conf_ln_g', 'conf_ln_b', 'sc_conv_w']
REPLICATED = ['pool_w', 'pool_scale', 'ssm_conv_b', 'ssm_dt_bias', 'ssm_a_log', 'ssm_d', 'ssm_norm']


def _dg(a, b, ca, cb, prec=None):
    return lax.dot_general(a, b, (((ca,), (cb,)), ((), ())), precision=prec, preferred_element_type=F32)


@functools.partial(jax.custom_vjp, nondiff_argnums=(2, 3))
def bdot(a, b, ca, cb):
    return _dg(a.astype(BF), b.astype(BF), ca, cb)


def _bdot_fwd(a, b, ca, cb):
    return bdot(a, b, ca, cb), (a, b)


def _bdot_bwd(ca, cb, res, g):
    a, b = res
    g16, a16, b16 = g.astype(BF), a.astype(BF), b.astype(BF)
    da = _dg(g16, b16, 1, 1 - cb) if ca == 1 else _dg(b16, g16, 1 - cb, 1)
    db = _dg(g16, a16, 0, 1 - ca) if cb == 1 else _dg(a16, g16, 1 - ca, 0)
    return da.astype(a.dtype), db.astype(b.dtype)


bdot.defvjp(_bdot_fwd, _bdot_bwd)


def _split3(a):
    a1 = a.astype(BF)
    r1 = a - a1.astype(F32)
    a2 = r1.astype(BF)
    a3 = (r1 - a2.astype(F32)).astype(BF)
    return a1, a2, a3


def _exact_right(a, c):
    m = a.shape[0]
    if m % 16:
        return sum(_dg(p, c, 1, 0) for p in _split3(a))
    o = _dg(jnp.concatenate(_split3(a), axis=0), c, 1, 0)
    return o[:m] + o[m:2 * m] + o[2 * m:]


def _exact_left(c, a):
    n = a.shape[1]
    o = _dg(c, jnp.concatenate(_split3(a), axis=1), 1, 0)
    return o[:, :n] + o[:, n:2 * n] + o[:, 2 * n:]


@jax.custom_vjp
def cmat(a, c, ct):
    return _exact_right(a, c)


def _cmat_fwd(a, c, ct):
    return cmat(a, c, ct), (c, ct)


def _cmat_bwd(res, g):
    c, ct = res
    return _exact_right(g, ct), jnp.zeros_like(c), jnp.zeros_like(ct)


cmat.defvjp(_cmat_fwd, _cmat_bwd)


@jax.custom_vjp
def cmatl(c, ct, a):
    return _exact_left(c, a)


def _cmatl_fwd(c, ct, a):
    return cmatl(c, ct, a), (c, ct)


def _cmatl_bwd(res, g):
    c, ct = res
    return jnp.zeros_like(c), jnp.zeros_like(ct), _exact_left(ct, g)


cmatl.defvjp(_cmatl_fwd, _cmatl_bwd)


SUBLANES = 8


def _taps(x, shifts, down):
    n, c = x.shape
    pad = _round_up(max(shifts), SUBLANES)
    if pad == 0:
        return {0: x}
    zeros = jnp.zeros((pad, c), x.dtype)
    xp = jnp.concatenate([zeros, x] if down else [x, zeros], axis=0)
    rolled, out = {0: xp}, {}
    for s in shifts:
        a, b = divmod(s, SUBLANES)
        if b not in rolled:
            rolled[b] = pltpu.roll(xp, b if down else n + pad - b, 0)
        off = pad - SUBLANES * a if down else SUBLANES * a
        out[s] = rolled[b][off:off + n]
    return out


def _shift_down(x, k):
    return _taps(x, [k], True)[k]


def _shift_up(x, k):
    return _taps(x, [k], False)[k]


@functools.partial(jax.custom_vjp, nondiff_argnums=(1,))
def shift(x, k):
    return _shift_down(x, k)


def _shift_fwd(x, k):
    return _shift_down(x, k), None


def _shift_bwd(k, _, g):
    return (_shift_up(g, k),)


shift.defvjp(_shift_fwd, _shift_bwd)


@functools.partial(jax.custom_vjp, nondiff_argnums=(2,))
def cconv(u, w, width):
    taps = _taps(u, list(range(width)), True)
    acc = u * w[width - 1:width, :]
    for k in range(width - 1):
        acc = acc + taps[width - 1 - k] * w[k:k + 1, :]
    return acc


def _cconv_fwd(u, w, width):
    return cconv(u, w, width), (u, w)


def _cconv_bwd(width, res, g):
    u, w = res
    rows = lax.broadcasted_iota(jnp.int32, w.shape, 0)
    du = g * w[width - 1:width, :]
    dw = jnp.where(rows == width - 1, jnp.sum(g * u, axis=0, keepdims=True), 0.0)
    g_taps = _taps(g, list(range(width)), False)
    u_taps = _taps(u, list(range(width)), True)
    for k in range(width - 1):
        s = width - 1 - k
        du = du + g_taps[s] * w[k:k + 1, :]
        dw = dw + jnp.where(rows == k, jnp.sum(g * u_taps[s], axis=0, keepdims=True), 0.0)
    return du, dw


cconv.defvjp(_cconv_fwd, _cconv_bwd)


def _rms(x, g):
    return x * lax.rsqrt(jnp.mean(x * x, axis=-1, keepdims=True) + RMS_EPS) * g


def _params(sem=None):
    return pltpu.CompilerParams(dimension_semantics=sem, vmem_limit_bytes=VMEM_LIMIT)


def _f32(v):
    return v if v.dtype == F32 else v.astype(F32)


def _first(axes):
    ok = None
    for ax in axes:
        c = pl.program_id(ax) == 0
        ok = c if ok is None else jnp.logical_and(ok, c)
    return ok


def fwd_call(fn, name, grid, ins, in_specs, out_shapes, out_specs):
    n_in = len(ins)

    def body(*refs):
        outs = fn(*[_f32(r[...]) for r in refs[:n_in]])
        for r, o in zip(refs[n_in:], outs):
            r[...] = o.astype(r.dtype)

    return pl.pallas_call(body, name=name, grid=grid, in_specs=in_specs, out_specs=out_specs, out_shape=out_shapes,
                          compiler_params=_params())(*ins)


def bwd_call(fn, name, grid, ins, in_specs, cots, cot_specs, gidx, g_shapes, g_specs, g_acc):
    n_in, n_cot = len(ins), len(cots)

    def body(*refs):
        vals = [_f32(r[...]) for r in refs[:n_in]]

        def f_sel(*dv):
            full = list(vals)
            for i, v in zip(gidx, dv):
                full[i] = v
            return tuple(fn(*full))

        outs, vjp = jax.vjp(f_sel, *[vals[i] for i in gidx])
        cts = tuple(_f32(r[...]) for r in refs[n_in:n_in + n_cot])
        grads = vjp(cts)
        for r, g, acc in zip(refs[n_in + n_cot:], grads, g_acc):
            if acc is None:
                r[...] = g.astype(r.dtype)
            else:
                @pl.when(_first(acc))
                def _():
                    r[...] = jnp.zeros_like(r)

                r[...] += g.astype(r.dtype)

    return pl.pallas_call(body, name=name, grid=grid, in_specs=list(in_specs) + list(cot_specs), out_specs=g_specs,
                          out_shape=g_shapes, compiler_params=_params())(*ins, *cots)


def _tile(dim, pref):
    if dim <= pref:
        return dim
    best = None
    for t in range(LANE, pref + 1, LANE):
        if dim % t == 0:
            best = t
    assert best is not None, dim
    return best


MATMUL_VMEM_BUDGET = 40 * 1024 * 1024


def _matmul_tiles(m, n, k, a_bytes, b_bytes, out_bytes):
    tn = _tile(n, 1024)
    for tk_pref in (k, 2048, 1024, 512):
        tk = _tile(k, tk_pref)
        for tm_pref in (1024, 512, 256):
            tm = _tile(m, tm_pref)
            need = 2 * (tm * tk * a_bytes + tk * tn * b_bytes + tm * tn * out_bytes) + (0 if tk == k else tm * tn * 4)
            need += (tm * tk * 2 if a_bytes == 4 else 0) + (tk * tn * 2 if b_bytes == 4 else 0)
            if need <= MATMUL_VMEM_BUDGET:
                return tm, tn, tk
    raise ValueError((m, n, k))


def matmul(a, b, mode, name, out_dtype=F32, epilogue=None, extras=()):
    if mode == 'nn':
        (m, k), (k2, n) = a.shape, b.shape
    elif mode == 'nt':
        (m, k), (n, k2) = a.shape, b.shape
    else:
        (k, m), (k2, n) = a.shape, b.shape
    assert k == k2, (name, a.shape, b.shape)
    n_extra = len(extras)
    out_dtypes = out_dtype if isinstance(out_dtype, tuple) else (out_dtype,)
    per_out = sum(jnp.dtype(dt).itemsize for dt in out_dtypes) + sum(e.dtype.itemsize for e in extras)
    tm, tn, tk = _matmul_tiles(m, n, k, a.dtype.itemsize, b.dtype.itemsize, per_out)
    nk = k // tk
    ca = 0 if mode == 'tn' else 1
    cb = 1 if mode == 'nt' else 0
    a_spec = pl.BlockSpec((tk, tm), lambda i, j, kk: (kk, i)) if mode == 'tn' else pl.BlockSpec((tm, tk), lambda i, j, kk: (i, kk))
    b_spec = pl.BlockSpec((tn, tk), lambda i, j, kk: (j, kk)) if mode == 'nt' else pl.BlockSpec((tk, tn), lambda i, j, kk: (kk, j))

    def finish(o_refs, extra_refs, acc):
        outs = (acc,) if epilogue is None else epilogue(acc, *[_f32(e[...]) for e in extra_refs])
        for o_ref, o in zip(o_refs, outs):
            o_ref[...] = o.astype(o_ref.dtype)

    def body_whole_k(a_ref, b_ref, *refs):
        finish(refs[n_extra:], refs[:n_extra], _dg(a_ref[...].astype(BF), b_ref[...].astype(BF), ca, cb))

    def body_split_k(a_ref, b_ref, *refs):
        extra_refs, o_refs, acc = refs[:n_extra], refs[n_extra:-1], refs[-1]
        kk = pl.program_id(2)

        @pl.when(kk == 0)
        def _():
            acc[...] = jnp.zeros_like(acc)

        acc[...] += _dg(a_ref[...].astype(BF), b_ref[...].astype(BF), ca, cb)

        @pl.when(kk == nk - 1)
        def _():
            finish(o_refs, extra_refs, acc[...])

    tile = pl.BlockSpec((tm, tn), lambda i, j, kk: (i, j))
    outs = pl.pallas_call(
        body_whole_k if nk == 1 else body_split_k, name=name, grid=(m // tm, n // tn, nk),
        in_specs=[a_spec, b_spec] + [tile] * n_extra, out_specs=[tile] * len(out_dtypes),
        out_shape=[jax.ShapeDtypeStruct((m, n), dt) for dt in out_dtypes],
        scratch_shapes=[] if nk == 1 else [pltpu.VMEM((tm, tn), F32)],
        compiler_params=_params(("parallel", "parallel", "arbitrary")))(a, b, *extras)
    return outs if isinstance(out_dtype, tuple) else outs[0]


_FLIPS = [(0, 0, 1), (1, 0, 0), (0, 1, 0), (1, 1, 0), (1, 0, 1), (0, 1, 1), (1, 1, 1)]


def _me():
    return lax.axis_index("x"), lax.axis_index("y"), lax.axis_index("c")


def _flip(pos, f):
    return tuple(jnp.where(fi == 1, 1 - p, p) if fi else p for p, fi in zip(pos, f))


def _slot(pos):
    return 4 * pos[0] + 2 * pos[1] + pos[2]


def all_gather(v, name):
    def body(v_ref, out_ref, send_sems, recv_sems, local_sem):
        me = _me()
        sibling = _flip(me, (0, 0, 1))
        chips = [_flip(me, f) for f in ((1, 0, 0), (0, 1, 0), (1, 1, 0))]

        def copy(k, block, to, src=None):
            return pltpu.make_async_remote_copy(
                src_ref=out_ref.at[_slot(block)] if src is None else src, dst_ref=out_ref.at[_slot(block)],
                send_sem=send_sems.at[k], recv_sem=recv_sems.at[k], device_id=to, device_id_type=pl.DeviceIdType.MESH)

        mine = pltpu.make_async_copy(v_ref, out_ref.at[_slot(me)], local_sem)
        mine.start()
        first = [copy(0, me, sibling, src=v_ref)] + [copy(1 + j, me, chip, src=v_ref) for j, chip in enumerate(chips)]
        for cp in first:
            cp.start()
        passed = [copy(4 + j, chip, sibling) for j, chip in enumerate(chips)]
        for j, chip in enumerate(chips):
            copy(1 + j, chip, me).wait_recv()
            passed[j].start()
        copy(0, sibling, me).wait_recv()
        for j, chip in enumerate(chips):
            copy(4 + j, _flip(chip, (0, 0, 1)), me).wait_recv()
        for cp in first + passed:
            cp.wait_send()
        mine.wait()

    return pl.pallas_call(
        body, name=name, out_shape=jax.ShapeDtypeStruct((N_DEV,) + v.shape, v.dtype),
        in_specs=[pl.BlockSpec(memory_space=pl.ANY)], out_specs=pl.BlockSpec(memory_space=pl.ANY),
        scratch_shapes=[pltpu.SemaphoreType.DMA((7,)), pltpu.SemaphoreType.DMA((7,)), pltpu.SemaphoreType.DMA(())],
    )(v)


def sum_slots(v, name, tr=256):
    _, r, c = v.shape
    tr = _tile_rows(r, tr)

    def body(v_ref, o_ref):
        acc = v_ref[0].astype(F32)
        for s in range(1, N_DEV):
            acc = acc + v_ref[s].astype(F32)
        o_ref[...] = acc

    return pl.pallas_call(body, name=name, grid=(r // tr,), in_specs=[pl.BlockSpec((N_DEV, tr, c), lambda i: (0, i, 0))],
                          out_specs=pl.BlockSpec((tr, c), lambda i: (i, 0)), out_shape=jax.ShapeDtypeStruct((r, c), F32),
                          compiler_params=_params())(v)


def _tile_rows(r, pref):
    if r <= pref:
        return r
    best = None
    for t in range(8, pref + 1, 8):
        if r % t == 0:
            best = t
    return r if best is None else best


def _adamw_math(w, m, v, g):
    nm = ADAM_B1 * m + (1.0 - ADAM_B1) * g
    nv = ADAM_B2 * v + (1.0 - ADAM_B2) * jnp.square(g)
    m_hat = nm / (1.0 - ADAM_B1 ** ADAM_STEP)
    v_hat = nv / (1.0 - ADAM_B2 ** ADAM_STEP)
    return -ADAM_LR * (m_hat / (jnp.sqrt(v_hat) + ADAM_EPS) + ADAM_WD * w), nm, nv


def update_from_slots(lands, offs, w, m, v, transposed, name):
    layers, a, b = w.shape
    n_land = len(lands)
    if transposed:
        rb, tk = LANE, 512
        assert a % tk == 0 and b % rb == 0 and all(o % rb == 0 for o in offs), (name, w.shape, offs)
        grid = (layers, a // tk, b // rb)
        land_block = (N_DEV, rb, tk)
        tile = pl.BlockSpec((None, tk, rb), lambda l, i, j: (l, i, j))

        def land_spec(layer):
            base = offs[layer] // rb
            return pl.BlockSpec(land_block, lambda l, i, j: (0, base + jnp.where(l == layer, j, 0), jnp.where(l == layer, i, 0)))
    else:
        tr = max(t for t in (256, 128, 64) if a % t == 0 and all(o % t == 0 for o in offs))
        grid = (layers, a // tr)
        land_block = (N_DEV, tr, b)
        tile = pl.BlockSpec((None, tr, b), lambda l, i: (l, i, 0))

        def land_spec(layer):
            base = offs[layer] // tr
            return pl.BlockSpec(land_block, lambda l, i: (0, base + jnp.where(l == layer, i, 0), 0))

    def body(*refs):
        land_refs, (w_ref, m_ref, v_ref, g_ref, d_ref, nm_ref, nv_ref, acc) = refs[:n_land], refs[n_land:]
        for layer, land in enumerate(land_refs):
            @pl.when(pl.program_id(0) == layer)
            def _(land=land):
                s = land[0].astype(F32)
                for k in range(1, N_DEV):
                    s = s + land[k].astype(F32)
                acc[...] = s

        g = acc[...].T if transposed else acc[...]
        d, nm, nv = _adamw_math(w_ref[...], m_ref[...], v_ref[...], g)
        g_ref[...] = g
        d_ref[...] = d
        nm_ref[...] = nm
        nv_ref[...] = nv

    sh = jax.ShapeDtypeStruct(w.shape, F32)
    return pl.pallas_call(
        body, name=name, grid=grid, in_specs=[land_spec(layer) for layer in range(n_land)] + [tile] * 3, out_specs=[tile] * 4,
        out_shape=[sh] * 4, scratch_shapes=[pltpu.VMEM(land_block[1:], F32)], compiler_params=_params())(*lands, w, m, v)


def adamw_many(ws, ms, vs, gs, name):
    n = len(ws)

    def body(*refs):
        for i in range(n):
            d, nm, nv = _adamw_math(refs[i][...], refs[n + i][...], refs[2 * n + i][...], refs[3 * n + i][...])
            refs[4 * n + i][...] = d
            refs[5 * n + i][...] = nm
            refs[6 * n + i][...] = nv

    vmem = pl.BlockSpec(memory_space=pltpu.VMEM)
    shapes = [jax.ShapeDtypeStruct(a.shape, F32) for a in ws]
    res = pl.pallas_call(body, name=name, in_specs=[vmem] * (4 * n), out_specs=[vmem] * (3 * n), out_shape=shapes * 3,
                         compiler_params=_params())(*ws, *ms, *vs, *gs)
    return res[:n], res[n:2 * n], res[2 * n:]


def adamw(w, m, v, g, name):
    r, c = w.shape
    tr = _tile_rows(r, 512 if c <= 1024 else 128)

    def body(w_ref, m_ref, v_ref, g_ref, d_ref, nm_ref, nv_ref):
        d_ref[...], nm_ref[...], nv_ref[...] = _adamw_math(w_ref[...], m_ref[...], v_ref[...], g_ref[...])

    spec = pl.BlockSpec((tr, c), lambda i: (i, 0))
    sh = jax.ShapeDtypeStruct((r, c), F32)
    return pl.pallas_call(body, name=name, grid=(r // tr,), in_specs=[spec] * 4, out_specs=[spec] * 3,
                          out_shape=[sh] * 3, compiler_params=_params())(w, m, v, g)


def seg_in(x, g):
    return (_rms(x, g),)


def seg_in_res(x, g):
    return x, _rms(x, g)


def seg_res(x, m, ga, gb):
    x1 = x + _rms(m, ga)
    return x1, _rms(x1, gb)


def seg_out(x, m, ga):
    return (x + _rms(m, ga),)


def act_epilogue(r):
    t = jnp.maximum(r, 0.0)
    return r, t * t


def act_bwd_epilogue(drr, r):
    return (drr * (2.0 * jnp.maximum(r, 0.0)),)


def seg_ln(v, g, b):
    mu = jnp.mean(v, axis=-1, keepdims=True)
    var = jnp.mean(jnp.square(v - mu), axis=-1, keepdims=True)
    vn = (v - mu) * lax.rsqrt(var + LN_EPS) * g + b
    return (jax.nn.silu(vn),)


def make_pool_fn(group):
    window = 2 ** (group + 1)

    def pool_fn(ug, pw, scale):
        s = ug
        for lvl in range(group + 1):
            s = s + shift(s, 2 ** lvl)
        cnt = jnp.minimum(lax.broadcasted_iota(jnp.int32, ug.shape, 0) + 1, window).astype(F32)
        return (bdot(s / cnt - ug, pw, 1, 0) * scale,)

    return pool_fn


def conv4_fn(xr, w, b):
    return (jax.nn.silu(cconv(xr, w, SSM_CONV) + b),)


def cd1_fn(u, dww, dwb, scw):
    val, gate, bg, cg, hh = (u[:, k * LANE:(k + 1) * LANE] for k in range(5))
    v = val * jax.nn.sigmoid(gate)
    vc = cconv(v, dww, CONF_K) + dwb
    sc = bg * cconv(cg * hh, scw, SC_K)
    return vc, sc


def attn_fn(q, kv):
    outs = []
    for h in range(XA_HEADS):
        cols = slice(h * XA_DH, (h + 1) * XA_DH)
        s = bdot(q[:, cols], kv[:, cols], 1, 1) / math.sqrt(XA_DH)
        p = jax.nn.softmax(s, axis=-1)
        outs.append(bdot(p, kv[:, D + h * XA_DH:D + (h + 1) * XA_DH], 1, 0))
    return (jnp.concatenate(outs, axis=1),)


def ssd_chunk(xbc, z, dtraw, dtb, alog, dsk, nw, h0, h1, h2, h3, e64, e64t, ecat, ecatt, tril, trilt):
    xs, bm, cm = xbc[:, :SSM_GSZ], xbc[:, SSM_GSZ:SSM_GSZ + SSM_N], xbc[:, SSM_GSZ + SSM_N:]
    hin = (h0, h1, h2, h3)
    dt = jax.nn.softplus(dtraw + dtb)
    a = -jnp.exp(alog)
    d_a = dt * a
    cs = cmatl(tril, trilt, d_a)
    cs_cat = cmat(cs, ecat, ecatt)
    cs64, cs128 = cs_cat[:, :SSM_GSZ], cs_cat[:, SSM_GSZ:]
    dt64 = cmat(dt, e64, e64t)
    row = lax.broadcasted_iota(jnp.int32, (8, LANE), 0)
    heads = jnp.where(row == 0, dsk, jnp.where(row == 1, jnp.sum(d_a, axis=0, keepdims=True), 0.0))
    heads64 = cmat(heads, e64, e64t)
    d64, tot64 = heads64[0:1, :], heads64[1:2, :]
    xdt = xs * dt64
    cb = bdot(cm, bm, 1, 1)
    li = lax.broadcasted_iota(jnp.int32, (CHUNK, CHUNK), 0)
    si = lax.broadcasted_iota(jnp.int32, (CHUNK, CHUNK), 1)
    causal = li >= si
    lane = lax.broadcasted_iota(jnp.int32, (CHUNK, LANE), 1)
    xw = xdt * jnp.exp(tot64 - cs64)
    ecs = jnp.exp(cs64)
    etot = jnp.exp(tot64)
    ycols, hout = [], []
    for j in range(4):
        sl = slice(j * LANE, (j + 1) * LANE)
        xj = xdt[:, sl]
        ys = []
        for hh in range(2):
            r = 2 * j + hh
            col = cs128[:, r * LANE:(r + 1) * LANE]
            decay = jnp.exp(jnp.where(causal, col - col.T, -1e30))
            ys.append(bdot(cb * decay, xj, 1, 0))
        y_diag = jnp.where(lane < SSM_P, ys[0], ys[1])
        y_off = bdot(cm, hin[j], 1, 0) * ecs[:, sl]
        ycols.append(y_diag + y_off)
        hout.append(etot[:, sl] * hin[j] + bdot(bm, xw[:, sl], 0, 0))
    y = jnp.concatenate(ycols, axis=1) + d64 * xs
    y = y * jax.nn.silu(z)
    yn = y * lax.rsqrt(jnp.mean(y * y, axis=-1, keepdims=True) + RMS_EPS) * nw
    return (yn,) + tuple(hout)


def _xbc_group(a, axis):
    parts = []
    for g in range(SSM_GROUPS):
        for start, width in ((g * SSM_GSZ, SSM_GSZ), (SSM_INNER + g * SSM_N, SSM_N), (SSM_INNER + (SSM_GROUPS + g) * SSM_N, SSM_N)):
            parts.append(lax.slice_in_dim(a, start, start + width, axis=axis))
    return jnp.concatenate(parts, axis=axis)


def _xbc_ungroup(a, axis):
    xs, bs, cs = [], [], []
    for g in range(SSM_GROUPS):
        base = g * SSM_XBC_G
        xs.append(lax.slice_in_dim(a, base, base + SSM_GSZ, axis=axis))
        bs.append(lax.slice_in_dim(a, base + SSM_GSZ, base + SSM_GSZ + SSM_N, axis=axis))
        cs.append(lax.slice_in_dim(a, base + SSM_GSZ + SSM_N, base + SSM_XBC_G, axis=axis))
    return jnp.concatenate(xs + bs + cs, axis=axis)


def _ssd_consts():
    h = np.arange(LANE)[:, None]
    e64 = np.stack([(h == g * 8 + np.arange(SSM_GSZ)[None, :] // SSM_P) for g in range(SSM_GROUPS)]).astype(np.float32)
    e128 = np.stack([(h == g * 8 + np.arange(8 * LANE)[None, :] // LANE) for g in range(SSM_GROUPS)]).astype(np.float32)
    ecat = np.concatenate([e64, e128], axis=2)
    tril = np.tril(np.ones((CHUNK, CHUNK), np.float32))
    return tuple(jnp.asarray(c, dtype=BF) for c in (e64, e64.transpose(0, 2, 1), ecat, ecat.transpose(0, 2, 1), tril, tril.T))


def _ssd_specs(nc, rev):
    def ci(c):
        return nc - 1 - c if rev else c

    def row(width, col):
        return pl.BlockSpec((CHUNK, width), lambda b, c: (b * nc + ci(c), col))

    def whole(shape):
        return pl.BlockSpec(shape, lambda b, c: (0,) * len(shape))

    data = [row(SSM_CONV_DIM, 0),
            row(SSM_GSZ, 1), row(SSM_GSZ, 2), row(LANE, 24)]
    par = [whole((1, LANE))] * 3 + [whole((1, SSM_INNER))]
    cst = [whole((SSM_GROUPS, LANE, SSM_GSZ)), whole((SSM_GROUPS, SSM_GSZ, LANE)), whole((SSM_GROUPS, LANE, 12 * LANE)),
           whole((SSM_GROUPS, 12 * LANE, LANE)), whole((CHUNK, CHUNK)), whole((CHUNK, CHUNK))]
    hsave = pl.BlockSpec((None, None, SSM_GROUPS, 4, SSM_N, LANE), lambda b, c: (b, ci(c), 0, 0, 0, 0))
    return data, par, cst, hsave, row, whole


def _ssd_group_args(g, xbc, z, dtr, dtb, alog, dsk, nw):
    return (xbc[:, g * SSM_XBC_G:(g + 1) * SSM_XBC_G], z[g], dtr, dtb, alog, dsk, nw[:, g * SSM_GSZ:(g + 1) * SSM_GSZ])


def ssd_fwd(xbc_act, u, dtb, alog, dsk, nw, consts, bsz, seq):
    nc = seq // CHUNK
    data, par, cst, hsave, row, _ = _ssd_specs(nc, False)

    def body(xbc, z0, z1, dtr, dtb_r, alog_r, dsk_r, nw_r, e64, e64t, ecat, ecatt, tril, trilt, yn_ref, hs_ref, h):
        @pl.when(pl.program_id(1) == 0)
        def _():
            h[...] = jnp.zeros_like(h)

        hs_ref[...] = h[...]
        ys = []
        for g in range(SSM_GROUPS):
            args = _ssd_group_args(g, xbc[...], (z0[...], z1[...]), dtr[...], dtb_r[...], alog_r[...], dsk_r[...], nw_r[...])
            outs = ssd_chunk(*args, h[g, 0], h[g, 1], h[g, 2], h[g, 3], e64[g], e64t[g], ecat[g], ecatt[g], tril[...], trilt[...])
            ys.append(outs[0])
            for j in range(4):
                h[g, j] = outs[1 + j]
        yn_ref[...] = jnp.concatenate(ys, axis=1).astype(yn_ref.dtype)

    t = bsz * seq
    return pl.pallas_call(
        body, name="ssd_fwd", grid=(bsz, nc), in_specs=data + par + cst, out_specs=[row(SSM_INNER, 0), hsave],
        out_shape=[jax.ShapeDtypeStruct((t, SSM_INNER), BF), jax.ShapeDtypeStruct((bsz, nc, SSM_GROUPS, 4, SSM_N, LANE), F32)],
        scratch_shapes=[pltpu.VMEM((SSM_GROUPS, 4, SSM_N, LANE), F32)], compiler_params=_params(),
    )(xbc_act, u, u, u, dtb, alog, dsk, nw, *consts)


def ssd_bwd(xbc_act, u, dtb, alog, dsk, nw, consts, hs, dmix, bsz, seq):
    nc = seq // CHUNK
    data, par, cst, hsave, row, whole = _ssd_specs(nc, True)
    t = bsz * seq
    pcol = POOL_W // SSM_GSZ

    def body(xbc, z0, z1, dtr, dtb_r, alog_r, dsk_r, nw_r, e64, e64t, ecat, ecatt, tril, trilt, hs_ref, dy0, dy1,
             dxbc, dz, ddt, ddtb, dalog, ddsk, dnw, dh):
        @pl.when(pl.program_id(1) == 0)
        def _():
            dh[...] = jnp.zeros_like(dh)

        per_group = []
        for g, dyn in enumerate((dy0, dy1)):
            cst_vals = (e64[g], e64t[g], ecat[g], ecatt[g], tril[...], trilt[...])
            prim = _ssd_group_args(g, xbc[...], (z0[...], z1[...]), dtr[...], dtb_r[...], alog_r[...], dsk_r[...], nw_r[...])
            prim = prim + (hs_ref[g, 0], hs_ref[g, 1], hs_ref[g, 2], hs_ref[g, 3])
            _, vjp = jax.vjp(lambda *args, c=cst_vals: ssd_chunk(*args, *c), *prim)
            gr = vjp((dyn[...].astype(F32), dh[g, 0], dh[g, 1], dh[g, 2], dh[g, 3]))
            for j in range(4):
                dh[g, j] = gr[7 + j]
            per_group.append(gr)
        g0, g1 = per_group
        dxbc[...] = jnp.concatenate([g0[0], g1[0]], axis=1)
        dz[...] = jnp.concatenate([g0[1], g1[1]], axis=1).astype(dz.dtype)
        ddt[...] = g0[2] + g1[2]

        @pl.when(_first((0, 1)))
        def _():
            for r in (ddtb, dalog, ddsk, dnw):
                r[...] = jnp.zeros_like(r)

        ddtb[...] += g0[3] + g1[3]
        dalog[...] += g0[4] + g1[4]
        ddsk[...] += g0[5] + g1[5]
        dnw[...] += jnp.concatenate([g0[6], g1[6]], axis=1)

    out_specs = [row(SSM_CONV_DIM, 0), row(SSM_INNER, 0), row(LANE, 0), whole((1, LANE)), whole((1, LANE)), whole((1, LANE)),
                 whole((1, SSM_INNER))]
    lane = jax.ShapeDtypeStruct((1, LANE), F32)
    out_shape = [jax.ShapeDtypeStruct((t, SSM_CONV_DIM), F32), jax.ShapeDtypeStruct((t, SSM_INNER), BF),
                 jax.ShapeDtypeStruct((t, LANE), F32), lane, lane, lane, jax.ShapeDtypeStruct((1, SSM_INNER), F32)]
    return pl.pallas_call(
        body, name="ssd_bwd", grid=(bsz, nc), in_specs=data + par + cst + [hsave, row(SSM_GSZ, pcol), row(SSM_GSZ, pcol + 1)],
        out_specs=out_specs, out_shape=out_shape, scratch_shapes=[pltpu.VMEM((SSM_GROUPS, 4, SSM_N, LANE), F32)],
        compiler_params=_params(),
    )(xbc_act, u, u, u, dtb, alog, dsk, nw, *consts, hs, dmix, dmix)


TB = 512


def _rows(d, col=0):
    return pl.BlockSpec((TB, d), lambda i: (i, col))


def _par(d):
    return pl.BlockSpec((1, d), lambda i: (0, 0))


def _sd(shape, dtype=F32):
    return jax.ShapeDtypeStruct(shape, dtype)


def _round_up(n, m):
    return -(-n // m) * m


def _pad_rows(a, rows):
    return jnp.pad(a, ((0, rows - a.shape[0]), (0, 0)))


def _pack128(arrs):
    flat = jnp.concatenate([a.reshape(-1) for a in arrs])
    n = flat.shape[0]
    rows = -(-n // (8 * LANE)) * 8
    return jnp.pad(flat, (0, rows * LANE - n)).reshape(rows, LANE)


def _unpack128(packed, shapes):
    flat = packed.reshape(-1)
    out, off = [], 0
    for s in shapes:
        n = int(np.prod(s))
        out.append(flat[off:off + n].reshape(s))
        off += n
    return out


def kernel(x, mem, norm_gains, xa_wq, xa_wkv, xa_wo, mlp_w1, mlp_w2, ab_w_in, pool_w, pool_scale, ssm_conv_w, ssm_conv_b, ssm_dt_bias, ssm_a_log, ssm_d, ssm_norm, ab_w_out, cd_w_in, conf_dw_w, conf_dw_b, conf_ln_g, conf_ln_b, sc_conv_w, cd_w_out, loss_target, m_norm_gains, m_xa_wq, m_xa_wkv, m_xa_wo, m_mlp_w1, m_mlp_w2, m_ab_w_in, m_pool_w, m_pool_scale, m_ssm_conv_w, m_ssm_conv_b, m_ssm_dt_bias, m_ssm_a_log, m_ssm_d, m_ssm_norm, m_ab_w_out, m_cd_w_in, m_conf_dw_w, m_conf_dw_b, m_conf_ln_g, m_conf_ln_b, m_sc_conv_w, m_cd_w_out, v_norm_gains, v_xa_wq, v_xa_wkv, v_xa_wo, v_mlp_w1, v_mlp_w2, v_ab_w_in, v_pool_w, v_pool_scale, v_ssm_conv_w, v_ssm_conv_b, v_ssm_dt_bias, v_ssm_a_log, v_ssm_d, v_ssm_norm, v_ab_w_out, v_cd_w_in, v_conf_dw_w, v_conf_dw_b, v_conf_ln_g, v_conf_ln_b, v_sc_conv_w, v_cd_w_out):
    args = locals()
    w = {n: args[n] for n in WEIGHTS}
    mom_m = {n: args["m_" + n] for n in WEIGHTS}
    mom_v = {n: args["v_" + n] for n in WEIGHTS}
    ex = Exchange(w)
    loss_local, grad_x, small_grads = local_step(x, mem, loss_target, ex)
    loss = lax.psum(loss_local, ("x", "y", "c"))
    outs = {}

    started = ex.put_small(small_grads)
    landed = {key: ex.landed(key, started) for key in ('l1', 'cd', 'l0')}
    late = []
    for n, keys in (('mlp_w1', ('l0', 'l1')), ('mlp_w2', ('l0', 'l1')), ('xa_wkv', ('l0', 'l1')), ('xa_wq', ('l0', 'l1')),
                    ('xa_wo', ('l0', 'l1')), ('cd_w_in', ('cd',)), ('cd_w_out', ('cd',))):
        lands = [landed[key][0] for key in keys]
        offs = [landed[key][1][(n, layer)] for layer, key in enumerate(keys)]
        outs[n] = update_from_slots(lands, offs, w[n], mom_m[n], mom_v[n], SHARD_AXIS[n] == 2, "update_" + n)
        late.append(outs[n][1])
    g_own = ex.reduced_small(late)
    land_ab, offs_ab = ex.landed('ab', late)
    outs['ab_w_out'] = update_from_slots([land_ab], [offs_ab[('ab_w_out', 0)]], w['ab_w_out'], mom_m['ab_w_out'],
                                         mom_v['ab_w_out'], False, "update_ab_w_out")
    g_ab_in = sum_slots(land_ab, "sum_ab", FLAT_ROW_TILE)[:w['ab_w_in'].shape[2]].T
    d, nm, nv = adamw(w['ab_w_in'][0], mom_m['ab_w_in'][0], mom_v['ab_w_in'][0], g_ab_in, "adamw_ab_w_in")
    outs['ab_w_in'] = (g_ab_in[None], d[None], nm[None], nv[None])
    small = SMALL_SHARDED + REPLICATED
    upd = adamw_many([w[n] for n in small], [mom_m[n] for n in small], [mom_v[n] for n in small], [g_own[n] for n in small],
                     "adamw_small")
    for i, n in enumerate(small):
        outs[n] = (g_own[n], upd[0][i], upd[1][i], upd[2][i])
    return (loss, grad_x.reshape(x.shape), *[outs[n][0] for n in WEIGHTS], *[outs[n][1] for n in WEIGHTS],
            *[outs[n][2] for n in WEIGHTS], *[outs[n][3] for n in WEIGHTS])


G_AB = (('ab_w_in', 0), ('ab_w_out', 0))
G_L0 = (('xa_wq', 0), ('xa_wkv', 0), ('xa_wo', 0), ('mlp_w1', 0), ('mlp_w2', 0))
G_L1 = (('xa_wq', 1), ('xa_wkv', 1), ('xa_wo', 1), ('mlp_w1', 1), ('mlp_w2', 1))
G_CD = (('cd_w_in', 0), ('cd_w_out', 0))
GATHER_CHAIN = {'l0': ('cd', G_CD), 'cd': ('l1', G_L1)}
SHARD_AXIS = dict(BIG)
MEMBER_ROW_TILE = 64
FLAT_ROW_TILE = 128


def _members(group, w):
    out = []
    for n, layer in group:
        shp = w[n].shape[1:]
        if SHARD_AXIS[n] == 2:
            shp = (shp[1], shp[0])
        assert shp[1] == D, (n, shp)
        out.append((n, layer, shp, shp[0], _round_up(shp[0], MEMBER_ROW_TILE)))
    return out


def _group_rows(group, w):
    return _round_up(sum(m[4] for m in _members(group, w)), FLAT_ROW_TILE)


def _flat_shards(group, w):
    parts = []
    for n, layer, _, _, padded in _members(group, w):
        shard = w[n][layer].astype(BF)
        parts.append(_pad_rows(shard.T if SHARD_AXIS[n] == 2 else shard, padded))
    return _pad_rows(jnp.concatenate(parts, axis=0), _group_rows(group, w))


def _full_from_slots(land, group, w):
    out, off = {}, 0
    for n, layer, shp, rows, padded in _members(group, w):
        out[(n, layer)] = land[:, off:off + rows].reshape(N_DEV * rows, D)
        off += padded
    return out


def _slots_from_full(grads, group, w):
    parts = []
    for n, layer, shp, rows, padded in _members(group, w):
        blk = grads[(n, layer)].astype(BF).reshape(N_DEV, rows, D)
        parts.append(jnp.pad(blk, ((0, 0), (0, padded - rows), (0, 0))))
    send = jnp.concatenate(parts, axis=1)
    return jnp.pad(send, ((0, 0), (0, _group_rows(group, w) - send.shape[1]), (0, 0)))


_HBM = pl.BlockSpec(memory_space=pltpu.HBM)
_SEM = pl.BlockSpec(memory_space=pltpu.SEMAPHORE)
_ANY = pl.BlockSpec(memory_space=pl.ANY)


def _peer_copy(k, src, dst, send_sems, recv_sems, peer):
    return pltpu.make_async_remote_copy(src_ref=src, dst_ref=dst, send_sem=send_sems.at[k], recv_sem=recv_sems.at[k],
                                        device_id=peer, device_id_type=pl.DeviceIdType.MESH)


def exchange_start(src, name, scatter):
    shape = src.shape[-2:]

    def body(src_ref, land_ref, send_sems, recv_sems, src_thru, land_thru, token):
        me = _me()
        for k, f in enumerate(_FLIPS):
            peer = _flip(me, f)
            piece = src_ref.at[_slot(peer)] if scatter else src_ref
            _peer_copy(k, piece, land_ref.at[_slot(me)], send_sems, recv_sems, peer).start()
        token[...] = jnp.zeros_like(token)

    land = pltpu.with_memory_space_constraint(lax.empty((N_DEV,) + shape, src.dtype), pltpu.HBM)
    return pl.pallas_call(
        body, name=name,
        out_shape=(pltpu.SemaphoreType.DMA((7,)), pltpu.SemaphoreType.DMA((7,)), pltpu.HBM(src.shape, src.dtype),
                   pltpu.HBM((N_DEV,) + shape, src.dtype), jax.ShapeDtypeStruct((8, LANE), F32)),
        in_specs=(_HBM, _HBM), out_specs=(_SEM, _SEM, _HBM, _HBM, pl.BlockSpec(memory_space=pltpu.VMEM)),
        input_output_aliases={0: 2, 1: 3},
        compiler_params=pltpu.CompilerParams(has_side_effects=pltpu.SideEffectType.DATAFLOW_SIDE_EFFECTING),
    )(pltpu.with_memory_space_constraint(src, pltpu.HBM), land)


def exchange_wait(handles, after, name, scatter):
    send_sems, recv_sems, src_thru, land_thru, _ = handles
    after = list(after) if isinstance(after, (list, tuple)) else [after]

    def body(src_ref, land_ref, send_sems, recv_sems, *rest):
        token = rest[-1]
        me = _me()
        for k, f in enumerate(_FLIPS):
            peer = _flip(me, f)
            piece = src_ref.at[_slot(peer)] if scatter else src_ref
            cp = _peer_copy(k, piece, land_ref.at[_slot(peer)], send_sems, recv_sems, peer)
            cp.wait_send()
            cp.wait_recv()
        token[...] = jnp.zeros_like(token)

    return pl.pallas_call(
        body, name=name, out_shape=(pltpu.HBM(src_thru.shape, src_thru.dtype), pltpu.HBM(land_thru.shape, land_thru.dtype),
                                    jax.ShapeDtypeStruct((8, LANE), F32)),
        in_specs=(_HBM, _HBM, _SEM, _SEM) + (_ANY,) * len(after), out_specs=(_HBM, _HBM, pl.BlockSpec(memory_space=pltpu.VMEM)),
        input_output_aliases={0: 0, 1: 1},
        compiler_params=pltpu.CompilerParams(has_side_effects=pltpu.SideEffectType.DATAFLOW_SIDE_EFFECTING),
    )(src_thru, land_thru, send_sems, recv_sems, *after)


class Exchange:
    def __init__(self, w):
        self.w = w
        self.me = _slot(_me())
        shapes = [w[n].shape for n in SMALL_SHARDED]
        gs = all_gather(_pack128([w[n] for n in SMALL_SHARDED]), "gather_small")
        per_dev = [_unpack128(gs[d], shapes) for d in range(N_DEV)]
        self.small = {n: jnp.concatenate([per_dev[d][i] for d in range(N_DEV)], axis=-1) for i, n in enumerate(SMALL_SHARDED)}
        self.small.update({n: w[n] for n in REPLICATED})
        self.now = _full_from_slots(all_gather(_flat_shards(G_AB, w), "gather_ab"), G_AB, w)
        self.gathers = {'l0': (G_L0, exchange_start(_flat_shards(G_L0, w), "gather_l0_start", False))}
        self.tokens = [self.gathers['l0'][1][4]]
        self.reductions = {}

    def take_tokens(self):
        toks, self.tokens = self.tokens, []
        return toks

    def weights(self, key, after):
        if key == 'ab':
            return self.now
        group, handles = self.gathers[key]
        _, land, done = exchange_wait(handles, after, f"gather_{key}_wait", False)
        nxt = GATHER_CHAIN.get(key)
        if nxt is not None:
            src = _flat_shards(nxt[1], self.w) + done[0, 0].astype(BF)
            self.gathers[nxt[0]] = (nxt[1], exchange_start(src, f"gather_{nxt[0]}_start", False))
            self.tokens.append(self.gathers[nxt[0]][1][4])
        land = lax.dynamic_update_slice(land, handles[2][None], (self.me, 0, 0))
        return _full_from_slots(land, group, self.w)

    def put_grads(self, key, group, grads):
        send = _slots_from_full(grads, group, self.w)
        handles = exchange_start(send, f"reduce_{key}_start", True)
        self.reductions[key] = (group, handles)
        self.tokens.append(handles[4])

    def landed(self, key, after):
        group, handles = self.reductions[key]
        send, land, _ = exchange_wait(handles, after, f"reduce_{key}_wait", True)
        mine = lax.dynamic_slice_in_dim(send, self.me, 1, axis=0)
        land = lax.dynamic_update_slice(land, mine, (self.me, 0, 0))
        offs, off = {}, 0
        for n, layer, _, _, padded in _members(group, self.w):
            offs[(n, layer)] = off
            off += padded
        return land, offs

    def put_small(self, small_grads):
        small = SMALL_SHARDED + REPLICATED
        self.small_shapes = [small_grads[n].shape for n in small]
        self.small_handles = exchange_start(_pack128([small_grads[n] for n in small]), "gather_small_grads_start", False)
        return self.small_handles[4]

    def reduced_small(self, after):
        small = SMALL_SHARDED + REPLICATED
        src, land, _ = exchange_wait(self.small_handles, after, "gather_small_grads_wait", False)
        gs = lax.dynamic_update_slice(land, src[None], (self.me, 0, 0))
        tot = _unpack128(sum_slots(gs, "sum_small", 1024), self.small_shapes)
        out = {}
        for n, g in zip(small, tot):
            if n in SMALL_SHARDED:
                width = self.w[n].shape[-1]
                g = lax.dynamic_slice_in_dim(g, self.me * width, width, axis=g.ndim - 1)
            out[n] = g
        return out


def local_step(x, mem, target, ex):
    bsz, seq, _ = x.shape
    t = bsz * seq
    nb = t // TB
    nc = seq // CHUNK
    x0 = x.reshape(t, D)
    mem2 = mem.reshape(bsz * N_MEM, D)
    tgt = target.reshape(t, D)
    p = ex.small
    gains = p['norm_gains']
    big = dict(ex.weights('ab', None))

    def gain(layer, i):
        g = gains[layer, i].reshape(1, D)
        for tok in ex.take_tokens():
            g = g + tok[0, 0]
        return g

    consts = _ssd_consts()
    grads = {}
    saved = [dict(), dict()]

    def run_seg_res(xin, m, ga, gb, name):
        return fwd_call(seg_res, name, (nb,), [xin, m, ga, gb], [_rows(D), _rows(D), _par(D), _par(D)],
                        [_sd((t, D)), _sd((t, D), BF)], [_rows(D), _rows(D)])

    def attn_specs():
        nq = seq // TB
        q = pl.BlockSpec((TB, D), lambda b, i: (b * nq + i, 0))
        kv = pl.BlockSpec((N_MEM, 2 * D), lambda b, i: (b, 0))
        return (bsz, nq), q, kv

    def attention_fwd(layer, xin, hin, sv):
        q = matmul(hin, big[('xa_wq', layer)], 'nn', f"q_{layer}", BF)
        kv = matmul(mem2, big[('xa_wkv', layer)], 'nt', f"kv_{layer}", BF)
        grid, qs, kvs = attn_specs()
        o, = fwd_call(attn_fn, f"attn_{layer}", grid, [q, kv], [qs, kvs], [_sd((t, D), BF)], [qs])
        ao = matmul(o, big[('xa_wo', layer)], 'nn', f"ao_{layer}")
        sv.update(q=q, kv=kv, o=o, ao=ao)
        return ao

    def mlp_fwd(layer, hin, sv):
        r, rr = matmul(hin, big[('mlp_w1', layer)], 'nt', f"mlp1_{layer}", (BF, BF), epilogue=act_epilogue)
        mo = matmul(rr, big[('mlp_w2', layer)], 'nn', f"mlp2_{layer}")
        sv.update(r=r, rr=rr, mo=mo)
        return mo

    sv = saved[0]
    h0, = fwd_call(seg_in, "norm_in", (nb,), [x0, gain(0, 0)], [_rows(D), _par(D)], [_sd((t, D), BF)], [_rows(D)])
    xbc0 = POOL_W + SSM_INNER
    w_ab_in = big[('ab_w_in', 0)]
    w_ab_in = _pad_rows(jnp.concatenate([w_ab_in[:xbc0], _xbc_group(w_ab_in[xbc0:xbc0 + SSM_CONV_DIM], 0),
                                         w_ab_in[xbc0 + SSM_CONV_DIM:]], axis=0), AB_IN_PAD)
    conv_w, conv_b = _xbc_group(p['ssm_conv_w'][0], 1), _xbc_group(p['ssm_conv_b'], 1)
    u0 = matmul(h0, w_ab_in, 'nt', "ab_in")
    pool_outs = []
    for g in range(POOL_GROUPS):
        seqspec = pl.BlockSpec((seq, PG), lambda b, g=g: (b, g))
        po, = fwd_call(make_pool_fn(g), f"pool_{g}", (bsz,), [u0, p['pool_w'][0, g], p['pool_scale']],
                       [seqspec, pl.BlockSpec((PG, PG), lambda b: (0, 0)), pl.BlockSpec((1, PG), lambda b, g=g: (0, g))],
                       [_sd((t, PG), BF)], [pl.BlockSpec((seq, PG), lambda b: (b, 0))])
        pool_outs.append(po)
    cw = 256
    ncb = SSM_CONV_DIM // cw
    cbase = (POOL_W + SSM_INNER) // cw
    conv_in_specs = [pl.BlockSpec((seq, cw), lambda j, b: (b, cbase + j)), pl.BlockSpec((SSM_CONV, cw), lambda j, b: (0, j)),
                     pl.BlockSpec((1, cw), lambda j, b: (0, j))]
    conv_out_spec = pl.BlockSpec((seq, cw), lambda j, b: (b, j))
    xbc_act, = fwd_call(conv4_fn, "ssm_conv", (ncb, bsz), [u0, conv_w, conv_b], conv_in_specs,
                        [_sd((t, SSM_CONV_DIM))], [conv_out_spec])
    dtb = jnp.pad(p['ssm_dt_bias'], ((0, 0), (0, LANE - SSM_HEADS)))
    alog = jnp.pad(p['ssm_a_log'], ((0, 0), (0, LANE - SSM_HEADS)))
    dsk = jnp.pad(p['ssm_d'], ((0, 0), (0, LANE - SSM_HEADS)))
    yn, hs = ssd_fwd(xbc_act, u0, dtb, alog, dsk, p['ssm_norm'], consts, bsz, seq)
    mix0 = jnp.concatenate(pool_outs + [yn], axis=1)
    m0 = matmul(mix0, big[('ab_w_out', 0)], 'nn', "ab_out")
    x1, h2 = run_seg_res(x0, m0, gain(0, 1), gain(0, 2), "res_0a")
    big.update(ex.weights('l0', h2))
    ao0 = attention_fwd(0, x1, h2, sv)
    x2, h3 = run_seg_res(x1, ao0, gain(0, 3), gain(0, 4), "res_0b")
    mo0 = mlp_fwd(0, h3, sv)
    big.update(ex.weights('cd', sv['r']))
    x3, h4 = run_seg_res(x2, mo0, gain(0, 5), gain(1, 0), "res_0c")

    sv1 = saved[1]
    nd = D // LANE
    w_cd_in = big[('cd_w_in', 0)].reshape(5, nd, LANE, D).transpose(1, 0, 2, 3).reshape(CD_IN, D)
    u1 = matmul(h4, w_cd_in, 'nt', "cd_in")
    cd_par = [pl.BlockSpec((CONF_K, LANE), lambda j, b: (0, j)), pl.BlockSpec((1, LANE), lambda j, b: (0, j)),
              pl.BlockSpec((SC_K, LANE), lambda j, b: (0, j))]
    cd_ins = [u1, p['conf_dw_w'][0], p['conf_dw_b'], p['sc_conv_w'][0]]
    cd_u_spec = pl.BlockSpec((seq, 5 * LANE), lambda j, b: (b, j))
    cd_in_specs = [cd_u_spec] + cd_par
    cd_out_spec = pl.BlockSpec((seq, LANE), lambda j, b: (b, j))
    vconv, sc_out = fwd_call(cd1_fn, "cd_conv", (nd, bsz), cd_ins, cd_in_specs, [_sd((t, D)), _sd((t, D), BF)],
                             [cd_out_spec, cd_out_spec])
    conf, = fwd_call(seg_ln, "conf_ln", (nb,), [vconv, p['conf_ln_g'], p['conf_ln_b']], [_rows(D), _par(D), _par(D)],
                     [_sd((t, D), BF)], [_rows(D)])
    mix1 = jnp.concatenate([conf, sc_out], axis=1)
    m1 = matmul(mix1, big[('cd_w_out', 0)], 'nn', "cd_out")
    x4, h5 = run_seg_res(x3, m1, gain(1, 1), gain(1, 2), "res_1a")
    big.update(ex.weights('l1', h5))
    ao1 = attention_fwd(1, x4, h5, sv1)
    x5, h6 = run_seg_res(x4, ao1, gain(1, 3), gain(1, 4), "res_1b")
    mo1 = mlp_fwd(1, h6, sv1)

    def loss_body(x_ref, m_ref, g_ref, t_ref, dx_ref, dm_ref, dg_ref, acc_ref):
        (y,), vjp = jax.vjp(seg_out, x_ref[...], m_ref[...], g_ref[...])
        d = y - t_ref[...]
        dx, dm, dg = vjp((d / float(D),))
        dx_ref[...] = dx
        dm_ref[...] = dm.astype(dm_ref.dtype)

        @pl.when(pl.program_id(0) == 0)
        def _():
            acc_ref[...] = jnp.zeros_like(acc_ref)
            dg_ref[...] = jnp.zeros_like(dg_ref)

        acc_ref[...] += jnp.sum(d * d, axis=0, keepdims=True)
        dg_ref[...] += dg

    dx5, dmo1, dg15, lanes = pl.pallas_call(
        loss_body, name="loss_head", grid=(nb,), in_specs=[_rows(D), _rows(D), _par(D), _rows(D)],
        out_specs=[_rows(D), _rows(D), _par(D), _par(D)], out_shape=[_sd((t, D)), _sd((t, D), BF), _sd((1, D)), _sd((1, D))],
        compiler_params=_params())(x5, mo1, gain(1, 5), tgt)
    loss = 0.5 * jnp.sum(lanes) / float(D)

    gain_grads = {(1, 5): dg15}

    def bwd_seg_res(xin, m, ga, gb, dx1, dh, name):
        return bwd_call(seg_res, name, (nb,), [xin, m, ga, gb], [_rows(D), _rows(D), _par(D), _par(D)], [dx1, dh],
                        [_rows(D), _rows(D)], [0, 1, 2, 3], [_sd((t, D)), _sd((t, D), BF), _sd((1, D)), _sd((1, D))],
                        [_rows(D), _rows(D), _par(D), _par(D)], [None, None, (0,), (0,)])

    def mlp_bwd(layer, hin, dmo, sv):
        grads_w2 = matmul(sv['rr'], dmo, 'tn', f"d_mlp_w2_{layer}", BF)
        dr, = matmul(dmo, big[('mlp_w2', layer)], 'nt', f"d_r_{layer}", (BF,), epilogue=act_bwd_epilogue, extras=[sv['r']])
        grads_w1 = matmul(dr, hin, 'tn', f"d_mlp_w1_{layer}", BF)
        dh = matmul(dr, big[('mlp_w1', layer)], 'nn', f"d_h_mlp_{layer}")
        return dh, grads_w1, grads_w2

    def attention_bwd(layer, hin, dao, sv):
        g_wo = matmul(sv['o'], dao, 'tn', f"d_xa_wo_{layer}", BF)
        do = matmul(dao, big[('xa_wo', layer)], 'nt', f"d_o_{layer}", BF)
        grid, qs, kvs = attn_specs()
        dq, dkv = bwd_call(attn_fn, f"d_attn_{layer}", grid, [sv['q'], sv['kv']], [qs, kvs], [do], [qs], [0, 1],
                           [_sd((t, D), BF), _sd((bsz * N_MEM, 2 * D))], [qs, kvs], [None, (1,)])
        g_wkv = matmul(dkv, mem2, 'tn', f"d_xa_wkv_{layer}", BF)
        g_wq = matmul(hin, dq, 'tn', f"d_xa_wq_{layer}", BF)
        dh = matmul(dq, big[('xa_wq', layer)], 'nt', f"d_h_attn_{layer}")
        return dh, g_wq, g_wkv, g_wo

    per_layer = {k: [None, None] for k in ('xa_wq', 'xa_wkv', 'xa_wo', 'mlp_w1', 'mlp_w2')}

    dh6, per_layer['mlp_w1'][1], per_layer['mlp_w2'][1] = mlp_bwd(1, h6, dmo1, sv1)
    dx4, dao1, gain_grads[(1, 3)], gain_grads[(1, 4)] = bwd_seg_res(x4, ao1, gain(1, 3), gain(1, 4), dx5, dh6, "d_res_1b")
    dh5, per_layer['xa_wq'][1], per_layer['xa_wkv'][1], per_layer['xa_wo'][1] = attention_bwd(1, h5, dao1, sv1)
    ex.put_grads('l1', G_L1, {(k, 1): v[1] for k, v in per_layer.items()})
    dx3, dm1, gain_grads[(1, 1)], gain_grads[(1, 2)] = bwd_seg_res(x3, m1, gain(1, 1), gain(1, 2), dx4, dh5, "d_res_1a")
    g_cd_out = matmul(mix1, dm1, 'tn', "d_cd_w_out", BF)
    dmix1 = matmul(dm1, big[('cd_w_out', 0)], 'nt', "d_mix1")
    dvconv, dlg, dlb = bwd_call(seg_ln, "d_conf_ln", (nb,), [vconv, p['conf_ln_g'], p['conf_ln_b']],
                                [_rows(D), _par(D), _par(D)], [dmix1], [_rows(D, 0)], [0, 1, 2],
                                [_sd((t, D)), _sd((1, D)), _sd((1, D))], [_rows(D), _par(D), _par(D)], [None, (0,), (0,)])
    grads['conf_ln_g'], grads['conf_ln_b'] = dlg, dlb
    cd_g = bwd_call(cd1_fn, "d_cd_conv", (nd, bsz), cd_ins, cd_in_specs, [dvconv, dmix1],
                    [cd_out_spec, pl.BlockSpec((seq, LANE), lambda j, b: (b, nd + j))], list(range(4)),
                    [_sd((t, CD_IN), BF), _sd((CONF_K, D)), _sd((1, D)), _sd((SC_K, D))], [cd_u_spec] + cd_par,
                    [None, (1,), (1,), (1,)])
    du1 = cd_g[0]
    grads['conf_dw_w'], grads['conf_dw_b'], grads['sc_conv_w'] = cd_g[1][None], cd_g[2], cd_g[3][None]
    g_cd_in = matmul(du1, h4, 'tn', "d_cd_w_in", BF).reshape(nd, 5, LANE, D).transpose(1, 0, 2, 3).reshape(CD_IN, D)
    ex.put_grads('cd', G_CD, {('cd_w_in', 0): g_cd_in, ('cd_w_out', 0): g_cd_out})
    dh4 = matmul(du1, w_cd_in, 'nn', "d_h_cd")

    dx2, dmo0, gain_grads[(0, 5)], gain_grads[(1, 0)] = bwd_seg_res(x2, mo0, gain(0, 5), gain(1, 0), dx3, dh4, "d_res_0c")
    dh3, per_layer['mlp_w1'][0], per_layer['mlp_w2'][0] = mlp_bwd(0, h3, dmo0, sv)
    dx1, dao0, gain_grads[(0, 3)], gain_grads[(0, 4)] = bwd_seg_res(x1, ao0, gain(0, 3), gain(0, 4), dx2, dh3, "d_res_0b")
    dh2, per_layer['xa_wq'][0], per_layer['xa_wkv'][0], per_layer['xa_wo'][0] = attention_bwd(0, h2, dao0, sv)
    ex.put_grads('l0', G_L0, {(k, 0): v[0] for k, v in per_layer.items()})
    dx0r, dm0, gain_grads[(0, 1)], gain_grads[(0, 2)] = bwd_seg_res(x0, m0, gain(0, 1), gain(0, 2), dx1, dh2, "d_res_0a")
    g_ab_out = matmul(mix0, dm0, 'tn', "d_ab_w_out", BF)
    dmix0 = matmul(dm0, big[('ab_w_out', 0)], 'nt', "d_mix0")
    dxbc_act, dz, ddt, ddtb, dalog, ddsk, dnw = ssd_bwd(xbc_act, u0, dtb, alog, dsk, p['ssm_norm'], consts, hs, dmix0, bsz, seq)
    grads['ssm_dt_bias'] = ddtb[:, :SSM_HEADS]
    grads['ssm_a_log'] = dalog[:, :SSM_HEADS]
    grads['ssm_d'] = ddsk[:, :SSM_HEADS]
    grads['ssm_norm'] = dnw
    dxr, dcw, dcb = bwd_call(conv4_fn, "d_ssm_conv", (ncb, bsz), [u0, conv_w, conv_b], conv_in_specs,
                             [dxbc_act], [conv_out_spec], [0, 1, 2],
                             [_sd((t, SSM_CONV_DIM), BF), _sd((SSM_CONV, SSM_CONV_DIM)), _sd((1, SSM_CONV_DIM))],
                             [conv_out_spec, conv_in_specs[1], conv_in_specs[2]], [None, (1,), (1,)])
    grads['ssm_conv_w'], grads['ssm_conv_b'] = _xbc_ungroup(dcw, 1)[None], _xbc_ungroup(dcb, 1)
    dpool, dpw, dps = [], [], []
    for g in range(POOL_GROUPS):
        seqspec = pl.BlockSpec((seq, PG), lambda b, g=g: (b, g))
        one = pl.BlockSpec((seq, PG), lambda b: (b, 0))
        wspec = pl.BlockSpec((PG, PG), lambda b: (0, 0))
        sspec = pl.BlockSpec((1, PG), lambda b, g=g: (0, g))
        a, bb, c = bwd_call(make_pool_fn(g), f"d_pool_{g}", (bsz,), [u0, p['pool_w'][0, g], p['pool_scale']],
                            [seqspec, wspec, sspec], [dmix0], [seqspec], [0, 1, 2],
                            [_sd((t, PG), BF), _sd((PG, PG)), _sd((1, PG))], [one, wspec, pl.BlockSpec((1, PG), lambda b: (0, 0))],
                            [None, (0,), (0,)])
        dpool.append(a)
        dpw.append(bb)
        dps.append(c)
    grads['pool_w'] = jnp.stack(dpw)[None]
    grads['pool_scale'] = jnp.concatenate(dps, axis=1)
    du0 = jnp.concatenate(dpool + [dz, dxr, ddt.astype(BF)], axis=1)
    g_ab_in = matmul(du0, h0, 'tn', "d_ab_w_in", BF)
    g_ab_in = jnp.concatenate([g_ab_in[:xbc0], _xbc_ungroup(g_ab_in[xbc0:xbc0 + SSM_CONV_DIM], 0),
                               g_ab_in[xbc0 + SSM_CONV_DIM:AB_IN]], axis=0)
    ex.put_grads('ab', G_AB, {('ab_w_in', 0): g_ab_in, ('ab_w_out', 0): g_ab_out})
    dh0 = matmul(du0, w_ab_in, 'nn', "d_h_ab")
    dx, dg00 = bwd_call(seg_in_res, "d_norm_in", (nb,), [x0, gain(0, 0)], [_rows(D), _par(D)], [dx0r, dh0],
                        [_rows(D), _rows(D)], [0, 1], [_sd((t, D)), _sd((1, D))], [_rows(D), _par(D)], [None, (0,)])
    gain_grads[(0, 0)] = dg00
    grads['norm_gains'] = jnp.stack([jnp.concatenate([gain_grads[(l, i)] for i in range(6)], axis=0) for l in range(2)])
    return loss, dx, grads
```

```python
import functools
import math

import numpy as np
import jax
import jax.numpy as jnp
from jax import lax
from jax.experimental import pallas as pl
from jax.experimental.pallas import tpu as pltpu

BF = jnp.bfloat16
F32 = jnp.float32
HI = lax.Precision.HIGHEST

N_DEV = 8
D = 1024
N_MEM = 256
XA_HEADS = 4
XA_DH = D // XA_HEADS
POOL_GROUPS = 4
PG = 128
POOL_W = POOL_GROUPS * PG
SSM_INNER = 1024
SSM_GROUPS = 2
SSM_GSZ = SSM_INNER // SSM_GROUPS
SSM_HEADS = 16
SSM_P = 64
SSM_N = 128
SSM_CONV = 4
SSM_CONV_DIM = SSM_INNER + 2 * SSM_GROUPS * SSM_N
SSM_XBC_G = SSM_GSZ + 2 * SSM_N
CHUNK = 128
AB_IN = POOL_W + SSM_INNER + SSM_CONV_DIM + SSM_HEADS
AB_IN_PAD = POOL_W + SSM_INNER + SSM_CONV_DIM + 128
AB_OUT = POOL_W + SSM_INNER
CONF_K = 31
SC_K = 3
CD_IN = 5 * D
CD_OUT = 2 * D
MLP_H = 4 * D
RMS_EPS = 1e-6
LN_EPS = 1e-5
ADAM_LR = 0.001
ADAM_B1 = 0.9
ADAM_B2 = 0.999
ADAM_EPS = 1e-08
ADAM_WD = 0.01
ADAM_STEP = 10
VMEM_LIMIT = 56 * 1024 * 1024
LANE = 128

NAMES = ['x', 'mem', 'norm_gains', 'xa_wq', 'xa_wkv', 'xa_wo', 'mlp_w1', 'mlp_w2', 'ab_w_in', 'pool_w', 'pool_scale',
         'ssm_conv_w', 'ssm_conv_b', 'ssm_dt_bias', 'ssm_a_log', 'ssm_d', 'ssm_norm', 'ab_w_out', 'cd_w_in', 'conf_dw_w',
         'conf_dw_b', 'conf_ln_g', 'conf_ln_b', 'sc_conv_w', 'cd_w_out', 'loss_target']
WEIGHTS = NAMES[2:25]
BIG = [('xa_wq', 1), ('xa_wkv', 2), ('xa_wo', 1), ('mlp_w1', 2), ('mlp_w2', 1), ('cd_w_in', 2), ('cd_w_out', 1),
       ('ab_w_out', 1), ('ab_w_in', 2)]
SMALL_SHARDED = ['norm_gains', 'ssm_conv_w', 'conf_dw_w', 'conf_dw_b', 'conf_ln_g', 'conf_ln_b', 'sc_conv_w']
REPLICATED = ['pool_w', 'pool_scale', 'ssm_conv_b', 'ssm_dt_bias', 'ssm_a_log', 'ssm_d', 'ssm_norm']


def _dg(a, b, ca, cb, prec=None):
    return lax.dot_general(a, b, (((ca,), (cb,)), ((), ())), precision=prec, preferred_element_type=F32)


@functools.partial(jax.custom_vjp, nondiff_argnums=(2, 3))
def bdot(a, b, ca, cb):
    return _dg(a.astype(BF), b.astype(BF), ca, cb)


def _bdot_fwd(a, b, ca, cb):
    return bdot(a, b, ca, cb), (a, b)


def _bdot_bwd(ca, cb, res, g):
    a, b = res
    g16, a16, b16 = g.astype(BF), a.astype(BF), b.astype(BF)
    da = _dg(g16, b16, 1, 1 - cb) if ca == 1 else _dg(b16, g16, 1 - cb, 1)
    db = _dg(g16, a16, 0, 1 - ca) if cb == 1 else _dg(a16, g16, 1 - ca, 0)
    return da.astype(a.dtype), db.astype(b.dtype)


bdot.defvjp(_bdot_fwd, _bdot_bwd)


def _split3(a):
    a1 = a.astype(BF)
    r1 = a - a1.astype(F32)
    a2 = r1.astype(BF)
    a3 = (r1 - a2.astype(F32)).astype(BF)
    return a1, a2, a3


def _exact_right(a, c):
    m = a.shape[0]
    if m % 16:
        return sum(_dg(p, c, 1, 0) for p in _split3(a))
    o = _dg(jnp.concatenate(_split3(a), axis=0), c, 1, 0)
    return o[:m] + o[m:2 * m] + o[2 * m:]


def _exact_left(c, a):
    n = a.shape[1]
    o = _dg(c, jnp.concatenate(_split3(a), axis=1), 1, 0)
    return o[:, :n] + o[:, n:2 * n] + o[:, 2 * n:]


@jax.custom_vjp
def cmat(a, c, ct):
    return _exact_right(a, c)


def _cmat_fwd(a, c, ct):
    return cmat(a, c, ct), (c, ct)


def _cmat_bwd(res, g):
    c, ct = res
    return _exact_right(g, ct), jnp.zeros_like(c), jnp.zeros_like(ct)


cmat.defvjp(_cmat_fwd, _cmat_bwd)


@jax.custom_vjp
def cmatl(c, ct, a):
    return _exact_left(c, a)


def _cmatl_fwd(c, ct, a):
    return cmatl(c, ct, a), (c, ct)


def _cmatl_bwd(res, g):
    c, ct = res
    return jnp.zeros_like(c), jnp.zeros_like(ct), _exact_left(ct, g)


cmatl.defvjp(_cmatl_fwd, _cmatl_bwd)


SUBLANES = 8


def _taps(x, shifts, down):
    n, c = x.shape
    pad = _round_up(max(shifts), SUBLANES)
    if pad == 0:
        return {0: x}
    zeros = jnp.zeros((pad, c), x.dtype)
    xp = jnp.concatenate([zeros, x] if down else [x, zeros], axis=0)
    rolled, out = {0: xp}, {}
    for s in shifts:
        a, b = divmod(s, SUBLANES)
        if b not in rolled:
            rolled[b] = pltpu.roll(xp, b if down else n + pad - b, 0)
        off = pad - SUBLANES * a if down else SUBLANES * a
        out[s] = rolled[b][off:off + n]
    return out


def _shift_down(x, k):
    return _taps(x, [k], True)[k]


def _shift_up(x, k):
    return _taps(x, [k], False)[k]


@functools.partial(jax.custom_vjp, nondiff_argnums=(1,))
def shift(x, k):
    return _shift_down(x, k)


def _shift_fwd(x, k):
    return _shift_down(x, k), None


def _shift_bwd(k, _, g):
    return (_shift_up(g, k),)


shift.defvjp(_shift_fwd, _shift_bwd)


@functools.partial(jax.custom_vjp, nondiff_argnums=(2,))
def cconv(u, w, width):
    taps = _taps(u, list(range(width)), True)
    acc = u * w[width - 1:width, :]
    for k in range(width - 1):
        acc = acc + taps[width - 1 - k] * w[k:k + 1, :]
    return acc


def _cconv_fwd(u, w, width):
    return cconv(u, w, width), (u, w)


def _cconv_bwd(width, res, g):
    u, w = res
    rows = lax.broadcasted_iota(jnp.int32, w.shape, 0)
    du = g * w[width - 1:width, :]
    dw = jnp.where(rows == width - 1, jnp.sum(g * u, axis=0, keepdims=True), 0.0)
    g_taps = _taps(g, list(range(width)), False)
    u_taps = _taps(u, list(range(width)), True)
    for k in range(width - 1):
        s = width - 1 - k
        du = du + g_taps[s] * w[k:k + 1, :]
        dw = dw + jnp.where(rows == k, jnp.sum(g * u_taps[s], axis=0, keepdims=True), 0.0)
    return du, dw


cconv.defvjp(_cconv_fwd, _cconv_bwd)


def _rms(x, g):
    return x * lax.rsqrt(jnp.mean(x * x, axis=-1, keepdims=True) + RMS_EPS) * g


def _params(sem=None):
    return pltpu.CompilerParams(dimension_semantics=sem, vmem_limit_bytes=VMEM_LIMIT)


def _f32(v):
    return v if v.dtype == F32 else v.astype(F32)


def _first(axes):
    ok = None
    for ax in axes:
        c = pl.program_id(ax) == 0
        ok = c if ok is None else jnp.logical_and(ok, c)
    return ok


def fwd_call(fn, name, grid, ins, in_specs, out_shapes, out_specs):
    n_in = len(ins)

    def body(*refs):
        outs = fn(*[_f32(r[...]) for r in refs[:n_in]])
        for r, o in zip(refs[n_in:], outs):
            r[...] = o.astype(r.dtype)

    return pl.pallas_call(body, name=name, grid=grid, in_specs=in_specs, out_specs=out_specs, out_shape=out_shapes,
                          compiler_params=_params())(*ins)


def bwd_call(fn, name, grid, ins, in_specs, cots, cot_specs, gidx, g_shapes, g_specs, g_acc):
    n_in, n_cot = len(ins), len(cots)

    def body(*refs):
        vals = [_f32(r[...]) for r in refs[:n_in]]

        def f_sel(*dv):
            full = list(vals)
            for i, v in zip(gidx, dv):
                full[i] = v
            return tuple(fn(*full))

        outs, vjp = jax.vjp(f_sel, *[vals[i] for i in gidx])
        cts = tuple(_f32(r[...]) for r in refs[n_in:n_in + n_cot])
        grads = vjp(cts)
        for r, g, acc in zip(refs[n_in + n_cot:], grads, g_acc):
            if acc is None:
                r[...] = g.astype(r.dtype)
            else:
                @pl.when(_first(acc))
                def _():
                    r[...] = jnp.zeros_like(r)

                r[...] += g.astype(r.dtype)

    return pl.pallas_call(body, name=name, grid=grid, in_specs=list(in_specs) + list(cot_specs), out_specs=g_specs,
                          out_shape=g_shapes, compiler_params=_params())(*ins, *cots)


def _tile(dim, pref):
    if dim <= pref:
        return dim
    best = None
    for t in range(LANE, pref + 1, LANE):
        if dim % t == 0:
            best = t
    assert best is not None, dim
    return best


MATMUL_VMEM_BUDGET = 40 * 1024 * 1024


def _matmul_tiles(m, n, k, a_bytes, b_bytes, out_bytes):
    tn = _tile(n, 1024)
    for tk_pref in (k, 2048, 1024, 512):
        tk = _tile(k, tk_pref)
        for tm_pref in (1024, 512, 256):
            tm = _tile(m, tm_pref)
            need = 2 * (tm * tk * a_bytes + tk * tn * b_bytes + tm * tn * out_bytes) + (0 if tk == k else tm * tn * 4)
            need += (tm * tk * 2 if a_bytes == 4 else 0) + (tk * tn * 2 if b_bytes == 4 else 0)
            if need <= MATMUL_VMEM_BUDGET:
                return tm, tn, tk
    raise ValueError((m, n, k))


def matmul(a, b, mode, name, out_dtype=F32, epilogue=None, extras=(), after=()):
    if mode == 'nn':
        (m, k), (k2, n) = a.shape, b.shape
    elif mode == 'nt':
        (m, k), (n, k2) = a.shape, b.shape
    else:
        (k, m), (k2, n) = a.shape, b.shape
    assert k == k2, (name, a.shape, b.shape)
    n_extra = len(extras)
    out_dtypes = out_dtype if isinstance(out_dtype, tuple) else (out_dtype,)
    per_out = sum(jnp.dtype(dt).itemsize for dt in out_dtypes) + sum(e.dtype.itemsize for e in extras)
    tm, tn, tk = _matmul_tiles(m, n, k, a.dtype.itemsize, b.dtype.itemsize, per_out)
    nk = k // tk
    ca = 0 if mode == 'tn' else 1
    cb = 1 if mode == 'nt' else 0
    a_spec = pl.BlockSpec((tk, tm), lambda i, j, kk: (kk, i)) if mode == 'tn' else pl.BlockSpec((tm, tk), lambda i, j, kk: (i, kk))
    b_spec = pl.BlockSpec((tn, tk), lambda i, j, kk: (j, kk)) if mode == 'nt' else pl.BlockSpec((tk, tn), lambda i, j, kk: (kk, j))

    def finish(o_refs, extra_refs, acc):
        outs = (acc,) if epilogue is None else epilogue(acc, *[_f32(e[...]) for e in extra_refs])
        for o_ref, o in zip(o_refs, outs):
            o_ref[...] = o.astype(o_ref.dtype)

    n_after = len(after)

    def body_whole_k(a_ref, b_ref, *refs):
        refs = refs[n_after:]
        finish(refs[n_extra:], refs[:n_extra], _dg(a_ref[...].astype(BF), b_ref[...].astype(BF), ca, cb))

    def body_split_k(a_ref, b_ref, *refs):
        refs = refs[n_after:]
        extra_refs, o_refs, acc = refs[:n_extra], refs[n_extra:-1], refs[-1]
        kk = pl.program_id(2)

        @pl.when(kk == 0)
        def _():
            acc[...] = jnp.zeros_like(acc)

        acc[...] += _dg(a_ref[...].astype(BF), b_ref[...].astype(BF), ca, cb)

        @pl.when(kk == nk - 1)
        def _():
            finish(o_refs, extra_refs, acc[...])

    tile = pl.BlockSpec((tm, tn), lambda i, j, kk: (i, j))
    outs = pl.pallas_call(
        body_whole_k if nk == 1 else body_split_k, name=name, grid=(m // tm, n // tn, nk),
        in_specs=[a_spec, b_spec] + [pl.BlockSpec(memory_space=pl.ANY)] * n_after + [tile] * n_extra, out_specs=[tile] * len(out_dtypes),
        out_shape=[jax.ShapeDtypeStruct((m, n), dt) for dt in out_dtypes],
        scratch_shapes=[] if nk == 1 else [pltpu.VMEM((tm, tn), F32)],
        compiler_params=_params(("parallel", "parallel", "arbitrary")))(a, b, *after, *extras)
    return outs if isinstance(out_dtype, tuple) else outs[0]


_FLIPS = [(0, 0, 1), (1, 0, 0), (0, 1, 0), (1, 1, 0), (1, 0, 1), (0, 1, 1), (1, 1, 1)]


def _me():
    return lax.axis_index("x"), lax.axis_index("y"), lax.axis_index("c")


def _flip(pos, f):
    return tuple(jnp.where(fi == 1, 1 - p, p) if fi else p for p, fi in zip(pos, f))


def _slot(pos):
    return 4 * pos[0] + 2 * pos[1] + pos[2]


def all_gather(v, name):
    def body(v_ref, out_ref, send_sems, recv_sems, local_sem):
        me = _me()
        sibling = _flip(me, (0, 0, 1))
        chips = [_flip(me, f) for f in ((1, 0, 0), (0, 1, 0), (1, 1, 0))]

        def copy(k, block, to, src=None):
            return pltpu.make_async_remote_copy(
                src_ref=out_ref.at[_slot(block)] if src is None else src, dst_ref=out_ref.at[_slot(block)],
                send_sem=send_sems.at[k], recv_sem=recv_sems.at[k], device_id=to, device_id_type=pl.DeviceIdType.MESH)

        mine = pltpu.make_async_copy(v_ref, out_ref.at[_slot(me)], local_sem)
        mine.start()
        first = [copy(0, me, sibling, src=v_ref)] + [copy(1 + j, me, chip, src=v_ref) for j, chip in enumerate(chips)]
        for cp in first:
            cp.start()
        passed = [copy(4 + j, chip, sibling) for j, chip in enumerate(chips)]
        for j, chip in enumerate(chips):
            copy(1 + j, chip, me).wait_recv()
            passed[j].start()
        copy(0, sibling, me).wait_recv()
        for j, chip in enumerate(chips):
            copy(4 + j, _flip(chip, (0, 0, 1)), me).wait_recv()
        for cp in first + passed:
            cp.wait_send()
        mine.wait()

    return pl.pallas_call(
        body, name=name, out_shape=jax.ShapeDtypeStruct((N_DEV,) + v.shape, v.dtype),
        in_specs=[pl.BlockSpec(memory_space=pl.ANY)], out_specs=pl.BlockSpec(memory_space=pl.ANY),
        scratch_shapes=[pltpu.SemaphoreType.DMA((7,)), pltpu.SemaphoreType.DMA((7,)), pltpu.SemaphoreType.DMA(())],
    )(v)


def sum_slots(v, name, tr=256):
    _, r, c = v.shape
    tr = _tile_rows(r, tr)

    def body(v_ref, o_ref):
        acc = v_ref[0].astype(F32)
        for s in range(1, N_DEV):
            acc = acc + v_ref[s].astype(F32)
        o_ref[...] = acc

    return pl.pallas_call(body, name=name, grid=(r // tr,), in_specs=[pl.BlockSpec((N_DEV, tr, c), lambda i: (0, i, 0))],
                          out_specs=pl.BlockSpec((tr, c), lambda i: (i, 0)), out_shape=jax.ShapeDtypeStruct((r, c), F32),
                          compiler_params=_params())(v)


def _tile_rows(r, pref):
    if r <= pref:
        return r
    best = None
    for t in range(8, pref + 1, 8):
        if r % t == 0:
            best = t
    return r if best is None else best


def _adamw_math(w, m, v, g):
    nm = ADAM_B1 * m + (1.0 - ADAM_B1) * g
    nv = ADAM_B2 * v + (1.0 - ADAM_B2) * jnp.square(g)
    m_hat = nm / (1.0 - ADAM_B1 ** ADAM_STEP)
    v_hat = nv / (1.0 - ADAM_B2 ** ADAM_STEP)
    return -ADAM_LR * (m_hat / (jnp.sqrt(v_hat) + ADAM_EPS) + ADAM_WD * w), nm, nv


def update_from_slots(lands, offs, w, m, v, transposed, name):
    layers, a, b = w.shape
    n_land = len(lands)
    if transposed:
        rb, tk = LANE, 512
        assert a % tk == 0 and b % rb == 0 and all(o % rb == 0 for o in offs), (name, w.shape, offs)
        grid = (layers, a // tk, b // rb)
        land_block = (N_DEV, rb, tk)
        tile = pl.BlockSpec((None, tk, rb), lambda l, i, j: (l, i, j))

        def land_spec(layer):
            base = offs[layer] // rb
            return pl.BlockSpec(land_block, lambda l, i, j: (0, base + jnp.where(l == layer, j, 0), jnp.where(l == layer, i, 0)))
    else:
        fits = [t for t in (256, 128, 64) if a % t == 0 and all(o % t == 0 for o in offs)]
        assert fits or all(o == 0 for o in offs), (name, w.shape, offs)
        tr = max(fits) if fits else a
        grid = (layers, a // tr)
        land_block = (N_DEV, _round_up(tr, MEMBER_ROW_TILE), b)
        tile = pl.BlockSpec((None, tr, b), lambda l, i: (l, i, 0))

        def land_spec(layer):
            base = offs[layer] // tr
            return pl.BlockSpec(land_block, lambda l, i: (0, base + jnp.where(l == layer, i, 0), 0))

    def body(*refs):
        land_refs, (w_ref, m_ref, v_ref, g_ref, d_ref, nm_ref, nv_ref, acc) = refs[:n_land], refs[n_land:]
        for layer, land in enumerate(land_refs):
            @pl.when(pl.program_id(0) == layer)
            def _(land=land):
                rows = acc.shape[0]
                s = land[0, :rows].astype(F32)
                for k in range(1, N_DEV):
                    s = s + land[k, :rows].astype(F32)
                acc[...] = s

        g = acc[...].T if transposed else acc[...]
        d, nm, nv = _adamw_math(w_ref[...], m_ref[...], v_ref[...], g)
        g_ref[...] = g
        d_ref[...] = d
        nm_ref[...] = nm
        nv_ref[...] = nv

    sh = jax.ShapeDtypeStruct(w.shape, F32)
    return pl.pallas_call(
        body, name=name, grid=grid, in_specs=[land_spec(layer) for layer in range(n_land)] + [tile] * 3, out_specs=[tile] * 4,
        out_shape=[sh] * 4, scratch_shapes=[pltpu.VMEM((rb, tk) if transposed else (tr, b), F32)],
        compiler_params=_params())(*lands, w, m, v)


def adamw_many(ws, ms, vs, gs, name):
    n = len(ws)

    def body(*refs):
        for i in range(n):
            d, nm, nv = _adamw_math(refs[i][...], refs[n + i][...], refs[2 * n + i][...], refs[3 * n + i][...])
            refs[4 * n + i][...] = d
            refs[5 * n + i][...] = nm
            refs[6 * n + i][...] = nv

    vmem = pl.BlockSpec(memory_space=pltpu.VMEM)
    shapes = [jax.ShapeDtypeStruct(a.shape, F32) for a in ws]
    res = pl.pallas_call(body, name=name, in_specs=[vmem] * (4 * n), out_specs=[vmem] * (3 * n), out_shape=shapes * 3,
                         compiler_params=_params())(*ws, *ms, *vs, *gs)
    return res[:n], res[n:2 * n], res[2 * n:]


def adamw(w, m, v, g, name):
    r, c = w.shape
    tr = _tile_rows(r, 512 if c <= 1024 else 128)

    def body(w_ref, m_ref, v_ref, g_ref, d_ref, nm_ref, nv_ref):
        d_ref[...], nm_ref[...], nv_ref[...] = _adamw_math(w_ref[...], m_ref[...], v_ref[...], g_ref[...])

    spec = pl.BlockSpec((tr, c), lambda i: (i, 0))
    sh = jax.ShapeDtypeStruct((r, c), F32)
    return pl.pallas_call(body, name=name, grid=(r // tr,), in_specs=[spec] * 4, out_specs=[spec] * 3,
                          out_shape=[sh] * 3, compiler_params=_params())(w, m, v, g)


def seg_in(x, g):
    return (_rms(x, g),)


def seg_in_res(x, g):
    return x, _rms(x, g)


def seg_res(x, m, ga, gb):
    x1 = x + _rms(m, ga)
    return x1, _rms(x1, gb)


def seg_out(x, m, ga):
    return (x + _rms(m, ga),)


def act_epilogue(r):
    t = jnp.maximum(r, 0.0)
    return r, t * t


def act_bwd_epilogue(drr, r):
    return (drr * (2.0 * jnp.maximum(r, 0.0)),)


def seg_ln(v, g, b):
    mu = jnp.mean(v, axis=-1, keepdims=True)
    var = jnp.mean(jnp.square(v - mu), axis=-1, keepdims=True)
    vn = (v - mu) * lax.rsqrt(var + LN_EPS) * g + b
    return (jax.nn.silu(vn),)


def make_pool_fn(group):
    window = 2 ** (group + 1)

    def pool_fn(ug, pw, scale):
        s = ug
        for lvl in range(group + 1):
            s = s + shift(s, 2 ** lvl)
        cnt = jnp.minimum(lax.broadcasted_iota(jnp.int32, ug.shape, 0) + 1, window).astype(F32)
        return (bdot(s / cnt - ug, pw, 1, 0) * scale,)

    return pool_fn


def conv4_fn(xr, w, b):
    return (jax.nn.silu(cconv(xr, w, SSM_CONV) + b),)


def cd1_fn(u, dww, dwb, scw):
    val, gate, bg, cg, hh = (u[:, k * LANE:(k + 1) * LANE] for k in range(5))
    v = val * jax.nn.sigmoid(gate)
    vc = cconv(v, dww, CONF_K) + dwb
    sc = bg * cconv(cg * hh, scw, SC_K)
    return vc, sc


def attn_fn(q, kv):
    outs = []
    for h in range(XA_HEADS):
        cols = slice(h * XA_DH, (h + 1) * XA_DH)
        s = bdot(q[:, cols], kv[:, cols], 1, 1) / math.sqrt(XA_DH)
        p = jax.nn.softmax(s, axis=-1)
        outs.append(bdot(p, kv[:, D + h * XA_DH:D + (h + 1) * XA_DH], 1, 0))
    return (jnp.concatenate(outs, axis=1),)


def ssd_chunk(xbc, z, dtraw, dtb, alog, dsk, nw, h0, h1, h2, h3, e64, e64t, ecat, ecatt, tril, trilt):
    xs, bm, cm = xbc[:, :SSM_GSZ], xbc[:, SSM_GSZ:SSM_GSZ + SSM_N], xbc[:, SSM_GSZ + SSM_N:]
    hin = (h0, h1, h2, h3)
    dt = jax.nn.softplus(dtraw + dtb)
    a = -jnp.exp(alog)
    d_a = dt * a
    cs = cmatl(tril, trilt, d_a)
    cs_cat = cmat(cs, ecat, ecatt)
    cs64, cs128 = cs_cat[:, :SSM_GSZ], cs_cat[:, SSM_GSZ:]
    dt64 = cmat(dt, e64, e64t)
    row = lax.broadcasted_iota(jnp.int32, (8, LANE), 0)
    heads = jnp.where(row == 0, dsk, jnp.where(row == 1, jnp.sum(d_a, axis=0, keepdims=True), 0.0))
    heads64 = cmat(heads, e64, e64t)
    d64, tot64 = heads64[0:1, :], heads64[1:2, :]
    xdt = xs * dt64
    cb = bdot(cm, bm, 1, 1)
    li = lax.broadcasted_iota(jnp.int32, (CHUNK, CHUNK), 0)
    si = lax.broadcasted_iota(jnp.int32, (CHUNK, CHUNK), 1)
    causal = li >= si
    lane = lax.broadcasted_iota(jnp.int32, (CHUNK, LANE), 1)
    xw = xdt * jnp.exp(tot64 - cs64)
    ecs = jnp.exp(cs64)
    etot = jnp.exp(tot64)
    ycols, hout = [], []
    for j in range(4):
        sl = slice(j * LANE, (j + 1) * LANE)
        xj = xdt[:, sl]
        ys = []
        for hh in range(2):
            r = 2 * j + hh
            col = cs128[:, r * LANE:(r + 1) * LANE]
            decay = jnp.exp(jnp.where(causal, col - col.T, -1e30))
            ys.append(bdot(cb * decay, xj, 1, 0))
        y_diag = jnp.where(lane < SSM_P, ys[0], ys[1])
        y_off = bdot(cm, hin[j], 1, 0) * ecs[:, sl]
        ycols.append(y_diag + y_off)
        hout.append(etot[:, sl] * hin[j] + bdot(bm, xw[:, sl], 0, 0))
    y = jnp.concatenate(ycols, axis=1) + d64 * xs
    y = y * jax.nn.silu(z)
    yn = y * lax.rsqrt(jnp.mean(y * y, axis=-1, keepdims=True) + RMS_EPS) * nw
    return (yn,) + tuple(hout)


def _xbc_group(a, axis):
    parts = []
    for g in range(SSM_GROUPS):
        for start, width in ((g * SSM_GSZ, SSM_GSZ), (SSM_INNER + g * SSM_N, SSM_N), (SSM_INNER + (SSM_GROUPS + g) * SSM_N, SSM_N)):
            parts.append(lax.slice_in_dim(a, start, start + width, axis=axis))
    return jnp.concatenate(parts, axis=axis)


def _xbc_ungroup(a, axis):
    xs, bs, cs = [], [], []
    for g in range(SSM_GROUPS):
        base = g * SSM_XBC_G
        xs.append(lax.slice_in_dim(a, base, base + SSM_GSZ, axis=axis))
        bs.append(lax.slice_in_dim(a, base + SSM_GSZ, base + SSM_GSZ + SSM_N, axis=axis))
        cs.append(lax.slice_in_dim(a, base + SSM_GSZ + SSM_N, base + SSM_XBC_G, axis=axis))
    return jnp.concatenate(xs + bs + cs, axis=axis)


def _ssd_consts():
    h = np.arange(LANE)[:, None]
    e64 = np.stack([(h == g * 8 + np.arange(SSM_GSZ)[None, :] // SSM_P) for g in range(SSM_GROUPS)]).astype(np.float32)
    e128 = np.stack([(h == g * 8 + np.arange(8 * LANE)[None, :] // LANE) for g in range(SSM_GROUPS)]).astype(np.float32)
    ecat = np.concatenate([e64, e128], axis=2)
    tril = np.tril(np.ones((CHUNK, CHUNK), np.float32))
    return tuple(jnp.asarray(c, dtype=BF) for c in (e64, e64.transpose(0, 2, 1), ecat, ecat.transpose(0, 2, 1), tril, tril.T))


def _ssd_specs(nc, rev):
    def ci(c):
        return nc - 1 - c if rev else c

    def row(width, col):
        return pl.BlockSpec((CHUNK, width), lambda b, c: (b * nc + ci(c), col))

    def whole(shape):
        return pl.BlockSpec(shape, lambda b, c: (0,) * len(shape))

    data = [row(SSM_CONV_DIM, 0),
            row(SSM_GSZ, 1), row(SSM_GSZ, 2), row(LANE, 24)]
    par = [whole((1, LANE))] * 3 + [whole((1, SSM_INNER))]
    cst = [whole((SSM_GROUPS, LANE, SSM_GSZ)), whole((SSM_GROUPS, SSM_GSZ, LANE)), whole((SSM_GROUPS, LANE, 12 * LANE)),
           whole((SSM_GROUPS, 12 * LANE, LANE)), whole((CHUNK, CHUNK)), whole((CHUNK, CHUNK))]
    hsave = pl.BlockSpec((None, None, SSM_GROUPS, 4, SSM_N, LANE), lambda b, c: (b, ci(c), 0, 0, 0, 0))
    return data, par, cst, hsave, row, whole


def _ssd_group_args(g, xbc, z, dtr, dtb, alog, dsk, nw):
    return (xbc[:, g * SSM_XBC_G:(g + 1) * SSM_XBC_G], z[g], dtr, dtb, alog, dsk, nw[:, g * SSM_GSZ:(g + 1) * SSM_GSZ])


def ssd_fwd(xbc_act, u, dtb, alog, dsk, nw, consts, bsz, seq):
    nc = seq // CHUNK
    data, par, cst, hsave, row, _ = _ssd_specs(nc, False)

    def body(xbc, z0, z1, dtr, dtb_r, alog_r, dsk_r, nw_r, e64, e64t, ecat, ecatt, tril, trilt, yn_ref, hs_ref, h):
        @pl.when(pl.program_id(1) == 0)
        def _():
            h[...] = jnp.zeros_like(h)

        hs_ref[...] = h[...]
        ys = []
        for g in range(SSM_GROUPS):
            args = _ssd_group_args(g, xbc[...], (z0[...], z1[...]), dtr[...], dtb_r[...], alog_r[...], dsk_r[...], nw_r[...])
            outs = ssd_chunk(*args, h[g, 0], h[g, 1], h[g, 2], h[g, 3], e64[g], e64t[g], ecat[g], ecatt[g], tril[...], trilt[...])
            ys.append(outs[0])
            for j in range(4):
                h[g, j] = outs[1 + j]
        yn_ref[...] = jnp.concatenate(ys, axis=1).astype(yn_ref.dtype)

    t = bsz * seq
    return pl.pallas_call(
        body, name="ssd_fwd", grid=(bsz, nc), in_specs=data + par + cst, out_specs=[row(SSM_INNER, 0), hsave],
        out_shape=[jax.ShapeDtypeStruct((t, SSM_INNER), BF), jax.ShapeDtypeStruct((bsz, nc, SSM_GROUPS, 4, SSM_N, LANE), F32)],
        scratch_shapes=[pltpu.VMEM((SSM_GROUPS, 4, SSM_N, LANE), F32)], compiler_params=_params(),
    )(xbc_act, u, u, u, dtb, alog, dsk, nw, *consts)


def ssd_bwd(xbc_act, u, dtb, alog, dsk, nw, consts, hs, dmix, bsz, seq):
    nc = seq // CHUNK
    data, par, cst, hsave, row, whole = _ssd_specs(nc, True)
    t = bsz * seq
    pcol = POOL_W // SSM_GSZ

    def body(xbc, z0, z1, dtr, dtb_r, alog_r, dsk_r, nw_r, e64, e64t, ecat, ecatt, tril, trilt, hs_ref, dy0, dy1,
             dxbc, dz, ddt, ddtb, dalog, ddsk, dnw, dh):
        @pl.when(pl.program_id(1) == 0)
        def _():
            dh[...] = jnp.zeros_like(dh)

        per_group = []
        for g, dyn in enumerate((dy0, dy1)):
            cst_vals = (e64[g], e64t[g], ecat[g], ecatt[g], tril[...], trilt[...])
            prim = _ssd_group_args(g, xbc[...], (z0[...], z1[...]), dtr[...], dtb_r[...], alog_r[...], dsk_r[...], nw_r[...])
            prim = prim + (hs_ref[g, 0], hs_ref[g, 1], hs_ref[g, 2], hs_ref[g, 3])
            _, vjp = jax.vjp(lambda *args, c=cst_vals: ssd_chunk(*args, *c), *prim)
            gr = vjp((dyn[...].astype(F32), dh[g, 0], dh[g, 1], dh[g, 2], dh[g, 3]))
            for j in range(4):
                dh[g, j] = gr[7 + j]
            per_group.append(gr)
        g0, g1 = per_group
        dxbc[...] = jnp.concatenate([g0[0], g1[0]], axis=1)
        dz[...] = jnp.concatenate([g0[1], g1[1]], axis=1).astype(dz.dtype)
        ddt[...] = g0[2] + g1[2]

        @pl.when(_first((0, 1)))
        def _():
            for r in (ddtb, dalog, ddsk, dnw):
                r[...] = jnp.zeros_like(r)

        ddtb[...] += g0[3] + g1[3]
        dalog[...] += g0[4] + g1[4]
        ddsk[...] += g0[5] + g1[5]
        dnw[...] += jnp.concatenate([g0[6], g1[6]], axis=1)

    out_specs = [row(SSM_CONV_DIM, 0), row(SSM_INNER, 0), row(LANE, 0), whole((1, LANE)), whole((1, LANE)), whole((1, LANE)),
                 whole((1, SSM_INNER))]
    lane = jax.ShapeDtypeStruct((1, LANE), F32)
    out_shape = [jax.ShapeDtypeStruct((t, SSM_CONV_DIM), F32), jax.ShapeDtypeStruct((t, SSM_INNER), BF),
                 jax.ShapeDtypeStruct((t, LANE), F32), lane, lane, lane, jax.ShapeDtypeStruct((1, SSM_INNER), F32)]
    return pl.pallas_call(
        body, name="ssd_bwd", grid=(bsz, nc), in_specs=data + par + cst + [hsave, row(SSM_GSZ, pcol), row(SSM_GSZ, pcol + 1)],
        out_specs=out_specs, out_shape=out_shape, scratch_shapes=[pltpu.VMEM((SSM_GROUPS, 4, SSM_N, LANE), F32)],
        compiler_params=_params(),
    )(xbc_act, u, u, u, dtb, alog, dsk, nw, *consts, hs, dmix, dmix)


TB = 512


def _rows(d, col=0):
    return pl.BlockSpec((TB, d), lambda i: (i, col))


def _par(d):
    return pl.BlockSpec((1, d), lambda i: (0, 0))


def _sd(shape, dtype=F32):
    return jax.ShapeDtypeStruct(shape, dtype)


def _round_up(n, m):
    return -(-n // m) * m


def _pad_rows(a, rows):
    return jnp.pad(a, ((0, rows - a.shape[0]), (0, 0)))


def _pack128(arrs):
    flat = jnp.concatenate([a.reshape(-1) for a in arrs])
    n = flat.shape[0]
    rows = -(-n // (8 * LANE)) * 8
    return jnp.pad(flat, (0, rows * LANE - n)).reshape(rows, LANE)


def _unpack128(packed, shapes):
    flat = packed.reshape(-1)
    out, off = [], 0
    for s in shapes:
        n = int(np.prod(s))
        out.append(flat[off:off + n].reshape(s))
        off += n
    return out


def kernel(x, mem, norm_gains, xa_wq, xa_wkv, xa_wo, mlp_w1, mlp_w2, ab_w_in, pool_w, pool_scale, ssm_conv_w, ssm_conv_b, ssm_dt_bias, ssm_a_log, ssm_d, ssm_norm, ab_w_out, cd_w_in, conf_dw_w, conf_dw_b, conf_ln_g, conf_ln_b, sc_conv_w, cd_w_out, loss_target, m_norm_gains, m_xa_wq, m_xa_wkv, m_xa_wo, m_mlp_w1, m_mlp_w2, m_ab_w_in, m_pool_w, m_pool_scale, m_ssm_conv_w, m_ssm_conv_b, m_ssm_dt_bias, m_ssm_a_log, m_ssm_d, m_ssm_norm, m_ab_w_out, m_cd_w_in, m_conf_dw_w, m_conf_dw_b, m_conf_ln_g, m_conf_ln_b, m_sc_conv_w, m_cd_w_out, v_norm_gains, v_xa_wq, v_xa_wkv, v_xa_wo, v_mlp_w1, v_mlp_w2, v_ab_w_in, v_pool_w, v_pool_scale, v_ssm_conv_w, v_ssm_conv_b, v_ssm_dt_bias, v_ssm_a_log, v_ssm_d, v_ssm_norm, v_ab_w_out, v_cd_w_in, v_conf_dw_w, v_conf_dw_b, v_conf_ln_g, v_conf_ln_b, v_sc_conv_w, v_cd_w_out):
    args = locals()
    w = {n: args[n] for n in WEIGHTS}
    mom_m = {n: args["m_" + n] for n in WEIGHTS}
    mom_v = {n: args["v_" + n] for n in WEIGHTS}
    ex = Exchange(w)
    loss_local, grad_x, small_grads = local_step(x, mem, loss_target, ex)
    loss = lax.psum(loss_local, ("x", "y", "c"))
    outs = {}

    started = ex.put_small(small_grads)
    landed = {key: ex.landed(key, started) for key in ('l1', 'cd', 'l0')}
    late = []
    for n, keys in (('mlp_w1', ('l0', 'l1')), ('mlp_w2', ('l0', 'l1')), ('xa_wkv', ('l0', 'l1')), ('xa_wq', ('l0', 'l1')),
                    ('xa_wo', ('l0', 'l1')), ('cd_w_in', ('cd',)), ('cd_w_out', ('cd',))):
        lands = [landed[key][0] for key in keys]
        offs = [landed[key][1][(n, layer)] for layer, key in enumerate(keys)]
        outs[n] = update_from_slots(lands, offs, w[n], mom_m[n], mom_v[n], SHARD_AXIS[n] == 2, "update_" + n)
        late.append(outs[n][1])
    g_own = ex.reduced_small(late)
    land_ab, offs_ab = ex.landed('ab', late)
    outs['ab_w_out'] = update_from_slots([land_ab], [offs_ab[('ab_w_out', 0)]], w['ab_w_out'], mom_m['ab_w_out'],
                                         mom_v['ab_w_out'], False, "update_ab_w_out")
    res = update_from_slots([land_ab], [offs_ab[('ab_w_in', 0)]], jnp.swapaxes(w['ab_w_in'], 1, 2), jnp.swapaxes(mom_m['ab_w_in'], 1, 2),
                            jnp.swapaxes(mom_v['ab_w_in'], 1, 2), False, "update_ab_w_in")
    outs['ab_w_in'] = tuple(jnp.swapaxes(r, 1, 2) for r in res)
    small = SMALL_SHARDED + REPLICATED
    upd = adamw_many([w[n] for n in small], [mom_m[n] for n in small], [mom_v[n] for n in small], [g_own[n] for n in small],
                     "adamw_small")
    for i, n in enumerate(small):
        outs[n] = (g_own[n], upd[0][i], upd[1][i], upd[2][i])
    return (loss, grad_x.reshape(x.shape), *[outs[n][0] for n in WEIGHTS], *[outs[n][1] for n in WEIGHTS],
            *[outs[n][2] for n in WEIGHTS], *[outs[n][3] for n in WEIGHTS])


G_AB = (('ab_w_in', 0), ('ab_w_out', 0))
G_L0 = (('xa_wq', 0), ('xa_wkv', 0), ('xa_wo', 0), ('mlp_w1', 0), ('mlp_w2', 0))
G_L1 = (('xa_wq', 1), ('xa_wkv', 1), ('xa_wo', 1), ('mlp_w1', 1), ('mlp_w2', 1))
G_CD = (('cd_w_in', 0), ('cd_w_out', 0))
GATHER_CHAIN = {'l0': ('cd', G_CD), 'cd': ('l1', G_L1)}
SHARD_AXIS = dict(BIG)
MEMBER_ROW_TILE = 64
FLAT_ROW_TILE = 128


def _members(group, w):
    out = []
    for n, layer in group:
        shp = w[n].shape[1:]
        if SHARD_AXIS[n] == 2:
            shp = (shp[1], shp[0])
        assert shp[1] == D, (n, shp)
        out.append((n, layer, shp, shp[0], _round_up(shp[0], MEMBER_ROW_TILE)))
    return out


def _group_rows(group, w):
    return _round_up(sum(m[4] for m in _members(group, w)), FLAT_ROW_TILE)


def _flat_shards(group, w):
    parts = []
    for n, layer, _, _, padded in _members(group, w):
        shard = w[n][layer].astype(BF)
        parts.append(_pad_rows(shard.T if SHARD_AXIS[n] == 2 else shard, padded))
    return _pad_rows(jnp.concatenate(parts, axis=0), _group_rows(group, w))


def _full_from_slots(land, group, w):
    out, off = {}, 0
    for n, layer, shp, rows, padded in _members(group, w):
        out[(n, layer)] = land[:, off:off + rows].reshape(N_DEV * rows, D)
        off += padded
    return out


def _slots_from_full(grads, group, w):
    parts = []
    for n, layer, shp, rows, padded in _members(group, w):
        blk = grads[(n, layer)].astype(BF).reshape(N_DEV, rows, D)
        parts.append(jnp.pad(blk, ((0, 0), (0, padded - rows), (0, 0))))
    send = jnp.concatenate(parts, axis=1)
    return jnp.pad(send, ((0, 0), (0, _group_rows(group, w) - send.shape[1]), (0, 0)))


_HBM = pl.BlockSpec(memory_space=pltpu.HBM)
_SEM = pl.BlockSpec(memory_space=pltpu.SEMAPHORE)
_ANY = pl.BlockSpec(memory_space=pl.ANY)


def _peer_copy(k, src, dst, send_sems, recv_sems, peer):
    return pltpu.make_async_remote_copy(src_ref=src, dst_ref=dst, send_sem=send_sems.at[k], recv_sem=recv_sems.at[k],
                                        device_id=peer, device_id_type=pl.DeviceIdType.MESH)


def exchange_start(src, name, scatter):
    shape = src.shape[-2:]

    def body(src_ref, land_ref, send_sems, recv_sems, src_thru, land_thru, token):
        me = _me()
        for k, f in enumerate(_FLIPS):
            peer = _flip(me, f)
            piece = src_ref.at[_slot(peer)] if scatter else src_ref
            _peer_copy(k, piece, land_ref.at[_slot(me)], send_sems, recv_sems, peer).start()
        token[...] = jnp.zeros_like(token)

    land = pltpu.with_memory_space_constraint(lax.empty((N_DEV,) + shape, src.dtype), pltpu.HBM)
    return pl.pallas_call(
        body, name=name,
        out_shape=(pltpu.SemaphoreType.DMA((7,)), pltpu.SemaphoreType.DMA((7,)), pltpu.HBM(src.shape, src.dtype),
                   pltpu.HBM((N_DEV,) + shape, src.dtype), jax.ShapeDtypeStruct((8, LANE), F32)),
        in_specs=(_HBM, _HBM), out_specs=(_SEM, _SEM, _HBM, _HBM, pl.BlockSpec(memory_space=pltpu.VMEM)),
        input_output_aliases={0: 2, 1: 3},
        compiler_params=pltpu.CompilerParams(has_side_effects=pltpu.SideEffectType.DATAFLOW_SIDE_EFFECTING),
    )(pltpu.with_memory_space_constraint(src, pltpu.HBM), land)


def exchange_wait(handles, after, name, scatter):
    send_sems, recv_sems, src_thru, land_thru, _ = handles
    after = list(after) if isinstance(after, (list, tuple)) else [after]

    def body(src_ref, land_ref, send_sems, recv_sems, *rest):
        token = rest[-1]
        me = _me()
        for k, f in enumerate(_FLIPS):
            peer = _flip(me, f)
            piece = src_ref.at[_slot(peer)] if scatter else src_ref
            cp = _peer_copy(k, piece, land_ref.at[_slot(peer)], send_sems, recv_sems, peer)
            cp.wait_send()
            cp.wait_recv()
        token[...] = jnp.zeros_like(token)

    return pl.pallas_call(
        body, name=name, out_shape=(pltpu.HBM(src_thru.shape, src_thru.dtype), pltpu.HBM(land_thru.shape, land_thru.dtype),
                                    jax.ShapeDtypeStruct((8, LANE), F32)),
        in_specs=(_HBM, _HBM, _SEM, _SEM) + (_ANY,) * len(after), out_specs=(_HBM, _HBM, pl.BlockSpec(memory_space=pltpu.VMEM)),
        input_output_aliases={0: 0, 1: 1},
        compiler_params=pltpu.CompilerParams(has_side_effects=pltpu.SideEffectType.DATAFLOW_SIDE_EFFECTING),
    )(src_thru, land_thru, send_sems, recv_sems, *after)


class Exchange:
    def __init__(self, w):
        self.w = w
        self.me = _slot(_me())
        shapes = [w[n].shape for n in SMALL_SHARDED]
        gs = all_gather(_pack128([w[n] for n in SMALL_SHARDED]), "gather_small")
        per_dev = [_unpack128(gs[d], shapes) for d in range(N_DEV)]
        self.small = {n: jnp.concatenate([per_dev[d][i] for d in range(N_DEV)], axis=-1) for i, n in enumerate(SMALL_SHARDED)}
        self.small.update({n: w[n] for n in REPLICATED})
        self.now = _full_from_slots(all_gather(_flat_shards(G_AB, w), "gather_ab"), G_AB, w)
        self.gathers = {'l0': (G_L0, exchange_start(_flat_shards(G_L0, w), "gather_l0_start", False))}
        self.tokens = [self.gathers['l0'][1][4]]
        self.reductions = {}

    def take_tokens(self):
        toks, self.tokens = self.tokens, []
        return toks

    def weights(self, key, after):
        if key == 'ab':
            return self.now
        group, handles = self.gathers[key]
        _, land, done = exchange_wait(handles, after, f"gather_{key}_wait", False)
        nxt = GATHER_CHAIN.get(key)
        if nxt is not None:
            src = _flat_shards(nxt[1], self.w) + done[0, 0].astype(BF)
            self.gathers[nxt[0]] = (nxt[1], exchange_start(src, f"gather_{nxt[0]}_start", False))
            self.tokens.append(self.gathers[nxt[0]][1][4])
        land = lax.dynamic_update_slice(land, handles[2][None], (self.me, 0, 0))
        return _full_from_slots(land, group, self.w)

    def put_grads(self, key, group, grads):
        send = _slots_from_full(grads, group, self.w)
        handles = exchange_start(send, f"reduce_{key}_start", True)
        self.reductions[key] = (group, handles)
        self.tokens.append(handles[4])

    def landed(self, key, after):
        group, handles = self.reductions[key]
        send, land, _ = exchange_wait(handles, after, f"reduce_{key}_wait", True)
        mine = lax.dynamic_slice_in_dim(send, self.me, 1, axis=0)
        land = lax.dynamic_update_slice(land, mine, (self.me, 0, 0))
        offs, off = {}, 0
        for n, layer, _, _, padded in _members(group, self.w):
            offs[(n, layer)] = off
            off += padded
        return land, offs

    def put_small(self, small_grads):
        small = SMALL_SHARDED + REPLICATED
        self.small_shapes = [small_grads[n].shape for n in small]
        self.small_handles = exchange_start(_pack128([small_grads[n] for n in small]), "gather_small_grads_start", False)
        return self.small_handles[4]

    def reduced_small(self, after):
        small = SMALL_SHARDED + REPLICATED
        src, land, _ = exchange_wait(self.small_handles, after, "gather_small_grads_wait", False)
        gs = lax.dynamic_update_slice(land, src[None], (self.me, 0, 0))
        tot = _unpack128(sum_slots(gs, "sum_small", 1024), self.small_shapes)
        out = {}
        for n, g in zip(small, tot):
            if n in SMALL_SHARDED:
                width = self.w[n].shape[-1]
                g = lax.dynamic_slice_in_dim(g, self.me * width, width, axis=g.ndim - 1)
            out[n] = g
        return out


def local_step(x, mem, target, ex):
    bsz, seq, _ = x.shape
    t = bsz * seq
    nb = t // TB
    nc = seq // CHUNK
    x0 = x.reshape(t, D)
    mem2 = mem.reshape(bsz * N_MEM, D)
    tgt = target.reshape(t, D)
    p = ex.small
    gains = p['norm_gains']
    big = dict(ex.weights('ab', None))

    def gain(layer, i):
        g = gains[layer, i].reshape(1, D)
        for tok in ex.take_tokens():
            g = g + tok[0, 0]
        return g

    consts = _ssd_consts()
    grads = {}
    saved = [dict(), dict()]

    def run_seg_res(xin, m, ga, gb, name):
        return fwd_call(seg_res, name, (nb,), [xin, m, ga, gb], [_rows(D), _rows(D), _par(D), _par(D)],
                        [_sd((t, D)), _sd((t, D), BF)], [_rows(D), _rows(D)])

    def attn_specs():
        nq = seq // TB
        q = pl.BlockSpec((TB, D), lambda b, i: (b * nq + i, 0))
        kv = pl.BlockSpec((N_MEM, 2 * D), lambda b, i: (b, 0))
        return (bsz, nq), q, kv

    def attention_fwd(layer, xin, hin, sv):
        q = matmul(hin, big[('xa_wq', layer)], 'nn', f"q_{layer}", BF)
        kv = matmul(mem2, big[('xa_wkv', layer)], 'nt', f"kv_{layer}", BF)
        grid, qs, kvs = attn_specs()
        o, = fwd_call(attn_fn, f"attn_{layer}", grid, [q, kv], [qs, kvs], [_sd((t, D), BF)], [qs])
        ao = matmul(o, big[('xa_wo', layer)], 'nn', f"ao_{layer}")
        sv.update(q=q, kv=kv, o=o, ao=ao)
        return ao

    def mlp_fwd(layer, hin, sv):
        r, rr = matmul(hin, big[('mlp_w1', layer)], 'nt', f"mlp1_{layer}", (BF, BF), epilogue=act_epilogue)
        mo = matmul(rr, big[('mlp_w2', layer)], 'nn', f"mlp2_{layer}")
        sv.update(r=r, rr=rr, mo=mo)
        return mo

    sv = saved[0]
    h0, = fwd_call(seg_in, "norm_in", (nb,), [x0, gain(0, 0)], [_rows(D), _par(D)], [_sd((t, D), BF)], [_rows(D)])
    xbc0 = POOL_W + SSM_INNER
    w_ab_in = big[('ab_w_in', 0)]
    w_ab_in = _pad_rows(jnp.concatenate([w_ab_in[:xbc0], _xbc_group(w_ab_in[xbc0:xbc0 + SSM_CONV_DIM], 0),
                                         w_ab_in[xbc0 + SSM_CONV_DIM:]], axis=0), AB_IN_PAD)
    conv_w, conv_b = _xbc_group(p['ssm_conv_w'][0], 1), _xbc_group(p['ssm_conv_b'], 1)
    u0 = matmul(h0, w_ab_in, 'nt', "ab_in")
    pool_outs = []
    for g in range(POOL_GROUPS):
        seqspec = pl.BlockSpec((seq, PG), lambda b, g=g: (b, g))
        po, = fwd_call(make_pool_fn(g), f"pool_{g}", (bsz,), [u0, p['pool_w'][0, g], p['pool_scale']],
                       [seqspec, pl.BlockSpec((PG, PG), lambda b: (0, 0)), pl.BlockSpec((1, PG), lambda b, g=g: (0, g))],
                       [_sd((t, PG), BF)], [pl.BlockSpec((seq, PG), lambda b: (b, 0))])
        pool_outs.append(po)
    cw = 256
    ncb = SSM_CONV_DIM // cw
    cbase = (POOL_W + SSM_INNER) // cw
    conv_in_specs = [pl.BlockSpec((seq, cw), lambda j, b: (b, cbase + j)), pl.BlockSpec((SSM_CONV, cw), lambda j, b: (0, j)),
                     pl.BlockSpec((1, cw), lambda j, b: (0, j))]
    conv_out_spec = pl.BlockSpec((seq, cw), lambda j, b: (b, j))
    xbc_act, = fwd_call(conv4_fn, "ssm_conv", (ncb, bsz), [u0, conv_w, conv_b], conv_in_specs,
                        [_sd((t, SSM_CONV_DIM))], [conv_out_spec])
    dtb = jnp.pad(p['ssm_dt_bias'], ((0, 0), (0, LANE - SSM_HEADS)))
    alog = jnp.pad(p['ssm_a_log'], ((0, 0), (0, LANE - SSM_HEADS)))
    dsk = jnp.pad(p['ssm_d'], ((0, 0), (0, LANE - SSM_HEADS)))
    yn, hs = ssd_fwd(xbc_act, u0, dtb, alog, dsk, p['ssm_norm'], consts, bsz, seq)
    mix0 = jnp.concatenate(pool_outs + [yn], axis=1)
    m0 = matmul(mix0, big[('ab_w_out', 0)], 'nn', "ab_out")
    x1, h2 = run_seg_res(x0, m0, gain(0, 1), gain(0, 2), "res_0a")
    big.update(ex.weights('l0', h2))
    ao0 = attention_fwd(0, x1, h2, sv)
    x2, h3 = run_seg_res(x1, ao0, gain(0, 3), gain(0, 4), "res_0b")
    mo0 = mlp_fwd(0, h3, sv)
    big.update(ex.weights('cd', mo0))
    x3, h4 = run_seg_res(x2, mo0, gain(0, 5), gain(1, 0), "res_0c")

    sv1 = saved[1]
    nd = D // LANE
    w_cd_in = big[('cd_w_in', 0)].reshape(5, nd, LANE, D).transpose(1, 0, 2, 3).reshape(CD_IN, D)
    u1 = matmul(h4, w_cd_in, 'nt', "cd_in")
    cd_par = [pl.BlockSpec((CONF_K, LANE), lambda j, b: (0, j)), pl.BlockSpec((1, LANE), lambda j, b: (0, j)),
              pl.BlockSpec((SC_K, LANE), lambda j, b: (0, j))]
    cd_ins = [u1, p['conf_dw_w'][0], p['conf_dw_b'], p['sc_conv_w'][0]]
    cd_u_spec = pl.BlockSpec((seq, 5 * LANE), lambda j, b: (b, j))
    cd_in_specs = [cd_u_spec] + cd_par
    cd_out_spec = pl.BlockSpec((seq, LANE), lambda j, b: (b, j))
    vconv, sc_out = fwd_call(cd1_fn, "cd_conv", (nd, bsz), cd_ins, cd_in_specs, [_sd((t, D)), _sd((t, D), BF)],
                             [cd_out_spec, cd_out_spec])
    conf, = fwd_call(seg_ln, "conf_ln", (nb,), [vconv, p['conf_ln_g'], p['conf_ln_b']], [_rows(D), _par(D), _par(D)],
                     [_sd((t, D), BF)], [_rows(D)])
    mix1 = jnp.concatenate([conf, sc_out], axis=1)
    m1 = matmul(mix1, big[('cd_w_out', 0)], 'nn', "cd_out")
    x4, h5 = run_seg_res(x3, m1, gain(1, 1), gain(1, 2), "res_1a")
    big.update(ex.weights('l1', h5))
    ao1 = attention_fwd(1, x4, h5, sv1)
    x5, h6 = run_seg_res(x4, ao1, gain(1, 3), gain(1, 4), "res_1b")
    mo1 = mlp_fwd(1, h6, sv1)

    def loss_body(x_ref, m_ref, g_ref, t_ref, dx_ref, dm_ref, dg_ref, acc_ref):
        (y,), vjp = jax.vjp(seg_out, x_ref[...], m_ref[...], g_ref[...])
        d = y - t_ref[...]
        dx, dm, dg = vjp((d / float(D),))
        dx_ref[...] = dx
        dm_ref[...] = dm.astype(dm_ref.dtype)

        @pl.when(pl.program_id(0) == 0)
        def _():
            acc_ref[...] = jnp.zeros_like(acc_ref)
            dg_ref[...] = jnp.zeros_like(dg_ref)

        acc_ref[...] += jnp.sum(d * d, axis=0, keepdims=True)
        dg_ref[...] += dg

    dx5, dmo1, dg15, lanes = pl.pallas_call(
        loss_body, name="loss_head", grid=(nb,), in_specs=[_rows(D), _rows(D), _par(D), _rows(D)],
        out_specs=[_rows(D), _rows(D), _par(D), _par(D)], out_shape=[_sd((t, D)), _sd((t, D), BF), _sd((1, D)), _sd((1, D))],
        compiler_params=_params())(x5, mo1, gain(1, 5), tgt)
    loss = 0.5 * jnp.sum(lanes) / float(D)

    gain_grads = {(1, 5): dg15}

    def bwd_seg_res(xin, m, ga, gb, dx1, dh, name):
        return bwd_call(seg_res, name, (nb,), [xin, m, ga, gb], [_rows(D), _rows(D), _par(D), _par(D)], [dx1, dh],
                        [_rows(D), _rows(D)], [0, 1, 2, 3], [_sd((t, D)), _sd((t, D), BF), _sd((1, D)), _sd((1, D))],
                        [_rows(D), _rows(D), _par(D), _par(D)], [None, None, (0,), (0,)])

    def mlp_bwd(layer, hin, dmo, sv):
        grads_w2 = matmul(sv['rr'], dmo, 'tn', f"d_mlp_w2_{layer}", BF)
        dr, = matmul(dmo, big[('mlp_w2', layer)], 'nt', f"d_r_{layer}", (BF,), epilogue=act_bwd_epilogue, extras=[sv['r']])
        grads_w1 = matmul(dr, hin, 'tn', f"d_mlp_w1_{layer}", BF)
        dh = matmul(dr, big[('mlp_w1', layer)], 'nn', f"d_h_mlp_{layer}")
        return dh, grads_w1, grads_w2

    def attention_bwd(layer, hin, dao, sv):
        g_wo = matmul(sv['o'], dao, 'tn', f"d_xa_wo_{layer}", BF)
        do = matmul(dao, big[('xa_wo', layer)], 'nt', f"d_o_{layer}", BF)
        grid, qs, kvs = attn_specs()
        dq, dkv = bwd_call(attn_fn, f"d_attn_{layer}", grid, [sv['q'], sv['kv']], [qs, kvs], [do], [qs], [0, 1],
                           [_sd((t, D), BF), _sd((bsz * N_MEM, 2 * D))], [qs, kvs], [None, (1,)])
        g_wkv = matmul(dkv, mem2, 'tn', f"d_xa_wkv_{layer}", BF)
        g_wq = matmul(hin, dq, 'tn', f"d_xa_wq_{layer}", BF)
        dh = matmul(dq, big[('xa_wq', layer)], 'nt', f"d_h_attn_{layer}")
        return dh, g_wq, g_wkv, g_wo

    per_layer = {k: [None, None] for k in ('xa_wq', 'xa_wkv', 'xa_wo', 'mlp_w1', 'mlp_w2')}

    dh6, per_layer['mlp_w1'][1], per_layer['mlp_w2'][1] = mlp_bwd(1, h6, dmo1, sv1)
    dx4, dao1, gain_grads[(1, 3)], gain_grads[(1, 4)] = bwd_seg_res(x4, ao1, gain(1, 3), gain(1, 4), dx5, dh6, "d_res_1b")
    dh5, per_layer['xa_wq'][1], per_layer['xa_wkv'][1], per_layer['xa_wo'][1] = attention_bwd(1, h5, dao1, sv1)
    ex.put_grads('l1', G_L1, {(k, 1): v[1] for k, v in per_layer.items()})
    dx3, dm1, gain_grads[(1, 1)], gain_grads[(1, 2)] = bwd_seg_res(x3, m1, gain(1, 1), gain(1, 2), dx4, dh5, "d_res_1a")
    g_cd_out = matmul(mix1, dm1, 'tn', "d_cd_w_out", BF)
    dmix1 = matmul(dm1, big[('cd_w_out', 0)], 'nt', "d_mix1")
    dvconv, dlg, dlb = bwd_call(seg_ln, "d_conf_ln", (nb,), [vconv, p['conf_ln_g'], p['conf_ln_b']],
                                [_rows(D), _par(D), _par(D)], [dmix1], [_rows(D, 0)], [0, 1, 2],
                                [_sd((t, D)), _sd((1, D)), _sd((1, D))], [_rows(D), _par(D), _par(D)], [None, (0,), (0,)])
    grads['conf_ln_g'], grads['conf_ln_b'] = dlg, dlb
    cd_g = bwd_call(cd1_fn, "d_cd_conv", (nd, bsz), cd_ins, cd_in_specs, [dvconv, dmix1],
                    [cd_out_spec, pl.BlockSpec((seq, LANE), lambda j, b: (b, nd + j))], list(range(4)),
                    [_sd((t, CD_IN), BF), _sd((CONF_K, D)), _sd((1, D)), _sd((SC_K, D))], [cd_u_spec] + cd_par,
                    [None, (1,), (1,), (1,)])
    du1 = cd_g[0]
    grads['conf_dw_w'], grads['conf_dw_b'], grads['sc_conv_w'] = cd_g[1][None], cd_g[2], cd_g[3][None]
    g_cd_in = matmul(du1, h4, 'tn', "d_cd_w_in", BF).reshape(nd, 5, LANE, D).transpose(1, 0, 2, 3).reshape(CD_IN, D)
    ex.put_grads('cd', G_CD, {('cd_w_in', 0): g_cd_in, ('cd_w_out', 0): g_cd_out})
    dh4 = matmul(du1, w_cd_in, 'nn', "d_h_cd")

    dx2, dmo0, gain_grads[(0, 5)], gain_grads[(1, 0)] = bwd_seg_res(x2, mo0, gain(0, 5), gain(1, 0), dx3, dh4, "d_res_0c")
    dh3, per_layer['mlp_w1'][0], per_layer['mlp_w2'][0] = mlp_bwd(0, h3, dmo0, sv)
    dx1, dao0, gain_grads[(0, 3)], gain_grads[(0, 4)] = bwd_seg_res(x1, ao0, gain(0, 3), gain(0, 4), dx2, dh3, "d_res_0b")
    dh2, per_layer['xa_wq'][0], per_layer['xa_wkv'][0], per_layer['xa_wo'][0] = attention_bwd(0, h2, dao0, sv)
    ex.put_grads('l0', G_L0, {(k, 0): v[0] for k, v in per_layer.items()})
    dx0r, dm0, gain_grads[(0, 1)], gain_grads[(0, 2)] = bwd_seg_res(x0, m0, gain(0, 1), gain(0, 2), dx1, dh2, "d_res_0a")
    g_ab_out = matmul(mix0, dm0, 'tn', "d_ab_w_out", BF)
    dmix0 = matmul(dm0, big[('ab_w_out', 0)], 'nt', "d_mix0")
    dxbc_act, dz, ddt, ddtb, dalog, ddsk, dnw = ssd_bwd(xbc_act, u0, dtb, alog, dsk, p['ssm_norm'], consts, hs, dmix0, bsz, seq)
    grads['ssm_dt_bias'] = ddtb[:, :SSM_HEADS]
    grads['ssm_a_log'] = dalog[:, :SSM_HEADS]
    grads['ssm_d'] = ddsk[:, :SSM_HEADS]
    grads['ssm_norm'] = dnw
    dxr, dcw, dcb = bwd_call(conv4_fn, "d_ssm_conv", (ncb, bsz), [u0, conv_w, conv_b], conv_in_specs,
                             [dxbc_act], [conv_out_spec], [0, 1, 2],
                             [_sd((t, SSM_CONV_DIM), BF), _sd((SSM_CONV, SSM_CONV_DIM)), _sd((1, SSM_CONV_DIM))],
                             [conv_out_spec, conv_in_specs[1], conv_in_specs[2]], [None, (1,), (1,)])
    grads['ssm_conv_w'], grads['ssm_conv_b'] = _xbc_ungroup(dcw, 1)[None], _xbc_ungroup(dcb, 1)
    dpool, dpw, dps = [], [], []
    for g in range(POOL_GROUPS):
        seqspec = pl.BlockSpec((seq, PG), lambda b, g=g: (b, g))
        one = pl.BlockSpec((seq, PG), lambda b: (b, 0))
        wspec = pl.BlockSpec((PG, PG), lambda b: (0, 0))
        sspec = pl.BlockSpec((1, PG), lambda b, g=g: (0, g))
        a, bb, c = bwd_call(make_pool_fn(g), f"d_pool_{g}", (bsz,), [u0, p['pool_w'][0, g], p['pool_scale']],
                            [seqspec, wspec, sspec], [dmix0], [seqspec], [0, 1, 2],
                            [_sd((t, PG), BF), _sd((PG, PG)), _sd((1, PG))], [one, wspec, pl.BlockSpec((1, PG), lambda b: (0, 0))],
                            [None, (0,), (0,)])
        dpool.append(a)
        dpw.append(bb)
        dps.append(c)
    grads['pool_w'] = jnp.stack(dpw)[None]
    grads['pool_scale'] = jnp.concatenate(dps, axis=1)
    du0 = jnp.concatenate(dpool + [dz, dxr, ddt.astype(BF)], axis=1)
    g_ab_in = matmul(du0, h0, 'tn', "d_ab_w_in", BF)
    g_ab_in = jnp.concatenate([g_ab_in[:xbc0], _xbc_ungroup(g_ab_in[xbc0:xbc0 + SSM_CONV_DIM], 0),
                               g_ab_in[xbc0 + SSM_CONV_DIM:AB_IN]], axis=0)
    ex.put_grads('ab', G_AB, {('ab_w_in', 0): g_ab_in, ('ab_w_out', 0): g_ab_out})
    dh0 = matmul(du0, w_ab_in, 'nn', "d_h_ab", after=ex.take_tokens())
    dx, dg00 = bwd_call(seg_in_res, "d_norm_in", (nb,), [x0, gain(0, 0)], [_rows(D), _par(D)], [dx0r, dh0],
                        [_rows(D), _rows(D)], [0, 1], [_sd((t, D)), _sd((1, D))], [_rows(D), _par(D)], [None, (0,)])
    gain_grads[(0, 0)] = dg00
    grads['norm_gains'] = jnp.stack([jnp.concatenate([gain_grads[(l, i)] for i in range(6)], axis=0) for l in range(2)])
    return loss, dx, grads
```

```python
import functools
import math

import numpy as np
import jax
import jax.numpy as jnp
from jax import lax
from jax.experimental import pallas as pl
from jax.experimental.pallas import tpu as pltpu

BF = jnp.bfloat16
F32 = jnp.float32
HI = lax.Precision.HIGHEST

N_DEV = 8
D = 1024
N_MEM = 256
XA_HEADS = 4
XA_DH = D // XA_HEADS
POOL_GROUPS = 4
PG = 128
POOL_W = POOL_GROUPS * PG
SSM_INNER = 1024
SSM_GROUPS = 2
SSM_GSZ = SSM_INNER // SSM_GROUPS
SSM_HEADS = 16
SSM_P = 64
SSM_N = 128
SSM_CONV = 4
SSM_CONV_DIM = SSM_INNER + 2 * SSM_GROUPS * SSM_N
SSM_XBC_G = SSM_GSZ + 2 * SSM_N
CHUNK = 128
AB_IN = POOL_W + SSM_INNER + SSM_CONV_DIM + SSM_HEADS
AB_IN_PAD = POOL_W + SSM_INNER + SSM_CONV_DIM + 128
AB_OUT = POOL_W + SSM_INNER
CONF_K = 31
SC_K = 3
CD_IN = 5 * D
CD_OUT = 2 * D
MLP_H = 4 * D
RMS_EPS = 1e-6
LN_EPS = 1e-5
ADAM_LR = 0.001
ADAM_B1 = 0.9
ADAM_B2 = 0.999
ADAM_EPS = 1e-08
ADAM_WD = 0.01
ADAM_STEP = 10
VMEM_LIMIT = 56 * 1024 * 1024
LANE = 128

NAMES = ['x', 'mem', 'norm_gains', 'xa_wq', 'xa_wkv', 'xa_wo', 'mlp_w1', 'mlp_w2', 'ab_w_in', 'pool_w', 'pool_scale',
         'ssm_conv_w', 'ssm_conv_b', 'ssm_dt_bias', 'ssm_a_log', 'ssm_d', 'ssm_norm', 'ab_w_out', 'cd_w_in', 'conf_dw_w',
         'conf_dw_b', 'conf_ln_g', 'conf_ln_b', 'sc_conv_w', 'cd_w_out', 'loss_target']
WEIGHTS = NAMES[2:25]
BIG = [('xa_wq', 1), ('xa_wkv', 2), ('xa_wo', 1), ('mlp_w1', 2), ('mlp_w2', 1), ('cd_w_in', 2), ('cd_w_out', 1),
       ('ab_w_out', 1), ('ab_w_in', 2)]
SMALL_SHARDED = ['norm_gains', 'ssm_conv_w', 'conf_dw_w', 'conf_dw_b', 'conf_ln_g', 'conf_ln_b', 'sc_conv_w']
REPLICATED = ['pool_w', 'pool_scale', 'ssm_conv_b', 'ssm_dt_bias', 'ssm_a_log', 'ssm_d', 'ssm_norm']


def _dg(a, b, ca, cb, prec=None):
    return lax.dot_general(a, b, (((ca,), (cb,)), ((), ())), precision=prec, preferred_element_type=F32)


@functools.partial(jax.custom_vjp, nondiff_argnums=(2, 3))
def bdot(a, b, ca, cb):
    return _dg(a.astype(BF), b.astype(BF), ca, cb)


def _bdot_fwd(a, b, ca, cb):
    return bdot(a, b, ca, cb), (a, b)


def _bdot_bwd(ca, cb, res, g):
    a, b = res
    g16, a16, b16 = g.astype(BF), a.astype(BF), b.astype(BF)
    da = _dg(g16, b16, 1, 1 - cb) if ca == 1 else _dg(b16, g16, 1 - cb, 1)
    db = _dg(g16, a16, 0, 1 - ca) if cb == 1 else _dg(a16, g16, 1 - ca, 0)
    return da.astype(a.dtype), db.astype(b.dtype)


bdot.defvjp(_bdot_fwd, _bdot_bwd)


def _split3(a):
    a1 = a.astype(BF)
    r1 = a - a1.astype(F32)
    a2 = r1.astype(BF)
    a3 = (r1 - a2.astype(F32)).astype(BF)
    return a1, a2, a3


def _exact_right(a, c):
    m = a.shape[0]
    if m % 16:
        return sum(_dg(p, c, 1, 0) for p in _split3(a))
    o = _dg(jnp.concatenate(_split3(a), axis=0), c, 1, 0)
    return o[:m] + o[m:2 * m] + o[2 * m:]


def _exact_left(c, a):
    n = a.shape[1]
    o = _dg(c, jnp.concatenate(_split3(a), axis=1), 1, 0)
    return o[:, :n] + o[:, n:2 * n] + o[:, 2 * n:]


@jax.custom_vjp
def cmat(a, c, ct):
    return _exact_right(a, c)


def _cmat_fwd(a, c, ct):
    return cmat(a, c, ct), (c, ct)


def _cmat_bwd(res, g):
    c, ct = res
    return _exact_right(g, ct), jnp.zeros_like(c), jnp.zeros_like(ct)


cmat.defvjp(_cmat_fwd, _cmat_bwd)


@jax.custom_vjp
def cmatl(c, ct, a):
    return _exact_left(c, a)


def _cmatl_fwd(c, ct, a):
    return cmatl(c, ct, a), (c, ct)


def _cmatl_bwd(res, g):
    c, ct = res
    return jnp.zeros_like(c), jnp.zeros_like(ct), _exact_left(ct, g)


cmatl.defvjp(_cmatl_fwd, _cmatl_bwd)


SUBLANES = 8


def _taps(x, shifts, down):
    n, c = x.shape
    pad = _round_up(max(shifts), SUBLANES)
    if pad == 0:
        return {0: x}
    zeros = jnp.zeros((pad, c), x.dtype)
    xp = jnp.concatenate([zeros, x] if down else [x, zeros], axis=0)
    rolled, out = {0: xp}, {}
    for s in shifts:
        a, b = divmod(s, SUBLANES)
        if b not in rolled:
            rolled[b] = pltpu.roll(xp, b if down else n + pad - b, 0)
        off = pad - SUBLANES * a if down else SUBLANES * a
        out[s] = rolled[b][off:off + n]
    return out


def _shift_down(x, k):
    return _taps(x, [k], True)[k]


def _shift_up(x, k):
    return _taps(x, [k], False)[k]


@functools.partial(jax.custom_vjp, nondiff_argnums=(1,))
def shift(x, k):
    return _shift_down(x, k)


def _shift_fwd(x, k):
    return _shift_down(x, k), None


def _shift_bwd(k, _, g):
    return (_shift_up(g, k),)


shift.defvjp(_shift_fwd, _shift_bwd)


@functools.partial(jax.custom_vjp, nondiff_argnums=(2,))
def cconv(u, w, width):
    taps = _taps(u, list(range(width)), True)
    acc = u * w[width - 1:width, :]
    for k in range(width - 1):
        acc = acc + taps[width - 1 - k] * w[k:k + 1, :]
    return acc


def _cconv_fwd(u, w, width):
    return cconv(u, w, width), (u, w)


def _cconv_bwd(width, res, g):
    u, w = res
    rows = lax.broadcasted_iota(jnp.int32, w.shape, 0)
    du = g * w[width - 1:width, :]
    dw = jnp.where(rows == width - 1, jnp.sum(g * u, axis=0, keepdims=True), 0.0)
    g_taps = _taps(g, list(range(width)), False)
    u_taps = _taps(u, list(range(width)), True)
    for k in range(width - 1):
        s = width - 1 - k
        du = du + g_taps[s] * w[k:k + 1, :]
        dw = dw + jnp.where(rows == k, jnp.sum(g * u_taps[s], axis=0, keepdims=True), 0.0)
    return du, dw


cconv.defvjp(_cconv_fwd, _cconv_bwd)


def _rms(x, g):
    return x * lax.rsqrt(jnp.mean(x * x, axis=-1, keepdims=True) + RMS_EPS) * g


def _params(sem=None):
    return pltpu.CompilerParams(dimension_semantics=sem, vmem_limit_bytes=VMEM_LIMIT)


def _f32(v):
    return v if v.dtype == F32 else v.astype(F32)


def _first(axes):
    ok = None
    for ax in axes:
        c = pl.program_id(ax) == 0
        ok = c if ok is None else jnp.logical_and(ok, c)
    return ok


def fwd_call(fn, name, grid, ins, in_specs, out_shapes, out_specs, into=None):
    n_in = len(ins)
    n_into = 0 if into is None else 1

    def body(*refs):
        outs = fn(*[_f32(r[...]) for r in refs[:n_in]])
        for r, o in zip(refs[n_in + n_into:], outs):
            r[...] = o.astype(r.dtype)

    extra = [] if into is None else [into]
    return pl.pallas_call(body, name=name, grid=grid, in_specs=list(in_specs) + [pl.BlockSpec(memory_space=pl.ANY)] * n_into,
                          out_specs=out_specs, out_shape=out_shapes, input_output_aliases={n_in: 0} if n_into else {},
                          compiler_params=_params())(*ins, *extra)


def bwd_call(fn, name, grid, ins, in_specs, cots, cot_specs, gidx, g_shapes, g_specs, g_acc):
    n_in, n_cot = len(ins), len(cots)

    def body(*refs):
        vals = [_f32(r[...]) for r in refs[:n_in]]

        def f_sel(*dv):
            full = list(vals)
            for i, v in zip(gidx, dv):
                full[i] = v
            return tuple(fn(*full))

        outs, vjp = jax.vjp(f_sel, *[vals[i] for i in gidx])
        cts = tuple(_f32(r[...]) for r in refs[n_in:n_in + n_cot])
        grads = vjp(cts)
        for r, g, acc in zip(refs[n_in + n_cot:], grads, g_acc):
            if acc is None:
                r[...] = g.astype(r.dtype)
            else:
                @pl.when(_first(acc))
                def _():
                    r[...] = jnp.zeros_like(r)

                r[...] += g.astype(r.dtype)

    return pl.pallas_call(body, name=name, grid=grid, in_specs=list(in_specs) + list(cot_specs), out_specs=g_specs,
                          out_shape=g_shapes, compiler_params=_params())(*ins, *cots)


def _tile(dim, pref):
    if dim <= pref:
        return dim
    best = None
    for t in range(LANE, pref + 1, LANE):
        if dim % t == 0:
            best = t
    assert best is not None, dim
    return best


MATMUL_VMEM_BUDGET = 40 * 1024 * 1024


def _matmul_tiles(m, n, k, a_bytes, b_bytes, out_bytes):
    tn = _tile(n, 1024)
    for tk_pref in (k, 2048, 1024, 512):
        tk = _tile(k, tk_pref)
        for tm_pref in (1024, 512, 256):
            tm = _tile(m, tm_pref)
            need = 2 * (tm * tk * a_bytes + tk * tn * b_bytes + tm * tn * out_bytes) + (0 if tk == k else tm * tn * 4)
            need += (tm * tk * 2 if a_bytes == 4 else 0) + (tk * tn * 2 if b_bytes == 4 else 0)
            if need <= MATMUL_VMEM_BUDGET:
                return tm, tn, tk
    raise ValueError((m, n, k))


def matmul(a, b, mode, name, out_dtype=F32, epilogue=None, extras=(), after=()):
    if mode == 'nn':
        (m, k), (k2, n) = a.shape, b.shape
    elif mode == 'nt':
        (m, k), (n, k2) = a.shape, b.shape
    else:
        (k, m), (k2, n) = a.shape, b.shape
    assert k == k2, (name, a.shape, b.shape)
    n_extra = len(extras)
    out_dtypes = out_dtype if isinstance(out_dtype, tuple) else (out_dtype,)
    per_out = sum(jnp.dtype(dt).itemsize for dt in out_dtypes) + sum(e.dtype.itemsize for e in extras)
    tm, tn, tk = _matmul_tiles(m, n, k, a.dtype.itemsize, b.dtype.itemsize, per_out)
    nk = k // tk
    ca = 0 if mode == 'tn' else 1
    cb = 1 if mode == 'nt' else 0
    a_spec = pl.BlockSpec((tk, tm), lambda i, j, kk: (kk, i)) if mode == 'tn' else pl.BlockSpec((tm, tk), lambda i, j, kk: (i, kk))
    b_spec = pl.BlockSpec((tn, tk), lambda i, j, kk: (j, kk)) if mode == 'nt' else pl.BlockSpec((tk, tn), lambda i, j, kk: (kk, j))

    def finish(o_refs, extra_refs, acc):
        outs = (acc,) if epilogue is None else epilogue(acc, *[_f32(e[...]) for e in extra_refs])
        for o_ref, o in zip(o_refs, outs):
            o_ref[...] = o.astype(o_ref.dtype)

    n_after = len(after)

    def body_whole_k(a_ref, b_ref, *refs):
        refs = refs[n_after:]
        finish(refs[n_extra:], refs[:n_extra], _dg(a_ref[...].astype(BF), b_ref[...].astype(BF), ca, cb))

    def body_split_k(a_ref, b_ref, *refs):
        refs = refs[n_after:]
        extra_refs, o_refs, acc = refs[:n_extra], refs[n_extra:-1], refs[-1]
        kk = pl.program_id(2)

        @pl.when(kk == 0)
        def _():
            acc[...] = jnp.zeros_like(acc)

        acc[...] += _dg(a_ref[...].astype(BF), b_ref[...].astype(BF), ca, cb)

        @pl.when(kk == nk - 1)
        def _():
            finish(o_refs, extra_refs, acc[...])

    tile = pl.BlockSpec((tm, tn), lambda i, j, kk: (i, j))
    outs = pl.pallas_call(
        body_whole_k if nk == 1 else body_split_k, name=name, grid=(m // tm, n // tn, nk),
        in_specs=[a_spec, b_spec] + [pl.BlockSpec(memory_space=pl.ANY)] * n_after + [tile] * n_extra, out_specs=[tile] * len(out_dtypes),
        out_shape=[jax.ShapeDtypeStruct((m, n), dt) for dt in out_dtypes],
        scratch_shapes=[] if nk == 1 else [pltpu.VMEM((tm, tn), F32)],
        compiler_params=_params(("parallel", "parallel", "arbitrary")))(a, b, *after, *extras)
    return outs if isinstance(out_dtype, tuple) else outs[0]


_FLIPS = [(0, 0, 1), (1, 0, 0), (0, 1, 0), (1, 1, 0), (1, 0, 1), (0, 1, 1), (1, 1, 1)]


def _me():
    return lax.axis_index("x"), lax.axis_index("y"), lax.axis_index("c")


def _flip(pos, f):
    return tuple(jnp.where(fi == 1, 1 - p, p) if fi else p for p, fi in zip(pos, f))


def _slot(pos):
    return 4 * pos[0] + 2 * pos[1] + pos[2]


def all_gather(v, name):
    def body(v_ref, out_ref, send_sems, recv_sems, local_sem):
        me = _me()
        sibling = _flip(me, (0, 0, 1))
        chips = [_flip(me, f) for f in ((1, 0, 0), (0, 1, 0), (1, 1, 0))]

        def copy(k, block, to, src=None):
            return pltpu.make_async_remote_copy(
                src_ref=out_ref.at[_slot(block)] if src is None else src, dst_ref=out_ref.at[_slot(block)],
                send_sem=send_sems.at[k], recv_sem=recv_sems.at[k], device_id=to, device_id_type=pl.DeviceIdType.MESH)

        mine = pltpu.make_async_copy(v_ref, out_ref.at[_slot(me)], local_sem)
        mine.start()
        first = [copy(0, me, sibling, src=v_ref)] + [copy(1 + j, me, chip, src=v_ref) for j, chip in enumerate(chips)]
        for cp in first:
            cp.start()
        passed = [copy(4 + j, chip, sibling) for j, chip in enumerate(chips)]
        for j, chip in enumerate(chips):
            copy(1 + j, chip, me).wait_recv()
            passed[j].start()
        copy(0, sibling, me).wait_recv()
        for j, chip in enumerate(chips):
            copy(4 + j, _flip(chip, (0, 0, 1)), me).wait_recv()
        for cp in first + passed:
            cp.wait_send()
        mine.wait()

    return pl.pallas_call(
        body, name=name, out_shape=jax.ShapeDtypeStruct((N_DEV,) + v.shape, v.dtype),
        in_specs=[pl.BlockSpec(memory_space=pl.ANY)], out_specs=pl.BlockSpec(memory_space=pl.ANY),
        scratch_shapes=[pltpu.SemaphoreType.DMA((7,)), pltpu.SemaphoreType.DMA((7,)), pltpu.SemaphoreType.DMA(())],
    )(v)


def sum_slots(v, name, tr=256):
    _, r, c = v.shape
    tr = _tile_rows(r, tr)

    def body(v_ref, o_ref):
        acc = v_ref[0].astype(F32)
        for s in range(1, N_DEV):
            acc = acc + v_ref[s].astype(F32)
        o_ref[...] = acc

    return pl.pallas_call(body, name=name, grid=(r // tr,), in_specs=[pl.BlockSpec((N_DEV, tr, c), lambda i: (0, i, 0))],
                          out_specs=pl.BlockSpec((tr, c), lambda i: (i, 0)), out_shape=jax.ShapeDtypeStruct((r, c), F32),
                          compiler_params=_params())(v)


def _tile_rows(r, pref):
    if r <= pref:
        return r
    best = None
    for t in range(8, pref + 1, 8):
        if r % t == 0:
            best = t
    return r if best is None else best


def _adamw_math(w, m, v, g):
    nm = ADAM_B1 * m + (1.0 - ADAM_B1) * g
    nv = ADAM_B2 * v + (1.0 - ADAM_B2) * jnp.square(g)
    m_hat = nm / (1.0 - ADAM_B1 ** ADAM_STEP)
    v_hat = nv / (1.0 - ADAM_B2 ** ADAM_STEP)
    return -ADAM_LR * (m_hat / (jnp.sqrt(v_hat) + ADAM_EPS) + ADAM_WD * w), nm, nv


def update_from_slots(lands, offs, w, m, v, transposed, name):
    layers, a, b = w.shape
    n_land = len(lands)
    if transposed:
        rb, tk = LANE, 512
        assert a % tk == 0 and b % rb == 0 and all(o % rb == 0 for o in offs), (name, w.shape, offs)
        grid = (layers, a // tk, b // rb)
        land_block = (N_DEV, rb, tk)
        tile = pl.BlockSpec((None, tk, rb), lambda l, i, j: (l, i, j))

        def land_spec(layer):
            base = offs[layer] // rb
            return pl.BlockSpec(land_block, lambda l, i, j: (0, base + jnp.where(l == layer, j, 0), jnp.where(l == layer, i, 0)))
    else:
        fits = [t for t in (256, 128, 64) if a % t == 0 and all(o % t == 0 for o in offs)]
        assert fits or all(o == 0 for o in offs), (name, w.shape, offs)
        tr = max(fits) if fits else a
        grid = (layers, a // tr)
        land_block = (N_DEV, _round_up(tr, MEMBER_ROW_TILE), b)
        tile = pl.BlockSpec((None, tr, b), lambda l, i: (l, i, 0))

        def land_spec(layer):
            base = offs[layer] // tr
            return pl.BlockSpec(land_block, lambda l, i: (0, base + jnp.where(l == layer, i, 0), 0))

    def body(*refs):
        land_refs, (w_ref, m_ref, v_ref, g_ref, d_ref, nm_ref, nv_ref, acc) = refs[:n_land], refs[n_land:]
        for layer, land in enumerate(land_refs):
            @pl.when(pl.program_id(0) == layer)
            def _(land=land):
                rows = acc.shape[0]
                s = land[0, :rows].astype(F32)
                for k in range(1, N_DEV):
                    s = s + land[k, :rows].astype(F32)
                acc[...] = s

        g = acc[...].T if transposed else acc[...]
        d, nm, nv = _adamw_math(w_ref[...], m_ref[...], v_ref[...], g)
        g_ref[...] = g
        d_ref[...] = d
        nm_ref[...] = nm
        nv_ref[...] = nv

    sh = jax.ShapeDtypeStruct(w.shape, F32)
    return pl.pallas_call(
        body, name=name, grid=grid, in_specs=[land_spec(layer) for layer in range(n_land)] + [tile] * 3, out_specs=[tile] * 4,
        out_shape=[sh] * 4, scratch_shapes=[pltpu.VMEM((rb, tk) if transposed else (tr, b), F32)],
        compiler_params=_params())(*lands, w, m, v)


def adamw_many(ws, ms, vs, gs, name):
    n = len(ws)

    def body(*refs):
        for i in range(n):
            d, nm, nv = _adamw_math(refs[i][...], refs[n + i][...], refs[2 * n + i][...], refs[3 * n + i][...])
            refs[4 * n + i][...] = d
            refs[5 * n + i][...] = nm
            refs[6 * n + i][...] = nv

    vmem = pl.BlockSpec(memory_space=pltpu.VMEM)
    shapes = [jax.ShapeDtypeStruct(a.shape, F32) for a in ws]
    res = pl.pallas_call(body, name=name, in_specs=[vmem] * (4 * n), out_specs=[vmem] * (3 * n), out_shape=shapes * 3,
                         compiler_params=_params())(*ws, *ms, *vs, *gs)
    return res[:n], res[n:2 * n], res[2 * n:]


def adamw(w, m, v, g, name):
    r, c = w.shape
    tr = _tile_rows(r, 512 if c <= 1024 else 128)

    def body(w_ref, m_ref, v_ref, g_ref, d_ref, nm_ref, nv_ref):
        d_ref[...], nm_ref[...], nv_ref[...] = _adamw_math(w_ref[...], m_ref[...], v_ref[...], g_ref[...])

    spec = pl.BlockSpec((tr, c), lambda i: (i, 0))
    sh = jax.ShapeDtypeStruct((r, c), F32)
    return pl.pallas_call(body, name=name, grid=(r // tr,), in_specs=[spec] * 4, out_specs=[spec] * 3,
                          out_shape=[sh] * 3, compiler_params=_params())(w, m, v, g)


def seg_in(x, g):
    return (_rms(x, g),)


def seg_in_res(x, g):
    return x, _rms(x, g)


def seg_res(x, m, ga, gb):
    x1 = x + _rms(m, ga)
    return x1, _rms(x1, gb)


def seg_out(x, m, ga):
    return (x + _rms(m, ga),)


def act_epilogue(r):
    t = jnp.maximum(r, 0.0)
    return r, t * t


def act_bwd_epilogue(drr, r):
    return (drr * (2.0 * jnp.maximum(r, 0.0)),)


def seg_ln(v, g, b):
    mu = jnp.mean(v, axis=-1, keepdims=True)
    var = jnp.mean(jnp.square(v - mu), axis=-1, keepdims=True)
    vn = (v - mu) * lax.rsqrt(var + LN_EPS) * g + b
    return (jax.nn.silu(vn),)


def make_pool_fn(group):
    window = 2 ** (group + 1)

    def pool_fn(ug, pw, scale):
        s = ug
        for lvl in range(group + 1):
            s = s + shift(s, 2 ** lvl)
        cnt = jnp.minimum(lax.broadcasted_iota(jnp.int32, ug.shape, 0) + 1, window).astype(F32)
        return (bdot(s / cnt - ug, pw, 1, 0) * scale,)

    return pool_fn


def conv4_fn(xr, w, b):
    return (jax.nn.silu(cconv(xr, w, SSM_CONV) + b),)


def cd1_fn(u, dww, dwb, scw):
    val, gate, bg, cg, hh = (u[:, k * LANE:(k + 1) * LANE] for k in range(5))
    v = val * jax.nn.sigmoid(gate)
    vc = cconv(v, dww, CONF_K) + dwb
    sc = bg * cconv(cg * hh, scw, SC_K)
    return vc, sc


def attn_fn(q, kv):
    outs = []
    for h in range(XA_HEADS):
        cols = slice(h * XA_DH, (h + 1) * XA_DH)
        s = bdot(q[:, cols], kv[:, cols], 1, 1) / math.sqrt(XA_DH)
        p = jax.nn.softmax(s, axis=-1)
        outs.append(bdot(p, kv[:, D + h * XA_DH:D + (h + 1) * XA_DH], 1, 0))
    return (jnp.concatenate(outs, axis=1),)


def ssd_chunk(xbc, z, dtraw, dtb, alog, dsk, nw, h0, h1, h2, h3, e64, e64t, ecat, ecatt, tril, trilt):
    xs, bm, cm = xbc[:, :SSM_GSZ], xbc[:, SSM_GSZ:SSM_GSZ + SSM_N], xbc[:, SSM_GSZ + SSM_N:]
    hin = (h0, h1, h2, h3)
    dt = jax.nn.softplus(dtraw + dtb)
    a = -jnp.exp(alog)
    d_a = dt * a
    cs = cmatl(tril, trilt, d_a)
    cs_cat = cmat(cs, ecat, ecatt)
    cs64, cs128 = cs_cat[:, :SSM_GSZ], cs_cat[:, SSM_GSZ:]
    dt64 = cmat(dt, e64, e64t)
    row = lax.broadcasted_iota(jnp.int32, (8, LANE), 0)
    heads = jnp.where(row == 0, dsk, jnp.where(row == 1, jnp.sum(d_a, axis=0, keepdims=True), 0.0))
    heads64 = cmat(heads, e64, e64t)
    d64, tot64 = heads64[0:1, :], heads64[1:2, :]
    xdt = xs * dt64
    cb = bdot(cm, bm, 1, 1)
    li = lax.broadcasted_iota(jnp.int32, (CHUNK, CHUNK), 0)
    si = lax.broadcasted_iota(jnp.int32, (CHUNK, CHUNK), 1)
    causal = li >= si
    lane = lax.broadcasted_iota(jnp.int32, (CHUNK, LANE), 1)
    xw = xdt * jnp.exp(tot64 - cs64)
    ecs = jnp.exp(cs64)
    etot = jnp.exp(tot64)
    ycols, hout = [], []
    for j in range(4):
        sl = slice(j * LANE, (j + 1) * LANE)
        xj = xdt[:, sl]
        ys = []
        for hh in range(2):
            r = 2 * j + hh
            col = cs128[:, r * LANE:(r + 1) * LANE]
            decay = jnp.exp(jnp.where(causal, col - col.T, -1e30))
            ys.append(bdot(cb * decay, xj, 1, 0))
        y_diag = jnp.where(lane < SSM_P, ys[0], ys[1])
        y_off = bdot(cm, hin[j], 1, 0) * ecs[:, sl]
        ycols.append(y_diag + y_off)
        hout.append(etot[:, sl] * hin[j] + bdot(bm, xw[:, sl], 0, 0))
    y = jnp.concatenate(ycols, axis=1) + d64 * xs
    y = y * jax.nn.silu(z)
    yn = y * lax.rsqrt(jnp.mean(y * y, axis=-1, keepdims=True) + RMS_EPS) * nw
    return (yn,) + tuple(hout)


def _xbc_group(a, axis):
    parts = []
    for g in range(SSM_GROUPS):
        for start, width in ((g * SSM_GSZ, SSM_GSZ), (SSM_INNER + g * SSM_N, SSM_N), (SSM_INNER + (SSM_GROUPS + g) * SSM_N, SSM_N)):
            parts.append(lax.slice_in_dim(a, start, start + width, axis=axis))
    return jnp.concatenate(parts, axis=axis)


def _xbc_ungroup(a, axis):
    xs, bs, cs = [], [], []
    for g in range(SSM_GROUPS):
        base = g * SSM_XBC_G
        xs.append(lax.slice_in_dim(a, base, base + SSM_GSZ, axis=axis))
        bs.append(lax.slice_in_dim(a, base + SSM_GSZ, base + SSM_GSZ + SSM_N, axis=axis))
        cs.append(lax.slice_in_dim(a, base + SSM_GSZ + SSM_N, base + SSM_XBC_G, axis=axis))
    return jnp.concatenate(xs + bs + cs, axis=axis)


def _ssd_consts():
    h = np.arange(LANE)[:, None]
    e64 = np.stack([(h == g * 8 + np.arange(SSM_GSZ)[None, :] // SSM_P) for g in range(SSM_GROUPS)]).astype(np.float32)
    e128 = np.stack([(h == g * 8 + np.arange(8 * LANE)[None, :] // LANE) for g in range(SSM_GROUPS)]).astype(np.float32)
    ecat = np.concatenate([e64, e128], axis=2)
    tril = np.tril(np.ones((CHUNK, CHUNK), np.float32))
    return tuple(jnp.asarray(c, dtype=BF) for c in (e64, e64.transpose(0, 2, 1), ecat, ecat.transpose(0, 2, 1), tril, tril.T))


def _ssd_specs(nc, rev):
    def ci(c):
        return nc - 1 - c if rev else c

    def row(width, col):
        return pl.BlockSpec((CHUNK, width), lambda b, c: (b * nc + ci(c), col))

    def whole(shape):
        return pl.BlockSpec(shape, lambda b, c: (0,) * len(shape))

    data = [row(SSM_CONV_DIM, 0),
            row(SSM_GSZ, 1), row(SSM_GSZ, 2), row(LANE, 24)]
    par = [whole((1, LANE))] * 3 + [whole((1, SSM_INNER))]
    cst = [whole((SSM_GROUPS, LANE, SSM_GSZ)), whole((SSM_GROUPS, SSM_GSZ, LANE)), whole((SSM_GROUPS, LANE, 12 * LANE)),
           whole((SSM_GROUPS, 12 * LANE, LANE)), whole((CHUNK, CHUNK)), whole((CHUNK, CHUNK))]
    hsave = pl.BlockSpec((None, None, SSM_GROUPS, 4, SSM_N, LANE), lambda b, c: (b, ci(c), 0, 0, 0, 0))
    return data, par, cst, hsave, row, whole


def _ssd_group_args(g, xbc, z, dtr, dtb, alog, dsk, nw):
    return (xbc[:, g * SSM_XBC_G:(g + 1) * SSM_XBC_G], z[g], dtr, dtb, alog, dsk, nw[:, g * SSM_GSZ:(g + 1) * SSM_GSZ])


def ssd_fwd(xbc_act, u, dtb, alog, dsk, nw, consts, bsz, seq):
    nc = seq // CHUNK
    data, par, cst, hsave, row, _ = _ssd_specs(nc, False)

    def body(xbc, z0, z1, dtr, dtb_r, alog_r, dsk_r, nw_r, e64, e64t, ecat, ecatt, tril, trilt, yn_ref, hs_ref, h):
        @pl.when(pl.program_id(1) == 0)
        def _():
            h[...] = jnp.zeros_like(h)

        hs_ref[...] = h[...]
        ys = []
        for g in range(SSM_GROUPS):
            args = _ssd_group_args(g, xbc[...], (z0[...], z1[...]), dtr[...], dtb_r[...], alog_r[...], dsk_r[...], nw_r[...])
            outs = ssd_chunk(*args, h[g, 0], h[g, 1], h[g, 2], h[g, 3], e64[g], e64t[g], ecat[g], ecatt[g], tril[...], trilt[...])
            ys.append(outs[0])
            for j in range(4):
                h[g, j] = outs[1 + j]
        yn_ref[...] = jnp.concatenate(ys, axis=1).astype(yn_ref.dtype)

    t = bsz * seq
    return pl.pallas_call(
        body, name="ssd_fwd", grid=(bsz, nc), in_specs=data + par + cst, out_specs=[row(SSM_INNER, 0), hsave],
        out_shape=[jax.ShapeDtypeStruct((t, SSM_INNER), BF), jax.ShapeDtypeStruct((bsz, nc, SSM_GROUPS, 4, SSM_N, LANE), F32)],
        scratch_shapes=[pltpu.VMEM((SSM_GROUPS, 4, SSM_N, LANE), F32)], compiler_params=_params(),
    )(xbc_act, u, u, u, dtb, alog, dsk, nw, *consts)


def ssd_bwd(xbc_act, u, dtb, alog, dsk, nw, consts, hs, dmix, bsz, seq):
    nc = seq // CHUNK
    data, par, cst, hsave, row, whole = _ssd_specs(nc, True)
    t = bsz * seq
    pcol = POOL_W // SSM_GSZ

    def body(xbc, z0, z1, dtr, dtb_r, alog_r, dsk_r, nw_r, e64, e64t, ecat, ecatt, tril, trilt, hs_ref, dy0, dy1,
             dxbc, dz, ddt, ddtb, dalog, ddsk, dnw, dh):
        @pl.when(pl.program_id(1) == 0)
        def _():
            dh[...] = jnp.zeros_like(dh)

        per_group = []
        for g, dyn in enumerate((dy0, dy1)):
            cst_vals = (e64[g], e64t[g], ecat[g], ecatt[g], tril[...], trilt[...])
            prim = _ssd_group_args(g, xbc[...], (z0[...], z1[...]), dtr[...], dtb_r[...], alog_r[...], dsk_r[...], nw_r[...])
            prim = prim + (hs_ref[g, 0], hs_ref[g, 1], hs_ref[g, 2], hs_ref[g, 3])
            _, vjp = jax.vjp(lambda *args, c=cst_vals: ssd_chunk(*args, *c), *prim)
            gr = vjp((dyn[...].astype(F32), dh[g, 0], dh[g, 1], dh[g, 2], dh[g, 3]))
            for j in range(4):
                dh[g, j] = gr[7 + j]
            per_group.append(gr)
        g0, g1 = per_group
        dxbc[...] = jnp.concatenate([g0[0], g1[0]], axis=1)
        dz[...] = jnp.concatenate([g0[1], g1[1]], axis=1).astype(dz.dtype)
        ddt[...] = g0[2] + g1[2]

        @pl.when(_first((0, 1)))
        def _():
            for r in (ddtb, dalog, ddsk, dnw):
                r[...] = jnp.zeros_like(r)

        ddtb[...] += g0[3] + g1[3]
        dalog[...] += g0[4] + g1[4]
        ddsk[...] += g0[5] + g1[5]
        dnw[...] += jnp.concatenate([g0[6], g1[6]], axis=1)

    out_specs = [row(SSM_CONV_DIM, 0), row(SSM_INNER, 0), row(LANE, 0), whole((1, LANE)), whole((1, LANE)), whole((1, LANE)),
                 whole((1, SSM_INNER))]
    lane = jax.ShapeDtypeStruct((1, LANE), F32)
    out_shape = [jax.ShapeDtypeStruct((t, SSM_CONV_DIM), F32), jax.ShapeDtypeStruct((t, SSM_INNER), BF),
                 jax.ShapeDtypeStruct((t, LANE), F32), lane, lane, lane, jax.ShapeDtypeStruct((1, SSM_INNER), F32)]
    return pl.pallas_call(
        body, name="ssd_bwd", grid=(bsz, nc), in_specs=data + par + cst + [hsave, row(SSM_GSZ, pcol), row(SSM_GSZ, pcol + 1)],
        out_specs=out_specs, out_shape=out_shape, scratch_shapes=[pltpu.VMEM((SSM_GROUPS, 4, SSM_N, LANE), F32)],
        compiler_params=_params(),
    )(xbc_act, u, u, u, dtb, alog, dsk, nw, *consts, hs, dmix, dmix)


TB = 512


def _rows(d, col=0):
    return pl.BlockSpec((TB, d), lambda i: (i, col))


def _par(d):
    return pl.BlockSpec((1, d), lambda i: (0, 0))


def _sd(shape, dtype=F32):
    return jax.ShapeDtypeStruct(shape, dtype)


def _round_up(n, m):
    return -(-n // m) * m


def _pad_rows(a, rows):
    return jnp.pad(a, ((0, rows - a.shape[0]), (0, 0)))


def _pack128(arrs):
    flat = jnp.concatenate([a.reshape(-1) for a in arrs])
    n = flat.shape[0]
    rows = -(-n // (8 * LANE)) * 8
    return jnp.pad(flat, (0, rows * LANE - n)).reshape(rows, LANE)


def _unpack128(packed, shapes):
    flat = packed.reshape(-1)
    out, off = [], 0
    for s in shapes:
        n = int(np.prod(s))
        out.append(flat[off:off + n].reshape(s))
        off += n
    return out


def kernel(x, mem, norm_gains, xa_wq, xa_wkv, xa_wo, mlp_w1, mlp_w2, ab_w_in, pool_w, pool_scale, ssm_conv_w, ssm_conv_b, ssm_dt_bias, ssm_a_log, ssm_d, ssm_norm, ab_w_out, cd_w_in, conf_dw_w, conf_dw_b, conf_ln_g, conf_ln_b, sc_conv_w, cd_w_out, loss_target, m_norm_gains, m_xa_wq, m_xa_wkv, m_xa_wo, m_mlp_w1, m_mlp_w2, m_ab_w_in, m_pool_w, m_pool_scale, m_ssm_conv_w, m_ssm_conv_b, m_ssm_dt_bias, m_ssm_a_log, m_ssm_d, m_ssm_norm, m_ab_w_out, m_cd_w_in, m_conf_dw_w, m_conf_dw_b, m_conf_ln_g, m_conf_ln_b, m_sc_conv_w, m_cd_w_out, v_norm_gains, v_xa_wq, v_xa_wkv, v_xa_wo, v_mlp_w1, v_mlp_w2, v_ab_w_in, v_pool_w, v_pool_scale, v_ssm_conv_w, v_ssm_conv_b, v_ssm_dt_bias, v_ssm_a_log, v_ssm_d, v_ssm_norm, v_ab_w_out, v_cd_w_in, v_conf_dw_w, v_conf_dw_b, v_conf_ln_g, v_conf_ln_b, v_sc_conv_w, v_cd_w_out):
    args = locals()
    w = {n: args[n] for n in WEIGHTS}
    mom_m = {n: args["m_" + n] for n in WEIGHTS}
    mom_v = {n: args["v_" + n] for n in WEIGHTS}
    ex = Exchange(w)
    loss_local, grad_x, small_grads = local_step(x, mem, loss_target, ex)
    outs = {}

    started = ex.put_small(small_grads, loss_local)
    landed = {key: ex.landed(key, started) for key in ('l1', 'cd', 'l0')}
    late = []
    for n, keys in (('mlp_w1', ('l0', 'l1')), ('mlp_w2', ('l0', 'l1')), ('xa_wkv', ('l0', 'l1')), ('xa_wq', ('l0', 'l1')),
                    ('xa_wo', ('l0', 'l1')), ('cd_w_in', ('cd',)), ('cd_w_out', ('cd',))):
        lands = [landed[key][0] for key in keys]
        offs = [landed[key][1][(n, layer)] for layer, key in enumerate(keys)]
        outs[n] = update_from_slots(lands, offs, w[n], mom_m[n], mom_v[n], SHARD_AXIS[n] == 2, "update_" + n)
        late.append(outs[n][1])
    g_own, loss = ex.reduced_small(late)
    land_ab, offs_ab = ex.landed('ab', late)
    outs['ab_w_out'] = update_from_slots([land_ab], [offs_ab[('ab_w_out', 0)]], w['ab_w_out'], mom_m['ab_w_out'],
                                         mom_v['ab_w_out'], False, "update_ab_w_out")
    res = update_from_slots([land_ab], [offs_ab[('ab_w_in', 0)]], jnp.swapaxes(w['ab_w_in'], 1, 2), jnp.swapaxes(mom_m['ab_w_in'], 1, 2),
                            jnp.swapaxes(mom_v['ab_w_in'], 1, 2), False, "update_ab_w_in")
    outs['ab_w_in'] = tuple(jnp.swapaxes(r, 1, 2) for r in res)
    small = SMALL_SHARDED + REPLICATED
    upd = adamw_many([w[n] for n in small], [mom_m[n] for n in small], [mom_v[n] for n in small], [g_own[n] for n in small],
                     "adamw_small")
    for i, n in enumerate(small):
        outs[n] = (g_own[n], upd[0][i], upd[1][i], upd[2][i])
    return (loss, grad_x.reshape(x.shape), *[outs[n][0] for n in WEIGHTS], *[outs[n][1] for n in WEIGHTS],
            *[outs[n][2] for n in WEIGHTS], *[outs[n][3] for n in WEIGHTS])


G_AB = (('ab_w_in', 0), ('ab_w_out', 0))
G_L0 = (('xa_wq', 0), ('xa_wkv', 0), ('xa_wo', 0), ('mlp_w1', 0), ('mlp_w2', 0))
G_L1 = (('xa_wq', 1), ('xa_wkv', 1), ('xa_wo', 1), ('mlp_w1', 1), ('mlp_w2', 1))
G_CD = (('cd_w_in', 0), ('cd_w_out', 0))
GATHER_CHAIN = {'ab': ('l0', G_L0), 'l0': ('cd', G_CD), 'cd': ('l1', G_L1)}
SHARD_AXIS = dict(BIG)
MEMBER_ROW_TILE = 64
FLAT_ROW_TILE = 128


def _members(group, w):
    out = []
    for n, layer in group:
        shp = w[n].shape[1:]
        if SHARD_AXIS[n] == 2:
            shp = (shp[1], shp[0])
        assert shp[1] == D, (n, shp)
        out.append((n, layer, shp, shp[0], _round_up(shp[0], MEMBER_ROW_TILE)))
    return out


def _group_rows(group, w):
    return _round_up(sum(m[4] for m in _members(group, w)), FLAT_ROW_TILE)


def _flat_shards(group, w):
    parts = []
    for n, layer, _, _, padded in _members(group, w):
        shard = w[n][layer].astype(BF)
        parts.append(_pad_rows(shard.T if SHARD_AXIS[n] == 2 else shard, padded))
    return _pad_rows(jnp.concatenate(parts, axis=0), _group_rows(group, w))


def _full_from_slots(land, group, w):
    out, off = {}, 0
    for n, layer, shp, rows, padded in _members(group, w):
        out[(n, layer)] = land[:, off:off + rows].reshape(N_DEV * rows, D)
        off += padded
    return out


def _slots_from_full(grads, group, w):
    parts = []
    for n, layer, shp, rows, padded in _members(group, w):
        blk = grads[(n, layer)].astype(BF).reshape(N_DEV, rows, D)
        parts.append(jnp.pad(blk, ((0, 0), (0, padded - rows), (0, 0))))
    send = jnp.concatenate(parts, axis=1)
    return jnp.pad(send, ((0, 0), (0, _group_rows(group, w) - send.shape[1]), (0, 0)))


_HBM = pl.BlockSpec(memory_space=pltpu.HBM)
_SEM = pl.BlockSpec(memory_space=pltpu.SEMAPHORE)
_ANY = pl.BlockSpec(memory_space=pl.ANY)


def _peer_copy(k, src, dst, send_sems, recv_sems, peer):
    return pltpu.make_async_remote_copy(src_ref=src, dst_ref=dst, send_sem=send_sems.at[k], recv_sem=recv_sems.at[k],
                                        device_id=peer, device_id_type=pl.DeviceIdType.MESH)


def exchange_start(src, name, scatter):
    shape = src.shape[-2:]

    def body(src_ref, land_ref, send_sems, recv_sems, src_thru, land_thru, token):
        me = _me()
        for k, f in enumerate(_FLIPS):
            peer = _flip(me, f)
            piece = src_ref.at[_slot(peer)] if scatter else src_ref
            _peer_copy(k, piece, land_ref.at[_slot(me)], send_sems, recv_sems, peer).start()
        token[...] = jnp.zeros_like(token)

    land = pltpu.with_memory_space_constraint(lax.empty((N_DEV,) + shape, src.dtype), pltpu.HBM)
    return pl.pallas_call(
        body, name=name,
        out_shape=(pltpu.SemaphoreType.DMA((7,)), pltpu.SemaphoreType.DMA((7,)), pltpu.HBM(src.shape, src.dtype),
                   pltpu.HBM((N_DEV,) + shape, src.dtype), jax.ShapeDtypeStruct((8, LANE), F32)),
        in_specs=(_HBM, _HBM), out_specs=(_SEM, _SEM, _HBM, _HBM, pl.BlockSpec(memory_space=pltpu.VMEM)),
        input_output_aliases={0: 2, 1: 3},
        compiler_params=pltpu.CompilerParams(has_side_effects=pltpu.SideEffectType.DATAFLOW_SIDE_EFFECTING),
    )(pltpu.with_memory_space_constraint(src, pltpu.HBM), land)


def exchange_wait(handles, after, name, scatter):
    send_sems, recv_sems, src_thru, land_thru, _ = handles
    after = list(after) if isinstance(after, (list, tuple)) else [after]

    def body(src_ref, land_ref, send_sems, recv_sems, *rest):
        token = rest[-1]
        me = _me()
        for k, f in enumerate(_FLIPS):
            peer = _flip(me, f)
            piece = src_ref.at[_slot(peer)] if scatter else src_ref
            cp = _peer_copy(k, piece, land_ref.at[_slot(peer)], send_sems, recv_sems, peer)
            cp.wait_send()
            cp.wait_recv()
        token[...] = jnp.zeros_like(token)

    return pl.pallas_call(
        body, name=name, out_shape=(pltpu.HBM(src_thru.shape, src_thru.dtype), pltpu.HBM(land_thru.shape, land_thru.dtype),
                                    jax.ShapeDtypeStruct((8, LANE), F32)),
        in_specs=(_HBM, _HBM, _SEM, _SEM) + (_ANY,) * len(after), out_specs=(_HBM, _HBM, pl.BlockSpec(memory_space=pltpu.VMEM)),
        input_output_aliases={0: 0, 1: 1},
        compiler_params=pltpu.CompilerParams(has_side_effects=pltpu.SideEffectType.DATAFLOW_SIDE_EFFECTING),
    )(src_thru, land_thru, send_sems, recv_sems, *after)


class Exchange:
    def __init__(self, w):
        self.w = w
        self.me = _slot(_me())
        self.gathers = {'ab': (G_AB, exchange_start(_flat_shards(G_AB, w), "gather_ab_start", False))}
        self.tokens = [self.gathers['ab'][1][4]]
        self.reductions = {}
        shapes = [w[n].shape for n in SMALL_SHARDED]
        gs = all_gather(_pack128([w[n] for n in SMALL_SHARDED]), "gather_small")
        per_dev = [_unpack128(gs[d], shapes) for d in range(N_DEV)]
        self.small = {n: jnp.concatenate([per_dev[d][i] for d in range(N_DEV)], axis=-1) for i, n in enumerate(SMALL_SHARDED)}
        self.small.update({n: w[n] for n in REPLICATED})

    def take_tokens(self):
        toks, self.tokens = self.tokens, []
        return toks

    def weights(self, key, after):
        group, handles = self.gathers[key]
        _, land, done = exchange_wait(handles, after, f"gather_{key}_wait", False)
        nxt = GATHER_CHAIN.get(key)
        if nxt is not None:
            src = _flat_shards(nxt[1], self.w) + done[0, 0].astype(BF)
            self.gathers[nxt[0]] = (nxt[1], exchange_start(src, f"gather_{nxt[0]}_start", False))
            self.tokens.append(self.gathers[nxt[0]][1][4])
        land = lax.dynamic_update_slice(land, handles[2][None], (self.me, 0, 0))
        return _full_from_slots(land, group, self.w)

    def put_grads(self, key, group, grads):
        send = _slots_from_full(grads, group, self.w)
        handles = exchange_start(send, f"reduce_{key}_start", True)
        self.reductions[key] = (group, handles)
        self.tokens.append(handles[4])

    def landed(self, key, after):
        group, handles = self.reductions[key]
        send, land, _ = exchange_wait(handles, after, f"reduce_{key}_wait", True)
        mine = lax.dynamic_slice_in_dim(send, self.me, 1, axis=0)
        land = lax.dynamic_update_slice(land, mine, (self.me, 0, 0))
        offs, off = {}, 0
        for n, layer, _, _, padded in _members(group, self.w):
            offs[(n, layer)] = off
            off += padded
        return land, offs

    def put_small(self, small_grads, loss_local):
        small = SMALL_SHARDED + REPLICATED
        self.small_shapes = [small_grads[n].shape for n in small] + [(1,)]
        packed = _pack128([small_grads[n] for n in small] + [loss_local.reshape(1)])
        self.small_handles = exchange_start(packed, "gather_small_grads_start", False)
        return self.small_handles[4]

    def reduced_small(self, after):
        small = SMALL_SHARDED + REPLICATED
        src, land, _ = exchange_wait(self.small_handles, after, "gather_small_grads_wait", False)
        gs = lax.dynamic_update_slice(land, src[None], (self.me, 0, 0))
        tot = _unpack128(sum_slots(gs, "sum_small", 1024), self.small_shapes)
        out = {}
        for n, g in zip(small, tot):
            if n in SMALL_SHARDED:
                width = self.w[n].shape[-1]
                g = lax.dynamic_slice_in_dim(g, self.me * width, width, axis=g.ndim - 1)
            out[n] = g
        return out, tot[-1].reshape(())


def local_step(x, mem, target, ex):
    bsz, seq, _ = x.shape
    t = bsz * seq
    nb = t // TB
    nc = seq // CHUNK
    x0 = x.reshape(t, D)
    mem2 = mem.reshape(bsz * N_MEM, D)
    tgt = target.reshape(t, D)
    p = ex.small
    gains = p['norm_gains']
    big = {}

    def gain(layer, i):
        g = gains[layer, i].reshape(1, D)
        for tok in ex.take_tokens():
            g = g + tok[0, 0]
        return g

    consts = _ssd_consts()
    grads = {}
    saved = [dict(), dict()]

    def run_seg_res(xin, m, ga, gb, name):
        return fwd_call(seg_res, name, (nb,), [xin, m, ga, gb], [_rows(D), _rows(D), _par(D), _par(D)],
                        [_sd((t, D)), _sd((t, D), BF)], [_rows(D), _rows(D)])

    def attn_specs():
        nq = seq // TB
        q = pl.BlockSpec((TB, D), lambda b, i: (b * nq + i, 0))
        kv = pl.BlockSpec((N_MEM, 2 * D), lambda b, i: (b, 0))
        return (bsz, nq), q, kv

    def attention_fwd(layer, xin, hin, sv):
        q = matmul(hin, big[('xa_wq', layer)], 'nn', f"q_{layer}", BF)
        kv = matmul(mem2, big[('xa_wkv', layer)], 'nt', f"kv_{layer}", BF)
        grid, qs, kvs = attn_specs()
        o, = fwd_call(attn_fn, f"attn_{layer}", grid, [q, kv], [qs, kvs], [_sd((t, D), BF)], [qs])
        ao = matmul(o, big[('xa_wo', layer)], 'nn', f"ao_{layer}")
        sv.update(q=q, kv=kv, o=o, ao=ao)
        return ao

    def mlp_fwd(layer, hin, sv):
        r, rr = matmul(hin, big[('mlp_w1', layer)], 'nt', f"mlp1_{layer}", (BF, BF), epilogue=act_epilogue)
        mo = matmul(rr, big[('mlp_w2', layer)], 'nn', f"mlp2_{layer}")
        sv.update(r=r, rr=rr, mo=mo)
        return mo

    sv = saved[0]
    h0, = fwd_call(seg_in, "norm_in", (nb,), [x0, gain(0, 0)], [_rows(D), _par(D)], [_sd((t, D), BF)], [_rows(D)])
    big.update(ex.weights('ab', h0))
    xbc0 = POOL_W + SSM_INNER
    w_ab_in = big[('ab_w_in', 0)]
    w_ab_in = _pad_rows(jnp.concatenate([w_ab_in[:xbc0], _xbc_group(w_ab_in[xbc0:xbc0 + SSM_CONV_DIM], 0),
                                         w_ab_in[xbc0 + SSM_CONV_DIM:]], axis=0), AB_IN_PAD)
    conv_w, conv_b = _xbc_group(p['ssm_conv_w'][0], 1), _xbc_group(p['ssm_conv_b'], 1)
    u0 = matmul(h0, w_ab_in, 'nt', "ab_in")
    pool_outs = []
    for g in range(POOL_GROUPS):
        seqspec = pl.BlockSpec((seq, PG), lambda b, g=g: (b, g))
        po, = fwd_call(make_pool_fn(g), f"pool_{g}", (bsz,), [u0, p['pool_w'][0, g], p['pool_scale']],
                       [seqspec, pl.BlockSpec((PG, PG), lambda b: (0, 0)), pl.BlockSpec((1, PG), lambda b, g=g: (0, g))],
                       [_sd((t, PG), BF)], [pl.BlockSpec((seq, PG), lambda b: (b, 0))])
        pool_outs.append(po)
    cw = 256
    ncb = SSM_CONV_DIM // cw
    cbase = (POOL_W + SSM_INNER) // cw
    conv_in_specs = [pl.BlockSpec((seq, cw), lambda j, b: (b, cbase + j)), pl.BlockSpec((SSM_CONV, cw), lambda j, b: (0, j)),
                     pl.BlockSpec((1, cw), lambda j, b: (0, j))]
    conv_out_spec = pl.BlockSpec((seq, cw), lambda j, b: (b, j))
    xbc_act, = fwd_call(conv4_fn, "ssm_conv", (ncb, bsz), [u0, conv_w, conv_b], conv_in_specs,
                        [_sd((t, SSM_CONV_DIM))], [conv_out_spec])
    dtb = jnp.pad(p['ssm_dt_bias'], ((0, 0), (0, LANE - SSM_HEADS)))
    alog = jnp.pad(p['ssm_a_log'], ((0, 0), (0, LANE - SSM_HEADS)))
    dsk = jnp.pad(p['ssm_d'], ((0, 0), (0, LANE - SSM_HEADS)))
    yn, hs = ssd_fwd(xbc_act, u0, dtb, alog, dsk, p['ssm_norm'], consts, bsz, seq)
    mix0 = jnp.concatenate(pool_outs + [yn], axis=1)
    m0 = matmul(mix0, big[('ab_w_out', 0)], 'nn', "ab_out")
    x1, h2 = run_seg_res(x0, m0, gain(0, 1), gain(0, 2), "res_0a")
    big.update(ex.weights('l0', h2))
    ao0 = attention_fwd(0, x1, h2, sv)
    x2, h3 = run_seg_res(x1, ao0, gain(0, 3), gain(0, 4), "res_0b")
    mo0 = mlp_fwd(0, h3, sv)
    big.update(ex.weights('cd', mo0))
    x3, h4 = run_seg_res(x2, mo0, gain(0, 5), gain(1, 0), "res_0c")

    sv1 = saved[1]
    nd = D // LANE
    w_cd_in = big[('cd_w_in', 0)].reshape(5, nd, LANE, D).transpose(1, 0, 2, 3).reshape(CD_IN, D)
    u1 = matmul(h4, w_cd_in, 'nt', "cd_in")
    cd_par = [pl.BlockSpec((CONF_K, LANE), lambda j, b: (0, j)), pl.BlockSpec((1, LANE), lambda j, b: (0, j)),
              pl.BlockSpec((SC_K, LANE), lambda j, b: (0, j))]
    cd_ins = [u1, p['conf_dw_w'][0], p['conf_dw_b'], p['sc_conv_w'][0]]
    cd_u_spec = pl.BlockSpec((seq, 5 * LANE), lambda j, b: (b, j))
    cd_in_specs = [cd_u_spec] + cd_par
    cd_out_spec = pl.BlockSpec((seq, LANE), lambda j, b: (b, j))
    vconv, mix1 = fwd_call(cd1_fn, "cd_conv", (nd, bsz), cd_ins, cd_in_specs, [_sd((t, D)), _sd((t, CD_OUT), BF)],
                           [cd_out_spec, pl.BlockSpec((seq, LANE), lambda j, b: (b, nd + j))])
    mix1, = fwd_call(seg_ln, "conf_ln", (nb,), [vconv, p['conf_ln_g'], p['conf_ln_b']], [_rows(D), _par(D), _par(D)],
                     [_sd((t, CD_OUT), BF)], [_rows(D)], into=mix1)
    m1 = matmul(mix1, big[('cd_w_out', 0)], 'nn', "cd_out")
    x4, h5 = run_seg_res(x3, m1, gain(1, 1), gain(1, 2), "res_1a")
    big.update(ex.weights('l1', h5))
    ao1 = attention_fwd(1, x4, h5, sv1)
    x5, h6 = run_seg_res(x4, ao1, gain(1, 3), gain(1, 4), "res_1b")
    mo1 = mlp_fwd(1, h6, sv1)

    def loss_body(x_ref, m_ref, g_ref, t_ref, dx_ref, dm_ref, dg_ref, acc_ref):
        (y,), vjp = jax.vjp(seg_out, x_ref[...], m_ref[...], g_ref[...])
        d = y - t_ref[...]
        dx, dm, dg = vjp((d / float(D),))
        dx_ref[...] = dx
        dm_ref[...] = dm.astype(dm_ref.dtype)

        @pl.when(pl.program_id(0) == 0)
        def _():
            acc_ref[...] = jnp.zeros_like(acc_ref)
            dg_ref[...] = jnp.zeros_like(dg_ref)

        acc_ref[...] += jnp.sum(d * d, axis=0, keepdims=True)
        dg_ref[...] += dg

    dx5, dmo1, dg15, lanes = pl.pallas_call(
        loss_body, name="loss_head", grid=(nb,), in_specs=[_rows(D), _rows(D), _par(D), _rows(D)],
        out_specs=[_rows(D), _rows(D), _par(D), _par(D)], out_shape=[_sd((t, D)), _sd((t, D), BF), _sd((1, D)), _sd((1, D))],
        compiler_params=_params())(x5, mo1, gain(1, 5), tgt)
    loss = 0.5 * jnp.sum(lanes) / float(D)

    gain_grads = {(1, 5): dg15}

    def bwd_seg_res(xin, m, ga, gb, dx1, dh, name):
        return bwd_call(seg_res, name, (nb,), [xin, m, ga, gb], [_rows(D), _rows(D), _par(D), _par(D)], [dx1, dh],
                        [_rows(D), _rows(D)], [0, 1, 2, 3], [_sd((t, D)), _sd((t, D), BF), _sd((1, D)), _sd((1, D))],
                        [_rows(D), _rows(D), _par(D), _par(D)], [None, None, (0,), (0,)])

    def mlp_bwd(layer, hin, dmo, sv):
        grads_w2 = matmul(sv['rr'], dmo, 'tn', f"d_mlp_w2_{layer}", BF)
        dr, = matmul(dmo, big[('mlp_w2', layer)], 'nt', f"d_r_{layer}", (BF,), epilogue=act_bwd_epilogue, extras=[sv['r']])
        grads_w1 = matmul(dr, hin, 'tn', f"d_mlp_w1_{layer}", BF)
        dh = matmul(dr, big[('mlp_w1', layer)], 'nn', f"d_h_mlp_{layer}")
        return dh, grads_w1, grads_w2

    def attention_bwd(layer, hin, dao, sv):
        g_wo = matmul(sv['o'], dao, 'tn', f"d_xa_wo_{layer}", BF)
        do = matmul(dao, big[('xa_wo', layer)], 'nt', f"d_o_{layer}", BF)
        grid, qs, kvs = attn_specs()
        dq, dkv = bwd_call(attn_fn, f"d_attn_{layer}", grid, [sv['q'], sv['kv']], [qs, kvs], [do], [qs], [0, 1],
                           [_sd((t, D), BF), _sd((bsz * N_MEM, 2 * D))], [qs, kvs], [None, (1,)])
        g_wkv = matmul(dkv, mem2, 'tn', f"d_xa_wkv_{layer}", BF)
        g_wq = matmul(hin, dq, 'tn', f"d_xa_wq_{layer}", BF)
        dh = matmul(dq, big[('xa_wq', layer)], 'nt', f"d_h_attn_{layer}")
        return dh, g_wq, g_wkv, g_wo

    per_layer = {k: [None, None] for k in ('xa_wq', 'xa_wkv', 'xa_wo', 'mlp_w1', 'mlp_w2')}

    dh6, per_layer['mlp_w1'][1], per_layer['mlp_w2'][1] = mlp_bwd(1, h6, dmo1, sv1)
    dx4, dao1, gain_grads[(1, 3)], gain_grads[(1, 4)] = bwd_seg_res(x4, ao1, gain(1, 3), gain(1, 4), dx5, dh6, "d_res_1b")
    dh5, per_layer['xa_wq'][1], per_layer['xa_wkv'][1], per_layer['xa_wo'][1] = attention_bwd(1, h5, dao1, sv1)
    ex.put_grads('l1', G_L1, {(k, 1): v[1] for k, v in per_layer.items()})
    dx3, dm1, gain_grads[(1, 1)], gain_grads[(1, 2)] = bwd_seg_res(x3, m1, gain(1, 1), gain(1, 2), dx4, dh5, "d_res_1a")
    g_cd_out = matmul(mix1, dm1, 'tn', "d_cd_w_out", BF)
    dmix1 = matmul(dm1, big[('cd_w_out', 0)], 'nt', "d_mix1")
    dvconv, dlg, dlb = bwd_call(seg_ln, "d_conf_ln", (nb,), [vconv, p['conf_ln_g'], p['conf_ln_b']],
                                [_rows(D), _par(D), _par(D)], [dmix1], [_rows(D, 0)], [0, 1, 2],
                                [_sd((t, D)), _sd((1, D)), _sd((1, D))], [_rows(D), _par(D), _par(D)], [None, (0,), (0,)])
    grads['conf_ln_g'], grads['conf_ln_b'] = dlg, dlb
    cd_g = bwd_call(cd1_fn, "d_cd_conv", (nd, bsz), cd_ins, cd_in_specs, [dvconv, dmix1],
                    [cd_out_spec, pl.BlockSpec((seq, LANE), lambda j, b: (b, nd + j))], list(range(4)),
                    [_sd((t, CD_IN), BF), _sd((CONF_K, D)), _sd((1, D)), _sd((SC_K, D))], [cd_u_spec] + cd_par,
                    [None, (1,), (1,), (1,)])
    du1 = cd_g[0]
    grads['conf_dw_w'], grads['conf_dw_b'], grads['sc_conv_w'] = cd_g[1][None], cd_g[2], cd_g[3][None]
    g_cd_in = matmul(du1, h4, 'tn', "d_cd_w_in", BF).reshape(nd, 5, LANE, D).transpose(1, 0, 2, 3).reshape(CD_IN, D)
    ex.put_grads('cd', G_CD, {('cd_w_in', 0): g_cd_in, ('cd_w_out', 0): g_cd_out})
    dh4 = matmul(du1, w_cd_in, 'nn', "d_h_cd")

    dx2, dmo0, gain_grads[(0, 5)], gain_grads[(1, 0)] = bwd_seg_res(x2, mo0, gain(0, 5), gain(1, 0), dx3, dh4, "d_res_0c")
    dh3, per_layer['mlp_w1'][0], per_layer['mlp_w2'][0] = mlp_bwd(0, h3, dmo0, sv)
    dx1, dao0, gain_grads[(0, 3)], gain_grads[(0, 4)] = bwd_seg_res(x1, ao0, gain(0, 3), gain(0, 4), dx2, dh3, "d_res_0b")
    dh2, per_layer['xa_wq'][0], per_layer['xa_wkv'][0], per_layer['xa_wo'][0] = attention_bwd(0, h2, dao0, sv)
    ex.put_grads('l0', G_L0, {(k, 0): v[0] for k, v in per_layer.items()})
    dx0r, dm0, gain_grads[(0, 1)], gain_grads[(0, 2)] = bwd_seg_res(x0, m0, gain(0, 1), gain(0, 2), dx1, dh2, "d_res_0a")
    g_ab_out = matmul(mix0, dm0, 'tn', "d_ab_w_out", BF)
    dmix0 = matmul(dm0, big[('ab_w_out', 0)], 'nt', "d_mix0")
    dxbc_act, dz, ddt, ddtb, dalog, ddsk, dnw = ssd_bwd(xbc_act, u0, dtb, alog, dsk, p['ssm_norm'], consts, hs, dmix0, bsz, seq)
    grads['ssm_dt_bias'] = ddtb[:, :SSM_HEADS]
    grads['ssm_a_log'] = dalog[:, :SSM_HEADS]
    grads['ssm_d'] = ddsk[:, :SSM_HEADS]
    grads['ssm_norm'] = dnw
    dxr, dcw, dcb = bwd_call(conv4_fn, "d_ssm_conv", (ncb, bsz), [u0, conv_w, conv_b], conv_in_specs,
                             [dxbc_act], [conv_out_spec], [0, 1, 2],
                             [_sd((t, SSM_CONV_DIM), BF), _sd((SSM_CONV, SSM_CONV_DIM)), _sd((1, SSM_CONV_DIM))],
                             [conv_out_spec, conv_in_specs[1], conv_in_specs[2]], [None, (1,), (1,)])
    grads['ssm_conv_w'], grads['ssm_conv_b'] = _xbc_ungroup(dcw, 1)[None], _xbc_ungroup(dcb, 1)
    dpool, dpw, dps = [], [], []
    for g in range(POOL_GROUPS):
        seqspec = pl.BlockSpec((seq, PG), lambda b, g=g: (b, g))
        one = pl.BlockSpec((seq, PG), lambda b: (b, 0))
        wspec = pl.BlockSpec((PG, PG), lambda b: (0, 0))
        sspec = pl.BlockSpec((1, PG), lambda b, g=g: (0, g))
        a, bb, c = bwd_call(make_pool_fn(g), f"d_pool_{g}", (bsz,), [u0, p['pool_w'][0, g], p['pool_scale']],
                            [seqspec, wspec, sspec], [dmix0], [seqspec], [0, 1, 2],
                            [_sd((t, PG), BF), _sd((PG, PG)), _sd((1, PG))], [one, wspec, pl.BlockSpec((1, PG), lambda b: (0, 0))],
                            [None, (0,), (0,)])
        dpool.append(a)
        dpw.append(bb)
        dps.append(c)
    grads['pool_w'] = jnp.stack(dpw)[None]
    grads['pool_scale'] = jnp.concatenate(dps, axis=1)
    du0 = jnp.concatenate(dpool + [dz, dxr, ddt.astype(BF)], axis=1)
    g_ab_in = matmul(du0, h0, 'tn', "d_ab_w_in", BF)
    g_ab_in = jnp.concatenate([g_ab_in[:xbc0], _xbc_ungroup(g_ab_in[xbc0:xbc0 + SSM_CONV_DIM], 0),
                               g_ab_in[xbc0 + SSM_CONV_DIM:AB_IN]], axis=0)
    ex.put_grads('ab', G_AB, {('ab_w_in', 0): g_ab_in, ('ab_w_out', 0): g_ab_out})
    dh0 = matmul(du0, w_ab_in, 'nn', "d_h_ab", after=ex.take_tokens())
    dx, dg00 = bwd_call(seg_in_res, "d_norm_in", (nb,), [x0, gain(0, 0)], [_rows(D), _par(D)], [dx0r, dh0],
                        [_rows(D), _rows(D)], [0, 1], [_sd((t, D)), _sd((1, D))], [_rows(D), _par(D)], [None, (0,)])
    gain_grads[(0, 0)] = dg00
    grads['norm_gains'] = jnp.stack([jnp.concatenate([gain_grads[(l, i)] for i in range(6)], axis=0) for l in range(2)])
    return loss, dx, grads
```

```python
import functools
import math

import numpy as np
import jax
import jax.numpy as jnp
from jax import lax
from jax.experimental import pallas as pl
from jax.experimental.pallas import tpu as pltpu

BF = jnp.bfloat16
F32 = jnp.float32
HI = lax.Precision.HIGHEST

N_DEV = 8
D = 1024
N_MEM = 256
XA_HEADS = 4
XA_DH = D // XA_HEADS
POOL_GROUPS = 4
PG = 128
POOL_W = POOL_GROUPS * PG
SSM_INNER = 1024
SSM_GROUPS = 2
SSM_GSZ = SSM_INNER // SSM_GROUPS
SSM_HEADS = 16
SSM_P = 64
SSM_N = 128
SSM_CONV = 4
SSM_CONV_DIM = SSM_INNER + 2 * SSM_GROUPS * SSM_N
SSM_XBC_G = SSM_GSZ + 2 * SSM_N
CHUNK = 128
AB_IN = POOL_W + SSM_INNER + SSM_CONV_DIM + SSM_HEADS
AB_IN_PAD = POOL_W + SSM_INNER + SSM_CONV_DIM + 128
AB_OUT = POOL_W + SSM_INNER
CONF_K = 31
SC_K = 3
CD_IN = 5 * D
CD_OUT = 2 * D
MLP_H = 4 * D
RMS_EPS = 1e-6
LN_EPS = 1e-5
ADAM_LR = 0.001
ADAM_B1 = 0.9
ADAM_B2 = 0.999
ADAM_EPS = 1e-08
ADAM_WD = 0.01
ADAM_STEP = 10
VMEM_LIMIT = 56 * 1024 * 1024
LANE = 128

NAMES = ['x', 'mem', 'norm_gains', 'xa_wq', 'xa_wkv', 'xa_wo', 'mlp_w1', 'mlp_w2', 'ab_w_in', 'pool_w', 'pool_scale',
         'ssm_conv_w', 'ssm_conv_b', 'ssm_dt_bias', 'ssm_a_log', 'ssm_d', 'ssm_norm', 'ab_w_out', 'cd_w_in', 'conf_dw_w',
         'conf_dw_b', 'conf_ln_g', 'conf_ln_b', 'sc_conv_w', 'cd_w_out', 'loss_target']
WEIGHTS = NAMES[2:25]
BIG = [('xa_wq', 1), ('xa_wkv', 2), ('xa_wo', 1), ('mlp_w1', 2), ('mlp_w2', 1), ('cd_w_in', 2), ('cd_w_out', 1),
       ('ab_w_out', 1), ('ab_w_in', 2)]
SMALL_SHARDED = ['norm_gains', 'ssm_conv_w', 'conf_dw_w', 'conf_dw_b', 'conf_ln_g', 'conf_ln_b', 'sc_conv_w']
REPLICATED = ['pool_w', 'pool_scale', 'ssm_conv_b', 'ssm_dt_bias', 'ssm_a_log', 'ssm_d', 'ssm_norm']


def _dg(a, b, ca, cb, prec=None):
    return lax.dot_general(a, b, (((ca,), (cb,)), ((), ())), precision=prec, preferred_element_type=F32)


@functools.partial(jax.custom_vjp, nondiff_argnums=(2, 3))
def bdot(a, b, ca, cb):
    return _dg(a.astype(BF), b.astype(BF), ca, cb)


def _bdot_fwd(a, b, ca, cb):
    return bdot(a, b, ca, cb), (a, b)


def _bdot_bwd(ca, cb, res, g):
    a, b = res
    g16, a16, b16 = g.astype(BF), a.astype(BF), b.astype(BF)
    da = _dg(g16, b16, 1, 1 - cb) if ca == 1 else _dg(b16, g16, 1 - cb, 1)
    db = _dg(g16, a16, 0, 1 - ca) if cb == 1 else _dg(a16, g16, 1 - ca, 0)
    return da.astype(a.dtype), db.astype(b.dtype)


bdot.defvjp(_bdot_fwd, _bdot_bwd)


def _split3(a):
    a1 = a.astype(BF)
    r1 = a - a1.astype(F32)
    a2 = r1.astype(BF)
    a3 = (r1 - a2.astype(F32)).astype(BF)
    return a1, a2, a3


def _exact_right(a, c):
    m = a.shape[0]
    if m % 16:
        return sum(_dg(p, c, 1, 0) for p in _split3(a))
    o = _dg(jnp.concatenate(_split3(a), axis=0), c, 1, 0)
    return o[:m] + o[m:2 * m] + o[2 * m:]


def _exact_left(c, a):
    n = a.shape[1]
    o = _dg(c, jnp.concatenate(_split3(a), axis=1), 1, 0)
    return o[:, :n] + o[:, n:2 * n] + o[:, 2 * n:]


@jax.custom_vjp
def cmat(a, c, ct):
    return _exact_right(a, c)


def _cmat_fwd(a, c, ct):
    return cmat(a, c, ct), (c, ct)


def _cmat_bwd(res, g):
    c, ct = res
    return _exact_right(g, ct), jnp.zeros_like(c), jnp.zeros_like(ct)


cmat.defvjp(_cmat_fwd, _cmat_bwd)


@jax.custom_vjp
def cmatl(c, ct, a):
    return _exact_left(c, a)


def _cmatl_fwd(c, ct, a):
    return cmatl(c, ct, a), (c, ct)


def _cmatl_bwd(res, g):
    c, ct = res
    return jnp.zeros_like(c), jnp.zeros_like(ct), _exact_left(ct, g)


cmatl.defvjp(_cmatl_fwd, _cmatl_bwd)


SUBLANES = 8


def _taps(x, shifts, down):
    n, c = x.shape
    pad = _round_up(max(shifts), SUBLANES)
    if pad == 0:
        return {0: x}
    zeros = jnp.zeros((pad, c), x.dtype)
    xp = jnp.concatenate([zeros, x] if down else [x, zeros], axis=0)
    rolled, out = {0: xp}, {}
    for s in shifts:
        a, b = divmod(s, SUBLANES)
        if b not in rolled:
            rolled[b] = pltpu.roll(xp, b if down else n + pad - b, 0)
        off = pad - SUBLANES * a if down else SUBLANES * a
        out[s] = rolled[b][off:off + n]
    return out


def _shift_down(x, k):
    return _taps(x, [k], True)[k]


def _shift_up(x, k):
    return _taps(x, [k], False)[k]


@functools.partial(jax.custom_vjp, nondiff_argnums=(1,))
def shift(x, k):
    return _shift_down(x, k)


def _shift_fwd(x, k):
    return _shift_down(x, k), None


def _shift_bwd(k, _, g):
    return (_shift_up(g, k),)


shift.defvjp(_shift_fwd, _shift_bwd)


@functools.partial(jax.custom_vjp, nondiff_argnums=(2,))
def cconv(u, w, width):
    taps = _taps(u, list(range(width)), True)
    acc = u * w[width - 1:width, :]
    for k in range(width - 1):
        acc = acc + taps[width - 1 - k] * w[k:k + 1, :]
    return acc


def _cconv_fwd(u, w, width):
    return cconv(u, w, width), (u, w)


def _cconv_bwd(width, res, g):
    u, w = res
    rows = lax.broadcasted_iota(jnp.int32, w.shape, 0)
    du = g * w[width - 1:width, :]
    dw = jnp.where(rows == width - 1, jnp.sum(g * u, axis=0, keepdims=True), 0.0)
    g_taps = _taps(g, list(range(width)), False)
    u_taps = _taps(u, list(range(width)), True)
    for k in range(width - 1):
        s = width - 1 - k
        du = du + g_taps[s] * w[k:k + 1, :]
        dw = dw + jnp.where(rows == k, jnp.sum(g * u_taps[s], axis=0, keepdims=True), 0.0)
    return du, dw


cconv.defvjp(_cconv_fwd, _cconv_bwd)


def _rms(x, g):
    return x * lax.rsqrt(jnp.mean(x * x, axis=-1, keepdims=True) + RMS_EPS) * g


def _params(sem=None):
    return pltpu.CompilerParams(dimension_semantics=sem, vmem_limit_bytes=VMEM_LIMIT)


def _f32(v):
    return v if v.dtype == F32 else v.astype(F32)


def _first(axes):
    ok = None
    for ax in axes:
        c = pl.program_id(ax) == 0
        ok = c if ok is None else jnp.logical_and(ok, c)
    return ok


def fwd_call(fn, name, grid, ins, in_specs, out_shapes, out_specs, into=None):
    n_in = len(ins)
    n_into = 0 if into is None else 1

    def body(*refs):
        outs = fn(*[_f32(r[...]) for r in refs[:n_in]])
        for r, o in zip(refs[n_in + n_into:], outs):
            r[...] = o.astype(r.dtype)

    extra = [] if into is None else [into]
    return pl.pallas_call(body, name=name, grid=grid, in_specs=list(in_specs) + [pl.BlockSpec(memory_space=pl.ANY)] * n_into,
                          out_specs=out_specs, out_shape=out_shapes, input_output_aliases={n_in: 0} if n_into else {},
                          compiler_params=_params())(*ins, *extra)


def bwd_call(fn, name, grid, ins, in_specs, cots, cot_specs, gidx, g_shapes, g_specs, g_acc):
    n_in, n_cot = len(ins), len(cots)

    def body(*refs):
        vals = [_f32(r[...]) for r in refs[:n_in]]

        def f_sel(*dv):
            full = list(vals)
            for i, v in zip(gidx, dv):
                full[i] = v
            return tuple(fn(*full))

        outs, vjp = jax.vjp(f_sel, *[vals[i] for i in gidx])
        cts = tuple(_f32(r[...]) for r in refs[n_in:n_in + n_cot])
        grads = vjp(cts)
        for r, g, acc in zip(refs[n_in + n_cot:], grads, g_acc):
            if acc is None:
                r[...] = g.astype(r.dtype)
            else:
                @pl.when(_first(acc))
                def _():
                    r[...] = jnp.zeros_like(r)

                r[...] += g.astype(r.dtype)

    return pl.pallas_call(body, name=name, grid=grid, in_specs=list(in_specs) + list(cot_specs), out_specs=g_specs,
                          out_shape=g_shapes, compiler_params=_params())(*ins, *cots)


def _tile(dim, pref):
    if dim <= pref:
        return dim
    best = None
    for t in range(LANE, pref + 1, LANE):
        if dim % t == 0:
            best = t
    assert best is not None, dim
    return best


MATMUL_VMEM_BUDGET = 40 * 1024 * 1024


def _matmul_tiles(m, n, k, a_bytes, b_bytes, out_bytes):
    tn = _tile(n, 1024)
    for tk_pref in (k, 2048, 1024, 512):
        tk = _tile(k, tk_pref)
        for tm_pref in (1024, 512, 256):
            tm = _tile(m, tm_pref)
            need = 2 * (tm * tk * a_bytes + tk * tn * b_bytes + tm * tn * out_bytes) + (0 if tk == k else tm * tn * 4)
            need += (tm * tk * 2 if a_bytes == 4 else 0) + (tk * tn * 2 if b_bytes == 4 else 0)
            if need <= MATMUL_VMEM_BUDGET:
                return tm, tn, tk
    raise ValueError((m, n, k))


def matmul(a, b, mode, name, out_dtype=F32, epilogue=None, extras=(), after=()):
    if mode == 'nn':
        (m, k), (k2, n) = a.shape, b.shape
    elif mode == 'nt':
        (m, k), (n, k2) = a.shape, b.shape
    else:
        (k, m), (k2, n) = a.shape, b.shape
    assert k == k2, (name, a.shape, b.shape)
    n_extra = len(extras)
    out_dtypes = out_dtype if isinstance(out_dtype, tuple) else (out_dtype,)
    per_out = sum(jnp.dtype(dt).itemsize for dt in out_dtypes) + sum(e.dtype.itemsize for e in extras)
    tm, tn, tk = _matmul_tiles(m, n, k, a.dtype.itemsize, b.dtype.itemsize, per_out)
    nk = k // tk
    ca = 0 if mode == 'tn' else 1
    cb = 1 if mode == 'nt' else 0
    a_spec = pl.BlockSpec((tk, tm), lambda i, j, kk: (kk, i)) if mode == 'tn' else pl.BlockSpec((tm, tk), lambda i, j, kk: (i, kk))
    b_spec = pl.BlockSpec((tn, tk), lambda i, j, kk: (j, kk)) if mode == 'nt' else pl.BlockSpec((tk, tn), lambda i, j, kk: (kk, j))

    def finish(o_refs, extra_refs, acc):
        outs = (acc,) if epilogue is None else epilogue(acc, *[_f32(e[...]) for e in extra_refs])
        for o_ref, o in zip(o_refs, outs):
            o_ref[...] = o.astype(o_ref.dtype)

    n_after = len(after)

    def body_whole_k(a_ref, b_ref, *refs):
        refs = refs[n_after:]
        finish(refs[n_extra:], refs[:n_extra], _dg(a_ref[...].astype(BF), b_ref[...].astype(BF), ca, cb))

    def body_split_k(a_ref, b_ref, *refs):
        refs = refs[n_after:]
        extra_refs, o_refs, acc = refs[:n_extra], refs[n_extra:-1], refs[-1]
        kk = pl.program_id(2)

        @pl.when(kk == 0)
        def _():
            acc[...] = jnp.zeros_like(acc)

        acc[...] += _dg(a_ref[...].astype(BF), b_ref[...].astype(BF), ca, cb)

        @pl.when(kk == nk - 1)
        def _():
            finish(o_refs, extra_refs, acc[...])

    tile = pl.BlockSpec((tm, tn), lambda i, j, kk: (i, j))
    outs = pl.pallas_call(
        body_whole_k if nk == 1 else body_split_k, name=name, grid=(m // tm, n // tn, nk),
        in_specs=[a_spec, b_spec] + [pl.BlockSpec(memory_space=pl.ANY)] * n_after + [tile] * n_extra, out_specs=[tile] * len(out_dtypes),
        out_shape=[jax.ShapeDtypeStruct((m, n), dt) for dt in out_dtypes],
        scratch_shapes=[] if nk == 1 else [pltpu.VMEM((tm, tn), F32)],
        compiler_params=_params(("parallel", "parallel", "arbitrary")))(a, b, *after, *extras)
    return outs if isinstance(out_dtype, tuple) else outs[0]


_FLIPS = [(0, 0, 1), (1, 0, 0), (0, 1, 0), (1, 1, 0), (1, 0, 1), (0, 1, 1), (1, 1, 1)]


def _me():
    return lax.axis_index("x"), lax.axis_index("y"), lax.axis_index("c")


def _flip(pos, f):
    return tuple(jnp.where(fi == 1, 1 - p, p) if fi else p for p, fi in zip(pos, f))


def _slot(pos):
    return 4 * pos[0] + 2 * pos[1] + pos[2]


def all_gather(v, name):
    def body(v_ref, out_ref, send_sems, recv_sems, local_sem):
        me = _me()
        sibling = _flip(me, (0, 0, 1))
        chips = [_flip(me, f) for f in ((1, 0, 0), (0, 1, 0), (1, 1, 0))]

        def copy(k, block, to, src=None):
            return pltpu.make_async_remote_copy(
                src_ref=out_ref.at[_slot(block)] if src is None else src, dst_ref=out_ref.at[_slot(block)],
                send_sem=send_sems.at[k], recv_sem=recv_sems.at[k], device_id=to, device_id_type=pl.DeviceIdType.MESH)

        mine = pltpu.make_async_copy(v_ref, out_ref.at[_slot(me)], local_sem)
        mine.start()
        first = [copy(0, me, sibling, src=v_ref)] + [copy(1 + j, me, chip, src=v_ref) for j, chip in enumerate(chips)]
        for cp in first:
            cp.start()
        passed = [copy(4 + j, chip, sibling) for j, chip in enumerate(chips)]
        for j, chip in enumerate(chips):
            copy(1 + j, chip, me).wait_recv()
            passed[j].start()
        copy(0, sibling, me).wait_recv()
        for j, chip in enumerate(chips):
            copy(4 + j, _flip(chip, (0, 0, 1)), me).wait_recv()
        for cp in first + passed:
            cp.wait_send()
        mine.wait()

    return pl.pallas_call(
        body, name=name, out_shape=jax.ShapeDtypeStruct((N_DEV,) + v.shape, v.dtype),
        in_specs=[pl.BlockSpec(memory_space=pl.ANY)], out_specs=pl.BlockSpec(memory_space=pl.ANY),
        scratch_shapes=[pltpu.SemaphoreType.DMA((7,)), pltpu.SemaphoreType.DMA((7,)), pltpu.SemaphoreType.DMA(())],
    )(v)


def sum_slots(v, name, tr=256):
    _, r, c = v.shape
    tr = _tile_rows(r, tr)

    def body(v_ref, o_ref):
        acc = v_ref[0].astype(F32)
        for s in range(1, N_DEV):
            acc = acc + v_ref[s].astype(F32)
        o_ref[...] = acc

    return pl.pallas_call(body, name=name, grid=(r // tr,), in_specs=[pl.BlockSpec((N_DEV, tr, c), lambda i: (0, i, 0))],
                          out_specs=pl.BlockSpec((tr, c), lambda i: (i, 0)), out_shape=jax.ShapeDtypeStruct((r, c), F32),
                          compiler_params=_params())(v)


def _tile_rows(r, pref):
    if r <= pref:
        return r
    best = None
    for t in range(8, pref + 1, 8):
        if r % t == 0:
            best = t
    return r if best is None else best


def _adamw_math(w, m, v, g):
    nm = ADAM_B1 * m + (1.0 - ADAM_B1) * g
    nv = ADAM_B2 * v + (1.0 - ADAM_B2) * jnp.square(g)
    m_hat = nm / (1.0 - ADAM_B1 ** ADAM_STEP)
    v_hat = nv / (1.0 - ADAM_B2 ** ADAM_STEP)
    return -ADAM_LR * (m_hat / (jnp.sqrt(v_hat) + ADAM_EPS) + ADAM_WD * w), nm, nv


def update_from_slots(lands, offs, w, m, v, transposed, name):
    layers, a, b = w.shape
    n_land = len(lands)
    if transposed:
        rb, tk = LANE, 512
        assert a % tk == 0 and b % rb == 0 and all(o % rb == 0 for o in offs), (name, w.shape, offs)
        grid = (layers, a // tk, b // rb)
        land_block = (N_DEV, rb, tk)
        tile = pl.BlockSpec((None, tk, rb), lambda l, i, j: (l, i, j))

        def land_spec(layer):
            base = offs[layer] // rb
            return pl.BlockSpec(land_block, lambda l, i, j: (0, base + jnp.where(l == layer, j, 0), jnp.where(l == layer, i, 0)))
    else:
        fits = [t for t in (256, 128, 64) if a % t == 0 and all(o % t == 0 for o in offs)]
        assert fits or all(o == 0 for o in offs), (name, w.shape, offs)
        tr = max(fits) if fits else a
        grid = (layers, a // tr)
        land_block = (N_DEV, _round_up(tr, MEMBER_ROW_TILE), b)
        tile = pl.BlockSpec((None, tr, b), lambda l, i: (l, i, 0))

        def land_spec(layer):
            base = offs[layer] // tr
            return pl.BlockSpec(land_block, lambda l, i: (0, base + jnp.where(l == layer, i, 0), 0))

    def body(*refs):
        land_refs, (w_ref, m_ref, v_ref, g_ref, d_ref, nm_ref, nv_ref, acc) = refs[:n_land], refs[n_land:]
        for layer, land in enumerate(land_refs):
            @pl.when(pl.program_id(0) == layer)
            def _(land=land):
                rows = acc.shape[0]
                s = land[0, :rows].astype(F32)
                for k in range(1, N_DEV):
                    s = s + land[k, :rows].astype(F32)
                acc[...] = s

        g = acc[...].T if transposed else acc[...]
        d, nm, nv = _adamw_math(w_ref[...], m_ref[...], v_ref[...], g)
        g_ref[...] = g
        d_ref[...] = d
        nm_ref[...] = nm
        nv_ref[...] = nv

    sh = jax.ShapeDtypeStruct(w.shape, F32)
    return pl.pallas_call(
        body, name=name, grid=grid, in_specs=[land_spec(layer) for layer in range(n_land)] + [tile] * 3, out_specs=[tile] * 4,
        out_shape=[sh] * 4, scratch_shapes=[pltpu.VMEM((rb, tk) if transposed else (tr, b), F32)],
        compiler_params=_params())(*lands, w, m, v)


def adamw_many(ws, ms, vs, gs, name):
    n = len(ws)

    def body(*refs):
        for i in range(n):
            d, nm, nv = _adamw_math(refs[i][...], refs[n + i][...], refs[2 * n + i][...], refs[3 * n + i][...])
            refs[4 * n + i][...] = d
            refs[5 * n + i][...] = nm
            refs[6 * n + i][...] = nv

    vmem = pl.BlockSpec(memory_space=pltpu.VMEM)
    shapes = [jax.ShapeDtypeStruct(a.shape, F32) for a in ws]
    res = pl.pallas_call(body, name=name, in_specs=[vmem] * (4 * n), out_specs=[vmem] * (3 * n), out_shape=shapes * 3,
                         compiler_params=_params())(*ws, *ms, *vs, *gs)
    return res[:n], res[n:2 * n], res[2 * n:]


def adamw(w, m, v, g, name):
    r, c = w.shape
    tr = _tile_rows(r, 512 if c <= 1024 else 128)

    def body(w_ref, m_ref, v_ref, g_ref, d_ref, nm_ref, nv_ref):
        d_ref[...], nm_ref[...], nv_ref[...] = _adamw_math(w_ref[...], m_ref[...], v_ref[...], g_ref[...])

    spec = pl.BlockSpec((tr, c), lambda i: (i, 0))
    sh = jax.ShapeDtypeStruct((r, c), F32)
    return pl.pallas_call(body, name=name, grid=(r // tr,), in_specs=[spec] * 4, out_specs=[spec] * 3,
                          out_shape=[sh] * 3, compiler_params=_params())(w, m, v, g)


def seg_in(x, g):
    return (_rms(x, g),)


def seg_in_res(x, g):
    return x, _rms(x, g)


def seg_res(x, m, ga, gb):
    x1 = x + _rms(m, ga)
    return x1, _rms(x1, gb)


def seg_out(x, m, ga):
    return (x + _rms(m, ga),)


def act_epilogue(r):
    t = jnp.maximum(r, 0.0)
    return r, t * t


def act_bwd_epilogue(drr, r):
    return (drr * (2.0 * jnp.maximum(r, 0.0)),)


def seg_ln(v, g, b):
    mu = jnp.mean(v, axis=-1, keepdims=True)
    var = jnp.mean(jnp.square(v - mu), axis=-1, keepdims=True)
    vn = (v - mu) * lax.rsqrt(var + LN_EPS) * g + b
    return (jax.nn.silu(vn),)


def make_pool_fn(group):
    window = 2 ** (group + 1)

    def pool_fn(ug, pw, scale):
        s = ug
        for lvl in range(group + 1):
            s = s + shift(s, 2 ** lvl)
        cnt = jnp.minimum(lax.broadcasted_iota(jnp.int32, ug.shape, 0) + 1, window).astype(F32)
        return (bdot(s / cnt - ug, pw, 1, 0) * scale,)

    return pool_fn


def conv4_fn(xr, w, b):
    return (jax.nn.silu(cconv(xr, w, SSM_CONV) + b),)


def cd1_fn(u, dww, dwb, scw):
    val, gate, bg, cg, hh = (u[:, k * LANE:(k + 1) * LANE] for k in range(5))
    v = val * jax.nn.sigmoid(gate)
    vc = cconv(v, dww, CONF_K) + dwb
    sc = bg * cconv(cg * hh, scw, SC_K)
    return vc, sc


def attn_fn(q, kv):
    outs = []
    for h in range(XA_HEADS):
        cols = slice(h * XA_DH, (h + 1) * XA_DH)
        s = bdot(q[:, cols], kv[:, cols], 1, 1) / math.sqrt(XA_DH)
        p = jax.nn.softmax(s, axis=-1)
        outs.append(bdot(p, kv[:, D + h * XA_DH:D + (h + 1) * XA_DH], 1, 0))
    return (jnp.concatenate(outs, axis=1),)


def ssd_chunk(xbc, z, dtraw, dtb, alog, dsk, nw, h0, h1, h2, h3, e64, e64t, ecat, ecatt, tril, trilt):
    xs, bm, cm = xbc[:, :SSM_GSZ], xbc[:, SSM_GSZ:SSM_GSZ + SSM_N], xbc[:, SSM_GSZ + SSM_N:]
    hin = (h0, h1, h2, h3)
    dt = jax.nn.softplus(dtraw + dtb)
    a = -jnp.exp(alog)
    d_a = dt * a
    cs = cmatl(tril, trilt, d_a)
    cs_cat = cmat(cs, ecat, ecatt)
    cs64, cs128 = cs_cat[:, :SSM_GSZ], cs_cat[:, SSM_GSZ:]
    dt64 = cmat(dt, e64, e64t)
    row = lax.broadcasted_iota(jnp.int32, (8, LANE), 0)
    heads = jnp.where(row == 0, dsk, jnp.where(row == 1, jnp.sum(d_a, axis=0, keepdims=True), 0.0))
    heads64 = cmat(heads, e64, e64t)
    d64, tot64 = heads64[0:1, :], heads64[1:2, :]
    xdt = xs * dt64
    cb = bdot(cm, bm, 1, 1)
    li = lax.broadcasted_iota(jnp.int32, (CHUNK, CHUNK), 0)
    si = lax.broadcasted_iota(jnp.int32, (CHUNK, CHUNK), 1)
    causal = li >= si
    lane = lax.broadcasted_iota(jnp.int32, (CHUNK, LANE), 1)
    xw = xdt * jnp.exp(tot64 - cs64)
    ecs = jnp.exp(cs64)
    etot = jnp.exp(tot64)
    ycols, hout = [], []
    for j in range(4):
        sl = slice(j * LANE, (j + 1) * LANE)
        xj = xdt[:, sl]
        ys = []
        for hh in range(2):
            r = 2 * j + hh
            col = cs128[:, r * LANE:(r + 1) * LANE]
            decay = jnp.exp(jnp.where(causal, col - col.T, -1e30))
            ys.append(bdot(cb * decay, xj, 1, 0))
        y_diag = jnp.where(lane < SSM_P, ys[0], ys[1])
        y_off = bdot(cm, hin[j], 1, 0) * ecs[:, sl]
        ycols.append(y_diag + y_off)
        hout.append(etot[:, sl] * hin[j] + bdot(bm, xw[:, sl], 0, 0))
    y = jnp.concatenate(ycols, axis=1) + d64 * xs
    y = y * jax.nn.silu(z)
    yn = y * lax.rsqrt(jnp.mean(y * y, axis=-1, keepdims=True) + RMS_EPS) * nw
    return (yn,) + tuple(hout)


def _xbc_group(a, axis):
    parts = []
    for g in range(SSM_GROUPS):
        for start, width in ((g * SSM_GSZ, SSM_GSZ), (SSM_INNER + g * SSM_N, SSM_N), (SSM_INNER + (SSM_GROUPS + g) * SSM_N, SSM_N)):
            parts.append(lax.slice_in_dim(a, start, start + width, axis=axis))
    return jnp.concatenate(parts, axis=axis)


def _xbc_ungroup(a, axis):
    xs, bs, cs = [], [], []
    for g in range(SSM_GROUPS):
        base = g * SSM_XBC_G
        xs.append(lax.slice_in_dim(a, base, base + SSM_GSZ, axis=axis))
        bs.append(lax.slice_in_dim(a, base + SSM_GSZ, base + SSM_GSZ + SSM_N, axis=axis))
        cs.append(lax.slice_in_dim(a, base + SSM_GSZ + SSM_N, base + SSM_XBC_G, axis=axis))
    return jnp.concatenate(xs + bs + cs, axis=axis)


def _ssd_consts():
    h = np.arange(LANE)[:, None]
    e64 = np.stack([(h == g * 8 + np.arange(SSM_GSZ)[None, :] // SSM_P) for g in range(SSM_GROUPS)]).astype(np.float32)
    e128 = np.stack([(h == g * 8 + np.arange(8 * LANE)[None, :] // LANE) for g in range(SSM_GROUPS)]).astype(np.float32)
    ecat = np.concatenate([e64, e128], axis=2)
    tril = np.tril(np.ones((CHUNK, CHUNK), np.float32))
    return tuple(jnp.asarray(c, dtype=BF) for c in (e64, e64.transpose(0, 2, 1), ecat, ecat.transpose(0, 2, 1), tril, tril.T))


def _ssd_specs(nc, rev):
    def ci(c):
        return nc - 1 - c if rev else c

    def row(width, col):
        return pl.BlockSpec((CHUNK, width), lambda b, c: (b * nc + ci(c), col))

    def whole(shape):
        return pl.BlockSpec(shape, lambda b, c: (0,) * len(shape))

    data = [row(SSM_CONV_DIM, 0),
            row(SSM_GSZ, 1), row(SSM_GSZ, 2), row(LANE, 24)]
    par = [whole((1, LANE))] * 3 + [whole((1, SSM_INNER))]
    cst = [whole((SSM_GROUPS, LANE, SSM_GSZ)), whole((SSM_GROUPS, SSM_GSZ, LANE)), whole((SSM_GROUPS, LANE, 12 * LANE)),
           whole((SSM_GROUPS, 12 * LANE, LANE)), whole((CHUNK, CHUNK)), whole((CHUNK, CHUNK))]
    hsave = pl.BlockSpec((None, None, SSM_GROUPS, 4, SSM_N, LANE), lambda b, c: (b, ci(c), 0, 0, 0, 0))
    return data, par, cst, hsave, row, whole


def _ssd_group_args(g, xbc, z, dtr, dtb, alog, dsk, nw):
    return (xbc[:, g * SSM_XBC_G:(g + 1) * SSM_XBC_G], z[g], dtr, dtb, alog, dsk, nw[:, g * SSM_GSZ:(g + 1) * SSM_GSZ])


def ssd_fwd(xbc_act, u, dtb, alog, dsk, nw, consts, bsz, seq):
    nc = seq // CHUNK
    data, par, cst, hsave, row, _ = _ssd_specs(nc, False)

    def body(xbc, z0, z1, dtr, dtb_r, alog_r, dsk_r, nw_r, e64, e64t, ecat, ecatt, tril, trilt, yn_ref, hs_ref, h):
        @pl.when(pl.program_id(1) == 0)
        def _():
            h[...] = jnp.zeros_like(h)

        hs_ref[...] = h[...]
        ys = []
        for g in range(SSM_GROUPS):
            args = _ssd_group_args(g, xbc[...], (z0[...], z1[...]), dtr[...], dtb_r[...], alog_r[...], dsk_r[...], nw_r[...])
            outs = ssd_chunk(*args, h[g, 0], h[g, 1], h[g, 2], h[g, 3], e64[g], e64t[g], ecat[g], ecatt[g], tril[...], trilt[...])
            ys.append(outs[0])
            for j in range(4):
                h[g, j] = outs[1 + j]
        yn_ref[...] = jnp.concatenate(ys, axis=1).astype(yn_ref.dtype)

    t = bsz * seq
    return pl.pallas_call(
        body, name="ssd_fwd", grid=(bsz, nc), in_specs=data + par + cst, out_specs=[row(SSM_INNER, 0), hsave],
        out_shape=[jax.ShapeDtypeStruct((t, SSM_INNER), BF), jax.ShapeDtypeStruct((bsz, nc, SSM_GROUPS, 4, SSM_N, LANE), F32)],
        scratch_shapes=[pltpu.VMEM((SSM_GROUPS, 4, SSM_N, LANE), F32)], compiler_params=_params(),
    )(xbc_act, u, u, u, dtb, alog, dsk, nw, *consts)


def ssd_bwd(xbc_act, u, dtb, alog, dsk, nw, consts, hs, dmix, bsz, seq):
    nc = seq // CHUNK
    data, par, cst, hsave, row, whole = _ssd_specs(nc, True)
    t = bsz * seq
    pcol = POOL_W // SSM_GSZ

    def body(xbc, z0, z1, dtr, dtb_r, alog_r, dsk_r, nw_r, e64, e64t, ecat, ecatt, tril, trilt, hs_ref, dy0, dy1,
             dxbc, dz, ddt, ddtb, dalog, ddsk, dnw, dh):
        @pl.when(pl.program_id(1) == 0)
        def _():
            dh[...] = jnp.zeros_like(dh)

        per_group = []
        for g, dyn in enumerate((dy0, dy1)):
            cst_vals = (e64[g], e64t[g], ecat[g], ecatt[g], tril[...], trilt[...])
            prim = _ssd_group_args(g, xbc[...], (z0[...], z1[...]), dtr[...], dtb_r[...], alog_r[...], dsk_r[...], nw_r[...])
            prim = prim + (hs_ref[g, 0], hs_ref[g, 1], hs_ref[g, 2], hs_ref[g, 3])
            _, vjp = jax.vjp(lambda *args, c=cst_vals: ssd_chunk(*args, *c), *prim)
            gr = vjp((dyn[...].astype(F32), dh[g, 0], dh[g, 1], dh[g, 2], dh[g, 3]))
            for j in range(4):
                dh[g, j] = gr[7 + j]
            per_group.append(gr)
        g0, g1 = per_group
        dxbc[...] = jnp.concatenate([g0[0], g1[0]], axis=1)
        dz[...] = jnp.concatenate([g0[1], g1[1]], axis=1).astype(dz.dtype)
        ddt[...] = g0[2] + g1[2]

        @pl.when(_first((0, 1)))
        def _():
            for r in (ddtb, dalog, ddsk, dnw):
                r[...] = jnp.zeros_like(r)

        ddtb[...] += g0[3] + g1[3]
        dalog[...] += g0[4] + g1[4]
        ddsk[...] += g0[5] + g1[5]
        dnw[...] += jnp.concatenate([g0[6], g1[6]], axis=1)

    out_specs = [row(SSM_CONV_DIM, 0), row(SSM_INNER, 0), row(LANE, 0), whole((1, LANE)), whole((1, LANE)), whole((1, LANE)),
                 whole((1, SSM_INNER))]
    lane = jax.ShapeDtypeStruct((1, LANE), F32)
    out_shape = [jax.ShapeDtypeStruct((t, SSM_CONV_DIM), F32), jax.ShapeDtypeStruct((t, SSM_INNER), BF),
                 jax.ShapeDtypeStruct((t, LANE), F32), lane, lane, lane, jax.ShapeDtypeStruct((1, SSM_INNER), F32)]
    return pl.pallas_call(
        body, name="ssd_bwd", grid=(bsz, nc), in_specs=data + par + cst + [hsave, row(SSM_GSZ, pcol), row(SSM_GSZ, pcol + 1)],
        out_specs=out_specs, out_shape=out_shape, scratch_shapes=[pltpu.VMEM((SSM_GROUPS, 4, SSM_N, LANE), F32)],
        compiler_params=_params(),
    )(xbc_act, u, u, u, dtb, alog, dsk, nw, *consts, hs, dmix, dmix)


TB = 512


def _rows(d, col=0):
    return pl.BlockSpec((TB, d), lambda i: (i, col))


def _par(d):
    return pl.BlockSpec((1, d), lambda i: (0, 0))


def _sd(shape, dtype=F32):
    return jax.ShapeDtypeStruct(shape, dtype)


def _round_up(n, m):
    return -(-n // m) * m


def _pad_rows(a, rows):
    return jnp.pad(a, ((0, rows - a.shape[0]), (0, 0)))


def _pack128(arrs):
    flat = jnp.concatenate([a.reshape(-1) for a in arrs])
    n = flat.shape[0]
    rows = -(-n // (8 * LANE)) * 8
    return jnp.pad(flat, (0, rows * LANE - n)).reshape(rows, LANE)


def _unpack128(packed, shapes):
    flat = packed.reshape(-1)
    out, off = [], 0
    for s in shapes:
        n = int(np.prod(s))
        out.append(flat[off:off + n].reshape(s))
        off += n
    return out


def kernel(x, mem, norm_gains, xa_wq, xa_wkv, xa_wo, mlp_w1, mlp_w2, ab_w_in, pool_w, pool_scale, ssm_conv_w, ssm_conv_b, ssm_dt_bias, ssm_a_log, ssm_d, ssm_norm, ab_w_out, cd_w_in, conf_dw_w, conf_dw_b, conf_ln_g, conf_ln_b, sc_conv_w, cd_w_out, loss_target, m_norm_gains, m_xa_wq, m_xa_wkv, m_xa_wo, m_mlp_w1, m_mlp_w2, m_ab_w_in, m_pool_w, m_pool_scale, m_ssm_conv_w, m_ssm_conv_b, m_ssm_dt_bias, m_ssm_a_log, m_ssm_d, m_ssm_norm, m_ab_w_out, m_cd_w_in, m_conf_dw_w, m_conf_dw_b, m_conf_ln_g, m_conf_ln_b, m_sc_conv_w, m_cd_w_out, v_norm_gains, v_xa_wq, v_xa_wkv, v_xa_wo, v_mlp_w1, v_mlp_w2, v_ab_w_in, v_pool_w, v_pool_scale, v_ssm_conv_w, v_ssm_conv_b, v_ssm_dt_bias, v_ssm_a_log, v_ssm_d, v_ssm_norm, v_ab_w_out, v_cd_w_in, v_conf_dw_w, v_conf_dw_b, v_conf_ln_g, v_conf_ln_b, v_sc_conv_w, v_cd_w_out):
    args = locals()
    w = {n: args[n] for n in WEIGHTS}
    mom_m = {n: args["m_" + n] for n in WEIGHTS}
    mom_v = {n: args["v_" + n] for n in WEIGHTS}
    ex = Exchange(w)
    loss_local, grad_x, small_grads = local_step(x, mem, loss_target, ex)
    outs = {}

    started = ex.put_small(small_grads, loss_local)
    landed = {key: ex.landed(key, started) for key in ('l1', 'cd', 'l0')}
    late = []
    for n, keys in (('mlp_w1', ('l0', 'l1')), ('mlp_w2', ('l0', 'l1')), ('xa_wkv', ('l0', 'l1')), ('xa_wq', ('l0', 'l1')),
                    ('xa_wo', ('l0', 'l1')), ('cd_w_in', ('cd',)), ('cd_w_out', ('cd',))):
        lands = [landed[key][0] for key in keys]
        offs = [landed[key][1][(n, layer)] for layer, key in enumerate(keys)]
        outs[n] = update_from_slots(lands, offs, w[n], mom_m[n], mom_v[n], SHARD_AXIS[n] == 2, "update_" + n)
        late.append(outs[n][1])
    g_own, loss = ex.reduced_small(late)
    land_ab, offs_ab = ex.landed('ab', late)
    outs['ab_w_out'] = update_from_slots([land_ab], [offs_ab[('ab_w_out', 0)]], w['ab_w_out'], mom_m['ab_w_out'],
                                         mom_v['ab_w_out'], False, "update_ab_w_out")
    res = update_from_slots([land_ab], [offs_ab[('ab_w_in', 0)]], jnp.swapaxes(w['ab_w_in'], 1, 2), jnp.swapaxes(mom_m['ab_w_in'], 1, 2),
                            jnp.swapaxes(mom_v['ab_w_in'], 1, 2), False, "update_ab_w_in")
    outs['ab_w_in'] = tuple(jnp.swapaxes(r, 1, 2) for r in res)
    small = SMALL_SHARDED + REPLICATED
    upd = adamw_many([w[n] for n in small], [mom_m[n] for n in small], [mom_v[n] for n in small], [g_own[n] for n in small],
                     "adamw_small")
    for i, n in enumerate(small):
        outs[n] = (g_own[n], upd[0][i], upd[1][i], upd[2][i])
    return (loss, grad_x.reshape(x.shape), *[outs[n][0] for n in WEIGHTS], *[outs[n][1] for n in WEIGHTS],
            *[outs[n][2] for n in WEIGHTS], *[outs[n][3] for n in WEIGHTS])


G_AB = (('ab_w_in', 0), ('ab_w_out', 0))
G_L0 = (('xa_wq', 0), ('xa_wkv', 0), ('xa_wo', 0), ('mlp_w1', 0), ('mlp_w2', 0))
G_L1 = (('xa_wq', 1), ('xa_wkv', 1), ('xa_wo', 1), ('mlp_w1', 1), ('mlp_w2', 1))
G_CD = (('cd_w_in', 0), ('cd_w_out', 0))
GATHER_CHAIN = {'l0': ('cd', G_CD), 'cd': ('l1', G_L1)}
SHARD_AXIS = dict(BIG)
MEMBER_ROW_TILE = 64
FLAT_ROW_TILE = 128


def _members(group, w):
    out = []
    for n, layer in group:
        shp = w[n].shape[1:]
        if SHARD_AXIS[n] == 2:
            shp = (shp[1], shp[0])
        assert shp[1] == D, (n, shp)
        out.append((n, layer, shp, shp[0], _round_up(shp[0], MEMBER_ROW_TILE)))
    return out


def _group_rows(group, w):
    return _round_up(sum(m[4] for m in _members(group, w)), FLAT_ROW_TILE)


def _flat_shards(group, w):
    parts = []
    for n, layer, _, _, padded in _members(group, w):
        shard = w[n][layer].astype(BF)
        parts.append(_pad_rows(shard.T if SHARD_AXIS[n] == 2 else shard, padded))
    return _pad_rows(jnp.concatenate(parts, axis=0), _group_rows(group, w))


def _full_from_slots(land, group, w):
    out, off = {}, 0
    for n, layer, shp, rows, padded in _members(group, w):
        out[(n, layer)] = land[:, off:off + rows].reshape(N_DEV * rows, D)
        off += padded
    return out


def _slots_from_full(grads, group, w):
    parts = []
    for n, layer, shp, rows, padded in _members(group, w):
        blk = grads[(n, layer)].astype(BF).reshape(N_DEV, rows, D)
        parts.append(jnp.pad(blk, ((0, 0), (0, padded - rows), (0, 0))))
    send = jnp.concatenate(parts, axis=1)
    return jnp.pad(send, ((0, 0), (0, _group_rows(group, w) - send.shape[1]), (0, 0)))


_HBM = pl.BlockSpec(memory_space=pltpu.HBM)
_SEM = pl.BlockSpec(memory_space=pltpu.SEMAPHORE)
_ANY = pl.BlockSpec(memory_space=pl.ANY)


def _peer_copy(k, src, dst, send_sems, recv_sems, peer):
    return pltpu.make_async_remote_copy(src_ref=src, dst_ref=dst, send_sem=send_sems.at[k], recv_sem=recv_sems.at[k],
                                        device_id=peer, device_id_type=pl.DeviceIdType.MESH)


def exchange_start(src, name, scatter, after=()):
    shape = src.shape[-2:]
    after = list(after)

    def body(src_ref, land_ref, *rest):
        send_sems, recv_sems, token = rest[len(after)], rest[len(after) + 1], rest[-1]
        me = _me()
        for k, f in enumerate(_FLIPS):
            peer = _flip(me, f)
            piece = src_ref.at[_slot(peer)] if scatter else src_ref
            _peer_copy(k, piece, land_ref.at[_slot(me)], send_sems, recv_sems, peer).start()
        token[...] = jnp.zeros_like(token)

    land = pltpu.with_memory_space_constraint(lax.empty((N_DEV,) + shape, src.dtype), pltpu.HBM)
    return pl.pallas_call(
        body, name=name,
        out_shape=(pltpu.SemaphoreType.DMA((7,)), pltpu.SemaphoreType.DMA((7,)), pltpu.HBM(src.shape, src.dtype),
                   pltpu.HBM((N_DEV,) + shape, src.dtype), jax.ShapeDtypeStruct((8, LANE), F32)),
        in_specs=(_HBM, _HBM) + (_ANY,) * len(after), out_specs=(_SEM, _SEM, _HBM, _HBM, pl.BlockSpec(memory_space=pltpu.VMEM)),
        input_output_aliases={0: 2, 1: 3},
        compiler_params=pltpu.CompilerParams(has_side_effects=pltpu.SideEffectType.DATAFLOW_SIDE_EFFECTING),
    )(pltpu.with_memory_space_constraint(src, pltpu.HBM), land, *after)


def exchange_wait(handles, after, name, scatter):
    send_sems, recv_sems, src_thru, land_thru, _ = handles
    after = list(after) if isinstance(after, (list, tuple)) else [after]

    def body(src_ref, land_ref, send_sems, recv_sems, *rest):
        token = rest[-1]
        me = _me()
        for k, f in enumerate(_FLIPS):
            peer = _flip(me, f)
            piece = src_ref.at[_slot(peer)] if scatter else src_ref
            cp = _peer_copy(k, piece, land_ref.at[_slot(peer)], send_sems, recv_sems, peer)
            cp.wait_send()
            cp.wait_recv()
        token[...] = jnp.zeros_like(token)

    return pl.pallas_call(
        body, name=name, out_shape=(pltpu.HBM(src_thru.shape, src_thru.dtype), pltpu.HBM(land_thru.shape, land_thru.dtype),
                                    jax.ShapeDtypeStruct((8, LANE), F32)),
        in_specs=(_HBM, _HBM, _SEM, _SEM) + (_ANY,) * len(after), out_specs=(_HBM, _HBM, pl.BlockSpec(memory_space=pltpu.VMEM)),
        input_output_aliases={0: 0, 1: 1},
        compiler_params=pltpu.CompilerParams(has_side_effects=pltpu.SideEffectType.DATAFLOW_SIDE_EFFECTING),
    )(src_thru, land_thru, send_sems, recv_sems, *after)


class Exchange:
    def __init__(self, w):
        self.w = w
        self.me = _slot(_me())
        shapes = [w[n].shape for n in SMALL_SHARDED]
        gs = all_gather(_pack128([w[n] for n in SMALL_SHARDED]), "gather_small")
        per_dev = [_unpack128(gs[d], shapes) for d in range(N_DEV)]
        self.small = {n: jnp.concatenate([per_dev[d][i] for d in range(N_DEV)], axis=-1) for i, n in enumerate(SMALL_SHARDED)}
        self.small.update({n: w[n] for n in REPLICATED})
        self.first = _full_from_slots(all_gather(_flat_shards(G_AB, w), "gather_ab"), G_AB, w)
        self.gathers = {'l0': (G_L0, exchange_start(_flat_shards(G_L0, w), "gather_l0_start", False))}
        self.tokens = [self.gathers['l0'][1][4]]
        self.reductions = {}

    def take_tokens(self):
        toks, self.tokens = self.tokens, []
        return toks

    def weights(self, key, after):
        if key == 'ab':
            return self.first
        group, handles = self.gathers[key]
        _, land, done = exchange_wait(handles, after, f"gather_{key}_wait", False)
        nxt = GATHER_CHAIN.get(key)
        if nxt is not None:
            self.gathers[nxt[0]] = (nxt[1], exchange_start(_flat_shards(nxt[1], self.w), f"gather_{nxt[0]}_start", False, after=[done]))
            self.tokens.append(self.gathers[nxt[0]][1][4])
        land = lax.dynamic_update_slice(land, handles[2][None], (self.me, 0, 0))
        return _full_from_slots(land, group, self.w)

    def put_grads(self, key, group, grads):
        send = _slots_from_full(grads, group, self.w)
        handles = exchange_start(send, f"reduce_{key}_start", True)
        self.reductions[key] = (group, handles)
        self.tokens.append(handles[4])

    def landed(self, key, after):
        group, handles = self.reductions[key]
        send, land, _ = exchange_wait(handles, after, f"reduce_{key}_wait", True)
        mine = lax.dynamic_slice_in_dim(send, self.me, 1, axis=0)
        land = lax.dynamic_update_slice(land, mine, (self.me, 0, 0))
        offs, off = {}, 0
        for n, layer, _, _, padded in _members(group, self.w):
            offs[(n, layer)] = off
            off += padded
        return land, offs

    def put_small(self, small_grads, loss_local):
        small = SMALL_SHARDED + REPLICATED
        self.small_shapes = [small_grads[n].shape for n in small] + [(1,)]
        packed = _pack128([small_grads[n] for n in small] + [loss_local.reshape(1)])
        self.small_handles = exchange_start(packed, "gather_small_grads_start", False)
        return self.small_handles[4]

    def reduced_small(self, after):
        small = SMALL_SHARDED + REPLICATED
        src, land, _ = exchange_wait(self.small_handles, after, "gather_small_grads_wait", False)
        gs = lax.dynamic_update_slice(land, src[None], (self.me, 0, 0))
        tot = _unpack128(sum_slots(gs, "sum_small", 1024), self.small_shapes)
        out = {}
        for n, g in zip(small, tot):
            if n in SMALL_SHARDED:
                width = self.w[n].shape[-1]
                g = lax.dynamic_slice_in_dim(g, self.me * width, width, axis=g.ndim - 1)
            out[n] = g
        return out, tot[-1].reshape(())


def local_step(x, mem, target, ex):
    bsz, seq, _ = x.shape
    t = bsz * seq
    nb = t // TB
    nc = seq // CHUNK
    x0 = x.reshape(t, D)
    mem2 = mem.reshape(bsz * N_MEM, D)
    tgt = target.reshape(t, D)
    p = ex.small
    gains = p['norm_gains']
    big = {}

    def gain(layer, i):
        g = gains[layer, i].reshape(1, D)
        for tok in ex.take_tokens():
            g = g + tok[0, 0]
        return g

    consts = _ssd_consts()
    grads = {}
    saved = [dict(), dict()]

    def run_seg_res(xin, m, ga, gb, name):
        return fwd_call(seg_res, name, (nb,), [xin, m, ga, gb], [_rows(D), _rows(D), _par(D), _par(D)],
                        [_sd((t, D)), _sd((t, D), BF)], [_rows(D), _rows(D)])

    def attn_specs():
        nq = seq // TB
        q = pl.BlockSpec((TB, D), lambda b, i: (b * nq + i, 0))
        kv = pl.BlockSpec((N_MEM, 2 * D), lambda b, i: (b, 0))
        return (bsz, nq), q, kv

    def attention_fwd(layer, xin, hin, sv):
        q = matmul(hin, big[('xa_wq', layer)], 'nn', f"q_{layer}", BF)
        kv = matmul(mem2, big[('xa_wkv', layer)], 'nt', f"kv_{layer}", BF)
        grid, qs, kvs = attn_specs()
        o, = fwd_call(attn_fn, f"attn_{layer}", grid, [q, kv], [qs, kvs], [_sd((t, D), BF)], [qs])
        ao = matmul(o, big[('xa_wo', layer)], 'nn', f"ao_{layer}")
        sv.update(q=q, kv=kv, o=o, ao=ao)
        return ao

    def mlp_fwd(layer, hin, sv):
        r, rr = matmul(hin, big[('mlp_w1', layer)], 'nt', f"mlp1_{layer}", (BF, BF), epilogue=act_epilogue)
        mo = matmul(rr, big[('mlp_w2', layer)], 'nn', f"mlp2_{layer}")
        sv.update(r=r, rr=rr, mo=mo)
        return mo

    sv = saved[0]
    h0, = fwd_call(seg_in, "norm_in", (nb,), [x0, gain(0, 0)], [_rows(D), _par(D)], [_sd((t, D), BF)], [_rows(D)])
    big.update(ex.weights('ab', h0))
    xbc0 = POOL_W + SSM_INNER
    w_ab_in = big[('ab_w_in', 0)]
    w_ab_in = _pad_rows(jnp.concatenate([w_ab_in[:xbc0], _xbc_group(w_ab_in[xbc0:xbc0 + SSM_CONV_DIM], 0),
                                         w_ab_in[xbc0 + SSM_CONV_DIM:]], axis=0), AB_IN_PAD)
    conv_w, conv_b = _xbc_group(p['ssm_conv_w'][0], 1), _xbc_group(p['ssm_conv_b'], 1)
    u0 = matmul(h0, w_ab_in, 'nt', "ab_in")
    pool_outs = []
    for g in range(POOL_GROUPS):
        seqspec = pl.BlockSpec((seq, PG), lambda b, g=g: (b, g))
        po, = fwd_call(make_pool_fn(g), f"pool_{g}", (bsz,), [u0, p['pool_w'][0, g], p['pool_scale']],
                       [seqspec, pl.BlockSpec((PG, PG), lambda b: (0, 0)), pl.BlockSpec((1, PG), lambda b, g=g: (0, g))],
                       [_sd((t, PG), BF)], [pl.BlockSpec((seq, PG), lambda b: (b, 0))])
        pool_outs.append(po)
    cw = 256
    ncb = SSM_CONV_DIM // cw
    cbase = (POOL_W + SSM_INNER) // cw
    conv_in_specs = [pl.BlockSpec((seq, cw), lambda j, b: (b, cbase + j)), pl.BlockSpec((SSM_CONV, cw), lambda j, b: (0, j)),
                     pl.BlockSpec((1, cw), lambda j, b: (0, j))]
    conv_out_spec = pl.BlockSpec((seq, cw), lambda j, b: (b, j))
    xbc_act, = fwd_call(conv4_fn, "ssm_conv", (ncb, bsz), [u0, conv_w, conv_b], conv_in_specs,
                        [_sd((t, SSM_CONV_DIM))], [conv_out_spec])
    dtb = jnp.pad(p['ssm_dt_bias'], ((0, 0), (0, LANE - SSM_HEADS)))
    alog = jnp.pad(p['ssm_a_log'], ((0, 0), (0, LANE - SSM_HEADS)))
    dsk = jnp.pad(p['ssm_d'], ((0, 0), (0, LANE - SSM_HEADS)))
    yn, hs = ssd_fwd(xbc_act, u0, dtb, alog, dsk, p['ssm_norm'], consts, bsz, seq)
    mix0 = jnp.concatenate(pool_outs + [yn], axis=1)
    m0 = matmul(mix0, big[('ab_w_out', 0)], 'nn', "ab_out")
    x1, h2 = run_seg_res(x0, m0, gain(0, 1), gain(0, 2), "res_0a")
    big.update(ex.weights('l0', h2))
    ao0 = attention_fwd(0, x1, h2, sv)
    x2, h3 = run_seg_res(x1, ao0, gain(0, 3), gain(0, 4), "res_0b")
    mo0 = mlp_fwd(0, h3, sv)
    big.update(ex.weights('cd', mo0))
    x3, h4 = run_seg_res(x2, mo0, gain(0, 5), gain(1, 0), "res_0c")

    sv1 = saved[1]
    nd = D // LANE
    w_cd_in = big[('cd_w_in', 0)].reshape(5, nd, LANE, D).transpose(1, 0, 2, 3).reshape(CD_IN, D)
    u1 = matmul(h4, w_cd_in, 'nt', "cd_in")
    cd_par = [pl.BlockSpec((CONF_K, LANE), lambda j, b: (0, j)), pl.BlockSpec((1, LANE), lambda j, b: (0, j)),
              pl.BlockSpec((SC_K, LANE), lambda j, b: (0, j))]
    cd_ins = [u1, p['conf_dw_w'][0], p['conf_dw_b'], p['sc_conv_w'][0]]
    cd_u_spec = pl.BlockSpec((seq, 5 * LANE), lambda j, b: (b, j))
    cd_in_specs = [cd_u_spec] + cd_par
    cd_out_spec = pl.BlockSpec((seq, LANE), lambda j, b: (b, j))
    vconv, mix1 = fwd_call(cd1_fn, "cd_conv", (nd, bsz), cd_ins, cd_in_specs, [_sd((t, D)), _sd((t, CD_OUT), BF)],
                           [cd_out_spec, pl.BlockSpec((seq, LANE), lambda j, b: (b, nd + j))])
    mix1, = fwd_call(seg_ln, "conf_ln", (nb,), [vconv, p['conf_ln_g'], p['conf_ln_b']], [_rows(D), _par(D), _par(D)],
                     [_sd((t, CD_OUT), BF)], [_rows(D)], into=mix1)
    m1 = matmul(mix1, big[('cd_w_out', 0)], 'nn', "cd_out")
    x4, h5 = run_seg_res(x3, m1, gain(1, 1), gain(1, 2), "res_1a")
    big.update(ex.weights('l1', h5))
    ao1 = attention_fwd(1, x4, h5, sv1)
    x5, h6 = run_seg_res(x4, ao1, gain(1, 3), gain(1, 4), "res_1b")
    mo1 = mlp_fwd(1, h6, sv1)

    def loss_body(x_ref, m_ref, g_ref, t_ref, dx_ref, dm_ref, dg_ref, acc_ref):
        (y,), vjp = jax.vjp(seg_out, x_ref[...], m_ref[...], g_ref[...])
        d = y - t_ref[...]
        dx, dm, dg = vjp((d / float(D),))
        dx_ref[...] = dx
        dm_ref[...] = dm.astype(dm_ref.dtype)

        @pl.when(pl.program_id(0) == 0)
        def _():
            acc_ref[...] = jnp.zeros_like(acc_ref)
            dg_ref[...] = jnp.zeros_like(dg_ref)

        acc_ref[...] += jnp.sum(d * d, axis=0, keepdims=True)
        dg_ref[...] += dg

    dx5, dmo1, dg15, lanes = pl.pallas_call(
        loss_body, name="loss_head", grid=(nb,), in_specs=[_rows(D), _rows(D), _par(D), _rows(D)],
        out_specs=[_rows(D), _rows(D), _par(D), _par(D)], out_shape=[_sd((t, D)), _sd((t, D), BF), _sd((1, D)), _sd((1, D))],
        compiler_params=_params())(x5, mo1, gain(1, 5), tgt)
    loss = 0.5 * jnp.sum(lanes) / float(D)

    gain_grads = {(1, 5): dg15}

    def bwd_seg_res(xin, m, ga, gb, dx1, dh, name):
        return bwd_call(seg_res, name, (nb,), [xin, m, ga, gb], [_rows(D), _rows(D), _par(D), _par(D)], [dx1, dh],
                        [_rows(D), _rows(D)], [0, 1, 2, 3], [_sd((t, D)), _sd((t, D), BF), _sd((1, D)), _sd((1, D))],
                        [_rows(D), _rows(D), _par(D), _par(D)], [None, None, (0,), (0,)])

    def mlp_bwd(layer, hin, dmo, sv):
        grads_w2 = matmul(sv['rr'], dmo, 'tn', f"d_mlp_w2_{layer}", BF)
        dr, = matmul(dmo, big[('mlp_w2', layer)], 'nt', f"d_r_{layer}", (BF,), epilogue=act_bwd_epilogue, extras=[sv['r']])
        grads_w1 = matmul(dr, hin, 'tn', f"d_mlp_w1_{layer}", BF)
        dh = matmul(dr, big[('mlp_w1', layer)], 'nn', f"d_h_mlp_{layer}")
        return dh, grads_w1, grads_w2

    def attention_bwd(layer, hin, dao, sv):
        g_wo = matmul(sv['o'], dao, 'tn', f"d_xa_wo_{layer}", BF)
        do = matmul(dao, big[('xa_wo', layer)], 'nt', f"d_o_{layer}", BF)
        grid, qs, kvs = attn_specs()
        dq, dkv = bwd_call(attn_fn, f"d_attn_{layer}", grid, [sv['q'], sv['kv']], [qs, kvs], [do], [qs], [0, 1],
                           [_sd((t, D), BF), _sd((bsz * N_MEM, 2 * D))], [qs, kvs], [None, (1,)])
        g_wkv = matmul(dkv, mem2, 'tn', f"d_xa_wkv_{layer}", BF)
        g_wq = matmul(hin, dq, 'tn', f"d_xa_wq_{layer}", BF)
        dh = matmul(dq, big[('xa_wq', layer)], 'nt', f"d_h_attn_{layer}")
        return dh, g_wq, g_wkv, g_wo

    per_layer = {k: [None, None] for k in ('xa_wq', 'xa_wkv', 'xa_wo', 'mlp_w1', 'mlp_w2')}

    dh6, per_layer['mlp_w1'][1], per_layer['mlp_w2'][1] = mlp_bwd(1, h6, dmo1, sv1)
    dx4, dao1, gain_grads[(1, 3)], gain_grads[(1, 4)] = bwd_seg_res(x4, ao1, gain(1, 3), gain(1, 4), dx5, dh6, "d_res_1b")
    dh5, per_layer['xa_wq'][1], per_layer['xa_wkv'][1], per_layer['xa_wo'][1] = attention_bwd(1, h5, dao1, sv1)
    ex.put_grads('l1', G_L1, {(k, 1): v[1] for k, v in per_layer.items()})
    dx3, dm1, gain_grads[(1, 1)], gain_grads[(1, 2)] = bwd_seg_res(x3, m1, gain(1, 1), gain(1, 2), dx4, dh5, "d_res_1a")
    g_cd_out = matmul(mix1, dm1, 'tn', "d_cd_w_out", BF)
    dmix1 = matmul(dm1, big[('cd_w_out', 0)], 'nt', "d_mix1")
    dvconv, dlg, dlb = bwd_call(seg_ln, "d_conf_ln", (nb,), [vconv, p['conf_ln_g'], p['conf_ln_b']],
                                [_rows(D), _par(D), _par(D)], [dmix1], [_rows(D, 0)], [0, 1, 2],
                                [_sd((t, D)), _sd((1, D)), _sd((1, D))], [_rows(D), _par(D), _par(D)], [None, (0,), (0,)])
    grads['conf_ln_g'], grads['conf_ln_b'] = dlg, dlb
    cd_g = bwd_call(cd1_fn, "d_cd_conv", (nd, bsz), cd_ins, cd_in_specs, [dvconv, dmix1],
                    [cd_out_spec, pl.BlockSpec((seq, LANE), lambda j, b: (b, nd + j))], list(range(4)),
                    [_sd((t, CD_IN), BF), _sd((CONF_K, D)), _sd((1, D)), _sd((SC_K, D))], [cd_u_spec] + cd_par,
                    [None, (1,), (1,), (1,)])
    du1 = cd_g[0]
    grads['conf_dw_w'], grads['conf_dw_b'], grads['sc_conv_w'] = cd_g[1][None], cd_g[2], cd_g[3][None]
    g_cd_in = matmul(du1, h4, 'tn', "d_cd_w_in", BF).reshape(nd, 5, LANE, D).transpose(1, 0, 2, 3).reshape(CD_IN, D)
    ex.put_grads('cd', G_CD, {('cd_w_in', 0): g_cd_in, ('cd_w_out', 0): g_cd_out})
    dh4 = matmul(du1, w_cd_in, 'nn', "d_h_cd")

    dx2, dmo0, gain_grads[(0, 5)], gain_grads[(1, 0)] = bwd_seg_res(x2, mo0, gain(0, 5), gain(1, 0), dx3, dh4, "d_res_0c")
    dh3, per_layer['mlp_w1'][0], per_layer['mlp_w2'][0] = mlp_bwd(0, h3, dmo0, sv)
    dx1, dao0, gain_grads[(0, 3)], gain_grads[(0, 4)] = bwd_seg_res(x1, ao0, gain(0, 3), gain(0, 4), dx2, dh3, "d_res_0b")
    dh2, per_layer['xa_wq'][0], per_layer['xa_wkv'][0], per_layer['xa_wo'][0] = attention_bwd(0, h2, dao0, sv)
    ex.put_grads('l0', G_L0, {(k, 0): v[0] for k, v in per_layer.items()})
    dx0r, dm0, gain_grads[(0, 1)], gain_grads[(0, 2)] = bwd_seg_res(x0, m0, gain(0, 1), gain(0, 2), dx1, dh2, "d_res_0a")
    g_ab_out = matmul(mix0, dm0, 'tn', "d_ab_w_out", BF)
    dmix0 = matmul(dm0, big[('ab_w_out', 0)], 'nt', "d_mix0")
    dxbc_act, dz, ddt, ddtb, dalog, ddsk, dnw = ssd_bwd(xbc_act, u0, dtb, alog, dsk, p['ssm_norm'], consts, hs, dmix0, bsz, seq)
    grads['ssm_dt_bias'] = ddtb[:, :SSM_HEADS]
    grads['ssm_a_log'] = dalog[:, :SSM_HEADS]
    grads['ssm_d'] = ddsk[:, :SSM_HEADS]
    grads['ssm_norm'] = dnw
    dxr, dcw, dcb = bwd_call(conv4_fn, "d_ssm_conv", (ncb, bsz), [u0, conv_w, conv_b], conv_in_specs,
                             [dxbc_act], [conv_out_spec], [0, 1, 2],
                             [_sd((t, SSM_CONV_DIM), BF), _sd((SSM_CONV, SSM_CONV_DIM)), _sd((1, SSM_CONV_DIM))],
                             [conv_out_spec, conv_in_specs[1], conv_in_specs[2]], [None, (1,), (1,)])
    grads['ssm_conv_w'], grads['ssm_conv_b'] = _xbc_ungroup(dcw, 1)[None], _xbc_ungroup(dcb, 1)
    dpool, dpw, dps = [], [], []
    for g in range(POOL_GROUPS):
        seqspec = pl.BlockSpec((seq, PG), lambda b, g=g: (b, g))
        one = pl.BlockSpec((seq, PG), lambda b: (b, 0))
        wspec = pl.BlockSpec((PG, PG), lambda b: (0, 0))
        sspec = pl.BlockSpec((1, PG), lambda b, g=g: (0, g))
        a, bb, c = bwd_call(make_pool_fn(g), f"d_pool_{g}", (bsz,), [u0, p['pool_w'][0, g], p['pool_scale']],
                            [seqspec, wspec, sspec], [dmix0], [seqspec], [0, 1, 2],
                            [_sd((t, PG), BF), _sd((PG, PG)), _sd((1, PG))], [one, wspec, pl.BlockSpec((1, PG), lambda b: (0, 0))],
                            [None, (0,), (0,)])
        dpool.append(a)
        dpw.append(bb)
        dps.append(c)
    grads['pool_w'] = jnp.stack(dpw)[None]
    grads['pool_scale'] = jnp.concatenate(dps, axis=1)
    du0 = jnp.concatenate(dpool + [dz, dxr, ddt.astype(BF)], axis=1)
    g_ab_in = matmul(du0, h0, 'tn', "d_ab_w_in", BF)
    g_ab_in = jnp.concatenate([g_ab_in[:xbc0], _xbc_ungroup(g_ab_in[xbc0:xbc0 + SSM_CONV_DIM], 0),
                               g_ab_in[xbc0 + SSM_CONV_DIM:AB_IN]], axis=0)
    ex.put_grads('ab', G_AB, {('ab_w_in', 0): g_ab_in, ('ab_w_out', 0): g_ab_out})
    dh0 = matmul(du0, w_ab_in, 'nn', "d_h_ab", after=ex.take_tokens())
    dx, dg00 = bwd_call(seg_in_res, "d_norm_in", (nb,), [x0, gain(0, 0)], [_rows(D), _par(D)], [dx0r, dh0],
                        [_rows(D), _rows(D)], [0, 1], [_sd((t, D)), _sd((1, D))], [_rows(D), _par(D)], [None, (0,)])
    gain_grads[(0, 0)] = dg00
    grads['norm_gains'] = jnp.stack([jnp.concatenate([gain_grads[(l, i)] for i in range(6)], axis=0) for l in range(2)])
    return loss, dx, grads
```

```python
import functools
import math

import numpy as np
import jax
import jax.numpy as jnp
from jax import lax
from jax.experimental import pallas as pl
from jax.experimental.pallas import tpu as pltpu

BF = jnp.bfloat16
F32 = jnp.float32

N_DEV = 8
D = 1024
N_MEM = 256
XA_HEADS = 4
XA_DH = D // XA_HEADS
POOL_GROUPS = 4
PG = 128
POOL_W = POOL_GROUPS * PG
SSM_INNER = 1024
SSM_GROUPS = 2
SSM_GSZ = SSM_INNER // SSM_GROUPS
SSM_HEADS = 16
SSM_P = 64
SSM_N = 128
SSM_CONV = 4
SSM_CONV_DIM = SSM_INNER + 2 * SSM_GROUPS * SSM_N
SSM_XBC_G = SSM_GSZ + 2 * SSM_N
CHUNK = 128
AB_IN = POOL_W + SSM_INNER + SSM_CONV_DIM + SSM_HEADS
AB_IN_PAD = POOL_W + SSM_INNER + SSM_CONV_DIM + 128
AB_OUT = POOL_W + SSM_INNER
CONF_K = 31
SC_K = 3
CD_IN = 5 * D
CD_OUT = 2 * D
MLP_H = 4 * D
RMS_EPS = 1e-6
LN_EPS = 1e-5
ADAM_LR = 0.001
ADAM_B1 = 0.9
ADAM_B2 = 0.999
ADAM_EPS = 1e-08
ADAM_WD = 0.01
ADAM_STEP = 10
VMEM_LIMIT = 56 * 1024 * 1024
LANE = 128

NAMES = ['x', 'mem', 'norm_gains', 'xa_wq', 'xa_wkv', 'xa_wo', 'mlp_w1', 'mlp_w2', 'ab_w_in', 'pool_w', 'pool_scale',
         'ssm_conv_w', 'ssm_conv_b', 'ssm_dt_bias', 'ssm_a_log', 'ssm_d', 'ssm_norm', 'ab_w_out', 'cd_w_in', 'conf_dw_w',
         'conf_dw_b', 'conf_ln_g', 'conf_ln_b', 'sc_conv_w', 'cd_w_out', 'loss_target']
WEIGHTS = NAMES[2:25]
BIG = [('xa_wq', 1), ('xa_wkv', 2), ('xa_wo', 1), ('mlp_w1', 2), ('mlp_w2', 1), ('cd_w_in', 2), ('cd_w_out', 1),
       ('ab_w_out', 1), ('ab_w_in', 2)]
SMALL_SHARDED = ['norm_gains', 'ssm_conv_w', 'conf_dw_w', 'conf_dw_b', 'conf_ln_g', 'conf_ln_b', 'sc_conv_w']
REPLICATED = ['pool_w', 'pool_scale', 'ssm_conv_b', 'ssm_dt_bias', 'ssm_a_log', 'ssm_d', 'ssm_norm']


def _dg(a, b, ca, cb, prec=None):
    return lax.dot_general(a, b, (((ca,), (cb,)), ((), ())), precision=prec, preferred_element_type=F32)


@functools.partial(jax.custom_vjp, nondiff_argnums=(2, 3))
def bdot(a, b, ca, cb):
    return _dg(a.astype(BF), b.astype(BF), ca, cb)


def _bdot_fwd(a, b, ca, cb):
    return bdot(a, b, ca, cb), (a, b)


def _bdot_bwd(ca, cb, res, g):
    a, b = res
    g16, a16, b16 = g.astype(BF), a.astype(BF), b.astype(BF)
    da = _dg(g16, b16, 1, 1 - cb) if ca == 1 else _dg(b16, g16, 1 - cb, 1)
    db = _dg(g16, a16, 0, 1 - ca) if cb == 1 else _dg(a16, g16, 1 - ca, 0)
    return da.astype(a.dtype), db.astype(b.dtype)


bdot.defvjp(_bdot_fwd, _bdot_bwd)


def _split3(a):
    a1 = a.astype(BF)
    r1 = a - a1.astype(F32)
    a2 = r1.astype(BF)
    a3 = (r1 - a2.astype(F32)).astype(BF)
    return a1, a2, a3


def _exact_right(a, c):
    m = a.shape[0]
    if m % 16:
        return sum(_dg(p, c, 1, 0) for p in _split3(a))
    o = _dg(jnp.concatenate(_split3(a), axis=0), c, 1, 0)
    return o[:m] + o[m:2 * m] + o[2 * m:]


def _exact_left(c, a):
    n = a.shape[1]
    o = _dg(c, jnp.concatenate(_split3(a), axis=1), 1, 0)
    return o[:, :n] + o[:, n:2 * n] + o[:, 2 * n:]


@jax.custom_vjp
def cmat(a, c, ct):
    return _exact_right(a, c)


def _cmat_fwd(a, c, ct):
    return cmat(a, c, ct), (c, ct)


def _cmat_bwd(res, g):
    c, ct = res
    return _exact_right(g, ct), jnp.zeros_like(c), jnp.zeros_like(ct)


cmat.defvjp(_cmat_fwd, _cmat_bwd)


@jax.custom_vjp
def cmatl(c, ct, a):
    return _exact_left(c, a)


def _cmatl_fwd(c, ct, a):
    return cmatl(c, ct, a), (c, ct)


def _cmatl_bwd(res, g):
    c, ct = res
    return jnp.zeros_like(c), jnp.zeros_like(ct), _exact_left(ct, g)


cmatl.defvjp(_cmatl_fwd, _cmatl_bwd)


SUBLANES = 8


def _taps(x, shifts, down):
    n, c = x.shape
    pad = _round_up(max(shifts), SUBLANES)
    if pad == 0:
        return {0: x}
    zeros = jnp.zeros((pad, c), x.dtype)
    xp = jnp.concatenate([zeros, x] if down else [x, zeros], axis=0)
    rolled, out = {0: xp}, {}
    for s in shifts:
        a, b = divmod(s, SUBLANES)
        if b not in rolled:
            rolled[b] = pltpu.roll(xp, b if down else n + pad - b, 0)
        off = pad - SUBLANES * a if down else SUBLANES * a
        out[s] = rolled[b][off:off + n]
    return out


def _shift_down(x, k):
    return _taps(x, [k], True)[k]


def _shift_up(x, k):
    return _taps(x, [k], False)[k]


@functools.partial(jax.custom_vjp, nondiff_argnums=(1,))
def shift(x, k):
    return _shift_down(x, k)


def _shift_fwd(x, k):
    return _shift_down(x, k), None


def _shift_bwd(k, _, g):
    return (_shift_up(g, k),)


shift.defvjp(_shift_fwd, _shift_bwd)


@functools.partial(jax.custom_vjp, nondiff_argnums=(2,))
def cconv(u, w, width):
    taps = _taps(u, list(range(width)), True)
    acc = u * w[width - 1:width, :]
    for k in range(width - 1):
        acc = acc + taps[width - 1 - k] * w[k:k + 1, :]
    return acc


def _cconv_fwd(u, w, width):
    return cconv(u, w, width), (u, w)


def _cconv_bwd(width, res, g):
    u, w = res
    rows = lax.broadcasted_iota(jnp.int32, w.shape, 0)
    du = g * w[width - 1:width, :]
    dw = jnp.where(rows == width - 1, jnp.sum(g * u, axis=0, keepdims=True), 0.0)
    g_taps = _taps(g, list(range(width)), False)
    u_taps = _taps(u, list(range(width)), True)
    for k in range(width - 1):
        s = width - 1 - k
        du = du + g_taps[s] * w[k:k + 1, :]
        dw = dw + jnp.where(rows == k, jnp.sum(g * u_taps[s], axis=0, keepdims=True), 0.0)
    return du, dw


cconv.defvjp(_cconv_fwd, _cconv_bwd)


def _rms(x, g):
    return x * lax.rsqrt(jnp.mean(x * x, axis=-1, keepdims=True) + RMS_EPS) * g


def _params(sem=None):
    return pltpu.CompilerParams(dimension_semantics=sem, vmem_limit_bytes=VMEM_LIMIT)


def _f32(v):
    return v if v.dtype == F32 else v.astype(F32)


def _first(axes):
    ok = None
    for ax in axes:
        c = pl.program_id(ax) == 0
        ok = c if ok is None else jnp.logical_and(ok, c)
    return ok


def fwd_call(fn, name, grid, ins, in_specs, out_shapes, out_specs, into=None):
    n_in = len(ins)
    n_into = 0 if into is None else 1

    def body(*refs):
        outs = fn(*[_f32(r[...]) for r in refs[:n_in]])
        for r, o in zip(refs[n_in + n_into:], outs):
            r[...] = o.astype(r.dtype)

    extra = [] if into is None else [into]
    return pl.pallas_call(body, name=name, grid=grid, in_specs=list(in_specs) + [pl.BlockSpec(memory_space=pl.ANY)] * n_into,
                          out_specs=out_specs, out_shape=out_shapes, input_output_aliases={n_in: 0} if n_into else {},
                          compiler_params=_params())(*ins, *extra)


def bwd_call(fn, name, grid, ins, in_specs, cots, cot_specs, gidx, g_shapes, g_specs, g_acc):
    n_in, n_cot = len(ins), len(cots)

    def body(*refs):
        vals = [_f32(r[...]) for r in refs[:n_in]]

        def f_sel(*dv):
            full = list(vals)
            for i, v in zip(gidx, dv):
                full[i] = v
            return tuple(fn(*full))

        outs, vjp = jax.vjp(f_sel, *[vals[i] for i in gidx])
        cts = tuple(_f32(r[...]) for r in refs[n_in:n_in + n_cot])
        grads = vjp(cts)
        for r, g, acc in zip(refs[n_in + n_cot:], grads, g_acc):
            if acc is None:
                r[...] = g.astype(r.dtype)
            else:
                @pl.when(_first(acc))
                def _():
                    r[...] = jnp.zeros_like(r)

                r[...] += g.astype(r.dtype)

    return pl.pallas_call(body, name=name, grid=grid, in_specs=list(in_specs) + list(cot_specs), out_specs=g_specs,
                          out_shape=g_shapes, compiler_params=_params())(*ins, *cots)


def _tile(dim, pref):
    if dim <= pref:
        return dim
    best = None
    for t in range(LANE, pref + 1, LANE):
        if dim % t == 0:
            best = t
    assert best is not None, dim
    return best


MATMUL_VMEM_BUDGET = 40 * 1024 * 1024


def _matmul_tiles(m, n, k, a_bytes, b_bytes, out_bytes):
    tn = _tile(n, 1024)
    for tk_pref in (k, 2048, 1024, 512):
        tk = _tile(k, tk_pref)
        for tm_pref in (1024, 512, 256):
            tm = _tile(m, tm_pref)
            need = 2 * (tm * tk * a_bytes + tk * tn * b_bytes + tm * tn * out_bytes) + (0 if tk == k else tm * tn * 4)
            need += (tm * tk * 2 if a_bytes == 4 else 0) + (tk * tn * 2 if b_bytes == 4 else 0)
            if need <= MATMUL_VMEM_BUDGET:
                return tm, tn, tk
    raise ValueError((m, n, k))


def matmul(a, b, mode, name, out_dtype=F32, epilogue=None, extras=(), params=(), after=()):
    if mode == 'nn':
        (m, k), (k2, n) = a.shape, b.shape
    elif mode == 'nt':
        (m, k), (n, k2) = a.shape, b.shape
    else:
        (k, m), (k2, n) = a.shape, b.shape
    assert k == k2, (name, a.shape, b.shape)
    n_extra = len(extras) + len(params)
    out_dtypes = out_dtype if isinstance(out_dtype, tuple) else (out_dtype,)
    per_out = sum(jnp.dtype(dt).itemsize for dt in out_dtypes) + sum(e.dtype.itemsize for e in extras)
    tm, tn, tk = _matmul_tiles(m, n, k, a.dtype.itemsize, b.dtype.itemsize, per_out)
    nk = k // tk
    ca = 0 if mode == 'tn' else 1
    cb = 1 if mode == 'nt' else 0
    a_spec = pl.BlockSpec((tk, tm), lambda i, j, kk: (kk, i)) if mode == 'tn' else pl.BlockSpec((tm, tk), lambda i, j, kk: (i, kk))
    b_spec = pl.BlockSpec((tn, tk), lambda i, j, kk: (j, kk)) if mode == 'nt' else pl.BlockSpec((tk, tn), lambda i, j, kk: (kk, j))

    def finish(o_refs, extra_refs, acc):
        outs = (acc,) if epilogue is None else epilogue(acc, *[_f32(e[...]) for e in extra_refs])
        for o_ref, o in zip(o_refs, outs):
            o_ref[...] = o.astype(o_ref.dtype)

    n_after = len(after)

    def body_whole_k(a_ref, b_ref, *refs):
        refs = refs[n_after:]
        finish(refs[n_extra:], refs[:n_extra], _dg(a_ref[...].astype(BF), b_ref[...].astype(BF), ca, cb))

    def body_split_k(a_ref, b_ref, *refs):
        refs = refs[n_after:]
        extra_refs, o_refs, acc = refs[:n_extra], refs[n_extra:-1], refs[-1]
        kk = pl.program_id(2)

        @pl.when(kk == 0)
        def _():
            acc[...] = jnp.zeros_like(acc)

        acc[...] += _dg(a_ref[...].astype(BF), b_ref[...].astype(BF), ca, cb)

        @pl.when(kk == nk - 1)
        def _():
            finish(o_refs, extra_refs, acc[...])

    tile = pl.BlockSpec((tm, tn), lambda i, j, kk: (i, j))
    row = pl.BlockSpec((1, tn), lambda i, j, kk: (0, j))
    n_par = len(params)
    outs = pl.pallas_call(
        body_whole_k if nk == 1 else body_split_k, name=name, grid=(m // tm, n // tn, nk),
        in_specs=[a_spec, b_spec] + [pl.BlockSpec(memory_space=pl.ANY)] * n_after + [tile] * len(extras) + [row] * n_par,
        out_specs=[tile] * len(out_dtypes),
        out_shape=[jax.ShapeDtypeStruct((m, n), dt) for dt in out_dtypes],
        scratch_shapes=[] if nk == 1 else [pltpu.VMEM((tm, tn), F32)],
        compiler_params=_params(("parallel", "parallel", "arbitrary")))(a, b, *after, *extras, *params)
    return outs if isinstance(out_dtype, tuple) else outs[0]


_FLIPS = [(0, 0, 1), (1, 0, 0), (0, 1, 0), (1, 1, 0), (1, 0, 1), (0, 1, 1), (1, 1, 1)]


def _me():
    return lax.axis_index("x"), lax.axis_index("y"), lax.axis_index("c")


def _flip(pos, f):
    return tuple(jnp.where(fi == 1, 1 - p, p) if fi else p for p, fi in zip(pos, f))


def _slot(pos):
    return 4 * pos[0] + 2 * pos[1] + pos[2]


def all_gather(v, name):
    def body(v_ref, out_ref, send_sems, recv_sems, local_sem):
        me = _me()
        sibling = _flip(me, (0, 0, 1))
        chips = [_flip(me, f) for f in ((1, 0, 0), (0, 1, 0), (1, 1, 0))]

        def copy(k, block, to, src=None):
            return pltpu.make_async_remote_copy(
                src_ref=out_ref.at[_slot(block)] if src is None else src, dst_ref=out_ref.at[_slot(block)],
                send_sem=send_sems.at[k], recv_sem=recv_sems.at[k], device_id=to, device_id_type=pl.DeviceIdType.MESH)

        mine = pltpu.make_async_copy(v_ref, out_ref.at[_slot(me)], local_sem)
        mine.start()
        first = [copy(0, me, sibling, src=v_ref)] + [copy(1 + j, me, chip, src=v_ref) for j, chip in enumerate(chips)]
        for cp in first:
            cp.start()
        passed = [copy(4 + j, chip, sibling) for j, chip in enumerate(chips)]
        for j, chip in enumerate(chips):
            copy(1 + j, chip, me).wait_recv()
            passed[j].start()
        copy(0, sibling, me).wait_recv()
        for j, chip in enumerate(chips):
            copy(4 + j, _flip(chip, (0, 0, 1)), me).wait_recv()
        for cp in first + passed:
            cp.wait_send()
        mine.wait()

    return pl.pallas_call(
        body, name=name, out_shape=jax.ShapeDtypeStruct((N_DEV,) + v.shape, v.dtype),
        in_specs=[pl.BlockSpec(memory_space=pl.ANY)], out_specs=pl.BlockSpec(memory_space=pl.ANY),
        scratch_shapes=[pltpu.SemaphoreType.DMA((7,)), pltpu.SemaphoreType.DMA((7,)), pltpu.SemaphoreType.DMA(())],
    )(v)


def sum_slots(v, name, tr=256):
    _, r, c = v.shape
    tr = _tile_rows(r, tr)

    def body(v_ref, o_ref):
        acc = v_ref[0].astype(F32)
        for s in range(1, N_DEV):
            acc = acc + v_ref[s].astype(F32)
        o_ref[...] = acc

    return pl.pallas_call(body, name=name, grid=(r // tr,), in_specs=[pl.BlockSpec((N_DEV, tr, c), lambda i: (0, i, 0))],
                          out_specs=pl.BlockSpec((tr, c), lambda i: (i, 0)), out_shape=jax.ShapeDtypeStruct((r, c), F32),
                          compiler_params=_params())(v)


def _tile_rows(r, pref):
    if r <= pref:
        return r
    best = None
    for t in range(8, pref + 1, 8):
        if r % t == 0:
            best = t
    return r if best is None else best


def _adamw_math(w, m, v, g):
    nm = ADAM_B1 * m + (1.0 - ADAM_B1) * g
    nv = ADAM_B2 * v + (1.0 - ADAM_B2) * jnp.square(g)
    m_hat = nm / (1.0 - ADAM_B1 ** ADAM_STEP)
    v_hat = nv / (1.0 - ADAM_B2 ** ADAM_STEP)
    return -ADAM_LR * (m_hat / (jnp.sqrt(v_hat) + ADAM_EPS) + ADAM_WD * w), nm, nv


def update_from_slots(lands, offs, w, m, v, transposed, name):
    layers, a, b = w.shape
    n_land = len(lands)
    if transposed:
        rb, tk = LANE, 512
        assert a % tk == 0 and b % rb == 0 and all(o % rb == 0 for o in offs), (name, w.shape, offs)
        grid = (layers, a // tk, b // rb)
        land_block = (N_DEV, rb, tk)
        tile = pl.BlockSpec((None, tk, rb), lambda l, i, j: (l, i, j))

        def land_spec(layer):
            base = offs[layer] // rb
            return pl.BlockSpec(land_block, lambda l, i, j: (0, base + jnp.where(l == layer, j, 0), jnp.where(l == layer, i, 0)))
    else:
        fits = [t for t in (256, 128, 64) if a % t == 0 and all(o % t == 0 for o in offs)]
        assert fits or all(o == 0 for o in offs), (name, w.shape, offs)
        tr = max(fits) if fits else a
        grid = (layers, a // tr)
        land_block = (N_DEV, _round_up(tr, MEMBER_ROW_TILE), b)
        tile = pl.BlockSpec((None, tr, b), lambda l, i: (l, i, 0))

        def land_spec(layer):
            base = offs[layer] // tr
            return pl.BlockSpec(land_block, lambda l, i: (0, base + jnp.where(l == layer, i, 0), 0))

    def body(*refs):
        land_refs, (w_ref, m_ref, v_ref, g_ref, d_ref, nm_ref, nv_ref, acc) = refs[:n_land], refs[n_land:]
        for layer, land in enumerate(land_refs):
            @pl.when(pl.program_id(0) == layer)
            def _(land=land):
                rows = acc.shape[0]
                s = land[0, :rows].astype(F32)
                for k in range(1, N_DEV):
                    s = s + land[k, :rows].astype(F32)
                acc[...] = s

        g = acc[...].T if transposed else acc[...]
        d, nm, nv = _adamw_math(w_ref[...], m_ref[...], v_ref[...], g)
        g_ref[...] = g
        d_ref[...] = d
        nm_ref[...] = nm
        nv_ref[...] = nv

    sh = jax.ShapeDtypeStruct(w.shape, F32)
    return pl.pallas_call(
        body, name=name, grid=grid, in_specs=[land_spec(layer) for layer in range(n_land)] + [tile] * 3, out_specs=[tile] * 4,
        out_shape=[sh] * 4, scratch_shapes=[pltpu.VMEM((rb, tk) if transposed else (tr, b), F32)],
        compiler_params=_params())(*lands, w, m, v)


def adamw_many(ws, ms, vs, gs, name):
    n = len(ws)

    def body(*refs):
        for i in range(n):
            d, nm, nv = _adamw_math(refs[i][...], refs[n + i][...], refs[2 * n + i][...], refs[3 * n + i][...])
            refs[4 * n + i][...] = d
            refs[5 * n + i][...] = nm
            refs[6 * n + i][...] = nv

    vmem = pl.BlockSpec(memory_space=pltpu.VMEM)
    shapes = [jax.ShapeDtypeStruct(a.shape, F32) for a in ws]
    res = pl.pallas_call(body, name=name, in_specs=[vmem] * (4 * n), out_specs=[vmem] * (3 * n), out_shape=shapes * 3,
                         compiler_params=_params())(*ws, *ms, *vs, *gs)
    return res[:n], res[n:2 * n], res[2 * n:]


def seg_in(x, g):
    return (_rms(x, g),)


def seg_in_res(x, g):
    return x, _rms(x, g)


def seg_res(x, m, ga, gb):
    x1 = x + _rms(m, ga)
    return x1, _rms(x1, gb)


def seg_out(x, m, ga):
    return (x + _rms(m, ga),)


def act_epilogue(r):
    t = jnp.maximum(r, 0.0)
    return r, t * t


def res_epilogue(m, x, ga, gb):
    x1, h = seg_res(x, m, ga, gb)
    return m, x1, h


def act_bwd_epilogue(drr, r):
    return (drr * (2.0 * jnp.maximum(r, 0.0)),)


def seg_ln(v, g, b):
    mu = jnp.mean(v, axis=-1, keepdims=True)
    var = jnp.mean(jnp.square(v - mu), axis=-1, keepdims=True)
    vn = (v - mu) * lax.rsqrt(var + LN_EPS) * g + b
    return (jax.nn.silu(vn),)


def make_pool_fn(group):
    window = 2 ** (group + 1)

    def pool_fn(ug, pw, scale):
        s = ug
        for lvl in range(group + 1):
            s = s + shift(s, 2 ** lvl)
        cnt = jnp.minimum(lax.broadcasted_iota(jnp.int32, ug.shape, 0) + 1, window).astype(F32)
        return (bdot(s / cnt - ug, pw, 1, 0) * scale,)

    return pool_fn


def conv4_fn(xr, w, b):
    return (jax.nn.silu(cconv(xr, w, SSM_CONV) + b),)


def cd1_fn(u, dww, dwb, scw):
    val, gate, bg, cg, hh = (u[:, k * LANE:(k + 1) * LANE] for k in range(5))
    v = val * jax.nn.sigmoid(gate)
    vc = cconv(v, dww, CONF_K) + dwb
    sc = bg * cconv(cg * hh, scw, SC_K)
    return vc, sc


def attn_fn(q, kv):
    outs = []
    for h in range(XA_HEADS):
        cols = slice(h * XA_DH, (h + 1) * XA_DH)
        s = bdot(q[:, cols], kv[:, cols], 1, 1) / math.sqrt(XA_DH)
        p = jax.nn.softmax(s, axis=-1)
        outs.append(bdot(p, kv[:, D + h * XA_DH:D + (h + 1) * XA_DH], 1, 0))
    return (jnp.concatenate(outs, axis=1),)


def ssd_chunk(xbc, z, dtraw, dtb, alog, dsk, nw, h0, h1, h2, h3, e64, e64t, ecat, ecatt, tril, trilt):
    xs, bm, cm = xbc[:, :SSM_GSZ], xbc[:, SSM_GSZ:SSM_GSZ + SSM_N], xbc[:, SSM_GSZ + SSM_N:]
    hin = (h0, h1, h2, h3)
    dt = jax.nn.softplus(dtraw + dtb)
    a = -jnp.exp(alog)
    d_a = dt * a
    cs = cmatl(tril, trilt, d_a)
    cs_cat = cmat(cs, ecat, ecatt)
    cs64, cs128 = cs_cat[:, :SSM_GSZ], cs_cat[:, SSM_GSZ:]
    dt64 = cmat(dt, e64, e64t)
    row = lax.broadcasted_iota(jnp.int32, (8, LANE), 0)
    heads = jnp.where(row == 0, dsk, jnp.where(row == 1, jnp.sum(d_a, axis=0, keepdims=True), 0.0))
    heads64 = cmat(heads, e64, e64t)
    d64, tot64 = heads64[0:1, :], heads64[1:2, :]
    xdt = xs * dt64
    cb = bdot(cm, bm, 1, 1)
    li = lax.broadcasted_iota(jnp.int32, (CHUNK, CHUNK), 0)
    si = lax.broadcasted_iota(jnp.int32, (CHUNK, CHUNK), 1)
    causal = li >= si
    lane = lax.broadcasted_iota(jnp.int32, (CHUNK, LANE), 1)
    xw = xdt * jnp.exp(tot64 - cs64)
    ecs = jnp.exp(cs64)
    etot = jnp.exp(tot64)
    ycols, hout = [], []
    for j in range(4):
        sl = slice(j * LANE, (j + 1) * LANE)
        xj = xdt[:, sl]
        ys = []
        for hh in range(2):
            r = 2 * j + hh
            col = cs128[:, r * LANE:(r + 1) * LANE]
            decay = jnp.exp(jnp.where(causal, col - col.T, -1e30))
            ys.append(bdot(cb * decay, xj, 1, 0))
        y_diag = jnp.where(lane < SSM_P, ys[0], ys[1])
        y_off = bdot(cm, hin[j], 1, 0) * ecs[:, sl]
        ycols.append(y_diag + y_off)
        hout.append(etot[:, sl] * hin[j] + bdot(bm, xw[:, sl], 0, 0))
    y = jnp.concatenate(ycols, axis=1) + d64 * xs
    y = y * jax.nn.silu(z)
    yn = y * lax.rsqrt(jnp.mean(y * y, axis=-1, keepdims=True) + RMS_EPS) * nw
    return (yn,) + tuple(hout)


def _xbc_group(a, axis):
    parts = []
    for g in range(SSM_GROUPS):
        for start, width in ((g * SSM_GSZ, SSM_GSZ), (SSM_INNER + g * SSM_N, SSM_N), (SSM_INNER + (SSM_GROUPS + g) * SSM_N, SSM_N)):
            parts.append(lax.slice_in_dim(a, start, start + width, axis=axis))
    return jnp.concatenate(parts, axis=axis)


def _xbc_ungroup(a, axis):
    xs, bs, cs = [], [], []
    for g in range(SSM_GROUPS):
        base = g * SSM_XBC_G
        xs.append(lax.slice_in_dim(a, base, base + SSM_GSZ, axis=axis))
        bs.append(lax.slice_in_dim(a, base + SSM_GSZ, base + SSM_GSZ + SSM_N, axis=axis))
        cs.append(lax.slice_in_dim(a, base + SSM_GSZ + SSM_N, base + SSM_XBC_G, axis=axis))
    return jnp.concatenate(xs + bs + cs, axis=axis)


def _ssd_consts():
    h = np.arange(LANE)[:, None]
    e64 = np.stack([(h == g * 8 + np.arange(SSM_GSZ)[None, :] // SSM_P) for g in range(SSM_GROUPS)]).astype(np.float32)
    e128 = np.stack([(h == g * 8 + np.arange(8 * LANE)[None, :] // LANE) for g in range(SSM_GROUPS)]).astype(np.float32)
    ecat = np.concatenate([e64, e128], axis=2)
    tril = np.tril(np.ones((CHUNK, CHUNK), np.float32))
    return tuple(jnp.asarray(c, dtype=BF) for c in (e64, e64.transpose(0, 2, 1), ecat, ecat.transpose(0, 2, 1), tril, tril.T))


def _ssd_specs(nc, rev):
    def ci(c):
        return nc - 1 - c if rev else c

    def row(width, col):
        return pl.BlockSpec((CHUNK, width), lambda b, c: (b * nc + ci(c), col))

    def whole(shape):
        return pl.BlockSpec(shape, lambda b, c: (0,) * len(shape))

    data = [row(SSM_CONV_DIM, 0),
            row(SSM_GSZ, 1), row(SSM_GSZ, 2), row(LANE, 24)]
    par = [whole((1, LANE))] * 3 + [whole((1, SSM_INNER))]
    cst = [whole((SSM_GROUPS, LANE, SSM_GSZ)), whole((SSM_GROUPS, SSM_GSZ, LANE)), whole((SSM_GROUPS, LANE, 12 * LANE)),
           whole((SSM_GROUPS, 12 * LANE, LANE)), whole((CHUNK, CHUNK)), whole((CHUNK, CHUNK))]
    hsave = pl.BlockSpec((None, None, SSM_GROUPS, 4, SSM_N, LANE), lambda b, c: (b, ci(c), 0, 0, 0, 0))
    return data, par, cst, hsave, row, whole


def _ssd_group_args(g, xbc, z, dtr, dtb, alog, dsk, nw):
    return (xbc[:, g * SSM_XBC_G:(g + 1) * SSM_XBC_G], z[g], dtr, dtb, alog, dsk, nw[:, g * SSM_GSZ:(g + 1) * SSM_GSZ])


def ssd_fwd(xbc_act, u, dtb, alog, dsk, nw, consts, bsz, seq):
    nc = seq // CHUNK
    data, par, cst, hsave, row, _ = _ssd_specs(nc, False)

    def body(xbc, z0, z1, dtr, dtb_r, alog_r, dsk_r, nw_r, e64, e64t, ecat, ecatt, tril, trilt, yn_ref, hs_ref, h):
        @pl.when(pl.program_id(1) == 0)
        def _():
            h[...] = jnp.zeros_like(h)

        hs_ref[...] = h[...]
        ys = []
        for g in range(SSM_GROUPS):
            args = _ssd_group_args(g, xbc[...], (z0[...], z1[...]), dtr[...], dtb_r[...], alog_r[...], dsk_r[...], nw_r[...])
            outs = ssd_chunk(*args, h[g, 0], h[g, 1], h[g, 2], h[g, 3], e64[g], e64t[g], ecat[g], ecatt[g], tril[...], trilt[...])
            ys.append(outs[0])
            for j in range(4):
                h[g, j] = outs[1 + j]
        yn_ref[...] = jnp.concatenate(ys, axis=1).astype(yn_ref.dtype)

    t = bsz * seq
    return pl.pallas_call(
        body, name="ssd_fwd", grid=(bsz, nc), in_specs=data + par + cst, out_specs=[row(SSM_INNER, 0), hsave],
        out_shape=[jax.ShapeDtypeStruct((t, SSM_INNER), BF), jax.ShapeDtypeStruct((bsz, nc, SSM_GROUPS, 4, SSM_N, LANE), F32)],
        scratch_shapes=[pltpu.VMEM((SSM_GROUPS, 4, SSM_N, LANE), F32)], compiler_params=_params(),
    )(xbc_act, u, u, u, dtb, alog, dsk, nw, *consts)


def ssd_bwd(xbc_act, u, dtb, alog, dsk, nw, consts, hs, dmix, bsz, seq):
    nc = seq // CHUNK
    data, par, cst, hsave, row, whole = _ssd_specs(nc, True)
    t = bsz * seq
    pcol = POOL_W // SSM_GSZ

    def body(xbc, z0, z1, dtr, dtb_r, alog_r, dsk_r, nw_r, e64, e64t, ecat, ecatt, tril, trilt, hs_ref, dy0, dy1,
             dxbc, dz, ddt, ddtb, dalog, ddsk, dnw, dh):
        @pl.when(pl.program_id(1) == 0)
        def _():
            dh[...] = jnp.zeros_like(dh)

        per_group = []
        for g, dyn in enumerate((dy0, dy1)):
            cst_vals = (e64[g], e64t[g], ecat[g], ecatt[g], tril[...], trilt[...])
            prim = _ssd_group_args(g, xbc[...], (z0[...], z1[...]), dtr[...], dtb_r[...], alog_r[...], dsk_r[...], nw_r[...])
            prim = prim + (hs_ref[g, 0], hs_ref[g, 1], hs_ref[g, 2], hs_ref[g, 3])
            _, vjp = jax.vjp(lambda *args, c=cst_vals: ssd_chunk(*args, *c), *prim)
            gr = vjp((dyn[...].astype(F32), dh[g, 0], dh[g, 1], dh[g, 2], dh[g, 3]))
            for j in range(4):
                dh[g, j] = gr[7 + j]
            per_group.append(gr)
        g0, g1 = per_group
        dxbc[...] = jnp.concatenate([g0[0], g1[0]], axis=1)
        dz[...] = jnp.concatenate([g0[1], g1[1]], axis=1).astype(dz.dtype)
        ddt[...] = g0[2] + g1[2]

        @pl.when(_first((0, 1)))
        def _():
            for r in (ddtb, dalog, ddsk, dnw):
                r[...] = jnp.zeros_like(r)

        ddtb[...] += g0[3] + g1[3]
        dalog[...] += g0[4] + g1[4]
        ddsk[...] += g0[5] + g1[5]
        dnw[...] += jnp.concatenate([g0[6], g1[6]], axis=1)

    out_specs = [row(SSM_CONV_DIM, 0), row(SSM_INNER, 0), row(LANE, 0), whole((1, LANE)), whole((1, LANE)), whole((1, LANE)),
                 whole((1, SSM_INNER))]
    lane = jax.ShapeDtypeStruct((1, LANE), F32)
    out_shape = [jax.ShapeDtypeStruct((t, SSM_CONV_DIM), F32), jax.ShapeDtypeStruct((t, SSM_INNER), BF),
                 jax.ShapeDtypeStruct((t, LANE), F32), lane, lane, lane, jax.ShapeDtypeStruct((1, SSM_INNER), F32)]
    return pl.pallas_call(
        body, name="ssd_bwd", grid=(bsz, nc), in_specs=data + par + cst + [hsave, row(SSM_GSZ, pcol), row(SSM_GSZ, pcol + 1)],
        out_specs=out_specs, out_shape=out_shape, scratch_shapes=[pltpu.VMEM((SSM_GROUPS, 4, SSM_N, LANE), F32)],
        compiler_params=_params(),
    )(xbc_act, u, u, u, dtb, alog, dsk, nw, *consts, hs, dmix, dmix)


TB = 512


def _rows(d, col=0):
    return pl.BlockSpec((TB, d), lambda i: (i, col))


def _par(d):
    return pl.BlockSpec((1, d), lambda i: (0, 0))


def _sd(shape, dtype=F32):
    return jax.ShapeDtypeStruct(shape, dtype)


def _round_up(n, m):
    return -(-n // m) * m


def _pad_rows(a, rows):
    return jnp.pad(a, ((0, rows - a.shape[0]), (0, 0)))


def _pack128(arrs):
    flat = jnp.concatenate([a.reshape(-1) for a in arrs])
    n = flat.shape[0]
    rows = -(-n // (8 * LANE)) * 8
    return jnp.pad(flat, (0, rows * LANE - n)).reshape(rows, LANE)


def _unpack128(packed, shapes):
    flat = packed.reshape(-1)
    out, off = [], 0
    for s in shapes:
        n = int(np.prod(s))
        out.append(flat[off:off + n].reshape(s))
        off += n
    return out


def kernel(x, mem, norm_gains, xa_wq, xa_wkv, xa_wo, mlp_w1, mlp_w2, ab_w_in, pool_w, pool_scale, ssm_conv_w, ssm_conv_b, ssm_dt_bias, ssm_a_log, ssm_d, ssm_norm, ab_w_out, cd_w_in, conf_dw_w, conf_dw_b, conf_ln_g, conf_ln_b, sc_conv_w, cd_w_out, loss_target, m_norm_gains, m_xa_wq, m_xa_wkv, m_xa_wo, m_mlp_w1, m_mlp_w2, m_ab_w_in, m_pool_w, m_pool_scale, m_ssm_conv_w, m_ssm_conv_b, m_ssm_dt_bias, m_ssm_a_log, m_ssm_d, m_ssm_norm, m_ab_w_out, m_cd_w_in, m_conf_dw_w, m_conf_dw_b, m_conf_ln_g, m_conf_ln_b, m_sc_conv_w, m_cd_w_out, v_norm_gains, v_xa_wq, v_xa_wkv, v_xa_wo, v_mlp_w1, v_mlp_w2, v_ab_w_in, v_pool_w, v_pool_scale, v_ssm_conv_w, v_ssm_conv_b, v_ssm_dt_bias, v_ssm_a_log, v_ssm_d, v_ssm_norm, v_ab_w_out, v_cd_w_in, v_conf_dw_w, v_conf_dw_b, v_conf_ln_g, v_conf_ln_b, v_sc_conv_w, v_cd_w_out):
    args = locals()
    w = {n: args[n] for n in WEIGHTS}
    mom_m = {n: args["m_" + n] for n in WEIGHTS}
    mom_v = {n: args["v_" + n] for n in WEIGHTS}
    ex = Exchange(w)
    loss_local, grad_x, small_grads = local_step(x, mem, loss_target, ex)
    outs = {}

    started = ex.put_small(small_grads, loss_local)
    landed = {key: ex.landed(key, started) for key in ('l1', 'cd', 'l0')}
    late = []
    for n, keys in (('mlp_w1', ('l0', 'l1')), ('mlp_w2', ('l0', 'l1')), ('xa_wkv', ('l0', 'l1')), ('xa_wq', ('l0', 'l1')),
                    ('xa_wo', ('l0', 'l1')), ('cd_w_in', ('cd',)), ('cd_w_out', ('cd',))):
        lands = [landed[key][0] for key in keys]
        offs = [landed[key][1][(n, layer)] for layer, key in enumerate(keys)]
        outs[n] = update_from_slots(lands, offs, w[n], mom_m[n], mom_v[n], SHARD_AXIS[n] == 2, "update_" + n)
        late.append(outs[n][1])
    g_own, loss = ex.reduced_small(late)
    land_ab, offs_ab = ex.landed('ab', late)
    outs['ab_w_out'] = update_from_slots([land_ab], [offs_ab[('ab_w_out', 0)]], w['ab_w_out'], mom_m['ab_w_out'],
                                         mom_v['ab_w_out'], False, "update_ab_w_out")
    res = update_from_slots([land_ab], [offs_ab[('ab_w_in', 0)]], jnp.swapaxes(w['ab_w_in'], 1, 2), jnp.swapaxes(mom_m['ab_w_in'], 1, 2),
                            jnp.swapaxes(mom_v['ab_w_in'], 1, 2), False, "update_ab_w_in")
    outs['ab_w_in'] = tuple(jnp.swapaxes(r, 1, 2) for r in res)
    small = SMALL_SHARDED + REPLICATED
    upd = adamw_many([w[n] for n in small], [mom_m[n] for n in small], [mom_v[n] for n in small], [g_own[n] for n in small],
                     "adamw_small")
    for i, n in enumerate(small):
        outs[n] = (g_own[n], upd[0][i], upd[1][i], upd[2][i])
    return (loss, grad_x.reshape(x.shape), *[outs[n][0] for n in WEIGHTS], *[outs[n][1] for n in WEIGHTS],
            *[outs[n][2] for n in WEIGHTS], *[outs[n][3] for n in WEIGHTS])


G_AB = (('ab_w_in', 0), ('ab_w_out', 0))
G_L0 = (('xa_wq', 0), ('xa_wkv', 0), ('xa_wo', 0), ('mlp_w1', 0), ('mlp_w2', 0))
G_L1 = (('xa_wq', 1), ('xa_wkv', 1), ('xa_wo', 1), ('mlp_w1', 1), ('mlp_w2', 1))
G_CD = (('cd_w_in', 0), ('cd_w_out', 0))
GATHER_CHAIN = {'l0': ('cd', G_CD), 'cd': ('l1', G_L1)}
SHARD_AXIS = dict(BIG)
MEMBER_ROW_TILE = 64
FLAT_ROW_TILE = 128


def _members(group, w):
    out = []
    for n, layer in group:
        shp = w[n].shape[1:]
        if SHARD_AXIS[n] == 2:
            shp = (shp[1], shp[0])
        assert shp[1] == D, (n, shp)
        out.append((n, layer, shp, shp[0], _round_up(shp[0], MEMBER_ROW_TILE)))
    return out


def _group_rows(group, w):
    return _round_up(sum(m[4] for m in _members(group, w)), FLAT_ROW_TILE)


def _flat_shards(group, w):
    parts = []
    for n, layer, _, _, padded in _members(group, w):
        shard = w[n][layer].astype(BF)
        parts.append(_pad_rows(shard.T if SHARD_AXIS[n] == 2 else shard, padded))
    return _pad_rows(jnp.concatenate(parts, axis=0), _group_rows(group, w))


def _full_from_slots(land, group, w):
    out, off = {}, 0
    for n, layer, shp, rows, padded in _members(group, w):
        out[(n, layer)] = land[:, off:off + rows].reshape(N_DEV * rows, D)
        off += padded
    return out


def _slots_from_full(grads, group, w):
    parts = []
    for n, layer, shp, rows, padded in _members(group, w):
        blk = grads[(n, layer)].astype(BF).reshape(N_DEV, rows, D)
        parts.append(jnp.pad(blk, ((0, 0), (0, padded - rows), (0, 0))))
    send = jnp.concatenate(parts, axis=1)
    return jnp.pad(send, ((0, 0), (0, _group_rows(group, w) - send.shape[1]), (0, 0)))


_HBM = pl.BlockSpec(memory_space=pltpu.HBM)
_SEM = pl.BlockSpec(memory_space=pltpu.SEMAPHORE)
_ANY = pl.BlockSpec(memory_space=pl.ANY)


def _peer_copy(k, src, dst, send_sems, recv_sems, peer):
    return pltpu.make_async_remote_copy(src_ref=src, dst_ref=dst, send_sem=send_sems.at[k], recv_sem=recv_sems.at[k],
                                        device_id=peer, device_id_type=pl.DeviceIdType.MESH)


def exchange_start(src, name, scatter, after=()):
    shape = src.shape[-2:]
    after = list(after)

    def body(src_ref, land_ref, *rest):
        send_sems, recv_sems, token = rest[len(after)], rest[len(after) + 1], rest[-1]
        me = _me()
        for k, f in enumerate(_FLIPS):
            peer = _flip(me, f)
            piece = src_ref.at[_slot(peer)] if scatter else src_ref
            _peer_copy(k, piece, land_ref.at[_slot(me)], send_sems, recv_sems, peer).start()
        token[...] = jnp.zeros_like(token)

    land = pltpu.with_memory_space_constraint(lax.empty((N_DEV,) + shape, src.dtype), pltpu.HBM)
    return pl.pallas_call(
        body, name=name,
        out_shape=(pltpu.SemaphoreType.DMA((7,)), pltpu.SemaphoreType.DMA((7,)), pltpu.HBM(src.shape, src.dtype),
                   pltpu.HBM((N_DEV,) + shape, src.dtype), jax.ShapeDtypeStruct((8, LANE), F32)),
        in_specs=(_HBM, _HBM) + (_ANY,) * len(after), out_specs=(_SEM, _SEM, _HBM, _HBM, pl.BlockSpec(memory_space=pltpu.VMEM)),
        input_output_aliases={0: 2, 1: 3},
        compiler_params=pltpu.CompilerParams(has_side_effects=pltpu.SideEffectType.DATAFLOW_SIDE_EFFECTING),
    )(pltpu.with_memory_space_constraint(src, pltpu.HBM), land, *after)


def exchange_wait(handles, after, name, scatter):
    send_sems, recv_sems, src_thru, land_thru, _ = handles
    after = list(after) if isinstance(after, (list, tuple)) else [after]

    def body(src_ref, land_ref, send_sems, recv_sems, *rest):
        token = rest[-1]
        me = _me()
        for k, f in enumerate(_FLIPS):
            peer = _flip(me, f)
            piece = src_ref.at[_slot(peer)] if scatter else src_ref
            cp = _peer_copy(k, piece, land_ref.at[_slot(peer)], send_sems, recv_sems, peer)
            cp.wait_send()
            cp.wait_recv()
        token[...] = jnp.zeros_like(token)

    return pl.pallas_call(
        body, name=name, out_shape=(pltpu.HBM(src_thru.shape, src_thru.dtype), pltpu.HBM(land_thru.shape, land_thru.dtype),
                                    jax.ShapeDtypeStruct((8, LANE), F32)),
        in_specs=(_HBM, _HBM, _SEM, _SEM) + (_ANY,) * len(after), out_specs=(_HBM, _HBM, pl.BlockSpec(memory_space=pltpu.VMEM)),
        input_output_aliases={0: 0, 1: 1},
        compiler_params=pltpu.CompilerParams(has_side_effects=pltpu.SideEffectType.DATAFLOW_SIDE_EFFECTING),
    )(src_thru, land_thru, send_sems, recv_sems, *after)


class Exchange:
    def __init__(self, w):
        self.w = w
        self.me = _slot(_me())
        shapes = [w[n].shape for n in SMALL_SHARDED]
        gs = all_gather(_pack128([w[n] for n in SMALL_SHARDED]), "gather_small")
        per_dev = [_unpack128(gs[d], shapes) for d in range(N_DEV)]
        self.small = {n: jnp.concatenate([per_dev[d][i] for d in range(N_DEV)], axis=-1) for i, n in enumerate(SMALL_SHARDED)}
        self.small.update({n: w[n] for n in REPLICATED})
        self.first = _full_from_slots(all_gather(_flat_shards(G_AB, w), "gather_ab"), G_AB, w)
        self.gathers = {'l0': (G_L0, exchange_start(_flat_shards(G_L0, w), "gather_l0_start", False))}
        self.tokens = [self.gathers['l0'][1][4]]
        self.reductions = {}

    def take_tokens(self):
        toks, self.tokens = self.tokens, []
        return toks

    def weights(self, key, after):
        if key == 'ab':
            return self.first
        group, handles = self.gathers[key]
        _, land, done = exchange_wait(handles, after, f"gather_{key}_wait", False)
        nxt = GATHER_CHAIN.get(key)
        if nxt is not None:
            self.gathers[nxt[0]] = (nxt[1], exchange_start(_flat_shards(nxt[1], self.w), f"gather_{nxt[0]}_start", False, after=[done]))
            self.tokens.append(self.gathers[nxt[0]][1][4])
        land = lax.dynamic_update_slice(land, handles[2][None], (self.me, 0, 0))
        return _full_from_slots(land, group, self.w)

    def put_grads(self, key, group, grads):
        send = _slots_from_full(grads, group, self.w)
        handles = exchange_start(send, f"reduce_{key}_start", True)
        self.reductions[key] = (group, handles)
        self.tokens.append(handles[4])

    def landed(self, key, after):
        group, handles = self.reductions[key]
        send, land, _ = exchange_wait(handles, after, f"reduce_{key}_wait", True)
        mine = lax.dynamic_slice_in_dim(send, self.me, 1, axis=0)
        land = lax.dynamic_update_slice(land, mine, (self.me, 0, 0))
        offs, off = {}, 0
        for n, layer, _, _, padded in _members(group, self.w):
            offs[(n, layer)] = off
            off += padded
        return land, offs

    def put_small(self, small_grads, loss_local):
        small = SMALL_SHARDED + REPLICATED
        self.small_shapes = [small_grads[n].shape for n in small] + [(1,)]
        packed = _pack128([small_grads[n] for n in small] + [loss_local.reshape(1)])
        self.small_handles = exchange_start(packed, "gather_small_grads_start", False)
        return self.small_handles[4]

    def reduced_small(self, after):
        small = SMALL_SHARDED + REPLICATED
        src, land, _ = exchange_wait(self.small_handles, after, "gather_small_grads_wait", False)
        gs = lax.dynamic_update_slice(land, src[None], (self.me, 0, 0))
        tot = _unpack128(sum_slots(gs, "sum_small", 1024), self.small_shapes)
        out = {}
        for n, g in zip(small, tot):
            if n in SMALL_SHARDED:
                width = self.w[n].shape[-1]
                g = lax.dynamic_slice_in_dim(g, self.me * width, width, axis=g.ndim - 1)
            out[n] = g
        return out, tot[-1].reshape(())


def local_step(x, mem, target, ex):
    bsz, seq, _ = x.shape
    t = bsz * seq
    nb = t // TB
    nc = seq // CHUNK
    x0 = x.reshape(t, D)
    mem2 = mem.reshape(bsz * N_MEM, D)
    tgt = target.reshape(t, D)
    p = ex.small
    gains = p['norm_gains']
    big = {}

    def gain(layer, i):
        g = gains[layer, i].reshape(1, D)
        for tok in ex.take_tokens():
            g = g + tok[0, 0]
        return g

    consts = _ssd_consts()
    grads = {}
    saved = [dict(), dict()]

    def matmul_res(a, b, name, xin, ga, gb):
        return matmul(a, b, 'nn', name, (F32, F32, BF), epilogue=res_epilogue, extras=[xin], params=[ga, gb])

    def attn_specs():
        nq = seq // TB
        q = pl.BlockSpec((TB, D), lambda b, i: (b * nq + i, 0))
        kv = pl.BlockSpec((N_MEM, 2 * D), lambda b, i: (b, 0))
        return (bsz, nq), q, kv

    def attention_fwd(layer, xin, hin, sv, ga, gb):
        q = matmul(hin, big[('xa_wq', layer)], 'nn', f"q_{layer}", BF)
        kv = matmul(mem2, big[('xa_wkv', layer)], 'nt', f"kv_{layer}", BF)
        grid, qs, kvs = attn_specs()
        o, = fwd_call(attn_fn, f"attn_{layer}", grid, [q, kv], [qs, kvs], [_sd((t, D), BF)], [qs])
        ao, x_next, h_next = matmul_res(o, big[('xa_wo', layer)], f"ao_{layer}", xin, ga, gb)
        sv.update(q=q, kv=kv, o=o, ao=ao)
        return ao, x_next, h_next

    def mlp_fwd(layer, hin, sv, res=None):
        r, rr = matmul(hin, big[('mlp_w1', layer)], 'nt', f"mlp1_{layer}", (BF, BF), epilogue=act_epilogue)
        if res is None:
            out = (matmul(rr, big[('mlp_w2', layer)], 'nn', f"mlp2_{layer}"),)
        else:
            out = matmul_res(rr, big[('mlp_w2', layer)], f"mlp2_{layer}", *res)
        sv.update(r=r, rr=rr, mo=out[0])
        return out

    sv = saved[0]
    h0, = fwd_call(seg_in, "norm_in", (nb,), [x0, gain(0, 0)], [_rows(D), _par(D)], [_sd((t, D), BF)], [_rows(D)])
    big.update(ex.weights('ab', h0))
    xbc0 = POOL_W + SSM_INNER
    w_ab_in = big[('ab_w_in', 0)]
    w_ab_in = _pad_rows(jnp.concatenate([w_ab_in[:xbc0], _xbc_group(w_ab_in[xbc0:xbc0 + SSM_CONV_DIM], 0),
                                         w_ab_in[xbc0 + SSM_CONV_DIM:]], axis=0), AB_IN_PAD)
    conv_w, conv_b = _xbc_group(p['ssm_conv_w'][0], 1), _xbc_group(p['ssm_conv_b'], 1)
    u0 = matmul(h0, w_ab_in, 'nt', "ab_in")
    pool_outs = []
    for g in range(POOL_GROUPS):
        seqspec = pl.BlockSpec((seq, PG), lambda b, g=g: (b, g))
        po, = fwd_call(make_pool_fn(g), f"pool_{g}", (bsz,), [u0, p['pool_w'][0, g], p['pool_scale']],
                       [seqspec, pl.BlockSpec((PG, PG), lambda b: (0, 0)), pl.BlockSpec((1, PG), lambda b, g=g: (0, g))],
                       [_sd((t, PG), BF)], [pl.BlockSpec((seq, PG), lambda b: (b, 0))])
        pool_outs.append(po)
    cw = 256
    ncb = SSM_CONV_DIM // cw
    cbase = (POOL_W + SSM_INNER) // cw
    conv_in_specs = [pl.BlockSpec((seq, cw), lambda j, b: (b, cbase + j)), pl.BlockSpec((SSM_CONV, cw), lambda j, b: (0, j)),
                     pl.BlockSpec((1, cw), lambda j, b: (0, j))]
    conv_out_spec = pl.BlockSpec((seq, cw), lambda j, b: (b, j))
    xbc_act, = fwd_call(conv4_fn, "ssm_conv", (ncb, bsz), [u0, conv_w, conv_b], conv_in_specs,
                        [_sd((t, SSM_CONV_DIM))], [conv_out_spec])
    dtb = jnp.pad(p['ssm_dt_bias'], ((0, 0), (0, LANE - SSM_HEADS)))
    alog = jnp.pad(p['ssm_a_log'], ((0, 0), (0, LANE - SSM_HEADS)))
    dsk = jnp.pad(p['ssm_d'], ((0, 0), (0, LANE - SSM_HEADS)))
    yn, hs = ssd_fwd(xbc_act, u0, dtb, alog, dsk, p['ssm_norm'], consts, bsz, seq)
    mix0 = jnp.concatenate(pool_outs + [yn], axis=1)
    m0, x1, h2 = matmul_res(mix0, big[('ab_w_out', 0)], "ab_out", x0, gain(0, 1), gain(0, 2))
    big.update(ex.weights('l0', h2))
    ao0, x2, h3 = attention_fwd(0, x1, h2, sv, gain(0, 3), gain(0, 4))
    mo0, x3, h4 = mlp_fwd(0, h3, sv, (x2, gain(0, 5), gain(1, 0)))
    big.update(ex.weights('cd', mo0))

    sv1 = saved[1]
    nd = D // LANE
    w_cd_in = big[('cd_w_in', 0)].reshape(5, nd, LANE, D).transpose(1, 0, 2, 3).reshape(CD_IN, D)
    u1 = matmul(h4, w_cd_in, 'nt', "cd_in")
    cd_par = [pl.BlockSpec((CONF_K, LANE), lambda j, b: (0, j)), pl.BlockSpec((1, LANE), lambda j, b: (0, j)),
              pl.BlockSpec((SC_K, LANE), lambda j, b: (0, j))]
    cd_ins = [u1, p['conf_dw_w'][0], p['conf_dw_b'], p['sc_conv_w'][0]]
    cd_u_spec = pl.BlockSpec((seq, 5 * LANE), lambda j, b: (b, j))
    cd_in_specs = [cd_u_spec] + cd_par
    cd_out_spec = pl.BlockSpec((seq, LANE), lambda j, b: (b, j))
    vconv, mix1 = fwd_call(cd1_fn, "cd_conv", (nd, bsz), cd_ins, cd_in_specs, [_sd((t, D)), _sd((t, CD_OUT), BF)],
                           [cd_out_spec, pl.BlockSpec((seq, LANE), lambda j, b: (b, nd + j))])
    mix1, = fwd_call(seg_ln, "conf_ln", (nb,), [vconv, p['conf_ln_g'], p['conf_ln_b']], [_rows(D), _par(D), _par(D)],
                     [_sd((t, CD_OUT), BF)], [_rows(D)], into=mix1)
    m1, x4, h5 = matmul_res(mix1, big[('cd_w_out', 0)], "cd_out", x3, gain(1, 1), gain(1, 2))
    big.update(ex.weights('l1', h5))
    ao1, x5, h6 = attention_fwd(1, x4, h5, sv1, gain(1, 3), gain(1, 4))
    mo1, = mlp_fwd(1, h6, sv1)

    def loss_body(x_ref, m_ref, g_ref, t_ref, dx_ref, dm_ref, dg_ref, acc_ref):
        (y,), vjp = jax.vjp(seg_out, x_ref[...], m_ref[...], g_ref[...])
        d = y - t_ref[...]
        dx, dm, dg = vjp((d / float(D),))
        dx_ref[...] = dx
        dm_ref[...] = dm.astype(dm_ref.dtype)

        @pl.when(pl.program_id(0) == 0)
        def _():
            acc_ref[...] = jnp.zeros_like(acc_ref)
            dg_ref[...] = jnp.zeros_like(dg_ref)

        acc_ref[...] += jnp.sum(d * d, axis=0, keepdims=True)
        dg_ref[...] += dg

    dx5, dmo1, dg15, lanes = pl.pallas_call(
        loss_body, name="loss_head", grid=(nb,), in_specs=[_rows(D), _rows(D), _par(D), _rows(D)],
        out_specs=[_rows(D), _rows(D), _par(D), _par(D)], out_shape=[_sd((t, D)), _sd((t, D), BF), _sd((1, D)), _sd((1, D))],
        compiler_params=_params())(x5, mo1, gain(1, 5), tgt)
    loss = 0.5 * jnp.sum(lanes) / float(D)

    gain_grads = {(1, 5): dg15}

    def bwd_seg_res(xin, m, ga, gb, dx1, dh, name):
        return bwd_call(seg_res, name, (nb,), [xin, m, ga, gb], [_rows(D), _rows(D), _par(D), _par(D)], [dx1, dh],
                        [_rows(D), _rows(D)], [0, 1, 2, 3], [_sd((t, D)), _sd((t, D), BF), _sd((1, D)), _sd((1, D))],
                        [_rows(D), _rows(D), _par(D), _par(D)], [None, None, (0,), (0,)])

    def mlp_bwd(layer, hin, dmo, sv):
        grads_w2 = matmul(sv['rr'], dmo, 'tn', f"d_mlp_w2_{layer}", BF)
        dr, = matmul(dmo, big[('mlp_w2', layer)], 'nt', f"d_r_{layer}", (BF,), epilogue=act_bwd_epilogue, extras=[sv['r']])
        grads_w1 = matmul(dr, hin, 'tn', f"d_mlp_w1_{layer}", BF)
        dh = matmul(dr, big[('mlp_w1', layer)], 'nn', f"d_h_mlp_{layer}")
        return dh, grads_w1, grads_w2

    def attention_bwd(layer, hin, dao, sv):
        g_wo = matmul(sv['o'], dao, 'tn', f"d_xa_wo_{layer}", BF)
        do = matmul(dao, big[('xa_wo', layer)], 'nt', f"d_o_{layer}", BF)
        grid, qs, kvs = attn_specs()
        dq, dkv = bwd_call(attn_fn, f"d_attn_{layer}", grid, [sv['q'], sv['kv']], [qs, kvs], [do], [qs], [0, 1],
                           [_sd((t, D), BF), _sd((bsz * N_MEM, 2 * D))], [qs, kvs], [None, (1,)])
        g_wkv = matmul(dkv, mem2, 'tn', f"d_xa_wkv_{layer}", BF)
        g_wq = matmul(hin, dq, 'tn', f"d_xa_wq_{layer}", BF)
        dh = matmul(dq, big[('xa_wq', layer)], 'nt', f"d_h_attn_{layer}")
        return dh, g_wq, g_wkv, g_wo

    per_layer = {k: [None, None] for k in ('xa_wq', 'xa_wkv', 'xa_wo', 'mlp_w1', 'mlp_w2')}

    dh6, per_layer['mlp_w1'][1], per_layer['mlp_w2'][1] = mlp_bwd(1, h6, dmo1, sv1)
    dx4, dao1, gain_grads[(1, 3)], gain_grads[(1, 4)] = bwd_seg_res(x4, ao1, gain(1, 3), gain(1, 4), dx5, dh6, "d_res_1b")
    dh5, per_layer['xa_wq'][1], per_layer['xa_wkv'][1], per_layer['xa_wo'][1] = attention_bwd(1, h5, dao1, sv1)
    ex.put_grads('l1', G_L1, {(k, 1): v[1] for k, v in per_layer.items()})
    dx3, dm1, gain_grads[(1, 1)], gain_grads[(1, 2)] = bwd_seg_res(x3, m1, gain(1, 1), gain(1, 2), dx4, dh5, "d_res_1a")
    g_cd_out = matmul(mix1, dm1, 'tn', "d_cd_w_out", BF)
    dmix1 = matmul(dm1, big[('cd_w_out', 0)], 'nt', "d_mix1")
    dvconv, dlg, dlb = bwd_call(seg_ln, "d_conf_ln", (nb,), [vconv, p['conf_ln_g'], p['conf_ln_b']],
                                [_rows(D), _par(D), _par(D)], [dmix1], [_rows(D, 0)], [0, 1, 2],
                                [_sd((t, D)), _sd((1, D)), _sd((1, D))], [_rows(D), _par(D), _par(D)], [None, (0,), (0,)])
    grads['conf_ln_g'], grads['conf_ln_b'] = dlg, dlb
    cd_g = bwd_call(cd1_fn, "d_cd_conv", (nd, bsz), cd_ins, cd_in_specs, [dvconv, dmix1],
                    [cd_out_spec, pl.BlockSpec((seq, LANE), lambda j, b: (b, nd + j))], list(range(4)),
                    [_sd((t, CD_IN), BF), _sd((CONF_K, D)), _sd((1, D)), _sd((SC_K, D))], [cd_u_spec] + cd_par,
                    [None, (1,), (1,), (1,)])
    du1 = cd_g[0]
    grads['conf_dw_w'], grads['conf_dw_b'], grads['sc_conv_w'] = cd_g[1][None], cd_g[2], cd_g[3][None]
    g_cd_in = matmul(du1, h4, 'tn', "d_cd_w_in", BF).reshape(nd, 5, LANE, D).transpose(1, 0, 2, 3).reshape(CD_IN, D)
    ex.put_grads('cd', G_CD, {('cd_w_in', 0): g_cd_in, ('cd_w_out', 0): g_cd_out})
    dh4 = matmul(du1, w_cd_in, 'nn', "d_h_cd")

    dx2, dmo0, gain_grads[(0, 5)], gain_grads[(1, 0)] = bwd_seg_res(x2, mo0, gain(0, 5), gain(1, 0), dx3, dh4, "d_res_0c")
    dh3, per_layer['mlp_w1'][0], per_layer['mlp_w2'][0] = mlp_bwd(0, h3, dmo0, sv)
    dx1, dao0, gain_grads[(0, 3)], gain_grads[(0, 4)] = bwd_seg_res(x1, ao0, gain(0, 3), gain(0, 4), dx2, dh3, "d_res_0b")
    dh2, per_layer['xa_wq'][0], per_layer['xa_wkv'][0], per_layer['xa_wo'][0] = attention_bwd(0, h2, dao0, sv)
    ex.put_grads('l0', G_L0, {(k, 0): v[0] for k, v in per_layer.items()})
    dx0r, dm0, gain_grads[(0, 1)], gain_grads[(0, 2)] = bwd_seg_res(x0, m0, gain(0, 1), gain(0, 2), dx1, dh2, "d_res_0a")
    g_ab_out = matmul(mix0, dm0, 'tn', "d_ab_w_out", BF)
    dmix0 = matmul(dm0, big[('ab_w_out', 0)], 'nt', "d_mix0")
    dxbc_act, dz, ddt, ddtb, dalog, ddsk, dnw = ssd_bwd(xbc_act, u0, dtb, alog, dsk, p['ssm_norm'], consts, hs, dmix0, bsz, seq)
    grads['ssm_dt_bias'] = ddtb[:, :SSM_HEADS]
    grads['ssm_a_log'] = dalog[:, :SSM_HEADS]
    grads['ssm_d'] = ddsk[:, :SSM_HEADS]
    grads['ssm_norm'] = dnw
    dxr, dcw, dcb = bwd_call(conv4_fn, "d_ssm_conv", (ncb, bsz), [u0, conv_w, conv_b], conv_in_specs,
                             [dxbc_act], [conv_out_spec], [0, 1, 2],
                             [_sd((t, SSM_CONV_DIM), BF), _sd((SSM_CONV, SSM_CONV_DIM)), _sd((1, SSM_CONV_DIM))],
                             [conv_out_spec, conv_in_specs[1], conv_in_specs[2]], [None, (1,), (1,)])
    grads['ssm_conv_w'], grads['ssm_conv_b'] = _xbc_ungroup(dcw, 1)[None], _xbc_ungroup(dcb, 1)
    dpool, dpw, dps = [], [], []
    for g in range(POOL_GROUPS):
        seqspec = pl.BlockSpec((seq, PG), lambda b, g=g: (b, g))
        one = pl.BlockSpec((seq, PG), lambda b: (b, 0))
        wspec = pl.BlockSpec((PG, PG), lambda b: (0, 0))
        sspec = pl.BlockSpec((1, PG), lambda b, g=g: (0, g))
        a, bb, c = bwd_call(make_pool_fn(g), f"d_pool_{g}", (bsz,), [u0, p['pool_w'][0, g], p['pool_scale']],
                            [seqspec, wspec, sspec], [dmix0], [seqspec], [0, 1, 2],
                            [_sd((t, PG), BF), _sd((PG, PG)), _sd((1, PG))], [one, wspec, pl.BlockSpec((1, PG), lambda b: (0, 0))],
                            [None, (0,), (0,)])
        dpool.append(a)
        dpw.append(bb)
        dps.append(c)
    grads['pool_w'] = jnp.stack(dpw)[None]
    grads['pool_scale'] = jnp.concatenate(dps, axis=1)
    du0 = jnp.concatenate(dpool + [dz, dxr, ddt.astype(BF)], axis=1)
    g_ab_in = matmul(du0, h0, 'tn', "d_ab_w_in", BF)
    g_ab_in = jnp.concatenate([g_ab_in[:xbc0], _xbc_ungroup(g_ab_in[xbc0:xbc0 + SSM_CONV_DIM], 0),
                               g_ab_in[xbc0 + SSM_CONV_DIM:AB_IN]], axis=0)
    ex.put_grads('ab', G_AB, {('ab_w_in', 0): g_ab_in, ('ab_w_out', 0): g_ab_out})
    dh0 = matmul(du0, w_ab_in, 'nn', "d_h_ab", after=ex.take_tokens())
    dx, dg00 = bwd_call(seg_in_res, "d_norm_in", (nb,), [x0, gain(0, 0)], [_rows(D), _par(D)], [dx0r, dh0],
                        [_rows(D), _rows(D)], [0, 1], [_sd((t, D)), _sd((1, D))], [_rows(D), _par(D)], [None, (0,)])
    gain_grads[(0, 0)] = dg00
    grads['norm_gains'] = jnp.stack([jnp.concatenate([gain_grads[(l, i)] for i in range(6)], axis=0) for l in range(2)])
    return loss, dx, grads
```

```python
import functools
import math

import numpy as np
import jax
import jax.numpy as jnp
from jax import lax
from jax.experimental import pallas as pl
from jax.experimental.pallas import tpu as pltpu

BF = jnp.bfloat16
F32 = jnp.float32

N_DEV = 8
D = 1024
N_MEM = 256
XA_HEADS = 4
XA_DH = D // XA_HEADS
POOL_GROUPS = 4
PG = 128
POOL_W = POOL_GROUPS * PG
SSM_INNER = 1024
SSM_GROUPS = 2
SSM_GSZ = SSM_INNER // SSM_GROUPS
SSM_HEADS = 16
SSM_P = 64
SSM_N = 128
SSM_CONV = 4
SSM_CONV_DIM = SSM_INNER + 2 * SSM_GROUPS * SSM_N
SSM_XBC_G = SSM_GSZ + 2 * SSM_N
CHUNK = 128
AB_IN = POOL_W + SSM_INNER + SSM_CONV_DIM + SSM_HEADS
AB_IN_PAD = POOL_W + SSM_INNER + SSM_CONV_DIM + 128
AB_OUT = POOL_W + SSM_INNER
CONF_K = 31
SC_K = 3
CD_IN = 5 * D
CD_OUT = 2 * D
MLP_H = 4 * D
RMS_EPS = 1e-6
LN_EPS = 1e-5
ADAM_LR = 0.001
ADAM_B1 = 0.9
ADAM_B2 = 0.999
ADAM_EPS = 1e-08
ADAM_WD = 0.01
ADAM_STEP = 10
VMEM_LIMIT = 56 * 1024 * 1024
LANE = 128

NAMES = ['x', 'mem', 'norm_gains', 'xa_wq', 'xa_wkv', 'xa_wo', 'mlp_w1', 'mlp_w2', 'ab_w_in', 'pool_w', 'pool_scale',
         'ssm_conv_w', 'ssm_conv_b', 'ssm_dt_bias', 'ssm_a_log', 'ssm_d', 'ssm_norm', 'ab_w_out', 'cd_w_in', 'conf_dw_w',
         'conf_dw_b', 'conf_ln_g', 'conf_ln_b', 'sc_conv_w', 'cd_w_out', 'loss_target']
WEIGHTS = NAMES[2:25]
BIG = [('xa_wq', 1), ('xa_wkv', 2), ('xa_wo', 1), ('mlp_w1', 2), ('mlp_w2', 1), ('cd_w_in', 2), ('cd_w_out', 1),
       ('ab_w_out', 1), ('ab_w_in', 2)]
SMALL_SHARDED = ['norm_gains', 'ssm_conv_w', 'conf_dw_w', 'conf_dw_b', 'conf_ln_g', 'conf_ln_b', 'sc_conv_w']
REPLICATED = ['pool_w', 'pool_scale', 'ssm_conv_b', 'ssm_dt_bias', 'ssm_a_log', 'ssm_d', 'ssm_norm']


def _dg(a, b, ca, cb, prec=None):
    return lax.dot_general(a, b, (((ca,), (cb,)), ((), ())), precision=prec, preferred_element_type=F32)


@functools.partial(jax.custom_vjp, nondiff_argnums=(2, 3))
def bdot(a, b, ca, cb):
    return _dg(a.astype(BF), b.astype(BF), ca, cb)


def _bdot_fwd(a, b, ca, cb):
    return bdot(a, b, ca, cb), (a, b)


def _bdot_bwd(ca, cb, res, g):
    a, b = res
    g16, a16, b16 = g.astype(BF), a.astype(BF), b.astype(BF)
    da = _dg(g16, b16, 1, 1 - cb) if ca == 1 else _dg(b16, g16, 1 - cb, 1)
    db = _dg(g16, a16, 0, 1 - ca) if cb == 1 else _dg(a16, g16, 1 - ca, 0)
    return da.astype(a.dtype), db.astype(b.dtype)


bdot.defvjp(_bdot_fwd, _bdot_bwd)


def _split3(a):
    a1 = a.astype(BF)
    r1 = a - a1.astype(F32)
    a2 = r1.astype(BF)
    a3 = (r1 - a2.astype(F32)).astype(BF)
    return a1, a2, a3


def _exact_right(a, c):
    m = a.shape[0]
    if m % 16:
        return sum(_dg(p, c, 1, 0) for p in _split3(a))
    o = _dg(jnp.concatenate(_split3(a), axis=0), c, 1, 0)
    return o[:m] + o[m:2 * m] + o[2 * m:]


def _exact_left(c, a):
    n = a.shape[1]
    o = _dg(c, jnp.concatenate(_split3(a), axis=1), 1, 0)
    return o[:, :n] + o[:, n:2 * n] + o[:, 2 * n:]


@jax.custom_vjp
def cmat(a, c, ct):
    return _exact_right(a, c)


def _cmat_fwd(a, c, ct):
    return cmat(a, c, ct), (c, ct)


def _cmat_bwd(res, g):
    c, ct = res
    return _exact_right(g, ct), jnp.zeros_like(c), jnp.zeros_like(ct)


cmat.defvjp(_cmat_fwd, _cmat_bwd)


@jax.custom_vjp
def cmatl(c, ct, a):
    return _exact_left(c, a)


def _cmatl_fwd(c, ct, a):
    return cmatl(c, ct, a), (c, ct)


def _cmatl_bwd(res, g):
    c, ct = res
    return jnp.zeros_like(c), jnp.zeros_like(ct), _exact_left(ct, g)


cmatl.defvjp(_cmatl_fwd, _cmatl_bwd)


SUBLANES = 8


def _taps(x, shifts, down):
    n, c = x.shape
    pad = _round_up(max(shifts), SUBLANES)
    if pad == 0:
        return {0: x}
    zeros = jnp.zeros((pad, c), x.dtype)
    xp = jnp.concatenate([zeros, x] if down else [x, zeros], axis=0)
    rolled, out = {0: xp}, {}
    for s in shifts:
        a, b = divmod(s, SUBLANES)
        if b not in rolled:
            rolled[b] = pltpu.roll(xp, b if down else n + pad - b, 0)
        off = pad - SUBLANES * a if down else SUBLANES * a
        out[s] = rolled[b][off:off + n]
    return out


def _shift_down(x, k):
    return _taps(x, [k], True)[k]


def _shift_up(x, k):
    return _taps(x, [k], False)[k]


@functools.partial(jax.custom_vjp, nondiff_argnums=(1,))
def shift(x, k):
    return _shift_down(x, k)


def _shift_fwd(x, k):
    return _shift_down(x, k), None


def _shift_bwd(k, _, g):
    return (_shift_up(g, k),)


shift.defvjp(_shift_fwd, _shift_bwd)


@functools.partial(jax.custom_vjp, nondiff_argnums=(2,))
def cconv(u, w, width):
    taps = _taps(u, list(range(width)), True)
    acc = u * w[width - 1:width, :]
    for k in range(width - 1):
        acc = acc + taps[width - 1 - k] * w[k:k + 1, :]
    return acc


def _cconv_fwd(u, w, width):
    return cconv(u, w, width), (u, w)


def _cconv_bwd(width, res, g):
    u, w = res
    rows = lax.broadcasted_iota(jnp.int32, w.shape, 0)
    du = g * w[width - 1:width, :]
    dw = jnp.where(rows == width - 1, jnp.sum(g * u, axis=0, keepdims=True), 0.0)
    g_taps = _taps(g, list(range(width)), False)
    u_taps = _taps(u, list(range(width)), True)
    for k in range(width - 1):
        s = width - 1 - k
        du = du + g_taps[s] * w[k:k + 1, :]
        dw = dw + jnp.where(rows == k, jnp.sum(g * u_taps[s], axis=0, keepdims=True), 0.0)
    return du, dw


cconv.defvjp(_cconv_fwd, _cconv_bwd)


def _rms(x, g):
    return x * lax.rsqrt(jnp.mean(x * x, axis=-1, keepdims=True) + RMS_EPS) * g


def _params(sem=None):
    return pltpu.CompilerParams(dimension_semantics=sem, vmem_limit_bytes=VMEM_LIMIT)


def _f32(v):
    return v if v.dtype == F32 else v.astype(F32)


def _first(axes):
    ok = None
    for ax in axes:
        c = pl.program_id(ax) == 0
        ok = c if ok is None else jnp.logical_and(ok, c)
    return ok


def fwd_call(fn, name, grid, ins, in_specs, out_shapes, out_specs, into=None):
    n_in = len(ins)
    n_into = 0 if into is None else 1

    def body(*refs):
        outs = fn(*[_f32(r[...]) for r in refs[:n_in]])
        for r, o in zip(refs[n_in + n_into:], outs):
            r[...] = o.astype(r.dtype)

    extra = [] if into is None else [into]
    return pl.pallas_call(body, name=name, grid=grid, in_specs=list(in_specs) + [pl.BlockSpec(memory_space=pl.ANY)] * n_into,
                          out_specs=out_specs, out_shape=out_shapes, input_output_aliases={n_in: 0} if n_into else {},
                          compiler_params=_params())(*ins, *extra)


def bwd_call(fn, name, grid, ins, in_specs, cots, cot_specs, gidx, g_shapes, g_specs, g_acc):
    n_in, n_cot = len(ins), len(cots)

    def body(*refs):
        vals = [_f32(r[...]) for r in refs[:n_in]]

        def f_sel(*dv):
            full = list(vals)
            for i, v in zip(gidx, dv):
                full[i] = v
            return tuple(fn(*full))

        outs, vjp = jax.vjp(f_sel, *[vals[i] for i in gidx])
        cts = tuple(_f32(r[...]) for r in refs[n_in:n_in + n_cot])
        grads = vjp(cts)
        for r, g, acc in zip(refs[n_in + n_cot:], grads, g_acc):
            if acc is None:
                r[...] = g.astype(r.dtype)
            else:
                @pl.when(_first(acc))
                def _():
                    r[...] = jnp.zeros_like(r)

                r[...] += g.astype(r.dtype)

    return pl.pallas_call(body, name=name, grid=grid, in_specs=list(in_specs) + list(cot_specs), out_specs=g_specs,
                          out_shape=g_shapes, compiler_params=_params())(*ins, *cots)


def _tile(dim, pref):
    if dim <= pref:
        return dim
    best = None
    for t in range(LANE, pref + 1, LANE):
        if dim % t == 0:
            best = t
    assert best is not None, dim
    return best


MATMUL_VMEM_BUDGET = 40 * 1024 * 1024


def _matmul_tiles(m, n, k, a_bytes, b_bytes, out_bytes):
    tn = _tile(n, 1024)
    for tk_pref in (k, 2048, 1024, 512):
        tk = _tile(k, tk_pref)
        for tm_pref in (1024, 512, 256):
            tm = _tile(m, tm_pref)
            need = 2 * (tm * tk * a_bytes + tk * tn * b_bytes + tm * tn * out_bytes) + (0 if tk == k else tm * tn * 4)
            need += (tm * tk * 2 if a_bytes == 4 else 0) + (tk * tn * 2 if b_bytes == 4 else 0)
            if need <= MATMUL_VMEM_BUDGET:
                return tm, tn, tk
    raise ValueError((m, n, k))


def matmul(a, b, mode, name, out_dtype=F32, epilogue=None, extras=(), params=(), after=()):
    if mode == 'nn':
        (m, k), (k2, n) = a.shape, b.shape
    elif mode == 'nt':
        (m, k), (n, k2) = a.shape, b.shape
    else:
        (k, m), (k2, n) = a.shape, b.shape
    assert k == k2, (name, a.shape, b.shape)
    n_extra = len(extras) + len(params)
    out_dtypes = out_dtype if isinstance(out_dtype, tuple) else (out_dtype,)
    per_out = sum(jnp.dtype(dt).itemsize for dt in out_dtypes) + sum(e.dtype.itemsize for e in extras)
    tm, tn, tk = _matmul_tiles(m, n, k, a.dtype.itemsize, b.dtype.itemsize, per_out)
    nk = k // tk
    ca = 0 if mode == 'tn' else 1
    cb = 1 if mode == 'nt' else 0
    a_spec = pl.BlockSpec((tk, tm), lambda i, j, kk: (kk, i)) if mode == 'tn' else pl.BlockSpec((tm, tk), lambda i, j, kk: (i, kk))
    b_spec = pl.BlockSpec((tn, tk), lambda i, j, kk: (j, kk)) if mode == 'nt' else pl.BlockSpec((tk, tn), lambda i, j, kk: (kk, j))

    def finish(o_refs, extra_refs, acc):
        outs = (acc,) if epilogue is None else epilogue(acc, *[_f32(e[...]) for e in extra_refs])
        for o_ref, o in zip(o_refs, outs):
            o_ref[...] = o.astype(o_ref.dtype)

    n_after = len(after)

    def body_whole_k(a_ref, b_ref, *refs):
        refs = refs[n_after:]
        finish(refs[n_extra:], refs[:n_extra], _dg(a_ref[...].astype(BF), b_ref[...].astype(BF), ca, cb))

    def body_split_k(a_ref, b_ref, *refs):
        refs = refs[n_after:]
        extra_refs, o_refs, acc = refs[:n_extra], refs[n_extra:-1], refs[-1]
        kk = pl.program_id(2)

        @pl.when(kk == 0)
        def _():
            acc[...] = jnp.zeros_like(acc)

        acc[...] += _dg(a_ref[...].astype(BF), b_ref[...].astype(BF), ca, cb)

        @pl.when(kk == nk - 1)
        def _():
            finish(o_refs, extra_refs, acc[...])

    tile = pl.BlockSpec((tm, tn), lambda i, j, kk: (i, j))
    row = pl.BlockSpec((1, tn), lambda i, j, kk: (0, j))
    n_par = len(params)
    outs = pl.pallas_call(
        body_whole_k if nk == 1 else body_split_k, name=name, grid=(m // tm, n // tn, nk),
        in_specs=[a_spec, b_spec] + [pl.BlockSpec(memory_space=pl.ANY)] * n_after + [tile] * len(extras) + [row] * n_par,
        out_specs=[tile] * len(out_dtypes),
        out_shape=[jax.ShapeDtypeStruct((m, n), dt) for dt in out_dtypes],
        scratch_shapes=[] if nk == 1 else [pltpu.VMEM((tm, tn), F32)],
        compiler_params=_params(("parallel", "parallel", "arbitrary")))(a, b, *after, *extras, *params)
    return outs if isinstance(out_dtype, tuple) else outs[0]


_FLIPS = [(0, 0, 1), (1, 0, 0), (0, 1, 0), (1, 1, 0), (1, 0, 1), (0, 1, 1), (1, 1, 1)]


def _me():
    return lax.axis_index("x"), lax.axis_index("y"), lax.axis_index("c")


def _flip(pos, f):
    return tuple(jnp.where(fi == 1, 1 - p, p) if fi else p for p, fi in zip(pos, f))


def _slot(pos):
    return 4 * pos[0] + 2 * pos[1] + pos[2]


def all_gather(v, name):
    def body(v_ref, out_ref, send_sems, recv_sems, local_sem):
        me = _me()
        sibling = _flip(me, (0, 0, 1))
        chips = [_flip(me, f) for f in ((1, 0, 0), (0, 1, 0), (1, 1, 0))]

        def copy(k, block, to, src=None):
            return pltpu.make_async_remote_copy(
                src_ref=out_ref.at[_slot(block)] if src is None else src, dst_ref=out_ref.at[_slot(block)],
                send_sem=send_sems.at[k], recv_sem=recv_sems.at[k], device_id=to, device_id_type=pl.DeviceIdType.MESH)

        mine = pltpu.make_async_copy(v_ref, out_ref.at[_slot(me)], local_sem)
        mine.start()
        first = [copy(0, me, sibling, src=v_ref)] + [copy(1 + j, me, chip, src=v_ref) for j, chip in enumerate(chips)]
        for cp in first:
            cp.start()
        passed = [copy(4 + j, chip, sibling) for j, chip in enumerate(chips)]
        for j, chip in enumerate(chips):
            copy(1 + j, chip, me).wait_recv()
            passed[j].start()
        copy(0, sibling, me).wait_recv()
        for j, chip in enumerate(chips):
            copy(4 + j, _flip(chip, (0, 0, 1)), me).wait_recv()
        for cp in first + passed:
            cp.wait_send()
        mine.wait()

    return pl.pallas_call(
        body, name=name, out_shape=jax.ShapeDtypeStruct((N_DEV,) + v.shape, v.dtype),
        in_specs=[pl.BlockSpec(memory_space=pl.ANY)], out_specs=pl.BlockSpec(memory_space=pl.ANY),
        scratch_shapes=[pltpu.SemaphoreType.DMA((7,)), pltpu.SemaphoreType.DMA((7,)), pltpu.SemaphoreType.DMA(())],
    )(v)


def sum_slots(v, name, tr=256):
    _, r, c = v.shape
    tr = _tile_rows(r, tr)

    def body(v_ref, o_ref):
        acc = v_ref[0].astype(F32)
        for s in range(1, N_DEV):
            acc = acc + v_ref[s].astype(F32)
        o_ref[...] = acc

    return pl.pallas_call(body, name=name, grid=(r // tr,), in_specs=[pl.BlockSpec((N_DEV, tr, c), lambda i: (0, i, 0))],
                          out_specs=pl.BlockSpec((tr, c), lambda i: (i, 0)), out_shape=jax.ShapeDtypeStruct((r, c), F32),
                          compiler_params=_params())(v)


def _tile_rows(r, pref):
    if r <= pref:
        return r
    best = None
    for t in range(8, pref + 1, 8):
        if r % t == 0:
            best = t
    return r if best is None else best


def _adamw_math(w, m, v, g):
    nm = ADAM_B1 * m + (1.0 - ADAM_B1) * g
    nv = ADAM_B2 * v + (1.0 - ADAM_B2) * jnp.square(g)
    m_hat = nm / (1.0 - ADAM_B1 ** ADAM_STEP)
    v_hat = nv / (1.0 - ADAM_B2 ** ADAM_STEP)
    return -ADAM_LR * (m_hat / (jnp.sqrt(v_hat) + ADAM_EPS) + ADAM_WD * w), nm, nv


def update_from_slots(lands, offs, w, m, v, transposed, name):
    layers, a, b = w.shape
    n_land = len(lands)
    if transposed:
        rb, tk = LANE, 512
        assert a % tk == 0 and b % rb == 0 and all(o % rb == 0 for o in offs), (name, w.shape, offs)
        grid = (layers, a // tk, b // rb)
        land_block = (N_DEV, rb, tk)
        tile = pl.BlockSpec((None, tk, rb), lambda l, i, j: (l, i, j))

        def land_spec(layer):
            base = offs[layer] // rb
            return pl.BlockSpec(land_block, lambda l, i, j: (0, base + jnp.where(l == layer, j, 0), jnp.where(l == layer, i, 0)))
    else:
        fits = [t for t in (256, 128, 64) if a % t == 0 and all(o % t == 0 for o in offs)]
        assert fits or all(o == 0 for o in offs), (name, w.shape, offs)
        tr = max(fits) if fits else a
        grid = (layers, a // tr)
        land_block = (N_DEV, _round_up(tr, MEMBER_ROW_TILE), b)
        tile = pl.BlockSpec((None, tr, b), lambda l, i: (l, i, 0))

        def land_spec(layer):
            base = offs[layer] // tr
            return pl.BlockSpec(land_block, lambda l, i: (0, base + jnp.where(l == layer, i, 0), 0))

    def body(*refs):
        land_refs, (w_ref, m_ref, v_ref, g_ref, d_ref, nm_ref, nv_ref, acc) = refs[:n_land], refs[n_land:]
        for layer, land in enumerate(land_refs):
            @pl.when(pl.program_id(0) == layer)
            def _(land=land):
                rows = acc.shape[0]
                s = land[0, :rows].astype(F32)
                for k in range(1, N_DEV):
                    s = s + land[k, :rows].astype(F32)
                acc[...] = s

        g = acc[...].T if transposed else acc[...]
        d, nm, nv = _adamw_math(w_ref[...], m_ref[...], v_ref[...], g)
        g_ref[...] = g
        d_ref[...] = d
        nm_ref[...] = nm
        nv_ref[...] = nv

    sh = jax.ShapeDtypeStruct(w.shape, F32)
    return pl.pallas_call(
        body, name=name, grid=grid, in_specs=[land_spec(layer) for layer in range(n_land)] + [tile] * 3, out_specs=[tile] * 4,
        out_shape=[sh] * 4, scratch_shapes=[pltpu.VMEM((rb, tk) if transposed else (tr, b), F32)],
        compiler_params=_params())(*lands, w, m, v)


def adamw_many(ws, ms, vs, gs, name):
    n = len(ws)

    def body(*refs):
        for i in range(n):
            d, nm, nv = _adamw_math(refs[i][...], refs[n + i][...], refs[2 * n + i][...], refs[3 * n + i][...])
            refs[4 * n + i][...] = d
            refs[5 * n + i][...] = nm
            refs[6 * n + i][...] = nv

    vmem = pl.BlockSpec(memory_space=pltpu.VMEM)
    shapes = [jax.ShapeDtypeStruct(a.shape, F32) for a in ws]
    res = pl.pallas_call(body, name=name, in_specs=[vmem] * (4 * n), out_specs=[vmem] * (3 * n), out_shape=shapes * 3,
                         compiler_params=_params())(*ws, *ms, *vs, *gs)
    return res[:n], res[n:2 * n], res[2 * n:]


def seg_in(x, g):
    return (_rms(x, g),)


def seg_in_res(x, g):
    return x, _rms(x, g)


def seg_res(x, m, ga, gb):
    x1 = x + _rms(m, ga)
    return x1, _rms(x1, gb)


def seg_out(x, m, ga):
    return (x + _rms(m, ga),)


def act_epilogue(r):
    t = jnp.maximum(r, 0.0)
    return r, t * t


def res_epilogue(m, x, ga, gb):
    x1, h = seg_res(x, m, ga, gb)
    return m, x1, h


def act_bwd_epilogue(drr, r):
    return (drr * (2.0 * jnp.maximum(r, 0.0)),)


def seg_ln(v, g, b):
    mu = jnp.mean(v, axis=-1, keepdims=True)
    var = jnp.mean(jnp.square(v - mu), axis=-1, keepdims=True)
    vn = (v - mu) * lax.rsqrt(var + LN_EPS) * g + b
    return (jax.nn.silu(vn),)


def make_pool_fn(group):
    window = 2 ** (group + 1)

    def pool_fn(ug, pw, scale):
        s = ug
        for lvl in range(group + 1):
            s = s + shift(s, 2 ** lvl)
        cnt = jnp.minimum(lax.broadcasted_iota(jnp.int32, ug.shape, 0) + 1, window).astype(F32)
        return (bdot(s / cnt - ug, pw, 1, 0) * scale,)

    return pool_fn


def conv4_fn(xr, w, b):
    return (jax.nn.silu(cconv(xr, w, SSM_CONV) + b),)


def cd1_fn(u, dww, dwb, scw):
    val, gate, bg, cg, hh = (u[:, k * LANE:(k + 1) * LANE] for k in range(5))
    v = val * jax.nn.sigmoid(gate)
    vc = cconv(v, dww, CONF_K) + dwb
    sc = bg * cconv(cg * hh, scw, SC_K)
    return vc, sc


def attn_fn(q, kv):
    outs = []
    for h in range(XA_HEADS):
        cols = slice(h * XA_DH, (h + 1) * XA_DH)
        s = bdot(q[:, cols], kv[:, cols], 1, 1) / math.sqrt(XA_DH)
        p = jax.nn.softmax(s, axis=-1)
        outs.append(bdot(p, kv[:, D + h * XA_DH:D + (h + 1) * XA_DH], 1, 0))
    return (jnp.concatenate(outs, axis=1),)


def ssd_chunk(xbc, z, dtraw, dtb, alog, dsk, nw, h0, h1, h2, h3, e64, e64t, ecat, ecatt, tril, trilt):
    xs, bm, cm = xbc[:, :SSM_GSZ], xbc[:, SSM_GSZ:SSM_GSZ + SSM_N], xbc[:, SSM_GSZ + SSM_N:]
    hin = (h0, h1, h2, h3)
    dt = jax.nn.softplus(dtraw + dtb)
    a = -jnp.exp(alog)
    d_a = dt * a
    cs = cmatl(tril, trilt, d_a)
    cs_cat = cmat(cs, ecat, ecatt)
    cs64, cs128 = cs_cat[:, :SSM_GSZ], cs_cat[:, SSM_GSZ:]
    dt64 = cmat(dt, e64, e64t)
    row = lax.broadcasted_iota(jnp.int32, (8, LANE), 0)
    heads = jnp.where(row == 0, dsk, jnp.where(row == 1, jnp.sum(d_a, axis=0, keepdims=True), 0.0))
    heads64 = cmat(heads, e64, e64t)
    d64, tot64 = heads64[0:1, :], heads64[1:2, :]
    xdt = xs * dt64
    cb = bdot(cm, bm, 1, 1)
    li = lax.broadcasted_iota(jnp.int32, (CHUNK, CHUNK), 0)
    si = lax.broadcasted_iota(jnp.int32, (CHUNK, CHUNK), 1)
    causal = li >= si
    lane = lax.broadcasted_iota(jnp.int32, (CHUNK, LANE), 1)
    xw = xdt * jnp.exp(tot64 - cs64)
    ecs = jnp.exp(cs64)
    etot = jnp.exp(tot64)
    ycols, hout = [], []
    for j in range(4):
        sl = slice(j * LANE, (j + 1) * LANE)
        xj = xdt[:, sl]
        ys = []
        for hh in range(2):
            r = 2 * j + hh
            col = cs128[:, r * LANE:(r + 1) * LANE]
            decay = jnp.exp(jnp.where(causal, col - col.T, -1e30))
            ys.append(bdot(cb * decay, xj, 1, 0))
        y_diag = jnp.where(lane < SSM_P, ys[0], ys[1])
        y_off = bdot(cm, hin[j], 1, 0) * ecs[:, sl]
        ycols.append(y_diag + y_off)
        hout.append(etot[:, sl] * hin[j] + bdot(bm, xw[:, sl], 0, 0))
    y = jnp.concatenate(ycols, axis=1) + d64 * xs
    y = y * jax.nn.silu(z)
    yn = y * lax.rsqrt(jnp.mean(y * y, axis=-1, keepdims=True) + RMS_EPS) * nw
    return (yn,) + tuple(hout)


def _xbc_group(a, axis):
    parts = []
    for g in range(SSM_GROUPS):
        for start, width in ((g * SSM_GSZ, SSM_GSZ), (SSM_INNER + g * SSM_N, SSM_N), (SSM_INNER + (SSM_GROUPS + g) * SSM_N, SSM_N)):
            parts.append(lax.slice_in_dim(a, start, start + width, axis=axis))
    return jnp.concatenate(parts, axis=axis)


def _xbc_ungroup(a, axis):
    xs, bs, cs = [], [], []
    for g in range(SSM_GROUPS):
        base = g * SSM_XBC_G
        xs.append(lax.slice_in_dim(a, base, base + SSM_GSZ, axis=axis))
        bs.append(lax.slice_in_dim(a, base + SSM_GSZ, base + SSM_GSZ + SSM_N, axis=axis))
        cs.append(lax.slice_in_dim(a, base + SSM_GSZ + SSM_N, base + SSM_XBC_G, axis=axis))
    return jnp.concatenate(xs + bs + cs, axis=axis)


def _ssd_consts():
    h = np.arange(LANE)[:, None]
    e64 = np.stack([(h == g * 8 + np.arange(SSM_GSZ)[None, :] // SSM_P) for g in range(SSM_GROUPS)]).astype(np.float32)
    e128 = np.stack([(h == g * 8 + np.arange(8 * LANE)[None, :] // LANE) for g in range(SSM_GROUPS)]).astype(np.float32)
    ecat = np.concatenate([e64, e128], axis=2)
    tril = np.tril(np.ones((CHUNK, CHUNK), np.float32))
    return tuple(jnp.asarray(c, dtype=BF) for c in (e64, e64.transpose(0, 2, 1), ecat, ecat.transpose(0, 2, 1), tril, tril.T))


def _ssd_specs(nc, rev):
    def ci(c):
        return nc - 1 - c if rev else c

    def row(width, col):
        return pl.BlockSpec((CHUNK, width), lambda b, c: (b * nc + ci(c), col))

    def whole(shape):
        return pl.BlockSpec(shape, lambda b, c: (0,) * len(shape))

    data = [row(SSM_CONV_DIM, 0),
            row(SSM_GSZ, 1), row(SSM_GSZ, 2), row(LANE, 24)]
    par = [whole((1, LANE))] * 3 + [whole((1, SSM_INNER))]
    cst = [whole((SSM_GROUPS, LANE, SSM_GSZ)), whole((SSM_GROUPS, SSM_GSZ, LANE)), whole((SSM_GROUPS, LANE, 12 * LANE)),
           whole((SSM_GROUPS, 12 * LANE, LANE)), whole((CHUNK, CHUNK)), whole((CHUNK, CHUNK))]
    hsave = pl.BlockSpec((None, None, SSM_GROUPS, 4, SSM_N, LANE), lambda b, c: (b, ci(c), 0, 0, 0, 0))
    return data, par, cst, hsave, row, whole


def _ssd_group_args(g, xbc, z, dtr, dtb, alog, dsk, nw):
    return (xbc[:, g * SSM_XBC_G:(g + 1) * SSM_XBC_G], z[g], dtr, dtb, alog, dsk, nw[:, g * SSM_GSZ:(g + 1) * SSM_GSZ])


def ssd_fwd(xbc_act, u, dtb, alog, dsk, nw, consts, bsz, seq):
    nc = seq // CHUNK
    data, par, cst, hsave, row, _ = _ssd_specs(nc, False)

    def body(xbc, z0, z1, dtr, dtb_r, alog_r, dsk_r, nw_r, e64, e64t, ecat, ecatt, tril, trilt, yn_ref, hs_ref, h):
        @pl.when(pl.program_id(1) == 0)
        def _():
            h[...] = jnp.zeros_like(h)

        hs_ref[...] = h[...]
        ys = []
        for g in range(SSM_GROUPS):
            args = _ssd_group_args(g, xbc[...], (z0[...], z1[...]), dtr[...], dtb_r[...], alog_r[...], dsk_r[...], nw_r[...])
            outs = ssd_chunk(*args, h[g, 0], h[g, 1], h[g, 2], h[g, 3], e64[g], e64t[g], ecat[g], ecatt[g], tril[...], trilt[...])
            ys.append(outs[0])
            for j in range(4):
                h[g, j] = outs[1 + j]
        yn_ref[...] = jnp.concatenate(ys, axis=1).astype(yn_ref.dtype)

    t = bsz * seq
    return pl.pallas_call(
        body, name="ssd_fwd", grid=(bsz, nc), in_specs=data + par + cst, out_specs=[row(SSM_INNER, 0), hsave],
        out_shape=[jax.ShapeDtypeStruct((t, SSM_INNER), BF), jax.ShapeDtypeStruct((bsz, nc, SSM_GROUPS, 4, SSM_N, LANE), F32)],
        scratch_shapes=[pltpu.VMEM((SSM_GROUPS, 4, SSM_N, LANE), F32)], compiler_params=_params(),
    )(xbc_act, u, u, u, dtb, alog, dsk, nw, *consts)


def ssd_bwd(xbc_act, u, dtb, alog, dsk, nw, consts, hs, dmix, bsz, seq):
    nc = seq // CHUNK
    data, par, cst, hsave, row, whole = _ssd_specs(nc, True)
    t = bsz * seq
    pcol = POOL_W // SSM_GSZ

    def body(xbc, z0, z1, dtr, dtb_r, alog_r, dsk_r, nw_r, e64, e64t, ecat, ecatt, tril, trilt, hs_ref, dy0, dy1,
             dxbc, dz, ddt, ddtb, dalog, ddsk, dnw, dh):
        @pl.when(pl.program_id(1) == 0)
        def _():
            dh[...] = jnp.zeros_like(dh)

        per_group = []
        for g, dyn in enumerate((dy0, dy1)):
            cst_vals = (e64[g], e64t[g], ecat[g], ecatt[g], tril[...], trilt[...])
            prim = _ssd_group_args(g, xbc[...], (z0[...], z1[...]), dtr[...], dtb_r[...], alog_r[...], dsk_r[...], nw_r[...])
            prim = prim + (hs_ref[g, 0], hs_ref[g, 1], hs_ref[g, 2], hs_ref[g, 3])
            _, vjp = jax.vjp(lambda *args, c=cst_vals: ssd_chunk(*args, *c), *prim)
            gr = vjp((dyn[...].astype(F32), dh[g, 0], dh[g, 1], dh[g, 2], dh[g, 3]))
            for j in range(4):
                dh[g, j] = gr[7 + j]
            per_group.append(gr)
        g0, g1 = per_group
        dxbc[...] = jnp.concatenate([g0[0], g1[0]], axis=1)
        dz[...] = jnp.concatenate([g0[1], g1[1]], axis=1).astype(dz.dtype)
        ddt[...] = g0[2] + g1[2]

        @pl.when(_first((0, 1)))
        def _():
            for r in (ddtb, dalog, ddsk, dnw):
                r[...] = jnp.zeros_like(r)

        ddtb[...] += g0[3] + g1[3]
        dalog[...] += g0[4] + g1[4]
        ddsk[...] += g0[5] + g1[5]
        dnw[...] += jnp.concatenate([g0[6], g1[6]], axis=1)

    out_specs = [row(SSM_CONV_DIM, 0), row(SSM_INNER, 0), row(LANE, 0), whole((1, LANE)), whole((1, LANE)), whole((1, LANE)),
                 whole((1, SSM_INNER))]
    lane = jax.ShapeDtypeStruct((1, LANE), F32)
    out_shape = [jax.ShapeDtypeStruct((t, SSM_CONV_DIM), F32), jax.ShapeDtypeStruct((t, SSM_INNER), BF),
                 jax.ShapeDtypeStruct((t, LANE), F32), lane, lane, lane, jax.ShapeDtypeStruct((1, SSM_INNER), F32)]
    return pl.pallas_call(
        body, name="ssd_bwd", grid=(bsz, nc), in_specs=data + par + cst + [hsave, row(SSM_GSZ, pcol), row(SSM_GSZ, pcol + 1)],
        out_specs=out_specs, out_shape=out_shape, scratch_shapes=[pltpu.VMEM((SSM_GROUPS, 4, SSM_N, LANE), F32)],
        compiler_params=_params(),
    )(xbc_act, u, u, u, dtb, alog, dsk, nw, *consts, hs, dmix, dmix)


TB = 512


def _rows(d, col=0):
    return pl.BlockSpec((TB, d), lambda i: (i, col))


def _par(d):
    return pl.BlockSpec((1, d), lambda i: (0, 0))


def _sd(shape, dtype=F32):
    return jax.ShapeDtypeStruct(shape, dtype)


def _round_up(n, m):
    return -(-n // m) * m


def _pad_rows(a, rows):
    return jnp.pad(a, ((0, rows - a.shape[0]), (0, 0)))


def _pack128(arrs):
    flat = jnp.concatenate([a.reshape(-1) for a in arrs])
    n = flat.shape[0]
    rows = -(-n // (8 * LANE)) * 8
    return jnp.pad(flat, (0, rows * LANE - n)).reshape(rows, LANE)


def _unpack128(packed, shapes):
    flat = packed.reshape(-1)
    out, off = [], 0
    for s in shapes:
        n = int(np.prod(s))
        out.append(flat[off:off + n].reshape(s))
        off += n
    return out


def kernel(x, mem, norm_gains, xa_wq, xa_wkv, xa_wo, mlp_w1, mlp_w2, ab_w_in, pool_w, pool_scale, ssm_conv_w, ssm_conv_b, ssm_dt_bias, ssm_a_log, ssm_d, ssm_norm, ab_w_out, cd_w_in, conf_dw_w, conf_dw_b, conf_ln_g, conf_ln_b, sc_conv_w, cd_w_out, loss_target, m_norm_gains, m_xa_wq, m_xa_wkv, m_xa_wo, m_mlp_w1, m_mlp_w2, m_ab_w_in, m_pool_w, m_pool_scale, m_ssm_conv_w, m_ssm_conv_b, m_ssm_dt_bias, m_ssm_a_log, m_ssm_d, m_ssm_norm, m_ab_w_out, m_cd_w_in, m_conf_dw_w, m_conf_dw_b, m_conf_ln_g, m_conf_ln_b, m_sc_conv_w, m_cd_w_out, v_norm_gains, v_xa_wq, v_xa_wkv, v_xa_wo, v_mlp_w1, v_mlp_w2, v_ab_w_in, v_pool_w, v_pool_scale, v_ssm_conv_w, v_ssm_conv_b, v_ssm_dt_bias, v_ssm_a_log, v_ssm_d, v_ssm_norm, v_ab_w_out, v_cd_w_in, v_conf_dw_w, v_conf_dw_b, v_conf_ln_g, v_conf_ln_b, v_sc_conv_w, v_cd_w_out):
    args = locals()
    w = {n: args[n] for n in WEIGHTS}
    mom_m = {n: args["m_" + n] for n in WEIGHTS}
    mom_v = {n: args["v_" + n] for n in WEIGHTS}
    ex = Exchange(w)
    loss_local, grad_x, small_grads = local_step(x, mem, loss_target, ex)
    outs = {}

    started = ex.put_small(small_grads, loss_local)
    landed = {key: ex.landed(key, started) for key in ('l1', 'cd', 'l0')}
    late = []
    for n, keys in (('mlp_w1', ('l0', 'l1')), ('mlp_w2', ('l0', 'l1')), ('xa_wkv', ('l0', 'l1')), ('xa_wq', ('l0', 'l1')),
                    ('xa_wo', ('l0', 'l1')), ('cd_w_in', ('cd',)), ('cd_w_out', ('cd',))):
        lands = [landed[key][0] for key in keys]
        offs = [landed[key][1][(n, layer)] for layer, key in enumerate(keys)]
        outs[n] = update_from_slots(lands, offs, w[n], mom_m[n], mom_v[n], SHARD_AXIS[n] == 2, "update_" + n)
        late.append(outs[n][1])
    g_own, loss = ex.reduced_small(late)
    land_ab, offs_ab = ex.landed('ab', late)
    outs['ab_w_out'] = update_from_slots([land_ab], [offs_ab[('ab_w_out', 0)]], w['ab_w_out'], mom_m['ab_w_out'],
                                         mom_v['ab_w_out'], False, "update_ab_w_out")
    res = update_from_slots([land_ab], [offs_ab[('ab_w_in', 0)]], jnp.swapaxes(w['ab_w_in'], 1, 2), jnp.swapaxes(mom_m['ab_w_in'], 1, 2),
                            jnp.swapaxes(mom_v['ab_w_in'], 1, 2), False, "update_ab_w_in")
    outs['ab_w_in'] = tuple(jnp.swapaxes(r, 1, 2) for r in res)
    small = SMALL_SHARDED + REPLICATED
    upd = adamw_many([w[n] for n in small], [mom_m[n] for n in small], [mom_v[n] for n in small], [g_own[n] for n in small],
                     "adamw_small")
    for i, n in enumerate(small):
        outs[n] = (g_own[n], upd[0][i], upd[1][i], upd[2][i])
    return (loss, grad_x.reshape(x.shape), *[outs[n][0] for n in WEIGHTS], *[outs[n][1] for n in WEIGHTS],
            *[outs[n][2] for n in WEIGHTS], *[outs[n][3] for n in WEIGHTS])


G_AB = (('ab_w_in', 0), ('ab_w_out', 0))
G_L0 = (('xa_wq', 0), ('xa_wkv', 0), ('xa_wo', 0), ('mlp_w1', 0), ('mlp_w2', 0))
G_L1 = (('xa_wq', 1), ('xa_wkv', 1), ('xa_wo', 1), ('mlp_w1', 1), ('mlp_w2', 1))
G_CD = (('cd_w_in', 0), ('cd_w_out', 0))
GATHER_GROUPS = {'ab': G_AB, 'l0a': G_L0[:3], 'l0b': G_L0[3:], 'cd': G_CD, 'l1a': G_L1[:3], 'l1b': G_L1[3:]}
SHARD_AXIS = dict(BIG)
MEMBER_ROW_TILE = 64
FLAT_ROW_TILE = 128


def _members(group, w):
    out = []
    for n, layer in group:
        shp = w[n].shape[1:]
        if SHARD_AXIS[n] == 2:
            shp = (shp[1], shp[0])
        assert shp[1] == D, (n, shp)
        out.append((n, layer, shp, shp[0], _round_up(shp[0], MEMBER_ROW_TILE)))
    return out


def _group_rows(group, w):
    return _round_up(sum(m[4] for m in _members(group, w)), FLAT_ROW_TILE)


def _flat_shards(group, w):
    parts = []
    for n, layer, _, _, padded in _members(group, w):
        shard = w[n][layer].astype(BF)
        parts.append(_pad_rows(shard.T if SHARD_AXIS[n] == 2 else shard, padded))
    return _pad_rows(jnp.concatenate(parts, axis=0), _group_rows(group, w))


def _full_from_slots(land, group, w):
    out, off = {}, 0
    for n, layer, shp, rows, padded in _members(group, w):
        out[(n, layer)] = land[:, off:off + rows].reshape(N_DEV * rows, D)
        off += padded
    return out


def _slots_from_full(grads, group, w):
    parts = []
    for n, layer, shp, rows, padded in _members(group, w):
        blk = grads[(n, layer)].astype(BF).reshape(N_DEV, rows, D)
        parts.append(jnp.pad(blk, ((0, 0), (0, padded - rows), (0, 0))))
    send = jnp.concatenate(parts, axis=1)
    return jnp.pad(send, ((0, 0), (0, _group_rows(group, w) - send.shape[1]), (0, 0)))


_HBM = pl.BlockSpec(memory_space=pltpu.HBM)
_SEM = pl.BlockSpec(memory_space=pltpu.SEMAPHORE)
_ANY = pl.BlockSpec(memory_space=pl.ANY)


def _peer_copy(k, src, dst, send_sems, recv_sems, peer):
    return pltpu.make_async_remote_copy(src_ref=src, dst_ref=dst, send_sem=send_sems.at[k], recv_sem=recv_sems.at[k],
                                        device_id=peer, device_id_type=pl.DeviceIdType.MESH)


def exchange_start(src, name, scatter, after=()):
    shape = src.shape[-2:]
    after = list(after)

    def body(src_ref, land_ref, *rest):
        send_sems, recv_sems, token = rest[len(after)], rest[len(after) + 1], rest[-1]
        me = _me()
        for k, f in enumerate(_FLIPS):
            peer = _flip(me, f)
            piece = src_ref.at[_slot(peer)] if scatter else src_ref
            _peer_copy(k, piece, land_ref.at[_slot(me)], send_sems, recv_sems, peer).start()
        token[...] = jnp.zeros_like(token)

    land = pltpu.with_memory_space_constraint(lax.empty((N_DEV,) + shape, src.dtype), pltpu.HBM)
    return pl.pallas_call(
        body, name=name,
        out_shape=(pltpu.SemaphoreType.DMA((7,)), pltpu.SemaphoreType.DMA((7,)), pltpu.HBM(src.shape, src.dtype),
                   pltpu.HBM((N_DEV,) + shape, src.dtype), jax.ShapeDtypeStruct((8, LANE), F32)),
        in_specs=(_HBM, _HBM) + (_ANY,) * len(after), out_specs=(_SEM, _SEM, _HBM, _HBM, pl.BlockSpec(memory_space=pltpu.VMEM)),
        input_output_aliases={0: 2, 1: 3},
        compiler_params=pltpu.CompilerParams(has_side_effects=pltpu.SideEffectType.DATAFLOW_SIDE_EFFECTING),
    )(pltpu.with_memory_space_constraint(src, pltpu.HBM), land, *after)


def exchange_wait(handles, after, name, scatter):
    send_sems, recv_sems, src_thru, land_thru, _ = handles
    after = list(after) if isinstance(after, (list, tuple)) else [after]

    def body(src_ref, land_ref, send_sems, recv_sems, *rest):
        token = rest[-1]
        me = _me()
        for k, f in enumerate(_FLIPS):
            peer = _flip(me, f)
            piece = src_ref.at[_slot(peer)] if scatter else src_ref
            cp = _peer_copy(k, piece, land_ref.at[_slot(peer)], send_sems, recv_sems, peer)
            cp.wait_send()
            cp.wait_recv()
        token[...] = jnp.zeros_like(token)

    return pl.pallas_call(
        body, name=name, out_shape=(pltpu.HBM(src_thru.shape, src_thru.dtype), pltpu.HBM(land_thru.shape, land_thru.dtype),
                                    jax.ShapeDtypeStruct((8, LANE), F32)),
        in_specs=(_HBM, _HBM, _SEM, _SEM) + (_ANY,) * len(after), out_specs=(_HBM, _HBM, pl.BlockSpec(memory_space=pltpu.VMEM)),
        input_output_aliases={0: 0, 1: 1},
        compiler_params=pltpu.CompilerParams(has_side_effects=pltpu.SideEffectType.DATAFLOW_SIDE_EFFECTING),
    )(src_thru, land_thru, send_sems, recv_sems, *after)


class Exchange:
    def __init__(self, w):
        self.w = w
        self.me = _slot(_me())
        shapes = [w[n].shape for n in SMALL_SHARDED]
        gs = all_gather(_pack128([w[n] for n in SMALL_SHARDED]), "gather_small")
        per_dev = [_unpack128(gs[d], shapes) for d in range(N_DEV)]
        self.small = {n: jnp.concatenate([per_dev[d][i] for d in range(N_DEV)], axis=-1) for i, n in enumerate(SMALL_SHARDED)}
        self.small.update({n: w[n] for n in REPLICATED})
        self.first = _full_from_slots(all_gather(_flat_shards(G_AB, w), "gather_ab"), G_AB, w)
        self.gathers, self.done, self.tokens, self.reductions = {}, {}, [], {}
        self.start_gather('l0a')
        self.start_gather('l0b')

    def take_tokens(self):
        toks, self.tokens = self.tokens, []
        return toks

    def start_gather(self, key, after=()):
        group = GATHER_GROUPS[key]
        self.gathers[key] = exchange_start(_flat_shards(group, self.w), f"gather_{key}_start", False, after=after)
        self.tokens.append(self.gathers[key][4])

    def weights(self, key, after):
        if key == 'ab':
            return self.first
        handles = self.gathers[key]
        _, land, self.done[key] = exchange_wait(handles, after, f"gather_{key}_wait", False)
        land = lax.dynamic_update_slice(land, handles[2][None], (self.me, 0, 0))
        return _full_from_slots(land, GATHER_GROUPS[key], self.w)

    def put_grads(self, key, group, grads):
        send = _slots_from_full(grads, group, self.w)
        handles = exchange_start(send, f"reduce_{key}_start", True)
        self.reductions[key] = (group, handles)
        self.tokens.append(handles[4])

    def landed(self, key, after):
        group, handles = self.reductions[key]
        send, land, _ = exchange_wait(handles, after, f"reduce_{key}_wait", True)
        mine = lax.dynamic_slice_in_dim(send, self.me, 1, axis=0)
        land = lax.dynamic_update_slice(land, mine, (self.me, 0, 0))
        offs, off = {}, 0
        for n, layer, _, _, padded in _members(group, self.w):
            offs[(n, layer)] = off
            off += padded
        return land, offs

    def put_small(self, small_grads, loss_local):
        small = SMALL_SHARDED + REPLICATED
        self.small_shapes = [small_grads[n].shape for n in small] + [(1,)]
        packed = _pack128([small_grads[n] for n in small] + [loss_local.reshape(1)])
        self.small_handles = exchange_start(packed, "gather_small_grads_start", False)
        return self.small_handles[4]

    def reduced_small(self, after):
        small = SMALL_SHARDED + REPLICATED
        src, land, _ = exchange_wait(self.small_handles, after, "gather_small_grads_wait", False)
        gs = lax.dynamic_update_slice(land, src[None], (self.me, 0, 0))
        tot = _unpack128(sum_slots(gs, "sum_small", 1024), self.small_shapes)
        out = {}
        for n, g in zip(small, tot):
            if n in SMALL_SHARDED:
                width = self.w[n].shape[-1]
                g = lax.dynamic_slice_in_dim(g, self.me * width, width, axis=g.ndim - 1)
            out[n] = g
        return out, tot[-1].reshape(())


def local_step(x, mem, target, ex):
    bsz, seq, _ = x.shape
    t = bsz * seq
    nb = t // TB
    nc = seq // CHUNK
    x0 = x.reshape(t, D)
    mem2 = mem.reshape(bsz * N_MEM, D)
    tgt = target.reshape(t, D)
    p = ex.small
    gains = p['norm_gains']
    big = {}

    def gain(layer, i):
        g = gains[layer, i].reshape(1, D)
        for tok in ex.take_tokens():
            g = g + tok[0, 0]
        return g

    consts = _ssd_consts()
    grads = {}
    saved = [dict(), dict()]

    def matmul_res(a, b, name, xin, ga, gb):
        return matmul(a, b, 'nn', name, (F32, F32, BF), epilogue=res_epilogue, extras=[xin], params=[ga, gb])

    def attn_specs():
        nq = seq // TB
        q = pl.BlockSpec((TB, D), lambda b, i: (b * nq + i, 0))
        kv = pl.BlockSpec((N_MEM, 2 * D), lambda b, i: (b, 0))
        return (bsz, nq), q, kv

    def attention_fwd(layer, xin, hin, sv, ga, gb):
        q = matmul(hin, big[('xa_wq', layer)], 'nn', f"q_{layer}", BF)
        kv = matmul(mem2, big[('xa_wkv', layer)], 'nt', f"kv_{layer}", BF)
        grid, qs, kvs = attn_specs()
        o, = fwd_call(attn_fn, f"attn_{layer}", grid, [q, kv], [qs, kvs], [_sd((t, D), BF)], [qs])
        ao, x_next, h_next = matmul_res(o, big[('xa_wo', layer)], f"ao_{layer}", xin, ga, gb)
        sv.update(q=q, kv=kv, o=o, ao=ao)
        return ao, x_next, h_next

    def mlp_fwd(layer, hin, sv, res=None):
        r, rr = matmul(hin, big[('mlp_w1', layer)], 'nt', f"mlp1_{layer}", (BF, BF), epilogue=act_epilogue)
        if res is None:
            out = (matmul(rr, big[('mlp_w2', layer)], 'nn', f"mlp2_{layer}"),)
        else:
            out = matmul_res(rr, big[('mlp_w2', layer)], f"mlp2_{layer}", *res)
        sv.update(r=r, rr=rr, mo=out[0])
        return out

    sv = saved[0]
    h0, = fwd_call(seg_in, "norm_in", (nb,), [x0, gain(0, 0)], [_rows(D), _par(D)], [_sd((t, D), BF)], [_rows(D)])
    big.update(ex.weights('ab', h0))
    xbc0 = POOL_W + SSM_INNER
    w_ab_in = big[('ab_w_in', 0)]
    w_ab_in = _pad_rows(jnp.concatenate([w_ab_in[:xbc0], _xbc_group(w_ab_in[xbc0:xbc0 + SSM_CONV_DIM], 0),
                                         w_ab_in[xbc0 + SSM_CONV_DIM:]], axis=0), AB_IN_PAD)
    conv_w, conv_b = _xbc_group(p['ssm_conv_w'][0], 1), _xbc_group(p['ssm_conv_b'], 1)
    u0 = matmul(h0, w_ab_in, 'nt', "ab_in")
    pool_outs = []
    for g in range(POOL_GROUPS):
        seqspec = pl.BlockSpec((seq, PG), lambda b, g=g: (b, g))
        po, = fwd_call(make_pool_fn(g), f"pool_{g}", (bsz,), [u0, p['pool_w'][0, g], p['pool_scale']],
                       [seqspec, pl.BlockSpec((PG, PG), lambda b: (0, 0)), pl.BlockSpec((1, PG), lambda b, g=g: (0, g))],
                       [_sd((t, PG), BF)], [pl.BlockSpec((seq, PG), lambda b: (b, 0))])
        pool_outs.append(po)
    cw = 256
    ncb = SSM_CONV_DIM // cw
    cbase = (POOL_W + SSM_INNER) // cw
    conv_in_specs = [pl.BlockSpec((seq, cw), lambda j, b: (b, cbase + j)), pl.BlockSpec((SSM_CONV, cw), lambda j, b: (0, j)),
                     pl.BlockSpec((1, cw), lambda j, b: (0, j))]
    conv_out_spec = pl.BlockSpec((seq, cw), lambda j, b: (b, j))
    xbc_act, = fwd_call(conv4_fn, "ssm_conv", (ncb, bsz), [u0, conv_w, conv_b], conv_in_specs,
                        [_sd((t, SSM_CONV_DIM))], [conv_out_spec])
    dtb = jnp.pad(p['ssm_dt_bias'], ((0, 0), (0, LANE - SSM_HEADS)))
    alog = jnp.pad(p['ssm_a_log'], ((0, 0), (0, LANE - SSM_HEADS)))
    dsk = jnp.pad(p['ssm_d'], ((0, 0), (0, LANE - SSM_HEADS)))
    yn, hs = ssd_fwd(xbc_act, u0, dtb, alog, dsk, p['ssm_norm'], consts, bsz, seq)
    mix0 = jnp.concatenate(pool_outs + [yn], axis=1)
    m0, x1, h2 = matmul_res(mix0, big[('ab_w_out', 0)], "ab_out", x0, gain(0, 1), gain(0, 2))
    big.update(ex.weights('l0a', h2))
    ex.start_gather('cd', after=[ex.done['l0a']])
    ao0, x2, h3 = attention_fwd(0, x1, h2, sv, gain(0, 3), gain(0, 4))
    big.update(ex.weights('l0b', h3))
    mo0, x3, h4 = mlp_fwd(0, h3, sv, (x2, gain(0, 5), gain(1, 0)))
    big.update(ex.weights('cd', mo0))
    ex.start_gather('l1a', after=[ex.done['cd']])

    sv1 = saved[1]
    nd = D // LANE
    w_cd_in = big[('cd_w_in', 0)].reshape(5, nd, LANE, D).transpose(1, 0, 2, 3).reshape(CD_IN, D)
    u1 = matmul(h4, w_cd_in, 'nt', "cd_in")
    cd_par = [pl.BlockSpec((CONF_K, LANE), lambda j, b: (0, j)), pl.BlockSpec((1, LANE), lambda j, b: (0, j)),
              pl.BlockSpec((SC_K, LANE), lambda j, b: (0, j))]
    cd_ins = [u1, p['conf_dw_w'][0], p['conf_dw_b'], p['sc_conv_w'][0]]
    cd_u_spec = pl.BlockSpec((seq, 5 * LANE), lambda j, b: (b, j))
    cd_in_specs = [cd_u_spec] + cd_par
    cd_out_spec = pl.BlockSpec((seq, LANE), lambda j, b: (b, j))
    vconv, mix1 = fwd_call(cd1_fn, "cd_conv", (nd, bsz), cd_ins, cd_in_specs, [_sd((t, D)), _sd((t, CD_OUT), BF)],
                           [cd_out_spec, pl.BlockSpec((seq, LANE), lambda j, b: (b, nd + j))])
    mix1, = fwd_call(seg_ln, "conf_ln", (nb,), [vconv, p['conf_ln_g'], p['conf_ln_b']], [_rows(D), _par(D), _par(D)],
                     [_sd((t, CD_OUT), BF)], [_rows(D)], into=mix1)
    ex.start_gather('l1b', after=[vconv])
    m1, x4, h5 = matmul_res(mix1, big[('cd_w_out', 0)], "cd_out", x3, gain(1, 1), gain(1, 2))
    big.update(ex.weights('l1a', h5))
    ao1, x5, h6 = attention_fwd(1, x4, h5, sv1, gain(1, 3), gain(1, 4))
    big.update(ex.weights('l1b', h6))
    mo1, = mlp_fwd(1, h6, sv1)

    def loss_body(x_ref, m_ref, g_ref, t_ref, dx_ref, dm_ref, dg_ref, acc_ref):
        (y,), vjp = jax.vjp(seg_out, x_ref[...], m_ref[...], g_ref[...])
        d = y - t_ref[...]
        dx, dm, dg = vjp((d / float(D),))
        dx_ref[...] = dx
        dm_ref[...] = dm.astype(dm_ref.dtype)

        @pl.when(pl.program_id(0) == 0)
        def _():
            acc_ref[...] = jnp.zeros_like(acc_ref)
            dg_ref[...] = jnp.zeros_like(dg_ref)

        acc_ref[...] += jnp.sum(d * d, axis=0, keepdims=True)
        dg_ref[...] += dg

    dx5, dmo1, dg15, lanes = pl.pallas_call(
        loss_body, name="loss_head", grid=(nb,), in_specs=[_rows(D), _rows(D), _par(D), _rows(D)],
        out_specs=[_rows(D), _rows(D), _par(D), _par(D)], out_shape=[_sd((t, D)), _sd((t, D), BF), _sd((1, D)), _sd((1, D))],
        compiler_params=_params())(x5, mo1, gain(1, 5), tgt)
    loss = 0.5 * jnp.sum(lanes) / float(D)

    gain_grads = {(1, 5): dg15}

    def bwd_seg_res(xin, m, ga, gb, dx1, dh, name):
        return bwd_call(seg_res, name, (nb,), [xin, m, ga, gb], [_rows(D), _rows(D), _par(D), _par(D)], [dx1, dh],
                        [_rows(D), _rows(D)], [0, 1, 2, 3], [_sd((t, D)), _sd((t, D), BF), _sd((1, D)), _sd((1, D))],
                        [_rows(D), _rows(D), _par(D), _par(D)], [None, None, (0,), (0,)])

    def mlp_bwd(layer, hin, dmo, sv):
        grads_w2 = matmul(sv['rr'], dmo, 'tn', f"d_mlp_w2_{layer}", BF)
        dr, = matmul(dmo, big[('mlp_w2', layer)], 'nt', f"d_r_{layer}", (BF,), epilogue=act_bwd_epilogue, extras=[sv['r']])
        grads_w1 = matmul(dr, hin, 'tn', f"d_mlp_w1_{layer}", BF)
        dh = matmul(dr, big[('mlp_w1', layer)], 'nn', f"d_h_mlp_{layer}")
        return dh, grads_w1, grads_w2

    def attention_bwd(layer, hin, dao, sv):
        g_wo = matmul(sv['o'], dao, 'tn', f"d_xa_wo_{layer}", BF)
        do = matmul(dao, big[('xa_wo', layer)], 'nt', f"d_o_{layer}", BF)
        grid, qs, kvs = attn_specs()
        dq, dkv = bwd_call(attn_fn, f"d_attn_{layer}", grid, [sv['q'], sv['kv']], [qs, kvs], [do], [qs], [0, 1],
                           [_sd((t, D), BF), _sd((bsz * N_MEM, 2 * D))], [qs, kvs], [None, (1,)])
        g_wkv = matmul(dkv, mem2, 'tn', f"d_xa_wkv_{layer}", BF)
        g_wq = matmul(hin, dq, 'tn', f"d_xa_wq_{layer}", BF)
        dh = matmul(dq, big[('xa_wq', layer)], 'nt', f"d_h_attn_{layer}")
        return dh, g_wq, g_wkv, g_wo

    per_layer = {k: [None, None] for k in ('xa_wq', 'xa_wkv', 'xa_wo', 'mlp_w1', 'mlp_w2')}

    dh6, per_layer['mlp_w1'][1], per_layer['mlp_w2'][1] = mlp_bwd(1, h6, dmo1, sv1)
    dx4, dao1, gain_grads[(1, 3)], gain_grads[(1, 4)] = bwd_seg_res(x4, ao1, gain(1, 3), gain(1, 4), dx5, dh6, "d_res_1b")
    dh5, per_layer['xa_wq'][1], per_layer['xa_wkv'][1], per_layer['xa_wo'][1] = attention_bwd(1, h5, dao1, sv1)
    ex.put_grads('l1', G_L1, {(k, 1): v[1] for k, v in per_layer.items()})
    dx3, dm1, gain_grads[(1, 1)], gain_grads[(1, 2)] = bwd_seg_res(x3, m1, gain(1, 1), gain(1, 2), dx4, dh5, "d_res_1a")
    g_cd_out = matmul(mix1, dm1, 'tn', "d_cd_w_out", BF)
    dmix1 = matmul(dm1, big[('cd_w_out', 0)], 'nt', "d_mix1")
    dvconv, dlg, dlb = bwd_call(seg_ln, "d_conf_ln", (nb,), [vconv, p['conf_ln_g'], p['conf_ln_b']],
                                [_rows(D), _par(D), _par(D)], [dmix1], [_rows(D, 0)], [0, 1, 2],
                                [_sd((t, D)), _sd((1, D)), _sd((1, D))], [_rows(D), _par(D), _par(D)], [None, (0,), (0,)])
    grads['conf_ln_g'], grads['conf_ln_b'] = dlg, dlb
    cd_g = bwd_call(cd1_fn, "d_cd_conv", (nd, bsz), cd_ins, cd_in_specs, [dvconv, dmix1],
                    [cd_out_spec, pl.BlockSpec((seq, LANE), lambda j, b: (b, nd + j))], list(range(4)),
                    [_sd((t, CD_IN), BF), _sd((CONF_K, D)), _sd((1, D)), _sd((SC_K, D))], [cd_u_spec] + cd_par,
                    [None, (1,), (1,), (1,)])
    du1 = cd_g[0]
    grads['conf_dw_w'], grads['conf_dw_b'], grads['sc_conv_w'] = cd_g[1][None], cd_g[2], cd_g[3][None]
    g_cd_in = matmul(du1, h4, 'tn', "d_cd_w_in", BF).reshape(nd, 5, LANE, D).transpose(1, 0, 2, 3).reshape(CD_IN, D)
    ex.put_grads('cd', G_CD, {('cd_w_in', 0): g_cd_in, ('cd_w_out', 0): g_cd_out})
    dh4 = matmul(du1, w_cd_in, 'nn', "d_h_cd")

    dx2, dmo0, gain_grads[(0, 5)], gain_grads[(1, 0)] = bwd_seg_res(x2, mo0, gain(0, 5), gain(1, 0), dx3, dh4, "d_res_0c")
    dh3, per_layer['mlp_w1'][0], per_layer['mlp_w2'][0] = mlp_bwd(0, h3, dmo0, sv)
    dx1, dao0, gain_grads[(0, 3)], gain_grads[(0, 4)] = bwd_seg_res(x1, ao0, gain(0, 3), gain(0, 4), dx2, dh3, "d_res_0b")
    dh2, per_layer['xa_wq'][0], per_layer['xa_wkv'][0], per_layer['xa_wo'][0] = attention_bwd(0, h2, dao0, sv)
    ex.put_grads('l0', G_L0, {(k, 0): v[0] for k, v in per_layer.items()})
    dx0r, dm0, gain_grads[(0, 1)], gain_grads[(0, 2)] = bwd_seg_res(x0, m0, gain(0, 1), gain(0, 2), dx1, dh2, "d_res_0a")
    g_ab_out = matmul(mix0, dm0, 'tn', "d_ab_w_out", BF)
    dmix0 = matmul(dm0, big[('ab_w_out', 0)], 'nt', "d_mix0")
    dxbc_act, dz, ddt, ddtb, dalog, ddsk, dnw = ssd_bwd(xbc_act, u0, dtb, alog, dsk, p['ssm_norm'], consts, hs, dmix0, bsz, seq)
    grads['ssm_dt_bias'] = ddtb[:, :SSM_HEADS]
    grads['ssm_a_log'] = dalog[:, :SSM_HEADS]
    grads['ssm_d'] = ddsk[:, :SSM_HEADS]
    grads['ssm_norm'] = dnw
    dxr, dcw, dcb = bwd_call(conv4_fn, "d_ssm_conv", (ncb, bsz), [u0, conv_w, conv_b], conv_in_specs,
                             [dxbc_act], [conv_out_spec], [0, 1, 2],
                             [_sd((t, SSM_CONV_DIM), BF), _sd((SSM_CONV, SSM_CONV_DIM)), _sd((1, SSM_CONV_DIM))],
                             [conv_out_spec, conv_in_specs[1], conv_in_specs[2]], [None, (1,), (1,)])
    grads['ssm_conv_w'], grads['ssm_conv_b'] = _xbc_ungroup(dcw, 1)[None], _xbc_ungroup(dcb, 1)
    dpool, dpw, dps = [], [], []
    for g in range(POOL_GROUPS):
        seqspec = pl.BlockSpec((seq, PG), lambda b, g=g: (b, g))
        one = pl.BlockSpec((seq, PG), lambda b: (b, 0))
        wspec = pl.BlockSpec((PG, PG), lambda b: (0, 0))
        sspec = pl.BlockSpec((1, PG), lambda b, g=g: (0, g))
        a, bb, c = bwd_call(make_pool_fn(g), f"d_pool_{g}", (bsz,), [u0, p['pool_w'][0, g], p['pool_scale']],
                            [seqspec, wspec, sspec], [dmix0], [seqspec], [0, 1, 2],
                            [_sd((t, PG), BF), _sd((PG, PG)), _sd((1, PG))], [one, wspec, pl.BlockSpec((1, PG), lambda b: (0, 0))],
                            [None, (0,), (0,)])
        dpool.append(a)
        dpw.append(bb)
        dps.append(c)
    grads['pool_w'] = jnp.stack(dpw)[None]
    grads['pool_scale'] = jnp.concatenate(dps, axis=1)
    du0 = jnp.concatenate(dpool + [dz, dxr, ddt.astype(BF)], axis=1)
    g_ab_in = matmul(du0, h0, 'tn', "d_ab_w_in", BF)
    g_ab_in = jnp.concatenate([g_ab_in[:xbc0], _xbc_ungroup(g_ab_in[xbc0:xbc0 + SSM_CONV_DIM], 0),
                               g_ab_in[xbc0 + SSM_CONV_DIM:AB_IN]], axis=0)
    ex.put_grads('ab', G_AB, {('ab_w_in', 0): g_ab_in, ('ab_w_out', 0): g_ab_out})
    dh0 = matmul(du0, w_ab_in, 'nn', "d_h_ab", after=ex.take_tokens())
    dx, dg00 = bwd_call(seg_in_res, "d_norm_in", (nb,), [x0, gain(0, 0)], [_rows(D), _par(D)], [dx0r, dh0],
                        [_rows(D), _rows(D)], [0, 1], [_sd((t, D)), _sd((1, D))], [_rows(D), _par(D)], [None, (0,)])
    gain_grads[(0, 0)] = dg00
    grads['norm_gains'] = jnp.stack([jnp.concatenate([gain_grads[(l, i)] for i in range(6)], axis=0) for l in range(2)])
    return loss, dx, grads
```

```python
import functools
import math

import numpy as np
import jax
import jax.numpy as jnp
from jax import lax
from jax.experimental import pallas as pl
from jax.experimental.pallas import tpu as pltpu

BF = jnp.bfloat16
F32 = jnp.float32

N_DEV = 8
D = 1024
N_MEM = 256
XA_HEADS = 4
XA_DH = D // XA_HEADS
POOL_GROUPS = 4
PG = 128
POOL_W = POOL_GROUPS * PG
SSM_INNER = 1024
SSM_GROUPS = 2
SSM_GSZ = SSM_INNER // SSM_GROUPS
SSM_HEADS = 16
SSM_P = 64
SSM_N = 128
SSM_CONV = 4
SSM_CONV_DIM = SSM_INNER + 2 * SSM_GROUPS * SSM_N
SSM_XBC_G = SSM_GSZ + 2 * SSM_N
CHUNK = 128
AB_IN = POOL_W + SSM_INNER + SSM_CONV_DIM + SSM_HEADS
AB_IN_PAD = POOL_W + SSM_INNER + SSM_CONV_DIM + 128
AB_OUT = POOL_W + SSM_INNER
CONF_K = 31
SC_K = 3
CD_IN = 5 * D
CD_OUT = 2 * D
MLP_H = 4 * D
RMS_EPS = 1e-6
LN_EPS = 1e-5
ADAM_LR = 0.001
ADAM_B1 = 0.9
ADAM_B2 = 0.999
ADAM_EPS = 1e-08
ADAM_WD = 0.01
ADAM_STEP = 10
VMEM_LIMIT = 56 * 1024 * 1024
LANE = 128

NAMES = ['x', 'mem', 'norm_gains', 'xa_wq', 'xa_wkv', 'xa_wo', 'mlp_w1', 'mlp_w2', 'ab_w_in', 'pool_w', 'pool_scale',
         'ssm_conv_w', 'ssm_conv_b', 'ssm_dt_bias', 'ssm_a_log', 'ssm_d', 'ssm_norm', 'ab_w_out', 'cd_w_in', 'conf_dw_w',
         'conf_dw_b', 'conf_ln_g', 'conf_ln_b', 'sc_conv_w', 'cd_w_out', 'loss_target']
WEIGHTS = NAMES[2:25]
BIG = [('xa_wq', 1), ('xa_wkv', 2), ('xa_wo', 1), ('mlp_w1', 2), ('mlp_w2', 1), ('cd_w_in', 2), ('cd_w_out', 1),
       ('ab_w_out', 1), ('ab_w_in', 2)]
SMALL_SHARDED = ['norm_gains', 'ssm_conv_w', 'conf_dw_w', 'conf_dw_b', 'conf_ln_g', 'conf_ln_b', 'sc_conv_w']
REPLICATED = ['pool_w', 'pool_scale', 'ssm_conv_b', 'ssm_dt_bias', 'ssm_a_log', 'ssm_d', 'ssm_norm']


def _dg(a, b, ca, cb, prec=None):
    return lax.dot_general(a, b, (((ca,), (cb,)), ((), ())), precision=prec, preferred_element_type=F32)


@functools.partial(jax.custom_vjp, nondiff_argnums=(2, 3))
def bdot(a, b, ca, cb):
    return _dg(a.astype(BF), b.astype(BF), ca, cb)


def _bdot_fwd(a, b, ca, cb):
    return bdot(a, b, ca, cb), (a, b)


def _bdot_bwd(ca, cb, res, g):
    a, b = res
    g16, a16, b16 = g.astype(BF), a.astype(BF), b.astype(BF)
    da = _dg(g16, b16, 1, 1 - cb) if ca == 1 else _dg(b16, g16, 1 - cb, 1)
    db = _dg(g16, a16, 0, 1 - ca) if cb == 1 else _dg(a16, g16, 1 - ca, 0)
    return da.astype(a.dtype), db.astype(b.dtype)


bdot.defvjp(_bdot_fwd, _bdot_bwd)


def _split3(a):
    a1 = a.astype(BF)
    r1 = a - a1.astype(F32)
    a2 = r1.astype(BF)
    a3 = (r1 - a2.astype(F32)).astype(BF)
    return a1, a2, a3


def _exact_right(a, c):
    m = a.shape[0]
    if m % 16:
        return sum(_dg(p, c, 1, 0) for p in _split3(a))
    o = _dg(jnp.concatenate(_split3(a), axis=0), c, 1, 0)
    return o[:m] + o[m:2 * m] + o[2 * m:]


def _exact_left(c, a):
    n = a.shape[1]
    o = _dg(c, jnp.concatenate(_split3(a), axis=1), 1, 0)
    return o[:, :n] + o[:, n:2 * n] + o[:, 2 * n:]


@jax.custom_vjp
def cmat(a, c, ct):
    return _exact_right(a, c)


def _cmat_fwd(a, c, ct):
    return cmat(a, c, ct), (c, ct)


def _cmat_bwd(res, g):
    c, ct = res
    return _exact_right(g, ct), jnp.zeros_like(c), jnp.zeros_like(ct)


cmat.defvjp(_cmat_fwd, _cmat_bwd)


@jax.custom_vjp
def cmatl(c, ct, a):
    return _exact_left(c, a)


def _cmatl_fwd(c, ct, a):
    return cmatl(c, ct, a), (c, ct)


def _cmatl_bwd(res, g):
    c, ct = res
    return jnp.zeros_like(c), jnp.zeros_like(ct), _exact_left(ct, g)


cmatl.defvjp(_cmatl_fwd, _cmatl_bwd)


SUBLANES = 8


def _taps(x, shifts, down):
    n, c = x.shape
    pad = _round_up(max(shifts), SUBLANES)
    if pad == 0:
        return {0: x}
    zeros = jnp.zeros((pad, c), x.dtype)
    xp = jnp.concatenate([zeros, x] if down else [x, zeros], axis=0)
    rolled, out = {0: xp}, {}
    for s in shifts:
        a, b = divmod(s, SUBLANES)
        if b not in rolled:
            rolled[b] = pltpu.roll(xp, b if down else n + pad - b, 0)
        off = pad - SUBLANES * a if down else SUBLANES * a
        out[s] = rolled[b][off:off + n]
    return out


def _shift_down(x, k):
    return _taps(x, [k], True)[k]


def _shift_up(x, k):
    return _taps(x, [k], False)[k]


@functools.partial(jax.custom_vjp, nondiff_argnums=(1,))
def shift(x, k):
    return _shift_down(x, k)


def _shift_fwd(x, k):
    return _shift_down(x, k), None


def _shift_bwd(k, _, g):
    return (_shift_up(g, k),)


shift.defvjp(_shift_fwd, _shift_bwd)


@functools.partial(jax.custom_vjp, nondiff_argnums=(2,))
def cconv(u, w, width):
    taps = _taps(u, list(range(width)), True)
    acc = u * w[width - 1:width, :]
    for k in range(width - 1):
        acc = acc + taps[width - 1 - k] * w[k:k + 1, :]
    return acc


def _cconv_fwd(u, w, width):
    return cconv(u, w, width), (u, w)


def _cconv_bwd(width, res, g):
    u, w = res
    rows = lax.broadcasted_iota(jnp.int32, w.shape, 0)
    du = g * w[width - 1:width, :]
    dw = jnp.where(rows == width - 1, jnp.sum(g * u, axis=0, keepdims=True), 0.0)
    g_taps = _taps(g, list(range(width)), False)
    u_taps = _taps(u, list(range(width)), True)
    for k in range(width - 1):
        s = width - 1 - k
        du = du + g_taps[s] * w[k:k + 1, :]
        dw = dw + jnp.where(rows == k, jnp.sum(g * u_taps[s], axis=0, keepdims=True), 0.0)
    return du, dw


cconv.defvjp(_cconv_fwd, _cconv_bwd)


def _rms(x, g):
    return x * lax.rsqrt(jnp.mean(x * x, axis=-1, keepdims=True) + RMS_EPS) * g


def _params(sem=None):
    return pltpu.CompilerParams(dimension_semantics=sem, vmem_limit_bytes=VMEM_LIMIT)


def _f32(v):
    return v if v.dtype == F32 else v.astype(F32)


def _first(axes):
    ok = None
    for ax in axes:
        c = pl.program_id(ax) == 0
        ok = c if ok is None else jnp.logical_and(ok, c)
    return ok


def fwd_call(fn, name, grid, ins, in_specs, out_shapes, out_specs, into=None):
    n_in = len(ins)
    n_into = 0 if into is None else 1

    def body(*refs):
        outs = fn(*[_f32(r[...]) for r in refs[:n_in]])
        for r, o in zip(refs[n_in + n_into:], outs):
            r[...] = o.astype(r.dtype)

    extra = [] if into is None else [into]
    return pl.pallas_call(body, name=name, grid=grid, in_specs=list(in_specs) + [pl.BlockSpec(memory_space=pl.ANY)] * n_into,
                          out_specs=out_specs, out_shape=out_shapes, input_output_aliases={n_in: 0} if n_into else {},
                          compiler_params=_params())(*ins, *extra)


def bwd_call(fn, name, grid, ins, in_specs, cots, cot_specs, gidx, g_shapes, g_specs, g_acc):
    n_in, n_cot = len(ins), len(cots)

    def body(*refs):
        vals = [_f32(r[...]) for r in refs[:n_in]]

        def f_sel(*dv):
            full = list(vals)
            for i, v in zip(gidx, dv):
                full[i] = v
            return tuple(fn(*full))

        outs, vjp = jax.vjp(f_sel, *[vals[i] for i in gidx])
        cts = tuple(_f32(r[...]) for r in refs[n_in:n_in + n_cot])
        grads = vjp(cts)
        for r, g, acc in zip(refs[n_in + n_cot:], grads, g_acc):
            if acc is None:
                r[...] = g.astype(r.dtype)
            else:
                @pl.when(_first(acc))
                def _():
                    r[...] = jnp.zeros_like(r)

                r[...] += g.astype(r.dtype)

    return pl.pallas_call(body, name=name, grid=grid, in_specs=list(in_specs) + list(cot_specs), out_specs=g_specs,
                          out_shape=g_shapes, compiler_params=_params())(*ins, *cots)


def _tile(dim, pref):
    if dim <= pref:
        return dim
    best = None
    for t in range(LANE, pref + 1, LANE):
        if dim % t == 0:
            best = t
    assert best is not None, dim
    return best


MATMUL_VMEM_BUDGET = 40 * 1024 * 1024


def _matmul_tiles(m, n, k, a_bytes, b_bytes, out_bytes):
    tn = _tile(n, 1024)
    for tk_pref in (k, 2048, 1024, 512):
        tk = _tile(k, tk_pref)
        for tm_pref in (1024, 512, 256):
            tm = _tile(m, tm_pref)
            need = 2 * (tm * tk * a_bytes + tk * tn * b_bytes + tm * tn * out_bytes) + (0 if tk == k else tm * tn * 4)
            need += (tm * tk * 2 if a_bytes == 4 else 0) + (tk * tn * 2 if b_bytes == 4 else 0)
            if need <= MATMUL_VMEM_BUDGET:
                return tm, tn, tk
    raise ValueError((m, n, k))


def matmul(a, b, mode, name, out_dtype=F32, epilogue=None, extras=(), params=(), after=()):
    if mode == 'nn':
        (m, k), (k2, n) = a.shape, b.shape
    elif mode == 'nt':
        (m, k), (n, k2) = a.shape, b.shape
    else:
        (k, m), (k2, n) = a.shape, b.shape
    assert k == k2, (name, a.shape, b.shape)
    n_extra = len(extras) + len(params)
    out_dtypes = out_dtype if isinstance(out_dtype, tuple) else (out_dtype,)
    per_out = sum(jnp.dtype(dt).itemsize for dt in out_dtypes) + sum(e.dtype.itemsize for e in extras)
    tm, tn, tk = _matmul_tiles(m, n, k, a.dtype.itemsize, b.dtype.itemsize, per_out)
    nk = k // tk
    ca = 0 if mode == 'tn' else 1
    cb = 1 if mode == 'nt' else 0
    a_spec = pl.BlockSpec((tk, tm), lambda i, j, kk: (kk, i)) if mode == 'tn' else pl.BlockSpec((tm, tk), lambda i, j, kk: (i, kk))
    b_spec = pl.BlockSpec((tn, tk), lambda i, j, kk: (j, kk)) if mode == 'nt' else pl.BlockSpec((tk, tn), lambda i, j, kk: (kk, j))

    def finish(o_refs, extra_refs, acc):
        outs = (acc,) if epilogue is None else epilogue(acc, *[_f32(e[...]) for e in extra_refs])
        for o_ref, o in zip(o_refs, outs):
            o_ref[...] = o.astype(o_ref.dtype)

    n_after = len(after)

    def body_whole_k(a_ref, b_ref, *refs):
        refs = refs[n_after:]
        finish(refs[n_extra:], refs[:n_extra], _dg(a_ref[...].astype(BF), b_ref[...].astype(BF), ca, cb))

    def body_split_k(a_ref, b_ref, *refs):
        refs = refs[n_after:]
        extra_refs, o_refs, acc = refs[:n_extra], refs[n_extra:-1], refs[-1]
        kk = pl.program_id(2)

        @pl.when(kk == 0)
        def _():
            acc[...] = jnp.zeros_like(acc)

        acc[...] += _dg(a_ref[...].astype(BF), b_ref[...].astype(BF), ca, cb)

        @pl.when(kk == nk - 1)
        def _():
            finish(o_refs, extra_refs, acc[...])

    tile = pl.BlockSpec((tm, tn), lambda i, j, kk: (i, j))
    row = pl.BlockSpec((1, tn), lambda i, j, kk: (0, j))
    n_par = len(params)
    outs = pl.pallas_call(
        body_whole_k if nk == 1 else body_split_k, name=name, grid=(m // tm, n // tn, nk),
        in_specs=[a_spec, b_spec] + [pl.BlockSpec(memory_space=pl.ANY)] * n_after + [tile] * len(extras) + [row] * n_par,
        out_specs=[tile] * len(out_dtypes),
        out_shape=[jax.ShapeDtypeStruct((m, n), dt) for dt in out_dtypes],
        scratch_shapes=[] if nk == 1 else [pltpu.VMEM((tm, tn), F32)],
        compiler_params=_params(("parallel", "parallel", "arbitrary")))(a, b, *after, *extras, *params)
    return outs if isinstance(out_dtype, tuple) else outs[0]


_FLIPS = [(0, 0, 1), (1, 0, 0), (0, 1, 0), (1, 1, 0), (1, 0, 1), (0, 1, 1), (1, 1, 1)]


def _me():
    return lax.axis_index("x"), lax.axis_index("y"), lax.axis_index("c")


def _flip(pos, f):
    return tuple(jnp.where(fi == 1, 1 - p, p) if fi else p for p, fi in zip(pos, f))


def _slot(pos):
    return 4 * pos[0] + 2 * pos[1] + pos[2]


def all_gather(v, name):
    def body(v_ref, out_ref, send_sems, recv_sems, local_sem):
        me = _me()
        sibling = _flip(me, (0, 0, 1))
        chips = [_flip(me, f) for f in ((1, 0, 0), (0, 1, 0), (1, 1, 0))]

        def copy(k, block, to, src=None):
            return pltpu.make_async_remote_copy(
                src_ref=out_ref.at[_slot(block)] if src is None else src, dst_ref=out_ref.at[_slot(block)],
                send_sem=send_sems.at[k], recv_sem=recv_sems.at[k], device_id=to, device_id_type=pl.DeviceIdType.MESH)

        mine = pltpu.make_async_copy(v_ref, out_ref.at[_slot(me)], local_sem)
        mine.start()
        first = [copy(0, me, sibling, src=v_ref)] + [copy(1 + j, me, chip, src=v_ref) for j, chip in enumerate(chips)]
        for cp in first:
            cp.start()
        passed = [copy(4 + j, chip, sibling) for j, chip in enumerate(chips)]
        for j, chip in enumerate(chips):
            copy(1 + j, chip, me).wait_recv()
            passed[j].start()
        copy(0, sibling, me).wait_recv()
        for j, chip in enumerate(chips):
            copy(4 + j, _flip(chip, (0, 0, 1)), me).wait_recv()
        for cp in first + passed:
            cp.wait_send()
        mine.wait()

    return pl.pallas_call(
        body, name=name, out_shape=jax.ShapeDtypeStruct((N_DEV,) + v.shape, v.dtype),
        in_specs=[pl.BlockSpec(memory_space=pl.ANY)], out_specs=pl.BlockSpec(memory_space=pl.ANY),
        scratch_shapes=[pltpu.SemaphoreType.DMA((7,)), pltpu.SemaphoreType.DMA((7,)), pltpu.SemaphoreType.DMA(())],
    )(v)


def sum_slots(v, name, tr=256):
    _, r, c = v.shape
    tr = _tile_rows(r, tr)

    def body(v_ref, o_ref):
        acc = v_ref[0].astype(F32)
        for s in range(1, N_DEV):
            acc = acc + v_ref[s].astype(F32)
        o_ref[...] = acc

    return pl.pallas_call(body, name=name, grid=(r // tr,), in_specs=[pl.BlockSpec((N_DEV, tr, c), lambda i: (0, i, 0))],
                          out_specs=pl.BlockSpec((tr, c), lambda i: (i, 0)), out_shape=jax.ShapeDtypeStruct((r, c), F32),
                          compiler_params=_params())(v)


def _tile_rows(r, pref):
    if r <= pref:
        return r
    best = None
    for t in range(8, pref + 1, 8):
        if r % t == 0:
            best = t
    return r if best is None else best


def _adamw_math(w, m, v, g):
    nm = ADAM_B1 * m + (1.0 - ADAM_B1) * g
    nv = ADAM_B2 * v + (1.0 - ADAM_B2) * jnp.square(g)
    m_hat = nm / (1.0 - ADAM_B1 ** ADAM_STEP)
    v_hat = nv / (1.0 - ADAM_B2 ** ADAM_STEP)
    return -ADAM_LR * (m_hat / (jnp.sqrt(v_hat) + ADAM_EPS) + ADAM_WD * w), nm, nv


def update_from_slots(lands, offs, w, m, v, transposed, name):
    layers, a, b = w.shape
    n_land = len(lands)
    if transposed:
        rb, tk = LANE, 512
        assert a % tk == 0 and b % rb == 0 and all(o % rb == 0 for o in offs), (name, w.shape, offs)
        grid = (layers, a // tk, b // rb)
        land_block = (N_DEV, rb, tk)
        tile = pl.BlockSpec((None, tk, rb), lambda l, i, j: (l, i, j))

        def land_spec(layer):
            base = offs[layer] // rb
            return pl.BlockSpec(land_block, lambda l, i, j: (0, base + jnp.where(l == layer, j, 0), jnp.where(l == layer, i, 0)))
    else:
        fits = [t for t in (256, 128, 64) if a % t == 0 and all(o % t == 0 for o in offs)]
        assert fits or all(o == 0 for o in offs), (name, w.shape, offs)
        tr = max(fits) if fits else a
        grid = (layers, a // tr)
        land_block = (N_DEV, _round_up(tr, MEMBER_ROW_TILE), b)
        tile = pl.BlockSpec((None, tr, b), lambda l, i: (l, i, 0))

        def land_spec(layer):
            base = offs[layer] // tr
            return pl.BlockSpec(land_block, lambda l, i: (0, base + jnp.where(l == layer, i, 0), 0))

    def body(*refs):
        land_refs, (w_ref, m_ref, v_ref, g_ref, d_ref, nm_ref, nv_ref, acc) = refs[:n_land], refs[n_land:]
        for layer, land in enumerate(land_refs):
            @pl.when(pl.program_id(0) == layer)
            def _(land=land):
                rows = acc.shape[0]
                s = land[0, :rows].astype(F32)
                for k in range(1, N_DEV):
                    s = s + land[k, :rows].astype(F32)
                acc[...] = s

        g = acc[...].T if transposed else acc[...]
        d, nm, nv = _adamw_math(w_ref[...], m_ref[...], v_ref[...], g)
        g_ref[...] = g
        d_ref[...] = d
        nm_ref[...] = nm
        nv_ref[...] = nv

    sh = jax.ShapeDtypeStruct(w.shape, F32)
    return pl.pallas_call(
        body, name=name, grid=grid, in_specs=[land_spec(layer) for layer in range(n_land)] + [tile] * 3, out_specs=[tile] * 4,
        out_shape=[sh] * 4, scratch_shapes=[pltpu.VMEM((rb, tk) if transposed else (tr, b), F32)],
        compiler_params=_params())(*lands, w, m, v)


def adamw_many(ws, ms, vs, gs, name):
    n = len(ws)

    def body(*refs):
        for i in range(n):
            d, nm, nv = _adamw_math(refs[i][...], refs[n + i][...], refs[2 * n + i][...], refs[3 * n + i][...])
            refs[4 * n + i][...] = d
            refs[5 * n + i][...] = nm
            refs[6 * n + i][...] = nv

    vmem = pl.BlockSpec(memory_space=pltpu.VMEM)
    shapes = [jax.ShapeDtypeStruct(a.shape, F32) for a in ws]
    res = pl.pallas_call(body, name=name, in_specs=[vmem] * (4 * n), out_specs=[vmem] * (3 * n), out_shape=shapes * 3,
                         compiler_params=_params())(*ws, *ms, *vs, *gs)
    return res[:n], res[n:2 * n], res[2 * n:]


def seg_in(x, g):
    return (_rms(x, g),)


def seg_in_res(x, g):
    return x, _rms(x, g)


def seg_res(x, m, ga, gb):
    x1 = x + _rms(m, ga)
    return x1, _rms(x1, gb)


def seg_out(x, m, ga):
    return (x + _rms(m, ga),)


def act_epilogue(r):
    t = jnp.maximum(r, 0.0)
    return r, t * t


def res_epilogue(m, x, ga, gb):
    x1, h = seg_res(x, m, ga, gb)
    return m, x1, h


def act_bwd_epilogue(drr, r):
    return (drr * (2.0 * jnp.maximum(r, 0.0)),)


def seg_ln(v, g, b):
    mu = jnp.mean(v, axis=-1, keepdims=True)
    var = jnp.mean(jnp.square(v - mu), axis=-1, keepdims=True)
    vn = (v - mu) * lax.rsqrt(var + LN_EPS) * g + b
    return (jax.nn.silu(vn),)


def make_pool_fn(group):
    window = 2 ** (group + 1)

    def pool_fn(ug, pw, scale):
        s = ug
        for lvl in range(group + 1):
            s = s + shift(s, 2 ** lvl)
        cnt = jnp.minimum(lax.broadcasted_iota(jnp.int32, ug.shape, 0) + 1, window).astype(F32)
        return (bdot(s / cnt - ug, pw, 1, 0) * scale,)

    return pool_fn


def conv4_fn(xr, w, b):
    return (jax.nn.silu(cconv(xr, w, SSM_CONV) + b),)


def cd1_fn(u, dww, dwb, scw):
    val, gate, bg, cg, hh = (u[:, k * LANE:(k + 1) * LANE] for k in range(5))
    v = val * jax.nn.sigmoid(gate)
    vc = cconv(v, dww, CONF_K) + dwb
    sc = bg * cconv(cg * hh, scw, SC_K)
    return vc, sc


def attn_fn(q, kv):
    outs = []
    for h in range(XA_HEADS):
        cols = slice(h * XA_DH, (h + 1) * XA_DH)
        s = bdot(q[:, cols], kv[:, cols], 1, 1) / math.sqrt(XA_DH)
        p = jax.nn.softmax(s, axis=-1)
        outs.append(bdot(p, kv[:, D + h * XA_DH:D + (h + 1) * XA_DH], 1, 0))
    return (jnp.concatenate(outs, axis=1),)


def ssd_chunk(xbc, z, dtraw, dtb, alog, dsk, nw, h0, h1, h2, h3, e64, e64t, ecat, ecatt, tril, trilt):
    xs, bm, cm = xbc[:, :SSM_GSZ], xbc[:, SSM_GSZ:SSM_GSZ + SSM_N], xbc[:, SSM_GSZ + SSM_N:]
    hin = (h0, h1, h2, h3)
    dt = jax.nn.softplus(dtraw + dtb)
    a = -jnp.exp(alog)
    d_a = dt * a
    cs = cmatl(tril, trilt, d_a)
    cs_cat = cmat(cs, ecat, ecatt)
    cs64, cs128 = cs_cat[:, :SSM_GSZ], cs_cat[:, SSM_GSZ:]
    dt64 = cmat(dt, e64, e64t)
    row = lax.broadcasted_iota(jnp.int32, (8, LANE), 0)
    heads = jnp.where(row == 0, dsk, jnp.where(row == 1, jnp.sum(d_a, axis=0, keepdims=True), 0.0))
    heads64 = cmat(heads, e64, e64t)
    d64, tot64 = heads64[0:1, :], heads64[1:2, :]
    xdt = xs * dt64
    cb = bdot(cm, bm, 1, 1)
    li = lax.broadcasted_iota(jnp.int32, (CHUNK, CHUNK), 0)
    si = lax.broadcasted_iota(jnp.int32, (CHUNK, CHUNK), 1)
    causal = li >= si
    lane = lax.broadcasted_iota(jnp.int32, (CHUNK, LANE), 1)
    xw = xdt * jnp.exp(tot64 - cs64)
    ecs = jnp.exp(cs64)
    etot = jnp.exp(tot64)
    ycols, hout = [], []
    for j in range(4):
        sl = slice(j * LANE, (j + 1) * LANE)
        xj = xdt[:, sl]
        ys = []
        for hh in range(2):
            r = 2 * j + hh
            col = cs128[:, r * LANE:(r + 1) * LANE]
            decay = jnp.exp(jnp.where(causal, col - col.T, -1e30))
            ys.append(bdot(cb * decay, xj, 1, 0))
        y_diag = jnp.where(lane < SSM_P, ys[0], ys[1])
        y_off = bdot(cm, hin[j], 1, 0) * ecs[:, sl]
        ycols.append(y_diag + y_off)
        hout.append(etot[:, sl] * hin[j] + bdot(bm, xw[:, sl], 0, 0))
    y = jnp.concatenate(ycols, axis=1) + d64 * xs
    y = y * jax.nn.silu(z)
    yn = y * lax.rsqrt(jnp.mean(y * y, axis=-1, keepdims=True) + RMS_EPS) * nw
    return (yn,) + tuple(hout)


def _xbc_group(a, axis):
    parts = []
    for g in range(SSM_GROUPS):
        for start, width in ((g * SSM_GSZ, SSM_GSZ), (SSM_INNER + g * SSM_N, SSM_N), (SSM_INNER + (SSM_GROUPS + g) * SSM_N, SSM_N)):
            parts.append(lax.slice_in_dim(a, start, start + width, axis=axis))
    return jnp.concatenate(parts, axis=axis)


def _xbc_ungroup(a, axis):
    xs, bs, cs = [], [], []
    for g in range(SSM_GROUPS):
        base = g * SSM_XBC_G
        xs.append(lax.slice_in_dim(a, base, base + SSM_GSZ, axis=axis))
        bs.append(lax.slice_in_dim(a, base + SSM_GSZ, base + SSM_GSZ + SSM_N, axis=axis))
        cs.append(lax.slice_in_dim(a, base + SSM_GSZ + SSM_N, base + SSM_XBC_G, axis=axis))
    return jnp.concatenate(xs + bs + cs, axis=axis)


def _ssd_consts():
    h = np.arange(LANE)[:, None]
    e64 = np.stack([(h == g * 8 + np.arange(SSM_GSZ)[None, :] // SSM_P) for g in range(SSM_GROUPS)]).astype(np.float32)
    e128 = np.stack([(h == g * 8 + np.arange(8 * LANE)[None, :] // LANE) for g in range(SSM_GROUPS)]).astype(np.float32)
    ecat = np.concatenate([e64, e128], axis=2)
    tril = np.tril(np.ones((CHUNK, CHUNK), np.float32))
    return tuple(jnp.asarray(c, dtype=BF) for c in (e64, e64.transpose(0, 2, 1), ecat, ecat.transpose(0, 2, 1), tril, tril.T))


def _ssd_specs(nc, rev):
    def ci(c):
        return nc - 1 - c if rev else c

    def row(width, col):
        return pl.BlockSpec((CHUNK, width), lambda b, c: (b * nc + ci(c), col))

    def whole(shape):
        return pl.BlockSpec(shape, lambda b, c: (0,) * len(shape))

    data = [row(SSM_CONV_DIM, 0),
            row(SSM_GSZ, 1), row(SSM_GSZ, 2), row(LANE, 24)]
    par = [whole((1, LANE))] * 3 + [whole((1, SSM_INNER))]
    cst = [whole((SSM_GROUPS, LANE, SSM_GSZ)), whole((SSM_GROUPS, SSM_GSZ, LANE)), whole((SSM_GROUPS, LANE, 12 * LANE)),
           whole((SSM_GROUPS, 12 * LANE, LANE)), whole((CHUNK, CHUNK)), whole((CHUNK, CHUNK))]
    hsave = pl.BlockSpec((None, None, SSM_GROUPS, 4, SSM_N, LANE), lambda b, c: (b, ci(c), 0, 0, 0, 0))
    return data, par, cst, hsave, row, whole


def _ssd_group_args(g, xbc, z, dtr, dtb, alog, dsk, nw):
    return (xbc[:, g * SSM_XBC_G:(g + 1) * SSM_XBC_G], z[g], dtr, dtb, alog, dsk, nw[:, g * SSM_GSZ:(g + 1) * SSM_GSZ])


def ssd_fwd(xbc_act, u, dtb, alog, dsk, nw, consts, bsz, seq):
    nc = seq // CHUNK
    data, par, cst, hsave, row, _ = _ssd_specs(nc, False)

    def body(xbc, z0, z1, dtr, dtb_r, alog_r, dsk_r, nw_r, e64, e64t, ecat, ecatt, tril, trilt, yn_ref, hs_ref, h):
        @pl.when(pl.program_id(1) == 0)
        def _():
            h[...] = jnp.zeros_like(h)

        hs_ref[...] = h[...]
        ys = []
        for g in range(SSM_GROUPS):
            args = _ssd_group_args(g, xbc[...], (z0[...], z1[...]), dtr[...], dtb_r[...], alog_r[...], dsk_r[...], nw_r[...])
            outs = ssd_chunk(*args, h[g, 0], h[g, 1], h[g, 2], h[g, 3], e64[g], e64t[g], ecat[g], ecatt[g], tril[...], trilt[...])
            ys.append(outs[0])
            for j in range(4):
                h[g, j] = outs[1 + j]
        yn_ref[...] = jnp.concatenate(ys, axis=1).astype(yn_ref.dtype)

    t = bsz * seq
    return pl.pallas_call(
        body, name="ssd_fwd", grid=(bsz, nc), in_specs=data + par + cst, out_specs=[row(SSM_INNER, 0), hsave],
        out_shape=[jax.ShapeDtypeStruct((t, SSM_INNER), BF), jax.ShapeDtypeStruct((bsz, nc, SSM_GROUPS, 4, SSM_N, LANE), F32)],
        scratch_shapes=[pltpu.VMEM((SSM_GROUPS, 4, SSM_N, LANE), F32)], compiler_params=_params(),
    )(xbc_act, u, u, u, dtb, alog, dsk, nw, *consts)


def ssd_bwd(xbc_act, u, dtb, alog, dsk, nw, consts, hs, dmix, bsz, seq):
    nc = seq // CHUNK
    data, par, cst, hsave, row, whole = _ssd_specs(nc, True)
    t = bsz * seq
    pcol = POOL_W // SSM_GSZ

    def body(xbc, z0, z1, dtr, dtb_r, alog_r, dsk_r, nw_r, e64, e64t, ecat, ecatt, tril, trilt, hs_ref, dy0, dy1,
             dxbc, dz, ddt, ddtb, dalog, ddsk, dnw, dh):
        @pl.when(pl.program_id(1) == 0)
        def _():
            dh[...] = jnp.zeros_like(dh)

        per_group = []
        for g, dyn in enumerate((dy0, dy1)):
            cst_vals = (e64[g], e64t[g], ecat[g], ecatt[g], tril[...], trilt[...])
            prim = _ssd_group_args(g, xbc[...], (z0[...], z1[...]), dtr[...], dtb_r[...], alog_r[...], dsk_r[...], nw_r[...])
            prim = prim + (hs_ref[g, 0], hs_ref[g, 1], hs_ref[g, 2], hs_ref[g, 3])
            _, vjp = jax.vjp(lambda *args, c=cst_vals: ssd_chunk(*args, *c), *prim)
            gr = vjp((dyn[...].astype(F32), dh[g, 0], dh[g, 1], dh[g, 2], dh[g, 3]))
            for j in range(4):
                dh[g, j] = gr[7 + j]
            per_group.append(gr)
        g0, g1 = per_group
        dxbc[...] = jnp.concatenate([g0[0], g1[0]], axis=1)
        dz[...] = jnp.concatenate([g0[1], g1[1]], axis=1).astype(dz.dtype)
        ddt[...] = g0[2] + g1[2]

        @pl.when(_first((0, 1)))
        def _():
            for r in (ddtb, dalog, ddsk, dnw):
                r[...] = jnp.zeros_like(r)

        ddtb[...] += g0[3] + g1[3]
        dalog[...] += g0[4] + g1[4]
        ddsk[...] += g0[5] + g1[5]
        dnw[...] += jnp.concatenate([g0[6], g1[6]], axis=1)

    out_specs = [row(SSM_CONV_DIM, 0), row(SSM_INNER, 0), row(LANE, 0), whole((1, LANE)), whole((1, LANE)), whole((1, LANE)),
                 whole((1, SSM_INNER))]
    lane = jax.ShapeDtypeStruct((1, LANE), F32)
    out_shape = [jax.ShapeDtypeStruct((t, SSM_CONV_DIM), F32), jax.ShapeDtypeStruct((t, SSM_INNER), BF),
                 jax.ShapeDtypeStruct((t, LANE), F32), lane, lane, lane, jax.ShapeDtypeStruct((1, SSM_INNER), F32)]
    return pl.pallas_call(
        body, name="ssd_bwd", grid=(bsz, nc), in_specs=data + par + cst + [hsave, row(SSM_GSZ, pcol), row(SSM_GSZ, pcol + 1)],
        out_specs=out_specs, out_shape=out_shape, scratch_shapes=[pltpu.VMEM((SSM_GROUPS, 4, SSM_N, LANE), F32)],
        compiler_params=_params(),
    )(xbc_act, u, u, u, dtb, alog, dsk, nw, *consts, hs, dmix, dmix)


TB = 512


def _rows(d, col=0):
    return pl.BlockSpec((TB, d), lambda i: (i, col))


def _par(d):
    return pl.BlockSpec((1, d), lambda i: (0, 0))


def _sd(shape, dtype=F32):
    return jax.ShapeDtypeStruct(shape, dtype)


def _round_up(n, m):
    return -(-n // m) * m


def _pad_rows(a, rows):
    return jnp.pad(a, ((0, rows - a.shape[0]), (0, 0)))


def _pack128(arrs):
    flat = jnp.concatenate([a.reshape(-1) for a in arrs])
    n = flat.shape[0]
    rows = -(-n // (8 * LANE)) * 8
    return jnp.pad(flat, (0, rows * LANE - n)).reshape(rows, LANE)


def _unpack128(packed, shapes):
    flat = packed.reshape(-1)
    out, off = [], 0
    for s in shapes:
        n = int(np.prod(s))
        out.append(flat[off:off + n].reshape(s))
        off += n
    return out


def kernel(x, mem, norm_gains, xa_wq, xa_wkv, xa_wo, mlp_w1, mlp_w2, ab_w_in, pool_w, pool_scale, ssm_conv_w, ssm_conv_b, ssm_dt_bias, ssm_a_log, ssm_d, ssm_norm, ab_w_out, cd_w_in, conf_dw_w, conf_dw_b, conf_ln_g, conf_ln_b, sc_conv_w, cd_w_out, loss_target, m_norm_gains, m_xa_wq, m_xa_wkv, m_xa_wo, m_mlp_w1, m_mlp_w2, m_ab_w_in, m_pool_w, m_pool_scale, m_ssm_conv_w, m_ssm_conv_b, m_ssm_dt_bias, m_ssm_a_log, m_ssm_d, m_ssm_norm, m_ab_w_out, m_cd_w_in, m_conf_dw_w, m_conf_dw_b, m_conf_ln_g, m_conf_ln_b, m_sc_conv_w, m_cd_w_out, v_norm_gains, v_xa_wq, v_xa_wkv, v_xa_wo, v_mlp_w1, v_mlp_w2, v_ab_w_in, v_pool_w, v_pool_scale, v_ssm_conv_w, v_ssm_conv_b, v_ssm_dt_bias, v_ssm_a_log, v_ssm_d, v_ssm_norm, v_ab_w_out, v_cd_w_in, v_conf_dw_w, v_conf_dw_b, v_conf_ln_g, v_conf_ln_b, v_sc_conv_w, v_cd_w_out):
    args = locals()
    w = {n: args[n] for n in WEIGHTS}
    mom_m = {n: args["m_" + n] for n in WEIGHTS}
    mom_v = {n: args["v_" + n] for n in WEIGHTS}
    ex = Exchange(w)
    loss_local, grad_x, small_grads = local_step(x, mem, loss_target, ex)
    outs = {}

    started = ex.put_small(small_grads, loss_local)
    landed = {key: ex.landed(key, started) for key in ('l1', 'cd', 'l0')}
    late = []
    for n, keys in (('mlp_w1', ('l0', 'l1')), ('mlp_w2', ('l0', 'l1')), ('xa_wkv', ('l0', 'l1')), ('xa_wq', ('l0', 'l1')),
                    ('xa_wo', ('l0', 'l1')), ('cd_w_in', ('cd',)), ('cd_w_out', ('cd',))):
        lands = [landed[key][0] for key in keys]
        offs = [landed[key][1][(n, layer)] for layer, key in enumerate(keys)]
        outs[n] = update_from_slots(lands, offs, w[n], mom_m[n], mom_v[n], SHARD_AXIS[n] == 2, "update_" + n)
        late.append(outs[n][1])
    g_own, loss = ex.reduced_small(late)
    land_ab, offs_ab = ex.landed('ab', late)
    outs['ab_w_out'] = update_from_slots([land_ab], [offs_ab[('ab_w_out', 0)]], w['ab_w_out'], mom_m['ab_w_out'],
                                         mom_v['ab_w_out'], False, "update_ab_w_out")
    res = update_from_slots([land_ab], [offs_ab[('ab_w_in', 0)]], jnp.swapaxes(w['ab_w_in'], 1, 2), jnp.swapaxes(mom_m['ab_w_in'], 1, 2),
                            jnp.swapaxes(mom_v['ab_w_in'], 1, 2), False, "update_ab_w_in")
    outs['ab_w_in'] = tuple(jnp.swapaxes(r, 1, 2) for r in res)
    small = SMALL_SHARDED + REPLICATED
    upd = adamw_many([w[n] for n in small], [mom_m[n] for n in small], [mom_v[n] for n in small], [g_own[n] for n in small],
                     "adamw_small")
    for i, n in enumerate(small):
        outs[n] = (g_own[n], upd[0][i], upd[1][i], upd[2][i])
    return (loss, grad_x.reshape(x.shape), *[outs[n][0] for n in WEIGHTS], *[outs[n][1] for n in WEIGHTS],
            *[outs[n][2] for n in WEIGHTS], *[outs[n][3] for n in WEIGHTS])


G_AB = (('ab_w_in', 0), ('ab_w_out', 0))
G_L0 = (('xa_wq', 0), ('xa_wkv', 0), ('xa_wo', 0), ('mlp_w1', 0), ('mlp_w2', 0))
G_L1 = (('xa_wq', 1), ('xa_wkv', 1), ('xa_wo', 1), ('mlp_w1', 1), ('mlp_w2', 1))
G_CD = (('cd_w_in', 0), ('cd_w_out', 0))
GATHER_GROUPS = {'ab': G_AB, 'l0a': G_L0[:3], 'l0b': G_L0[3:], 'cd': G_CD, 'l1a': G_L1[:3], 'l1b': G_L1[3:]}
SHARD_AXIS = dict(BIG)
MEMBER_ROW_TILE = 64
FLAT_ROW_TILE = 128


def _members(group, w):
    out = []
    for n, layer in group:
        shp = w[n].shape[1:]
        if SHARD_AXIS[n] == 2:
            shp = (shp[1], shp[0])
        assert shp[1] == D, (n, shp)
        out.append((n, layer, shp, shp[0], _round_up(shp[0], MEMBER_ROW_TILE)))
    return out


def _group_rows(group, w):
    return _round_up(sum(m[4] for m in _members(group, w)), FLAT_ROW_TILE)


def _flat_shards(group, w):
    parts = []
    for n, layer, _, _, padded in _members(group, w):
        shard = w[n][layer].astype(BF)
        parts.append(_pad_rows(shard.T if SHARD_AXIS[n] == 2 else shard, padded))
    return _pad_rows(jnp.concatenate(parts, axis=0), _group_rows(group, w))


def _full_from_slots(land, group, w):
    out, off = {}, 0
    for n, layer, shp, rows, padded in _members(group, w):
        out[(n, layer)] = land[:, off:off + rows].reshape(N_DEV * rows, D)
        off += padded
    return out


def _slots_from_full(grads, group, w):
    parts = []
    for n, layer, shp, rows, padded in _members(group, w):
        blk = grads[(n, layer)].astype(BF).reshape(N_DEV, rows, D)
        parts.append(jnp.pad(blk, ((0, 0), (0, padded - rows), (0, 0))))
    send = jnp.concatenate(parts, axis=1)
    return jnp.pad(send, ((0, 0), (0, _group_rows(group, w) - send.shape[1]), (0, 0)))


_HBM = pl.BlockSpec(memory_space=pltpu.HBM)
_SEM = pl.BlockSpec(memory_space=pltpu.SEMAPHORE)
_ANY = pl.BlockSpec(memory_space=pl.ANY)


def _peer_copy(k, src, dst, send_sems, recv_sems, peer):
    return pltpu.make_async_remote_copy(src_ref=src, dst_ref=dst, send_sem=send_sems.at[k], recv_sem=recv_sems.at[k],
                                        device_id=peer, device_id_type=pl.DeviceIdType.MESH)


def exchange_start(src, name, scatter, after=()):
    shape = src.shape[-2:]
    after = list(after)

    def body(src_ref, land_ref, *rest):
        send_sems, recv_sems, token = rest[len(after)], rest[len(after) + 1], rest[-1]
        me = _me()
        for k, f in enumerate(_FLIPS):
            peer = _flip(me, f)
            piece = src_ref.at[_slot(peer)] if scatter else src_ref
            _peer_copy(k, piece, land_ref.at[_slot(me)], send_sems, recv_sems, peer).start()
        token[...] = jnp.zeros_like(token)

    land = pltpu.with_memory_space_constraint(lax.empty((N_DEV,) + shape, src.dtype), pltpu.HBM)
    return pl.pallas_call(
        body, name=name,
        out_shape=(pltpu.SemaphoreType.DMA((7,)), pltpu.SemaphoreType.DMA((7,)), pltpu.HBM(src.shape, src.dtype),
                   pltpu.HBM((N_DEV,) + shape, src.dtype), jax.ShapeDtypeStruct((8, LANE), F32)),
        in_specs=(_HBM, _HBM) + (_ANY,) * len(after), out_specs=(_SEM, _SEM, _HBM, _HBM, pl.BlockSpec(memory_space=pltpu.VMEM)),
        input_output_aliases={0: 2, 1: 3},
        compiler_params=pltpu.CompilerParams(has_side_effects=pltpu.SideEffectType.DATAFLOW_SIDE_EFFECTING),
    )(pltpu.with_memory_space_constraint(src, pltpu.HBM), land, *after)


def exchange_wait(handles, after, name, scatter):
    send_sems, recv_sems, src_thru, land_thru, _ = handles
    after = list(after) if isinstance(after, (list, tuple)) else [after]

    def body(src_ref, land_ref, send_sems, recv_sems, *rest):
        token = rest[-1]
        me = _me()
        for k, f in enumerate(_FLIPS):
            peer = _flip(me, f)
            piece = src_ref.at[_slot(peer)] if scatter else src_ref
            cp = _peer_copy(k, piece, land_ref.at[_slot(peer)], send_sems, recv_sems, peer)
            cp.wait_send()
            cp.wait_recv()
        token[...] = jnp.zeros_like(token)

    return pl.pallas_call(
        body, name=name, out_shape=(pltpu.HBM(src_thru.shape, src_thru.dtype), pltpu.HBM(land_thru.shape, land_thru.dtype),
                                    jax.ShapeDtypeStruct((8, LANE), F32)),
        in_specs=(_HBM, _HBM, _SEM, _SEM) + (_ANY,) * len(after), out_specs=(_HBM, _HBM, pl.BlockSpec(memory_space=pltpu.VMEM)),
        input_output_aliases={0: 0, 1: 1},
        compiler_params=pltpu.CompilerParams(has_side_effects=pltpu.SideEffectType.DATAFLOW_SIDE_EFFECTING),
    )(src_thru, land_thru, send_sems, recv_sems, *after)


class Exchange:
    def __init__(self, w):
        self.w = w
        self.me = _slot(_me())
        shapes = [w[n].shape for n in SMALL_SHARDED]
        gs = all_gather(_pack128([w[n] for n in SMALL_SHARDED]), "gather_small")
        per_dev = [_unpack128(gs[d], shapes) for d in range(N_DEV)]
        self.small = {n: jnp.concatenate([per_dev[d][i] for d in range(N_DEV)], axis=-1) for i, n in enumerate(SMALL_SHARDED)}
        self.small.update({n: w[n] for n in REPLICATED})
        self.first = _full_from_slots(all_gather(_flat_shards(G_AB, w), "gather_ab"), G_AB, w)
        self.gathers, self.done, self.tokens, self.reductions = {}, {}, [], {}
        self.start_gather('l0a')
        self.start_gather('l0b')

    def take_tokens(self):
        toks, self.tokens = self.tokens, []
        return toks

    def start_gather(self, key, after=()):
        group = GATHER_GROUPS[key]
        self.gathers[key] = exchange_start(_flat_shards(group, self.w), f"gather_{key}_start", False, after=after)
        self.tokens.append(self.gathers[key][4])

    def weights(self, key, after):
        if key == 'ab':
            return self.first
        handles = self.gathers[key]
        _, land, self.done[key] = exchange_wait(handles, after, f"gather_{key}_wait", False)
        land = lax.dynamic_update_slice(land, handles[2][None], (self.me, 0, 0))
        return _full_from_slots(land, GATHER_GROUPS[key], self.w)

    def put_grads(self, key, group, grads):
        send = _slots_from_full(grads, group, self.w)
        handles = exchange_start(send, f"reduce_{key}_start", True)
        self.reductions[key] = (group, handles)
        self.tokens.append(handles[4])

    def landed(self, key, after):
        group, handles = self.reductions[key]
        send, land, _ = exchange_wait(handles, after, f"reduce_{key}_wait", True)
        mine = lax.dynamic_slice_in_dim(send, self.me, 1, axis=0)
        land = lax.dynamic_update_slice(land, mine, (self.me, 0, 0))
        offs, off = {}, 0
        for n, layer, _, _, padded in _members(group, self.w):
            offs[(n, layer)] = off
            off += padded
        return land, offs

    def put_small(self, small_grads, loss_local):
        small = SMALL_SHARDED + REPLICATED
        self.small_shapes = [small_grads[n].shape for n in small] + [(1,)]
        packed = _pack128([small_grads[n] for n in small] + [loss_local.reshape(1)])
        self.small_handles = exchange_start(packed, "gather_small_grads_start", False)
        return self.small_handles[4]

    def reduced_small(self, after):
        small = SMALL_SHARDED + REPLICATED
        src, land, _ = exchange_wait(self.small_handles, after, "gather_small_grads_wait", False)
        gs = lax.dynamic_update_slice(land, src[None], (self.me, 0, 0))
        tot = _unpack128(sum_slots(gs, "sum_small", 1024), self.small_shapes)
        out = {}
        for n, g in zip(small, tot):
            if n in SMALL_SHARDED:
                width = self.w[n].shape[-1]
                g = lax.dynamic_slice_in_dim(g, self.me * width, width, axis=g.ndim - 1)
            out[n] = g
        return out, tot[-1].reshape(())


def local_step(x, mem, target, ex):
    bsz, seq, _ = x.shape
    t = bsz * seq
    nb = t // TB
    nc = seq // CHUNK
    x0 = x.reshape(t, D)
    mem2 = mem.reshape(bsz * N_MEM, D)
    tgt = target.reshape(t, D)
    p = ex.small
    gains = p['norm_gains']
    big = {}

    def gain(layer, i):
        g = gains[layer, i].reshape(1, D)
        for tok in ex.take_tokens():
            g = g + tok[0, 0]
        return g

    consts = _ssd_consts()
    grads = {}
    saved = [dict(), dict()]

    def matmul_res(a, b, name, xin, ga, gb):
        return matmul(a, b, 'nn', name, (F32, F32, BF), epilogue=res_epilogue, extras=[xin], params=[ga, gb])

    def attn_specs():
        nq = seq // TB
        q = pl.BlockSpec((TB, D), lambda b, i: (b * nq + i, 0))
        kv = pl.BlockSpec((N_MEM, 2 * D), lambda b, i: (b, 0))
        return (bsz, nq), q, kv

    def attention_fwd(layer, xin, hin, sv, ga, gb):
        q = matmul(hin, big[('xa_wq', layer)], 'nn', f"q_{layer}", BF)
        kv = matmul(mem2, big[('xa_wkv', layer)], 'nt', f"kv_{layer}", BF)
        grid, qs, kvs = attn_specs()
        o, = fwd_call(attn_fn, f"attn_{layer}", grid, [q, kv], [qs, kvs], [_sd((t, D), BF)], [qs])
        ao, x_next, h_next = matmul_res(o, big[('xa_wo', layer)], f"ao_{layer}", xin, ga, gb)
        sv.update(q=q, kv=kv, o=o, ao=ao)
        return ao, x_next, h_next

    def mlp_fwd(layer, hin, sv, res=None):
        r, rr = matmul(hin, big[('mlp_w1', layer)], 'nt', f"mlp1_{layer}", (BF, BF), epilogue=act_epilogue)
        if res is None:
            out = (matmul(rr, big[('mlp_w2', layer)], 'nn', f"mlp2_{layer}"),)
        else:
            out = matmul_res(rr, big[('mlp_w2', layer)], f"mlp2_{layer}", *res)
        sv.update(r=r, rr=rr, mo=out[0])
        return out

    sv = saved[0]
    h0, = fwd_call(seg_in, "norm_in", (nb,), [x0, gain(0, 0)], [_rows(D), _par(D)], [_sd((t, D), BF)], [_rows(D)])
    big.update(ex.weights('ab', h0))
    xbc0 = POOL_W + SSM_INNER
    w_ab_in = big[('ab_w_in', 0)]
    w_ab_in = _pad_rows(jnp.concatenate([w_ab_in[:xbc0], _xbc_group(w_ab_in[xbc0:xbc0 + SSM_CONV_DIM], 0),
                                         w_ab_in[xbc0 + SSM_CONV_DIM:]], axis=0), AB_IN_PAD)
    conv_w, conv_b = _xbc_group(p['ssm_conv_w'][0], 1), _xbc_group(p['ssm_conv_b'], 1)
    u0 = matmul(h0, w_ab_in, 'nt', "ab_in")
    pool_outs = []
    for g in range(POOL_GROUPS):
        seqspec = pl.BlockSpec((seq, PG), lambda b, g=g: (b, g))
        po, = fwd_call(make_pool_fn(g), f"pool_{g}", (bsz,), [u0, p['pool_w'][0, g], p['pool_scale']],
                       [seqspec, pl.BlockSpec((PG, PG), lambda b: (0, 0)), pl.BlockSpec((1, PG), lambda b, g=g: (0, g))],
                       [_sd((t, PG), BF)], [pl.BlockSpec((seq, PG), lambda b: (b, 0))])
        pool_outs.append(po)
    cw = 256
    ncb = SSM_CONV_DIM // cw
    cbase = (POOL_W + SSM_INNER) // cw
    conv_in_specs = [pl.BlockSpec((seq, cw), lambda j, b: (b, cbase + j)), pl.BlockSpec((SSM_CONV, cw), lambda j, b: (0, j)),
                     pl.BlockSpec((1, cw), lambda j, b: (0, j))]
    conv_out_spec = pl.BlockSpec((seq, cw), lambda j, b: (b, j))
    xbc_act, = fwd_call(conv4_fn, "ssm_conv", (ncb, bsz), [u0, conv_w, conv_b], conv_in_specs,
                        [_sd((t, SSM_CONV_DIM))], [conv_out_spec])
    dtb = jnp.pad(p['ssm_dt_bias'], ((0, 0), (0, LANE - SSM_HEADS)))
    alog = jnp.pad(p['ssm_a_log'], ((0, 0), (0, LANE - SSM_HEADS)))
    dsk = jnp.pad(p['ssm_d'], ((0, 0), (0, LANE - SSM_HEADS)))
    yn, hs = ssd_fwd(xbc_act, u0, dtb, alog, dsk, p['ssm_norm'], consts, bsz, seq)
    mix0 = jnp.concatenate(pool_outs + [yn], axis=1)
    m0, x1, h2 = matmul_res(mix0, big[('ab_w_out', 0)], "ab_out", x0, gain(0, 1), gain(0, 2))
    big.update(ex.weights('l0a', h2))
    ex.start_gather('cd', after=[ex.done['l0a']])
    ao0, x2, h3 = attention_fwd(0, x1, h2, sv, gain(0, 3), gain(0, 4))
    big.update(ex.weights('l0b', h3))
    mo0, x3, h4 = mlp_fwd(0, h3, sv, (x2, gain(0, 5), gain(1, 0)))
    big.update(ex.weights('cd', mo0))
    ex.start_gather('l1a', after=[ex.done['cd']])
    ex.start_gather('l1b', after=[ex.done['cd']])

    sv1 = saved[1]
    nd = D // LANE
    w_cd_in = big[('cd_w_in', 0)].reshape(5, nd, LANE, D).transpose(1, 0, 2, 3).reshape(CD_IN, D)
    u1 = matmul(h4, w_cd_in, 'nt', "cd_in")
    cd_par = [pl.BlockSpec((CONF_K, LANE), lambda j, b: (0, j)), pl.BlockSpec((1, LANE), lambda j, b: (0, j)),
              pl.BlockSpec((SC_K, LANE), lambda j, b: (0, j))]
    cd_ins = [u1, p['conf_dw_w'][0], p['conf_dw_b'], p['sc_conv_w'][0]]
    cd_u_spec = pl.BlockSpec((seq, 5 * LANE), lambda j, b: (b, j))
    cd_in_specs = [cd_u_spec] + cd_par
    cd_out_spec = pl.BlockSpec((seq, LANE), lambda j, b: (b, j))
    vconv, mix1 = fwd_call(cd1_fn, "cd_conv", (nd, bsz), cd_ins, cd_in_specs, [_sd((t, D)), _sd((t, CD_OUT), BF)],
                           [cd_out_spec, pl.BlockSpec((seq, LANE), lambda j, b: (b, nd + j))])
    mix1, = fwd_call(seg_ln, "conf_ln", (nb,), [vconv, p['conf_ln_g'], p['conf_ln_b']], [_rows(D), _par(D), _par(D)],
                     [_sd((t, CD_OUT), BF)], [_rows(D)], into=mix1)
    m1, x4, h5 = matmul_res(mix1, big[('cd_w_out', 0)], "cd_out", x3, gain(1, 1), gain(1, 2))
    big.update(ex.weights('l1a', h5))
    ao1, x5, h6 = attention_fwd(1, x4, h5, sv1, gain(1, 3), gain(1, 4))
    big.update(ex.weights('l1b', h6))
    mo1, = mlp_fwd(1, h6, sv1)

    def loss_body(x_ref, m_ref, g_ref, t_ref, dx_ref, dm_ref, dg_ref, acc_ref):
        (y,), vjp = jax.vjp(seg_out, x_ref[...], m_ref[...], g_ref[...])
        d = y - t_ref[...]
        dx, dm, dg = vjp((d / float(D),))
        dx_ref[...] = dx
        dm_ref[...] = dm.astype(dm_ref.dtype)

        @pl.when(pl.program_id(0) == 0)
        def _():
            acc_ref[...] = jnp.zeros_like(acc_ref)
            dg_ref[...] = jnp.zeros_like(dg_ref)

        acc_ref[...] += jnp.sum(d * d, axis=0, keepdims=True)
        dg_ref[...] += dg

    dx5, dmo1, dg15, lanes = pl.pallas_call(
        loss_body, name="loss_head", grid=(nb,), in_specs=[_rows(D), _rows(D), _par(D), _rows(D)],
        out_specs=[_rows(D), _rows(D), _par(D), _par(D)], out_shape=[_sd((t, D)), _sd((t, D), BF), _sd((1, D)), _sd((1, D))],
        compiler_params=_params())(x5, mo1, gain(1, 5), tgt)
    loss = 0.5 * jnp.sum(lanes) / float(D)

    gain_grads = {(1, 5): dg15}

    def bwd_seg_res(xin, m, ga, gb, dx1, dh, name):
        return bwd_call(seg_res, name, (nb,), [xin, m, ga, gb], [_rows(D), _rows(D), _par(D), _par(D)], [dx1, dh],
                        [_rows(D), _rows(D)], [0, 1, 2, 3], [_sd((t, D)), _sd((t, D), BF), _sd((1, D)), _sd((1, D))],
                        [_rows(D), _rows(D), _par(D), _par(D)], [None, None, (0,), (0,)])

    def mlp_bwd(layer, hin, dmo, sv):
        grads_w2 = matmul(sv['rr'], dmo, 'tn', f"d_mlp_w2_{layer}", BF)
        dr, = matmul(dmo, big[('mlp_w2', layer)], 'nt', f"d_r_{layer}", (BF,), epilogue=act_bwd_epilogue, extras=[sv['r']])
        grads_w1 = matmul(dr, hin, 'tn', f"d_mlp_w1_{layer}", BF)
        dh = matmul(dr, big[('mlp_w1', layer)], 'nn', f"d_h_mlp_{layer}")
        return dh, grads_w1, grads_w2

    def attention_bwd(layer, hin, dao, sv):
        g_wo = matmul(sv['o'], dao, 'tn', f"d_xa_wo_{layer}", BF)
        do = matmul(dao, big[('xa_wo', layer)], 'nt', f"d_o_{layer}", BF)
        grid, qs, kvs = attn_specs()
        dq, dkv = bwd_call(attn_fn, f"d_attn_{layer}", grid, [sv['q'], sv['kv']], [qs, kvs], [do], [qs], [0, 1],
                           [_sd((t, D), BF), _sd((bsz * N_MEM, 2 * D))], [qs, kvs], [None, (1,)])
        g_wkv = matmul(dkv, mem2, 'tn', f"d_xa_wkv_{layer}", BF)
        g_wq = matmul(hin, dq, 'tn', f"d_xa_wq_{layer}", BF)
        dh = matmul(dq, big[('xa_wq', layer)], 'nt', f"d_h_attn_{layer}")
        return dh, g_wq, g_wkv, g_wo

    per_layer = {k: [None, None] for k in ('xa_wq', 'xa_wkv', 'xa_wo', 'mlp_w1', 'mlp_w2')}

    dh6, per_layer['mlp_w1'][1], per_layer['mlp_w2'][1] = mlp_bwd(1, h6, dmo1, sv1)
    dx4, dao1, gain_grads[(1, 3)], gain_grads[(1, 4)] = bwd_seg_res(x4, ao1, gain(1, 3), gain(1, 4), dx5, dh6, "d_res_1b")
    dh5, per_layer['xa_wq'][1], per_layer['xa_wkv'][1], per_layer['xa_wo'][1] = attention_bwd(1, h5, dao1, sv1)
    ex.put_grads('l1', G_L1, {(k, 1): v[1] for k, v in per_layer.items()})
    dx3, dm1, gain_grads[(1, 1)], gain_grads[(1, 2)] = bwd_seg_res(x3, m1, gain(1, 1), gain(1, 2), dx4, dh5, "d_res_1a")
    g_cd_out = matmul(mix1, dm1, 'tn', "d_cd_w_out", BF)
    dmix1 = matmul(dm1, big[('cd_w_out', 0)], 'nt', "d_mix1")
    dvconv, dlg, dlb = bwd_call(seg_ln, "d_conf_ln", (nb,), [vconv, p['conf_ln_g'], p['conf_ln_b']],
                                [_rows(D), _par(D), _par(D)], [dmix1], [_rows(D, 0)], [0, 1, 2],
                                [_sd((t, D)), _sd((1, D)), _sd((1, D))], [_rows(D), _par(D), _par(D)], [None, (0,), (0,)])
    grads['conf_ln_g'], grads['conf_ln_b'] = dlg, dlb
    cd_g = bwd_call(cd1_fn, "d_cd_conv", (nd, bsz), cd_ins, cd_in_specs, [dvconv, dmix1],
                    [cd_out_spec, pl.BlockSpec((seq, LANE), lambda j, b: (b, nd + j))], list(range(4)),
                    [_sd((t, CD_IN), BF), _sd((CONF_K, D)), _sd((1, D)), _sd((SC_K, D))], [cd_u_spec] + cd_par,
                    [None, (1,), (1,), (1,)])
    du1 = cd_g[0]
    grads['conf_dw_w'], grads['conf_dw_b'], grads['sc_conv_w'] = cd_g[1][None], cd_g[2], cd_g[3][None]
    g_cd_in = matmul(du1, h4, 'tn', "d_cd_w_in", BF).reshape(nd, 5, LANE, D).transpose(1, 0, 2, 3).reshape(CD_IN, D)
    ex.put_grads('cd', G_CD, {('cd_w_in', 0): g_cd_in, ('cd_w_out', 0): g_cd_out})
    dh4 = matmul(du1, w_cd_in, 'nn', "d_h_cd")

    dx2, dmo0, gain_grads[(0, 5)], gain_grads[(1, 0)] = bwd_seg_res(x2, mo0, gain(0, 5), gain(1, 0), dx3, dh4, "d_res_0c")
    dh3, per_layer['mlp_w1'][0], per_layer['mlp_w2'][0] = mlp_bwd(0, h3, dmo0, sv)
    dx1, dao0, gain_grads[(0, 3)], gain_grads[(0, 4)] = bwd_seg_res(x1, ao0, gain(0, 3), gain(0, 4), dx2, dh3, "d_res_0b")
    dh2, per_layer['xa_wq'][0], per_layer['xa_wkv'][0], per_layer['xa_wo'][0] = attention_bwd(0, h2, dao0, sv)
    ex.put_grads('l0', G_L0, {(k, 0): v[0] for k, v in per_layer.items()})
    dx0r, dm0, gain_grads[(0, 1)], gain_grads[(0, 2)] = bwd_seg_res(x0, m0, gain(0, 1), gain(0, 2), dx1, dh2, "d_res_0a")
    g_ab_out = matmul(mix0, dm0, 'tn', "d_ab_w_out", BF)
    dmix0 = matmul(dm0, big[('ab_w_out', 0)], 'nt', "d_mix0")
    dxbc_act, dz, ddt, ddtb, dalog, ddsk, dnw = ssd_bwd(xbc_act, u0, dtb, alog, dsk, p['ssm_norm'], consts, hs, dmix0, bsz, seq)
    grads['ssm_dt_bias'] = ddtb[:, :SSM_HEADS]
    grads['ssm_a_log'] = dalog[:, :SSM_HEADS]
    grads['ssm_d'] = ddsk[:, :SSM_HEADS]
    grads['ssm_norm'] = dnw
    dxr, dcw, dcb = bwd_call(conv4_fn, "d_ssm_conv", (ncb, bsz), [u0, conv_w, conv_b], conv_in_specs,
                             [dxbc_act], [conv_out_spec], [0, 1, 2],
                             [_sd((t, SSM_CONV_DIM), BF), _sd((SSM_CONV, SSM_CONV_DIM)), _sd((1, SSM_CONV_DIM))],
                             [conv_out_spec, conv_in_specs[1], conv_in_specs[2]], [None, (1,), (1,)])
    grads['ssm_conv_w'], grads['ssm_conv_b'] = _xbc_ungroup(dcw, 1)[None], _xbc_ungroup(dcb, 1)
    dpool, dpw, dps = [], [], []
    for g in range(POOL_GROUPS):
        seqspec = pl.BlockSpec((seq, PG), lambda b, g=g: (b, g))
        one = pl.BlockSpec((seq, PG), lambda b: (b, 0))
        wspec = pl.BlockSpec((PG, PG), lambda b: (0, 0))
        sspec = pl.BlockSpec((1, PG), lambda b, g=g: (0, g))
        a, bb, c = bwd_call(make_pool_fn(g), f"d_pool_{g}", (bsz,), [u0, p['pool_w'][0, g], p['pool_scale']],
                            [seqspec, wspec, sspec], [dmix0], [seqspec], [0, 1, 2],
                            [_sd((t, PG), BF), _sd((PG, PG)), _sd((1, PG))], [one, wspec, pl.BlockSpec((1, PG), lambda b: (0, 0))],
                            [None, (0,), (0,)])
        dpool.append(a)
        dpw.append(bb)
        dps.append(c)
    grads['pool_w'] = jnp.stack(dpw)[None]
    grads['pool_scale'] = jnp.concatenate(dps, axis=1)
    du0 = jnp.concatenate(dpool + [dz, dxr, ddt.astype(BF)], axis=1)
    g_ab_in = matmul(du0, h0, 'tn', "d_ab_w_in", BF)
    g_ab_in = jnp.concatenate([g_ab_in[:xbc0], _xbc_ungroup(g_ab_in[xbc0:xbc0 + SSM_CONV_DIM], 0),
                               g_ab_in[xbc0 + SSM_CONV_DIM:AB_IN]], axis=0)
    ex.put_grads('ab', G_AB, {('ab_w_in', 0): g_ab_in, ('ab_w_out', 0): g_ab_out})
    dh0 = matmul(du0, w_ab_in, 'nn', "d_h_ab", after=ex.take_tokens())
    dx, dg00 = bwd_call(seg_in_res, "d_norm_in", (nb,), [x0, gain(0, 0)], [_rows(D), _par(D)], [dx0r, dh0],
                        [_rows(D), _rows(D)], [0, 1], [_sd((t, D)), _sd((1, D))], [_rows(D), _par(D)], [None, (0,)])
    gain_grads[(0, 0)] = dg00
    grads['norm_gains'] = jnp.stack([jnp.concatenate([gain_grads[(l, i)] for i in range(6)], axis=0) for l in range(2)])
    return loss, dx, grads
```

```python
import functools
import math

import numpy as np
import jax
import jax.numpy as jnp
from jax import lax
from jax.experimental import pallas as pl
from jax.experimental.pallas import tpu as pltpu

BF = jnp.bfloat16
F32 = jnp.float32

N_DEV = 8
D = 1024
N_MEM = 256
XA_HEADS = 4
XA_DH = D // XA_HEADS
POOL_GROUPS = 4
PG = 128
POOL_W = POOL_GROUPS * PG
SSM_INNER = 1024
SSM_GROUPS = 2
SSM_GSZ = SSM_INNER // SSM_GROUPS
SSM_HEADS = 16
SSM_P = 64
SSM_N = 128
SSM_CONV = 4
SSM_CONV_DIM = SSM_INNER + 2 * SSM_GROUPS * SSM_N
SSM_XBC_G = SSM_GSZ + 2 * SSM_N
CHUNK = 128
AB_IN = POOL_W + SSM_INNER + SSM_CONV_DIM + SSM_HEADS
AB_IN_PAD = POOL_W + SSM_INNER + SSM_CONV_DIM + 128
AB_OUT = POOL_W + SSM_INNER
CONF_K = 31
SC_K = 3
CD_IN = 5 * D
CD_OUT = 2 * D
MLP_H = 4 * D
RMS_EPS = 1e-6
LN_EPS = 1e-5
ADAM_LR = 0.001
ADAM_B1 = 0.9
ADAM_B2 = 0.999
ADAM_EPS = 1e-08
ADAM_WD = 0.01
ADAM_STEP = 10
VMEM_LIMIT = 56 * 1024 * 1024
LANE = 128

NAMES = ['x', 'mem', 'norm_gains', 'xa_wq', 'xa_wkv', 'xa_wo', 'mlp_w1', 'mlp_w2', 'ab_w_in', 'pool_w', 'pool_scale',
         'ssm_conv_w', 'ssm_conv_b', 'ssm_dt_bias', 'ssm_a_log', 'ssm_d', 'ssm_norm', 'ab_w_out', 'cd_w_in', 'conf_dw_w',
         'conf_dw_b', 'conf_ln_g', 'conf_ln_b', 'sc_conv_w', 'cd_w_out', 'loss_target']
WEIGHTS = NAMES[2:25]
BIG = [('xa_wq', 1), ('xa_wkv', 2), ('xa_wo', 1), ('mlp_w1', 2), ('mlp_w2', 1), ('cd_w_in', 2), ('cd_w_out', 1),
       ('ab_w_out', 1), ('ab_w_in', 2)]
SMALL_SHARDED = ['norm_gains', 'ssm_conv_w', 'conf_dw_w', 'conf_dw_b', 'conf_ln_g', 'conf_ln_b', 'sc_conv_w']
REPLICATED = ['pool_w', 'pool_scale', 'ssm_conv_b', 'ssm_dt_bias', 'ssm_a_log', 'ssm_d', 'ssm_norm']


def _dg(a, b, ca, cb, prec=None):
    return lax.dot_general(a, b, (((ca,), (cb,)), ((), ())), precision=prec, preferred_element_type=F32)


@functools.partial(jax.custom_vjp, nondiff_argnums=(2, 3))
def bdot(a, b, ca, cb):
    return _dg(a.astype(BF), b.astype(BF), ca, cb)


def _bdot_fwd(a, b, ca, cb):
    return bdot(a, b, ca, cb), (a, b)


def _bdot_bwd(ca, cb, res, g):
    a, b = res
    g16, a16, b16 = g.astype(BF), a.astype(BF), b.astype(BF)
    da = _dg(g16, b16, 1, 1 - cb) if ca == 1 else _dg(b16, g16, 1 - cb, 1)
    db = _dg(g16, a16, 0, 1 - ca) if cb == 1 else _dg(a16, g16, 1 - ca, 0)
    return da.astype(a.dtype), db.astype(b.dtype)


bdot.defvjp(_bdot_fwd, _bdot_bwd)


def _split3(a):
    a1 = a.astype(BF)
    r1 = a - a1.astype(F32)
    a2 = r1.astype(BF)
    a3 = (r1 - a2.astype(F32)).astype(BF)
    return a1, a2, a3


def _exact_right(a, c):
    m = a.shape[0]
    if m % 16:
        return sum(_dg(p, c, 1, 0) for p in _split3(a))
    o = _dg(jnp.concatenate(_split3(a), axis=0), c, 1, 0)
    return o[:m] + o[m:2 * m] + o[2 * m:]


def _exact_left(c, a):
    n = a.shape[1]
    o = _dg(c, jnp.concatenate(_split3(a), axis=1), 1, 0)
    return o[:, :n] + o[:, n:2 * n] + o[:, 2 * n:]


@jax.custom_vjp
def cmat(a, c, ct):
    return _exact_right(a, c)


def _cmat_fwd(a, c, ct):
    return cmat(a, c, ct), (c, ct)


def _cmat_bwd(res, g):
    c, ct = res
    return _exact_right(g, ct), jnp.zeros_like(c), jnp.zeros_like(ct)


cmat.defvjp(_cmat_fwd, _cmat_bwd)


@jax.custom_vjp
def cmatl(c, ct, a):
    return _exact_left(c, a)


def _cmatl_fwd(c, ct, a):
    return cmatl(c, ct, a), (c, ct)


def _cmatl_bwd(res, g):
    c, ct = res
    return jnp.zeros_like(c), jnp.zeros_like(ct), _exact_left(ct, g)


cmatl.defvjp(_cmatl_fwd, _cmatl_bwd)


SUBLANES = 8


def _taps(x, shifts, down):
    n, c = x.shape
    pad = _round_up(max(shifts), SUBLANES)
    if pad == 0:
        return {0: x}
    zeros = jnp.zeros((pad, c), x.dtype)
    xp = jnp.concatenate([zeros, x] if down else [x, zeros], axis=0)
    rolled, out = {0: xp}, {}
    for s in shifts:
        a, b = divmod(s, SUBLANES)
        if b not in rolled:
            rolled[b] = pltpu.roll(xp, b if down else n + pad - b, 0)
        off = pad - SUBLANES * a if down else SUBLANES * a
        out[s] = rolled[b][off:off + n]
    return out


def _shift_down(x, k):
    return _taps(x, [k], True)[k]


def _shift_up(x, k):
    return _taps(x, [k], False)[k]


@functools.partial(jax.custom_vjp, nondiff_argnums=(1,))
def shift(x, k):
    return _shift_down(x, k)


def _shift_fwd(x, k):
    return _shift_down(x, k), None


def _shift_bwd(k, _, g):
    return (_shift_up(g, k),)


shift.defvjp(_shift_fwd, _shift_bwd)


@functools.partial(jax.custom_vjp, nondiff_argnums=(2,))
def cconv(u, w, width):
    taps = _taps(u, list(range(width)), True)
    acc = u * w[width - 1:width, :]
    for k in range(width - 1):
        acc = acc + taps[width - 1 - k] * w[k:k + 1, :]
    return acc


def _cconv_fwd(u, w, width):
    return cconv(u, w, width), (u, w)


def _cconv_bwd(width, res, g):
    u, w = res
    rows = lax.broadcasted_iota(jnp.int32, w.shape, 0)
    du = g * w[width - 1:width, :]
    dw = jnp.where(rows == width - 1, jnp.sum(g * u, axis=0, keepdims=True), 0.0)
    g_taps = _taps(g, list(range(width)), False)
    u_taps = _taps(u, list(range(width)), True)
    for k in range(width - 1):
        s = width - 1 - k
        du = du + g_taps[s] * w[k:k + 1, :]
        dw = dw + jnp.where(rows == k, jnp.sum(g * u_taps[s], axis=0, keepdims=True), 0.0)
    return du, dw


cconv.defvjp(_cconv_fwd, _cconv_bwd)


def _rms(x, g):
    return x * lax.rsqrt(jnp.mean(x * x, axis=-1, keepdims=True) + RMS_EPS) * g


def _params(sem=None):
    return pltpu.CompilerParams(dimension_semantics=sem, vmem_limit_bytes=VMEM_LIMIT)


def _f32(v):
    return v if v.dtype == F32 else v.astype(F32)


def _first(axes):
    ok = None
    for ax in axes:
        c = pl.program_id(ax) == 0
        ok = c if ok is None else jnp.logical_and(ok, c)
    return ok


def fwd_call(fn, name, grid, ins, in_specs, out_shapes, out_specs, into=None):
    n_in = len(ins)
    n_into = 0 if into is None else 1

    def body(*refs):
        outs = fn(*[_f32(r[...]) for r in refs[:n_in]])
        for r, o in zip(refs[n_in + n_into:], outs):
            r[...] = o.astype(r.dtype)

    extra = [] if into is None else [into]
    return pl.pallas_call(body, name=name, grid=grid, in_specs=list(in_specs) + [pl.BlockSpec(memory_space=pl.ANY)] * n_into,
                          out_specs=out_specs, out_shape=out_shapes, input_output_aliases={n_in: 0} if n_into else {},
                          compiler_params=_params())(*ins, *extra)


def bwd_call(fn, name, grid, ins, in_specs, cots, cot_specs, gidx, g_shapes, g_specs, g_acc):
    n_in, n_cot = len(ins), len(cots)

    def body(*refs):
        vals = [_f32(r[...]) for r in refs[:n_in]]

        def f_sel(*dv):
            full = list(vals)
            for i, v in zip(gidx, dv):
                full[i] = v
            return tuple(fn(*full))

        outs, vjp = jax.vjp(f_sel, *[vals[i] for i in gidx])
        cts = tuple(_f32(r[...]) for r in refs[n_in:n_in + n_cot])
        grads = vjp(cts)
        for r, g, acc in zip(refs[n_in + n_cot:], grads, g_acc):
            if acc is None:
                r[...] = g.astype(r.dtype)
            else:
                @pl.when(_first(acc))
                def _():
                    r[...] = jnp.zeros_like(r)

                r[...] += g.astype(r.dtype)

    return pl.pallas_call(body, name=name, grid=grid, in_specs=list(in_specs) + list(cot_specs), out_specs=g_specs,
                          out_shape=g_shapes, compiler_params=_params())(*ins, *cots)


def _tile(dim, pref):
    if dim <= pref:
        return dim
    best = None
    for t in range(LANE, pref + 1, LANE):
        if dim % t == 0:
            best = t
    assert best is not None, dim
    return best


MATMUL_VMEM_BUDGET = 40 * 1024 * 1024


def _matmul_tiles(m, n, k, a_bytes, b_bytes, out_bytes):
    tn = _tile(n, 1024)
    for tk_pref in (k, 2048, 1024, 512):
        tk = _tile(k, tk_pref)
        for tm_pref in (1024, 512, 256):
            tm = _tile(m, tm_pref)
            need = 2 * (tm * tk * a_bytes + tk * tn * b_bytes + tm * tn * out_bytes) + (0 if tk == k else tm * tn * 4)
            need += (tm * tk * 2 if a_bytes == 4 else 0) + (tk * tn * 2 if b_bytes == 4 else 0)
            if need <= MATMUL_VMEM_BUDGET:
                return tm, tn, tk
    raise ValueError((m, n, k))


def matmul(a, b, mode, name, out_dtype=F32, epilogue=None, extras=(), params=(), after=()):
    if mode == 'nn':
        (m, k), (k2, n) = a.shape, b.shape
    elif mode == 'nt':
        (m, k), (n, k2) = a.shape, b.shape
    else:
        (k, m), (k2, n) = a.shape, b.shape
    assert k == k2, (name, a.shape, b.shape)
    n_extra = len(extras) + len(params)
    out_dtypes = out_dtype if isinstance(out_dtype, tuple) else (out_dtype,)
    per_out = sum(jnp.dtype(dt).itemsize for dt in out_dtypes) + sum(e.dtype.itemsize for e in extras)
    tm, tn, tk = _matmul_tiles(m, n, k, a.dtype.itemsize, b.dtype.itemsize, per_out)
    nk = k // tk
    ca = 0 if mode == 'tn' else 1
    cb = 1 if mode == 'nt' else 0
    a_spec = pl.BlockSpec((tk, tm), lambda i, j, kk: (kk, i)) if mode == 'tn' else pl.BlockSpec((tm, tk), lambda i, j, kk: (i, kk))
    b_spec = pl.BlockSpec((tn, tk), lambda i, j, kk: (j, kk)) if mode == 'nt' else pl.BlockSpec((tk, tn), lambda i, j, kk: (kk, j))

    def finish(o_refs, extra_refs, acc):
        outs = (acc,) if epilogue is None else epilogue(acc, *[_f32(e[...]) for e in extra_refs])
        for o_ref, o in zip(o_refs, outs):
            o_ref[...] = o.astype(o_ref.dtype)

    n_after = len(after)

    def body_whole_k(a_ref, b_ref, *refs):
        refs = refs[n_after:]
        finish(refs[n_extra:], refs[:n_extra], _dg(a_ref[...].astype(BF), b_ref[...].astype(BF), ca, cb))

    def body_split_k(a_ref, b_ref, *refs):
        refs = refs[n_after:]
        extra_refs, o_refs, acc = refs[:n_extra], refs[n_extra:-1], refs[-1]
        kk = pl.program_id(2)

        @pl.when(kk == 0)
        def _():
            acc[...] = jnp.zeros_like(acc)

        acc[...] += _dg(a_ref[...].astype(BF), b_ref[...].astype(BF), ca, cb)

        @pl.when(kk == nk - 1)
        def _():
            finish(o_refs, extra_refs, acc[...])

    tile = pl.BlockSpec((tm, tn), lambda i, j, kk: (i, j))
    row = pl.BlockSpec((1, tn), lambda i, j, kk: (0, j))
    n_par = len(params)
    outs = pl.pallas_call(
        body_whole_k if nk == 1 else body_split_k, name=name, grid=(m // tm, n // tn, nk),
        in_specs=[a_spec, b_spec] + [pl.BlockSpec(memory_space=pl.ANY)] * n_after + [tile] * len(extras) + [row] * n_par,
        out_specs=[tile] * len(out_dtypes),
        out_shape=[jax.ShapeDtypeStruct((m, n), dt) for dt in out_dtypes],
        scratch_shapes=[] if nk == 1 else [pltpu.VMEM((tm, tn), F32)],
        compiler_params=_params(("parallel", "parallel", "arbitrary")))(a, b, *after, *extras, *params)
    return outs if isinstance(out_dtype, tuple) else outs[0]


_FLIPS = [(0, 0, 1), (1, 0, 0), (0, 1, 0), (1, 1, 0), (1, 0, 1), (0, 1, 1), (1, 1, 1)]


def _me():
    return lax.axis_index("x"), lax.axis_index("y"), lax.axis_index("c")


def _flip(pos, f):
    return tuple(jnp.where(fi == 1, 1 - p, p) if fi else p for p, fi in zip(pos, f))


def _slot(pos):
    return 4 * pos[0] + 2 * pos[1] + pos[2]


def all_gather(v, name):
    def body(v_ref, out_ref, send_sems, recv_sems, local_sem):
        me = _me()
        sibling = _flip(me, (0, 0, 1))
        chips = [_flip(me, f) for f in ((1, 0, 0), (0, 1, 0), (1, 1, 0))]

        def copy(k, block, to, src=None):
            return pltpu.make_async_remote_copy(
                src_ref=out_ref.at[_slot(block)] if src is None else src, dst_ref=out_ref.at[_slot(block)],
                send_sem=send_sems.at[k], recv_sem=recv_sems.at[k], device_id=to, device_id_type=pl.DeviceIdType.MESH)

        mine = pltpu.make_async_copy(v_ref, out_ref.at[_slot(me)], local_sem)
        mine.start()
        first = [copy(0, me, sibling, src=v_ref)] + [copy(1 + j, me, chip, src=v_ref) for j, chip in enumerate(chips)]
        for cp in first:
            cp.start()
        passed = [copy(4 + j, chip, sibling) for j, chip in enumerate(chips)]
        for j, chip in enumerate(chips):
            copy(1 + j, chip, me).wait_recv()
            passed[j].start()
        copy(0, sibling, me).wait_recv()
        for j, chip in enumerate(chips):
            copy(4 + j, _flip(chip, (0, 0, 1)), me).wait_recv()
        for cp in first + passed:
            cp.wait_send()
        mine.wait()

    return pl.pallas_call(
        body, name=name, out_shape=jax.ShapeDtypeStruct((N_DEV,) + v.shape, v.dtype),
        in_specs=[pl.BlockSpec(memory_space=pl.ANY)], out_specs=pl.BlockSpec(memory_space=pl.ANY),
        scratch_shapes=[pltpu.SemaphoreType.DMA((7,)), pltpu.SemaphoreType.DMA((7,)), pltpu.SemaphoreType.DMA(())],
    )(v)


def sum_slots(v, name, tr=256):
    _, r, c = v.shape
    tr = _tile_rows(r, tr)

    def body(v_ref, o_ref):
        acc = v_ref[0].astype(F32)
        for s in range(1, N_DEV):
            acc = acc + v_ref[s].astype(F32)
        o_ref[...] = acc

    return pl.pallas_call(body, name=name, grid=(r // tr,), in_specs=[pl.BlockSpec((N_DEV, tr, c), lambda i: (0, i, 0))],
                          out_specs=pl.BlockSpec((tr, c), lambda i: (i, 0)), out_shape=jax.ShapeDtypeStruct((r, c), F32),
                          compiler_params=_params())(v)


def _tile_rows(r, pref):
    if r <= pref:
        return r
    best = None
    for t in range(8, pref + 1, 8):
        if r % t == 0:
            best = t
    return r if best is None else best


def _adamw_math(w, m, v, g):
    nm = ADAM_B1 * m + (1.0 - ADAM_B1) * g
    nv = ADAM_B2 * v + (1.0 - ADAM_B2) * jnp.square(g)
    m_hat = nm / (1.0 - ADAM_B1 ** ADAM_STEP)
    v_hat = nv / (1.0 - ADAM_B2 ** ADAM_STEP)
    return -ADAM_LR * (m_hat / (jnp.sqrt(v_hat) + ADAM_EPS) + ADAM_WD * w), nm, nv


def update_from_slots(lands, offs, w, m, v, transposed, name):
    layers, a, b = w.shape
    n_land = len(lands)
    if transposed:
        rb, tk = LANE, 512
        assert a % tk == 0 and b % rb == 0 and all(o % rb == 0 for o in offs), (name, w.shape, offs)
        grid = (layers, a // tk, b // rb)
        land_block = (N_DEV, rb, tk)
        tile = pl.BlockSpec((None, tk, rb), lambda l, i, j: (l, i, j))

        def land_spec(layer):
            base = offs[layer] // rb
            return pl.BlockSpec(land_block, lambda l, i, j: (0, base + jnp.where(l == layer, j, 0), jnp.where(l == layer, i, 0)))
    else:
        fits = [t for t in (256, 128, 64) if a % t == 0 and all(o % t == 0 for o in offs)]
        assert fits or all(o == 0 for o in offs), (name, w.shape, offs)
        tr = max(fits) if fits else a
        grid = (layers, a // tr)
        land_block = (N_DEV, _round_up(tr, MEMBER_ROW_TILE), b)
        tile = pl.BlockSpec((None, tr, b), lambda l, i: (l, i, 0))

        def land_spec(layer):
            base = offs[layer] // tr
            return pl.BlockSpec(land_block, lambda l, i: (0, base + jnp.where(l == layer, i, 0), 0))

    def body(*refs):
        land_refs, (w_ref, m_ref, v_ref, g_ref, d_ref, nm_ref, nv_ref, acc) = refs[:n_land], refs[n_land:]
        for layer, land in enumerate(land_refs):
            @pl.when(pl.program_id(0) == layer)
            def _(land=land):
                rows = acc.shape[0]
                s = land[0, :rows].astype(F32)
                for k in range(1, N_DEV):
                    s = s + land[k, :rows].astype(F32)
                acc[...] = s

        g = acc[...].T if transposed else acc[...]
        d, nm, nv = _adamw_math(w_ref[...], m_ref[...], v_ref[...], g)
        g_ref[...] = g
        d_ref[...] = d
        nm_ref[...] = nm
        nv_ref[...] = nv

    sh = jax.ShapeDtypeStruct(w.shape, F32)
    return pl.pallas_call(
        body, name=name, grid=grid, in_specs=[land_spec(layer) for layer in range(n_land)] + [tile] * 3, out_specs=[tile] * 4,
        out_shape=[sh] * 4, scratch_shapes=[pltpu.VMEM((rb, tk) if transposed else (tr, b), F32)],
        compiler_params=_params())(*lands, w, m, v)


def adamw_many(ws, ms, vs, gs, name):
    n = len(ws)

    def body(*refs):
        for i in range(n):
            d, nm, nv = _adamw_math(refs[i][...], refs[n + i][...], refs[2 * n + i][...], refs[3 * n + i][...])
            refs[4 * n + i][...] = d
            refs[5 * n + i][...] = nm
            refs[6 * n + i][...] = nv

    vmem = pl.BlockSpec(memory_space=pltpu.VMEM)
    shapes = [jax.ShapeDtypeStruct(a.shape, F32) for a in ws]
    res = pl.pallas_call(body, name=name, in_specs=[vmem] * (4 * n), out_specs=[vmem] * (3 * n), out_shape=shapes * 3,
                         compiler_params=_params())(*ws, *ms, *vs, *gs)
    return res[:n], res[n:2 * n], res[2 * n:]


def seg_in(x, g):
    return (_rms(x, g),)


def seg_in_res(x, g):
    return x, _rms(x, g)


def seg_res(x, m, ga, gb):
    x1 = x + _rms(m, ga)
    return x1, _rms(x1, gb)


def seg_out(x, m, ga):
    return (x + _rms(m, ga),)


def act_epilogue(r):
    t = jnp.maximum(r, 0.0)
    return r, t * t


def res_epilogue(m, x, ga, gb):
    x1, h = seg_res(x, m, ga, gb)
    return m, x1, h


def act_bwd_epilogue(drr, r):
    return (drr * (2.0 * jnp.maximum(r, 0.0)),)


def seg_ln(v, g, b):
    mu = jnp.mean(v, axis=-1, keepdims=True)
    var = jnp.mean(jnp.square(v - mu), axis=-1, keepdims=True)
    vn = (v - mu) * lax.rsqrt(var + LN_EPS) * g + b
    return (jax.nn.silu(vn),)


def make_pool_fn(group):
    window = 2 ** (group + 1)

    def pool_fn(ug, pw, scale):
        s = ug
        for lvl in range(group + 1):
            s = s + shift(s, 2 ** lvl)
        cnt = jnp.minimum(lax.broadcasted_iota(jnp.int32, ug.shape, 0) + 1, window).astype(F32)
        return (bdot(s / cnt - ug, pw, 1, 0) * scale,)

    return pool_fn


def conv4_fn(xr, w, b):
    return (jax.nn.silu(cconv(xr, w, SSM_CONV) + b),)


def cd1_fn(u, dww, dwb, scw):
    val, gate, bg, cg, hh = (u[:, k * LANE:(k + 1) * LANE] for k in range(5))
    v = val * jax.nn.sigmoid(gate)
    vc = cconv(v, dww, CONF_K) + dwb
    sc = bg * cconv(cg * hh, scw, SC_K)
    return vc, sc


def attn_fn(q, kv):
    outs = []
    for h in range(XA_HEADS):
        cols = slice(h * XA_DH, (h + 1) * XA_DH)
        s = bdot(q[:, cols], kv[:, cols], 1, 1) / math.sqrt(XA_DH)
        p = jax.nn.softmax(s, axis=-1)
        outs.append(bdot(p, kv[:, D + h * XA_DH:D + (h + 1) * XA_DH], 1, 0))
    return (jnp.concatenate(outs, axis=1),)


def ssd_chunk(xbc, z, dtraw, dtb, alog, dsk, nw, h0, h1, h2, h3, e64, e64t, ecat, ecatt, tril, trilt):
    xs, bm, cm = xbc[:, :SSM_GSZ], xbc[:, SSM_GSZ:SSM_GSZ + SSM_N], xbc[:, SSM_GSZ + SSM_N:]
    hin = (h0, h1, h2, h3)
    dt = jax.nn.softplus(dtraw + dtb)
    a = -jnp.exp(alog)
    d_a = dt * a
    cs = cmatl(tril, trilt, d_a)
    cs_cat = cmat(cs, ecat, ecatt)
    cs64, cs128 = cs_cat[:, :SSM_GSZ], cs_cat[:, SSM_GSZ:]
    dt64 = cmat(dt, e64, e64t)
    row = lax.broadcasted_iota(jnp.int32, (8, LANE), 0)
    heads = jnp.where(row == 0, dsk, jnp.where(row == 1, jnp.sum(d_a, axis=0, keepdims=True), 0.0))
    heads64 = cmat(heads, e64, e64t)
    d64, tot64 = heads64[0:1, :], heads64[1:2, :]
    xdt = xs * dt64
    cb = bdot(cm, bm, 1, 1)
    li = lax.broadcasted_iota(jnp.int32, (CHUNK, CHUNK), 0)
    si = lax.broadcasted_iota(jnp.int32, (CHUNK, CHUNK), 1)
    causal = li >= si
    lane = lax.broadcasted_iota(jnp.int32, (CHUNK, LANE), 1)
    xw = xdt * jnp.exp(tot64 - cs64)
    ecs = jnp.exp(cs64)
    etot = jnp.exp(tot64)
    ycols, hout = [], []
    for j in range(4):
        sl = slice(j * LANE, (j + 1) * LANE)
        xj = xdt[:, sl]
        ys = []
        for hh in range(2):
            r = 2 * j + hh
            col = cs128[:, r * LANE:(r + 1) * LANE]
            decay = jnp.exp(jnp.where(causal, col - col.T, -1e30))
            ys.append(bdot(cb * decay, xj, 1, 0))
        y_diag = jnp.where(lane < SSM_P, ys[0], ys[1])
        y_off = bdot(cm, hin[j], 1, 0) * ecs[:, sl]
        ycols.append(y_diag + y_off)
        hout.append(etot[:, sl] * hin[j] + bdot(bm, xw[:, sl], 0, 0))
    y = jnp.concatenate(ycols, axis=1) + d64 * xs
    y = y * jax.nn.silu(z)
    yn = y * lax.rsqrt(jnp.mean(y * y, axis=-1, keepdims=True) + RMS_EPS) * nw
    return (yn,) + tuple(hout)


def _xbc_group(a, axis):
    parts = []
    for g in range(SSM_GROUPS):
        for start, width in ((g * SSM_GSZ, SSM_GSZ), (SSM_INNER + g * SSM_N, SSM_N), (SSM_INNER + (SSM_GROUPS + g) * SSM_N, SSM_N)):
            parts.append(lax.slice_in_dim(a, start, start + width, axis=axis))
    return jnp.concatenate(parts, axis=axis)


def _xbc_ungroup(a, axis):
    xs, bs, cs = [], [], []
    for g in range(SSM_GROUPS):
        base = g * SSM_XBC_G
        xs.append(lax.slice_in_dim(a, base, base + SSM_GSZ, axis=axis))
        bs.append(lax.slice_in_dim(a, base + SSM_GSZ, base + SSM_GSZ + SSM_N, axis=axis))
        cs.append(lax.slice_in_dim(a, base + SSM_GSZ + SSM_N, base + SSM_XBC_G, axis=axis))
    return jnp.concatenate(xs + bs + cs, axis=axis)


def _ssd_consts():
    h = np.arange(LANE)[:, None]
    e64 = np.stack([(h == g * 8 + np.arange(SSM_GSZ)[None, :] // SSM_P) for g in range(SSM_GROUPS)]).astype(np.float32)
    e128 = np.stack([(h == g * 8 + np.arange(8 * LANE)[None, :] // LANE) for g in range(SSM_GROUPS)]).astype(np.float32)
    ecat = np.concatenate([e64, e128], axis=2)
    tril = np.tril(np.ones((CHUNK, CHUNK), np.float32))
    return tuple(jnp.asarray(c, dtype=BF) for c in (e64, e64.transpose(0, 2, 1), ecat, ecat.transpose(0, 2, 1), tril, tril.T))


def _ssd_specs(nc, rev):
    def ci(c):
        return nc - 1 - c if rev else c

    def row(width, col):
        return pl.BlockSpec((CHUNK, width), lambda b, c: (b * nc + ci(c), col))

    def whole(shape):
        return pl.BlockSpec(shape, lambda b, c: (0,) * len(shape))

    data = [row(SSM_CONV_DIM, 0),
            row(SSM_GSZ, 1), row(SSM_GSZ, 2), row(LANE, 24)]
    par = [whole((1, LANE))] * 3 + [whole((1, SSM_INNER))]
    cst = [whole((SSM_GROUPS, LANE, SSM_GSZ)), whole((SSM_GROUPS, SSM_GSZ, LANE)), whole((SSM_GROUPS, LANE, 12 * LANE)),
           whole((SSM_GROUPS, 12 * LANE, LANE)), whole((CHUNK, CHUNK)), whole((CHUNK, CHUNK))]
    hsave = pl.BlockSpec((None, None, SSM_GROUPS, 4, SSM_N, LANE), lambda b, c: (b, ci(c), 0, 0, 0, 0))
    return data, par, cst, hsave, row, whole


def _ssd_group_args(g, xbc, z, dtr, dtb, alog, dsk, nw):
    return (xbc[:, g * SSM_XBC_G:(g + 1) * SSM_XBC_G], z[g], dtr, dtb, alog, dsk, nw[:, g * SSM_GSZ:(g + 1) * SSM_GSZ])


def ssd_fwd(xbc_act, u, dtb, alog, dsk, nw, consts, bsz, seq):
    nc = seq // CHUNK
    data, par, cst, hsave, row, _ = _ssd_specs(nc, False)

    def body(xbc, z0, z1, dtr, dtb_r, alog_r, dsk_r, nw_r, e64, e64t, ecat, ecatt, tril, trilt, yn_ref, hs_ref, h):
        @pl.when(pl.program_id(1) == 0)
        def _():
            h[...] = jnp.zeros_like(h)

        hs_ref[...] = h[...]
        ys = []
        for g in range(SSM_GROUPS):
            args = _ssd_group_args(g, xbc[...], (z0[...], z1[...]), dtr[...], dtb_r[...], alog_r[...], dsk_r[...], nw_r[...])
            outs = ssd_chunk(*args, h[g, 0], h[g, 1], h[g, 2], h[g, 3], e64[g], e64t[g], ecat[g], ecatt[g], tril[...], trilt[...])
            ys.append(outs[0])
            for j in range(4):
                h[g, j] = outs[1 + j]
        yn_ref[...] = jnp.concatenate(ys, axis=1).astype(yn_ref.dtype)

    t = bsz * seq
    return pl.pallas_call(
        body, name="ssd_fwd", grid=(bsz, nc), in_specs=data + par + cst, out_specs=[row(SSM_INNER, 0), hsave],
        out_shape=[jax.ShapeDtypeStruct((t, SSM_INNER), BF), jax.ShapeDtypeStruct((bsz, nc, SSM_GROUPS, 4, SSM_N, LANE), F32)],
        scratch_shapes=[pltpu.VMEM((SSM_GROUPS, 4, SSM_N, LANE), F32)], compiler_params=_params(),
    )(xbc_act, u, u, u, dtb, alog, dsk, nw, *consts)


def ssd_bwd(xbc_act, u, dtb, alog, dsk, nw, consts, hs, dmix, bsz, seq):
    nc = seq // CHUNK
    data, par, cst, hsave, row, whole = _ssd_specs(nc, True)
    t = bsz * seq
    pcol = POOL_W // SSM_GSZ

    def body(xbc, z0, z1, dtr, dtb_r, alog_r, dsk_r, nw_r, e64, e64t, ecat, ecatt, tril, trilt, hs_ref, dy0, dy1,
             dxbc, dz, ddt, ddtb, dalog, ddsk, dnw, dh):
        @pl.when(pl.program_id(1) == 0)
        def _():
            dh[...] = jnp.zeros_like(dh)

        per_group = []
        for g, dyn in enumerate((dy0, dy1)):
            cst_vals = (e64[g], e64t[g], ecat[g], ecatt[g], tril[...], trilt[...])
            prim = _ssd_group_args(g, xbc[...], (z0[...], z1[...]), dtr[...], dtb_r[...], alog_r[...], dsk_r[...], nw_r[...])
            prim = prim + (hs_ref[g, 0], hs_ref[g, 1], hs_ref[g, 2], hs_ref[g, 3])
            _, vjp = jax.vjp(lambda *args, c=cst_vals: ssd_chunk(*args, *c), *prim)
            gr = vjp((dyn[...].astype(F32), dh[g, 0], dh[g, 1], dh[g, 2], dh[g, 3]))
            for j in range(4):
                dh[g, j] = gr[7 + j]
            per_group.append(gr)
        g0, g1 = per_group
        dxbc[...] = jnp.concatenate([g0[0], g1[0]], axis=1)
        dz[...] = jnp.concatenate([g0[1], g1[1]], axis=1).astype(dz.dtype)
        ddt[...] = g0[2] + g1[2]

        @pl.when(_first((0, 1)))
        def _():
            for r in (ddtb, dalog, ddsk, dnw):
                r[...] = jnp.zeros_like(r)

        ddtb[...] += g0[3] + g1[3]
        dalog[...] += g0[4] + g1[4]
        ddsk[...] += g0[5] + g1[5]
        dnw[...] += jnp.concatenate([g0[6], g1[6]], axis=1)

    out_specs = [row(SSM_CONV_DIM, 0), row(SSM_INNER, 0), row(LANE, 0), whole((1, LANE)), whole((1, LANE)), whole((1, LANE)),
                 whole((1, SSM_INNER))]
    lane = jax.ShapeDtypeStruct((1, LANE), F32)
    out_shape = [jax.ShapeDtypeStruct((t, SSM_CONV_DIM), F32), jax.ShapeDtypeStruct((t, SSM_INNER), BF),
                 jax.ShapeDtypeStruct((t, LANE), F32), lane, lane, lane, jax.ShapeDtypeStruct((1, SSM_INNER), F32)]
    return pl.pallas_call(
        body, name="ssd_bwd", grid=(bsz, nc), in_specs=data + par + cst + [hsave, row(SSM_GSZ, pcol), row(SSM_GSZ, pcol + 1)],
        out_specs=out_specs, out_shape=out_shape, scratch_shapes=[pltpu.VMEM((SSM_GROUPS, 4, SSM_N, LANE), F32)],
        compiler_params=_params(),
    )(xbc_act, u, u, u, dtb, alog, dsk, nw, *consts, hs, dmix, dmix)


TB = 512


def _rows(d, col=0):
    return pl.BlockSpec((TB, d), lambda i: (i, col))


def _par(d):
    return pl.BlockSpec((1, d), lambda i: (0, 0))


def _sd(shape, dtype=F32):
    return jax.ShapeDtypeStruct(shape, dtype)


def _round_up(n, m):
    return -(-n // m) * m


def _pad_rows(a, rows):
    return jnp.pad(a, ((0, rows - a.shape[0]), (0, 0)))


def _pack128(arrs):
    flat = jnp.concatenate([a.reshape(-1) for a in arrs])
    n = flat.shape[0]
    rows = -(-n // (8 * LANE)) * 8
    return jnp.pad(flat, (0, rows * LANE - n)).reshape(rows, LANE)


def _unpack128(packed, shapes):
    flat = packed.reshape(-1)
    out, off = [], 0
    for s in shapes:
        n = int(np.prod(s))
        out.append(flat[off:off + n].reshape(s))
        off += n
    return out


def kernel(x, mem, norm_gains, xa_wq, xa_wkv, xa_wo, mlp_w1, mlp_w2, ab_w_in, pool_w, pool_scale, ssm_conv_w, ssm_conv_b, ssm_dt_bias, ssm_a_log, ssm_d, ssm_norm, ab_w_out, cd_w_in, conf_dw_w, conf_dw_b, conf_ln_g, conf_ln_b, sc_conv_w, cd_w_out, loss_target, m_norm_gains, m_xa_wq, m_xa_wkv, m_xa_wo, m_mlp_w1, m_mlp_w2, m_ab_w_in, m_pool_w, m_pool_scale, m_ssm_conv_w, m_ssm_conv_b, m_ssm_dt_bias, m_ssm_a_log, m_ssm_d, m_ssm_norm, m_ab_w_out, m_cd_w_in, m_conf_dw_w, m_conf_dw_b, m_conf_ln_g, m_conf_ln_b, m_sc_conv_w, m_cd_w_out, v_norm_gains, v_xa_wq, v_xa_wkv, v_xa_wo, v_mlp_w1, v_mlp_w2, v_ab_w_in, v_pool_w, v_pool_scale, v_ssm_conv_w, v_ssm_conv_b, v_ssm_dt_bias, v_ssm_a_log, v_ssm_d, v_ssm_norm, v_ab_w_out, v_cd_w_in, v_conf_dw_w, v_conf_dw_b, v_conf_ln_g, v_conf_ln_b, v_sc_conv_w, v_cd_w_out):
    args = locals()
    w = {n: args[n] for n in WEIGHTS}
    mom_m = {n: args["m_" + n] for n in WEIGHTS}
    mom_v = {n: args["v_" + n] for n in WEIGHTS}
    ex = Exchange(w)
    loss_local, grad_x, small_grads = local_step(x, mem, loss_target, ex)
    outs = {}

    started = ex.put_small(small_grads, loss_local)
    landed = {key: ex.landed(key, started) for key in ('l1', 'cd', 'l0')}
    late = []
    for n, keys in (('mlp_w1', ('l0', 'l1')), ('mlp_w2', ('l0', 'l1')), ('xa_wkv', ('l0', 'l1')), ('xa_wq', ('l0', 'l1')),
                    ('xa_wo', ('l0', 'l1')), ('cd_w_in', ('cd',)), ('cd_w_out', ('cd',))):
        lands = [landed[key][0] for key in keys]
        offs = [landed[key][1][(n, layer)] for layer, key in enumerate(keys)]
        outs[n] = update_from_slots(lands, offs, w[n], mom_m[n], mom_v[n], SHARD_AXIS[n] == 2, "update_" + n)
        late.append(outs[n][1])
    g_own, loss = ex.reduced_small(late)
    land_ab, offs_ab = ex.landed('ab', late)
    outs['ab_w_out'] = update_from_slots([land_ab], [offs_ab[('ab_w_out', 0)]], w['ab_w_out'], mom_m['ab_w_out'],
                                         mom_v['ab_w_out'], False, "update_ab_w_out")
    res = update_from_slots([land_ab], [offs_ab[('ab_w_in', 0)]], jnp.swapaxes(w['ab_w_in'], 1, 2), jnp.swapaxes(mom_m['ab_w_in'], 1, 2),
                            jnp.swapaxes(mom_v['ab_w_in'], 1, 2), False, "update_ab_w_in")
    outs['ab_w_in'] = tuple(jnp.swapaxes(r, 1, 2) for r in res)
    small = SMALL_SHARDED + REPLICATED
    upd = adamw_many([w[n] for n in small], [mom_m[n] for n in small], [mom_v[n] for n in small], [g_own[n] for n in small],
                     "adamw_small")
    for i, n in enumerate(small):
        outs[n] = (g_own[n], upd[0][i], upd[1][i], upd[2][i])
    return (loss, grad_x.reshape(x.shape), *[outs[n][0] for n in WEIGHTS], *[outs[n][1] for n in WEIGHTS],
            *[outs[n][2] for n in WEIGHTS], *[outs[n][3] for n in WEIGHTS])


G_AB = (('ab_w_in', 0), ('ab_w_out', 0))
G_L0 = (('xa_wq', 0), ('xa_wkv', 0), ('xa_wo', 0), ('mlp_w1', 0), ('mlp_w2', 0))
G_L1 = (('xa_wq', 1), ('xa_wkv', 1), ('xa_wo', 1), ('mlp_w1', 1), ('mlp_w2', 1))
G_CD = (('cd_w_in', 0), ('cd_w_out', 0))
GATHER_GROUPS = {'ab': G_AB, 'l0a': G_L0[:3], 'l0b': G_L0[3:], 'cd': G_CD, 'l1a': G_L1[:3], 'l1b': G_L1[3:]}
SHARD_AXIS = dict(BIG)
MEMBER_ROW_TILE = 64
FLAT_ROW_TILE = 128


def _members(group, w):
    out = []
    for n, layer in group:
        shp = w[n].shape[1:]
        if SHARD_AXIS[n] == 2:
            shp = (shp[1], shp[0])
        assert shp[1] == D, (n, shp)
        out.append((n, layer, shp, shp[0], _round_up(shp[0], MEMBER_ROW_TILE)))
    return out


def _group_rows(group, w):
    return _round_up(sum(m[4] for m in _members(group, w)), FLAT_ROW_TILE)


def _flat_shards(group, w):
    parts = []
    for n, layer, _, _, padded in _members(group, w):
        shard = w[n][layer].astype(BF)
        parts.append(_pad_rows(shard.T if SHARD_AXIS[n] == 2 else shard, padded))
    return _pad_rows(jnp.concatenate(parts, axis=0), _group_rows(group, w))


def _full_from_slots(land, group, w):
    out, off = {}, 0
    for n, layer, shp, rows, padded in _members(group, w):
        out[(n, layer)] = land[:, off:off + rows].reshape(N_DEV * rows, D)
        off += padded
    return out


def _slots_from_full(grads, group, w):
    parts = []
    for n, layer, shp, rows, padded in _members(group, w):
        blk = grads[(n, layer)].astype(BF).reshape(N_DEV, rows, D)
        parts.append(jnp.pad(blk, ((0, 0), (0, padded - rows), (0, 0))))
    send = jnp.concatenate(parts, axis=1)
    return jnp.pad(send, ((0, 0), (0, _group_rows(group, w) - send.shape[1]), (0, 0)))


_HBM = pl.BlockSpec(memory_space=pltpu.HBM)
_SEM = pl.BlockSpec(memory_space=pltpu.SEMAPHORE)
_ANY = pl.BlockSpec(memory_space=pl.ANY)


def _peer_copy(k, src, dst, send_sems, recv_sems, peer):
    return pltpu.make_async_remote_copy(src_ref=src, dst_ref=dst, send_sem=send_sems.at[k], recv_sem=recv_sems.at[k],
                                        device_id=peer, device_id_type=pl.DeviceIdType.MESH)


def exchange_start(src, name, scatter, after=()):
    shape = src.shape[-2:]
    after = list(after)

    def body(src_ref, land_ref, *rest):
        send_sems, recv_sems, token = rest[len(after)], rest[len(after) + 1], rest[-1]
        me = _me()
        for k, f in enumerate(_FLIPS):
            peer = _flip(me, f)
            piece = src_ref.at[_slot(peer)] if scatter else src_ref
            _peer_copy(k, piece, land_ref.at[_slot(me)], send_sems, recv_sems, peer).start()
        token[...] = jnp.zeros_like(token)

    land = pltpu.with_memory_space_constraint(lax.empty((N_DEV,) + shape, src.dtype), pltpu.HBM)
    return pl.pallas_call(
        body, name=name,
        out_shape=(pltpu.SemaphoreType.DMA((7,)), pltpu.SemaphoreType.DMA((7,)), pltpu.HBM(src.shape, src.dtype),
                   pltpu.HBM((N_DEV,) + shape, src.dtype), jax.ShapeDtypeStruct((8, LANE), F32)),
        in_specs=(_HBM, _HBM) + (_ANY,) * len(after), out_specs=(_SEM, _SEM, _HBM, _HBM, pl.BlockSpec(memory_space=pltpu.VMEM)),
        input_output_aliases={0: 2, 1: 3},
        compiler_params=pltpu.CompilerParams(has_side_effects=pltpu.SideEffectType.DATAFLOW_SIDE_EFFECTING),
    )(pltpu.with_memory_space_constraint(src, pltpu.HBM), land, *after)


def exchange_wait(handles, after, name, scatter):
    send_sems, recv_sems, src_thru, land_thru, _ = handles
    after = list(after) if isinstance(after, (list, tuple)) else [after]

    def body(src_ref, land_ref, send_sems, recv_sems, *rest):
        token = rest[-1]
        me = _me()
        for k, f in enumerate(_FLIPS):
            peer = _flip(me, f)
            piece = src_ref.at[_slot(peer)] if scatter else src_ref
            cp = _peer_copy(k, piece, land_ref.at[_slot(peer)], send_sems, recv_sems, peer)
            cp.wait_send()
            cp.wait_recv()
        token[...] = jnp.zeros_like(token)

    return pl.pallas_call(
        body, name=name, out_shape=(pltpu.HBM(src_thru.shape, src_thru.dtype), pltpu.HBM(land_thru.shape, land_thru.dtype),
                                    jax.ShapeDtypeStruct((8, LANE), F32)),
        in_specs=(_HBM, _HBM, _SEM, _SEM) + (_ANY,) * len(after), out_specs=(_HBM, _HBM, pl.BlockSpec(memory_space=pltpu.VMEM)),
        input_output_aliases={0: 0, 1: 1},
        compiler_params=pltpu.CompilerParams(has_side_effects=pltpu.SideEffectType.DATAFLOW_SIDE_EFFECTING),
    )(src_thru, land_thru, send_sems, recv_sems, *after)


class Exchange:
    def __init__(self, w):
        self.w = w
        self.me = _slot(_me())
        shapes = [w[n].shape for n in SMALL_SHARDED]
        gs = all_gather(_pack128([w[n] for n in SMALL_SHARDED]), "gather_small")
        per_dev = [_unpack128(gs[d], shapes) for d in range(N_DEV)]
        self.small = {n: jnp.concatenate([per_dev[d][i] for d in range(N_DEV)], axis=-1) for i, n in enumerate(SMALL_SHARDED)}
        self.small.update({n: w[n] for n in REPLICATED})
        self.first = _full_from_slots(all_gather(_flat_shards(G_AB, w), "gather_ab"), G_AB, w)
        self.gathers, self.done, self.tokens, self.reductions = {}, {}, [], {}
        self.start_gather('l0a')
        self.start_gather('l0b', after=[self.gathers['l0a'][4]])

    def take_tokens(self):
        toks, self.tokens = self.tokens, []
        return toks

    def start_gather(self, key, after=()):
        group = GATHER_GROUPS[key]
        self.gathers[key] = exchange_start(_flat_shards(group, self.w), f"gather_{key}_start", False, after=after)
        self.tokens.append(self.gathers[key][4])

    def weights(self, key, after):
        if key == 'ab':
            return self.first
        handles = self.gathers[key]
        _, land, self.done[key] = exchange_wait(handles, after, f"gather_{key}_wait", False)
        land = lax.dynamic_update_slice(land, handles[2][None], (self.me, 0, 0))
        return _full_from_slots(land, GATHER_GROUPS[key], self.w)

    def put_grads(self, key, group, grads):
        send = _slots_from_full(grads, group, self.w)
        handles = exchange_start(send, f"reduce_{key}_start", True)
        self.reductions[key] = (group, handles)
        self.tokens.append(handles[4])

    def landed(self, key, after):
        group, handles = self.reductions[key]
        send, land, _ = exchange_wait(handles, after, f"reduce_{key}_wait", True)
        mine = lax.dynamic_slice_in_dim(send, self.me, 1, axis=0)
        land = lax.dynamic_update_slice(land, mine, (self.me, 0, 0))
        offs, off = {}, 0
        for n, layer, _, _, padded in _members(group, self.w):
            offs[(n, layer)] = off
            off += padded
        return land, offs

    def put_small(self, small_grads, loss_local):
        small = SMALL_SHARDED + REPLICATED
        self.small_shapes = [small_grads[n].shape for n in small] + [(1,)]
        packed = _pack128([small_grads[n] for n in small] + [loss_local.reshape(1)])
        self.small_handles = exchange_start(packed, "gather_small_grads_start", False)
        return self.small_handles[4]

    def reduced_small(self, after):
        small = SMALL_SHARDED + REPLICATED
        src, land, _ = exchange_wait(self.small_handles, after, "gather_small_grads_wait", False)
        gs = lax.dynamic_update_slice(land, src[None], (self.me, 0, 0))
        tot = _unpack128(sum_slots(gs, "sum_small", 1024), self.small_shapes)
        out = {}
        for n, g in zip(small, tot):
            if n in SMALL_SHARDED:
                width = self.w[n].shape[-1]
                g = lax.dynamic_slice_in_dim(g, self.me * width, width, axis=g.ndim - 1)
            out[n] = g
        return out, tot[-1].reshape(())


def local_step(x, mem, target, ex):
    bsz, seq, _ = x.shape
    t = bsz * seq
    nb = t // TB
    nc = seq // CHUNK
    x0 = x.reshape(t, D)
    mem2 = mem.reshape(bsz * N_MEM, D)
    tgt = target.reshape(t, D)
    p = ex.small
    gains = p['norm_gains']
    big = {}

    def gain(layer, i):
        g = gains[layer, i].reshape(1, D)
        for tok in ex.take_tokens():
            g = g + tok[0, 0]
        return g

    consts = _ssd_consts()
    grads = {}
    saved = [dict(), dict()]

    def matmul_res(a, b, name, xin, ga, gb):
        return matmul(a, b, 'nn', name, (F32, F32, BF), epilogue=res_epilogue, extras=[xin], params=[ga, gb])

    def attn_specs():
        nq = seq // TB
        q = pl.BlockSpec((TB, D), lambda b, i: (b * nq + i, 0))
        kv = pl.BlockSpec((N_MEM, 2 * D), lambda b, i: (b, 0))
        return (bsz, nq), q, kv

    def attention_fwd(layer, xin, hin, sv, ga, gb):
        q = matmul(hin, big[('xa_wq', layer)], 'nn', f"q_{layer}", BF)
        kv = matmul(mem2, big[('xa_wkv', layer)], 'nt', f"kv_{layer}", BF)
        grid, qs, kvs = attn_specs()
        o, = fwd_call(attn_fn, f"attn_{layer}", grid, [q, kv], [qs, kvs], [_sd((t, D), BF)], [qs])
        ao, x_next, h_next = matmul_res(o, big[('xa_wo', layer)], f"ao_{layer}", xin, ga, gb)
        sv.update(q=q, kv=kv, o=o, ao=ao)
        return ao, x_next, h_next

    def mlp_fwd(layer, hin, sv, res=None):
        r, rr = matmul(hin, big[('mlp_w1', layer)], 'nt', f"mlp1_{layer}", (BF, BF), epilogue=act_epilogue)
        if res is None:
            out = (matmul(rr, big[('mlp_w2', layer)], 'nn', f"mlp2_{layer}"),)
        else:
            out = matmul_res(rr, big[('mlp_w2', layer)], f"mlp2_{layer}", *res)
        sv.update(r=r, rr=rr, mo=out[0])
        return out

    sv = saved[0]
    h0, = fwd_call(seg_in, "norm_in", (nb,), [x0, gain(0, 0)], [_rows(D), _par(D)], [_sd((t, D), BF)], [_rows(D)])
    big.update(ex.weights('ab', h0))
    xbc0 = POOL_W + SSM_INNER
    w_ab_in = big[('ab_w_in', 0)]
    w_ab_in = _pad_rows(jnp.concatenate([w_ab_in[:xbc0], _xbc_group(w_ab_in[xbc0:xbc0 + SSM_CONV_DIM], 0),
                                         w_ab_in[xbc0 + SSM_CONV_DIM:]], axis=0), AB_IN_PAD)
    conv_w, conv_b = _xbc_group(p['ssm_conv_w'][0], 1), _xbc_group(p['ssm_conv_b'], 1)
    u0 = matmul(h0, w_ab_in, 'nt', "ab_in")
    pool_outs = []
    for g in range(POOL_GROUPS):
        seqspec = pl.BlockSpec((seq, PG), lambda b, g=g: (b, g))
        po, = fwd_call(make_pool_fn(g), f"pool_{g}", (bsz,), [u0, p['pool_w'][0, g], p['pool_scale']],
                       [seqspec, pl.BlockSpec((PG, PG), lambda b: (0, 0)), pl.BlockSpec((1, PG), lambda b, g=g: (0, g))],
                       [_sd((t, PG), BF)], [pl.BlockSpec((seq, PG), lambda b: (b, 0))])
        pool_outs.append(po)
    cw = 256
    ncb = SSM_CONV_DIM // cw
    cbase = (POOL_W + SSM_INNER) // cw
    conv_in_specs = [pl.BlockSpec((seq, cw), lambda j, b: (b, cbase + j)), pl.BlockSpec((SSM_CONV, cw), lambda j, b: (0, j)),
                     pl.BlockSpec((1, cw), lambda j, b: (0, j))]
    conv_out_spec = pl.BlockSpec((seq, cw), lambda j, b: (b, j))
    xbc_act, = fwd_call(conv4_fn, "ssm_conv", (ncb, bsz), [u0, conv_w, conv_b], conv_in_specs,
                        [_sd((t, SSM_CONV_DIM))], [conv_out_spec])
    dtb = jnp.pad(p['ssm_dt_bias'], ((0, 0), (0, LANE - SSM_HEADS)))
    alog = jnp.pad(p['ssm_a_log'], ((0, 0), (0, LANE - SSM_HEADS)))
    dsk = jnp.pad(p['ssm_d'], ((0, 0), (0, LANE - SSM_HEADS)))
    yn, hs = ssd_fwd(xbc_act, u0, dtb, alog, dsk, p['ssm_norm'], consts, bsz, seq)
    mix0 = jnp.concatenate(pool_outs + [yn], axis=1)
    m0, x1, h2 = matmul_res(mix0, big[('ab_w_out', 0)], "ab_out", x0, gain(0, 1), gain(0, 2))
    big.update(ex.weights('l0a', h2))
    ex.start_gather('cd', after=[ex.done['l0a']])
    ao0, x2, h3 = attention_fwd(0, x1, h2, sv, gain(0, 3), gain(0, 4))
    big.update(ex.weights('l0b', h3))
    mo0, x3, h4 = mlp_fwd(0, h3, sv, (x2, gain(0, 5), gain(1, 0)))
    big.update(ex.weights('cd', mo0))
    ex.start_gather('l1a', after=[ex.done['cd']])
    ex.start_gather('l1b', after=[ex.gathers['l1a'][4]])

    sv1 = saved[1]
    nd = D // LANE
    w_cd_in = big[('cd_w_in', 0)].reshape(5, nd, LANE, D).transpose(1, 0, 2, 3).reshape(CD_IN, D)
    u1 = matmul(h4, w_cd_in, 'nt', "cd_in")
    cd_par = [pl.BlockSpec((CONF_K, LANE), lambda j, b: (0, j)), pl.BlockSpec((1, LANE), lambda j, b: (0, j)),
              pl.BlockSpec((SC_K, LANE), lambda j, b: (0, j))]
    cd_ins = [u1, p['conf_dw_w'][0], p['conf_dw_b'], p['sc_conv_w'][0]]
    cd_u_spec = pl.BlockSpec((seq, 5 * LANE), lambda j, b: (b, j))
    cd_in_specs = [cd_u_spec] + cd_par
    cd_out_spec = pl.BlockSpec((seq, LANE), lambda j, b: (b, j))
    vconv, mix1 = fwd_call(cd1_fn, "cd_conv", (nd, bsz), cd_ins, cd_in_specs, [_sd((t, D)), _sd((t, CD_OUT), BF)],
                           [cd_out_spec, pl.BlockSpec((seq, LANE), lambda j, b: (b, nd + j))])
    mix1, = fwd_call(seg_ln, "conf_ln", (nb,), [vconv, p['conf_ln_g'], p['conf_ln_b']], [_rows(D), _par(D), _par(D)],
                     [_sd((t, CD_OUT), BF)], [_rows(D)], into=mix1)
    m1, x4, h5 = matmul_res(mix1, big[('cd_w_out', 0)], "cd_out", x3, gain(1, 1), gain(1, 2))
    big.update(ex.weights('l1a', h5))
    ao1, x5, h6 = attention_fwd(1, x4, h5, sv1, gain(1, 3), gain(1, 4))
    big.update(ex.weights('l1b', h6))
    mo1, = mlp_fwd(1, h6, sv1)

    def loss_body(x_ref, m_ref, g_ref, t_ref, dx_ref, dm_ref, dg_ref, acc_ref):
        (y,), vjp = jax.vjp(seg_out, x_ref[...], m_ref[...], g_ref[...])
        d = y - t_ref[...]
        dx, dm, dg = vjp((d / float(D),))
        dx_ref[...] = dx
        dm_ref[...] = dm.astype(dm_ref.dtype)

        @pl.when(pl.program_id(0) == 0)
        def _():
            acc_ref[...] = jnp.zeros_like(acc_ref)
            dg_ref[...] = jnp.zeros_like(dg_ref)

        acc_ref[...] += jnp.sum(d * d, axis=0, keepdims=True)
        dg_ref[...] += dg

    dx5, dmo1, dg15, lanes = pl.pallas_call(
        loss_body, name="loss_head", grid=(nb,), in_specs=[_rows(D), _rows(D), _par(D), _rows(D)],
        out_specs=[_rows(D), _rows(D), _par(D), _par(D)], out_shape=[_sd((t, D)), _sd((t, D), BF), _sd((1, D)), _sd((1, D))],
        compiler_params=_params())(x5, mo1, gain(1, 5), tgt)
    loss = 0.5 * jnp.sum(lanes) / float(D)

    gain_grads = {(1, 5): dg15}

    def bwd_seg_res(xin, m, ga, gb, dx1, dh, name):
        return bwd_call(seg_res, name, (nb,), [xin, m, ga, gb], [_rows(D), _rows(D), _par(D), _par(D)], [dx1, dh],
                        [_rows(D), _rows(D)], [0, 1, 2, 3], [_sd((t, D)), _sd((t, D), BF), _sd((1, D)), _sd((1, D))],
                        [_rows(D), _rows(D), _par(D), _par(D)], [None, None, (0,), (0,)])

    def mlp_bwd(layer, hin, dmo, sv):
        grads_w2 = matmul(sv['rr'], dmo, 'tn', f"d_mlp_w2_{layer}", BF)
        dr, = matmul(dmo, big[('mlp_w2', layer)], 'nt', f"d_r_{layer}", (BF,), epilogue=act_bwd_epilogue, extras=[sv['r']])
        grads_w1 = matmul(dr, hin, 'tn', f"d_mlp_w1_{layer}", BF)
        dh = matmul(dr, big[('mlp_w1', layer)], 'nn', f"d_h_mlp_{layer}")
        return dh, grads_w1, grads_w2

    def attention_bwd(layer, hin, dao, sv):
        g_wo = matmul(sv['o'], dao, 'tn', f"d_xa_wo_{layer}", BF)
        do = matmul(dao, big[('xa_wo', layer)], 'nt', f"d_o_{layer}", BF)
        grid, qs, kvs = attn_specs()
        dq, dkv = bwd_call(attn_fn, f"d_attn_{layer}", grid, [sv['q'], sv['kv']], [qs, kvs], [do], [qs], [0, 1],
                           [_sd((t, D), BF), _sd((bsz * N_MEM, 2 * D))], [qs, kvs], [None, (1,)])
        g_wkv = matmul(dkv, mem2, 'tn', f"d_xa_wkv_{layer}", BF)
        g_wq = matmul(hin, dq, 'tn', f"d_xa_wq_{layer}", BF)
        dh = matmul(dq, big[('xa_wq', layer)], 'nt', f"d_h_attn_{layer}")
        return dh, g_wq, g_wkv, g_wo

    per_layer = {k: [None, None] for k in ('xa_wq', 'xa_wkv', 'xa_wo', 'mlp_w1', 'mlp_w2')}

    dh6, per_layer['mlp_w1'][1], per_layer['mlp_w2'][1] = mlp_bwd(1, h6, dmo1, sv1)
    dx4, dao1, gain_grads[(1, 3)], gain_grads[(1, 4)] = bwd_seg_res(x4, ao1, gain(1, 3), gain(1, 4), dx5, dh6, "d_res_1b")
    dh5, per_layer['xa_wq'][1], per_layer['xa_wkv'][1], per_layer['xa_wo'][1] = attention_bwd(1, h5, dao1, sv1)
    ex.put_grads('l1', G_L1, {(k, 1): v[1] for k, v in per_layer.items()})
    dx3, dm1, gain_grads[(1, 1)], gain_grads[(1, 2)] = bwd_seg_res(x3, m1, gain(1, 1), gain(1, 2), dx4, dh5, "d_res_1a")
    g_cd_out = matmul(mix1, dm1, 'tn', "d_cd_w_out", BF)
    dmix1 = matmul(dm1, big[('cd_w_out', 0)], 'nt', "d_mix1")
    dvconv, dlg, dlb = bwd_call(seg_ln, "d_conf_ln", (nb,), [vconv, p['conf_ln_g'], p['conf_ln_b']],
                                [_rows(D), _par(D), _par(D)], [dmix1], [_rows(D, 0)], [0, 1, 2],
                                [_sd((t, D)), _sd((1, D)), _sd((1, D))], [_rows(D), _par(D), _par(D)], [None, (0,), (0,)])
    grads['conf_ln_g'], grads['conf_ln_b'] = dlg, dlb
    cd_g = bwd_call(cd1_fn, "d_cd_conv", (nd, bsz), cd_ins, cd_in_specs, [dvconv, dmix1],
                    [cd_out_spec, pl.BlockSpec((seq, LANE), lambda j, b: (b, nd + j))], list(range(4)),
                    [_sd((t, CD_IN), BF), _sd((CONF_K, D)), _sd((1, D)), _sd((SC_K, D))], [cd_u_spec] + cd_par,
                    [None, (1,), (1,), (1,)])
    du1 = cd_g[0]
    grads['conf_dw_w'], grads['conf_dw_b'], grads['sc_conv_w'] = cd_g[1][None], cd_g[2], cd_g[3][None]
    g_cd_in = matmul(du1, h4, 'tn', "d_cd_w_in", BF).reshape(nd, 5, LANE, D).transpose(1, 0, 2, 3).reshape(CD_IN, D)
    ex.put_grads('cd', G_CD, {('cd_w_in', 0): g_cd_in, ('cd_w_out', 0): g_cd_out})
    dh4 = matmul(du1, w_cd_in, 'nn', "d_h_cd")

    dx2, dmo0, gain_grads[(0, 5)], gain_grads[(1, 0)] = bwd_seg_res(x2, mo0, gain(0, 5), gain(1, 0), dx3, dh4, "d_res_0c")
    dh3, per_layer['mlp_w1'][0], per_layer['mlp_w2'][0] = mlp_bwd(0, h3, dmo0, sv)
    dx1, dao0, gain_grads[(0, 3)], gain_grads[(0, 4)] = bwd_seg_res(x1, ao0, gain(0, 3), gain(0, 4), dx2, dh3, "d_res_0b")
    dh2, per_layer['xa_wq'][0], per_layer['xa_wkv'][0], per_layer['xa_wo'][0] = attention_bwd(0, h2, dao0, sv)
    ex.put_grads('l0', G_L0, {(k, 0): v[0] for k, v in per_layer.items()})
    dx0r, dm0, gain_grads[(0, 1)], gain_grads[(0, 2)] = bwd_seg_res(x0, m0, gain(0, 1), gain(0, 2), dx1, dh2, "d_res_0a")
    g_ab_out = matmul(mix0, dm0, 'tn', "d_ab_w_out", BF)
    dmix0 = matmul(dm0, big[('ab_w_out', 0)], 'nt', "d_mix0")
    dxbc_act, dz, ddt, ddtb, dalog, ddsk, dnw = ssd_bwd(xbc_act, u0, dtb, alog, dsk, p['ssm_norm'], consts, hs, dmix0, bsz, seq)
    grads['ssm_dt_bias'] = ddtb[:, :SSM_HEADS]
    grads['ssm_a_log'] = dalog[:, :SSM_HEADS]
    grads['ssm_d'] = ddsk[:, :SSM_HEADS]
    grads['ssm_norm'] = dnw
    dxr, dcw, dcb = bwd_call(conv4_fn, "d_ssm_conv", (ncb, bsz), [u0, conv_w, conv_b], conv_in_specs,
                             [dxbc_act], [conv_out_spec], [0, 1, 2],
                             [_sd((t, SSM_CONV_DIM), BF), _sd((SSM_CONV, SSM_CONV_DIM)), _sd((1, SSM_CONV_DIM))],
                             [conv_out_spec, conv_in_specs[1], conv_in_specs[2]], [None, (1,), (1,)])
    grads['ssm_conv_w'], grads['ssm_conv_b'] = _xbc_ungroup(dcw, 1)[None], _xbc_ungroup(dcb, 1)
    dpool, dpw, dps = [], [], []
    for g in range(POOL_GROUPS):
        seqspec = pl.BlockSpec((seq, PG), lambda b, g=g: (b, g))
        one = pl.BlockSpec((seq, PG), lambda b: (b, 0))
        wspec = pl.BlockSpec((PG, PG), lambda b: (0, 0))
        sspec = pl.BlockSpec((1, PG), lambda b, g=g: (0, g))
        a, bb, c = bwd_call(make_pool_fn(g), f"d_pool_{g}", (bsz,), [u0, p['pool_w'][0, g], p['pool_scale']],
                            [seqspec, wspec, sspec], [dmix0], [seqspec], [0, 1, 2],
                            [_sd((t, PG), BF), _sd((PG, PG)), _sd((1, PG))], [one, wspec, pl.BlockSpec((1, PG), lambda b: (0, 0))],
                            [None, (0,), (0,)])
        dpool.append(a)
        dpw.append(bb)
        dps.append(c)
    grads['pool_w'] = jnp.stack(dpw)[None]
    grads['pool_scale'] = jnp.concatenate(dps, axis=1)
    du0 = jnp.concatenate(dpool + [dz, dxr, ddt.astype(BF)], axis=1)
    g_ab_in = matmul(du0, h0, 'tn', "d_ab_w_in", BF)
    g_ab_in = jnp.concatenate([g_ab_in[:xbc0], _xbc_ungroup(g_ab_in[xbc0:xbc0 + SSM_CONV_DIM], 0),
                               g_ab_in[xbc0 + SSM_CONV_DIM:AB_IN]], axis=0)
    ex.put_grads('ab', G_AB, {('ab_w_in', 0): g_ab_in, ('ab_w_out', 0): g_ab_out})
    dh0 = matmul(du0, w_ab_in, 'nn', "d_h_ab", after=ex.take_tokens())
    dx, dg00 = bwd_call(seg_in_res, "d_norm_in", (nb,), [x0, gain(0, 0)], [_rows(D), _par(D)], [dx0r, dh0],
                        [_rows(D), _rows(D)], [0, 1], [_sd((t, D)), _sd((1, D))], [_rows(D), _par(D)], [None, (0,)])
    gain_grads[(0, 0)] = dg00
    grads['norm_gains'] = jnp.stack([jnp.concatenate([gain_grads[(l, i)] for i in range(6)], axis=0) for l in range(2)])
    return loss, dx, grads
```

```python
import functools
import math

import numpy as np
import jax
import jax.numpy as jnp
from jax import lax
from jax.experimental import pallas as pl
from jax.experimental.pallas import tpu as pltpu

BF = jnp.bfloat16
F32 = jnp.float32

N_DEV = 8
D = 1024
N_MEM = 256
XA_HEADS = 4
XA_DH = D // XA_HEADS
POOL_GROUPS = 4
PG = 128
POOL_W = POOL_GROUPS * PG
SSM_INNER = 1024
SSM_GROUPS = 2
SSM_GSZ = SSM_INNER // SSM_GROUPS
SSM_HEADS = 16
SSM_P = 64
SSM_N = 128
SSM_CONV = 4
SSM_CONV_DIM = SSM_INNER + 2 * SSM_GROUPS * SSM_N
SSM_XBC_G = SSM_GSZ + 2 * SSM_N
CHUNK = 128
AB_IN = POOL_W + SSM_INNER + SSM_CONV_DIM + SSM_HEADS
AB_IN_PAD = POOL_W + SSM_INNER + SSM_CONV_DIM + 128
AB_OUT = POOL_W + SSM_INNER
CONF_K = 31
SC_K = 3
CD_IN = 5 * D
CD_OUT = 2 * D
MLP_H = 4 * D
RMS_EPS = 1e-6
LN_EPS = 1e-5
ADAM_LR = 0.001
ADAM_B1 = 0.9
ADAM_B2 = 0.999
ADAM_EPS = 1e-08
ADAM_WD = 0.01
ADAM_STEP = 10
VMEM_LIMIT = 56 * 1024 * 1024
LANE = 128

NAMES = ['x', 'mem', 'norm_gains', 'xa_wq', 'xa_wkv', 'xa_wo', 'mlp_w1', 'mlp_w2', 'ab_w_in', 'pool_w', 'pool_scale',
         'ssm_conv_w', 'ssm_conv_b', 'ssm_dt_bias', 'ssm_a_log', 'ssm_d', 'ssm_norm', 'ab_w_out', 'cd_w_in', 'conf_dw_w',
         'conf_dw_b', 'conf_ln_g', 'conf_ln_b', 'sc_conv_w', 'cd_w_out', 'loss_target']
WEIGHTS = NAMES[2:25]
BIG = [('xa_wq', 1), ('xa_wkv', 2), ('xa_wo', 1), ('mlp_w1', 2), ('mlp_w2', 1), ('cd_w_in', 2), ('cd_w_out', 1),
       ('ab_w_out', 1), ('ab_w_in', 2)]
SMALL_SHARDED = ['norm_gains', 'ssm_conv_w', 'conf_dw_w', 'conf_dw_b', 'conf_ln_g', 'conf_ln_b', 'sc_conv_w']
REPLICATED = ['pool_w', 'pool_scale', 'ssm_conv_b', 'ssm_dt_bias', 'ssm_a_log', 'ssm_d', 'ssm_norm']


def _dg(a, b, ca, cb, prec=None):
    return lax.dot_general(a, b, (((ca,), (cb,)), ((), ())), precision=prec, preferred_element_type=F32)


@functools.partial(jax.custom_vjp, nondiff_argnums=(2, 3))
def bdot(a, b, ca, cb):
    return _dg(a.astype(BF), b.astype(BF), ca, cb)


def _bdot_fwd(a, b, ca, cb):
    return bdot(a, b, ca, cb), (a, b)


def _bdot_bwd(ca, cb, res, g):
    a, b = res
    g16, a16, b16 = g.astype(BF), a.astype(BF), b.astype(BF)
    da = _dg(g16, b16, 1, 1 - cb) if ca == 1 else _dg(b16, g16, 1 - cb, 1)
    db = _dg(g16, a16, 0, 1 - ca) if cb == 1 else _dg(a16, g16, 1 - ca, 0)
    return da.astype(a.dtype), db.astype(b.dtype)


bdot.defvjp(_bdot_fwd, _bdot_bwd)


def _split3(a):
    a1 = a.astype(BF)
    r1 = a - a1.astype(F32)
    a2 = r1.astype(BF)
    a3 = (r1 - a2.astype(F32)).astype(BF)
    return a1, a2, a3


def _exact_right(a, c):
    m = a.shape[0]
    if m % 16:
        return sum(_dg(p, c, 1, 0) for p in _split3(a))
    o = _dg(jnp.concatenate(_split3(a), axis=0), c, 1, 0)
    return o[:m] + o[m:2 * m] + o[2 * m:]


def _exact_left(c, a):
    n = a.shape[1]
    o = _dg(c, jnp.concatenate(_split3(a), axis=1), 1, 0)
    return o[:, :n] + o[:, n:2 * n] + o[:, 2 * n:]


@jax.custom_vjp
def cmat(a, c, ct):
    return _exact_right(a, c)


def _cmat_fwd(a, c, ct):
    return cmat(a, c, ct), (c, ct)


def _cmat_bwd(res, g):
    c, ct = res
    return _exact_right(g, ct), jnp.zeros_like(c), jnp.zeros_like(ct)


cmat.defvjp(_cmat_fwd, _cmat_bwd)


@jax.custom_vjp
def cmatl(c, ct, a):
    return _exact_left(c, a)


def _cmatl_fwd(c, ct, a):
    return cmatl(c, ct, a), (c, ct)


def _cmatl_bwd(res, g):
    c, ct = res
    return jnp.zeros_like(c), jnp.zeros_like(ct), _exact_left(ct, g)


cmatl.defvjp(_cmatl_fwd, _cmatl_bwd)


SUBLANES = 8


def _taps(x, shifts, down):
    n, c = x.shape
    pad = _round_up(max(shifts), SUBLANES)
    if pad == 0:
        return {0: x}
    zeros = jnp.zeros((pad, c), x.dtype)
    xp = jnp.concatenate([zeros, x] if down else [x, zeros], axis=0)
    rolled, out = {0: xp}, {}
    for s in shifts:
        a, b = divmod(s, SUBLANES)
        if b not in rolled:
            rolled[b] = pltpu.roll(xp, b if down else n + pad - b, 0)
        off = pad - SUBLANES * a if down else SUBLANES * a
        out[s] = rolled[b][off:off + n]
    return out


def _shift_down(x, k):
    return _taps(x, [k], True)[k]


def _shift_up(x, k):
    return _taps(x, [k], False)[k]


@functools.partial(jax.custom_vjp, nondiff_argnums=(1,))
def shift(x, k):
    return _shift_down(x, k)


def _shift_fwd(x, k):
    return _shift_down(x, k), None


def _shift_bwd(k, _, g):
    return (_shift_up(g, k),)


shift.defvjp(_shift_fwd, _shift_bwd)


@functools.partial(jax.custom_vjp, nondiff_argnums=(2,))
def cconv(u, w, width):
    taps = _taps(u, list(range(width)), True)
    acc = u * w[width - 1:width, :]
    for k in range(width - 1):
        acc = acc + taps[width - 1 - k] * w[k:k + 1, :]
    return acc


def _cconv_fwd(u, w, width):
    return cconv(u, w, width), (u, w)


def _cconv_bwd(width, res, g):
    u, w = res
    rows = lax.broadcasted_iota(jnp.int32, w.shape, 0)
    du = g * w[width - 1:width, :]
    dw = jnp.where(rows == width - 1, jnp.sum(g * u, axis=0, keepdims=True), 0.0)
    g_taps = _taps(g, list(range(width)), False)
    u_taps = _taps(u, list(range(width)), True)
    for k in range(width - 1):
        s = width - 1 - k
        du = du + g_taps[s] * w[k:k + 1, :]
        dw = dw + jnp.where(rows == k, jnp.sum(g * u_taps[s], axis=0, keepdims=True), 0.0)
    return du, dw


cconv.defvjp(_cconv_fwd, _cconv_bwd)


def _rms(x, g):
    return x * lax.rsqrt(jnp.mean(x * x, axis=-1, keepdims=True) + RMS_EPS) * g


def _params(sem=None):
    return pltpu.CompilerParams(dimension_semantics=sem, vmem_limit_bytes=VMEM_LIMIT)


def _f32(v):
    return v if v.dtype == F32 else v.astype(F32)


def _first(axes):
    ok = None
    for ax in axes:
        c = pl.program_id(ax) == 0
        ok = c if ok is None else jnp.logical_and(ok, c)
    return ok


def fwd_call(fn, name, grid, ins, in_specs, out_shapes, out_specs, into=None):
    n_in = len(ins)
    n_into = 0 if into is None else 1

    def body(*refs):
        outs = fn(*[_f32(r[...]) for r in refs[:n_in]])
        for r, o in zip(refs[n_in + n_into:], outs):
            r[...] = o.astype(r.dtype)

    extra = [] if into is None else [into]
    return pl.pallas_call(body, name=name, grid=grid, in_specs=list(in_specs) + [pl.BlockSpec(memory_space=pl.ANY)] * n_into,
                          out_specs=out_specs, out_shape=out_shapes, input_output_aliases={n_in: 0} if n_into else {},
                          compiler_params=_params())(*ins, *extra)


def bwd_call(fn, name, grid, ins, in_specs, cots, cot_specs, gidx, g_shapes, g_specs, g_acc):
    n_in, n_cot = len(ins), len(cots)

    def body(*refs):
        vals = [_f32(r[...]) for r in refs[:n_in]]

        def f_sel(*dv):
            full = list(vals)
            for i, v in zip(gidx, dv):
                full[i] = v
            return tuple(fn(*full))

        outs, vjp = jax.vjp(f_sel, *[vals[i] for i in gidx])
        cts = tuple(_f32(r[...]) for r in refs[n_in:n_in + n_cot])
        grads = vjp(cts)
        for r, g, acc in zip(refs[n_in + n_cot:], grads, g_acc):
            if acc is None:
                r[...] = g.astype(r.dtype)
            else:
                @pl.when(_first(acc))
                def _():
                    r[...] = jnp.zeros_like(r)

                r[...] += g.astype(r.dtype)

    return pl.pallas_call(body, name=name, grid=grid, in_specs=list(in_specs) + list(cot_specs), out_specs=g_specs,
                          out_shape=g_shapes, compiler_params=_params())(*ins, *cots)


def _tile(dim, pref):
    if dim <= pref:
        return dim
    best = None
    for t in range(LANE, pref + 1, LANE):
        if dim % t == 0:
            best = t
    assert best is not None, dim
    return best


MATMUL_VMEM_BUDGET = 40 * 1024 * 1024


def _matmul_tiles(m, n, k, a_bytes, b_bytes, out_bytes):
    tn = _tile(n, 1024)
    for tk_pref in (k, 2048, 1024, 512):
        tk = _tile(k, tk_pref)
        for tm_pref in (1024, 512, 256):
            tm = _tile(m, tm_pref)
            need = 2 * (tm * tk * a_bytes + tk * tn * b_bytes + tm * tn * out_bytes) + (0 if tk == k else tm * tn * 4)
            need += (tm * tk * 2 if a_bytes == 4 else 0) + (tk * tn * 2 if b_bytes == 4 else 0)
            if need <= MATMUL_VMEM_BUDGET:
                return tm, tn, tk
    raise ValueError((m, n, k))


def matmul(a, b, mode, name, out_dtype=F32, epilogue=None, extras=(), params=(), after=(), n_acc=0):
    if mode == 'nn':
        (m, k), (k2, n) = a.shape, b.shape
    elif mode == 'nt':
        (m, k), (n, k2) = a.shape, b.shape
    else:
        (k, m), (k2, n) = a.shape, b.shape
    assert k == k2, (name, a.shape, b.shape)
    n_extra = len(extras) + len(params)
    out_dtypes = out_dtype if isinstance(out_dtype, tuple) else (out_dtype,)
    per_out = sum(jnp.dtype(dt).itemsize for dt in out_dtypes) + sum(e.dtype.itemsize for e in extras)
    tm, tn, tk = _matmul_tiles(m, n, k, a.dtype.itemsize, b.dtype.itemsize, per_out)
    nk = k // tk
    ca = 0 if mode == 'tn' else 1
    cb = 1 if mode == 'nt' else 0
    a_spec = pl.BlockSpec((tk, tm), lambda i, j, kk: (kk, i)) if mode == 'tn' else pl.BlockSpec((tm, tk), lambda i, j, kk: (i, kk))
    b_spec = pl.BlockSpec((tn, tk), lambda i, j, kk: (j, kk)) if mode == 'nt' else pl.BlockSpec((tk, tn), lambda i, j, kk: (kk, j))

    def finish(o_refs, extra_refs, acc, first_row_tile):
        outs = (acc,) if epilogue is None else epilogue(acc, *[_f32(e[...]) for e in extra_refs])
        n_tile = len(o_refs) - n_acc
        for o_ref, o in zip(o_refs[:n_tile], outs[:n_tile]):
            o_ref[...] = o.astype(o_ref.dtype)
        for o_ref, o in zip(o_refs[n_tile:], outs[n_tile:]):
            o_ref[...] = jnp.where(first_row_tile, o, o_ref[...] + o)

    n_after = len(after)

    def body_whole_k(a_ref, b_ref, *refs):
        refs = refs[n_after:]
        finish(refs[n_extra:], refs[:n_extra], _dg(a_ref[...].astype(BF), b_ref[...].astype(BF), ca, cb), pl.program_id(0) == 0)

    def body_split_k(a_ref, b_ref, *refs):
        refs = refs[n_after:]
        extra_refs, o_refs, acc = refs[:n_extra], refs[n_extra:-1], refs[-1]
        kk = pl.program_id(2)
        first_row_tile = pl.program_id(0) == 0

        @pl.when(kk == 0)
        def _():
            acc[...] = jnp.zeros_like(acc)

        acc[...] += _dg(a_ref[...].astype(BF), b_ref[...].astype(BF), ca, cb)

        @pl.when(kk == nk - 1)
        def _():
            finish(o_refs, extra_refs, acc[...], first_row_tile)

    tile = pl.BlockSpec((tm, tn), lambda i, j, kk: (i, j))
    row = pl.BlockSpec((1, tn), lambda i, j, kk: (0, j))
    n_par = len(params)
    outs = pl.pallas_call(
        body_whole_k if nk == 1 else body_split_k, name=name, grid=(m // tm, n // tn, nk),
        in_specs=[a_spec, b_spec] + [pl.BlockSpec(memory_space=pl.ANY)] * n_after + [tile] * len(extras) + [row] * n_par,
        out_specs=[tile] * len(out_dtypes) + [row] * n_acc,
        out_shape=[jax.ShapeDtypeStruct((m, n), dt) for dt in out_dtypes] + [jax.ShapeDtypeStruct((1, n), F32)] * n_acc,
        scratch_shapes=[] if nk == 1 else [pltpu.VMEM((tm, tn), F32)],
        compiler_params=_params(("arbitrary",) * 3 if n_acc else ("parallel", "parallel", "arbitrary")))(a, b, *after, *extras, *params)
    return outs if isinstance(out_dtype, tuple) or n_acc else outs[0]


_FLIPS = [(0, 0, 1), (1, 0, 0), (0, 1, 0), (1, 1, 0), (1, 0, 1), (0, 1, 1), (1, 1, 1)]


def _me():
    return lax.axis_index("x"), lax.axis_index("y"), lax.axis_index("c")


def _flip(pos, f):
    return tuple(jnp.where(fi == 1, 1 - p, p) if fi else p for p, fi in zip(pos, f))


def _slot(pos):
    return 4 * pos[0] + 2 * pos[1] + pos[2]


def all_gather(v, name):
    def body(v_ref, out_ref, send_sems, recv_sems, local_sem):
        me = _me()
        sibling = _flip(me, (0, 0, 1))
        chips = [_flip(me, f) for f in ((1, 0, 0), (0, 1, 0), (1, 1, 0))]

        def copy(k, block, to, src=None):
            return pltpu.make_async_remote_copy(
                src_ref=out_ref.at[_slot(block)] if src is None else src, dst_ref=out_ref.at[_slot(block)],
                send_sem=send_sems.at[k], recv_sem=recv_sems.at[k], device_id=to, device_id_type=pl.DeviceIdType.MESH)

        mine = pltpu.make_async_copy(v_ref, out_ref.at[_slot(me)], local_sem)
        mine.start()
        first = [copy(0, me, sibling, src=v_ref)] + [copy(1 + j, me, chip, src=v_ref) for j, chip in enumerate(chips)]
        for cp in first:
            cp.start()
        passed = [copy(4 + j, chip, sibling) for j, chip in enumerate(chips)]
        for j, chip in enumerate(chips):
            copy(1 + j, chip, me).wait_recv()
            passed[j].start()
        copy(0, sibling, me).wait_recv()
        for j, chip in enumerate(chips):
            copy(4 + j, _flip(chip, (0, 0, 1)), me).wait_recv()
        for cp in first + passed:
            cp.wait_send()
        mine.wait()

    return pl.pallas_call(
        body, name=name, out_shape=jax.ShapeDtypeStruct((N_DEV,) + v.shape, v.dtype),
        in_specs=[pl.BlockSpec(memory_space=pl.ANY)], out_specs=pl.BlockSpec(memory_space=pl.ANY),
        scratch_shapes=[pltpu.SemaphoreType.DMA((7,)), pltpu.SemaphoreType.DMA((7,)), pltpu.SemaphoreType.DMA(())],
    )(v)


def sum_slots(v, name, tr=256):
    _, r, c = v.shape
    tr = _tile_rows(r, tr)

    def body(v_ref, o_ref):
        acc = v_ref[0].astype(F32)
        for s in range(1, N_DEV):
            acc = acc + v_ref[s].astype(F32)
        o_ref[...] = acc

    return pl.pallas_call(body, name=name, grid=(r // tr,), in_specs=[pl.BlockSpec((N_DEV, tr, c), lambda i: (0, i, 0))],
                          out_specs=pl.BlockSpec((tr, c), lambda i: (i, 0)), out_shape=jax.ShapeDtypeStruct((r, c), F32),
                          compiler_params=_params())(v)


def _tile_rows(r, pref):
    if r <= pref:
        return r
    best = None
    for t in range(8, pref + 1, 8):
        if r % t == 0:
            best = t
    return r if best is None else best


def _adamw_math(w, m, v, g):
    nm = ADAM_B1 * m + (1.0 - ADAM_B1) * g
    nv = ADAM_B2 * v + (1.0 - ADAM_B2) * jnp.square(g)
    m_hat = nm / (1.0 - ADAM_B1 ** ADAM_STEP)
    v_hat = nv / (1.0 - ADAM_B2 ** ADAM_STEP)
    return -ADAM_LR * (m_hat / (jnp.sqrt(v_hat) + ADAM_EPS) + ADAM_WD * w), nm, nv


def update_from_slots(lands, offs, w, m, v, transposed, name):
    layers, a, b = w.shape
    n_land = len(lands)
    if transposed:
        rb, tk = LANE, 512
        assert a % tk == 0 and b % rb == 0 and all(o % rb == 0 for o in offs), (name, w.shape, offs)
        grid = (layers, a // tk, b // rb)
        land_block = (N_DEV, rb, tk)
        tile = pl.BlockSpec((None, tk, rb), lambda l, i, j: (l, i, j))

        def land_spec(layer):
            base = offs[layer] // rb
            return pl.BlockSpec(land_block, lambda l, i, j: (0, base + jnp.where(l == layer, j, 0), jnp.where(l == layer, i, 0)))
    else:
        fits = [t for t in (256, 128, 64) if a % t == 0 and all(o % t == 0 for o in offs)]
        assert fits or all(o == 0 for o in offs), (name, w.shape, offs)
        tr = max(fits) if fits else a
        grid = (layers, a // tr)
        land_block = (N_DEV, _round_up(tr, MEMBER_ROW_TILE), b)
        tile = pl.BlockSpec((None, tr, b), lambda l, i: (l, i, 0))

        def land_spec(layer):
            base = offs[layer] // tr
            return pl.BlockSpec(land_block, lambda l, i: (0, base + jnp.where(l == layer, i, 0), 0))

    def body(*refs):
        land_refs, (w_ref, m_ref, v_ref, g_ref, d_ref, nm_ref, nv_ref, acc) = refs[:n_land], refs[n_land:]
        for layer, land in enumerate(land_refs):
            @pl.when(pl.program_id(0) == layer)
            def _(land=land):
                rows = acc.shape[0]
                s = land[0, :rows].astype(F32)
                for k in range(1, N_DEV):
                    s = s + land[k, :rows].astype(F32)
                acc[...] = s

        g = acc[...].T if transposed else acc[...]
        d, nm, nv = _adamw_math(w_ref[...], m_ref[...], v_ref[...], g)
        g_ref[...] = g
        d_ref[...] = d
        nm_ref[...] = nm
        nv_ref[...] = nv

    sh = jax.ShapeDtypeStruct(w.shape, F32)
    return pl.pallas_call(
        body, name=name, grid=grid, in_specs=[land_spec(layer) for layer in range(n_land)] + [tile] * 3, out_specs=[tile] * 4,
        out_shape=[sh] * 4, scratch_shapes=[pltpu.VMEM((rb, tk) if transposed else (tr, b), F32)],
        compiler_params=_params())(*lands, w, m, v)


def adamw_many(ws, ms, vs, gs, name):
    n = len(ws)

    def body(*refs):
        for i in range(n):
            d, nm, nv = _adamw_math(refs[i][...], refs[n + i][...], refs[2 * n + i][...], refs[3 * n + i][...])
            refs[4 * n + i][...] = d
            refs[5 * n + i][...] = nm
            refs[6 * n + i][...] = nv

    vmem = pl.BlockSpec(memory_space=pltpu.VMEM)
    shapes = [jax.ShapeDtypeStruct(a.shape, F32) for a in ws]
    res = pl.pallas_call(body, name=name, in_specs=[vmem] * (4 * n), out_specs=[vmem] * (3 * n), out_shape=shapes * 3,
                         compiler_params=_params())(*ws, *ms, *vs, *gs)
    return res[:n], res[n:2 * n], res[2 * n:]


def seg_in(x, g):
    return (_rms(x, g),)


def seg_in_res(x, g):
    return x, _rms(x, g)


def seg_res(x, m, ga, gb):
    x1 = x + _rms(m, ga)
    return x1, _rms(x1, gb)


def seg_out(x, m, ga):
    return (x + _rms(m, ga),)


def act_epilogue(r):
    t = jnp.maximum(r, 0.0)
    return r, t * t


def res_epilogue(m, x, ga, gb):
    x1, h = seg_res(x, m, ga, gb)
    return m, x1, h


def res_bwd_epilogue(dh, x, m, dx1, ga, gb):
    _, vjp = jax.vjp(seg_res, x, m, ga, gb)
    return vjp((dx1, dh))


def in_bwd_epilogue(dh, x, dx_res, g):
    _, vjp = jax.vjp(seg_in_res, x, g)
    return vjp((dx_res, dh))


def act_bwd_epilogue(drr, r):
    return (drr * (2.0 * jnp.maximum(r, 0.0)),)


def seg_ln(v, g, b):
    mu = jnp.mean(v, axis=-1, keepdims=True)
    var = jnp.mean(jnp.square(v - mu), axis=-1, keepdims=True)
    vn = (v - mu) * lax.rsqrt(var + LN_EPS) * g + b
    return (jax.nn.silu(vn),)


def make_pool_fn(group):
    window = 2 ** (group + 1)

    def pool_fn(ug, pw, scale):
        s = ug
        for lvl in range(group + 1):
            s = s + shift(s, 2 ** lvl)
        cnt = jnp.minimum(lax.broadcasted_iota(jnp.int32, ug.shape, 0) + 1, window).astype(F32)
        return (bdot(s / cnt - ug, pw, 1, 0) * scale,)

    return pool_fn


def conv4_fn(xr, w, b):
    return (jax.nn.silu(cconv(xr, w, SSM_CONV) + b),)


def cd1_fn(u, dww, dwb, scw):
    val, gate, bg, cg, hh = (u[:, k * LANE:(k + 1) * LANE] for k in range(5))
    v = val * jax.nn.sigmoid(gate)
    vc = cconv(v, dww, CONF_K) + dwb
    sc = bg * cconv(cg * hh, scw, SC_K)
    return vc, sc


def attn_fn(q, kv):
    outs = []
    for h in range(XA_HEADS):
        cols = slice(h * XA_DH, (h + 1) * XA_DH)
        s = bdot(q[:, cols], kv[:, cols], 1, 1) / math.sqrt(XA_DH)
        p = jax.nn.softmax(s, axis=-1)
        outs.append(bdot(p, kv[:, D + h * XA_DH:D + (h + 1) * XA_DH], 1, 0))
    return (jnp.concatenate(outs, axis=1),)


def ssd_chunk(xbc, z, dtraw, dtb, alog, dsk, nw, h0, h1, h2, h3, e64, e64t, ecat, ecatt, tril, trilt):
    xs, bm, cm = xbc[:, :SSM_GSZ], xbc[:, SSM_GSZ:SSM_GSZ + SSM_N], xbc[:, SSM_GSZ + SSM_N:]
    hin = (h0, h1, h2, h3)
    dt = jax.nn.softplus(dtraw + dtb)
    a = -jnp.exp(alog)
    d_a = dt * a
    cs = cmatl(tril, trilt, d_a)
    cs_cat = cmat(cs, ecat, ecatt)
    cs64, cs128 = cs_cat[:, :SSM_GSZ], cs_cat[:, SSM_GSZ:]
    dt64 = cmat(dt, e64, e64t)
    row = lax.broadcasted_iota(jnp.int32, (8, LANE), 0)
    heads = jnp.where(row == 0, dsk, jnp.where(row == 1, jnp.sum(d_a, axis=0, keepdims=True), 0.0))
    heads64 = cmat(heads, e64, e64t)
    d64, tot64 = heads64[0:1, :], heads64[1:2, :]
    xdt = xs * dt64
    cb = bdot(cm, bm, 1, 1)
    li = lax.broadcasted_iota(jnp.int32, (CHUNK, CHUNK), 0)
    si = lax.broadcasted_iota(jnp.int32, (CHUNK, CHUNK), 1)
    causal = li >= si
    lane = lax.broadcasted_iota(jnp.int32, (CHUNK, LANE), 1)
    xw = xdt * jnp.exp(tot64 - cs64)
    ecs = jnp.exp(cs64)
    etot = jnp.exp(tot64)
    ycols, hout = [], []
    for j in range(4):
        sl = slice(j * LANE, (j + 1) * LANE)
        xj = xdt[:, sl]
        ys = []
        for hh in range(2):
            r = 2 * j + hh
            col = cs128[:, r * LANE:(r + 1) * LANE]
            decay = jnp.exp(jnp.where(causal, col - col.T, -1e30))
            ys.append(bdot(cb * decay, xj, 1, 0))
        y_diag = jnp.where(lane < SSM_P, ys[0], ys[1])
        y_off = bdot(cm, hin[j], 1, 0) * ecs[:, sl]
        ycols.append(y_diag + y_off)
        hout.append(etot[:, sl] * hin[j] + bdot(bm, xw[:, sl], 0, 0))
    y = jnp.concatenate(ycols, axis=1) + d64 * xs
    y = y * jax.nn.silu(z)
    yn = y * lax.rsqrt(jnp.mean(y * y, axis=-1, keepdims=True) + RMS_EPS) * nw
    return (yn,) + tuple(hout)


def _xbc_group(a, axis):
    parts = []
    for g in range(SSM_GROUPS):
        for start, width in ((g * SSM_GSZ, SSM_GSZ), (SSM_INNER + g * SSM_N, SSM_N), (SSM_INNER + (SSM_GROUPS + g) * SSM_N, SSM_N)):
            parts.append(lax.slice_in_dim(a, start, start + width, axis=axis))
    return jnp.concatenate(parts, axis=axis)


def _xbc_ungroup(a, axis):
    xs, bs, cs = [], [], []
    for g in range(SSM_GROUPS):
        base = g * SSM_XBC_G
        xs.append(lax.slice_in_dim(a, base, base + SSM_GSZ, axis=axis))
        bs.append(lax.slice_in_dim(a, base + SSM_GSZ, base + SSM_GSZ + SSM_N, axis=axis))
        cs.append(lax.slice_in_dim(a, base + SSM_GSZ + SSM_N, base + SSM_XBC_G, axis=axis))
    return jnp.concatenate(xs + bs + cs, axis=axis)


def _ssd_consts():
    h = np.arange(LANE)[:, None]
    e64 = np.stack([(h == g * 8 + np.arange(SSM_GSZ)[None, :] // SSM_P) for g in range(SSM_GROUPS)]).astype(np.float32)
    e128 = np.stack([(h == g * 8 + np.arange(8 * LANE)[None, :] // LANE) for g in range(SSM_GROUPS)]).astype(np.float32)
    ecat = np.concatenate([e64, e128], axis=2)
    tril = np.tril(np.ones((CHUNK, CHUNK), np.float32))
    return tuple(jnp.asarray(c, dtype=BF) for c in (e64, e64.transpose(0, 2, 1), ecat, ecat.transpose(0, 2, 1), tril, tril.T))


def _ssd_specs(nc, rev):
    def ci(c):
        return nc - 1 - c if rev else c

    def row(width, col):
        return pl.BlockSpec((CHUNK, width), lambda b, c: (b * nc + ci(c), col))

    def whole(shape):
        return pl.BlockSpec(shape, lambda b, c: (0,) * len(shape))

    data = [row(SSM_CONV_DIM, 0),
            row(SSM_GSZ, 1), row(SSM_GSZ, 2), row(LANE, 24)]
    par = [whole((1, LANE))] * 3 + [whole((1, SSM_INNER))]
    cst = [whole((SSM_GROUPS, LANE, SSM_GSZ)), whole((SSM_GROUPS, SSM_GSZ, LANE)), whole((SSM_GROUPS, LANE, 12 * LANE)),
           whole((SSM_GROUPS, 12 * LANE, LANE)), whole((CHUNK, CHUNK)), whole((CHUNK, CHUNK))]
    hsave = pl.BlockSpec((None, None, SSM_GROUPS, 4, SSM_N, LANE), lambda b, c: (b, ci(c), 0, 0, 0, 0))
    return data, par, cst, hsave, row, whole


def _ssd_group_args(g, xbc, z, dtr, dtb, alog, dsk, nw):
    return (xbc[:, g * SSM_XBC_G:(g + 1) * SSM_XBC_G], z[g], dtr, dtb, alog, dsk, nw[:, g * SSM_GSZ:(g + 1) * SSM_GSZ])


def ssd_fwd(xbc_act, u, dtb, alog, dsk, nw, consts, bsz, seq):
    nc = seq // CHUNK
    data, par, cst, hsave, row, _ = _ssd_specs(nc, False)

    def body(xbc, z0, z1, dtr, dtb_r, alog_r, dsk_r, nw_r, e64, e64t, ecat, ecatt, tril, trilt, yn_ref, hs_ref, h):
        @pl.when(pl.program_id(1) == 0)
        def _():
            h[...] = jnp.zeros_like(h)

        hs_ref[...] = h[...]
        ys = []
        for g in range(SSM_GROUPS):
            args = _ssd_group_args(g, xbc[...], (z0[...], z1[...]), dtr[...], dtb_r[...], alog_r[...], dsk_r[...], nw_r[...])
            outs = ssd_chunk(*args, h[g, 0], h[g, 1], h[g, 2], h[g, 3], e64[g], e64t[g], ecat[g], ecatt[g], tril[...], trilt[...])
            ys.append(outs[0])
            for j in range(4):
                h[g, j] = outs[1 + j]
        yn_ref[...] = jnp.concatenate(ys, axis=1).astype(yn_ref.dtype)

    t = bsz * seq
    return pl.pallas_call(
        body, name="ssd_fwd", grid=(bsz, nc), in_specs=data + par + cst, out_specs=[row(SSM_INNER, 0), hsave],
        out_shape=[jax.ShapeDtypeStruct((t, SSM_INNER), BF), jax.ShapeDtypeStruct((bsz, nc, SSM_GROUPS, 4, SSM_N, LANE), F32)],
        scratch_shapes=[pltpu.VMEM((SSM_GROUPS, 4, SSM_N, LANE), F32)], compiler_params=_params(),
    )(xbc_act, u, u, u, dtb, alog, dsk, nw, *consts)


def ssd_bwd(xbc_act, u, dtb, alog, dsk, nw, consts, hs, dmix, bsz, seq):
    nc = seq // CHUNK
    data, par, cst, hsave, row, whole = _ssd_specs(nc, True)
    t = bsz * seq
    pcol = POOL_W // SSM_GSZ

    def body(xbc, z0, z1, dtr, dtb_r, alog_r, dsk_r, nw_r, e64, e64t, ecat, ecatt, tril, trilt, hs_ref, dy0, dy1,
             dxbc, dz, ddt, ddtb, dalog, ddsk, dnw, dh):
        @pl.when(pl.program_id(1) == 0)
        def _():
            dh[...] = jnp.zeros_like(dh)

        per_group = []
        for g, dyn in enumerate((dy0, dy1)):
            cst_vals = (e64[g], e64t[g], ecat[g], ecatt[g], tril[...], trilt[...])
            prim = _ssd_group_args(g, xbc[...], (z0[...], z1[...]), dtr[...], dtb_r[...], alog_r[...], dsk_r[...], nw_r[...])
            prim = prim + (hs_ref[g, 0], hs_ref[g, 1], hs_ref[g, 2], hs_ref[g, 3])
            _, vjp = jax.vjp(lambda *args, c=cst_vals: ssd_chunk(*args, *c), *prim)
            gr = vjp((dyn[...].astype(F32), dh[g, 0], dh[g, 1], dh[g, 2], dh[g, 3]))
            for j in range(4):
                dh[g, j] = gr[7 + j]
            per_group.append(gr)
        g0, g1 = per_group
        dxbc[...] = jnp.concatenate([g0[0], g1[0]], axis=1)
        dz[...] = jnp.concatenate([g0[1], g1[1]], axis=1).astype(dz.dtype)
        ddt[...] = g0[2] + g1[2]

        @pl.when(_first((0, 1)))
        def _():
            for r in (ddtb, dalog, ddsk, dnw):
                r[...] = jnp.zeros_like(r)

        ddtb[...] += g0[3] + g1[3]
        dalog[...] += g0[4] + g1[4]
        ddsk[...] += g0[5] + g1[5]
        dnw[...] += jnp.concatenate([g0[6], g1[6]], axis=1)

    out_specs = [row(SSM_CONV_DIM, 0), row(SSM_INNER, 0), row(LANE, 0), whole((1, LANE)), whole((1, LANE)), whole((1, LANE)),
                 whole((1, SSM_INNER))]
    lane = jax.ShapeDtypeStruct((1, LANE), F32)
    out_shape = [jax.ShapeDtypeStruct((t, SSM_CONV_DIM), F32), jax.ShapeDtypeStruct((t, SSM_INNER), BF),
                 jax.ShapeDtypeStruct((t, LANE), F32), lane, lane, lane, jax.ShapeDtypeStruct((1, SSM_INNER), F32)]
    return pl.pallas_call(
        body, name="ssd_bwd", grid=(bsz, nc), in_specs=data + par + cst + [hsave, row(SSM_GSZ, pcol), row(SSM_GSZ, pcol + 1)],
        out_specs=out_specs, out_shape=out_shape, scratch_shapes=[pltpu.VMEM((SSM_GROUPS, 4, SSM_N, LANE), F32)],
        compiler_params=_params(),
    )(xbc_act, u, u, u, dtb, alog, dsk, nw, *consts, hs, dmix, dmix)


TB = 512


def _rows(d, col=0):
    return pl.BlockSpec((TB, d), lambda i: (i, col))


def _par(d):
    return pl.BlockSpec((1, d), lambda i: (0, 0))


def _sd(shape, dtype=F32):
    return jax.ShapeDtypeStruct(shape, dtype)


def _round_up(n, m):
    return -(-n // m) * m


def _pad_rows(a, rows):
    return jnp.pad(a, ((0, rows - a.shape[0]), (0, 0)))


def _pack128(arrs):
    flat = jnp.concatenate([a.reshape(-1) for a in arrs])
    n = flat.shape[0]
    rows = -(-n // (8 * LANE)) * 8
    return jnp.pad(flat, (0, rows * LANE - n)).reshape(rows, LANE)


def _unpack128(packed, shapes):
    flat = packed.reshape(-1)
    out, off = [], 0
    for s in shapes:
        n = int(np.prod(s))
        out.append(flat[off:off + n].reshape(s))
        off += n
    return out


def kernel(x, mem, norm_gains, xa_wq, xa_wkv, xa_wo, mlp_w1, mlp_w2, ab_w_in, pool_w, pool_scale, ssm_conv_w, ssm_conv_b, ssm_dt_bias, ssm_a_log, ssm_d, ssm_norm, ab_w_out, cd_w_in, conf_dw_w, conf_dw_b, conf_ln_g, conf_ln_b, sc_conv_w, cd_w_out, loss_target, m_norm_gains, m_xa_wq, m_xa_wkv, m_xa_wo, m_mlp_w1, m_mlp_w2, m_ab_w_in, m_pool_w, m_pool_scale, m_ssm_conv_w, m_ssm_conv_b, m_ssm_dt_bias, m_ssm_a_log, m_ssm_d, m_ssm_norm, m_ab_w_out, m_cd_w_in, m_conf_dw_w, m_conf_dw_b, m_conf_ln_g, m_conf_ln_b, m_sc_conv_w, m_cd_w_out, v_norm_gains, v_xa_wq, v_xa_wkv, v_xa_wo, v_mlp_w1, v_mlp_w2, v_ab_w_in, v_pool_w, v_pool_scale, v_ssm_conv_w, v_ssm_conv_b, v_ssm_dt_bias, v_ssm_a_log, v_ssm_d, v_ssm_norm, v_ab_w_out, v_cd_w_in, v_conf_dw_w, v_conf_dw_b, v_conf_ln_g, v_conf_ln_b, v_sc_conv_w, v_cd_w_out):
    args = locals()
    w = {n: args[n] for n in WEIGHTS}
    mom_m = {n: args["m_" + n] for n in WEIGHTS}
    mom_v = {n: args["v_" + n] for n in WEIGHTS}
    ex = Exchange(w)
    loss_local, grad_x, small_grads = local_step(x, mem, loss_target, ex)
    outs = {}

    started = ex.put_small(small_grads, loss_local)
    landed = {key: ex.landed(key, started) for key in ('l1', 'cd', 'l0')}
    late = []
    for n, keys in (('mlp_w1', ('l0', 'l1')), ('mlp_w2', ('l0', 'l1')), ('xa_wkv', ('l0', 'l1')), ('xa_wq', ('l0', 'l1')),
                    ('xa_wo', ('l0', 'l1')), ('cd_w_in', ('cd',)), ('cd_w_out', ('cd',))):
        lands = [landed[key][0] for key in keys]
        offs = [landed[key][1][(n, layer)] for layer, key in enumerate(keys)]
        outs[n] = update_from_slots(lands, offs, w[n], mom_m[n], mom_v[n], SHARD_AXIS[n] == 2, "update_" + n)
        late.append(outs[n][1])
    g_own, loss = ex.reduced_small(late)
    land_ab, offs_ab = ex.landed('ab', late)
    outs['ab_w_out'] = update_from_slots([land_ab], [offs_ab[('ab_w_out', 0)]], w['ab_w_out'], mom_m['ab_w_out'],
                                         mom_v['ab_w_out'], False, "update_ab_w_out")
    res = update_from_slots([land_ab], [offs_ab[('ab_w_in', 0)]], jnp.swapaxes(w['ab_w_in'], 1, 2), jnp.swapaxes(mom_m['ab_w_in'], 1, 2),
                            jnp.swapaxes(mom_v['ab_w_in'], 1, 2), False, "update_ab_w_in")
    outs['ab_w_in'] = tuple(jnp.swapaxes(r, 1, 2) for r in res)
    small = SMALL_SHARDED + REPLICATED
    upd = adamw_many([w[n] for n in small], [mom_m[n] for n in small], [mom_v[n] for n in small], [g_own[n] for n in small],
                     "adamw_small")
    for i, n in enumerate(small):
        outs[n] = (g_own[n], upd[0][i], upd[1][i], upd[2][i])
    return (loss, grad_x.reshape(x.shape), *[outs[n][0] for n in WEIGHTS], *[outs[n][1] for n in WEIGHTS],
            *[outs[n][2] for n in WEIGHTS], *[outs[n][3] for n in WEIGHTS])


G_AB = (('ab_w_in', 0), ('ab_w_out', 0))
G_L0 = (('xa_wq', 0), ('xa_wkv', 0), ('xa_wo', 0), ('mlp_w1', 0), ('mlp_w2', 0))
G_L1 = (('xa_wq', 1), ('xa_wkv', 1), ('xa_wo', 1), ('mlp_w1', 1), ('mlp_w2', 1))
G_CD = (('cd_w_in', 0), ('cd_w_out', 0))
GATHER_GROUPS = {'ab': G_AB, 'l0a': G_L0[:3], 'l0b': G_L0[3:], 'cd': G_CD, 'l1a': G_L1[:3], 'l1b': G_L1[3:]}
SHARD_AXIS = dict(BIG)
MEMBER_ROW_TILE = 64
FLAT_ROW_TILE = 128


def _members(group, w):
    out = []
    for n, layer in group:
        shp = w[n].shape[1:]
        if SHARD_AXIS[n] == 2:
            shp = (shp[1], shp[0])
        assert shp[1] == D, (n, shp)
        out.append((n, layer, shp, shp[0], _round_up(shp[0], MEMBER_ROW_TILE)))
    return out


def _group_rows(group, w):
    return _round_up(sum(m[4] for m in _members(group, w)), FLAT_ROW_TILE)


def _flat_shards(group, w):
    parts = []
    for n, layer, _, _, padded in _members(group, w):
        shard = w[n][layer].astype(BF)
        parts.append(_pad_rows(shard.T if SHARD_AXIS[n] == 2 else shard, padded))
    return _pad_rows(jnp.concatenate(parts, axis=0), _group_rows(group, w))


def _full_from_slots(land, group, w):
    out, off = {}, 0
    for n, layer, shp, rows, padded in _members(group, w):
        out[(n, layer)] = land[:, off:off + rows].reshape(N_DEV * rows, D)
        off += padded
    return out


def _slots_from_full(grads, group, w):
    parts = []
    for n, layer, shp, rows, padded in _members(group, w):
        blk = grads[(n, layer)].astype(BF).reshape(N_DEV, rows, D)
        parts.append(jnp.pad(blk, ((0, 0), (0, padded - rows), (0, 0))))
    send = jnp.concatenate(parts, axis=1)
    return jnp.pad(send, ((0, 0), (0, _group_rows(group, w) - send.shape[1]), (0, 0)))


_HBM = pl.BlockSpec(memory_space=pltpu.HBM)
_SEM = pl.BlockSpec(memory_space=pltpu.SEMAPHORE)
_ANY = pl.BlockSpec(memory_space=pl.ANY)


def _peer_copy(k, src, dst, send_sems, recv_sems, peer):
    return pltpu.make_async_remote_copy(src_ref=src, dst_ref=dst, send_sem=send_sems.at[k], recv_sem=recv_sems.at[k],
                                        device_id=peer, device_id_type=pl.DeviceIdType.MESH)


def exchange_start(src, name, scatter, after=()):
    shape = src.shape[-2:]
    after = list(after)

    def body(src_ref, land_ref, *rest):
        send_sems, recv_sems, token = rest[len(after)], rest[len(after) + 1], rest[-1]
        me = _me()
        for k, f in enumerate(_FLIPS):
            peer = _flip(me, f)
            piece = src_ref.at[_slot(peer)] if scatter else src_ref
            _peer_copy(k, piece, land_ref.at[_slot(me)], send_sems, recv_sems, peer).start()
        token[...] = jnp.zeros_like(token)

    land = pltpu.with_memory_space_constraint(lax.empty((N_DEV,) + shape, src.dtype), pltpu.HBM)
    return pl.pallas_call(
        body, name=name,
        out_shape=(pltpu.SemaphoreType.DMA((7,)), pltpu.SemaphoreType.DMA((7,)), pltpu.HBM(src.shape, src.dtype),
                   pltpu.HBM((N_DEV,) + shape, src.dtype), jax.ShapeDtypeStruct((8, LANE), F32)),
        in_specs=(_HBM, _HBM) + (_ANY,) * len(after), out_specs=(_SEM, _SEM, _HBM, _HBM, pl.BlockSpec(memory_space=pltpu.VMEM)),
        input_output_aliases={0: 2, 1: 3},
        compiler_params=pltpu.CompilerParams(has_side_effects=pltpu.SideEffectType.DATAFLOW_SIDE_EFFECTING),
    )(pltpu.with_memory_space_constraint(src, pltpu.HBM), land, *after)


def exchange_wait(handles, after, name, scatter):
    send_sems, recv_sems, src_thru, land_thru, _ = handles
    after = list(after) if isinstance(after, (list, tuple)) else [after]

    def body(src_ref, land_ref, send_sems, recv_sems, *rest):
        token = rest[-1]
        me = _me()
        for k, f in enumerate(_FLIPS):
            peer = _flip(me, f)
            piece = src_ref.at[_slot(peer)] if scatter else src_ref
            cp = _peer_copy(k, piece, land_ref.at[_slot(peer)], send_sems, recv_sems, peer)
            cp.wait_send()
            cp.wait_recv()
        token[...] = jnp.zeros_like(token)

    return pl.pallas_call(
        body, name=name, out_shape=(pltpu.HBM(src_thru.shape, src_thru.dtype), pltpu.HBM(land_thru.shape, land_thru.dtype),
                                    jax.ShapeDtypeStruct((8, LANE), F32)),
        in_specs=(_HBM, _HBM, _SEM, _SEM) + (_ANY,) * len(after), out_specs=(_HBM, _HBM, pl.BlockSpec(memory_space=pltpu.VMEM)),
        input_output_aliases={0: 0, 1: 1},
        compiler_params=pltpu.CompilerParams(has_side_effects=pltpu.SideEffectType.DATAFLOW_SIDE_EFFECTING),
    )(src_thru, land_thru, send_sems, recv_sems, *after)


class Exchange:
    def __init__(self, w):
        self.w = w
        self.me = _slot(_me())
        shapes = [w[n].shape for n in SMALL_SHARDED]
        gs = all_gather(_pack128([w[n] for n in SMALL_SHARDED]), "gather_small")
        per_dev = [_unpack128(gs[d], shapes) for d in range(N_DEV)]
        self.small = {n: jnp.concatenate([per_dev[d][i] for d in range(N_DEV)], axis=-1) for i, n in enumerate(SMALL_SHARDED)}
        self.small.update({n: w[n] for n in REPLICATED})
        self.first = _full_from_slots(all_gather(_flat_shards(G_AB, w), "gather_ab"), G_AB, w)
        self.gathers, self.done, self.tokens, self.reductions = {}, {}, [], {}
        self.start_gather('l0a')
        self.start_gather('l0b', after=[self.gathers['l0a'][4]])

    def take_tokens(self):
        toks, self.tokens = self.tokens, []
        return toks

    def start_gather(self, key, after=()):
        group = GATHER_GROUPS[key]
        self.gathers[key] = exchange_start(_flat_shards(group, self.w), f"gather_{key}_start", False, after=after)
        self.tokens.append(self.gathers[key][4])

    def weights(self, key, after):
        if key == 'ab':
            return self.first
        handles = self.gathers[key]
        _, land, self.done[key] = exchange_wait(handles, after, f"gather_{key}_wait", False)
        land = lax.dynamic_update_slice(land, handles[2][None], (self.me, 0, 0))
        return _full_from_slots(land, GATHER_GROUPS[key], self.w)

    def put_grads(self, key, group, grads):
        send = _slots_from_full(grads, group, self.w)
        handles = exchange_start(send, f"reduce_{key}_start", True)
        self.reductions[key] = (group, handles)
        self.tokens.append(handles[4])

    def landed(self, key, after):
        group, handles = self.reductions[key]
        send, land, _ = exchange_wait(handles, after, f"reduce_{key}_wait", True)
        mine = lax.dynamic_slice_in_dim(send, self.me, 1, axis=0)
        land = lax.dynamic_update_slice(land, mine, (self.me, 0, 0))
        offs, off = {}, 0
        for n, layer, _, _, padded in _members(group, self.w):
            offs[(n, layer)] = off
            off += padded
        return land, offs

    def put_small(self, small_grads, loss_local):
        small = SMALL_SHARDED + REPLICATED
        self.small_shapes = [small_grads[n].shape for n in small] + [(1,)]
        packed = _pack128([small_grads[n] for n in small] + [loss_local.reshape(1)])
        self.small_handles = exchange_start(packed, "gather_small_grads_start", False)
        return self.small_handles[4]

    def reduced_small(self, after):
        small = SMALL_SHARDED + REPLICATED
        src, land, _ = exchange_wait(self.small_handles, after, "gather_small_grads_wait", False)
        gs = lax.dynamic_update_slice(land, src[None], (self.me, 0, 0))
        tot = _unpack128(sum_slots(gs, "sum_small", 1024), self.small_shapes)
        out = {}
        for n, g in zip(small, tot):
            if n in SMALL_SHARDED:
                width = self.w[n].shape[-1]
                g = lax.dynamic_slice_in_dim(g, self.me * width, width, axis=g.ndim - 1)
            out[n] = g
        return out, tot[-1].reshape(())


def local_step(x, mem, target, ex):
    bsz, seq, _ = x.shape
    t = bsz * seq
    nb = t // TB
    nc = seq // CHUNK
    x0 = x.reshape(t, D)
    mem2 = mem.reshape(bsz * N_MEM, D)
    tgt = target.reshape(t, D)
    p = ex.small
    gains = p['norm_gains']
    big = {}

    def gain(layer, i):
        g = gains[layer, i].reshape(1, D)
        for tok in ex.take_tokens():
            g = g + tok[0, 0]
        return g

    consts = _ssd_consts()
    grads = {}
    saved = [dict(), dict()]

    def matmul_res(a, b, name, xin, ga, gb):
        return matmul(a, b, 'nn', name, (F32, F32, BF), epilogue=res_epilogue, extras=[xin], params=[ga, gb])

    def attn_specs():
        nq = seq // TB
        q = pl.BlockSpec((TB, D), lambda b, i: (b * nq + i, 0))
        kv = pl.BlockSpec((N_MEM, 2 * D), lambda b, i: (b, 0))
        return (bsz, nq), q, kv

    def attention_fwd(layer, xin, hin, sv, ga, gb):
        q = matmul(hin, big[('xa_wq', layer)], 'nn', f"q_{layer}", BF)
        kv = matmul(mem2, big[('xa_wkv', layer)], 'nt', f"kv_{layer}", BF)
        grid, qs, kvs = attn_specs()
        o, = fwd_call(attn_fn, f"attn_{layer}", grid, [q, kv], [qs, kvs], [_sd((t, D), BF)], [qs])
        ao, x_next, h_next = matmul_res(o, big[('xa_wo', layer)], f"ao_{layer}", xin, ga, gb)
        sv.update(q=q, kv=kv, o=o, ao=ao)
        return ao, x_next, h_next

    def mlp_fwd(layer, hin, sv, res=None):
        r, rr = matmul(hin, big[('mlp_w1', layer)], 'nt', f"mlp1_{layer}", (BF, BF), epilogue=act_epilogue)
        if res is None:
            out = (matmul(rr, big[('mlp_w2', layer)], 'nn', f"mlp2_{layer}"),)
        else:
            out = matmul_res(rr, big[('mlp_w2', layer)], f"mlp2_{layer}", *res)
        sv.update(r=r, rr=rr, mo=out[0])
        return out

    sv = saved[0]
    h0, = fwd_call(seg_in, "norm_in", (nb,), [x0, gain(0, 0)], [_rows(D), _par(D)], [_sd((t, D), BF)], [_rows(D)])
    big.update(ex.weights('ab', h0))
    xbc0 = POOL_W + SSM_INNER
    w_ab_in = big[('ab_w_in', 0)]
    w_ab_in = _pad_rows(jnp.concatenate([w_ab_in[:xbc0], _xbc_group(w_ab_in[xbc0:xbc0 + SSM_CONV_DIM], 0),
                                         w_ab_in[xbc0 + SSM_CONV_DIM:]], axis=0), AB_IN_PAD)
    conv_w, conv_b = _xbc_group(p['ssm_conv_w'][0], 1), _xbc_group(p['ssm_conv_b'], 1)
    u0 = matmul(h0, w_ab_in, 'nt', "ab_in")
    pool_outs = []
    for g in range(POOL_GROUPS):
        seqspec = pl.BlockSpec((seq, PG), lambda b, g=g: (b, g))
        po, = fwd_call(make_pool_fn(g), f"pool_{g}", (bsz,), [u0, p['pool_w'][0, g], p['pool_scale']],
                       [seqspec, pl.BlockSpec((PG, PG), lambda b: (0, 0)), pl.BlockSpec((1, PG), lambda b, g=g: (0, g))],
                       [_sd((t, PG), BF)], [pl.BlockSpec((seq, PG), lambda b: (b, 0))])
        pool_outs.append(po)
    cw = 256
    ncb = SSM_CONV_DIM // cw
    cbase = (POOL_W + SSM_INNER) // cw
    conv_in_specs = [pl.BlockSpec((seq, cw), lambda j, b: (b, cbase + j)), pl.BlockSpec((SSM_CONV, cw), lambda j, b: (0, j)),
                     pl.BlockSpec((1, cw), lambda j, b: (0, j))]
    conv_out_spec = pl.BlockSpec((seq, cw), lambda j, b: (b, j))
    xbc_act, = fwd_call(conv4_fn, "ssm_conv", (ncb, bsz), [u0, conv_w, conv_b], conv_in_specs,
                        [_sd((t, SSM_CONV_DIM))], [conv_out_spec])
    dtb = jnp.pad(p['ssm_dt_bias'], ((0, 0), (0, LANE - SSM_HEADS)))
    alog = jnp.pad(p['ssm_a_log'], ((0, 0), (0, LANE - SSM_HEADS)))
    dsk = jnp.pad(p['ssm_d'], ((0, 0), (0, LANE - SSM_HEADS)))
    yn, hs = ssd_fwd(xbc_act, u0, dtb, alog, dsk, p['ssm_norm'], consts, bsz, seq)
    mix0 = jnp.concatenate(pool_outs + [yn], axis=1)
    m0, x1, h2 = matmul_res(mix0, big[('ab_w_out', 0)], "ab_out", x0, gain(0, 1), gain(0, 2))
    big.update(ex.weights('l0a', h2))
    ex.start_gather('cd', after=[ex.done['l0a']])
    ao0, x2, h3 = attention_fwd(0, x1, h2, sv, gain(0, 3), gain(0, 4))
    big.update(ex.weights('l0b', h3))
    mo0, x3, h4 = mlp_fwd(0, h3, sv, (x2, gain(0, 5), gain(1, 0)))
    big.update(ex.weights('cd', mo0))
    ex.start_gather('l1a', after=[ex.done['cd']])
    ex.start_gather('l1b', after=[ex.gathers['l1a'][4]])

    sv1 = saved[1]
    nd = D // LANE
    w_cd_in = big[('cd_w_in', 0)].reshape(5, nd, LANE, D).transpose(1, 0, 2, 3).reshape(CD_IN, D)
    u1 = matmul(h4, w_cd_in, 'nt', "cd_in")
    cd_par = [pl.BlockSpec((CONF_K, LANE), lambda j, b: (0, j)), pl.BlockSpec((1, LANE), lambda j, b: (0, j)),
              pl.BlockSpec((SC_K, LANE), lambda j, b: (0, j))]
    cd_ins = [u1, p['conf_dw_w'][0], p['conf_dw_b'], p['sc_conv_w'][0]]
    cd_u_spec = pl.BlockSpec((seq, 5 * LANE), lambda j, b: (b, j))
    cd_in_specs = [cd_u_spec] + cd_par
    cd_out_spec = pl.BlockSpec((seq, LANE), lambda j, b: (b, j))
    vconv, mix1 = fwd_call(cd1_fn, "cd_conv", (nd, bsz), cd_ins, cd_in_specs, [_sd((t, D)), _sd((t, CD_OUT), BF)],
                           [cd_out_spec, pl.BlockSpec((seq, LANE), lambda j, b: (b, nd + j))])
    mix1, = fwd_call(seg_ln, "conf_ln", (nb,), [vconv, p['conf_ln_g'], p['conf_ln_b']], [_rows(D), _par(D), _par(D)],
                     [_sd((t, CD_OUT), BF)], [_rows(D)], into=mix1)
    m1, x4, h5 = matmul_res(mix1, big[('cd_w_out', 0)], "cd_out", x3, gain(1, 1), gain(1, 2))
    big.update(ex.weights('l1a', h5))
    ao1, x5, h6 = attention_fwd(1, x4, h5, sv1, gain(1, 3), gain(1, 4))
    big.update(ex.weights('l1b', h6))
    mo1, = mlp_fwd(1, h6, sv1)

    def loss_body(x_ref, m_ref, g_ref, t_ref, dx_ref, dm_ref, dg_ref, acc_ref):
        (y,), vjp = jax.vjp(seg_out, x_ref[...], m_ref[...], g_ref[...])
        d = y - t_ref[...]
        dx, dm, dg = vjp((d / float(D),))
        dx_ref[...] = dx
        dm_ref[...] = dm.astype(dm_ref.dtype)

        @pl.when(pl.program_id(0) == 0)
        def _():
            acc_ref[...] = jnp.zeros_like(acc_ref)
            dg_ref[...] = jnp.zeros_like(dg_ref)

        acc_ref[...] += jnp.sum(d * d, axis=0, keepdims=True)
        dg_ref[...] += dg

    dx5, dmo1, dg15, lanes = pl.pallas_call(
        loss_body, name="loss_head", grid=(nb,), in_specs=[_rows(D), _rows(D), _par(D), _rows(D)],
        out_specs=[_rows(D), _rows(D), _par(D), _par(D)], out_shape=[_sd((t, D)), _sd((t, D), BF), _sd((1, D)), _sd((1, D))],
        compiler_params=_params())(x5, mo1, gain(1, 5), tgt)
    loss = 0.5 * jnp.sum(lanes) / float(D)

    gain_grads = {(1, 5): dg15}

    def matmul_res_bwd(a, b, mode, name, xin, m, ga, gb, dx1):
        return list(matmul(a, b, mode, name, (F32, BF), epilogue=res_bwd_epilogue, extras=[xin, m, dx1], params=[ga, gb], n_acc=2))

    def mlp_bwd(layer, hin, dmo, sv, res):
        grads_w2 = matmul(sv['rr'], dmo, 'tn', f"d_mlp_w2_{layer}", BF)
        dr, = matmul(dmo, big[('mlp_w2', layer)], 'nt', f"d_r_{layer}", (BF,), epilogue=act_bwd_epilogue, extras=[sv['r']])
        grads_w1 = matmul(dr, hin, 'tn', f"d_mlp_w1_{layer}", BF)
        return matmul_res_bwd(dr, big[('mlp_w1', layer)], 'nn', f"d_h_mlp_{layer}", *res) + [grads_w1, grads_w2]

    def attention_bwd(layer, hin, dao, sv, res):
        g_wo = matmul(sv['o'], dao, 'tn', f"d_xa_wo_{layer}", BF)
        do = matmul(dao, big[('xa_wo', layer)], 'nt', f"d_o_{layer}", BF)
        grid, qs, kvs = attn_specs()
        dq, dkv = bwd_call(attn_fn, f"d_attn_{layer}", grid, [sv['q'], sv['kv']], [qs, kvs], [do], [qs], [0, 1],
                           [_sd((t, D), BF), _sd((bsz * N_MEM, 2 * D))], [qs, kvs], [None, (1,)])
        g_wkv = matmul(dkv, mem2, 'tn', f"d_xa_wkv_{layer}", BF)
        g_wq = matmul(hin, dq, 'tn', f"d_xa_wq_{layer}", BF)
        return matmul_res_bwd(dq, big[('xa_wq', layer)], 'nt', f"d_h_attn_{layer}", *res) + [g_wq, g_wkv, g_wo]

    per_layer = {k: [None, None] for k in ('xa_wq', 'xa_wkv', 'xa_wo', 'mlp_w1', 'mlp_w2')}

    (dx4, dao1, gain_grads[(1, 3)], gain_grads[(1, 4)], per_layer['mlp_w1'][1],
     per_layer['mlp_w2'][1]) = mlp_bwd(1, h6, dmo1, sv1, (x4, ao1, gain(1, 3), gain(1, 4), dx5))
    (dx3, dm1, gain_grads[(1, 1)], gain_grads[(1, 2)], per_layer['xa_wq'][1], per_layer['xa_wkv'][1],
     per_layer['xa_wo'][1]) = attention_bwd(1, h5, dao1, sv1, (x3, m1, gain(1, 1), gain(1, 2), dx4))
    ex.put_grads('l1', G_L1, {(k, 1): v[1] for k, v in per_layer.items()})
    g_cd_out = matmul(mix1, dm1, 'tn', "d_cd_w_out", BF)
    dmix1 = matmul(dm1, big[('cd_w_out', 0)], 'nt', "d_mix1")
    dvconv, dlg, dlb = bwd_call(seg_ln, "d_conf_ln", (nb,), [vconv, p['conf_ln_g'], p['conf_ln_b']],
                                [_rows(D), _par(D), _par(D)], [dmix1], [_rows(D, 0)], [0, 1, 2],
                                [_sd((t, D)), _sd((1, D)), _sd((1, D))], [_rows(D), _par(D), _par(D)], [None, (0,), (0,)])
    grads['conf_ln_g'], grads['conf_ln_b'] = dlg, dlb
    cd_g = bwd_call(cd1_fn, "d_cd_conv", (nd, bsz), cd_ins, cd_in_specs, [dvconv, dmix1],
                    [cd_out_spec, pl.BlockSpec((seq, LANE), lambda j, b: (b, nd + j))], list(range(4)),
                    [_sd((t, CD_IN), BF), _sd((CONF_K, D)), _sd((1, D)), _sd((SC_K, D))], [cd_u_spec] + cd_par,
                    [None, (1,), (1,), (1,)])
    du1 = cd_g[0]
    grads['conf_dw_w'], grads['conf_dw_b'], grads['sc_conv_w'] = cd_g[1][None], cd_g[2], cd_g[3][None]
    g_cd_in = matmul(du1, h4, 'tn', "d_cd_w_in", BF).reshape(nd, 5, LANE, D).transpose(1, 0, 2, 3).reshape(CD_IN, D)
    ex.put_grads('cd', G_CD, {('cd_w_in', 0): g_cd_in, ('cd_w_out', 0): g_cd_out})
    dx2, dmo0, gain_grads[(0, 5)], gain_grads[(1, 0)] = matmul_res_bwd(du1, w_cd_in, 'nn', "d_h_cd", x2, mo0, gain(0, 5),
                                                                       gain(1, 0), dx3)
    (dx1, dao0, gain_grads[(0, 3)], gain_grads[(0, 4)], per_layer['mlp_w1'][0],
     per_layer['mlp_w2'][0]) = mlp_bwd(0, h3, dmo0, sv, (x1, ao0, gain(0, 3), gain(0, 4), dx2))
    (dx0r, dm0, gain_grads[(0, 1)], gain_grads[(0, 2)], per_layer['xa_wq'][0], per_layer['xa_wkv'][0],
     per_layer['xa_wo'][0]) = attention_bwd(0, h2, dao0, sv, (x0, m0, gain(0, 1), gain(0, 2), dx1))
    ex.put_grads('l0', G_L0, {(k, 0): v[0] for k, v in per_layer.items()})
    g_ab_out = matmul(mix0, dm0, 'tn', "d_ab_w_out", BF)
    dmix0 = matmul(dm0, big[('ab_w_out', 0)], 'nt', "d_mix0")
    dxbc_act, dz, ddt, ddtb, dalog, ddsk, dnw = ssd_bwd(xbc_act, u0, dtb, alog, dsk, p['ssm_norm'], consts, hs, dmix0, bsz, seq)
    grads['ssm_dt_bias'] = ddtb[:, :SSM_HEADS]
    grads['ssm_a_log'] = dalog[:, :SSM_HEADS]
    grads['ssm_d'] = ddsk[:, :SSM_HEADS]
    grads['ssm_norm'] = dnw
    dxr, dcw, dcb = bwd_call(conv4_fn, "d_ssm_conv", (ncb, bsz), [u0, conv_w, conv_b], conv_in_specs,
                             [dxbc_act], [conv_out_spec], [0, 1, 2],
                             [_sd((t, SSM_CONV_DIM), BF), _sd((SSM_CONV, SSM_CONV_DIM)), _sd((1, SSM_CONV_DIM))],
                             [conv_out_spec, conv_in_specs[1], conv_in_specs[2]], [None, (1,), (1,)])
    grads['ssm_conv_w'], grads['ssm_conv_b'] = _xbc_ungroup(dcw, 1)[None], _xbc_ungroup(dcb, 1)
    dpool, dpw, dps = [], [], []
    for g in range(POOL_GROUPS):
        seqspec = pl.BlockSpec((seq, PG), lambda b, g=g: (b, g))
        one = pl.BlockSpec((seq, PG), lambda b: (b, 0))
        wspec = pl.BlockSpec((PG, PG), lambda b: (0, 0))
        sspec = pl.BlockSpec((1, PG), lambda b, g=g: (0, g))
        a, bb, c = bwd_call(make_pool_fn(g), f"d_pool_{g}", (bsz,), [u0, p['pool_w'][0, g], p['pool_scale']],
                            [seqspec, wspec, sspec], [dmix0], [seqspec], [0, 1, 2],
                            [_sd((t, PG), BF), _sd((PG, PG)), _sd((1, PG))], [one, wspec, pl.BlockSpec((1, PG), lambda b: (0, 0))],
                            [None, (0,), (0,)])
        dpool.append(a)
        dpw.append(bb)
        dps.append(c)
    grads['pool_w'] = jnp.stack(dpw)[None]
    grads['pool_scale'] = jnp.concatenate(dps, axis=1)
    du0 = jnp.concatenate(dpool + [dz, dxr, ddt.astype(BF)], axis=1)
    g_ab_in = matmul(du0, h0, 'tn', "d_ab_w_in", BF)
    g_ab_in = jnp.concatenate([g_ab_in[:xbc0], _xbc_ungroup(g_ab_in[xbc0:xbc0 + SSM_CONV_DIM], 0),
                               g_ab_in[xbc0 + SSM_CONV_DIM:AB_IN]], axis=0)
    ex.put_grads('ab', G_AB, {('ab_w_in', 0): g_ab_in, ('ab_w_out', 0): g_ab_out})
    dx, dg00 = matmul(du0, w_ab_in, 'nn', "d_h_ab", (F32,), epilogue=in_bwd_epilogue, extras=[x0, dx0r], params=[gain(0, 0)],
                      after=ex.take_tokens(), n_acc=1)
    gain_grads[(0, 0)] = dg00
    grads['norm_gains'] = jnp.stack([jnp.concatenate([gain_grads[(l, i)] for i in range(6)], axis=0) for l in range(2)])
    return loss, dx, grads
```

```python
import functools
import math

import numpy as np
import jax
import jax.numpy as jnp
from jax import lax
from jax.experimental import pallas as pl
from jax.experimental.pallas import tpu as pltpu

BF = jnp.bfloat16
F32 = jnp.float32

N_DEV = 8
D = 1024
N_MEM = 256
XA_HEADS = 4
XA_DH = D // XA_HEADS
POOL_GROUPS = 4
PG = 128
POOL_W = POOL_GROUPS * PG
SSM_INNER = 1024
SSM_GROUPS = 2
SSM_GSZ = SSM_INNER // SSM_GROUPS
SSM_HEADS = 16
SSM_P = 64
SSM_N = 128
SSM_CONV = 4
SSM_CONV_DIM = SSM_INNER + 2 * SSM_GROUPS * SSM_N
SSM_XBC_G = SSM_GSZ + 2 * SSM_N
CHUNK = 128
AB_IN = POOL_W + SSM_INNER + SSM_CONV_DIM + SSM_HEADS
AB_IN_PAD = POOL_W + SSM_INNER + SSM_CONV_DIM + 128
AB_OUT = POOL_W + SSM_INNER
CONF_K = 31
SC_K = 3
CD_IN = 5 * D
CD_OUT = 2 * D
MLP_H = 4 * D
RMS_EPS = 1e-6
LN_EPS = 1e-5
ADAM_LR = 0.001
ADAM_B1 = 0.9
ADAM_B2 = 0.999
ADAM_EPS = 1e-08
ADAM_WD = 0.01
ADAM_STEP = 10
VMEM_LIMIT = 56 * 1024 * 1024
LANE = 128

NAMES = ['x', 'mem', 'norm_gains', 'xa_wq', 'xa_wkv', 'xa_wo', 'mlp_w1', 'mlp_w2', 'ab_w_in', 'pool_w', 'pool_scale',
         'ssm_conv_w', 'ssm_conv_b', 'ssm_dt_bias', 'ssm_a_log', 'ssm_d', 'ssm_norm', 'ab_w_out', 'cd_w_in', 'conf_dw_w',
         'conf_dw_b', 'conf_ln_g', 'conf_ln_b', 'sc_conv_w', 'cd_w_out', 'loss_target']
WEIGHTS = NAMES[2:25]
BIG = [('xa_wq', 1), ('xa_wkv', 2), ('xa_wo', 1), ('mlp_w1', 2), ('mlp_w2', 1), ('cd_w_in', 2), ('cd_w_out', 1),
       ('ab_w_out', 1), ('ab_w_in', 2)]
SMALL_SHARDED = ['norm_gains', 'ssm_conv_w', 'conf_dw_w', 'conf_dw_b', 'conf_ln_g', 'conf_ln_b', 'sc_conv_w']
REPLICATED = ['pool_w', 'pool_scale', 'ssm_conv_b', 'ssm_dt_bias', 'ssm_a_log', 'ssm_d', 'ssm_norm']


def _dg(a, b, ca, cb, prec=None):
    return lax.dot_general(a, b, (((ca,), (cb,)), ((), ())), precision=prec, preferred_element_type=F32)


@functools.partial(jax.custom_vjp, nondiff_argnums=(2, 3))
def bdot(a, b, ca, cb):
    return _dg(a.astype(BF), b.astype(BF), ca, cb)


def _bdot_fwd(a, b, ca, cb):
    return bdot(a, b, ca, cb), (a, b)


def _bdot_bwd(ca, cb, res, g):
    a, b = res
    g16, a16, b16 = g.astype(BF), a.astype(BF), b.astype(BF)
    da = _dg(g16, b16, 1, 1 - cb) if ca == 1 else _dg(b16, g16, 1 - cb, 1)
    db = _dg(g16, a16, 0, 1 - ca) if cb == 1 else _dg(a16, g16, 1 - ca, 0)
    return da.astype(a.dtype), db.astype(b.dtype)


bdot.defvjp(_bdot_fwd, _bdot_bwd)


def _split3(a):
    a1 = a.astype(BF)
    r1 = a - a1.astype(F32)
    a2 = r1.astype(BF)
    a3 = (r1 - a2.astype(F32)).astype(BF)
    return a1, a2, a3


def _exact_right(a, c):
    m = a.shape[0]
    if m % 16:
        return sum(_dg(p, c, 1, 0) for p in _split3(a))
    o = _dg(jnp.concatenate(_split3(a), axis=0), c, 1, 0)
    return o[:m] + o[m:2 * m] + o[2 * m:]


def _exact_left(c, a):
    n = a.shape[1]
    o = _dg(c, jnp.concatenate(_split3(a), axis=1), 1, 0)
    return o[:, :n] + o[:, n:2 * n] + o[:, 2 * n:]


@jax.custom_vjp
def cmat(a, c, ct):
    return _exact_right(a, c)


def _cmat_fwd(a, c, ct):
    return cmat(a, c, ct), (c, ct)


def _cmat_bwd(res, g):
    c, ct = res
    return _exact_right(g, ct), jnp.zeros_like(c), jnp.zeros_like(ct)


cmat.defvjp(_cmat_fwd, _cmat_bwd)


@jax.custom_vjp
def cmatl(c, ct, a):
    return _exact_left(c, a)


def _cmatl_fwd(c, ct, a):
    return cmatl(c, ct, a), (c, ct)


def _cmatl_bwd(res, g):
    c, ct = res
    return jnp.zeros_like(c), jnp.zeros_like(ct), _exact_left(ct, g)


cmatl.defvjp(_cmatl_fwd, _cmatl_bwd)


SUBLANES = 8


def _taps(x, shifts, down):
    n, c = x.shape
    pad = _round_up(max(shifts), SUBLANES)
    if pad == 0:
        return {0: x}
    zeros = jnp.zeros((pad, c), x.dtype)
    xp = jnp.concatenate([zeros, x] if down else [x, zeros], axis=0)
    rolled, out = {0: xp}, {}
    for s in shifts:
        a, b = divmod(s, SUBLANES)
        if b not in rolled:
            rolled[b] = pltpu.roll(xp, b if down else n + pad - b, 0)
        off = pad - SUBLANES * a if down else SUBLANES * a
        out[s] = rolled[b][off:off + n]
    return out


def _shift_down(x, k):
    return _taps(x, [k], True)[k]


def _shift_up(x, k):
    return _taps(x, [k], False)[k]


@functools.partial(jax.custom_vjp, nondiff_argnums=(1,))
def shift(x, k):
    return _shift_down(x, k)


def _shift_fwd(x, k):
    return _shift_down(x, k), None


def _shift_bwd(k, _, g):
    return (_shift_up(g, k),)


shift.defvjp(_shift_fwd, _shift_bwd)


@functools.partial(jax.custom_vjp, nondiff_argnums=(2,))
def cconv(u, w, width):
    taps = _taps(u, list(range(width)), True)
    acc = u * w[width - 1:width, :]
    for k in range(width - 1):
        acc = acc + taps[width - 1 - k] * w[k:k + 1, :]
    return acc


def _cconv_fwd(u, w, width):
    return cconv(u, w, width), (u, w)


def _cconv_bwd(width, res, g):
    u, w = res
    rows = lax.broadcasted_iota(jnp.int32, w.shape, 0)
    du = g * w[width - 1:width, :]
    dw = jnp.where(rows == width - 1, jnp.sum(g * u, axis=0, keepdims=True), 0.0)
    g_taps = _taps(g, list(range(width)), False)
    u_taps = _taps(u, list(range(width)), True)
    for k in range(width - 1):
        s = width - 1 - k
        du = du + g_taps[s] * w[k:k + 1, :]
        dw = dw + jnp.where(rows == k, jnp.sum(g * u_taps[s], axis=0, keepdims=True), 0.0)
    return du, dw


cconv.defvjp(_cconv_fwd, _cconv_bwd)


def _rms(x, g):
    return x * lax.rsqrt(jnp.mean(x * x, axis=-1, keepdims=True) + RMS_EPS) * g


def _params(sem=None):
    return pltpu.CompilerParams(dimension_semantics=sem, vmem_limit_bytes=VMEM_LIMIT)


def _f32(v):
    return v if v.dtype == F32 else v.astype(F32)


def _first(axes):
    ok = None
    for ax in axes:
        c = pl.program_id(ax) == 0
        ok = c if ok is None else jnp.logical_and(ok, c)
    return ok


def fwd_call(fn, name, grid, ins, in_specs, out_shapes, out_specs, into=None):
    n_in = len(ins)
    n_into = 0 if into is None else 1

    def body(*refs):
        outs = fn(*[_f32(r[...]) for r in refs[:n_in]])
        for r, o in zip(refs[n_in + n_into:], outs):
            r[...] = o.astype(r.dtype)

    extra = [] if into is None else [into]
    return pl.pallas_call(body, name=name, grid=grid, in_specs=list(in_specs) + [pl.BlockSpec(memory_space=pl.ANY)] * n_into,
                          out_specs=out_specs, out_shape=out_shapes, input_output_aliases={n_in: 0} if n_into else {},
                          compiler_params=_params())(*ins, *extra)


def bwd_call(fn, name, grid, ins, in_specs, cots, cot_specs, gidx, g_shapes, g_specs, g_acc):
    n_in, n_cot = len(ins), len(cots)

    def body(*refs):
        vals = [_f32(r[...]) for r in refs[:n_in]]

        def f_sel(*dv):
            full = list(vals)
            for i, v in zip(gidx, dv):
                full[i] = v
            return tuple(fn(*full))

        outs, vjp = jax.vjp(f_sel, *[vals[i] for i in gidx])
        cts = tuple(_f32(r[...]) for r in refs[n_in:n_in + n_cot])
        grads = vjp(cts)
        for r, g, acc in zip(refs[n_in + n_cot:], grads, g_acc):
            if acc is None:
                r[...] = g.astype(r.dtype)
            else:
                @pl.when(_first(acc))
                def _():
                    r[...] = jnp.zeros_like(r)

                r[...] += g.astype(r.dtype)

    return pl.pallas_call(body, name=name, grid=grid, in_specs=list(in_specs) + list(cot_specs), out_specs=g_specs,
                          out_shape=g_shapes, compiler_params=_params())(*ins, *cots)


def _tile(dim, pref):
    if dim <= pref:
        return dim
    best = None
    for t in range(LANE, pref + 1, LANE):
        if dim % t == 0:
            best = t
    assert best is not None, dim
    return best


MATMUL_VMEM_BUDGET = 40 * 1024 * 1024


def _matmul_tiles(m, n, k, a_bytes, b_bytes, out_bytes):
    tn = _tile(n, 1024)
    for tk_pref in (k, 2048, 1024, 512):
        tk = _tile(k, tk_pref)
        for tm_pref in (1024, 512, 256):
            tm = _tile(m, tm_pref)
            need = 2 * (tm * tk * a_bytes + tk * tn * b_bytes + tm * tn * out_bytes) + (0 if tk == k else tm * tn * 4)
            need += (tm * tk * 2 if a_bytes == 4 else 0) + (tk * tn * 2 if b_bytes == 4 else 0)
            if need <= MATMUL_VMEM_BUDGET:
                return tm, tn, tk
    raise ValueError((m, n, k))


def matmul(a, b, mode, name, out_dtype=F32, epilogue=None, extras=(), params=(), after=(), n_acc=0):
    if mode == 'nn':
        (m, k), (k2, n) = a.shape, b.shape
    elif mode == 'nt':
        (m, k), (n, k2) = a.shape, b.shape
    else:
        (k, m), (k2, n) = a.shape, b.shape
    assert k == k2, (name, a.shape, b.shape)
    n_extra = len(extras) + len(params)
    out_dtypes = out_dtype if isinstance(out_dtype, tuple) else (out_dtype,)
    per_out = sum(jnp.dtype(dt).itemsize for dt in out_dtypes) + sum(e.dtype.itemsize for e in extras)
    tm, tn, tk = _matmul_tiles(m, n, k, a.dtype.itemsize, b.dtype.itemsize, per_out)
    nk = k // tk
    ca = 0 if mode == 'tn' else 1
    cb = 1 if mode == 'nt' else 0
    a_spec = pl.BlockSpec((tk, tm), lambda i, j, kk: (kk, i)) if mode == 'tn' else pl.BlockSpec((tm, tk), lambda i, j, kk: (i, kk))
    b_spec = pl.BlockSpec((tn, tk), lambda i, j, kk: (j, kk)) if mode == 'nt' else pl.BlockSpec((tk, tn), lambda i, j, kk: (kk, j))

    def finish(o_refs, extra_refs, acc, first_row_tile):
        outs = (acc,) if epilogue is None else epilogue(acc, *[_f32(e[...]) for e in extra_refs])
        n_tile = len(o_refs) - n_acc
        for o_ref, o in zip(o_refs[:n_tile], outs[:n_tile]):
            o_ref[...] = o.astype(o_ref.dtype)
        for o_ref, o in zip(o_refs[n_tile:], outs[n_tile:]):
            o_ref[...] = jnp.where(first_row_tile, o, o_ref[...] + o)

    n_after = len(after)

    def body_whole_k(a_ref, b_ref, *refs):
        refs = refs[n_after:]
        finish(refs[n_extra:], refs[:n_extra], _dg(a_ref[...].astype(BF), b_ref[...].astype(BF), ca, cb), pl.program_id(0) == 0)

    def body_split_k(a_ref, b_ref, *refs):
        refs = refs[n_after:]
        extra_refs, o_refs, acc = refs[:n_extra], refs[n_extra:-1], refs[-1]
        kk = pl.program_id(2)
        first_row_tile = pl.program_id(0) == 0

        @pl.when(kk == 0)
        def _():
            acc[...] = jnp.zeros_like(acc)

        acc[...] += _dg(a_ref[...].astype(BF), b_ref[...].astype(BF), ca, cb)

        @pl.when(kk == nk - 1)
        def _():
            finish(o_refs, extra_refs, acc[...], first_row_tile)

    tile = pl.BlockSpec((tm, tn), lambda i, j, kk: (i, j))
    row = pl.BlockSpec((1, tn), lambda i, j, kk: (0, j))
    n_par = len(params)
    outs = pl.pallas_call(
        body_whole_k if nk == 1 else body_split_k, name=name, grid=(m // tm, n // tn, nk),
        in_specs=[a_spec, b_spec] + [pl.BlockSpec(memory_space=pl.ANY)] * n_after + [tile] * len(extras) + [row] * n_par,
        out_specs=[tile] * len(out_dtypes) + [row] * n_acc,
        out_shape=[jax.ShapeDtypeStruct((m, n), dt) for dt in out_dtypes] + [jax.ShapeDtypeStruct((1, n), F32)] * n_acc,
        scratch_shapes=[] if nk == 1 else [pltpu.VMEM((tm, tn), F32)],
        compiler_params=_params(("arbitrary",) * 3 if n_acc else ("parallel", "parallel", "arbitrary")))(a, b, *after, *extras, *params)
    return outs if isinstance(out_dtype, tuple) or n_acc else outs[0]


_FLIPS = [(0, 0, 1), (1, 0, 0), (0, 1, 0), (1, 1, 0), (1, 0, 1), (0, 1, 1), (1, 1, 1)]


def _me():
    return lax.axis_index("x"), lax.axis_index("y"), lax.axis_index("c")


def _flip(pos, f):
    return tuple(jnp.where(fi == 1, 1 - p, p) if fi else p for p, fi in zip(pos, f))


def _slot(pos):
    return 4 * pos[0] + 2 * pos[1] + pos[2]


def all_gather(v, name):
    def body(v_ref, out_ref, send_sems, recv_sems, local_sem):
        me = _me()
        sibling = _flip(me, (0, 0, 1))
        chips = [_flip(me, f) for f in ((1, 0, 0), (0, 1, 0), (1, 1, 0))]

        def copy(k, block, to, src=None):
            return pltpu.make_async_remote_copy(
                src_ref=out_ref.at[_slot(block)] if src is None else src, dst_ref=out_ref.at[_slot(block)],
                send_sem=send_sems.at[k], recv_sem=recv_sems.at[k], device_id=to, device_id_type=pl.DeviceIdType.MESH)

        mine = pltpu.make_async_copy(v_ref, out_ref.at[_slot(me)], local_sem)
        mine.start()
        first = [copy(0, me, sibling, src=v_ref)] + [copy(1 + j, me, chip, src=v_ref) for j, chip in enumerate(chips)]
        for cp in first:
            cp.start()
        passed = [copy(4 + j, chip, sibling) for j, chip in enumerate(chips)]
        for j, chip in enumerate(chips):
            copy(1 + j, chip, me).wait_recv()
            passed[j].start()
        copy(0, sibling, me).wait_recv()
        for j, chip in enumerate(chips):
            copy(4 + j, _flip(chip, (0, 0, 1)), me).wait_recv()
        for cp in first + passed:
            cp.wait_send()
        mine.wait()

    return pl.pallas_call(
        body, name=name, out_shape=jax.ShapeDtypeStruct((N_DEV,) + v.shape, v.dtype),
        in_specs=[pl.BlockSpec(memory_space=pl.ANY)], out_specs=pl.BlockSpec(memory_space=pl.ANY),
        scratch_shapes=[pltpu.SemaphoreType.DMA((7,)), pltpu.SemaphoreType.DMA((7,)), pltpu.SemaphoreType.DMA(())],
    )(v)


def sum_slots(v, name, tr=256):
    _, r, c = v.shape
    tr = _tile_rows(r, tr)

    def body(v_ref, o_ref):
        acc = v_ref[0].astype(F32)
        for s in range(1, N_DEV):
            acc = acc + v_ref[s].astype(F32)
        o_ref[...] = acc

    return pl.pallas_call(body, name=name, grid=(r // tr,), in_specs=[pl.BlockSpec((N_DEV, tr, c), lambda i: (0, i, 0))],
                          out_specs=pl.BlockSpec((tr, c), lambda i: (i, 0)), out_shape=jax.ShapeDtypeStruct((r, c), F32),
                          compiler_params=_params())(v)


def _tile_rows(r, pref):
    if r <= pref:
        return r
    best = None
    for t in range(8, pref + 1, 8):
        if r % t == 0:
            best = t
    return r if best is None else best


def _adamw_math(w, m, v, g):
    nm = ADAM_B1 * m + (1.0 - ADAM_B1) * g
    nv = ADAM_B2 * v + (1.0 - ADAM_B2) * jnp.square(g)
    m_hat = nm / (1.0 - ADAM_B1 ** ADAM_STEP)
    v_hat = nv / (1.0 - ADAM_B2 ** ADAM_STEP)
    return -ADAM_LR * (m_hat / (jnp.sqrt(v_hat) + ADAM_EPS) + ADAM_WD * w), nm, nv


def update_from_slots(lands, offs, w, m, v, transposed, name):
    layers, a, b = w.shape
    n_land = len(lands)
    if transposed:
        rb, tk = LANE, 512
        assert a % tk == 0 and b % rb == 0 and all(o % rb == 0 for o in offs), (name, w.shape, offs)
        grid = (layers, a // tk, b // rb)
        land_block = (N_DEV, rb, tk)
        tile = pl.BlockSpec((None, tk, rb), lambda l, i, j: (l, i, j))

        def land_spec(layer):
            base = offs[layer] // rb
            return pl.BlockSpec(land_block, lambda l, i, j: (0, base + jnp.where(l == layer, j, 0), jnp.where(l == layer, i, 0)))
    else:
        fits = [t for t in (256, 128, 64) if a % t == 0 and all(o % t == 0 for o in offs)]
        assert fits or all(o == 0 for o in offs), (name, w.shape, offs)
        tr = max(fits) if fits else a
        grid = (layers, a // tr)
        land_block = (N_DEV, _round_up(tr, MEMBER_ROW_TILE), b)
        tile = pl.BlockSpec((None, tr, b), lambda l, i: (l, i, 0))

        def land_spec(layer):
            base = offs[layer] // tr
            return pl.BlockSpec(land_block, lambda l, i: (0, base + jnp.where(l == layer, i, 0), 0))

    def body(*refs):
        land_refs, (w_ref, m_ref, v_ref, g_ref, d_ref, nm_ref, nv_ref, acc) = refs[:n_land], refs[n_land:]
        for layer, land in enumerate(land_refs):
            @pl.when(pl.program_id(0) == layer)
            def _(land=land):
                rows = acc.shape[0]
                s = land[0, :rows].astype(F32)
                for k in range(1, N_DEV):
                    s = s + land[k, :rows].astype(F32)
                acc[...] = s

        g = acc[...].T if transposed else acc[...]
        d, nm, nv = _adamw_math(w_ref[...], m_ref[...], v_ref[...], g)
        g_ref[...] = g
        d_ref[...] = d
        nm_ref[...] = nm
        nv_ref[...] = nv

    sh = jax.ShapeDtypeStruct(w.shape, F32)
    return pl.pallas_call(
        body, name=name, grid=grid, in_specs=[land_spec(layer) for layer in range(n_land)] + [tile] * 3, out_specs=[tile] * 4,
        out_shape=[sh] * 4, scratch_shapes=[pltpu.VMEM((rb, tk) if transposed else (tr, b), F32)],
        compiler_params=_params())(*lands, w, m, v)


def adamw_many(ws, ms, vs, gs, name):
    n = len(ws)

    def body(*refs):
        for i in range(n):
            d, nm, nv = _adamw_math(refs[i][...], refs[n + i][...], refs[2 * n + i][...], refs[3 * n + i][...])
            refs[4 * n + i][...] = d
            refs[5 * n + i][...] = nm
            refs[6 * n + i][...] = nv

    vmem = pl.BlockSpec(memory_space=pltpu.VMEM)
    shapes = [jax.ShapeDtypeStruct(a.shape, F32) for a in ws]
    res = pl.pallas_call(body, name=name, in_specs=[vmem] * (4 * n), out_specs=[vmem] * (3 * n), out_shape=shapes * 3,
                         compiler_params=_params())(*ws, *ms, *vs, *gs)
    return res[:n], res[n:2 * n], res[2 * n:]


def seg_in(x, g):
    return (_rms(x, g),)


def seg_in_res(x, g):
    return x, _rms(x, g)


def seg_res(x, m, ga, gb):
    x1 = x + _rms(m, ga)
    return x1, _rms(x1, gb)


def seg_out(x, m, ga):
    return (x + _rms(m, ga),)


def act_epilogue(r):
    t = jnp.maximum(r, 0.0)
    return r, t * t


def res_epilogue(m, x, ga, gb):
    x1, h = seg_res(x, m, ga, gb)
    return m, x1, h


def res_bwd_epilogue(dh, x, m, dx1, ga, gb):
    _, vjp = jax.vjp(seg_res, x, m, ga, gb)
    return vjp((dx1, dh))


def in_bwd_epilogue(dh, x, dx_res, g):
    _, vjp = jax.vjp(seg_in_res, x, g)
    return vjp((dx_res, dh))


def act_bwd_epilogue(drr, r):
    return (drr * (2.0 * jnp.maximum(r, 0.0)),)


def seg_ln(v, g, b):
    mu = jnp.mean(v, axis=-1, keepdims=True)
    var = jnp.mean(jnp.square(v - mu), axis=-1, keepdims=True)
    vn = (v - mu) * lax.rsqrt(var + LN_EPS) * g + b
    return (jax.nn.silu(vn),)


def make_pool_fn(group):
    window = 2 ** (group + 1)

    def pool_fn(ug, pw, scale):
        s = ug
        for lvl in range(group + 1):
            s = s + shift(s, 2 ** lvl)
        cnt = jnp.minimum(lax.broadcasted_iota(jnp.int32, ug.shape, 0) + 1, window).astype(F32)
        return (bdot(s / cnt - ug, pw, 1, 0) * scale,)

    return pool_fn


def conv4_fn(xr, w, b):
    return (jax.nn.silu(cconv(xr, w, SSM_CONV) + b),)


def cd1_fn(u, dww, dwb, scw):
    val, gate, bg, cg, hh = (u[:, k * LANE:(k + 1) * LANE] for k in range(5))
    v = val * jax.nn.sigmoid(gate)
    vc = cconv(v, dww, CONF_K) + dwb
    sc = bg * cconv(cg * hh, scw, SC_K)
    return vc, sc


def attn_fn(q, kv):
    outs = []
    for h in range(XA_HEADS):
        cols = slice(h * XA_DH, (h + 1) * XA_DH)
        s = bdot(q[:, cols], kv[:, cols], 1, 1) / math.sqrt(XA_DH)
        p = jax.nn.softmax(s, axis=-1)
        outs.append(bdot(p, kv[:, D + h * XA_DH:D + (h + 1) * XA_DH], 1, 0))
    return (jnp.concatenate(outs, axis=1),)


def ssd_chunk(xbc, z, dtraw, dtb, alog, dsk, nw, h0, h1, h2, h3, e64, e64t, ecat, ecatt, tril, trilt):
    xs, bm, cm = xbc[:, :SSM_GSZ], xbc[:, SSM_GSZ:SSM_GSZ + SSM_N], xbc[:, SSM_GSZ + SSM_N:]
    hin = (h0, h1, h2, h3)
    dt = jax.nn.softplus(dtraw + dtb)
    a = -jnp.exp(alog)
    d_a = dt * a
    cs = cmatl(tril, trilt, d_a)
    cs_cat = cmat(cs, ecat, ecatt)
    cs64, cs128 = cs_cat[:, :SSM_GSZ], cs_cat[:, SSM_GSZ:]
    dt64 = cmat(dt, e64, e64t)
    row = lax.broadcasted_iota(jnp.int32, (8, LANE), 0)
    heads = jnp.where(row == 0, dsk, jnp.where(row == 1, jnp.sum(d_a, axis=0, keepdims=True), 0.0))
    heads64 = cmat(heads, e64, e64t)
    d64, tot64 = heads64[0:1, :], heads64[1:2, :]
    xdt = xs * dt64
    cb = bdot(cm, bm, 1, 1)
    li = lax.broadcasted_iota(jnp.int32, (CHUNK, CHUNK), 0)
    si = lax.broadcasted_iota(jnp.int32, (CHUNK, CHUNK), 1)
    causal = li >= si
    lane = lax.broadcasted_iota(jnp.int32, (CHUNK, LANE), 1)
    xw = xdt * jnp.exp(tot64 - cs64)
    ecs = jnp.exp(cs64)
    etot = jnp.exp(tot64)
    ycols, hout = [], []
    for j in range(4):
        sl = slice(j * LANE, (j + 1) * LANE)
        xj = xdt[:, sl]
        ys = []
        for hh in range(2):
            r = 2 * j + hh
            col = cs128[:, r * LANE:(r + 1) * LANE]
            decay = jnp.exp(jnp.where(causal, col - col.T, -1e30))
            ys.append(bdot(cb * decay, xj, 1, 0))
        y_diag = jnp.where(lane < SSM_P, ys[0], ys[1])
        y_off = bdot(cm, hin[j], 1, 0) * ecs[:, sl]
        ycols.append(y_diag + y_off)
        hout.append(etot[:, sl] * hin[j] + bdot(bm, xw[:, sl], 0, 0))
    y = jnp.concatenate(ycols, axis=1) + d64 * xs
    y = y * jax.nn.silu(z)
    yn = y * lax.rsqrt(jnp.mean(y * y, axis=-1, keepdims=True) + RMS_EPS) * nw
    return (yn,) + tuple(hout)


def _xbc_group(a, axis):
    parts = []
    for g in range(SSM_GROUPS):
        for start, width in ((g * SSM_GSZ, SSM_GSZ), (SSM_INNER + g * SSM_N, SSM_N), (SSM_INNER + (SSM_GROUPS + g) * SSM_N, SSM_N)):
            parts.append(lax.slice_in_dim(a, start, start + width, axis=axis))
    return jnp.concatenate(parts, axis=axis)


def _xbc_ungroup(a, axis):
    xs, bs, cs = [], [], []
    for g in range(SSM_GROUPS):
        base = g * SSM_XBC_G
        xs.append(lax.slice_in_dim(a, base, base + SSM_GSZ, axis=axis))
        bs.append(lax.slice_in_dim(a, base + SSM_GSZ, base + SSM_GSZ + SSM_N, axis=axis))
        cs.append(lax.slice_in_dim(a, base + SSM_GSZ + SSM_N, base + SSM_XBC_G, axis=axis))
    return jnp.concatenate(xs + bs + cs, axis=axis)


def _ssd_consts():
    h = np.arange(LANE)[:, None]
    e64 = np.stack([(h == g * 8 + np.arange(SSM_GSZ)[None, :] // SSM_P) for g in range(SSM_GROUPS)]).astype(np.float32)
    e128 = np.stack([(h == g * 8 + np.arange(8 * LANE)[None, :] // LANE) for g in range(SSM_GROUPS)]).astype(np.float32)
    ecat = np.concatenate([e64, e128], axis=2)
    tril = np.tril(np.ones((CHUNK, CHUNK), np.float32))
    return tuple(jnp.asarray(c, dtype=BF) for c in (e64, e64.transpose(0, 2, 1), ecat, ecat.transpose(0, 2, 1), tril, tril.T))


def _ssd_specs(nc, rev):
    def ci(c):
        return nc - 1 - c if rev else c

    def row(width, col):
        return pl.BlockSpec((CHUNK, width), lambda b, c: (b * nc + ci(c), col))

    def whole(shape):
        return pl.BlockSpec(shape, lambda b, c: (0,) * len(shape))

    data = [row(SSM_CONV_DIM, 0),
            row(SSM_GSZ, 1), row(SSM_GSZ, 2), row(LANE, 24)]
    par = [whole((1, LANE))] * 3 + [whole((1, SSM_INNER))]
    cst = [whole((SSM_GROUPS, LANE, SSM_GSZ)), whole((SSM_GROUPS, SSM_GSZ, LANE)), whole((SSM_GROUPS, LANE, 12 * LANE)),
           whole((SSM_GROUPS, 12 * LANE, LANE)), whole((CHUNK, CHUNK)), whole((CHUNK, CHUNK))]
    hsave = pl.BlockSpec((None, None, SSM_GROUPS, 4, SSM_N, LANE), lambda b, c: (b, ci(c), 0, 0, 0, 0))
    return data, par, cst, hsave, row, whole


def _ssd_group_args(g, xbc, z, dtr, dtb, alog, dsk, nw):
    return (xbc[:, g * SSM_XBC_G:(g + 1) * SSM_XBC_G], z[g], dtr, dtb, alog, dsk, nw[:, g * SSM_GSZ:(g + 1) * SSM_GSZ])


def ssd_fwd(xbc_act, u, dtb, alog, dsk, nw, consts, bsz, seq):
    nc = seq // CHUNK
    data, par, cst, hsave, row, _ = _ssd_specs(nc, False)

    def body(xbc, z0, z1, dtr, dtb_r, alog_r, dsk_r, nw_r, e64, e64t, ecat, ecatt, tril, trilt, yn_ref, hs_ref, h):
        @pl.when(pl.program_id(1) == 0)
        def _():
            h[...] = jnp.zeros_like(h)

        hs_ref[...] = h[...]
        ys = []
        for g in range(SSM_GROUPS):
            args = _ssd_group_args(g, xbc[...], (z0[...], z1[...]), dtr[...], dtb_r[...], alog_r[...], dsk_r[...], nw_r[...])
            outs = ssd_chunk(*args, h[g, 0], h[g, 1], h[g, 2], h[g, 3], e64[g], e64t[g], ecat[g], ecatt[g], tril[...], trilt[...])
            ys.append(outs[0])
            for j in range(4):
                h[g, j] = outs[1 + j]
        yn_ref[...] = jnp.concatenate(ys, axis=1).astype(yn_ref.dtype)

    t = bsz * seq
    return pl.pallas_call(
        body, name="ssd_fwd", grid=(bsz, nc), in_specs=data + par + cst, out_specs=[row(SSM_INNER, 0), hsave],
        out_shape=[jax.ShapeDtypeStruct((t, SSM_INNER), BF), jax.ShapeDtypeStruct((bsz, nc, SSM_GROUPS, 4, SSM_N, LANE), F32)],
        scratch_shapes=[pltpu.VMEM((SSM_GROUPS, 4, SSM_N, LANE), F32)], compiler_params=_params(),
    )(xbc_act, u, u, u, dtb, alog, dsk, nw, *consts)


def ssd_bwd(xbc_act, u, dtb, alog, dsk, nw, consts, hs, dmix, bsz, seq):
    nc = seq // CHUNK
    data, par, cst, hsave, row, whole = _ssd_specs(nc, True)
    t = bsz * seq
    pcol = POOL_W // SSM_GSZ

    def body(xbc, z0, z1, dtr, dtb_r, alog_r, dsk_r, nw_r, e64, e64t, ecat, ecatt, tril, trilt, hs_ref, dy0, dy1,
             dxbc, dz, ddt, ddtb, dalog, ddsk, dnw, dh):
        @pl.when(pl.program_id(1) == 0)
        def _():
            dh[...] = jnp.zeros_like(dh)

        per_group = []
        for g, dyn in enumerate((dy0, dy1)):
            cst_vals = (e64[g], e64t[g], ecat[g], ecatt[g], tril[...], trilt[...])
            prim = _ssd_group_args(g, xbc[...], (z0[...], z1[...]), dtr[...], dtb_r[...], alog_r[...], dsk_r[...], nw_r[...])
            prim = prim + (hs_ref[g, 0], hs_ref[g, 1], hs_ref[g, 2], hs_ref[g, 3])
            _, vjp = jax.vjp(lambda *args, c=cst_vals: ssd_chunk(*args, *c), *prim)
            gr = vjp((dyn[...].astype(F32), dh[g, 0], dh[g, 1], dh[g, 2], dh[g, 3]))
            for j in range(4):
                dh[g, j] = gr[7 + j]
            per_group.append(gr)
        g0, g1 = per_group
        dxbc[...] = jnp.concatenate([g0[0], g1[0]], axis=1)
        dz[...] = jnp.concatenate([g0[1], g1[1]], axis=1).astype(dz.dtype)
        ddt[...] = g0[2] + g1[2]

        @pl.when(_first((0, 1)))
        def _():
            for r in (ddtb, dalog, ddsk, dnw):
                r[...] = jnp.zeros_like(r)

        ddtb[...] += g0[3] + g1[3]
        dalog[...] += g0[4] + g1[4]
        ddsk[...] += g0[5] + g1[5]
        dnw[...] += jnp.concatenate([g0[6], g1[6]], axis=1)

    out_specs = [row(SSM_CONV_DIM, 0), row(SSM_INNER, 0), row(LANE, 0), whole((1, LANE)), whole((1, LANE)), whole((1, LANE)),
                 whole((1, SSM_INNER))]
    lane = jax.ShapeDtypeStruct((1, LANE), F32)
    out_shape = [jax.ShapeDtypeStruct((t, SSM_CONV_DIM), F32), jax.ShapeDtypeStruct((t, SSM_INNER), BF),
                 jax.ShapeDtypeStruct((t, LANE), F32), lane, lane, lane, jax.ShapeDtypeStruct((1, SSM_INNER), F32)]
    return pl.pallas_call(
        body, name="ssd_bwd", grid=(bsz, nc), in_specs=data + par + cst + [hsave, row(SSM_GSZ, pcol), row(SSM_GSZ, pcol + 1)],
        out_specs=out_specs, out_shape=out_shape, scratch_shapes=[pltpu.VMEM((SSM_GROUPS, 4, SSM_N, LANE), F32)],
        compiler_params=_params(),
    )(xbc_act, u, u, u, dtb, alog, dsk, nw, *consts, hs, dmix, dmix)


TB = 512


def _rows(d, col=0):
    return pl.BlockSpec((TB, d), lambda i: (i, col))


def _par(d):
    return pl.BlockSpec((1, d), lambda i: (0, 0))


def _sd(shape, dtype=F32):
    return jax.ShapeDtypeStruct(shape, dtype)


def _round_up(n, m):
    return -(-n // m) * m


def _pad_rows(a, rows):
    return jnp.pad(a, ((0, rows - a.shape[0]), (0, 0)))


def _pack128(arrs):
    flat = jnp.concatenate([a.reshape(-1) for a in arrs])
    n = flat.shape[0]
    rows = -(-n // (8 * LANE)) * 8
    return jnp.pad(flat, (0, rows * LANE - n)).reshape(rows, LANE)


def _unpack128(packed, shapes):
    flat = packed.reshape(-1)
    out, off = [], 0
    for s in shapes:
        n = int(np.prod(s))
        out.append(flat[off:off + n].reshape(s))
        off += n
    return out


def kernel(x, mem, norm_gains, xa_wq, xa_wkv, xa_wo, mlp_w1, mlp_w2, ab_w_in, pool_w, pool_scale, ssm_conv_w, ssm_conv_b, ssm_dt_bias, ssm_a_log, ssm_d, ssm_norm, ab_w_out, cd_w_in, conf_dw_w, conf_dw_b, conf_ln_g, conf_ln_b, sc_conv_w, cd_w_out, loss_target, m_norm_gains, m_xa_wq, m_xa_wkv, m_xa_wo, m_mlp_w1, m_mlp_w2, m_ab_w_in, m_pool_w, m_pool_scale, m_ssm_conv_w, m_ssm_conv_b, m_ssm_dt_bias, m_ssm_a_log, m_ssm_d, m_ssm_norm, m_ab_w_out, m_cd_w_in, m_conf_dw_w, m_conf_dw_b, m_conf_ln_g, m_conf_ln_b, m_sc_conv_w, m_cd_w_out, v_norm_gains, v_xa_wq, v_xa_wkv, v_xa_wo, v_mlp_w1, v_mlp_w2, v_ab_w_in, v_pool_w, v_pool_scale, v_ssm_conv_w, v_ssm_conv_b, v_ssm_dt_bias, v_ssm_a_log, v_ssm_d, v_ssm_norm, v_ab_w_out, v_cd_w_in, v_conf_dw_w, v_conf_dw_b, v_conf_ln_g, v_conf_ln_b, v_sc_conv_w, v_cd_w_out):
    args = locals()
    w = {n: args[n] for n in WEIGHTS}
    mom_m = {n: args["m_" + n] for n in WEIGHTS}
    mom_v = {n: args["v_" + n] for n in WEIGHTS}
    ex = Exchange(w)
    loss_local, grad_x, small_grads = local_step(x, mem, loss_target, ex)
    outs = {}

    started = ex.put_small(small_grads, loss_local)
    landed = {key: ex.landed(key, started) for key in ('l1', 'cd', 'l0')}
    late = []
    for n, keys in (('mlp_w1', ('l0', 'l1')), ('mlp_w2', ('l0', 'l1')), ('xa_wkv', ('l0', 'l1')), ('xa_wq', ('l0', 'l1')),
                    ('xa_wo', ('l0', 'l1')), ('cd_w_in', ('cd',)), ('cd_w_out', ('cd',))):
        lands = [landed[key][0] for key in keys]
        offs = [landed[key][1][(n, layer)] for layer, key in enumerate(keys)]
        outs[n] = update_from_slots(lands, offs, w[n], mom_m[n], mom_v[n], SHARD_AXIS[n] == 2, "update_" + n)
        late.append(outs[n][1])
    g_own, loss = ex.reduced_small(late)
    land_ab, offs_ab = ex.landed('ab', late)
    outs['ab_w_out'] = update_from_slots([land_ab], [offs_ab[('ab_w_out', 0)]], w['ab_w_out'], mom_m['ab_w_out'],
                                         mom_v['ab_w_out'], False, "update_ab_w_out")
    res = update_from_slots([land_ab], [offs_ab[('ab_w_in', 0)]], jnp.swapaxes(w['ab_w_in'], 1, 2), jnp.swapaxes(mom_m['ab_w_in'], 1, 2),
                            jnp.swapaxes(mom_v['ab_w_in'], 1, 2), False, "update_ab_w_in")
    outs['ab_w_in'] = tuple(jnp.swapaxes(r, 1, 2) for r in res)
    small = SMALL_SHARDED + REPLICATED
    upd = adamw_many([w[n] for n in small], [mom_m[n] for n in small], [mom_v[n] for n in small], [g_own[n] for n in small],
                     "adamw_small")
    for i, n in enumerate(small):
        outs[n] = (g_own[n], upd[0][i], upd[1][i], upd[2][i])
    return (loss, grad_x.reshape(x.shape), *[outs[n][0] for n in WEIGHTS], *[outs[n][1] for n in WEIGHTS],
            *[outs[n][2] for n in WEIGHTS], *[outs[n][3] for n in WEIGHTS])


G_AB = (('ab_w_in', 0), ('ab_w_out', 0))
G_L0 = (('xa_wq', 0), ('xa_wkv', 0), ('xa_wo', 0), ('mlp_w1', 0), ('mlp_w2', 0))
G_L1 = (('xa_wq', 1), ('xa_wkv', 1), ('xa_wo', 1), ('mlp_w1', 1), ('mlp_w2', 1))
G_CD = (('cd_w_in', 0), ('cd_w_out', 0))
GATHER_GROUPS = {'ab': G_AB, 'l0a': G_L0[:3], 'l0b': G_L0[3:], 'cd': G_CD, 'l1a': G_L1[:3], 'l1b': G_L1[3:]}
SHARD_AXIS = dict(BIG)
MEMBER_ROW_TILE = 64
FLAT_ROW_TILE = 128


def _members(group, w):
    out = []
    for n, layer in group:
        shp = w[n].shape[1:]
        if SHARD_AXIS[n] == 2:
            shp = (shp[1], shp[0])
        assert shp[1] == D, (n, shp)
        out.append((n, layer, shp, shp[0], _round_up(shp[0], MEMBER_ROW_TILE)))
    return out


def _group_rows(group, w):
    return _round_up(sum(m[4] for m in _members(group, w)), FLAT_ROW_TILE)


def _flat_shards(group, w):
    parts = []
    for n, layer, _, _, padded in _members(group, w):
        shard = w[n][layer].astype(BF)
        parts.append(_pad_rows(shard.T if SHARD_AXIS[n] == 2 else shard, padded))
    return _pad_rows(jnp.concatenate(parts, axis=0), _group_rows(group, w))


def _full_from_slots(land, group, w):
    out, off = {}, 0
    for n, layer, shp, rows, padded in _members(group, w):
        out[(n, layer)] = land[:, off:off + rows].reshape(N_DEV * rows, D)
        off += padded
    return out


def _slots_from_full(grads, group, w):
    parts = []
    for n, layer, shp, rows, padded in _members(group, w):
        blk = grads[(n, layer)].astype(BF).reshape(N_DEV, rows, D)
        parts.append(jnp.pad(blk, ((0, 0), (0, padded - rows), (0, 0))))
    send = jnp.concatenate(parts, axis=1)
    return jnp.pad(send, ((0, 0), (0, _group_rows(group, w) - send.shape[1]), (0, 0)))


_HBM = pl.BlockSpec(memory_space=pltpu.HBM)
_SEM = pl.BlockSpec(memory_space=pltpu.SEMAPHORE)
_ANY = pl.BlockSpec(memory_space=pl.ANY)


def _peer_copy(k, src, dst, send_sems, recv_sems, peer):
    return pltpu.make_async_remote_copy(src_ref=src, dst_ref=dst, send_sem=send_sems.at[k], recv_sem=recv_sems.at[k],
                                        device_id=peer, device_id_type=pl.DeviceIdType.MESH)


def exchange_start(src, name, scatter, after=()):
    shape = src.shape[-2:]
    after = list(after)

    def body(src_ref, land_ref, *rest):
        send_sems, recv_sems, token = rest[len(after)], rest[len(after) + 1], rest[-1]
        me = _me()
        for k, f in enumerate(_FLIPS):
            peer = _flip(me, f)
            piece = src_ref.at[_slot(peer)] if scatter else src_ref
            _peer_copy(k, piece, land_ref.at[_slot(me)], send_sems, recv_sems, peer).start()
        token[...] = jnp.zeros_like(token)

    land = pltpu.with_memory_space_constraint(lax.empty((N_DEV,) + shape, src.dtype), pltpu.HBM)
    return pl.pallas_call(
        body, name=name,
        out_shape=(pltpu.SemaphoreType.DMA((7,)), pltpu.SemaphoreType.DMA((7,)), pltpu.HBM(src.shape, src.dtype),
                   pltpu.HBM((N_DEV,) + shape, src.dtype), jax.ShapeDtypeStruct((8, LANE), F32)),
        in_specs=(_HBM, _HBM) + (_ANY,) * len(after), out_specs=(_SEM, _SEM, _HBM, _HBM, pl.BlockSpec(memory_space=pltpu.VMEM)),
        input_output_aliases={0: 2, 1: 3},
        compiler_params=pltpu.CompilerParams(has_side_effects=pltpu.SideEffectType.DATAFLOW_SIDE_EFFECTING),
    )(pltpu.with_memory_space_constraint(src, pltpu.HBM), land, *after)


def exchange_wait(handles, after, name, scatter):
    send_sems, recv_sems, src_thru, land_thru, _ = handles
    after = list(after) if isinstance(after, (list, tuple)) else [after]

    def body(src_ref, land_ref, send_sems, recv_sems, *rest):
        token = rest[-1]
        me = _me()
        for k, f in enumerate(_FLIPS):
            peer = _flip(me, f)
            piece = src_ref.at[_slot(peer)] if scatter else src_ref
            cp = _peer_copy(k, piece, land_ref.at[_slot(peer)], send_sems, recv_sems, peer)
            cp.wait_send()
            cp.wait_recv()
        token[...] = jnp.zeros_like(token)

    return pl.pallas_call(
        body, name=name, out_shape=(pltpu.HBM(src_thru.shape, src_thru.dtype), pltpu.HBM(land_thru.shape, land_thru.dtype),
                                    jax.ShapeDtypeStruct((8, LANE), F32)),
        in_specs=(_HBM, _HBM, _SEM, _SEM) + (_ANY,) * len(after), out_specs=(_HBM, _HBM, pl.BlockSpec(memory_space=pltpu.VMEM)),
        input_output_aliases={0: 0, 1: 1},
        compiler_params=pltpu.CompilerParams(has_side_effects=pltpu.SideEffectType.DATAFLOW_SIDE_EFFECTING),
    )(src_thru, land_thru, send_sems, recv_sems, *after)


class Exchange:
    def __init__(self, w):
        self.w = w
        self.me = _slot(_me())
        shapes = [w[n].shape for n in SMALL_SHARDED]
        gs = all_gather(_pack128([w[n] for n in SMALL_SHARDED]), "gather_small")
        per_dev = [_unpack128(gs[d], shapes) for d in range(N_DEV)]
        self.small = {n: jnp.concatenate([per_dev[d][i] for d in range(N_DEV)], axis=-1) for i, n in enumerate(SMALL_SHARDED)}
        self.small.update({n: w[n] for n in REPLICATED})
        self.first = _full_from_slots(all_gather(_flat_shards(G_AB, w), "gather_ab"), G_AB, w)
        self.gathers, self.done, self.tokens, self.reductions = {}, {}, [], {}
        self.start_gather('l0a')
        self.start_gather('l0b', after=[self.gathers['l0a'][4]])

    def take_tokens(self):
        toks, self.tokens = self.tokens, []
        return toks

    def start_gather(self, key, after=()):
        group = GATHER_GROUPS[key]
        self.gathers[key] = exchange_start(_flat_shards(group, self.w), f"gather_{key}_start", False, after=after)
        self.tokens.append(self.gathers[key][4])

    def weights(self, key, after):
        if key == 'ab':
            return self.first
        handles = self.gathers[key]
        _, land, self.done[key] = exchange_wait(handles, after, f"gather_{key}_wait", False)
        land = lax.dynamic_update_slice(land, handles[2][None], (self.me, 0, 0))
        return _full_from_slots(land, GATHER_GROUPS[key], self.w)

    def put_grads(self, key, group, grads):
        send = _slots_from_full(grads, group, self.w)
        handles = exchange_start(send, f"reduce_{key}_start", True)
        self.reductions[key] = (group, handles)
        self.tokens.append(handles[4])

    def landed(self, key, after):
        group, handles = self.reductions[key]
        send, land, _ = exchange_wait(handles, after, f"reduce_{key}_wait", True)
        mine = lax.dynamic_slice_in_dim(send, self.me, 1, axis=0)
        land = lax.dynamic_update_slice(land, mine, (self.me, 0, 0))
        offs, off = {}, 0
        for n, layer, _, _, padded in _members(group, self.w):
            offs[(n, layer)] = off
            off += padded
        return land, offs

    def put_small(self, small_grads, loss_local):
        small = SMALL_SHARDED + REPLICATED
        self.small_shapes = [small_grads[n].shape for n in small] + [(1,)]
        packed = _pack128([small_grads[n] for n in small] + [loss_local.reshape(1)])
        self.small_handles = exchange_start(packed, "gather_small_grads_start", False)
        return self.small_handles[4]

    def reduced_small(self, after):
        small = SMALL_SHARDED + REPLICATED
        src, land, _ = exchange_wait(self.small_handles, after, "gather_small_grads_wait", False)
        gs = lax.dynamic_update_slice(land, src[None], (self.me, 0, 0))
        tot = _unpack128(sum_slots(gs, "sum_small", 1024), self.small_shapes)
        out = {}
        for n, g in zip(small, tot):
            if n in SMALL_SHARDED:
                width = self.w[n].shape[-1]
                g = lax.dynamic_slice_in_dim(g, self.me * width, width, axis=g.ndim - 1)
            out[n] = g
        return out, tot[-1].reshape(())


def local_step(x, mem, target, ex):
    bsz, seq, _ = x.shape
    t = bsz * seq
    nb = t // TB
    nc = seq // CHUNK
    x0 = x.reshape(t, D)
    mem2 = mem.reshape(bsz * N_MEM, D)
    tgt = target.reshape(t, D)
    p = ex.small
    gains = p['norm_gains']
    big = {}

    def gain(layer, i):
        g = gains[layer, i].reshape(1, D)
        for tok in ex.take_tokens():
            g = g + tok[0, 0]
        return g

    consts = _ssd_consts()
    grads = {}
    saved = [dict(), dict()]

    def matmul_res(a, b, name, xin, ga, gb):
        return matmul(a, b, 'nn', name, (F32, F32, BF), epilogue=res_epilogue, extras=[xin], params=[ga, gb])

    def attn_specs():
        nq = seq // TB
        q = pl.BlockSpec((TB, D), lambda b, i: (b * nq + i, 0))
        kv = pl.BlockSpec((N_MEM, 2 * D), lambda b, i: (b, 0))
        return (bsz, nq), q, kv

    def attention_fwd(layer, xin, hin, sv, ga, gb):
        q = matmul(hin, big[('xa_wq', layer)], 'nn', f"q_{layer}", BF)
        kv = matmul(mem2, big[('xa_wkv', layer)], 'nt', f"kv_{layer}", BF)
        grid, qs, kvs = attn_specs()
        o, = fwd_call(attn_fn, f"attn_{layer}", grid, [q, kv], [qs, kvs], [_sd((t, D), BF)], [qs])
        ao, x_next, h_next = matmul_res(o, big[('xa_wo', layer)], f"ao_{layer}", xin, ga, gb)
        sv.update(q=q, kv=kv, o=o, ao=ao)
        return ao, x_next, h_next

    def mlp_fwd(layer, hin, sv, res=None):
        r, rr = matmul(hin, big[('mlp_w1', layer)], 'nt', f"mlp1_{layer}", (BF, BF), epilogue=act_epilogue)
        if res is None:
            out = (matmul(rr, big[('mlp_w2', layer)], 'nn', f"mlp2_{layer}"),)
        else:
            out = matmul_res(rr, big[('mlp_w2', layer)], f"mlp2_{layer}", *res)
        sv.update(r=r, rr=rr, mo=out[0])
        return out

    sv = saved[0]
    h0, = fwd_call(seg_in, "norm_in", (nb,), [x0, gain(0, 0)], [_rows(D), _par(D)], [_sd((t, D), BF)], [_rows(D)])
    big.update(ex.weights('ab', h0))
    xbc0 = POOL_W + SSM_INNER
    w_ab_in = big[('ab_w_in', 0)]
    w_ab_in = _pad_rows(jnp.concatenate([w_ab_in[:xbc0], _xbc_group(w_ab_in[xbc0:xbc0 + SSM_CONV_DIM], 0),
                                         w_ab_in[xbc0 + SSM_CONV_DIM:]], axis=0), AB_IN_PAD)
    conv_w, conv_b = _xbc_group(p['ssm_conv_w'][0], 1), _xbc_group(p['ssm_conv_b'], 1)
    u0 = matmul(h0, w_ab_in, 'nt', "ab_in")
    pool_outs = []
    for g in range(POOL_GROUPS):
        seqspec = pl.BlockSpec((seq, PG), lambda b, g=g: (b, g))
        po, = fwd_call(make_pool_fn(g), f"pool_{g}", (bsz,), [u0, p['pool_w'][0, g], p['pool_scale']],
                       [seqspec, pl.BlockSpec((PG, PG), lambda b: (0, 0)), pl.BlockSpec((1, PG), lambda b, g=g: (0, g))],
                       [_sd((t, PG), BF)], [pl.BlockSpec((seq, PG), lambda b: (b, 0))])
        pool_outs.append(po)
    cw = 256
    ncb = SSM_CONV_DIM // cw
    cbase = (POOL_W + SSM_INNER) // cw
    conv_in_specs = [pl.BlockSpec((seq, cw), lambda j, b: (b, cbase + j)), pl.BlockSpec((SSM_CONV, cw), lambda j, b: (0, j)),
                     pl.BlockSpec((1, cw), lambda j, b: (0, j))]
    conv_out_spec = pl.BlockSpec((seq, cw), lambda j, b: (b, j))
    xbc_act, = fwd_call(conv4_fn, "ssm_conv", (ncb, bsz), [u0, conv_w, conv_b], conv_in_specs,
                        [_sd((t, SSM_CONV_DIM))], [conv_out_spec])
    dtb = jnp.pad(p['ssm_dt_bias'], ((0, 0), (0, LANE - SSM_HEADS)))
    alog = jnp.pad(p['ssm_a_log'], ((0, 0), (0, LANE - SSM_HEADS)))
    dsk = jnp.pad(p['ssm_d'], ((0, 0), (0, LANE - SSM_HEADS)))
    yn, hs = ssd_fwd(xbc_act, u0, dtb, alog, dsk, p['ssm_norm'], consts, bsz, seq)
    mix0 = jnp.concatenate(pool_outs + [yn], axis=1)
    m0, x1, h2 = matmul_res(mix0, big[('ab_w_out', 0)], "ab_out", x0, gain(0, 1), gain(0, 2))
    big.update(ex.weights('l0a', h2))
    ex.start_gather('cd', after=[ex.done['l0a']])
    ao0, x2, h3 = attention_fwd(0, x1, h2, sv, gain(0, 3), gain(0, 4))
    big.update(ex.weights('l0b', h3))
    mo0, x3, h4 = mlp_fwd(0, h3, sv, (x2, gain(0, 5), gain(1, 0)))
    big.update(ex.weights('cd', mo0))
    ex.start_gather('l1a', after=[ex.done['cd']])
    ex.start_gather('l1b', after=[ex.gathers['l1a'][4]])

    sv1 = saved[1]
    nd = D // LANE
    w_cd_in = big[('cd_w_in', 0)].reshape(5, nd, LANE, D).transpose(1, 0, 2, 3).reshape(CD_IN, D)
    u1 = matmul(h4, w_cd_in, 'nt', "cd_in")
    cd_par = [pl.BlockSpec((CONF_K, LANE), lambda j, b: (0, j)), pl.BlockSpec((1, LANE), lambda j, b: (0, j)),
              pl.BlockSpec((SC_K, LANE), lambda j, b: (0, j))]
    cd_ins = [u1, p['conf_dw_w'][0], p['conf_dw_b'], p['sc_conv_w'][0]]
    cd_u_spec = pl.BlockSpec((seq, 5 * LANE), lambda j, b: (b, j))
    cd_in_specs = [cd_u_spec] + cd_par
    cd_out_spec = pl.BlockSpec((seq, LANE), lambda j, b: (b, j))
    vconv, mix1 = fwd_call(cd1_fn, "cd_conv", (nd, bsz), cd_ins, cd_in_specs, [_sd((t, D)), _sd((t, CD_OUT), BF)],
                           [cd_out_spec, pl.BlockSpec((seq, LANE), lambda j, b: (b, nd + j))])
    mix1, = fwd_call(seg_ln, "conf_ln", (nb,), [vconv, p['conf_ln_g'], p['conf_ln_b']], [_rows(D), _par(D), _par(D)],
                     [_sd((t, CD_OUT), BF)], [_rows(D)], into=mix1)
    m1, x4, h5 = matmul_res(mix1, big[('cd_w_out', 0)], "cd_out", x3, gain(1, 1), gain(1, 2))
    big.update(ex.weights('l1a', h5))
    ao1, x5, h6 = attention_fwd(1, x4, h5, sv1, gain(1, 3), gain(1, 4))
    big.update(ex.weights('l1b', h6))
    mo1, = mlp_fwd(1, h6, sv1)

    def loss_body(x_ref, m_ref, g_ref, t_ref, dx_ref, dm_ref, dg_ref, acc_ref):
        (y,), vjp = jax.vjp(seg_out, x_ref[...], m_ref[...], g_ref[...])
        d = y - t_ref[...]
        dx, dm, dg = vjp((d / float(D),))
        dx_ref[...] = dx
        dm_ref[...] = dm.astype(dm_ref.dtype)

        @pl.when(pl.program_id(0) == 0)
        def _():
            acc_ref[...] = jnp.zeros_like(acc_ref)
            dg_ref[...] = jnp.zeros_like(dg_ref)

        acc_ref[...] += jnp.sum(d * d, axis=0, keepdims=True)
        dg_ref[...] += dg

    dx5, dmo1, dg15, lanes = pl.pallas_call(
        loss_body, name="loss_head", grid=(nb,), in_specs=[_rows(D), _rows(D), _par(D), _rows(D)],
        out_specs=[_rows(D), _rows(D), _par(D), _par(D)], out_shape=[_sd((t, D)), _sd((t, D), BF), _sd((1, D)), _sd((1, D))],
        compiler_params=_params())(x5, mo1, gain(1, 5), tgt)
    loss = 0.5 * jnp.sum(lanes) / float(D)

    gain_grads = {(1, 5): dg15}

    def matmul_res_bwd(a, b, mode, name, xin, m, ga, gb, dx1):
        return list(matmul(a, b, mode, name, (F32, BF), epilogue=res_bwd_epilogue, extras=[xin, m, dx1], params=[ga, gb], n_acc=2))

    def mlp_bwd(layer, hin, dmo, sv, res):
        grads_w2 = matmul(sv['rr'], dmo, 'tn', f"d_mlp_w2_{layer}", BF)
        dr, = matmul(dmo, big[('mlp_w2', layer)], 'nt', f"d_r_{layer}", (BF,), epilogue=act_bwd_epilogue, extras=[sv['r']])
        grads_w1 = matmul(dr, hin, 'tn', f"d_mlp_w1_{layer}", BF)
        return matmul_res_bwd(dr, big[('mlp_w1', layer)], 'nn', f"d_h_mlp_{layer}", *res) + [grads_w1, grads_w2]

    def attention_bwd(layer, hin, dao, sv, res):
        g_wo = matmul(sv['o'], dao, 'tn', f"d_xa_wo_{layer}", BF)
        do = matmul(dao, big[('xa_wo', layer)], 'nt', f"d_o_{layer}", BF)
        grid, qs, kvs = attn_specs()
        dq, dkv = bwd_call(attn_fn, f"d_attn_{layer}", grid, [sv['q'], sv['kv']], [qs, kvs], [do], [qs], [0, 1],
                           [_sd((t, D), BF), _sd((bsz * N_MEM, 2 * D))], [qs, kvs], [None, (1,)])
        g_wkv = matmul(dkv, mem2, 'tn', f"d_xa_wkv_{layer}", BF)
        g_wq = matmul(hin, dq, 'tn', f"d_xa_wq_{layer}", BF)
        return matmul_res_bwd(dq, big[('xa_wq', layer)], 'nt', f"d_h_attn_{layer}", *res) + [g_wq, g_wkv, g_wo]

    per_layer = {k: [None, None] for k in ('xa_wq', 'xa_wkv', 'xa_wo', 'mlp_w1', 'mlp_w2')}

    (dx4, dao1, gain_grads[(1, 3)], gain_grads[(1, 4)], per_layer['mlp_w1'][1],
     per_layer['mlp_w2'][1]) = mlp_bwd(1, h6, dmo1, sv1, (x4, ao1, gain(1, 3), gain(1, 4), dx5))
    (dx3, dm1, gain_grads[(1, 1)], gain_grads[(1, 2)], per_layer['xa_wq'][1], per_layer['xa_wkv'][1],
     per_layer['xa_wo'][1]) = attention_bwd(1, h5, dao1, sv1, (x3, m1, gain(1, 1), gain(1, 2), dx4))
    ex.put_grads('l1', G_L1, {(k, 1): v[1] for k, v in per_layer.items()})
    g_cd_out = matmul(mix1, dm1, 'tn', "d_cd_w_out", BF)
    dmix1 = matmul(dm1, big[('cd_w_out', 0)], 'nt', "d_mix1", after=ex.take_tokens())
    dvconv, dlg, dlb = bwd_call(seg_ln, "d_conf_ln", (nb,), [vconv, p['conf_ln_g'], p['conf_ln_b']],
                                [_rows(D), _par(D), _par(D)], [dmix1], [_rows(D, 0)], [0, 1, 2],
                                [_sd((t, D)), _sd((1, D)), _sd((1, D))], [_rows(D), _par(D), _par(D)], [None, (0,), (0,)])
    grads['conf_ln_g'], grads['conf_ln_b'] = dlg, dlb
    cd_g = bwd_call(cd1_fn, "d_cd_conv", (nd, bsz), cd_ins, cd_in_specs, [dvconv, dmix1],
                    [cd_out_spec, pl.BlockSpec((seq, LANE), lambda j, b: (b, nd + j))], list(range(4)),
                    [_sd((t, CD_IN), BF), _sd((CONF_K, D)), _sd((1, D)), _sd((SC_K, D))], [cd_u_spec] + cd_par,
                    [None, (1,), (1,), (1,)])
    du1 = cd_g[0]
    grads['conf_dw_w'], grads['conf_dw_b'], grads['sc_conv_w'] = cd_g[1][None], cd_g[2], cd_g[3][None]
    g_cd_in = matmul(du1, h4, 'tn', "d_cd_w_in", BF).reshape(nd, 5, LANE, D).transpose(1, 0, 2, 3).reshape(CD_IN, D)
    ex.put_grads('cd', G_CD, {('cd_w_in', 0): g_cd_in, ('cd_w_out', 0): g_cd_out})
    dx2, dmo0, gain_grads[(0, 5)], gain_grads[(1, 0)] = matmul_res_bwd(du1, w_cd_in, 'nn', "d_h_cd", x2, mo0, gain(0, 5),
                                                                       gain(1, 0), dx3)
    (dx1, dao0, gain_grads[(0, 3)], gain_grads[(0, 4)], per_layer['mlp_w1'][0],
     per_layer['mlp_w2'][0]) = mlp_bwd(0, h3, dmo0, sv, (x1, ao0, gain(0, 3), gain(0, 4), dx2))
    (dx0r, dm0, gain_grads[(0, 1)], gain_grads[(0, 2)], per_layer['xa_wq'][0], per_layer['xa_wkv'][0],
     per_layer['xa_wo'][0]) = attention_bwd(0, h2, dao0, sv, (x0, m0, gain(0, 1), gain(0, 2), dx1))
    ex.put_grads('l0', G_L0, {(k, 0): v[0] for k, v in per_layer.items()})
    g_ab_out = matmul(mix0, dm0, 'tn', "d_ab_w_out", BF)
    dmix0 = matmul(dm0, big[('ab_w_out', 0)], 'nt', "d_mix0", after=ex.take_tokens())
    dxbc_act, dz, ddt, ddtb, dalog, ddsk, dnw = ssd_bwd(xbc_act, u0, dtb, alog, dsk, p['ssm_norm'], consts, hs, dmix0, bsz, seq)
    grads['ssm_dt_bias'] = ddtb[:, :SSM_HEADS]
    grads['ssm_a_log'] = dalog[:, :SSM_HEADS]
    grads['ssm_d'] = ddsk[:, :SSM_HEADS]
    grads['ssm_norm'] = dnw
    dxr, dcw, dcb = bwd_call(conv4_fn, "d_ssm_conv", (ncb, bsz), [u0, conv_w, conv_b], conv_in_specs,
                             [dxbc_act], [conv_out_spec], [0, 1, 2],
                             [_sd((t, SSM_CONV_DIM), BF), _sd((SSM_CONV, SSM_CONV_DIM)), _sd((1, SSM_CONV_DIM))],
                             [conv_out_spec, conv_in_specs[1], conv_in_specs[2]], [None, (1,), (1,)])
    grads['ssm_conv_w'], grads['ssm_conv_b'] = _xbc_ungroup(dcw, 1)[None], _xbc_ungroup(dcb, 1)
    dpool, dpw, dps = [], [], []
    for g in range(POOL_GROUPS):
        seqspec = pl.BlockSpec((seq, PG), lambda b, g=g: (b, g))
        one = pl.BlockSpec((seq, PG), lambda b: (b, 0))
        wspec = pl.BlockSpec((PG, PG), lambda b: (0, 0))
        sspec = pl.BlockSpec((1, PG), lambda b, g=g: (0, g))
        a, bb, c = bwd_call(make_pool_fn(g), f"d_pool_{g}", (bsz,), [u0, p['pool_w'][0, g], p['pool_scale']],
                            [seqspec, wspec, sspec], [dmix0], [seqspec], [0, 1, 2],
                            [_sd((t, PG), BF), _sd((PG, PG)), _sd((1, PG))], [one, wspec, pl.BlockSpec((1, PG), lambda b: (0, 0))],
                            [None, (0,), (0,)])
        dpool.append(a)
        dpw.append(bb)
        dps.append(c)
    grads['pool_w'] = jnp.stack(dpw)[None]
    grads['pool_scale'] = jnp.concatenate(dps, axis=1)
    du0 = jnp.concatenate(dpool + [dz, dxr, ddt.astype(BF)], axis=1)
    g_ab_in = matmul(du0, h0, 'tn', "d_ab_w_in", BF)
    g_ab_in = jnp.concatenate([g_ab_in[:xbc0], _xbc_ungroup(g_ab_in[xbc0:xbc0 + SSM_CONV_DIM], 0),
                               g_ab_in[xbc0 + SSM_CONV_DIM:AB_IN]], axis=0)
    ex.put_grads('ab', G_AB, {('ab_w_in', 0): g_ab_in, ('ab_w_out', 0): g_ab_out})
    dx, dg00 = matmul(du0, w_ab_in, 'nn', "d_h_ab", (F32,), epilogue=in_bwd_epilogue, extras=[x0, dx0r], params=[gain(0, 0)],
                      after=ex.take_tokens(), n_acc=1)
    gain_grads[(0, 0)] = dg00
    grads['norm_gains'] = jnp.stack([jnp.concatenate([gain_grads[(l, i)] for i in range(6)], axis=0) for l in range(2)])
    return loss, dx, grads
```

```python
import functools
import math

import numpy as np
import jax
import jax.numpy as jnp
from jax import lax
from jax.experimental import pallas as pl
from jax.experimental.pallas import tpu as pltpu

BF = jnp.bfloat16
F32 = jnp.float32

N_DEV = 8
D = 1024
N_MEM = 256
XA_HEADS = 4
XA_DH = D // XA_HEADS
POOL_GROUPS = 4
PG = 128
POOL_W = POOL_GROUPS * PG
SSM_INNER = 1024
SSM_GROUPS = 2
SSM_GSZ = SSM_INNER // SSM_GROUPS
SSM_HEADS = 16
SSM_P = 64
SSM_N = 128
SSM_CONV = 4
SSM_CONV_DIM = SSM_INNER + 2 * SSM_GROUPS * SSM_N
SSM_XBC_G = SSM_GSZ + 2 * SSM_N
CHUNK = 128
AB_IN = POOL_W + SSM_INNER + SSM_CONV_DIM + SSM_HEADS
AB_IN_PAD = POOL_W + SSM_INNER + SSM_CONV_DIM + 128
AB_OUT = POOL_W + SSM_INNER
CONF_K = 31
SC_K = 3
CD_IN = 5 * D
CD_OUT = 2 * D
MLP_H = 4 * D
RMS_EPS = 1e-6
LN_EPS = 1e-5
ADAM_LR = 0.001
ADAM_B1 = 0.9
ADAM_B2 = 0.999
ADAM_EPS = 1e-08
ADAM_WD = 0.01
ADAM_STEP = 10
VMEM_LIMIT = 56 * 1024 * 1024
LANE = 128

NAMES = ['x', 'mem', 'norm_gains', 'xa_wq', 'xa_wkv', 'xa_wo', 'mlp_w1', 'mlp_w2', 'ab_w_in', 'pool_w', 'pool_scale',
         'ssm_conv_w', 'ssm_conv_b', 'ssm_dt_bias', 'ssm_a_log', 'ssm_d', 'ssm_norm', 'ab_w_out', 'cd_w_in', 'conf_dw_w',
         'conf_dw_b', 'conf_ln_g', 'conf_ln_b', 'sc_conv_w', 'cd_w_out', 'loss_target']
WEIGHTS = NAMES[2:25]
BIG = [('xa_wq', 1), ('xa_wkv', 2), ('xa_wo', 1), ('mlp_w1', 2), ('mlp_w2', 1), ('cd_w_in', 2), ('cd_w_out', 1),
       ('ab_w_out', 1), ('ab_w_in', 2)]
SMALL_SHARDED = ['norm_gains', 'ssm_conv_w', 'conf_dw_w', 'conf_dw_b', 'conf_ln_g', 'conf_ln_b', 'sc_conv_w']
REPLICATED = ['pool_w', 'pool_scale', 'ssm_conv_b', 'ssm_dt_bias', 'ssm_a_log', 'ssm_d', 'ssm_norm']


def _dg(a, b, ca, cb, prec=None):
    return lax.dot_general(a, b, (((ca,), (cb,)), ((), ())), precision=prec, preferred_element_type=F32)


@functools.partial(jax.custom_vjp, nondiff_argnums=(2, 3))
def bdot(a, b, ca, cb):
    return _dg(a.astype(BF), b.astype(BF), ca, cb)


def _bdot_fwd(a, b, ca, cb):
    return bdot(a, b, ca, cb), (a, b)


def _bdot_bwd(ca, cb, res, g):
    a, b = res
    g16, a16, b16 = g.astype(BF), a.astype(BF), b.astype(BF)
    da = _dg(g16, b16, 1, 1 - cb) if ca == 1 else _dg(b16, g16, 1 - cb, 1)
    db = _dg(g16, a16, 0, 1 - ca) if cb == 1 else _dg(a16, g16, 1 - ca, 0)
    return da.astype(a.dtype), db.astype(b.dtype)


bdot.defvjp(_bdot_fwd, _bdot_bwd)


def _split3(a):
    a1 = a.astype(BF)
    r1 = a - a1.astype(F32)
    a2 = r1.astype(BF)
    a3 = (r1 - a2.astype(F32)).astype(BF)
    return a1, a2, a3


def _exact_right(a, c):
    m = a.shape[0]
    if m % 16:
        return sum(_dg(p, c, 1, 0) for p in _split3(a))
    o = _dg(jnp.concatenate(_split3(a), axis=0), c, 1, 0)
    return o[:m] + o[m:2 * m] + o[2 * m:]


def _exact_left(c, a):
    n = a.shape[1]
    o = _dg(c, jnp.concatenate(_split3(a), axis=1), 1, 0)
    return o[:, :n] + o[:, n:2 * n] + o[:, 2 * n:]


@jax.custom_vjp
def cmat(a, c, ct):
    return _exact_right(a, c)


def _cmat_fwd(a, c, ct):
    return cmat(a, c, ct), (c, ct)


def _cmat_bwd(res, g):
    c, ct = res
    return _exact_right(g, ct), jnp.zeros_like(c), jnp.zeros_like(ct)


cmat.defvjp(_cmat_fwd, _cmat_bwd)


@jax.custom_vjp
def cmatl(c, ct, a):
    return _exact_left(c, a)


def _cmatl_fwd(c, ct, a):
    return cmatl(c, ct, a), (c, ct)


def _cmatl_bwd(res, g):
    c, ct = res
    return jnp.zeros_like(c), jnp.zeros_like(ct), _exact_left(ct, g)


cmatl.defvjp(_cmatl_fwd, _cmatl_bwd)


SUBLANES = 8


def _taps(x, shifts, down):
    n, c = x.shape
    pad = _round_up(max(shifts), SUBLANES)
    if pad == 0:
        return {0: x}
    zeros = jnp.zeros((pad, c), x.dtype)
    xp = jnp.concatenate([zeros, x] if down else [x, zeros], axis=0)
    rolled, out = {0: xp}, {}
    for s in shifts:
        a, b = divmod(s, SUBLANES)
        if b not in rolled:
            rolled[b] = pltpu.roll(xp, b if down else n + pad - b, 0)
        off = pad - SUBLANES * a if down else SUBLANES * a
        out[s] = rolled[b][off:off + n]
    return out


def _shift_down(x, k):
    return _taps(x, [k], True)[k]


def _shift_up(x, k):
    return _taps(x, [k], False)[k]


@functools.partial(jax.custom_vjp, nondiff_argnums=(1,))
def shift(x, k):
    return _shift_down(x, k)


def _shift_fwd(x, k):
    return _shift_down(x, k), None


def _shift_bwd(k, _, g):
    return (_shift_up(g, k),)


shift.defvjp(_shift_fwd, _shift_bwd)


@functools.partial(jax.custom_vjp, nondiff_argnums=(2,))
def cconv(u, w, width):
    taps = _taps(u, list(range(width)), True)
    acc = u * w[width - 1:width, :]
    for k in range(width - 1):
        acc = acc + taps[width - 1 - k] * w[k:k + 1, :]
    return acc


def _cconv_fwd(u, w, width):
    return cconv(u, w, width), (u, w)


def _cconv_bwd(width, res, g):
    u, w = res
    rows = lax.broadcasted_iota(jnp.int32, w.shape, 0)
    du = g * w[width - 1:width, :]
    dw = jnp.where(rows == width - 1, jnp.sum(g * u, axis=0, keepdims=True), 0.0)
    g_taps = _taps(g, list(range(width)), False)
    u_taps = _taps(u, list(range(width)), True)
    for k in range(width - 1):
        s = width - 1 - k
        du = du + g_taps[s] * w[k:k + 1, :]
        dw = dw + jnp.where(rows == k, jnp.sum(g * u_taps[s], axis=0, keepdims=True), 0.0)
    return du, dw


cconv.defvjp(_cconv_fwd, _cconv_bwd)


def _rms(x, g):
    return x * lax.rsqrt(jnp.mean(x * x, axis=-1, keepdims=True) + RMS_EPS) * g


def _params(sem=None):
    return pltpu.CompilerParams(dimension_semantics=sem, vmem_limit_bytes=VMEM_LIMIT)


def _f32(v):
    return v if v.dtype == F32 else v.astype(F32)


def _first(axes):
    ok = None
    for ax in axes:
        c = pl.program_id(ax) == 0
        ok = c if ok is None else jnp.logical_and(ok, c)
    return ok


def fwd_call(fn, name, grid, ins, in_specs, out_shapes, out_specs, into=None):
    n_in = len(ins)
    n_into = 0 if into is None else 1

    def body(*refs):
        outs = fn(*[_f32(r[...]) for r in refs[:n_in]])
        for r, o in zip(refs[n_in + n_into:], outs):
            r[...] = o.astype(r.dtype)

    extra = [] if into is None else [into]
    return pl.pallas_call(body, name=name, grid=grid, in_specs=list(in_specs) + [pl.BlockSpec(memory_space=pl.ANY)] * n_into,
                          out_specs=out_specs, out_shape=out_shapes, input_output_aliases={n_in: 0} if n_into else {},
                          compiler_params=_params())(*ins, *extra)


def bwd_call(fn, name, grid, ins, in_specs, cots, cot_specs, gidx, g_shapes, g_specs, g_acc):
    n_in, n_cot = len(ins), len(cots)

    def body(*refs):
        vals = [_f32(r[...]) for r in refs[:n_in]]

        def f_sel(*dv):
            full = list(vals)
            for i, v in zip(gidx, dv):
                full[i] = v
            return tuple(fn(*full))

        outs, vjp = jax.vjp(f_sel, *[vals[i] for i in gidx])
        cts = tuple(_f32(r[...]) for r in refs[n_in:n_in + n_cot])
        grads = vjp(cts)
        for r, g, acc in zip(refs[n_in + n_cot:], grads, g_acc):
            if acc is None:
                r[...] = g.astype(r.dtype)
            else:
                @pl.when(_first(acc))
                def _():
                    r[...] = jnp.zeros_like(r)

                r[...] += g.astype(r.dtype)

    return pl.pallas_call(body, name=name, grid=grid, in_specs=list(in_specs) + list(cot_specs), out_specs=g_specs,
                          out_shape=g_shapes, compiler_params=_params())(*ins, *cots)


def _tile(dim, pref):
    if dim <= pref:
        return dim
    best = None
    for t in range(LANE, pref + 1, LANE):
        if dim % t == 0:
            best = t
    assert best is not None, dim
    return best


MATMUL_VMEM_BUDGET = 40 * 1024 * 1024


def _matmul_tiles(m, n, k, a_bytes, b_bytes, out_bytes):
    tn = _tile(n, 1024)
    for tk_pref in (k, 2048, 1024, 512):
        tk = _tile(k, tk_pref)
        for tm_pref in (1024, 512, 256):
            tm = _tile(m, tm_pref)
            need = 2 * (tm * tk * a_bytes + tk * tn * b_bytes + tm * tn * out_bytes) + (0 if tk == k else tm * tn * 4)
            need += (tm * tk * 2 if a_bytes == 4 else 0) + (tk * tn * 2 if b_bytes == 4 else 0)
            if need <= MATMUL_VMEM_BUDGET:
                return tm, tn, tk
    raise ValueError((m, n, k))


def matmul(a, b, mode, name, out_dtype=F32, epilogue=None, extras=(), params=(), after=(), n_acc=0):
    if mode == 'nn':
        (m, k), (k2, n) = a.shape, b.shape
    elif mode == 'nt':
        (m, k), (n, k2) = a.shape, b.shape
    else:
        (k, m), (k2, n) = a.shape, b.shape
    assert k == k2, (name, a.shape, b.shape)
    n_extra = len(extras) + len(params)
    out_dtypes = out_dtype if isinstance(out_dtype, tuple) else (out_dtype,)
    per_out = sum(jnp.dtype(dt).itemsize for dt in out_dtypes) + sum(e.dtype.itemsize for e in extras)
    tm, tn, tk = _matmul_tiles(m, n, k, a.dtype.itemsize, b.dtype.itemsize, per_out)
    nk = k // tk
    ca = 0 if mode == 'tn' else 1
    cb = 1 if mode == 'nt' else 0
    a_spec = pl.BlockSpec((tk, tm), lambda i, j, kk: (kk, i)) if mode == 'tn' else pl.BlockSpec((tm, tk), lambda i, j, kk: (i, kk))
    b_spec = pl.BlockSpec((tn, tk), lambda i, j, kk: (j, kk)) if mode == 'nt' else pl.BlockSpec((tk, tn), lambda i, j, kk: (kk, j))

    def finish(o_refs, extra_refs, acc, first_row_tile):
        outs = (acc,) if epilogue is None else epilogue(acc, *[_f32(e[...]) for e in extra_refs])
        n_tile = len(o_refs) - n_acc
        for o_ref, o in zip(o_refs[:n_tile], outs[:n_tile]):
            o_ref[...] = o.astype(o_ref.dtype)
        for o_ref, o in zip(o_refs[n_tile:], outs[n_tile:]):
            o_ref[...] = jnp.where(first_row_tile, o, o_ref[...] + o)

    n_after = len(after)

    def body_whole_k(a_ref, b_ref, *refs):
        refs = refs[n_after:]
        finish(refs[n_extra:], refs[:n_extra], _dg(a_ref[...].astype(BF), b_ref[...].astype(BF), ca, cb), pl.program_id(0) == 0)

    def body_split_k(a_ref, b_ref, *refs):
        refs = refs[n_after:]
        extra_refs, o_refs, acc = refs[:n_extra], refs[n_extra:-1], refs[-1]
        kk = pl.program_id(2)
        first_row_tile = pl.program_id(0) == 0

        @pl.when(kk == 0)
        def _():
            acc[...] = jnp.zeros_like(acc)

        acc[...] += _dg(a_ref[...].astype(BF), b_ref[...].astype(BF), ca, cb)

        @pl.when(kk == nk - 1)
        def _():
            finish(o_refs, extra_refs, acc[...], first_row_tile)

    tile = pl.BlockSpec((tm, tn), lambda i, j, kk: (i, j))
    row = pl.BlockSpec((1, tn), lambda i, j, kk: (0, j))
    n_par = len(params)
    outs = pl.pallas_call(
        body_whole_k if nk == 1 else body_split_k, name=name, grid=(m // tm, n // tn, nk),
        in_specs=[a_spec, b_spec] + [pl.BlockSpec(memory_space=pl.ANY)] * n_after + [tile] * len(extras) + [row] * n_par,
        out_specs=[tile] * len(out_dtypes) + [row] * n_acc,
        out_shape=[jax.ShapeDtypeStruct((m, n), dt) for dt in out_dtypes] + [jax.ShapeDtypeStruct((1, n), F32)] * n_acc,
        scratch_shapes=[] if nk == 1 else [pltpu.VMEM((tm, tn), F32)],
        compiler_params=_params(("arbitrary",) * 3 if n_acc else ("parallel", "parallel", "arbitrary")))(a, b, *after, *extras, *params)
    return outs if isinstance(out_dtype, tuple) or n_acc else outs[0]


_FLIPS = [(0, 0, 1), (1, 0, 0), (0, 1, 0), (1, 1, 0), (1, 0, 1), (0, 1, 1), (1, 1, 1)]


def _me():
    return lax.axis_index("x"), lax.axis_index("y"), lax.axis_index("c")


def _flip(pos, f):
    return tuple(jnp.where(fi == 1, 1 - p, p) if fi else p for p, fi in zip(pos, f))


def _slot(pos):
    return 4 * pos[0] + 2 * pos[1] + pos[2]


def all_gather(v, name):
    def body(v_ref, out_ref, send_sems, recv_sems, local_sem):
        me = _me()
        sibling = _flip(me, (0, 0, 1))
        chips = [_flip(me, f) for f in ((1, 0, 0), (0, 1, 0), (1, 1, 0))]

        def copy(k, block, to, src=None):
            return pltpu.make_async_remote_copy(
                src_ref=out_ref.at[_slot(block)] if src is None else src, dst_ref=out_ref.at[_slot(block)],
                send_sem=send_sems.at[k], recv_sem=recv_sems.at[k], device_id=to, device_id_type=pl.DeviceIdType.MESH)

        mine = pltpu.make_async_copy(v_ref, out_ref.at[_slot(me)], local_sem)
        mine.start()
        first = [copy(0, me, sibling, src=v_ref)] + [copy(1 + j, me, chip, src=v_ref) for j, chip in enumerate(chips)]
        for cp in first:
            cp.start()
        passed = [copy(4 + j, chip, sibling) for j, chip in enumerate(chips)]
        for j, chip in enumerate(chips):
            copy(1 + j, chip, me).wait_recv()
            passed[j].start()
        copy(0, sibling, me).wait_recv()
        for j, chip in enumerate(chips):
            copy(4 + j, _flip(chip, (0, 0, 1)), me).wait_recv()
        for cp in first + passed:
            cp.wait_send()
        mine.wait()

    return pl.pallas_call(
        body, name=name, out_shape=jax.ShapeDtypeStruct((N_DEV,) + v.shape, v.dtype),
        in_specs=[pl.BlockSpec(memory_space=pl.ANY)], out_specs=pl.BlockSpec(memory_space=pl.ANY),
        scratch_shapes=[pltpu.SemaphoreType.DMA((7,)), pltpu.SemaphoreType.DMA((7,)), pltpu.SemaphoreType.DMA(())],
    )(v)


def sum_slots(v, name, tr=256):
    _, r, c = v.shape
    tr = _tile_rows(r, tr)

    def body(v_ref, o_ref):
        acc = v_ref[0].astype(F32)
        for s in range(1, N_DEV):
            acc = acc + v_ref[s].astype(F32)
        o_ref[...] = acc

    return pl.pallas_call(body, name=name, grid=(r // tr,), in_specs=[pl.BlockSpec((N_DEV, tr, c), lambda i: (0, i, 0))],
                          out_specs=pl.BlockSpec((tr, c), lambda i: (i, 0)), out_shape=jax.ShapeDtypeStruct((r, c), F32),
                          compiler_params=_params())(v)


def _tile_rows(r, pref):
    if r <= pref:
        return r
    best = None
    for t in range(8, pref + 1, 8):
        if r % t == 0:
            best = t
    return r if best is None else best


def _adamw_math(w, m, v, g):
    nm = ADAM_B1 * m + (1.0 - ADAM_B1) * g
    nv = ADAM_B2 * v + (1.0 - ADAM_B2) * jnp.square(g)
    m_hat = nm / (1.0 - ADAM_B1 ** ADAM_STEP)
    v_hat = nv / (1.0 - ADAM_B2 ** ADAM_STEP)
    return -ADAM_LR * (m_hat / (jnp.sqrt(v_hat) + ADAM_EPS) + ADAM_WD * w), nm, nv


def update_from_slots(lands, offs, w, m, v, transposed, name):
    layers, a, b = w.shape
    n_land = len(lands)
    if transposed:
        rb, tk = LANE, 512
        assert a % tk == 0 and b % rb == 0 and all(o % rb == 0 for o in offs), (name, w.shape, offs)
        grid = (layers, a // tk, b // rb)
        land_block = (N_DEV, rb, tk)
        tile = pl.BlockSpec((None, tk, rb), lambda l, i, j: (l, i, j))

        def land_spec(layer):
            base = offs[layer] // rb
            return pl.BlockSpec(land_block, lambda l, i, j: (0, base + jnp.where(l == layer, j, 0), jnp.where(l == layer, i, 0)))
    else:
        fits = [t for t in (256, 128, 64) if a % t == 0 and all(o % t == 0 for o in offs)]
        assert fits or all(o == 0 for o in offs), (name, w.shape, offs)
        tr = max(fits) if fits else a
        grid = (layers, a // tr)
        land_block = (N_DEV, _round_up(tr, MEMBER_ROW_TILE), b)
        tile = pl.BlockSpec((None, tr, b), lambda l, i: (l, i, 0))

        def land_spec(layer):
            base = offs[layer] // tr
            return pl.BlockSpec(land_block, lambda l, i: (0, base + jnp.where(l == layer, i, 0), 0))

    def body(*refs):
        land_refs, (w_ref, m_ref, v_ref, g_ref, d_ref, nm_ref, nv_ref, acc) = refs[:n_land], refs[n_land:]
        for layer, land in enumerate(land_refs):
            @pl.when(pl.program_id(0) == layer)
            def _(land=land):
                rows = acc.shape[0]
                s = land[0, :rows].astype(F32)
                for k in range(1, N_DEV):
                    s = s + land[k, :rows].astype(F32)
                acc[...] = s

        g = acc[...].T if transposed else acc[...]
        d, nm, nv = _adamw_math(w_ref[...], m_ref[...], v_ref[...], g)
        g_ref[...] = g
        d_ref[...] = d
        nm_ref[...] = nm
        nv_ref[...] = nv

    sh = jax.ShapeDtypeStruct(w.shape, F32)
    return pl.pallas_call(
        body, name=name, grid=grid, in_specs=[land_spec(layer) for layer in range(n_land)] + [tile] * 3, out_specs=[tile] * 4,
        out_shape=[sh] * 4, scratch_shapes=[pltpu.VMEM((rb, tk) if transposed else (tr, b), F32)],
        compiler_params=_params())(*lands, w, m, v)


def adamw_many(ws, ms, vs, gs, name):
    n = len(ws)

    def body(*refs):
        for i in range(n):
            d, nm, nv = _adamw_math(refs[i][...], refs[n + i][...], refs[2 * n + i][...], refs[3 * n + i][...])
            refs[4 * n + i][...] = d
            refs[5 * n + i][...] = nm
            refs[6 * n + i][...] = nv

    vmem = pl.BlockSpec(memory_space=pltpu.VMEM)
    shapes = [jax.ShapeDtypeStruct(a.shape, F32) for a in ws]
    res = pl.pallas_call(body, name=name, in_specs=[vmem] * (4 * n), out_specs=[vmem] * (3 * n), out_shape=shapes * 3,
                         compiler_params=_params())(*ws, *ms, *vs, *gs)
    return res[:n], res[n:2 * n], res[2 * n:]


def seg_in(x, g):
    return (_rms(x, g),)


def seg_in_res(x, g):
    return x, _rms(x, g)


def seg_res(x, m, ga, gb):
    x1 = x + _rms(m, ga)
    return x1, _rms(x1, gb)


def seg_out(x, m, ga):
    return (x + _rms(m, ga),)


def act_epilogue(r):
    t = jnp.maximum(r, 0.0)
    return r, t * t


def res_epilogue(m, x, ga, gb):
    x1, h = seg_res(x, m, ga, gb)
    return m, x1, h


def res_bwd_epilogue(dh, x, m, dx1, ga, gb):
    _, vjp = jax.vjp(seg_res, x, m, ga, gb)
    return vjp((dx1, dh))


def in_bwd_epilogue(dh, x, dx_res, g):
    _, vjp = jax.vjp(seg_in_res, x, g)
    return vjp((dx_res, dh))


def loss_epilogue(mo, x, target, g):
    (y,), vjp = jax.vjp(seg_out, x, mo, g)
    d = y - target
    dx, dm, dg = vjp((d / float(D),))
    return dx, dm, dg, jnp.sum(d * d, axis=0, keepdims=True)


def act_bwd_epilogue(drr, r):
    return (drr * (2.0 * jnp.maximum(r, 0.0)),)


def seg_ln(v, g, b):
    mu = jnp.mean(v, axis=-1, keepdims=True)
    var = jnp.mean(jnp.square(v - mu), axis=-1, keepdims=True)
    vn = (v - mu) * lax.rsqrt(var + LN_EPS) * g + b
    return (jax.nn.silu(vn),)


def make_pool_fn(group):
    window = 2 ** (group + 1)

    def pool_fn(ug, pw, scale):
        s = ug
        for lvl in range(group + 1):
            s = s + shift(s, 2 ** lvl)
        cnt = jnp.minimum(lax.broadcasted_iota(jnp.int32, ug.shape, 0) + 1, window).astype(F32)
        return (bdot(s / cnt - ug, pw, 1, 0) * scale,)

    return pool_fn


def conv4_fn(xr, w, b):
    return (jax.nn.silu(cconv(xr, w, SSM_CONV) + b),)


def cd1_fn(u, dww, dwb, scw):
    val, gate, bg, cg, hh = (u[:, k * LANE:(k + 1) * LANE] for k in range(5))
    v = val * jax.nn.sigmoid(gate)
    vc = cconv(v, dww, CONF_K) + dwb
    sc = bg * cconv(cg * hh, scw, SC_K)
    return vc, sc


def attn_fn(q, kv):
    outs = []
    for h in range(XA_HEADS):
        cols = slice(h * XA_DH, (h + 1) * XA_DH)
        s = bdot(q[:, cols], kv[:, cols], 1, 1) / math.sqrt(XA_DH)
        p = jax.nn.softmax(s, axis=-1)
        outs.append(bdot(p, kv[:, D + h * XA_DH:D + (h + 1) * XA_DH], 1, 0))
    return (jnp.concatenate(outs, axis=1),)


def ssd_chunk(xbc, z, dtraw, dtb, alog, dsk, nw, h0, h1, h2, h3, e64, e64t, ecat, ecatt, tril, trilt):
    xs, bm, cm = xbc[:, :SSM_GSZ], xbc[:, SSM_GSZ:SSM_GSZ + SSM_N], xbc[:, SSM_GSZ + SSM_N:]
    hin = (h0, h1, h2, h3)
    dt = jax.nn.softplus(dtraw + dtb)
    a = -jnp.exp(alog)
    d_a = dt * a
    cs = cmatl(tril, trilt, d_a)
    cs_cat = cmat(cs, ecat, ecatt)
    cs64, cs128 = cs_cat[:, :SSM_GSZ], cs_cat[:, SSM_GSZ:]
    dt64 = cmat(dt, e64, e64t)
    row = lax.broadcasted_iota(jnp.int32, (8, LANE), 0)
    heads = jnp.where(row == 0, dsk, jnp.where(row == 1, jnp.sum(d_a, axis=0, keepdims=True), 0.0))
    heads64 = cmat(heads, e64, e64t)
    d64, tot64 = heads64[0:1, :], heads64[1:2, :]
    xdt = xs * dt64
    cb = bdot(cm, bm, 1, 1)
    li = lax.broadcasted_iota(jnp.int32, (CHUNK, CHUNK), 0)
    si = lax.broadcasted_iota(jnp.int32, (CHUNK, CHUNK), 1)
    causal = li >= si
    lane = lax.broadcasted_iota(jnp.int32, (CHUNK, LANE), 1)
    xw = xdt * jnp.exp(tot64 - cs64)
    ecs = jnp.exp(cs64)
    etot = jnp.exp(tot64)
    ycols, hout = [], []
    for j in range(4):
        sl = slice(j * LANE, (j + 1) * LANE)
        xj = xdt[:, sl]
        ys = []
        for hh in range(2):
            r = 2 * j + hh
            col = cs128[:, r * LANE:(r + 1) * LANE]
            decay = jnp.exp(jnp.where(causal, col - col.T, -1e30))
            ys.append(bdot(cb * decay, xj, 1, 0))
        y_diag = jnp.where(lane < SSM_P, ys[0], ys[1])
        y_off = bdot(cm, hin[j], 1, 0) * ecs[:, sl]
        ycols.append(y_diag + y_off)
        hout.append(etot[:, sl] * hin[j] + bdot(bm, xw[:, sl], 0, 0))
    y = jnp.concatenate(ycols, axis=1) + d64 * xs
    y = y * jax.nn.silu(z)
    yn = y * lax.rsqrt(jnp.mean(y * y, axis=-1, keepdims=True) + RMS_EPS) * nw
    return (yn,) + tuple(hout)


def _xbc_group(a, axis):
    parts = []
    for g in range(SSM_GROUPS):
        for start, width in ((g * SSM_GSZ, SSM_GSZ), (SSM_INNER + g * SSM_N, SSM_N), (SSM_INNER + (SSM_GROUPS + g) * SSM_N, SSM_N)):
            parts.append(lax.slice_in_dim(a, start, start + width, axis=axis))
    return jnp.concatenate(parts, axis=axis)


def _xbc_ungroup(a, axis):
    xs, bs, cs = [], [], []
    for g in range(SSM_GROUPS):
        base = g * SSM_XBC_G
        xs.append(lax.slice_in_dim(a, base, base + SSM_GSZ, axis=axis))
        bs.append(lax.slice_in_dim(a, base + SSM_GSZ, base + SSM_GSZ + SSM_N, axis=axis))
        cs.append(lax.slice_in_dim(a, base + SSM_GSZ + SSM_N, base + SSM_XBC_G, axis=axis))
    return jnp.concatenate(xs + bs + cs, axis=axis)


def _ssd_consts():
    h = np.arange(LANE)[:, None]
    e64 = np.stack([(h == g * 8 + np.arange(SSM_GSZ)[None, :] // SSM_P) for g in range(SSM_GROUPS)]).astype(np.float32)
    e128 = np.stack([(h == g * 8 + np.arange(8 * LANE)[None, :] // LANE) for g in range(SSM_GROUPS)]).astype(np.float32)
    ecat = np.concatenate([e64, e128], axis=2)
    tril = np.tril(np.ones((CHUNK, CHUNK), np.float32))
    return tuple(jnp.asarray(c, dtype=BF) for c in (e64, e64.transpose(0, 2, 1), ecat, ecat.transpose(0, 2, 1), tril, tril.T))


def _ssd_specs(nc, rev):
    def ci(c):
        return nc - 1 - c if rev else c

    def row(width, col):
        return pl.BlockSpec((CHUNK, width), lambda b, c: (b * nc + ci(c), col))

    def whole(shape):
        return pl.BlockSpec(shape, lambda b, c: (0,) * len(shape))

    data = [row(SSM_CONV_DIM, 0),
            row(SSM_GSZ, 1), row(SSM_GSZ, 2), row(LANE, 24)]
    par = [whole((1, LANE))] * 3 + [whole((1, SSM_INNER))]
    cst = [whole((SSM_GROUPS, LANE, SSM_GSZ)), whole((SSM_GROUPS, SSM_GSZ, LANE)), whole((SSM_GROUPS, LANE, 12 * LANE)),
           whole((SSM_GROUPS, 12 * LANE, LANE)), whole((CHUNK, CHUNK)), whole((CHUNK, CHUNK))]
    hsave = pl.BlockSpec((None, None, SSM_GROUPS, 4, SSM_N, LANE), lambda b, c: (b, ci(c), 0, 0, 0, 0))
    return data, par, cst, hsave, row, whole


def _ssd_group_args(g, xbc, z, dtr, dtb, alog, dsk, nw):
    return (xbc[:, g * SSM_XBC_G:(g + 1) * SSM_XBC_G], z[g], dtr, dtb, alog, dsk, nw[:, g * SSM_GSZ:(g + 1) * SSM_GSZ])


def ssd_fwd(xbc_act, u, dtb, alog, dsk, nw, consts, bsz, seq):
    nc = seq // CHUNK
    data, par, cst, hsave, row, _ = _ssd_specs(nc, False)

    def body(xbc, z0, z1, dtr, dtb_r, alog_r, dsk_r, nw_r, e64, e64t, ecat, ecatt, tril, trilt, yn_ref, hs_ref, h):
        @pl.when(pl.program_id(1) == 0)
        def _():
            h[...] = jnp.zeros_like(h)

        hs_ref[...] = h[...]
        ys = []
        for g in range(SSM_GROUPS):
            args = _ssd_group_args(g, xbc[...], (z0[...], z1[...]), dtr[...], dtb_r[...], alog_r[...], dsk_r[...], nw_r[...])
            outs = ssd_chunk(*args, h[g, 0], h[g, 1], h[g, 2], h[g, 3], e64[g], e64t[g], ecat[g], ecatt[g], tril[...], trilt[...])
            ys.append(outs[0])
            for j in range(4):
                h[g, j] = outs[1 + j]
        yn_ref[...] = jnp.concatenate(ys, axis=1).astype(yn_ref.dtype)

    t = bsz * seq
    return pl.pallas_call(
        body, name="ssd_fwd", grid=(bsz, nc), in_specs=data + par + cst, out_specs=[row(SSM_INNER, 0), hsave],
        out_shape=[jax.ShapeDtypeStruct((t, SSM_INNER), BF), jax.ShapeDtypeStruct((bsz, nc, SSM_GROUPS, 4, SSM_N, LANE), F32)],
        scratch_shapes=[pltpu.VMEM((SSM_GROUPS, 4, SSM_N, LANE), F32)], compiler_params=_params(),
    )(xbc_act, u, u, u, dtb, alog, dsk, nw, *consts)


def ssd_bwd(xbc_act, u, dtb, alog, dsk, nw, consts, hs, dmix, bsz, seq):
    nc = seq // CHUNK
    data, par, cst, hsave, row, whole = _ssd_specs(nc, True)
    t = bsz * seq
    pcol = POOL_W // SSM_GSZ

    def body(xbc, z0, z1, dtr, dtb_r, alog_r, dsk_r, nw_r, e64, e64t, ecat, ecatt, tril, trilt, hs_ref, dy0, dy1,
             dxbc, dz, ddt, ddtb, dalog, ddsk, dnw, dh):
        @pl.when(pl.program_id(1) == 0)
        def _():
            dh[...] = jnp.zeros_like(dh)

        per_group = []
        for g, dyn in enumerate((dy0, dy1)):
            cst_vals = (e64[g], e64t[g], ecat[g], ecatt[g], tril[...], trilt[...])
            prim = _ssd_group_args(g, xbc[...], (z0[...], z1[...]), dtr[...], dtb_r[...], alog_r[...], dsk_r[...], nw_r[...])
            prim = prim + (hs_ref[g, 0], hs_ref[g, 1], hs_ref[g, 2], hs_ref[g, 3])
            _, vjp = jax.vjp(lambda *args, c=cst_vals: ssd_chunk(*args, *c), *prim)
            gr = vjp((dyn[...].astype(F32), dh[g, 0], dh[g, 1], dh[g, 2], dh[g, 3]))
            for j in range(4):
                dh[g, j] = gr[7 + j]
            per_group.append(gr)
        g0, g1 = per_group
        dxbc[...] = jnp.concatenate([g0[0], g1[0]], axis=1)
        dz[...] = jnp.concatenate([g0[1], g1[1]], axis=1).astype(dz.dtype)
        ddt[...] = g0[2] + g1[2]

        @pl.when(_first((0, 1)))
        def _():
            for r in (ddtb, dalog, ddsk, dnw):
                r[...] = jnp.zeros_like(r)

        ddtb[...] += g0[3] + g1[3]
        dalog[...] += g0[4] + g1[4]
        ddsk[...] += g0[5] + g1[5]
        dnw[...] += jnp.concatenate([g0[6], g1[6]], axis=1)

    out_specs = [row(SSM_CONV_DIM, 0), row(SSM_INNER, 0), row(LANE, 0), whole((1, LANE)), whole((1, LANE)), whole((1, LANE)),
                 whole((1, SSM_INNER))]
    lane = jax.ShapeDtypeStruct((1, LANE), F32)
    out_shape = [jax.ShapeDtypeStruct((t, SSM_CONV_DIM), F32), jax.ShapeDtypeStruct((t, SSM_INNER), BF),
                 jax.ShapeDtypeStruct((t, LANE), F32), lane, lane, lane, jax.ShapeDtypeStruct((1, SSM_INNER), F32)]
    return pl.pallas_call(
        body, name="ssd_bwd", grid=(bsz, nc), in_specs=data + par + cst + [hsave, row(SSM_GSZ, pcol), row(SSM_GSZ, pcol + 1)],
        out_specs=out_specs, out_shape=out_shape, scratch_shapes=[pltpu.VMEM((SSM_GROUPS, 4, SSM_N, LANE), F32)],
        compiler_params=_params(),
    )(xbc_act, u, u, u, dtb, alog, dsk, nw, *consts, hs, dmix, dmix)


TB = 512


def _rows(d, col=0):
    return pl.BlockSpec((TB, d), lambda i: (i, col))


def _par(d):
    return pl.BlockSpec((1, d), lambda i: (0, 0))


def _sd(shape, dtype=F32):
    return jax.ShapeDtypeStruct(shape, dtype)


def _round_up(n, m):
    return -(-n // m) * m


def _pad_rows(a, rows):
    return jnp.pad(a, ((0, rows - a.shape[0]), (0, 0)))


def _pack128(arrs):
    flat = jnp.concatenate([a.reshape(-1) for a in arrs])
    n = flat.shape[0]
    rows = -(-n // (8 * LANE)) * 8
    return jnp.pad(flat, (0, rows * LANE - n)).reshape(rows, LANE)


def _unpack128(packed, shapes):
    flat = packed.reshape(-1)
    out, off = [], 0
    for s in shapes:
        n = int(np.prod(s))
        out.append(flat[off:off + n].reshape(s))
        off += n
    return out


def kernel(x, mem, norm_gains, xa_wq, xa_wkv, xa_wo, mlp_w1, mlp_w2, ab_w_in, pool_w, pool_scale, ssm_conv_w, ssm_conv_b, ssm_dt_bias, ssm_a_log, ssm_d, ssm_norm, ab_w_out, cd_w_in, conf_dw_w, conf_dw_b, conf_ln_g, conf_ln_b, sc_conv_w, cd_w_out, loss_target, m_norm_gains, m_xa_wq, m_xa_wkv, m_xa_wo, m_mlp_w1, m_mlp_w2, m_ab_w_in, m_pool_w, m_pool_scale, m_ssm_conv_w, m_ssm_conv_b, m_ssm_dt_bias, m_ssm_a_log, m_ssm_d, m_ssm_norm, m_ab_w_out, m_cd_w_in, m_conf_dw_w, m_conf_dw_b, m_conf_ln_g, m_conf_ln_b, m_sc_conv_w, m_cd_w_out, v_norm_gains, v_xa_wq, v_xa_wkv, v_xa_wo, v_mlp_w1, v_mlp_w2, v_ab_w_in, v_pool_w, v_pool_scale, v_ssm_conv_w, v_ssm_conv_b, v_ssm_dt_bias, v_ssm_a_log, v_ssm_d, v_ssm_norm, v_ab_w_out, v_cd_w_in, v_conf_dw_w, v_conf_dw_b, v_conf_ln_g, v_conf_ln_b, v_sc_conv_w, v_cd_w_out):
    args = locals()
    w = {n: args[n] for n in WEIGHTS}
    mom_m = {n: args["m_" + n] for n in WEIGHTS}
    mom_v = {n: args["v_" + n] for n in WEIGHTS}
    ex = Exchange(w)
    loss_local, grad_x, small_grads = local_step(x, mem, loss_target, ex)
    outs = {}

    started = ex.put_small(small_grads, loss_local)
    landed = {key: ex.landed(key, started) for key in ('l1', 'cd', 'l0')}
    late = []
    for n, keys in (('mlp_w1', ('l0', 'l1')), ('mlp_w2', ('l0', 'l1')), ('xa_wkv', ('l0', 'l1')), ('xa_wq', ('l0', 'l1')),
                    ('xa_wo', ('l0', 'l1')), ('cd_w_in', ('cd',)), ('cd_w_out', ('cd',))):
        lands = [landed[key][0] for key in keys]
        offs = [landed[key][1][(n, layer)] for layer, key in enumerate(keys)]
        outs[n] = update_from_slots(lands, offs, w[n], mom_m[n], mom_v[n], SHARD_AXIS[n] == 2, "update_" + n)
        late.append(outs[n][1])
    g_own, loss = ex.reduced_small(late)
    land_ab, offs_ab = ex.landed('ab', late)
    outs['ab_w_out'] = update_from_slots([land_ab], [offs_ab[('ab_w_out', 0)]], w['ab_w_out'], mom_m['ab_w_out'],
                                         mom_v['ab_w_out'], False, "update_ab_w_out")
    res = update_from_slots([land_ab], [offs_ab[('ab_w_in', 0)]], jnp.swapaxes(w['ab_w_in'], 1, 2), jnp.swapaxes(mom_m['ab_w_in'], 1, 2),
                            jnp.swapaxes(mom_v['ab_w_in'], 1, 2), False, "update_ab_w_in")
    outs['ab_w_in'] = tuple(jnp.swapaxes(r, 1, 2) for r in res)
    small = SMALL_SHARDED + REPLICATED
    upd = adamw_many([w[n] for n in small], [mom_m[n] for n in small], [mom_v[n] for n in small], [g_own[n] for n in small],
                     "adamw_small")
    for i, n in enumerate(small):
        outs[n] = (g_own[n], upd[0][i], upd[1][i], upd[2][i])
    return (loss, grad_x.reshape(x.shape), *[outs[n][0] for n in WEIGHTS], *[outs[n][1] for n in WEIGHTS],
            *[outs[n][2] for n in WEIGHTS], *[outs[n][3] for n in WEIGHTS])


G_AB = (('ab_w_in', 0), ('ab_w_out', 0))
G_L0 = (('xa_wq', 0), ('xa_wkv', 0), ('xa_wo', 0), ('mlp_w1', 0), ('mlp_w2', 0))
G_L1 = (('xa_wq', 1), ('xa_wkv', 1), ('xa_wo', 1), ('mlp_w1', 1), ('mlp_w2', 1))
G_CD = (('cd_w_in', 0), ('cd_w_out', 0))
GATHER_GROUPS = {'ab': G_AB[:1], 'l0a': G_AB[1:] + G_L0[:3], 'l0b': G_L0[3:], 'cd': G_CD, 'l1a': G_L1[:3], 'l1b': G_L1[3:]}
SHARD_AXIS = dict(BIG)
MEMBER_ROW_TILE = 64
FLAT_ROW_TILE = 128


def _members(group, w):
    out = []
    for n, layer in group:
        shp = w[n].shape[1:]
        if SHARD_AXIS[n] == 2:
            shp = (shp[1], shp[0])
        assert shp[1] == D, (n, shp)
        out.append((n, layer, shp, shp[0], _round_up(shp[0], MEMBER_ROW_TILE)))
    return out


def _group_rows(group, w):
    return _round_up(sum(m[4] for m in _members(group, w)), FLAT_ROW_TILE)


def _flat_shards(group, w):
    parts = []
    for n, layer, _, _, padded in _members(group, w):
        shard = w[n][layer].astype(BF)
        parts.append(_pad_rows(shard.T if SHARD_AXIS[n] == 2 else shard, padded))
    return _pad_rows(jnp.concatenate(parts, axis=0), _group_rows(group, w))


def _full_from_slots(land, group, w):
    out, off = {}, 0
    for n, layer, shp, rows, padded in _members(group, w):
        out[(n, layer)] = land[:, off:off + rows].reshape(N_DEV * rows, D)
        off += padded
    return out


def _slots_from_full(grads, group, w):
    parts = []
    for n, layer, shp, rows, padded in _members(group, w):
        blk = grads[(n, layer)].astype(BF).reshape(N_DEV, rows, D)
        parts.append(jnp.pad(blk, ((0, 0), (0, padded - rows), (0, 0))))
    send = jnp.concatenate(parts, axis=1)
    return jnp.pad(send, ((0, 0), (0, _group_rows(group, w) - send.shape[1]), (0, 0)))


_HBM = pl.BlockSpec(memory_space=pltpu.HBM)
_SEM = pl.BlockSpec(memory_space=pltpu.SEMAPHORE)
_ANY = pl.BlockSpec(memory_space=pl.ANY)


def _peer_copy(k, src, dst, send_sems, recv_sems, peer):
    return pltpu.make_async_remote_copy(src_ref=src, dst_ref=dst, send_sem=send_sems.at[k], recv_sem=recv_sems.at[k],
                                        device_id=peer, device_id_type=pl.DeviceIdType.MESH)


def exchange_start(src, name, scatter, after=()):
    shape = src.shape[-2:]
    after = list(after)

    def body(src_ref, land_ref, *rest):
        send_sems, recv_sems, token = rest[len(after)], rest[len(after) + 1], rest[-1]
        me = _me()
        for k, f in enumerate(_FLIPS):
            peer = _flip(me, f)
            piece = src_ref.at[_slot(peer)] if scatter else src_ref
            _peer_copy(k, piece, land_ref.at[_slot(me)], send_sems, recv_sems, peer).start()
        token[...] = jnp.zeros_like(token)

    land = pltpu.with_memory_space_constraint(lax.empty((N_DEV,) + shape, src.dtype), pltpu.HBM)
    return pl.pallas_call(
        body, name=name,
        out_shape=(pltpu.SemaphoreType.DMA((7,)), pltpu.SemaphoreType.DMA((7,)), pltpu.HBM(src.shape, src.dtype),
                   pltpu.HBM((N_DEV,) + shape, src.dtype), jax.ShapeDtypeStruct((8, LANE), F32)),
        in_specs=(_HBM, _HBM) + (_ANY,) * len(after), out_specs=(_SEM, _SEM, _HBM, _HBM, pl.BlockSpec(memory_space=pltpu.VMEM)),
        input_output_aliases={0: 2, 1: 3},
        compiler_params=pltpu.CompilerParams(has_side_effects=pltpu.SideEffectType.DATAFLOW_SIDE_EFFECTING),
    )(pltpu.with_memory_space_constraint(src, pltpu.HBM), land, *after)


def exchange_wait(handles, after, name, scatter):
    send_sems, recv_sems, src_thru, land_thru, _ = handles
    after = list(after) if isinstance(after, (list, tuple)) else [after]

    def body(src_ref, land_ref, send_sems, recv_sems, *rest):
        token = rest[-1]
        me = _me()
        for k, f in enumerate(_FLIPS):
            peer = _flip(me, f)
            piece = src_ref.at[_slot(peer)] if scatter else src_ref
            cp = _peer_copy(k, piece, land_ref.at[_slot(peer)], send_sems, recv_sems, peer)
            cp.wait_send()
            cp.wait_recv()
        token[...] = jnp.zeros_like(token)

    return pl.pallas_call(
        body, name=name, out_shape=(pltpu.HBM(src_thru.shape, src_thru.dtype), pltpu.HBM(land_thru.shape, land_thru.dtype),
                                    jax.ShapeDtypeStruct((8, LANE), F32)),
        in_specs=(_HBM, _HBM, _SEM, _SEM) + (_ANY,) * len(after), out_specs=(_HBM, _HBM, pl.BlockSpec(memory_space=pltpu.VMEM)),
        input_output_aliases={0: 0, 1: 1},
        compiler_params=pltpu.CompilerParams(has_side_effects=pltpu.SideEffectType.DATAFLOW_SIDE_EFFECTING),
    )(src_thru, land_thru, send_sems, recv_sems, *after)


class Exchange:
    def __init__(self, w):
        self.w = w
        self.me = _slot(_me())
        shapes = [w[n].shape for n in SMALL_SHARDED]
        gs = all_gather(_pack128([w[n] for n in SMALL_SHARDED]), "gather_small")
        per_dev = [_unpack128(gs[d], shapes) for d in range(N_DEV)]
        self.small = {n: jnp.concatenate([per_dev[d][i] for d in range(N_DEV)], axis=-1) for i, n in enumerate(SMALL_SHARDED)}
        self.small.update({n: w[n] for n in REPLICATED})
        self.first = _full_from_slots(all_gather(_flat_shards(GATHER_GROUPS['ab'], w), "gather_ab"), GATHER_GROUPS['ab'], w)
        self.gathers, self.done, self.tokens, self.reductions = {}, {}, [], {}
        self.start_gather('l0a')
        self.start_gather('l0b', after=[self.gathers['l0a'][4]])

    def take_tokens(self):
        toks, self.tokens = self.tokens, []
        return toks

    def start_gather(self, key, after=()):
        group = GATHER_GROUPS[key]
        self.gathers[key] = exchange_start(_flat_shards(group, self.w), f"gather_{key}_start", False, after=after)
        self.tokens.append(self.gathers[key][4])

    def weights(self, key, after):
        if key == 'ab':
            return self.first
        handles = self.gathers[key]
        _, land, self.done[key] = exchange_wait(handles, after, f"gather_{key}_wait", False)
        land = lax.dynamic_update_slice(land, handles[2][None], (self.me, 0, 0))
        return _full_from_slots(land, GATHER_GROUPS[key], self.w)

    def put_grads(self, key, group, grads):
        send = _slots_from_full(grads, group, self.w)
        handles = exchange_start(send, f"reduce_{key}_start", True)
        self.reductions[key] = (group, handles)
        self.tokens.append(handles[4])

    def landed(self, key, after):
        group, handles = self.reductions[key]
        send, land, _ = exchange_wait(handles, after, f"reduce_{key}_wait", True)
        mine = lax.dynamic_slice_in_dim(send, self.me, 1, axis=0)
        land = lax.dynamic_update_slice(land, mine, (self.me, 0, 0))
        offs, off = {}, 0
        for n, layer, _, _, padded in _members(group, self.w):
            offs[(n, layer)] = off
            off += padded
        return land, offs

    def put_small(self, small_grads, loss_local):
        small = SMALL_SHARDED + REPLICATED
        self.small_shapes = [small_grads[n].shape for n in small] + [(1,)]
        packed = _pack128([small_grads[n] for n in small] + [loss_local.reshape(1)])
        self.small_handles = exchange_start(packed, "gather_small_grads_start", False)
        return self.small_handles[4]

    def reduced_small(self, after):
        small = SMALL_SHARDED + REPLICATED
        src, land, _ = exchange_wait(self.small_handles, after, "gather_small_grads_wait", False)
        gs = lax.dynamic_update_slice(land, src[None], (self.me, 0, 0))
        tot = _unpack128(sum_slots(gs, "sum_small", 1024), self.small_shapes)
        out = {}
        for n, g in zip(small, tot):
            if n in SMALL_SHARDED:
                width = self.w[n].shape[-1]
                g = lax.dynamic_slice_in_dim(g, self.me * width, width, axis=g.ndim - 1)
            out[n] = g
        return out, tot[-1].reshape(())


def local_step(x, mem, target, ex):
    bsz, seq, _ = x.shape
    t = bsz * seq
    nb = t // TB
    nc = seq // CHUNK
    x0 = x.reshape(t, D)
    mem2 = mem.reshape(bsz * N_MEM, D)
    tgt = target.reshape(t, D)
    p = ex.small
    gains = p['norm_gains']
    big = {}

    def gain(layer, i):
        g = gains[layer, i].reshape(1, D)
        for tok in ex.take_tokens():
            g = g + tok[0, 0]
        return g

    consts = _ssd_consts()
    grads = {}
    saved = [dict(), dict()]

    def matmul_res(a, b, name, xin, ga, gb):
        return matmul(a, b, 'nn', name, (F32, F32, BF), epilogue=res_epilogue, extras=[xin], params=[ga, gb])

    def attn_specs():
        nq = seq // TB
        q = pl.BlockSpec((TB, D), lambda b, i: (b * nq + i, 0))
        kv = pl.BlockSpec((N_MEM, 2 * D), lambda b, i: (b, 0))
        return (bsz, nq), q, kv

    def attention_fwd(layer, xin, hin, sv, ga, gb):
        q = matmul(hin, big[('xa_wq', layer)], 'nn', f"q_{layer}", BF)
        kv = matmul(mem2, big[('xa_wkv', layer)], 'nt', f"kv_{layer}", BF)
        grid, qs, kvs = attn_specs()
        o, = fwd_call(attn_fn, f"attn_{layer}", grid, [q, kv], [qs, kvs], [_sd((t, D), BF)], [qs])
        ao, x_next, h_next = matmul_res(o, big[('xa_wo', layer)], f"ao_{layer}", xin, ga, gb)
        sv.update(q=q, kv=kv, o=o, ao=ao)
        return ao, x_next, h_next

    def mlp_fwd(layer, hin, sv, res):
        r, rr = matmul(hin, big[('mlp_w1', layer)], 'nt', f"mlp1_{layer}", (BF, BF), epilogue=act_epilogue)
        out = matmul_res(rr, big[('mlp_w2', layer)], f"mlp2_{layer}", *res)
        sv.update(r=r, rr=rr, mo=out[0])
        return out

    sv = saved[0]
    h0, = fwd_call(seg_in, "norm_in", (nb,), [x0, gain(0, 0)], [_rows(D), _par(D)], [_sd((t, D), BF)], [_rows(D)])
    big.update(ex.weights('ab', h0))
    xbc0 = POOL_W + SSM_INNER
    w_ab_in = big[('ab_w_in', 0)]
    w_ab_in = _pad_rows(jnp.concatenate([w_ab_in[:xbc0], _xbc_group(w_ab_in[xbc0:xbc0 + SSM_CONV_DIM], 0),
                                         w_ab_in[xbc0 + SSM_CONV_DIM:]], axis=0), AB_IN_PAD)
    conv_w, conv_b = _xbc_group(p['ssm_conv_w'][0], 1), _xbc_group(p['ssm_conv_b'], 1)
    u0 = matmul(h0, w_ab_in, 'nt', "ab_in")
    pool_outs = []
    for g in range(POOL_GROUPS):
        seqspec = pl.BlockSpec((seq, PG), lambda b, g=g: (b, g))
        po, = fwd_call(make_pool_fn(g), f"pool_{g}", (bsz,), [u0, p['pool_w'][0, g], p['pool_scale']],
                       [seqspec, pl.BlockSpec((PG, PG), lambda b: (0, 0)), pl.BlockSpec((1, PG), lambda b, g=g: (0, g))],
                       [_sd((t, PG), BF)], [pl.BlockSpec((seq, PG), lambda b: (b, 0))])
        pool_outs.append(po)
    cw = 256
    ncb = SSM_CONV_DIM // cw
    cbase = (POOL_W + SSM_INNER) // cw
    conv_in_specs = [pl.BlockSpec((seq, cw), lambda j, b: (b, cbase + j)), pl.BlockSpec((SSM_CONV, cw), lambda j, b: (0, j)),
                     pl.BlockSpec((1, cw), lambda j, b: (0, j))]
    conv_out_spec = pl.BlockSpec((seq, cw), lambda j, b: (b, j))
    xbc_act, = fwd_call(conv4_fn, "ssm_conv", (ncb, bsz), [u0, conv_w, conv_b], conv_in_specs,
                        [_sd((t, SSM_CONV_DIM))], [conv_out_spec])
    dtb = jnp.pad(p['ssm_dt_bias'], ((0, 0), (0, LANE - SSM_HEADS)))
    alog = jnp.pad(p['ssm_a_log'], ((0, 0), (0, LANE - SSM_HEADS)))
    dsk = jnp.pad(p['ssm_d'], ((0, 0), (0, LANE - SSM_HEADS)))
    yn, hs = ssd_fwd(xbc_act, u0, dtb, alog, dsk, p['ssm_norm'], consts, bsz, seq)
    mix0 = jnp.concatenate(pool_outs + [yn], axis=1)
    big.update(ex.weights('l0a', yn))
    ex.start_gather('cd', after=[ex.done['l0a']])
    m0, x1, h2 = matmul_res(mix0, big[('ab_w_out', 0)], "ab_out", x0, gain(0, 1), gain(0, 2))
    ao0, x2, h3 = attention_fwd(0, x1, h2, sv, gain(0, 3), gain(0, 4))
    big.update(ex.weights('l0b', h3))
    mo0, x3, h4 = mlp_fwd(0, h3, sv, (x2, gain(0, 5), gain(1, 0)))
    big.update(ex.weights('cd', mo0))
    ex.start_gather('l1a', after=[ex.done['cd']])
    ex.start_gather('l1b', after=[ex.gathers['l1a'][4]])

    sv1 = saved[1]
    nd = D // LANE
    w_cd_in = big[('cd_w_in', 0)].reshape(5, nd, LANE, D).transpose(1, 0, 2, 3).reshape(CD_IN, D)
    u1 = matmul(h4, w_cd_in, 'nt', "cd_in")
    cd_par = [pl.BlockSpec((CONF_K, LANE), lambda j, b: (0, j)), pl.BlockSpec((1, LANE), lambda j, b: (0, j)),
              pl.BlockSpec((SC_K, LANE), lambda j, b: (0, j))]
    cd_ins = [u1, p['conf_dw_w'][0], p['conf_dw_b'], p['sc_conv_w'][0]]
    cd_u_spec = pl.BlockSpec((seq, 5 * LANE), lambda j, b: (b, j))
    cd_in_specs = [cd_u_spec] + cd_par
    cd_out_spec = pl.BlockSpec((seq, LANE), lambda j, b: (b, j))
    vconv, mix1 = fwd_call(cd1_fn, "cd_conv", (nd, bsz), cd_ins, cd_in_specs, [_sd((t, D)), _sd((t, CD_OUT), BF)],
                           [cd_out_spec, pl.BlockSpec((seq, LANE), lambda j, b: (b, nd + j))])
    mix1, = fwd_call(seg_ln, "conf_ln", (nb,), [vconv, p['conf_ln_g'], p['conf_ln_b']], [_rows(D), _par(D), _par(D)],
                     [_sd((t, CD_OUT), BF)], [_rows(D)], into=mix1)
    m1, x4, h5 = matmul_res(mix1, big[('cd_w_out', 0)], "cd_out", x3, gain(1, 1), gain(1, 2))
    big.update(ex.weights('l1a', h5))
    ao1, x5, h6 = attention_fwd(1, x4, h5, sv1, gain(1, 3), gain(1, 4))
    big.update(ex.weights('l1b', h6))
    r1, rr1 = matmul(h6, big[('mlp_w1', 1)], 'nt', "mlp1_1", (BF, BF), epilogue=act_epilogue)
    sv1.update(r=r1, rr=rr1)
    dx5, dmo1, dg15, lanes = matmul(rr1, big[('mlp_w2', 1)], 'nn', "mlp2_1", (F32, BF), epilogue=loss_epilogue, extras=[x5, tgt],
                                    params=[gain(1, 5)], n_acc=2)
    loss = 0.5 * jnp.sum(lanes) / float(D)

    gain_grads = {(1, 5): dg15}

    def matmul_res_bwd(a, b, mode, name, xin, m, ga, gb, dx1):
        return list(matmul(a, b, mode, name, (F32, BF), epilogue=res_bwd_epilogue, extras=[xin, m, dx1], params=[ga, gb], n_acc=2))

    def mlp_bwd(layer, hin, dmo, sv, res):
        grads_w2 = matmul(sv['rr'], dmo, 'tn', f"d_mlp_w2_{layer}", BF)
        dr, = matmul(dmo, big[('mlp_w2', layer)], 'nt', f"d_r_{layer}", (BF,), epilogue=act_bwd_epilogue, extras=[sv['r']])
        grads_w1 = matmul(dr, hin, 'tn', f"d_mlp_w1_{layer}", BF)
        return matmul_res_bwd(dr, big[('mlp_w1', layer)], 'nn', f"d_h_mlp_{layer}", *res) + [grads_w1, grads_w2]

    def attention_bwd(layer, hin, dao, sv, res):
        g_wo = matmul(sv['o'], dao, 'tn', f"d_xa_wo_{layer}", BF)
        do = matmul(dao, big[('xa_wo', layer)], 'nt', f"d_o_{layer}", BF)
        grid, qs, kvs = attn_specs()
        dq, dkv = bwd_call(attn_fn, f"d_attn_{layer}", grid, [sv['q'], sv['kv']], [qs, kvs], [do], [qs], [0, 1],
                           [_sd((t, D), BF), _sd((bsz * N_MEM, 2 * D))], [qs, kvs], [None, (1,)])
        g_wkv = matmul(dkv, mem2, 'tn', f"d_xa_wkv_{layer}", BF)
        g_wq = matmul(hin, dq, 'tn', f"d_xa_wq_{layer}", BF)
        return matmul_res_bwd(dq, big[('xa_wq', layer)], 'nt', f"d_h_attn_{layer}", *res) + [g_wq, g_wkv, g_wo]

    per_layer = {k: [None, None] for k in ('xa_wq', 'xa_wkv', 'xa_wo', 'mlp_w1', 'mlp_w2')}

    (dx4, dao1, gain_grads[(1, 3)], gain_grads[(1, 4)], per_layer['mlp_w1'][1],
     per_layer['mlp_w2'][1]) = mlp_bwd(1, h6, dmo1, sv1, (x4, ao1, gain(1, 3), gain(1, 4), dx5))
    (dx3, dm1, gain_grads[(1, 1)], gain_grads[(1, 2)], per_layer['xa_wq'][1], per_layer['xa_wkv'][1],
     per_layer['xa_wo'][1]) = attention_bwd(1, h5, dao1, sv1, (x3, m1, gain(1, 1), gain(1, 2), dx4))
    ex.put_grads('l1', G_L1, {(k, 1): v[1] for k, v in per_layer.items()})
    g_cd_out = matmul(mix1, dm1, 'tn', "d_cd_w_out", BF)
    dmix1 = matmul(dm1, big[('cd_w_out', 0)], 'nt', "d_mix1", after=ex.take_tokens())
    dvconv, dlg, dlb = bwd_call(seg_ln, "d_conf_ln", (nb,), [vconv, p['conf_ln_g'], p['conf_ln_b']],
                                [_rows(D), _par(D), _par(D)], [dmix1], [_rows(D, 0)], [0, 1, 2],
                                [_sd((t, D)), _sd((1, D)), _sd((1, D))], [_rows(D), _par(D), _par(D)], [None, (0,), (0,)])
    grads['conf_ln_g'], grads['conf_ln_b'] = dlg, dlb
    cd_g = bwd_call(cd1_fn, "d_cd_conv", (nd, bsz), cd_ins, cd_in_specs, [dvconv, dmix1],
                    [cd_out_spec, pl.BlockSpec((seq, LANE), lambda j, b: (b, nd + j))], list(range(4)),
                    [_sd((t, CD_IN), BF), _sd((CONF_K, D)), _sd((1, D)), _sd((SC_K, D))], [cd_u_spec] + cd_par,
                    [None, (1,), (1,), (1,)])
    du1 = cd_g[0]
    grads['conf_dw_w'], grads['conf_dw_b'], grads['sc_conv_w'] = cd_g[1][None], cd_g[2], cd_g[3][None]
    g_cd_in = matmul(du1, h4, 'tn', "d_cd_w_in", BF).reshape(nd, 5, LANE, D).transpose(1, 0, 2, 3).reshape(CD_IN, D)
    ex.put_grads('cd', G_CD, {('cd_w_in', 0): g_cd_in, ('cd_w_out', 0): g_cd_out})
    dx2, dmo0, gain_grads[(0, 5)], gain_grads[(1, 0)] = matmul_res_bwd(du1, w_cd_in, 'nn', "d_h_cd", x2, mo0, gain(0, 5),
                                                                       gain(1, 0), dx3)
    (dx1, dao0, gain_grads[(0, 3)], gain_grads[(0, 4)], per_layer['mlp_w1'][0],
     per_layer['mlp_w2'][0]) = mlp_bwd(0, h3, dmo0, sv, (x1, ao0, gain(0, 3), gain(0, 4), dx2))
    (dx0r, dm0, gain_grads[(0, 1)], gain_grads[(0, 2)], per_layer['xa_wq'][0], per_layer['xa_wkv'][0],
     per_layer['xa_wo'][0]) = attention_bwd(0, h2, dao0, sv, (x0, m0, gain(0, 1), gain(0, 2), dx1))
    ex.put_grads('l0', G_L0, {(k, 0): v[0] for k, v in per_layer.items()})
    g_ab_out = matmul(mix0, dm0, 'tn', "d_ab_w_out", BF)
    dmix0 = matmul(dm0, big[('ab_w_out', 0)], 'nt', "d_mix0", after=ex.take_tokens())
    dxbc_act, dz, ddt, ddtb, dalog, ddsk, dnw = ssd_bwd(xbc_act, u0, dtb, alog, dsk, p['ssm_norm'], consts, hs, dmix0, bsz, seq)
    grads['ssm_dt_bias'] = ddtb[:, :SSM_HEADS]
    grads['ssm_a_log'] = dalog[:, :SSM_HEADS]
    grads['ssm_d'] = ddsk[:, :SSM_HEADS]
    grads['ssm_norm'] = dnw
    dxr, dcw, dcb = bwd_call(conv4_fn, "d_ssm_conv", (ncb, bsz), [u0, conv_w, conv_b], conv_in_specs,
                             [dxbc_act], [conv_out_spec], [0, 1, 2],
                             [_sd((t, SSM_CONV_DIM), BF), _sd((SSM_CONV, SSM_CONV_DIM)), _sd((1, SSM_CONV_DIM))],
                             [conv_out_spec, conv_in_specs[1], conv_in_specs[2]], [None, (1,), (1,)])
    grads['ssm_conv_w'], grads['ssm_conv_b'] = _xbc_ungroup(dcw, 1)[None], _xbc_ungroup(dcb, 1)
    dpool, dpw, dps = [], [], []
    for g in range(POOL_GROUPS):
        seqspec = pl.BlockSpec((seq, PG), lambda b, g=g: (b, g))
        one = pl.BlockSpec((seq, PG), lambda b: (b, 0))
        wspec = pl.BlockSpec((PG, PG), lambda b: (0, 0))
        sspec = pl.BlockSpec((1, PG), lambda b, g=g: (0, g))
        a, bb, c = bwd_call(make_pool_fn(g), f"d_pool_{g}", (bsz,), [u0, p['pool_w'][0, g], p['pool_scale']],
                            [seqspec, wspec, sspec], [dmix0], [seqspec], [0, 1, 2],
                            [_sd((t, PG), BF), _sd((PG, PG)), _sd((1, PG))], [one, wspec, pl.BlockSpec((1, PG), lambda b: (0, 0))],
                            [None, (0,), (0,)])
        dpool.append(a)
        dpw.append(bb)
        dps.append(c)
    grads['pool_w'] = jnp.stack(dpw)[None]
    grads['pool_scale'] = jnp.concatenate(dps, axis=1)
    du0 = jnp.concatenate(dpool + [dz, dxr, ddt.astype(BF)], axis=1)
    g_ab_in = matmul(du0, h0, 'tn', "d_ab_w_in", BF)
    g_ab_in = jnp.concatenate([g_ab_in[:xbc0], _xbc_ungroup(g_ab_in[xbc0:xbc0 + SSM_CONV_DIM], 0),
                               g_ab_in[xbc0 + SSM_CONV_DIM:AB_IN]], axis=0)
    ex.put_grads('ab', G_AB, {('ab_w_in', 0): g_ab_in, ('ab_w_out', 0): g_ab_out})
    dx, dg00 = matmul(du0, w_ab_in, 'nn', "d_h_ab", (F32,), epilogue=in_bwd_epilogue, extras=[x0, dx0r], params=[gain(0, 0)],
                      after=ex.take_tokens(), n_acc=1)
    gain_grads[(0, 0)] = dg00
    grads['norm_gains'] = jnp.stack([jnp.concatenate([gain_grads[(l, i)] for i in range(6)], axis=0) for l in range(2)])
    return loss, dx, grads
```

```python
import functools
import math

import numpy as np
import jax
import jax.numpy as jnp
from jax import lax
from jax.experimental import pallas as pl
from jax.experimental.pallas import tpu as pltpu

BF = jnp.bfloat16
F32 = jnp.float32

N_DEV = 8
D = 1024
N_MEM = 256
XA_HEADS = 4
XA_DH = D // XA_HEADS
POOL_GROUPS = 4
PG = 128
POOL_W = POOL_GROUPS * PG
SSM_INNER = 1024
SSM_GROUPS = 2
SSM_GSZ = SSM_INNER // SSM_GROUPS
SSM_HEADS = 16
SSM_P = 64
SSM_N = 128
SSM_CONV = 4
SSM_CONV_DIM = SSM_INNER + 2 * SSM_GROUPS * SSM_N
SSM_XBC_G = SSM_GSZ + 2 * SSM_N
CHUNK = 128
AB_IN = POOL_W + SSM_INNER + SSM_CONV_DIM + SSM_HEADS
AB_IN_PAD = POOL_W + SSM_INNER + SSM_CONV_DIM + 128
AB_OUT = POOL_W + SSM_INNER
CONF_K = 31
SC_K = 3
CD_IN = 5 * D
CD_OUT = 2 * D
MLP_H = 4 * D
RMS_EPS = 1e-6
LN_EPS = 1e-5
ADAM_LR = 0.001
ADAM_B1 = 0.9
ADAM_B2 = 0.999
ADAM_EPS = 1e-08
ADAM_WD = 0.01
ADAM_STEP = 10
VMEM_LIMIT = 56 * 1024 * 1024
LANE = 128

NAMES = ['x', 'mem', 'norm_gains', 'xa_wq', 'xa_wkv', 'xa_wo', 'mlp_w1', 'mlp_w2', 'ab_w_in', 'pool_w', 'pool_scale',
         'ssm_conv_w', 'ssm_conv_b', 'ssm_dt_bias', 'ssm_a_log', 'ssm_d', 'ssm_norm', 'ab_w_out', 'cd_w_in', 'conf_dw_w',
         'conf_dw_b', 'conf_ln_g', 'conf_ln_b', 'sc_conv_w', 'cd_w_out', 'loss_target']
WEIGHTS = NAMES[2:25]
BIG = [('xa_wq', 1), ('xa_wkv', 2), ('xa_wo', 1), ('mlp_w1', 2), ('mlp_w2', 1), ('cd_w_in', 2), ('cd_w_out', 1),
       ('ab_w_out', 1), ('ab_w_in', 2)]
SMALL_SHARDED = ['norm_gains', 'ssm_conv_w', 'conf_dw_w', 'conf_dw_b', 'conf_ln_g', 'conf_ln_b', 'sc_conv_w']
REPLICATED = ['pool_w', 'pool_scale', 'ssm_conv_b', 'ssm_dt_bias', 'ssm_a_log', 'ssm_d', 'ssm_norm']


def _dg(a, b, ca, cb, prec=None):
    return lax.dot_general(a, b, (((ca,), (cb,)), ((), ())), precision=prec, preferred_element_type=F32)


@functools.partial(jax.custom_vjp, nondiff_argnums=(2, 3))
def bdot(a, b, ca, cb):
    return _dg(a.astype(BF), b.astype(BF), ca, cb)


def _bdot_fwd(a, b, ca, cb):
    return bdot(a, b, ca, cb), (a, b)


def _bdot_bwd(ca, cb, res, g):
    a, b = res
    g16, a16, b16 = g.astype(BF), a.astype(BF), b.astype(BF)
    da = _dg(g16, b16, 1, 1 - cb) if ca == 1 else _dg(b16, g16, 1 - cb, 1)
    db = _dg(g16, a16, 0, 1 - ca) if cb == 1 else _dg(a16, g16, 1 - ca, 0)
    return da.astype(a.dtype), db.astype(b.dtype)


bdot.defvjp(_bdot_fwd, _bdot_bwd)


def _split3(a):
    a1 = a.astype(BF)
    r1 = a - a1.astype(F32)
    a2 = r1.astype(BF)
    a3 = (r1 - a2.astype(F32)).astype(BF)
    return a1, a2, a3


def _exact_right(a, c):
    m = a.shape[0]
    if m % 16:
        return sum(_dg(p, c, 1, 0) for p in _split3(a))
    o = _dg(jnp.concatenate(_split3(a), axis=0), c, 1, 0)
    return o[:m] + o[m:2 * m] + o[2 * m:]


def _exact_left(c, a):
    n = a.shape[1]
    o = _dg(c, jnp.concatenate(_split3(a), axis=1), 1, 0)
    return o[:, :n] + o[:, n:2 * n] + o[:, 2 * n:]


@jax.custom_vjp
def cmat(a, c, ct):
    return _exact_right(a, c)


def _cmat_fwd(a, c, ct):
    return cmat(a, c, ct), (c, ct)


def _cmat_bwd(res, g):
    c, ct = res
    return _exact_right(g, ct), jnp.zeros_like(c), jnp.zeros_like(ct)


cmat.defvjp(_cmat_fwd, _cmat_bwd)


@jax.custom_vjp
def cmatl(c, ct, a):
    return _exact_left(c, a)


def _cmatl_fwd(c, ct, a):
    return cmatl(c, ct, a), (c, ct)


def _cmatl_bwd(res, g):
    c, ct = res
    return jnp.zeros_like(c), jnp.zeros_like(ct), _exact_left(ct, g)


cmatl.defvjp(_cmatl_fwd, _cmatl_bwd)


SUBLANES = 8


def _taps(x, shifts, down):
    n, c = x.shape
    pad = _round_up(max(shifts), SUBLANES)
    if pad == 0:
        return {0: x}
    zeros = jnp.zeros((pad, c), x.dtype)
    xp = jnp.concatenate([zeros, x] if down else [x, zeros], axis=0)
    rolled, out = {0: xp}, {}
    for s in shifts:
        a, b = divmod(s, SUBLANES)
        if b not in rolled:
            rolled[b] = pltpu.roll(xp, b if down else n + pad - b, 0)
        off = pad - SUBLANES * a if down else SUBLANES * a
        out[s] = rolled[b][off:off + n]
    return out


def _shift_down(x, k):
    return _taps(x, [k], True)[k]


def _shift_up(x, k):
    return _taps(x, [k], False)[k]


@functools.partial(jax.custom_vjp, nondiff_argnums=(1,))
def shift(x, k):
    return _shift_down(x, k)


def _shift_fwd(x, k):
    return _shift_down(x, k), None


def _shift_bwd(k, _, g):
    return (_shift_up(g, k),)


shift.defvjp(_shift_fwd, _shift_bwd)


@functools.partial(jax.custom_vjp, nondiff_argnums=(2,))
def cconv(u, w, width):
    taps = _taps(u, list(range(width)), True)
    acc = u * w[width - 1:width, :]
    for k in range(width - 1):
        acc = acc + taps[width - 1 - k] * w[k:k + 1, :]
    return acc


def _cconv_fwd(u, w, width):
    return cconv(u, w, width), (u, w)


def _cconv_bwd(width, res, g):
    u, w = res
    rows = lax.broadcasted_iota(jnp.int32, w.shape, 0)
    du = g * w[width - 1:width, :]
    dw = jnp.where(rows == width - 1, jnp.sum(g * u, axis=0, keepdims=True), 0.0)
    g_taps = _taps(g, list(range(width)), False)
    u_taps = _taps(u, list(range(width)), True)
    for k in range(width - 1):
        s = width - 1 - k
        du = du + g_taps[s] * w[k:k + 1, :]
        dw = dw + jnp.where(rows == k, jnp.sum(g * u_taps[s], axis=0, keepdims=True), 0.0)
    return du, dw


cconv.defvjp(_cconv_fwd, _cconv_bwd)


def _rms(x, g):
    return x * lax.rsqrt(jnp.mean(x * x, axis=-1, keepdims=True) + RMS_EPS) * g


def _params(sem=None):
    return pltpu.CompilerParams(dimension_semantics=sem, vmem_limit_bytes=VMEM_LIMIT)


def _f32(v):
    return v if v.dtype == F32 else v.astype(F32)


def _first(axes):
    ok = None
    for ax in axes:
        c = pl.program_id(ax) == 0
        ok = c if ok is None else jnp.logical_and(ok, c)
    return ok


def fwd_call(fn, name, grid, ins, in_specs, out_shapes, out_specs, into=None):
    n_in = len(ins)
    n_into = 0 if into is None else 1

    def body(*refs):
        outs = fn(*[_f32(r[...]) for r in refs[:n_in]])
        for r, o in zip(refs[n_in + n_into:], outs):
            r[...] = o.astype(r.dtype)

    extra = [] if into is None else [into]
    return pl.pallas_call(body, name=name, grid=grid, in_specs=list(in_specs) + [pl.BlockSpec(memory_space=pl.ANY)] * n_into,
                          out_specs=out_specs, out_shape=out_shapes, input_output_aliases={n_in: 0} if n_into else {},
                          compiler_params=_params())(*ins, *extra)


def bwd_call(fn, name, grid, ins, in_specs, cots, cot_specs, gidx, g_shapes, g_specs, g_acc):
    n_in, n_cot = len(ins), len(cots)

    def body(*refs):
        vals = [_f32(r[...]) for r in refs[:n_in]]

        def f_sel(*dv):
            full = list(vals)
            for i, v in zip(gidx, dv):
                full[i] = v
            return tuple(fn(*full))

        outs, vjp = jax.vjp(f_sel, *[vals[i] for i in gidx])
        cts = tuple(_f32(r[...]) for r in refs[n_in:n_in + n_cot])
        grads = vjp(cts)
        for r, g, acc in zip(refs[n_in + n_cot:], grads, g_acc):
            if acc is None:
                r[...] = g.astype(r.dtype)
            else:
                @pl.when(_first(acc))
                def _():
                    r[...] = jnp.zeros_like(r)

                r[...] += g.astype(r.dtype)

    return pl.pallas_call(body, name=name, grid=grid, in_specs=list(in_specs) + list(cot_specs), out_specs=g_specs,
                          out_shape=g_shapes, compiler_params=_params())(*ins, *cots)


def _tile(dim, pref):
    if dim <= pref:
        return dim
    best = None
    for t in range(LANE, pref + 1, LANE):
        if dim % t == 0:
            best = t
    assert best is not None, dim
    return best


MATMUL_VMEM_BUDGET = 40 * 1024 * 1024


def _matmul_tiles(m, n, k, a_bytes, b_bytes, out_bytes):
    tn = _tile(n, 1024)
    for tk_pref in (k, 2048, 1024, 512):
        tk = _tile(k, tk_pref)
        for tm_pref in (1024, 512, 256):
            tm = _tile(m, tm_pref)
            need = 2 * (tm * tk * a_bytes + tk * tn * b_bytes + tm * tn * out_bytes) + (0 if tk == k else tm * tn * 4)
            need += (tm * tk * 2 if a_bytes == 4 else 0) + (tk * tn * 2 if b_bytes == 4 else 0)
            if need <= MATMUL_VMEM_BUDGET:
                return tm, tn, tk
    raise ValueError((m, n, k))


def matmul(a, b, mode, name, out_dtype=F32, epilogue=None, extras=(), params=(), after=(), n_acc=0):
    if mode == 'nn':
        (m, k), (k2, n) = a.shape, b.shape
    elif mode == 'nt':
        (m, k), (n, k2) = a.shape, b.shape
    else:
        (k, m), (k2, n) = a.shape, b.shape
    assert k == k2, (name, a.shape, b.shape)
    n_extra = len(extras) + len(params)
    out_dtypes = out_dtype if isinstance(out_dtype, tuple) else (out_dtype,)
    per_out = sum(jnp.dtype(dt).itemsize for dt in out_dtypes) + sum(e.dtype.itemsize for e in extras)
    tm, tn, tk = _matmul_tiles(m, n, k, a.dtype.itemsize, b.dtype.itemsize, per_out)
    nk = k // tk
    ca = 0 if mode == 'tn' else 1
    cb = 1 if mode == 'nt' else 0
    a_spec = pl.BlockSpec((tk, tm), lambda i, j, kk: (kk, i)) if mode == 'tn' else pl.BlockSpec((tm, tk), lambda i, j, kk: (i, kk))
    b_spec = pl.BlockSpec((tn, tk), lambda i, j, kk: (j, kk)) if mode == 'nt' else pl.BlockSpec((tk, tn), lambda i, j, kk: (kk, j))

    def finish(o_refs, extra_refs, acc, first_row_tile):
        outs = (acc,) if epilogue is None else epilogue(acc, *[_f32(e[...]) for e in extra_refs])
        n_tile = len(o_refs) - n_acc
        for o_ref, o in zip(o_refs[:n_tile], outs[:n_tile]):
            o_ref[...] = o.astype(o_ref.dtype)
        for o_ref, o in zip(o_refs[n_tile:], outs[n_tile:]):
            o_ref[...] = jnp.where(first_row_tile, o, o_ref[...] + o)

    n_after = len(after)

    def body_whole_k(a_ref, b_ref, *refs):
        refs = refs[n_after:]
        finish(refs[n_extra:], refs[:n_extra], _dg(a_ref[...].astype(BF), b_ref[...].astype(BF), ca, cb), pl.program_id(0) == 0)

    def body_split_k(a_ref, b_ref, *refs):
        refs = refs[n_after:]
        extra_refs, o_refs, acc = refs[:n_extra], refs[n_extra:-1], refs[-1]
        kk = pl.program_id(2)
        first_row_tile = pl.program_id(0) == 0

        @pl.when(kk == 0)
        def _():
            acc[...] = jnp.zeros_like(acc)

        acc[...] += _dg(a_ref[...].astype(BF), b_ref[...].astype(BF), ca, cb)

        @pl.when(kk == nk - 1)
        def _():
            finish(o_refs, extra_refs, acc[...], first_row_tile)

    tile = pl.BlockSpec((tm, tn), lambda i, j, kk: (i, j))
    row = pl.BlockSpec((1, tn), lambda i, j, kk: (0, j))
    n_par = len(params)
    outs = pl.pallas_call(
        body_whole_k if nk == 1 else body_split_k, name=name, grid=(m // tm, n // tn, nk),
        in_specs=[a_spec, b_spec] + [pl.BlockSpec(memory_space=pl.ANY)] * n_after + [tile] * len(extras) + [row] * n_par,
        out_specs=[tile] * len(out_dtypes) + [row] * n_acc,
        out_shape=[jax.ShapeDtypeStruct((m, n), dt) for dt in out_dtypes] + [jax.ShapeDtypeStruct((1, n), F32)] * n_acc,
        scratch_shapes=[] if nk == 1 else [pltpu.VMEM((tm, tn), F32)],
        compiler_params=_params(("arbitrary",) * 3 if n_acc else ("parallel", "parallel", "arbitrary")))(a, b, *after, *extras, *params)
    return outs if isinstance(out_dtype, tuple) or n_acc else outs[0]


_FLIPS = [(0, 0, 1), (1, 0, 0), (0, 1, 0), (1, 1, 0), (1, 0, 1), (0, 1, 1), (1, 1, 1)]


def _me():
    return lax.axis_index("x"), lax.axis_index("y"), lax.axis_index("c")


def _flip(pos, f):
    return tuple(jnp.where(fi == 1, 1 - p, p) if fi else p for p, fi in zip(pos, f))


def _slot(pos):
    return 4 * pos[0] + 2 * pos[1] + pos[2]


def all_gather(v, name):
    def body(v_ref, out_ref, send_sems, recv_sems, local_sem):
        me = _me()
        sibling = _flip(me, (0, 0, 1))
        chips = [_flip(me, f) for f in ((1, 0, 0), (0, 1, 0), (1, 1, 0))]

        def copy(k, block, to, src=None):
            return pltpu.make_async_remote_copy(
                src_ref=out_ref.at[_slot(block)] if src is None else src, dst_ref=out_ref.at[_slot(block)],
                send_sem=send_sems.at[k], recv_sem=recv_sems.at[k], device_id=to, device_id_type=pl.DeviceIdType.MESH)

        mine = pltpu.make_async_copy(v_ref, out_ref.at[_slot(me)], local_sem)
        mine.start()
        first = [copy(0, me, sibling, src=v_ref)] + [copy(1 + j, me, chip, src=v_ref) for j, chip in enumerate(chips)]
        for cp in first:
            cp.start()
        passed = [copy(4 + j, chip, sibling) for j, chip in enumerate(chips)]
        for j, chip in enumerate(chips):
            copy(1 + j, chip, me).wait_recv()
            passed[j].start()
        copy(0, sibling, me).wait_recv()
        for j, chip in enumerate(chips):
            copy(4 + j, _flip(chip, (0, 0, 1)), me).wait_recv()
        for cp in first + passed:
            cp.wait_send()
        mine.wait()

    return pl.pallas_call(
        body, name=name, out_shape=jax.ShapeDtypeStruct((N_DEV,) + v.shape, v.dtype),
        in_specs=[pl.BlockSpec(memory_space=pl.ANY)], out_specs=pl.BlockSpec(memory_space=pl.ANY),
        scratch_shapes=[pltpu.SemaphoreType.DMA((7,)), pltpu.SemaphoreType.DMA((7,)), pltpu.SemaphoreType.DMA(())],
    )(v)


def sum_slots(v, name, tr=256):
    _, r, c = v.shape
    tr = _tile_rows(r, tr)

    def body(v_ref, o_ref):
        acc = v_ref[0].astype(F32)
        for s in range(1, N_DEV):
            acc = acc + v_ref[s].astype(F32)
        o_ref[...] = acc

    return pl.pallas_call(body, name=name, grid=(r // tr,), in_specs=[pl.BlockSpec((N_DEV, tr, c), lambda i: (0, i, 0))],
                          out_specs=pl.BlockSpec((tr, c), lambda i: (i, 0)), out_shape=jax.ShapeDtypeStruct((r, c), F32),
                          compiler_params=_params())(v)


def _tile_rows(r, pref):
    if r <= pref:
        return r
    best = None
    for t in range(8, pref + 1, 8):
        if r % t == 0:
            best = t
    return r if best is None else best


def _adamw_math(w, m, v, g):
    nm = ADAM_B1 * m + (1.0 - ADAM_B1) * g
    nv = ADAM_B2 * v + (1.0 - ADAM_B2) * jnp.square(g)
    m_hat = nm / (1.0 - ADAM_B1 ** ADAM_STEP)
    v_hat = nv / (1.0 - ADAM_B2 ** ADAM_STEP)
    return -ADAM_LR * (m_hat / (jnp.sqrt(v_hat) + ADAM_EPS) + ADAM_WD * w), nm, nv


def update_from_slots(lands, offs, w, m, v, transposed, name):
    layers, a, b = w.shape
    n_land = len(lands)
    if transposed:
        rb, tk = LANE, 512
        assert a % tk == 0 and b % rb == 0 and all(o % rb == 0 for o in offs), (name, w.shape, offs)
        grid = (layers, a // tk, b // rb)
        land_block = (N_DEV, rb, tk)
        tile = pl.BlockSpec((None, tk, rb), lambda l, i, j: (l, i, j))

        def land_spec(layer):
            base = offs[layer] // rb
            return pl.BlockSpec(land_block, lambda l, i, j: (0, base + jnp.where(l == layer, j, 0), jnp.where(l == layer, i, 0)))
    else:
        fits = [t for t in (256, 128, 64) if a % t == 0 and all(o % t == 0 for o in offs)]
        assert fits or all(o == 0 for o in offs), (name, w.shape, offs)
        tr = max(fits) if fits else a
        grid = (layers, a // tr)
        land_block = (N_DEV, _round_up(tr, MEMBER_ROW_TILE), b)
        tile = pl.BlockSpec((None, tr, b), lambda l, i: (l, i, 0))

        def land_spec(layer):
            base = offs[layer] // tr
            return pl.BlockSpec(land_block, lambda l, i: (0, base + jnp.where(l == layer, i, 0), 0))

    def body(*refs):
        land_refs, (w_ref, m_ref, v_ref, g_ref, d_ref, nm_ref, nv_ref, acc) = refs[:n_land], refs[n_land:]
        for layer, land in enumerate(land_refs):
            @pl.when(pl.program_id(0) == layer)
            def _(land=land):
                rows = acc.shape[0]
                s = land[0, :rows].astype(F32)
                for k in range(1, N_DEV):
                    s = s + land[k, :rows].astype(F32)
                acc[...] = s

        g = acc[...].T if transposed else acc[...]
        d, nm, nv = _adamw_math(w_ref[...], m_ref[...], v_ref[...], g)
        g_ref[...] = g
        d_ref[...] = d
        nm_ref[...] = nm
        nv_ref[...] = nv

    sh = jax.ShapeDtypeStruct(w.shape, F32)
    return pl.pallas_call(
        body, name=name, grid=grid, in_specs=[land_spec(layer) for layer in range(n_land)] + [tile] * 3, out_specs=[tile] * 4,
        out_shape=[sh] * 4, scratch_shapes=[pltpu.VMEM((rb, tk) if transposed else (tr, b), F32)],
        compiler_params=_params())(*lands, w, m, v)


def adamw_many(ws, ms, vs, gs, name):
    n = len(ws)

    def body(*refs):
        for i in range(n):
            d, nm, nv = _adamw_math(refs[i][...], refs[n + i][...], refs[2 * n + i][...], refs[3 * n + i][...])
            refs[4 * n + i][...] = d
            refs[5 * n + i][...] = nm
            refs[6 * n + i][...] = nv

    vmem = pl.BlockSpec(memory_space=pltpu.VMEM)
    shapes = [jax.ShapeDtypeStruct(a.shape, F32) for a in ws]
    res = pl.pallas_call(body, name=name, in_specs=[vmem] * (4 * n), out_specs=[vmem] * (3 * n), out_shape=shapes * 3,
                         compiler_params=_params())(*ws, *ms, *vs, *gs)
    return res[:n], res[n:2 * n], res[2 * n:]


def seg_in(x, g):
    return (_rms(x, g),)


def seg_in_res(x, g):
    return x, _rms(x, g)


def seg_res(x, m, ga, gb):
    x1 = x + _rms(m, ga)
    return x1, _rms(x1, gb)


def seg_out(x, m, ga):
    return (x + _rms(m, ga),)


def act_epilogue(r):
    t = jnp.maximum(r, 0.0)
    return r, t * t


def res_epilogue(m, x, ga, gb):
    x1, h = seg_res(x, m, ga, gb)
    return m, x1, h


def res_bwd_epilogue(dh, x, m, dx1, ga, gb):
    _, vjp = jax.vjp(seg_res, x, m, ga, gb)
    return vjp((dx1, dh))


def in_bwd_epilogue(dh, x, dx_res, g):
    _, vjp = jax.vjp(seg_in_res, x, g)
    return vjp((dx_res, dh))


def loss_epilogue(mo, x, target, g):
    (y,), vjp = jax.vjp(seg_out, x, mo, g)
    d = y - target
    dx, dm, dg = vjp((d / float(D),))
    return dx, dm, dg, jnp.sum(d * d, axis=0, keepdims=True)


def act_bwd_epilogue(drr, r):
    return (drr * (2.0 * jnp.maximum(r, 0.0)),)


def seg_ln(v, g, b):
    mu = jnp.mean(v, axis=-1, keepdims=True)
    var = jnp.mean(jnp.square(v - mu), axis=-1, keepdims=True)
    vn = (v - mu) * lax.rsqrt(var + LN_EPS) * g + b
    return (jax.nn.silu(vn),)


def make_pool_fn(group):
    window = 2 ** (group + 1)

    def pool_fn(ug, pw, scale):
        s = ug
        for lvl in range(group + 1):
            s = s + shift(s, 2 ** lvl)
        cnt = jnp.minimum(lax.broadcasted_iota(jnp.int32, ug.shape, 0) + 1, window).astype(F32)
        return (bdot(s / cnt - ug, pw, 1, 0) * scale,)

    return pool_fn


def conv4_fn(xr, w, b):
    return (jax.nn.silu(cconv(xr, w, SSM_CONV) + b),)


def cd1_fn(u, dww, dwb, scw):
    val, gate, bg, cg, hh = (u[:, k * LANE:(k + 1) * LANE] for k in range(5))
    v = val * jax.nn.sigmoid(gate)
    vc = cconv(v, dww, CONF_K) + dwb
    sc = bg * cconv(cg * hh, scw, SC_K)
    return vc, sc


def attn_fn(q, kv):
    outs = []
    for h in range(XA_HEADS):
        cols = slice(h * XA_DH, (h + 1) * XA_DH)
        s = bdot(q[:, cols], kv[:, cols], 1, 1) / math.sqrt(XA_DH)
        p = jax.nn.softmax(s, axis=-1)
        outs.append(bdot(p, kv[:, D + h * XA_DH:D + (h + 1) * XA_DH], 1, 0))
    return (jnp.concatenate(outs, axis=1),)


def ssd_chunk(xbc, z, dtraw, dtb, alog, dsk, nw, h0, h1, h2, h3, e64, e64t, ecat, ecatt, tril, trilt):
    xs, bm, cm = xbc[:, :SSM_GSZ], xbc[:, SSM_GSZ:SSM_GSZ + SSM_N], xbc[:, SSM_GSZ + SSM_N:]
    hin = (h0, h1, h2, h3)
    dt = jax.nn.softplus(dtraw + dtb)
    a = -jnp.exp(alog)
    d_a = dt * a
    cs = cmatl(tril, trilt, d_a)
    cs_cat = cmat(cs, ecat, ecatt)
    cs64, cs128 = cs_cat[:, :SSM_GSZ], cs_cat[:, SSM_GSZ:]
    dt64 = cmat(dt, e64, e64t)
    row = lax.broadcasted_iota(jnp.int32, (8, LANE), 0)
    heads = jnp.where(row == 0, dsk, jnp.where(row == 1, jnp.sum(d_a, axis=0, keepdims=True), 0.0))
    heads64 = cmat(heads, e64, e64t)
    d64, tot64 = heads64[0:1, :], heads64[1:2, :]
    xdt = xs * dt64
    cb = bdot(cm, bm, 1, 1)
    li = lax.broadcasted_iota(jnp.int32, (CHUNK, CHUNK), 0)
    si = lax.broadcasted_iota(jnp.int32, (CHUNK, CHUNK), 1)
    causal = li >= si
    lane = lax.broadcasted_iota(jnp.int32, (CHUNK, LANE), 1)
    xw = xdt * jnp.exp(tot64 - cs64)
    ecs = jnp.exp(cs64)
    etot = jnp.exp(tot64)
    ycols, hout = [], []
    for j in range(4):
        sl = slice(j * LANE, (j + 1) * LANE)
        xj = xdt[:, sl]
        ys = []
        for hh in range(2):
            r = 2 * j + hh
            col = cs128[:, r * LANE:(r + 1) * LANE]
            decay = jnp.exp(jnp.where(causal, col - col.T, -1e30))
            ys.append(bdot(cb * decay, xj, 1, 0))
        y_diag = jnp.where(lane < SSM_P, ys[0], ys[1])
        y_off = bdot(cm, hin[j], 1, 0) * ecs[:, sl]
        ycols.append(y_diag + y_off)
        hout.append(etot[:, sl] * hin[j] + bdot(bm, xw[:, sl], 0, 0))
    y = jnp.concatenate(ycols, axis=1) + d64 * xs
    y = y * jax.nn.silu(z)
    yn = y * lax.rsqrt(jnp.mean(y * y, axis=-1, keepdims=True) + RMS_EPS) * nw
    return (yn,) + tuple(hout)


def _xbc_group(a, axis):
    parts = []
    for g in range(SSM_GROUPS):
        for start, width in ((g * SSM_GSZ, SSM_GSZ), (SSM_INNER + g * SSM_N, SSM_N), (SSM_INNER + (SSM_GROUPS + g) * SSM_N, SSM_N)):
            parts.append(lax.slice_in_dim(a, start, start + width, axis=axis))
    return jnp.concatenate(parts, axis=axis)


def _xbc_ungroup(a, axis):
    xs, bs, cs = [], [], []
    for g in range(SSM_GROUPS):
        base = g * SSM_XBC_G
        xs.append(lax.slice_in_dim(a, base, base + SSM_GSZ, axis=axis))
        bs.append(lax.slice_in_dim(a, base + SSM_GSZ, base + SSM_GSZ + SSM_N, axis=axis))
        cs.append(lax.slice_in_dim(a, base + SSM_GSZ + SSM_N, base + SSM_XBC_G, axis=axis))
    return jnp.concatenate(xs + bs + cs, axis=axis)


def _ssd_consts():
    h = np.arange(LANE)[:, None]
    e64 = np.stack([(h == g * 8 + np.arange(SSM_GSZ)[None, :] // SSM_P) for g in range(SSM_GROUPS)]).astype(np.float32)
    e128 = np.stack([(h == g * 8 + np.arange(8 * LANE)[None, :] // LANE) for g in range(SSM_GROUPS)]).astype(np.float32)
    ecat = np.concatenate([e64, e128], axis=2)
    tril = np.tril(np.ones((CHUNK, CHUNK), np.float32))
    return tuple(jnp.asarray(c, dtype=BF) for c in (e64, e64.transpose(0, 2, 1), ecat, ecat.transpose(0, 2, 1), tril, tril.T))


def _ssd_specs(nc, rev):
    def ci(c):
        return nc - 1 - c if rev else c

    def row(width, col):
        return pl.BlockSpec((CHUNK, width), lambda b, c: (b * nc + ci(c), col))

    def whole(shape):
        return pl.BlockSpec(shape, lambda b, c: (0,) * len(shape))

    data = [row(SSM_CONV_DIM, 0),
            row(SSM_GSZ, 1), row(SSM_GSZ, 2), row(LANE, 24)]
    par = [whole((1, LANE))] * 3 + [whole((1, SSM_INNER))]
    cst = [whole((SSM_GROUPS, LANE, SSM_GSZ)), whole((SSM_GROUPS, SSM_GSZ, LANE)), whole((SSM_GROUPS, LANE, 12 * LANE)),
           whole((SSM_GROUPS, 12 * LANE, LANE)), whole((CHUNK, CHUNK)), whole((CHUNK, CHUNK))]
    hsave = pl.BlockSpec((None, None, SSM_GROUPS, 4, SSM_N, LANE), lambda b, c: (b, ci(c), 0, 0, 0, 0))
    return data, par, cst, hsave, row, whole


def _ssd_group_args(g, xbc, z, dtr, dtb, alog, dsk, nw):
    return (xbc[:, g * SSM_XBC_G:(g + 1) * SSM_XBC_G], z[g], dtr, dtb, alog, dsk, nw[:, g * SSM_GSZ:(g + 1) * SSM_GSZ])


def ssd_fwd(xbc_act, u, dtb, alog, dsk, nw, consts, bsz, seq):
    nc = seq // CHUNK
    data, par, cst, hsave, row, _ = _ssd_specs(nc, False)

    def body(xbc, z0, z1, dtr, dtb_r, alog_r, dsk_r, nw_r, e64, e64t, ecat, ecatt, tril, trilt, yn_ref, hs_ref, h):
        @pl.when(pl.program_id(1) == 0)
        def _():
            h[...] = jnp.zeros_like(h)

        hs_ref[...] = h[...]
        ys = []
        for g in range(SSM_GROUPS):
            args = _ssd_group_args(g, xbc[...], (z0[...], z1[...]), dtr[...], dtb_r[...], alog_r[...], dsk_r[...], nw_r[...])
            outs = ssd_chunk(*args, h[g, 0], h[g, 1], h[g, 2], h[g, 3], e64[g], e64t[g], ecat[g], ecatt[g], tril[...], trilt[...])
            ys.append(outs[0])
            for j in range(4):
                h[g, j] = outs[1 + j]
        yn_ref[...] = jnp.concatenate(ys, axis=1).astype(yn_ref.dtype)

    t = bsz * seq
    return pl.pallas_call(
        body, name="ssd_fwd", grid=(bsz, nc), in_specs=data + par + cst, out_specs=[row(SSM_INNER, 0), hsave],
        out_shape=[jax.ShapeDtypeStruct((t, SSM_INNER), BF), jax.ShapeDtypeStruct((bsz, nc, SSM_GROUPS, 4, SSM_N, LANE), F32)],
        scratch_shapes=[pltpu.VMEM((SSM_GROUPS, 4, SSM_N, LANE), F32)], compiler_params=_params(),
    )(xbc_act, u, u, u, dtb, alog, dsk, nw, *consts)


def ssd_bwd(xbc_act, u, dtb, alog, dsk, nw, consts, hs, dmix, bsz, seq):
    nc = seq // CHUNK
    data, par, cst, hsave, row, whole = _ssd_specs(nc, True)
    t = bsz * seq
    pcol = POOL_W // SSM_GSZ

    def body(xbc, z0, z1, dtr, dtb_r, alog_r, dsk_r, nw_r, e64, e64t, ecat, ecatt, tril, trilt, hs_ref, dy0, dy1,
             dxbc, dz, ddt, ddtb, dalog, ddsk, dnw, dh):
        @pl.when(pl.program_id(1) == 0)
        def _():
            dh[...] = jnp.zeros_like(dh)

        per_group = []
        for g, dyn in enumerate((dy0, dy1)):
            cst_vals = (e64[g], e64t[g], ecat[g], ecatt[g], tril[...], trilt[...])
            prim = _ssd_group_args(g, xbc[...], (z0[...], z1[...]), dtr[...], dtb_r[...], alog_r[...], dsk_r[...], nw_r[...])
            prim = prim + (hs_ref[g, 0], hs_ref[g, 1], hs_ref[g, 2], hs_ref[g, 3])
            _, vjp = jax.vjp(lambda *args, c=cst_vals: ssd_chunk(*args, *c), *prim)
            gr = vjp((dyn[...].astype(F32), dh[g, 0], dh[g, 1], dh[g, 2], dh[g, 3]))
            for j in range(4):
                dh[g, j] = gr[7 + j]
            per_group.append(gr)
        g0, g1 = per_group
        dxbc[...] = jnp.concatenate([g0[0], g1[0]], axis=1)
        dz[...] = jnp.concatenate([g0[1], g1[1]], axis=1).astype(dz.dtype)
        ddt[...] = g0[2] + g1[2]

        @pl.when(_first((0, 1)))
        def _():
            for r in (ddtb, dalog, ddsk, dnw):
                r[...] = jnp.zeros_like(r)

        ddtb[...] += g0[3] + g1[3]
        dalog[...] += g0[4] + g1[4]
        ddsk[...] += g0[5] + g1[5]
        dnw[...] += jnp.concatenate([g0[6], g1[6]], axis=1)

    out_specs = [row(SSM_CONV_DIM, 0), row(SSM_INNER, 0), row(LANE, 0), whole((1, LANE)), whole((1, LANE)), whole((1, LANE)),
                 whole((1, SSM_INNER))]
    lane = jax.ShapeDtypeStruct((1, LANE), F32)
    out_shape = [jax.ShapeDtypeStruct((t, SSM_CONV_DIM), F32), jax.ShapeDtypeStruct((t, SSM_INNER), BF),
                 jax.ShapeDtypeStruct((t, LANE), F32), lane, lane, lane, jax.ShapeDtypeStruct((1, SSM_INNER), F32)]
    return pl.pallas_call(
        body, name="ssd_bwd", grid=(bsz, nc), in_specs=data + par + cst + [hsave, row(SSM_GSZ, pcol), row(SSM_GSZ, pcol + 1)],
        out_specs=out_specs, out_shape=out_shape, scratch_shapes=[pltpu.VMEM((SSM_GROUPS, 4, SSM_N, LANE), F32)],
        compiler_params=_params(),
    )(xbc_act, u, u, u, dtb, alog, dsk, nw, *consts, hs, dmix, dmix)


TB = 512


def _rows(d, col=0):
    return pl.BlockSpec((TB, d), lambda i: (i, col))


def _par(d):
    return pl.BlockSpec((1, d), lambda i: (0, 0))


def _sd(shape, dtype=F32):
    return jax.ShapeDtypeStruct(shape, dtype)


def _round_up(n, m):
    return -(-n // m) * m


def _pad_rows(a, rows):
    return jnp.pad(a, ((0, rows - a.shape[0]), (0, 0)))


def _pack128(arrs):
    flat = jnp.concatenate([a.reshape(-1) for a in arrs])
    n = flat.shape[0]
    rows = -(-n // (8 * LANE)) * 8
    return jnp.pad(flat, (0, rows * LANE - n)).reshape(rows, LANE)


def _unpack128(packed, shapes):
    flat = packed.reshape(-1)
    out, off = [], 0
    for s in shapes:
        n = int(np.prod(s))
        out.append(flat[off:off + n].reshape(s))
        off += n
    return out


def kernel(x, mem, norm_gains, xa_wq, xa_wkv, xa_wo, mlp_w1, mlp_w2, ab_w_in, pool_w, pool_scale, ssm_conv_w, ssm_conv_b, ssm_dt_bias, ssm_a_log, ssm_d, ssm_norm, ab_w_out, cd_w_in, conf_dw_w, conf_dw_b, conf_ln_g, conf_ln_b, sc_conv_w, cd_w_out, loss_target, m_norm_gains, m_xa_wq, m_xa_wkv, m_xa_wo, m_mlp_w1, m_mlp_w2, m_ab_w_in, m_pool_w, m_pool_scale, m_ssm_conv_w, m_ssm_conv_b, m_ssm_dt_bias, m_ssm_a_log, m_ssm_d, m_ssm_norm, m_ab_w_out, m_cd_w_in, m_conf_dw_w, m_conf_dw_b, m_conf_ln_g, m_conf_ln_b, m_sc_conv_w, m_cd_w_out, v_norm_gains, v_xa_wq, v_xa_wkv, v_xa_wo, v_mlp_w1, v_mlp_w2, v_ab_w_in, v_pool_w, v_pool_scale, v_ssm_conv_w, v_ssm_conv_b, v_ssm_dt_bias, v_ssm_a_log, v_ssm_d, v_ssm_norm, v_ab_w_out, v_cd_w_in, v_conf_dw_w, v_conf_dw_b, v_conf_ln_g, v_conf_ln_b, v_sc_conv_w, v_cd_w_out):
    args = locals()
    w = {n: args[n] for n in WEIGHTS}
    mom_m = {n: args["m_" + n] for n in WEIGHTS}
    mom_v = {n: args["v_" + n] for n in WEIGHTS}
    ex = Exchange(w)
    loss_local, grad_x, small_grads = local_step(x, mem, loss_target, ex)
    outs = {}

    started = ex.put_small(small_grads, loss_local)
    landed = {key: ex.landed(key, started) for key in ('l1', 'cd', 'l0')}
    late = []
    for n, keys in (('mlp_w1', ('l0', 'l1')), ('mlp_w2', ('l0', 'l1')), ('xa_wkv', ('l0', 'l1')), ('xa_wq', ('l0', 'l1')),
                    ('xa_wo', ('l0', 'l1')), ('cd_w_in', ('cd',)), ('cd_w_out', ('cd',))):
        lands = [landed[key][0] for key in keys]
        offs = [landed[key][1][(n, layer)] for layer, key in enumerate(keys)]
        outs[n] = update_from_slots(lands, offs, w[n], mom_m[n], mom_v[n], SHARD_AXIS[n] == 2, "update_" + n)
        late.append(outs[n][1])
    g_own, loss = ex.reduced_small(late)
    land_ab, offs_ab = ex.landed('ab', late)
    outs['ab_w_out'] = update_from_slots([land_ab], [offs_ab[('ab_w_out', 0)]], w['ab_w_out'], mom_m['ab_w_out'],
                                         mom_v['ab_w_out'], False, "update_ab_w_out")
    res = update_from_slots([land_ab], [offs_ab[('ab_w_in', 0)]], jnp.swapaxes(w['ab_w_in'], 1, 2), jnp.swapaxes(mom_m['ab_w_in'], 1, 2),
                            jnp.swapaxes(mom_v['ab_w_in'], 1, 2), False, "update_ab_w_in")
    outs['ab_w_in'] = tuple(jnp.swapaxes(r, 1, 2) for r in res)
    small = SMALL_SHARDED + REPLICATED
    upd = adamw_many([w[n] for n in small], [mom_m[n] for n in small], [mom_v[n] for n in small], [g_own[n] for n in small],
                     "adamw_small")
    for i, n in enumerate(small):
        outs[n] = (g_own[n], upd[0][i], upd[1][i], upd[2][i])
    return (loss, grad_x.reshape(x.shape), *[outs[n][0] for n in WEIGHTS], *[outs[n][1] for n in WEIGHTS],
            *[outs[n][2] for n in WEIGHTS], *[outs[n][3] for n in WEIGHTS])


G_AB = (('ab_w_in', 0), ('ab_w_out', 0))
G_L0 = (('xa_wq', 0), ('xa_wkv', 0), ('xa_wo', 0), ('mlp_w1', 0), ('mlp_w2', 0))
G_L1 = (('xa_wq', 1), ('xa_wkv', 1), ('xa_wo', 1), ('mlp_w1', 1), ('mlp_w2', 1))
G_CD = (('cd_w_in', 0), ('cd_w_out', 0))
GATHER_GROUPS = {'ab': G_AB[:1], 'l0a': G_AB[1:] + G_L0[:3], 'l0b': G_L0[3:], 'cd': G_CD, 'l1a': G_L1[:3], 'l1b': G_L1[3:]}
SHARD_AXIS = dict(BIG)
MEMBER_ROW_TILE = 64
FLAT_ROW_TILE = 128


def _members(group, w):
    out = []
    for n, layer in group:
        shp = w[n].shape[1:]
        if SHARD_AXIS[n] == 2:
            shp = (shp[1], shp[0])
        assert shp[1] == D, (n, shp)
        out.append((n, layer, shp, shp[0], _round_up(shp[0], MEMBER_ROW_TILE)))
    return out


def _group_rows(group, w):
    return _round_up(sum(m[4] for m in _members(group, w)), FLAT_ROW_TILE)


def _flat_shards(group, w):
    parts = []
    for n, layer, _, _, padded in _members(group, w):
        shard = w[n][layer].astype(BF)
        parts.append(_pad_rows(shard.T if SHARD_AXIS[n] == 2 else shard, padded))
    return _pad_rows(jnp.concatenate(parts, axis=0), _group_rows(group, w))


def _full_from_slots(land, group, w):
    out, off = {}, 0
    for n, layer, shp, rows, padded in _members(group, w):
        out[(n, layer)] = land[:, off:off + rows].reshape(N_DEV * rows, D)
        off += padded
    return out


def _slots_from_full(grads, group, w):
    parts = []
    for n, layer, shp, rows, padded in _members(group, w):
        blk = grads[(n, layer)].astype(BF).reshape(N_DEV, rows, D)
        parts.append(jnp.pad(blk, ((0, 0), (0, padded - rows), (0, 0))))
    send = jnp.concatenate(parts, axis=1)
    return jnp.pad(send, ((0, 0), (0, _group_rows(group, w) - send.shape[1]), (0, 0)))


_HBM = pl.BlockSpec(memory_space=pltpu.HBM)
_SEM = pl.BlockSpec(memory_space=pltpu.SEMAPHORE)
_ANY = pl.BlockSpec(memory_space=pl.ANY)


def _peer_copy(k, src, dst, send_sems, recv_sems, peer):
    return pltpu.make_async_remote_copy(src_ref=src, dst_ref=dst, send_sem=send_sems.at[k], recv_sem=recv_sems.at[k],
                                        device_id=peer, device_id_type=pl.DeviceIdType.MESH)


def exchange_start(src, name, scatter, after=()):
    shape = src.shape[-2:]
    after = list(after)

    def body(src_ref, land_ref, *rest):
        send_sems, recv_sems, token = rest[len(after)], rest[len(after) + 1], rest[-1]
        me = _me()
        for k, f in enumerate(_FLIPS):
            peer = _flip(me, f)
            piece = src_ref.at[_slot(peer)] if scatter else src_ref
            _peer_copy(k, piece, land_ref.at[_slot(me)], send_sems, recv_sems, peer).start()
        token[...] = jnp.zeros_like(token)

    land = pltpu.with_memory_space_constraint(lax.empty((N_DEV,) + shape, src.dtype), pltpu.HBM)
    return pl.pallas_call(
        body, name=name,
        out_shape=(pltpu.SemaphoreType.DMA((7,)), pltpu.SemaphoreType.DMA((7,)), pltpu.HBM(src.shape, src.dtype),
                   pltpu.HBM((N_DEV,) + shape, src.dtype), jax.ShapeDtypeStruct((8, LANE), F32)),
        in_specs=(_HBM, _HBM) + (_ANY,) * len(after), out_specs=(_SEM, _SEM, _HBM, _HBM, pl.BlockSpec(memory_space=pltpu.VMEM)),
        input_output_aliases={0: 2, 1: 3},
        compiler_params=pltpu.CompilerParams(has_side_effects=pltpu.SideEffectType.DATAFLOW_SIDE_EFFECTING),
    )(pltpu.with_memory_space_constraint(src, pltpu.HBM), land, *after)


def exchange_wait(handles, after, name, scatter):
    send_sems, recv_sems, src_thru, land_thru, _ = handles
    after = list(after) if isinstance(after, (list, tuple)) else [after]

    def body(src_ref, land_ref, send_sems, recv_sems, *rest):
        token = rest[-1]
        me = _me()
        for k, f in enumerate(_FLIPS):
            peer = _flip(me, f)
            piece = src_ref.at[_slot(peer)] if scatter else src_ref
            cp = _peer_copy(k, piece, land_ref.at[_slot(peer)], send_sems, recv_sems, peer)
            cp.wait_send()
            cp.wait_recv()
        token[...] = jnp.zeros_like(token)

    return pl.pallas_call(
        body, name=name, out_shape=(pltpu.HBM(src_thru.shape, src_thru.dtype), pltpu.HBM(land_thru.shape, land_thru.dtype),
                                    jax.ShapeDtypeStruct((8, LANE), F32)),
        in_specs=(_HBM, _HBM, _SEM, _SEM) + (_ANY,) * len(after), out_specs=(_HBM, _HBM, pl.BlockSpec(memory_space=pltpu.VMEM)),
        input_output_aliases={0: 0, 1: 1},
        compiler_params=pltpu.CompilerParams(has_side_effects=pltpu.SideEffectType.DATAFLOW_SIDE_EFFECTING),
    )(src_thru, land_thru, send_sems, recv_sems, *after)


class Exchange:
    def __init__(self, w):
        self.w = w
        self.me = _slot(_me())
        shapes = [w[n].shape for n in SMALL_SHARDED]
        gs = all_gather(_pack128([w[n] for n in SMALL_SHARDED]), "gather_small")
        per_dev = [_unpack128(gs[d], shapes) for d in range(N_DEV)]
        self.small = {n: jnp.concatenate([per_dev[d][i] for d in range(N_DEV)], axis=-1) for i, n in enumerate(SMALL_SHARDED)}
        self.small.update({n: w[n] for n in REPLICATED})
        first = all_gather(_flat_shards(GATHER_GROUPS['ab'], w), "gather_ab")
        self.first = _full_from_slots(first, GATHER_GROUPS['ab'], w)
        self.gathers, self.done, self.tokens, self.reductions = {}, {}, [], {}
        self.start_gather('l0a', after=[first])
        self.start_gather('l0b', after=[self.gathers['l0a'][4]])

    def take_tokens(self):
        toks, self.tokens = self.tokens, []
        return toks

    def start_gather(self, key, after=()):
        group = GATHER_GROUPS[key]
        self.gathers[key] = exchange_start(_flat_shards(group, self.w), f"gather_{key}_start", False, after=after)
        self.tokens.append(self.gathers[key][4])

    def weights(self, key, after):
        if key == 'ab':
            return self.first
        handles = self.gathers[key]
        _, land, self.done[key] = exchange_wait(handles, after, f"gather_{key}_wait", False)
        land = lax.dynamic_update_slice(land, handles[2][None], (self.me, 0, 0))
        return _full_from_slots(land, GATHER_GROUPS[key], self.w)

    def put_grads(self, key, group, grads):
        send = _slots_from_full(grads, group, self.w)
        handles = exchange_start(send, f"reduce_{key}_start", True)
        self.reductions[key] = (group, handles)
        self.tokens.append(handles[4])

    def landed(self, key, after):
        group, handles = self.reductions[key]
        send, land, _ = exchange_wait(handles, after, f"reduce_{key}_wait", True)
        mine = lax.dynamic_slice_in_dim(send, self.me, 1, axis=0)
        land = lax.dynamic_update_slice(land, mine, (self.me, 0, 0))
        offs, off = {}, 0
        for n, layer, _, _, padded in _members(group, self.w):
            offs[(n, layer)] = off
            off += padded
        return land, offs

    def put_small(self, small_grads, loss_local):
        small = SMALL_SHARDED + REPLICATED
        self.small_shapes = [small_grads[n].shape for n in small] + [(1,)]
        packed = _pack128([small_grads[n] for n in small] + [loss_local.reshape(1)])
        self.small_handles = exchange_start(packed, "gather_small_grads_start", False)
        return self.small_handles[4]

    def reduced_small(self, after):
        small = SMALL_SHARDED + REPLICATED
        src, land, _ = exchange_wait(self.small_handles, after, "gather_small_grads_wait", False)
        gs = lax.dynamic_update_slice(land, src[None], (self.me, 0, 0))
        tot = _unpack128(sum_slots(gs, "sum_small", 1024), self.small_shapes)
        out = {}
        for n, g in zip(small, tot):
            if n in SMALL_SHARDED:
                width = self.w[n].shape[-1]
                g = lax.dynamic_slice_in_dim(g, self.me * width, width, axis=g.ndim - 1)
            out[n] = g
        return out, tot[-1].reshape(())


def local_step(x, mem, target, ex):
    bsz, seq, _ = x.shape
    t = bsz * seq
    nb = t // TB
    nc = seq // CHUNK
    x0 = x.reshape(t, D)
    mem2 = mem.reshape(bsz * N_MEM, D)
    tgt = target.reshape(t, D)
    p = ex.small
    gains = p['norm_gains']
    big = {}

    def gain(layer, i):
        g = gains[layer, i].reshape(1, D)
        for tok in ex.take_tokens():
            g = g + tok[0, 0]
        return g

    consts = _ssd_consts()
    grads = {}
    saved = [dict(), dict()]

    def matmul_res(a, b, name, xin, ga, gb):
        return matmul(a, b, 'nn', name, (F32, F32, BF), epilogue=res_epilogue, extras=[xin], params=[ga, gb])

    def attn_specs():
        nq = seq // TB
        q = pl.BlockSpec((TB, D), lambda b, i: (b * nq + i, 0))
        kv = pl.BlockSpec((N_MEM, 2 * D), lambda b, i: (b, 0))
        return (bsz, nq), q, kv

    def attention_fwd(layer, xin, hin, sv, ga, gb):
        q = matmul(hin, big[('xa_wq', layer)], 'nn', f"q_{layer}", BF)
        kv = matmul(mem2, big[('xa_wkv', layer)], 'nt', f"kv_{layer}", BF)
        grid, qs, kvs = attn_specs()
        o, = fwd_call(attn_fn, f"attn_{layer}", grid, [q, kv], [qs, kvs], [_sd((t, D), BF)], [qs])
        ao, x_next, h_next = matmul_res(o, big[('xa_wo', layer)], f"ao_{layer}", xin, ga, gb)
        sv.update(q=q, kv=kv, o=o, ao=ao)
        return ao, x_next, h_next

    def mlp_fwd(layer, hin, sv, res):
        r, rr = matmul(hin, big[('mlp_w1', layer)], 'nt', f"mlp1_{layer}", (BF, BF), epilogue=act_epilogue)
        out = matmul_res(rr, big[('mlp_w2', layer)], f"mlp2_{layer}", *res)
        sv.update(r=r, rr=rr, mo=out[0])
        return out

    sv = saved[0]
    h0, = fwd_call(seg_in, "norm_in", (nb,), [x0, gain(0, 0)], [_rows(D), _par(D)], [_sd((t, D), BF)], [_rows(D)])
    big.update(ex.weights('ab', h0))
    xbc0 = POOL_W + SSM_INNER
    w_ab_in = big[('ab_w_in', 0)]
    w_ab_in = _pad_rows(jnp.concatenate([w_ab_in[:xbc0], _xbc_group(w_ab_in[xbc0:xbc0 + SSM_CONV_DIM], 0),
                                         w_ab_in[xbc0 + SSM_CONV_DIM:]], axis=0), AB_IN_PAD)
    conv_w, conv_b = _xbc_group(p['ssm_conv_w'][0], 1), _xbc_group(p['ssm_conv_b'], 1)
    u0 = matmul(h0, w_ab_in, 'nt', "ab_in")
    pool_outs = []
    for g in range(POOL_GROUPS):
        seqspec = pl.BlockSpec((seq, PG), lambda b, g=g: (b, g))
        po, = fwd_call(make_pool_fn(g), f"pool_{g}", (bsz,), [u0, p['pool_w'][0, g], p['pool_scale']],
                       [seqspec, pl.BlockSpec((PG, PG), lambda b: (0, 0)), pl.BlockSpec((1, PG), lambda b, g=g: (0, g))],
                       [_sd((t, PG), BF)], [pl.BlockSpec((seq, PG), lambda b: (b, 0))])
        pool_outs.append(po)
    cw = 256
    ncb = SSM_CONV_DIM // cw
    cbase = (POOL_W + SSM_INNER) // cw
    conv_in_specs = [pl.BlockSpec((seq, cw), lambda j, b: (b, cbase + j)), pl.BlockSpec((SSM_CONV, cw), lambda j, b: (0, j)),
                     pl.BlockSpec((1, cw), lambda j, b: (0, j))]
    conv_out_spec = pl.BlockSpec((seq, cw), lambda j, b: (b, j))
    xbc_act, = fwd_call(conv4_fn, "ssm_conv", (ncb, bsz), [u0, conv_w, conv_b], conv_in_specs,
                        [_sd((t, SSM_CONV_DIM))], [conv_out_spec])
    dtb = jnp.pad(p['ssm_dt_bias'], ((0, 0), (0, LANE - SSM_HEADS)))
    alog = jnp.pad(p['ssm_a_log'], ((0, 0), (0, LANE - SSM_HEADS)))
    dsk = jnp.pad(p['ssm_d'], ((0, 0), (0, LANE - SSM_HEADS)))
    yn, hs = ssd_fwd(xbc_act, u0, dtb, alog, dsk, p['ssm_norm'], consts, bsz, seq)
    mix0 = jnp.concatenate(pool_outs + [yn], axis=1)
    big.update(ex.weights('l0a', yn))
    ex.start_gather('cd', after=[ex.done['l0a']])
    m0, x1, h2 = matmul_res(mix0, big[('ab_w_out', 0)], "ab_out", x0, gain(0, 1), gain(0, 2))
    ao0, x2, h3 = attention_fwd(0, x1, h2, sv, gain(0, 3), gain(0, 4))
    big.update(ex.weights('l0b', h3))
    mo0, x3, h4 = mlp_fwd(0, h3, sv, (x2, gain(0, 5), gain(1, 0)))
    big.update(ex.weights('cd', mo0))
    ex.start_gather('l1a', after=[ex.done['cd']])
    ex.start_gather('l1b', after=[ex.gathers['l1a'][4]])

    sv1 = saved[1]
    nd = D // LANE
    w_cd_in = big[('cd_w_in', 0)].reshape(5, nd, LANE, D).transpose(1, 0, 2, 3).reshape(CD_IN, D)
    u1 = matmul(h4, w_cd_in, 'nt', "cd_in")
    cd_par = [pl.BlockSpec((CONF_K, LANE), lambda j, b: (0, j)), pl.BlockSpec((1, LANE), lambda j, b: (0, j)),
              pl.BlockSpec((SC_K, LANE), lambda j, b: (0, j))]
    cd_ins = [u1, p['conf_dw_w'][0], p['conf_dw_b'], p['sc_conv_w'][0]]
    cd_u_spec = pl.BlockSpec((seq, 5 * LANE), lambda j, b: (b, j))
    cd_in_specs = [cd_u_spec] + cd_par
    cd_out_spec = pl.BlockSpec((seq, LANE), lambda j, b: (b, j))
    vconv, mix1 = fwd_call(cd1_fn, "cd_conv", (nd, bsz), cd_ins, cd_in_specs, [_sd((t, D)), _sd((t, CD_OUT), BF)],
                           [cd_out_spec, pl.BlockSpec((seq, LANE), lambda j, b: (b, nd + j))])
    mix1, = fwd_call(seg_ln, "conf_ln", (nb,), [vconv, p['conf_ln_g'], p['conf_ln_b']], [_rows(D), _par(D), _par(D)],
                     [_sd((t, CD_OUT), BF)], [_rows(D)], into=mix1)
    m1, x4, h5 = matmul_res(mix1, big[('cd_w_out', 0)], "cd_out", x3, gain(1, 1), gain(1, 2))
    big.update(ex.weights('l1a', h5))
    ao1, x5, h6 = attention_fwd(1, x4, h5, sv1, gain(1, 3), gain(1, 4))
    big.update(ex.weights('l1b', h6))
    r1, rr1 = matmul(h6, big[('mlp_w1', 1)], 'nt', "mlp1_1", (BF, BF), epilogue=act_epilogue)
    sv1.update(r=r1, rr=rr1)
    dx5, dmo1, dg15, lanes = matmul(rr1, big[('mlp_w2', 1)], 'nn', "mlp2_1", (F32, BF), epilogue=loss_epilogue, extras=[x5, tgt],
                                    params=[gain(1, 5)], n_acc=2)
    loss = 0.5 * jnp.sum(lanes) / float(D)

    gain_grads = {(1, 5): dg15}

    def matmul_res_bwd(a, b, mode, name, xin, m, ga, gb, dx1):
        return list(matmul(a, b, mode, name, (F32, BF), epilogue=res_bwd_epilogue, extras=[xin, m, dx1], params=[ga, gb], n_acc=2))

    def mlp_bwd(layer, hin, dmo, sv, res):
        grads_w2 = matmul(sv['rr'], dmo, 'tn', f"d_mlp_w2_{layer}", BF)
        dr, = matmul(dmo, big[('mlp_w2', layer)], 'nt', f"d_r_{layer}", (BF,), epilogue=act_bwd_epilogue, extras=[sv['r']])
        grads_w1 = matmul(dr, hin, 'tn', f"d_mlp_w1_{layer}", BF)
        return matmul_res_bwd(dr, big[('mlp_w1', layer)], 'nn', f"d_h_mlp_{layer}", *res) + [grads_w1, grads_w2]

    def attention_bwd(layer, hin, dao, sv, res):
        g_wo = matmul(sv['o'], dao, 'tn', f"d_xa_wo_{layer}", BF)
        do = matmul(dao, big[('xa_wo', layer)], 'nt', f"d_o_{layer}", BF)
        grid, qs, kvs = attn_specs()
        dq, dkv = bwd_call(attn_fn, f"d_attn_{layer}", grid, [sv['q'], sv['kv']], [qs, kvs], [do], [qs], [0, 1],
                           [_sd((t, D), BF), _sd((bsz * N_MEM, 2 * D))], [qs, kvs], [None, (1,)])
        g_wkv = matmul(dkv, mem2, 'tn', f"d_xa_wkv_{layer}", BF)
        g_wq = matmul(hin, dq, 'tn', f"d_xa_wq_{layer}", BF)
        return matmul_res_bwd(dq, big[('xa_wq', layer)], 'nt', f"d_h_attn_{layer}", *res) + [g_wq, g_wkv, g_wo]

    per_layer = {k: [None, None] for k in ('xa_wq', 'xa_wkv', 'xa_wo', 'mlp_w1', 'mlp_w2')}

    (dx4, dao1, gain_grads[(1, 3)], gain_grads[(1, 4)], per_layer['mlp_w1'][1],
     per_layer['mlp_w2'][1]) = mlp_bwd(1, h6, dmo1, sv1, (x4, ao1, gain(1, 3), gain(1, 4), dx5))
    (dx3, dm1, gain_grads[(1, 1)], gain_grads[(1, 2)], per_layer['xa_wq'][1], per_layer['xa_wkv'][1],
     per_layer['xa_wo'][1]) = attention_bwd(1, h5, dao1, sv1, (x3, m1, gain(1, 1), gain(1, 2), dx4))
    ex.put_grads('l1', G_L1, {(k, 1): v[1] for k, v in per_layer.items()})
    g_cd_out = matmul(mix1, dm1, 'tn', "d_cd_w_out", BF)
    dmix1 = matmul(dm1, big[('cd_w_out', 0)], 'nt', "d_mix1", after=ex.take_tokens())
    dvconv, dlg, dlb = bwd_call(seg_ln, "d_conf_ln", (nb,), [vconv, p['conf_ln_g'], p['conf_ln_b']],
                                [_rows(D), _par(D), _par(D)], [dmix1], [_rows(D, 0)], [0, 1, 2],
                                [_sd((t, D)), _sd((1, D)), _sd((1, D))], [_rows(D), _par(D), _par(D)], [None, (0,), (0,)])
    grads['conf_ln_g'], grads['conf_ln_b'] = dlg, dlb
    cd_g = bwd_call(cd1_fn, "d_cd_conv", (nd, bsz), cd_ins, cd_in_specs, [dvconv, dmix1],
                    [cd_out_spec, pl.BlockSpec((seq, LANE), lambda j, b: (b, nd + j))], list(range(4)),
                    [_sd((t, CD_IN), BF), _sd((CONF_K, D)), _sd((1, D)), _sd((SC_K, D))], [cd_u_spec] + cd_par,
                    [None, (1,), (1,), (1,)])
    du1 = cd_g[0]
    grads['conf_dw_w'], grads['conf_dw_b'], grads['sc_conv_w'] = cd_g[1][None], cd_g[2], cd_g[3][None]
    g_cd_in = matmul(du1, h4, 'tn', "d_cd_w_in", BF).reshape(nd, 5, LANE, D).transpose(1, 0, 2, 3).reshape(CD_IN, D)
    ex.put_grads('cd', G_CD, {('cd_w_in', 0): g_cd_in, ('cd_w_out', 0): g_cd_out})
    dx2, dmo0, gain_grads[(0, 5)], gain_grads[(1, 0)] = matmul_res_bwd(du1, w_cd_in, 'nn', "d_h_cd", x2, mo0, gain(0, 5),
                                                                       gain(1, 0), dx3)
    (dx1, dao0, gain_grads[(0, 3)], gain_grads[(0, 4)], per_layer['mlp_w1'][0],
     per_layer['mlp_w2'][0]) = mlp_bwd(0, h3, dmo0, sv, (x1, ao0, gain(0, 3), gain(0, 4), dx2))
    (dx0r, dm0, gain_grads[(0, 1)], gain_grads[(0, 2)], per_layer['xa_wq'][0], per_layer['xa_wkv'][0],
     per_layer['xa_wo'][0]) = attention_bwd(0, h2, dao0, sv, (x0, m0, gain(0, 1), gain(0, 2), dx1))
    ex.put_grads('l0', G_L0, {(k, 0): v[0] for k, v in per_layer.items()})
    g_ab_out = matmul(mix0, dm0, 'tn', "d_ab_w_out", BF)
    dmix0 = matmul(dm0, big[('ab_w_out', 0)], 'nt', "d_mix0", after=ex.take_tokens())
    dxbc_act, dz, ddt, ddtb, dalog, ddsk, dnw = ssd_bwd(xbc_act, u0, dtb, alog, dsk, p['ssm_norm'], consts, hs, dmix0, bsz, seq)
    grads['ssm_dt_bias'] = ddtb[:, :SSM_HEADS]
    grads['ssm_a_log'] = dalog[:, :SSM_HEADS]
    grads['ssm_d'] = ddsk[:, :SSM_HEADS]
    grads['ssm_norm'] = dnw
    dxr, dcw, dcb = bwd_call(conv4_fn, "d_ssm_conv", (ncb, bsz), [u0, conv_w, conv_b], conv_in_specs,
                             [dxbc_act], [conv_out_spec], [0, 1, 2],
                             [_sd((t, SSM_CONV_DIM), BF), _sd((SSM_CONV, SSM_CONV_DIM)), _sd((1, SSM_CONV_DIM))],
                             [conv_out_spec, conv_in_specs[1], conv_in_specs[2]], [None, (1,), (1,)])
    grads['ssm_conv_w'], grads['ssm_conv_b'] = _xbc_ungroup(dcw, 1)[None], _xbc_ungroup(dcb, 1)
    dpool, dpw, dps = [], [], []
    for g in range(POOL_GROUPS):
        seqspec = pl.BlockSpec((seq, PG), lambda b, g=g: (b, g))
        one = pl.BlockSpec((seq, PG), lambda b: (b, 0))
        wspec = pl.BlockSpec((PG, PG), lambda b: (0, 0))
        sspec = pl.BlockSpec((1, PG), lambda b, g=g: (0, g))
        a, bb, c = bwd_call(make_pool_fn(g), f"d_pool_{g}", (bsz,), [u0, p['pool_w'][0, g], p['pool_scale']],
                            [seqspec, wspec, sspec], [dmix0], [seqspec], [0, 1, 2],
                            [_sd((t, PG), BF), _sd((PG, PG)), _sd((1, PG))], [one, wspec, pl.BlockSpec((1, PG), lambda b: (0, 0))],
                            [None, (0,), (0,)])
        dpool.append(a)
        dpw.append(bb)
        dps.append(c)
    grads['pool_w'] = jnp.stack(dpw)[None]
    grads['pool_scale'] = jnp.concatenate(dps, axis=1)
    du0 = jnp.concatenate(dpool + [dz, dxr, ddt.astype(BF)], axis=1)
    g_ab_in = matmul(du0, h0, 'tn', "d_ab_w_in", BF)
    g_ab_in = jnp.concatenate([g_ab_in[:xbc0], _xbc_ungroup(g_ab_in[xbc0:xbc0 + SSM_CONV_DIM], 0),
                               g_ab_in[xbc0 + SSM_CONV_DIM:AB_IN]], axis=0)
    ex.put_grads('ab', G_AB, {('ab_w_in', 0): g_ab_in, ('ab_w_out', 0): g_ab_out})
    dx, dg00 = matmul(du0, w_ab_in, 'nn', "d_h_ab", (F32,), epilogue=in_bwd_epilogue, extras=[x0, dx0r], params=[gain(0, 0)],
                      after=ex.take_tokens(), n_acc=1)
    gain_grads[(0, 0)] = dg00
    grads['norm_gains'] = jnp.stack([jnp.concatenate([gain_grads[(l, i)] for i in range(6)], axis=0) for l in range(2)])
    return loss, dx, grads
```

```python
import functools
import math

import numpy as np
import jax
import jax.numpy as jnp
from jax import lax
from jax.experimental import pallas as pl
from jax.experimental.pallas import tpu as pltpu

BF = jnp.bfloat16
F32 = jnp.float32

N_DEV = 8
D = 1024
N_MEM = 256
XA_HEADS = 4
XA_DH = D // XA_HEADS
POOL_GROUPS = 4
PG = 128
POOL_W = POOL_GROUPS * PG
SSM_INNER = 1024
SSM_GROUPS = 2
SSM_GSZ = SSM_INNER // SSM_GROUPS
SSM_HEADS = 16
SSM_P = 64
SSM_N = 128
SSM_CONV = 4
SSM_CONV_DIM = SSM_INNER + 2 * SSM_GROUPS * SSM_N
SSM_XBC_G = SSM_GSZ + 2 * SSM_N
CHUNK = 128
AB_IN = POOL_W + SSM_INNER + SSM_CONV_DIM + SSM_HEADS
AB_IN_PAD = POOL_W + SSM_INNER + SSM_CONV_DIM + 128
AB_OUT = POOL_W + SSM_INNER
CONF_K = 31
SC_K = 3
CD_IN = 5 * D
CD_OUT = 2 * D
MLP_H = 4 * D
RMS_EPS = 1e-6
LN_EPS = 1e-5
ADAM_LR = 0.001
ADAM_B1 = 0.9
ADAM_B2 = 0.999
ADAM_EPS = 1e-08
ADAM_WD = 0.01
ADAM_STEP = 10
VMEM_LIMIT = 56 * 1024 * 1024
LANE = 128

NAMES = ['x', 'mem', 'norm_gains', 'xa_wq', 'xa_wkv', 'xa_wo', 'mlp_w1', 'mlp_w2', 'ab_w_in', 'pool_w', 'pool_scale',
         'ssm_conv_w', 'ssm_conv_b', 'ssm_dt_bias', 'ssm_a_log', 'ssm_d', 'ssm_norm', 'ab_w_out', 'cd_w_in', 'conf_dw_w',
         'conf_dw_b', 'conf_ln_g', 'conf_ln_b', 'sc_conv_w', 'cd_w_out', 'loss_target']
WEIGHTS = NAMES[2:25]
BIG = [('xa_wq', 1), ('xa_wkv', 2), ('xa_wo', 1), ('mlp_w1', 2), ('mlp_w2', 1), ('cd_w_in', 2), ('cd_w_out', 1),
       ('ab_w_out', 1), ('ab_w_in', 2)]
SMALL_SHARDED = ['norm_gains', 'ssm_conv_w', 'conf_dw_w', 'conf_dw_b', 'conf_ln_g', 'conf_ln_b', 'sc_conv_w']
REPLICATED = ['pool_w', 'pool_scale', 'ssm_conv_b', 'ssm_dt_bias', 'ssm_a_log', 'ssm_d', 'ssm_norm']


def _dg(a, b, ca, cb, prec=None):
    return lax.dot_general(a, b, (((ca,), (cb,)), ((), ())), precision=prec, preferred_element_type=F32)


@functools.partial(jax.custom_vjp, nondiff_argnums=(2, 3))
def bdot(a, b, ca, cb):
    return _dg(a.astype(BF), b.astype(BF), ca, cb)


def _bdot_fwd(a, b, ca, cb):
    return bdot(a, b, ca, cb), (a, b)


def _bdot_bwd(ca, cb, res, g):
    a, b = res
    g16, a16, b16 = g.astype(BF), a.astype(BF), b.astype(BF)
    da = _dg(g16, b16, 1, 1 - cb) if ca == 1 else _dg(b16, g16, 1 - cb, 1)
    db = _dg(g16, a16, 0, 1 - ca) if cb == 1 else _dg(a16, g16, 1 - ca, 0)
    return da.astype(a.dtype), db.astype(b.dtype)


bdot.defvjp(_bdot_fwd, _bdot_bwd)


def _split3(a):
    a1 = a.astype(BF)
    r1 = a - a1.astype(F32)
    a2 = r1.astype(BF)
    a3 = (r1 - a2.astype(F32)).astype(BF)
    return a1, a2, a3


def _exact_right(a, c):
    m = a.shape[0]
    if m % 16:
        return sum(_dg(p, c, 1, 0) for p in _split3(a))
    o = _dg(jnp.concatenate(_split3(a), axis=0), c, 1, 0)
    return o[:m] + o[m:2 * m] + o[2 * m:]


def _exact_left(c, a):
    n = a.shape[1]
    o = _dg(c, jnp.concatenate(_split3(a), axis=1), 1, 0)
    return o[:, :n] + o[:, n:2 * n] + o[:, 2 * n:]


@jax.custom_vjp
def cmat(a, c, ct):
    return _exact_right(a, c)


def _cmat_fwd(a, c, ct):
    return cmat(a, c, ct), (c, ct)


def _cmat_bwd(res, g):
    c, ct = res
    return _exact_right(g, ct), jnp.zeros_like(c), jnp.zeros_like(ct)


cmat.defvjp(_cmat_fwd, _cmat_bwd)


@jax.custom_vjp
def cmatl(c, ct, a):
    return _exact_left(c, a)


def _cmatl_fwd(c, ct, a):
    return cmatl(c, ct, a), (c, ct)


def _cmatl_bwd(res, g):
    c, ct = res
    return jnp.zeros_like(c), jnp.zeros_like(ct), _exact_left(ct, g)


cmatl.defvjp(_cmatl_fwd, _cmatl_bwd)


SUBLANES = 8


def _taps(x, shifts, down):
    n, c = x.shape
    pad = _round_up(max(shifts), SUBLANES)
    if pad == 0:
        return {0: x}
    zeros = jnp.zeros((pad, c), x.dtype)
    xp = jnp.concatenate([zeros, x] if down else [x, zeros], axis=0)
    rolled, out = {0: xp}, {}
    for s in shifts:
        a, b = divmod(s, SUBLANES)
        if b not in rolled:
            rolled[b] = pltpu.roll(xp, b if down else n + pad - b, 0)
        off = pad - SUBLANES * a if down else SUBLANES * a
        out[s] = rolled[b][off:off + n]
    return out


def _shift_down(x, k):
    return _taps(x, [k], True)[k]


def _shift_up(x, k):
    return _taps(x, [k], False)[k]


@functools.partial(jax.custom_vjp, nondiff_argnums=(1,))
def shift(x, k):
    return _shift_down(x, k)


def _shift_fwd(x, k):
    return _shift_down(x, k), None


def _shift_bwd(k, _, g):
    return (_shift_up(g, k),)


shift.defvjp(_shift_fwd, _shift_bwd)


@functools.partial(jax.custom_vjp, nondiff_argnums=(2,))
def cconv(u, w, width):
    taps = _taps(u, list(range(width)), True)
    acc = u * w[width - 1:width, :]
    for k in range(width - 1):
        acc = acc + taps[width - 1 - k] * w[k:k + 1, :]
    return acc


def _cconv_fwd(u, w, width):
    return cconv(u, w, width), (u, w)


def _cconv_bwd(width, res, g):
    u, w = res
    rows = lax.broadcasted_iota(jnp.int32, w.shape, 0)
    du = g * w[width - 1:width, :]
    dw = jnp.where(rows == width - 1, jnp.sum(g * u, axis=0, keepdims=True), 0.0)
    g_taps = _taps(g, list(range(width)), False)
    u_taps = _taps(u, list(range(width)), True)
    for k in range(width - 1):
        s = width - 1 - k
        du = du + g_taps[s] * w[k:k + 1, :]
        dw = dw + jnp.where(rows == k, jnp.sum(g * u_taps[s], axis=0, keepdims=True), 0.0)
    return du, dw


cconv.defvjp(_cconv_fwd, _cconv_bwd)


def _rms(x, g):
    return x * lax.rsqrt(jnp.mean(x * x, axis=-1, keepdims=True) + RMS_EPS) * g


def _params(sem=None):
    return pltpu.CompilerParams(dimension_semantics=sem, vmem_limit_bytes=VMEM_LIMIT)


def _f32(v):
    return v if v.dtype == F32 else v.astype(F32)


def _first(axes):
    ok = None
    for ax in axes:
        c = pl.program_id(ax) == 0
        ok = c if ok is None else jnp.logical_and(ok, c)
    return ok


def fwd_call(fn, name, grid, ins, in_specs, out_shapes, out_specs, into=None):
    n_in = len(ins)
    n_into = 0 if into is None else 1

    def body(*refs):
        outs = fn(*[_f32(r[...]) for r in refs[:n_in]])
        for r, o in zip(refs[n_in + n_into:], outs):
            r[...] = o.astype(r.dtype)

    extra = [] if into is None else [into]
    return pl.pallas_call(body, name=name, grid=grid, in_specs=list(in_specs) + [pl.BlockSpec(memory_space=pl.ANY)] * n_into,
                          out_specs=out_specs, out_shape=out_shapes, input_output_aliases={n_in: 0} if n_into else {},
                          compiler_params=_params())(*ins, *extra)


def bwd_call(fn, name, grid, ins, in_specs, cots, cot_specs, gidx, g_shapes, g_specs, g_acc):
    n_in, n_cot = len(ins), len(cots)

    def body(*refs):
        vals = [_f32(r[...]) for r in refs[:n_in]]

        def f_sel(*dv):
            full = list(vals)
            for i, v in zip(gidx, dv):
                full[i] = v
            return tuple(fn(*full))

        outs, vjp = jax.vjp(f_sel, *[vals[i] for i in gidx])
        cts = tuple(_f32(r[...]) for r in refs[n_in:n_in + n_cot])
        grads = vjp(cts)
        for r, g, acc in zip(refs[n_in + n_cot:], grads, g_acc):
            if acc is None:
                r[...] = g.astype(r.dtype)
            else:
                @pl.when(_first(acc))
                def _():
                    r[...] = jnp.zeros_like(r)

                r[...] += g.astype(r.dtype)

    return pl.pallas_call(body, name=name, grid=grid, in_specs=list(in_specs) + list(cot_specs), out_specs=g_specs,
                          out_shape=g_shapes, compiler_params=_params())(*ins, *cots)


def _tile(dim, pref):
    if dim <= pref:
        return dim
    best = None
    for t in range(LANE, pref + 1, LANE):
        if dim % t == 0:
            best = t
    assert best is not None, dim
    return best


MATMUL_VMEM_BUDGET = 40 * 1024 * 1024


def _matmul_tiles(m, n, k, a_bytes, b_bytes, out_bytes):
    tn = _tile(n, 1024)
    for tk_pref in (k, 2048, 1024, 512):
        tk = _tile(k, tk_pref)
        for tm_pref in (1024, 512, 256):
            tm = _tile(m, tm_pref)
            need = 2 * (tm * tk * a_bytes + tk * tn * b_bytes + tm * tn * out_bytes) + (0 if tk == k else tm * tn * 4)
            need += (tm * tk * 2 if a_bytes == 4 else 0) + (tk * tn * 2 if b_bytes == 4 else 0)
            if need <= MATMUL_VMEM_BUDGET:
                return tm, tn, tk
    raise ValueError((m, n, k))


def matmul(a, b, mode, name, out_dtype=F32, epilogue=None, extras=(), params=(), after=(), n_acc=0):
    if mode == 'nn':
        (m, k), (k2, n) = a.shape, b.shape
    elif mode == 'nt':
        (m, k), (n, k2) = a.shape, b.shape
    else:
        (k, m), (k2, n) = a.shape, b.shape
    assert k == k2, (name, a.shape, b.shape)
    n_extra = len(extras) + len(params)
    out_dtypes = out_dtype if isinstance(out_dtype, tuple) else (out_dtype,)
    per_out = sum(jnp.dtype(dt).itemsize for dt in out_dtypes) + sum(e.dtype.itemsize for e in extras)
    tm, tn, tk = _matmul_tiles(m, n, k, a.dtype.itemsize, b.dtype.itemsize, per_out)
    nk = k // tk
    ca = 0 if mode == 'tn' else 1
    cb = 1 if mode == 'nt' else 0
    a_spec = pl.BlockSpec((tk, tm), lambda i, j, kk: (kk, i)) if mode == 'tn' else pl.BlockSpec((tm, tk), lambda i, j, kk: (i, kk))
    b_spec = pl.BlockSpec((tn, tk), lambda i, j, kk: (j, kk)) if mode == 'nt' else pl.BlockSpec((tk, tn), lambda i, j, kk: (kk, j))

    def finish(o_refs, extra_refs, acc, first_row_tile):
        outs = (acc,) if epilogue is None else epilogue(acc, *[_f32(e[...]) for e in extra_refs])
        n_tile = len(o_refs) - n_acc
        for o_ref, o in zip(o_refs[:n_tile], outs[:n_tile]):
            o_ref[...] = o.astype(o_ref.dtype)
        for o_ref, o in zip(o_refs[n_tile:], outs[n_tile:]):
            o_ref[...] = jnp.where(first_row_tile, o, o_ref[...] + o)

    n_after = len(after)

    def body_whole_k(a_ref, b_ref, *refs):
        refs = refs[n_after:]
        finish(refs[n_extra:], refs[:n_extra], _dg(a_ref[...].astype(BF), b_ref[...].astype(BF), ca, cb), pl.program_id(0) == 0)

    def body_split_k(a_ref, b_ref, *refs):
        refs = refs[n_after:]
        extra_refs, o_refs, acc = refs[:n_extra], refs[n_extra:-1], refs[-1]
        kk = pl.program_id(2)
        first_row_tile = pl.program_id(0) == 0

        @pl.when(kk == 0)
        def _():
            acc[...] = jnp.zeros_like(acc)

        acc[...] += _dg(a_ref[...].astype(BF), b_ref[...].astype(BF), ca, cb)

        @pl.when(kk == nk - 1)
        def _():
            finish(o_refs, extra_refs, acc[...], first_row_tile)

    tile = pl.BlockSpec((tm, tn), lambda i, j, kk: (i, j))
    row = pl.BlockSpec((1, tn), lambda i, j, kk: (0, j))
    n_par = len(params)
    outs = pl.pallas_call(
        body_whole_k if nk == 1 else body_split_k, name=name, grid=(m // tm, n // tn, nk),
        in_specs=[a_spec, b_spec] + [pl.BlockSpec(memory_space=pl.ANY)] * n_after + [tile] * len(extras) + [row] * n_par,
        out_specs=[tile] * len(out_dtypes) + [row] * n_acc,
        out_shape=[jax.ShapeDtypeStruct((m, n), dt) for dt in out_dtypes] + [jax.ShapeDtypeStruct((1, n), F32)] * n_acc,
        scratch_shapes=[] if nk == 1 else [pltpu.VMEM((tm, tn), F32)],
        compiler_params=_params(("arbitrary",) * 3 if n_acc else ("parallel", "parallel", "arbitrary")))(a, b, *after, *extras, *params)
    return outs if isinstance(out_dtype, tuple) or n_acc else outs[0]


_FLIPS = [(0, 0, 1), (1, 0, 0), (0, 1, 0), (1, 1, 0), (1, 0, 1), (0, 1, 1), (1, 1, 1)]


def _me():
    return lax.axis_index("x"), lax.axis_index("y"), lax.axis_index("c")


def _flip(pos, f):
    return tuple(jnp.where(fi == 1, 1 - p, p) if fi else p for p, fi in zip(pos, f))


def _slot(pos):
    return 4 * pos[0] + 2 * pos[1] + pos[2]


def all_gather(v, name):
    def body(v_ref, out_ref, send_sems, recv_sems, local_sem):
        me = _me()
        sibling = _flip(me, (0, 0, 1))
        chips = [_flip(me, f) for f in ((1, 0, 0), (0, 1, 0), (1, 1, 0))]

        def copy(k, block, to, src=None):
            return pltpu.make_async_remote_copy(
                src_ref=out_ref.at[_slot(block)] if src is None else src, dst_ref=out_ref.at[_slot(block)],
                send_sem=send_sems.at[k], recv_sem=recv_sems.at[k], device_id=to, device_id_type=pl.DeviceIdType.MESH)

        mine = pltpu.make_async_copy(v_ref, out_ref.at[_slot(me)], local_sem)
        mine.start()
        first = [copy(0, me, sibling, src=v_ref)] + [copy(1 + j, me, chip, src=v_ref) for j, chip in enumerate(chips)]
        for cp in first:
            cp.start()
        passed = [copy(4 + j, chip, sibling) for j, chip in enumerate(chips)]
        for j, chip in enumerate(chips):
            copy(1 + j, chip, me).wait_recv()
            passed[j].start()
        copy(0, sibling, me).wait_recv()
        for j, chip in enumerate(chips):
            copy(4 + j, _flip(chip, (0, 0, 1)), me).wait_recv()
        for cp in first + passed:
            cp.wait_send()
        mine.wait()

    return pl.pallas_call(
        body, name=name, out_shape=jax.ShapeDtypeStruct((N_DEV,) + v.shape, v.dtype),
        in_specs=[pl.BlockSpec(memory_space=pl.ANY)], out_specs=pl.BlockSpec(memory_space=pl.ANY),
        scratch_shapes=[pltpu.SemaphoreType.DMA((7,)), pltpu.SemaphoreType.DMA((7,)), pltpu.SemaphoreType.DMA(())],
    )(v)


def sum_slots(v, name, tr=256):
    _, r, c = v.shape
    tr = _tile_rows(r, tr)

    def body(v_ref, o_ref):
        acc = v_ref[0].astype(F32)
        for s in range(1, N_DEV):
            acc = acc + v_ref[s].astype(F32)
        o_ref[...] = acc

    return pl.pallas_call(body, name=name, grid=(r // tr,), in_specs=[pl.BlockSpec((N_DEV, tr, c), lambda i: (0, i, 0))],
                          out_specs=pl.BlockSpec((tr, c), lambda i: (i, 0)), out_shape=jax.ShapeDtypeStruct((r, c), F32),
                          compiler_params=_params())(v)


def _tile_rows(r, pref):
    if r <= pref:
        return r
    best = None
    for t in range(8, pref + 1, 8):
        if r % t == 0:
            best = t
    return r if best is None else best


def _adamw_math(w, m, v, g):
    nm = ADAM_B1 * m + (1.0 - ADAM_B1) * g
    nv = ADAM_B2 * v + (1.0 - ADAM_B2) * jnp.square(g)
    m_hat = nm / (1.0 - ADAM_B1 ** ADAM_STEP)
    v_hat = nv / (1.0 - ADAM_B2 ** ADAM_STEP)
    return -ADAM_LR * (m_hat / (jnp.sqrt(v_hat) + ADAM_EPS) + ADAM_WD * w), nm, nv


def update_from_slots(lands, offs, w, m, v, transposed, name):
    layers, a, b = w.shape
    n_land = len(lands)
    if transposed:
        rb, tk = LANE, 512
        assert a % tk == 0 and b % rb == 0 and all(o % rb == 0 for o in offs), (name, w.shape, offs)
        grid = (layers, a // tk, b // rb)
        land_block = (N_DEV, rb, tk)
        tile = pl.BlockSpec((None, tk, rb), lambda l, i, j: (l, i, j))

        def land_spec(layer):
            base = offs[layer] // rb
            return pl.BlockSpec(land_block, lambda l, i, j: (0, base + jnp.where(l == layer, j, 0), jnp.where(l == layer, i, 0)))
    else:
        fits = [t for t in (256, 128, 64) if a % t == 0 and all(o % t == 0 for o in offs)]
        assert fits or all(o == 0 for o in offs), (name, w.shape, offs)
        tr = max(fits) if fits else a
        grid = (layers, a // tr)
        land_block = (N_DEV, _round_up(tr, MEMBER_ROW_TILE), b)
        tile = pl.BlockSpec((None, tr, b), lambda l, i: (l, i, 0))

        def land_spec(layer):
            base = offs[layer] // tr
            return pl.BlockSpec(land_block, lambda l, i: (0, base + jnp.where(l == layer, i, 0), 0))

    def body(*refs):
        land_refs, (w_ref, m_ref, v_ref, g_ref, d_ref, nm_ref, nv_ref, acc) = refs[:n_land], refs[n_land:]
        for layer, land in enumerate(land_refs):
            @pl.when(pl.program_id(0) == layer)
            def _(land=land):
                rows = acc.shape[0]
                s = land[0, :rows].astype(F32)
                for k in range(1, N_DEV):
                    s = s + land[k, :rows].astype(F32)
                acc[...] = s

        g = acc[...].T if transposed else acc[...]
        d, nm, nv = _adamw_math(w_ref[...], m_ref[...], v_ref[...], g)
        g_ref[...] = g
        d_ref[...] = d
        nm_ref[...] = nm
        nv_ref[...] = nv

    sh = jax.ShapeDtypeStruct(w.shape, F32)
    return pl.pallas_call(
        body, name=name, grid=grid, in_specs=[land_spec(layer) for layer in range(n_land)] + [tile] * 3, out_specs=[tile] * 4,
        out_shape=[sh] * 4, scratch_shapes=[pltpu.VMEM((rb, tk) if transposed else (tr, b), F32)],
        compiler_params=_params())(*lands, w, m, v)


def adamw_many(ws, ms, vs, gs, name):
    n = len(ws)

    def body(*refs):
        for i in range(n):
            d, nm, nv = _adamw_math(refs[i][...], refs[n + i][...], refs[2 * n + i][...], refs[3 * n + i][...])
            refs[4 * n + i][...] = d
            refs[5 * n + i][...] = nm
            refs[6 * n + i][...] = nv

    vmem = pl.BlockSpec(memory_space=pltpu.VMEM)
    shapes = [jax.ShapeDtypeStruct(a.shape, F32) for a in ws]
    res = pl.pallas_call(body, name=name, in_specs=[vmem] * (4 * n), out_specs=[vmem] * (3 * n), out_shape=shapes * 3,
                         compiler_params=_params())(*ws, *ms, *vs, *gs)
    return res[:n], res[n:2 * n], res[2 * n:]


def seg_in(x, g):
    return (_rms(x, g),)


def seg_in_res(x, g):
    return x, _rms(x, g)


def seg_res(x, m, ga, gb):
    x1 = x + _rms(m, ga)
    return x1, _rms(x1, gb)


def seg_out(x, m, ga):
    return (x + _rms(m, ga),)


def act_epilogue(r):
    t = jnp.maximum(r, 0.0)
    return r, t * t


def res_epilogue(m, x, ga, gb):
    x1, h = seg_res(x, m, ga, gb)
    return m, x1, h


def res_bwd_epilogue(dh, x, m, dx1, ga, gb):
    _, vjp = jax.vjp(seg_res, x, m, ga, gb)
    return vjp((dx1, dh))


def in_bwd_epilogue(dh, x, dx_res, g):
    _, vjp = jax.vjp(seg_in_res, x, g)
    return vjp((dx_res, dh))


def loss_epilogue(mo, x, target, g):
    (y,), vjp = jax.vjp(seg_out, x, mo, g)
    d = y - target
    dx, dm, dg = vjp((d / float(D),))
    return dx, dm, dg, jnp.sum(d * d, axis=0, keepdims=True)


def act_bwd_epilogue(drr, r):
    return (drr * (2.0 * jnp.maximum(r, 0.0)),)


def seg_ln(v, g, b):
    mu = jnp.mean(v, axis=-1, keepdims=True)
    var = jnp.mean(jnp.square(v - mu), axis=-1, keepdims=True)
    vn = (v - mu) * lax.rsqrt(var + LN_EPS) * g + b
    return (jax.nn.silu(vn),)


def make_pool_fn(group):
    window = 2 ** (group + 1)

    def pool_fn(ug, pw, scale):
        s = ug
        for lvl in range(group + 1):
            s = s + shift(s, 2 ** lvl)
        cnt = jnp.minimum(lax.broadcasted_iota(jnp.int32, ug.shape, 0) + 1, window).astype(F32)
        return (bdot(s / cnt - ug, pw, 1, 0) * scale,)

    return pool_fn


def conv4_fn(xr, w, b):
    return (jax.nn.silu(cconv(xr, w, SSM_CONV) + b),)


def cd1_fn(u, dww, dwb, scw):
    val, gate, bg, cg, hh = (u[:, k * LANE:(k + 1) * LANE] for k in range(5))
    v = val * jax.nn.sigmoid(gate)
    vc = cconv(v, dww, CONF_K) + dwb
    sc = bg * cconv(cg * hh, scw, SC_K)
    return vc, sc


def attn_fn(q, kv):
    outs = []
    for h in range(XA_HEADS):
        cols = slice(h * XA_DH, (h + 1) * XA_DH)
        s = bdot(q[:, cols], kv[:, cols], 1, 1) / math.sqrt(XA_DH)
        p = jax.nn.softmax(s, axis=-1)
        outs.append(bdot(p, kv[:, D + h * XA_DH:D + (h + 1) * XA_DH], 1, 0))
    return (jnp.concatenate(outs, axis=1),)


def ssd_chunk(xbc, z, dtraw, dtb, alog, dsk, nw, h0, h1, h2, h3, e64, e64t, ecat, ecatt, tril, trilt):
    xs, bm, cm = xbc[:, :SSM_GSZ], xbc[:, SSM_GSZ:SSM_GSZ + SSM_N], xbc[:, SSM_GSZ + SSM_N:]
    hin = (h0, h1, h2, h3)
    dt = jax.nn.softplus(dtraw + dtb)
    a = -jnp.exp(alog)
    d_a = dt * a
    cs = cmatl(tril, trilt, d_a)
    cs_cat = cmat(cs, ecat, ecatt)
    cs64, cs128 = cs_cat[:, :SSM_GSZ], cs_cat[:, SSM_GSZ:]
    dt64 = cmat(dt, e64, e64t)
    row = lax.broadcasted_iota(jnp.int32, (8, LANE), 0)
    heads = jnp.where(row == 0, dsk, jnp.where(row == 1, jnp.sum(d_a, axis=0, keepdims=True), 0.0))
    heads64 = cmat(heads, e64, e64t)
    d64, tot64 = heads64[0:1, :], heads64[1:2, :]
    xdt = xs * dt64
    cb = bdot(cm, bm, 1, 1)
    li = lax.broadcasted_iota(jnp.int32, (CHUNK, CHUNK), 0)
    si = lax.broadcasted_iota(jnp.int32, (CHUNK, CHUNK), 1)
    causal = li >= si
    lane = lax.broadcasted_iota(jnp.int32, (CHUNK, LANE), 1)
    xw = xdt * jnp.exp(tot64 - cs64)
    ecs = jnp.exp(cs64)
    etot = jnp.exp(tot64)
    ycols, hout = [], []
    for j in range(4):
        sl = slice(j * LANE, (j + 1) * LANE)
        xj = xdt[:, sl]
        ys = []
        for hh in range(2):
            r = 2 * j + hh
            col = cs128[:, r * LANE:(r + 1) * LANE]
            decay = jnp.exp(jnp.where(causal, col - col.T, -1e30))
            ys.append(bdot(cb * decay, xj, 1, 0))
        y_diag = jnp.where(lane < SSM_P, ys[0], ys[1])
        y_off = bdot(cm, hin[j], 1, 0) * ecs[:, sl]
        ycols.append(y_diag + y_off)
        hout.append(etot[:, sl] * hin[j] + bdot(bm, xw[:, sl], 0, 0))
    y = jnp.concatenate(ycols, axis=1) + d64 * xs
    y = y * jax.nn.silu(z)
    yn = y * lax.rsqrt(jnp.mean(y * y, axis=-1, keepdims=True) + RMS_EPS) * nw
    return (yn,) + tuple(hout)


def _xbc_group(a, axis):
    parts = []
    for g in range(SSM_GROUPS):
        for start, width in ((g * SSM_GSZ, SSM_GSZ), (SSM_INNER + g * SSM_N, SSM_N), (SSM_INNER + (SSM_GROUPS + g) * SSM_N, SSM_N)):
            parts.append(lax.slice_in_dim(a, start, start + width, axis=axis))
    return jnp.concatenate(parts, axis=axis)


def _xbc_ungroup(a, axis):
    xs, bs, cs = [], [], []
    for g in range(SSM_GROUPS):
        base = g * SSM_XBC_G
        xs.append(lax.slice_in_dim(a, base, base + SSM_GSZ, axis=axis))
        bs.append(lax.slice_in_dim(a, base + SSM_GSZ, base + SSM_GSZ + SSM_N, axis=axis))
        cs.append(lax.slice_in_dim(a, base + SSM_GSZ + SSM_N, base + SSM_XBC_G, axis=axis))
    return jnp.concatenate(xs + bs + cs, axis=axis)


def _ssd_consts():
    h = np.arange(LANE)[:, None]
    e64 = np.stack([(h == g * 8 + np.arange(SSM_GSZ)[None, :] // SSM_P) for g in range(SSM_GROUPS)]).astype(np.float32)
    e128 = np.stack([(h == g * 8 + np.arange(8 * LANE)[None, :] // LANE) for g in range(SSM_GROUPS)]).astype(np.float32)
    ecat = np.concatenate([e64, e128], axis=2)
    tril = np.tril(np.ones((CHUNK, CHUNK), np.float32))
    return tuple(jnp.asarray(c, dtype=BF) for c in (e64, e64.transpose(0, 2, 1), ecat, ecat.transpose(0, 2, 1), tril, tril.T))


def _ssd_specs(nc, rev):
    def ci(c):
        return nc - 1 - c if rev else c

    def row(width, col):
        return pl.BlockSpec((CHUNK, width), lambda b, c: (b * nc + ci(c), col))

    def whole(shape):
        return pl.BlockSpec(shape, lambda b, c: (0,) * len(shape))

    data = [row(SSM_CONV_DIM, 0),
            row(SSM_GSZ, 1), row(SSM_GSZ, 2), row(LANE, 24)]
    par = [whole((1, LANE))] * 3 + [whole((1, SSM_INNER))]
    cst = [whole((SSM_GROUPS, LANE, SSM_GSZ)), whole((SSM_GROUPS, SSM_GSZ, LANE)), whole((SSM_GROUPS, LANE, 12 * LANE)),
           whole((SSM_GROUPS, 12 * LANE, LANE)), whole((CHUNK, CHUNK)), whole((CHUNK, CHUNK))]
    hsave = pl.BlockSpec((None, None, SSM_GROUPS, 4, SSM_N, LANE), lambda b, c: (b, ci(c), 0, 0, 0, 0))
    return data, par, cst, hsave, row, whole


def _ssd_group_args(g, xbc, z, dtr, dtb, alog, dsk, nw):
    return (xbc[:, g * SSM_XBC_G:(g + 1) * SSM_XBC_G], z[g], dtr, dtb, alog, dsk, nw[:, g * SSM_GSZ:(g + 1) * SSM_GSZ])


def ssd_fwd(xbc_act, u, dtb, alog, dsk, nw, consts, bsz, seq):
    nc = seq // CHUNK
    data, par, cst, hsave, row, _ = _ssd_specs(nc, False)

    def body(xbc, z0, z1, dtr, dtb_r, alog_r, dsk_r, nw_r, e64, e64t, ecat, ecatt, tril, trilt, yn_ref, hs_ref, h):
        @pl.when(pl.program_id(1) == 0)
        def _():
            h[...] = jnp.zeros_like(h)

        hs_ref[...] = h[...]
        ys = []
        for g in range(SSM_GROUPS):
            args = _ssd_group_args(g, xbc[...], (z0[...], z1[...]), dtr[...], dtb_r[...], alog_r[...], dsk_r[...], nw_r[...])
            outs = ssd_chunk(*args, h[g, 0], h[g, 1], h[g, 2], h[g, 3], e64[g], e64t[g], ecat[g], ecatt[g], tril[...], trilt[...])
            ys.append(outs[0])
            for j in range(4):
                h[g, j] = outs[1 + j]
        yn_ref[...] = jnp.concatenate(ys, axis=1).astype(yn_ref.dtype)

    t = bsz * seq
    return pl.pallas_call(
        body, name="ssd_fwd", grid=(bsz, nc), in_specs=data + par + cst, out_specs=[row(SSM_INNER, 0), hsave],
        out_shape=[jax.ShapeDtypeStruct((t, SSM_INNER), BF), jax.ShapeDtypeStruct((bsz, nc, SSM_GROUPS, 4, SSM_N, LANE), F32)],
        scratch_shapes=[pltpu.VMEM((SSM_GROUPS, 4, SSM_N, LANE), F32)], compiler_params=_params(),
    )(xbc_act, u, u, u, dtb, alog, dsk, nw, *consts)


def ssd_bwd(xbc_act, u, dtb, alog, dsk, nw, consts, hs, dmix, bsz, seq):
    nc = seq // CHUNK
    data, par, cst, hsave, row, whole = _ssd_specs(nc, True)
    t = bsz * seq
    pcol = POOL_W // SSM_GSZ

    def body(xbc, z0, z1, dtr, dtb_r, alog_r, dsk_r, nw_r, e64, e64t, ecat, ecatt, tril, trilt, hs_ref, dy0, dy1,
             dxbc, dz, ddt, ddtb, dalog, ddsk, dnw, dh):
        @pl.when(pl.program_id(1) == 0)
        def _():
            dh[...] = jnp.zeros_like(dh)

        per_group = []
        for g, dyn in enumerate((dy0, dy1)):
            cst_vals = (e64[g], e64t[g], ecat[g], ecatt[g], tril[...], trilt[...])
            prim = _ssd_group_args(g, xbc[...], (z0[...], z1[...]), dtr[...], dtb_r[...], alog_r[...], dsk_r[...], nw_r[...])
            prim = prim + (hs_ref[g, 0], hs_ref[g, 1], hs_ref[g, 2], hs_ref[g, 3])
            _, vjp = jax.vjp(lambda *args, c=cst_vals: ssd_chunk(*args, *c), *prim)
            gr = vjp((dyn[...].astype(F32), dh[g, 0], dh[g, 1], dh[g, 2], dh[g, 3]))
            for j in range(4):
                dh[g, j] = gr[7 + j]
            per_group.append(gr)
        g0, g1 = per_group
        dxbc[...] = jnp.concatenate([g0[0], g1[0]], axis=1)
        dz[...] = jnp.concatenate([g0[1], g1[1]], axis=1).astype(dz.dtype)
        ddt[...] = g0[2] + g1[2]

        @pl.when(_first((0, 1)))
        def _():
            for r in (ddtb, dalog, ddsk, dnw):
                r[...] = jnp.zeros_like(r)

        ddtb[...] += g0[3] + g1[3]
        dalog[...] += g0[4] + g1[4]
        ddsk[...] += g0[5] + g1[5]
        dnw[...] += jnp.concatenate([g0[6], g1[6]], axis=1)

    out_specs = [row(SSM_CONV_DIM, 0), row(SSM_INNER, 0), row(LANE, 0), whole((1, LANE)), whole((1, LANE)), whole((1, LANE)),
                 whole((1, SSM_INNER))]
    lane = jax.ShapeDtypeStruct((1, LANE), F32)
    out_shape = [jax.ShapeDtypeStruct((t, SSM_CONV_DIM), F32), jax.ShapeDtypeStruct((t, SSM_INNER), BF),
                 jax.ShapeDtypeStruct((t, LANE), F32), lane, lane, lane, jax.ShapeDtypeStruct((1, SSM_INNER), F32)]
    return pl.pallas_call(
        body, name="ssd_bwd", grid=(bsz, nc), in_specs=data + par + cst + [hsave, row(SSM_GSZ, pcol), row(SSM_GSZ, pcol + 1)],
        out_specs=out_specs, out_shape=out_shape, scratch_shapes=[pltpu.VMEM((SSM_GROUPS, 4, SSM_N, LANE), F32)],
        compiler_params=_params(),
    )(xbc_act, u, u, u, dtb, alog, dsk, nw, *consts, hs, dmix, dmix)


TB = 512


def _rows(d, col=0):
    return pl.BlockSpec((TB, d), lambda i: (i, col))


def _par(d):
    return pl.BlockSpec((1, d), lambda i: (0, 0))


def _sd(shape, dtype=F32):
    return jax.ShapeDtypeStruct(shape, dtype)


def _round_up(n, m):
    return -(-n // m) * m


def _pad_rows(a, rows):
    return jnp.pad(a, ((0, rows - a.shape[0]), (0, 0)))


def _pack128(arrs):
    flat = jnp.concatenate([a.reshape(-1) for a in arrs])
    n = flat.shape[0]
    rows = -(-n // (8 * LANE)) * 8
    return jnp.pad(flat, (0, rows * LANE - n)).reshape(rows, LANE)


def _unpack128(packed, shapes):
    flat = packed.reshape(-1)
    out, off = [], 0
    for s in shapes:
        n = int(np.prod(s))
        out.append(flat[off:off + n].reshape(s))
        off += n
    return out


def kernel(x, mem, norm_gains, xa_wq, xa_wkv, xa_wo, mlp_w1, mlp_w2, ab_w_in, pool_w, pool_scale, ssm_conv_w, ssm_conv_b, ssm_dt_bias, ssm_a_log, ssm_d, ssm_norm, ab_w_out, cd_w_in, conf_dw_w, conf_dw_b, conf_ln_g, conf_ln_b, sc_conv_w, cd_w_out, loss_target, m_norm_gains, m_xa_wq, m_xa_wkv, m_xa_wo, m_mlp_w1, m_mlp_w2, m_ab_w_in, m_pool_w, m_pool_scale, m_ssm_conv_w, m_ssm_conv_b, m_ssm_dt_bias, m_ssm_a_log, m_ssm_d, m_ssm_norm, m_ab_w_out, m_cd_w_in, m_conf_dw_w, m_conf_dw_b, m_conf_ln_g, m_conf_ln_b, m_sc_conv_w, m_cd_w_out, v_norm_gains, v_xa_wq, v_xa_wkv, v_xa_wo, v_mlp_w1, v_mlp_w2, v_ab_w_in, v_pool_w, v_pool_scale, v_ssm_conv_w, v_ssm_conv_b, v_ssm_dt_bias, v_ssm_a_log, v_ssm_d, v_ssm_norm, v_ab_w_out, v_cd_w_in, v_conf_dw_w, v_conf_dw_b, v_conf_ln_g, v_conf_ln_b, v_sc_conv_w, v_cd_w_out):
    args = locals()
    w = {n: args[n] for n in WEIGHTS}
    mom_m = {n: args["m_" + n] for n in WEIGHTS}
    mom_v = {n: args["v_" + n] for n in WEIGHTS}
    ex = Exchange(w)
    loss_local, grad_x, small_grads = local_step(x, mem, loss_target, ex)
    outs = {}

    started = ex.put_small(small_grads, loss_local)
    landed = {key: ex.landed(key, started) for key in ('l1', 'cd', 'l0')}
    late = []
    for n, keys in (('mlp_w1', ('l0', 'l1')), ('mlp_w2', ('l0', 'l1')), ('xa_wkv', ('l0', 'l1')), ('xa_wq', ('l0', 'l1')),
                    ('xa_wo', ('l0', 'l1')), ('cd_w_in', ('cd',)), ('cd_w_out', ('cd',))):
        lands = [landed[key][0] for key in keys]
        offs = [landed[key][1][(n, layer)] for layer, key in enumerate(keys)]
        outs[n] = update_from_slots(lands, offs, w[n], mom_m[n], mom_v[n], SHARD_AXIS[n] == 2, "update_" + n)
        late.append(outs[n][1])
    g_own, loss = ex.reduced_small(late)
    land_ab, offs_ab = ex.landed('ab', late)
    outs['ab_w_out'] = update_from_slots([land_ab], [offs_ab[('ab_w_out', 0)]], w['ab_w_out'], mom_m['ab_w_out'],
                                         mom_v['ab_w_out'], False, "update_ab_w_out")
    res = update_from_slots([land_ab], [offs_ab[('ab_w_in', 0)]], jnp.swapaxes(w['ab_w_in'], 1, 2), jnp.swapaxes(mom_m['ab_w_in'], 1, 2),
                            jnp.swapaxes(mom_v['ab_w_in'], 1, 2), False, "update_ab_w_in")
    outs['ab_w_in'] = tuple(jnp.swapaxes(r, 1, 2) for r in res)
    small = SMALL_SHARDED + REPLICATED
    upd = adamw_many([w[n] for n in small], [mom_m[n] for n in small], [mom_v[n] for n in small], [g_own[n] for n in small],
                     "adamw_small")
    for i, n in enumerate(small):
        outs[n] = (g_own[n], upd[0][i], upd[1][i], upd[2][i])
    return (loss, grad_x.reshape(x.shape), *[outs[n][0] for n in WEIGHTS], *[outs[n][1] for n in WEIGHTS],
            *[outs[n][2] for n in WEIGHTS], *[outs[n][3] for n in WEIGHTS])


G_AB = (('ab_w_in', 0), ('ab_w_out', 0))
G_L0 = (('xa_wq', 0), ('xa_wkv', 0), ('xa_wo', 0), ('mlp_w1', 0), ('mlp_w2', 0))
G_L1 = (('xa_wq', 1), ('xa_wkv', 1), ('xa_wo', 1), ('mlp_w1', 1), ('mlp_w2', 1))
G_CD = (('cd_w_in', 0), ('cd_w_out', 0))
GATHER_GROUPS = {'ab': G_AB[:1], 'l0a': G_AB[1:] + G_L0[:3], 'l0b': G_L0[3:], 'cd': G_CD, 'l1a': G_L1[:3], 'l1b': G_L1[3:]}
SHARD_AXIS = dict(BIG)
MEMBER_ROW_TILE = 64
FLAT_ROW_TILE = 128


def _members(group, w):
    out = []
    for n, layer in group:
        shp = w[n].shape[1:]
        if SHARD_AXIS[n] == 2:
            shp = (shp[1], shp[0])
        assert shp[1] == D, (n, shp)
        out.append((n, layer, shp, shp[0], _round_up(shp[0], MEMBER_ROW_TILE)))
    return out


def _group_rows(group, w):
    return _round_up(sum(m[4] for m in _members(group, w)), FLAT_ROW_TILE)


def _flat_shards(group, w):
    parts = []
    for n, layer, _, _, padded in _members(group, w):
        shard = w[n][layer].astype(BF)
        parts.append(_pad_rows(shard.T if SHARD_AXIS[n] == 2 else shard, padded))
    return _pad_rows(jnp.concatenate(parts, axis=0), _group_rows(group, w))


def _full_from_slots(land, group, w):
    out, off = {}, 0
    for n, layer, shp, rows, padded in _members(group, w):
        out[(n, layer)] = land[:, off:off + rows].reshape(N_DEV * rows, D)
        off += padded
    return out


def _slots_from_full(grads, group, w):
    parts = []
    for n, layer, shp, rows, padded in _members(group, w):
        blk = grads[(n, layer)].astype(BF).reshape(N_DEV, rows, D)
        parts.append(jnp.pad(blk, ((0, 0), (0, padded - rows), (0, 0))))
    send = jnp.concatenate(parts, axis=1)
    return jnp.pad(send, ((0, 0), (0, _group_rows(group, w) - send.shape[1]), (0, 0)))


_HBM = pl.BlockSpec(memory_space=pltpu.HBM)
_SEM = pl.BlockSpec(memory_space=pltpu.SEMAPHORE)
_ANY = pl.BlockSpec(memory_space=pl.ANY)


def _peer_copy(k, src, dst, send_sems, recv_sems, peer):
    return pltpu.make_async_remote_copy(src_ref=src, dst_ref=dst, send_sem=send_sems.at[k], recv_sem=recv_sems.at[k],
                                        device_id=peer, device_id_type=pl.DeviceIdType.MESH)


def exchange_start(src, name, scatter, after=()):
    shape = src.shape[-2:]
    after = list(after)

    def body(src_ref, land_ref, *rest):
        send_sems, recv_sems, token = rest[len(after)], rest[len(after) + 1], rest[-1]
        me = _me()
        for k, f in enumerate(_FLIPS):
            peer = _flip(me, f)
            piece = src_ref.at[_slot(peer)] if scatter else src_ref
            _peer_copy(k, piece, land_ref.at[_slot(me)], send_sems, recv_sems, peer).start()
        token[...] = jnp.zeros_like(token)

    land = pltpu.with_memory_space_constraint(lax.empty((N_DEV,) + shape, src.dtype), pltpu.HBM)
    return pl.pallas_call(
        body, name=name,
        out_shape=(pltpu.SemaphoreType.DMA((7,)), pltpu.SemaphoreType.DMA((7,)), pltpu.HBM(src.shape, src.dtype),
                   pltpu.HBM((N_DEV,) + shape, src.dtype), jax.ShapeDtypeStruct((8, LANE), F32)),
        in_specs=(_HBM, _HBM) + (_ANY,) * len(after), out_specs=(_SEM, _SEM, _HBM, _HBM, pl.BlockSpec(memory_space=pltpu.VMEM)),
        input_output_aliases={0: 2, 1: 3},
        compiler_params=pltpu.CompilerParams(has_side_effects=pltpu.SideEffectType.DATAFLOW_SIDE_EFFECTING),
    )(pltpu.with_memory_space_constraint(src, pltpu.HBM), land, *after)


def exchange_wait(handles, after, name, scatter):
    send_sems, recv_sems, src_thru, land_thru, _ = handles
    after = list(after) if isinstance(after, (list, tuple)) else [after]

    def body(src_ref, land_ref, send_sems, recv_sems, *rest):
        token = rest[-1]
        me = _me()
        for k, f in enumerate(_FLIPS):
            peer = _flip(me, f)
            piece = src_ref.at[_slot(peer)] if scatter else src_ref
            cp = _peer_copy(k, piece, land_ref.at[_slot(peer)], send_sems, recv_sems, peer)
            cp.wait_send()
            cp.wait_recv()
        token[...] = jnp.zeros_like(token)

    return pl.pallas_call(
        body, name=name, out_shape=(pltpu.HBM(src_thru.shape, src_thru.dtype), pltpu.HBM(land_thru.shape, land_thru.dtype),
                                    jax.ShapeDtypeStruct((8, LANE), F32)),
        in_specs=(_HBM, _HBM, _SEM, _SEM) + (_ANY,) * len(after), out_specs=(_HBM, _HBM, pl.BlockSpec(memory_space=pltpu.VMEM)),
        input_output_aliases={0: 0, 1: 1},
        compiler_params=pltpu.CompilerParams(has_side_effects=pltpu.SideEffectType.DATAFLOW_SIDE_EFFECTING),
    )(src_thru, land_thru, send_sems, recv_sems, *after)


class Exchange:
    def __init__(self, w):
        self.w = w
        self.me = _slot(_me())
        shapes = [w[n].shape for n in SMALL_SHARDED]
        gs = all_gather(_pack128([w[n] for n in SMALL_SHARDED]), "gather_small")
        per_dev = [_unpack128(gs[d], shapes) for d in range(N_DEV)]
        self.small = {n: jnp.concatenate([per_dev[d][i] for d in range(N_DEV)], axis=-1) for i, n in enumerate(SMALL_SHARDED)}
        self.small.update({n: w[n] for n in REPLICATED})
        first = all_gather(_flat_shards(GATHER_GROUPS['ab'], w), "gather_ab")
        self.first = _full_from_slots(first, GATHER_GROUPS['ab'], w)
        self.gathers, self.done, self.tokens, self.reductions = {}, {}, [], {}
        self.start_gather('l0a', after=[first])
        self.start_gather('l0b', after=[self.gathers['l0a'][4]])

    def take_tokens(self):
        toks, self.tokens = self.tokens, []
        return toks

    def start_gather(self, key, after=()):
        group = GATHER_GROUPS[key]
        self.gathers[key] = exchange_start(_flat_shards(group, self.w), f"gather_{key}_start", False, after=after)
        self.tokens.append(self.gathers[key][4])

    def weights(self, key, after):
        if key == 'ab':
            return self.first
        handles = self.gathers[key]
        _, land, self.done[key] = exchange_wait(handles, after, f"gather_{key}_wait", False)
        land = lax.dynamic_update_slice(land, handles[2][None], (self.me, 0, 0))
        return _full_from_slots(land, GATHER_GROUPS[key], self.w)

    def put_grads(self, key, group, grads):
        send = _slots_from_full(grads, group, self.w)
        handles = exchange_start(send, f"reduce_{key}_start", True)
        self.reductions[key] = (group, handles)
        self.tokens.append(handles[4])

    def landed(self, key, after):
        group, handles = self.reductions[key]
        send, land, _ = exchange_wait(handles, after, f"reduce_{key}_wait", True)
        mine = lax.dynamic_slice_in_dim(send, self.me, 1, axis=0)
        land = lax.dynamic_update_slice(land, mine, (self.me, 0, 0))
        offs, off = {}, 0
        for n, layer, _, _, padded in _members(group, self.w):
            offs[(n, layer)] = off
            off += padded
        return land, offs

    def put_small(self, small_grads, loss_local):
        small = SMALL_SHARDED + REPLICATED
        self.small_shapes = [small_grads[n].shape for n in small] + [(1,)]
        packed = _pack128([small_grads[n] for n in small] + [loss_local.reshape(1)])
        self.small_handles = exchange_start(packed, "gather_small_grads_start", False)
        return self.small_handles[4]

    def reduced_small(self, after):
        small = SMALL_SHARDED + REPLICATED
        src, land, _ = exchange_wait(self.small_handles, after, "gather_small_grads_wait", False)
        gs = lax.dynamic_update_slice(land, src[None], (self.me, 0, 0))
        tot = _unpack128(sum_slots(gs, "sum_small", 1024), self.small_shapes)
        out = {}
        for n, g in zip(small, tot):
            if n in SMALL_SHARDED:
                width = self.w[n].shape[-1]
                g = lax.dynamic_slice_in_dim(g, self.me * width, width, axis=g.ndim - 1)
            out[n] = g
        return out, tot[-1].reshape(())


def local_step(x, mem, target, ex):
    bsz, seq, _ = x.shape
    t = bsz * seq
    nb = t // TB
    nc = seq // CHUNK
    x0 = x.reshape(t, D)
    mem2 = mem.reshape(bsz * N_MEM, D)
    tgt = target.reshape(t, D)
    p = ex.small
    gains = p['norm_gains']
    big = {}

    def gain(layer, i):
        g = gains[layer, i].reshape(1, D)
        for tok in ex.take_tokens():
            g = g + tok[0, 0]
        return g

    consts = _ssd_consts()
    grads = {}
    saved = [dict(), dict()]

    def matmul_res(a, b, name, xin, ga, gb):
        return matmul(a, b, 'nn', name, (F32, F32, BF), epilogue=res_epilogue, extras=[xin], params=[ga, gb])

    def attn_specs():
        nq = seq // TB
        q = pl.BlockSpec((TB, D), lambda b, i: (b * nq + i, 0))
        kv = pl.BlockSpec((N_MEM, 2 * D), lambda b, i: (b, 0))
        return (bsz, nq), q, kv

    def attention_fwd(layer, xin, hin, sv, ga, gb):
        q = matmul(hin, big[('xa_wq', layer)], 'nn', f"q_{layer}", BF)
        kv = matmul(mem2, big[('xa_wkv', layer)], 'nt', f"kv_{layer}", BF)
        grid, qs, kvs = attn_specs()
        o, = fwd_call(attn_fn, f"attn_{layer}", grid, [q, kv], [qs, kvs], [_sd((t, D), BF)], [qs])
        ao, x_next, h_next = matmul_res(o, big[('xa_wo', layer)], f"ao_{layer}", xin, ga, gb)
        sv.update(q=q, kv=kv, o=o, ao=ao)
        return ao, x_next, h_next

    def mlp_fwd(layer, hin, sv, res):
        r, rr = matmul(hin, big[('mlp_w1', layer)], 'nt', f"mlp1_{layer}", (BF, BF), epilogue=act_epilogue)
        out = matmul_res(rr, big[('mlp_w2', layer)], f"mlp2_{layer}", *res)
        sv.update(r=r, rr=rr, mo=out[0])
        return out

    sv = saved[0]
    h0, = fwd_call(seg_in, "norm_in", (nb,), [x0, gain(0, 0)], [_rows(D), _par(D)], [_sd((t, D), BF)], [_rows(D)])
    big.update(ex.weights('ab', h0))
    xbc0 = POOL_W + SSM_INNER
    w_ab_in = big[('ab_w_in', 0)]
    w_ab_in = _pad_rows(jnp.concatenate([w_ab_in[:xbc0], _xbc_group(w_ab_in[xbc0:xbc0 + SSM_CONV_DIM], 0),
                                         w_ab_in[xbc0 + SSM_CONV_DIM:]], axis=0), AB_IN_PAD)
    conv_w, conv_b = _xbc_group(p['ssm_conv_w'][0], 1), _xbc_group(p['ssm_conv_b'], 1)
    u0 = matmul(h0, w_ab_in, 'nt', "ab_in")
    pool_outs = []
    for g in range(POOL_GROUPS):
        seqspec = pl.BlockSpec((seq, PG), lambda b, g=g: (b, g))
        po, = fwd_call(make_pool_fn(g), f"pool_{g}", (bsz,), [u0, p['pool_w'][0, g], p['pool_scale']],
                       [seqspec, pl.BlockSpec((PG, PG), lambda b: (0, 0)), pl.BlockSpec((1, PG), lambda b, g=g: (0, g))],
                       [_sd((t, PG), BF)], [pl.BlockSpec((seq, PG), lambda b: (b, 0))])
        pool_outs.append(po)
    cw = 256
    ncb = SSM_CONV_DIM // cw
    cbase = (POOL_W + SSM_INNER) // cw
    conv_in_specs = [pl.BlockSpec((seq, cw), lambda j, b: (b, cbase + j)), pl.BlockSpec((SSM_CONV, cw), lambda j, b: (0, j)),
                     pl.BlockSpec((1, cw), lambda j, b: (0, j))]
    conv_out_spec = pl.BlockSpec((seq, cw), lambda j, b: (b, j))
    xbc_act, = fwd_call(conv4_fn, "ssm_conv", (ncb, bsz), [u0, conv_w, conv_b], conv_in_specs,
                        [_sd((t, SSM_CONV_DIM))], [conv_out_spec])
    dtb = jnp.pad(p['ssm_dt_bias'], ((0, 0), (0, LANE - SSM_HEADS)))
    alog = jnp.pad(p['ssm_a_log'], ((0, 0), (0, LANE - SSM_HEADS)))
    dsk = jnp.pad(p['ssm_d'], ((0, 0), (0, LANE - SSM_HEADS)))
    yn, hs = ssd_fwd(xbc_act, u0, dtb, alog, dsk, p['ssm_norm'], consts, bsz, seq)
    mix0 = jnp.concatenate(pool_outs + [yn], axis=1)
    big.update(ex.weights('l0a', yn))
    ex.start_gather('cd', after=[ex.done['l0a']])
    ex.start_gather('l1a', after=[ex.gathers['cd'][4]])
    ex.start_gather('l1b', after=[ex.gathers['l1a'][4]])
    m0, x1, h2 = matmul_res(mix0, big[('ab_w_out', 0)], "ab_out", x0, gain(0, 1), gain(0, 2))
    ao0, x2, h3 = attention_fwd(0, x1, h2, sv, gain(0, 3), gain(0, 4))
    big.update(ex.weights('l0b', h3))
    mo0, x3, h4 = mlp_fwd(0, h3, sv, (x2, gain(0, 5), gain(1, 0)))
    big.update(ex.weights('cd', mo0))

    sv1 = saved[1]
    nd = D // LANE
    w_cd_in = big[('cd_w_in', 0)].reshape(5, nd, LANE, D).transpose(1, 0, 2, 3).reshape(CD_IN, D)
    u1 = matmul(h4, w_cd_in, 'nt', "cd_in")
    cd_par = [pl.BlockSpec((CONF_K, LANE), lambda j, b: (0, j)), pl.BlockSpec((1, LANE), lambda j, b: (0, j)),
              pl.BlockSpec((SC_K, LANE), lambda j, b: (0, j))]
    cd_ins = [u1, p['conf_dw_w'][0], p['conf_dw_b'], p['sc_conv_w'][0]]
    cd_u_spec = pl.BlockSpec((seq, 5 * LANE), lambda j, b: (b, j))
    cd_in_specs = [cd_u_spec] + cd_par
    cd_out_spec = pl.BlockSpec((seq, LANE), lambda j, b: (b, j))
    vconv, mix1 = fwd_call(cd1_fn, "cd_conv", (nd, bsz), cd_ins, cd_in_specs, [_sd((t, D)), _sd((t, CD_OUT), BF)],
                           [cd_out_spec, pl.BlockSpec((seq, LANE), lambda j, b: (b, nd + j))])
    mix1, = fwd_call(seg_ln, "conf_ln", (nb,), [vconv, p['conf_ln_g'], p['conf_ln_b']], [_rows(D), _par(D), _par(D)],
                     [_sd((t, CD_OUT), BF)], [_rows(D)], into=mix1)
    m1, x4, h5 = matmul_res(mix1, big[('cd_w_out', 0)], "cd_out", x3, gain(1, 1), gain(1, 2))
    big.update(ex.weights('l1a', h5))
    ao1, x5, h6 = attention_fwd(1, x4, h5, sv1, gain(1, 3), gain(1, 4))
    big.update(ex.weights('l1b', h6))
    r1, rr1 = matmul(h6, big[('mlp_w1', 1)], 'nt', "mlp1_1", (BF, BF), epilogue=act_epilogue)
    sv1.update(r=r1, rr=rr1)
    dx5, dmo1, dg15, lanes = matmul(rr1, big[('mlp_w2', 1)], 'nn', "mlp2_1", (F32, BF), epilogue=loss_epilogue, extras=[x5, tgt],
                                    params=[gain(1, 5)], n_acc=2)
    loss = 0.5 * jnp.sum(lanes) / float(D)

    gain_grads = {(1, 5): dg15}

    def matmul_res_bwd(a, b, mode, name, xin, m, ga, gb, dx1):
        return list(matmul(a, b, mode, name, (F32, BF), epilogue=res_bwd_epilogue, extras=[xin, m, dx1], params=[ga, gb], n_acc=2))

    def mlp_bwd(layer, hin, dmo, sv, res):
        grads_w2 = matmul(sv['rr'], dmo, 'tn', f"d_mlp_w2_{layer}", BF)
        dr, = matmul(dmo, big[('mlp_w2', layer)], 'nt', f"d_r_{layer}", (BF,), epilogue=act_bwd_epilogue, extras=[sv['r']])
        grads_w1 = matmul(dr, hin, 'tn', f"d_mlp_w1_{layer}", BF)
        return matmul_res_bwd(dr, big[('mlp_w1', layer)], 'nn', f"d_h_mlp_{layer}", *res) + [grads_w1, grads_w2]

    def attention_bwd(layer, hin, dao, sv, res):
        g_wo = matmul(sv['o'], dao, 'tn', f"d_xa_wo_{layer}", BF)
        do = matmul(dao, big[('xa_wo', layer)], 'nt', f"d_o_{layer}", BF)
        grid, qs, kvs = attn_specs()
        dq, dkv = bwd_call(attn_fn, f"d_attn_{layer}", grid, [sv['q'], sv['kv']], [qs, kvs], [do], [qs], [0, 1],
                           [_sd((t, D), BF), _sd((bsz * N_MEM, 2 * D))], [qs, kvs], [None, (1,)])
        g_wkv = matmul(dkv, mem2, 'tn', f"d_xa_wkv_{layer}", BF)
        g_wq = matmul(hin, dq, 'tn', f"d_xa_wq_{layer}", BF)
        return matmul_res_bwd(dq, big[('xa_wq', layer)], 'nt', f"d_h_attn_{layer}", *res) + [g_wq, g_wkv, g_wo]

    per_layer = {k: [None, None] for k in ('xa_wq', 'xa_wkv', 'xa_wo', 'mlp_w1', 'mlp_w2')}

    (dx4, dao1, gain_grads[(1, 3)], gain_grads[(1, 4)], per_layer['mlp_w1'][1],
     per_layer['mlp_w2'][1]) = mlp_bwd(1, h6, dmo1, sv1, (x4, ao1, gain(1, 3), gain(1, 4), dx5))
    (dx3, dm1, gain_grads[(1, 1)], gain_grads[(1, 2)], per_layer['xa_wq'][1], per_layer['xa_wkv'][1],
     per_layer['xa_wo'][1]) = attention_bwd(1, h5, dao1, sv1, (x3, m1, gain(1, 1), gain(1, 2), dx4))
    ex.put_grads('l1', G_L1, {(k, 1): v[1] for k, v in per_layer.items()})
    g_cd_out = matmul(mix1, dm1, 'tn', "d_cd_w_out", BF)
    dmix1 = matmul(dm1, big[('cd_w_out', 0)], 'nt', "d_mix1", after=ex.take_tokens())
    dvconv, dlg, dlb = bwd_call(seg_ln, "d_conf_ln", (nb,), [vconv, p['conf_ln_g'], p['conf_ln_b']],
                                [_rows(D), _par(D), _par(D)], [dmix1], [_rows(D, 0)], [0, 1, 2],
                                [_sd((t, D)), _sd((1, D)), _sd((1, D))], [_rows(D), _par(D), _par(D)], [None, (0,), (0,)])
    grads['conf_ln_g'], grads['conf_ln_b'] = dlg, dlb
    cd_g = bwd_call(cd1_fn, "d_cd_conv", (nd, bsz), cd_ins, cd_in_specs, [dvconv, dmix1],
                    [cd_out_spec, pl.BlockSpec((seq, LANE), lambda j, b: (b, nd + j))], list(range(4)),
                    [_sd((t, CD_IN), BF), _sd((CONF_K, D)), _sd((1, D)), _sd((SC_K, D))], [cd_u_spec] + cd_par,
                    [None, (1,), (1,), (1,)])
    du1 = cd_g[0]
    grads['conf_dw_w'], grads['conf_dw_b'], grads['sc_conv_w'] = cd_g[1][None], cd_g[2], cd_g[3][None]
    g_cd_in = matmul(du1, h4, 'tn', "d_cd_w_in", BF).reshape(nd, 5, LANE, D).transpose(1, 0, 2, 3).reshape(CD_IN, D)
    ex.put_grads('cd', G_CD, {('cd_w_in', 0): g_cd_in, ('cd_w_out', 0): g_cd_out})
    dx2, dmo0, gain_grads[(0, 5)], gain_grads[(1, 0)] = matmul_res_bwd(du1, w_cd_in, 'nn', "d_h_cd", x2, mo0, gain(0, 5),
                                                                       gain(1, 0), dx3)
    (dx1, dao0, gain_grads[(0, 3)], gain_grads[(0, 4)], per_layer['mlp_w1'][0],
     per_layer['mlp_w2'][0]) = mlp_bwd(0, h3, dmo0, sv, (x1, ao0, gain(0, 3), gain(0, 4), dx2))
    (dx0r, dm0, gain_grads[(0, 1)], gain_grads[(0, 2)], per_layer['xa_wq'][0], per_layer['xa_wkv'][0],
     per_layer['xa_wo'][0]) = attention_bwd(0, h2, dao0, sv, (x0, m0, gain(0, 1), gain(0, 2), dx1))
    ex.put_grads('l0', G_L0, {(k, 0): v[0] for k, v in per_layer.items()})
    g_ab_out = matmul(mix0, dm0, 'tn', "d_ab_w_out", BF)
    dmix0 = matmul(dm0, big[('ab_w_out', 0)], 'nt', "d_mix0", after=ex.take_tokens())
    dxbc_act, dz, ddt, ddtb, dalog, ddsk, dnw = ssd_bwd(xbc_act, u0, dtb, alog, dsk, p['ssm_norm'], consts, hs, dmix0, bsz, seq)
    grads['ssm_dt_bias'] = ddtb[:, :SSM_HEADS]
    grads['ssm_a_log'] = dalog[:, :SSM_HEADS]
    grads['ssm_d'] = ddsk[:, :SSM_HEADS]
    grads['ssm_norm'] = dnw
    dxr, dcw, dcb = bwd_call(conv4_fn, "d_ssm_conv", (ncb, bsz), [u0, conv_w, conv_b], conv_in_specs,
                             [dxbc_act], [conv_out_spec], [0, 1, 2],
                             [_sd((t, SSM_CONV_DIM), BF), _sd((SSM_CONV, SSM_CONV_DIM)), _sd((1, SSM_CONV_DIM))],
                             [conv_out_spec, conv_in_specs[1], conv_in_specs[2]], [None, (1,), (1,)])
    grads['ssm_conv_w'], grads['ssm_conv_b'] = _xbc_ungroup(dcw, 1)[None], _xbc_ungroup(dcb, 1)
    dpool, dpw, dps = [], [], []
    for g in range(POOL_GROUPS):
        seqspec = pl.BlockSpec((seq, PG), lambda b, g=g: (b, g))
        one = pl.BlockSpec((seq, PG), lambda b: (b, 0))
        wspec = pl.BlockSpec((PG, PG), lambda b: (0, 0))
        sspec = pl.BlockSpec((1, PG), lambda b, g=g: (0, g))
        a, bb, c = bwd_call(make_pool_fn(g), f"d_pool_{g}", (bsz,), [u0, p['pool_w'][0, g], p['pool_scale']],
                            [seqspec, wspec, sspec], [dmix0], [seqspec], [0, 1, 2],
                            [_sd((t, PG), BF), _sd((PG, PG)), _sd((1, PG))], [one, wspec, pl.BlockSpec((1, PG), lambda b: (0, 0))],
                            [None, (0,), (0,)])
        dpool.append(a)
        dpw.append(bb)
        dps.append(c)
    grads['pool_w'] = jnp.stack(dpw)[None]
    grads['pool_scale'] = jnp.concatenate(dps, axis=1)
    du0 = jnp.concatenate(dpool + [dz, dxr, ddt.astype(BF)], axis=1)
    g_ab_in = matmul(du0, h0, 'tn', "d_ab_w_in", BF)
    g_ab_in = jnp.concatenate([g_ab_in[:xbc0], _xbc_ungroup(g_ab_in[xbc0:xbc0 + SSM_CONV_DIM], 0),
                               g_ab_in[xbc0 + SSM_CONV_DIM:AB_IN]], axis=0)
    ex.put_grads('ab', G_AB, {('ab_w_in', 0): g_ab_in, ('ab_w_out', 0): g_ab_out})
    dx, dg00 = matmul(du0, w_ab_in, 'nn', "d_h_ab", (F32,), epilogue=in_bwd_epilogue, extras=[x0, dx0r], params=[gain(0, 0)],
                      after=ex.take_tokens(), n_acc=1)
    gain_grads[(0, 0)] = dg00
    grads['norm_gains'] = jnp.stack([jnp.concatenate([gain_grads[(l, i)] for i in range(6)], axis=0) for l in range(2)])
    return loss, dx, grads
```

```python
import functools
import math

import numpy as np
import jax
import jax.numpy as jnp
from jax import lax
from jax.experimental import pallas as pl
from jax.experimental.pallas import tpu as pltpu

BF = jnp.bfloat16
F32 = jnp.float32

N_DEV = 8
D = 1024
N_MEM = 256
XA_HEADS = 4
XA_DH = D // XA_HEADS
POOL_GROUPS = 4
PG = 128
POOL_W = POOL_GROUPS * PG
SSM_INNER = 1024
SSM_GROUPS = 2
SSM_GSZ = SSM_INNER // SSM_GROUPS
SSM_HEADS = 16
SSM_P = 64
SSM_N = 128
SSM_CONV = 4
SSM_CONV_DIM = SSM_INNER + 2 * SSM_GROUPS * SSM_N
SSM_XBC_G = SSM_GSZ + 2 * SSM_N
CHUNK = 128
AB_IN = POOL_W + SSM_INNER + SSM_CONV_DIM + SSM_HEADS
AB_IN_PAD = POOL_W + SSM_INNER + SSM_CONV_DIM + 128
AB_OUT = POOL_W + SSM_INNER
CONF_K = 31
SC_K = 3
CD_IN = 5 * D
CD_OUT = 2 * D
MLP_H = 4 * D
RMS_EPS = 1e-6
LN_EPS = 1e-5
ADAM_LR = 0.001
ADAM_B1 = 0.9
ADAM_B2 = 0.999
ADAM_EPS = 1e-08
ADAM_WD = 0.01
ADAM_STEP = 10
VMEM_LIMIT = 56 * 1024 * 1024
LANE = 128

NAMES = ['x', 'mem', 'norm_gains', 'xa_wq', 'xa_wkv', 'xa_wo', 'mlp_w1', 'mlp_w2', 'ab_w_in', 'pool_w', 'pool_scale',
         'ssm_conv_w', 'ssm_conv_b', 'ssm_dt_bias', 'ssm_a_log', 'ssm_d', 'ssm_norm', 'ab_w_out', 'cd_w_in', 'conf_dw_w',
         'conf_dw_b', 'conf_ln_g', 'conf_ln_b', 'sc_conv_w', 'cd_w_out', 'loss_target']
WEIGHTS = NAMES[2:25]
BIG = [('xa_wq', 1), ('xa_wkv', 2), ('xa_wo', 1), ('mlp_w1', 2), ('mlp_w2', 1), ('cd_w_in', 2), ('cd_w_out', 1),
       ('ab_w_out', 1), ('ab_w_in', 2)]
SMALL_SHARDED = ['norm_gains', 'ssm_conv_w', 'conf_dw_w', 'conf_dw_b', 'conf_ln_g', 'conf_ln_b', 'sc_conv_w']
REPLICATED = ['pool_w', 'pool_scale', 'ssm_conv_b', 'ssm_dt_bias', 'ssm_a_log', 'ssm_d', 'ssm_norm']


def _dg(a, b, ca, cb, prec=None):
    return lax.dot_general(a, b, (((ca,), (cb,)), ((), ())), precision=prec, preferred_element_type=F32)


@functools.partial(jax.custom_vjp, nondiff_argnums=(2, 3))
def bdot(a, b, ca, cb):
    return _dg(a.astype(BF), b.astype(BF), ca, cb)


def _bdot_fwd(a, b, ca, cb):
    return bdot(a, b, ca, cb), (a, b)


def _bdot_bwd(ca, cb, res, g):
    a, b = res
    g16, a16, b16 = g.astype(BF), a.astype(BF), b.astype(BF)
    da = _dg(g16, b16, 1, 1 - cb) if ca == 1 else _dg(b16, g16, 1 - cb, 1)
    db = _dg(g16, a16, 0, 1 - ca) if cb == 1 else _dg(a16, g16, 1 - ca, 0)
    return da.astype(a.dtype), db.astype(b.dtype)


bdot.defvjp(_bdot_fwd, _bdot_bwd)


def _split3(a):
    a1 = a.astype(BF)
    r1 = a - a1.astype(F32)
    a2 = r1.astype(BF)
    a3 = (r1 - a2.astype(F32)).astype(BF)
    return a1, a2, a3


def _exact_right(a, c):
    m = a.shape[0]
    if m % 16:
        return sum(_dg(p, c, 1, 0) for p in _split3(a))
    o = _dg(jnp.concatenate(_split3(a), axis=0), c, 1, 0)
    return o[:m] + o[m:2 * m] + o[2 * m:]


def _exact_left(c, a):
    n = a.shape[1]
    o = _dg(c, jnp.concatenate(_split3(a), axis=1), 1, 0)
    return o[:, :n] + o[:, n:2 * n] + o[:, 2 * n:]


@jax.custom_vjp
def cmat(a, c, ct):
    return _exact_right(a, c)


def _cmat_fwd(a, c, ct):
    return cmat(a, c, ct), (c, ct)


def _cmat_bwd(res, g):
    c, ct = res
    return _exact_right(g, ct), jnp.zeros_like(c), jnp.zeros_like(ct)


cmat.defvjp(_cmat_fwd, _cmat_bwd)


@jax.custom_vjp
def cmatl(c, ct, a):
    return _exact_left(c, a)


def _cmatl_fwd(c, ct, a):
    return cmatl(c, ct, a), (c, ct)


def _cmatl_bwd(res, g):
    c, ct = res
    return jnp.zeros_like(c), jnp.zeros_like(ct), _exact_left(ct, g)


cmatl.defvjp(_cmatl_fwd, _cmatl_bwd)


SUBLANES = 8


def _taps(x, shifts, down):
    n, c = x.shape
    pad = _round_up(max(shifts), SUBLANES)
    if pad == 0:
        return {0: x}
    zeros = jnp.zeros((pad, c), x.dtype)
    xp = jnp.concatenate([zeros, x] if down else [x, zeros], axis=0)
    rolled, out = {0: xp}, {}
    for s in shifts:
        a, b = divmod(s, SUBLANES)
        if b not in rolled:
            rolled[b] = pltpu.roll(xp, b if down else n + pad - b, 0)
        off = pad - SUBLANES * a if down else SUBLANES * a
        out[s] = rolled[b][off:off + n]
    return out


def _shift_down(x, k):
    return _taps(x, [k], True)[k]


def _shift_up(x, k):
    return _taps(x, [k], False)[k]


@functools.partial(jax.custom_vjp, nondiff_argnums=(1,))
def shift(x, k):
    return _shift_down(x, k)


def _shift_fwd(x, k):
    return _shift_down(x, k), None


def _shift_bwd(k, _, g):
    return (_shift_up(g, k),)


shift.defvjp(_shift_fwd, _shift_bwd)


@functools.partial(jax.custom_vjp, nondiff_argnums=(2,))
def cconv(u, w, width):
    taps = _taps(u, list(range(width)), True)
    acc = u * w[width - 1:width, :]
    for k in range(width - 1):
        acc = acc + taps[width - 1 - k] * w[k:k + 1, :]
    return acc


def _cconv_fwd(u, w, width):
    return cconv(u, w, width), (u, w)


def _cconv_bwd(width, res, g):
    u, w = res
    rows = lax.broadcasted_iota(jnp.int32, w.shape, 0)
    du = g * w[width - 1:width, :]
    dw = jnp.where(rows == width - 1, jnp.sum(g * u, axis=0, keepdims=True), 0.0)
    g_taps = _taps(g, list(range(width)), False)
    u_taps = _taps(u, list(range(width)), True)
    for k in range(width - 1):
        s = width - 1 - k
        du = du + g_taps[s] * w[k:k + 1, :]
        dw = dw + jnp.where(rows == k, jnp.sum(g * u_taps[s], axis=0, keepdims=True), 0.0)
    return du, dw


cconv.defvjp(_cconv_fwd, _cconv_bwd)


def _rms(x, g):
    return x * lax.rsqrt(jnp.mean(x * x, axis=-1, keepdims=True) + RMS_EPS) * g


def _rms_bwd(v, g, dout):
    r = lax.rsqrt(jnp.mean(v * v, axis=-1, keepdims=True) + RMS_EPS)
    n = v * r
    dn = dout * g
    dv = (dn - n * jnp.mean(dn * n, axis=-1, keepdims=True)) * r
    return dv, jnp.sum(dout * n, axis=0, keepdims=True)


def _params(sem=None):
    return pltpu.CompilerParams(dimension_semantics=sem, vmem_limit_bytes=VMEM_LIMIT)


def _f32(v):
    return v if v.dtype == F32 else v.astype(F32)


def _first(axes):
    ok = None
    for ax in axes:
        c = pl.program_id(ax) == 0
        ok = c if ok is None else jnp.logical_and(ok, c)
    return ok


def fwd_call(fn, name, grid, ins, in_specs, out_shapes, out_specs, into=None):
    n_in = len(ins)
    n_into = 0 if into is None else 1

    def body(*refs):
        outs = fn(*[_f32(r[...]) for r in refs[:n_in]])
        for r, o in zip(refs[n_in + n_into:], outs):
            r[...] = o.astype(r.dtype)

    extra = [] if into is None else [into]
    return pl.pallas_call(body, name=name, grid=grid, in_specs=list(in_specs) + [pl.BlockSpec(memory_space=pl.ANY)] * n_into,
                          out_specs=out_specs, out_shape=out_shapes, input_output_aliases={n_in: 0} if n_into else {},
                          compiler_params=_params())(*ins, *extra)


def bwd_call(fn, name, grid, ins, in_specs, cots, cot_specs, gidx, g_shapes, g_specs, g_acc):
    n_in, n_cot = len(ins), len(cots)

    def body(*refs):
        vals = [_f32(r[...]) for r in refs[:n_in]]

        def f_sel(*dv):
            full = list(vals)
            for i, v in zip(gidx, dv):
                full[i] = v
            return tuple(fn(*full))

        outs, vjp = jax.vjp(f_sel, *[vals[i] for i in gidx])
        cts = tuple(_f32(r[...]) for r in refs[n_in:n_in + n_cot])
        grads = vjp(cts)
        for r, g, acc in zip(refs[n_in + n_cot:], grads, g_acc):
            if acc is None:
                r[...] = g.astype(r.dtype)
            else:
                @pl.when(_first(acc))
                def _():
                    r[...] = jnp.zeros_like(r)

                r[...] += g.astype(r.dtype)

    return pl.pallas_call(body, name=name, grid=grid, in_specs=list(in_specs) + list(cot_specs), out_specs=g_specs,
                          out_shape=g_shapes, compiler_params=_params())(*ins, *cots)


def _tile(dim, pref):
    if dim <= pref:
        return dim
    best = None
    for t in range(LANE, pref + 1, LANE):
        if dim % t == 0:
            best = t
    assert best is not None, dim
    return best


MATMUL_VMEM_BUDGET = 40 * 1024 * 1024


def _matmul_tiles(m, n, k, a_bytes, b_bytes, out_bytes):
    tn = _tile(n, 1024)
    for tk_pref in (k, 2048, 1024, 512):
        tk = _tile(k, tk_pref)
        for tm_pref in (1024, 512, 256):
            tm = _tile(m, tm_pref)
            need = 2 * (tm * tk * a_bytes + tk * tn * b_bytes + tm * tn * out_bytes) + (0 if tk == k else tm * tn * 4)
            need += (tm * tk * 2 if a_bytes == 4 else 0) + (tk * tn * 2 if b_bytes == 4 else 0)
            if need <= MATMUL_VMEM_BUDGET:
                return tm, tn, tk
    raise ValueError((m, n, k))


def matmul(a, b, mode, name, out_dtype=F32, epilogue=None, extras=(), params=(), after=(), n_acc=0):
    if mode == 'nn':
        (m, k), (k2, n) = a.shape, b.shape
    elif mode == 'nt':
        (m, k), (n, k2) = a.shape, b.shape
    else:
        (k, m), (k2, n) = a.shape, b.shape
    assert k == k2, (name, a.shape, b.shape)
    n_extra = len(extras) + len(params)
    out_dtypes = out_dtype if isinstance(out_dtype, tuple) else (out_dtype,)
    per_out = sum(jnp.dtype(dt).itemsize for dt in out_dtypes) + sum(e.dtype.itemsize for e in extras)
    tm, tn, tk = _matmul_tiles(m, n, k, a.dtype.itemsize, b.dtype.itemsize, per_out)
    nk = k // tk
    ca = 0 if mode == 'tn' else 1
    cb = 1 if mode == 'nt' else 0
    a_spec = pl.BlockSpec((tk, tm), lambda i, j, kk: (kk, i)) if mode == 'tn' else pl.BlockSpec((tm, tk), lambda i, j, kk: (i, kk))
    b_spec = pl.BlockSpec((tn, tk), lambda i, j, kk: (j, kk)) if mode == 'nt' else pl.BlockSpec((tk, tn), lambda i, j, kk: (kk, j))

    def finish(o_refs, extra_refs, acc, first_row_tile):
        outs = (acc,) if epilogue is None else epilogue(acc, *[_f32(e[...]) for e in extra_refs])
        n_tile = len(o_refs) - n_acc
        for o_ref, o in zip(o_refs[:n_tile], outs[:n_tile]):
            o_ref[...] = o.astype(o_ref.dtype)
        for o_ref, o in zip(o_refs[n_tile:], outs[n_tile:]):
            o_ref[...] = jnp.where(first_row_tile, o, o_ref[...] + o)

    n_after = len(after)

    def body_whole_k(a_ref, b_ref, *refs):
        refs = refs[n_after:]
        finish(refs[n_extra:], refs[:n_extra], _dg(a_ref[...].astype(BF), b_ref[...].astype(BF), ca, cb), pl.program_id(0) == 0)

    def body_split_k(a_ref, b_ref, *refs):
        refs = refs[n_after:]
        extra_refs, o_refs, acc = refs[:n_extra], refs[n_extra:-1], refs[-1]
        kk = pl.program_id(2)
        first_row_tile = pl.program_id(0) == 0

        @pl.when(kk == 0)
        def _():
            acc[...] = jnp.zeros_like(acc)

        acc[...] += _dg(a_ref[...].astype(BF), b_ref[...].astype(BF), ca, cb)

        @pl.when(kk == nk - 1)
        def _():
            finish(o_refs, extra_refs, acc[...], first_row_tile)

    tile = pl.BlockSpec((tm, tn), lambda i, j, kk: (i, j))
    row = pl.BlockSpec((1, tn), lambda i, j, kk: (0, j))
    n_par = len(params)
    outs = pl.pallas_call(
        body_whole_k if nk == 1 else body_split_k, name=name, grid=(m // tm, n // tn, nk),
        in_specs=[a_spec, b_spec] + [pl.BlockSpec(memory_space=pl.ANY)] * n_after + [tile] * len(extras) + [row] * n_par,
        out_specs=[tile] * len(out_dtypes) + [row] * n_acc,
        out_shape=[jax.ShapeDtypeStruct((m, n), dt) for dt in out_dtypes] + [jax.ShapeDtypeStruct((1, n), F32)] * n_acc,
        scratch_shapes=[] if nk == 1 else [pltpu.VMEM((tm, tn), F32)],
        compiler_params=_params(("arbitrary",) * 3 if n_acc else ("parallel", "parallel", "arbitrary")))(a, b, *after, *extras, *params)
    return outs if isinstance(out_dtype, tuple) or n_acc else outs[0]


_FLIPS = [(0, 0, 1), (1, 0, 0), (0, 1, 0), (1, 1, 0), (1, 0, 1), (0, 1, 1), (1, 1, 1)]


def _me():
    return lax.axis_index("x"), lax.axis_index("y"), lax.axis_index("c")


def _flip(pos, f):
    return tuple(jnp.where(fi == 1, 1 - p, p) if fi else p for p, fi in zip(pos, f))


def _slot(pos):
    return 4 * pos[0] + 2 * pos[1] + pos[2]


def all_gather(v, name):
    def body(v_ref, out_ref, send_sems, recv_sems, local_sem):
        me = _me()
        sibling = _flip(me, (0, 0, 1))
        chips = [_flip(me, f) for f in ((1, 0, 0), (0, 1, 0), (1, 1, 0))]

        def copy(k, block, to, src=None):
            return pltpu.make_async_remote_copy(
                src_ref=out_ref.at[_slot(block)] if src is None else src, dst_ref=out_ref.at[_slot(block)],
                send_sem=send_sems.at[k], recv_sem=recv_sems.at[k], device_id=to, device_id_type=pl.DeviceIdType.MESH)

        mine = pltpu.make_async_copy(v_ref, out_ref.at[_slot(me)], local_sem)
        mine.start()
        first = [copy(0, me, sibling, src=v_ref)] + [copy(1 + j, me, chip, src=v_ref) for j, chip in enumerate(chips)]
        for cp in first:
            cp.start()
        passed = [copy(4 + j, chip, sibling) for j, chip in enumerate(chips)]
        for j, chip in enumerate(chips):
            copy(1 + j, chip, me).wait_recv()
            passed[j].start()
        copy(0, sibling, me).wait_recv()
        for j, chip in enumerate(chips):
            copy(4 + j, _flip(chip, (0, 0, 1)), me).wait_recv()
        for cp in first + passed:
            cp.wait_send()
        mine.wait()

    return pl.pallas_call(
        body, name=name, out_shape=jax.ShapeDtypeStruct((N_DEV,) + v.shape, v.dtype),
        in_specs=[pl.BlockSpec(memory_space=pl.ANY)], out_specs=pl.BlockSpec(memory_space=pl.ANY),
        scratch_shapes=[pltpu.SemaphoreType.DMA((7,)), pltpu.SemaphoreType.DMA((7,)), pltpu.SemaphoreType.DMA(())],
    )(v)


def sum_slots(v, name, tr=256):
    _, r, c = v.shape
    tr = _tile_rows(r, tr)

    def body(v_ref, o_ref):
        acc = v_ref[0].astype(F32)
        for s in range(1, N_DEV):
            acc = acc + v_ref[s].astype(F32)
        o_ref[...] = acc

    return pl.pallas_call(body, name=name, grid=(r // tr,), in_specs=[pl.BlockSpec((N_DEV, tr, c), lambda i: (0, i, 0))],
                          out_specs=pl.BlockSpec((tr, c), lambda i: (i, 0)), out_shape=jax.ShapeDtypeStruct((r, c), F32),
                          compiler_params=_params())(v)


def _tile_rows(r, pref):
    if r <= pref:
        return r
    best = None
    for t in range(8, pref + 1, 8):
        if r % t == 0:
            best = t
    return r if best is None else best


def _adamw_math(w, m, v, g):
    nm = ADAM_B1 * m + (1.0 - ADAM_B1) * g
    nv = ADAM_B2 * v + (1.0 - ADAM_B2) * jnp.square(g)
    m_hat = nm / (1.0 - ADAM_B1 ** ADAM_STEP)
    v_hat = nv / (1.0 - ADAM_B2 ** ADAM_STEP)
    return -ADAM_LR * (m_hat / (jnp.sqrt(v_hat) + ADAM_EPS) + ADAM_WD * w), nm, nv


def update_from_slots(lands, offs, w, m, v, transposed, name):
    layers, a, b = w.shape
    n_land = len(lands)
    if transposed:
        rb, tk = LANE, 512
        assert a % tk == 0 and b % rb == 0 and all(o % rb == 0 for o in offs), (name, w.shape, offs)
        grid = (layers, a // tk, b // rb)
        land_block = (N_DEV, rb, tk)
        tile = pl.BlockSpec((None, tk, rb), lambda l, i, j: (l, i, j))

        def land_spec(layer):
            base = offs[layer] // rb
            return pl.BlockSpec(land_block, lambda l, i, j: (0, base + jnp.where(l == layer, j, 0), jnp.where(l == layer, i, 0)))
    else:
        fits = [t for t in (256, 128, 64) if a % t == 0 and all(o % t == 0 for o in offs)]
        assert fits or all(o == 0 for o in offs), (name, w.shape, offs)
        tr = max(fits) if fits else a
        grid = (layers, a // tr)
        land_block = (N_DEV, _round_up(tr, MEMBER_ROW_TILE), b)
        tile = pl.BlockSpec((None, tr, b), lambda l, i: (l, i, 0))

        def land_spec(layer):
            base = offs[layer] // tr
            return pl.BlockSpec(land_block, lambda l, i: (0, base + jnp.where(l == layer, i, 0), 0))

    def body(*refs):
        land_refs, (w_ref, m_ref, v_ref, g_ref, d_ref, nm_ref, nv_ref, acc) = refs[:n_land], refs[n_land:]
        for layer, land in enumerate(land_refs):
            @pl.when(pl.program_id(0) == layer)
            def _(land=land):
                rows = acc.shape[0]
                s = land[0, :rows].astype(F32)
                for k in range(1, N_DEV):
                    s = s + land[k, :rows].astype(F32)
                acc[...] = s

        g = acc[...].T if transposed else acc[...]
        d, nm, nv = _adamw_math(w_ref[...], m_ref[...], v_ref[...], g)
        g_ref[...] = g
        d_ref[...] = d
        nm_ref[...] = nm
        nv_ref[...] = nv

    sh = jax.ShapeDtypeStruct(w.shape, F32)
    return pl.pallas_call(
        body, name=name, grid=grid, in_specs=[land_spec(layer) for layer in range(n_land)] + [tile] * 3, out_specs=[tile] * 4,
        out_shape=[sh] * 4, scratch_shapes=[pltpu.VMEM((rb, tk) if transposed else (tr, b), F32)],
        compiler_params=_params())(*lands, w, m, v)


def adamw_many(ws, ms, vs, gs, name):
    n = len(ws)

    def body(*refs):
        for i in range(n):
            d, nm, nv = _adamw_math(refs[i][...], refs[n + i][...], refs[2 * n + i][...], refs[3 * n + i][...])
            refs[4 * n + i][...] = d
            refs[5 * n + i][...] = nm
            refs[6 * n + i][...] = nv

    vmem = pl.BlockSpec(memory_space=pltpu.VMEM)
    shapes = [jax.ShapeDtypeStruct(a.shape, F32) for a in ws]
    res = pl.pallas_call(body, name=name, in_specs=[vmem] * (4 * n), out_specs=[vmem] * (3 * n), out_shape=shapes * 3,
                         compiler_params=_params())(*ws, *ms, *vs, *gs)
    return res[:n], res[n:2 * n], res[2 * n:]


def seg_in(x, g):
    return (_rms(x, g),)


def seg_res(x, m, ga, gb):
    x1 = x + _rms(m, ga)
    return x1, _rms(x1, gb)


def act_epilogue(r):
    t = jnp.maximum(r, 0.0)
    return r, t * t


def res_epilogue(m, x, ga, gb):
    x1, h = seg_res(x, m, ga, gb)
    return m, x1, h


def res_bwd_epilogue(dh, x, m, dx1, ga, gb):
    x1 = x + _rms(m, ga)
    d1, dgb = _rms_bwd(x1, gb, dh)
    dx = dx1 + d1
    dm, dga = _rms_bwd(m, ga, dx)
    return dx, dm, dga, dgb


def in_bwd_epilogue(dh, x, dx_res, g):
    d, dg = _rms_bwd(x, g, dh)
    return dx_res + d, dg


def loss_epilogue(mo, x, target, g):
    d = x + _rms(mo, g) - target
    dy = d / float(D)
    dm, dg = _rms_bwd(mo, g, dy)
    return dy, dm, dg, jnp.sum(d * d, axis=0, keepdims=True)


def act_bwd_epilogue(drr, r):
    return (drr * (2.0 * jnp.maximum(r, 0.0)),)


def seg_ln(v, g, b):
    mu = jnp.mean(v, axis=-1, keepdims=True)
    var = jnp.mean(jnp.square(v - mu), axis=-1, keepdims=True)
    vn = (v - mu) * lax.rsqrt(var + LN_EPS) * g + b
    return (jax.nn.silu(vn),)


def make_pool_fn(group):
    window = 2 ** (group + 1)

    def pool_fn(ug, pw, scale):
        s = ug
        for lvl in range(group + 1):
            s = s + shift(s, 2 ** lvl)
        cnt = jnp.minimum(lax.broadcasted_iota(jnp.int32, ug.shape, 0) + 1, window).astype(F32)
        return (bdot(s / cnt - ug, pw, 1, 0) * scale,)

    return pool_fn


def conv4_fn(xr, w, b):
    return (jax.nn.silu(cconv(xr, w, SSM_CONV) + b),)


def cd1_fn(u, dww, dwb, scw):
    val, gate, bg, cg, hh = (u[:, k * LANE:(k + 1) * LANE] for k in range(5))
    v = val * jax.nn.sigmoid(gate)
    vc = cconv(v, dww, CONF_K) + dwb
    sc = bg * cconv(cg * hh, scw, SC_K)
    return vc, sc


def attn_fn(q, kv):
    outs = []
    for h in range(XA_HEADS):
        cols = slice(h * XA_DH, (h + 1) * XA_DH)
        s = bdot(q[:, cols], kv[:, cols], 1, 1) / math.sqrt(XA_DH)
        p = jax.nn.softmax(s, axis=-1)
        outs.append(bdot(p, kv[:, D + h * XA_DH:D + (h + 1) * XA_DH], 1, 0))
    return (jnp.concatenate(outs, axis=1),)


def ssd_chunk(xbc, z, dtraw, dtb, alog, dsk, nw, h0, h1, h2, h3, e64, e64t, ecat, ecatt, tril, trilt):
    xs, bm, cm = xbc[:, :SSM_GSZ], xbc[:, SSM_GSZ:SSM_GSZ + SSM_N], xbc[:, SSM_GSZ + SSM_N:]
    hin = (h0, h1, h2, h3)
    dt = jax.nn.softplus(dtraw + dtb)
    a = -jnp.exp(alog)
    d_a = dt * a
    cs = cmatl(tril, trilt, d_a)
    cs_cat = cmat(cs, ecat, ecatt)
    cs64, cs128 = cs_cat[:, :SSM_GSZ], cs_cat[:, SSM_GSZ:]
    dt64 = cmat(dt, e64, e64t)
    row = lax.broadcasted_iota(jnp.int32, (8, LANE), 0)
    heads = jnp.where(row == 0, dsk, jnp.where(row == 1, jnp.sum(d_a, axis=0, keepdims=True), 0.0))
    heads64 = cmat(heads, e64, e64t)
    d64, tot64 = heads64[0:1, :], heads64[1:2, :]
    xdt = xs * dt64
    cb = bdot(cm, bm, 1, 1)
    li = lax.broadcasted_iota(jnp.int32, (CHUNK, CHUNK), 0)
    si = lax.broadcasted_iota(jnp.int32, (CHUNK, CHUNK), 1)
    causal = li >= si
    lane = lax.broadcasted_iota(jnp.int32, (CHUNK, LANE), 1)
    xw = xdt * jnp.exp(tot64 - cs64)
    ecs = jnp.exp(cs64)
    etot = jnp.exp(tot64)
    ycols, hout = [], []
    for j in range(4):
        sl = slice(j * LANE, (j + 1) * LANE)
        xj = xdt[:, sl]
        ys = []
        for hh in range(2):
            r = 2 * j + hh
            col = cs128[:, r * LANE:(r + 1) * LANE]
            decay = jnp.exp(jnp.where(causal, col - col.T, -1e30))
            ys.append(bdot(cb * decay, xj, 1, 0))
        y_diag = jnp.where(lane < SSM_P, ys[0], ys[1])
        y_off = bdot(cm, hin[j], 1, 0) * ecs[:, sl]
        ycols.append(y_diag + y_off)
        hout.append(etot[:, sl] * hin[j] + bdot(bm, xw[:, sl], 0, 0))
    y = jnp.concatenate(ycols, axis=1) + d64 * xs
    y = y * jax.nn.silu(z)
    yn = y * lax.rsqrt(jnp.mean(y * y, axis=-1, keepdims=True) + RMS_EPS) * nw
    return (yn,) + tuple(hout)


def _xbc_group(a, axis):
    parts = []
    for g in range(SSM_GROUPS):
        for start, width in ((g * SSM_GSZ, SSM_GSZ), (SSM_INNER + g * SSM_N, SSM_N), (SSM_INNER + (SSM_GROUPS + g) * SSM_N, SSM_N)):
            parts.append(lax.slice_in_dim(a, start, start + width, axis=axis))
    return jnp.concatenate(parts, axis=axis)


def _xbc_ungroup(a, axis):
    xs, bs, cs = [], [], []
    for g in range(SSM_GROUPS):
        base = g * SSM_XBC_G
        xs.append(lax.slice_in_dim(a, base, base + SSM_GSZ, axis=axis))
        bs.append(lax.slice_in_dim(a, base + SSM_GSZ, base + SSM_GSZ + SSM_N, axis=axis))
        cs.append(lax.slice_in_dim(a, base + SSM_GSZ + SSM_N, base + SSM_XBC_G, axis=axis))
    return jnp.concatenate(xs + bs + cs, axis=axis)


def _ssd_consts():
    h = np.arange(LANE)[:, None]
    e64 = np.stack([(h == g * 8 + np.arange(SSM_GSZ)[None, :] // SSM_P) for g in range(SSM_GROUPS)]).astype(np.float32)
    e128 = np.stack([(h == g * 8 + np.arange(8 * LANE)[None, :] // LANE) for g in range(SSM_GROUPS)]).astype(np.float32)
    ecat = np.concatenate([e64, e128], axis=2)
    tril = np.tril(np.ones((CHUNK, CHUNK), np.float32))
    return tuple(jnp.asarray(c, dtype=BF) for c in (e64, e64.transpose(0, 2, 1), ecat, ecat.transpose(0, 2, 1), tril, tril.T))


def _ssd_specs(nc, rev):
    def ci(c):
        return nc - 1 - c if rev else c

    def row(width, col):
        return pl.BlockSpec((CHUNK, width), lambda b, c: (b * nc + ci(c), col))

    def whole(shape):
        return pl.BlockSpec(shape, lambda b, c: (0,) * len(shape))

    data = [row(SSM_CONV_DIM, 0),
            row(SSM_GSZ, 1), row(SSM_GSZ, 2), row(LANE, 24)]
    par = [whole((1, LANE))] * 3 + [whole((1, SSM_INNER))]
    cst = [whole((SSM_GROUPS, LANE, SSM_GSZ)), whole((SSM_GROUPS, SSM_GSZ, LANE)), whole((SSM_GROUPS, LANE, 12 * LANE)),
           whole((SSM_GROUPS, 12 * LANE, LANE)), whole((CHUNK, CHUNK)), whole((CHUNK, CHUNK))]
    hsave = pl.BlockSpec((None, None, SSM_GROUPS, 4, SSM_N, LANE), lambda b, c: (b, ci(c), 0, 0, 0, 0))
    return data, par, cst, hsave, row, whole


def _ssd_group_args(g, xbc, z, dtr, dtb, alog, dsk, nw):
    return (xbc[:, g * SSM_XBC_G:(g + 1) * SSM_XBC_G], z[g], dtr, dtb, alog, dsk, nw[:, g * SSM_GSZ:(g + 1) * SSM_GSZ])


def ssd_fwd(xbc_act, u, dtb, alog, dsk, nw, consts, bsz, seq):
    nc = seq // CHUNK
    data, par, cst, hsave, row, _ = _ssd_specs(nc, False)

    def body(xbc, z0, z1, dtr, dtb_r, alog_r, dsk_r, nw_r, e64, e64t, ecat, ecatt, tril, trilt, yn_ref, hs_ref, h):
        @pl.when(pl.program_id(1) == 0)
        def _():
            h[...] = jnp.zeros_like(h)

        hs_ref[...] = h[...]
        ys = []
        for g in range(SSM_GROUPS):
            args = _ssd_group_args(g, xbc[...], (z0[...], z1[...]), dtr[...], dtb_r[...], alog_r[...], dsk_r[...], nw_r[...])
            outs = ssd_chunk(*args, h[g, 0], h[g, 1], h[g, 2], h[g, 3], e64[g], e64t[g], ecat[g], ecatt[g], tril[...], trilt[...])
            ys.append(outs[0])
            for j in range(4):
                h[g, j] = outs[1 + j]
        yn_ref[...] = jnp.concatenate(ys, axis=1).astype(yn_ref.dtype)

    t = bsz * seq
    return pl.pallas_call(
        body, name="ssd_fwd", grid=(bsz, nc), in_specs=data + par + cst, out_specs=[row(SSM_INNER, 0), hsave],
        out_shape=[jax.ShapeDtypeStruct((t, SSM_INNER), BF), jax.ShapeDtypeStruct((bsz, nc, SSM_GROUPS, 4, SSM_N, LANE), F32)],
        scratch_shapes=[pltpu.VMEM((SSM_GROUPS, 4, SSM_N, LANE), F32)], compiler_params=_params(),
    )(xbc_act, u, u, u, dtb, alog, dsk, nw, *consts)


def ssd_bwd(xbc_act, u, dtb, alog, dsk, nw, consts, hs, dmix, bsz, seq):
    nc = seq // CHUNK
    data, par, cst, hsave, row, whole = _ssd_specs(nc, True)
    t = bsz * seq
    pcol = POOL_W // SSM_GSZ

    def body(xbc, z0, z1, dtr, dtb_r, alog_r, dsk_r, nw_r, e64, e64t, ecat, ecatt, tril, trilt, hs_ref, dy0, dy1,
             dxbc, dz, ddt, ddtb, dalog, ddsk, dnw, dh):
        @pl.when(pl.program_id(1) == 0)
        def _():
            dh[...] = jnp.zeros_like(dh)

        per_group = []
        for g, dyn in enumerate((dy0, dy1)):
            cst_vals = (e64[g], e64t[g], ecat[g], ecatt[g], tril[...], trilt[...])
            prim = _ssd_group_args(g, xbc[...], (z0[...], z1[...]), dtr[...], dtb_r[...], alog_r[...], dsk_r[...], nw_r[...])
            prim = prim + (hs_ref[g, 0], hs_ref[g, 1], hs_ref[g, 2], hs_ref[g, 3])
            _, vjp = jax.vjp(lambda *args, c=cst_vals: ssd_chunk(*args, *c), *prim)
            gr = vjp((dyn[...].astype(F32), dh[g, 0], dh[g, 1], dh[g, 2], dh[g, 3]))
            for j in range(4):
                dh[g, j] = gr[7 + j]
            per_group.append(gr)
        g0, g1 = per_group
        dxbc[...] = jnp.concatenate([g0[0], g1[0]], axis=1)
        dz[...] = jnp.concatenate([g0[1], g1[1]], axis=1).astype(dz.dtype)
        ddt[...] = g0[2] + g1[2]

        @pl.when(_first((0, 1)))
        def _():
            for r in (ddtb, dalog, ddsk, dnw):
                r[...] = jnp.zeros_like(r)

        ddtb[...] += g0[3] + g1[3]
        dalog[...] += g0[4] + g1[4]
        ddsk[...] += g0[5] + g1[5]
        dnw[...] += jnp.concatenate([g0[6], g1[6]], axis=1)

    out_specs = [row(SSM_CONV_DIM, 0), row(SSM_INNER, 0), row(LANE, 0), whole((1, LANE)), whole((1, LANE)), whole((1, LANE)),
                 whole((1, SSM_INNER))]
    lane = jax.ShapeDtypeStruct((1, LANE), F32)
    out_shape = [jax.ShapeDtypeStruct((t, SSM_CONV_DIM), F32), jax.ShapeDtypeStruct((t, SSM_INNER), BF),
                 jax.ShapeDtypeStruct((t, LANE), F32), lane, lane, lane, jax.ShapeDtypeStruct((1, SSM_INNER), F32)]
    return pl.pallas_call(
        body, name="ssd_bwd", grid=(bsz, nc), in_specs=data + par + cst + [hsave, row(SSM_GSZ, pcol), row(SSM_GSZ, pcol + 1)],
        out_specs=out_specs, out_shape=out_shape, scratch_shapes=[pltpu.VMEM((SSM_GROUPS, 4, SSM_N, LANE), F32)],
        compiler_params=_params(),
    )(xbc_act, u, u, u, dtb, alog, dsk, nw, *consts, hs, dmix, dmix)


TB = 512


def _rows(d, col=0):
    return pl.BlockSpec((TB, d), lambda i: (i, col))


def _par(d):
    return pl.BlockSpec((1, d), lambda i: (0, 0))


def _sd(shape, dtype=F32):
    return jax.ShapeDtypeStruct(shape, dtype)


def _round_up(n, m):
    return -(-n // m) * m


def _pad_rows(a, rows):
    return jnp.pad(a, ((0, rows - a.shape[0]), (0, 0)))


def _pack128(arrs):
    flat = jnp.concatenate([a.reshape(-1) for a in arrs])
    n = flat.shape[0]
    rows = -(-n // (8 * LANE)) * 8
    return jnp.pad(flat, (0, rows * LANE - n)).reshape(rows, LANE)


def _unpack128(packed, shapes):
    flat = packed.reshape(-1)
    out, off = [], 0
    for s in shapes:
        n = int(np.prod(s))
        out.append(flat[off:off + n].reshape(s))
        off += n
    return out


def kernel(x, mem, norm_gains, xa_wq, xa_wkv, xa_wo, mlp_w1, mlp_w2, ab_w_in, pool_w, pool_scale, ssm_conv_w, ssm_conv_b, ssm_dt_bias, ssm_a_log, ssm_d, ssm_norm, ab_w_out, cd_w_in, conf_dw_w, conf_dw_b, conf_ln_g, conf_ln_b, sc_conv_w, cd_w_out, loss_target, m_norm_gains, m_xa_wq, m_xa_wkv, m_xa_wo, m_mlp_w1, m_mlp_w2, m_ab_w_in, m_pool_w, m_pool_scale, m_ssm_conv_w, m_ssm_conv_b, m_ssm_dt_bias, m_ssm_a_log, m_ssm_d, m_ssm_norm, m_ab_w_out, m_cd_w_in, m_conf_dw_w, m_conf_dw_b, m_conf_ln_g, m_conf_ln_b, m_sc_conv_w, m_cd_w_out, v_norm_gains, v_xa_wq, v_xa_wkv, v_xa_wo, v_mlp_w1, v_mlp_w2, v_ab_w_in, v_pool_w, v_pool_scale, v_ssm_conv_w, v_ssm_conv_b, v_ssm_dt_bias, v_ssm_a_log, v_ssm_d, v_ssm_norm, v_ab_w_out, v_cd_w_in, v_conf_dw_w, v_conf_dw_b, v_conf_ln_g, v_conf_ln_b, v_sc_conv_w, v_cd_w_out):
    args = locals()
    w = {n: args[n] for n in WEIGHTS}
    mom_m = {n: args["m_" + n] for n in WEIGHTS}
    mom_v = {n: args["v_" + n] for n in WEIGHTS}
    ex = Exchange(w)
    loss_local, grad_x, small_grads = local_step(x, mem, loss_target, ex)
    outs = {}

    started = ex.put_small(small_grads, loss_local)
    landed = {key: ex.landed(key, started) for key in ('l1', 'cd', 'l0')}
    late = []
    for n, keys in (('mlp_w1', ('l0', 'l1')), ('mlp_w2', ('l0', 'l1')), ('xa_wkv', ('l0', 'l1')), ('xa_wq', ('l0', 'l1')),
                    ('xa_wo', ('l0', 'l1')), ('cd_w_in', ('cd',)), ('cd_w_out', ('cd',))):
        lands = [landed[key][0] for key in keys]
        offs = [landed[key][1][(n, layer)] for layer, key in enumerate(keys)]
        outs[n] = update_from_slots(lands, offs, w[n], mom_m[n], mom_v[n], SHARD_AXIS[n] == 2, "update_" + n)
        late.append(outs[n][1])
    g_own, loss = ex.reduced_small(late)
    land_ab, offs_ab = ex.landed('ab', late)
    outs['ab_w_out'] = update_from_slots([land_ab], [offs_ab[('ab_w_out', 0)]], w['ab_w_out'], mom_m['ab_w_out'],
                                         mom_v['ab_w_out'], False, "update_ab_w_out")
    res = update_from_slots([land_ab], [offs_ab[('ab_w_in', 0)]], jnp.swapaxes(w['ab_w_in'], 1, 2), jnp.swapaxes(mom_m['ab_w_in'], 1, 2),
                            jnp.swapaxes(mom_v['ab_w_in'], 1, 2), False, "update_ab_w_in")
    outs['ab_w_in'] = tuple(jnp.swapaxes(r, 1, 2) for r in res)
    small = SMALL_SHARDED + REPLICATED
    upd = adamw_many([w[n] for n in small], [mom_m[n] for n in small], [mom_v[n] for n in small], [g_own[n] for n in small],
                     "adamw_small")
    for i, n in enumerate(small):
        outs[n] = (g_own[n], upd[0][i], upd[1][i], upd[2][i])
    return (loss, grad_x.reshape(x.shape), *[outs[n][0] for n in WEIGHTS], *[outs[n][1] for n in WEIGHTS],
            *[outs[n][2] for n in WEIGHTS], *[outs[n][3] for n in WEIGHTS])


G_AB = (('ab_w_in', 0), ('ab_w_out', 0))
G_L0 = (('xa_wq', 0), ('xa_wkv', 0), ('xa_wo', 0), ('mlp_w1', 0), ('mlp_w2', 0))
G_L1 = (('xa_wq', 1), ('xa_wkv', 1), ('xa_wo', 1), ('mlp_w1', 1), ('mlp_w2', 1))
G_CD = (('cd_w_in', 0), ('cd_w_out', 0))
GATHER_GROUPS = {'ab': G_AB[:1], 'l0a': G_AB[1:] + G_L0[:3], 'l0b': G_L0[3:], 'cd': G_CD, 'l1a': G_L1[:3], 'l1b': G_L1[3:]}
SHARD_AXIS = dict(BIG)
MEMBER_ROW_TILE = 64
FLAT_ROW_TILE = 128


def _members(group, w):
    out = []
    for n, layer in group:
        shp = w[n].shape[1:]
        if SHARD_AXIS[n] == 2:
            shp = (shp[1], shp[0])
        assert shp[1] == D, (n, shp)
        out.append((n, layer, shp, shp[0], _round_up(shp[0], MEMBER_ROW_TILE)))
    return out


def _group_rows(group, w):
    return _round_up(sum(m[4] for m in _members(group, w)), FLAT_ROW_TILE)


def _flat_shards(group, w):
    parts = []
    for n, layer, _, _, padded in _members(group, w):
        shard = w[n][layer].astype(BF)
        parts.append(_pad_rows(shard.T if SHARD_AXIS[n] == 2 else shard, padded))
    return _pad_rows(jnp.concatenate(parts, axis=0), _group_rows(group, w))


def _full_from_slots(land, group, w):
    out, off = {}, 0
    for n, layer, shp, rows, padded in _members(group, w):
        out[(n, layer)] = land[:, off:off + rows].reshape(N_DEV * rows, D)
        off += padded
    return out


def _slots_from_full(grads, group, w):
    parts = []
    for n, layer, shp, rows, padded in _members(group, w):
        blk = grads[(n, layer)].astype(BF).reshape(N_DEV, rows, D)
        parts.append(jnp.pad(blk, ((0, 0), (0, padded - rows), (0, 0))))
    send = jnp.concatenate(parts, axis=1)
    return jnp.pad(send, ((0, 0), (0, _group_rows(group, w) - send.shape[1]), (0, 0)))


_HBM = pl.BlockSpec(memory_space=pltpu.HBM)
_SEM = pl.BlockSpec(memory_space=pltpu.SEMAPHORE)
_ANY = pl.BlockSpec(memory_space=pl.ANY)


def _peer_copy(k, src, dst, send_sems, recv_sems, peer):
    return pltpu.make_async_remote_copy(src_ref=src, dst_ref=dst, send_sem=send_sems.at[k], recv_sem=recv_sems.at[k],
                                        device_id=peer, device_id_type=pl.DeviceIdType.MESH)


def exchange_start(src, name, scatter, after=()):
    shape = src.shape[-2:]
    after = list(after)

    def body(src_ref, land_ref, *rest):
        send_sems, recv_sems, token = rest[len(after)], rest[len(after) + 1], rest[-1]
        me = _me()
        for k, f in enumerate(_FLIPS):
            peer = _flip(me, f)
            piece = src_ref.at[_slot(peer)] if scatter else src_ref
            _peer_copy(k, piece, land_ref.at[_slot(me)], send_sems, recv_sems, peer).start()
        token[...] = jnp.zeros_like(token)

    land = pltpu.with_memory_space_constraint(lax.empty((N_DEV,) + shape, src.dtype), pltpu.HBM)
    return pl.pallas_call(
        body, name=name,
        out_shape=(pltpu.SemaphoreType.DMA((7,)), pltpu.SemaphoreType.DMA((7,)), pltpu.HBM(src.shape, src.dtype),
                   pltpu.HBM((N_DEV,) + shape, src.dtype), jax.ShapeDtypeStruct((8, LANE), F32)),
        in_specs=(_HBM, _HBM) + (_ANY,) * len(after), out_specs=(_SEM, _SEM, _HBM, _HBM, pl.BlockSpec(memory_space=pltpu.VMEM)),
        input_output_aliases={0: 2, 1: 3},
        compiler_params=pltpu.CompilerParams(has_side_effects=pltpu.SideEffectType.DATAFLOW_SIDE_EFFECTING),
    )(pltpu.with_memory_space_constraint(src, pltpu.HBM), land, *after)


def exchange_wait(handles, after, name, scatter):
    send_sems, recv_sems, src_thru, land_thru, _ = handles
    after = list(after) if isinstance(after, (list, tuple)) else [after]

    def body(src_ref, land_ref, send_sems, recv_sems, *rest):
        token = rest[-1]
        me = _me()
        for k, f in enumerate(_FLIPS):
            peer = _flip(me, f)
            piece = src_ref.at[_slot(peer)] if scatter else src_ref
            cp = _peer_copy(k, piece, land_ref.at[_slot(peer)], send_sems, recv_sems, peer)
            cp.wait_send()
            cp.wait_recv()
        token[...] = jnp.zeros_like(token)

    return pl.pallas_call(
        body, name=name, out_shape=(pltpu.HBM(src_thru.shape, src_thru.dtype), pltpu.HBM(land_thru.shape, land_thru.dtype),
                                    jax.ShapeDtypeStruct((8, LANE), F32)),
        in_specs=(_HBM, _HBM, _SEM, _SEM) + (_ANY,) * len(after), out_specs=(_HBM, _HBM, pl.BlockSpec(memory_space=pltpu.VMEM)),
        input_output_aliases={0: 0, 1: 1},
        compiler_params=pltpu.CompilerParams(has_side_effects=pltpu.SideEffectType.DATAFLOW_SIDE_EFFECTING),
    )(src_thru, land_thru, send_sems, recv_sems, *after)


class Exchange:
    def __init__(self, w):
        self.w = w
        self.me = _slot(_me())
        shapes = [w[n].shape for n in SMALL_SHARDED]
        gs = all_gather(_pack128([w[n] for n in SMALL_SHARDED]), "gather_small")
        per_dev = [_unpack128(gs[d], shapes) for d in range(N_DEV)]
        self.small = {n: jnp.concatenate([per_dev[d][i] for d in range(N_DEV)], axis=-1) for i, n in enumerate(SMALL_SHARDED)}
        self.small.update({n: w[n] for n in REPLICATED})
        first = all_gather(_flat_shards(GATHER_GROUPS['ab'], w), "gather_ab")
        self.first = _full_from_slots(first, GATHER_GROUPS['ab'], w)
        self.gathers, self.done, self.tokens, self.reductions = {}, {}, [], {}
        self.start_gather('l0a', after=[first])
        self.start_gather('l0b', after=[self.gathers['l0a'][4]])

    def take_tokens(self):
        toks, self.tokens = self.tokens, []
        return toks

    def start_gather(self, key, after=()):
        group = GATHER_GROUPS[key]
        self.gathers[key] = exchange_start(_flat_shards(group, self.w), f"gather_{key}_start", False, after=after)
        self.tokens.append(self.gathers[key][4])

    def weights(self, key, after):
        if key == 'ab':
            return self.first
        handles = self.gathers[key]
        _, land, self.done[key] = exchange_wait(handles, after, f"gather_{key}_wait", False)
        land = lax.dynamic_update_slice(land, handles[2][None], (self.me, 0, 0))
        return _full_from_slots(land, GATHER_GROUPS[key], self.w)

    def put_grads(self, key, group, grads):
        send = _slots_from_full(grads, group, self.w)
        handles = exchange_start(send, f"reduce_{key}_start", True)
        self.reductions[key] = (group, handles)
        self.tokens.append(handles[4])

    def landed(self, key, after):
        group, handles = self.reductions[key]
        send, land, _ = exchange_wait(handles, after, f"reduce_{key}_wait", True)
        mine = lax.dynamic_slice_in_dim(send, self.me, 1, axis=0)
        land = lax.dynamic_update_slice(land, mine, (self.me, 0, 0))
        offs, off = {}, 0
        for n, layer, _, _, padded in _members(group, self.w):
            offs[(n, layer)] = off
            off += padded
        return land, offs

    def put_small(self, small_grads, loss_local):
        small = SMALL_SHARDED + REPLICATED
        self.small_shapes = [small_grads[n].shape for n in small] + [(1,)]
        packed = _pack128([small_grads[n] for n in small] + [loss_local.reshape(1)])
        self.small_handles = exchange_start(packed, "gather_small_grads_start", False)
        return self.small_handles[4]

    def reduced_small(self, after):
        small = SMALL_SHARDED + REPLICATED
        src, land, _ = exchange_wait(self.small_handles, after, "gather_small_grads_wait", False)
        gs = lax.dynamic_update_slice(land, src[None], (self.me, 0, 0))
        tot = _unpack128(sum_slots(gs, "sum_small", 1024), self.small_shapes)
        out = {}
        for n, g in zip(small, tot):
            if n in SMALL_SHARDED:
                width = self.w[n].shape[-1]
                g = lax.dynamic_slice_in_dim(g, self.me * width, width, axis=g.ndim - 1)
            out[n] = g
        return out, tot[-1].reshape(())


def local_step(x, mem, target, ex):
    bsz, seq, _ = x.shape
    t = bsz * seq
    nb = t // TB
    nc = seq // CHUNK
    x0 = x.reshape(t, D)
    mem2 = mem.reshape(bsz * N_MEM, D)
    tgt = target.reshape(t, D)
    p = ex.small
    gains = p['norm_gains']
    big = {}

    def gain(layer, i):
        g = gains[layer, i].reshape(1, D)
        for tok in ex.take_tokens():
            g = g + tok[0, 0]
        return g

    consts = _ssd_consts()
    grads = {}
    saved = [dict(), dict()]

    def matmul_res(a, b, name, xin, ga, gb):
        return matmul(a, b, 'nn', name, (F32, F32, BF), epilogue=res_epilogue, extras=[xin], params=[ga, gb])

    def attn_specs():
        nq = seq // TB
        q = pl.BlockSpec((TB, D), lambda b, i: (b * nq + i, 0))
        kv = pl.BlockSpec((N_MEM, 2 * D), lambda b, i: (b, 0))
        return (bsz, nq), q, kv

    def attention_fwd(layer, xin, hin, sv, ga, gb):
        q = matmul(hin, big[('xa_wq', layer)], 'nn', f"q_{layer}", BF)
        kv = matmul(mem2, big[('xa_wkv', layer)], 'nt', f"kv_{layer}", BF)
        grid, qs, kvs = attn_specs()
        o, = fwd_call(attn_fn, f"attn_{layer}", grid, [q, kv], [qs, kvs], [_sd((t, D), BF)], [qs])
        ao, x_next, h_next = matmul_res(o, big[('xa_wo', layer)], f"ao_{layer}", xin, ga, gb)
        sv.update(q=q, kv=kv, o=o, ao=ao)
        return ao, x_next, h_next

    def mlp_fwd(layer, hin, sv, res):
        r, rr = matmul(hin, big[('mlp_w1', layer)], 'nt', f"mlp1_{layer}", (BF, BF), epilogue=act_epilogue)
        out = matmul_res(rr, big[('mlp_w2', layer)], f"mlp2_{layer}", *res)
        sv.update(r=r, rr=rr, mo=out[0])
        return out

    sv = saved[0]
    h0, = fwd_call(seg_in, "norm_in", (nb,), [x0, gain(0, 0)], [_rows(D), _par(D)], [_sd((t, D), BF)], [_rows(D)])
    big.update(ex.weights('ab', h0))
    xbc0 = POOL_W + SSM_INNER
    w_ab_in = big[('ab_w_in', 0)]
    w_ab_in = _pad_rows(jnp.concatenate([w_ab_in[:xbc0], _xbc_group(w_ab_in[xbc0:xbc0 + SSM_CONV_DIM], 0),
                                         w_ab_in[xbc0 + SSM_CONV_DIM:]], axis=0), AB_IN_PAD)
    conv_w, conv_b = _xbc_group(p['ssm_conv_w'][0], 1), _xbc_group(p['ssm_conv_b'], 1)
    u0 = matmul(h0, w_ab_in, 'nt', "ab_in")
    pool_outs = []
    for g in range(POOL_GROUPS):
        seqspec = pl.BlockSpec((seq, PG), lambda b, g=g: (b, g))
        po, = fwd_call(make_pool_fn(g), f"pool_{g}", (bsz,), [u0, p['pool_w'][0, g], p['pool_scale']],
                       [seqspec, pl.BlockSpec((PG, PG), lambda b: (0, 0)), pl.BlockSpec((1, PG), lambda b, g=g: (0, g))],
                       [_sd((t, PG), BF)], [pl.BlockSpec((seq, PG), lambda b: (b, 0))])
        pool_outs.append(po)
    cw = 256
    ncb = SSM_CONV_DIM // cw
    cbase = (POOL_W + SSM_INNER) // cw
    conv_in_specs = [pl.BlockSpec((seq, cw), lambda j, b: (b, cbase + j)), pl.BlockSpec((SSM_CONV, cw), lambda j, b: (0, j)),
                     pl.BlockSpec((1, cw), lambda j, b: (0, j))]
    conv_out_spec = pl.BlockSpec((seq, cw), lambda j, b: (b, j))
    xbc_act, = fwd_call(conv4_fn, "ssm_conv", (ncb, bsz), [u0, conv_w, conv_b], conv_in_specs,
                        [_sd((t, SSM_CONV_DIM))], [conv_out_spec])
    dtb = jnp.pad(p['ssm_dt_bias'], ((0, 0), (0, LANE - SSM_HEADS)))
    alog = jnp.pad(p['ssm_a_log'], ((0, 0), (0, LANE - SSM_HEADS)))
    dsk = jnp.pad(p['ssm_d'], ((0, 0), (0, LANE - SSM_HEADS)))
    yn, hs = ssd_fwd(xbc_act, u0, dtb, alog, dsk, p['ssm_norm'], consts, bsz, seq)
    mix0 = jnp.concatenate(pool_outs + [yn], axis=1)
    big.update(ex.weights('l0a', yn))
    ex.start_gather('cd', after=[ex.done['l0a']])
    ex.start_gather('l1a', after=[ex.gathers['cd'][4]])
    ex.start_gather('l1b', after=[ex.gathers['l1a'][4]])
    m0, x1, h2 = matmul_res(mix0, big[('ab_w_out', 0)], "ab_out", x0, gain(0, 1), gain(0, 2))
    ao0, x2, h3 = attention_fwd(0, x1, h2, sv, gain(0, 3), gain(0, 4))
    big.update(ex.weights('l0b', h3))
    mo0, x3, h4 = mlp_fwd(0, h3, sv, (x2, gain(0, 5), gain(1, 0)))
    big.update(ex.weights('cd', mo0))

    sv1 = saved[1]
    nd = D // LANE
    w_cd_in = big[('cd_w_in', 0)].reshape(5, nd, LANE, D).transpose(1, 0, 2, 3).reshape(CD_IN, D)
    u1 = matmul(h4, w_cd_in, 'nt', "cd_in")
    cd_par = [pl.BlockSpec((CONF_K, LANE), lambda j, b: (0, j)), pl.BlockSpec((1, LANE), lambda j, b: (0, j)),
              pl.BlockSpec((SC_K, LANE), lambda j, b: (0, j))]
    cd_ins = [u1, p['conf_dw_w'][0], p['conf_dw_b'], p['sc_conv_w'][0]]
    cd_u_spec = pl.BlockSpec((seq, 5 * LANE), lambda j, b: (b, j))
    cd_in_specs = [cd_u_spec] + cd_par
    cd_out_spec = pl.BlockSpec((seq, LANE), lambda j, b: (b, j))
    vconv, mix1 = fwd_call(cd1_fn, "cd_conv", (nd, bsz), cd_ins, cd_in_specs, [_sd((t, D)), _sd((t, CD_OUT), BF)],
                           [cd_out_spec, pl.BlockSpec((seq, LANE), lambda j, b: (b, nd + j))])
    mix1, = fwd_call(seg_ln, "conf_ln", (nb,), [vconv, p['conf_ln_g'], p['conf_ln_b']], [_rows(D), _par(D), _par(D)],
                     [_sd((t, CD_OUT), BF)], [_rows(D)], into=mix1)
    m1, x4, h5 = matmul_res(mix1, big[('cd_w_out', 0)], "cd_out", x3, gain(1, 1), gain(1, 2))
    big.update(ex.weights('l1a', h5))
    ao1, x5, h6 = attention_fwd(1, x4, h5, sv1, gain(1, 3), gain(1, 4))
    big.update(ex.weights('l1b', h6))
    r1, rr1 = matmul(h6, big[('mlp_w1', 1)], 'nt', "mlp1_1", (BF, BF), epilogue=act_epilogue)
    sv1.update(r=r1, rr=rr1)
    dx5, dmo1, dg15, lanes = matmul(rr1, big[('mlp_w2', 1)], 'nn', "mlp2_1", (F32, BF), epilogue=loss_epilogue, extras=[x5, tgt],
                                    params=[gain(1, 5)], n_acc=2)
    loss = 0.5 * jnp.sum(lanes) / float(D)

    gain_grads = {(1, 5): dg15}

    def matmul_res_bwd(a, b, mode, name, xin, m, ga, gb, dx1):
        return list(matmul(a, b, mode, name, (F32, BF), epilogue=res_bwd_epilogue, extras=[xin, m, dx1], params=[ga, gb], n_acc=2))

    def mlp_bwd(layer, hin, dmo, sv, res):
        grads_w2 = matmul(sv['rr'], dmo, 'tn', f"d_mlp_w2_{layer}", BF)
        dr, = matmul(dmo, big[('mlp_w2', layer)], 'nt', f"d_r_{layer}", (BF,), epilogue=act_bwd_epilogue, extras=[sv['r']])
        grads_w1 = matmul(dr, hin, 'tn', f"d_mlp_w1_{layer}", BF)
        return matmul_res_bwd(dr, big[('mlp_w1', layer)], 'nn', f"d_h_mlp_{layer}", *res) + [grads_w1, grads_w2]

    def attention_bwd(layer, hin, dao, sv, res):
        g_wo = matmul(sv['o'], dao, 'tn', f"d_xa_wo_{layer}", BF)
        do = matmul(dao, big[('xa_wo', layer)], 'nt', f"d_o_{layer}", BF)
        grid, qs, kvs = attn_specs()
        dq, dkv = bwd_call(attn_fn, f"d_attn_{layer}", grid, [sv['q'], sv['kv']], [qs, kvs], [do], [qs], [0, 1],
                           [_sd((t, D), BF), _sd((bsz * N_MEM, 2 * D))], [qs, kvs], [None, (1,)])
        g_wkv = matmul(dkv, mem2, 'tn', f"d_xa_wkv_{layer}", BF)
        g_wq = matmul(hin, dq, 'tn', f"d_xa_wq_{layer}", BF)
        return matmul_res_bwd(dq, big[('xa_wq', layer)], 'nt', f"d_h_attn_{layer}", *res) + [g_wq, g_wkv, g_wo]

    per_layer = {k: [None, None] for k in ('xa_wq', 'xa_wkv', 'xa_wo', 'mlp_w1', 'mlp_w2')}

    (dx4, dao1, gain_grads[(1, 3)], gain_grads[(1, 4)], per_layer['mlp_w1'][1],
     per_layer['mlp_w2'][1]) = mlp_bwd(1, h6, dmo1, sv1, (x4, ao1, gain(1, 3), gain(1, 4), dx5))
    (dx3, dm1, gain_grads[(1, 1)], gain_grads[(1, 2)], per_layer['xa_wq'][1], per_layer['xa_wkv'][1],
     per_layer['xa_wo'][1]) = attention_bwd(1, h5, dao1, sv1, (x3, m1, gain(1, 1), gain(1, 2), dx4))
    ex.put_grads('l1', G_L1, {(k, 1): v[1] for k, v in per_layer.items()})
    g_cd_out = matmul(mix1, dm1, 'tn', "d_cd_w_out", BF)
    dmix1 = matmul(dm1, big[('cd_w_out', 0)], 'nt', "d_mix1", after=ex.take_tokens())
    dvconv, dlg, dlb = bwd_call(seg_ln, "d_conf_ln", (nb,), [vconv, p['conf_ln_g'], p['conf_ln_b']],
                                [_rows(D), _par(D), _par(D)], [dmix1], [_rows(D, 0)], [0, 1, 2],
                                [_sd((t, D)), _sd((1, D)), _sd((1, D))], [_rows(D), _par(D), _par(D)], [None, (0,), (0,)])
    grads['conf_ln_g'], grads['conf_ln_b'] = dlg, dlb
    cd_g = bwd_call(cd1_fn, "d_cd_conv", (nd, bsz), cd_ins, cd_in_specs, [dvconv, dmix1],
                    [cd_out_spec, pl.BlockSpec((seq, LANE), lambda j, b: (b, nd + j))], list(range(4)),
                    [_sd((t, CD_IN), BF), _sd((CONF_K, D)), _sd((1, D)), _sd((SC_K, D))], [cd_u_spec] + cd_par,
                    [None, (1,), (1,), (1,)])
    du1 = cd_g[0]
    grads['conf_dw_w'], grads['conf_dw_b'], grads['sc_conv_w'] = cd_g[1][None], cd_g[2], cd_g[3][None]
    g_cd_in = matmul(du1, h4, 'tn', "d_cd_w_in", BF).reshape(nd, 5, LANE, D).transpose(1, 0, 2, 3).reshape(CD_IN, D)
    ex.put_grads('cd', G_CD, {('cd_w_in', 0): g_cd_in, ('cd_w_out', 0): g_cd_out})
    dx2, dmo0, gain_grads[(0, 5)], gain_grads[(1, 0)] = matmul_res_bwd(du1, w_cd_in, 'nn', "d_h_cd", x2, mo0, gain(0, 5),
                                                                       gain(1, 0), dx3)
    (dx1, dao0, gain_grads[(0, 3)], gain_grads[(0, 4)], per_layer['mlp_w1'][0],
     per_layer['mlp_w2'][0]) = mlp_bwd(0, h3, dmo0, sv, (x1, ao0, gain(0, 3), gain(0, 4), dx2))
    (dx0r, dm0, gain_grads[(0, 1)], gain_grads[(0, 2)], per_layer['xa_wq'][0], per_layer['xa_wkv'][0],
     per_layer['xa_wo'][0]) = attention_bwd(0, h2, dao0, sv, (x0, m0, gain(0, 1), gain(0, 2), dx1))
    ex.put_grads('l0', G_L0, {(k, 0): v[0] for k, v in per_layer.items()})
    g_ab_out = matmul(mix0, dm0, 'tn', "d_ab_w_out", BF)
    dmix0 = matmul(dm0, big[('ab_w_out', 0)], 'nt', "d_mix0", after=ex.take_tokens())
    dxbc_act, dz, ddt, ddtb, dalog, ddsk, dnw = ssd_bwd(xbc_act, u0, dtb, alog, dsk, p['ssm_norm'], consts, hs, dmix0, bsz, seq)
    grads['ssm_dt_bias'] = ddtb[:, :SSM_HEADS]
    grads['ssm_a_log'] = dalog[:, :SSM_HEADS]
    grads['ssm_d'] = ddsk[:, :SSM_HEADS]
    grads['ssm_norm'] = dnw
    dxr, dcw, dcb = bwd_call(conv4_fn, "d_ssm_conv", (ncb, bsz), [u0, conv_w, conv_b], conv_in_specs,
                             [dxbc_act], [conv_out_spec], [0, 1, 2],
                             [_sd((t, SSM_CONV_DIM), BF), _sd((SSM_CONV, SSM_CONV_DIM)), _sd((1, SSM_CONV_DIM))],
                             [conv_out_spec, conv_in_specs[1], conv_in_specs[2]], [None, (1,), (1,)])
    grads['ssm_conv_w'], grads['ssm_conv_b'] = _xbc_ungroup(dcw, 1)[None], _xbc_ungroup(dcb, 1)
    dpool, dpw, dps = [], [], []
    for g in range(POOL_GROUPS):
        seqspec = pl.BlockSpec((seq, PG), lambda b, g=g: (b, g))
        one = pl.BlockSpec((seq, PG), lambda b: (b, 0))
        wspec = pl.BlockSpec((PG, PG), lambda b: (0, 0))
        sspec = pl.BlockSpec((1, PG), lambda b, g=g: (0, g))
        a, bb, c = bwd_call(make_pool_fn(g), f"d_pool_{g}", (bsz,), [u0, p['pool_w'][0, g], p['pool_scale']],
                            [seqspec, wspec, sspec], [dmix0], [seqspec], [0, 1, 2],
                            [_sd((t, PG), BF), _sd((PG, PG)), _sd((1, PG))], [one, wspec, pl.BlockSpec((1, PG), lambda b: (0, 0))],
                            [None, (0,), (0,)])
        dpool.append(a)
        dpw.append(bb)
        dps.append(c)
    grads['pool_w'] = jnp.stack(dpw)[None]
    grads['pool_scale'] = jnp.concatenate(dps, axis=1)
    du0 = jnp.concatenate(dpool + [dz, dxr, ddt.astype(BF)], axis=1)
    g_ab_in = matmul(du0, h0, 'tn', "d_ab_w_in", BF)
    g_ab_in = jnp.concatenate([g_ab_in[:xbc0], _xbc_ungroup(g_ab_in[xbc0:xbc0 + SSM_CONV_DIM], 0),
                               g_ab_in[xbc0 + SSM_CONV_DIM:AB_IN]], axis=0)
    ex.put_grads('ab', G_AB, {('ab_w_in', 0): g_ab_in, ('ab_w_out', 0): g_ab_out})
    dx, dg00 = matmul(du0, w_ab_in, 'nn', "d_h_ab", (F32,), epilogue=in_bwd_epilogue, extras=[x0, dx0r], params=[gain(0, 0)],
                      after=ex.take_tokens(), n_acc=1)
    gain_grads[(0, 0)] = dg00
    grads['norm_gains'] = jnp.stack([jnp.concatenate([gain_grads[(l, i)] for i in range(6)], axis=0) for l in range(2)])
    return loss, dx, grads
```

```python
import functools
import math

import numpy as np
import jax
import jax.numpy as jnp
from jax import lax
from jax.experimental import pallas as pl
from jax.experimental.pallas import tpu as pltpu

BF = jnp.bfloat16
F32 = jnp.float32

N_DEV = 8
D = 1024
N_MEM = 256
XA_HEADS = 4
XA_DH = D // XA_HEADS
POOL_GROUPS = 4
PG = 128
POOL_W = POOL_GROUPS * PG
SSM_INNER = 1024
SSM_GROUPS = 2
SSM_GSZ = SSM_INNER // SSM_GROUPS
SSM_HEADS = 16
SSM_P = 64
SSM_N = 128
SSM_CONV = 4
SSM_CONV_DIM = SSM_INNER + 2 * SSM_GROUPS * SSM_N
SSM_XBC_G = SSM_GSZ + 2 * SSM_N
CHUNK = 128
AB_IN = POOL_W + SSM_INNER + SSM_CONV_DIM + SSM_HEADS
AB_IN_PAD = POOL_W + SSM_INNER + SSM_CONV_DIM + 128
AB_OUT = POOL_W + SSM_INNER
CONF_K = 31
SC_K = 3
CD_IN = 5 * D
CD_OUT = 2 * D
MLP_H = 4 * D
RMS_EPS = 1e-6
LN_EPS = 1e-5
ADAM_LR = 0.001
ADAM_B1 = 0.9
ADAM_B2 = 0.999
ADAM_EPS = 1e-08
ADAM_WD = 0.01
ADAM_STEP = 10
VMEM_LIMIT = 56 * 1024 * 1024
LANE = 128

NAMES = ['x', 'mem', 'norm_gains', 'xa_wq', 'xa_wkv', 'xa_wo', 'mlp_w1', 'mlp_w2', 'ab_w_in', 'pool_w', 'pool_scale',
         'ssm_conv_w', 'ssm_conv_b', 'ssm_dt_bias', 'ssm_a_log', 'ssm_d', 'ssm_norm', 'ab_w_out', 'cd_w_in', 'conf_dw_w',
         'conf_dw_b', 'conf_ln_g', 'conf_ln_b', 'sc_conv_w', 'cd_w_out', 'loss_target']
WEIGHTS = NAMES[2:25]
BIG = [('xa_wq', 1), ('xa_wkv', 2), ('xa_wo', 1), ('mlp_w1', 2), ('mlp_w2', 1), ('cd_w_in', 2), ('cd_w_out', 1),
       ('ab_w_out', 1), ('ab_w_in', 2)]
SMALL_SHARDED = ['norm_gains', 'ssm_conv_w', 'conf_dw_w', 'conf_dw_b', 'conf_ln_g', 'conf_ln_b', 'sc_conv_w']
REPLICATED = ['pool_w', 'pool_scale', 'ssm_conv_b', 'ssm_dt_bias', 'ssm_a_log', 'ssm_d', 'ssm_norm']


def _dg(a, b, ca, cb, prec=None):
    return lax.dot_general(a, b, (((ca,), (cb,)), ((), ())), precision=prec, preferred_element_type=F32)


@functools.partial(jax.custom_vjp, nondiff_argnums=(2, 3))
def bdot(a, b, ca, cb):
    return _dg(a.astype(BF), b.astype(BF), ca, cb)


def _bdot_fwd(a, b, ca, cb):
    return bdot(a, b, ca, cb), (a, b)


def _bdot_bwd(ca, cb, res, g):
    a, b = res
    g16, a16, b16 = g.astype(BF), a.astype(BF), b.astype(BF)
    da = _dg(g16, b16, 1, 1 - cb) if ca == 1 else _dg(b16, g16, 1 - cb, 1)
    db = _dg(g16, a16, 0, 1 - ca) if cb == 1 else _dg(a16, g16, 1 - ca, 0)
    return da.astype(a.dtype), db.astype(b.dtype)


bdot.defvjp(_bdot_fwd, _bdot_bwd)


def _split3(a):
    a1 = a.astype(BF)
    r1 = a - a1.astype(F32)
    a2 = r1.astype(BF)
    a3 = (r1 - a2.astype(F32)).astype(BF)
    return a1, a2, a3


def _exact_right(a, c):
    m = a.shape[0]
    if m % 16:
        return sum(_dg(p, c, 1, 0) for p in _split3(a))
    o = _dg(jnp.concatenate(_split3(a), axis=0), c, 1, 0)
    return o[:m] + o[m:2 * m] + o[2 * m:]


def _exact_left(c, a):
    n = a.shape[1]
    o = _dg(c, jnp.concatenate(_split3(a), axis=1), 1, 0)
    return o[:, :n] + o[:, n:2 * n] + o[:, 2 * n:]


@jax.custom_vjp
def cmat(a, c, ct):
    return _exact_right(a, c)


def _cmat_fwd(a, c, ct):
    return cmat(a, c, ct), (c, ct)


def _cmat_bwd(res, g):
    c, ct = res
    return _exact_right(g, ct), jnp.zeros_like(c), jnp.zeros_like(ct)


cmat.defvjp(_cmat_fwd, _cmat_bwd)


@jax.custom_vjp
def cmatl(c, ct, a):
    return _exact_left(c, a)


def _cmatl_fwd(c, ct, a):
    return cmatl(c, ct, a), (c, ct)


def _cmatl_bwd(res, g):
    c, ct = res
    return jnp.zeros_like(c), jnp.zeros_like(ct), _exact_left(ct, g)


cmatl.defvjp(_cmatl_fwd, _cmatl_bwd)


SUBLANES = 8


def _taps(x, shifts, down):
    n, c = x.shape
    pad = _round_up(max(shifts), SUBLANES)
    if pad == 0:
        return {0: x}
    zeros = jnp.zeros((pad, c), x.dtype)
    xp = jnp.concatenate([zeros, x] if down else [x, zeros], axis=0)
    rolled, out = {0: xp}, {}
    for s in shifts:
        a, b = divmod(s, SUBLANES)
        if b not in rolled:
            rolled[b] = pltpu.roll(xp, b if down else n + pad - b, 0)
        off = pad - SUBLANES * a if down else SUBLANES * a
        out[s] = rolled[b][off:off + n]
    return out


def _shift_down(x, k):
    return _taps(x, [k], True)[k]


def _shift_up(x, k):
    return _taps(x, [k], False)[k]


@functools.partial(jax.custom_vjp, nondiff_argnums=(1,))
def shift(x, k):
    return _shift_down(x, k)


def _shift_fwd(x, k):
    return _shift_down(x, k), None


def _shift_bwd(k, _, g):
    return (_shift_up(g, k),)


shift.defvjp(_shift_fwd, _shift_bwd)


@functools.partial(jax.custom_vjp, nondiff_argnums=(2,))
def cconv(u, w, width):
    taps = _taps(u, list(range(width)), True)
    acc = u * w[width - 1:width, :]
    for k in range(width - 1):
        acc = acc + taps[width - 1 - k] * w[k:k + 1, :]
    return acc


def _cconv_fwd(u, w, width):
    return cconv(u, w, width), (u, w)


def _cconv_bwd(width, res, g):
    u, w = res
    rows = lax.broadcasted_iota(jnp.int32, w.shape, 0)
    du = g * w[width - 1:width, :]
    dw = jnp.where(rows == width - 1, jnp.sum(g * u, axis=0, keepdims=True), 0.0)
    g_taps = _taps(g, list(range(width)), False)
    u_taps = _taps(u, list(range(width)), True)
    for k in range(width - 1):
        s = width - 1 - k
        du = du + g_taps[s] * w[k:k + 1, :]
        dw = dw + jnp.where(rows == k, jnp.sum(g * u_taps[s], axis=0, keepdims=True), 0.0)
    return du, dw


cconv.defvjp(_cconv_fwd, _cconv_bwd)


def _rms(x, g):
    return x * lax.rsqrt(jnp.mean(x * x, axis=-1, keepdims=True) + RMS_EPS) * g


def _rms_bwd(v, g, dout):
    r = lax.rsqrt(jnp.mean(v * v, axis=-1, keepdims=True) + RMS_EPS)
    n = v * r
    dn = dout * g
    dv = (dn - n * jnp.mean(dn * n, axis=-1, keepdims=True)) * r
    return dv, jnp.sum(dout * n, axis=0, keepdims=True)


def _params(sem=None):
    return pltpu.CompilerParams(dimension_semantics=sem, vmem_limit_bytes=VMEM_LIMIT)


def _f32(v):
    return v if v.dtype == F32 else v.astype(F32)


def _first(axes):
    ok = None
    for ax in axes:
        c = pl.program_id(ax) == 0
        ok = c if ok is None else jnp.logical_and(ok, c)
    return ok


def fwd_call(fn, name, grid, ins, in_specs, out_shapes, out_specs, into=None):
    n_in = len(ins)
    n_into = 0 if into is None else 1

    def body(*refs):
        outs = fn(*[_f32(r[...]) for r in refs[:n_in]])
        for r, o in zip(refs[n_in + n_into:], outs):
            r[...] = o.astype(r.dtype)

    extra = [] if into is None else [into]
    return pl.pallas_call(body, name=name, grid=grid, in_specs=list(in_specs) + [pl.BlockSpec(memory_space=pl.ANY)] * n_into,
                          out_specs=out_specs, out_shape=out_shapes, input_output_aliases={n_in: 0} if n_into else {},
                          compiler_params=_params())(*ins, *extra)


def bwd_call(fn, name, grid, ins, in_specs, cots, cot_specs, gidx, g_shapes, g_specs, g_acc):
    n_in, n_cot = len(ins), len(cots)

    def body(*refs):
        vals = [_f32(r[...]) for r in refs[:n_in]]

        def f_sel(*dv):
            full = list(vals)
            for i, v in zip(gidx, dv):
                full[i] = v
            return tuple(fn(*full))

        outs, vjp = jax.vjp(f_sel, *[vals[i] for i in gidx])
        cts = tuple(_f32(r[...]) for r in refs[n_in:n_in + n_cot])
        grads = vjp(cts)
        for r, g, acc in zip(refs[n_in + n_cot:], grads, g_acc):
            if acc is None:
                r[...] = g.astype(r.dtype)
            else:
                @pl.when(_first(acc))
                def _():
                    r[...] = jnp.zeros_like(r)

                r[...] += g.astype(r.dtype)

    return pl.pallas_call(body, name=name, grid=grid, in_specs=list(in_specs) + list(cot_specs), out_specs=g_specs,
                          out_shape=g_shapes, compiler_params=_params())(*ins, *cots)


def _tile(dim, pref):
    if dim <= pref:
        return dim
    best = None
    for t in range(LANE, pref + 1, LANE):
        if dim % t == 0:
            best = t
    assert best is not None, dim
    return best


MATMUL_VMEM_BUDGET = 40 * 1024 * 1024


def _matmul_tiles(m, n, k, a_bytes, b_bytes, out_bytes):
    tn = _tile(n, 1024)
    for tk_pref in (k, 2048, 1024, 512):
        tk = _tile(k, tk_pref)
        for tm_pref in (1024, 512, 256):
            tm = _tile(m, tm_pref)
            need = 2 * (tm * tk * a_bytes + tk * tn * b_bytes + tm * tn * out_bytes) + (0 if tk == k else tm * tn * 4)
            need += (tm * tk * 2 if a_bytes == 4 else 0) + (tk * tn * 2 if b_bytes == 4 else 0)
            if need <= MATMUL_VMEM_BUDGET:
                return tm, tn, tk
    raise ValueError((m, n, k))


def matmul(a, b, mode, name, out_dtype=F32, epilogue=None, extras=(), params=(), after=(), n_acc=0):
    if mode == 'nn':
        (m, k), (k2, n) = a.shape, b.shape
    elif mode == 'nt':
        (m, k), (n, k2) = a.shape, b.shape
    else:
        (k, m), (k2, n) = a.shape, b.shape
    assert k == k2, (name, a.shape, b.shape)
    n_extra = len(extras) + len(params)
    out_dtypes = out_dtype if isinstance(out_dtype, tuple) else (out_dtype,)
    per_out = sum(jnp.dtype(dt).itemsize for dt in out_dtypes) + sum(e.dtype.itemsize for e in extras)
    tm, tn, tk = _matmul_tiles(m, n, k, a.dtype.itemsize, b.dtype.itemsize, per_out)
    nk = k // tk
    ca = 0 if mode == 'tn' else 1
    cb = 1 if mode == 'nt' else 0
    a_spec = pl.BlockSpec((tk, tm), lambda i, j, kk: (kk, i)) if mode == 'tn' else pl.BlockSpec((tm, tk), lambda i, j, kk: (i, kk))
    b_spec = pl.BlockSpec((tn, tk), lambda i, j, kk: (j, kk)) if mode == 'nt' else pl.BlockSpec((tk, tn), lambda i, j, kk: (kk, j))

    def finish(o_refs, extra_refs, acc, first_row_tile):
        outs = (acc,) if epilogue is None else epilogue(acc, *[_f32(e[...]) for e in extra_refs])
        n_tile = len(o_refs) - n_acc
        for o_ref, o in zip(o_refs[:n_tile], outs[:n_tile]):
            o_ref[...] = o.astype(o_ref.dtype)
        for o_ref, o in zip(o_refs[n_tile:], outs[n_tile:]):
            o_ref[...] = jnp.where(first_row_tile, o, o_ref[...] + o)

    n_after = len(after)

    def body_whole_k(a_ref, b_ref, *refs):
        refs = refs[n_after:]
        finish(refs[n_extra:], refs[:n_extra], _dg(a_ref[...].astype(BF), b_ref[...].astype(BF), ca, cb), pl.program_id(0) == 0)

    def body_split_k(a_ref, b_ref, *refs):
        refs = refs[n_after:]
        extra_refs, o_refs, acc = refs[:n_extra], refs[n_extra:-1], refs[-1]
        kk = pl.program_id(2)
        first_row_tile = pl.program_id(0) == 0

        @pl.when(kk == 0)
        def _():
            acc[...] = jnp.zeros_like(acc)

        acc[...] += _dg(a_ref[...].astype(BF), b_ref[...].astype(BF), ca, cb)

        @pl.when(kk == nk - 1)
        def _():
            finish(o_refs, extra_refs, acc[...], first_row_tile)

    tile = pl.BlockSpec((tm, tn), lambda i, j, kk: (i, j))
    row = pl.BlockSpec((1, tn), lambda i, j, kk: (0, j))
    n_par = len(params)
    outs = pl.pallas_call(
        body_whole_k if nk == 1 else body_split_k, name=name, grid=(m // tm, n // tn, nk),
        in_specs=[a_spec, b_spec] + [pl.BlockSpec(memory_space=pl.ANY)] * n_after + [tile] * len(extras) + [row] * n_par,
        out_specs=[tile] * len(out_dtypes) + [row] * n_acc,
        out_shape=[jax.ShapeDtypeStruct((m, n), dt) for dt in out_dtypes] + [jax.ShapeDtypeStruct((1, n), F32)] * n_acc,
        scratch_shapes=[] if nk == 1 else [pltpu.VMEM((tm, tn), F32)],
        compiler_params=_params(("arbitrary",) * 3 if n_acc else ("parallel", "parallel", "arbitrary")))(a, b, *after, *extras, *params)
    return outs if isinstance(out_dtype, tuple) or n_acc else outs[0]


_FLIPS = [(0, 0, 1), (1, 0, 0), (0, 1, 0), (1, 1, 0), (1, 0, 1), (0, 1, 1), (1, 1, 1)]


def _me():
    return lax.axis_index("x"), lax.axis_index("y"), lax.axis_index("c")


def _flip(pos, f):
    return tuple(jnp.where(fi == 1, 1 - p, p) if fi else p for p, fi in zip(pos, f))


def _slot(pos):
    return 4 * pos[0] + 2 * pos[1] + pos[2]


def all_gather(v, name):
    def body(v_ref, out_ref, send_sems, recv_sems, local_sem):
        me = _me()
        sibling = _flip(me, (0, 0, 1))
        chips = [_flip(me, f) for f in ((1, 0, 0), (0, 1, 0), (1, 1, 0))]

        def copy(k, block, to, src=None):
            return pltpu.make_async_remote_copy(
                src_ref=out_ref.at[_slot(block)] if src is None else src, dst_ref=out_ref.at[_slot(block)],
                send_sem=send_sems.at[k], recv_sem=recv_sems.at[k], device_id=to, device_id_type=pl.DeviceIdType.MESH)

        mine = pltpu.make_async_copy(v_ref, out_ref.at[_slot(me)], local_sem)
        mine.start()
        first = [copy(0, me, sibling, src=v_ref)] + [copy(1 + j, me, chip, src=v_ref) for j, chip in enumerate(chips)]
        for cp in first:
            cp.start()
        passed = [copy(4 + j, chip, sibling) for j, chip in enumerate(chips)]
        for j, chip in enumerate(chips):
            copy(1 + j, chip, me).wait_recv()
            passed[j].start()
        copy(0, sibling, me).wait_recv()
        for j, chip in enumerate(chips):
            copy(4 + j, _flip(chip, (0, 0, 1)), me).wait_recv()
        for cp in first + passed:
            cp.wait_send()
        mine.wait()

    return pl.pallas_call(
        body, name=name, out_shape=jax.ShapeDtypeStruct((N_DEV,) + v.shape, v.dtype),
        in_specs=[pl.BlockSpec(memory_space=pl.ANY)], out_specs=pl.BlockSpec(memory_space=pl.ANY),
        scratch_shapes=[pltpu.SemaphoreType.DMA((7,)), pltpu.SemaphoreType.DMA((7,)), pltpu.SemaphoreType.DMA(())],
    )(v)


def sum_slots(v, name, tr=256):
    _, r, c = v.shape
    tr = _tile_rows(r, tr)

    def body(v_ref, o_ref):
        acc = v_ref[0].astype(F32)
        for s in range(1, N_DEV):
            acc = acc + v_ref[s].astype(F32)
        o_ref[...] = acc

    return pl.pallas_call(body, name=name, grid=(r // tr,), in_specs=[pl.BlockSpec((N_DEV, tr, c), lambda i: (0, i, 0))],
                          out_specs=pl.BlockSpec((tr, c), lambda i: (i, 0)), out_shape=jax.ShapeDtypeStruct((r, c), F32),
                          compiler_params=_params())(v)


def _tile_rows(r, pref):
    if r <= pref:
        return r
    best = None
    for t in range(8, pref + 1, 8):
        if r % t == 0:
            best = t
    return r if best is None else best


def _adamw_math(w, m, v, g):
    nm = ADAM_B1 * m + (1.0 - ADAM_B1) * g
    nv = ADAM_B2 * v + (1.0 - ADAM_B2) * jnp.square(g)
    m_hat = nm / (1.0 - ADAM_B1 ** ADAM_STEP)
    v_hat = nv / (1.0 - ADAM_B2 ** ADAM_STEP)
    return -ADAM_LR * (m_hat / (jnp.sqrt(v_hat) + ADAM_EPS) + ADAM_WD * w), nm, nv


def update_from_slots(lands, offs, w, m, v, transposed, name):
    layers, a, b = w.shape
    n_land = len(lands)
    if transposed:
        rb, tk = LANE, 1024
        assert a % tk == 0 and b % rb == 0 and all(o % rb == 0 for o in offs), (name, w.shape, offs)
        grid = (layers, a // tk, b // rb)
        land_block = (N_DEV, rb, tk)
        tile = pl.BlockSpec((None, tk, rb), lambda l, i, j: (l, i, j))

        def land_spec(layer):
            base = offs[layer] // rb
            return pl.BlockSpec(land_block, lambda l, i, j: (0, base + jnp.where(l == layer, j, 0), jnp.where(l == layer, i, 0)))
    else:
        fits = [t for t in (256, 128, 64) if a % t == 0 and all(o % t == 0 for o in offs)]
        assert fits or all(o == 0 for o in offs), (name, w.shape, offs)
        tr = max(fits) if fits else a
        grid = (layers, a // tr)
        land_block = (N_DEV, _round_up(tr, MEMBER_ROW_TILE), b)
        tile = pl.BlockSpec((None, tr, b), lambda l, i: (l, i, 0))

        def land_spec(layer):
            base = offs[layer] // tr
            return pl.BlockSpec(land_block, lambda l, i: (0, base + jnp.where(l == layer, i, 0), 0))

    def body(*refs):
        land_refs, (w_ref, m_ref, v_ref, g_ref, d_ref, nm_ref, nv_ref, acc) = refs[:n_land], refs[n_land:]
        for layer, land in enumerate(land_refs):
            @pl.when(pl.program_id(0) == layer)
            def _(land=land):
                rows = acc.shape[0]
                s = land[0, :rows].astype(F32)
                for k in range(1, N_DEV):
                    s = s + land[k, :rows].astype(F32)
                acc[...] = s

        g = acc[...].T if transposed else acc[...]
        d, nm, nv = _adamw_math(w_ref[...], m_ref[...], v_ref[...], g)
        g_ref[...] = g
        d_ref[...] = d
        nm_ref[...] = nm
        nv_ref[...] = nv

    sh = jax.ShapeDtypeStruct(w.shape, F32)
    return pl.pallas_call(
        body, name=name, grid=grid, in_specs=[land_spec(layer) for layer in range(n_land)] + [tile] * 3, out_specs=[tile] * 4,
        out_shape=[sh] * 4, scratch_shapes=[pltpu.VMEM((rb, tk) if transposed else (tr, b), F32)],
        compiler_params=_params())(*lands, w, m, v)


def adamw_many(ws, ms, vs, gs, name):
    n = len(ws)

    def body(*refs):
        for i in range(n):
            d, nm, nv = _adamw_math(refs[i][...], refs[n + i][...], refs[2 * n + i][...], refs[3 * n + i][...])
            refs[4 * n + i][...] = d
            refs[5 * n + i][...] = nm
            refs[6 * n + i][...] = nv

    vmem = pl.BlockSpec(memory_space=pltpu.VMEM)
    shapes = [jax.ShapeDtypeStruct(a.shape, F32) for a in ws]
    res = pl.pallas_call(body, name=name, in_specs=[vmem] * (4 * n), out_specs=[vmem] * (3 * n), out_shape=shapes * 3,
                         compiler_params=_params())(*ws, *ms, *vs, *gs)
    return res[:n], res[n:2 * n], res[2 * n:]


def seg_in(x, g):
    return (_rms(x, g),)


def seg_res(x, m, ga, gb):
    x1 = x + _rms(m, ga)
    return x1, _rms(x1, gb)


def act_epilogue(r):
    t = jnp.maximum(r, 0.0)
    return r, t * t


def res_epilogue(m, x, ga, gb):
    x1, h = seg_res(x, m, ga, gb)
    return m, x1, h


def res_bwd_epilogue(dh, x, m, dx1, ga, gb):
    x1 = x + _rms(m, ga)
    d1, dgb = _rms_bwd(x1, gb, dh)
    dx = dx1 + d1
    dm, dga = _rms_bwd(m, ga, dx)
    return dx, dm, dga, dgb


def in_bwd_epilogue(dh, x, dx_res, g):
    d, dg = _rms_bwd(x, g, dh)
    return dx_res + d, dg


def loss_epilogue(mo, x, target, g):
    d = x + _rms(mo, g) - target
    dy = d / float(D)
    dm, dg = _rms_bwd(mo, g, dy)
    return dy, dm, dg, jnp.sum(d * d, axis=0, keepdims=True)


def act_bwd_epilogue(drr, r):
    return (drr * (2.0 * jnp.maximum(r, 0.0)),)


def seg_ln(v, g, b):
    mu = jnp.mean(v, axis=-1, keepdims=True)
    var = jnp.mean(jnp.square(v - mu), axis=-1, keepdims=True)
    vn = (v - mu) * lax.rsqrt(var + LN_EPS) * g + b
    return (jax.nn.silu(vn),)


def make_pool_fn(group):
    window = 2 ** (group + 1)

    def pool_fn(ug, pw, scale):
        s = ug
        for lvl in range(group + 1):
            s = s + shift(s, 2 ** lvl)
        cnt = jnp.minimum(lax.broadcasted_iota(jnp.int32, ug.shape, 0) + 1, window).astype(F32)
        return (bdot(s / cnt - ug, pw, 1, 0) * scale,)

    return pool_fn


def conv4_fn(xr, w, b):
    return (jax.nn.silu(cconv(xr, w, SSM_CONV) + b),)


def cd1_fn(u, dww, dwb, scw):
    val, gate, bg, cg, hh = (u[:, k * LANE:(k + 1) * LANE] for k in range(5))
    v = val * jax.nn.sigmoid(gate)
    vc = cconv(v, dww, CONF_K) + dwb
    sc = bg * cconv(cg * hh, scw, SC_K)
    return vc, sc


def attn_fn(q, kv):
    outs = []
    for h in range(XA_HEADS):
        cols = slice(h * XA_DH, (h + 1) * XA_DH)
        s = bdot(q[:, cols], kv[:, cols], 1, 1) / math.sqrt(XA_DH)
        p = jax.nn.softmax(s, axis=-1)
        outs.append(bdot(p, kv[:, D + h * XA_DH:D + (h + 1) * XA_DH], 1, 0))
    return (jnp.concatenate(outs, axis=1),)


def ssd_chunk(xbc, z, dtraw, dtb, alog, dsk, nw, h0, h1, h2, h3, e64, e64t, ecat, ecatt, tril, trilt):
    xs, bm, cm = xbc[:, :SSM_GSZ], xbc[:, SSM_GSZ:SSM_GSZ + SSM_N], xbc[:, SSM_GSZ + SSM_N:]
    hin = (h0, h1, h2, h3)
    dt = jax.nn.softplus(dtraw + dtb)
    a = -jnp.exp(alog)
    d_a = dt * a
    cs = cmatl(tril, trilt, d_a)
    cs_cat = cmat(cs, ecat, ecatt)
    cs64, cs128 = cs_cat[:, :SSM_GSZ], cs_cat[:, SSM_GSZ:]
    dt64 = cmat(dt, e64, e64t)
    row = lax.broadcasted_iota(jnp.int32, (8, LANE), 0)
    heads = jnp.where(row == 0, dsk, jnp.where(row == 1, jnp.sum(d_a, axis=0, keepdims=True), 0.0))
    heads64 = cmat(heads, e64, e64t)
    d64, tot64 = heads64[0:1, :], heads64[1:2, :]
    xdt = xs * dt64
    cb = bdot(cm, bm, 1, 1)
    li = lax.broadcasted_iota(jnp.int32, (CHUNK, CHUNK), 0)
    si = lax.broadcasted_iota(jnp.int32, (CHUNK, CHUNK), 1)
    causal = li >= si
    lane = lax.broadcasted_iota(jnp.int32, (CHUNK, LANE), 1)
    xw = xdt * jnp.exp(tot64 - cs64)
    ecs = jnp.exp(cs64)
    etot = jnp.exp(tot64)
    ycols, hout = [], []
    for j in range(4):
        sl = slice(j * LANE, (j + 1) * LANE)
        xj = xdt[:, sl]
        ys = []
        for hh in range(2):
            r = 2 * j + hh
            col = cs128[:, r * LANE:(r + 1) * LANE]
            decay = jnp.exp(jnp.where(causal, col - col.T, -1e30))
            ys.append(bdot(cb * decay, xj, 1, 0))
        y_diag = jnp.where(lane < SSM_P, ys[0], ys[1])
        y_off = bdot(cm, hin[j], 1, 0) * ecs[:, sl]
        ycols.append(y_diag + y_off)
        hout.append(etot[:, sl] * hin[j] + bdot(bm, xw[:, sl], 0, 0))
    y = jnp.concatenate(ycols, axis=1) + d64 * xs
    y = y * jax.nn.silu(z)
    yn = y * lax.rsqrt(jnp.mean(y * y, axis=-1, keepdims=True) + RMS_EPS) * nw
    return (yn,) + tuple(hout)


def _xbc_group(a, axis):
    parts = []
    for g in range(SSM_GROUPS):
        for start, width in ((g * SSM_GSZ, SSM_GSZ), (SSM_INNER + g * SSM_N, SSM_N), (SSM_INNER + (SSM_GROUPS + g) * SSM_N, SSM_N)):
            parts.append(lax.slice_in_dim(a, start, start + width, axis=axis))
    return jnp.concatenate(parts, axis=axis)


def _xbc_ungroup(a, axis):
    xs, bs, cs = [], [], []
    for g in range(SSM_GROUPS):
        base = g * SSM_XBC_G
        xs.append(lax.slice_in_dim(a, base, base + SSM_GSZ, axis=axis))
        bs.append(lax.slice_in_dim(a, base + SSM_GSZ, base + SSM_GSZ + SSM_N, axis=axis))
        cs.append(lax.slice_in_dim(a, base + SSM_GSZ + SSM_N, base + SSM_XBC_G, axis=axis))
    return jnp.concatenate(xs + bs + cs, axis=axis)


def _ssd_consts():
    h = np.arange(LANE)[:, None]
    e64 = np.stack([(h == g * 8 + np.arange(SSM_GSZ)[None, :] // SSM_P) for g in range(SSM_GROUPS)]).astype(np.float32)
    e128 = np.stack([(h == g * 8 + np.arange(8 * LANE)[None, :] // LANE) for g in range(SSM_GROUPS)]).astype(np.float32)
    ecat = np.concatenate([e64, e128], axis=2)
    tril = np.tril(np.ones((CHUNK, CHUNK), np.float32))
    return tuple(jnp.asarray(c, dtype=BF) for c in (e64, e64.transpose(0, 2, 1), ecat, ecat.transpose(0, 2, 1), tril, tril.T))


def _ssd_specs(nc, rev):
    def ci(c):
        return nc - 1 - c if rev else c

    def row(width, col):
        return pl.BlockSpec((CHUNK, width), lambda b, c: (b * nc + ci(c), col))

    def whole(shape):
        return pl.BlockSpec(shape, lambda b, c: (0,) * len(shape))

    data = [row(SSM_CONV_DIM, 0),
            row(SSM_GSZ, 1), row(SSM_GSZ, 2), row(LANE, 24)]
    par = [whole((1, LANE))] * 3 + [whole((1, SSM_INNER))]
    cst = [whole((SSM_GROUPS, LANE, SSM_GSZ)), whole((SSM_GROUPS, SSM_GSZ, LANE)), whole((SSM_GROUPS, LANE, 12 * LANE)),
           whole((SSM_GROUPS, 12 * LANE, LANE)), whole((CHUNK, CHUNK)), whole((CHUNK, CHUNK))]
    hsave = pl.BlockSpec((None, None, SSM_GROUPS, 4, SSM_N, LANE), lambda b, c: (b, ci(c), 0, 0, 0, 0))
    return data, par, cst, hsave, row, whole


def _ssd_group_args(g, xbc, z, dtr, dtb, alog, dsk, nw):
    return (xbc[:, g * SSM_XBC_G:(g + 1) * SSM_XBC_G], z[g], dtr, dtb, alog, dsk, nw[:, g * SSM_GSZ:(g + 1) * SSM_GSZ])


def ssd_fwd(xbc_act, u, dtb, alog, dsk, nw, consts, bsz, seq):
    nc = seq // CHUNK
    data, par, cst, hsave, row, _ = _ssd_specs(nc, False)

    def body(xbc, z0, z1, dtr, dtb_r, alog_r, dsk_r, nw_r, e64, e64t, ecat, ecatt, tril, trilt, yn_ref, hs_ref, h):
        @pl.when(pl.program_id(1) == 0)
        def _():
            h[...] = jnp.zeros_like(h)

        hs_ref[...] = h[...]
        ys = []
        for g in range(SSM_GROUPS):
            args = _ssd_group_args(g, xbc[...], (z0[...], z1[...]), dtr[...], dtb_r[...], alog_r[...], dsk_r[...], nw_r[...])
            outs = ssd_chunk(*args, h[g, 0], h[g, 1], h[g, 2], h[g, 3], e64[g], e64t[g], ecat[g], ecatt[g], tril[...], trilt[...])
            ys.append(outs[0])
            for j in range(4):
                h[g, j] = outs[1 + j]
        yn_ref[...] = jnp.concatenate(ys, axis=1).astype(yn_ref.dtype)

    t = bsz * seq
    return pl.pallas_call(
        body, name="ssd_fwd", grid=(bsz, nc), in_specs=data + par + cst, out_specs=[row(SSM_INNER, 0), hsave],
        out_shape=[jax.ShapeDtypeStruct((t, SSM_INNER), BF), jax.ShapeDtypeStruct((bsz, nc, SSM_GROUPS, 4, SSM_N, LANE), F32)],
        scratch_shapes=[pltpu.VMEM((SSM_GROUPS, 4, SSM_N, LANE), F32)], compiler_params=_params(),
    )(xbc_act, u, u, u, dtb, alog, dsk, nw, *consts)


def ssd_bwd(xbc_act, u, dtb, alog, dsk, nw, consts, hs, dmix, bsz, seq):
    nc = seq // CHUNK
    data, par, cst, hsave, row, whole = _ssd_specs(nc, True)
    t = bsz * seq
    pcol = POOL_W // SSM_GSZ

    def body(xbc, z0, z1, dtr, dtb_r, alog_r, dsk_r, nw_r, e64, e64t, ecat, ecatt, tril, trilt, hs_ref, dy0, dy1,
             dxbc, dz, ddt, ddtb, dalog, ddsk, dnw, dh):
        @pl.when(pl.program_id(1) == 0)
        def _():
            dh[...] = jnp.zeros_like(dh)

        per_group = []
        for g, dyn in enumerate((dy0, dy1)):
            cst_vals = (e64[g], e64t[g], ecat[g], ecatt[g], tril[...], trilt[...])
            prim = _ssd_group_args(g, xbc[...], (z0[...], z1[...]), dtr[...], dtb_r[...], alog_r[...], dsk_r[...], nw_r[...])
            prim = prim + (hs_ref[g, 0], hs_ref[g, 1], hs_ref[g, 2], hs_ref[g, 3])
            _, vjp = jax.vjp(lambda *args, c=cst_vals: ssd_chunk(*args, *c), *prim)
            gr = vjp((dyn[...].astype(F32), dh[g, 0], dh[g, 1], dh[g, 2], dh[g, 3]))
            for j in range(4):
                dh[g, j] = gr[7 + j]
            per_group.append(gr)
        g0, g1 = per_group
        dxbc[...] = jnp.concatenate([g0[0], g1[0]], axis=1)
        dz[...] = jnp.concatenate([g0[1], g1[1]], axis=1).astype(dz.dtype)
        ddt[...] = g0[2] + g1[2]

        @pl.when(_first((0, 1)))
        def _():
            for r in (ddtb, dalog, ddsk, dnw):
                r[...] = jnp.zeros_like(r)

        ddtb[...] += g0[3] + g1[3]
        dalog[...] += g0[4] + g1[4]
        ddsk[...] += g0[5] + g1[5]
        dnw[...] += jnp.concatenate([g0[6], g1[6]], axis=1)

    out_specs = [row(SSM_CONV_DIM, 0), row(SSM_INNER, 0), row(LANE, 0), whole((1, LANE)), whole((1, LANE)), whole((1, LANE)),
                 whole((1, SSM_INNER))]
    lane = jax.ShapeDtypeStruct((1, LANE), F32)
    out_shape = [jax.ShapeDtypeStruct((t, SSM_CONV_DIM), F32), jax.ShapeDtypeStruct((t, SSM_INNER), BF),
                 jax.ShapeDtypeStruct((t, LANE), F32), lane, lane, lane, jax.ShapeDtypeStruct((1, SSM_INNER), F32)]
    return pl.pallas_call(
        body, name="ssd_bwd", grid=(bsz, nc), in_specs=data + par + cst + [hsave, row(SSM_GSZ, pcol), row(SSM_GSZ, pcol + 1)],
        out_specs=out_specs, out_shape=out_shape, scratch_shapes=[pltpu.VMEM((SSM_GROUPS, 4, SSM_N, LANE), F32)],
        compiler_params=_params(),
    )(xbc_act, u, u, u, dtb, alog, dsk, nw, *consts, hs, dmix, dmix)


TB = 512


def _rows(d, col=0):
    return pl.BlockSpec((TB, d), lambda i: (i, col))


def _par(d):
    return pl.BlockSpec((1, d), lambda i: (0, 0))


def _sd(shape, dtype=F32):
    return jax.ShapeDtypeStruct(shape, dtype)


def _round_up(n, m):
    return -(-n // m) * m


def _pad_rows(a, rows):
    return jnp.pad(a, ((0, rows - a.shape[0]), (0, 0)))


def _pack128(arrs):
    flat = jnp.concatenate([a.reshape(-1) for a in arrs])
    n = flat.shape[0]
    rows = -(-n // (8 * LANE)) * 8
    return jnp.pad(flat, (0, rows * LANE - n)).reshape(rows, LANE)


def _unpack128(packed, shapes):
    flat = packed.reshape(-1)
    out, off = [], 0
    for s in shapes:
        n = int(np.prod(s))
        out.append(flat[off:off + n].reshape(s))
        off += n
    return out


def kernel(x, mem, norm_gains, xa_wq, xa_wkv, xa_wo, mlp_w1, mlp_w2, ab_w_in, pool_w, pool_scale, ssm_conv_w, ssm_conv_b, ssm_dt_bias, ssm_a_log, ssm_d, ssm_norm, ab_w_out, cd_w_in, conf_dw_w, conf_dw_b, conf_ln_g, conf_ln_b, sc_conv_w, cd_w_out, loss_target, m_norm_gains, m_xa_wq, m_xa_wkv, m_xa_wo, m_mlp_w1, m_mlp_w2, m_ab_w_in, m_pool_w, m_pool_scale, m_ssm_conv_w, m_ssm_conv_b, m_ssm_dt_bias, m_ssm_a_log, m_ssm_d, m_ssm_norm, m_ab_w_out, m_cd_w_in, m_conf_dw_w, m_conf_dw_b, m_conf_ln_g, m_conf_ln_b, m_sc_conv_w, m_cd_w_out, v_norm_gains, v_xa_wq, v_xa_wkv, v_xa_wo, v_mlp_w1, v_mlp_w2, v_ab_w_in, v_pool_w, v_pool_scale, v_ssm_conv_w, v_ssm_conv_b, v_ssm_dt_bias, v_ssm_a_log, v_ssm_d, v_ssm_norm, v_ab_w_out, v_cd_w_in, v_conf_dw_w, v_conf_dw_b, v_conf_ln_g, v_conf_ln_b, v_sc_conv_w, v_cd_w_out):
    args = locals()
    w = {n: args[n] for n in WEIGHTS}
    mom_m = {n: args["m_" + n] for n in WEIGHTS}
    mom_v = {n: args["v_" + n] for n in WEIGHTS}
    ex = Exchange(w)
    loss_local, grad_x, small_grads = local_step(x, mem, loss_target, ex)
    outs = {}

    started = ex.put_small(small_grads, loss_local)
    landed = {key: ex.landed(key, started) for key in ('l1', 'cd', 'l0')}
    late = []
    for n, keys in (('mlp_w1', ('l0', 'l1')), ('mlp_w2', ('l0', 'l1')), ('xa_wkv', ('l0', 'l1')), ('xa_wq', ('l0', 'l1')),
                    ('xa_wo', ('l0', 'l1')), ('cd_w_in', ('cd',)), ('cd_w_out', ('cd',))):
        lands = [landed[key][0] for key in keys]
        offs = [landed[key][1][(n, layer)] for layer, key in enumerate(keys)]
        outs[n] = update_from_slots(lands, offs, w[n], mom_m[n], mom_v[n], SHARD_AXIS[n] == 2, "update_" + n)
        late.append(outs[n][1])
    g_own, loss = ex.reduced_small(late)
    land_ab, offs_ab = ex.landed('ab', late)
    outs['ab_w_out'] = update_from_slots([land_ab], [offs_ab[('ab_w_out', 0)]], w['ab_w_out'], mom_m['ab_w_out'],
                                         mom_v['ab_w_out'], False, "update_ab_w_out")
    res = update_from_slots([land_ab], [offs_ab[('ab_w_in', 0)]], jnp.swapaxes(w['ab_w_in'], 1, 2), jnp.swapaxes(mom_m['ab_w_in'], 1, 2),
                            jnp.swapaxes(mom_v['ab_w_in'], 1, 2), False, "update_ab_w_in")
    outs['ab_w_in'] = tuple(jnp.swapaxes(r, 1, 2) for r in res)
    small = SMALL_SHARDED + REPLICATED
    upd = adamw_many([w[n] for n in small], [mom_m[n] for n in small], [mom_v[n] for n in small], [g_own[n] for n in small],
                     "adamw_small")
    for i, n in enumerate(small):
        outs[n] = (g_own[n], upd[0][i], upd[1][i], upd[2][i])
    return (loss, grad_x.reshape(x.shape), *[outs[n][0] for n in WEIGHTS], *[outs[n][1] for n in WEIGHTS],
            *[outs[n][2] for n in WEIGHTS], *[outs[n][3] for n in WEIGHTS])


G_AB = (('ab_w_in', 0), ('ab_w_out', 0))
G_L0 = (('xa_wq', 0), ('xa_wkv', 0), ('xa_wo', 0), ('mlp_w1', 0), ('mlp_w2', 0))
G_L1 = (('xa_wq', 1), ('xa_wkv', 1), ('xa_wo', 1), ('mlp_w1', 1), ('mlp_w2', 1))
G_CD = (('cd_w_in', 0), ('cd_w_out', 0))
GATHER_GROUPS = {'ab': G_AB[:1], 'l0a': G_AB[1:] + G_L0[:3], 'l0b': G_L0[3:], 'cd': G_CD, 'l1a': G_L1[:3], 'l1b': G_L1[3:]}
SHARD_AXIS = dict(BIG)
MEMBER_ROW_TILE = 64
FLAT_ROW_TILE = 128


def _members(group, w):
    out = []
    for n, layer in group:
        shp = w[n].shape[1:]
        if SHARD_AXIS[n] == 2:
            shp = (shp[1], shp[0])
        assert shp[1] == D, (n, shp)
        out.append((n, layer, shp, shp[0], _round_up(shp[0], MEMBER_ROW_TILE)))
    return out


def _group_rows(group, w):
    return _round_up(sum(m[4] for m in _members(group, w)), FLAT_ROW_TILE)


def _flat_shards(group, w):
    parts = []
    for n, layer, _, _, padded in _members(group, w):
        shard = w[n][layer].astype(BF)
        parts.append(_pad_rows(shard.T if SHARD_AXIS[n] == 2 else shard, padded))
    return _pad_rows(jnp.concatenate(parts, axis=0), _group_rows(group, w))


def _full_from_slots(land, group, w):
    out, off = {}, 0
    for n, layer, shp, rows, padded in _members(group, w):
        out[(n, layer)] = land[:, off:off + rows].reshape(N_DEV * rows, D)
        off += padded
    return out


def _slots_from_full(grads, group, w):
    parts = []
    for n, layer, shp, rows, padded in _members(group, w):
        blk = grads[(n, layer)].astype(BF).reshape(N_DEV, rows, D)
        parts.append(jnp.pad(blk, ((0, 0), (0, padded - rows), (0, 0))))
    send = jnp.concatenate(parts, axis=1)
    return jnp.pad(send, ((0, 0), (0, _group_rows(group, w) - send.shape[1]), (0, 0)))


_HBM = pl.BlockSpec(memory_space=pltpu.HBM)
_SEM = pl.BlockSpec(memory_space=pltpu.SEMAPHORE)
_ANY = pl.BlockSpec(memory_space=pl.ANY)


def _peer_copy(k, src, dst, send_sems, recv_sems, peer):
    return pltpu.make_async_remote_copy(src_ref=src, dst_ref=dst, send_sem=send_sems.at[k], recv_sem=recv_sems.at[k],
                                        device_id=peer, device_id_type=pl.DeviceIdType.MESH)


def exchange_start(src, name, scatter, after=()):
    shape = src.shape[-2:]
    after = list(after)

    def body(src_ref, land_ref, *rest):
        send_sems, recv_sems, token = rest[len(after)], rest[len(after) + 1], rest[-1]
        me = _me()
        for k, f in enumerate(_FLIPS):
            peer = _flip(me, f)
            piece = src_ref.at[_slot(peer)] if scatter else src_ref
            _peer_copy(k, piece, land_ref.at[_slot(me)], send_sems, recv_sems, peer).start()
        token[...] = jnp.zeros_like(token)

    land = pltpu.with_memory_space_constraint(lax.empty((N_DEV,) + shape, src.dtype), pltpu.HBM)
    return pl.pallas_call(
        body, name=name,
        out_shape=(pltpu.SemaphoreType.DMA((7,)), pltpu.SemaphoreType.DMA((7,)), pltpu.HBM(src.shape, src.dtype),
                   pltpu.HBM((N_DEV,) + shape, src.dtype), jax.ShapeDtypeStruct((8, LANE), F32)),
        in_specs=(_HBM, _HBM) + (_ANY,) * len(after), out_specs=(_SEM, _SEM, _HBM, _HBM, pl.BlockSpec(memory_space=pltpu.VMEM)),
        input_output_aliases={0: 2, 1: 3},
        compiler_params=pltpu.CompilerParams(has_side_effects=pltpu.SideEffectType.DATAFLOW_SIDE_EFFECTING),
    )(pltpu.with_memory_space_constraint(src, pltpu.HBM), land, *after)


def exchange_wait(handles, after, name, scatter):
    send_sems, recv_sems, src_thru, land_thru, _ = handles
    after = list(after) if isinstance(after, (list, tuple)) else [after]

    def body(src_ref, land_ref, send_sems, recv_sems, *rest):
        token = rest[-1]
        me = _me()
        for k, f in enumerate(_FLIPS):
            peer = _flip(me, f)
            piece = src_ref.at[_slot(peer)] if scatter else src_ref
            cp = _peer_copy(k, piece, land_ref.at[_slot(peer)], send_sems, recv_sems, peer)
            cp.wait_send()
            cp.wait_recv()
        token[...] = jnp.zeros_like(token)

    return pl.pallas_call(
        body, name=name, out_shape=(pltpu.HBM(src_thru.shape, src_thru.dtype), pltpu.HBM(land_thru.shape, land_thru.dtype),
                                    jax.ShapeDtypeStruct((8, LANE), F32)),
        in_specs=(_HBM, _HBM, _SEM, _SEM) + (_ANY,) * len(after), out_specs=(_HBM, _HBM, pl.BlockSpec(memory_space=pltpu.VMEM)),
        input_output_aliases={0: 0, 1: 1},
        compiler_params=pltpu.CompilerParams(has_side_effects=pltpu.SideEffectType.DATAFLOW_SIDE_EFFECTING),
    )(src_thru, land_thru, send_sems, recv_sems, *after)


class Exchange:
    def __init__(self, w):
        self.w = w
        self.me = _slot(_me())
        shapes = [w[n].shape for n in SMALL_SHARDED]
        gs = all_gather(_pack128([w[n] for n in SMALL_SHARDED]), "gather_small")
        per_dev = [_unpack128(gs[d], shapes) for d in range(N_DEV)]
        self.small = {n: jnp.concatenate([per_dev[d][i] for d in range(N_DEV)], axis=-1) for i, n in enumerate(SMALL_SHARDED)}
        self.small.update({n: w[n] for n in REPLICATED})
        first = all_gather(_flat_shards(GATHER_GROUPS['ab'], w), "gather_ab")
        self.first = _full_from_slots(first, GATHER_GROUPS['ab'], w)
        self.gathers, self.done, self.tokens, self.reductions = {}, {}, [], {}
        self.start_gather('l0a', after=[first])
        self.start_gather('l0b', after=[self.gathers['l0a'][4]])

    def take_tokens(self):
        toks, self.tokens = self.tokens, []
        return toks

    def start_gather(self, key, after=()):
        group = GATHER_GROUPS[key]
        self.gathers[key] = exchange_start(_flat_shards(group, self.w), f"gather_{key}_start", False, after=after)
        self.tokens.append(self.gathers[key][4])

    def weights(self, key, after):
        if key == 'ab':
            return self.first
        handles = self.gathers[key]
        _, land, self.done[key] = exchange_wait(handles, after, f"gather_{key}_wait", False)
        land = lax.dynamic_update_slice(land, handles[2][None], (self.me, 0, 0))
        return _full_from_slots(land, GATHER_GROUPS[key], self.w)

    def put_grads(self, key, group, grads):
        send = _slots_from_full(grads, group, self.w)
        handles = exchange_start(send, f"reduce_{key}_start", True)
        self.reductions[key] = (group, handles)
        self.tokens.append(handles[4])

    def landed(self, key, after):
        group, handles = self.reductions[key]
        send, land, _ = exchange_wait(handles, after, f"reduce_{key}_wait", True)
        mine = lax.dynamic_slice_in_dim(send, self.me, 1, axis=0)
        land = lax.dynamic_update_slice(land, mine, (self.me, 0, 0))
        offs, off = {}, 0
        for n, layer, _, _, padded in _members(group, self.w):
            offs[(n, layer)] = off
            off += padded
        return land, offs

    def put_small(self, small_grads, loss_local):
        small = SMALL_SHARDED + REPLICATED
        self.small_shapes = [small_grads[n].shape for n in small] + [(1,)]
        packed = _pack128([small_grads[n] for n in small] + [loss_local.reshape(1)])
        self.small_handles = exchange_start(packed, "gather_small_grads_start", False)
        return self.small_handles[4]

    def reduced_small(self, after):
        small = SMALL_SHARDED + REPLICATED
        src, land, _ = exchange_wait(self.small_handles, after, "gather_small_grads_wait", False)
        gs = lax.dynamic_update_slice(land, src[None], (self.me, 0, 0))
        tot = _unpack128(sum_slots(gs, "sum_small", 1024), self.small_shapes)
        out = {}
        for n, g in zip(small, tot):
            if n in SMALL_SHARDED:
                width = self.w[n].shape[-1]
                g = lax.dynamic_slice_in_dim(g, self.me * width, width, axis=g.ndim - 1)
            out[n] = g
        return out, tot[-1].reshape(())


def local_step(x, mem, target, ex):
    bsz, seq, _ = x.shape
    t = bsz * seq
    nb = t // TB
    nc = seq // CHUNK
    x0 = x.reshape(t, D)
    mem2 = mem.reshape(bsz * N_MEM, D)
    tgt = target.reshape(t, D)
    p = ex.small
    gains = p['norm_gains']
    big = {}

    def gain(layer, i):
        g = gains[layer, i].reshape(1, D)
        for tok in ex.take_tokens():
            g = g + tok[0, 0]
        return g

    consts = _ssd_consts()
    grads = {}
    saved = [dict(), dict()]

    def matmul_res(a, b, name, xin, ga, gb):
        return matmul(a, b, 'nn', name, (F32, F32, BF), epilogue=res_epilogue, extras=[xin], params=[ga, gb])

    def attn_specs():
        nq = seq // TB
        q = pl.BlockSpec((TB, D), lambda b, i: (b * nq + i, 0))
        kv = pl.BlockSpec((N_MEM, 2 * D), lambda b, i: (b, 0))
        return (bsz, nq), q, kv

    def attention_fwd(layer, xin, hin, sv, ga, gb):
        q = matmul(hin, big[('xa_wq', layer)], 'nn', f"q_{layer}", BF)
        kv = matmul(mem2, big[('xa_wkv', layer)], 'nt', f"kv_{layer}", BF)
        grid, qs, kvs = attn_specs()
        o, = fwd_call(attn_fn, f"attn_{layer}", grid, [q, kv], [qs, kvs], [_sd((t, D), BF)], [qs])
        ao, x_next, h_next = matmul_res(o, big[('xa_wo', layer)], f"ao_{layer}", xin, ga, gb)
        sv.update(q=q, kv=kv, o=o, ao=ao)
        return ao, x_next, h_next

    def mlp_fwd(layer, hin, sv, res):
        r, rr = matmul(hin, big[('mlp_w1', layer)], 'nt', f"mlp1_{layer}", (BF, BF), epilogue=act_epilogue)
        out = matmul_res(rr, big[('mlp_w2', layer)], f"mlp2_{layer}", *res)
        sv.update(r=r, rr=rr, mo=out[0])
        return out

    sv = saved[0]
    h0, = fwd_call(seg_in, "norm_in", (nb,), [x0, gain(0, 0)], [_rows(D), _par(D)], [_sd((t, D), BF)], [_rows(D)])
    big.update(ex.weights('ab', h0))
    xbc0 = POOL_W + SSM_INNER
    w_ab_in = big[('ab_w_in', 0)]
    w_ab_in = _pad_rows(jnp.concatenate([w_ab_in[:xbc0], _xbc_group(w_ab_in[xbc0:xbc0 + SSM_CONV_DIM], 0),
                                         w_ab_in[xbc0 + SSM_CONV_DIM:]], axis=0), AB_IN_PAD)
    conv_w, conv_b = _xbc_group(p['ssm_conv_w'][0], 1), _xbc_group(p['ssm_conv_b'], 1)
    u0 = matmul(h0, w_ab_in, 'nt', "ab_in")
    pool_outs = []
    for g in range(POOL_GROUPS):
        seqspec = pl.BlockSpec((seq, PG), lambda b, g=g: (b, g))
        po, = fwd_call(make_pool_fn(g), f"pool_{g}", (bsz,), [u0, p['pool_w'][0, g], p['pool_scale']],
                       [seqspec, pl.BlockSpec((PG, PG), lambda b: (0, 0)), pl.BlockSpec((1, PG), lambda b, g=g: (0, g))],
                       [_sd((t, PG), BF)], [pl.BlockSpec((seq, PG), lambda b: (b, 0))])
        pool_outs.append(po)
    cw = 256
    ncb = SSM_CONV_DIM // cw
    cbase = (POOL_W + SSM_INNER) // cw
    conv_in_specs = [pl.BlockSpec((seq, cw), lambda j, b: (b, cbase + j)), pl.BlockSpec((SSM_CONV, cw), lambda j, b: (0, j)),
                     pl.BlockSpec((1, cw), lambda j, b: (0, j))]
    conv_out_spec = pl.BlockSpec((seq, cw), lambda j, b: (b, j))
    xbc_act, = fwd_call(conv4_fn, "ssm_conv", (ncb, bsz), [u0, conv_w, conv_b], conv_in_specs,
                        [_sd((t, SSM_CONV_DIM))], [conv_out_spec])
    dtb = jnp.pad(p['ssm_dt_bias'], ((0, 0), (0, LANE - SSM_HEADS)))
    alog = jnp.pad(p['ssm_a_log'], ((0, 0), (0, LANE - SSM_HEADS)))
    dsk = jnp.pad(p['ssm_d'], ((0, 0), (0, LANE - SSM_HEADS)))
    yn, hs = ssd_fwd(xbc_act, u0, dtb, alog, dsk, p['ssm_norm'], consts, bsz, seq)
    mix0 = jnp.concatenate(pool_outs + [yn], axis=1)
    big.update(ex.weights('l0a', yn))
    ex.start_gather('cd', after=[ex.done['l0a']])
    ex.start_gather('l1a', after=[ex.gathers['cd'][4]])
    ex.start_gather('l1b', after=[ex.gathers['l1a'][4]])
    m0, x1, h2 = matmul_res(mix0, big[('ab_w_out', 0)], "ab_out", x0, gain(0, 1), gain(0, 2))
    ao0, x2, h3 = attention_fwd(0, x1, h2, sv, gain(0, 3), gain(0, 4))
    big.update(ex.weights('l0b', h3))
    mo0, x3, h4 = mlp_fwd(0, h3, sv, (x2, gain(0, 5), gain(1, 0)))
    big.update(ex.weights('cd', mo0))

    sv1 = saved[1]
    nd = D // LANE
    w_cd_in = big[('cd_w_in', 0)].reshape(5, nd, LANE, D).transpose(1, 0, 2, 3).reshape(CD_IN, D)
    u1 = matmul(h4, w_cd_in, 'nt', "cd_in")
    cd_par = [pl.BlockSpec((CONF_K, LANE), lambda j, b: (0, j)), pl.BlockSpec((1, LANE), lambda j, b: (0, j)),
              pl.BlockSpec((SC_K, LANE), lambda j, b: (0, j))]
    cd_ins = [u1, p['conf_dw_w'][0], p['conf_dw_b'], p['sc_conv_w'][0]]
    cd_u_spec = pl.BlockSpec((seq, 5 * LANE), lambda j, b: (b, j))
    cd_in_specs = [cd_u_spec] + cd_par
    cd_out_spec = pl.BlockSpec((seq, LANE), lambda j, b: (b, j))
    vconv, mix1 = fwd_call(cd1_fn, "cd_conv", (nd, bsz), cd_ins, cd_in_specs, [_sd((t, D)), _sd((t, CD_OUT), BF)],
                           [cd_out_spec, pl.BlockSpec((seq, LANE), lambda j, b: (b, nd + j))])
    mix1, = fwd_call(seg_ln, "conf_ln", (nb,), [vconv, p['conf_ln_g'], p['conf_ln_b']], [_rows(D), _par(D), _par(D)],
                     [_sd((t, CD_OUT), BF)], [_rows(D)], into=mix1)
    m1, x4, h5 = matmul_res(mix1, big[('cd_w_out', 0)], "cd_out", x3, gain(1, 1), gain(1, 2))
    big.update(ex.weights('l1a', h5))
    ao1, x5, h6 = attention_fwd(1, x4, h5, sv1, gain(1, 3), gain(1, 4))
    big.update(ex.weights('l1b', h6))
    r1, rr1 = matmul(h6, big[('mlp_w1', 1)], 'nt', "mlp1_1", (BF, BF), epilogue=act_epilogue)
    sv1.update(r=r1, rr=rr1)
    dx5, dmo1, dg15, lanes = matmul(rr1, big[('mlp_w2', 1)], 'nn', "mlp2_1", (F32, BF), epilogue=loss_epilogue, extras=[x5, tgt],
                                    params=[gain(1, 5)], n_acc=2)
    loss = 0.5 * jnp.sum(lanes) / float(D)

    gain_grads = {(1, 5): dg15}

    def matmul_res_bwd(a, b, mode, name, xin, m, ga, gb, dx1):
        return list(matmul(a, b, mode, name, (F32, BF), epilogue=res_bwd_epilogue, extras=[xin, m, dx1], params=[ga, gb], n_acc=2))

    def mlp_bwd(layer, hin, dmo, sv, res):
        grads_w2 = matmul(sv['rr'], dmo, 'tn', f"d_mlp_w2_{layer}", BF)
        dr, = matmul(dmo, big[('mlp_w2', layer)], 'nt', f"d_r_{layer}", (BF,), epilogue=act_bwd_epilogue, extras=[sv['r']])
        grads_w1 = matmul(dr, hin, 'tn', f"d_mlp_w1_{layer}", BF)
        return matmul_res_bwd(dr, big[('mlp_w1', layer)], 'nn', f"d_h_mlp_{layer}", *res) + [grads_w1, grads_w2]

    def attention_bwd(layer, hin, dao, sv, res):
        g_wo = matmul(sv['o'], dao, 'tn', f"d_xa_wo_{layer}", BF)
        do = matmul(dao, big[('xa_wo', layer)], 'nt', f"d_o_{layer}", BF)
        grid, qs, kvs = attn_specs()
        dq, dkv = bwd_call(attn_fn, f"d_attn_{layer}", grid, [sv['q'], sv['kv']], [qs, kvs], [do], [qs], [0, 1],
                           [_sd((t, D), BF), _sd((bsz * N_MEM, 2 * D))], [qs, kvs], [None, (1,)])
        g_wkv = matmul(dkv, mem2, 'tn', f"d_xa_wkv_{layer}", BF)
        g_wq = matmul(hin, dq, 'tn', f"d_xa_wq_{layer}", BF)
        return matmul_res_bwd(dq, big[('xa_wq', layer)], 'nt', f"d_h_attn_{layer}", *res) + [g_wq, g_wkv, g_wo]

    per_layer = {k: [None, None] for k in ('xa_wq', 'xa_wkv', 'xa_wo', 'mlp_w1', 'mlp_w2')}

    (dx4, dao1, gain_grads[(1, 3)], gain_grads[(1, 4)], per_layer['mlp_w1'][1],
     per_layer['mlp_w2'][1]) = mlp_bwd(1, h6, dmo1, sv1, (x4, ao1, gain(1, 3), gain(1, 4), dx5))
    (dx3, dm1, gain_grads[(1, 1)], gain_grads[(1, 2)], per_layer['xa_wq'][1], per_layer['xa_wkv'][1],
     per_layer['xa_wo'][1]) = attention_bwd(1, h5, dao1, sv1, (x3, m1, gain(1, 1), gain(1, 2), dx4))
    ex.put_grads('l1', G_L1, {(k, 1): v[1] for k, v in per_layer.items()})
    g_cd_out = matmul(mix1, dm1, 'tn', "d_cd_w_out", BF)
    dmix1 = matmul(dm1, big[('cd_w_out', 0)], 'nt', "d_mix1", after=ex.take_tokens())
    dvconv, dlg, dlb = bwd_call(seg_ln, "d_conf_ln", (nb,), [vconv, p['conf_ln_g'], p['conf_ln_b']],
                                [_rows(D), _par(D), _par(D)], [dmix1], [_rows(D, 0)], [0, 1, 2],
                                [_sd((t, D)), _sd((1, D)), _sd((1, D))], [_rows(D), _par(D), _par(D)], [None, (0,), (0,)])
    grads['conf_ln_g'], grads['conf_ln_b'] = dlg, dlb
    cd_g = bwd_call(cd1_fn, "d_cd_conv", (nd, bsz), cd_ins, cd_in_specs, [dvconv, dmix1],
                    [cd_out_spec, pl.BlockSpec((seq, LANE), lambda j, b: (b, nd + j))], list(range(4)),
                    [_sd((t, CD_IN), BF), _sd((CONF_K, D)), _sd((1, D)), _sd((SC_K, D))], [cd_u_spec] + cd_par,
                    [None, (1,), (1,), (1,)])
    du1 = cd_g[0]
    grads['conf_dw_w'], grads['conf_dw_b'], grads['sc_conv_w'] = cd_g[1][None], cd_g[2], cd_g[3][None]
    g_cd_in = matmul(du1, h4, 'tn', "d_cd_w_in", BF).reshape(nd, 5, LANE, D).transpose(1, 0, 2, 3).reshape(CD_IN, D)
    ex.put_grads('cd', G_CD, {('cd_w_in', 0): g_cd_in, ('cd_w_out', 0): g_cd_out})
    dx2, dmo0, gain_grads[(0, 5)], gain_grads[(1, 0)] = matmul_res_bwd(du1, w_cd_in, 'nn', "d_h_cd", x2, mo0, gain(0, 5),
                                                                       gain(1, 0), dx3)
    (dx1, dao0, gain_grads[(0, 3)], gain_grads[(0, 4)], per_layer['mlp_w1'][0],
     per_layer['mlp_w2'][0]) = mlp_bwd(0, h3, dmo0, sv, (x1, ao0, gain(0, 3), gain(0, 4), dx2))
    (dx0r, dm0, gain_grads[(0, 1)], gain_grads[(0, 2)], per_layer['xa_wq'][0], per_layer['xa_wkv'][0],
     per_layer['xa_wo'][0]) = attention_bwd(0, h2, dao0, sv, (x0, m0, gain(0, 1), gain(0, 2), dx1))
    ex.put_grads('l0', G_L0, {(k, 0): v[0] for k, v in per_layer.items()})
    g_ab_out = matmul(mix0, dm0, 'tn', "d_ab_w_out", BF)
    dmix0 = matmul(dm0, big[('ab_w_out', 0)], 'nt', "d_mix0", after=ex.take_tokens())
    dxbc_act, dz, ddt, ddtb, dalog, ddsk, dnw = ssd_bwd(xbc_act, u0, dtb, alog, dsk, p['ssm_norm'], consts, hs, dmix0, bsz, seq)
    grads['ssm_dt_bias'] = ddtb[:, :SSM_HEADS]
    grads['ssm_a_log'] = dalog[:, :SSM_HEADS]
    grads['ssm_d'] = ddsk[:, :SSM_HEADS]
    grads['ssm_norm'] = dnw
    dxr, dcw, dcb = bwd_call(conv4_fn, "d_ssm_conv", (ncb, bsz), [u0, conv_w, conv_b], conv_in_specs,
                             [dxbc_act], [conv_out_spec], [0, 1, 2],
                             [_sd((t, SSM_CONV_DIM), BF), _sd((SSM_CONV, SSM_CONV_DIM)), _sd((1, SSM_CONV_DIM))],
                             [conv_out_spec, conv_in_specs[1], conv_in_specs[2]], [None, (1,), (1,)])
    grads['ssm_conv_w'], grads['ssm_conv_b'] = _xbc_ungroup(dcw, 1)[None], _xbc_ungroup(dcb, 1)
    dpool, dpw, dps = [], [], []
    for g in range(POOL_GROUPS):
        seqspec = pl.BlockSpec((seq, PG), lambda b, g=g: (b, g))
        one = pl.BlockSpec((seq, PG), lambda b: (b, 0))
        wspec = pl.BlockSpec((PG, PG), lambda b: (0, 0))
        sspec = pl.BlockSpec((1, PG), lambda b, g=g: (0, g))
        a, bb, c = bwd_call(make_pool_fn(g), f"d_pool_{g}", (bsz,), [u0, p['pool_w'][0, g], p['pool_scale']],
                            [seqspec, wspec, sspec], [dmix0], [seqspec], [0, 1, 2],
                            [_sd((t, PG), BF), _sd((PG, PG)), _sd((1, PG))], [one, wspec, pl.BlockSpec((1, PG), lambda b: (0, 0))],
                            [None, (0,), (0,)])
        dpool.append(a)
        dpw.append(bb)
        dps.append(c)
    grads['pool_w'] = jnp.stack(dpw)[None]
    grads['pool_scale'] = jnp.concatenate(dps, axis=1)
    du0 = jnp.concatenate(dpool + [dz, dxr, ddt.astype(BF)], axis=1)
    g_ab_in = matmul(du0, h0, 'tn', "d_ab_w_in", BF)
    g_ab_in = jnp.concatenate([g_ab_in[:xbc0], _xbc_ungroup(g_ab_in[xbc0:xbc0 + SSM_CONV_DIM], 0),
                               g_ab_in[xbc0 + SSM_CONV_DIM:AB_IN]], axis=0)
    ex.put_grads('ab', G_AB, {('ab_w_in', 0): g_ab_in, ('ab_w_out', 0): g_ab_out})
    dx, dg00 = matmul(du0, w_ab_in, 'nn', "d_h_ab", (F32,), epilogue=in_bwd_epilogue, extras=[x0, dx0r], params=[gain(0, 0)],
                      after=ex.take_tokens(), n_acc=1)
    gain_grads[(0, 0)] = dg00
    grads['norm_gains'] = jnp.stack([jnp.concatenate([gain_grads[(l, i)] for i in range(6)], axis=0) for l in range(2)])
    return loss, dx, grads
```

```python
import functools
import math

import numpy as np
import jax
import jax.numpy as jnp
from jax import lax
from jax.experimental import pallas as pl
from jax.experimental.pallas import tpu as pltpu

BF = jnp.bfloat16
F32 = jnp.float32

N_DEV = 8
D = 1024
N_MEM = 256
XA_HEADS = 4
XA_DH = D // XA_HEADS
POOL_GROUPS = 4
PG = 128
POOL_W = POOL_GROUPS * PG
SSM_INNER = 1024
SSM_GROUPS = 2
SSM_GSZ = SSM_INNER // SSM_GROUPS
SSM_HEADS = 16
SSM_P = 64
SSM_N = 128
SSM_CONV = 4
SSM_CONV_DIM = SSM_INNER + 2 * SSM_GROUPS * SSM_N
SSM_XBC_G = SSM_GSZ + 2 * SSM_N
CHUNK = 128
AB_IN = POOL_W + SSM_INNER + SSM_CONV_DIM + SSM_HEADS
AB_IN_PAD = POOL_W + SSM_INNER + SSM_CONV_DIM + 128
AB_OUT = POOL_W + SSM_INNER
CONF_K = 31
SC_K = 3
CD_IN = 5 * D
CD_OUT = 2 * D
MLP_H = 4 * D
RMS_EPS = 1e-6
LN_EPS = 1e-5
ADAM_LR = 0.001
ADAM_B1 = 0.9
ADAM_B2 = 0.999
ADAM_EPS = 1e-08
ADAM_WD = 0.01
ADAM_STEP = 10
VMEM_LIMIT = 56 * 1024 * 1024
LANE = 128

NAMES = ['x', 'mem', 'norm_gains', 'xa_wq', 'xa_wkv', 'xa_wo', 'mlp_w1', 'mlp_w2', 'ab_w_in', 'pool_w', 'pool_scale',
         'ssm_conv_w', 'ssm_conv_b', 'ssm_dt_bias', 'ssm_a_log', 'ssm_d', 'ssm_norm', 'ab_w_out', 'cd_w_in', 'conf_dw_w',
         'conf_dw_b', 'conf_ln_g', 'conf_ln_b', 'sc_conv_w', 'cd_w_out', 'loss_target']
WEIGHTS = NAMES[2:25]
BIG = [('xa_wq', 1), ('xa_wkv', 2), ('xa_wo', 1), ('mlp_w1', 2), ('mlp_w2', 1), ('cd_w_in', 2), ('cd_w_out', 1),
       ('ab_w_out', 1), ('ab_w_in', 2)]
SMALL_SHARDED = ['norm_gains', 'ssm_conv_w', 'conf_dw_w', 'conf_dw_b', 'conf_ln_g', 'conf_ln_b', 'sc_conv_w']
REPLICATED = ['pool_w', 'pool_scale', 'ssm_conv_b', 'ssm_dt_bias', 'ssm_a_log', 'ssm_d', 'ssm_norm']


def _dg(a, b, ca, cb, prec=None):
    return lax.dot_general(a, b, (((ca,), (cb,)), ((), ())), precision=prec, preferred_element_type=F32)


@functools.partial(jax.custom_vjp, nondiff_argnums=(2, 3))
def bdot(a, b, ca, cb):
    return _dg(a.astype(BF), b.astype(BF), ca, cb)


def _bdot_fwd(a, b, ca, cb):
    return bdot(a, b, ca, cb), (a, b)


def _bdot_bwd(ca, cb, res, g):
    a, b = res
    g16, a16, b16 = g.astype(BF), a.astype(BF), b.astype(BF)
    da = _dg(g16, b16, 1, 1 - cb) if ca == 1 else _dg(b16, g16, 1 - cb, 1)
    db = _dg(g16, a16, 0, 1 - ca) if cb == 1 else _dg(a16, g16, 1 - ca, 0)
    return da.astype(a.dtype), db.astype(b.dtype)


bdot.defvjp(_bdot_fwd, _bdot_bwd)


def _split3(a):
    a1 = a.astype(BF)
    r1 = a - a1.astype(F32)
    a2 = r1.astype(BF)
    a3 = (r1 - a2.astype(F32)).astype(BF)
    return a1, a2, a3


def _exact_right(a, c):
    m = a.shape[0]
    if m % 16:
        return sum(_dg(p, c, 1, 0) for p in _split3(a))
    o = _dg(jnp.concatenate(_split3(a), axis=0), c, 1, 0)
    return o[:m] + o[m:2 * m] + o[2 * m:]


def _exact_left(c, a):
    n = a.shape[1]
    o = _dg(c, jnp.concatenate(_split3(a), axis=1), 1, 0)
    return o[:, :n] + o[:, n:2 * n] + o[:, 2 * n:]


@jax.custom_vjp
def cmat(a, c, ct):
    return _exact_right(a, c)


def _cmat_fwd(a, c, ct):
    return cmat(a, c, ct), (c, ct)


def _cmat_bwd(res, g):
    c, ct = res
    return _exact_right(g, ct), jnp.zeros_like(c), jnp.zeros_like(ct)


cmat.defvjp(_cmat_fwd, _cmat_bwd)


@jax.custom_vjp
def cmatl(c, ct, a):
    return _exact_left(c, a)


def _cmatl_fwd(c, ct, a):
    return cmatl(c, ct, a), (c, ct)


def _cmatl_bwd(res, g):
    c, ct = res
    return jnp.zeros_like(c), jnp.zeros_like(ct), _exact_left(ct, g)


cmatl.defvjp(_cmatl_fwd, _cmatl_bwd)


SUBLANES = 8


def _taps(x, shifts, down):
    n, c = x.shape
    pad = _round_up(max(shifts), SUBLANES)
    if pad == 0:
        return {0: x}
    zeros = jnp.zeros((pad, c), x.dtype)
    xp = jnp.concatenate([zeros, x] if down else [x, zeros], axis=0)
    rolled, out = {0: xp}, {}
    for s in shifts:
        a, b = divmod(s, SUBLANES)
        if b not in rolled:
            rolled[b] = pltpu.roll(xp, b if down else n + pad - b, 0)
        off = pad - SUBLANES * a if down else SUBLANES * a
        out[s] = rolled[b][off:off + n]
    return out


def _shift_down(x, k):
    return _taps(x, [k], True)[k]


def _shift_up(x, k):
    return _taps(x, [k], False)[k]


@functools.partial(jax.custom_vjp, nondiff_argnums=(1,))
def shift(x, k):
    return _shift_down(x, k)


def _shift_fwd(x, k):
    return _shift_down(x, k), None


def _shift_bwd(k, _, g):
    return (_shift_up(g, k),)


shift.defvjp(_shift_fwd, _shift_bwd)


@functools.partial(jax.custom_vjp, nondiff_argnums=(2,))
def cconv(u, w, width):
    taps = _taps(u, list(range(width)), True)
    acc = u * w[width - 1:width, :]
    for k in range(width - 1):
        acc = acc + taps[width - 1 - k] * w[k:k + 1, :]
    return acc


def _cconv_fwd(u, w, width):
    return cconv(u, w, width), (u, w)


def _cconv_bwd(width, res, g):
    u, w = res
    rows = lax.broadcasted_iota(jnp.int32, w.shape, 0)
    du = g * w[width - 1:width, :]
    dw = jnp.where(rows == width - 1, jnp.sum(g * u, axis=0, keepdims=True), 0.0)
    g_taps = _taps(g, list(range(width)), False)
    u_taps = _taps(u, list(range(width)), True)
    for k in range(width - 1):
        s = width - 1 - k
        du = du + g_taps[s] * w[k:k + 1, :]
        dw = dw + jnp.where(rows == k, jnp.sum(g * u_taps[s], axis=0, keepdims=True), 0.0)
    return du, dw


cconv.defvjp(_cconv_fwd, _cconv_bwd)


def _rms(x, g):
    return x * lax.rsqrt(jnp.mean(x * x, axis=-1, keepdims=True) + RMS_EPS) * g


def _rms_bwd(v, g, dout):
    r = lax.rsqrt(jnp.mean(v * v, axis=-1, keepdims=True) + RMS_EPS)
    n = v * r
    dn = dout * g
    dv = (dn - n * jnp.mean(dn * n, axis=-1, keepdims=True)) * r
    return dv, jnp.sum(dout * n, axis=0, keepdims=True)


def _params(sem=None):
    return pltpu.CompilerParams(dimension_semantics=sem, vmem_limit_bytes=VMEM_LIMIT)


def _f32(v):
    return v if v.dtype == F32 else v.astype(F32)


def _first(axes):
    ok = None
    for ax in axes:
        c = pl.program_id(ax) == 0
        ok = c if ok is None else jnp.logical_and(ok, c)
    return ok


def fwd_call(fn, name, grid, ins, in_specs, out_shapes, out_specs, into=None):
    n_in = len(ins)
    n_into = 0 if into is None else 1

    def body(*refs):
        outs = fn(*[_f32(r[...]) for r in refs[:n_in]])
        for r, o in zip(refs[n_in + n_into:], outs):
            r[...] = o.astype(r.dtype)

    extra = [] if into is None else [into]
    return pl.pallas_call(body, name=name, grid=grid, in_specs=list(in_specs) + [pl.BlockSpec(memory_space=pl.ANY)] * n_into,
                          out_specs=out_specs, out_shape=out_shapes, input_output_aliases={n_in: 0} if n_into else {},
                          compiler_params=_params())(*ins, *extra)


def bwd_call(fn, name, grid, ins, in_specs, cots, cot_specs, gidx, g_shapes, g_specs, g_acc):
    n_in, n_cot = len(ins), len(cots)

    def body(*refs):
        vals = [_f32(r[...]) for r in refs[:n_in]]

        def f_sel(*dv):
            full = list(vals)
            for i, v in zip(gidx, dv):
                full[i] = v
            return tuple(fn(*full))

        outs, vjp = jax.vjp(f_sel, *[vals[i] for i in gidx])
        cts = tuple(_f32(r[...]) for r in refs[n_in:n_in + n_cot])
        grads = vjp(cts)
        for r, g, acc in zip(refs[n_in + n_cot:], grads, g_acc):
            if acc is None:
                r[...] = g.astype(r.dtype)
            else:
                @pl.when(_first(acc))
                def _():
                    r[...] = jnp.zeros_like(r)

                r[...] += g.astype(r.dtype)

    return pl.pallas_call(body, name=name, grid=grid, in_specs=list(in_specs) + list(cot_specs), out_specs=g_specs,
                          out_shape=g_shapes, compiler_params=_params())(*ins, *cots)


def _tile(dim, pref):
    if dim <= pref:
        return dim
    best = None
    for t in range(LANE, pref + 1, LANE):
        if dim % t == 0:
            best = t
    assert best is not None, dim
    return best


MATMUL_VMEM_BUDGET = 40 * 1024 * 1024


def _matmul_tiles(m, n, k, a_bytes, b_bytes, out_bytes):
    tn = _tile(n, 1024)
    for tk_pref in (k, 2048, 1024, 512):
        tk = _tile(k, tk_pref)
        for tm_pref in (1024, 512, 256):
            tm = _tile(m, tm_pref)
            need = 2 * (tm * tk * a_bytes + tk * tn * b_bytes + tm * tn * out_bytes) + (0 if tk == k else tm * tn * 4)
            need += (tm * tk * 2 if a_bytes == 4 else 0) + (tk * tn * 2 if b_bytes == 4 else 0)
            if need <= MATMUL_VMEM_BUDGET:
                return tm, tn, tk
    raise ValueError((m, n, k))


def matmul(a, b, mode, name, out_dtype=F32, epilogue=None, extras=(), params=(), after=(), n_acc=0):
    if mode == 'nn':
        (m, k), (k2, n) = a.shape, b.shape
    elif mode == 'nt':
        (m, k), (n, k2) = a.shape, b.shape
    else:
        (k, m), (k2, n) = a.shape, b.shape
    assert k == k2, (name, a.shape, b.shape)
    n_extra = len(extras) + len(params)
    out_dtypes = out_dtype if isinstance(out_dtype, tuple) else (out_dtype,)
    per_out = sum(jnp.dtype(dt).itemsize for dt in out_dtypes) + sum(e.dtype.itemsize for e in extras)
    tm, tn, tk = _matmul_tiles(m, n, k, a.dtype.itemsize, b.dtype.itemsize, per_out)
    nk = k // tk
    ca = 0 if mode == 'tn' else 1
    cb = 1 if mode == 'nt' else 0
    a_spec = pl.BlockSpec((tk, tm), lambda i, j, kk: (kk, i)) if mode == 'tn' else pl.BlockSpec((tm, tk), lambda i, j, kk: (i, kk))
    b_spec = pl.BlockSpec((tn, tk), lambda i, j, kk: (j, kk)) if mode == 'nt' else pl.BlockSpec((tk, tn), lambda i, j, kk: (kk, j))

    def finish(o_refs, extra_refs, acc, first_row_tile):
        outs = (acc,) if epilogue is None else epilogue(acc, *[_f32(e[...]) for e in extra_refs])
        n_tile = len(o_refs) - n_acc
        for o_ref, o in zip(o_refs[:n_tile], outs[:n_tile]):
            o_ref[...] = o.astype(o_ref.dtype)
        for o_ref, o in zip(o_refs[n_tile:], outs[n_tile:]):
            o_ref[...] = jnp.where(first_row_tile, o, o_ref[...] + o)

    n_after = len(after)

    def body_whole_k(a_ref, b_ref, *refs):
        refs = refs[n_after:]
        finish(refs[n_extra:], refs[:n_extra], _dg(a_ref[...].astype(BF), b_ref[...].astype(BF), ca, cb), pl.program_id(0) == 0)

    def body_split_k(a_ref, b_ref, *refs):
        refs = refs[n_after:]
        extra_refs, o_refs, acc = refs[:n_extra], refs[n_extra:-1], refs[-1]
        kk = pl.program_id(2)
        first_row_tile = pl.program_id(0) == 0

        @pl.when(kk == 0)
        def _():
            acc[...] = jnp.zeros_like(acc)

        acc[...] += _dg(a_ref[...].astype(BF), b_ref[...].astype(BF), ca, cb)

        @pl.when(kk == nk - 1)
        def _():
            finish(o_refs, extra_refs, acc[...], first_row_tile)

    tile = pl.BlockSpec((tm, tn), lambda i, j, kk: (i, j))
    row = pl.BlockSpec((1, tn), lambda i, j, kk: (0, j))
    n_par = len(params)
    outs = pl.pallas_call(
        body_whole_k if nk == 1 else body_split_k, name=name, grid=(m // tm, n // tn, nk),
        in_specs=[a_spec, b_spec] + [pl.BlockSpec(memory_space=pl.ANY)] * n_after + [tile] * len(extras) + [row] * n_par,
        out_specs=[tile] * len(out_dtypes) + [row] * n_acc,
        out_shape=[jax.ShapeDtypeStruct((m, n), dt) for dt in out_dtypes] + [jax.ShapeDtypeStruct((1, n), F32)] * n_acc,
        scratch_shapes=[] if nk == 1 else [pltpu.VMEM((tm, tn), F32)],
        compiler_params=_params(("arbitrary",) * 3 if n_acc else ("parallel", "parallel", "arbitrary")))(a, b, *after, *extras, *params)
    return outs if isinstance(out_dtype, tuple) or n_acc else outs[0]


_FLIPS = [(0, 0, 1), (1, 0, 0), (0, 1, 0), (1, 1, 0), (1, 0, 1), (0, 1, 1), (1, 1, 1)]


def _me():
    return lax.axis_index("x"), lax.axis_index("y"), lax.axis_index("c")


def _flip(pos, f):
    return tuple(jnp.where(fi == 1, 1 - p, p) if fi else p for p, fi in zip(pos, f))


def _slot(pos):
    return 4 * pos[0] + 2 * pos[1] + pos[2]


def all_gather(v, name):
    def body(v_ref, out_ref, send_sems, recv_sems, local_sem):
        me = _me()
        sibling = _flip(me, (0, 0, 1))
        chips = [_flip(me, f) for f in ((1, 0, 0), (0, 1, 0), (1, 1, 0))]

        def copy(k, block, to, src=None):
            return pltpu.make_async_remote_copy(
                src_ref=out_ref.at[_slot(block)] if src is None else src, dst_ref=out_ref.at[_slot(block)],
                send_sem=send_sems.at[k], recv_sem=recv_sems.at[k], device_id=to, device_id_type=pl.DeviceIdType.MESH)

        mine = pltpu.make_async_copy(v_ref, out_ref.at[_slot(me)], local_sem)
        mine.start()
        first = [copy(0, me, sibling, src=v_ref)] + [copy(1 + j, me, chip, src=v_ref) for j, chip in enumerate(chips)]
        for cp in first:
            cp.start()
        passed = [copy(4 + j, chip, sibling) for j, chip in enumerate(chips)]
        for j, chip in enumerate(chips):
            copy(1 + j, chip, me).wait_recv()
            passed[j].start()
        copy(0, sibling, me).wait_recv()
        for j, chip in enumerate(chips):
            copy(4 + j, _flip(chip, (0, 0, 1)), me).wait_recv()
        for cp in first + passed:
            cp.wait_send()
        mine.wait()

    return pl.pallas_call(
        body, name=name, out_shape=jax.ShapeDtypeStruct((N_DEV,) + v.shape, v.dtype),
        in_specs=[pl.BlockSpec(memory_space=pl.ANY)], out_specs=pl.BlockSpec(memory_space=pl.ANY),
        scratch_shapes=[pltpu.SemaphoreType.DMA((7,)), pltpu.SemaphoreType.DMA((7,)), pltpu.SemaphoreType.DMA(())],
    )(v)


def sum_slots(v, name, tr=256):
    _, r, c = v.shape
    tr = _tile_rows(r, tr)

    def body(v_ref, o_ref):
        acc = v_ref[0].astype(F32)
        for s in range(1, N_DEV):
            acc = acc + v_ref[s].astype(F32)
        o_ref[...] = acc

    return pl.pallas_call(body, name=name, grid=(r // tr,), in_specs=[pl.BlockSpec((N_DEV, tr, c), lambda i: (0, i, 0))],
                          out_specs=pl.BlockSpec((tr, c), lambda i: (i, 0)), out_shape=jax.ShapeDtypeStruct((r, c), F32),
                          compiler_params=_params())(v)


def _tile_rows(r, pref):
    if r <= pref:
        return r
    best = None
    for t in range(8, pref + 1, 8):
        if r % t == 0:
            best = t
    return r if best is None else best


def _adamw_math(w, m, v, g):
    nm = ADAM_B1 * m + (1.0 - ADAM_B1) * g
    nv = ADAM_B2 * v + (1.0 - ADAM_B2) * jnp.square(g)
    m_hat = nm / (1.0 - ADAM_B1 ** ADAM_STEP)
    v_hat = nv / (1.0 - ADAM_B2 ** ADAM_STEP)
    return -ADAM_LR * (m_hat / (jnp.sqrt(v_hat) + ADAM_EPS) + ADAM_WD * w), nm, nv


def update_from_slots(lands, offs, w, m, v, transposed, name):
    layers, a, b = w.shape
    n_land = len(lands)
    if transposed:
        rb, tk = LANE, 512
        assert a % tk == 0 and b % rb == 0 and all(o % rb == 0 for o in offs), (name, w.shape, offs)
        grid = (layers, a // tk, b // rb)
        land_block = (N_DEV, rb, tk)
        tile = pl.BlockSpec((None, tk, rb), lambda l, i, j: (l, i, j))

        def land_spec(layer):
            base = offs[layer] // rb
            return pl.BlockSpec(land_block, lambda l, i, j: (0, base + jnp.where(l == layer, j, 0), jnp.where(l == layer, i, 0)))
    else:
        fits = [t for t in (256, 128, 64) if a % t == 0 and all(o % t == 0 for o in offs)]
        assert fits or all(o == 0 for o in offs), (name, w.shape, offs)
        tr = max(fits) if fits else a
        grid = (layers, a // tr)
        land_block = (N_DEV, _round_up(tr, MEMBER_ROW_TILE), b)
        tile = pl.BlockSpec((None, tr, b), lambda l, i: (l, i, 0))

        def land_spec(layer):
            base = offs[layer] // tr
            return pl.BlockSpec(land_block, lambda l, i: (0, base + jnp.where(l == layer, i, 0), 0))

    def body(*refs):
        land_refs, (w_ref, m_ref, v_ref, g_ref, d_ref, nm_ref, nv_ref, acc) = refs[:n_land], refs[n_land:]
        for layer, land in enumerate(land_refs):
            @pl.when(pl.program_id(0) == layer)
            def _(land=land):
                rows = acc.shape[0]
                s = land[0, :rows].astype(F32)
                for k in range(1, N_DEV):
                    s = s + land[k, :rows].astype(F32)
                acc[...] = s

        g = acc[...].T if transposed else acc[...]
        d, nm, nv = _adamw_math(w_ref[...], m_ref[...], v_ref[...], g)
        g_ref[...] = g
        d_ref[...] = d
        nm_ref[...] = nm
        nv_ref[...] = nv

    sh = jax.ShapeDtypeStruct(w.shape, F32)
    return pl.pallas_call(
        body, name=name, grid=grid, in_specs=[land_spec(layer) for layer in range(n_land)] + [tile] * 3, out_specs=[tile] * 4,
        out_shape=[sh] * 4, scratch_shapes=[pltpu.VMEM((rb, tk) if transposed else (tr, b), F32)],
        compiler_params=_params())(*lands, w, m, v)


def adamw_many(ws, ms, vs, gs, name):
    n = len(ws)

    def body(*refs):
        for i in range(n):
            d, nm, nv = _adamw_math(refs[i][...], refs[n + i][...], refs[2 * n + i][...], refs[3 * n + i][...])
            refs[4 * n + i][...] = d
            refs[5 * n + i][...] = nm
            refs[6 * n + i][...] = nv

    vmem = pl.BlockSpec(memory_space=pltpu.VMEM)
    shapes = [jax.ShapeDtypeStruct(a.shape, F32) for a in ws]
    res = pl.pallas_call(body, name=name, in_specs=[vmem] * (4 * n), out_specs=[vmem] * (3 * n), out_shape=shapes * 3,
                         compiler_params=_params())(*ws, *ms, *vs, *gs)
    return res[:n], res[n:2 * n], res[2 * n:]


def seg_in(x, g):
    return (_rms(x, g),)


def seg_res(x, m, ga, gb):
    x1 = x + _rms(m, ga)
    return x1, _rms(x1, gb)


def act_epilogue(r):
    t = jnp.maximum(r, 0.0)
    return r, t * t


def res_epilogue(m, x, ga, gb):
    x1, h = seg_res(x, m, ga, gb)
    return m, x1, h


def res_bwd_epilogue(dh, x, m, dx1, ga, gb):
    x1 = x + _rms(m, ga)
    d1, dgb = _rms_bwd(x1, gb, dh)
    dx = dx1 + d1
    dm, dga = _rms_bwd(m, ga, dx)
    return dx, dm, dga, dgb


def in_bwd_epilogue(dh, x, dx_res, g):
    d, dg = _rms_bwd(x, g, dh)
    return dx_res + d, dg


def loss_epilogue(mo, x, target, g):
    d = x + _rms(mo, g) - target
    dy = d / float(D)
    dm, dg = _rms_bwd(mo, g, dy)
    return dy, dm, dg, jnp.sum(d * d, axis=0, keepdims=True)


def act_bwd_epilogue(drr, r):
    return (drr * (2.0 * jnp.maximum(r, 0.0)),)


def seg_ln(v, g, b):
    mu = jnp.mean(v, axis=-1, keepdims=True)
    var = jnp.mean(jnp.square(v - mu), axis=-1, keepdims=True)
    vn = (v - mu) * lax.rsqrt(var + LN_EPS) * g + b
    return (jax.nn.silu(vn),)


def make_pool_fn(group):
    window = 2 ** (group + 1)

    def pool_fn(ug, pw, scale):
        s = ug
        for lvl in range(group + 1):
            s = s + shift(s, 2 ** lvl)
        cnt = jnp.minimum(lax.broadcasted_iota(jnp.int32, ug.shape, 0) + 1, window).astype(F32)
        return (bdot(s / cnt - ug, pw, 1, 0) * scale,)

    return pool_fn


def conv4_fn(xr, w, b):
    return (jax.nn.silu(cconv(xr, w, SSM_CONV) + b),)


def cd1_fn(u, dww, dwb, scw):
    val, gate, bg, cg, hh = (u[:, k * LANE:(k + 1) * LANE] for k in range(5))
    v = val * jax.nn.sigmoid(gate)
    vc = cconv(v, dww, CONF_K) + dwb
    sc = bg * cconv(cg * hh, scw, SC_K)
    return vc, sc


def attn_fn(q, kv):
    outs = []
    for h in range(XA_HEADS):
        cols = slice(h * XA_DH, (h + 1) * XA_DH)
        s = bdot(q[:, cols], kv[:, cols], 1, 1) / math.sqrt(XA_DH)
        p = jax.nn.softmax(s, axis=-1)
        outs.append(bdot(p, kv[:, D + h * XA_DH:D + (h + 1) * XA_DH], 1, 0))
    return (jnp.concatenate(outs, axis=1),)


def ssd_chunk(xbc, z, dtraw, dtb, alog, dsk, nw, h0, h1, h2, h3, e64, e64t, ecat, ecatt, tril, trilt):
    xs, bm, cm = xbc[:, :SSM_GSZ], xbc[:, SSM_GSZ:SSM_GSZ + SSM_N], xbc[:, SSM_GSZ + SSM_N:]
    hin = (h0, h1, h2, h3)
    dt = jax.nn.softplus(dtraw + dtb)
    a = -jnp.exp(alog)
    d_a = dt * a
    cs = cmatl(tril, trilt, d_a)
    cs_cat = cmat(cs, ecat, ecatt)
    cs64, cs128 = cs_cat[:, :SSM_GSZ], cs_cat[:, SSM_GSZ:]
    dt64 = cmat(dt, e64, e64t)
    row = lax.broadcasted_iota(jnp.int32, (8, LANE), 0)
    heads = jnp.where(row == 0, dsk, jnp.where(row == 1, jnp.sum(d_a, axis=0, keepdims=True), 0.0))
    heads64 = cmat(heads, e64, e64t)
    d64, tot64 = heads64[0:1, :], heads64[1:2, :]
    xdt = xs * dt64
    cb = bdot(cm, bm, 1, 1)
    li = lax.broadcasted_iota(jnp.int32, (CHUNK, CHUNK), 0)
    si = lax.broadcasted_iota(jnp.int32, (CHUNK, CHUNK), 1)
    causal = li >= si
    lane = lax.broadcasted_iota(jnp.int32, (CHUNK, LANE), 1)
    xw = xdt * jnp.exp(tot64 - cs64)
    ecs = jnp.exp(cs64)
    etot = jnp.exp(tot64)
    ycols, hout = [], []
    for j in range(4):
        sl = slice(j * LANE, (j + 1) * LANE)
        xj = xdt[:, sl]
        ys = []
        for hh in range(2):
            r = 2 * j + hh
            col = cs128[:, r * LANE:(r + 1) * LANE]
            decay = jnp.exp(jnp.where(causal, col - col.T, -1e30))
            ys.append(bdot(cb * decay, xj, 1, 0))
        y_diag = jnp.where(lane < SSM_P, ys[0], ys[1])
        y_off = bdot(cm, hin[j], 1, 0) * ecs[:, sl]
        ycols.append(y_diag + y_off)
        hout.append(etot[:, sl] * hin[j] + bdot(bm, xw[:, sl], 0, 0))
    y = jnp.concatenate(ycols, axis=1) + d64 * xs
    y = y * jax.nn.silu(z)
    yn = y * lax.rsqrt(jnp.mean(y * y, axis=-1, keepdims=True) + RMS_EPS) * nw
    return (yn,) + tuple(hout)


def _xbc_group(a, axis):
    parts = []
    for g in range(SSM_GROUPS):
        for start, width in ((g * SSM_GSZ, SSM_GSZ), (SSM_INNER + g * SSM_N, SSM_N), (SSM_INNER + (SSM_GROUPS + g) * SSM_N, SSM_N)):
            parts.append(lax.slice_in_dim(a, start, start + width, axis=axis))
    return jnp.concatenate(parts, axis=axis)


def _xbc_ungroup(a, axis):
    xs, bs, cs = [], [], []
    for g in range(SSM_GROUPS):
        base = g * SSM_XBC_G
        xs.append(lax.slice_in_dim(a, base, base + SSM_GSZ, axis=axis))
        bs.append(lax.slice_in_dim(a, base + SSM_GSZ, base + SSM_GSZ + SSM_N, axis=axis))
        cs.append(lax.slice_in_dim(a, base + SSM_GSZ + SSM_N, base + SSM_XBC_G, axis=axis))
    return jnp.concatenate(xs + bs + cs, axis=axis)


def _ssd_consts():
    h = np.arange(LANE)[:, None]
    e64 = np.stack([(h == g * 8 + np.arange(SSM_GSZ)[None, :] // SSM_P) for g in range(SSM_GROUPS)]).astype(np.float32)
    e128 = np.stack([(h == g * 8 + np.arange(8 * LANE)[None, :] // LANE) for g in range(SSM_GROUPS)]).astype(np.float32)
    ecat = np.concatenate([e64, e128], axis=2)
    tril = np.tril(np.ones((CHUNK, CHUNK), np.float32))
    return tuple(jnp.asarray(c, dtype=BF) for c in (e64, e64.transpose(0, 2, 1), ecat, ecat.transpose(0, 2, 1), tril, tril.T))


def _ssd_specs(nc, rev):
    def ci(c):
        return nc - 1 - c if rev else c

    def row(width, col):
        return pl.BlockSpec((CHUNK, width), lambda b, c: (b * nc + ci(c), col))

    def whole(shape):
        return pl.BlockSpec(shape, lambda b, c: (0,) * len(shape))

    data = [row(SSM_CONV_DIM, 0),
            row(SSM_GSZ, 1), row(SSM_GSZ, 2), row(LANE, 24)]
    par = [whole((1, LANE))] * 3 + [whole((1, SSM_INNER))]
    cst = [whole((SSM_GROUPS, LANE, SSM_GSZ)), whole((SSM_GROUPS, SSM_GSZ, LANE)), whole((SSM_GROUPS, LANE, 12 * LANE)),
           whole((SSM_GROUPS, 12 * LANE, LANE)), whole((CHUNK, CHUNK)), whole((CHUNK, CHUNK))]
    hsave = pl.BlockSpec((None, None, SSM_GROUPS, 4, SSM_N, LANE), lambda b, c: (b, ci(c), 0, 0, 0, 0))
    return data, par, cst, hsave, row, whole


def _ssd_group_args(g, xbc, z, dtr, dtb, alog, dsk, nw):
    return (xbc[:, g * SSM_XBC_G:(g + 1) * SSM_XBC_G], z[g], dtr, dtb, alog, dsk, nw[:, g * SSM_GSZ:(g + 1) * SSM_GSZ])


def ssd_fwd(xbc_act, u, dtb, alog, dsk, nw, consts, bsz, seq):
    nc = seq // CHUNK
    data, par, cst, hsave, row, _ = _ssd_specs(nc, False)

    def body(xbc, z0, z1, dtr, dtb_r, alog_r, dsk_r, nw_r, e64, e64t, ecat, ecatt, tril, trilt, yn_ref, hs_ref, h):
        @pl.when(pl.program_id(1) == 0)
        def _():
            h[...] = jnp.zeros_like(h)

        hs_ref[...] = h[...]
        ys = []
        for g in range(SSM_GROUPS):
            args = _ssd_group_args(g, xbc[...], (z0[...], z1[...]), dtr[...], dtb_r[...], alog_r[...], dsk_r[...], nw_r[...])
            outs = ssd_chunk(*args, h[g, 0], h[g, 1], h[g, 2], h[g, 3], e64[g], e64t[g], ecat[g], ecatt[g], tril[...], trilt[...])
            ys.append(outs[0])
            for j in range(4):
                h[g, j] = outs[1 + j]
        yn_ref[...] = jnp.concatenate(ys, axis=1).astype(yn_ref.dtype)

    t = bsz * seq
    return pl.pallas_call(
        body, name="ssd_fwd", grid=(bsz, nc), in_specs=data + par + cst, out_specs=[row(SSM_INNER, 0), hsave],
        out_shape=[jax.ShapeDtypeStruct((t, SSM_INNER), BF), jax.ShapeDtypeStruct((bsz, nc, SSM_GROUPS, 4, SSM_N, LANE), F32)],
        scratch_shapes=[pltpu.VMEM((SSM_GROUPS, 4, SSM_N, LANE), F32)], compiler_params=_params(),
    )(xbc_act, u, u, u, dtb, alog, dsk, nw, *consts)


def ssd_bwd(xbc_act, u, dtb, alog, dsk, nw, consts, hs, dmix, bsz, seq):
    nc = seq // CHUNK
    data, par, cst, hsave, row, whole = _ssd_specs(nc, True)
    t = bsz * seq
    pcol = POOL_W // SSM_GSZ

    def body(xbc, z0, z1, dtr, dtb_r, alog_r, dsk_r, nw_r, e64, e64t, ecat, ecatt, tril, trilt, hs_ref, dy0, dy1,
             dxbc, dz, ddt, ddtb, dalog, ddsk, dnw, dh):
        @pl.when(pl.program_id(1) == 0)
        def _():
            dh[...] = jnp.zeros_like(dh)

        per_group = []
        for g, dyn in enumerate((dy0, dy1)):
            cst_vals = (e64[g], e64t[g], ecat[g], ecatt[g], tril[...], trilt[...])
            prim = _ssd_group_args(g, xbc[...], (z0[...], z1[...]), dtr[...], dtb_r[...], alog_r[...], dsk_r[...], nw_r[...])
            prim = prim + (hs_ref[g, 0], hs_ref[g, 1], hs_ref[g, 2], hs_ref[g, 3])
            _, vjp = jax.vjp(lambda *args, c=cst_vals: ssd_chunk(*args, *c), *prim)
            gr = vjp((dyn[...].astype(F32), dh[g, 0], dh[g, 1], dh[g, 2], dh[g, 3]))
            for j in range(4):
                dh[g, j] = gr[7 + j]
            per_group.append(gr)
        g0, g1 = per_group
        dxbc[...] = jnp.concatenate([g0[0], g1[0]], axis=1)
        dz[...] = jnp.concatenate([g0[1], g1[1]], axis=1).astype(dz.dtype)
        ddt[...] = g0[2] + g1[2]

        @pl.when(_first((0, 1)))
        def _():
            for r in (ddtb, dalog, ddsk, dnw):
                r[...] = jnp.zeros_like(r)

        ddtb[...] += g0[3] + g1[3]
        dalog[...] += g0[4] + g1[4]
        ddsk[...] += g0[5] + g1[5]
        dnw[...] += jnp.concatenate([g0[6], g1[6]], axis=1)

    out_specs = [row(SSM_CONV_DIM, 0), row(SSM_INNER, 0), row(LANE, 0), whole((1, LANE)), whole((1, LANE)), whole((1, LANE)),
                 whole((1, SSM_INNER))]
    lane = jax.ShapeDtypeStruct((1, LANE), F32)
    out_shape = [jax.ShapeDtypeStruct((t, SSM_CONV_DIM), F32), jax.ShapeDtypeStruct((t, SSM_INNER), BF),
                 jax.ShapeDtypeStruct((t, LANE), F32), lane, lane, lane, jax.ShapeDtypeStruct((1, SSM_INNER), F32)]
    return pl.pallas_call(
        body, name="ssd_bwd", grid=(bsz, nc), in_specs=data + par + cst + [hsave, row(SSM_GSZ, pcol), row(SSM_GSZ, pcol + 1)],
        out_specs=out_specs, out_shape=out_shape, scratch_shapes=[pltpu.VMEM((SSM_GROUPS, 4, SSM_N, LANE), F32)],
        compiler_params=_params(),
    )(xbc_act, u, u, u, dtb, alog, dsk, nw, *consts, hs, dmix, dmix)


TB = 1024


def _rows(d, col=0):
    return pl.BlockSpec((TB, d), lambda i: (i, col))


def _par(d):
    return pl.BlockSpec((1, d), lambda i: (0, 0))


def _sd(shape, dtype=F32):
    return jax.ShapeDtypeStruct(shape, dtype)


def _round_up(n, m):
    return -(-n // m) * m


def _pad_rows(a, rows):
    return jnp.pad(a, ((0, rows - a.shape[0]), (0, 0)))


def _pack128(arrs):
    flat = jnp.concatenate([a.reshape(-1) for a in arrs])
    n = flat.shape[0]
    rows = -(-n // (8 * LANE)) * 8
    return jnp.pad(flat, (0, rows * LANE - n)).reshape(rows, LANE)


def _unpack128(packed, shapes):
    flat = packed.reshape(-1)
    out, off = [], 0
    for s in shapes:
        n = int(np.prod(s))
        out.append(flat[off:off + n].reshape(s))
        off += n
    return out


def kernel(x, mem, norm_gains, xa_wq, xa_wkv, xa_wo, mlp_w1, mlp_w2, ab_w_in, pool_w, pool_scale, ssm_conv_w, ssm_conv_b, ssm_dt_bias, ssm_a_log, ssm_d, ssm_norm, ab_w_out, cd_w_in, conf_dw_w, conf_dw_b, conf_ln_g, conf_ln_b, sc_conv_w, cd_w_out, loss_target, m_norm_gains, m_xa_wq, m_xa_wkv, m_xa_wo, m_mlp_w1, m_mlp_w2, m_ab_w_in, m_pool_w, m_pool_scale, m_ssm_conv_w, m_ssm_conv_b, m_ssm_dt_bias, m_ssm_a_log, m_ssm_d, m_ssm_norm, m_ab_w_out, m_cd_w_in, m_conf_dw_w, m_conf_dw_b, m_conf_ln_g, m_conf_ln_b, m_sc_conv_w, m_cd_w_out, v_norm_gains, v_xa_wq, v_xa_wkv, v_xa_wo, v_mlp_w1, v_mlp_w2, v_ab_w_in, v_pool_w, v_pool_scale, v_ssm_conv_w, v_ssm_conv_b, v_ssm_dt_bias, v_ssm_a_log, v_ssm_d, v_ssm_norm, v_ab_w_out, v_cd_w_in, v_conf_dw_w, v_conf_dw_b, v_conf_ln_g, v_conf_ln_b, v_sc_conv_w, v_cd_w_out):
    args = locals()
    w = {n: args[n] for n in WEIGHTS}
    mom_m = {n: args["m_" + n] for n in WEIGHTS}
    mom_v = {n: args["v_" + n] for n in WEIGHTS}
    ex = Exchange(w)
    loss_local, grad_x, small_grads = local_step(x, mem, loss_target, ex)
    outs = {}

    started = ex.put_small(small_grads, loss_local)
    landed = {key: ex.landed(key, started) for key in ('l1', 'cd', 'l0')}
    late = []
    for n, keys in (('mlp_w1', ('l0', 'l1')), ('mlp_w2', ('l0', 'l1')), ('xa_wkv', ('l0', 'l1')), ('xa_wq', ('l0', 'l1')),
                    ('xa_wo', ('l0', 'l1')), ('cd_w_in', ('cd',)), ('cd_w_out', ('cd',))):
        lands = [landed[key][0] for key in keys]
        offs = [landed[key][1][(n, layer)] for layer, key in enumerate(keys)]
        outs[n] = update_from_slots(lands, offs, w[n], mom_m[n], mom_v[n], SHARD_AXIS[n] == 2, "update_" + n)
        late.append(outs[n][1])
    g_own, loss = ex.reduced_small(late)
    land_ab, offs_ab = ex.landed('ab', late)
    outs['ab_w_out'] = update_from_slots([land_ab], [offs_ab[('ab_w_out', 0)]], w['ab_w_out'], mom_m['ab_w_out'],
                                         mom_v['ab_w_out'], False, "update_ab_w_out")
    res = update_from_slots([land_ab], [offs_ab[('ab_w_in', 0)]], jnp.swapaxes(w['ab_w_in'], 1, 2), jnp.swapaxes(mom_m['ab_w_in'], 1, 2),
                            jnp.swapaxes(mom_v['ab_w_in'], 1, 2), False, "update_ab_w_in")
    outs['ab_w_in'] = tuple(jnp.swapaxes(r, 1, 2) for r in res)
    small = SMALL_SHARDED + REPLICATED
    upd = adamw_many([w[n] for n in small], [mom_m[n] for n in small], [mom_v[n] for n in small], [g_own[n] for n in small],
                     "adamw_small")
    for i, n in enumerate(small):
        outs[n] = (g_own[n], upd[0][i], upd[1][i], upd[2][i])
    return (loss, grad_x.reshape(x.shape), *[outs[n][0] for n in WEIGHTS], *[outs[n][1] for n in WEIGHTS],
            *[outs[n][2] for n in WEIGHTS], *[outs[n][3] for n in WEIGHTS])


G_AB = (('ab_w_in', 0), ('ab_w_out', 0))
G_L0 = (('xa_wq', 0), ('xa_wkv', 0), ('xa_wo', 0), ('mlp_w1', 0), ('mlp_w2', 0))
G_L1 = (('xa_wq', 1), ('xa_wkv', 1), ('xa_wo', 1), ('mlp_w1', 1), ('mlp_w2', 1))
G_CD = (('cd_w_in', 0), ('cd_w_out', 0))
GATHER_GROUPS = {'ab': G_AB[:1], 'l0a': G_AB[1:] + G_L0[:3], 'l0b': G_L0[3:], 'cd': G_CD, 'l1a': G_L1[:3], 'l1b': G_L1[3:]}
SHARD_AXIS = dict(BIG)
MEMBER_ROW_TILE = 64
FLAT_ROW_TILE = 128


def _members(group, w):
    out = []
    for n, layer in group:
        shp = w[n].shape[1:]
        if SHARD_AXIS[n] == 2:
            shp = (shp[1], shp[0])
        assert shp[1] == D, (n, shp)
        out.append((n, layer, shp, shp[0], _round_up(shp[0], MEMBER_ROW_TILE)))
    return out


def _group_rows(group, w):
    return _round_up(sum(m[4] for m in _members(group, w)), FLAT_ROW_TILE)


def _flat_shards(group, w):
    parts = []
    for n, layer, _, _, padded in _members(group, w):
        shard = w[n][layer].astype(BF)
        parts.append(_pad_rows(shard.T if SHARD_AXIS[n] == 2 else shard, padded))
    return _pad_rows(jnp.concatenate(parts, axis=0), _group_rows(group, w))


def _full_from_slots(land, group, w):
    out, off = {}, 0
    for n, layer, shp, rows, padded in _members(group, w):
        out[(n, layer)] = land[:, off:off + rows].reshape(N_DEV * rows, D)
        off += padded
    return out


def _slots_from_full(grads, group, w):
    parts = []
    for n, layer, shp, rows, padded in _members(group, w):
        blk = grads[(n, layer)].astype(BF).reshape(N_DEV, rows, D)
        parts.append(jnp.pad(blk, ((0, 0), (0, padded - rows), (0, 0))))
    send = jnp.concatenate(parts, axis=1)
    return jnp.pad(send, ((0, 0), (0, _group_rows(group, w) - send.shape[1]), (0, 0)))


_HBM = pl.BlockSpec(memory_space=pltpu.HBM)
_SEM = pl.BlockSpec(memory_space=pltpu.SEMAPHORE)
_ANY = pl.BlockSpec(memory_space=pl.ANY)


def _peer_copy(k, src, dst, send_sems, recv_sems, peer):
    return pltpu.make_async_remote_copy(src_ref=src, dst_ref=dst, send_sem=send_sems.at[k], recv_sem=recv_sems.at[k],
                                        device_id=peer, device_id_type=pl.DeviceIdType.MESH)


def exchange_start(src, name, scatter, after=()):
    shape = src.shape[-2:]
    after = list(after)

    def body(src_ref, land_ref, *rest):
        send_sems, recv_sems, token = rest[len(after)], rest[len(after) + 1], rest[-1]
        me = _me()
        for k, f in enumerate(_FLIPS):
            peer = _flip(me, f)
            piece = src_ref.at[_slot(peer)] if scatter else src_ref
            _peer_copy(k, piece, land_ref.at[_slot(me)], send_sems, recv_sems, peer).start()
        token[...] = jnp.zeros_like(token)

    land = pltpu.with_memory_space_constraint(lax.empty((N_DEV,) + shape, src.dtype), pltpu.HBM)
    return pl.pallas_call(
        body, name=name,
        out_shape=(pltpu.SemaphoreType.DMA((7,)), pltpu.SemaphoreType.DMA((7,)), pltpu.HBM(src.shape, src.dtype),
                   pltpu.HBM((N_DEV,) + shape, src.dtype), jax.ShapeDtypeStruct((8, LANE), F32)),
        in_specs=(_HBM, _HBM) + (_ANY,) * len(after), out_specs=(_SEM, _SEM, _HBM, _HBM, pl.BlockSpec(memory_space=pltpu.VMEM)),
        input_output_aliases={0: 2, 1: 3},
        compiler_params=pltpu.CompilerParams(has_side_effects=pltpu.SideEffectType.DATAFLOW_SIDE_EFFECTING),
    )(pltpu.with_memory_space_constraint(src, pltpu.HBM), land, *after)


def exchange_wait(handles, after, name, scatter):
    send_sems, recv_sems, src_thru, land_thru, _ = handles
    after = list(after) if isinstance(after, (list, tuple)) else [after]

    def body(src_ref, land_ref, send_sems, recv_sems, *rest):
        token = rest[-1]
        me = _me()
        for k, f in enumerate(_FLIPS):
            peer = _flip(me, f)
            piece = src_ref.at[_slot(peer)] if scatter else src_ref
            cp = _peer_copy(k, piece, land_ref.at[_slot(peer)], send_sems, recv_sems, peer)
            cp.wait_send()
            cp.wait_recv()
        token[...] = jnp.zeros_like(token)

    return pl.pallas_call(
        body, name=name, out_shape=(pltpu.HBM(src_thru.shape, src_thru.dtype), pltpu.HBM(land_thru.shape, land_thru.dtype),
                                    jax.ShapeDtypeStruct((8, LANE), F32)),
        in_specs=(_HBM, _HBM, _SEM, _SEM) + (_ANY,) * len(after), out_specs=(_HBM, _HBM, pl.BlockSpec(memory_space=pltpu.VMEM)),
        input_output_aliases={0: 0, 1: 1},
        compiler_params=pltpu.CompilerParams(has_side_effects=pltpu.SideEffectType.DATAFLOW_SIDE_EFFECTING),
    )(src_thru, land_thru, send_sems, recv_sems, *after)


class Exchange:
    def __init__(self, w):
        self.w = w
        self.me = _slot(_me())
        shapes = [w[n].shape for n in SMALL_SHARDED]
        gs = all_gather(_pack128([w[n] for n in SMALL_SHARDED]), "gather_small")
        per_dev = [_unpack128(gs[d], shapes) for d in range(N_DEV)]
        self.small = {n: jnp.concatenate([per_dev[d][i] for d in range(N_DEV)], axis=-1) for i, n in enumerate(SMALL_SHARDED)}
        self.small.update({n: w[n] for n in REPLICATED})
        first = all_gather(_flat_shards(GATHER_GROUPS['ab'], w), "gather_ab")
        self.first = _full_from_slots(first, GATHER_GROUPS['ab'], w)
        self.gathers, self.done, self.tokens, self.reductions = {}, {}, [], {}
        self.start_gather('l0a', after=[first])
        self.start_gather('l0b', after=[self.gathers['l0a'][4]])

    def take_tokens(self):
        toks, self.tokens = self.tokens, []
        return toks

    def start_gather(self, key, after=()):
        group = GATHER_GROUPS[key]
        self.gathers[key] = exchange_start(_flat_shards(group, self.w), f"gather_{key}_start", False, after=after)
        self.tokens.append(self.gathers[key][4])

    def weights(self, key, after):
        if key == 'ab':
            return self.first
        handles = self.gathers[key]
        _, land, self.done[key] = exchange_wait(handles, after, f"gather_{key}_wait", False)
        land = lax.dynamic_update_slice(land, handles[2][None], (self.me, 0, 0))
        return _full_from_slots(land, GATHER_GROUPS[key], self.w)

    def put_grads(self, key, group, grads):
        send = _slots_from_full(grads, group, self.w)
        handles = exchange_start(send, f"reduce_{key}_start", True)
        self.reductions[key] = (group, handles)
        self.tokens.append(handles[4])

    def landed(self, key, after):
        group, handles = self.reductions[key]
        send, land, _ = exchange_wait(handles, after, f"reduce_{key}_wait", True)
        mine = lax.dynamic_slice_in_dim(send, self.me, 1, axis=0)
        land = lax.dynamic_update_slice(land, mine, (self.me, 0, 0))
        offs, off = {}, 0
        for n, layer, _, _, padded in _members(group, self.w):
            offs[(n, layer)] = off
            off += padded
        return land, offs

    def put_small(self, small_grads, loss_local):
        small = SMALL_SHARDED + REPLICATED
        self.small_shapes = [small_grads[n].shape for n in small] + [(1,)]
        packed = _pack128([small_grads[n] for n in small] + [loss_local.reshape(1)])
        self.small_handles = exchange_start(packed, "gather_small_grads_start", False)
        return self.small_handles[4]

    def reduced_small(self, after):
        small = SMALL_SHARDED + REPLICATED
        src, land, _ = exchange_wait(self.small_handles, after, "gather_small_grads_wait", False)
        gs = lax.dynamic_update_slice(land, src[None], (self.me, 0, 0))
        tot = _unpack128(sum_slots(gs, "sum_small", 1024), self.small_shapes)
        out = {}
        for n, g in zip(small, tot):
            if n in SMALL_SHARDED:
                width = self.w[n].shape[-1]
                g = lax.dynamic_slice_in_dim(g, self.me * width, width, axis=g.ndim - 1)
            out[n] = g
        return out, tot[-1].reshape(())


def local_step(x, mem, target, ex):
    bsz, seq, _ = x.shape
    t = bsz * seq
    nb = t // TB
    nc = seq // CHUNK
    x0 = x.reshape(t, D)
    mem2 = mem.reshape(bsz * N_MEM, D)
    tgt = target.reshape(t, D)
    p = ex.small
    gains = p['norm_gains']
    big = {}

    def gain(layer, i):
        g = gains[layer, i].reshape(1, D)
        for tok in ex.take_tokens():
            g = g + tok[0, 0]
        return g

    consts = _ssd_consts()
    grads = {}
    saved = [dict(), dict()]

    def matmul_res(a, b, name, xin, ga, gb):
        return matmul(a, b, 'nn', name, (F32, F32, BF), epilogue=res_epilogue, extras=[xin], params=[ga, gb])

    def attn_specs():
        nq = seq // TB
        q = pl.BlockSpec((TB, D), lambda b, i: (b * nq + i, 0))
        kv = pl.BlockSpec((N_MEM, 2 * D), lambda b, i: (b, 0))
        return (bsz, nq), q, kv

    def attention_fwd(layer, xin, hin, sv, ga, gb):
        q = matmul(hin, big[('xa_wq', layer)], 'nn', f"q_{layer}", BF)
        kv = matmul(mem2, big[('xa_wkv', layer)], 'nt', f"kv_{layer}", BF)
        grid, qs, kvs = attn_specs()
        o, = fwd_call(attn_fn, f"attn_{layer}", grid, [q, kv], [qs, kvs], [_sd((t, D), BF)], [qs])
        ao, x_next, h_next = matmul_res(o, big[('xa_wo', layer)], f"ao_{layer}", xin, ga, gb)
        sv.update(q=q, kv=kv, o=o, ao=ao)
        return ao, x_next, h_next

    def mlp_fwd(layer, hin, sv, res):
        r, rr = matmul(hin, big[('mlp_w1', layer)], 'nt', f"mlp1_{layer}", (BF, BF), epilogue=act_epilogue)
        out = matmul_res(rr, big[('mlp_w2', layer)], f"mlp2_{layer}", *res)
        sv.update(r=r, rr=rr, mo=out[0])
        return out

    sv = saved[0]
    h0, = fwd_call(seg_in, "norm_in", (nb,), [x0, gain(0, 0)], [_rows(D), _par(D)], [_sd((t, D), BF)], [_rows(D)])
    big.update(ex.weights('ab', h0))
    xbc0 = POOL_W + SSM_INNER
    w_ab_in = big[('ab_w_in', 0)]
    w_ab_in = _pad_rows(jnp.concatenate([w_ab_in[:xbc0], _xbc_group(w_ab_in[xbc0:xbc0 + SSM_CONV_DIM], 0),
                                         w_ab_in[xbc0 + SSM_CONV_DIM:]], axis=0), AB_IN_PAD)
    conv_w, conv_b = _xbc_group(p['ssm_conv_w'][0], 1), _xbc_group(p['ssm_conv_b'], 1)
    u0 = matmul(h0, w_ab_in, 'nt', "ab_in")
    pool_outs = []
    for g in range(POOL_GROUPS):
        seqspec = pl.BlockSpec((seq, PG), lambda b, g=g: (b, g))
        po, = fwd_call(make_pool_fn(g), f"pool_{g}", (bsz,), [u0, p['pool_w'][0, g], p['pool_scale']],
                       [seqspec, pl.BlockSpec((PG, PG), lambda b: (0, 0)), pl.BlockSpec((1, PG), lambda b, g=g: (0, g))],
                       [_sd((t, PG), BF)], [pl.BlockSpec((seq, PG), lambda b: (b, 0))])
        pool_outs.append(po)
    cw = 256
    ncb = SSM_CONV_DIM // cw
    cbase = (POOL_W + SSM_INNER) // cw
    conv_in_specs = [pl.BlockSpec((seq, cw), lambda j, b: (b, cbase + j)), pl.BlockSpec((SSM_CONV, cw), lambda j, b: (0, j)),
                     pl.BlockSpec((1, cw), lambda j, b: (0, j))]
    conv_out_spec = pl.BlockSpec((seq, cw), lambda j, b: (b, j))
    xbc_act, = fwd_call(conv4_fn, "ssm_conv", (ncb, bsz), [u0, conv_w, conv_b], conv_in_specs,
                        [_sd((t, SSM_CONV_DIM))], [conv_out_spec])
    dtb = jnp.pad(p['ssm_dt_bias'], ((0, 0), (0, LANE - SSM_HEADS)))
    alog = jnp.pad(p['ssm_a_log'], ((0, 0), (0, LANE - SSM_HEADS)))
    dsk = jnp.pad(p['ssm_d'], ((0, 0), (0, LANE - SSM_HEADS)))
    yn, hs = ssd_fwd(xbc_act, u0, dtb, alog, dsk, p['ssm_norm'], consts, bsz, seq)
    mix0 = jnp.concatenate(pool_outs + [yn], axis=1)
    big.update(ex.weights('l0a', yn))
    ex.start_gather('cd', after=[ex.done['l0a']])
    ex.start_gather('l1a', after=[ex.gathers['cd'][4]])
    ex.start_gather('l1b', after=[ex.gathers['l1a'][4]])
    m0, x1, h2 = matmul_res(mix0, big[('ab_w_out', 0)], "ab_out", x0, gain(0, 1), gain(0, 2))
    ao0, x2, h3 = attention_fwd(0, x1, h2, sv, gain(0, 3), gain(0, 4))
    big.update(ex.weights('l0b', h3))
    mo0, x3, h4 = mlp_fwd(0, h3, sv, (x2, gain(0, 5), gain(1, 0)))
    big.update(ex.weights('cd', mo0))

    sv1 = saved[1]
    nd = D // LANE
    w_cd_in = big[('cd_w_in', 0)].reshape(5, nd, LANE, D).transpose(1, 0, 2, 3).reshape(CD_IN, D)
    u1 = matmul(h4, w_cd_in, 'nt', "cd_in")
    cd_par = [pl.BlockSpec((CONF_K, LANE), lambda j, b: (0, j)), pl.BlockSpec((1, LANE), lambda j, b: (0, j)),
              pl.BlockSpec((SC_K, LANE), lambda j, b: (0, j))]
    cd_ins = [u1, p['conf_dw_w'][0], p['conf_dw_b'], p['sc_conv_w'][0]]
    cd_u_spec = pl.BlockSpec((seq, 5 * LANE), lambda j, b: (b, j))
    cd_in_specs = [cd_u_spec] + cd_par
    cd_out_spec = pl.BlockSpec((seq, LANE), lambda j, b: (b, j))
    vconv, mix1 = fwd_call(cd1_fn, "cd_conv", (nd, bsz), cd_ins, cd_in_specs, [_sd((t, D)), _sd((t, CD_OUT), BF)],
                           [cd_out_spec, pl.BlockSpec((seq, LANE), lambda j, b: (b, nd + j))])
    mix1, = fwd_call(seg_ln, "conf_ln", (nb,), [vconv, p['conf_ln_g'], p['conf_ln_b']], [_rows(D), _par(D), _par(D)],
                     [_sd((t, CD_OUT), BF)], [_rows(D)], into=mix1)
    m1, x4, h5 = matmul_res(mix1, big[('cd_w_out', 0)], "cd_out", x3, gain(1, 1), gain(1, 2))
    big.update(ex.weights('l1a', h5))
    ao1, x5, h6 = attention_fwd(1, x4, h5, sv1, gain(1, 3), gain(1, 4))
    big.update(ex.weights('l1b', h6))
    r1, rr1 = matmul(h6, big[('mlp_w1', 1)], 'nt', "mlp1_1", (BF, BF), epilogue=act_epilogue)
    sv1.update(r=r1, rr=rr1)
    dx5, dmo1, dg15, lanes = matmul(rr1, big[('mlp_w2', 1)], 'nn', "mlp2_1", (F32, BF), epilogue=loss_epilogue, extras=[x5, tgt],
                                    params=[gain(1, 5)], n_acc=2)
    loss = 0.5 * jnp.sum(lanes) / float(D)

    gain_grads = {(1, 5): dg15}

    def matmul_res_bwd(a, b, mode, name, xin, m, ga, gb, dx1):
        return list(matmul(a, b, mode, name, (F32, BF), epilogue=res_bwd_epilogue, extras=[xin, m, dx1], params=[ga, gb], n_acc=2))

    def mlp_bwd(layer, hin, dmo, sv, res):
        grads_w2 = matmul(sv['rr'], dmo, 'tn', f"d_mlp_w2_{layer}", BF)
        dr, = matmul(dmo, big[('mlp_w2', layer)], 'nt', f"d_r_{layer}", (BF,), epilogue=act_bwd_epilogue, extras=[sv['r']])
        grads_w1 = matmul(dr, hin, 'tn', f"d_mlp_w1_{layer}", BF)
        return matmul_res_bwd(dr, big[('mlp_w1', layer)], 'nn', f"d_h_mlp_{layer}", *res) + [grads_w1, grads_w2]

    def attention_bwd(layer, hin, dao, sv, res):
        g_wo = matmul(sv['o'], dao, 'tn', f"d_xa_wo_{layer}", BF)
        do = matmul(dao, big[('xa_wo', layer)], 'nt', f"d_o_{layer}", BF)
        grid, qs, kvs = attn_specs()
        dq, dkv = bwd_call(attn_fn, f"d_attn_{layer}", grid, [sv['q'], sv['kv']], [qs, kvs], [do], [qs], [0, 1],
                           [_sd((t, D), BF), _sd((bsz * N_MEM, 2 * D))], [qs, kvs], [None, (1,)])
        g_wkv = matmul(dkv, mem2, 'tn', f"d_xa_wkv_{layer}", BF)
        g_wq = matmul(hin, dq, 'tn', f"d_xa_wq_{layer}", BF)
        return matmul_res_bwd(dq, big[('xa_wq', layer)], 'nt', f"d_h_attn_{layer}", *res) + [g_wq, g_wkv, g_wo]

    per_layer = {k: [None, None] for k in ('xa_wq', 'xa_wkv', 'xa_wo', 'mlp_w1', 'mlp_w2')}

    (dx4, dao1, gain_grads[(1, 3)], gain_grads[(1, 4)], per_layer['mlp_w1'][1],
     per_layer['mlp_w2'][1]) = mlp_bwd(1, h6, dmo1, sv1, (x4, ao1, gain(1, 3), gain(1, 4), dx5))
    (dx3, dm1, gain_grads[(1, 1)], gain_grads[(1, 2)], per_layer['xa_wq'][1], per_layer['xa_wkv'][1],
     per_layer['xa_wo'][1]) = attention_bwd(1, h5, dao1, sv1, (x3, m1, gain(1, 1), gain(1, 2), dx4))
    ex.put_grads('l1', G_L1, {(k, 1): v[1] for k, v in per_layer.items()})
    g_cd_out = matmul(mix1, dm1, 'tn', "d_cd_w_out", BF)
    dmix1 = matmul(dm1, big[('cd_w_out', 0)], 'nt', "d_mix1", after=ex.take_tokens())
    dvconv, dlg, dlb = bwd_call(seg_ln, "d_conf_ln", (nb,), [vconv, p['conf_ln_g'], p['conf_ln_b']],
                                [_rows(D), _par(D), _par(D)], [dmix1], [_rows(D, 0)], [0, 1, 2],
                                [_sd((t, D)), _sd((1, D)), _sd((1, D))], [_rows(D), _par(D), _par(D)], [None, (0,), (0,)])
    grads['conf_ln_g'], grads['conf_ln_b'] = dlg, dlb
    cd_g = bwd_call(cd1_fn, "d_cd_conv", (nd, bsz), cd_ins, cd_in_specs, [dvconv, dmix1],
                    [cd_out_spec, pl.BlockSpec((seq, LANE), lambda j, b: (b, nd + j))], list(range(4)),
                    [_sd((t, CD_IN), BF), _sd((CONF_K, D)), _sd((1, D)), _sd((SC_K, D))], [cd_u_spec] + cd_par,
                    [None, (1,), (1,), (1,)])
    du1 = cd_g[0]
    grads['conf_dw_w'], grads['conf_dw_b'], grads['sc_conv_w'] = cd_g[1][None], cd_g[2], cd_g[3][None]
    g_cd_in = matmul(du1, h4, 'tn', "d_cd_w_in", BF).reshape(nd, 5, LANE, D).transpose(1, 0, 2, 3).reshape(CD_IN, D)
    ex.put_grads('cd', G_CD, {('cd_w_in', 0): g_cd_in, ('cd_w_out', 0): g_cd_out})
    dx2, dmo0, gain_grads[(0, 5)], gain_grads[(1, 0)] = matmul_res_bwd(du1, w_cd_in, 'nn', "d_h_cd", x2, mo0, gain(0, 5),
                                                                       gain(1, 0), dx3)
    (dx1, dao0, gain_grads[(0, 3)], gain_grads[(0, 4)], per_layer['mlp_w1'][0],
     per_layer['mlp_w2'][0]) = mlp_bwd(0, h3, dmo0, sv, (x1, ao0, gain(0, 3), gain(0, 4), dx2))
    (dx0r, dm0, gain_grads[(0, 1)], gain_grads[(0, 2)], per_layer['xa_wq'][0], per_layer['xa_wkv'][0],
     per_layer['xa_wo'][0]) = attention_bwd(0, h2, dao0, sv, (x0, m0, gain(0, 1), gain(0, 2), dx1))
    ex.put_grads('l0', G_L0, {(k, 0): v[0] for k, v in per_layer.items()})
    g_ab_out = matmul(mix0, dm0, 'tn', "d_ab_w_out", BF)
    dmix0 = matmul(dm0, big[('ab_w_out', 0)], 'nt', "d_mix0", after=ex.take_tokens())
    dxbc_act, dz, ddt, ddtb, dalog, ddsk, dnw = ssd_bwd(xbc_act, u0, dtb, alog, dsk, p['ssm_norm'], consts, hs, dmix0, bsz, seq)
    grads['ssm_dt_bias'] = ddtb[:, :SSM_HEADS]
    grads['ssm_a_log'] = dalog[:, :SSM_HEADS]
    grads['ssm_d'] = ddsk[:, :SSM_HEADS]
    grads['ssm_norm'] = dnw
    dxr, dcw, dcb = bwd_call(conv4_fn, "d_ssm_conv", (ncb, bsz), [u0, conv_w, conv_b], conv_in_specs,
                             [dxbc_act], [conv_out_spec], [0, 1, 2],
                             [_sd((t, SSM_CONV_DIM), BF), _sd((SSM_CONV, SSM_CONV_DIM)), _sd((1, SSM_CONV_DIM))],
                             [conv_out_spec, conv_in_specs[1], conv_in_specs[2]], [None, (1,), (1,)])
    grads['ssm_conv_w'], grads['ssm_conv_b'] = _xbc_ungroup(dcw, 1)[None], _xbc_ungroup(dcb, 1)
    dpool, dpw, dps = [], [], []
    for g in range(POOL_GROUPS):
        seqspec = pl.BlockSpec((seq, PG), lambda b, g=g: (b, g))
        one = pl.BlockSpec((seq, PG), lambda b: (b, 0))
        wspec = pl.BlockSpec((PG, PG), lambda b: (0, 0))
        sspec = pl.BlockSpec((1, PG), lambda b, g=g: (0, g))
        a, bb, c = bwd_call(make_pool_fn(g), f"d_pool_{g}", (bsz,), [u0, p['pool_w'][0, g], p['pool_scale']],
                            [seqspec, wspec, sspec], [dmix0], [seqspec], [0, 1, 2],
                            [_sd((t, PG), BF), _sd((PG, PG)), _sd((1, PG))], [one, wspec, pl.BlockSpec((1, PG), lambda b: (0, 0))],
                            [None, (0,), (0,)])
        dpool.append(a)
        dpw.append(bb)
        dps.append(c)
    grads['pool_w'] = jnp.stack(dpw)[None]
    grads['pool_scale'] = jnp.concatenate(dps, axis=1)
    du0 = jnp.concatenate(dpool + [dz, dxr, ddt.astype(BF)], axis=1)
    g_ab_in = matmul(du0, h0, 'tn', "d_ab_w_in", BF)
    g_ab_in = jnp.concatenate([g_ab_in[:xbc0], _xbc_ungroup(g_ab_in[xbc0:xbc0 + SSM_CONV_DIM], 0),
                               g_ab_in[xbc0 + SSM_CONV_DIM:AB_IN]], axis=0)
    ex.put_grads('ab', G_AB, {('ab_w_in', 0): g_ab_in, ('ab_w_out', 0): g_ab_out})
    dx, dg00 = matmul(du0, w_ab_in, 'nn', "d_h_ab", (F32,), epilogue=in_bwd_epilogue, extras=[x0, dx0r], params=[gain(0, 0)],
                      after=ex.take_tokens(), n_acc=1)
    gain_grads[(0, 0)] = dg00
    grads['norm_gains'] = jnp.stack([jnp.concatenate([gain_grads[(l, i)] for i in range(6)], axis=0) for l in range(2)])
    return loss, dx, grads
```

```python
import functools
import math

import numpy as np
import jax
import jax.numpy as jnp
from jax import lax
from jax.experimental import pallas as pl
from jax.experimental.pallas import tpu as pltpu

BF = jnp.bfloat16
F32 = jnp.float32

N_DEV = 8
D = 1024
N_MEM = 256
XA_HEADS = 4
XA_DH = D // XA_HEADS
POOL_GROUPS = 4
PG = 128
POOL_W = POOL_GROUPS * PG
SSM_INNER = 1024
SSM_GROUPS = 2
SSM_GSZ = SSM_INNER // SSM_GROUPS
SSM_HEADS = 16
SSM_P = 64
SSM_N = 128
SSM_CONV = 4
SSM_CONV_DIM = SSM_INNER + 2 * SSM_GROUPS * SSM_N
SSM_XBC_G = SSM_GSZ + 2 * SSM_N
CHUNK = 128
AB_IN = POOL_W + SSM_INNER + SSM_CONV_DIM + SSM_HEADS
AB_IN_PAD = POOL_W + SSM_INNER + SSM_CONV_DIM + 128
AB_OUT = POOL_W + SSM_INNER
CONF_K = 31
SC_K = 3
CD_IN = 5 * D
CD_OUT = 2 * D
MLP_H = 4 * D
RMS_EPS = 1e-6
LN_EPS = 1e-5
ADAM_LR = 0.001
ADAM_B1 = 0.9
ADAM_B2 = 0.999
ADAM_EPS = 1e-08
ADAM_WD = 0.01
ADAM_STEP = 10
VMEM_LIMIT = 56 * 1024 * 1024
LANE = 128

NAMES = ['x', 'mem', 'norm_gains', 'xa_wq', 'xa_wkv', 'xa_wo', 'mlp_w1', 'mlp_w2', 'ab_w_in', 'pool_w', 'pool_scale',
         'ssm_conv_w', 'ssm_conv_b', 'ssm_dt_bias', 'ssm_a_log', 'ssm_d', 'ssm_norm', 'ab_w_out', 'cd_w_in', 'conf_dw_w',
         'conf_dw_b', 'conf_ln_g', 'conf_ln_b', 'sc_conv_w', 'cd_w_out', 'loss_target']
WEIGHTS = NAMES[2:25]
BIG = [('xa_wq', 1), ('xa_wkv', 2), ('xa_wo', 1), ('mlp_w1', 2), ('mlp_w2', 1), ('cd_w_in', 2), ('cd_w_out', 1),
       ('ab_w_out', 1), ('ab_w_in', 2)]
SMALL_SHARDED = ['norm_gains', 'ssm_conv_w', 'conf_dw_w', 'conf_dw_b', 'conf_ln_g', 'conf_ln_b', 'sc_conv_w']
REPLICATED = ['pool_w', 'pool_scale', 'ssm_conv_b', 'ssm_dt_bias', 'ssm_a_log', 'ssm_d', 'ssm_norm']


def _dg(a, b, ca, cb, prec=None):
    return lax.dot_general(a, b, (((ca,), (cb,)), ((), ())), precision=prec, preferred_element_type=F32)


@functools.partial(jax.custom_vjp, nondiff_argnums=(2, 3))
def bdot(a, b, ca, cb):
    return _dg(a.astype(BF), b.astype(BF), ca, cb)


def _bdot_fwd(a, b, ca, cb):
    return bdot(a, b, ca, cb), (a, b)


def _bdot_bwd(ca, cb, res, g):
    a, b = res
    g16, a16, b16 = g.astype(BF), a.astype(BF), b.astype(BF)
    da = _dg(g16, b16, 1, 1 - cb) if ca == 1 else _dg(b16, g16, 1 - cb, 1)
    db = _dg(g16, a16, 0, 1 - ca) if cb == 1 else _dg(a16, g16, 1 - ca, 0)
    return da.astype(a.dtype), db.astype(b.dtype)


bdot.defvjp(_bdot_fwd, _bdot_bwd)


def _split3(a):
    a1 = a.astype(BF)
    r1 = a - a1.astype(F32)
    a2 = r1.astype(BF)
    a3 = (r1 - a2.astype(F32)).astype(BF)
    return a1, a2, a3


def _exact_right(a, c):
    m = a.shape[0]
    if m % 16:
        return sum(_dg(p, c, 1, 0) for p in _split3(a))
    o = _dg(jnp.concatenate(_split3(a), axis=0), c, 1, 0)
    return o[:m] + o[m:2 * m] + o[2 * m:]


def _exact_left(c, a):
    n = a.shape[1]
    o = _dg(c, jnp.concatenate(_split3(a), axis=1), 1, 0)
    return o[:, :n] + o[:, n:2 * n] + o[:, 2 * n:]


@jax.custom_vjp
def cmat(a, c, ct):
    return _exact_right(a, c)


def _cmat_fwd(a, c, ct):
    return cmat(a, c, ct), (c, ct)


def _cmat_bwd(res, g):
    c, ct = res
    return _exact_right(g, ct), jnp.zeros_like(c), jnp.zeros_like(ct)


cmat.defvjp(_cmat_fwd, _cmat_bwd)


@jax.custom_vjp
def cmatl(c, ct, a):
    return _exact_left(c, a)


def _cmatl_fwd(c, ct, a):
    return cmatl(c, ct, a), (c, ct)


def _cmatl_bwd(res, g):
    c, ct = res
    return jnp.zeros_like(c), jnp.zeros_like(ct), _exact_left(ct, g)


cmatl.defvjp(_cmatl_fwd, _cmatl_bwd)


SUBLANES = 8


def _taps(x, shifts, down):
    n, c = x.shape
    pad = _round_up(max(shifts), SUBLANES)
    if pad == 0:
        return {0: x}
    zeros = jnp.zeros((pad, c), x.dtype)
    xp = jnp.concatenate([zeros, x] if down else [x, zeros], axis=0)
    rolled, out = {0: xp}, {}
    for s in shifts:
        a, b = divmod(s, SUBLANES)
        if b not in rolled:
            rolled[b] = pltpu.roll(xp, b if down else n + pad - b, 0)
        off = pad - SUBLANES * a if down else SUBLANES * a
        out[s] = rolled[b][off:off + n]
    return out


def _shift_down(x, k):
    return _taps(x, [k], True)[k]


def _shift_up(x, k):
    return _taps(x, [k], False)[k]


@functools.partial(jax.custom_vjp, nondiff_argnums=(1,))
def shift(x, k):
    return _shift_down(x, k)


def _shift_fwd(x, k):
    return _shift_down(x, k), None


def _shift_bwd(k, _, g):
    return (_shift_up(g, k),)


shift.defvjp(_shift_fwd, _shift_bwd)


@functools.partial(jax.custom_vjp, nondiff_argnums=(2,))
def cconv(u, w, width):
    taps = _taps(u, list(range(width)), True)
    acc = u * w[width - 1:width, :]
    for k in range(width - 1):
        acc = acc + taps[width - 1 - k] * w[k:k + 1, :]
    return acc


def _cconv_fwd(u, w, width):
    return cconv(u, w, width), (u, w)


def _cconv_bwd(width, res, g):
    u, w = res
    rows = lax.broadcasted_iota(jnp.int32, w.shape, 0)
    du = g * w[width - 1:width, :]
    dw = jnp.where(rows == width - 1, jnp.sum(g * u, axis=0, keepdims=True), 0.0)
    g_taps = _taps(g, list(range(width)), False)
    u_taps = _taps(u, list(range(width)), True)
    for k in range(width - 1):
        s = width - 1 - k
        du = du + g_taps[s] * w[k:k + 1, :]
        dw = dw + jnp.where(rows == k, jnp.sum(g * u_taps[s], axis=0, keepdims=True), 0.0)
    return du, dw


cconv.defvjp(_cconv_fwd, _cconv_bwd)


def _rms(x, g):
    return x * lax.rsqrt(jnp.mean(x * x, axis=-1, keepdims=True) + RMS_EPS) * g


def _rms_bwd(v, g, dout):
    r = lax.rsqrt(jnp.mean(v * v, axis=-1, keepdims=True) + RMS_EPS)
    n = v * r
    dn = dout * g
    dv = (dn - n * jnp.mean(dn * n, axis=-1, keepdims=True)) * r
    return dv, jnp.sum(dout * n, axis=0, keepdims=True)


def _params(sem=None):
    return pltpu.CompilerParams(dimension_semantics=sem, vmem_limit_bytes=VMEM_LIMIT)


def _f32(v):
    return v if v.dtype == F32 else v.astype(F32)


def _first(axes):
    ok = None
    for ax in axes:
        c = pl.program_id(ax) == 0
        ok = c if ok is None else jnp.logical_and(ok, c)
    return ok


def fwd_call(fn, name, grid, ins, in_specs, out_shapes, out_specs, into=None):
    n_in = len(ins)
    n_into = 0 if into is None else 1

    def body(*refs):
        outs = fn(*[_f32(r[...]) for r in refs[:n_in]])
        for r, o in zip(refs[n_in + n_into:], outs):
            r[...] = o.astype(r.dtype)

    extra = [] if into is None else [into]
    return pl.pallas_call(body, name=name, grid=grid, in_specs=list(in_specs) + [pl.BlockSpec(memory_space=pl.ANY)] * n_into,
                          out_specs=out_specs, out_shape=out_shapes, input_output_aliases={n_in: 0} if n_into else {},
                          compiler_params=_params())(*ins, *extra)


def bwd_call(fn, name, grid, ins, in_specs, cots, cot_specs, gidx, g_shapes, g_specs, g_acc):
    n_in, n_cot = len(ins), len(cots)

    def body(*refs):
        vals = [_f32(r[...]) for r in refs[:n_in]]

        def f_sel(*dv):
            full = list(vals)
            for i, v in zip(gidx, dv):
                full[i] = v
            return tuple(fn(*full))

        outs, vjp = jax.vjp(f_sel, *[vals[i] for i in gidx])
        cts = tuple(_f32(r[...]) for r in refs[n_in:n_in + n_cot])
        grads = vjp(cts)
        for r, g, acc in zip(refs[n_in + n_cot:], grads, g_acc):
            if acc is None:
                r[...] = g.astype(r.dtype)
            else:
                @pl.when(_first(acc))
                def _():
                    r[...] = jnp.zeros_like(r)

                r[...] += g.astype(r.dtype)

    return pl.pallas_call(body, name=name, grid=grid, in_specs=list(in_specs) + list(cot_specs), out_specs=g_specs,
                          out_shape=g_shapes, compiler_params=_params())(*ins, *cots)


def _tile(dim, pref):
    if dim <= pref:
        return dim
    best = None
    for t in range(LANE, pref + 1, LANE):
        if dim % t == 0:
            best = t
    assert best is not None, dim
    return best


MATMUL_VMEM_BUDGET = 44 * 1024 * 1024


def _matmul_tiles(m, n, k, a_bytes, b_bytes, out_bytes):
    tn = _tile(n, 1024)
    for tk_pref in (k, 2048, 1024, 512):
        tk = _tile(k, tk_pref)
        for tm_pref in (1024, 512, 256):
            tm = _tile(m, tm_pref)
            need = 2 * (tm * tk * a_bytes + tk * tn * b_bytes + tm * tn * out_bytes) + (0 if tk == k else tm * tn * 4)
            need += (tm * tk * 2 if a_bytes == 4 else 0) + (tk * tn * 2 if b_bytes == 4 else 0)
            if need <= MATMUL_VMEM_BUDGET:
                return tm, tn, tk
    raise ValueError((m, n, k))


def matmul(a, b, mode, name, out_dtype=F32, epilogue=None, extras=(), params=(), after=(), n_acc=0):
    if mode == 'nn':
        (m, k), (k2, n) = a.shape, b.shape
    elif mode == 'nt':
        (m, k), (n, k2) = a.shape, b.shape
    else:
        (k, m), (k2, n) = a.shape, b.shape
    assert k == k2, (name, a.shape, b.shape)
    n_extra = len(extras) + len(params)
    out_dtypes = out_dtype if isinstance(out_dtype, tuple) else (out_dtype,)
    per_out = sum(jnp.dtype(dt).itemsize for dt in out_dtypes) + sum(e.dtype.itemsize for e in extras)
    tm, tn, tk = _matmul_tiles(m, n, k, a.dtype.itemsize, b.dtype.itemsize, per_out)
    nk = k // tk
    ca = 0 if mode == 'tn' else 1
    cb = 1 if mode == 'nt' else 0
    a_spec = pl.BlockSpec((tk, tm), lambda i, j, kk: (kk, i)) if mode == 'tn' else pl.BlockSpec((tm, tk), lambda i, j, kk: (i, kk))
    b_spec = pl.BlockSpec((tn, tk), lambda i, j, kk: (j, kk)) if mode == 'nt' else pl.BlockSpec((tk, tn), lambda i, j, kk: (kk, j))

    def finish(o_refs, extra_refs, acc, first_row_tile):
        outs = (acc,) if epilogue is None else epilogue(acc, *[_f32(e[...]) for e in extra_refs])
        n_tile = len(o_refs) - n_acc
        for o_ref, o in zip(o_refs[:n_tile], outs[:n_tile]):
            o_ref[...] = o.astype(o_ref.dtype)
        for o_ref, o in zip(o_refs[n_tile:], outs[n_tile:]):
            o_ref[...] = jnp.where(first_row_tile, o, o_ref[...] + o)

    n_after = len(after)

    def body_whole_k(a_ref, b_ref, *refs):
        refs = refs[n_after:]
        finish(refs[n_extra:], refs[:n_extra], _dg(a_ref[...].astype(BF), b_ref[...].astype(BF), ca, cb), pl.program_id(0) == 0)

    def body_split_k(a_ref, b_ref, *refs):
        refs = refs[n_after:]
        extra_refs, o_refs, acc = refs[:n_extra], refs[n_extra:-1], refs[-1]
        kk = pl.program_id(2)
        first_row_tile = pl.program_id(0) == 0

        @pl.when(kk == 0)
        def _():
            acc[...] = jnp.zeros_like(acc)

        acc[...] += _dg(a_ref[...].astype(BF), b_ref[...].astype(BF), ca, cb)

        @pl.when(kk == nk - 1)
        def _():
            finish(o_refs, extra_refs, acc[...], first_row_tile)

    tile = pl.BlockSpec((tm, tn), lambda i, j, kk: (i, j))
    row = pl.BlockSpec((1, tn), lambda i, j, kk: (0, j))
    n_par = len(params)
    outs = pl.pallas_call(
        body_whole_k if nk == 1 else body_split_k, name=name, grid=(m // tm, n // tn, nk),
        in_specs=[a_spec, b_spec] + [pl.BlockSpec(memory_space=pl.ANY)] * n_after + [tile] * len(extras) + [row] * n_par,
        out_specs=[tile] * len(out_dtypes) + [row] * n_acc,
        out_shape=[jax.ShapeDtypeStruct((m, n), dt) for dt in out_dtypes] + [jax.ShapeDtypeStruct((1, n), F32)] * n_acc,
        scratch_shapes=[] if nk == 1 else [pltpu.VMEM((tm, tn), F32)],
        compiler_params=_params(("arbitrary",) * 3 if n_acc else ("parallel", "parallel", "arbitrary")))(a, b, *after, *extras, *params)
    return outs if isinstance(out_dtype, tuple) or n_acc else outs[0]


_FLIPS = [(0, 0, 1), (1, 0, 0), (0, 1, 0), (1, 1, 0), (1, 0, 1), (0, 1, 1), (1, 1, 1)]


def _me():
    return lax.axis_index("x"), lax.axis_index("y"), lax.axis_index("c")


def _flip(pos, f):
    return tuple(jnp.where(fi == 1, 1 - p, p) if fi else p for p, fi in zip(pos, f))


def _slot(pos):
    return 4 * pos[0] + 2 * pos[1] + pos[2]


def all_gather(v, name):
    def body(v_ref, out_ref, send_sems, recv_sems, local_sem):
        me = _me()
        sibling = _flip(me, (0, 0, 1))
        chips = [_flip(me, f) for f in ((1, 0, 0), (0, 1, 0), (1, 1, 0))]

        def copy(k, block, to, src=None):
            return pltpu.make_async_remote_copy(
                src_ref=out_ref.at[_slot(block)] if src is None else src, dst_ref=out_ref.at[_slot(block)],
                send_sem=send_sems.at[k], recv_sem=recv_sems.at[k], device_id=to, device_id_type=pl.DeviceIdType.MESH)

        mine = pltpu.make_async_copy(v_ref, out_ref.at[_slot(me)], local_sem)
        mine.start()
        first = [copy(0, me, sibling, src=v_ref)] + [copy(1 + j, me, chip, src=v_ref) for j, chip in enumerate(chips)]
        for cp in first:
            cp.start()
        passed = [copy(4 + j, chip, sibling) for j, chip in enumerate(chips)]
        for j, chip in enumerate(chips):
            copy(1 + j, chip, me).wait_recv()
            passed[j].start()
        copy(0, sibling, me).wait_recv()
        for j, chip in enumerate(chips):
            copy(4 + j, _flip(chip, (0, 0, 1)), me).wait_recv()
        for cp in first + passed:
            cp.wait_send()
        mine.wait()

    return pl.pallas_call(
        body, name=name, out_shape=jax.ShapeDtypeStruct((N_DEV,) + v.shape, v.dtype),
        in_specs=[pl.BlockSpec(memory_space=pl.ANY)], out_specs=pl.BlockSpec(memory_space=pl.ANY),
        scratch_shapes=[pltpu.SemaphoreType.DMA((7,)), pltpu.SemaphoreType.DMA((7,)), pltpu.SemaphoreType.DMA(())],
    )(v)


def sum_slots(v, name, tr=256):
    _, r, c = v.shape
    tr = _tile_rows(r, tr)

    def body(v_ref, o_ref):
        acc = v_ref[0].astype(F32)
        for s in range(1, N_DEV):
            acc = acc + v_ref[s].astype(F32)
        o_ref[...] = acc

    return pl.pallas_call(body, name=name, grid=(r // tr,), in_specs=[pl.BlockSpec((N_DEV, tr, c), lambda i: (0, i, 0))],
                          out_specs=pl.BlockSpec((tr, c), lambda i: (i, 0)), out_shape=jax.ShapeDtypeStruct((r, c), F32),
                          compiler_params=_params())(v)


def _tile_rows(r, pref):
    if r <= pref:
        return r
    best = None
    for t in range(8, pref + 1, 8):
        if r % t == 0:
            best = t
    return r if best is None else best


def _adamw_math(w, m, v, g):
    nm = ADAM_B1 * m + (1.0 - ADAM_B1) * g
    nv = ADAM_B2 * v + (1.0 - ADAM_B2) * jnp.square(g)
    m_hat = nm / (1.0 - ADAM_B1 ** ADAM_STEP)
    v_hat = nv / (1.0 - ADAM_B2 ** ADAM_STEP)
    return -ADAM_LR * (m_hat / (jnp.sqrt(v_hat) + ADAM_EPS) + ADAM_WD * w), nm, nv


def update_from_slots(lands, offs, w, m, v, transposed, name):
    layers, a, b = w.shape
    n_land = len(lands)
    if transposed:
        rb, tk = LANE, 512
        assert a % tk == 0 and b % rb == 0 and all(o % rb == 0 for o in offs), (name, w.shape, offs)
        grid = (layers, a // tk, b // rb)
        land_block = (N_DEV, rb, tk)
        tile = pl.BlockSpec((None, tk, rb), lambda l, i, j: (l, i, j))

        def land_spec(layer):
            base = offs[layer] // rb
            return pl.BlockSpec(land_block, lambda l, i, j: (0, base + jnp.where(l == layer, j, 0), jnp.where(l == layer, i, 0)))
    else:
        fits = [t for t in (256, 128, 64) if a % t == 0 and all(o % t == 0 for o in offs)]
        assert fits or all(o == 0 for o in offs), (name, w.shape, offs)
        tr = max(fits) if fits else a
        grid = (layers, a // tr)
        land_block = (N_DEV, _round_up(tr, MEMBER_ROW_TILE), b)
        tile = pl.BlockSpec((None, tr, b), lambda l, i: (l, i, 0))

        def land_spec(layer):
            base = offs[layer] // tr
            return pl.BlockSpec(land_block, lambda l, i: (0, base + jnp.where(l == layer, i, 0), 0))

    def body(*refs):
        land_refs, (w_ref, m_ref, v_ref, g_ref, d_ref, nm_ref, nv_ref, acc) = refs[:n_land], refs[n_land:]
        for layer, land in enumerate(land_refs):
            @pl.when(pl.program_id(0) == layer)
            def _(land=land):
                rows = acc.shape[0]
                s = land[0, :rows].astype(F32)
                for k in range(1, N_DEV):
                    s = s + land[k, :rows].astype(F32)
                acc[...] = s

        g = acc[...].T if transposed else acc[...]
        d, nm, nv = _adamw_math(w_ref[...], m_ref[...], v_ref[...], g)
        g_ref[...] = g
        d_ref[...] = d
        nm_ref[...] = nm
        nv_ref[...] = nv

    sh = jax.ShapeDtypeStruct(w.shape, F32)
    return pl.pallas_call(
        body, name=name, grid=grid, in_specs=[land_spec(layer) for layer in range(n_land)] + [tile] * 3, out_specs=[tile] * 4,
        out_shape=[sh] * 4, scratch_shapes=[pltpu.VMEM((rb, tk) if transposed else (tr, b), F32)],
        compiler_params=_params())(*lands, w, m, v)


def adamw_many(ws, ms, vs, gs, name):
    n = len(ws)

    def body(*refs):
        for i in range(n):
            d, nm, nv = _adamw_math(refs[i][...], refs[n + i][...], refs[2 * n + i][...], refs[3 * n + i][...])
            refs[4 * n + i][...] = d
            refs[5 * n + i][...] = nm
            refs[6 * n + i][...] = nv

    vmem = pl.BlockSpec(memory_space=pltpu.VMEM)
    shapes = [jax.ShapeDtypeStruct(a.shape, F32) for a in ws]
    res = pl.pallas_call(body, name=name, in_specs=[vmem] * (4 * n), out_specs=[vmem] * (3 * n), out_shape=shapes * 3,
                         compiler_params=_params())(*ws, *ms, *vs, *gs)
    return res[:n], res[n:2 * n], res[2 * n:]


def seg_in(x, g):
    return (_rms(x, g),)


def seg_res(x, m, ga, gb):
    x1 = x + _rms(m, ga)
    return x1, _rms(x1, gb)


def act_epilogue(r):
    t = jnp.maximum(r, 0.0)
    return r, t * t


def res_epilogue(m, x, ga, gb):
    x1, h = seg_res(x, m, ga, gb)
    return m, x1, h


def res_bwd_epilogue(dh, x, m, dx1, ga, gb):
    x1 = x + _rms(m, ga)
    d1, dgb = _rms_bwd(x1, gb, dh)
    dx = dx1 + d1
    dm, dga = _rms_bwd(m, ga, dx)
    return dx, dm, dga, dgb


def in_bwd_epilogue(dh, x, dx_res, g):
    d, dg = _rms_bwd(x, g, dh)
    return dx_res + d, dg


def loss_epilogue(mo, x, target, g):
    d = x + _rms(mo, g) - target
    dy = d / float(D)
    dm, dg = _rms_bwd(mo, g, dy)
    return dy, dm, dg, jnp.sum(d * d, axis=0, keepdims=True)


def act_bwd_epilogue(drr, r):
    return (drr * (2.0 * jnp.maximum(r, 0.0)),)


def seg_ln(v, g, b):
    mu = jnp.mean(v, axis=-1, keepdims=True)
    var = jnp.mean(jnp.square(v - mu), axis=-1, keepdims=True)
    vn = (v - mu) * lax.rsqrt(var + LN_EPS) * g + b
    return (jax.nn.silu(vn),)


def make_pool_fn(group):
    window = 2 ** (group + 1)

    def pool_fn(ug, pw, scale):
        s = ug
        for lvl in range(group + 1):
            s = s + shift(s, 2 ** lvl)
        cnt = jnp.minimum(lax.broadcasted_iota(jnp.int32, ug.shape, 0) + 1, window).astype(F32)
        return (bdot(s / cnt - ug, pw, 1, 0) * scale,)

    return pool_fn


def conv4_fn(xr, w, b):
    return (jax.nn.silu(cconv(xr, w, SSM_CONV) + b),)


def cd1_fn(u, dww, dwb, scw):
    val, gate, bg, cg, hh = (u[:, k * LANE:(k + 1) * LANE] for k in range(5))
    v = val * jax.nn.sigmoid(gate)
    vc = cconv(v, dww, CONF_K) + dwb
    sc = bg * cconv(cg * hh, scw, SC_K)
    return vc, sc


def attn_fn(q, kv):
    outs = []
    for h in range(XA_HEADS):
        cols = slice(h * XA_DH, (h + 1) * XA_DH)
        s = bdot(q[:, cols], kv[:, cols], 1, 1) / math.sqrt(XA_DH)
        p = jax.nn.softmax(s, axis=-1)
        outs.append(bdot(p, kv[:, D + h * XA_DH:D + (h + 1) * XA_DH], 1, 0))
    return (jnp.concatenate(outs, axis=1),)


def ssd_chunk(xbc, z, dtraw, dtb, alog, dsk, nw, h0, h1, h2, h3, e64, e64t, ecat, ecatt, tril, trilt):
    xs, bm, cm = xbc[:, :SSM_GSZ], xbc[:, SSM_GSZ:SSM_GSZ + SSM_N], xbc[:, SSM_GSZ + SSM_N:]
    hin = (h0, h1, h2, h3)
    dt = jax.nn.softplus(dtraw + dtb)
    a = -jnp.exp(alog)
    d_a = dt * a
    cs = cmatl(tril, trilt, d_a)
    cs_cat = cmat(cs, ecat, ecatt)
    cs64, cs128 = cs_cat[:, :SSM_GSZ], cs_cat[:, SSM_GSZ:]
    dt64 = cmat(dt, e64, e64t)
    row = lax.broadcasted_iota(jnp.int32, (8, LANE), 0)
    heads = jnp.where(row == 0, dsk, jnp.where(row == 1, jnp.sum(d_a, axis=0, keepdims=True), 0.0))
    heads64 = cmat(heads, e64, e64t)
    d64, tot64 = heads64[0:1, :], heads64[1:2, :]
    xdt = xs * dt64
    cb = bdot(cm, bm, 1, 1)
    li = lax.broadcasted_iota(jnp.int32, (CHUNK, CHUNK), 0)
    si = lax.broadcasted_iota(jnp.int32, (CHUNK, CHUNK), 1)
    causal = li >= si
    lane = lax.broadcasted_iota(jnp.int32, (CHUNK, LANE), 1)
    xw = xdt * jnp.exp(tot64 - cs64)
    ecs = jnp.exp(cs64)
    etot = jnp.exp(tot64)
    ycols, hout = [], []
    for j in range(4):
        sl = slice(j * LANE, (j + 1) * LANE)
        xj = xdt[:, sl]
        ys = []
        for hh in range(2):
            r = 2 * j + hh
            col = cs128[:, r * LANE:(r + 1) * LANE]
            decay = jnp.exp(jnp.where(causal, col - col.T, -1e30))
            ys.append(bdot(cb * decay, xj, 1, 0))
        y_diag = jnp.where(lane < SSM_P, ys[0], ys[1])
        y_off = bdot(cm, hin[j], 1, 0) * ecs[:, sl]
        ycols.append(y_diag + y_off)
        hout.append(etot[:, sl] * hin[j] + bdot(bm, xw[:, sl], 0, 0))
    y = jnp.concatenate(ycols, axis=1) + d64 * xs
    y = y * jax.nn.silu(z)
    yn = y * lax.rsqrt(jnp.mean(y * y, axis=-1, keepdims=True) + RMS_EPS) * nw
    return (yn,) + tuple(hout)


def _xbc_group(a, axis):
    parts = []
    for g in range(SSM_GROUPS):
        for start, width in ((g * SSM_GSZ, SSM_GSZ), (SSM_INNER + g * SSM_N, SSM_N), (SSM_INNER + (SSM_GROUPS + g) * SSM_N, SSM_N)):
            parts.append(lax.slice_in_dim(a, start, start + width, axis=axis))
    return jnp.concatenate(parts, axis=axis)


def _xbc_ungroup(a, axis):
    xs, bs, cs = [], [], []
    for g in range(SSM_GROUPS):
        base = g * SSM_XBC_G
        xs.append(lax.slice_in_dim(a, base, base + SSM_GSZ, axis=axis))
        bs.append(lax.slice_in_dim(a, base + SSM_GSZ, base + SSM_GSZ + SSM_N, axis=axis))
        cs.append(lax.slice_in_dim(a, base + SSM_GSZ + SSM_N, base + SSM_XBC_G, axis=axis))
    return jnp.concatenate(xs + bs + cs, axis=axis)


def _ssd_consts():
    h = np.arange(LANE)[:, None]
    e64 = np.stack([(h == g * 8 + np.arange(SSM_GSZ)[None, :] // SSM_P) for g in range(SSM_GROUPS)]).astype(np.float32)
    e128 = np.stack([(h == g * 8 + np.arange(8 * LANE)[None, :] // LANE) for g in range(SSM_GROUPS)]).astype(np.float32)
    ecat = np.concatenate([e64, e128], axis=2)
    tril = np.tril(np.ones((CHUNK, CHUNK), np.float32))
    return tuple(jnp.asarray(c, dtype=BF) for c in (e64, e64.transpose(0, 2, 1), ecat, ecat.transpose(0, 2, 1), tril, tril.T))


def _ssd_specs(nc, rev):
    def ci(c):
        return nc - 1 - c if rev else c

    def row(width, col):
        return pl.BlockSpec((CHUNK, width), lambda b, c: (b * nc + ci(c), col))

    def whole(shape):
        return pl.BlockSpec(shape, lambda b, c: (0,) * len(shape))

    data = [row(SSM_CONV_DIM, 0),
            row(SSM_GSZ, 1), row(SSM_GSZ, 2), row(LANE, 24)]
    par = [whole((1, LANE))] * 3 + [whole((1, SSM_INNER))]
    cst = [whole((SSM_GROUPS, LANE, SSM_GSZ)), whole((SSM_GROUPS, SSM_GSZ, LANE)), whole((SSM_GROUPS, LANE, 12 * LANE)),
           whole((SSM_GROUPS, 12 * LANE, LANE)), whole((CHUNK, CHUNK)), whole((CHUNK, CHUNK))]
    hsave = pl.BlockSpec((None, None, SSM_GROUPS, 4, SSM_N, LANE), lambda b, c: (b, ci(c), 0, 0, 0, 0))
    return data, par, cst, hsave, row, whole


def _ssd_group_args(g, xbc, z, dtr, dtb, alog, dsk, nw):
    return (xbc[:, g * SSM_XBC_G:(g + 1) * SSM_XBC_G], z[g], dtr, dtb, alog, dsk, nw[:, g * SSM_GSZ:(g + 1) * SSM_GSZ])


def ssd_fwd(xbc_act, u, dtb, alog, dsk, nw, consts, bsz, seq):
    nc = seq // CHUNK
    data, par, cst, hsave, row, _ = _ssd_specs(nc, False)

    def body(xbc, z0, z1, dtr, dtb_r, alog_r, dsk_r, nw_r, e64, e64t, ecat, ecatt, tril, trilt, yn_ref, hs_ref, h):
        @pl.when(pl.program_id(1) == 0)
        def _():
            h[...] = jnp.zeros_like(h)

        hs_ref[...] = h[...]
        ys = []
        for g in range(SSM_GROUPS):
            args = _ssd_group_args(g, xbc[...], (z0[...], z1[...]), dtr[...], dtb_r[...], alog_r[...], dsk_r[...], nw_r[...])
            outs = ssd_chunk(*args, h[g, 0], h[g, 1], h[g, 2], h[g, 3], e64[g], e64t[g], ecat[g], ecatt[g], tril[...], trilt[...])
            ys.append(outs[0])
            for j in range(4):
                h[g, j] = outs[1 + j]
        yn_ref[...] = jnp.concatenate(ys, axis=1).astype(yn_ref.dtype)

    t = bsz * seq
    return pl.pallas_call(
        body, name="ssd_fwd", grid=(bsz, nc), in_specs=data + par + cst, out_specs=[row(SSM_INNER, 0), hsave],
        out_shape=[jax.ShapeDtypeStruct((t, SSM_INNER), BF), jax.ShapeDtypeStruct((bsz, nc, SSM_GROUPS, 4, SSM_N, LANE), F32)],
        scratch_shapes=[pltpu.VMEM((SSM_GROUPS, 4, SSM_N, LANE), F32)], compiler_params=_params(),
    )(xbc_act, u, u, u, dtb, alog, dsk, nw, *consts)


def ssd_bwd(xbc_act, u, dtb, alog, dsk, nw, consts, hs, dmix, bsz, seq):
    nc = seq // CHUNK
    data, par, cst, hsave, row, whole = _ssd_specs(nc, True)
    t = bsz * seq
    pcol = POOL_W // SSM_GSZ

    def body(xbc, z0, z1, dtr, dtb_r, alog_r, dsk_r, nw_r, e64, e64t, ecat, ecatt, tril, trilt, hs_ref, dy0, dy1,
             dxbc, dz, ddt, ddtb, dalog, ddsk, dnw, dh):
        @pl.when(pl.program_id(1) == 0)
        def _():
            dh[...] = jnp.zeros_like(dh)

        per_group = []
        for g, dyn in enumerate((dy0, dy1)):
            cst_vals = (e64[g], e64t[g], ecat[g], ecatt[g], tril[...], trilt[...])
            prim = _ssd_group_args(g, xbc[...], (z0[...], z1[...]), dtr[...], dtb_r[...], alog_r[...], dsk_r[...], nw_r[...])
            prim = prim + (hs_ref[g, 0], hs_ref[g, 1], hs_ref[g, 2], hs_ref[g, 3])
            _, vjp = jax.vjp(lambda *args, c=cst_vals: ssd_chunk(*args, *c), *prim)
            gr = vjp((dyn[...].astype(F32), dh[g, 0], dh[g, 1], dh[g, 2], dh[g, 3]))
            for j in range(4):
                dh[g, j] = gr[7 + j]
            per_group.append(gr)
        g0, g1 = per_group
        dxbc[...] = jnp.concatenate([g0[0], g1[0]], axis=1)
        dz[...] = jnp.concatenate([g0[1], g1[1]], axis=1).astype(dz.dtype)
        ddt[...] = g0[2] + g1[2]

        @pl.when(_first((0, 1)))
        def _():
            for r in (ddtb, dalog, ddsk, dnw):
                r[...] = jnp.zeros_like(r)

        ddtb[...] += g0[3] + g1[3]
        dalog[...] += g0[4] + g1[4]
        ddsk[...] += g0[5] + g1[5]
        dnw[...] += jnp.concatenate([g0[6], g1[6]], axis=1)

    out_specs = [row(SSM_CONV_DIM, 0), row(SSM_INNER, 0), row(LANE, 0), whole((1, LANE)), whole((1, LANE)), whole((1, LANE)),
                 whole((1, SSM_INNER))]
    lane = jax.ShapeDtypeStruct((1, LANE), F32)
    out_shape = [jax.ShapeDtypeStruct((t, SSM_CONV_DIM), F32), jax.ShapeDtypeStruct((t, SSM_INNER), BF),
                 jax.ShapeDtypeStruct((t, LANE), F32), lane, lane, lane, jax.ShapeDtypeStruct((1, SSM_INNER), F32)]
    return pl.pallas_call(
        body, name="ssd_bwd", grid=(bsz, nc), in_specs=data + par + cst + [hsave, row(SSM_GSZ, pcol), row(SSM_GSZ, pcol + 1)],
        out_specs=out_specs, out_shape=out_shape, scratch_shapes=[pltpu.VMEM((SSM_GROUPS, 4, SSM_N, LANE), F32)],
        compiler_params=_params(),
    )(xbc_act, u, u, u, dtb, alog, dsk, nw, *consts, hs, dmix, dmix)


TB = 1024


def _rows(d, col=0):
    return pl.BlockSpec((TB, d), lambda i: (i, col))


def _par(d):
    return pl.BlockSpec((1, d), lambda i: (0, 0))


def _sd(shape, dtype=F32):
    return jax.ShapeDtypeStruct(shape, dtype)


def _round_up(n, m):
    return -(-n // m) * m


def _pad_rows(a, rows):
    return jnp.pad(a, ((0, rows - a.shape[0]), (0, 0)))


def _pack128(arrs):
    flat = jnp.concatenate([a.reshape(-1) for a in arrs])
    n = flat.shape[0]
    rows = -(-n // (8 * LANE)) * 8
    return jnp.pad(flat, (0, rows * LANE - n)).reshape(rows, LANE)


def _unpack128(packed, shapes):
    flat = packed.reshape(-1)
    out, off = [], 0
    for s in shapes:
        n = int(np.prod(s))
        out.append(flat[off:off + n].reshape(s))
        off += n
    return out


def kernel(x, mem, norm_gains, xa_wq, xa_wkv, xa_wo, mlp_w1, mlp_w2, ab_w_in, pool_w, pool_scale, ssm_conv_w, ssm_conv_b, ssm_dt_bias, ssm_a_log, ssm_d, ssm_norm, ab_w_out, cd_w_in, conf_dw_w, conf_dw_b, conf_ln_g, conf_ln_b, sc_conv_w, cd_w_out, loss_target, m_norm_gains, m_xa_wq, m_xa_wkv, m_xa_wo, m_mlp_w1, m_mlp_w2, m_ab_w_in, m_pool_w, m_pool_scale, m_ssm_conv_w, m_ssm_conv_b, m_ssm_dt_bias, m_ssm_a_log, m_ssm_d, m_ssm_norm, m_ab_w_out, m_cd_w_in, m_conf_dw_w, m_conf_dw_b, m_conf_ln_g, m_conf_ln_b, m_sc_conv_w, m_cd_w_out, v_norm_gains, v_xa_wq, v_xa_wkv, v_xa_wo, v_mlp_w1, v_mlp_w2, v_ab_w_in, v_pool_w, v_pool_scale, v_ssm_conv_w, v_ssm_conv_b, v_ssm_dt_bias, v_ssm_a_log, v_ssm_d, v_ssm_norm, v_ab_w_out, v_cd_w_in, v_conf_dw_w, v_conf_dw_b, v_conf_ln_g, v_conf_ln_b, v_sc_conv_w, v_cd_w_out):
    args = locals()
    w = {n: args[n] for n in WEIGHTS}
    mom_m = {n: args["m_" + n] for n in WEIGHTS}
    mom_v = {n: args["v_" + n] for n in WEIGHTS}
    ex = Exchange(w)
    loss_local, grad_x, small_grads = local_step(x, mem, loss_target, ex)
    outs = {}

    started = ex.put_small(small_grads, loss_local)
    landed = {key: ex.landed(key, started) for key in ('l1', 'cd', 'l0')}
    late = []
    for n, keys in (('mlp_w1', ('l0', 'l1')), ('mlp_w2', ('l0', 'l1')), ('xa_wkv', ('l0', 'l1')), ('xa_wq', ('l0', 'l1')),
                    ('xa_wo', ('l0', 'l1')), ('cd_w_in', ('cd',)), ('cd_w_out', ('cd',))):
        lands = [landed[key][0] for key in keys]
        offs = [landed[key][1][(n, layer)] for layer, key in enumerate(keys)]
        outs[n] = update_from_slots(lands, offs, w[n], mom_m[n], mom_v[n], SHARD_AXIS[n] == 2, "update_" + n)
        late.append(outs[n][1])
    g_own, loss = ex.reduced_small(late)
    land_ab, offs_ab = ex.landed('ab', late)
    outs['ab_w_out'] = update_from_slots([land_ab], [offs_ab[('ab_w_out', 0)]], w['ab_w_out'], mom_m['ab_w_out'],
                                         mom_v['ab_w_out'], False, "update_ab_w_out")
    res = update_from_slots([land_ab], [offs_ab[('ab_w_in', 0)]], jnp.swapaxes(w['ab_w_in'], 1, 2), jnp.swapaxes(mom_m['ab_w_in'], 1, 2),
                            jnp.swapaxes(mom_v['ab_w_in'], 1, 2), False, "update_ab_w_in")
    outs['ab_w_in'] = tuple(jnp.swapaxes(r, 1, 2) for r in res)
    small = SMALL_SHARDED + REPLICATED
    upd = adamw_many([w[n] for n in small], [mom_m[n] for n in small], [mom_v[n] for n in small], [g_own[n] for n in small],
                     "adamw_small")
    for i, n in enumerate(small):
        outs[n] = (g_own[n], upd[0][i], upd[1][i], upd[2][i])
    return (loss, grad_x.reshape(x.shape), *[outs[n][0] for n in WEIGHTS], *[outs[n][1] for n in WEIGHTS],
            *[outs[n][2] for n in WEIGHTS], *[outs[n][3] for n in WEIGHTS])


G_AB = (('ab_w_in', 0), ('ab_w_out', 0))
G_L0 = (('xa_wq', 0), ('xa_wkv', 0), ('xa_wo', 0), ('mlp_w1', 0), ('mlp_w2', 0))
G_L1 = (('xa_wq', 1), ('xa_wkv', 1), ('xa_wo', 1), ('mlp_w1', 1), ('mlp_w2', 1))
G_CD = (('cd_w_in', 0), ('cd_w_out', 0))
GATHER_GROUPS = {'ab': G_AB[:1], 'l0a': G_AB[1:] + G_L0[:3], 'l0b': G_L0[3:], 'cd': G_CD, 'l1a': G_L1[:3], 'l1b': G_L1[3:]}
SHARD_AXIS = dict(BIG)
MEMBER_ROW_TILE = 64
FLAT_ROW_TILE = 128


def _members(group, w):
    out = []
    for n, layer in group:
        shp = w[n].shape[1:]
        if SHARD_AXIS[n] == 2:
            shp = (shp[1], shp[0])
        assert shp[1] == D, (n, shp)
        out.append((n, layer, shp, shp[0], _round_up(shp[0], MEMBER_ROW_TILE)))
    return out


def _group_rows(group, w):
    return _round_up(sum(m[4] for m in _members(group, w)), FLAT_ROW_TILE)


def _flat_shards(group, w):
    parts = []
    for n, layer, _, _, padded in _members(group, w):
        shard = w[n][layer].astype(BF)
        parts.append(_pad_rows(shard.T if SHARD_AXIS[n] == 2 else shard, padded))
    return _pad_rows(jnp.concatenate(parts, axis=0), _group_rows(group, w))


def _full_from_slots(land, group, w):
    out, off = {}, 0
    for n, layer, shp, rows, padded in _members(group, w):
        out[(n, layer)] = land[:, off:off + rows].reshape(N_DEV * rows, D)
        off += padded
    return out


def _slots_from_full(grads, group, w):
    parts = []
    for n, layer, shp, rows, padded in _members(group, w):
        blk = grads[(n, layer)].astype(BF).reshape(N_DEV, rows, D)
        parts.append(jnp.pad(blk, ((0, 0), (0, padded - rows), (0, 0))))
    send = jnp.concatenate(parts, axis=1)
    return jnp.pad(send, ((0, 0), (0, _group_rows(group, w) - send.shape[1]), (0, 0)))


_HBM = pl.BlockSpec(memory_space=pltpu.HBM)
_SEM = pl.BlockSpec(memory_space=pltpu.SEMAPHORE)
_ANY = pl.BlockSpec(memory_space=pl.ANY)


def _peer_copy(k, src, dst, send_sems, recv_sems, peer):
    return pltpu.make_async_remote_copy(src_ref=src, dst_ref=dst, send_sem=send_sems.at[k], recv_sem=recv_sems.at[k],
                                        device_id=peer, device_id_type=pl.DeviceIdType.MESH)


def exchange_start(src, name, scatter, after=()):
    shape = src.shape[-2:]
    after = list(after)

    def body(src_ref, land_ref, *rest):
        send_sems, recv_sems, token = rest[len(after)], rest[len(after) + 1], rest[-1]
        me = _me()
        for k, f in enumerate(_FLIPS):
            peer = _flip(me, f)
            piece = src_ref.at[_slot(peer)] if scatter else src_ref
            _peer_copy(k, piece, land_ref.at[_slot(me)], send_sems, recv_sems, peer).start()
        token[...] = jnp.zeros_like(token)

    land = pltpu.with_memory_space_constraint(lax.empty((N_DEV,) + shape, src.dtype), pltpu.HBM)
    return pl.pallas_call(
        body, name=name,
        out_shape=(pltpu.SemaphoreType.DMA((7,)), pltpu.SemaphoreType.DMA((7,)), pltpu.HBM(src.shape, src.dtype),
                   pltpu.HBM((N_DEV,) + shape, src.dtype), jax.ShapeDtypeStruct((8, LANE), F32)),
        in_specs=(_HBM, _HBM) + (_ANY,) * len(after), out_specs=(_SEM, _SEM, _HBM, _HBM, pl.BlockSpec(memory_space=pltpu.VMEM)),
        input_output_aliases={0: 2, 1: 3},
        compiler_params=pltpu.CompilerParams(has_side_effects=pltpu.SideEffectType.DATAFLOW_SIDE_EFFECTING),
    )(pltpu.with_memory_space_constraint(src, pltpu.HBM), land, *after)


def exchange_wait(handles, after, name, scatter):
    send_sems, recv_sems, src_thru, land_thru, _ = handles
    after = list(after) if isinstance(after, (list, tuple)) else [after]

    def body(src_ref, land_ref, send_sems, recv_sems, *rest):
        token = rest[-1]
        me = _me()
        for k, f in enumerate(_FLIPS):
            peer = _flip(me, f)
            piece = src_ref.at[_slot(peer)] if scatter else src_ref
            cp = _peer_copy(k, piece, land_ref.at[_slot(peer)], send_sems, recv_sems, peer)
            cp.wait_send()
            cp.wait_recv()
        token[...] = jnp.zeros_like(token)

    return pl.pallas_call(
        body, name=name, out_shape=(pltpu.HBM(src_thru.shape, src_thru.dtype), pltpu.HBM(land_thru.shape, land_thru.dtype),
                                    jax.ShapeDtypeStruct((8, LANE), F32)),
        in_specs=(_HBM, _HBM, _SEM, _SEM) + (_ANY,) * len(after), out_specs=(_HBM, _HBM, pl.BlockSpec(memory_space=pltpu.VMEM)),
        input_output_aliases={0: 0, 1: 1},
        compiler_params=pltpu.CompilerParams(has_side_effects=pltpu.SideEffectType.DATAFLOW_SIDE_EFFECTING),
    )(src_thru, land_thru, send_sems, recv_sems, *after)


class Exchange:
    def __init__(self, w):
        self.w = w
        self.me = _slot(_me())
        shapes = [w[n].shape for n in SMALL_SHARDED]
        gs = all_gather(_pack128([w[n] for n in SMALL_SHARDED]), "gather_small")
        per_dev = [_unpack128(gs[d], shapes) for d in range(N_DEV)]
        self.small = {n: jnp.concatenate([per_dev[d][i] for d in range(N_DEV)], axis=-1) for i, n in enumerate(SMALL_SHARDED)}
        self.small.update({n: w[n] for n in REPLICATED})
        first = all_gather(_flat_shards(GATHER_GROUPS['ab'], w), "gather_ab")
        self.first = _full_from_slots(first, GATHER_GROUPS['ab'], w)
        self.gathers, self.done, self.tokens, self.reductions = {}, {}, [], {}
        self.start_gather('l0a', after=[first])
        self.start_gather('l0b', after=[self.gathers['l0a'][4]])

    def take_tokens(self):
        toks, self.tokens = self.tokens, []
        return toks

    def start_gather(self, key, after=()):
        group = GATHER_GROUPS[key]
        self.gathers[key] = exchange_start(_flat_shards(group, self.w), f"gather_{key}_start", False, after=after)
        self.tokens.append(self.gathers[key][4])

    def weights(self, key, after):
        if key == 'ab':
            return self.first
        handles = self.gathers[key]
        _, land, self.done[key] = exchange_wait(handles, after, f"gather_{key}_wait", False)
        land = lax.dynamic_update_slice(land, handles[2][None], (self.me, 0, 0))
        return _full_from_slots(land, GATHER_GROUPS[key], self.w)

    def put_grads(self, key, group, grads):
        send = _slots_from_full(grads, group, self.w)
        handles = exchange_start(send, f"reduce_{key}_start", True)
        self.reductions[key] = (group, handles)
        self.tokens.append(handles[4])

    def landed(self, key, after):
        group, handles = self.reductions[key]
        send, land, _ = exchange_wait(handles, after, f"reduce_{key}_wait", True)
        mine = lax.dynamic_slice_in_dim(send, self.me, 1, axis=0)
        land = lax.dynamic_update_slice(land, mine, (self.me, 0, 0))
        offs, off = {}, 0
        for n, layer, _, _, padded in _members(group, self.w):
            offs[(n, layer)] = off
            off += padded
        return land, offs

    def put_small(self, small_grads, loss_local):
        small = SMALL_SHARDED + REPLICATED
        self.small_shapes = [small_grads[n].shape for n in small] + [(1,)]
        packed = _pack128([small_grads[n] for n in small] + [loss_local.reshape(1)])
        self.small_handles = exchange_start(packed, "gather_small_grads_start", False)
        return self.small_handles[4]

    def reduced_small(self, after):
        small = SMALL_SHARDED + REPLICATED
        src, land, _ = exchange_wait(self.small_handles, after, "gather_small_grads_wait", False)
        gs = lax.dynamic_update_slice(land, src[None], (self.me, 0, 0))
        tot = _unpack128(sum_slots(gs, "sum_small", 1024), self.small_shapes)
        out = {}
        for n, g in zip(small, tot):
            if n in SMALL_SHARDED:
                width = self.w[n].shape[-1]
                g = lax.dynamic_slice_in_dim(g, self.me * width, width, axis=g.ndim - 1)
            out[n] = g
        return out, tot[-1].reshape(())


def local_step(x, mem, target, ex):
    bsz, seq, _ = x.shape
    t = bsz * seq
    nb = t // TB
    nc = seq // CHUNK
    x0 = x.reshape(t, D)
    mem2 = mem.reshape(bsz * N_MEM, D)
    tgt = target.reshape(t, D)
    p = ex.small
    gains = p['norm_gains']
    big = {}

    def gain(layer, i):
        g = gains[layer, i].reshape(1, D)
        for tok in ex.take_tokens():
            g = g + tok[0, 0]
        return g

    consts = _ssd_consts()
    grads = {}
    saved = [dict(), dict()]

    def matmul_res(a, b, name, xin, ga, gb):
        return matmul(a, b, 'nn', name, (F32, F32, BF), epilogue=res_epilogue, extras=[xin], params=[ga, gb])

    def attn_specs():
        nq = seq // TB
        q = pl.BlockSpec((TB, D), lambda b, i: (b * nq + i, 0))
        kv = pl.BlockSpec((N_MEM, 2 * D), lambda b, i: (b, 0))
        return (bsz, nq), q, kv

    def attention_fwd(layer, xin, hin, sv, ga, gb):
        q = matmul(hin, big[('xa_wq', layer)], 'nn', f"q_{layer}", BF)
        kv = matmul(mem2, big[('xa_wkv', layer)], 'nt', f"kv_{layer}", BF)
        grid, qs, kvs = attn_specs()
        o, = fwd_call(attn_fn, f"attn_{layer}", grid, [q, kv], [qs, kvs], [_sd((t, D), BF)], [qs])
        ao, x_next, h_next = matmul_res(o, big[('xa_wo', layer)], f"ao_{layer}", xin, ga, gb)
        sv.update(q=q, kv=kv, o=o, ao=ao)
        return ao, x_next, h_next

    def mlp_fwd(layer, hin, sv, res):
        r, rr = matmul(hin, big[('mlp_w1', layer)], 'nt', f"mlp1_{layer}", (BF, BF), epilogue=act_epilogue)
        out = matmul_res(rr, big[('mlp_w2', layer)], f"mlp2_{layer}", *res)
        sv.update(r=r, rr=rr, mo=out[0])
        return out

    sv = saved[0]
    h0, = fwd_call(seg_in, "norm_in", (nb,), [x0, gain(0, 0)], [_rows(D), _par(D)], [_sd((t, D), BF)], [_rows(D)])
    big.update(ex.weights('ab', h0))
    xbc0 = POOL_W + SSM_INNER
    w_ab_in = big[('ab_w_in', 0)]
    w_ab_in = _pad_rows(jnp.concatenate([w_ab_in[:xbc0], _xbc_group(w_ab_in[xbc0:xbc0 + SSM_CONV_DIM], 0),
                                         w_ab_in[xbc0 + SSM_CONV_DIM:]], axis=0), AB_IN_PAD)
    conv_w, conv_b = _xbc_group(p['ssm_conv_w'][0], 1), _xbc_group(p['ssm_conv_b'], 1)
    u0 = matmul(h0, w_ab_in, 'nt', "ab_in")
    pool_outs = []
    for g in range(POOL_GROUPS):
        seqspec = pl.BlockSpec((seq, PG), lambda b, g=g: (b, g))
        po, = fwd_call(make_pool_fn(g), f"pool_{g}", (bsz,), [u0, p['pool_w'][0, g], p['pool_scale']],
                       [seqspec, pl.BlockSpec((PG, PG), lambda b: (0, 0)), pl.BlockSpec((1, PG), lambda b, g=g: (0, g))],
                       [_sd((t, PG), BF)], [pl.BlockSpec((seq, PG), lambda b: (b, 0))])
        pool_outs.append(po)
    cw = 256
    ncb = SSM_CONV_DIM // cw
    cbase = (POOL_W + SSM_INNER) // cw
    conv_in_specs = [pl.BlockSpec((seq, cw), lambda j, b: (b, cbase + j)), pl.BlockSpec((SSM_CONV, cw), lambda j, b: (0, j)),
                     pl.BlockSpec((1, cw), lambda j, b: (0, j))]
    conv_out_spec = pl.BlockSpec((seq, cw), lambda j, b: (b, j))
    xbc_act, = fwd_call(conv4_fn, "ssm_conv", (ncb, bsz), [u0, conv_w, conv_b], conv_in_specs,
                        [_sd((t, SSM_CONV_DIM))], [conv_out_spec])
    dtb = jnp.pad(p['ssm_dt_bias'], ((0, 0), (0, LANE - SSM_HEADS)))
    alog = jnp.pad(p['ssm_a_log'], ((0, 0), (0, LANE - SSM_HEADS)))
    dsk = jnp.pad(p['ssm_d'], ((0, 0), (0, LANE - SSM_HEADS)))
    yn, hs = ssd_fwd(xbc_act, u0, dtb, alog, dsk, p['ssm_norm'], consts, bsz, seq)
    mix0 = jnp.concatenate(pool_outs + [yn], axis=1)
    big.update(ex.weights('l0a', yn))
    ex.start_gather('cd', after=[ex.done['l0a']])
    ex.start_gather('l1a', after=[ex.gathers['cd'][4]])
    ex.start_gather('l1b', after=[ex.gathers['l1a'][4]])
    m0, x1, h2 = matmul_res(mix0, big[('ab_w_out', 0)], "ab_out", x0, gain(0, 1), gain(0, 2))
    ao0, x2, h3 = attention_fwd(0, x1, h2, sv, gain(0, 3), gain(0, 4))
    big.update(ex.weights('l0b', h3))
    mo0, x3, h4 = mlp_fwd(0, h3, sv, (x2, gain(0, 5), gain(1, 0)))
    big.update(ex.weights('cd', mo0))

    sv1 = saved[1]
    nd = D // LANE
    w_cd_in = big[('cd_w_in', 0)].reshape(5, nd, LANE, D).transpose(1, 0, 2, 3).reshape(CD_IN, D)
    u1 = matmul(h4, w_cd_in, 'nt', "cd_in")
    cd_par = [pl.BlockSpec((CONF_K, LANE), lambda j, b: (0, j)), pl.BlockSpec((1, LANE), lambda j, b: (0, j)),
              pl.BlockSpec((SC_K, LANE), lambda j, b: (0, j))]
    cd_ins = [u1, p['conf_dw_w'][0], p['conf_dw_b'], p['sc_conv_w'][0]]
    cd_u_spec = pl.BlockSpec((seq, 5 * LANE), lambda j, b: (b, j))
    cd_in_specs = [cd_u_spec] + cd_par
    cd_out_spec = pl.BlockSpec((seq, LANE), lambda j, b: (b, j))
    vconv, mix1 = fwd_call(cd1_fn, "cd_conv", (nd, bsz), cd_ins, cd_in_specs, [_sd((t, D)), _sd((t, CD_OUT), BF)],
                           [cd_out_spec, pl.BlockSpec((seq, LANE), lambda j, b: (b, nd + j))])
    mix1, = fwd_call(seg_ln, "conf_ln", (nb,), [vconv, p['conf_ln_g'], p['conf_ln_b']], [_rows(D), _par(D), _par(D)],
                     [_sd((t, CD_OUT), BF)], [_rows(D)], into=mix1)
    m1, x4, h5 = matmul_res(mix1, big[('cd_w_out', 0)], "cd_out", x3, gain(1, 1), gain(1, 2))
    big.update(ex.weights('l1a', h5))
    ao1, x5, h6 = attention_fwd(1, x4, h5, sv1, gain(1, 3), gain(1, 4))
    big.update(ex.weights('l1b', h6))
    r1, rr1 = matmul(h6, big[('mlp_w1', 1)], 'nt', "mlp1_1", (BF, BF), epilogue=act_epilogue)
    sv1.update(r=r1, rr=rr1)
    dx5, dmo1, dg15, lanes = matmul(rr1, big[('mlp_w2', 1)], 'nn', "mlp2_1", (F32, BF), epilogue=loss_epilogue, extras=[x5, tgt],
                                    params=[gain(1, 5)], n_acc=2)
    loss = 0.5 * jnp.sum(lanes) / float(D)

    gain_grads = {(1, 5): dg15}

    def matmul_res_bwd(a, b, mode, name, xin, m, ga, gb, dx1):
        return list(matmul(a, b, mode, name, (F32, BF), epilogue=res_bwd_epilogue, extras=[xin, m, dx1], params=[ga, gb], n_acc=2))

    def mlp_bwd(layer, hin, dmo, sv, res):
        grads_w2 = matmul(sv['rr'], dmo, 'tn', f"d_mlp_w2_{layer}", BF)
        dr, = matmul(dmo, big[('mlp_w2', layer)], 'nt', f"d_r_{layer}", (BF,), epilogue=act_bwd_epilogue, extras=[sv['r']])
        grads_w1 = matmul(dr, hin, 'tn', f"d_mlp_w1_{layer}", BF)
        return matmul_res_bwd(dr, big[('mlp_w1', layer)], 'nn', f"d_h_mlp_{layer}", *res) + [grads_w1, grads_w2]

    def attention_bwd(layer, hin, dao, sv, res):
        g_wo = matmul(sv['o'], dao, 'tn', f"d_xa_wo_{layer}", BF)
        do = matmul(dao, big[('xa_wo', layer)], 'nt', f"d_o_{layer}", BF)
        grid, qs, kvs = attn_specs()
        dq, dkv = bwd_call(attn_fn, f"d_attn_{layer}", grid, [sv['q'], sv['kv']], [qs, kvs], [do], [qs], [0, 1],
                           [_sd((t, D), BF), _sd((bsz * N_MEM, 2 * D))], [qs, kvs], [None, (1,)])
        g_wkv = matmul(dkv, mem2, 'tn', f"d_xa_wkv_{layer}", BF)
        g_wq = matmul(hin, dq, 'tn', f"d_xa_wq_{layer}", BF)
        return matmul_res_bwd(dq, big[('xa_wq', layer)], 'nt', f"d_h_attn_{layer}", *res) + [g_wq, g_wkv, g_wo]

    per_layer = {k: [None, None] for k in ('xa_wq', 'xa_wkv', 'xa_wo', 'mlp_w1', 'mlp_w2')}

    (dx4, dao1, gain_grads[(1, 3)], gain_grads[(1, 4)], per_layer['mlp_w1'][1],
     per_layer['mlp_w2'][1]) = mlp_bwd(1, h6, dmo1, sv1, (x4, ao1, gain(1, 3), gain(1, 4), dx5))
    (dx3, dm1, gain_grads[(1, 1)], gain_grads[(1, 2)], per_layer['xa_wq'][1], per_layer['xa_wkv'][1],
     per_layer['xa_wo'][1]) = attention_bwd(1, h5, dao1, sv1, (x3, m1, gain(1, 1), gain(1, 2), dx4))
    ex.put_grads('l1', G_L1, {(k, 1): v[1] for k, v in per_layer.items()})
    g_cd_out = matmul(mix1, dm1, 'tn', "d_cd_w_out", BF)
    dmix1 = matmul(dm1, big[('cd_w_out', 0)], 'nt', "d_mix1", after=ex.take_tokens())
    dvconv, dlg, dlb = bwd_call(seg_ln, "d_conf_ln", (nb,), [vconv, p['conf_ln_g'], p['conf_ln_b']],
                                [_rows(D), _par(D), _par(D)], [dmix1], [_rows(D, 0)], [0, 1, 2],
                                [_sd((t, D)), _sd((1, D)), _sd((1, D))], [_rows(D), _par(D), _par(D)], [None, (0,), (0,)])
    grads['conf_ln_g'], grads['conf_ln_b'] = dlg, dlb
    cd_g = bwd_call(cd1_fn, "d_cd_conv", (nd, bsz), cd_ins, cd_in_specs, [dvconv, dmix1],
                    [cd_out_spec, pl.BlockSpec((seq, LANE), lambda j, b: (b, nd + j))], list(range(4)),
                    [_sd((t, CD_IN), BF), _sd((CONF_K, D)), _sd((1, D)), _sd((SC_K, D))], [cd_u_spec] + cd_par,
                    [None, (1,), (1,), (1,)])
    du1 = cd_g[0]
    grads['conf_dw_w'], grads['conf_dw_b'], grads['sc_conv_w'] = cd_g[1][None], cd_g[2], cd_g[3][None]
    g_cd_in = matmul(du1, h4, 'tn', "d_cd_w_in", BF).reshape(nd, 5, LANE, D).transpose(1, 0, 2, 3).reshape(CD_IN, D)
    ex.put_grads('cd', G_CD, {('cd_w_in', 0): g_cd_in, ('cd_w_out', 0): g_cd_out})
    dx2, dmo0, gain_grads[(0, 5)], gain_grads[(1, 0)] = matmul_res_bwd(du1, w_cd_in, 'nn', "d_h_cd", x2, mo0, gain(0, 5),
                                                                       gain(1, 0), dx3)
    (dx1, dao0, gain_grads[(0, 3)], gain_grads[(0, 4)], per_layer['mlp_w1'][0],
     per_layer['mlp_w2'][0]) = mlp_bwd(0, h3, dmo0, sv, (x1, ao0, gain(0, 3), gain(0, 4), dx2))
    (dx0r, dm0, gain_grads[(0, 1)], gain_grads[(0, 2)], per_layer['xa_wq'][0], per_layer['xa_wkv'][0],
     per_layer['xa_wo'][0]) = attention_bwd(0, h2, dao0, sv, (x0, m0, gain(0, 1), gain(0, 2), dx1))
    ex.put_grads('l0', G_L0, {(k, 0): v[0] for k, v in per_layer.items()})
    g_ab_out = matmul(mix0, dm0, 'tn', "d_ab_w_out", BF)
    dmix0 = matmul(dm0, big[('ab_w_out', 0)], 'nt', "d_mix0", after=ex.take_tokens())
    dxbc_act, dz, ddt, ddtb, dalog, ddsk, dnw = ssd_bwd(xbc_act, u0, dtb, alog, dsk, p['ssm_norm'], consts, hs, dmix0, bsz, seq)
    grads['ssm_dt_bias'] = ddtb[:, :SSM_HEADS]
    grads['ssm_a_log'] = dalog[:, :SSM_HEADS]
    grads['ssm_d'] = ddsk[:, :SSM_HEADS]
    grads['ssm_norm'] = dnw
    dxr, dcw, dcb = bwd_call(conv4_fn, "d_ssm_conv", (ncb, bsz), [u0, conv_w, conv_b], conv_in_specs,
                             [dxbc_act], [conv_out_spec], [0, 1, 2],
                             [_sd((t, SSM_CONV_DIM), BF), _sd((SSM_CONV, SSM_CONV_DIM)), _sd((1, SSM_CONV_DIM))],
                             [conv_out_spec, conv_in_specs[1], conv_in_specs[2]], [None, (1,), (1,)])
    grads['ssm_conv_w'], grads['ssm_conv_b'] = _xbc_ungroup(dcw, 1)[None], _xbc_ungroup(dcb, 1)
    dpool, dpw, dps = [], [], []
    for g in range(POOL_GROUPS):
        seqspec = pl.BlockSpec((seq, PG), lambda b, g=g: (b, g))
        one = pl.BlockSpec((seq, PG), lambda b: (b, 0))
        wspec = pl.BlockSpec((PG, PG), lambda b: (0, 0))
        sspec = pl.BlockSpec((1, PG), lambda b, g=g: (0, g))
        a, bb, c = bwd_call(make_pool_fn(g), f"d_pool_{g}", (bsz,), [u0, p['pool_w'][0, g], p['pool_scale']],
                            [seqspec, wspec, sspec], [dmix0], [seqspec], [0, 1, 2],
                            [_sd((t, PG), BF), _sd((PG, PG)), _sd((1, PG))], [one, wspec, pl.BlockSpec((1, PG), lambda b: (0, 0))],
                            [None, (0,), (0,)])
        dpool.append(a)
        dpw.append(bb)
        dps.append(c)
    grads['pool_w'] = jnp.stack(dpw)[None]
    grads['pool_scale'] = jnp.concatenate(dps, axis=1)
    du0 = jnp.concatenate(dpool + [dz, dxr, ddt.astype(BF)], axis=1)
    g_ab_in = matmul(du0, h0, 'tn', "d_ab_w_in", BF)
    g_ab_in = jnp.concatenate([g_ab_in[:xbc0], _xbc_ungroup(g_ab_in[xbc0:xbc0 + SSM_CONV_DIM], 0),
                               g_ab_in[xbc0 + SSM_CONV_DIM:AB_IN]], axis=0)
    ex.put_grads('ab', G_AB, {('ab_w_in', 0): g_ab_in, ('ab_w_out', 0): g_ab_out})
    dx, dg00 = matmul(du0, w_ab_in, 'nn', "d_h_ab", (F32,), epilogue=in_bwd_epilogue, extras=[x0, dx0r], params=[gain(0, 0)],
                      after=ex.take_tokens(), n_acc=1)
    gain_grads[(0, 0)] = dg00
    grads['norm_gains'] = jnp.stack([jnp.concatenate([gain_grads[(l, i)] for i in range(6)], axis=0) for l in range(2)])
    return loss, dx, grads
```

```python
import functools
import math

import numpy as np
import jax
import jax.numpy as jnp
from jax import lax
from jax.experimental import pallas as pl
from jax.experimental.pallas import tpu as pltpu

BF = jnp.bfloat16
F32 = jnp.float32

N_DEV = 8
D = 1024
N_MEM = 256
XA_HEADS = 4
XA_DH = D // XA_HEADS
POOL_GROUPS = 4
PG = 128
POOL_W = POOL_GROUPS * PG
SSM_INNER = 1024
SSM_GROUPS = 2
SSM_GSZ = SSM_INNER // SSM_GROUPS
SSM_HEADS = 16
SSM_P = 64
SSM_N = 128
SSM_CONV = 4
SSM_CONV_DIM = SSM_INNER + 2 * SSM_GROUPS * SSM_N
SSM_XBC_G = SSM_GSZ + 2 * SSM_N
CHUNK = 128
AB_IN = POOL_W + SSM_INNER + SSM_CONV_DIM + SSM_HEADS
AB_IN_PAD = POOL_W + SSM_INNER + SSM_CONV_DIM + 128
AB_OUT = POOL_W + SSM_INNER
CONF_K = 31
SC_K = 3
CD_IN = 5 * D
CD_OUT = 2 * D
MLP_H = 4 * D
RMS_EPS = 1e-6
LN_EPS = 1e-5
ADAM_LR = 0.001
ADAM_B1 = 0.9
ADAM_B2 = 0.999
ADAM_EPS = 1e-08
ADAM_WD = 0.01
ADAM_STEP = 10
VMEM_LIMIT = 56 * 1024 * 1024
LANE = 128

NAMES = ['x', 'mem', 'norm_gains', 'xa_wq', 'xa_wkv', 'xa_wo', 'mlp_w1', 'mlp_w2', 'ab_w_in', 'pool_w', 'pool_scale',
         'ssm_conv_w', 'ssm_conv_b', 'ssm_dt_bias', 'ssm_a_log', 'ssm_d', 'ssm_norm', 'ab_w_out', 'cd_w_in', 'conf_dw_w',
         'conf_dw_b', 'conf_ln_g', 'conf_ln_b', 'sc_conv_w', 'cd_w_out', 'loss_target']
WEIGHTS = NAMES[2:25]
BIG = [('xa_wq', 1), ('xa_wkv', 2), ('xa_wo', 1), ('mlp_w1', 2), ('mlp_w2', 1), ('cd_w_in', 2), ('cd_w_out', 1),
       ('ab_w_out', 1), ('ab_w_in', 2)]
SMALL_SHARDED = ['norm_gains', 'ssm_conv_w', 'conf_dw_w', 'conf_dw_b', 'conf_ln_g', 'conf_ln_b', 'sc_conv_w']
REPLICATED = ['pool_w', 'pool_scale', 'ssm_conv_b', 'ssm_dt_bias', 'ssm_a_log', 'ssm_d', 'ssm_norm']


def _dg(a, b, ca, cb, prec=None):
    return lax.dot_general(a, b, (((ca,), (cb,)), ((), ())), precision=prec, preferred_element_type=F32)


@functools.partial(jax.custom_vjp, nondiff_argnums=(2, 3))
def bdot(a, b, ca, cb):
    return _dg(a.astype(BF), b.astype(BF), ca, cb)


def _bdot_fwd(a, b, ca, cb):
    return bdot(a, b, ca, cb), (a, b)


def _bdot_bwd(ca, cb, res, g):
    a, b = res
    g16, a16, b16 = g.astype(BF), a.astype(BF), b.astype(BF)
    da = _dg(g16, b16, 1, 1 - cb) if ca == 1 else _dg(b16, g16, 1 - cb, 1)
    db = _dg(g16, a16, 0, 1 - ca) if cb == 1 else _dg(a16, g16, 1 - ca, 0)
    return da.astype(a.dtype), db.astype(b.dtype)


bdot.defvjp(_bdot_fwd, _bdot_bwd)


def _split3(a):
    a1 = a.astype(BF)
    r1 = a - a1.astype(F32)
    a2 = r1.astype(BF)
    a3 = (r1 - a2.astype(F32)).astype(BF)
    return a1, a2, a3


def _exact_right(a, c):
    m = a.shape[0]
    if m % 16:
        return sum(_dg(p, c, 1, 0) for p in _split3(a))
    o = _dg(jnp.concatenate(_split3(a), axis=0), c, 1, 0)
    return o[:m] + o[m:2 * m] + o[2 * m:]


def _exact_left(c, a):
    n = a.shape[1]
    o = _dg(c, jnp.concatenate(_split3(a), axis=1), 1, 0)
    return o[:, :n] + o[:, n:2 * n] + o[:, 2 * n:]


@jax.custom_vjp
def cmat(a, c, ct):
    return _exact_right(a, c)


def _cmat_fwd(a, c, ct):
    return cmat(a, c, ct), (c, ct)


def _cmat_bwd(res, g):
    c, ct = res
    return _exact_right(g, ct), jnp.zeros_like(c), jnp.zeros_like(ct)


cmat.defvjp(_cmat_fwd, _cmat_bwd)


@jax.custom_vjp
def cmatl(c, ct, a):
    return _exact_left(c, a)


def _cmatl_fwd(c, ct, a):
    return cmatl(c, ct, a), (c, ct)


def _cmatl_bwd(res, g):
    c, ct = res
    return jnp.zeros_like(c), jnp.zeros_like(ct), _exact_left(ct, g)


cmatl.defvjp(_cmatl_fwd, _cmatl_bwd)


SUBLANES = 8


def _taps(x, shifts, down):
    n, c = x.shape
    pad = _round_up(max(shifts), SUBLANES)
    if pad == 0:
        return {0: x}
    zeros = jnp.zeros((pad, c), x.dtype)
    xp = jnp.concatenate([zeros, x] if down else [x, zeros], axis=0)
    rolled, out = {0: xp}, {}
    for s in shifts:
        a, b = divmod(s, SUBLANES)
        if b not in rolled:
            rolled[b] = pltpu.roll(xp, b if down else n + pad - b, 0)
        off = pad - SUBLANES * a if down else SUBLANES * a
        out[s] = rolled[b][off:off + n]
    return out


def _shift_down(x, k):
    return _taps(x, [k], True)[k]


def _shift_up(x, k):
    return _taps(x, [k], False)[k]


@functools.partial(jax.custom_vjp, nondiff_argnums=(1,))
def shift(x, k):
    return _shift_down(x, k)


def _shift_fwd(x, k):
    return _shift_down(x, k), None


def _shift_bwd(k, _, g):
    return (_shift_up(g, k),)


shift.defvjp(_shift_fwd, _shift_bwd)


@functools.partial(jax.custom_vjp, nondiff_argnums=(2,))
def cconv(u, w, width):
    taps = _taps(u, list(range(width)), True)
    acc = u * w[width - 1:width, :]
    for k in range(width - 1):
        acc = acc + taps[width - 1 - k] * w[k:k + 1, :]
    return acc


def _cconv_fwd(u, w, width):
    return cconv(u, w, width), (u, w)


def _cconv_bwd(width, res, g):
    u, w = res
    rows = lax.broadcasted_iota(jnp.int32, w.shape, 0)
    du = g * w[width - 1:width, :]
    dw = jnp.where(rows == width - 1, jnp.sum(g * u, axis=0, keepdims=True), 0.0)
    g_taps = _taps(g, list(range(width)), False)
    u_taps = _taps(u, list(range(width)), True)
    for k in range(width - 1):
        s = width - 1 - k
        du = du + g_taps[s] * w[k:k + 1, :]
        dw = dw + jnp.where(rows == k, jnp.sum(g * u_taps[s], axis=0, keepdims=True), 0.0)
    return du, dw


cconv.defvjp(_cconv_fwd, _cconv_bwd)


def _rms(x, g):
    return x * lax.rsqrt(jnp.mean(x * x, axis=-1, keepdims=True) + RMS_EPS) * g


def _rms_bwd(v, g, dout):
    r = lax.rsqrt(jnp.mean(v * v, axis=-1, keepdims=True) + RMS_EPS)
    n = v * r
    dn = dout * g
    dv = (dn - n * jnp.mean(dn * n, axis=-1, keepdims=True)) * r
    return dv, jnp.sum(dout * n, axis=0, keepdims=True)


def _params(sem=None):
    return pltpu.CompilerParams(dimension_semantics=sem, vmem_limit_bytes=VMEM_LIMIT)


def _f32(v):
    return v if v.dtype == F32 else v.astype(F32)


def _first(axes):
    ok = None
    for ax in axes:
        c = pl.program_id(ax) == 0
        ok = c if ok is None else jnp.logical_and(ok, c)
    return ok


def fwd_call(fn, name, grid, ins, in_specs, out_shapes, out_specs, into=None):
    n_in = len(ins)
    n_into = 0 if into is None else 1

    def body(*refs):
        outs = fn(*[_f32(r[...]) for r in refs[:n_in]])
        for r, o in zip(refs[n_in + n_into:], outs):
            r[...] = o.astype(r.dtype)

    extra = [] if into is None else [into]
    return pl.pallas_call(body, name=name, grid=grid, in_specs=list(in_specs) + [pl.BlockSpec(memory_space=pl.ANY)] * n_into,
                          out_specs=out_specs, out_shape=out_shapes, input_output_aliases={n_in: 0} if n_into else {},
                          compiler_params=_params())(*ins, *extra)


def bwd_call(fn, name, grid, ins, in_specs, cots, cot_specs, gidx, g_shapes, g_specs, g_acc):
    n_in, n_cot = len(ins), len(cots)

    def body(*refs):
        vals = [_f32(r[...]) for r in refs[:n_in]]

        def f_sel(*dv):
            full = list(vals)
            for i, v in zip(gidx, dv):
                full[i] = v
            return tuple(fn(*full))

        outs, vjp = jax.vjp(f_sel, *[vals[i] for i in gidx])
        cts = tuple(_f32(r[...]) for r in refs[n_in:n_in + n_cot])
        grads = vjp(cts)
        for r, g, acc in zip(refs[n_in + n_cot:], grads, g_acc):
            if acc is None:
                r[...] = g.astype(r.dtype)
            else:
                @pl.when(_first(acc))
                def _():
                    r[...] = jnp.zeros_like(r)

                r[...] += g.astype(r.dtype)

    return pl.pallas_call(body, name=name, grid=grid, in_specs=list(in_specs) + list(cot_specs), out_specs=g_specs,
                          out_shape=g_shapes, compiler_params=_params())(*ins, *cots)


def _tile(dim, pref):
    if dim <= pref:
        return dim
    best = None
    for t in range(LANE, pref + 1, LANE):
        if dim % t == 0:
            best = t
    assert best is not None, dim
    return best


MATMUL_VMEM_BUDGET = 40 * 1024 * 1024


def _matmul_tiles(m, n, k, a_bytes, b_bytes, out_bytes):
    tn = _tile(n, 1024)
    for tk_pref in (k, 2048, 1024, 512):
        tk = _tile(k, tk_pref)
        for tm_pref in (1024, 512, 256):
            tm = _tile(m, tm_pref)
            need = 2 * (tm * tk * a_bytes + tk * tn * b_bytes + tm * tn * out_bytes) + (0 if tk == k else tm * tn * 4)
            need += (tm * tk * 2 if a_bytes == 4 else 0) + (tk * tn * 2 if b_bytes == 4 else 0)
            if need <= MATMUL_VMEM_BUDGET:
                return tm, tn, tk
    raise ValueError((m, n, k))


def matmul(a, b, mode, name, out_dtype=F32, epilogue=None, extras=(), params=(), after=(), n_acc=0):
    if mode == 'nn':
        (m, k), (k2, n) = a.shape, b.shape
    elif mode == 'nt':
        (m, k), (n, k2) = a.shape, b.shape
    else:
        (k, m), (k2, n) = a.shape, b.shape
    assert k == k2, (name, a.shape, b.shape)
    n_extra = len(extras) + len(params)
    out_dtypes = out_dtype if isinstance(out_dtype, tuple) else (out_dtype,)
    per_out = sum(jnp.dtype(dt).itemsize for dt in out_dtypes) + sum(e.dtype.itemsize for e in extras)
    tm, tn, tk = _matmul_tiles(m, n, k, a.dtype.itemsize, b.dtype.itemsize, per_out)
    nk = k // tk
    ca = 0 if mode == 'tn' else 1
    cb = 1 if mode == 'nt' else 0
    a_spec = pl.BlockSpec((tk, tm), lambda i, j, kk: (kk, i)) if mode == 'tn' else pl.BlockSpec((tm, tk), lambda i, j, kk: (i, kk))
    b_spec = pl.BlockSpec((tn, tk), lambda i, j, kk: (j, kk)) if mode == 'nt' else pl.BlockSpec((tk, tn), lambda i, j, kk: (kk, j))

    def finish(o_refs, extra_refs, acc, first_row_tile):
        outs = (acc,) if epilogue is None else epilogue(acc, *[_f32(e[...]) for e in extra_refs])
        n_tile = len(o_refs) - n_acc
        for o_ref, o in zip(o_refs[:n_tile], outs[:n_tile]):
            o_ref[...] = o.astype(o_ref.dtype)
        for o_ref, o in zip(o_refs[n_tile:], outs[n_tile:]):
            o_ref[...] = jnp.where(first_row_tile, o, o_ref[...] + o)

    n_after = len(after)

    def body_whole_k(a_ref, b_ref, *refs):
        refs = refs[n_after:]
        finish(refs[n_extra:], refs[:n_extra], _dg(a_ref[...].astype(BF), b_ref[...].astype(BF), ca, cb), pl.program_id(0) == 0)

    def body_split_k(a_ref, b_ref, *refs):
        refs = refs[n_after:]
        extra_refs, o_refs, acc = refs[:n_extra], refs[n_extra:-1], refs[-1]
        kk = pl.program_id(2)
        first_row_tile = pl.program_id(0) == 0

        @pl.when(kk == 0)
        def _():
            acc[...] = jnp.zeros_like(acc)

        acc[...] += _dg(a_ref[...].astype(BF), b_ref[...].astype(BF), ca, cb)

        @pl.when(kk == nk - 1)
        def _():
            finish(o_refs, extra_refs, acc[...], first_row_tile)

    tile = pl.BlockSpec((tm, tn), lambda i, j, kk: (i, j))
    row = pl.BlockSpec((1, tn), lambda i, j, kk: (0, j))
    n_par = len(params)
    outs = pl.pallas_call(
        body_whole_k if nk == 1 else body_split_k, name=name, grid=(m // tm, n // tn, nk),
        in_specs=[a_spec, b_spec] + [pl.BlockSpec(memory_space=pl.ANY)] * n_after + [tile] * len(extras) + [row] * n_par,
        out_specs=[tile] * len(out_dtypes) + [row] * n_acc,
        out_shape=[jax.ShapeDtypeStruct((m, n), dt) for dt in out_dtypes] + [jax.ShapeDtypeStruct((1, n), F32)] * n_acc,
        scratch_shapes=[] if nk == 1 else [pltpu.VMEM((tm, tn), F32)],
        compiler_params=_params(("arbitrary",) * 3 if n_acc else ("parallel", "parallel", "arbitrary")))(a, b, *after, *extras, *params)
    return outs if isinstance(out_dtype, tuple) or n_acc else outs[0]


_FLIPS = [(0, 0, 1), (1, 0, 0), (0, 1, 0), (1, 1, 0), (1, 0, 1), (0, 1, 1), (1, 1, 1)]


def _me():
    return lax.axis_index("x"), lax.axis_index("y"), lax.axis_index("c")


def _flip(pos, f):
    return tuple(jnp.where(fi == 1, 1 - p, p) if fi else p for p, fi in zip(pos, f))


def _slot(pos):
    return 4 * pos[0] + 2 * pos[1] + pos[2]


def all_gather(vs, name):
    n = len(vs)

    def body(*refs):
        v_refs, out_refs, (send_sems, recv_sems, local_sems) = refs[:n], refs[n:2 * n], refs[2 * n:]
        me = _me()
        sibling = _flip(me, (0, 0, 1))
        chips = [_flip(me, f) for f in ((1, 0, 0), (0, 1, 0), (1, 1, 0))]

        def copy(a, k, block, to, src=None):
            out_ref = out_refs[a]
            return pltpu.make_async_remote_copy(
                src_ref=out_ref.at[_slot(block)] if src is None else src, dst_ref=out_ref.at[_slot(block)],
                send_sem=send_sems.at[7 * a + k], recv_sem=recv_sems.at[7 * a + k], device_id=to,
                device_id_type=pl.DeviceIdType.MESH)

        mine = [pltpu.make_async_copy(v_refs[a], out_refs[a].at[_slot(me)], local_sems.at[a]) for a in range(n)]
        for cp in mine:
            cp.start()
        first = []
        for a in range(n):
            first += [copy(a, 0, me, sibling, src=v_refs[a])]
            first += [copy(a, 1 + j, me, chip, src=v_refs[a]) for j, chip in enumerate(chips)]
        for cp in first:
            cp.start()
        passed = []
        for a in range(n):
            for j, chip in enumerate(chips):
                copy(a, 1 + j, chip, me).wait_recv()
                passed.append(copy(a, 4 + j, chip, sibling))
                passed[-1].start()
        for a in range(n):
            copy(a, 0, sibling, me).wait_recv()
            for j, chip in enumerate(chips):
                copy(a, 4 + j, _flip(chip, (0, 0, 1)), me).wait_recv()
        for cp in first + passed:
            cp.wait_send()
        for cp in mine:
            cp.wait()

    any_spec = pl.BlockSpec(memory_space=pl.ANY)
    return pl.pallas_call(
        body, name=name, out_shape=[jax.ShapeDtypeStruct((N_DEV,) + v.shape, v.dtype) for v in vs],
        in_specs=[any_spec] * n, out_specs=[any_spec] * n,
        scratch_shapes=[pltpu.SemaphoreType.DMA((7 * n,)), pltpu.SemaphoreType.DMA((7 * n,)), pltpu.SemaphoreType.DMA((n,))],
    )(*vs)


def sum_slots(v, name, tr=256):
    _, r, c = v.shape
    tr = _tile_rows(r, tr)

    def body(v_ref, o_ref):
        acc = v_ref[0].astype(F32)
        for s in range(1, N_DEV):
            acc = acc + v_ref[s].astype(F32)
        o_ref[...] = acc

    return pl.pallas_call(body, name=name, grid=(r // tr,), in_specs=[pl.BlockSpec((N_DEV, tr, c), lambda i: (0, i, 0))],
                          out_specs=pl.BlockSpec((tr, c), lambda i: (i, 0)), out_shape=jax.ShapeDtypeStruct((r, c), F32),
                          compiler_params=_params())(v)


def _tile_rows(r, pref):
    if r <= pref:
        return r
    best = None
    for t in range(8, pref + 1, 8):
        if r % t == 0:
            best = t
    return r if best is None else best


def _adamw_math(w, m, v, g):
    nm = ADAM_B1 * m + (1.0 - ADAM_B1) * g
    nv = ADAM_B2 * v + (1.0 - ADAM_B2) * jnp.square(g)
    m_hat = nm / (1.0 - ADAM_B1 ** ADAM_STEP)
    v_hat = nv / (1.0 - ADAM_B2 ** ADAM_STEP)
    return -ADAM_LR * (m_hat / (jnp.sqrt(v_hat) + ADAM_EPS) + ADAM_WD * w), nm, nv


def update_from_slots(lands, offs, w, m, v, transposed, name):
    layers, a, b = w.shape
    n_land = len(lands)
    if transposed:
        rb, tk = LANE, 512
        assert a % tk == 0 and b % rb == 0 and all(o % rb == 0 for o in offs), (name, w.shape, offs)
        grid = (layers, a // tk, b // rb)
        land_block = (N_DEV, rb, tk)
        tile = pl.BlockSpec((None, tk, rb), lambda l, i, j: (l, i, j))

        def land_spec(layer):
            base = offs[layer] // rb
            return pl.BlockSpec(land_block, lambda l, i, j: (0, base + jnp.where(l == layer, j, 0), jnp.where(l == layer, i, 0)))
    else:
        fits = [t for t in (256, 128, 64) if a % t == 0 and all(o % t == 0 for o in offs)]
        assert fits or all(o == 0 for o in offs), (name, w.shape, offs)
        tr = max(fits) if fits else a
        grid = (layers, a // tr)
        land_block = (N_DEV, _round_up(tr, MEMBER_ROW_TILE), b)
        tile = pl.BlockSpec((None, tr, b), lambda l, i: (l, i, 0))

        def land_spec(layer):
            base = offs[layer] // tr
            return pl.BlockSpec(land_block, lambda l, i: (0, base + jnp.where(l == layer, i, 0), 0))

    def body(*refs):
        land_refs, (w_ref, m_ref, v_ref, g_ref, d_ref, nm_ref, nv_ref, acc) = refs[:n_land], refs[n_land:]
        for layer, land in enumerate(land_refs):
            @pl.when(pl.program_id(0) == layer)
            def _(land=land):
                rows = acc.shape[0]
                s = land[0, :rows].astype(F32)
                for k in range(1, N_DEV):
                    s = s + land[k, :rows].astype(F32)
                acc[...] = s

        g = acc[...].T if transposed else acc[...]
        d, nm, nv = _adamw_math(w_ref[...], m_ref[...], v_ref[...], g)
        g_ref[...] = g
        d_ref[...] = d
        nm_ref[...] = nm
        nv_ref[...] = nv

    sh = jax.ShapeDtypeStruct(w.shape, F32)
    return pl.pallas_call(
        body, name=name, grid=grid, in_specs=[land_spec(layer) for layer in range(n_land)] + [tile] * 3, out_specs=[tile] * 4,
        out_shape=[sh] * 4, scratch_shapes=[pltpu.VMEM((rb, tk) if transposed else (tr, b), F32)],
        compiler_params=_params())(*lands, w, m, v)


def adamw_many(ws, ms, vs, gs, name):
    n = len(ws)

    def body(*refs):
        for i in range(n):
            d, nm, nv = _adamw_math(refs[i][...], refs[n + i][...], refs[2 * n + i][...], refs[3 * n + i][...])
            refs[4 * n + i][...] = d
            refs[5 * n + i][...] = nm
            refs[6 * n + i][...] = nv

    vmem = pl.BlockSpec(memory_space=pltpu.VMEM)
    shapes = [jax.ShapeDtypeStruct(a.shape, F32) for a in ws]
    res = pl.pallas_call(body, name=name, in_specs=[vmem] * (4 * n), out_specs=[vmem] * (3 * n), out_shape=shapes * 3,
                         compiler_params=_params())(*ws, *ms, *vs, *gs)
    return res[:n], res[n:2 * n], res[2 * n:]


def seg_in(x, g):
    return (_rms(x, g),)


def seg_res(x, m, ga, gb):
    x1 = x + _rms(m, ga)
    return x1, _rms(x1, gb)


def act_epilogue(r):
    t = jnp.maximum(r, 0.0)
    return r, t * t


def res_epilogue(m, x, ga, gb):
    x1, h = seg_res(x, m, ga, gb)
    return m, x1, h


def res_bwd_epilogue(dh, x, m, dx1, ga, gb):
    x1 = x + _rms(m, ga)
    d1, dgb = _rms_bwd(x1, gb, dh)
    dx = dx1 + d1
    dm, dga = _rms_bwd(m, ga, dx)
    return dx, dm, dga, dgb


def in_bwd_epilogue(dh, x, dx_res, g):
    d, dg = _rms_bwd(x, g, dh)
    return dx_res + d, dg


def loss_epilogue(mo, x, target, g):
    d = x + _rms(mo, g) - target
    dy = d / float(D)
    dm, dg = _rms_bwd(mo, g, dy)
    return dy, dm, dg, jnp.sum(d * d, axis=0, keepdims=True)


def act_bwd_epilogue(drr, r):
    return (drr * (2.0 * jnp.maximum(r, 0.0)),)


def seg_ln(v, g, b):
    mu = jnp.mean(v, axis=-1, keepdims=True)
    var = jnp.mean(jnp.square(v - mu), axis=-1, keepdims=True)
    vn = (v - mu) * lax.rsqrt(var + LN_EPS) * g + b
    return (jax.nn.silu(vn),)


def make_pool_fn(group):
    window = 2 ** (group + 1)

    def pool_fn(ug, pw, scale):
        s = ug
        for lvl in range(group + 1):
            s = s + shift(s, 2 ** lvl)
        cnt = jnp.minimum(lax.broadcasted_iota(jnp.int32, ug.shape, 0) + 1, window).astype(F32)
        return (bdot(s / cnt - ug, pw, 1, 0) * scale,)

    return pool_fn


def conv4_fn(xr, w, b):
    return (jax.nn.silu(cconv(xr, w, SSM_CONV) + b),)


def cd1_fn(u, dww, dwb, scw):
    val, gate, bg, cg, hh = (u[:, k * LANE:(k + 1) * LANE] for k in range(5))
    v = val * jax.nn.sigmoid(gate)
    vc = cconv(v, dww, CONF_K) + dwb
    sc = bg * cconv(cg * hh, scw, SC_K)
    return vc, sc


def attn_fn(q, kv):
    outs = []
    for h in range(XA_HEADS):
        cols = slice(h * XA_DH, (h + 1) * XA_DH)
        s = bdot(q[:, cols], kv[:, cols], 1, 1) / math.sqrt(XA_DH)
        p = jax.nn.softmax(s, axis=-1)
        outs.append(bdot(p, kv[:, D + h * XA_DH:D + (h + 1) * XA_DH], 1, 0))
    return (jnp.concatenate(outs, axis=1),)


def ssd_chunk(xbc, z, dtraw, dtb, alog, dsk, nw, h0, h1, h2, h3, e64, e64t, ecat, ecatt, tril, trilt):
    xs, bm, cm = xbc[:, :SSM_GSZ], xbc[:, SSM_GSZ:SSM_GSZ + SSM_N], xbc[:, SSM_GSZ + SSM_N:]
    hin = (h0, h1, h2, h3)
    dt = jax.nn.softplus(dtraw + dtb)
    a = -jnp.exp(alog)
    d_a = dt * a
    cs = cmatl(tril, trilt, d_a)
    cs_cat = cmat(cs, ecat, ecatt)
    cs64, cs128 = cs_cat[:, :SSM_GSZ], cs_cat[:, SSM_GSZ:]
    dt64 = cmat(dt, e64, e64t)
    row = lax.broadcasted_iota(jnp.int32, (8, LANE), 0)
    heads = jnp.where(row == 0, dsk, jnp.where(row == 1, jnp.sum(d_a, axis=0, keepdims=True), 0.0))
    heads64 = cmat(heads, e64, e64t)
    d64, tot64 = heads64[0:1, :], heads64[1:2, :]
    xdt = xs * dt64
    cb = bdot(cm, bm, 1, 1)
    li = lax.broadcasted_iota(jnp.int32, (CHUNK, CHUNK), 0)
    si = lax.broadcasted_iota(jnp.int32, (CHUNK, CHUNK), 1)
    causal = li >= si
    lane = lax.broadcasted_iota(jnp.int32, (CHUNK, LANE), 1)
    xw = xdt * jnp.exp(tot64 - cs64)
    ecs = jnp.exp(cs64)
    etot = jnp.exp(tot64)
    ycols, hout = [], []
    for j in range(4):
        sl = slice(j * LANE, (j + 1) * LANE)
        xj = xdt[:, sl]
        ys = []
        for hh in range(2):
            r = 2 * j + hh
            col = cs128[:, r * LANE:(r + 1) * LANE]
            decay = jnp.exp(jnp.where(causal, col - col.T, -1e30))
            ys.append(bdot(cb * decay, xj, 1, 0))
        y_diag = jnp.where(lane < SSM_P, ys[0], ys[1])
        y_off = bdot(cm, hin[j], 1, 0) * ecs[:, sl]
        ycols.append(y_diag + y_off)
        hout.append(etot[:, sl] * hin[j] + bdot(bm, xw[:, sl], 0, 0))
    y = jnp.concatenate(ycols, axis=1) + d64 * xs
    y = y * jax.nn.silu(z)
    yn = y * lax.rsqrt(jnp.mean(y * y, axis=-1, keepdims=True) + RMS_EPS) * nw
    return (yn,) + tuple(hout)


def _xbc_group(a, axis):
    parts = []
    for g in range(SSM_GROUPS):
        for start, width in ((g * SSM_GSZ, SSM_GSZ), (SSM_INNER + g * SSM_N, SSM_N), (SSM_INNER + (SSM_GROUPS + g) * SSM_N, SSM_N)):
            parts.append(lax.slice_in_dim(a, start, start + width, axis=axis))
    return jnp.concatenate(parts, axis=axis)


def _xbc_ungroup(a, axis):
    xs, bs, cs = [], [], []
    for g in range(SSM_GROUPS):
        base = g * SSM_XBC_G
        xs.append(lax.slice_in_dim(a, base, base + SSM_GSZ, axis=axis))
        bs.append(lax.slice_in_dim(a, base + SSM_GSZ, base + SSM_GSZ + SSM_N, axis=axis))
        cs.append(lax.slice_in_dim(a, base + SSM_GSZ + SSM_N, base + SSM_XBC_G, axis=axis))
    return jnp.concatenate(xs + bs + cs, axis=axis)


def _ssd_consts():
    h = np.arange(LANE)[:, None]
    e64 = np.stack([(h == g * 8 + np.arange(SSM_GSZ)[None, :] // SSM_P) for g in range(SSM_GROUPS)]).astype(np.float32)
    e128 = np.stack([(h == g * 8 + np.arange(8 * LANE)[None, :] // LANE) for g in range(SSM_GROUPS)]).astype(np.float32)
    ecat = np.concatenate([e64, e128], axis=2)
    tril = np.tril(np.ones((CHUNK, CHUNK), np.float32))
    return tuple(jnp.asarray(c, dtype=BF) for c in (e64, e64.transpose(0, 2, 1), ecat, ecat.transpose(0, 2, 1), tril, tril.T))


def _ssd_specs(nc, rev):
    def ci(c):
        return nc - 1 - c if rev else c

    def row(width, col):
        return pl.BlockSpec((CHUNK, width), lambda b, c: (b * nc + ci(c), col))

    def whole(shape):
        return pl.BlockSpec(shape, lambda b, c: (0,) * len(shape))

    data = [row(SSM_CONV_DIM, 0),
            row(SSM_GSZ, 1), row(SSM_GSZ, 2), row(LANE, 24)]
    par = [whole((1, LANE))] * 3 + [whole((1, SSM_INNER))]
    cst = [whole((SSM_GROUPS, LANE, SSM_GSZ)), whole((SSM_GROUPS, SSM_GSZ, LANE)), whole((SSM_GROUPS, LANE, 12 * LANE)),
           whole((SSM_GROUPS, 12 * LANE, LANE)), whole((CHUNK, CHUNK)), whole((CHUNK, CHUNK))]
    hsave = pl.BlockSpec((None, None, SSM_GROUPS, 4, SSM_N, LANE), lambda b, c: (b, ci(c), 0, 0, 0, 0))
    return data, par, cst, hsave, row, whole


def _ssd_group_args(g, xbc, z, dtr, dtb, alog, dsk, nw):
    return (xbc[:, g * SSM_XBC_G:(g + 1) * SSM_XBC_G], z[g], dtr, dtb, alog, dsk, nw[:, g * SSM_GSZ:(g + 1) * SSM_GSZ])


def ssd_fwd(xbc_act, u, dtb, alog, dsk, nw, consts, bsz, seq):
    nc = seq // CHUNK
    data, par, cst, hsave, row, _ = _ssd_specs(nc, False)

    def body(xbc, z0, z1, dtr, dtb_r, alog_r, dsk_r, nw_r, e64, e64t, ecat, ecatt, tril, trilt, yn_ref, hs_ref, h):
        @pl.when(pl.program_id(1) == 0)
        def _():
            h[...] = jnp.zeros_like(h)

        hs_ref[...] = h[...]
        ys = []
        for g in range(SSM_GROUPS):
            args = _ssd_group_args(g, xbc[...], (z0[...], z1[...]), dtr[...], dtb_r[...], alog_r[...], dsk_r[...], nw_r[...])
            outs = ssd_chunk(*args, h[g, 0], h[g, 1], h[g, 2], h[g, 3], e64[g], e64t[g], ecat[g], ecatt[g], tril[...], trilt[...])
            ys.append(outs[0])
            for j in range(4):
                h[g, j] = outs[1 + j]
        yn_ref[...] = jnp.concatenate(ys, axis=1).astype(yn_ref.dtype)

    t = bsz * seq
    return pl.pallas_call(
        body, name="ssd_fwd", grid=(bsz, nc), in_specs=data + par + cst, out_specs=[row(SSM_INNER, 0), hsave],
        out_shape=[jax.ShapeDtypeStruct((t, SSM_INNER), BF), jax.ShapeDtypeStruct((bsz, nc, SSM_GROUPS, 4, SSM_N, LANE), F32)],
        scratch_shapes=[pltpu.VMEM((SSM_GROUPS, 4, SSM_N, LANE), F32)], compiler_params=_params(),
    )(xbc_act, u, u, u, dtb, alog, dsk, nw, *consts)


def ssd_bwd(xbc_act, u, dtb, alog, dsk, nw, consts, hs, dmix, bsz, seq):
    nc = seq // CHUNK
    data, par, cst, hsave, row, whole = _ssd_specs(nc, True)
    t = bsz * seq
    pcol = POOL_W // SSM_GSZ

    def body(xbc, z0, z1, dtr, dtb_r, alog_r, dsk_r, nw_r, e64, e64t, ecat, ecatt, tril, trilt, hs_ref, dy0, dy1,
             dxbc, dz, ddt, ddtb, dalog, ddsk, dnw, dh):
        @pl.when(pl.program_id(1) == 0)
        def _():
            dh[...] = jnp.zeros_like(dh)

        per_group = []
        for g, dyn in enumerate((dy0, dy1)):
            cst_vals = (e64[g], e64t[g], ecat[g], ecatt[g], tril[...], trilt[...])
            prim = _ssd_group_args(g, xbc[...], (z0[...], z1[...]), dtr[...], dtb_r[...], alog_r[...], dsk_r[...], nw_r[...])
            prim = prim + (hs_ref[g, 0], hs_ref[g, 1], hs_ref[g, 2], hs_ref[g, 3])
            _, vjp = jax.vjp(lambda *args, c=cst_vals: ssd_chunk(*args, *c), *prim)
            gr = vjp((dyn[...].astype(F32), dh[g, 0], dh[g, 1], dh[g, 2], dh[g, 3]))
            for j in range(4):
                dh[g, j] = gr[7 + j]
            per_group.append(gr)
        g0, g1 = per_group
        dxbc[...] = jnp.concatenate([g0[0], g1[0]], axis=1)
        dz[...] = jnp.concatenate([g0[1], g1[1]], axis=1).astype(dz.dtype)
        ddt[...] = g0[2] + g1[2]

        @pl.when(_first((0, 1)))
        def _():
            for r in (ddtb, dalog, ddsk, dnw):
                r[...] = jnp.zeros_like(r)

        ddtb[...] += g0[3] + g1[3]
        dalog[...] += g0[4] + g1[4]
        ddsk[...] += g0[5] + g1[5]
        dnw[...] += jnp.concatenate([g0[6], g1[6]], axis=1)

    out_specs = [row(SSM_CONV_DIM, 0), row(SSM_INNER, 0), row(LANE, 0), whole((1, LANE)), whole((1, LANE)), whole((1, LANE)),
                 whole((1, SSM_INNER))]
    lane = jax.ShapeDtypeStruct((1, LANE), F32)
    out_shape = [jax.ShapeDtypeStruct((t, SSM_CONV_DIM), F32), jax.ShapeDtypeStruct((t, SSM_INNER), BF),
                 jax.ShapeDtypeStruct((t, LANE), F32), lane, lane, lane, jax.ShapeDtypeStruct((1, SSM_INNER), F32)]
    return pl.pallas_call(
        body, name="ssd_bwd", grid=(bsz, nc), in_specs=data + par + cst + [hsave, row(SSM_GSZ, pcol), row(SSM_GSZ, pcol + 1)],
        out_specs=out_specs, out_shape=out_shape, scratch_shapes=[pltpu.VMEM((SSM_GROUPS, 4, SSM_N, LANE), F32)],
        compiler_params=_params(),
    )(xbc_act, u, u, u, dtb, alog, dsk, nw, *consts, hs, dmix, dmix)


TB = 1024


def _rows(d, col=0):
    return pl.BlockSpec((TB, d), lambda i: (i, col))


def _par(d):
    return pl.BlockSpec((1, d), lambda i: (0, 0))


def _sd(shape, dtype=F32):
    return jax.ShapeDtypeStruct(shape, dtype)


def _round_up(n, m):
    return -(-n // m) * m


def _pad_rows(a, rows):
    return jnp.pad(a, ((0, rows - a.shape[0]), (0, 0)))


def _pack128(arrs):
    flat = jnp.concatenate([a.reshape(-1) for a in arrs])
    n = flat.shape[0]
    rows = -(-n // (8 * LANE)) * 8
    return jnp.pad(flat, (0, rows * LANE - n)).reshape(rows, LANE)


def _unpack128(packed, shapes):
    flat = packed.reshape(-1)
    out, off = [], 0
    for s in shapes:
        n = int(np.prod(s))
        out.append(flat[off:off + n].reshape(s))
        off += n
    return out


def kernel(x, mem, norm_gains, xa_wq, xa_wkv, xa_wo, mlp_w1, mlp_w2, ab_w_in, pool_w, pool_scale, ssm_conv_w, ssm_conv_b, ssm_dt_bias, ssm_a_log, ssm_d, ssm_norm, ab_w_out, cd_w_in, conf_dw_w, conf_dw_b, conf_ln_g, conf_ln_b, sc_conv_w, cd_w_out, loss_target, m_norm_gains, m_xa_wq, m_xa_wkv, m_xa_wo, m_mlp_w1, m_mlp_w2, m_ab_w_in, m_pool_w, m_pool_scale, m_ssm_conv_w, m_ssm_conv_b, m_ssm_dt_bias, m_ssm_a_log, m_ssm_d, m_ssm_norm, m_ab_w_out, m_cd_w_in, m_conf_dw_w, m_conf_dw_b, m_conf_ln_g, m_conf_ln_b, m_sc_conv_w, m_cd_w_out, v_norm_gains, v_xa_wq, v_xa_wkv, v_xa_wo, v_mlp_w1, v_mlp_w2, v_ab_w_in, v_pool_w, v_pool_scale, v_ssm_conv_w, v_ssm_conv_b, v_ssm_dt_bias, v_ssm_a_log, v_ssm_d, v_ssm_norm, v_ab_w_out, v_cd_w_in, v_conf_dw_w, v_conf_dw_b, v_conf_ln_g, v_conf_ln_b, v_sc_conv_w, v_cd_w_out):
    args = locals()
    w = {n: args[n] for n in WEIGHTS}
    mom_m = {n: args["m_" + n] for n in WEIGHTS}
    mom_v = {n: args["v_" + n] for n in WEIGHTS}
    ex = Exchange(w)
    loss_local, grad_x, small_grads = local_step(x, mem, loss_target, ex)
    outs = {}

    started = ex.put_small(small_grads, loss_local)
    landed = {key: ex.landed(key, started) for key in ('l1', 'cd', 'l0')}
    late = []
    for n, keys in (('mlp_w1', ('l0', 'l1')), ('mlp_w2', ('l0', 'l1')), ('xa_wkv', ('l0', 'l1')), ('xa_wq', ('l0', 'l1')),
                    ('xa_wo', ('l0', 'l1')), ('cd_w_in', ('cd',)), ('cd_w_out', ('cd',))):
        lands = [landed[key][0] for key in keys]
        offs = [landed[key][1][(n, layer)] for layer, key in enumerate(keys)]
        outs[n] = update_from_slots(lands, offs, w[n], mom_m[n], mom_v[n], SHARD_AXIS[n] == 2, "update_" + n)
        late.append(outs[n][1])
    g_own, loss = ex.reduced_small(late)
    land_ab, offs_ab = ex.landed('ab', late)
    outs['ab_w_out'] = update_from_slots([land_ab], [offs_ab[('ab_w_out', 0)]], w['ab_w_out'], mom_m['ab_w_out'],
                                         mom_v['ab_w_out'], False, "update_ab_w_out")
    res = update_from_slots([land_ab], [offs_ab[('ab_w_in', 0)]], jnp.swapaxes(w['ab_w_in'], 1, 2), jnp.swapaxes(mom_m['ab_w_in'], 1, 2),
                            jnp.swapaxes(mom_v['ab_w_in'], 1, 2), False, "update_ab_w_in")
    outs['ab_w_in'] = tuple(jnp.swapaxes(r, 1, 2) for r in res)
    small = SMALL_SHARDED + REPLICATED
    upd = adamw_many([w[n] for n in small], [mom_m[n] for n in small], [mom_v[n] for n in small], [g_own[n] for n in small],
                     "adamw_small")
    for i, n in enumerate(small):
        outs[n] = (g_own[n], upd[0][i], upd[1][i], upd[2][i])
    return (loss, grad_x.reshape(x.shape), *[outs[n][0] for n in WEIGHTS], *[outs[n][1] for n in WEIGHTS],
            *[outs[n][2] for n in WEIGHTS], *[outs[n][3] for n in WEIGHTS])


G_AB = (('ab_w_in', 0), ('ab_w_out', 0))
G_L0 = (('xa_wq', 0), ('xa_wkv', 0), ('xa_wo', 0), ('mlp_w1', 0), ('mlp_w2', 0))
G_L1 = (('xa_wq', 1), ('xa_wkv', 1), ('xa_wo', 1), ('mlp_w1', 1), ('mlp_w2', 1))
G_CD = (('cd_w_in', 0), ('cd_w_out', 0))
GATHER_GROUPS = {'ab': G_AB[:1], 'l0a': G_AB[1:] + G_L0[:3], 'l0b': G_L0[3:], 'cd': G_CD, 'l1a': G_L1[:3], 'l1b': G_L1[3:]}
SHARD_AXIS = dict(BIG)
MEMBER_ROW_TILE = 64
FLAT_ROW_TILE = 128


def _members(group, w):
    out = []
    for n, layer in group:
        shp = w[n].shape[1:]
        if SHARD_AXIS[n] == 2:
            shp = (shp[1], shp[0])
        assert shp[1] == D, (n, shp)
        out.append((n, layer, shp, shp[0], _round_up(shp[0], MEMBER_ROW_TILE)))
    return out


def _group_rows(group, w):
    return _round_up(sum(m[4] for m in _members(group, w)), FLAT_ROW_TILE)


def _flat_shards(group, w):
    parts = []
    for n, layer, _, _, padded in _members(group, w):
        shard = w[n][layer].astype(BF)
        parts.append(_pad_rows(shard.T if SHARD_AXIS[n] == 2 else shard, padded))
    return _pad_rows(jnp.concatenate(parts, axis=0), _group_rows(group, w))


def _full_from_slots(land, group, w):
    out, off = {}, 0
    for n, layer, shp, rows, padded in _members(group, w):
        out[(n, layer)] = land[:, off:off + rows].reshape(N_DEV * rows, D)
        off += padded
    return out


def _slots_from_full(grads, group, w):
    parts = []
    for n, layer, shp, rows, padded in _members(group, w):
        blk = grads[(n, layer)].astype(BF).reshape(N_DEV, rows, D)
        parts.append(jnp.pad(blk, ((0, 0), (0, padded - rows), (0, 0))))
    send = jnp.concatenate(parts, axis=1)
    return jnp.pad(send, ((0, 0), (0, _group_rows(group, w) - send.shape[1]), (0, 0)))


_HBM = pl.BlockSpec(memory_space=pltpu.HBM)
_SEM = pl.BlockSpec(memory_space=pltpu.SEMAPHORE)
_ANY = pl.BlockSpec(memory_space=pl.ANY)


def _peer_copy(k, src, dst, send_sems, recv_sems, peer):
    return pltpu.make_async_remote_copy(src_ref=src, dst_ref=dst, send_sem=send_sems.at[k], recv_sem=recv_sems.at[k],
                                        device_id=peer, device_id_type=pl.DeviceIdType.MESH)


def exchange_start(src, name, scatter, after=()):
    shape = src.shape[-2:]
    after = list(after)

    def body(src_ref, land_ref, *rest):
        send_sems, recv_sems, token = rest[len(after)], rest[len(after) + 1], rest[-1]
        me = _me()
        for k, f in enumerate(_FLIPS):
            peer = _flip(me, f)
            piece = src_ref.at[_slot(peer)] if scatter else src_ref
            _peer_copy(k, piece, land_ref.at[_slot(me)], send_sems, recv_sems, peer).start()
        token[...] = jnp.zeros_like(token)

    land = pltpu.with_memory_space_constraint(lax.empty((N_DEV,) + shape, src.dtype), pltpu.HBM)
    return pl.pallas_call(
        body, name=name,
        out_shape=(pltpu.SemaphoreType.DMA((7,)), pltpu.SemaphoreType.DMA((7,)), pltpu.HBM(src.shape, src.dtype),
                   pltpu.HBM((N_DEV,) + shape, src.dtype), jax.ShapeDtypeStruct((8, LANE), F32)),
        in_specs=(_HBM, _HBM) + (_ANY,) * len(after), out_specs=(_SEM, _SEM, _HBM, _HBM, pl.BlockSpec(memory_space=pltpu.VMEM)),
        input_output_aliases={0: 2, 1: 3},
        compiler_params=pltpu.CompilerParams(has_side_effects=pltpu.SideEffectType.DATAFLOW_SIDE_EFFECTING),
    )(pltpu.with_memory_space_constraint(src, pltpu.HBM), land, *after)


def exchange_wait(handles, after, name, scatter):
    send_sems, recv_sems, src_thru, land_thru, _ = handles
    after = list(after) if isinstance(after, (list, tuple)) else [after]

    def body(src_ref, land_ref, send_sems, recv_sems, *rest):
        token = rest[-1]
        me = _me()
        for k, f in enumerate(_FLIPS):
            peer = _flip(me, f)
            piece = src_ref.at[_slot(peer)] if scatter else src_ref
            cp = _peer_copy(k, piece, land_ref.at[_slot(peer)], send_sems, recv_sems, peer)
            cp.wait_send()
            cp.wait_recv()
        token[...] = jnp.zeros_like(token)

    return pl.pallas_call(
        body, name=name, out_shape=(pltpu.HBM(src_thru.shape, src_thru.dtype), pltpu.HBM(land_thru.shape, land_thru.dtype),
                                    jax.ShapeDtypeStruct((8, LANE), F32)),
        in_specs=(_HBM, _HBM, _SEM, _SEM) + (_ANY,) * len(after), out_specs=(_HBM, _HBM, pl.BlockSpec(memory_space=pltpu.VMEM)),
        input_output_aliases={0: 0, 1: 1},
        compiler_params=pltpu.CompilerParams(has_side_effects=pltpu.SideEffectType.DATAFLOW_SIDE_EFFECTING),
    )(src_thru, land_thru, send_sems, recv_sems, *after)


class Exchange:
    def __init__(self, w):
        self.w = w
        self.me = _slot(_me())
        shapes = [w[n].shape for n in SMALL_SHARDED]
        gs, first = all_gather([_pack128([w[n] for n in SMALL_SHARDED]), _flat_shards(GATHER_GROUPS['ab'], w)], "gather_first")
        per_dev = [_unpack128(gs[d], shapes) for d in range(N_DEV)]
        self.small = {n: jnp.concatenate([per_dev[d][i] for d in range(N_DEV)], axis=-1) for i, n in enumerate(SMALL_SHARDED)}
        self.small.update({n: w[n] for n in REPLICATED})
        self.first = _full_from_slots(first, GATHER_GROUPS['ab'], w)
        self.gathers, self.done, self.tokens, self.reductions = {}, {}, [], {}
        self.start_gather('l0a', after=[first])
        self.start_gather('l0b', after=[self.gathers['l0a'][4]])

    def take_tokens(self):
        toks, self.tokens = self.tokens, []
        return toks

    def start_gather(self, key, after=()):
        group = GATHER_GROUPS[key]
        self.gathers[key] = exchange_start(_flat_shards(group, self.w), f"gather_{key}_start", False, after=after)
        self.tokens.append(self.gathers[key][4])

    def weights(self, key, after):
        if key == 'ab':
            return self.first
        handles = self.gathers[key]
        _, land, self.done[key] = exchange_wait(handles, after, f"gather_{key}_wait", False)
        land = lax.dynamic_update_slice(land, handles[2][None], (self.me, 0, 0))
        return _full_from_slots(land, GATHER_GROUPS[key], self.w)

    def put_grads(self, key, group, grads):
        send = _slots_from_full(grads, group, self.w)
        handles = exchange_start(send, f"reduce_{key}_start", True)
        self.reductions[key] = (group, handles)
        self.tokens.append(handles[4])

    def landed(self, key, after):
        group, handles = self.reductions[key]
        send, land, _ = exchange_wait(handles, after, f"reduce_{key}_wait", True)
        mine = lax.dynamic_slice_in_dim(send, self.me, 1, axis=0)
        land = lax.dynamic_update_slice(land, mine, (self.me, 0, 0))
        offs, off = {}, 0
        for n, layer, _, _, padded in _members(group, self.w):
            offs[(n, layer)] = off
            off += padded
        return land, offs

    def put_small(self, small_grads, loss_local):
        small = SMALL_SHARDED + REPLICATED
        self.small_shapes = [small_grads[n].shape for n in small] + [(1,)]
        packed = _pack128([small_grads[n] for n in small] + [loss_local.reshape(1)])
        self.small_handles = exchange_start(packed, "gather_small_grads_start", False)
        return self.small_handles[4]

    def reduced_small(self, after):
        small = SMALL_SHARDED + REPLICATED
        src, land, _ = exchange_wait(self.small_handles, after, "gather_small_grads_wait", False)
        gs = lax.dynamic_update_slice(land, src[None], (self.me, 0, 0))
        tot = _unpack128(sum_slots(gs, "sum_small", 1024), self.small_shapes)
        out = {}
        for n, g in zip(small, tot):
            if n in SMALL_SHARDED:
                width = self.w[n].shape[-1]
                g = lax.dynamic_slice_in_dim(g, self.me * width, width, axis=g.ndim - 1)
            out[n] = g
        return out, tot[-1].reshape(())


def local_step(x, mem, target, ex):
    bsz, seq, _ = x.shape
    t = bsz * seq
    nb = t // TB
    nc = seq // CHUNK
    x0 = x.reshape(t, D)
    mem2 = mem.reshape(bsz * N_MEM, D)
    tgt = target.reshape(t, D)
    p = ex.small
    gains = p['norm_gains']
    big = {}

    def gain(layer, i):
        g = gains[layer, i].reshape(1, D)
        for tok in ex.take_tokens():
            g = g + tok[0, 0]
        return g

    consts = _ssd_consts()
    grads = {}
    saved = [dict(), dict()]

    def matmul_res(a, b, name, xin, ga, gb):
        return matmul(a, b, 'nn', name, (F32, F32, BF), epilogue=res_epilogue, extras=[xin], params=[ga, gb])

    def attn_specs():
        nq = seq // TB
        q = pl.BlockSpec((TB, D), lambda b, i: (b * nq + i, 0))
        kv = pl.BlockSpec((N_MEM, 2 * D), lambda b, i: (b, 0))
        return (bsz, nq), q, kv

    def attention_fwd(layer, xin, hin, sv, ga, gb):
        q = matmul(hin, big[('xa_wq', layer)], 'nn', f"q_{layer}", BF)
        kv = matmul(mem2, big[('xa_wkv', layer)], 'nt', f"kv_{layer}", BF)
        grid, qs, kvs = attn_specs()
        o, = fwd_call(attn_fn, f"attn_{layer}", grid, [q, kv], [qs, kvs], [_sd((t, D), BF)], [qs])
        ao, x_next, h_next = matmul_res(o, big[('xa_wo', layer)], f"ao_{layer}", xin, ga, gb)
        sv.update(q=q, kv=kv, o=o, ao=ao)
        return ao, x_next, h_next

    def mlp_fwd(layer, hin, sv, res):
        r, rr = matmul(hin, big[('mlp_w1', layer)], 'nt', f"mlp1_{layer}", (BF, BF), epilogue=act_epilogue)
        out = matmul_res(rr, big[('mlp_w2', layer)], f"mlp2_{layer}", *res)
        sv.update(r=r, rr=rr, mo=out[0])
        return out

    sv = saved[0]
    h0, = fwd_call(seg_in, "norm_in", (nb,), [x0, gain(0, 0)], [_rows(D), _par(D)], [_sd((t, D), BF)], [_rows(D)])
    big.update(ex.weights('ab', h0))
    xbc0 = POOL_W + SSM_INNER
    w_ab_in = big[('ab_w_in', 0)]
    w_ab_in = _pad_rows(jnp.concatenate([w_ab_in[:xbc0], _xbc_group(w_ab_in[xbc0:xbc0 + SSM_CONV_DIM], 0),
                                         w_ab_in[xbc0 + SSM_CONV_DIM:]], axis=0), AB_IN_PAD)
    conv_w, conv_b = _xbc_group(p['ssm_conv_w'][0], 1), _xbc_group(p['ssm_conv_b'], 1)
    u0 = matmul(h0, w_ab_in, 'nt', "ab_in")
    pool_outs = []
    for g in range(POOL_GROUPS):
        seqspec = pl.BlockSpec((seq, PG), lambda b, g=g: (b, g))
        po, = fwd_call(make_pool_fn(g), f"pool_{g}", (bsz,), [u0, p['pool_w'][0, g], p['pool_scale']],
                       [seqspec, pl.BlockSpec((PG, PG), lambda b: (0, 0)), pl.BlockSpec((1, PG), lambda b, g=g: (0, g))],
                       [_sd((t, PG), BF)], [pl.BlockSpec((seq, PG), lambda b: (b, 0))])
        pool_outs.append(po)
    cw = 256
    ncb = SSM_CONV_DIM // cw
    cbase = (POOL_W + SSM_INNER) // cw
    conv_in_specs = [pl.BlockSpec((seq, cw), lambda j, b: (b, cbase + j)), pl.BlockSpec((SSM_CONV, cw), lambda j, b: (0, j)),
                     pl.BlockSpec((1, cw), lambda j, b: (0, j))]
    conv_out_spec = pl.BlockSpec((seq, cw), lambda j, b: (b, j))
    xbc_act, = fwd_call(conv4_fn, "ssm_conv", (ncb, bsz), [u0, conv_w, conv_b], conv_in_specs,
                        [_sd((t, SSM_CONV_DIM))], [conv_out_spec])
    dtb = jnp.pad(p['ssm_dt_bias'], ((0, 0), (0, LANE - SSM_HEADS)))
    alog = jnp.pad(p['ssm_a_log'], ((0, 0), (0, LANE - SSM_HEADS)))
    dsk = jnp.pad(p['ssm_d'], ((0, 0), (0, LANE - SSM_HEADS)))
    yn, hs = ssd_fwd(xbc_act, u0, dtb, alog, dsk, p['ssm_norm'], consts, bsz, seq)
    mix0 = jnp.concatenate(pool_outs + [yn], axis=1)
    big.update(ex.weights('l0a', yn))
    ex.start_gather('cd', after=[ex.done['l0a']])
    ex.start_gather('l1a', after=[ex.gathers['cd'][4]])
    ex.start_gather('l1b', after=[ex.gathers['l1a'][4]])
    m0, x1, h2 = matmul_res(mix0, big[('ab_w_out', 0)], "ab_out", x0, gain(0, 1), gain(0, 2))
    ao0, x2, h3 = attention_fwd(0, x1, h2, sv, gain(0, 3), gain(0, 4))
    big.update(ex.weights('l0b', h3))
    mo0, x3, h4 = mlp_fwd(0, h3, sv, (x2, gain(0, 5), gain(1, 0)))
    big.update(ex.weights('cd', mo0))

    sv1 = saved[1]
    nd = D // LANE
    w_cd_in = big[('cd_w_in', 0)].reshape(5, nd, LANE, D).transpose(1, 0, 2, 3).reshape(CD_IN, D)
    u1 = matmul(h4, w_cd_in, 'nt', "cd_in")
    cd_par = [pl.BlockSpec((CONF_K, LANE), lambda j, b: (0, j)), pl.BlockSpec((1, LANE), lambda j, b: (0, j)),
              pl.BlockSpec((SC_K, LANE), lambda j, b: (0, j))]
    cd_ins = [u1, p['conf_dw_w'][0], p['conf_dw_b'], p['sc_conv_w'][0]]
    cd_u_spec = pl.BlockSpec((seq, 5 * LANE), lambda j, b: (b, j))
    cd_in_specs = [cd_u_spec] + cd_par
    cd_out_spec = pl.BlockSpec((seq, LANE), lambda j, b: (b, j))
    vconv, mix1 = fwd_call(cd1_fn, "cd_conv", (nd, bsz), cd_ins, cd_in_specs, [_sd((t, D)), _sd((t, CD_OUT), BF)],
                           [cd_out_spec, pl.BlockSpec((seq, LANE), lambda j, b: (b, nd + j))])
    mix1, = fwd_call(seg_ln, "conf_ln", (nb,), [vconv, p['conf_ln_g'], p['conf_ln_b']], [_rows(D), _par(D), _par(D)],
                     [_sd((t, CD_OUT), BF)], [_rows(D)], into=mix1)
    m1, x4, h5 = matmul_res(mix1, big[('cd_w_out', 0)], "cd_out", x3, gain(1, 1), gain(1, 2))
    big.update(ex.weights('l1a', h5))
    ao1, x5, h6 = attention_fwd(1, x4, h5, sv1, gain(1, 3), gain(1, 4))
    big.update(ex.weights('l1b', h6))
    r1, rr1 = matmul(h6, big[('mlp_w1', 1)], 'nt', "mlp1_1", (BF, BF), epilogue=act_epilogue)
    sv1.update(r=r1, rr=rr1)
    dx5, dmo1, dg15, lanes = matmul(rr1, big[('mlp_w2', 1)], 'nn', "mlp2_1", (F32, BF), epilogue=loss_epilogue, extras=[x5, tgt],
                                    params=[gain(1, 5)], n_acc=2)
    loss = 0.5 * jnp.sum(lanes) / float(D)

    gain_grads = {(1, 5): dg15}

    def matmul_res_bwd(a, b, mode, name, xin, m, ga, gb, dx1):
        return list(matmul(a, b, mode, name, (F32, BF), epilogue=res_bwd_epilogue, extras=[xin, m, dx1], params=[ga, gb], n_acc=2))

    def mlp_bwd(layer, hin, dmo, sv, res):
        grads_w2 = matmul(sv['rr'], dmo, 'tn', f"d_mlp_w2_{layer}", BF)
        dr, = matmul(dmo, big[('mlp_w2', layer)], 'nt', f"d_r_{layer}", (BF,), epilogue=act_bwd_epilogue, extras=[sv['r']])
        grads_w1 = matmul(dr, hin, 'tn', f"d_mlp_w1_{layer}", BF)
        return matmul_res_bwd(dr, big[('mlp_w1', layer)], 'nn', f"d_h_mlp_{layer}", *res) + [grads_w1, grads_w2]

    def attention_bwd(layer, hin, dao, sv, res):
        g_wo = matmul(sv['o'], dao, 'tn', f"d_xa_wo_{layer}", BF)
        do = matmul(dao, big[('xa_wo', layer)], 'nt', f"d_o_{layer}", BF)
        grid, qs, kvs = attn_specs()
        dq, dkv = bwd_call(attn_fn, f"d_attn_{layer}", grid, [sv['q'], sv['kv']], [qs, kvs], [do], [qs], [0, 1],
                           [_sd((t, D), BF), _sd((bsz * N_MEM, 2 * D))], [qs, kvs], [None, (1,)])
        g_wkv = matmul(dkv, mem2, 'tn', f"d_xa_wkv_{layer}", BF)
        g_wq = matmul(hin, dq, 'tn', f"d_xa_wq_{layer}", BF)
        return matmul_res_bwd(dq, big[('xa_wq', layer)], 'nt', f"d_h_attn_{layer}", *res) + [g_wq, g_wkv, g_wo]

    per_layer = {k: [None, None] for k in ('xa_wq', 'xa_wkv', 'xa_wo', 'mlp_w1', 'mlp_w2')}

    (dx4, dao1, gain_grads[(1, 3)], gain_grads[(1, 4)], per_layer['mlp_w1'][1],
     per_layer['mlp_w2'][1]) = mlp_bwd(1, h6, dmo1, sv1, (x4, ao1, gain(1, 3), gain(1, 4), dx5))
    (dx3, dm1, gain_grads[(1, 1)], gain_grads[(1, 2)], per_layer['xa_wq'][1], per_layer['xa_wkv'][1],
     per_layer['xa_wo'][1]) = attention_bwd(1, h5, dao1, sv1, (x3, m1, gain(1, 1), gain(1, 2), dx4))
    ex.put_grads('l1', G_L1, {(k, 1): v[1] for k, v in per_layer.items()})
    g_cd_out = matmul(mix1, dm1, 'tn', "d_cd_w_out", BF)
    dmix1 = matmul(dm1, big[('cd_w_out', 0)], 'nt', "d_mix1", after=ex.take_tokens())
    dvconv, dlg, dlb = bwd_call(seg_ln, "d_conf_ln", (nb,), [vconv, p['conf_ln_g'], p['conf_ln_b']],
                                [_rows(D), _par(D), _par(D)], [dmix1], [_rows(D, 0)], [0, 1, 2],
                                [_sd((t, D)), _sd((1, D)), _sd((1, D))], [_rows(D), _par(D), _par(D)], [None, (0,), (0,)])
    grads['conf_ln_g'], grads['conf_ln_b'] = dlg, dlb
    cd_g = bwd_call(cd1_fn, "d_cd_conv", (nd, bsz), cd_ins, cd_in_specs, [dvconv, dmix1],
                    [cd_out_spec, pl.BlockSpec((seq, LANE), lambda j, b: (b, nd + j))], list(range(4)),
                    [_sd((t, CD_IN), BF), _sd((CONF_K, D)), _sd((1, D)), _sd((SC_K, D))], [cd_u_spec] + cd_par,
                    [None, (1,), (1,), (1,)])
    du1 = cd_g[0]
    grads['conf_dw_w'], grads['conf_dw_b'], grads['sc_conv_w'] = cd_g[1][None], cd_g[2], cd_g[3][None]
    g_cd_in = matmul(du1, h4, 'tn', "d_cd_w_in", BF).reshape(nd, 5, LANE, D).transpose(1, 0, 2, 3).reshape(CD_IN, D)
    ex.put_grads('cd', G_CD, {('cd_w_in', 0): g_cd_in, ('cd_w_out', 0): g_cd_out})
    dx2, dmo0, gain_grads[(0, 5)], gain_grads[(1, 0)] = matmul_res_bwd(du1, w_cd_in, 'nn', "d_h_cd", x2, mo0, gain(0, 5),
                                                                       gain(1, 0), dx3)
    (dx1, dao0, gain_grads[(0, 3)], gain_grads[(0, 4)], per_layer['mlp_w1'][0],
     per_layer['mlp_w2'][0]) = mlp_bwd(0, h3, dmo0, sv, (x1, ao0, gain(0, 3), gain(0, 4), dx2))
    (dx0r, dm0, gain_grads[(0, 1)], gain_grads[(0, 2)], per_layer['xa_wq'][0], per_layer['xa_wkv'][0],
     per_layer['xa_wo'][0]) = attention_bwd(0, h2, dao0, sv, (x0, m0, gain(0, 1), gain(0, 2), dx1))
    ex.put_grads('l0', G_L0, {(k, 0): v[0] for k, v in per_layer.items()})
    g_ab_out = matmul(mix0, dm0, 'tn', "d_ab_w_out", BF)
    dmix0 = matmul(dm0, big[('ab_w_out', 0)], 'nt', "d_mix0", after=ex.take_tokens())
    dxbc_act, dz, ddt, ddtb, dalog, ddsk, dnw = ssd_bwd(xbc_act, u0, dtb, alog, dsk, p['ssm_norm'], consts, hs, dmix0, bsz, seq)
    grads['ssm_dt_bias'] = ddtb[:, :SSM_HEADS]
    grads['ssm_a_log'] = dalog[:, :SSM_HEADS]
    grads['ssm_d'] = ddsk[:, :SSM_HEADS]
    grads['ssm_norm'] = dnw
    dxr, dcw, dcb = bwd_call(conv4_fn, "d_ssm_conv", (ncb, bsz), [u0, conv_w, conv_b], conv_in_specs,
                             [dxbc_act], [conv_out_spec], [0, 1, 2],
                             [_sd((t, SSM_CONV_DIM), BF), _sd((SSM_CONV, SSM_CONV_DIM)), _sd((1, SSM_CONV_DIM))],
                             [conv_out_spec, conv_in_specs[1], conv_in_specs[2]], [None, (1,), (1,)])
    grads['ssm_conv_w'], grads['ssm_conv_b'] = _xbc_ungroup(dcw, 1)[None], _xbc_ungroup(dcb, 1)
    dpool, dpw, dps = [], [], []
    for g in range(POOL_GROUPS):
        seqspec = pl.BlockSpec((seq, PG), lambda b, g=g: (b, g))
        one = pl.BlockSpec((seq, PG), lambda b: (b, 0))
        wspec = pl.BlockSpec((PG, PG), lambda b: (0, 0))
        sspec = pl.BlockSpec((1, PG), lambda b, g=g: (0, g))
        a, bb, c = bwd_call(make_pool_fn(g), f"d_pool_{g}", (bsz,), [u0, p['pool_w'][0, g], p['pool_scale']],
                            [seqspec, wspec, sspec], [dmix0], [seqspec], [0, 1, 2],
                            [_sd((t, PG), BF), _sd((PG, PG)), _sd((1, PG))], [one, wspec, pl.BlockSpec((1, PG), lambda b: (0, 0))],
                            [None, (0,), (0,)])
        dpool.append(a)
        dpw.append(bb)
        dps.append(c)
    grads['pool_w'] = jnp.stack(dpw)[None]
    grads['pool_scale'] = jnp.concatenate(dps, axis=1)
    du0 = jnp.concatenate(dpool + [dz, dxr, ddt.astype(BF)], axis=1)
    g_ab_in = matmul(du0, h0, 'tn', "d_ab_w_in", BF)
    g_ab_in = jnp.concatenate([g_ab_in[:xbc0], _xbc_ungroup(g_ab_in[xbc0:xbc0 + SSM_CONV_DIM], 0),
                               g_ab_in[xbc0 + SSM_CONV_DIM:AB_IN]], axis=0)
    ex.put_grads('ab', G_AB, {('ab_w_in', 0): g_ab_in, ('ab_w_out', 0): g_ab_out})
    dx, dg00 = matmul(du0, w_ab_in, 'nn', "d_h_ab", (F32,), epilogue=in_bwd_epilogue, extras=[x0, dx0r], params=[gain(0, 0)],
                      after=ex.take_tokens(), n_acc=1)
    gain_grads[(0, 0)] = dg00
    grads['norm_gains'] = jnp.stack([jnp.concatenate([gain_grads[(l, i)] for i in range(6)], axis=0) for l in range(2)])
    return loss, dx, grads
```
